```python
import jax, jax.numpy as jnp
from jax import lax
import numpy as np

D_MODEL = 1024
BATCH = 8
SEQ = 4096
DEPTH = 2

HEAD_DIM = 64
A_HEADS = 6
DILATED_PATTERNS = ((128, 1), (512, 4), (2048, 16))
CONV_CH = 256
CONV_K = 3
C_Q_HEADS = 6
C_KV_HEADS = 2
C_GROUP = C_Q_HEADS // C_KV_HEADS
C_WINDOW = 128
BLOCK = 128
D_FF = 4 * D_MODEL
EPS = 1e-6

A_WIDTH = A_HEADS * HEAD_DIM
C_WIDTH = C_Q_HEADS * HEAD_DIM
KV_WIDTH = C_KV_HEADS * HEAD_DIM
MIX_WIDTH = A_WIDTH + CONV_CH + C_WIDTH
IN_SPLITS = (A_WIDTH, A_WIDTH, A_WIDTH, CONV_CH, CONV_CH, CONV_CH, C_WIDTH, KV_WIDTH, KV_WIDTH)
IN_WIDTH = sum(IN_SPLITS)
SPLIT_POINTS = tuple(int(p) for p in np.cumsum(IN_SPLITS)[:-1])

kernel_name = "hymba_dilated_conv_swa_sink_trunk"


def rms_normalize(t):
    t32 = t.astype(jnp.float32)
    return (t32 * lax.rsqrt(jnp.mean(t32 * t32, axis=-1, keepdims=True) + EPS)).astype(t.dtype)


def rmsnorm(t, g):
    return rms_normalize(t) * g


def banded_attention(q, k, v, max_dist, sink_logits=None):
    n, L, hkv, g, dh = q.shape
    nb = -(-L // BLOCK)
    lp = nb * BLOCK
    pad = lp - L
    q = jnp.pad(q, ((0, 0), (0, pad), (0, 0), (0, 0), (0, 0)))
    kv_pad = ((0, 0), (BLOCK, pad), (0, 0), (0, 0))
    k = jnp.pad(k, kv_pad).reshape(n, nb + 1, BLOCK, hkv, dh)
    v = jnp.pad(v, kv_pad).reshape(n, nb + 1, BLOCK, hkv, dh)
    k2 = jnp.concatenate([k[:, :-1], k[:, 1:]], axis=2)
    v2 = jnp.concatenate([v[:, :-1], v[:, 1:]], axis=2)
    qb = q.reshape(n, nb, BLOCK, hkv, g, dh)
    s = jnp.einsum('nbqhgd,nbkhd->nbhgqk', qb, k2).astype(jnp.float32) * (dh ** -0.5)
    qi = jnp.arange(BLOCK)[:, None]
    kj = jnp.arange(2 * BLOCK)[None, :]
    dist = BLOCK + qi - kj
    band = (dist >= 0) & (dist <= max_dist)
    first = jnp.arange(nb)[:, None, None] == 0
    mask = band[None] & ~(first & (kj < BLOCK)[None])
    s = jnp.where(mask[None, :, None, None], s, -jnp.inf)
    if sink_logits is not None:
        sink = jnp.broadcast_to(sink_logits.astype(jnp.float32)[None, None, :, :, None, None], s.shape[:-1] + (1,))
        lse = jax.nn.logsumexp(jnp.concatenate([s, sink], axis=-1), axis=-1)
    else:
        lse = jax.nn.logsumexp(s, axis=-1)
    p = jnp.exp(s - lse[..., None]).astype(v2.dtype)
    o = jnp.einsum('nbhgqk,nbkhd->nbqhgd', p, v2).reshape(n, lp, hkv, g, dh)[:, :L]
    lse = lse.transpose(0, 1, 4, 2, 3).reshape(n, lp, hkv, g)[:, :L]
    return o, lse


def to_residues(t, dil):
    b, s, h, dh = t.shape
    return t.reshape(b, s // dil, dil, h, dh).transpose(0, 2, 1, 3, 4).reshape(b * dil, s // dil, h, dh)


def from_residues(t, dil, b):
    sub = t.shape[1]
    rest = t.shape[2:]
    t = t.reshape((b, dil, sub) + rest)
    t = jnp.moveaxis(t, 1, 2)
    return t.reshape((b, sub * dil) + rest)


def dilated_attention(q, k, v):
    b = q.shape[0]
    outs, lses = [], []
    for window, dil in DILATED_PATTERNS:
        o, lse = banded_attention(to_residues(q, dil)[:, :, :, None], to_residues(k, dil),
                                  to_residues(v, dil), window // dil)
        outs.append(from_residues(o[:, :, :, 0], dil, b))
        lses.append(from_residues(lse[..., 0], dil, b))
    wts = jax.nn.softmax(jnp.stack(lses), axis=0)
    return jnp.einsum('pbsh,pbshd->bshd', wts.astype(q.dtype), jnp.stack(outs))


def short_gated_conv(gate_b, gate_c, xb, w):
    s = xb.shape[1]
    u = gate_c * xb
    up = jnp.pad(u, ((0, 0), (CONV_K - 1, 0), (0, 0)))
    y = sum(w[i] * up[:, i:i + s] for i in range(CONV_K))
    return gate_b * y


def _fwd_setup_inputs(seed: int = 0) -> dict:
    key = jax.random.key(seed)
    ks = jax.random.split(key, 12)
    nrm = jax.random.normal
    x = nrm(ks[0], (BATCH, SEQ, D_MODEL), jnp.float32)
    w_in = nrm(ks[1], (DEPTH, D_MODEL, IN_WIDTH), jnp.float32) * D_MODEL ** -0.5
    conv_w = nrm(ks[2], (DEPTH, CONV_K, CONV_CH), jnp.float32) * CONV_K ** -0.5
    sinks = nrm(ks[3], (DEPTH, C_KV_HEADS, C_GROUP), jnp.float32) * 0.5
    g_mix = 1.0 + 0.02 * nrm(ks[4], (DEPTH, D_MODEL), jnp.float32)
    g_group = 1.0 + 0.02 * nrm(ks[5], (DEPTH, MIX_WIDTH), jnp.float32)
    w_o = nrm(ks[6], (DEPTH, MIX_WIDTH, D_MODEL), jnp.float32) * MIX_WIDTH ** -0.5
    g_mlp = 1.0 + 0.02 * nrm(ks[7], (DEPTH, D_MODEL), jnp.float32)
    w_ff_in = nrm(ks[8], (DEPTH, D_MODEL, D_FF), jnp.float32) * D_MODEL ** -0.5
    w_ff_out = nrm(ks[9], (DEPTH, D_FF, D_MODEL), jnp.float32) * D_FF ** -0.5
    g_final = 1.0 + 0.02 * nrm(ks[10], (D_MODEL,), jnp.float32)
    return {"x": x, "w_in": w_in, "conv_w": conv_w, "sinks": sinks, "g_mix": g_mix,
            "g_group": g_group, "w_o": w_o, "g_mlp": g_mlp, "w_ff_in": w_ff_in,
            "w_ff_out": w_ff_out, "g_final": g_final}


def _fwd_reference(x, w_in, conv_w, sinks, g_mix, g_group, w_o, g_mlp, w_ff_in, w_ff_out, g_final):
    b, s, _ = x.shape
    for l in range(DEPTH):
        h = rmsnorm(x, g_mix[l])
        z = jnp.einsum('bsd,de->bse', h, w_in[l])
        qa, ka, va, gb, gc, xb, qc, kc, vc = jnp.split(z, SPLIT_POINTS, axis=-1)
        ya = dilated_attention(qa.reshape(b, s, A_HEADS, HEAD_DIM),
                               ka.reshape(b, s, A_HEADS, HEAD_DIM),
                               va.reshape(b, s, A_HEADS, HEAD_DIM)).reshape(b, s, A_WIDTH)
        yb = short_gated_conv(gb, gc, xb, conv_w[l])
        oc, _ = banded_attention(qc.reshape(b, s, C_KV_HEADS, C_GROUP, HEAD_DIM),
                                 kc.reshape(b, s, C_KV_HEADS, HEAD_DIM),
                                 vc.reshape(b, s, C_KV_HEADS, HEAD_DIM),
                                 C_WINDOW - 1, sinks[l])
        yc = oc.reshape(b, s, C_WIDTH)
        y = jnp.concatenate([rms_normalize(ya), rms_normalize(yb), rms_normalize(yc)], axis=-1) * g_group[l]
        x = x + jnp.einsum('bse,ed->bsd', y, w_o[l])
        h2 = rmsnorm(x, g_mlp[l])
        a = jnp.square(jax.nn.relu(jnp.einsum('bsd,df->bsf', h2, w_ff_in[l])))
        x = x + jnp.einsum('bsf,fd->bsd', a, w_ff_out[l])
    return rmsnorm(x, g_final)


import jax as _jax
import jax.numpy as _jnp

TWIN_FORMAT = 'train_step'
FWD_PARAMS = ['x', 'w_in', 'conv_w', 'sinks', 'g_mix', 'g_group', 'w_o', 'g_mlp', 'w_ff_in', 'w_ff_out', 'g_final']
TWIN_WEIGHTS = ['w_in', 'conv_w', 'sinks', 'g_mix', 'g_group', 'w_o', 'g_mlp', 'w_ff_in', 'w_ff_out', 'g_final']
TWIN_DIFF_INPUT = 'x'
TWIN_INPUTS = ['x', 'w_in', 'conv_w', 'sinks', 'g_mix', 'g_group', 'w_o', 'g_mlp', 'w_ff_in', 'w_ff_out', 'g_final', 'loss_target', 'm_w_in', 'm_conv_w', 'm_sinks', 'm_g_mix', 'm_g_group', 'm_w_o', 'm_g_mlp', 'm_w_ff_in', 'm_w_ff_out', 'm_g_final', 'v_w_in', 'v_conv_w', 'v_sinks', 'v_g_mix', 'v_g_group', 'v_w_o', 'v_g_mlp', 'v_w_ff_in', 'v_w_ff_out', 'v_g_final']
TWIN_OUTPUTS = ['loss', 'grad_x', 'grad_w_in', 'grad_conv_w', 'grad_sinks', 'grad_g_mix', 'grad_g_group', 'grad_w_o', 'grad_g_mlp', 'grad_w_ff_in', 'grad_w_ff_out', 'grad_g_final', 'delta_w_in', 'delta_conv_w', 'delta_sinks', 'delta_g_mix', 'delta_g_group', 'delta_w_o', 'delta_g_mlp', 'delta_w_ff_in', 'delta_w_ff_out', 'delta_g_final', 'new_m_w_in', 'new_m_conv_w', 'new_m_sinks', 'new_m_g_mix', 'new_m_g_group', 'new_m_w_o', 'new_m_g_mlp', 'new_m_w_ff_in', 'new_m_w_ff_out', 'new_m_g_final', 'new_v_w_in', 'new_v_conv_w', 'new_v_sinks', 'new_v_g_mix', 'new_v_g_group', 'new_v_w_o', 'new_v_g_mlp', 'new_v_w_ff_in', 'new_v_w_ff_out', 'new_v_g_final']
TWIN_LEAF_KINDS = {'loss': 'loss', 'grad_x': 'grad_x', 'grad_w_in': 'grad_w', 'grad_conv_w': 'grad_w', 'grad_sinks': 'grad_w', 'grad_g_mix': 'grad_w', 'grad_g_group': 'grad_w', 'grad_w_o': 'grad_w', 'grad_g_mlp': 'grad_w', 'grad_w_ff_in': 'grad_w', 'grad_w_ff_out': 'grad_w', 'grad_g_final': 'grad_w', 'delta_w_in': 'delta_w', 'delta_conv_w': 'delta_w', 'delta_sinks': 'delta_w', 'delta_g_mix': 'delta_w', 'delta_g_group': 'delta_w', 'delta_w_o': 'delta_w', 'delta_g_mlp': 'delta_w', 'delta_w_ff_in': 'delta_w', 'delta_w_ff_out': 'delta_w', 'delta_g_final': 'delta_w', 'new_m_w_in': 'new_m', 'new_m_conv_w': 'new_m', 'new_m_sinks': 'new_m', 'new_m_g_mix': 'new_m', 'new_m_g_group': 'new_m', 'new_m_w_o': 'new_m', 'new_m_g_mlp': 'new_m', 'new_m_w_ff_in': 'new_m', 'new_m_w_ff_out': 'new_m', 'new_m_g_final': 'new_m', 'new_v_w_in': 'new_v', 'new_v_conv_w': 'new_v', 'new_v_sinks': 'new_v', 'new_v_g_mix': 'new_v', 'new_v_g_group': 'new_v', 'new_v_w_o': 'new_v', 'new_v_g_mlp': 'new_v', 'new_v_w_ff_in': 'new_v', 'new_v_w_ff_out': 'new_v', 'new_v_g_final': 'new_v'}


def _forward(args):
    return _fwd_reference(*[args[k] for k in FWD_PARAMS])


def _output_shape():
    out = _jax.eval_shape(lambda: _forward(_fwd_setup_inputs(0)))
    return out.shape, out.dtype

N_MICROBATCH = 1
ADAM_LR = 0.001
ADAM_B1 = 0.9
ADAM_B2 = 0.999
ADAM_EPS = 1e-08
ADAM_WD = 0.01
ADAM_STEP = 10
PER_EXAMPLE_BATCH_AXIS = {'x': 0, 'loss_target': 0}
SHARED_INPUTS = []
_WEIGHT_DTYPES = {'w_in': _jnp.float32, 'conv_w': _jnp.float32, 'sinks': _jnp.float32, 'g_mix': _jnp.float32, 'g_group': _jnp.float32, 'w_o': _jnp.float32, 'g_mlp': _jnp.float32, 'w_ff_in': _jnp.float32, 'w_ff_out': _jnp.float32, 'g_final': _jnp.float32}
MOMENT_SCALE = {'w_in': 1.285608e-01, 'conv_w': 1.276104e-01, 'sinks': 5.350309e-02, 'g_mix': 2.113290e-01, 'g_group': 1.279154e-01, 'w_o': 1.324466e-01, 'g_mlp': 1.353708e-01, 'w_ff_in': 6.387023e-02, 'w_ff_out': 1.470869e-01, 'g_final': 3.264706e+01}


def _to_microbatches(a, axis):
    t = _jnp.moveaxis(a, axis, 0)
    t = t.reshape((N_MICROBATCH, t.shape[0] // N_MICROBATCH) + t.shape[1:])
    return _jnp.moveaxis(t, 1, axis + 1)


def setup_inputs(seed: int = 0) -> dict:
    inp = _fwd_setup_inputs(seed)
    key = _jax.random.fold_in(_jax.random.key(seed), 7919)
    shape, _ = _output_shape()
    out = dict(inp)
    out["loss_target"] = _jax.random.normal(_jax.random.fold_in(key, 0), shape, _jnp.float32)
    for i, name in enumerate(TWIN_WEIGHTS):
        w = inp[name].astype(_jnp.float32)
        if MOMENT_SCALE is None:
            s = _jnp.sqrt(_jnp.mean(_jnp.square(w)) + 1e-30)
        else:
            s = MOMENT_SCALE[name]
        km, kv = _jax.random.split(_jax.random.fold_in(key, i + 1))
        out[name] = w
        out["m_" + name] = s * _jax.random.normal(km, w.shape, _jnp.float32)
        out["v_" + name] = (s * s) * _jax.random.uniform(kv, w.shape, _jnp.float32, 0.5, 1.5)
    if N_MICROBATCH > 1:
        for name, axis in PER_EXAMPLE_BATCH_AXIS.items():
            out[name] = _to_microbatches(out[name], axis)
    return {'x': out['x'], 'w_in': out['w_in'], 'conv_w': out['conv_w'], 'sinks': out['sinks'], 'g_mix': out['g_mix'], 'g_group': out['g_group'], 'w_o': out['w_o'], 'g_mlp': out['g_mlp'], 'w_ff_in': out['w_ff_in'], 'w_ff_out': out['w_ff_out'], 'g_final': out['g_final'], 'loss_target': out['loss_target'], 'm_w_in': out['m_w_in'], 'm_conv_w': out['m_conv_w'], 'm_sinks': out['m_sinks'], 'm_g_mix': out['m_g_mix'], 'm_g_group': out['m_g_group'], 'm_w_o': out['m_w_o'], 'm_g_mlp': out['m_g_mlp'], 'm_w_ff_in': out['m_w_ff_in'], 'm_w_ff_out': out['m_w_ff_out'], 'm_g_final': out['m_g_final'], 'v_w_in': out['v_w_in'], 'v_conv_w': out['v_conv_w'], 'v_sinks': out['v_sinks'], 'v_g_mix': out['v_g_mix'], 'v_g_group': out['v_g_group'], 'v_w_o': out['v_w_o'], 'v_g_mlp': out['v_g_mlp'], 'v_w_ff_in': out['v_w_ff_in'], 'v_w_ff_out': out['v_w_ff_out'], 'v_g_final': out['v_g_final']}


def _loss(weights, diff, rest, loss_target):
    with _jax.named_scope("forward"):
        args = {**rest, TWIN_DIFF_INPUT: diff, **{k: w.astype(_WEIGHT_DTYPES[k]) for k, w in weights.items()}}
        y = _forward(args)
    with _jax.named_scope("loss_head"):
        err = _jnp.square(y.astype(_jnp.float32) - loss_target)
        return 0.5 * _jnp.sum(_jnp.mean(err, axis=-1)) if err.ndim else 0.5 * err


def _adamw(w, g, m, v):
    m = ADAM_B1 * m + (1.0 - ADAM_B1) * g
    v = ADAM_B2 * v + (1.0 - ADAM_B2) * _jnp.square(g)
    m_hat = m / (1.0 - ADAM_B1 ** ADAM_STEP)
    v_hat = v / (1.0 - ADAM_B2 ** ADAM_STEP)
    delta = -ADAM_LR * (m_hat / (_jnp.sqrt(v_hat) + ADAM_EPS) + ADAM_WD * w)
    return delta, m, v


def reference(x, w_in, conv_w, sinks, g_mix, g_group, w_o, g_mlp, w_ff_in, w_ff_out, g_final, loss_target, m_w_in, m_conv_w, m_sinks, m_g_mix, m_g_group, m_w_o, m_g_mlp, m_w_ff_in, m_w_ff_out, m_g_final, v_w_in, v_conv_w, v_sinks, v_g_mix, v_g_group, v_w_o, v_g_mlp, v_w_ff_in, v_w_ff_out, v_g_final):
    given = dict(x=x, w_in=w_in, conv_w=conv_w, sinks=sinks, g_mix=g_mix, g_group=g_group, w_o=w_o, g_mlp=g_mlp, w_ff_in=w_ff_in, w_ff_out=w_ff_out, g_final=g_final, loss_target=loss_target, m_w_in=m_w_in, m_conv_w=m_conv_w, m_sinks=m_sinks, m_g_mix=m_g_mix, m_g_group=m_g_group, m_w_o=m_w_o, m_g_mlp=m_g_mlp, m_w_ff_in=m_w_ff_in, m_w_ff_out=m_w_ff_out, m_g_final=m_g_final, v_w_in=v_w_in, v_conv_w=v_conv_w, v_sinks=v_sinks, v_g_mix=v_g_mix, v_g_group=v_g_group, v_w_o=v_w_o, v_g_mlp=v_g_mlp, v_w_ff_in=v_w_ff_in, v_w_ff_out=v_w_ff_out, v_g_final=v_g_final)
    weights = {n: given[n] for n in TWIN_WEIGHTS}
    shared = {n: given[n] for n in SHARED_INPUTS}
    per_example = {n: given[n] for n in ['x']}
    grad_fn = _jax.value_and_grad(_loss, argnums=(0, 1))

    def one_microbatch(ex, loss_target):
        ex = dict(ex)
        diff = ex.pop(TWIN_DIFF_INPUT)
        return grad_fn(weights, diff, {**shared, **ex}, loss_target)

    if N_MICROBATCH == 1:
        loss, (grad_w, grad_x) = one_microbatch(per_example, given["loss_target"])
    else:
        def body(carry, xs):
            loss_sum, grad_sum = carry
            l_k, (gw_k, gx_k) = one_microbatch(xs[0], xs[1])
            with _jax.named_scope("update"):
                return (loss_sum + l_k, _jax.tree.map(_jnp.add, grad_sum, gw_k)), gx_k

        init = (_jnp.zeros((), _jnp.float32), _jax.tree.map(_jnp.zeros_like, weights))
        (loss, grad_w), grad_x = _jax.lax.scan(body, init, (per_example, given["loss_target"]))
    with _jax.named_scope("update"):
        delta_w, new_m, new_v = {}, {}, {}
        for n in TWIN_WEIGHTS:
            delta_w[n], new_m[n], new_v[n] = _adamw(weights[n], grad_w[n], given["m_" + n], given["v_" + n])
    return (loss, grad_x, *[grad_w[n] for n in TWIN_WEIGHTS], *[delta_w[n] for n in TWIN_WEIGHTS],
            *[new_m[n] for n in TWIN_WEIGHTS], *[new_v[n] for n in TWIN_WEIGHTS])
```

```python
import functools

import jax
import jax.numpy as jnp
from jax import lax
from jax.experimental import pallas as pl
from jax.experimental.pallas import tpu as pltpu

HEAD_DIM = 64
N_HEADS = 6
C_GROUP = 3
A_WIDTH = N_HEADS * HEAD_DIM
C_KV_WIDTH = 2 * HEAD_DIM
CONV_CH = 256
ZA_W = 3 * A_WIDTH
ZB_W = 3 * CONV_CH
ZC_W = A_WIDTH + 2 * C_KV_WIDTH
IN_WIDTH = ZA_W + ZB_W + ZC_W
MIX_WIDTH = A_WIDTH + CONV_CH + A_WIDTH
DILATIONS = (1, 4, 16)
A_MAX_DIST = 128
C_MAX_DIST = 127
TQ = 128
EPS = 1e-6
SCALE = HEAD_DIM ** -0.5
NEG = -1e30
HALO = 8

ADAM_LR = 0.001
ADAM_B1 = 0.9
ADAM_B2 = 0.999
ADAM_EPS = 1e-08
ADAM_WD = 0.01
ADAM_STEP = 10

BF = jnp.bfloat16
F32 = jnp.float32
MESH = pl.DeviceIdType.MESH
VMEM_LIMIT = 56 * 1024 * 1024


def _cparams(*sem):
    return pltpu.CompilerParams(dimension_semantics=sem, vmem_limit_bytes=VMEM_LIMIT)


def _nt(a, b):
    return lax.dot_general(a, b, (((1,), (1,)), ((), ())), preferred_element_type=F32)


def _tn(a, b):
    return lax.dot_general(a, b, (((0,), (0,)), ((), ())), preferred_element_type=F32)


def _nn(a, b):
    return jnp.dot(a, b, preferred_element_type=F32)


def _rows(tb, w):
    return pl.BlockSpec((tb, w), lambda i: (i, 0))


def _whole(shape):
    return pl.BlockSpec(shape, lambda *_: (0,) * len(shape))


def _layer(shape, l):
    return pl.BlockSpec((None,) + shape, lambda *_: (l,) + (0,) * len(shape))


def _rms_scale(v):
    return lax.rsqrt(jnp.mean(v * v, axis=-1, keepdims=True) + EPS)


def _norm_bwd(dxhat, xhat, r):
    return r * (dxhat - xhat * jnp.mean(dxhat * xhat, axis=-1, keepdims=True))


def _qkv_fwd(x, g, w_all, l, tb):
    s, d = x.shape

    def body(x_ref, g_ref, w_ref, h_ref, za_ref, zb_ref, zc_ref):
        xv = x_ref[...]
        h = ((xv * _rms_scale(xv)) * g_ref[...]).astype(BF)
        h_ref[...] = h
        z = _nn(h, w_ref[...])
        za_ref[...] = z[:, :ZA_W].astype(BF)
        zb_ref[...] = z[:, ZA_W:ZA_W + ZB_W]
        zc_ref[...] = z[:, ZA_W + ZB_W:].astype(BF)

    return pl.pallas_call(
        body, grid=(s // tb,), name="qkv_fwd",
        in_specs=[_rows(tb, d), _whole((1, d)), _layer((d, IN_WIDTH), l)],
        out_specs=[_rows(tb, d), _rows(tb, ZA_W), _rows(tb, ZB_W), _rows(tb, ZC_W)],
        out_shape=[jax.ShapeDtypeStruct((s, d), BF), jax.ShapeDtypeStruct((s, ZA_W), BF),
                   jax.ShapeDtypeStruct((s, ZB_W), F32), jax.ShapeDtypeStruct((s, ZC_W), BF)],
        compiler_params=_cparams("parallel"),
    )(x, g, w_all)


def _band_mask(b, max_dist):
    qi = lax.broadcasted_iota(jnp.int32, (TQ, 2 * TQ), 0)
    kj = lax.broadcasted_iota(jnp.int32, (TQ, 2 * TQ), 1)
    dist = TQ + qi - kj
    return (dist >= 0) & (dist <= max_dist) & ((kj >= TQ) | (b > 0))


def _attn_specs(dil, zw, kw, kcol, vcol):
    nq, nk = zw // A_WIDTH, zw // kw
    q = pl.BlockSpec((TQ, A_WIDTH), lambda r, b: (b, r * nq))
    kp = pl.BlockSpec((TQ, kw), lambda r, b: (jnp.maximum(b - 1, 0), r * nk + kcol))
    kc = pl.BlockSpec((TQ, kw), lambda r, b: (b, r * nk + kcol))
    vp = pl.BlockSpec((TQ, kw), lambda r, b: (jnp.maximum(b - 1, 0), r * nk + vcol))
    vc = pl.BlockSpec((TQ, kw), lambda r, b: (b, r * nk + vcol))
    return [q, kp, kc, vp, vc]


def _head_spec(w=A_WIDTH):
    return pl.BlockSpec((TQ, w), lambda r, b: (b, r))


def _hs(h):
    return slice(h * HEAD_DIM, (h + 1) * HEAD_DIM)


def _attn_fwd(z, dil, kw, kcol, vcol, n_rep, max_dist, state, sink, last, name):
    s, zw = z.shape
    sub = s // dil
    zv = z.reshape(sub, dil * zw)
    have_state, have_sink = state is not None, sink is not None

    def body(*refs):
        q_ref, kp_ref, kc_ref, vp_ref, vc_ref = refs[:5]
        pos = 5
        if have_state:
            acc_in, m_in, l_in = refs[pos:pos + 3]
            pos += 3
        if have_sink:
            sink_ref = refs[pos]
            pos += 1
        outs = refs[pos:]
        mask = _band_mask(pl.program_id(1), max_dist)
        for h in range(N_HEADS):
            kh = h // n_rep
            q = q_ref[:, _hs(h)]
            k2 = jnp.concatenate([kp_ref[:, _hs(kh)], kc_ref[:, _hs(kh)]], axis=0)
            v2 = jnp.concatenate([vp_ref[:, _hs(kh)], vc_ref[:, _hs(kh)]], axis=0)
            sc = jnp.where(mask, _nt(q, k2) * SCALE, NEG)
            m_new = jnp.max(sc, axis=1, keepdims=True)
            if have_sink:
                sk = sink_ref[0:1, h:h + 1]
                m_new = jnp.maximum(m_new, sk)
            if have_state:
                m_old = m_in[:, h * HEAD_DIM:h * HEAD_DIM + 1]
                m_new = jnp.maximum(m_new, m_old)
            p = jnp.exp(sc - m_new)
            l_new = jnp.sum(p, axis=1, keepdims=True)
            acc = _nn(p.astype(BF), v2)
            if have_state:
                alpha = jnp.exp(m_old - m_new)
                l_new = l_new + alpha * l_in[:, h * HEAD_DIM:h * HEAD_DIM + 1]
                acc = acc + alpha * acc_in[:, _hs(h)]
            if have_sink:
                l_new = l_new + jnp.exp(sk - m_new)
            if last:
                outs[0][:, _hs(h)] = acc / l_new
                outs[1][:, _hs(h)] = jnp.broadcast_to(m_new + jnp.log(l_new), (TQ, HEAD_DIM))
            else:
                outs[0][:, _hs(h)] = acc
                outs[1][:, _hs(h)] = jnp.broadcast_to(m_new, (TQ, HEAD_DIM))
                outs[2][:, _hs(h)] = jnp.broadcast_to(l_new, (TQ, HEAD_DIM))

    args = [zv] * 5
    in_specs = _attn_specs(dil, zw, kw, kcol, vcol)
    if have_state:
        args += [a.reshape(sub, dil * A_WIDTH) for a in state]
        in_specs += [_head_spec()] * 3
    if have_sink:
        args.append(sink)
        in_specs.append(_whole((HALO, 128)))
    n_out = 2 if last else 3
    res = pl.pallas_call(
        body, grid=(dil, sub // TQ), name=name, in_specs=in_specs,
        out_specs=[_head_spec()] * n_out,
        out_shape=[jax.ShapeDtypeStruct((sub, dil * A_WIDTH), F32)] * n_out,
        compiler_params=_cparams("parallel", "parallel"),
    )(*args)
    return [a.reshape(s, A_WIDTH) for a in res]


def _shift_down(v, n, halo):
    rows = v.shape[0]
    out = pltpu.roll(v, n, 0)
    row = lax.broadcasted_iota(jnp.int32, v.shape, 0)
    for t in range(n):
        out = jnp.where(row == t, halo[HALO - n + t:HALO - n + t + 1, :], out)
    return out


def _shift_up(v, n, halo):
    rows = v.shape[0]
    out = pltpu.roll(v, rows - n, 0)
    row = lax.broadcasted_iota(jnp.int32, v.shape, 0)
    for t in range(n):
        out = jnp.where(row == rows - n + t, halo[t:t + 1, :], out)
    return out


def _conv_parts(zb, zb_prev, cw):
    gb, gc, xb = zb[:, :CONV_CH], zb[:, CONV_CH:2 * CONV_CH], zb[:, 2 * CONV_CH:]
    u = gc * xb
    uh = zb_prev[:, CONV_CH:2 * CONV_CH] * zb_prev[:, 2 * CONV_CH:]
    u1 = _shift_down(u, 1, uh)
    u2 = _shift_down(u, 2, uh)
    c = cw[0:1, :] * u2 + cw[1:2, :] * u1 + cw[2:3, :] * u
    return gb, gc, xb, u, u1, u2, c


def _prev_halo(tb, w):
    return pl.BlockSpec((HALO, w), lambda i: (jnp.maximum(i * (tb // HALO) - 1, 0), 0))


def _next_halo(tb, w, nblk):
    return pl.BlockSpec((HALO, w), lambda i: (jnp.minimum((i + 1) * (tb // HALO), nblk * (tb // HALO) - 1), 0))


def _mix_fwd(x, ya, yc, zb, cw, gg, wo_all, l, tb):
    s, d = x.shape

    def body(x_ref, ya_ref, yc_ref, zb_ref, zbp_ref, cw_ref, gg_ref, wo_ref, x1_ref, yb_ref):
        i = pl.program_id(0)
        zbp = jnp.where(i > 0, zbp_ref[...], 0.0)
        gb, _, _, _, _, _, c = _conv_parts(zb_ref[...], zbp, cw_ref[...])
        yb = gb * c
        yb_ref[...] = yb
        ya, yc = ya_ref[...], yc_ref[...]
        n = jnp.concatenate([ya * _rms_scale(ya), yb * _rms_scale(yb), yc * _rms_scale(yc)], axis=1)
        n = (n * gg_ref[...]).astype(BF)
        x1_ref[...] = x_ref[...] + _nn(n, wo_ref[...])

    return pl.pallas_call(
        body, grid=(s // tb,), name="mix_fwd",
        in_specs=[_rows(tb, d), _rows(tb, A_WIDTH), _rows(tb, A_WIDTH), _rows(tb, ZB_W), _prev_halo(tb, ZB_W),
                  _whole((HALO, CONV_CH)), _whole((1, MIX_WIDTH)), _layer((MIX_WIDTH, d), l)],
        out_specs=[_rows(tb, d), _rows(tb, CONV_CH)],
        out_shape=[jax.ShapeDtypeStruct((s, d), F32), jax.ShapeDtypeStruct((s, CONV_CH), F32)],
        compiler_params=_cparams("parallel"),
    )(x, ya, yc, zb, zb, cw, gg, wo_all)


def _mlp_fwd(x1, g, w1_all, w2_all, l, tb, tf):
    s, d = x1.shape
    ff = w1_all.shape[2]
    nj = ff // tf

    def body(x_ref, g_ref, w1_ref, w2_ref, x2_ref, h2_ref, ap_ref, acc):
        j = pl.program_id(1)

        @pl.when(j == 0)
        def _():
            xv = x_ref[...]
            h2_ref[...] = ((xv * _rms_scale(xv)) * g_ref[...]).astype(BF)
            acc[...] = jnp.zeros_like(acc)

        ap = _nn(h2_ref[...], w1_ref[...])
        ap_ref[...] = ap.astype(BF)
        a = jnp.square(jnp.maximum(ap, 0.0)).astype(BF)
        acc[...] += _nn(a, w2_ref[...])

        @pl.when(j == nj - 1)
        def _():
            x2_ref[...] = x_ref[...] + acc[...]

    return pl.pallas_call(
        body, grid=(s // tb, nj), name="mlp_fwd",
        in_specs=[pl.BlockSpec((tb, d), lambda i, j: (i, 0)), _whole((1, d)),
                  pl.BlockSpec((None, d, tf), lambda i, j: (l, 0, j)),
                  pl.BlockSpec((None, tf, d), lambda i, j: (l, j, 0))],
        out_specs=[pl.BlockSpec((tb, d), lambda i, j: (i, 0)), pl.BlockSpec((tb, d), lambda i, j: (i, 0)),
                   pl.BlockSpec((tb, tf), lambda i, j: (i, j))],
        out_shape=[jax.ShapeDtypeStruct((s, d), F32), jax.ShapeDtypeStruct((s, d), BF),
                   jax.ShapeDtypeStruct((s, ff), BF)],
        scratch_shapes=[pltpu.VMEM((tb, d), F32)],
        compiler_params=_cparams("parallel", "arbitrary"),
    )(x1, g, w1_all, w2_all)


def _loss_head(x, g, tgt, tb):
    s, d = x.shape

    def body(x_ref, g_ref, t_ref, dx_ref, loss_ref, dg_ref):
        i = pl.program_id(0)

        @pl.when(i == 0)
        def _():
            loss_ref[...] = jnp.zeros_like(loss_ref)
            dg_ref[...] = jnp.zeros_like(dg_ref)

        xv = x_ref[...]
        r = _rms_scale(xv)
        xhat = xv * r
        err = xhat * g_ref[...] - t_ref[...]
        part = jnp.sum(jnp.mean(jnp.square(err), axis=-1, keepdims=True), axis=0, keepdims=True)
        loss_ref[...] += 0.5 * part
        dy = err * (1.0 / d)
        dg_ref[...] += jnp.sum(dy * xhat, axis=0, keepdims=True)
        dx_ref[...] = _norm_bwd(dy * g_ref[...], xhat, r)

    return pl.pallas_call(
        body, grid=(s // tb,), name="loss_head",
        in_specs=[_rows(tb, d), _whole((1, d)), _rows(tb, d)],
        out_specs=[_rows(tb, d), _whole((HALO, 128)), _whole((HALO, d))],
        out_shape=[jax.ShapeDtypeStruct((s, d), F32), jax.ShapeDtypeStruct((HALO, 128), F32),
                   jax.ShapeDtypeStruct((HALO, d), F32)],
        compiler_params=_cparams("arbitrary"),
    )(x, g, tgt)


def _mlp_bwd(dx2, x1, ap, g, w1_all, w2_all, l, tb, tf):
    s, d = x1.shape
    ff = ap.shape[1]
    nj = ff // tf

    def body(dx2_ref, x1_ref, ap_ref, g_ref, w1_ref, w2_ref, dx1_ref, dap_ref, dg_ref, acc):
        i, j = pl.program_id(0), pl.program_id(1)

        @pl.when((i == 0) & (j == 0))
        def _():
            dg_ref[...] = jnp.zeros_like(dg_ref)

        @pl.when(j == 0)
        def _():
            acc[...] = jnp.zeros_like(acc)

        da = _nt(dx2_ref[...].astype(BF), w2_ref[...])
        dap = (da * (2.0 * jnp.maximum(ap_ref[...].astype(F32), 0.0))).astype(BF)
        dap_ref[...] = dap
        acc[...] += _nt(dap, w1_ref[...])

        @pl.when(j == nj - 1)
        def _():
            xv = x1_ref[...]
            r = _rms_scale(xv)
            xhat = xv * r
            dh = acc[...]
            dg_ref[...] += jnp.sum(dh * xhat, axis=0, keepdims=True)
            dx1_ref[...] = dx2_ref[...] + _norm_bwd(dh * g_ref[...], xhat, r)

    return pl.pallas_call(
        body, grid=(s // tb, nj), name="mlp_bwd",
        in_specs=[pl.BlockSpec((tb, d), lambda i, j: (i, 0)), pl.BlockSpec((tb, d), lambda i, j: (i, 0)),
                  pl.BlockSpec((tb, tf), lambda i, j: (i, j)),
                  _whole((1, d)), pl.BlockSpec((None, d, tf), lambda i, j: (l, 0, j)),
                  pl.BlockSpec((None, tf, d), lambda i, j: (l, j, 0))],
        out_specs=[pl.BlockSpec((tb, d), lambda i, j: (i, 0)), pl.BlockSpec((tb, tf), lambda i, j: (i, j)),
                   _whole((HALO, d))],
        out_shape=[jax.ShapeDtypeStruct((s, d), F32), jax.ShapeDtypeStruct((s, ff), BF),
                   jax.ShapeDtypeStruct((HALO, d), F32)],
        scratch_shapes=[pltpu.VMEM((tb, d), F32)],
        compiler_params=_cparams("arbitrary", "arbitrary"),
    )(dx2, x1, ap, g, w1_all, w2_all)


def _wgrad(a, b, tm, tn, ts, name, relu2=False):
    s, m = a.shape
    n = b.shape[1]
    ns = s // ts

    def body(a_ref, b_ref, o_ref, acc):
        k = pl.program_id(2)

        @pl.when(k == 0)
        def _():
            acc[...] = jnp.zeros_like(acc)

        av = a_ref[...]
        if relu2:
            av = jnp.square(jnp.maximum(av.astype(F32), 0.0)).astype(BF)
        acc[...] += _tn(av, b_ref[...].astype(BF))

        @pl.when(k == ns - 1)
        def _():
            o_ref[...] = acc[...].astype(BF)

    return pl.pallas_call(
        body, grid=(m // tm, n // tn, ns), name=name,
        in_specs=[pl.BlockSpec((ts, tm), lambda i, j, k: (k, i)), pl.BlockSpec((ts, tn), lambda i, j, k: (k, j))],
        out_specs=pl.BlockSpec((tm, tn), lambda i, j, k: (i, j)),
        out_shape=jax.ShapeDtypeStruct((m, n), BF),
        scratch_shapes=[pltpu.VMEM((tm, tn), F32)],
        compiler_params=_cparams("parallel", "parallel", "arbitrary"),
    )(a, b)


def _mix_bwd(dx1, ya, yb, yc, gg, wo_all, l, tb):
    s, d = dx1.shape

    def body(dx_ref, ya_ref, yb_ref, yc_ref, gg_ref, wo_ref, n_ref, dya_ref, dyc_ref, da_ref, dc_ref, dyb_ref, dg_ref):
        i = pl.program_id(0)

        @pl.when(i == 0)
        def _():
            dg_ref[...] = jnp.zeros_like(dg_ref)

        dn = _nt(dx_ref[...].astype(BF), wo_ref[...])
        ys = [ya_ref[...], yb_ref[...], yc_ref[...]]
        rs = [_rms_scale(v) for v in ys]
        nhat = jnp.concatenate([v * r for v, r in zip(ys, rs)], axis=1)
        gg = gg_ref[...]
        n_ref[...] = (nhat * gg).astype(BF)
        dg_ref[...] += jnp.sum(dn * nhat, axis=0, keepdims=True)
        dnh = dn * gg
        bounds = [(0, A_WIDTH), (A_WIDTH, A_WIDTH + CONV_CH), (A_WIDTH + CONV_CH, MIX_WIDTH)]
        dys = [_norm_bwd(dnh[:, lo:hi], nhat[:, lo:hi], r) for (lo, hi), r in zip(bounds, rs)]
        dyb_ref[...] = dys[1]
        for dy, y, dy_ref, dd_ref in ((dys[0], ys[0], dya_ref, da_ref), (dys[2], ys[2], dyc_ref, dc_ref)):
            dy_ref[...] = dy.astype(BF)
            t = dy * y
            for h in range(N_HEADS):
                dd_ref[:, _hs(h)] = jnp.broadcast_to(jnp.sum(t[:, _hs(h)], axis=1, keepdims=True), (tb, HEAD_DIM))

    return pl.pallas_call(
        body, grid=(s // tb,), name="mix_bwd",
        in_specs=[_rows(tb, d), _rows(tb, A_WIDTH), _rows(tb, CONV_CH), _rows(tb, A_WIDTH), _whole((1, MIX_WIDTH)),
                  _layer((MIX_WIDTH, d), l)],
        out_specs=[_rows(tb, MIX_WIDTH), _rows(tb, A_WIDTH), _rows(tb, A_WIDTH), _rows(tb, A_WIDTH),
                   _rows(tb, A_WIDTH), _rows(tb, CONV_CH), _whole((HALO, MIX_WIDTH))],
        out_shape=[jax.ShapeDtypeStruct((s, MIX_WIDTH), BF), jax.ShapeDtypeStruct((s, A_WIDTH), BF),
                   jax.ShapeDtypeStruct((s, A_WIDTH), BF), jax.ShapeDtypeStruct((s, A_WIDTH), F32),
                   jax.ShapeDtypeStruct((s, A_WIDTH), F32), jax.ShapeDtypeStruct((s, CONV_CH), F32),
                   jax.ShapeDtypeStruct((HALO, MIX_WIDTH), F32)],
        compiler_params=_cparams("arbitrary"),
    )(dx1, ya, yb, yc, gg, wo_all)


def _attn_bwd(z, dy, lse, dd, dil, kw, kcol, vcol, n_rep, max_dist, sink, name):
    s, zw = z.shape
    sub = s // dil
    zv = z.reshape(sub, dil * zw)
    have_sink = sink is not None
    n_kv = N_HEADS // n_rep

    def body(*refs):
        q_ref, kp_ref, kc_ref, vp_ref, vc_ref, dy_ref, lse_ref, dd_ref = refs[:8]
        pos = 8
        if have_sink:
            sink_ref = refs[pos]
            pos += 1
        dq_ref, dkp_ref, dkc_ref, dvp_ref, dvc_ref = refs[pos:pos + 5]
        b = pl.program_id(1)
        mask = _band_mask(b, max_dist)
        if have_sink:
            dsink_ref = refs[pos + 5]

            @pl.when(b == 0)
            def _():
                dsink_ref[...] = jnp.zeros_like(dsink_ref)

            row = lax.broadcasted_iota(jnp.int32, (HALO, 128), 0)
            lane = lax.broadcasted_iota(jnp.int32, (HALO, 128), 1)
        for kh in range(n_kv):
            k2 = jnp.concatenate([kp_ref[:, _hs(kh)], kc_ref[:, _hs(kh)]], axis=0)
            v2 = jnp.concatenate([vp_ref[:, _hs(kh)], vc_ref[:, _hs(kh)]], axis=0)
            dk2 = jnp.zeros((2 * TQ, HEAD_DIM), F32)
            dv2 = jnp.zeros((2 * TQ, HEAD_DIM), F32)
            for h in range(kh * n_rep, (kh + 1) * n_rep):
                q = q_ref[:, _hs(h)]
                lse_h = lse_ref[:, h * HEAD_DIM:h * HEAD_DIM + 1]
                dd_h = dd_ref[:, h * HEAD_DIM:h * HEAD_DIM + 1]
                dyh = dy_ref[:, _hs(h)]
                sc = jnp.where(mask, _nt(q, k2) * SCALE, NEG)
                p = jnp.exp(sc - lse_h)
                dp = _nt(dyh, v2)
                ds = ((p * (dp - dd_h)) * SCALE).astype(BF)
                dq_ref[:, _hs(h)] = _nn(ds, k2)
                dk2 = dk2 + _tn(ds, q)
                dv2 = dv2 + _tn(p.astype(BF), dyh)
                if have_sink:
                    sk = sink_ref[0:1, h:h + 1]
                    val = -jnp.sum(jnp.exp(sk - lse_h) * dd_h, axis=0, keepdims=True)
                    dsink_ref[...] += jnp.where((row == 0) & (lane == h), val, 0.0)
            dkp_ref[:, _hs(kh)] = dk2[:TQ]
            dkc_ref[:, _hs(kh)] = dk2[TQ:]
            dvp_ref[:, _hs(kh)] = dv2[:TQ]
            dvc_ref[:, _hs(kh)] = dv2[TQ:]

    args = [zv] * 5 + [a.reshape(sub, dil * A_WIDTH) for a in (dy, lse, dd)]
    in_specs = _attn_specs(dil, zw, kw, kcol, vcol) + [_head_spec()] * 3
    out_specs = [_head_spec()] + [_head_spec(kw)] * 4
    out_shape = [jax.ShapeDtypeStruct((sub, dil * A_WIDTH), F32)] + [jax.ShapeDtypeStruct((sub, dil * kw), F32)] * 4
    if have_sink:
        args.append(sink)
        in_specs.append(_whole((HALO, 128)))
        out_specs.append(_whole((HALO, 128)))
        out_shape.append(jax.ShapeDtypeStruct((HALO, 128), F32))
    res = pl.pallas_call(
        body, grid=(dil, sub // TQ), name=name, in_specs=in_specs, out_specs=out_specs, out_shape=out_shape,
        compiler_params=_cparams("arbitrary", "arbitrary"),
    )(*args)
    outs = [res[0].reshape(s, A_WIDTH)] + [a.reshape(s, kw) for a in res[1:5]]
    return outs + list(res[5:])


def _dz_assemble(parts_a, parts_c, dyb, zb, cw):
    s = zb.shape[0]
    nb = s // TQ

    def shifted(w, dil):
        return pl.BlockSpec((TQ, w), lambda i: (jnp.minimum(i + dil, nb - 1), 0))

    args, in_specs = [], []
    for dil, (dq, dkp, dkc, dvp, dvc) in zip(DILATIONS + (1,), parts_a + [parts_c]):
        w = dkp.shape[1]
        args += [dq, dkp, dkc, dvp, dvc]
        in_specs += [_rows(TQ, A_WIDTH), shifted(w, dil), _rows(TQ, w), shifted(w, dil), _rows(TQ, w)]
    args += [dyb, dyb, zb, zb, zb, cw]
    in_specs += [_rows(TQ, CONV_CH), _next_halo(TQ, CONV_CH, nb), _rows(TQ, ZB_W), _prev_halo(TQ, ZB_W),
                 _next_halo(TQ, ZB_W, nb), _whole((HALO, CONV_CH))]
    n_att = 20

    def body(*refs):
        att = refs[:n_att]
        dyb_ref, dybn_ref, zb_ref, zbp_ref, zbn_ref, cw_ref, dz_ref, dcw_ref = refs[n_att:]
        i = pl.program_id(0)

        @pl.when(i == 0)
        def _():
            dcw_ref[...] = jnp.zeros_like(dcw_ref)

        dq = jnp.zeros((TQ, A_WIDTH), F32)
        dk = jnp.zeros((TQ, A_WIDTH), F32)
        dv = jnp.zeros((TQ, A_WIDTH), F32)
        for p, dil in enumerate(DILATIONS):
            dq_r, dkp_r, dkc_r, dvp_r, dvc_r = att[5 * p:5 * p + 5]
            live = i + dil < nb
            dq = dq + dq_r[...]
            dk = dk + dkc_r[...] + jnp.where(live, dkp_r[...], 0.0)
            dv = dv + dvc_r[...] + jnp.where(live, dvp_r[...], 0.0)
        dz_ref[:, 0:A_WIDTH] = dq.astype(BF)
        dz_ref[:, A_WIDTH:2 * A_WIDTH] = dk.astype(BF)
        dz_ref[:, 2 * A_WIDTH:ZA_W] = dv.astype(BF)
        dq_r, dkp_r, dkc_r, dvp_r, dvc_r = att[15:20]
        live = i + 1 < nb
        c0 = ZA_W + ZB_W
        dz_ref[:, c0:c0 + A_WIDTH] = dq_r[...].astype(BF)
        dz_ref[:, c0 + A_WIDTH:c0 + A_WIDTH + C_KV_WIDTH] = (dkc_r[...] + jnp.where(live, dkp_r[...], 0.0)).astype(BF)
        dz_ref[:, c0 + A_WIDTH + C_KV_WIDTH:IN_WIDTH] = (dvc_r[...] + jnp.where(live, dvp_r[...], 0.0)).astype(BF)

        cw = cw_ref[...]
        zbp = jnp.where(i > 0, zbp_ref[...], 0.0)
        gb, gc, xb, u, u1, u2, c = _conv_parts(zb_ref[...], zbp, cw)
        dyb = dyb_ref[...]
        dcv = dyb * gb
        dcn = jnp.where(i + 1 < nb, dybn_ref[...] * zbn_ref[:, :CONV_CH], 0.0)
        du = cw[2:3, :] * dcv + cw[1:2, :] * _shift_up(dcv, 1, dcn) + cw[0:1, :] * _shift_up(dcv, 2, dcn)
        dz_ref[:, ZA_W:ZA_W + CONV_CH] = (dyb * c).astype(BF)
        dz_ref[:, ZA_W + CONV_CH:ZA_W + 2 * CONV_CH] = (du * xb).astype(BF)
        dz_ref[:, ZA_W + 2 * CONV_CH:c0] = (du * gc).astype(BF)
        row = lax.broadcasted_iota(jnp.int32, (HALO, CONV_CH), 0)
        upd = jnp.zeros((HALO, CONV_CH), F32)
        for t, uu in enumerate((u2, u1, u)):
            upd = jnp.where(row == t, jnp.sum(dcv * uu, axis=0, keepdims=True), upd)
        dcw_ref[...] += upd

    return pl.pallas_call(
        body, grid=(nb,), name="dz_assemble", in_specs=in_specs,
        out_specs=[_rows(TQ, IN_WIDTH), _whole((HALO, CONV_CH))],
        out_shape=[jax.ShapeDtypeStruct((s, IN_WIDTH), BF), jax.ShapeDtypeStruct((HALO, CONV_CH), F32)],
        compiler_params=_cparams("arbitrary"),
    )(*args)


def _qkv_bwd(dz, dx1, x, g, w_all, l, tb):
    s, d = x.shape

    def body(dz_ref, dx1_ref, x_ref, g_ref, w_ref, dx_ref, dg_ref):
        i = pl.program_id(0)

        @pl.when(i == 0)
        def _():
            dg_ref[...] = jnp.zeros_like(dg_ref)

        dh = _nt(dz_ref[...], w_ref[...])
        xv = x_ref[...]
        r = _rms_scale(xv)
        xhat = xv * r
        dg_ref[...] += jnp.sum(dh * xhat, axis=0, keepdims=True)
        dx_ref[...] = dx1_ref[...] + _norm_bwd(dh * g_ref[...], xhat, r)

    return pl.pallas_call(
        body, grid=(s // tb,), name="qkv_bwd",
        in_specs=[_rows(tb, IN_WIDTH), _rows(tb, d), _rows(tb, d), _whole((1, d)), _layer((d, IN_WIDTH), l)],
        out_specs=[_rows(tb, d), _whole((HALO, d))],
        out_shape=[jax.ShapeDtypeStruct((s, d), F32), jax.ShapeDtypeStruct((HALO, d), F32)],
        compiler_params=_cparams("arbitrary"),
    )(dz, dx1, x, g, w_all)


def _tile_rows(rows):
    return jnp.pad(rows, ((0, HALO - rows.shape[0]), (0, 0)))


def _local_step(x, tgt, w_in, w_o, w1, w2, conv_w, sinks, g_mix, g_group, g_mlp, g_final):
    s, d = x.shape
    ff = w1.shape[2]
    depth = w_in.shape[0]
    tb = min(512, s)
    tf = min(1024, ff)
    saved = []
    for l in range(depth):
        cw = _tile_rows(conv_w[l])
        sk = jnp.pad(sinks[l].reshape(1, N_HEADS), ((0, HALO - 1), (0, 128 - N_HEADS)))
        h, za, zb, zc = _qkv_fwd(x, g_mix[l][None], w_in, l, tb)
        state = None
        for p, dil in enumerate(DILATIONS):
            state = _attn_fwd(za, dil, A_WIDTH, 1, 2, 1, A_MAX_DIST, state, None, p == len(DILATIONS) - 1,
                              "attn_a_fwd_%d" % dil)
        ya, lse_a = state
        yc, lse_c = _attn_fwd(zc, 1, C_KV_WIDTH, 3, 4, C_GROUP, C_MAX_DIST, None, sk, True, "attn_c_fwd")
        x1, yb = _mix_fwd(x, ya, yc, zb, cw, g_group[l][None], w_o, l, tb)
        x2, h2, ap = _mlp_fwd(x1, g_mlp[l][None], w1, w2, l, tb, tf)
        saved.append((x, h, za, zb, zc, ya, lse_a, yc, lse_c, yb, x1, h2, ap, cw, sk))
        x = x2
    dx, loss_tile, dg_final = _loss_head(x, g_final[None], tgt, tb)
    grads = [None] * depth
    for l in reversed(range(depth)):
        x0, h, za, zb, zc, ya, lse_a, yc, lse_c, yb, x1, h2, ap, cw, sk = saved[l]
        dx1, dap, dg_mlp = _mlp_bwd(dx, x1, ap, g_mlp[l][None], w1, w2, l, tb, tf)
        gw2 = _wgrad(ap, dx, min(1024, ff), d, tb, "wgrad_ff_out", relu2=True)
        gw1 = _wgrad(h2, dap, d, min(1024, ff), tb, "wgrad_ff_in")
        n, dya, dyc, dd_a, dd_c, dyb, dg_group = _mix_bwd(dx1, ya, yb, yc, g_group[l][None], w_o, l, tb)
        gwo = _wgrad(n, dx1, MIX_WIDTH, d, tb, "wgrad_o")
        parts_a = [_attn_bwd(za, dya, lse_a, dd_a, dil, A_WIDTH, 1, 2, 1, A_MAX_DIST, None, "attn_a_bwd_%d" % dil)
                   for dil in DILATIONS]
        *parts_c, dsink = _attn_bwd(zc, dyc, lse_c, dd_c, 1, C_KV_WIDTH, 3, 4, C_GROUP, C_MAX_DIST, sk, "attn_c_bwd")
        dz, dcw = _dz_assemble(parts_a, parts_c, dyb, zb, cw)
        dx, dg_mix = _qkv_bwd(dz, dx1, x0, g_mix[l][None], w_in, l, tb)
        gwin = _wgrad(h, dz, d, IN_WIDTH // 4, tb, "wgrad_in")
        grads[l] = (gwin, gwo, gw1, gw2, dcw, dsink, dg_mix, dg_group, dg_mlp)
    return loss_tile, dx, grads, dg_final


ANY = pl.BlockSpec(memory_space=pl.ANY)
SHARD_AXES = (2, 1, 2, 1)
N_BIG = len(SHARD_AXES)
N_CHIPS = 4
N_DEV = 8


def _mesh_pos():
    return lax.axis_index("x"), lax.axis_index("y"), lax.axis_index("c")


def _flip(v, bit):
    return 1 - v if bit else v


def _shard_of(ref, w, layer, chip, n):
    start = pl.multiple_of(chip * n, 128)
    if SHARD_AXES[w] == 2:
        return ref.at[layer, :, pl.ds(start, n)]
    return ref.at[layer, pl.ds(start, n), :]


def _gather_weights(shards, conv_tile):
    widths = [sh.shape[ax] for sh, ax in zip(shards, SHARD_AXES)]
    full_shapes = []
    for sh, ax in zip(shards, SHARD_AXES):
        shape = list(sh.shape)
        shape[ax] *= N_CHIPS
        full_shapes.append(jax.ShapeDtypeStruct(tuple(shape), sh.dtype))
    fwd0 = 3 * N_BIG + 3

    def body(*refs):
        srcs, conv_src = refs[:N_BIG], refs[N_BIG]
        dsts, conv_dst = refs[N_BIG + 1:2 * N_BIG + 1], refs[2 * N_BIG + 1]
        send_sems, recv_sems, local_sems = refs[2 * N_BIG + 2:]
        x, y, c = _mesh_pos()
        me = 2 * x + y
        sibling = (x, y, 1 - c)
        chips = [(1 - x, y), (x, 1 - y), (1 - x, 1 - y)]

        def remote(src, dst, k, to):
            return pltpu.make_async_remote_copy(src_ref=src, dst_ref=dst, send_sem=send_sems.at[k],
                                                recv_sem=recv_sems.at[k], device_id=to, device_id_type=MESH)

        local = []
        for w in range(N_BIG):
            n = widths[w]
            start = pl.multiple_of(me * n, 128)
            dst = dsts[w].at[:, :, pl.ds(start, n)] if SHARD_AXES[w] == 2 else dsts[w].at[:, pl.ds(start, n), :]
            local.append(pltpu.make_async_copy(srcs[w], dst, local_sems.at[w]))
        local.append(pltpu.make_async_copy(conv_src, conv_dst.at[me], local_sems.at[N_BIG]))
        for cp in local:
            cp.start()
        sends = []
        for j, (qx, qy) in enumerate(chips):
            for w in range(N_BIG):
                sends.append(remote(srcs[w].at[c], _shard_of(dsts[w], w, c, me, widths[w]), j * N_BIG + w, (qx, qy, c)))
            sends.append(remote(conv_src, conv_dst.at[me], 3 * N_BIG + j, (qx, qy, c)))
        for cp in sends:
            cp.start()
        passed = []
        for j, (qx, qy) in enumerate(chips):
            q = 2 * qx + qy
            for w in range(N_BIG):
                landed = _shard_of(dsts[w], w, c, q, widths[w])
                remote(landed, landed, j * N_BIG + w, sibling).wait_recv()
                cp = remote(landed, landed, fwd0 + j * N_BIG + w, sibling)
                cp.start()
                passed.append(cp)
            remote(conv_src, conv_dst.at[q], 3 * N_BIG + j, sibling).wait_recv()
        for j, (qx, qy) in enumerate(chips):
            q = 2 * qx + qy
            for w in range(N_BIG):
                landed = _shard_of(dsts[w], w, 1 - c, q, widths[w])
                remote(landed, landed, fwd0 + j * N_BIG + w, sibling).wait_recv()
        for cp in sends + passed:
            cp.wait_send()
        for cp in local:
            cp.wait()

    n_sem = fwd0 + 3 * N_BIG
    return pl.pallas_call(
        body, name="gather_weights", in_specs=[ANY] * (N_BIG + 1), out_specs=[ANY] * (N_BIG + 1),
        out_shape=full_shapes + [jax.ShapeDtypeStruct((N_CHIPS,) + conv_tile.shape, conv_tile.dtype)],
        scratch_shapes=[pltpu.SemaphoreType.DMA((n_sem,)), pltpu.SemaphoreType.DMA((n_sem,)),
                        pltpu.SemaphoreType.DMA((N_BIG + 1,))],
    )(*shards, conv_tile)


def _scatter_grads(grads):
    widths = [g.shape[ax] // N_CHIPS for g, ax in zip(grads, SHARD_AXES)]
    slot_shapes = []
    for g, ax, n in zip(grads, SHARD_AXES, widths):
        shape = list(g.shape[1:])
        shape[ax - 1] = n
        slot_shapes.append(jax.ShapeDtypeStruct((N_DEV,) + tuple(shape), g.dtype))

    def body(*refs):
        srcs, dsts = refs[:N_BIG], refs[N_BIG:2 * N_BIG]
        send_sems, recv_sems, local_sems = refs[2 * N_BIG:]
        x, y, c = _mesh_pos()
        local = [pltpu.make_async_copy(_shard_of(srcs[w], w, c, 2 * x + y, widths[w]), dsts[w].at[0], local_sems.at[w])
                 for w in range(N_BIG)]
        for cp in local:
            cp.start()
        sends = []
        for r in range(1, N_DEV):
            tx, ty, tc = _flip(x, r & 4), _flip(y, r & 2), _flip(c, r & 1)
            for w in range(N_BIG):
                k = (r - 1) * N_BIG + w
                sends.append(pltpu.make_async_remote_copy(
                    src_ref=_shard_of(srcs[w], w, tc, 2 * tx + ty, widths[w]), dst_ref=dsts[w].at[r],
                    send_sem=send_sems.at[k], recv_sem=recv_sems.at[k], device_id=(tx, ty, tc), device_id_type=MESH))
        for cp in sends:
            cp.start()
        for cp in sends:
            cp.wait_recv()
        for cp in sends:
            cp.wait_send()
        for cp in local:
            cp.wait()

    n_sem = (N_DEV - 1) * N_BIG
    return pl.pallas_call(
        body, name="scatter_grads", in_specs=[ANY] * N_BIG, out_specs=[ANY] * N_BIG, out_shape=slot_shapes,
        scratch_shapes=[pltpu.SemaphoreType.DMA((n_sem,)), pltpu.SemaphoreType.DMA((n_sem,)),
                        pltpu.SemaphoreType.DMA((N_BIG,))],
    )(*grads)


def _sum_slots(slots, name):
    _, rows, cols = slots.shape
    tr = min(256, rows)

    def body(s_ref, o_ref):
        acc = s_ref[0].astype(F32)
        for r in range(1, N_DEV):
            acc = acc + s_ref[r].astype(F32)
        o_ref[...] = acc

    return pl.pallas_call(
        body, grid=(rows // tr,), name=name,
        in_specs=[pl.BlockSpec((N_DEV, tr, cols), lambda i: (0, i, 0))], out_specs=_rows(tr, cols),
        out_shape=jax.ShapeDtypeStruct((rows, cols), F32), compiler_params=_cparams("parallel"),
    )(slots)


def _swap_layers(halves):
    def body(*refs):
        srcs, dsts = refs[:N_BIG], refs[N_BIG:2 * N_BIG]
        send_sems, recv_sems, local_sems = refs[2 * N_BIG:]
        x, y, c = _mesh_pos()
        local = [pltpu.make_async_copy(srcs[w], dsts[w].at[c], local_sems.at[w]) for w in range(N_BIG)]
        sends = [pltpu.make_async_remote_copy(src_ref=srcs[w], dst_ref=dsts[w].at[c], send_sem=send_sems.at[w],
                                              recv_sem=recv_sems.at[w], device_id=(x, y, 1 - c), device_id_type=MESH)
                 for w in range(N_BIG)]
        for cp in local + sends:
            cp.start()
        for w in range(N_BIG):
            pltpu.make_async_remote_copy(src_ref=srcs[w], dst_ref=dsts[w].at[1 - c], send_sem=send_sems.at[w],
                                         recv_sem=recv_sems.at[w], device_id=(x, y, 1 - c),
                                         device_id_type=MESH).wait_recv()
        for cp in sends:
            cp.wait_send()
        for cp in local:
            cp.wait()

    return pl.pallas_call(
        body, name="swap_layers", in_specs=[ANY] * N_BIG, out_specs=[ANY] * N_BIG,
        out_shape=[jax.ShapeDtypeStruct((2,) + h.shape, h.dtype) for h in halves],
        scratch_shapes=[pltpu.SemaphoreType.DMA((N_BIG,)), pltpu.SemaphoreType.DMA((N_BIG,)),
                        pltpu.SemaphoreType.DMA((N_BIG,))],
    )(*halves)


def _adamw_math(w, g, m, v):
    m = ADAM_B1 * m + (1.0 - ADAM_B1) * g
    v = ADAM_B2 * v + (1.0 - ADAM_B2) * jnp.square(g)
    m_hat = m / (1.0 - ADAM_B1 ** ADAM_STEP)
    v_hat = v / (1.0 - ADAM_B2 ** ADAM_STEP)
    delta = -ADAM_LR * (m_hat / (jnp.sqrt(v_hat) + ADAM_EPS) + ADAM_WD * w)
    return delta, m, v


def _adamw(w, g, m, v, name):
    shape = w.shape
    rows, cols = shape[0] * shape[1], shape[2]
    tr = min(256, rows)

    def body(w_ref, g_ref, m_ref, v_ref, d_ref, m2_ref, v2_ref):
        d_ref[...], m2_ref[...], v2_ref[...] = _adamw_math(w_ref[...], g_ref[...], m_ref[...], v_ref[...])

    res = pl.pallas_call(
        body, grid=(rows // tr,), name=name, in_specs=[_rows(tr, cols)] * 4, out_specs=[_rows(tr, cols)] * 3,
        out_shape=[jax.ShapeDtypeStruct((rows, cols), F32)] * 3, compiler_params=_cparams("parallel"),
    )(*[a.reshape(rows, cols) for a in (w, g, m, v)])
    return [a.reshape(shape) for a in res]


def _small_sync(part, w, m, v):
    rows, cols = part.shape

    def body(p_ref, w_ref, m_ref, v_ref, g_ref, d_ref, m2_ref, v2_ref, slots, send_sems, recv_sems):
        x, y, c = _mesh_pos()
        me = 4 * x + 2 * y + c
        slots[me] = p_ref[...]
        sends = []
        for r in range(1, N_DEV):
            to = (_flip(x, r & 4), _flip(y, r & 2), _flip(c, r & 1))
            sends.append(pltpu.make_async_remote_copy(
                src_ref=p_ref, dst_ref=slots.at[me], send_sem=send_sems.at[r - 1], recv_sem=recv_sems.at[r - 1],
                device_id=to, device_id_type=MESH))
        for cp in sends:
            cp.start()
        for cp in sends:
            cp.wait_recv()
        for cp in sends:
            cp.wait_send()
        g = slots[0]
        for i in range(1, N_DEV):
            g = g + slots[i]
        g_ref[...] = g
        d_ref[...], m2_ref[...], v2_ref[...] = _adamw_math(w_ref[...], g, m_ref[...], v_ref[...])

    vm = pl.BlockSpec(memory_space=pltpu.VMEM)
    return pl.pallas_call(
        body, name="small_sync", in_specs=[vm] * 4, out_specs=[vm] * 4,
        out_shape=[jax.ShapeDtypeStruct((rows, cols), F32)] * 4,
        scratch_shapes=[pltpu.VMEM((N_DEV, rows, cols), F32), pltpu.SemaphoreType.DMA((N_DEV - 1,)),
                        pltpu.SemaphoreType.DMA((N_DEV - 1,))],
    )(part, w, m, v)


def _pack_small(d, g_mix, g_group, g_mlp, g_final, conv_full, sinks, scalar):
    def part(rows):
        return jnp.pad(rows, ((0, HALO - rows.shape[0]), (0, d - rows.shape[1])))
    return jnp.concatenate([part(g_mix), part(g_group), part(g_mlp), part(g_final[None]),
                            part(conv_full.reshape(6, CONV_CH)), part(sinks.reshape(2, N_HEADS)),
                            part(scalar.reshape(1, 1))], axis=0)


def _unpack_small(p, dm):
    return (p[0:2, :dm], p[8:10, :MIX_WIDTH], p[16:18, :dm], p[24, :dm], p[32:38, :CONV_CH].reshape(2, 3, CONV_CH),
            p[40:42, :N_HEADS].reshape(2, 2, C_GROUP), p[48, 0])


def kernel(x, w_in, conv_w, sinks, g_mix, g_group, w_o, g_mlp, w_ff_in, w_ff_out, g_final, loss_target, m_w_in, m_conv_w, m_sinks, m_g_mix, m_g_group, m_w_o, m_g_mlp, m_w_ff_in, m_w_ff_out, m_g_final, v_w_in, v_conv_w, v_sinks, v_g_mix, v_g_group, v_w_o, v_g_mlp, v_w_ff_in, v_w_ff_out, v_g_final):
    d = max(x.shape[2], MIX_WIDTH)
    chip = 2 * lax.axis_index("x") + lax.axis_index("y")
    conv_n = conv_w.shape[2]

    shards = [w.astype(BF) for w in (w_in, w_o, w_ff_in, w_ff_out)]
    conv_tile = jnp.pad(conv_w.reshape(6, conv_n), ((0, HALO - 6), (0, 128 - conv_n)))
    fw_in, fw_o, fw1, fw2, conv_all = _gather_weights(shards, conv_tile)
    conv_full = conv_all[:, :6, :conv_n].reshape(N_CHIPS, 2, 3, conv_n).transpose(1, 2, 0, 3).reshape(2, 3, CONV_CH)

    loss_tile, dx, grads, dg_final = _local_step(x[0], loss_target[0], fw_in, fw_o, fw1, fw2, conv_full, sinks,
                                                 g_mix, g_group, g_mlp, g_final)

    full_grads = [jnp.stack([grads[0][w], grads[1][w]]) for w in range(N_BIG)]
    slots = _scatter_grads(full_grads)
    halves = [_sum_slots(s, "sum_slots_%d" % w) for w, s in enumerate(slots)]
    g_w_in, g_w_o, g_w1, g_w2 = _swap_layers(halves)

    def both(i):
        return jnp.stack([grads[0][i][0], grads[1][i][0]])
    dconv = jnp.stack([grads[0][4][:3], grads[1][4][:3]])
    dsinks = jnp.stack([grads[0][5][0, :N_HEADS], grads[1][5][0, :N_HEADS]])
    part = _pack_small(d, both(6), both(7), both(8), dg_final[0], dconv, dsinks, loss_tile[0, 0])

    def spread(shard):
        return lax.dynamic_update_slice(jnp.zeros((2, 3, CONV_CH), F32), shard, (0, 0, chip * conv_n))
    zero = jnp.zeros((), F32)
    packs = [_pack_small(d, a, b, c_, e, spread(f), g_, zero) for a, b, c_, e, f, g_ in (
        (g_mix, g_group, g_mlp, g_final, conv_w, sinks),
        (m_g_mix, m_g_group, m_g_mlp, m_g_final, m_conv_w, m_sinks),
        (v_g_mix, v_g_group, v_g_mlp, v_g_final, v_conv_w, v_sinks))]
    small = [_unpack_small(p, x.shape[2]) for p in _small_sync(part, *packs)]

    def shard_of(full):
        return lax.dynamic_slice(full, (0, 0, chip * conv_n), (2, 3, conv_n))
    small = [(s[0], s[1], s[2], s[3], shard_of(s[4]), s[5], s[6]) for s in small]
    loss = small[0][6]

    big = [_adamw(w, g, m, v, "adamw_%d" % i) for i, (w, g, m, v) in enumerate((
        (w_in, g_w_in, m_w_in, v_w_in), (w_o, g_w_o, m_w_o, v_w_o),
        (w_ff_in, g_w1, m_w_ff_in, v_w_ff_in), (w_ff_out, g_w2, m_w_ff_out, v_w_ff_out)))]

    def ordered(kind):
        if kind == 0:
            b = [g_w_in, g_w_o, g_w1, g_w2]
        else:
            b = [big[i][kind - 1] for i in range(N_BIG)]
        s = small[kind]
        return [b[0], s[4], s[5], s[0], s[1], b[1], s[2], b[2], b[3], s[3]]

    return (loss, dx[None], *ordered(0), *ordered(1), *ordered(2), *ordered(3))
```

```python
import functools

import jax
import jax.numpy as jnp
from jax import lax
from jax.experimental import pallas as pl
from jax.experimental.pallas import tpu as pltpu

HEAD_DIM = 64
N_HEADS = 6
C_GROUP = 3
A_WIDTH = N_HEADS * HEAD_DIM
C_KV_WIDTH = 2 * HEAD_DIM
CONV_CH = 256
ZA_W = 3 * A_WIDTH
ZB_W = 3 * CONV_CH
ZC_W = A_WIDTH + 2 * C_KV_WIDTH
IN_WIDTH = ZA_W + ZB_W + ZC_W
MIX_WIDTH = A_WIDTH + CONV_CH + A_WIDTH
DILATIONS = (1, 4, 16)
A_MAX_DIST = 128
C_MAX_DIST = 127
TQ = 128
EPS = 1e-6
SCALE = HEAD_DIM ** -0.5
NEG = -1e30
HALO = 8

ADAM_LR = 0.001
ADAM_B1 = 0.9
ADAM_B2 = 0.999
ADAM_EPS = 1e-08
ADAM_WD = 0.01
ADAM_STEP = 10

BF = jnp.bfloat16
F32 = jnp.float32
MESH = pl.DeviceIdType.MESH
VMEM_LIMIT = 56 * 1024 * 1024


def _cparams(*sem):
    return pltpu.CompilerParams(dimension_semantics=sem, vmem_limit_bytes=VMEM_LIMIT)


def _nt(a, b):
    return lax.dot_general(a, b, (((1,), (1,)), ((), ())), preferred_element_type=F32)


def _tn(a, b):
    return lax.dot_general(a, b, (((0,), (0,)), ((), ())), preferred_element_type=F32)


def _nn(a, b):
    return jnp.dot(a, b, preferred_element_type=F32)


def _rows(tb, w):
    return pl.BlockSpec((tb, w), lambda i: (i, 0))


def _whole(shape):
    return pl.BlockSpec(shape, lambda *_: (0,) * len(shape))


def _layer(shape, l):
    return pl.BlockSpec((None,) + shape, lambda *_: (l,) + (0,) * len(shape))


def _rms_scale(v):
    return lax.rsqrt(jnp.mean(v * v, axis=-1, keepdims=True) + EPS)


def _norm_bwd(dxhat, xhat, r):
    return r * (dxhat - xhat * jnp.mean(dxhat * xhat, axis=-1, keepdims=True))


def _qkv_fwd(x, g, w_all, l, tb):
    s, d = x.shape

    def body(x_ref, g_ref, w_ref, h_ref, za_ref, zb_ref, zc_ref):
        xv = x_ref[...]
        h = ((xv * _rms_scale(xv)) * g_ref[...]).astype(BF)
        h_ref[...] = h
        z = _nn(h, w_ref[...])
        za_ref[...] = z[:, :ZA_W].astype(BF)
        zb_ref[...] = z[:, ZA_W:ZA_W + ZB_W]
        zc_ref[...] = z[:, ZA_W + ZB_W:].astype(BF)

    return pl.pallas_call(
        body, grid=(s // tb,), name="qkv_fwd",
        in_specs=[_rows(tb, d), _whole((1, d)), _layer((d, IN_WIDTH), l)],
        out_specs=[_rows(tb, d), _rows(tb, ZA_W), _rows(tb, ZB_W), _rows(tb, ZC_W)],
        out_shape=[jax.ShapeDtypeStruct((s, d), BF), jax.ShapeDtypeStruct((s, ZA_W), BF),
                   jax.ShapeDtypeStruct((s, ZB_W), F32), jax.ShapeDtypeStruct((s, ZC_W), BF)],
        compiler_params=_cparams("parallel"),
    )(x, g, w_all)


def _band_mask(b, max_dist):
    qi = lax.broadcasted_iota(jnp.int32, (TQ, 2 * TQ), 0)
    kj = lax.broadcasted_iota(jnp.int32, (TQ, 2 * TQ), 1)
    dist = TQ + qi - kj
    return (dist >= 0) & (dist <= max_dist) & ((kj >= TQ) | (b > 0))


def _attn_specs(dil, zw, kw, kcol, vcol):
    nq, nk = zw // A_WIDTH, zw // kw
    q = pl.BlockSpec((TQ, A_WIDTH), lambda r, b: (b, r * nq))
    kp = pl.BlockSpec((TQ, kw), lambda r, b: (jnp.maximum(b - 1, 0), r * nk + kcol))
    kc = pl.BlockSpec((TQ, kw), lambda r, b: (b, r * nk + kcol))
    vp = pl.BlockSpec((TQ, kw), lambda r, b: (jnp.maximum(b - 1, 0), r * nk + vcol))
    vc = pl.BlockSpec((TQ, kw), lambda r, b: (b, r * nk + vcol))
    return [q, kp, kc, vp, vc]


def _head_spec(w=A_WIDTH):
    return pl.BlockSpec((TQ, w), lambda r, b: (b, r))


def _hs(h):
    return slice(h * HEAD_DIM, (h + 1) * HEAD_DIM)


def _attn_fwd(z, dil, kw, kcol, vcol, n_rep, max_dist, state, sink, last, name):
    s, zw = z.shape
    sub = s // dil
    zv = z.reshape(sub, dil * zw)
    have_state, have_sink = state is not None, sink is not None

    def body(*refs):
        q_ref, kp_ref, kc_ref, vp_ref, vc_ref = refs[:5]
        pos = 5
        if have_state:
            acc_in, m_in, l_in = refs[pos:pos + 3]
            pos += 3
        if have_sink:
            sink_ref = refs[pos]
            pos += 1
        outs = refs[pos:]
        mask = _band_mask(pl.program_id(1), max_dist)
        for h in range(N_HEADS):
            kh = h // n_rep
            q = q_ref[:, _hs(h)]
            k2 = jnp.concatenate([kp_ref[:, _hs(kh)], kc_ref[:, _hs(kh)]], axis=0)
            v2 = jnp.concatenate([vp_ref[:, _hs(kh)], vc_ref[:, _hs(kh)]], axis=0)
            sc = jnp.where(mask, _nt(q, k2) * SCALE, NEG)
            m_new = jnp.max(sc, axis=1, keepdims=True)
            if have_sink:
                sk = sink_ref[0:1, h:h + 1]
                m_new = jnp.maximum(m_new, sk)
            if have_state:
                m_old = m_in[:, h * HEAD_DIM:h * HEAD_DIM + 1]
                m_new = jnp.maximum(m_new, m_old)
            p = jnp.exp(sc - m_new)
            l_new = jnp.sum(p, axis=1, keepdims=True)
            acc = _nn(p.astype(BF), v2)
            if have_state:
                alpha = jnp.exp(m_old - m_new)
                l_new = l_new + alpha * l_in[:, h * HEAD_DIM:h * HEAD_DIM + 1]
                acc = acc + alpha * acc_in[:, _hs(h)]
            if have_sink:
                l_new = l_new + jnp.exp(sk - m_new)
            if last:
                outs[0][:, _hs(h)] = acc / l_new
                outs[1][:, _hs(h)] = jnp.broadcast_to(m_new + jnp.log(l_new), (TQ, HEAD_DIM))
            else:
                outs[0][:, _hs(h)] = acc
                outs[1][:, _hs(h)] = jnp.broadcast_to(m_new, (TQ, HEAD_DIM))
                outs[2][:, _hs(h)] = jnp.broadcast_to(l_new, (TQ, HEAD_DIM))

    args = [zv] * 5
    in_specs = _attn_specs(dil, zw, kw, kcol, vcol)
    if have_state:
        args += [a.reshape(sub, dil * A_WIDTH) for a in state]
        in_specs += [_head_spec()] * 3
    if have_sink:
        args.append(sink)
        in_specs.append(_whole((HALO, 128)))
    n_out = 2 if last else 3
    res = pl.pallas_call(
        body, grid=(dil, sub // TQ), name=name, in_specs=in_specs,
        out_specs=[_head_spec()] * n_out,
        out_shape=[jax.ShapeDtypeStruct((sub, dil * A_WIDTH), F32)] * n_out,
        compiler_params=_cparams("parallel", "parallel"),
    )(*args)
    return [a.reshape(s, A_WIDTH) for a in res]


def _shift_down(v, n, halo):
    rows = v.shape[0]
    out = pltpu.roll(v, n, 0)
    row = lax.broadcasted_iota(jnp.int32, v.shape, 0)
    for t in range(n):
        out = jnp.where(row == t, halo[HALO - n + t:HALO - n + t + 1, :], out)
    return out


def _shift_up(v, n, halo):
    rows = v.shape[0]
    out = pltpu.roll(v, rows - n, 0)
    row = lax.broadcasted_iota(jnp.int32, v.shape, 0)
    for t in range(n):
        out = jnp.where(row == rows - n + t, halo[t:t + 1, :], out)
    return out


def _conv_parts(zb, zb_prev, cw):
    gb, gc, xb = zb[:, :CONV_CH], zb[:, CONV_CH:2 * CONV_CH], zb[:, 2 * CONV_CH:]
    u = gc * xb
    uh = zb_prev[:, CONV_CH:2 * CONV_CH] * zb_prev[:, 2 * CONV_CH:]
    u1 = _shift_down(u, 1, uh)
    u2 = _shift_down(u, 2, uh)
    c = cw[0:1, :] * u2 + cw[1:2, :] * u1 + cw[2:3, :] * u
    return gb, gc, xb, u, u1, u2, c


def _prev_halo(tb, w):
    return pl.BlockSpec((HALO, w), lambda i: (jnp.maximum(i * (tb // HALO) - 1, 0), 0))


def _next_halo(tb, w, nblk):
    return pl.BlockSpec((HALO, w), lambda i: (jnp.minimum((i + 1) * (tb // HALO), nblk * (tb // HALO) - 1), 0))


def _mix_fwd(x, ya, yc, zb, cw, gg, wo_all, l, tb):
    s, d = x.shape

    def body(x_ref, ya_ref, yc_ref, zb_ref, zbp_ref, cw_ref, gg_ref, wo_ref, x1_ref, yb_ref):
        i = pl.program_id(0)
        zbp = jnp.where(i > 0, zbp_ref[...], 0.0)
        gb, _, _, _, _, _, c = _conv_parts(zb_ref[...], zbp, cw_ref[...])
        yb = gb * c
        yb_ref[...] = yb
        ya, yc = ya_ref[...], yc_ref[...]
        n = jnp.concatenate([ya * _rms_scale(ya), yb * _rms_scale(yb), yc * _rms_scale(yc)], axis=1)
        n = (n * gg_ref[...]).astype(BF)
        x1_ref[...] = x_ref[...] + _nn(n, wo_ref[...])

    return pl.pallas_call(
        body, grid=(s // tb,), name="mix_fwd",
        in_specs=[_rows(tb, d), _rows(tb, A_WIDTH), _rows(tb, A_WIDTH), _rows(tb, ZB_W), _prev_halo(tb, ZB_W),
                  _whole((HALO, CONV_CH)), _whole((1, MIX_WIDTH)), _layer((MIX_WIDTH, d), l)],
        out_specs=[_rows(tb, d), _rows(tb, CONV_CH)],
        out_shape=[jax.ShapeDtypeStruct((s, d), F32), jax.ShapeDtypeStruct((s, CONV_CH), F32)],
        compiler_params=_cparams("parallel"),
    )(x, ya, yc, zb, zb, cw, gg, wo_all)


def _mlp_fwd(x1, g, w1_all, w2_all, l, tb, tf):
    s, d = x1.shape
    ff = w1_all.shape[2]
    nj = ff // tf

    def body(x_ref, g_ref, w1_ref, w2_ref, x2_ref, h2_ref, ap_ref, acc):
        j = pl.program_id(1)

        @pl.when(j == 0)
        def _():
            xv = x_ref[...]
            h2_ref[...] = ((xv * _rms_scale(xv)) * g_ref[...]).astype(BF)
            acc[...] = jnp.zeros_like(acc)

        ap = _nn(h2_ref[...], w1_ref[...])
        ap_ref[...] = ap.astype(BF)
        a = jnp.square(jnp.maximum(ap, 0.0)).astype(BF)
        acc[...] += _nn(a, w2_ref[...])

        @pl.when(j == nj - 1)
        def _():
            x2_ref[...] = x_ref[...] + acc[...]

    return pl.pallas_call(
        body, grid=(s // tb, nj), name="mlp_fwd",
        in_specs=[pl.BlockSpec((tb, d), lambda i, j: (i, 0)), _whole((1, d)),
                  pl.BlockSpec((None, d, tf), lambda i, j: (l, 0, j)),
                  pl.BlockSpec((None, tf, d), lambda i, j: (l, j, 0))],
        out_specs=[pl.BlockSpec((tb, d), lambda i, j: (i, 0)), pl.BlockSpec((tb, d), lambda i, j: (i, 0)),
                   pl.BlockSpec((tb, tf), lambda i, j: (i, j))],
        out_shape=[jax.ShapeDtypeStruct((s, d), F32), jax.ShapeDtypeStruct((s, d), BF),
                   jax.ShapeDtypeStruct((s, ff), BF)],
        scratch_shapes=[pltpu.VMEM((tb, d), F32)],
        compiler_params=_cparams("parallel", "arbitrary"),
    )(x1, g, w1_all, w2_all)


def _loss_head(x, g, tgt, tb):
    s, d = x.shape

    def body(x_ref, g_ref, t_ref, dx_ref, loss_ref, dg_ref):
        i = pl.program_id(0)

        @pl.when(i == 0)
        def _():
            loss_ref[...] = jnp.zeros_like(loss_ref)
            dg_ref[...] = jnp.zeros_like(dg_ref)

        xv = x_ref[...]
        r = _rms_scale(xv)
        xhat = xv * r
        err = xhat * g_ref[...] - t_ref[...]
        part = jnp.sum(jnp.mean(jnp.square(err), axis=-1, keepdims=True), axis=0, keepdims=True)
        loss_ref[...] += 0.5 * part
        dy = err * (1.0 / d)
        dg_ref[...] += jnp.sum(dy * xhat, axis=0, keepdims=True)
        dx_ref[...] = _norm_bwd(dy * g_ref[...], xhat, r)

    return pl.pallas_call(
        body, grid=(s // tb,), name="loss_head",
        in_specs=[_rows(tb, d), _whole((1, d)), _rows(tb, d)],
        out_specs=[_rows(tb, d), _whole((HALO, 128)), _whole((HALO, d))],
        out_shape=[jax.ShapeDtypeStruct((s, d), F32), jax.ShapeDtypeStruct((HALO, 128), F32),
                   jax.ShapeDtypeStruct((HALO, d), F32)],
        compiler_params=_cparams("arbitrary"),
    )(x, g, tgt)


def _mlp_bwd(dx2, x1, ap, g, w1_all, w2_all, l, tb, tf):
    s, d = x1.shape
    ff = ap.shape[1]
    nj = ff // tf

    def body(dx2_ref, x1_ref, ap_ref, g_ref, w1_ref, w2_ref, dx1_ref, dap_ref, dg_ref, acc):
        i, j = pl.program_id(0), pl.program_id(1)

        @pl.when((i == 0) & (j == 0))
        def _():
            dg_ref[...] = jnp.zeros_like(dg_ref)

        @pl.when(j == 0)
        def _():
            acc[...] = jnp.zeros_like(acc)

        da = _nt(dx2_ref[...].astype(BF), w2_ref[...])
        dap = (da * (2.0 * jnp.maximum(ap_ref[...].astype(F32), 0.0))).astype(BF)
        dap_ref[...] = dap
        acc[...] += _nt(dap, w1_ref[...])

        @pl.when(j == nj - 1)
        def _():
            xv = x1_ref[...]
            r = _rms_scale(xv)
            xhat = xv * r
            dh = acc[...]
            dg_ref[...] += jnp.sum(dh * xhat, axis=0, keepdims=True)
            dx1_ref[...] = dx2_ref[...] + _norm_bwd(dh * g_ref[...], xhat, r)

    return pl.pallas_call(
        body, grid=(s // tb, nj), name="mlp_bwd",
        in_specs=[pl.BlockSpec((tb, d), lambda i, j: (i, 0)), pl.BlockSpec((tb, d), lambda i, j: (i, 0)),
                  pl.BlockSpec((tb, tf), lambda i, j: (i, j)),
                  _whole((1, d)), pl.BlockSpec((None, d, tf), lambda i, j: (l, 0, j)),
                  pl.BlockSpec((None, tf, d), lambda i, j: (l, j, 0))],
        out_specs=[pl.BlockSpec((tb, d), lambda i, j: (i, 0)), pl.BlockSpec((tb, tf), lambda i, j: (i, j)),
                   _whole((HALO, d))],
        out_shape=[jax.ShapeDtypeStruct((s, d), F32), jax.ShapeDtypeStruct((s, ff), BF),
                   jax.ShapeDtypeStruct((HALO, d), F32)],
        scratch_shapes=[pltpu.VMEM((tb, d), F32)],
        compiler_params=_cparams("arbitrary", "arbitrary"),
    )(dx2, x1, ap, g, w1_all, w2_all)


def _wgrad(a, b, l, into, tm, tn, ts, name, relu2=False):
    s, m = a.shape
    n = b.shape[1]
    ns = s // ts

    def body(a_ref, b_ref, *rest):
        o_ref, acc = rest[-2:]
        k = pl.program_id(2)

        @pl.when(k == 0)
        def _():
            acc[...] = jnp.zeros_like(acc)

        av = a_ref[...]
        if relu2:
            av = jnp.square(jnp.maximum(av.astype(F32), 0.0)).astype(BF)
        acc[...] += _tn(av, b_ref[...].astype(BF))

        @pl.when(k == ns - 1)
        def _():
            o_ref[...] = acc[...].astype(BF)

    args = [a, b] + ([] if into is None else [into])
    return pl.pallas_call(
        body, grid=(m // tm, n // tn, ns), name=name,
        in_specs=[pl.BlockSpec((ts, tm), lambda i, j, k: (k, i)), pl.BlockSpec((ts, tn), lambda i, j, k: (k, j))]
        + ([] if into is None else [ANY]),
        out_specs=pl.BlockSpec((None, tm, tn), lambda i, j, k: (l, i, j)),
        out_shape=jax.ShapeDtypeStruct((2, m, n), BF),
        input_output_aliases={} if into is None else {2: 0},
        scratch_shapes=[pltpu.VMEM((tm, tn), F32)],
        compiler_params=_cparams("parallel", "parallel", "arbitrary"),
    )(*args)


def _mix_bwd(dx1, ya, yb, yc, gg, wo_all, l, tb):
    s, d = dx1.shape

    def body(dx_ref, ya_ref, yb_ref, yc_ref, gg_ref, wo_ref, n_ref, dya_ref, dyc_ref, da_ref, dc_ref, dyb_ref, dg_ref):
        i = pl.program_id(0)

        @pl.when(i == 0)
        def _():
            dg_ref[...] = jnp.zeros_like(dg_ref)

        dn = _nt(dx_ref[...].astype(BF), wo_ref[...])
        ys = [ya_ref[...], yb_ref[...], yc_ref[...]]
        rs = [_rms_scale(v) for v in ys]
        nhat = jnp.concatenate([v * r for v, r in zip(ys, rs)], axis=1)
        gg = gg_ref[...]
        n_ref[...] = (nhat * gg).astype(BF)
        dg_ref[...] += jnp.sum(dn * nhat, axis=0, keepdims=True)
        dnh = dn * gg
        bounds = [(0, A_WIDTH), (A_WIDTH, A_WIDTH + CONV_CH), (A_WIDTH + CONV_CH, MIX_WIDTH)]
        dys = [_norm_bwd(dnh[:, lo:hi], nhat[:, lo:hi], r) for (lo, hi), r in zip(bounds, rs)]
        dyb_ref[...] = dys[1]
        for dy, y, dy_ref, dd_ref in ((dys[0], ys[0], dya_ref, da_ref), (dys[2], ys[2], dyc_ref, dc_ref)):
            dy_ref[...] = dy.astype(BF)
            t = dy * y
            for h in range(N_HEADS):
                dd_ref[:, _hs(h)] = jnp.broadcast_to(jnp.sum(t[:, _hs(h)], axis=1, keepdims=True), (tb, HEAD_DIM))

    return pl.pallas_call(
        body, grid=(s // tb,), name="mix_bwd",
        in_specs=[_rows(tb, d), _rows(tb, A_WIDTH), _rows(tb, CONV_CH), _rows(tb, A_WIDTH), _whole((1, MIX_WIDTH)),
                  _layer((MIX_WIDTH, d), l)],
        out_specs=[_rows(tb, MIX_WIDTH), _rows(tb, A_WIDTH), _rows(tb, A_WIDTH), _rows(tb, A_WIDTH),
                   _rows(tb, A_WIDTH), _rows(tb, CONV_CH), _whole((HALO, MIX_WIDTH))],
        out_shape=[jax.ShapeDtypeStruct((s, MIX_WIDTH), BF), jax.ShapeDtypeStruct((s, A_WIDTH), BF),
                   jax.ShapeDtypeStruct((s, A_WIDTH), BF), jax.ShapeDtypeStruct((s, A_WIDTH), F32),
                   jax.ShapeDtypeStruct((s, A_WIDTH), F32), jax.ShapeDtypeStruct((s, CONV_CH), F32),
                   jax.ShapeDtypeStruct((HALO, MIX_WIDTH), F32)],
        compiler_params=_cparams("arbitrary"),
    )(dx1, ya, yb, yc, gg, wo_all)


def _attn_bwd(z, dy, lse, dd, dil, kw, kcol, vcol, n_rep, max_dist, sink, name):
    s, zw = z.shape
    sub = s // dil
    zv = z.reshape(sub, dil * zw)
    have_sink = sink is not None
    n_kv = N_HEADS // n_rep

    def body(*refs):
        q_ref, kp_ref, kc_ref, vp_ref, vc_ref, dy_ref, lse_ref, dd_ref = refs[:8]
        pos = 8
        if have_sink:
            sink_ref = refs[pos]
            pos += 1
        dq_ref, dkp_ref, dkc_ref, dvp_ref, dvc_ref = refs[pos:pos + 5]
        b = pl.program_id(1)
        mask = _band_mask(b, max_dist)
        if have_sink:
            dsink_ref = refs[pos + 5]

            @pl.when(b == 0)
            def _():
                dsink_ref[...] = jnp.zeros_like(dsink_ref)

            row = lax.broadcasted_iota(jnp.int32, (HALO, 128), 0)
            lane = lax.broadcasted_iota(jnp.int32, (HALO, 128), 1)
        for kh in range(n_kv):
            k2 = jnp.concatenate([kp_ref[:, _hs(kh)], kc_ref[:, _hs(kh)]], axis=0)
            v2 = jnp.concatenate([vp_ref[:, _hs(kh)], vc_ref[:, _hs(kh)]], axis=0)
            dk2 = jnp.zeros((2 * TQ, HEAD_DIM), F32)
            dv2 = jnp.zeros((2 * TQ, HEAD_DIM), F32)
            for h in range(kh * n_rep, (kh + 1) * n_rep):
                q = q_ref[:, _hs(h)]
                lse_h = lse_ref[:, h * HEAD_DIM:h * HEAD_DIM + 1]
                dd_h = dd_ref[:, h * HEAD_DIM:h * HEAD_DIM + 1]
                dyh = dy_ref[:, _hs(h)]
                sc = jnp.where(mask, _nt(q, k2) * SCALE, NEG)
                p = jnp.exp(sc - lse_h)
                dp = _nt(dyh, v2)
                ds = ((p * (dp - dd_h)) * SCALE).astype(BF)
                dq_ref[:, _hs(h)] = _nn(ds, k2)
                dk2 = dk2 + _tn(ds, q)
                dv2 = dv2 + _tn(p.astype(BF), dyh)
                if have_sink:
                    sk = sink_ref[0:1, h:h + 1]
                    val = -jnp.sum(jnp.exp(sk - lse_h) * dd_h, axis=0, keepdims=True)
                    dsink_ref[...] += jnp.where((row == 0) & (lane == h), val, 0.0)
            dkp_ref[:, _hs(kh)] = dk2[:TQ]
            dkc_ref[:, _hs(kh)] = dk2[TQ:]
            dvp_ref[:, _hs(kh)] = dv2[:TQ]
            dvc_ref[:, _hs(kh)] = dv2[TQ:]

    args = [zv] * 5 + [a.reshape(sub, dil * A_WIDTH) for a in (dy, lse, dd)]
    in_specs = _attn_specs(dil, zw, kw, kcol, vcol) + [_head_spec()] * 3
    out_specs = [_head_spec()] + [_head_spec(kw)] * 4
    out_shape = [jax.ShapeDtypeStruct((sub, dil * A_WIDTH), F32)] + [jax.ShapeDtypeStruct((sub, dil * kw), F32)] * 4
    if have_sink:
        args.append(sink)
        in_specs.append(_whole((HALO, 128)))
        out_specs.append(_whole((HALO, 128)))
        out_shape.append(jax.ShapeDtypeStruct((HALO, 128), F32))
    res = pl.pallas_call(
        body, grid=(dil, sub // TQ), name=name, in_specs=in_specs, out_specs=out_specs, out_shape=out_shape,
        compiler_params=_cparams("arbitrary", "arbitrary"),
    )(*args)
    outs = [res[0].reshape(s, A_WIDTH)] + [a.reshape(s, kw) for a in res[1:5]]
    return outs + list(res[5:])


def _dz_assemble(parts_a, parts_c, dyb, zb, cw):
    s = zb.shape[0]
    nb = s // TQ

    def shifted(w, dil):
        return pl.BlockSpec((TQ, w), lambda i: (jnp.minimum(i + dil, nb - 1), 0))

    args, in_specs = [], []
    for dil, (dq, dkp, dkc, dvp, dvc) in zip(DILATIONS + (1,), parts_a + [parts_c]):
        w = dkp.shape[1]
        args += [dq, dkp, dkc, dvp, dvc]
        in_specs += [_rows(TQ, A_WIDTH), shifted(w, dil), _rows(TQ, w), shifted(w, dil), _rows(TQ, w)]
    args += [dyb, dyb, zb, zb, zb, cw]
    in_specs += [_rows(TQ, CONV_CH), _next_halo(TQ, CONV_CH, nb), _rows(TQ, ZB_W), _prev_halo(TQ, ZB_W),
                 _next_halo(TQ, ZB_W, nb), _whole((HALO, CONV_CH))]
    n_att = 20

    def body(*refs):
        att = refs[:n_att]
        dyb_ref, dybn_ref, zb_ref, zbp_ref, zbn_ref, cw_ref, dz_ref, dcw_ref = refs[n_att:]
        i = pl.program_id(0)

        @pl.when(i == 0)
        def _():
            dcw_ref[...] = jnp.zeros_like(dcw_ref)

        dq = jnp.zeros((TQ, A_WIDTH), F32)
        dk = jnp.zeros((TQ, A_WIDTH), F32)
        dv = jnp.zeros((TQ, A_WIDTH), F32)
        for p, dil in enumerate(DILATIONS):
            dq_r, dkp_r, dkc_r, dvp_r, dvc_r = att[5 * p:5 * p + 5]
            live = i + dil < nb
            dq = dq + dq_r[...]
            dk = dk + dkc_r[...] + jnp.where(live, dkp_r[...], 0.0)
            dv = dv + dvc_r[...] + jnp.where(live, dvp_r[...], 0.0)
        dz_ref[:, 0:A_WIDTH] = dq.astype(BF)
        dz_ref[:, A_WIDTH:2 * A_WIDTH] = dk.astype(BF)
        dz_ref[:, 2 * A_WIDTH:ZA_W] = dv.astype(BF)
        dq_r, dkp_r, dkc_r, dvp_r, dvc_r = att[15:20]
        live = i + 1 < nb
        c0 = ZA_W + ZB_W
        dz_ref[:, c0:c0 + A_WIDTH] = dq_r[...].astype(BF)
        dz_ref[:, c0 + A_WIDTH:c0 + A_WIDTH + C_KV_WIDTH] = (dkc_r[...] + jnp.where(live, dkp_r[...], 0.0)).astype(BF)
        dz_ref[:, c0 + A_WIDTH + C_KV_WIDTH:IN_WIDTH] = (dvc_r[...] + jnp.where(live, dvp_r[...], 0.0)).astype(BF)

        cw = cw_ref[...]
        zbp = jnp.where(i > 0, zbp_ref[...], 0.0)
        gb, gc, xb, u, u1, u2, c = _conv_parts(zb_ref[...], zbp, cw)
        dyb = dyb_ref[...]
        dcv = dyb * gb
        dcn = jnp.where(i + 1 < nb, dybn_ref[...] * zbn_ref[:, :CONV_CH], 0.0)
        du = cw[2:3, :] * dcv + cw[1:2, :] * _shift_up(dcv, 1, dcn) + cw[0:1, :] * _shift_up(dcv, 2, dcn)
        dz_ref[:, ZA_W:ZA_W + CONV_CH] = (dyb * c).astype(BF)
        dz_ref[:, ZA_W + CONV_CH:ZA_W + 2 * CONV_CH] = (du * xb).astype(BF)
        dz_ref[:, ZA_W + 2 * CONV_CH:c0] = (du * gc).astype(BF)
        row = lax.broadcasted_iota(jnp.int32, (HALO, CONV_CH), 0)
        upd = jnp.zeros((HALO, CONV_CH), F32)
        for t, uu in enumerate((u2, u1, u)):
            upd = jnp.where(row == t, jnp.sum(dcv * uu, axis=0, keepdims=True), upd)
        dcw_ref[...] += upd

    return pl.pallas_call(
        body, grid=(nb,), name="dz_assemble", in_specs=in_specs,
        out_specs=[_rows(TQ, IN_WIDTH), _whole((HALO, CONV_CH))],
        out_shape=[jax.ShapeDtypeStruct((s, IN_WIDTH), BF), jax.ShapeDtypeStruct((HALO, CONV_CH), F32)],
        compiler_params=_cparams("arbitrary"),
    )(*args)


def _qkv_bwd(dz, dx1, x, g, w_all, l, tb):
    s, d = x.shape

    def body(dz_ref, dx1_ref, x_ref, g_ref, w_ref, dx_ref, dg_ref):
        i = pl.program_id(0)

        @pl.when(i == 0)
        def _():
            dg_ref[...] = jnp.zeros_like(dg_ref)

        dh = _nt(dz_ref[...], w_ref[...])
        xv = x_ref[...]
        r = _rms_scale(xv)
        xhat = xv * r
        dg_ref[...] += jnp.sum(dh * xhat, axis=0, keepdims=True)
        dx_ref[...] = dx1_ref[...] + _norm_bwd(dh * g_ref[...], xhat, r)

    return pl.pallas_call(
        body, grid=(s // tb,), name="qkv_bwd",
        in_specs=[_rows(tb, IN_WIDTH), _rows(tb, d), _rows(tb, d), _whole((1, d)), _layer((d, IN_WIDTH), l)],
        out_specs=[_rows(tb, d), _whole((HALO, d))],
        out_shape=[jax.ShapeDtypeStruct((s, d), F32), jax.ShapeDtypeStruct((HALO, d), F32)],
        compiler_params=_cparams("arbitrary"),
    )(dz, dx1, x, g, w_all)


def _tile_rows(rows):
    return jnp.pad(rows, ((0, HALO - rows.shape[0]), (0, 0)))


def _local_step(x, tgt, w_in, w_o, w1, w2, conv_w, sinks, g_mix, g_group, g_mlp, g_final):
    s, d = x.shape
    ff = w1.shape[2]
    depth = w_in.shape[0]
    tb = min(512, s)
    tf = min(1024, ff)
    saved = []
    for l in range(depth):
        cw = _tile_rows(conv_w[l])
        sk = jnp.pad(sinks[l].reshape(1, N_HEADS), ((0, HALO - 1), (0, 128 - N_HEADS)))
        h, za, zb, zc = _qkv_fwd(x, g_mix[l][None], w_in, l, tb)
        state = None
        for p, dil in enumerate(DILATIONS):
            state = _attn_fwd(za, dil, A_WIDTH, 1, 2, 1, A_MAX_DIST, state, None, p == len(DILATIONS) - 1,
                              "attn_a_fwd_%d" % dil)
        ya, lse_a = state
        yc, lse_c = _attn_fwd(zc, 1, C_KV_WIDTH, 3, 4, C_GROUP, C_MAX_DIST, None, sk, True, "attn_c_fwd")
        x1, yb = _mix_fwd(x, ya, yc, zb, cw, g_group[l][None], w_o, l, tb)
        x2, h2, ap = _mlp_fwd(x1, g_mlp[l][None], w1, w2, l, tb, tf)
        saved.append((x, h, za, zb, zc, ya, lse_a, yc, lse_c, yb, x1, h2, ap, cw, sk))
        x = x2
    dx, loss_tile, dg_final = _loss_head(x, g_final[None], tgt, tb)
    grads = [None] * depth
    gwin = gwo = gw1 = gw2 = None
    for l in reversed(range(depth)):
        x0, h, za, zb, zc, ya, lse_a, yc, lse_c, yb, x1, h2, ap, cw, sk = saved[l]
        dx1, dap, dg_mlp = _mlp_bwd(dx, x1, ap, g_mlp[l][None], w1, w2, l, tb, tf)
        gw2 = _wgrad(ap, dx, l, gw2, min(1024, ff), d, tb, "wgrad_ff_out", relu2=True)
        gw1 = _wgrad(h2, dap, l, gw1, d, min(1024, ff), tb, "wgrad_ff_in")
        n, dya, dyc, dd_a, dd_c, dyb, dg_group = _mix_bwd(dx1, ya, yb, yc, g_group[l][None], w_o, l, tb)
        gwo = _wgrad(n, dx1, l, gwo, MIX_WIDTH, d, tb, "wgrad_o")
        parts_a = [_attn_bwd(za, dya, lse_a, dd_a, dil, A_WIDTH, 1, 2, 1, A_MAX_DIST, None, "attn_a_bwd_%d" % dil)
                   for dil in DILATIONS]
        *parts_c, dsink = _attn_bwd(zc, dyc, lse_c, dd_c, 1, C_KV_WIDTH, 3, 4, C_GROUP, C_MAX_DIST, sk, "attn_c_bwd")
        dz, dcw = _dz_assemble(parts_a, parts_c, dyb, zb, cw)
        dx, dg_mix = _qkv_bwd(dz, dx1, x0, g_mix[l][None], w_in, l, tb)
        gwin = _wgrad(h, dz, l, gwin, d, IN_WIDTH // 4, tb, "wgrad_in")
        grads[l] = (dcw, dsink, dg_mix, dg_group, dg_mlp)
    return loss_tile, dx, [gwin, gwo, gw1, gw2], grads, dg_final


ANY = pl.BlockSpec(memory_space=pl.ANY)
SHARD_AXES = (2, 1, 2, 1)
N_BIG = len(SHARD_AXES)
N_CHIPS = 4
N_DEV = 8


def _mesh_pos():
    return lax.axis_index("x"), lax.axis_index("y"), lax.axis_index("c")


def _flip(v, bit):
    return 1 - v if bit else v


def _shard_of(ref, w, layer, chip, n):
    start = pl.multiple_of(chip * n, 128)
    if SHARD_AXES[w] == 2:
        return ref.at[layer, :, pl.ds(start, n)]
    return ref.at[layer, pl.ds(start, n), :]


def _place_shard(shard, w, chip_arr, name):
    _, rows, cols = shard.shape
    tr = min(256, rows)
    nr = rows // tr
    if SHARD_AXES[w] == 2:
        full = (2, rows, cols * N_CHIPS)
        out_map = lambda l, i, chip: (l, i, chip[0])
    else:
        full = (2, rows * N_CHIPS, cols)
        out_map = lambda l, i, chip: (l, chip[0] * nr + i, 0)

    def body(chip_ref, x_ref, o_ref):
        o_ref[...] = x_ref[...].astype(BF)

    return pl.pallas_call(
        body, name=name,
        grid_spec=pltpu.PrefetchScalarGridSpec(
            num_scalar_prefetch=1, grid=(2, nr),
            in_specs=[pl.BlockSpec((None, tr, cols), lambda l, i, chip: (l, i, 0))],
            out_specs=pl.BlockSpec((None, tr, cols), out_map)),
        out_shape=jax.ShapeDtypeStruct(full, BF), compiler_params=_cparams("parallel", "parallel"),
    )(chip_arr, shard)


def _gather_weights(placed, conv_tile):
    widths = [p.shape[ax] // N_CHIPS for p, ax in zip(placed, SHARD_AXES)]
    fwd0 = 3 * N_BIG + 3

    def body(*refs):
        conv_src = refs[N_BIG]
        dsts, conv_dst = refs[N_BIG + 1:2 * N_BIG + 1], refs[2 * N_BIG + 1]
        send_sems, recv_sems, local_sem = refs[2 * N_BIG + 2:]
        x, y, c = _mesh_pos()
        me = 2 * x + y
        sibling = (x, y, 1 - c)
        chips = [(1 - x, y), (x, 1 - y), (1 - x, 1 - y)]

        def remote(src, dst, k, to):
            return pltpu.make_async_remote_copy(src_ref=src, dst_ref=dst, send_sem=send_sems.at[k],
                                                recv_sem=recv_sems.at[k], device_id=to, device_id_type=MESH)

        local = pltpu.make_async_copy(conv_src, conv_dst.at[me], local_sem)
        local.start()
        sends = []
        for j, (qx, qy) in enumerate(chips):
            for w in range(N_BIG):
                mine = _shard_of(dsts[w], w, c, me, widths[w])
                sends.append(remote(mine, mine, j * N_BIG + w, (qx, qy, c)))
            sends.append(remote(conv_src, conv_dst.at[me], 3 * N_BIG + j, (qx, qy, c)))
        for cp in sends:
            cp.start()
        passed = []
        for j, (qx, qy) in enumerate(chips):
            q = 2 * qx + qy
            for w in range(N_BIG):
                landed = _shard_of(dsts[w], w, c, q, widths[w])
                remote(landed, landed, j * N_BIG + w, sibling).wait_recv()
                cp = remote(landed, landed, fwd0 + j * N_BIG + w, sibling)
                cp.start()
                passed.append(cp)
            remote(conv_src, conv_dst.at[q], 3 * N_BIG + j, sibling).wait_recv()
        for j, (qx, qy) in enumerate(chips):
            q = 2 * qx + qy
            for w in range(N_BIG):
                landed = _shard_of(dsts[w], w, 1 - c, q, widths[w])
                remote(landed, landed, fwd0 + j * N_BIG + w, sibling).wait_recv()
        for cp in sends + passed:
            cp.wait_send()
        local.wait()

    n_sem = fwd0 + 3 * N_BIG
    return pl.pallas_call(
        body, name="gather_weights", in_specs=[ANY] * (N_BIG + 1), out_specs=[ANY] * (N_BIG + 1),
        out_shape=[jax.ShapeDtypeStruct(p.shape, p.dtype) for p in placed]
        + [jax.ShapeDtypeStruct((N_CHIPS,) + conv_tile.shape, conv_tile.dtype)],
        input_output_aliases={w: w for w in range(N_BIG)},
        scratch_shapes=[pltpu.SemaphoreType.DMA((n_sem,)), pltpu.SemaphoreType.DMA((n_sem,)),
                        pltpu.SemaphoreType.DMA(())],
    )(*placed, conv_tile)


def _scatter_grads(grads):
    widths = [g.shape[ax] // N_CHIPS for g, ax in zip(grads, SHARD_AXES)]
    slot_shapes = []
    for g, ax, n in zip(grads, SHARD_AXES, widths):
        shape = list(g.shape[1:])
        shape[ax - 1] = n
        slot_shapes.append(jax.ShapeDtypeStruct((N_DEV - 1,) + tuple(shape), g.dtype))

    def body(*refs):
        srcs, dsts = refs[:N_BIG], refs[N_BIG:2 * N_BIG]
        send_sems, recv_sems = refs[2 * N_BIG:]
        x, y, c = _mesh_pos()
        sends = []
        for r in range(1, N_DEV):
            tx, ty, tc = _flip(x, r & 4), _flip(y, r & 2), _flip(c, r & 1)
            for w in range(N_BIG):
                k = (r - 1) * N_BIG + w
                sends.append(pltpu.make_async_remote_copy(
                    src_ref=_shard_of(srcs[w], w, tc, 2 * tx + ty, widths[w]), dst_ref=dsts[w].at[r - 1],
                    send_sem=send_sems.at[k], recv_sem=recv_sems.at[k], device_id=(tx, ty, tc), device_id_type=MESH))
        for cp in sends:
            cp.start()
        for cp in sends:
            cp.wait_recv()
        for cp in sends:
            cp.wait_send()

    n_sem = (N_DEV - 1) * N_BIG
    return pl.pallas_call(
        body, name="scatter_grads", in_specs=[ANY] * N_BIG, out_specs=[ANY] * N_BIG, out_shape=slot_shapes,
        scratch_shapes=[pltpu.SemaphoreType.DMA((n_sem,)), pltpu.SemaphoreType.DMA((n_sem,))],
    )(*grads)


def _sum_slots(grad, slots, w, pos_arr, name):
    _, rows, cols = slots.shape
    tr = min(256, rows)
    nr = rows // tr
    if SHARD_AXES[w] == 2:
        own_map = lambda i, pos: (pos[1], i, pos[0])
    else:
        own_map = lambda i, pos: (pos[1], pos[0] * nr + i, 0)

    def body(pos_ref, own_ref, s_ref, o_ref):
        acc = own_ref[...].astype(F32)
        for r in range(N_DEV - 1):
            acc = acc + s_ref[r].astype(F32)
        o_ref[...] = acc

    return pl.pallas_call(
        body, name=name,
        grid_spec=pltpu.PrefetchScalarGridSpec(
            num_scalar_prefetch=1, grid=(nr,),
            in_specs=[pl.BlockSpec((None, tr, cols), own_map),
                      pl.BlockSpec((N_DEV - 1, tr, cols), lambda i, pos: (0, i, 0))],
            out_specs=pl.BlockSpec((tr, cols), lambda i, pos: (i, 0))),
        out_shape=jax.ShapeDtypeStruct((rows, cols), F32), compiler_params=_cparams("parallel"),
    )(pos_arr, grad, slots)


def _swap_layers(halves):
    def body(*refs):
        srcs, dsts = refs[:N_BIG], refs[N_BIG:2 * N_BIG]
        send_sems, recv_sems = refs[2 * N_BIG:]
        x, y, c = _mesh_pos()
        sends = [pltpu.make_async_remote_copy(src_ref=srcs[w], dst_ref=dsts[w], send_sem=send_sems.at[w],
                                              recv_sem=recv_sems.at[w], device_id=(x, y, 1 - c), device_id_type=MESH)
                 for w in range(N_BIG)]
        for cp in sends:
            cp.start()
        for cp in sends:
            cp.wait_recv()
        for cp in sends:
            cp.wait_send()

    return pl.pallas_call(
        body, name="swap_layers", in_specs=[ANY] * N_BIG, out_specs=[ANY] * N_BIG,
        out_shape=[jax.ShapeDtypeStruct(h.shape, h.dtype) for h in halves],
        scratch_shapes=[pltpu.SemaphoreType.DMA((N_BIG,)), pltpu.SemaphoreType.DMA((N_BIG,))],
    )(*halves)


def _adamw_math(w, g, m, v):
    m = ADAM_B1 * m + (1.0 - ADAM_B1) * g
    v = ADAM_B2 * v + (1.0 - ADAM_B2) * jnp.square(g)
    m_hat = m / (1.0 - ADAM_B1 ** ADAM_STEP)
    v_hat = v / (1.0 - ADAM_B2 ** ADAM_STEP)
    delta = -ADAM_LR * (m_hat / (jnp.sqrt(v_hat) + ADAM_EPS) + ADAM_WD * w)
    return delta, m, v


def _adamw(w, g_own, g_other, m, v, pos_arr, name):
    shape = w.shape
    _, rows, cols = shape
    tr = min(256, rows)

    def body(pos_ref, w_ref, own_ref, other_ref, m_ref, v_ref, g_ref, d_ref, m2_ref, v2_ref):
        g = jnp.where(pl.program_id(0) == pos_ref[1], own_ref[...], other_ref[...])
        g_ref[...] = g
        d_ref[...], m2_ref[...], v2_ref[...] = _adamw_math(w_ref[...], g, m_ref[...], v_ref[...])

    full = pl.BlockSpec((None, tr, cols), lambda l, i, pos: (l, i, 0))
    half = pl.BlockSpec((tr, cols), lambda l, i, pos: (i, 0))
    return pl.pallas_call(
        body, name=name,
        grid_spec=pltpu.PrefetchScalarGridSpec(
            num_scalar_prefetch=1, grid=(2, rows // tr),
            in_specs=[full, half, half, full, full], out_specs=[full] * 4),
        out_shape=[jax.ShapeDtypeStruct(shape, F32)] * 4, compiler_params=_cparams("parallel", "parallel"),
    )(pos_arr, w, g_own, g_other, m, v)


def _small_sync(part, w, m, v):
    rows, cols = part.shape

    def body(p_ref, w_ref, m_ref, v_ref, g_ref, d_ref, m2_ref, v2_ref, slots, send_sems, recv_sems):
        x, y, c = _mesh_pos()
        me = 4 * x + 2 * y + c
        slots[me] = p_ref[...]
        sends = []
        for r in range(1, N_DEV):
            to = (_flip(x, r & 4), _flip(y, r & 2), _flip(c, r & 1))
            sends.append(pltpu.make_async_remote_copy(
                src_ref=p_ref, dst_ref=slots.at[me], send_sem=send_sems.at[r - 1], recv_sem=recv_sems.at[r - 1],
                device_id=to, device_id_type=MESH))
        for cp in sends:
            cp.start()
        for cp in sends:
            cp.wait_recv()
        for cp in sends:
            cp.wait_send()
        g = slots[0]
        for i in range(1, N_DEV):
            g = g + slots[i]
        g_ref[...] = g
        d_ref[...], m2_ref[...], v2_ref[...] = _adamw_math(w_ref[...], g, m_ref[...], v_ref[...])

    vm = pl.BlockSpec(memory_space=pltpu.VMEM)
    return pl.pallas_call(
        body, name="small_sync", in_specs=[vm] * 4, out_specs=[vm] * 4,
        out_shape=[jax.ShapeDtypeStruct((rows, cols), F32)] * 4,
        scratch_shapes=[pltpu.VMEM((N_DEV, rows, cols), F32), pltpu.SemaphoreType.DMA((N_DEV - 1,)),
                        pltpu.SemaphoreType.DMA((N_DEV - 1,))],
    )(part, w, m, v)


def _pack_small(d, g_mix, g_group, g_mlp, g_final, conv_full, sinks, scalar):
    def part(rows):
        return jnp.pad(rows, ((0, HALO - rows.shape[0]), (0, d - rows.shape[1])))
    return jnp.concatenate([part(g_mix), part(g_group), part(g_mlp), part(g_final[None]),
                            part(conv_full.reshape(6, CONV_CH)), part(sinks.reshape(2, N_HEADS)),
                            part(scalar.reshape(1, 1))], axis=0)


def _unpack_small(p, dm):
    return (p[0:2, :dm], p[8:10, :MIX_WIDTH], p[16:18, :dm], p[24, :dm], p[32:38, :CONV_CH].reshape(2, 3, CONV_CH),
            p[40:42, :N_HEADS].reshape(2, 2, C_GROUP), p[48, 0])


def kernel(x, w_in, conv_w, sinks, g_mix, g_group, w_o, g_mlp, w_ff_in, w_ff_out, g_final, loss_target, m_w_in, m_conv_w, m_sinks, m_g_mix, m_g_group, m_w_o, m_g_mlp, m_w_ff_in, m_w_ff_out, m_g_final, v_w_in, v_conv_w, v_sinks, v_g_mix, v_g_group, v_w_o, v_g_mlp, v_w_ff_in, v_w_ff_out, v_g_final):
    d = max(x.shape[2], MIX_WIDTH)
    chip = 2 * lax.axis_index("x") + lax.axis_index("y")
    conv_n = conv_w.shape[2]

    pos_arr = jnp.stack([chip, lax.axis_index("c")]).astype(jnp.int32)
    placed = [_place_shard(w, i, pos_arr[:1], "place_shard_%d" % i)
              for i, w in enumerate((w_in, w_o, w_ff_in, w_ff_out))]
    conv_tile = jnp.pad(conv_w.reshape(6, conv_n), ((0, HALO - 6), (0, 128 - conv_n)))
    fw_in, fw_o, fw1, fw2, conv_all = _gather_weights(placed, conv_tile)
    conv_full = conv_all[:, :6, :conv_n].reshape(N_CHIPS, 2, 3, conv_n).transpose(1, 2, 0, 3).reshape(2, 3, CONV_CH)

    loss_tile, dx, full_grads, grads, dg_final = _local_step(x[0], loss_target[0], fw_in, fw_o, fw1, fw2, conv_full,
                                                             sinks, g_mix, g_group, g_mlp, g_final)

    slots = _scatter_grads(full_grads)
    own = [_sum_slots(g, s, w, pos_arr, "sum_slots_%d" % w) for w, (g, s) in enumerate(zip(full_grads, slots))]
    other = _swap_layers(own)

    def both(i):
        return jnp.stack([grads[0][i][0], grads[1][i][0]])
    dconv = jnp.stack([grads[0][0][:3], grads[1][0][:3]])
    dsinks = jnp.stack([grads[0][1][0, :N_HEADS], grads[1][1][0, :N_HEADS]])
    part = _pack_small(d, both(2), both(3), both(4), dg_final[0], dconv, dsinks, loss_tile[0, 0])

    def spread(shard):
        return lax.dynamic_update_slice(jnp.zeros((2, 3, CONV_CH), F32), shard, (0, 0, chip * conv_n))
    zero = jnp.zeros((), F32)
    packs = [_pack_small(d, a, b, c_, e, spread(f), g_, zero) for a, b, c_, e, f, g_ in (
        (g_mix, g_group, g_mlp, g_final, conv_w, sinks),
        (m_g_mix, m_g_group, m_g_mlp, m_g_final, m_conv_w, m_sinks),
        (v_g_mix, v_g_group, v_g_mlp, v_g_final, v_conv_w, v_sinks))]
    small = [_unpack_small(p, x.shape[2]) for p in _small_sync(part, *packs)]

    def shard_of(full):
        return lax.dynamic_slice(full, (0, 0, chip * conv_n), (2, 3, conv_n))
    small = [(s[0], s[1], s[2], s[3], shard_of(s[4]), s[5], s[6]) for s in small]
    loss = small[0][6]

    big = [_adamw(w, own[i], other[i], m, v, pos_arr, "adamw_%d" % i) for i, (w, m, v) in enumerate((
        (w_in, m_w_in, v_w_in), (w_o, m_w_o, v_w_o), (w_ff_in, m_w_ff_in, v_w_ff_in),
        (w_ff_out, m_w_ff_out, v_w_ff_out)))]

    def ordered(kind):
        b = [big[i][kind] for i in range(N_BIG)]
        s = small[kind]
        return [b[0], s[4], s[5], s[0], s[1], b[1], s[2], b[2], b[3], s[3]]

    return (loss, dx[None], *ordered(0), *ordered(1), *ordered(2), *ordered(3))
```

```python
import functools

import jax
import jax.numpy as jnp
from jax import lax
from jax.experimental import pallas as pl
from jax.experimental.pallas import tpu as pltpu

HEAD_DIM = 64
N_HEADS = 6
C_GROUP = 3
A_WIDTH = N_HEADS * HEAD_DIM
C_KV_WIDTH = 2 * HEAD_DIM
CONV_CH = 256
ZA_W = 3 * A_WIDTH
ZB_W = 3 * CONV_CH
ZC_W = A_WIDTH + 2 * C_KV_WIDTH
IN_WIDTH = ZA_W + ZB_W + ZC_W
MIX_WIDTH = A_WIDTH + CONV_CH + A_WIDTH
DILATIONS = (1, 4, 16)
A_MAX_DIST = 128
C_MAX_DIST = 127
TQ = 128
EPS = 1e-6
SCALE = HEAD_DIM ** -0.5
NEG = -1e30
HALO = 8

ADAM_LR = 0.001
ADAM_B1 = 0.9
ADAM_B2 = 0.999
ADAM_EPS = 1e-08
ADAM_WD = 0.01
ADAM_STEP = 10

BF = jnp.bfloat16
F32 = jnp.float32
MESH = pl.DeviceIdType.MESH
VMEM_LIMIT = 56 * 1024 * 1024


def _cparams(*sem):
    return pltpu.CompilerParams(dimension_semantics=sem, vmem_limit_bytes=VMEM_LIMIT)


def _nt(a, b):
    return lax.dot_general(a, b, (((1,), (1,)), ((), ())), preferred_element_type=F32)


def _tn(a, b):
    return lax.dot_general(a, b, (((0,), (0,)), ((), ())), preferred_element_type=F32)


def _nn(a, b):
    return jnp.dot(a, b, preferred_element_type=F32)


def _rows(tb, w):
    return pl.BlockSpec((tb, w), lambda i: (i, 0))


def _whole(shape):
    return pl.BlockSpec(shape, lambda *_: (0,) * len(shape))


def _layer(shape, l):
    return pl.BlockSpec((None,) + shape, lambda *_: (l,) + (0,) * len(shape))


def _rms_scale(v):
    return lax.rsqrt(jnp.mean(v * v, axis=-1, keepdims=True) + EPS)


def _norm_bwd(dxhat, xhat, r):
    return r * (dxhat - xhat * jnp.mean(dxhat * xhat, axis=-1, keepdims=True))


def _qkv_fwd(x, g, w_all, l, tb):
    s, d = x.shape

    def body(x_ref, g_ref, w_ref, h_ref, za_ref, zb_ref, zc_ref):
        xv = x_ref[...]
        h = ((xv * _rms_scale(xv)) * g_ref[...]).astype(BF)
        h_ref[...] = h
        z = _nn(h, w_ref[...])
        za_ref[...] = z[:, :ZA_W].astype(BF)
        zb_ref[...] = z[:, ZA_W:ZA_W + ZB_W]
        zc_ref[...] = z[:, ZA_W + ZB_W:].astype(BF)

    return pl.pallas_call(
        body, grid=(s // tb,), name="qkv_fwd",
        in_specs=[_rows(tb, d), _whole((1, d)), _layer((d, IN_WIDTH), l)],
        out_specs=[_rows(tb, d), _rows(tb, ZA_W), _rows(tb, ZB_W), _rows(tb, ZC_W)],
        out_shape=[jax.ShapeDtypeStruct((s, d), BF), jax.ShapeDtypeStruct((s, ZA_W), BF),
                   jax.ShapeDtypeStruct((s, ZB_W), F32), jax.ShapeDtypeStruct((s, ZC_W), BF)],
        compiler_params=_cparams("parallel"),
    )(x, g, w_all)


def _band_mask(b, max_dist):
    qi = lax.broadcasted_iota(jnp.int32, (TQ, 2 * TQ), 0)
    kj = lax.broadcasted_iota(jnp.int32, (TQ, 2 * TQ), 1)
    dist = TQ + qi - kj
    return (dist >= 0) & (dist <= max_dist) & ((kj >= TQ) | (b > 0))


def _attn_specs(dil, zw, kw, kcol, vcol):
    nq, nk = zw // A_WIDTH, zw // kw
    q = pl.BlockSpec((TQ, A_WIDTH), lambda r, b: (b, r * nq))
    kp = pl.BlockSpec((TQ, kw), lambda r, b: (jnp.maximum(b - 1, 0), r * nk + kcol))
    kc = pl.BlockSpec((TQ, kw), lambda r, b: (b, r * nk + kcol))
    vp = pl.BlockSpec((TQ, kw), lambda r, b: (jnp.maximum(b - 1, 0), r * nk + vcol))
    vc = pl.BlockSpec((TQ, kw), lambda r, b: (b, r * nk + vcol))
    return [q, kp, kc, vp, vc]


def _head_spec(w=A_WIDTH):
    return pl.BlockSpec((TQ, w), lambda r, b: (b, r))


def _hs(h):
    return slice(h * HEAD_DIM, (h + 1) * HEAD_DIM)


def _attn_fwd(z, dil, kw, kcol, vcol, n_rep, max_dist, state, sink, last, name):
    s, zw = z.shape
    sub = s // dil
    zv = z.reshape(sub, dil * zw)
    have_state, have_sink = state is not None, sink is not None

    def body(*refs):
        q_ref, kp_ref, kc_ref, vp_ref, vc_ref = refs[:5]
        pos = 5
        if have_state:
            acc_in, m_in, l_in = refs[pos:pos + 3]
            pos += 3
        if have_sink:
            sink_ref = refs[pos]
            pos += 1
        outs = refs[pos:]
        mask = _band_mask(pl.program_id(1), max_dist)
        for h in range(N_HEADS):
            kh = h // n_rep
            q = q_ref[:, _hs(h)]
            k2 = jnp.concatenate([kp_ref[:, _hs(kh)], kc_ref[:, _hs(kh)]], axis=0)
            v2 = jnp.concatenate([vp_ref[:, _hs(kh)], vc_ref[:, _hs(kh)]], axis=0)
            sc = jnp.where(mask, _nt(q, k2) * SCALE, NEG)
            m_new = jnp.max(sc, axis=1, keepdims=True)
            if have_sink:
                sk = sink_ref[0:1, h:h + 1]
                m_new = jnp.maximum(m_new, sk)
            if have_state:
                m_old = m_in[:, h * HEAD_DIM:h * HEAD_DIM + 1]
                m_new = jnp.maximum(m_new, m_old)
            p = jnp.exp(sc - m_new)
            l_new = jnp.sum(p, axis=1, keepdims=True)
            acc = _nn(p.astype(BF), v2)
            if have_state:
                alpha = jnp.exp(m_old - m_new)
                l_new = l_new + alpha * l_in[:, h * HEAD_DIM:h * HEAD_DIM + 1]
                acc = acc + alpha * acc_in[:, _hs(h)]
            if have_sink:
                l_new = l_new + jnp.exp(sk - m_new)
            if last:
                outs[0][:, _hs(h)] = acc / l_new
                outs[1][:, _hs(h)] = jnp.broadcast_to(m_new + jnp.log(l_new), (TQ, HEAD_DIM))
            else:
                outs[0][:, _hs(h)] = acc
                outs[1][:, _hs(h)] = jnp.broadcast_to(m_new, (TQ, HEAD_DIM))
                outs[2][:, _hs(h)] = jnp.broadcast_to(l_new, (TQ, HEAD_DIM))

    args = [zv] * 5
    in_specs = _attn_specs(dil, zw, kw, kcol, vcol)
    if have_state:
        args += [a.reshape(sub, dil * A_WIDTH) for a in state]
        in_specs += [_head_spec()] * 3
    if have_sink:
        args.append(sink)
        in_specs.append(_whole((HALO, 128)))
    n_out = 2 if last else 3
    res = pl.pallas_call(
        body, grid=(dil, sub // TQ), name=name, in_specs=in_specs,
        out_specs=[_head_spec()] * n_out,
        out_shape=[jax.ShapeDtypeStruct((sub, dil * A_WIDTH), F32)] * n_out,
        compiler_params=_cparams("parallel", "parallel"),
    )(*args)
    return [a.reshape(s, A_WIDTH) for a in res]


def _shift_down(v, n, halo):
    rows = v.shape[0]
    out = pltpu.roll(v, n, 0)
    row = lax.broadcasted_iota(jnp.int32, v.shape, 0)
    for t in range(n):
        out = jnp.where(row == t, halo[HALO - n + t:HALO - n + t + 1, :], out)
    return out


def _shift_up(v, n, halo):
    rows = v.shape[0]
    out = pltpu.roll(v, rows - n, 0)
    row = lax.broadcasted_iota(jnp.int32, v.shape, 0)
    for t in range(n):
        out = jnp.where(row == rows - n + t, halo[t:t + 1, :], out)
    return out


def _conv_parts(zb, zb_prev, cw):
    gb, gc, xb = zb[:, :CONV_CH], zb[:, CONV_CH:2 * CONV_CH], zb[:, 2 * CONV_CH:]
    u = gc * xb
    uh = zb_prev[:, CONV_CH:2 * CONV_CH] * zb_prev[:, 2 * CONV_CH:]
    u1 = _shift_down(u, 1, uh)
    u2 = _shift_down(u, 2, uh)
    c = cw[0:1, :] * u2 + cw[1:2, :] * u1 + cw[2:3, :] * u
    return gb, gc, xb, u, u1, u2, c


def _prev_halo(tb, w):
    return pl.BlockSpec((HALO, w), lambda i: (jnp.maximum(i * (tb // HALO) - 1, 0), 0))


def _next_halo(tb, w, nblk):
    return pl.BlockSpec((HALO, w), lambda i: (jnp.minimum((i + 1) * (tb // HALO), nblk * (tb // HALO) - 1), 0))


def _mix_fwd(x, ya, yc, zb, cw, gg, wo_all, l, tb):
    s, d = x.shape

    def body(x_ref, ya_ref, yc_ref, zb_ref, zbp_ref, cw_ref, gg_ref, wo_ref, x1_ref, yb_ref):
        i = pl.program_id(0)
        zbp = jnp.where(i > 0, zbp_ref[...], 0.0)
        gb, _, _, _, _, _, c = _conv_parts(zb_ref[...], zbp, cw_ref[...])
        yb = gb * c
        yb_ref[...] = yb
        ya, yc = ya_ref[...], yc_ref[...]
        n = jnp.concatenate([ya * _rms_scale(ya), yb * _rms_scale(yb), yc * _rms_scale(yc)], axis=1)
        n = (n * gg_ref[...]).astype(BF)
        x1_ref[...] = x_ref[...] + _nn(n, wo_ref[...])

    return pl.pallas_call(
        body, grid=(s // tb,), name="mix_fwd",
        in_specs=[_rows(tb, d), _rows(tb, A_WIDTH), _rows(tb, A_WIDTH), _rows(tb, ZB_W), _prev_halo(tb, ZB_W),
                  _whole((HALO, CONV_CH)), _whole((1, MIX_WIDTH)), _layer((MIX_WIDTH, d), l)],
        out_specs=[_rows(tb, d), _rows(tb, CONV_CH)],
        out_shape=[jax.ShapeDtypeStruct((s, d), F32), jax.ShapeDtypeStruct((s, CONV_CH), F32)],
        compiler_params=_cparams("parallel"),
    )(x, ya, yc, zb, zb, cw, gg, wo_all)


def _mlp_fwd(x1, g, w1_all, w2_all, l, tb, tf):
    s, d = x1.shape
    ff = w1_all.shape[2]
    nj = ff // tf

    def body(x_ref, g_ref, w1_ref, w2_ref, x2_ref, h2_ref, ap_ref, acc):
        j = pl.program_id(1)

        @pl.when(j == 0)
        def _():
            xv = x_ref[...]
            h2_ref[...] = ((xv * _rms_scale(xv)) * g_ref[...]).astype(BF)
            acc[...] = jnp.zeros_like(acc)

        ap = _nn(h2_ref[...], w1_ref[...])
        ap_ref[...] = ap.astype(BF)
        a = jnp.square(jnp.maximum(ap, 0.0)).astype(BF)
        acc[...] += _nn(a, w2_ref[...])

        @pl.when(j == nj - 1)
        def _():
            x2_ref[...] = x_ref[...] + acc[...]

    return pl.pallas_call(
        body, grid=(s // tb, nj), name="mlp_fwd",
        in_specs=[pl.BlockSpec((tb, d), lambda i, j: (i, 0)), _whole((1, d)),
                  pl.BlockSpec((None, d, tf), lambda i, j: (l, 0, j)),
                  pl.BlockSpec((None, tf, d), lambda i, j: (l, j, 0))],
        out_specs=[pl.BlockSpec((tb, d), lambda i, j: (i, 0)), pl.BlockSpec((tb, d), lambda i, j: (i, 0)),
                   pl.BlockSpec((tb, tf), lambda i, j: (i, j))],
        out_shape=[jax.ShapeDtypeStruct((s, d), F32), jax.ShapeDtypeStruct((s, d), BF),
                   jax.ShapeDtypeStruct((s, ff), BF)],
        scratch_shapes=[pltpu.VMEM((tb, d), F32)],
        compiler_params=_cparams("parallel", "arbitrary"),
    )(x1, g, w1_all, w2_all)


def _loss_head(x, g, tgt, tb):
    s, d = x.shape

    def body(x_ref, g_ref, t_ref, dx_ref, loss_ref, dg_ref):
        i = pl.program_id(0)

        @pl.when(i == 0)
        def _():
            loss_ref[...] = jnp.zeros_like(loss_ref)
            dg_ref[...] = jnp.zeros_like(dg_ref)

        xv = x_ref[...]
        r = _rms_scale(xv)
        xhat = xv * r
        err = xhat * g_ref[...] - t_ref[...]
        part = jnp.sum(jnp.mean(jnp.square(err), axis=-1, keepdims=True), axis=0, keepdims=True)
        loss_ref[...] += 0.5 * part
        dy = err * (1.0 / d)
        dg_ref[...] += jnp.sum(dy * xhat, axis=0, keepdims=True)
        dx_ref[...] = _norm_bwd(dy * g_ref[...], xhat, r)

    return pl.pallas_call(
        body, grid=(s // tb,), name="loss_head",
        in_specs=[_rows(tb, d), _whole((1, d)), _rows(tb, d)],
        out_specs=[_rows(tb, d), _whole((HALO, 128)), _whole((HALO, d))],
        out_shape=[jax.ShapeDtypeStruct((s, d), F32), jax.ShapeDtypeStruct((HALO, 128), F32),
                   jax.ShapeDtypeStruct((HALO, d), F32)],
        compiler_params=_cparams("arbitrary"),
    )(x, g, tgt)


def _mlp_bwd(dx2, x1, ap, g, w1_all, w2_all, l, tb, tf):
    s, d = x1.shape
    ff = ap.shape[1]
    nj = ff // tf

    def body(dx2_ref, x1_ref, ap_ref, g_ref, w1_ref, w2_ref, dx1_ref, dap_ref, dg_ref, acc):
        i, j = pl.program_id(0), pl.program_id(1)

        @pl.when((i == 0) & (j == 0))
        def _():
            dg_ref[...] = jnp.zeros_like(dg_ref)

        @pl.when(j == 0)
        def _():
            acc[...] = jnp.zeros_like(acc)

        da = _nt(dx2_ref[...].astype(BF), w2_ref[...])
        dap = (da * (2.0 * jnp.maximum(ap_ref[...].astype(F32), 0.0))).astype(BF)
        dap_ref[...] = dap
        acc[...] += _nt(dap, w1_ref[...])

        @pl.when(j == nj - 1)
        def _():
            xv = x1_ref[...]
            r = _rms_scale(xv)
            xhat = xv * r
            dh = acc[...]
            dg_ref[...] += jnp.sum(dh * xhat, axis=0, keepdims=True)
            dx1_ref[...] = dx2_ref[...] + _norm_bwd(dh * g_ref[...], xhat, r)

    return pl.pallas_call(
        body, grid=(s // tb, nj), name="mlp_bwd",
        in_specs=[pl.BlockSpec((tb, d), lambda i, j: (i, 0)), pl.BlockSpec((tb, d), lambda i, j: (i, 0)),
                  pl.BlockSpec((tb, tf), lambda i, j: (i, j)),
                  _whole((1, d)), pl.BlockSpec((None, d, tf), lambda i, j: (l, 0, j)),
                  pl.BlockSpec((None, tf, d), lambda i, j: (l, j, 0))],
        out_specs=[pl.BlockSpec((tb, d), lambda i, j: (i, 0)), pl.BlockSpec((tb, tf), lambda i, j: (i, j)),
                   _whole((HALO, d))],
        out_shape=[jax.ShapeDtypeStruct((s, d), F32), jax.ShapeDtypeStruct((s, ff), BF),
                   jax.ShapeDtypeStruct((HALO, d), F32)],
        scratch_shapes=[pltpu.VMEM((tb, d), F32)],
        compiler_params=_cparams("arbitrary", "arbitrary"),
    )(dx2, x1, ap, g, w1_all, w2_all)


def _wgrad(a, b, tm, tn, ts, name, relu2=False):
    s, m = a.shape
    n = b.shape[1]
    ns = s // ts

    def body(a_ref, b_ref, o_ref, acc):
        k = pl.program_id(2)

        @pl.when(k == 0)
        def _():
            acc[...] = jnp.zeros_like(acc)

        av = a_ref[...]
        if relu2:
            av = jnp.square(jnp.maximum(av.astype(F32), 0.0)).astype(BF)
        acc[...] += _tn(av, b_ref[...].astype(BF))

        @pl.when(k == ns - 1)
        def _():
            o_ref[...] = acc[...].astype(BF)

    return pl.pallas_call(
        body, grid=(m // tm, n // tn, ns), name=name,
        in_specs=[pl.BlockSpec((ts, tm), lambda i, j, k: (k, i)), pl.BlockSpec((ts, tn), lambda i, j, k: (k, j))],
        out_specs=pl.BlockSpec((tm, tn), lambda i, j, k: (i, j)),
        out_shape=jax.ShapeDtypeStruct((m, n), BF),
        scratch_shapes=[pltpu.VMEM((tm, tn), F32)],
        compiler_params=_cparams("parallel", "parallel", "arbitrary"),
    )(a, b)


def _mix_bwd(dx1, ya, yb, yc, gg, wo_all, l, tb):
    s, d = dx1.shape

    def body(dx_ref, ya_ref, yb_ref, yc_ref, gg_ref, wo_ref, n_ref, dya_ref, dyc_ref, da_ref, dc_ref, dyb_ref, dg_ref):
        i = pl.program_id(0)

        @pl.when(i == 0)
        def _():
            dg_ref[...] = jnp.zeros_like(dg_ref)

        dn = _nt(dx_ref[...].astype(BF), wo_ref[...])
        ys = [ya_ref[...], yb_ref[...], yc_ref[...]]
        rs = [_rms_scale(v) for v in ys]
        nhat = jnp.concatenate([v * r for v, r in zip(ys, rs)], axis=1)
        gg = gg_ref[...]
        n_ref[...] = (nhat * gg).astype(BF)
        dg_ref[...] += jnp.sum(dn * nhat, axis=0, keepdims=True)
        dnh = dn * gg
        bounds = [(0, A_WIDTH), (A_WIDTH, A_WIDTH + CONV_CH), (A_WIDTH + CONV_CH, MIX_WIDTH)]
        dys = [_norm_bwd(dnh[:, lo:hi], nhat[:, lo:hi], r) for (lo, hi), r in zip(bounds, rs)]
        dyb_ref[...] = dys[1]
        for dy, y, dy_ref, dd_ref in ((dys[0], ys[0], dya_ref, da_ref), (dys[2], ys[2], dyc_ref, dc_ref)):
            dy_ref[...] = dy.astype(BF)
            t = dy * y
            for h in range(N_HEADS):
                dd_ref[:, _hs(h)] = jnp.broadcast_to(jnp.sum(t[:, _hs(h)], axis=1, keepdims=True), (tb, HEAD_DIM))

    return pl.pallas_call(
        body, grid=(s // tb,), name="mix_bwd",
        in_specs=[_rows(tb, d), _rows(tb, A_WIDTH), _rows(tb, CONV_CH), _rows(tb, A_WIDTH), _whole((1, MIX_WIDTH)),
                  _layer((MIX_WIDTH, d), l)],
        out_specs=[_rows(tb, MIX_WIDTH), _rows(tb, A_WIDTH), _rows(tb, A_WIDTH), _rows(tb, A_WIDTH),
                   _rows(tb, A_WIDTH), _rows(tb, CONV_CH), _whole((HALO, MIX_WIDTH))],
        out_shape=[jax.ShapeDtypeStruct((s, MIX_WIDTH), BF), jax.ShapeDtypeStruct((s, A_WIDTH), BF),
                   jax.ShapeDtypeStruct((s, A_WIDTH), BF), jax.ShapeDtypeStruct((s, A_WIDTH), F32),
                   jax.ShapeDtypeStruct((s, A_WIDTH), F32), jax.ShapeDtypeStruct((s, CONV_CH), F32),
                   jax.ShapeDtypeStruct((HALO, MIX_WIDTH), F32)],
        compiler_params=_cparams("arbitrary"),
    )(dx1, ya, yb, yc, gg, wo_all)


def _attn_bwd(z, dy, lse, dd, dil, kw, kcol, vcol, n_rep, max_dist, sink, name):
    s, zw = z.shape
    sub = s // dil
    zv = z.reshape(sub, dil * zw)
    have_sink = sink is not None
    n_kv = N_HEADS // n_rep

    def body(*refs):
        q_ref, kp_ref, kc_ref, vp_ref, vc_ref, dy_ref, lse_ref, dd_ref = refs[:8]
        pos = 8
        if have_sink:
            sink_ref = refs[pos]
            pos += 1
        dq_ref, dkp_ref, dkc_ref, dvp_ref, dvc_ref = refs[pos:pos + 5]
        b = pl.program_id(1)
        mask = _band_mask(b, max_dist)
        if have_sink:
            dsink_ref = refs[pos + 5]

            @pl.when(b == 0)
            def _():
                dsink_ref[...] = jnp.zeros_like(dsink_ref)

            row = lax.broadcasted_iota(jnp.int32, (HALO, 128), 0)
            lane = lax.broadcasted_iota(jnp.int32, (HALO, 128), 1)
        for kh in range(n_kv):
            k2 = jnp.concatenate([kp_ref[:, _hs(kh)], kc_ref[:, _hs(kh)]], axis=0)
            v2 = jnp.concatenate([vp_ref[:, _hs(kh)], vc_ref[:, _hs(kh)]], axis=0)
            dk2 = jnp.zeros((2 * TQ, HEAD_DIM), F32)
            dv2 = jnp.zeros((2 * TQ, HEAD_DIM), F32)
            for h in range(kh * n_rep, (kh + 1) * n_rep):
                q = q_ref[:, _hs(h)]
                lse_h = lse_ref[:, h * HEAD_DIM:h * HEAD_DIM + 1]
                dd_h = dd_ref[:, h * HEAD_DIM:h * HEAD_DIM + 1]
                dyh = dy_ref[:, _hs(h)]
                sc = jnp.where(mask, _nt(q, k2) * SCALE, NEG)
                p = jnp.exp(sc - lse_h)
                dp = _nt(dyh, v2)
                ds = ((p * (dp - dd_h)) * SCALE).astype(BF)
                dq_ref[:, _hs(h)] = _nn(ds, k2)
                dk2 = dk2 + _tn(ds, q)
                dv2 = dv2 + _tn(p.astype(BF), dyh)
                if have_sink:
                    sk = sink_ref[0:1, h:h + 1]
                    val = -jnp.sum(jnp.exp(sk - lse_h) * dd_h, axis=0, keepdims=True)
                    dsink_ref[...] += jnp.where((row == 0) & (lane == h), val, 0.0)
            dkp_ref[:, _hs(kh)] = dk2[:TQ]
            dkc_ref[:, _hs(kh)] = dk2[TQ:]
            dvp_ref[:, _hs(kh)] = dv2[:TQ]
            dvc_ref[:, _hs(kh)] = dv2[TQ:]

    args = [zv] * 5 + [a.reshape(sub, dil * A_WIDTH) for a in (dy, lse, dd)]
    in_specs = _attn_specs(dil, zw, kw, kcol, vcol) + [_head_spec()] * 3
    out_specs = [_head_spec()] + [_head_spec(kw)] * 4
    out_shape = [jax.ShapeDtypeStruct((sub, dil * A_WIDTH), F32)] + [jax.ShapeDtypeStruct((sub, dil * kw), F32)] * 4
    if have_sink:
        args.append(sink)
        in_specs.append(_whole((HALO, 128)))
        out_specs.append(_whole((HALO, 128)))
        out_shape.append(jax.ShapeDtypeStruct((HALO, 128), F32))
    res = pl.pallas_call(
        body, grid=(dil, sub // TQ), name=name, in_specs=in_specs, out_specs=out_specs, out_shape=out_shape,
        compiler_params=_cparams("arbitrary", "arbitrary"),
    )(*args)
    outs = [res[0].reshape(s, A_WIDTH)] + [a.reshape(s, kw) for a in res[1:5]]
    return outs + list(res[5:])


def _dz_assemble(parts_a, parts_c, dyb, zb, cw):
    s = zb.shape[0]
    nb = s // TQ

    def shifted(w, dil):
        return pl.BlockSpec((TQ, w), lambda i: (jnp.minimum(i + dil, nb - 1), 0))

    args, in_specs = [], []
    for dil, (dq, dkp, dkc, dvp, dvc) in zip(DILATIONS + (1,), parts_a + [parts_c]):
        w = dkp.shape[1]
        args += [dq, dkp, dkc, dvp, dvc]
        in_specs += [_rows(TQ, A_WIDTH), shifted(w, dil), _rows(TQ, w), shifted(w, dil), _rows(TQ, w)]
    args += [dyb, dyb, zb, zb, zb, cw]
    in_specs += [_rows(TQ, CONV_CH), _next_halo(TQ, CONV_CH, nb), _rows(TQ, ZB_W), _prev_halo(TQ, ZB_W),
                 _next_halo(TQ, ZB_W, nb), _whole((HALO, CONV_CH))]
    n_att = 20

    def body(*refs):
        att = refs[:n_att]
        dyb_ref, dybn_ref, zb_ref, zbp_ref, zbn_ref, cw_ref, dz_ref, dcw_ref = refs[n_att:]
        i = pl.program_id(0)

        @pl.when(i == 0)
        def _():
            dcw_ref[...] = jnp.zeros_like(dcw_ref)

        dq = jnp.zeros((TQ, A_WIDTH), F32)
        dk = jnp.zeros((TQ, A_WIDTH), F32)
        dv = jnp.zeros((TQ, A_WIDTH), F32)
        for p, dil in enumerate(DILATIONS):
            dq_r, dkp_r, dkc_r, dvp_r, dvc_r = att[5 * p:5 * p + 5]
            live = i + dil < nb
            dq = dq + dq_r[...]
            dk = dk + dkc_r[...] + jnp.where(live, dkp_r[...], 0.0)
            dv = dv + dvc_r[...] + jnp.where(live, dvp_r[...], 0.0)
        dz_ref[:, 0:A_WIDTH] = dq.astype(BF)
        dz_ref[:, A_WIDTH:2 * A_WIDTH] = dk.astype(BF)
        dz_ref[:, 2 * A_WIDTH:ZA_W] = dv.astype(BF)
        dq_r, dkp_r, dkc_r, dvp_r, dvc_r = att[15:20]
        live = i + 1 < nb
        c0 = ZA_W + ZB_W
        dz_ref[:, c0:c0 + A_WIDTH] = dq_r[...].astype(BF)
        dz_ref[:, c0 + A_WIDTH:c0 + A_WIDTH + C_KV_WIDTH] = (dkc_r[...] + jnp.where(live, dkp_r[...], 0.0)).astype(BF)
        dz_ref[:, c0 + A_WIDTH + C_KV_WIDTH:IN_WIDTH] = (dvc_r[...] + jnp.where(live, dvp_r[...], 0.0)).astype(BF)

        cw = cw_ref[...]
        zbp = jnp.where(i > 0, zbp_ref[...], 0.0)
        gb, gc, xb, u, u1, u2, c = _conv_parts(zb_ref[...], zbp, cw)
        dyb = dyb_ref[...]
        dcv = dyb * gb
        dcn = jnp.where(i + 1 < nb, dybn_ref[...] * zbn_ref[:, :CONV_CH], 0.0)
        du = cw[2:3, :] * dcv + cw[1:2, :] * _shift_up(dcv, 1, dcn) + cw[0:1, :] * _shift_up(dcv, 2, dcn)
        dz_ref[:, ZA_W:ZA_W + CONV_CH] = (dyb * c).astype(BF)
        dz_ref[:, ZA_W + CONV_CH:ZA_W + 2 * CONV_CH] = (du * xb).astype(BF)
        dz_ref[:, ZA_W + 2 * CONV_CH:c0] = (du * gc).astype(BF)
        row = lax.broadcasted_iota(jnp.int32, (HALO, CONV_CH), 0)
        upd = jnp.zeros((HALO, CONV_CH), F32)
        for t, uu in enumerate((u2, u1, u)):
            upd = jnp.where(row == t, jnp.sum(dcv * uu, axis=0, keepdims=True), upd)
        dcw_ref[...] += upd

    return pl.pallas_call(
        body, grid=(nb,), name="dz_assemble", in_specs=in_specs,
        out_specs=[_rows(TQ, IN_WIDTH), _whole((HALO, CONV_CH))],
        out_shape=[jax.ShapeDtypeStruct((s, IN_WIDTH), BF), jax.ShapeDtypeStruct((HALO, CONV_CH), F32)],
        compiler_params=_cparams("arbitrary"),
    )(*args)


def _qkv_bwd(dz, dx1, x, g, w_all, l, tb):
    s, d = x.shape

    def body(dz_ref, dx1_ref, x_ref, g_ref, w_ref, dx_ref, dg_ref):
        i = pl.program_id(0)

        @pl.when(i == 0)
        def _():
            dg_ref[...] = jnp.zeros_like(dg_ref)

        dh = _nt(dz_ref[...], w_ref[...])
        xv = x_ref[...]
        r = _rms_scale(xv)
        xhat = xv * r
        dg_ref[...] += jnp.sum(dh * xhat, axis=0, keepdims=True)
        dx_ref[...] = dx1_ref[...] + _norm_bwd(dh * g_ref[...], xhat, r)

    return pl.pallas_call(
        body, grid=(s // tb,), name="qkv_bwd",
        in_specs=[_rows(tb, IN_WIDTH), _rows(tb, d), _rows(tb, d), _whole((1, d)), _layer((d, IN_WIDTH), l)],
        out_specs=[_rows(tb, d), _whole((HALO, d))],
        out_shape=[jax.ShapeDtypeStruct((s, d), F32), jax.ShapeDtypeStruct((HALO, d), F32)],
        compiler_params=_cparams("arbitrary"),
    )(dz, dx1, x, g, w_all)


def _tile_rows(rows):
    return jnp.pad(rows, ((0, HALO - rows.shape[0]), (0, 0)))


def _local_step(x, tgt, w_in, w_o, w1, w2, conv_w, sinks, g_mix, g_group, g_mlp, g_final, emit):
    s, d = x.shape
    ff = w1.shape[2]
    depth = w_in.shape[0]
    tb = min(512, s)
    tf = min(1024, ff)
    saved = []
    for l in range(depth):
        cw = _tile_rows(conv_w[l])
        sk = jnp.pad(sinks[l].reshape(1, N_HEADS), ((0, HALO - 1), (0, 128 - N_HEADS)))
        h, za, zb, zc = _qkv_fwd(x, g_mix[l][None], w_in, l, tb)
        state = None
        for p, dil in enumerate(DILATIONS):
            state = _attn_fwd(za, dil, A_WIDTH, 1, 2, 1, A_MAX_DIST, state, None, p == len(DILATIONS) - 1,
                              "attn_a_fwd_%d" % dil)
        ya, lse_a = state
        yc, lse_c = _attn_fwd(zc, 1, C_KV_WIDTH, 3, 4, C_GROUP, C_MAX_DIST, None, sk, True, "attn_c_fwd")
        x1, yb = _mix_fwd(x, ya, yc, zb, cw, g_group[l][None], w_o, l, tb)
        x2, h2, ap = _mlp_fwd(x1, g_mlp[l][None], w1, w2, l, tb, tf)
        saved.append((x, h, za, zb, zc, ya, lse_a, yc, lse_c, yb, x1, h2, ap, cw, sk))
        x = x2
    dx, loss_tile, dg_final = _loss_head(x, g_final[None], tgt, tb)
    grads = [None] * depth
    tok = jnp.zeros((), F32)
    for l in reversed(range(depth)):
        x0, h, za, zb, zc, ya, lse_a, yc, lse_c, yb, x1, h2, ap, cw, sk = saved[l]
        dx1, dap, dg_mlp = _mlp_bwd(dx, x1, ap, g_mlp[l][None] + tok, w1, w2, l, tb, tf)
        tok = emit(l, 3, _wgrad(ap, dx, min(1024, ff), d, tb, "wgrad_ff_out", relu2=True))
        tok = tok + emit(l, 2, _wgrad(h2, dap, d, min(1024, ff), tb, "wgrad_ff_in"))
        n, dya, dyc, dd_a, dd_c, dyb, dg_group = _mix_bwd(dx1, ya, yb, yc, g_group[l][None] + tok, w_o, l, tb)
        tok = emit(l, 1, _wgrad(n, dx1, MIX_WIDTH, d, tb, "wgrad_o"))
        cw = cw + tok
        parts_a = [_attn_bwd(za, dya, lse_a, dd_a, dil, A_WIDTH, 1, 2, 1, A_MAX_DIST, None, "attn_a_bwd_%d" % dil)
                   for dil in DILATIONS]
        *parts_c, dsink = _attn_bwd(zc, dyc, lse_c, dd_c, 1, C_KV_WIDTH, 3, 4, C_GROUP, C_MAX_DIST, sk, "attn_c_bwd")
        dz, dcw = _dz_assemble(parts_a, parts_c, dyb, zb, cw)
        dx, dg_mix = _qkv_bwd(dz, dx1, x0, g_mix[l][None], w_in, l, tb)
        tok = emit(l, 0, _wgrad(h, dz, d, IN_WIDTH // 4, tb, "wgrad_in"))
        grads[l] = (dcw, dsink, dg_mix, dg_group, dg_mlp)
    return loss_tile, dx, grads, dg_final


ANY = pl.BlockSpec(memory_space=pl.ANY)
SHARD_AXES = (2, 1, 2, 1)
N_BIG = len(SHARD_AXES)
N_CHIPS = 4
N_DEV = 8


def _mesh_pos():
    return lax.axis_index("x"), lax.axis_index("y"), lax.axis_index("c")


def _flip(v, bit):
    return 1 - v if bit else v


def _shard_of(ref, w, layer, chip, n):
    start = pl.multiple_of(chip * n, 128)
    if SHARD_AXES[w] == 2:
        return ref.at[layer, :, pl.ds(start, n)]
    return ref.at[layer, pl.ds(start, n), :]


def _place_shard(shard, w, chip_arr, name):
    _, rows, cols = shard.shape
    tr = min(256, rows)
    nr = rows // tr
    if SHARD_AXES[w] == 2:
        full = (2, rows, cols * N_CHIPS)
        out_map = lambda l, i, chip: (l, i, chip[0])
    else:
        full = (2, rows * N_CHIPS, cols)
        out_map = lambda l, i, chip: (l, chip[0] * nr + i, 0)

    def body(chip_ref, x_ref, o_ref):
        o_ref[...] = x_ref[...].astype(BF)

    return pl.pallas_call(
        body, name=name,
        grid_spec=pltpu.PrefetchScalarGridSpec(
            num_scalar_prefetch=1, grid=(2, nr),
            in_specs=[pl.BlockSpec((None, tr, cols), lambda l, i, chip: (l, i, 0))],
            out_specs=pl.BlockSpec((None, tr, cols), out_map)),
        out_shape=jax.ShapeDtypeStruct(full, BF), compiler_params=_cparams("parallel", "parallel"),
    )(chip_arr, shard)


def _gather_weights(placed, conv_tile):
    widths = [p.shape[ax] // N_CHIPS for p, ax in zip(placed, SHARD_AXES)]
    fwd0 = 3 * N_BIG + 3

    def body(*refs):
        conv_src = refs[N_BIG]
        dsts, conv_dst = refs[N_BIG + 1:2 * N_BIG + 1], refs[2 * N_BIG + 1]
        send_sems, recv_sems, local_sem = refs[2 * N_BIG + 2:]
        x, y, c = _mesh_pos()
        me = 2 * x + y
        sibling = (x, y, 1 - c)
        chips = [(1 - x, y), (x, 1 - y), (1 - x, 1 - y)]

        def remote(src, dst, k, to):
            return pltpu.make_async_remote_copy(src_ref=src, dst_ref=dst, send_sem=send_sems.at[k],
                                                recv_sem=recv_sems.at[k], device_id=to, device_id_type=MESH)

        local = pltpu.make_async_copy(conv_src, conv_dst.at[me], local_sem)
        local.start()
        sends = []
        for j, (qx, qy) in enumerate(chips):
            for w in range(N_BIG):
                mine = _shard_of(dsts[w], w, c, me, widths[w])
                sends.append(remote(mine, mine, j * N_BIG + w, (qx, qy, c)))
            sends.append(remote(conv_src, conv_dst.at[me], 3 * N_BIG + j, (qx, qy, c)))
        for cp in sends:
            cp.start()
        passed = []
        for j, (qx, qy) in enumerate(chips):
            q = 2 * qx + qy
            for w in range(N_BIG):
                landed = _shard_of(dsts[w], w, c, q, widths[w])
                remote(landed, landed, j * N_BIG + w, sibling).wait_recv()
                cp = remote(landed, landed, fwd0 + j * N_BIG + w, sibling)
                cp.start()
                passed.append(cp)
            remote(conv_src, conv_dst.at[q], 3 * N_BIG + j, sibling).wait_recv()
        for j, (qx, qy) in enumerate(chips):
            q = 2 * qx + qy
            for w in range(N_BIG):
                landed = _shard_of(dsts[w], w, 1 - c, q, widths[w])
                remote(landed, landed, fwd0 + j * N_BIG + w, sibling).wait_recv()
        for cp in sends + passed:
            cp.wait_send()
        local.wait()

    n_sem = fwd0 + 3 * N_BIG
    return pl.pallas_call(
        body, name="gather_weights", in_specs=[ANY] * (N_BIG + 1), out_specs=[ANY] * (N_BIG + 1),
        out_shape=[jax.ShapeDtypeStruct(p.shape, p.dtype) for p in placed]
        + [jax.ShapeDtypeStruct((N_CHIPS,) + conv_tile.shape, conv_tile.dtype)],
        input_output_aliases={w: w for w in range(N_BIG)},
        scratch_shapes=[pltpu.SemaphoreType.DMA((n_sem,)), pltpu.SemaphoreType.DMA((n_sem,)),
                        pltpu.SemaphoreType.DMA(())],
    )(*placed, conv_tile)


HBM =pl.BlockSpec(memory_space=pltpu.HBM)
SEM = pl.BlockSpec(memory_space=pltpu.SEMAPHORE)
EFFECT = pltpu.SideEffectType.DATAFLOW_SIDE_EFFECTING


def _grad_shard(ref, w, chip, n):
    start = pl.multiple_of(chip * n, 128)
    if SHARD_AXES[w] == 2:
        return ref.at[:, pl.ds(start, n)]
    return ref.at[pl.ds(start, n), :]


def _slot_shape(g, w):
    shape = list(g.shape)
    shape[SHARD_AXES[w] - 1] //= N_CHIPS
    return (N_DEV - 1,) + tuple(shape)


def _scatter_copies(g_ref, land_ref, send_sems, recv_sems, layer, w):
    x, y, c = _mesh_pos()
    n = g_ref.shape[SHARD_AXES[w] - 1] // N_CHIPS
    out = []
    for r in range(1, N_DEV):
        tx, ty, tc = _flip(x, r & 4), _flip(y, r & 2), _flip(c, r & 1)
        cp = pltpu.make_async_remote_copy(
            src_ref=_grad_shard(g_ref, w, 2 * tx + ty, n), dst_ref=land_ref.at[r - 1], send_sem=send_sems.at[r - 1],
            recv_sem=recv_sems.at[r - 1], device_id=(tx, ty, tc), device_id_type=MESH)
        out.append((cp, (c != layer) if r & 1 else (c == layer)))
    return out


def _scatter_start(g, land, layer, w, name):
    def body(g_ref, land_ref, send_sems, recv_sems, g_thru, land_thru, token):
        for cp, mine in _scatter_copies(g_ref, land_ref, send_sems, recv_sems, layer, w):
            @pl.when(mine)
            def _():
                cp.start()
        token[...] = jnp.zeros_like(token)

    return pl.pallas_call(
        body, name=name,
        out_shape=(pltpu.SemaphoreType.DMA((N_DEV - 1,)), pltpu.SemaphoreType.DMA((N_DEV - 1,)),
                   pltpu.HBM(g.shape, g.dtype), pltpu.HBM(land.shape, land.dtype),
                   jax.ShapeDtypeStruct((HALO, 128), F32)),
        in_specs=(HBM, HBM), out_specs=(SEM, SEM, HBM, HBM, pl.BlockSpec(memory_space=pltpu.VMEM)),
        input_output_aliases={0: 2, 1: 3}, compiler_params=pltpu.CompilerParams(has_side_effects=EFFECT),
    )(pltpu.with_memory_space_constraint(g, pltpu.HBM), pltpu.with_memory_space_constraint(land, pltpu.HBM))


def _scatter_wait(started, land, after, w, name):
    def body(g0_ref, g1_ref, land_ref, ss0, rs0, ss1, rs1, after_ref, g0_out, g1_out, land_out):
        c = lax.axis_index("c")
        for layer, g_ref, ss, rs in ((0, g0_ref, ss0, rs0), (1, g1_ref, ss1, rs1)):
            for cp, mine in _scatter_copies(g_ref, land_ref, ss, rs, layer, w):
                @pl.when(mine)
                def _():
                    cp.wait_send()

                @pl.when(c == layer)
                def _():
                    cp.wait_recv()

    (ss0, rs0, g0), (ss1, rs1, g1) = started
    return pl.pallas_call(
        body, name=name,
        out_shape=(pltpu.HBM(g0.shape, g0.dtype), pltpu.HBM(g1.shape, g1.dtype), pltpu.HBM(land.shape, land.dtype)),
        in_specs=(HBM, HBM, HBM, SEM, SEM, SEM, SEM, ANY), out_specs=(HBM, HBM, HBM),
        input_output_aliases={0: 0, 1: 1, 2: 2}, compiler_params=pltpu.CompilerParams(has_side_effects=EFFECT),
    )(g0, g1, land, ss0, rs0, ss1, rs1, after)


def _sum_slots(g0, g1, slots, w, pos_arr, name):
    _, rows, cols = slots.shape
    tr = min(256, rows)
    nr = rows // tr
    if SHARD_AXES[w] == 2:
        own = pl.BlockSpec((tr, cols), lambda i, pos: (i, pos[0]))
    else:
        own = pl.BlockSpec((tr, cols), lambda i, pos: (pos[0] * nr + i, 0))

    def body(pos_ref, own0_ref, own1_ref, s_ref, o_ref):
        acc = jnp.where(pos_ref[1] == 0, own0_ref[...], own1_ref[...]).astype(F32)
        for r in range(N_DEV - 1):
            acc = acc + s_ref[r].astype(F32)
        o_ref[...] = acc

    return pl.pallas_call(
        body, name=name,
        grid_spec=pltpu.PrefetchScalarGridSpec(
            num_scalar_prefetch=1, grid=(nr,),
            in_specs=[own, own, pl.BlockSpec((N_DEV - 1, tr, cols), lambda i, pos: (0, i, 0))],
            out_specs=pl.BlockSpec((tr, cols), lambda i, pos: (i, 0))),
        out_shape=jax.ShapeDtypeStruct((rows, cols), F32), compiler_params=_cparams("parallel"),
    )(pos_arr, g0, g1, slots)


def _swap_layers(halves):
    def body(*refs):
        srcs, dsts = refs[:N_BIG], refs[N_BIG:2 * N_BIG]
        send_sems, recv_sems = refs[2 * N_BIG:]
        x, y, c = _mesh_pos()
        sends = [pltpu.make_async_remote_copy(src_ref=srcs[w], dst_ref=dsts[w], send_sem=send_sems.at[w],
                                              recv_sem=recv_sems.at[w], device_id=(x, y, 1 - c), device_id_type=MESH)
                 for w in range(N_BIG)]
        for cp in sends:
            cp.start()
        for cp in sends:
            cp.wait_recv()
        for cp in sends:
            cp.wait_send()

    return pl.pallas_call(
        body, name="swap_layers", in_specs=[ANY] * N_BIG, out_specs=[ANY] * N_BIG,
        out_shape=[jax.ShapeDtypeStruct(h.shape, h.dtype) for h in halves],
        scratch_shapes=[pltpu.SemaphoreType.DMA((N_BIG,)), pltpu.SemaphoreType.DMA((N_BIG,))],
    )(*halves)


def _adamw_math(w, g, m, v):
    m = ADAM_B1 * m + (1.0 - ADAM_B1) * g
    v = ADAM_B2 * v + (1.0 - ADAM_B2) * jnp.square(g)
    m_hat = m / (1.0 - ADAM_B1 ** ADAM_STEP)
    v_hat = v / (1.0 - ADAM_B2 ** ADAM_STEP)
    delta = -ADAM_LR * (m_hat / (jnp.sqrt(v_hat) + ADAM_EPS) + ADAM_WD * w)
    return delta, m, v


def _adamw(w, g_own, g_other, m, v, pos_arr, name):
    shape = w.shape
    _, rows, cols = shape
    tr = min(256, rows)

    def body(pos_ref, w_ref, own_ref, other_ref, m_ref, v_ref, g_ref, d_ref, m2_ref, v2_ref):
        g = jnp.where(pl.program_id(0) == pos_ref[1], own_ref[...], other_ref[...])
        g_ref[...] = g
        d_ref[...], m2_ref[...], v2_ref[...] = _adamw_math(w_ref[...], g, m_ref[...], v_ref[...])

    full = pl.BlockSpec((None, tr, cols), lambda l, i, pos: (l, i, 0))
    half = pl.BlockSpec((tr, cols), lambda l, i, pos: (i, 0))
    return pl.pallas_call(
        body, name=name,
        grid_spec=pltpu.PrefetchScalarGridSpec(
            num_scalar_prefetch=1, grid=(2, rows // tr),
            in_specs=[full, half, half, full, full], out_specs=[full] * 4),
        out_shape=[jax.ShapeDtypeStruct(shape, F32)] * 4, compiler_params=_cparams("parallel", "parallel"),
    )(pos_arr, w, g_own, g_other, m, v)


def _small_sync(part, w, m, v):
    rows, cols = part.shape

    def body(p_ref, w_ref, m_ref, v_ref, g_ref, d_ref, m2_ref, v2_ref, slots, send_sems, recv_sems):
        x, y, c = _mesh_pos()
        me = 4 * x + 2 * y + c
        slots[me] = p_ref[...]
        sends = []
        for r in range(1, N_DEV):
            to = (_flip(x, r & 4), _flip(y, r & 2), _flip(c, r & 1))
            sends.append(pltpu.make_async_remote_copy(
                src_ref=p_ref, dst_ref=slots.at[me], send_sem=send_sems.at[r - 1], recv_sem=recv_sems.at[r - 1],
                device_id=to, device_id_type=MESH))
        for cp in sends:
            cp.start()
        for cp in sends:
            cp.wait_recv()
        for cp in sends:
            cp.wait_send()
        g = slots[0]
        for i in range(1, N_DEV):
            g = g + slots[i]
        g_ref[...] = g
        d_ref[...], m2_ref[...], v2_ref[...] = _adamw_math(w_ref[...], g, m_ref[...], v_ref[...])

    vm = pl.BlockSpec(memory_space=pltpu.VMEM)
    return pl.pallas_call(
        body, name="small_sync", in_specs=[vm] * 4, out_specs=[vm] * 4,
        out_shape=[jax.ShapeDtypeStruct((rows, cols), F32)] * 4,
        scratch_shapes=[pltpu.VMEM((N_DEV, rows, cols), F32), pltpu.SemaphoreType.DMA((N_DEV - 1,)),
                        pltpu.SemaphoreType.DMA((N_DEV - 1,))],
    )(part, w, m, v)


def _pack_small(d, g_mix, g_group, g_mlp, g_final, conv_full, sinks, scalar):
    def part(rows):
        return jnp.pad(rows, ((0, HALO - rows.shape[0]), (0, d - rows.shape[1])))
    return jnp.concatenate([part(g_mix), part(g_group), part(g_mlp), part(g_final[None]),
                            part(conv_full.reshape(6, CONV_CH)), part(sinks.reshape(2, N_HEADS)),
                            part(scalar.reshape(1, 1))], axis=0)


def _unpack_small(p, dm):
    return (p[0:2, :dm], p[8:10, :MIX_WIDTH], p[16:18, :dm], p[24, :dm], p[32:38, :CONV_CH].reshape(2, 3, CONV_CH),
            p[40:42, :N_HEADS].reshape(2, 2, C_GROUP), p[48, 0])


def kernel(x, w_in, conv_w, sinks, g_mix, g_group, w_o, g_mlp, w_ff_in, w_ff_out, g_final, loss_target, m_w_in, m_conv_w, m_sinks, m_g_mix, m_g_group, m_w_o, m_g_mlp, m_w_ff_in, m_w_ff_out, m_g_final, v_w_in, v_conv_w, v_sinks, v_g_mix, v_g_group, v_w_o, v_g_mlp, v_w_ff_in, v_w_ff_out, v_g_final):
    d = max(x.shape[2], MIX_WIDTH)
    chip = 2 * lax.axis_index("x") + lax.axis_index("y")
    conv_n = conv_w.shape[2]

    pos_arr = jnp.stack([chip, lax.axis_index("c")]).astype(jnp.int32)
    placed = [_place_shard(w, i, pos_arr[:1], "place_shard_%d" % i)
              for i, w in enumerate((w_in, w_o, w_ff_in, w_ff_out))]
    conv_tile = jnp.pad(conv_w.reshape(6, conv_n), ((0, HALO - 6), (0, 128 - conv_n)))
    fw_in, fw_o, fw1, fw2, conv_all = _gather_weights(placed, conv_tile)
    conv_full = conv_all[:, :6, :conv_n].reshape(N_CHIPS, 2, 3, conv_n).transpose(1, 2, 0, 3).reshape(2, 3, CONV_CH)

    lands, started = [None] * N_BIG, {}

    def emit(layer, w, g):
        if lands[w] is None:
            lands[w] = lax.empty(_slot_shape(g, w), g.dtype)
        *started[layer, w], lands[w], token = _scatter_start(g, lands[w], layer, w, "scatter_start_%d_%d" % (layer, w))
        return token[0, 0]

    loss_tile, dx, grads, dg_final = _local_step(x[0], loss_target[0], fw_in, fw_o, fw1, fw2, conv_full,
                                                 sinks, g_mix, g_group, g_mlp, g_final, emit)

    own = []
    for w in range(N_BIG):
        g0, g1, slots = _scatter_wait((started[0, w], started[1, w]), lands[w], dx, w, "scatter_wait_%d" % w)
        own.append(_sum_slots(g0, g1, slots, w, pos_arr, "sum_slots_%d" % w))
    other = _swap_layers(own)

    def both(i):
        return jnp.stack([grads[0][i][0], grads[1][i][0]])
    dconv = jnp.stack([grads[0][0][:3], grads[1][0][:3]])
    dsinks = jnp.stack([grads[0][1][0, :N_HEADS], grads[1][1][0, :N_HEADS]])
    part = _pack_small(d, both(2), both(3), both(4), dg_final[0], dconv, dsinks, loss_tile[0, 0])

    def spread(shard):
        return lax.dynamic_update_slice(jnp.zeros((2, 3, CONV_CH), F32), shard, (0, 0, chip * conv_n))
    zero = jnp.zeros((), F32)
    packs = [_pack_small(d, a, b, c_, e, spread(f), g_, zero) for a, b, c_, e, f, g_ in (
        (g_mix, g_group, g_mlp, g_final, conv_w, sinks),
        (m_g_mix, m_g_group, m_g_mlp, m_g_final, m_conv_w, m_sinks),
        (v_g_mix, v_g_group, v_g_mlp, v_g_final, v_conv_w, v_sinks))]
    small = [_unpack_small(p, x.shape[2]) for p in _small_sync(part, *packs)]

    def shard_of(full):
        return lax.dynamic_slice(full, (0, 0, chip * conv_n), (2, 3, conv_n))
    small = [(s[0], s[1], s[2], s[3], shard_of(s[4]), s[5], s[6]) for s in small]
    loss = small[0][6]

    big = [_adamw(w, own[i], other[i], m, v, pos_arr, "adamw_%d" % i) for i, (w, m, v) in enumerate((
        (w_in, m_w_in, v_w_in), (w_o, m_w_o, v_w_o), (w_ff_in, m_w_ff_in, v_w_ff_in),
        (w_ff_out, m_w_ff_out, v_w_ff_out)))]

    def ordered(kind):
        b = [big[i][kind] for i in range(N_BIG)]
        s = small[kind]
        return [b[0], s[4], s[5], s[0], s[1], b[1], s[2], b[2], b[3], s[3]]

    return (loss, dx[None], *ordered(0), *ordered(1), *ordered(2), *ordered(3))
```

```python
import functools

import jax
import jax.numpy as jnp
from jax import lax
from jax.experimental import pallas as pl
from jax.experimental.pallas import tpu as pltpu

HEAD_DIM = 64
N_HEADS = 6
C_GROUP = 3
A_WIDTH = N_HEADS * HEAD_DIM
C_KV_WIDTH = 2 * HEAD_DIM
CONV_CH = 256
ZA_W = 3 * A_WIDTH
ZB_W = 3 * CONV_CH
ZC_W = A_WIDTH + 2 * C_KV_WIDTH
IN_WIDTH = ZA_W + ZB_W + ZC_W
MIX_WIDTH = A_WIDTH + CONV_CH + A_WIDTH
DILATIONS = (1, 4, 16)
A_MAX_DIST = 128
C_MAX_DIST = 127
TQ = 128
EPS = 1e-6
SCALE = HEAD_DIM ** -0.5
NEG = -1e30
HALO = 8

ADAM_LR = 0.001
ADAM_B1 = 0.9
ADAM_B2 = 0.999
ADAM_EPS = 1e-08
ADAM_WD = 0.01
ADAM_STEP = 10

BF = jnp.bfloat16
F32 = jnp.float32
MESH = pl.DeviceIdType.MESH
VMEM_LIMIT = 56 * 1024 * 1024


def _cparams(*sem):
    return pltpu.CompilerParams(dimension_semantics=sem, vmem_limit_bytes=VMEM_LIMIT)


def _nt(a, b):
    return lax.dot_general(a, b, (((1,), (1,)), ((), ())), preferred_element_type=F32)


def _tn(a, b):
    return lax.dot_general(a, b, (((0,), (0,)), ((), ())), preferred_element_type=F32)


def _nn(a, b):
    return jnp.dot(a, b, preferred_element_type=F32)


def _rows(tb, w):
    return pl.BlockSpec((tb, w), lambda i: (i, 0))


def _whole(shape):
    return pl.BlockSpec(shape, lambda *_: (0,) * len(shape))


def _layer(shape, l):
    return pl.BlockSpec((None,) + shape, lambda *_: (l,) + (0,) * len(shape))


def _rms_scale(v):
    return lax.rsqrt(jnp.mean(v * v, axis=-1, keepdims=True) + EPS)


def _norm_bwd(dxhat, xhat, r):
    return r * (dxhat - xhat * jnp.mean(dxhat * xhat, axis=-1, keepdims=True))


def _qkv_fwd(x, g, w_all, l, tb):
    s, d = x.shape

    def body(x_ref, g_ref, w_ref, h_ref, za_ref, zb_ref, zc_ref):
        xv = x_ref[...]
        h = ((xv * _rms_scale(xv)) * g_ref[...]).astype(BF)
        h_ref[...] = h
        z = _nn(h, w_ref[...])
        za_ref[...] = z[:, :ZA_W].astype(BF)
        zb_ref[...] = z[:, ZA_W:ZA_W + ZB_W]
        zc_ref[...] = z[:, ZA_W + ZB_W:].astype(BF)

    return pl.pallas_call(
        body, grid=(s // tb,), name="qkv_fwd",
        in_specs=[_rows(tb, d), _whole((1, d)), _layer((d, IN_WIDTH), l)],
        out_specs=[_rows(tb, d), _rows(tb, ZA_W), _rows(tb, ZB_W), _rows(tb, ZC_W)],
        out_shape=[jax.ShapeDtypeStruct((s, d), BF), jax.ShapeDtypeStruct((s, ZA_W), BF),
                   jax.ShapeDtypeStruct((s, ZB_W), F32), jax.ShapeDtypeStruct((s, ZC_W), BF)],
        compiler_params=_cparams("parallel"),
    )(x, g, w_all)


def _band_mask(b, max_dist):
    qi = lax.broadcasted_iota(jnp.int32, (TQ, 2 * TQ), 0)
    kj = lax.broadcasted_iota(jnp.int32, (TQ, 2 * TQ), 1)
    dist = TQ + qi - kj
    return (dist >= 0) & (dist <= max_dist) & ((kj >= TQ) | (b > 0))


def _attn_specs(dil, zw, kw, kcol, vcol):
    nq, nk = zw // A_WIDTH, zw // kw
    q = pl.BlockSpec((TQ, A_WIDTH), lambda r, b: (b, r * nq))
    kp = pl.BlockSpec((TQ, kw), lambda r, b: (jnp.maximum(b - 1, 0), r * nk + kcol))
    kc = pl.BlockSpec((TQ, kw), lambda r, b: (b, r * nk + kcol))
    vp = pl.BlockSpec((TQ, kw), lambda r, b: (jnp.maximum(b - 1, 0), r * nk + vcol))
    vc = pl.BlockSpec((TQ, kw), lambda r, b: (b, r * nk + vcol))
    return [q, kp, kc, vp, vc]


def _head_spec(w=A_WIDTH):
    return pl.BlockSpec((TQ, w), lambda r, b: (b, r))


def _hs(h):
    return slice(h * HEAD_DIM, (h + 1) * HEAD_DIM)


def _attn_fwd(z, dil, kw, kcol, vcol, n_rep, max_dist, state, sink, last, name):
    s, zw = z.shape
    sub = s // dil
    zv = z.reshape(sub, dil * zw)
    have_state, have_sink = state is not None, sink is not None

    def body(*refs):
        q_ref, kp_ref, kc_ref, vp_ref, vc_ref = refs[:5]
        pos = 5
        if have_state:
            acc_in, m_in, l_in = refs[pos:pos + 3]
            pos += 3
        if have_sink:
            sink_ref = refs[pos]
            pos += 1
        outs = refs[pos:]
        mask = _band_mask(pl.program_id(1), max_dist)
        for h in range(N_HEADS):
            kh = h // n_rep
            q = q_ref[:, _hs(h)]
            k2 = jnp.concatenate([kp_ref[:, _hs(kh)], kc_ref[:, _hs(kh)]], axis=0)
            v2 = jnp.concatenate([vp_ref[:, _hs(kh)], vc_ref[:, _hs(kh)]], axis=0)
            sc = jnp.where(mask, _nt(q, k2) * SCALE, NEG)
            m_new = jnp.max(sc, axis=1, keepdims=True)
            if have_sink:
                sk = sink_ref[0:1, h:h + 1]
                m_new = jnp.maximum(m_new, sk)
            if have_state:
                m_old = m_in[:, h * HEAD_DIM:h * HEAD_DIM + 1]
                m_new = jnp.maximum(m_new, m_old)
            p = jnp.exp(sc - m_new)
            l_new = jnp.sum(p, axis=1, keepdims=True)
            acc = _nn(p.astype(BF), v2)
            if have_state:
                alpha = jnp.exp(m_old - m_new)
                l_new = l_new + alpha * l_in[:, h * HEAD_DIM:h * HEAD_DIM + 1]
                acc = acc + alpha * acc_in[:, _hs(h)]
            if have_sink:
                l_new = l_new + jnp.exp(sk - m_new)
            if last:
                outs[0][:, _hs(h)] = acc / l_new
                outs[1][:, _hs(h)] = jnp.broadcast_to(m_new + jnp.log(l_new), (TQ, HEAD_DIM))
            else:
                outs[0][:, _hs(h)] = acc
                outs[1][:, _hs(h)] = jnp.broadcast_to(m_new, (TQ, HEAD_DIM))
                outs[2][:, _hs(h)] = jnp.broadcast_to(l_new, (TQ, HEAD_DIM))

    args = [zv] * 5
    in_specs = _attn_specs(dil, zw, kw, kcol, vcol)
    if have_state:
        args += [a.reshape(sub, dil * A_WIDTH) for a in state]
        in_specs += [_head_spec()] * 3
    if have_sink:
        args.append(sink)
        in_specs.append(_whole((HALO, 128)))
    n_out = 2 if last else 3
    res = pl.pallas_call(
        body, grid=(dil, sub // TQ), name=name, in_specs=in_specs,
        out_specs=[_head_spec()] * n_out,
        out_shape=[jax.ShapeDtypeStruct((sub, dil * A_WIDTH), F32)] * n_out,
        compiler_params=_cparams("parallel", "parallel"),
    )(*args)
    return [a.reshape(s, A_WIDTH) for a in res]


def _shift_down(v, n, halo):
    rows = v.shape[0]
    out = pltpu.roll(v, n, 0)
    row = lax.broadcasted_iota(jnp.int32, v.shape, 0)
    for t in range(n):
        out = jnp.where(row == t, halo[HALO - n + t:HALO - n + t + 1, :], out)
    return out


def _shift_up(v, n, halo):
    rows = v.shape[0]
    out = pltpu.roll(v, rows - n, 0)
    row = lax.broadcasted_iota(jnp.int32, v.shape, 0)
    for t in range(n):
        out = jnp.where(row == rows - n + t, halo[t:t + 1, :], out)
    return out


def _conv_parts(zb, zb_prev, cw):
    gb, gc, xb = zb[:, :CONV_CH], zb[:, CONV_CH:2 * CONV_CH], zb[:, 2 * CONV_CH:]
    u = gc * xb
    uh = zb_prev[:, CONV_CH:2 * CONV_CH] * zb_prev[:, 2 * CONV_CH:]
    u1 = _shift_down(u, 1, uh)
    u2 = _shift_down(u, 2, uh)
    c = cw[0:1, :] * u2 + cw[1:2, :] * u1 + cw[2:3, :] * u
    return gb, gc, xb, u, u1, u2, c


def _prev_halo(tb, w):
    return pl.BlockSpec((HALO, w), lambda i: (jnp.maximum(i * (tb // HALO) - 1, 0), 0))


def _next_halo(tb, w, nblk):
    return pl.BlockSpec((HALO, w), lambda i: (jnp.minimum((i + 1) * (tb // HALO), nblk * (tb // HALO) - 1), 0))


def _mix_fwd(x, ya, yc, zb, cw, gg, wo_all, l, tb):
    s, d = x.shape

    def body(x_ref, ya_ref, yc_ref, zb_ref, zbp_ref, cw_ref, gg_ref, wo_ref, x1_ref, yb_ref):
        i = pl.program_id(0)
        zbp = jnp.where(i > 0, zbp_ref[...], 0.0)
        gb, _, _, _, _, _, c = _conv_parts(zb_ref[...], zbp, cw_ref[...])
        yb = gb * c
        yb_ref[...] = yb
        ya, yc = ya_ref[...], yc_ref[...]
        n = jnp.concatenate([ya * _rms_scale(ya), yb * _rms_scale(yb), yc * _rms_scale(yc)], axis=1)
        n = (n * gg_ref[...]).astype(BF)
        x1_ref[...] = x_ref[...] + _nn(n, wo_ref[...])

    return pl.pallas_call(
        body, grid=(s // tb,), name="mix_fwd",
        in_specs=[_rows(tb, d), _rows(tb, A_WIDTH), _rows(tb, A_WIDTH), _rows(tb, ZB_W), _prev_halo(tb, ZB_W),
                  _whole((HALO, CONV_CH)), _whole((1, MIX_WIDTH)), _layer((MIX_WIDTH, d), l)],
        out_specs=[_rows(tb, d), _rows(tb, CONV_CH)],
        out_shape=[jax.ShapeDtypeStruct((s, d), F32), jax.ShapeDtypeStruct((s, CONV_CH), F32)],
        compiler_params=_cparams("parallel"),
    )(x, ya, yc, zb, zb, cw, gg, wo_all)


def _mlp_fwd(x1, g, w1_all, w2_all, l, tb, tf):
    s, d = x1.shape
    ff = w1_all.shape[2]
    nj = ff // tf

    def body(x_ref, g_ref, w1_ref, w2_ref, x2_ref, h2_ref, ap_ref, acc):
        j = pl.program_id(1)

        @pl.when(j == 0)
        def _():
            xv = x_ref[...]
            h2_ref[...] = ((xv * _rms_scale(xv)) * g_ref[...]).astype(BF)
            acc[...] = jnp.zeros_like(acc)

        ap = _nn(h2_ref[...], w1_ref[...])
        ap_ref[...] = ap.astype(BF)
        a = jnp.square(jnp.maximum(ap, 0.0)).astype(BF)
        acc[...] += _nn(a, w2_ref[...])

        @pl.when(j == nj - 1)
        def _():
            x2_ref[...] = x_ref[...] + acc[...]

    return pl.pallas_call(
        body, grid=(s // tb, nj), name="mlp_fwd",
        in_specs=[pl.BlockSpec((tb, d), lambda i, j: (i, 0)), _whole((1, d)),
                  pl.BlockSpec((None, d, tf), lambda i, j: (l, 0, j)),
                  pl.BlockSpec((None, tf, d), lambda i, j: (l, j, 0))],
        out_specs=[pl.BlockSpec((tb, d), lambda i, j: (i, 0)), pl.BlockSpec((tb, d), lambda i, j: (i, 0)),
                   pl.BlockSpec((tb, tf), lambda i, j: (i, j))],
        out_shape=[jax.ShapeDtypeStruct((s, d), F32), jax.ShapeDtypeStruct((s, d), BF),
                   jax.ShapeDtypeStruct((s, ff), BF)],
        scratch_shapes=[pltpu.VMEM((tb, d), F32)],
        compiler_params=_cparams("parallel", "arbitrary"),
    )(x1, g, w1_all, w2_all)


def _loss_head(x, g, tgt, tb):
    s, d = x.shape

    def body(x_ref, g_ref, t_ref, dx_ref, loss_ref, dg_ref):
        i = pl.program_id(0)

        @pl.when(i == 0)
        def _():
            loss_ref[...] = jnp.zeros_like(loss_ref)
            dg_ref[...] = jnp.zeros_like(dg_ref)

        xv = x_ref[...]
        r = _rms_scale(xv)
        xhat = xv * r
        err = xhat * g_ref[...] - t_ref[...]
        part = jnp.sum(jnp.mean(jnp.square(err), axis=-1, keepdims=True), axis=0, keepdims=True)
        loss_ref[...] += 0.5 * part
        dy = err * (1.0 / d)
        dg_ref[...] += jnp.sum(dy * xhat, axis=0, keepdims=True)
        dx_ref[...] = _norm_bwd(dy * g_ref[...], xhat, r)

    return pl.pallas_call(
        body, grid=(s // tb,), name="loss_head",
        in_specs=[_rows(tb, d), _whole((1, d)), _rows(tb, d)],
        out_specs=[_rows(tb, d), _whole((HALO, 128)), _whole((HALO, d))],
        out_shape=[jax.ShapeDtypeStruct((s, d), F32), jax.ShapeDtypeStruct((HALO, 128), F32),
                   jax.ShapeDtypeStruct((HALO, d), F32)],
        compiler_params=_cparams("arbitrary"),
    )(x, g, tgt)


def _mlp_bwd(dx2, x1, ap, g, w1_all, w2_all, l, tb, tf):
    s, d = x1.shape
    ff = ap.shape[1]
    nj = ff // tf

    def body(dx2_ref, x1_ref, ap_ref, g_ref, w1_ref, w2_ref, dx1_ref, dap_ref, dg_ref, acc):
        i, j = pl.program_id(0), pl.program_id(1)

        @pl.when((i == 0) & (j == 0))
        def _():
            dg_ref[...] = jnp.zeros_like(dg_ref)

        @pl.when(j == 0)
        def _():
            acc[...] = jnp.zeros_like(acc)

        da = _nt(dx2_ref[...].astype(BF), w2_ref[...])
        dap = (da * (2.0 * jnp.maximum(ap_ref[...].astype(F32), 0.0))).astype(BF)
        dap_ref[...] = dap
        acc[...] += _nt(dap, w1_ref[...])

        @pl.when(j == nj - 1)
        def _():
            xv = x1_ref[...]
            r = _rms_scale(xv)
            xhat = xv * r
            dh = acc[...]
            dg_ref[...] += jnp.sum(dh * xhat, axis=0, keepdims=True)
            dx1_ref[...] = dx2_ref[...] + _norm_bwd(dh * g_ref[...], xhat, r)

    return pl.pallas_call(
        body, grid=(s // tb, nj), name="mlp_bwd",
        in_specs=[pl.BlockSpec((tb, d), lambda i, j: (i, 0)), pl.BlockSpec((tb, d), lambda i, j: (i, 0)),
                  pl.BlockSpec((tb, tf), lambda i, j: (i, j)),
                  _whole((1, d)), pl.BlockSpec((None, d, tf), lambda i, j: (l, 0, j)),
                  pl.BlockSpec((None, tf, d), lambda i, j: (l, j, 0))],
        out_specs=[pl.BlockSpec((tb, d), lambda i, j: (i, 0)), pl.BlockSpec((tb, tf), lambda i, j: (i, j)),
                   _whole((HALO, d))],
        out_shape=[jax.ShapeDtypeStruct((s, d), F32), jax.ShapeDtypeStruct((s, ff), BF),
                   jax.ShapeDtypeStruct((HALO, d), F32)],
        scratch_shapes=[pltpu.VMEM((tb, d), F32)],
        compiler_params=_cparams("arbitrary", "arbitrary"),
    )(dx2, x1, ap, g, w1_all, w2_all)


def _wgrad(a, b, tm, tn, ts, name, relu2=False):
    s, m = a.shape
    n = b.shape[1]
    ns = s // ts

    def body(a_ref, b_ref, o_ref, acc):
        k = pl.program_id(2)

        @pl.when(k == 0)
        def _():
            acc[...] = jnp.zeros_like(acc)

        av = a_ref[...]
        if relu2:
            av = jnp.square(jnp.maximum(av.astype(F32), 0.0)).astype(BF)
        acc[...] += _tn(av, b_ref[...].astype(BF))

        @pl.when(k == ns - 1)
        def _():
            o_ref[...] = acc[...].astype(BF)

    return pl.pallas_call(
        body, grid=(m // tm, n // tn, ns), name=name,
        in_specs=[pl.BlockSpec((ts, tm), lambda i, j, k: (k, i)), pl.BlockSpec((ts, tn), lambda i, j, k: (k, j))],
        out_specs=pl.BlockSpec((tm, tn), lambda i, j, k: (i, j)),
        out_shape=jax.ShapeDtypeStruct((m, n), BF),
        scratch_shapes=[pltpu.VMEM((tm, tn), F32)],
        compiler_params=_cparams("parallel", "parallel", "arbitrary"),
    )(a, b)


def _mix_bwd(dx1, ya, yb, yc, gg, wo_all, l, tb):
    s, d = dx1.shape

    def body(dx_ref, ya_ref, yb_ref, yc_ref, gg_ref, wo_ref, n_ref, dya_ref, dyc_ref, da_ref, dc_ref, dyb_ref, dg_ref):
        i = pl.program_id(0)

        @pl.when(i == 0)
        def _():
            dg_ref[...] = jnp.zeros_like(dg_ref)

        dn = _nt(dx_ref[...].astype(BF), wo_ref[...])
        ys = [ya_ref[...], yb_ref[...], yc_ref[...]]
        rs = [_rms_scale(v) for v in ys]
        nhat = jnp.concatenate([v * r for v, r in zip(ys, rs)], axis=1)
        gg = gg_ref[...]
        n_ref[...] = (nhat * gg).astype(BF)
        dg_ref[...] += jnp.sum(dn * nhat, axis=0, keepdims=True)
        dnh = dn * gg
        bounds = [(0, A_WIDTH), (A_WIDTH, A_WIDTH + CONV_CH), (A_WIDTH + CONV_CH, MIX_WIDTH)]
        dys = [_norm_bwd(dnh[:, lo:hi], nhat[:, lo:hi], r) for (lo, hi), r in zip(bounds, rs)]
        dyb_ref[...] = dys[1]
        for dy, y, dy_ref, dd_ref in ((dys[0], ys[0], dya_ref, da_ref), (dys[2], ys[2], dyc_ref, dc_ref)):
            dy_ref[...] = dy.astype(BF)
            t = dy * y
            for h in range(N_HEADS):
                dd_ref[:, _hs(h)] = jnp.broadcast_to(jnp.sum(t[:, _hs(h)], axis=1, keepdims=True), (tb, HEAD_DIM))

    return pl.pallas_call(
        body, grid=(s // tb,), name="mix_bwd",
        in_specs=[_rows(tb, d), _rows(tb, A_WIDTH), _rows(tb, CONV_CH), _rows(tb, A_WIDTH), _whole((1, MIX_WIDTH)),
                  _layer((MIX_WIDTH, d), l)],
        out_specs=[_rows(tb, MIX_WIDTH), _rows(tb, A_WIDTH), _rows(tb, A_WIDTH), _rows(tb, A_WIDTH),
                   _rows(tb, A_WIDTH), _rows(tb, CONV_CH), _whole((HALO, MIX_WIDTH))],
        out_shape=[jax.ShapeDtypeStruct((s, MIX_WIDTH), BF), jax.ShapeDtypeStruct((s, A_WIDTH), BF),
                   jax.ShapeDtypeStruct((s, A_WIDTH), BF), jax.ShapeDtypeStruct((s, A_WIDTH), F32),
                   jax.ShapeDtypeStruct((s, A_WIDTH), F32), jax.ShapeDtypeStruct((s, CONV_CH), F32),
                   jax.ShapeDtypeStruct((HALO, MIX_WIDTH), F32)],
        compiler_params=_cparams("arbitrary"),
    )(dx1, ya, yb, yc, gg, wo_all)


def _attn_bwd(z, dy, lse, dd, dil, kw, kcol, vcol, n_rep, max_dist, sink, name):
    s, zw = z.shape
    sub = s // dil
    zv = z.reshape(sub, dil * zw)
    have_sink = sink is not None
    n_kv = N_HEADS // n_rep

    def body(*refs):
        q_ref, kp_ref, kc_ref, vp_ref, vc_ref, dy_ref, lse_ref, dd_ref = refs[:8]
        pos = 8
        if have_sink:
            sink_ref = refs[pos]
            pos += 1
        dq_ref, dkp_ref, dkc_ref, dvp_ref, dvc_ref = refs[pos:pos + 5]
        b = pl.program_id(1)
        mask = _band_mask(b, max_dist)
        if have_sink:
            dsink_ref = refs[pos + 5]

            @pl.when(b == 0)
            def _():
                dsink_ref[...] = jnp.zeros_like(dsink_ref)

            row = lax.broadcasted_iota(jnp.int32, (HALO, 128), 0)
            lane = lax.broadcasted_iota(jnp.int32, (HALO, 128), 1)
        for kh in range(n_kv):
            k2 = jnp.concatenate([kp_ref[:, _hs(kh)], kc_ref[:, _hs(kh)]], axis=0)
            v2 = jnp.concatenate([vp_ref[:, _hs(kh)], vc_ref[:, _hs(kh)]], axis=0)
            dk2 = jnp.zeros((2 * TQ, HEAD_DIM), F32)
            dv2 = jnp.zeros((2 * TQ, HEAD_DIM), F32)
            for h in range(kh * n_rep, (kh + 1) * n_rep):
                q = q_ref[:, _hs(h)]
                lse_h = lse_ref[:, h * HEAD_DIM:h * HEAD_DIM + 1]
                dd_h = dd_ref[:, h * HEAD_DIM:h * HEAD_DIM + 1]
                dyh = dy_ref[:, _hs(h)]
                sc = jnp.where(mask, _nt(q, k2) * SCALE, NEG)
                p = jnp.exp(sc - lse_h)
                dp = _nt(dyh, v2)
                ds = ((p * (dp - dd_h)) * SCALE).astype(BF)
                dq_ref[:, _hs(h)] = _nn(ds, k2)
                dk2 = dk2 + _tn(ds, q)
                dv2 = dv2 + _tn(p.astype(BF), dyh)
                if have_sink:
                    sk = sink_ref[0:1, h:h + 1]
                    val = -jnp.sum(jnp.exp(sk - lse_h) * dd_h, axis=0, keepdims=True)
                    dsink_ref[...] += jnp.where((row == 0) & (lane == h), val, 0.0)
            dkp_ref[:, _hs(kh)] = dk2[:TQ]
            dkc_ref[:, _hs(kh)] = dk2[TQ:]
            dvp_ref[:, _hs(kh)] = dv2[:TQ]
            dvc_ref[:, _hs(kh)] = dv2[TQ:]

    args = [zv] * 5 + [a.reshape(sub, dil * A_WIDTH) for a in (dy, lse, dd)]
    in_specs = _attn_specs(dil, zw, kw, kcol, vcol) + [_head_spec()] * 3
    out_specs = [_head_spec()] + [_head_spec(kw)] * 4
    out_shape = [jax.ShapeDtypeStruct((sub, dil * A_WIDTH), F32)] + [jax.ShapeDtypeStruct((sub, dil * kw), F32)] * 4
    if have_sink:
        args.append(sink)
        in_specs.append(_whole((HALO, 128)))
        out_specs.append(_whole((HALO, 128)))
        out_shape.append(jax.ShapeDtypeStruct((HALO, 128), F32))
    res = pl.pallas_call(
        body, grid=(dil, sub // TQ), name=name, in_specs=in_specs, out_specs=out_specs, out_shape=out_shape,
        compiler_params=_cparams("arbitrary", "arbitrary"),
    )(*args)
    outs = [res[0].reshape(s, A_WIDTH)] + [a.reshape(s, kw) for a in res[1:5]]
    return outs + list(res[5:])


def _dz_assemble(parts_a, parts_c, dyb, zb, cw):
    s = zb.shape[0]
    nb = s // TQ

    def shifted(w, dil):
        return pl.BlockSpec((TQ, w), lambda i: (jnp.minimum(i + dil, nb - 1), 0))

    args, in_specs = [], []
    for dil, (dq, dkp, dkc, dvp, dvc) in zip(DILATIONS + (1,), parts_a + [parts_c]):
        w = dkp.shape[1]
        args += [dq, dkp, dkc, dvp, dvc]
        in_specs += [_rows(TQ, A_WIDTH), shifted(w, dil), _rows(TQ, w), shifted(w, dil), _rows(TQ, w)]
    args += [dyb, dyb, zb, zb, zb, cw]
    in_specs += [_rows(TQ, CONV_CH), _next_halo(TQ, CONV_CH, nb), _rows(TQ, ZB_W), _prev_halo(TQ, ZB_W),
                 _next_halo(TQ, ZB_W, nb), _whole((HALO, CONV_CH))]
    n_att = 20

    def body(*refs):
        att = refs[:n_att]
        dyb_ref, dybn_ref, zb_ref, zbp_ref, zbn_ref, cw_ref, dz_ref, dcw_ref = refs[n_att:]
        i = pl.program_id(0)

        @pl.when(i == 0)
        def _():
            dcw_ref[...] = jnp.zeros_like(dcw_ref)

        dq = jnp.zeros((TQ, A_WIDTH), F32)
        dk = jnp.zeros((TQ, A_WIDTH), F32)
        dv = jnp.zeros((TQ, A_WIDTH), F32)
        for p, dil in enumerate(DILATIONS):
            dq_r, dkp_r, dkc_r, dvp_r, dvc_r = att[5 * p:5 * p + 5]
            live = i + dil < nb
            dq = dq + dq_r[...]
            dk = dk + dkc_r[...] + jnp.where(live, dkp_r[...], 0.0)
            dv = dv + dvc_r[...] + jnp.where(live, dvp_r[...], 0.0)
        dz_ref[:, 0:A_WIDTH] = dq.astype(BF)
        dz_ref[:, A_WIDTH:2 * A_WIDTH] = dk.astype(BF)
        dz_ref[:, 2 * A_WIDTH:ZA_W] = dv.astype(BF)
        dq_r, dkp_r, dkc_r, dvp_r, dvc_r = att[15:20]
        live = i + 1 < nb
        c0 = ZA_W + ZB_W
        dz_ref[:, c0:c0 + A_WIDTH] = dq_r[...].astype(BF)
        dz_ref[:, c0 + A_WIDTH:c0 + A_WIDTH + C_KV_WIDTH] = (dkc_r[...] + jnp.where(live, dkp_r[...], 0.0)).astype(BF)
        dz_ref[:, c0 + A_WIDTH + C_KV_WIDTH:IN_WIDTH] = (dvc_r[...] + jnp.where(live, dvp_r[...], 0.0)).astype(BF)

        cw = cw_ref[...]
        zbp = jnp.where(i > 0, zbp_ref[...], 0.0)
        gb, gc, xb, u, u1, u2, c = _conv_parts(zb_ref[...], zbp, cw)
        dyb = dyb_ref[...]
        dcv = dyb * gb
        dcn = jnp.where(i + 1 < nb, dybn_ref[...] * zbn_ref[:, :CONV_CH], 0.0)
        du = cw[2:3, :] * dcv + cw[1:2, :] * _shift_up(dcv, 1, dcn) + cw[0:1, :] * _shift_up(dcv, 2, dcn)
        dz_ref[:, ZA_W:ZA_W + CONV_CH] = (dyb * c).astype(BF)
        dz_ref[:, ZA_W + CONV_CH:ZA_W + 2 * CONV_CH] = (du * xb).astype(BF)
        dz_ref[:, ZA_W + 2 * CONV_CH:c0] = (du * gc).astype(BF)
        row = lax.broadcasted_iota(jnp.int32, (HALO, CONV_CH), 0)
        upd = jnp.zeros((HALO, CONV_CH), F32)
        for t, uu in enumerate((u2, u1, u)):
            upd = jnp.where(row == t, jnp.sum(dcv * uu, axis=0, keepdims=True), upd)
        dcw_ref[...] += upd

    return pl.pallas_call(
        body, grid=(nb,), name="dz_assemble", in_specs=in_specs,
        out_specs=[_rows(TQ, IN_WIDTH), _whole((HALO, CONV_CH))],
        out_shape=[jax.ShapeDtypeStruct((s, IN_WIDTH), BF), jax.ShapeDtypeStruct((HALO, CONV_CH), F32)],
        compiler_params=_cparams("arbitrary"),
    )(*args)


def _qkv_bwd(dz, dx1, x, g, w_all, l, tb):
    s, d = x.shape

    def body(dz_ref, dx1_ref, x_ref, g_ref, w_ref, dx_ref, dg_ref):
        i = pl.program_id(0)

        @pl.when(i == 0)
        def _():
            dg_ref[...] = jnp.zeros_like(dg_ref)

        dh = _nt(dz_ref[...], w_ref[...])
        xv = x_ref[...]
        r = _rms_scale(xv)
        xhat = xv * r
        dg_ref[...] += jnp.sum(dh * xhat, axis=0, keepdims=True)
        dx_ref[...] = dx1_ref[...] + _norm_bwd(dh * g_ref[...], xhat, r)

    return pl.pallas_call(
        body, grid=(s // tb,), name="qkv_bwd",
        in_specs=[_rows(tb, IN_WIDTH), _rows(tb, d), _rows(tb, d), _whole((1, d)), _layer((d, IN_WIDTH), l)],
        out_specs=[_rows(tb, d), _whole((HALO, d))],
        out_shape=[jax.ShapeDtypeStruct((s, d), F32), jax.ShapeDtypeStruct((HALO, d), F32)],
        compiler_params=_cparams("arbitrary"),
    )(dz, dx1, x, g, w_all)


def _tile_rows(rows):
    return jnp.pad(rows, ((0, HALO - rows.shape[0]), (0, 0)))


def _local_step(x, tgt, fetch, ff, sinks, g_mix, g_group, g_mlp, g_final, emit):
    s, d = x.shape
    depth = g_mix.shape[0]
    tb = min(512, s)
    tf = min(1024, ff)
    saved = []
    for l in range(depth):
        w_in, _, _, _, conv_w = fetch(0, l, x)
        cw = _tile_rows(conv_w[l])
        sk = jnp.pad(sinks[l].reshape(1, N_HEADS), ((0, HALO - 1), (0, 128 - N_HEADS)))
        h, za, zb, zc = _qkv_fwd(x, g_mix[l][None], w_in, l, tb)
        state = None
        for p, dil in enumerate(DILATIONS):
            state = _attn_fwd(za, dil, A_WIDTH, 1, 2, 1, A_MAX_DIST, state, None, p == len(DILATIONS) - 1,
                              "attn_a_fwd_%d" % dil)
        ya, lse_a = state
        yc, lse_c = _attn_fwd(zc, 1, C_KV_WIDTH, 3, 4, C_GROUP, C_MAX_DIST, None, sk, True, "attn_c_fwd")
        w_in, w_o, w1, w2, _ = fetch(1, l, yc)
        x1, yb = _mix_fwd(x, ya, yc, zb, cw, g_group[l][None], w_o, l, tb)
        x2, h2, ap = _mlp_fwd(x1, g_mlp[l][None], w1, w2, l, tb, tf)
        saved.append((x, h, za, zb, zc, ya, lse_a, yc, lse_c, yb, x1, h2, ap, cw, sk))
        x = x2
    dx, loss_tile, dg_final = _loss_head(x, g_final[None], tgt, tb)
    grads = [None] * depth
    tok = jnp.zeros((), F32)
    for l in reversed(range(depth)):
        x0, h, za, zb, zc, ya, lse_a, yc, lse_c, yb, x1, h2, ap, cw, sk = saved[l]
        dx1, dap, dg_mlp = _mlp_bwd(dx, x1, ap, g_mlp[l][None] + tok, w1, w2, l, tb, tf)
        tok = emit(l, 3, _wgrad(ap, dx, min(1024, ff), d, tb, "wgrad_ff_out", relu2=True))
        tok = tok + emit(l, 2, _wgrad(h2, dap, d, min(1024, ff), tb, "wgrad_ff_in"))
        n, dya, dyc, dd_a, dd_c, dyb, dg_group = _mix_bwd(dx1, ya, yb, yc, g_group[l][None] + tok, w_o, l, tb)
        tok = emit(l, 1, _wgrad(n, dx1, MIX_WIDTH, d, tb, "wgrad_o"))
        cw = cw + tok
        parts_a = [_attn_bwd(za, dya, lse_a, dd_a, dil, A_WIDTH, 1, 2, 1, A_MAX_DIST, None, "attn_a_bwd_%d" % dil)
                   for dil in DILATIONS]
        *parts_c, dsink = _attn_bwd(zc, dyc, lse_c, dd_c, 1, C_KV_WIDTH, 3, 4, C_GROUP, C_MAX_DIST, sk, "attn_c_bwd")
        dz, dcw = _dz_assemble(parts_a, parts_c, dyb, zb, cw)
        dx, dg_mix = _qkv_bwd(dz, dx1, x0, g_mix[l][None], w_in, l, tb)
        tok = emit(l, 0, _wgrad(h, dz, d, IN_WIDTH // 4, tb, "wgrad_in"))
        grads[l] = (dcw, dsink, dg_mix, dg_group, dg_mlp)
    return loss_tile, dx, grads, dg_final


ANY = pl.BlockSpec(memory_space=pl.ANY)
SHARD_AXES = (2, 1, 2, 1)
N_BIG = len(SHARD_AXES)
N_CHIPS = 4
N_DEV = 8


def _mesh_pos():
    return lax.axis_index("x"), lax.axis_index("y"), lax.axis_index("c")


def _flip(v, bit):
    return 1 - v if bit else v


def _shard_of(ref, w, layer, chip, n):
    start = pl.multiple_of(chip * n, 128)
    if SHARD_AXES[w] == 2:
        return ref.at[layer, :, pl.ds(start, n)]
    return ref.at[layer, pl.ds(start, n), :]


def _place_shard(shard, w, chip_arr, name):
    _, rows, cols = shard.shape
    tr = min(256, rows)
    nr = rows // tr
    if SHARD_AXES[w] == 2:
        full = (2, rows, cols * N_CHIPS)
        out_map = lambda l, i, chip: (l, i, chip[0])
    else:
        full = (2, rows * N_CHIPS, cols)
        out_map = lambda l, i, chip: (l, chip[0] * nr + i, 0)

    def body(chip_ref, x_ref, o_ref):
        o_ref[...] = x_ref[...].astype(BF)

    return pl.pallas_call(
        body, name=name,
        grid_spec=pltpu.PrefetchScalarGridSpec(
            num_scalar_prefetch=1, grid=(2, nr),
            in_specs=[pl.BlockSpec((None, tr, cols), lambda l, i, chip: (l, i, 0))],
            out_specs=pl.BlockSpec((None, tr, cols), out_map)),
        out_shape=jax.ShapeDtypeStruct(full, BF), compiler_params=_cparams("parallel", "parallel"),
    )(chip_arr, shard)


HBM = pl.BlockSpec(memory_space=pltpu.HBM)
SEM = pl.BlockSpec(memory_space=pltpu.SEMAPHORE)
EFFECT = pltpu.SideEffectType.DATAFLOW_SIDE_EFFECTING

GATHER_GROUPS = (((0, 0),), ((1, 0), (2, 0), (3, 0)), ((0, 1), (1, 1), (2, 1), (3, 1)))


def _gather_copies(arrs, group, send_sems, recv_sems):
    x, y, c = _mesh_pos()
    me = 2 * x + y
    out = []
    for i, (w, layer) in enumerate(group):
        n = arrs[w].shape[SHARD_AXES[w]] // N_CHIPS
        mine = _shard_of(arrs[w], w, layer, me, n)
        for j, (qx, qy) in enumerate([(1 - x, y), (x, 1 - y), (1 - x, 1 - y)]):
            landed = _shard_of(arrs[w], w, layer, 2 * qx + qy, n)
            out.append(tuple(pltpu.make_async_remote_copy(
                src_ref=piece, dst_ref=piece, send_sem=send_sems.at[i * 3 + j], recv_sem=recv_sems.at[i * 3 + j],
                device_id=(qx, qy, c), device_id_type=MESH) for piece in (mine, landed)))
    return out


def _conv_copies(conv_src, conv_dst, send_sems, recv_sems):
    x, y, c = _mesh_pos()
    out = []
    for j, (qx, qy) in enumerate([(1 - x, y), (x, 1 - y), (1 - x, 1 - y)]):
        out.append(tuple(pltpu.make_async_remote_copy(
            src_ref=conv_src, dst_ref=conv_dst.at[q], send_sem=send_sems.at[j], recv_sem=recv_sems.at[j],
            device_id=(qx, qy, c), device_id_type=MESH) for q in (2 * x + y, 2 * qx + qy)))
    return out


def _gather_start(placed, conv_tile, conv_land):
    n_groups = len(GATHER_GROUPS)
    n_sems = 2 * (n_groups + 1)

    def body(*refs):
        arrs, conv_src, conv_dst = refs[:N_BIG], refs[N_BIG], refs[N_BIG + 1]
        sems = refs[N_BIG + 2:N_BIG + 2 + n_sems]
        for cp, _ in _conv_copies(conv_src, conv_dst, sems[-2], sems[-1]):
            cp.start()
        for k, group in enumerate(GATHER_GROUPS):
            for cp, _ in _gather_copies(arrs, group, sems[2 * k], sems[2 * k + 1]):
                cp.start()

    sem_shapes = []
    for group in GATHER_GROUPS + (((0, 0),),):
        sem_shapes += [pltpu.SemaphoreType.DMA((3 * len(group),))] * 2
    operands = list(placed) + [conv_tile, conv_land]
    n_op = len(operands)
    res = pl.pallas_call(
        body, name="gather_start",
        out_shape=tuple(sem_shapes) + tuple(pltpu.HBM(a.shape, a.dtype) for a in operands),
        in_specs=(HBM,) * n_op, out_specs=(SEM,) * n_sems + (HBM,) * n_op,
        input_output_aliases={i: n_sems + i for i in range(n_op)},
        compiler_params=pltpu.CompilerParams(has_side_effects=EFFECT),
    )(*[pltpu.with_memory_space_constraint(a, pltpu.HBM) for a in operands])
    return res[:n_sems], list(res[n_sems:])


def _gather_wait(k, sems, arrs, conv, after, name):
    group = GATHER_GROUPS[k]
    mats = sorted({w for w, _ in group})
    n_conv = 0 if conv is None else 2

    def body(*refs):
        local = refs[:len(mats)]
        arrs_ref = [None] * N_BIG
        for w, ref in zip(mats, local):
            arrs_ref[w] = ref
        pos = len(mats) + n_conv
        copies = _gather_copies(arrs_ref, group, refs[pos], refs[pos + 1])
        if conv is not None:
            copies += _conv_copies(refs[len(mats)], refs[len(mats) + 1], refs[pos + 2], refs[pos + 3])
        for send, recv in copies:
            recv.wait_recv()
            send.wait_send()

    operands = [arrs[w] for w in mats] + ([] if conv is None else [conv[1], conv[2]])
    sem_ops = list(sems) + ([] if conv is None else list(conv[0]))
    n_op = len(operands)
    res = pl.pallas_call(
        body, name=name, out_shape=tuple(pltpu.HBM(a.shape, a.dtype) for a in operands),
        in_specs=(HBM,) * n_op + (SEM,) * len(sem_ops) + (ANY,), out_specs=(HBM,) * n_op,
        input_output_aliases={i: i for i in range(n_op)},
        compiler_params=pltpu.CompilerParams(has_side_effects=EFFECT),
    )(*operands, *sem_ops, after)
    arrs = list(arrs)
    for w, a in zip(mats, res):
        arrs[w] = a
    return arrs, (res[-1] if conv is not None else None)


def _grad_shard(ref, w, chip, n):
    start = pl.multiple_of(chip * n, 128)
    if SHARD_AXES[w] == 2:
        return ref.at[:, pl.ds(start, n)]
    return ref.at[pl.ds(start, n), :]


def _slot_shape(g, w):
    shape = list(g.shape)
    shape[SHARD_AXES[w] - 1] //= N_CHIPS
    return (N_DEV - 1,) + tuple(shape)


def _scatter_copies(g_ref, land_ref, send_sems, recv_sems, layer, w):
    x, y, c = _mesh_pos()
    n = g_ref.shape[SHARD_AXES[w] - 1] // N_CHIPS
    out = []
    for r in range(1, N_DEV):
        tx, ty, tc = _flip(x, r & 4), _flip(y, r & 2), _flip(c, r & 1)
        cp = pltpu.make_async_remote_copy(
            src_ref=_grad_shard(g_ref, w, 2 * tx + ty, n), dst_ref=land_ref.at[r - 1], send_sem=send_sems.at[r - 1],
            recv_sem=recv_sems.at[r - 1], device_id=(tx, ty, tc), device_id_type=MESH)
        out.append((cp, (c != layer) if r & 1 else (c == layer)))
    return out


def _scatter_start(g, land, layer, w, name):
    def body(g_ref, land_ref, send_sems, recv_sems, g_thru, land_thru, token):
        for cp, mine in _scatter_copies(g_ref, land_ref, send_sems, recv_sems, layer, w):
            @pl.when(mine)
            def _():
                cp.start()
        token[...] = jnp.zeros_like(token)

    return pl.pallas_call(
        body, name=name,
        out_shape=(pltpu.SemaphoreType.DMA((N_DEV - 1,)), pltpu.SemaphoreType.DMA((N_DEV - 1,)),
                   pltpu.HBM(g.shape, g.dtype), pltpu.HBM(land.shape, land.dtype),
                   jax.ShapeDtypeStruct((HALO, 128), F32)),
        in_specs=(HBM, HBM), out_specs=(SEM, SEM, HBM, HBM, pl.BlockSpec(memory_space=pltpu.VMEM)),
        input_output_aliases={0: 2, 1: 3}, compiler_params=pltpu.CompilerParams(has_side_effects=EFFECT),
    )(pltpu.with_memory_space_constraint(g, pltpu.HBM), pltpu.with_memory_space_constraint(land, pltpu.HBM))


def _scatter_wait(started, land, after, w, name):
    def body(g0_ref, g1_ref, land_ref, ss0, rs0, ss1, rs1, after_ref, g0_out, g1_out, land_out):
        c = lax.axis_index("c")
        for layer, g_ref, ss, rs in ((0, g0_ref, ss0, rs0), (1, g1_ref, ss1, rs1)):
            for cp, mine in _scatter_copies(g_ref, land_ref, ss, rs, layer, w):
                @pl.when(mine)
                def _():
                    cp.wait_send()

                @pl.when(c == layer)
                def _():
                    cp.wait_recv()

    (ss0, rs0, g0), (ss1, rs1, g1) = started
    return pl.pallas_call(
        body, name=name,
        out_shape=(pltpu.HBM(g0.shape, g0.dtype), pltpu.HBM(g1.shape, g1.dtype), pltpu.HBM(land.shape, land.dtype)),
        in_specs=(HBM, HBM, HBM, SEM, SEM, SEM, SEM, ANY), out_specs=(HBM, HBM, HBM),
        input_output_aliases={0: 0, 1: 1, 2: 2}, compiler_params=pltpu.CompilerParams(has_side_effects=EFFECT),
    )(g0, g1, land, ss0, rs0, ss1, rs1, after)


def _sum_slots(g0, g1, slots, w, pos_arr, name):
    _, rows, cols = slots.shape
    tr = min(256, rows)
    nr = rows // tr
    if SHARD_AXES[w] == 2:
        own = pl.BlockSpec((tr, cols), lambda i, pos: (i, pos[0]))
    else:
        own = pl.BlockSpec((tr, cols), lambda i, pos: (pos[0] * nr + i, 0))

    def body(pos_ref, own0_ref, own1_ref, s_ref, o_ref):
        acc = jnp.where(pos_ref[1] == 0, own0_ref[...], own1_ref[...]).astype(F32)
        for r in range(N_DEV - 1):
            acc = acc + s_ref[r].astype(F32)
        o_ref[...] = acc

    return pl.pallas_call(
        body, name=name,
        grid_spec=pltpu.PrefetchScalarGridSpec(
            num_scalar_prefetch=1, grid=(nr,),
            in_specs=[own, own, pl.BlockSpec((N_DEV - 1, tr, cols), lambda i, pos: (0, i, 0))],
            out_specs=pl.BlockSpec((tr, cols), lambda i, pos: (i, 0))),
        out_shape=jax.ShapeDtypeStruct((rows, cols), F32), compiler_params=_cparams("parallel"),
    )(pos_arr, g0, g1, slots)


def _swap_layers(halves):
    def body(*refs):
        srcs, dsts = refs[:N_BIG], refs[N_BIG:2 * N_BIG]
        send_sems, recv_sems = refs[2 * N_BIG:]
        x, y, c = _mesh_pos()
        sends = [pltpu.make_async_remote_copy(src_ref=srcs[w], dst_ref=dsts[w], send_sem=send_sems.at[w],
                                              recv_sem=recv_sems.at[w], device_id=(x, y, 1 - c), device_id_type=MESH)
                 for w in range(N_BIG)]
        for cp in sends:
            cp.start()
        for cp in sends:
            cp.wait_recv()
        for cp in sends:
            cp.wait_send()

    return pl.pallas_call(
        body, name="swap_layers", in_specs=[ANY] * N_BIG, out_specs=[ANY] * N_BIG,
        out_shape=[jax.ShapeDtypeStruct(h.shape, h.dtype) for h in halves],
        scratch_shapes=[pltpu.SemaphoreType.DMA((N_BIG,)), pltpu.SemaphoreType.DMA((N_BIG,))],
    )(*halves)


def _adamw_math(w, g, m, v):
    m = ADAM_B1 * m + (1.0 - ADAM_B1) * g
    v = ADAM_B2 * v + (1.0 - ADAM_B2) * jnp.square(g)
    m_hat = m / (1.0 - ADAM_B1 ** ADAM_STEP)
    v_hat = v / (1.0 - ADAM_B2 ** ADAM_STEP)
    delta = -ADAM_LR * (m_hat / (jnp.sqrt(v_hat) + ADAM_EPS) + ADAM_WD * w)
    return delta, m, v


def _adamw(w, g_own, g_other, m, v, pos_arr, name):
    shape = w.shape
    _, rows, cols = shape
    tr = min(256, rows)

    def body(pos_ref, w_ref, own_ref, other_ref, m_ref, v_ref, g_ref, d_ref, m2_ref, v2_ref):
        g = jnp.where(pl.program_id(0) == pos_ref[1], own_ref[...], other_ref[...])
        g_ref[...] = g
        d_ref[...], m2_ref[...], v2_ref[...] = _adamw_math(w_ref[...], g, m_ref[...], v_ref[...])

    full = pl.BlockSpec((None, tr, cols), lambda l, i, pos: (l, i, 0))
    half = pl.BlockSpec((tr, cols), lambda l, i, pos: (i, 0))
    return pl.pallas_call(
        body, name=name,
        grid_spec=pltpu.PrefetchScalarGridSpec(
            num_scalar_prefetch=1, grid=(2, rows // tr),
            in_specs=[full, half, half, full, full], out_specs=[full] * 4),
        out_shape=[jax.ShapeDtypeStruct(shape, F32)] * 4, compiler_params=_cparams("parallel", "parallel"),
    )(pos_arr, w, g_own, g_other, m, v)


def _small_sync(part, w, m, v):
    rows, cols = part.shape

    def body(p_ref, w_ref, m_ref, v_ref, g_ref, d_ref, m2_ref, v2_ref, slots, send_sems, recv_sems):
        x, y, c = _mesh_pos()
        me = 4 * x + 2 * y + c
        slots[me] = p_ref[...]
        sends = []
        for r in range(1, N_DEV):
            to = (_flip(x, r & 4), _flip(y, r & 2), _flip(c, r & 1))
            sends.append(pltpu.make_async_remote_copy(
                src_ref=p_ref, dst_ref=slots.at[me], send_sem=send_sems.at[r - 1], recv_sem=recv_sems.at[r - 1],
                device_id=to, device_id_type=MESH))
        for cp in sends:
            cp.start()
        for cp in sends:
            cp.wait_recv()
        for cp in sends:
            cp.wait_send()
        g = slots[0]
        for i in range(1, N_DEV):
            g = g + slots[i]
        g_ref[...] = g
        d_ref[...], m2_ref[...], v2_ref[...] = _adamw_math(w_ref[...], g, m_ref[...], v_ref[...])

    vm = pl.BlockSpec(memory_space=pltpu.VMEM)
    return pl.pallas_call(
        body, name="small_sync", in_specs=[vm] * 4, out_specs=[vm] * 4,
        out_shape=[jax.ShapeDtypeStruct((rows, cols), F32)] * 4,
        scratch_shapes=[pltpu.VMEM((N_DEV, rows, cols), F32), pltpu.SemaphoreType.DMA((N_DEV - 1,)),
                        pltpu.SemaphoreType.DMA((N_DEV - 1,))],
    )(part, w, m, v)


def _pack_small(d, g_mix, g_group, g_mlp, g_final, conv_full, sinks, scalar):
    def part(rows):
        return jnp.pad(rows, ((0, HALO - rows.shape[0]), (0, d - rows.shape[1])))
    return jnp.concatenate([part(g_mix), part(g_group), part(g_mlp), part(g_final[None]),
                            part(conv_full.reshape(6, CONV_CH)), part(sinks.reshape(2, N_HEADS)),
                            part(scalar.reshape(1, 1))], axis=0)


def _unpack_small(p, dm):
    return (p[0:2, :dm], p[8:10, :MIX_WIDTH], p[16:18, :dm], p[24, :dm], p[32:38, :CONV_CH].reshape(2, 3, CONV_CH),
            p[40:42, :N_HEADS].reshape(2, 2, C_GROUP), p[48, 0])


def kernel(x, w_in, conv_w, sinks, g_mix, g_group, w_o, g_mlp, w_ff_in, w_ff_out, g_final, loss_target, m_w_in, m_conv_w, m_sinks, m_g_mix, m_g_group, m_w_o, m_g_mlp, m_w_ff_in, m_w_ff_out, m_g_final, v_w_in, v_conv_w, v_sinks, v_g_mix, v_g_group, v_w_o, v_g_mlp, v_w_ff_in, v_w_ff_out, v_g_final):
    d = max(x.shape[2], MIX_WIDTH)
    chip = 2 * lax.axis_index("x") + lax.axis_index("y")
    conv_n = conv_w.shape[2]

    pos_arr = jnp.stack([chip, lax.axis_index("c")]).astype(jnp.int32)
    placed = [_place_shard(w, i, pos_arr[:1], "place_shard_%d" % i)
              for i, w in enumerate((w_in, w_o, w_ff_in, w_ff_out))]
    conv_tile = jnp.pad(conv_w.reshape(6, conv_n), ((0, HALO - 6), (0, 128 - conv_n)))
    sems, thru = _gather_start(placed, conv_tile, lax.empty((N_CHIPS,) + conv_tile.shape, conv_tile.dtype))
    full = {"arrs": thru[:N_BIG], "conv": None}

    def fetch(stage, layer, after):
        k = 2 * layer + stage
        if k == 0:
            full["arrs"], land = _gather_wait(0, sems[0:2], full["arrs"], (sems[-2:], thru[N_BIG], thru[N_BIG + 1]),
                                              after, "gather_wait_0")
            conv_all = lax.dynamic_update_slice(land, conv_tile[None], (chip, 0, 0))
            full["conv"] = conv_all[:, :6, :conv_n].reshape(N_CHIPS, 2, 3, conv_n).transpose(1, 2, 0, 3).reshape(
                2, 3, CONV_CH)
        elif k < len(GATHER_GROUPS):
            full["arrs"], _ = _gather_wait(k, sems[2 * k:2 * k + 2], full["arrs"], None, after, "gather_wait_%d" % k)
        return (*full["arrs"], full["conv"])

    lands, started = [None] * N_BIG, {}

    def emit(layer, w, g):
        if lands[w] is None:
            lands[w] = lax.empty(_slot_shape(g, w), g.dtype)
        *started[layer, w], lands[w], token = _scatter_start(g, lands[w], layer, w, "scatter_start_%d_%d" % (layer, w))
        return token[0, 0]

    loss_tile, dx, grads, dg_final = _local_step(x[0], loss_target[0], fetch, w_ff_in.shape[2] * N_CHIPS,
                                                 sinks, g_mix, g_group, g_mlp, g_final, emit)

    own = []
    for w in range(N_BIG):
        g0, g1, slots = _scatter_wait((started[0, w], started[1, w]), lands[w], dx, w, "scatter_wait_%d" % w)
        own.append(_sum_slots(g0, g1, slots, w, pos_arr, "sum_slots_%d" % w))
    other = _swap_layers(own)

    def both(i):
        return jnp.stack([grads[0][i][0], grads[1][i][0]])
    dconv = jnp.stack([grads[0][0][:3], grads[1][0][:3]])
    dsinks = jnp.stack([grads[0][1][0, :N_HEADS], grads[1][1][0, :N_HEADS]])
    part = _pack_small(d, both(2), both(3), both(4), dg_final[0], dconv, dsinks, loss_tile[0, 0])

    def spread(shard):
        return lax.dynamic_update_slice(jnp.zeros((2, 3, CONV_CH), F32), shard, (0, 0, chip * conv_n))
    zero = jnp.zeros((), F32)
    packs = [_pack_small(d, a, b, c_, e, spread(f), g_, zero) for a, b, c_, e, f, g_ in (
        (g_mix, g_group, g_mlp, g_final, conv_w, sinks),
        (m_g_mix, m_g_group, m_g_mlp, m_g_final, m_conv_w, m_sinks),
        (v_g_mix, v_g_group, v_g_mlp, v_g_final, v_conv_w, v_sinks))]
    small = [_unpack_small(p, x.shape[2]) for p in _small_sync(part, *packs)]

    def shard_of(full):
        return lax.dynamic_slice(full, (0, 0, chip * conv_n), (2, 3, conv_n))
    small = [(s[0], s[1], s[2], s[3], shard_of(s[4]), s[5], s[6]) for s in small]
    loss = small[0][6]

    big = [_adamw(w, own[i], other[i], m, v, pos_arr, "adamw_%d" % i) for i, (w, m, v) in enumerate((
        (w_in, m_w_in, v_w_in), (w_o, m_w_o, v_w_o), (w_ff_in, m_w_ff_in, v_w_ff_in),
        (w_ff_out, m_w_ff_out, v_w_ff_out)))]

    def ordered(kind):
        b = [big[i][kind] for i in range(N_BIG)]
        s = small[kind]
        return [b[0], s[4], s[5], s[0], s[1], b[1], s[2], b[2], b[3], s[3]]

    return (loss, dx[None], *ordered(0), *ordered(1), *ordered(2), *ordered(3))
```

```python
import functools

import jax
import jax.numpy as jnp
from jax import lax
from jax.experimental import pallas as pl
from jax.experimental.pallas import tpu as pltpu

HEAD_DIM = 64
N_HEADS = 6
C_GROUP = 3
A_WIDTH = N_HEADS * HEAD_DIM
C_KV_WIDTH = 2 * HEAD_DIM
CONV_CH = 256
ZA_W = 3 * A_WIDTH
ZB_W = 3 * CONV_CH
ZC_W = A_WIDTH + 2 * C_KV_WIDTH
IN_WIDTH = ZA_W + ZB_W + ZC_W
MIX_WIDTH = A_WIDTH + CONV_CH + A_WIDTH
DILATIONS = (1, 4, 16)
A_MAX_DIST = 128
C_MAX_DIST = 127
TQ = 128
EPS = 1e-6
SCALE = HEAD_DIM ** -0.5
NEG = -1e30
HALO = 8

ADAM_LR = 0.001
ADAM_B1 = 0.9
ADAM_B2 = 0.999
ADAM_EPS = 1e-08
ADAM_WD = 0.01
ADAM_STEP = 10

BF = jnp.bfloat16
F32 = jnp.float32
MESH = pl.DeviceIdType.MESH
VMEM_LIMIT = 56 * 1024 * 1024


def _cparams(*sem):
    return pltpu.CompilerParams(dimension_semantics=sem, vmem_limit_bytes=VMEM_LIMIT)


def _nt(a, b):
    return lax.dot_general(a, b, (((1,), (1,)), ((), ())), preferred_element_type=F32)


def _tn(a, b):
    return lax.dot_general(a, b, (((0,), (0,)), ((), ())), preferred_element_type=F32)


def _nn(a, b):
    return jnp.dot(a, b, preferred_element_type=F32)


def _rows(tb, w):
    return pl.BlockSpec((tb, w), lambda i: (i, 0))


def _whole(shape):
    return pl.BlockSpec(shape, lambda *_: (0,) * len(shape))


def _layer(shape, l):
    return pl.BlockSpec((None,) + shape, lambda *_: (l,) + (0,) * len(shape))


def _rms_scale(v):
    return lax.rsqrt(jnp.mean(v * v, axis=-1, keepdims=True) + EPS)


def _norm_bwd(dxhat, xhat, r):
    return r * (dxhat - xhat * jnp.mean(dxhat * xhat, axis=-1, keepdims=True))


def _qkv_fwd(x, g, w_all, l, tb):
    s, d = x.shape

    def body(x_ref, g_ref, w_ref, h_ref, za_ref, zb_ref, zc_ref):
        xv = x_ref[...]
        h = ((xv * _rms_scale(xv)) * g_ref[...]).astype(BF)
        h_ref[...] = h
        z = _nn(h, w_ref[...])
        za_ref[...] = z[:, :ZA_W].astype(BF)
        zb_ref[...] = z[:, ZA_W:ZA_W + ZB_W]
        zc_ref[...] = z[:, ZA_W + ZB_W:].astype(BF)

    return pl.pallas_call(
        body, grid=(s // tb,), name="qkv_fwd",
        in_specs=[_rows(tb, d), _whole((1, d)), _layer((d, IN_WIDTH), l)],
        out_specs=[_rows(tb, d), _rows(tb, ZA_W), _rows(tb, ZB_W), _rows(tb, ZC_W)],
        out_shape=[jax.ShapeDtypeStruct((s, d), BF), jax.ShapeDtypeStruct((s, ZA_W), BF),
                   jax.ShapeDtypeStruct((s, ZB_W), F32), jax.ShapeDtypeStruct((s, ZC_W), BF)],
        compiler_params=_cparams("parallel"),
    )(x, g, w_all)


def _band_mask(b, max_dist):
    qi = lax.broadcasted_iota(jnp.int32, (TQ, 2 * TQ), 0)
    kj = lax.broadcasted_iota(jnp.int32, (TQ, 2 * TQ), 1)
    dist = TQ + qi - kj
    return (dist >= 0) & (dist <= max_dist) & ((kj >= TQ) | (b > 0))


def _attn_specs(dil, zw, kw, kcol, vcol):
    nq, nk = zw // A_WIDTH, zw // kw
    q = pl.BlockSpec((TQ, A_WIDTH), lambda r, b: (b, r * nq))
    kp = pl.BlockSpec((TQ, kw), lambda r, b: (jnp.maximum(b - 1, 0), r * nk + kcol))
    kc = pl.BlockSpec((TQ, kw), lambda r, b: (b, r * nk + kcol))
    vp = pl.BlockSpec((TQ, kw), lambda r, b: (jnp.maximum(b - 1, 0), r * nk + vcol))
    vc = pl.BlockSpec((TQ, kw), lambda r, b: (b, r * nk + vcol))
    return [q, kp, kc, vp, vc]


def _head_spec(w=A_WIDTH):
    return pl.BlockSpec((TQ, w), lambda r, b: (b, r))


def _hs(h):
    return slice(h * HEAD_DIM, (h + 1) * HEAD_DIM)


def _attn_fwd(z, dil, kw, kcol, vcol, n_rep, max_dist, state, sink, last, name):
    s, zw = z.shape
    sub = s // dil
    zv = z.reshape(sub, dil * zw)
    have_state, have_sink = state is not None, sink is not None

    def body(*refs):
        q_ref, kp_ref, kc_ref, vp_ref, vc_ref = refs[:5]
        pos = 5
        if have_state:
            acc_in, m_in, l_in = refs[pos:pos + 3]
            pos += 3
        if have_sink:
            sink_ref = refs[pos]
            pos += 1
        outs = refs[pos:]
        mask = _band_mask(pl.program_id(1), max_dist)
        for h in range(N_HEADS):
            kh = h // n_rep
            q = q_ref[:, _hs(h)]
            k2 = jnp.concatenate([kp_ref[:, _hs(kh)], kc_ref[:, _hs(kh)]], axis=0)
            v2 = jnp.concatenate([vp_ref[:, _hs(kh)], vc_ref[:, _hs(kh)]], axis=0)
            sc = jnp.where(mask, _nt(q, k2) * SCALE, NEG)
            m_new = jnp.max(sc, axis=1, keepdims=True)
            if have_sink:
                sk = sink_ref[0:1, h:h + 1]
                m_new = jnp.maximum(m_new, sk)
            if have_state:
                m_old = m_in[:, h * HEAD_DIM:h * HEAD_DIM + 1]
                m_new = jnp.maximum(m_new, m_old)
            p = jnp.exp(sc - m_new)
            l_new = jnp.sum(p, axis=1, keepdims=True)
            acc = _nn(p.astype(BF), v2)
            if have_state:
                alpha = jnp.exp(m_old - m_new)
                l_new = l_new + alpha * l_in[:, h * HEAD_DIM:h * HEAD_DIM + 1]
                acc = acc + alpha * acc_in[:, _hs(h)]
            if have_sink:
                l_new = l_new + jnp.exp(sk - m_new)
            if last:
                outs[0][:, _hs(h)] = acc / l_new
                outs[1][:, _hs(h)] = jnp.broadcast_to(m_new + jnp.log(l_new), (TQ, HEAD_DIM))
            else:
                outs[0][:, _hs(h)] = acc
                outs[1][:, _hs(h)] = jnp.broadcast_to(m_new, (TQ, HEAD_DIM))
                outs[2][:, _hs(h)] = jnp.broadcast_to(l_new, (TQ, HEAD_DIM))

    args = [zv] * 5
    in_specs = _attn_specs(dil, zw, kw, kcol, vcol)
    if have_state:
        args += [a.reshape(sub, dil * A_WIDTH) for a in state]
        in_specs += [_head_spec()] * 3
    if have_sink:
        args.append(sink)
        in_specs.append(_whole((HALO, 128)))
    n_out = 2 if last else 3
    res = pl.pallas_call(
        body, grid=(dil, sub // TQ), name=name, in_specs=in_specs,
        out_specs=[_head_spec()] * n_out,
        out_shape=[jax.ShapeDtypeStruct((sub, dil * A_WIDTH), F32)] * n_out,
        compiler_params=_cparams("parallel", "parallel"),
    )(*args)
    return [a.reshape(s, A_WIDTH) for a in res]


def _shift_down(v, n, halo):
    rows = v.shape[0]
    out = pltpu.roll(v, n, 0)
    row = lax.broadcasted_iota(jnp.int32, v.shape, 0)
    for t in range(n):
        out = jnp.where(row == t, halo[HALO - n + t:HALO - n + t + 1, :], out)
    return out


def _shift_up(v, n, halo):
    rows = v.shape[0]
    out = pltpu.roll(v, rows - n, 0)
    row = lax.broadcasted_iota(jnp.int32, v.shape, 0)
    for t in range(n):
        out = jnp.where(row == rows - n + t, halo[t:t + 1, :], out)
    return out


def _conv_parts(zb, zb_prev, cw):
    gb, gc, xb = zb[:, :CONV_CH], zb[:, CONV_CH:2 * CONV_CH], zb[:, 2 * CONV_CH:]
    u = gc * xb
    uh = zb_prev[:, CONV_CH:2 * CONV_CH] * zb_prev[:, 2 * CONV_CH:]
    u1 = _shift_down(u, 1, uh)
    u2 = _shift_down(u, 2, uh)
    c = cw[0:1, :] * u2 + cw[1:2, :] * u1 + cw[2:3, :] * u
    return gb, gc, xb, u, u1, u2, c


def _prev_halo(tb, w):
    return pl.BlockSpec((HALO, w), lambda i: (jnp.maximum(i * (tb // HALO) - 1, 0), 0))


def _next_halo(tb, w, nblk):
    return pl.BlockSpec((HALO, w), lambda i: (jnp.minimum((i + 1) * (tb // HALO), nblk * (tb // HALO) - 1), 0))


def _mix_fwd(x, ya, yc, zb, cw, gg, wo_all, l, tb):
    s, d = x.shape

    def body(x_ref, ya_ref, yc_ref, zb_ref, zbp_ref, cw_ref, gg_ref, wo_ref, x1_ref, yb_ref):
        i = pl.program_id(0)
        zbp = jnp.where(i > 0, zbp_ref[...], 0.0)
        gb, _, _, _, _, _, c = _conv_parts(zb_ref[...], zbp, cw_ref[...])
        yb = gb * c
        yb_ref[...] = yb
        ya, yc = ya_ref[...], yc_ref[...]
        n = jnp.concatenate([ya * _rms_scale(ya), yb * _rms_scale(yb), yc * _rms_scale(yc)], axis=1)
        n = (n * gg_ref[...]).astype(BF)
        x1_ref[...] = x_ref[...] + _nn(n, wo_ref[...])

    return pl.pallas_call(
        body, grid=(s // tb,), name="mix_fwd",
        in_specs=[_rows(tb, d), _rows(tb, A_WIDTH), _rows(tb, A_WIDTH), _rows(tb, ZB_W), _prev_halo(tb, ZB_W),
                  _whole((HALO, CONV_CH)), _whole((1, MIX_WIDTH)), _layer((MIX_WIDTH, d), l)],
        out_specs=[_rows(tb, d), _rows(tb, CONV_CH)],
        out_shape=[jax.ShapeDtypeStruct((s, d), F32), jax.ShapeDtypeStruct((s, CONV_CH), F32)],
        compiler_params=_cparams("parallel"),
    )(x, ya, yc, zb, zb, cw, gg, wo_all)


def _mlp_fwd(x1, g, w1_all, w2_all, l, tb, tf):
    s, d = x1.shape
    ff = w1_all.shape[2]
    nj = ff // tf

    def body(x_ref, g_ref, w1_ref, w2_ref, x2_ref, h2_ref, ap_ref, acc):
        j = pl.program_id(1)

        @pl.when(j == 0)
        def _():
            xv = x_ref[...]
            h2_ref[...] = ((xv * _rms_scale(xv)) * g_ref[...]).astype(BF)
            acc[...] = jnp.zeros_like(acc)

        ap = _nn(h2_ref[...], w1_ref[...])
        ap_ref[...] = ap.astype(BF)
        a = jnp.square(jnp.maximum(ap, 0.0)).astype(BF)
        acc[...] += _nn(a, w2_ref[...])

        @pl.when(j == nj - 1)
        def _():
            x2_ref[...] = x_ref[...] + acc[...]

    return pl.pallas_call(
        body, grid=(s // tb, nj), name="mlp_fwd",
        in_specs=[pl.BlockSpec((tb, d), lambda i, j: (i, 0)), _whole((1, d)),
                  pl.BlockSpec((None, d, tf), lambda i, j: (l, 0, j)),
                  pl.BlockSpec((None, tf, d), lambda i, j: (l, j, 0))],
        out_specs=[pl.BlockSpec((tb, d), lambda i, j: (i, 0)), pl.BlockSpec((tb, d), lambda i, j: (i, 0)),
                   pl.BlockSpec((tb, tf), lambda i, j: (i, j))],
        out_shape=[jax.ShapeDtypeStruct((s, d), F32), jax.ShapeDtypeStruct((s, d), BF),
                   jax.ShapeDtypeStruct((s, ff), BF)],
        scratch_shapes=[pltpu.VMEM((tb, d), F32)],
        compiler_params=_cparams("parallel", "arbitrary"),
    )(x1, g, w1_all, w2_all)


def _loss_head(x, g, tgt, tb):
    s, d = x.shape

    def body(x_ref, g_ref, t_ref, dx_ref, loss_ref, dg_ref):
        i = pl.program_id(0)

        @pl.when(i == 0)
        def _():
            loss_ref[...] = jnp.zeros_like(loss_ref)
            dg_ref[...] = jnp.zeros_like(dg_ref)

        xv = x_ref[...]
        r = _rms_scale(xv)
        xhat = xv * r
        err = xhat * g_ref[...] - t_ref[...]
        part = jnp.sum(jnp.mean(jnp.square(err), axis=-1, keepdims=True), axis=0, keepdims=True)
        loss_ref[...] += 0.5 * part
        dy = err * (1.0 / d)
        dg_ref[...] += jnp.sum(dy * xhat, axis=0, keepdims=True)
        dx_ref[...] = _norm_bwd(dy * g_ref[...], xhat, r)

    return pl.pallas_call(
        body, grid=(s // tb,), name="loss_head",
        in_specs=[_rows(tb, d), _whole((1, d)), _rows(tb, d)],
        out_specs=[_rows(tb, d), _whole((HALO, 128)), _whole((HALO, d))],
        out_shape=[jax.ShapeDtypeStruct((s, d), F32), jax.ShapeDtypeStruct((HALO, 128), F32),
                   jax.ShapeDtypeStruct((HALO, d), F32)],
        compiler_params=_cparams("arbitrary"),
    )(x, g, tgt)


def _mlp_bwd(dx2, x1, ap, g, w1_all, w2_all, l, tb, tf):
    s, d = x1.shape
    ff = ap.shape[1]
    nj = ff // tf

    def body(dx2_ref, x1_ref, ap_ref, g_ref, w1_ref, w2_ref, dx1_ref, dap_ref, dg_ref, acc):
        i, j = pl.program_id(0), pl.program_id(1)

        @pl.when((i == 0) & (j == 0))
        def _():
            dg_ref[...] = jnp.zeros_like(dg_ref)

        @pl.when(j == 0)
        def _():
            acc[...] = jnp.zeros_like(acc)

        da = _nt(dx2_ref[...].astype(BF), w2_ref[...])
        dap = (da * (2.0 * jnp.maximum(ap_ref[...].astype(F32), 0.0))).astype(BF)
        dap_ref[...] = dap
        acc[...] += _nt(dap, w1_ref[...])

        @pl.when(j == nj - 1)
        def _():
            xv = x1_ref[...]
            r = _rms_scale(xv)
            xhat = xv * r
            dh = acc[...]
            dg_ref[...] += jnp.sum(dh * xhat, axis=0, keepdims=True)
            dx1_ref[...] = dx2_ref[...] + _norm_bwd(dh * g_ref[...], xhat, r)

    return pl.pallas_call(
        body, grid=(s // tb, nj), name="mlp_bwd",
        in_specs=[pl.BlockSpec((tb, d), lambda i, j: (i, 0)), pl.BlockSpec((tb, d), lambda i, j: (i, 0)),
                  pl.BlockSpec((tb, tf), lambda i, j: (i, j)),
                  _whole((1, d)), pl.BlockSpec((None, d, tf), lambda i, j: (l, 0, j)),
                  pl.BlockSpec((None, tf, d), lambda i, j: (l, j, 0))],
        out_specs=[pl.BlockSpec((tb, d), lambda i, j: (i, 0)), pl.BlockSpec((tb, tf), lambda i, j: (i, j)),
                   _whole((HALO, d))],
        out_shape=[jax.ShapeDtypeStruct((s, d), F32), jax.ShapeDtypeStruct((s, ff), BF),
                   jax.ShapeDtypeStruct((HALO, d), F32)],
        scratch_shapes=[pltpu.VMEM((tb, d), F32)],
        compiler_params=_cparams("arbitrary", "arbitrary"),
    )(dx2, x1, ap, g, w1_all, w2_all)


def _wgrad(a, b, tm, tn, ts, name, relu2=False):
    s, m = a.shape
    n = b.shape[1]
    ns = s // ts

    def body(a_ref, b_ref, o_ref, acc):
        k = pl.program_id(2)

        @pl.when(k == 0)
        def _():
            acc[...] = jnp.zeros_like(acc)

        av = a_ref[...]
        if relu2:
            av = jnp.square(jnp.maximum(av.astype(F32), 0.0)).astype(BF)
        acc[...] += _tn(av, b_ref[...].astype(BF))

        @pl.when(k == ns - 1)
        def _():
            o_ref[...] = acc[...].astype(BF)

    return pl.pallas_call(
        body, grid=(m // tm, n // tn, ns), name=name,
        in_specs=[pl.BlockSpec((ts, tm), lambda i, j, k: (k, i)), pl.BlockSpec((ts, tn), lambda i, j, k: (k, j))],
        out_specs=pl.BlockSpec((tm, tn), lambda i, j, k: (i, j)),
        out_shape=jax.ShapeDtypeStruct((m, n), BF),
        scratch_shapes=[pltpu.VMEM((tm, tn), F32)],
        compiler_params=_cparams("parallel", "parallel", "arbitrary"),
    )(a, b)


def _mix_bwd(dx1, ya, yb, yc, gg, wo_all, l, tb):
    s, d = dx1.shape

    def body(dx_ref, ya_ref, yb_ref, yc_ref, gg_ref, wo_ref, n_ref, dya_ref, dyc_ref, da_ref, dc_ref, dyb_ref, dg_ref):
        i = pl.program_id(0)

        @pl.when(i == 0)
        def _():
            dg_ref[...] = jnp.zeros_like(dg_ref)

        dn = _nt(dx_ref[...].astype(BF), wo_ref[...])
        ys = [ya_ref[...], yb_ref[...], yc_ref[...]]
        rs = [_rms_scale(v) for v in ys]
        nhat = jnp.concatenate([v * r for v, r in zip(ys, rs)], axis=1)
        gg = gg_ref[...]
        n_ref[...] = (nhat * gg).astype(BF)
        dg_ref[...] += jnp.sum(dn * nhat, axis=0, keepdims=True)
        dnh = dn * gg
        bounds = [(0, A_WIDTH), (A_WIDTH, A_WIDTH + CONV_CH), (A_WIDTH + CONV_CH, MIX_WIDTH)]
        dys = [_norm_bwd(dnh[:, lo:hi], nhat[:, lo:hi], r) for (lo, hi), r in zip(bounds, rs)]
        dyb_ref[...] = dys[1]
        for dy, y, dy_ref, dd_ref in ((dys[0], ys[0], dya_ref, da_ref), (dys[2], ys[2], dyc_ref, dc_ref)):
            dy_ref[...] = dy.astype(BF)
            t = dy * y
            for h in range(N_HEADS):
                dd_ref[:, _hs(h)] = jnp.broadcast_to(jnp.sum(t[:, _hs(h)], axis=1, keepdims=True), (tb, HEAD_DIM))

    return pl.pallas_call(
        body, grid=(s // tb,), name="mix_bwd",
        in_specs=[_rows(tb, d), _rows(tb, A_WIDTH), _rows(tb, CONV_CH), _rows(tb, A_WIDTH), _whole((1, MIX_WIDTH)),
                  _layer((MIX_WIDTH, d), l)],
        out_specs=[_rows(tb, MIX_WIDTH), _rows(tb, A_WIDTH), _rows(tb, A_WIDTH), _rows(tb, A_WIDTH),
                   _rows(tb, A_WIDTH), _rows(tb, CONV_CH), _whole((HALO, MIX_WIDTH))],
        out_shape=[jax.ShapeDtypeStruct((s, MIX_WIDTH), BF), jax.ShapeDtypeStruct((s, A_WIDTH), BF),
                   jax.ShapeDtypeStruct((s, A_WIDTH), BF), jax.ShapeDtypeStruct((s, A_WIDTH), F32),
                   jax.ShapeDtypeStruct((s, A_WIDTH), F32), jax.ShapeDtypeStruct((s, CONV_CH), F32),
                   jax.ShapeDtypeStruct((HALO, MIX_WIDTH), F32)],
        compiler_params=_cparams("arbitrary"),
    )(dx1, ya, yb, yc, gg, wo_all)


def _attn_bwd(z, dy, lse, dd, dil, kw, kcol, vcol, n_rep, max_dist, sink, name):
    s, zw = z.shape
    sub = s // dil
    zv = z.reshape(sub, dil * zw)
    have_sink = sink is not None
    n_kv = N_HEADS // n_rep

    def body(*refs):
        q_ref, kp_ref, kc_ref, vp_ref, vc_ref, dy_ref, lse_ref, dd_ref = refs[:8]
        pos = 8
        if have_sink:
            sink_ref = refs[pos]
            pos += 1
        dq_ref, dkp_ref, dkc_ref, dvp_ref, dvc_ref = refs[pos:pos + 5]
        b = pl.program_id(1)
        mask = _band_mask(b, max_dist)
        if have_sink:
            dsink_ref = refs[pos + 5]

            @pl.when(b == 0)
            def _():
                dsink_ref[...] = jnp.zeros_like(dsink_ref)

            row = lax.broadcasted_iota(jnp.int32, (HALO, 128), 0)
            lane = lax.broadcasted_iota(jnp.int32, (HALO, 128), 1)
        for kh in range(n_kv):
            k2 = jnp.concatenate([kp_ref[:, _hs(kh)], kc_ref[:, _hs(kh)]], axis=0)
            v2 = jnp.concatenate([vp_ref[:, _hs(kh)], vc_ref[:, _hs(kh)]], axis=0)
            dk2 = jnp.zeros((2 * TQ, HEAD_DIM), F32)
            dv2 = jnp.zeros((2 * TQ, HEAD_DIM), F32)
            for h in range(kh * n_rep, (kh + 1) * n_rep):
                q = q_ref[:, _hs(h)]
                lse_h = lse_ref[:, h * HEAD_DIM:h * HEAD_DIM + 1]
                dd_h = dd_ref[:, h * HEAD_DIM:h * HEAD_DIM + 1]
                dyh = dy_ref[:, _hs(h)]
                sc = jnp.where(mask, _nt(q, k2) * SCALE, NEG)
                p = jnp.exp(sc - lse_h)
                dp = _nt(dyh, v2)
                ds = ((p * (dp - dd_h)) * SCALE).astype(BF)
                dq_ref[:, _hs(h)] = _nn(ds, k2)
                dk2 = dk2 + _tn(ds, q)
                dv2 = dv2 + _tn(p.astype(BF), dyh)
                if have_sink:
                    sk = sink_ref[0:1, h:h + 1]
                    val = -jnp.sum(jnp.exp(sk - lse_h) * dd_h, axis=0, keepdims=True)
                    dsink_ref[...] += jnp.where((row == 0) & (lane == h), val, 0.0)
            dkp_ref[:, _hs(kh)] = dk2[:TQ]
            dkc_ref[:, _hs(kh)] = dk2[TQ:]
            dvp_ref[:, _hs(kh)] = dv2[:TQ]
            dvc_ref[:, _hs(kh)] = dv2[TQ:]

    args = [zv] * 5 + [a.reshape(sub, dil * A_WIDTH) for a in (dy, lse, dd)]
    in_specs = _attn_specs(dil, zw, kw, kcol, vcol) + [_head_spec()] * 3
    out_specs = [_head_spec()] + [_head_spec(kw)] * 4
    out_shape = [jax.ShapeDtypeStruct((sub, dil * A_WIDTH), F32)] + [jax.ShapeDtypeStruct((sub, dil * kw), F32)] * 4
    if have_sink:
        args.append(sink)
        in_specs.append(_whole((HALO, 128)))
        out_specs.append(_whole((HALO, 128)))
        out_shape.append(jax.ShapeDtypeStruct((HALO, 128), F32))
    res = pl.pallas_call(
        body, grid=(dil, sub // TQ), name=name, in_specs=in_specs, out_specs=out_specs, out_shape=out_shape,
        compiler_params=_cparams("arbitrary", "arbitrary"),
    )(*args)
    outs = [res[0].reshape(s, A_WIDTH)] + [a.reshape(s, kw) for a in res[1:5]]
    return outs + list(res[5:])


def _dz_assemble(parts_a, parts_c, dyb, zb, cw):
    s = zb.shape[0]
    nb = s // TQ

    def shifted(w, dil):
        return pl.BlockSpec((TQ, w), lambda i: (jnp.minimum(i + dil, nb - 1), 0))

    args, in_specs = [], []
    for dil, (dq, dkp, dkc, dvp, dvc) in zip(DILATIONS + (1,), parts_a + [parts_c]):
        w = dkp.shape[1]
        args += [dq, dkp, dkc, dvp, dvc]
        in_specs += [_rows(TQ, A_WIDTH), shifted(w, dil), _rows(TQ, w), shifted(w, dil), _rows(TQ, w)]
    args += [dyb, dyb, zb, zb, zb, cw]
    in_specs += [_rows(TQ, CONV_CH), _next_halo(TQ, CONV_CH, nb), _rows(TQ, ZB_W), _prev_halo(TQ, ZB_W),
                 _next_halo(TQ, ZB_W, nb), _whole((HALO, CONV_CH))]
    n_att = 20

    def body(*refs):
        att = refs[:n_att]
        dyb_ref, dybn_ref, zb_ref, zbp_ref, zbn_ref, cw_ref, dz_ref, dcw_ref = refs[n_att:]
        i = pl.program_id(0)

        @pl.when(i == 0)
        def _():
            dcw_ref[...] = jnp.zeros_like(dcw_ref)

        dq = jnp.zeros((TQ, A_WIDTH), F32)
        dk = jnp.zeros((TQ, A_WIDTH), F32)
        dv = jnp.zeros((TQ, A_WIDTH), F32)
        for p, dil in enumerate(DILATIONS):
            dq_r, dkp_r, dkc_r, dvp_r, dvc_r = att[5 * p:5 * p + 5]
            live = i + dil < nb
            dq = dq + dq_r[...]
            dk = dk + dkc_r[...] + jnp.where(live, dkp_r[...], 0.0)
            dv = dv + dvc_r[...] + jnp.where(live, dvp_r[...], 0.0)
        dz_ref[:, 0:A_WIDTH] = dq.astype(BF)
        dz_ref[:, A_WIDTH:2 * A_WIDTH] = dk.astype(BF)
        dz_ref[:, 2 * A_WIDTH:ZA_W] = dv.astype(BF)
        dq_r, dkp_r, dkc_r, dvp_r, dvc_r = att[15:20]
        live = i + 1 < nb
        c0 = ZA_W + ZB_W
        dz_ref[:, c0:c0 + A_WIDTH] = dq_r[...].astype(BF)
        dz_ref[:, c0 + A_WIDTH:c0 + A_WIDTH + C_KV_WIDTH] = (dkc_r[...] + jnp.where(live, dkp_r[...], 0.0)).astype(BF)
        dz_ref[:, c0 + A_WIDTH + C_KV_WIDTH:IN_WIDTH] = (dvc_r[...] + jnp.where(live, dvp_r[...], 0.0)).astype(BF)

        cw = cw_ref[...]
        zbp = jnp.where(i > 0, zbp_ref[...], 0.0)
        gb, gc, xb, u, u1, u2, c = _conv_parts(zb_ref[...], zbp, cw)
        dyb = dyb_ref[...]
        dcv = dyb * gb
        dcn = jnp.where(i + 1 < nb, dybn_ref[...] * zbn_ref[:, :CONV_CH], 0.0)
        du = cw[2:3, :] * dcv + cw[1:2, :] * _shift_up(dcv, 1, dcn) + cw[0:1, :] * _shift_up(dcv, 2, dcn)
        dz_ref[:, ZA_W:ZA_W + CONV_CH] = (dyb * c).astype(BF)
        dz_ref[:, ZA_W + CONV_CH:ZA_W + 2 * CONV_CH] = (du * xb).astype(BF)
        dz_ref[:, ZA_W + 2 * CONV_CH:c0] = (du * gc).astype(BF)
        row = lax.broadcasted_iota(jnp.int32, (HALO, CONV_CH), 0)
        upd = jnp.zeros((HALO, CONV_CH), F32)
        for t, uu in enumerate((u2, u1, u)):
            upd = jnp.where(row == t, jnp.sum(dcv * uu, axis=0, keepdims=True), upd)
        dcw_ref[...] += upd

    return pl.pallas_call(
        body, grid=(nb,), name="dz_assemble", in_specs=in_specs,
        out_specs=[_rows(TQ, IN_WIDTH), _whole((HALO, CONV_CH))],
        out_shape=[jax.ShapeDtypeStruct((s, IN_WIDTH), BF), jax.ShapeDtypeStruct((HALO, CONV_CH), F32)],
        compiler_params=_cparams("arbitrary"),
    )(*args)


def _qkv_bwd(dz, dx1, x, g, w_all, l, tb):
    s, d = x.shape

    def body(dz_ref, dx1_ref, x_ref, g_ref, w_ref, dx_ref, dg_ref):
        i = pl.program_id(0)

        @pl.when(i == 0)
        def _():
            dg_ref[...] = jnp.zeros_like(dg_ref)

        dh = _nt(dz_ref[...], w_ref[...])
        xv = x_ref[...]
        r = _rms_scale(xv)
        xhat = xv * r
        dg_ref[...] += jnp.sum(dh * xhat, axis=0, keepdims=True)
        dx_ref[...] = dx1_ref[...] + _norm_bwd(dh * g_ref[...], xhat, r)

    return pl.pallas_call(
        body, grid=(s // tb,), name="qkv_bwd",
        in_specs=[_rows(tb, IN_WIDTH), _rows(tb, d), _rows(tb, d), _whole((1, d)), _layer((d, IN_WIDTH), l)],
        out_specs=[_rows(tb, d), _whole((HALO, d))],
        out_shape=[jax.ShapeDtypeStruct((s, d), F32), jax.ShapeDtypeStruct((HALO, d), F32)],
        compiler_params=_cparams("arbitrary"),
    )(dz, dx1, x, g, w_all)


def _tile_rows(rows):
    return jnp.pad(rows, ((0, HALO - rows.shape[0]), (0, 0)))


def _local_step(x, tgt, fetch, ff, sinks, g_mix, g_group, g_mlp, g_final, emit):
    s, d = x.shape
    depth = g_mix.shape[0]
    tb = min(512, s)
    tf = min(1024, ff)
    saved = []
    for l in range(depth):
        w_in, _, _, _, conv_w = fetch(0, l, x)
        cw = _tile_rows(conv_w[l])
        sk = jnp.pad(sinks[l].reshape(1, N_HEADS), ((0, HALO - 1), (0, 128 - N_HEADS)))
        h, za, zb, zc = _qkv_fwd(x, g_mix[l][None], w_in, l, tb)
        state = None
        for p, dil in enumerate(DILATIONS):
            state = _attn_fwd(za, dil, A_WIDTH, 1, 2, 1, A_MAX_DIST, state, None, p == len(DILATIONS) - 1,
                              "attn_a_fwd_%d" % dil)
        ya, lse_a = state
        yc, lse_c = _attn_fwd(zc, 1, C_KV_WIDTH, 3, 4, C_GROUP, C_MAX_DIST, None, sk, True, "attn_c_fwd")
        w_in, w_o, w1, w2, _ = fetch(1, l, yc)
        x1, yb = _mix_fwd(x, ya, yc, zb, cw, g_group[l][None], w_o, l, tb)
        x2, h2, ap = _mlp_fwd(x1, g_mlp[l][None], w1, w2, l, tb, tf)
        saved.append((x, h, za, zb, zc, ya, lse_a, yc, lse_c, yb, x1, h2, ap, cw, sk))
        x = x2
    dx, loss_tile, dg_final = _loss_head(x, g_final[None], tgt, tb)
    grads = [None] * depth
    tok = jnp.zeros((), F32)
    for l in reversed(range(depth)):
        x0, h, za, zb, zc, ya, lse_a, yc, lse_c, yb, x1, h2, ap, cw, sk = saved[l]
        dx1, dap, dg_mlp = _mlp_bwd(dx, x1, ap, g_mlp[l][None] + tok, w1, w2, l, tb, tf)
        tok = emit(l, 3, _wgrad(ap, dx, min(1024, ff), d, tb, "wgrad_ff_out", relu2=True))
        tok = tok + emit(l, 2, _wgrad(h2, dap, d, min(1024, ff), tb, "wgrad_ff_in"))
        n, dya, dyc, dd_a, dd_c, dyb, dg_group = _mix_bwd(dx1, ya, yb, yc, g_group[l][None] + tok, w_o, l, tb)
        tok = emit(l, 1, _wgrad(n, dx1, MIX_WIDTH, d, tb, "wgrad_o"))
        cw = cw + tok
        parts_a = [_attn_bwd(za, dya, lse_a, dd_a, dil, A_WIDTH, 1, 2, 1, A_MAX_DIST, None, "attn_a_bwd_%d" % dil)
                   for dil in DILATIONS]
        *parts_c, dsink = _attn_bwd(zc, dyc, lse_c, dd_c, 1, C_KV_WIDTH, 3, 4, C_GROUP, C_MAX_DIST, sk, "attn_c_bwd")
        dz, dcw = _dz_assemble(parts_a, parts_c, dyb, zb, cw)
        dx, dg_mix = _qkv_bwd(dz, dx1, x0, g_mix[l][None], w_in, l, tb)
        tok = emit(l, 0, _wgrad(h, dz, d, IN_WIDTH // 4, tb, "wgrad_in"))
        grads[l] = (dcw, dsink, dg_mix, dg_group, dg_mlp)
    return loss_tile, dx, grads, dg_final


ANY = pl.BlockSpec(memory_space=pl.ANY)
SHARD_AXES = (2, 1, 2, 1)
N_BIG = len(SHARD_AXES)
N_CHIPS = 4
N_DEV = 8


def _mesh_pos():
    return lax.axis_index("x"), lax.axis_index("y"), lax.axis_index("c")


def _flip(v, bit):
    return 1 - v if bit else v


def _shard_of(ref, w, layer, chip, n):
    start = pl.multiple_of(chip * n, 128)
    if SHARD_AXES[w] == 2:
        return ref.at[layer, :, pl.ds(start, n)]
    return ref.at[layer, pl.ds(start, n), :]


def _place_shard(shard, w, chip_arr, name):
    _, rows, cols = shard.shape
    tr = min(256, rows)
    nr = rows // tr
    if SHARD_AXES[w] == 2:
        full = (2, rows, cols * N_CHIPS)
        out_map = lambda l, i, chip: (l, i, chip[0])
    else:
        full = (2, rows * N_CHIPS, cols)
        out_map = lambda l, i, chip: (l, chip[0] * nr + i, 0)

    def body(chip_ref, x_ref, o_ref):
        o_ref[...] = x_ref[...].astype(BF)

    return pl.pallas_call(
        body, name=name,
        grid_spec=pltpu.PrefetchScalarGridSpec(
            num_scalar_prefetch=1, grid=(2, nr),
            in_specs=[pl.BlockSpec((None, tr, cols), lambda l, i, chip: (l, i, 0))],
            out_specs=pl.BlockSpec((None, tr, cols), out_map)),
        out_shape=jax.ShapeDtypeStruct(full, BF), compiler_params=_cparams("parallel", "parallel"),
    )(chip_arr, shard)


HBM = pl.BlockSpec(memory_space=pltpu.HBM)
SEM = pl.BlockSpec(memory_space=pltpu.SEMAPHORE)
EFFECT = pltpu.SideEffectType.DATAFLOW_SIDE_EFFECTING

GATHER_GROUPS = (((0, 0),), ((1, 0), (2, 0), (3, 0)), ((0, 1),), ((1, 1), (2, 1), (3, 1)))
GATHER_STARTS = ((0, 1), (2, 3))


def _gather_copies(arrs, group, send_sems, recv_sems):
    x, y, c = _mesh_pos()
    me = 2 * x + y
    out = []
    for i, (w, layer) in enumerate(group):
        n = arrs[w].shape[SHARD_AXES[w]] // N_CHIPS
        mine = _shard_of(arrs[w], w, layer, me, n)
        for j, (qx, qy) in enumerate([(1 - x, y), (x, 1 - y), (1 - x, 1 - y)]):
            landed = _shard_of(arrs[w], w, layer, 2 * qx + qy, n)
            out.append(tuple(pltpu.make_async_remote_copy(
                src_ref=piece, dst_ref=piece, send_sem=send_sems.at[i * 3 + j], recv_sem=recv_sems.at[i * 3 + j],
                device_id=(qx, qy, c), device_id_type=MESH) for piece in (mine, landed)))
    return out


def _conv_copies(conv_src, conv_dst, send_sems, recv_sems):
    x, y, c = _mesh_pos()
    out = []
    for j, (qx, qy) in enumerate([(1 - x, y), (x, 1 - y), (1 - x, 1 - y)]):
        out.append(tuple(pltpu.make_async_remote_copy(
            src_ref=conv_src, dst_ref=conv_dst.at[q], send_sem=send_sems.at[j], recv_sem=recv_sems.at[j],
            device_id=(qx, qy, c), device_id_type=MESH) for q in (2 * x + y, 2 * qx + qy)))
    return out


def _gather_start(groups, arrs, conv, name):
    n_sems = 2 * (len(groups) + (conv is not None))

    def body(*refs):
        mats = refs[:N_BIG]
        sems = refs[n_op:n_op + n_sems]
        if conv is not None:
            for cp, _ in _conv_copies(refs[N_BIG], refs[N_BIG + 1], sems[-2], sems[-1]):
                cp.start()
        for k, g in enumerate(groups):
            for cp, _ in _gather_copies(mats, GATHER_GROUPS[g], sems[2 * k], sems[2 * k + 1]):
                cp.start()

    sem_shapes = []
    for n in [len(GATHER_GROUPS[g]) for g in groups] + ([1] if conv is not None else []):
        sem_shapes += [pltpu.SemaphoreType.DMA((3 * n,))] * 2
    operands = list(arrs) + ([] if conv is None else list(conv))
    n_op = len(operands)
    res = pl.pallas_call(
        body, name=name,
        out_shape=tuple(sem_shapes) + tuple(pltpu.HBM(a.shape, a.dtype) for a in operands),
        in_specs=(HBM,) * n_op, out_specs=(SEM,) * n_sems + (HBM,) * n_op,
        input_output_aliases={i: n_sems + i for i in range(n_op)},
        compiler_params=pltpu.CompilerParams(has_side_effects=EFFECT),
    )(*[pltpu.with_memory_space_constraint(a, pltpu.HBM) for a in operands])
    return res[:n_sems], list(res[n_sems:])


def _gather_wait(k, sems, arrs, conv, after, name):
    group = GATHER_GROUPS[k]
    mats = sorted({w for w, _ in group})
    n_conv = 0 if conv is None else 2

    def body(*refs):
        local = refs[:len(mats)]
        arrs_ref = [None] * N_BIG
        for w, ref in zip(mats, local):
            arrs_ref[w] = ref
        pos = len(mats) + n_conv
        copies = _gather_copies(arrs_ref, group, refs[pos], refs[pos + 1])
        if conv is not None:
            copies += _conv_copies(refs[len(mats)], refs[len(mats) + 1], refs[pos + 2], refs[pos + 3])
        for send, recv in copies:
            recv.wait_recv()
            send.wait_send()

    operands = [arrs[w] for w in mats] + ([] if conv is None else [conv[1], conv[2]])
    sem_ops = list(sems) + ([] if conv is None else list(conv[0]))
    n_op = len(operands)
    res = pl.pallas_call(
        body, name=name, out_shape=tuple(pltpu.HBM(a.shape, a.dtype) for a in operands),
        in_specs=(HBM,) * n_op + (SEM,) * len(sem_ops) + (ANY,), out_specs=(HBM,) * n_op,
        input_output_aliases={i: i for i in range(n_op)},
        compiler_params=pltpu.CompilerParams(has_side_effects=EFFECT),
    )(*operands, *sem_ops, after)
    arrs = list(arrs)
    for w, a in zip(mats, res):
        arrs[w] = a
    return arrs, (res[-1] if conv is not None else None)


def _grad_shard(ref, w, chip, n):
    start = pl.multiple_of(chip * n, 128)
    if SHARD_AXES[w] == 2:
        return ref.at[:, pl.ds(start, n)]
    return ref.at[pl.ds(start, n), :]


def _slot_shape(g, w):
    shape = list(g.shape)
    shape[SHARD_AXES[w] - 1] //= N_CHIPS
    return (N_DEV - 1,) + tuple(shape)


def _scatter_copies(g_ref, land_ref, send_sems, recv_sems, layer, w):
    x, y, c = _mesh_pos()
    n = g_ref.shape[SHARD_AXES[w] - 1] // N_CHIPS
    out = []
    for r in range(1, N_DEV):
        tx, ty, tc = _flip(x, r & 4), _flip(y, r & 2), _flip(c, r & 1)
        cp = pltpu.make_async_remote_copy(
            src_ref=_grad_shard(g_ref, w, 2 * tx + ty, n), dst_ref=land_ref.at[r - 1], send_sem=send_sems.at[r - 1],
            recv_sem=recv_sems.at[r - 1], device_id=(tx, ty, tc), device_id_type=MESH)
        out.append((cp, (c != layer) if r & 1 else (c == layer)))
    return out


def _scatter_start(g, land, layer, w, name):
    def body(g_ref, land_ref, send_sems, recv_sems, g_thru, land_thru, token):
        for cp, mine in _scatter_copies(g_ref, land_ref, send_sems, recv_sems, layer, w):
            @pl.when(mine)
            def _():
                cp.start()
        token[...] = jnp.zeros_like(token)

    return pl.pallas_call(
        body, name=name,
        out_shape=(pltpu.SemaphoreType.DMA((N_DEV - 1,)), pltpu.SemaphoreType.DMA((N_DEV - 1,)),
                   pltpu.HBM(g.shape, g.dtype), pltpu.HBM(land.shape, land.dtype),
                   jax.ShapeDtypeStruct((HALO, 128), F32)),
        in_specs=(HBM, HBM), out_specs=(SEM, SEM, HBM, HBM, pl.BlockSpec(memory_space=pltpu.VMEM)),
        input_output_aliases={0: 2, 1: 3}, compiler_params=pltpu.CompilerParams(has_side_effects=EFFECT),
    )(pltpu.with_memory_space_constraint(g, pltpu.HBM), pltpu.with_memory_space_constraint(land, pltpu.HBM))


def _scatter_wait(started, land, after, w, name):
    def body(g0_ref, g1_ref, land_ref, ss0, rs0, ss1, rs1, after_ref, g0_out, g1_out, land_out):
        c = lax.axis_index("c")
        for layer, g_ref, ss, rs in ((0, g0_ref, ss0, rs0), (1, g1_ref, ss1, rs1)):
            for cp, mine in _scatter_copies(g_ref, land_ref, ss, rs, layer, w):
                @pl.when(mine)
                def _():
                    cp.wait_send()

                @pl.when(c == layer)
                def _():
                    cp.wait_recv()

    (ss0, rs0, g0), (ss1, rs1, g1) = started
    return pl.pallas_call(
        body, name=name,
        out_shape=(pltpu.HBM(g0.shape, g0.dtype), pltpu.HBM(g1.shape, g1.dtype), pltpu.HBM(land.shape, land.dtype)),
        in_specs=(HBM, HBM, HBM, SEM, SEM, SEM, SEM, ANY), out_specs=(HBM, HBM, HBM),
        input_output_aliases={0: 0, 1: 1, 2: 2}, compiler_params=pltpu.CompilerParams(has_side_effects=EFFECT),
    )(g0, g1, land, ss0, rs0, ss1, rs1, after)


def _sum_slots(g0, g1, slots, w, pos_arr, name):
    _, rows, cols = slots.shape
    tr = min(256, rows)
    nr = rows // tr
    if SHARD_AXES[w] == 2:
        own = pl.BlockSpec((tr, cols), lambda i, pos: (i, pos[0]))
    else:
        own = pl.BlockSpec((tr, cols), lambda i, pos: (pos[0] * nr + i, 0))

    def body(pos_ref, own0_ref, own1_ref, s_ref, o_ref):
        acc = jnp.where(pos_ref[1] == 0, own0_ref[...], own1_ref[...]).astype(F32)
        for r in range(N_DEV - 1):
            acc = acc + s_ref[r].astype(F32)
        o_ref[...] = acc

    return pl.pallas_call(
        body, name=name,
        grid_spec=pltpu.PrefetchScalarGridSpec(
            num_scalar_prefetch=1, grid=(nr,),
            in_specs=[own, own, pl.BlockSpec((N_DEV - 1, tr, cols), lambda i, pos: (0, i, 0))],
            out_specs=pl.BlockSpec((tr, cols), lambda i, pos: (i, 0))),
        out_shape=jax.ShapeDtypeStruct((rows, cols), F32), compiler_params=_cparams("parallel"),
    )(pos_arr, g0, g1, slots)


def _swap_layers(halves):
    def body(*refs):
        srcs, dsts = refs[:N_BIG], refs[N_BIG:2 * N_BIG]
        send_sems, recv_sems = refs[2 * N_BIG:]
        x, y, c = _mesh_pos()
        sends = [pltpu.make_async_remote_copy(src_ref=srcs[w], dst_ref=dsts[w], send_sem=send_sems.at[w],
                                              recv_sem=recv_sems.at[w], device_id=(x, y, 1 - c), device_id_type=MESH)
                 for w in range(N_BIG)]
        for cp in sends:
            cp.start()
        for cp in sends:
            cp.wait_recv()
        for cp in sends:
            cp.wait_send()

    return pl.pallas_call(
        body, name="swap_layers", in_specs=[ANY] * N_BIG, out_specs=[ANY] * N_BIG,
        out_shape=[jax.ShapeDtypeStruct(h.shape, h.dtype) for h in halves],
        scratch_shapes=[pltpu.SemaphoreType.DMA((N_BIG,)), pltpu.SemaphoreType.DMA((N_BIG,))],
    )(*halves)


def _adamw_math(w, g, m, v):
    m = ADAM_B1 * m + (1.0 - ADAM_B1) * g
    v = ADAM_B2 * v + (1.0 - ADAM_B2) * jnp.square(g)
    m_hat = m / (1.0 - ADAM_B1 ** ADAM_STEP)
    v_hat = v / (1.0 - ADAM_B2 ** ADAM_STEP)
    delta = -ADAM_LR * (m_hat / (jnp.sqrt(v_hat) + ADAM_EPS) + ADAM_WD * w)
    return delta, m, v


def _adamw(w, g_own, g_other, m, v, pos_arr, name):
    shape = w.shape
    _, rows, cols = shape
    tr = min(256, rows)

    def body(pos_ref, w_ref, own_ref, other_ref, m_ref, v_ref, g_ref, d_ref, m2_ref, v2_ref):
        g = jnp.where(pl.program_id(0) == pos_ref[1], own_ref[...], other_ref[...])
        g_ref[...] = g
        d_ref[...], m2_ref[...], v2_ref[...] = _adamw_math(w_ref[...], g, m_ref[...], v_ref[...])

    full = pl.BlockSpec((None, tr, cols), lambda l, i, pos: (l, i, 0))
    half = pl.BlockSpec((tr, cols), lambda l, i, pos: (i, 0))
    return pl.pallas_call(
        body, name=name,
        grid_spec=pltpu.PrefetchScalarGridSpec(
            num_scalar_prefetch=1, grid=(2, rows // tr),
            in_specs=[full, half, half, full, full], out_specs=[full] * 4),
        out_shape=[jax.ShapeDtypeStruct(shape, F32)] * 4, compiler_params=_cparams("parallel", "parallel"),
    )(pos_arr, w, g_own, g_other, m, v)


def _small_sync(part, w, m, v):
    rows, cols = part.shape

    def body(p_ref, w_ref, m_ref, v_ref, g_ref, d_ref, m2_ref, v2_ref, slots, send_sems, recv_sems):
        x, y, c = _mesh_pos()
        me = 4 * x + 2 * y + c
        slots[me] = p_ref[...]
        sends = []
        for r in range(1, N_DEV):
            to = (_flip(x, r & 4), _flip(y, r & 2), _flip(c, r & 1))
            sends.append(pltpu.make_async_remote_copy(
                src_ref=p_ref, dst_ref=slots.at[me], send_sem=send_sems.at[r - 1], recv_sem=recv_sems.at[r - 1],
                device_id=to, device_id_type=MESH))
        for cp in sends:
            cp.start()
        for cp in sends:
            cp.wait_recv()
        for cp in sends:
            cp.wait_send()
        g = slots[0]
        for i in range(1, N_DEV):
            g = g + slots[i]
        g_ref[...] = g
        d_ref[...], m2_ref[...], v2_ref[...] = _adamw_math(w_ref[...], g, m_ref[...], v_ref[...])

    vm = pl.BlockSpec(memory_space=pltpu.VMEM)
    return pl.pallas_call(
        body, name="small_sync", in_specs=[vm] * 4, out_specs=[vm] * 4,
        out_shape=[jax.ShapeDtypeStruct((rows, cols), F32)] * 4,
        scratch_shapes=[pltpu.VMEM((N_DEV, rows, cols), F32), pltpu.SemaphoreType.DMA((N_DEV - 1,)),
                        pltpu.SemaphoreType.DMA((N_DEV - 1,))],
    )(part, w, m, v)


def _pack_small(d, g_mix, g_group, g_mlp, g_final, conv_full, sinks, scalar):
    def part(rows):
        return jnp.pad(rows, ((0, HALO - rows.shape[0]), (0, d - rows.shape[1])))
    return jnp.concatenate([part(g_mix), part(g_group), part(g_mlp), part(g_final[None]),
                            part(conv_full.reshape(6, CONV_CH)), part(sinks.reshape(2, N_HEADS)),
                            part(scalar.reshape(1, 1))], axis=0)


def _unpack_small(p, dm):
    return (p[0:2, :dm], p[8:10, :MIX_WIDTH], p[16:18, :dm], p[24, :dm], p[32:38, :CONV_CH].reshape(2, 3, CONV_CH),
            p[40:42, :N_HEADS].reshape(2, 2, C_GROUP), p[48, 0])


def kernel(x, w_in, conv_w, sinks, g_mix, g_group, w_o, g_mlp, w_ff_in, w_ff_out, g_final, loss_target, m_w_in, m_conv_w, m_sinks, m_g_mix, m_g_group, m_w_o, m_g_mlp, m_w_ff_in, m_w_ff_out, m_g_final, v_w_in, v_conv_w, v_sinks, v_g_mix, v_g_group, v_w_o, v_g_mlp, v_w_ff_in, v_w_ff_out, v_g_final):
    d = max(x.shape[2], MIX_WIDTH)
    chip = 2 * lax.axis_index("x") + lax.axis_index("y")
    conv_n = conv_w.shape[2]

    pos_arr = jnp.stack([chip, lax.axis_index("c")]).astype(jnp.int32)
    placed = [_place_shard(w, i, pos_arr[:1], "place_shard_%d" % i)
              for i, w in enumerate((w_in, w_o, w_ff_in, w_ff_out))]
    conv_tile = jnp.pad(conv_w.reshape(6, conv_n), ((0, HALO - 6), (0, 128 - conv_n)))
    sems0, thru = _gather_start(GATHER_STARTS[0], placed,
                                (conv_tile, lax.empty((N_CHIPS,) + conv_tile.shape, conv_tile.dtype)), "gather_start_0")
    full = {"arrs": thru[:N_BIG], "conv": None, "sems": list(sems0[:4])}

    def fetch(stage, layer, after):
        k = 2 * layer + stage
        sems = full["sems"][2 * k:2 * k + 2]
        if k == 0:
            full["arrs"], land = _gather_wait(0, sems, full["arrs"], (sems0[-2:], thru[N_BIG], thru[N_BIG + 1]),
                                              after, "gather_wait_0")
            conv_all = lax.dynamic_update_slice(land, conv_tile[None], (chip, 0, 0))
            full["conv"] = conv_all[:, :6, :conv_n].reshape(N_CHIPS, 2, 3, conv_n).transpose(1, 2, 0, 3).reshape(
                2, 3, CONV_CH)
        else:
            full["arrs"], _ = _gather_wait(k, sems, full["arrs"], None, after, "gather_wait_%d" % k)
        if k == 1:
            sems1, full["arrs"] = _gather_start(GATHER_STARTS[1], full["arrs"], None, "gather_start_1")
            full["sems"] += list(sems1)
        return (*full["arrs"], full["conv"])

    lands, started = [None] * N_BIG, {}

    def emit(layer, w, g):
        if lands[w] is None:
            lands[w] = lax.empty(_slot_shape(g, w), g.dtype)
        *started[layer, w], lands[w], token = _scatter_start(g, lands[w], layer, w, "scatter_start_%d_%d" % (layer, w))
        return token[0, 0]

    loss_tile, dx, grads, dg_final = _local_step(x[0], loss_target[0], fetch, w_ff_in.shape[2] * N_CHIPS,
                                                 sinks, g_mix, g_group, g_mlp, g_final, emit)

    own = []
    for w in range(N_BIG):
        g0, g1, slots = _scatter_wait((started[0, w], started[1, w]), lands[w], dx, w, "scatter_wait_%d" % w)
        own.append(_sum_slots(g0, g1, slots, w, pos_arr, "sum_slots_%d" % w))
    other = _swap_layers(own)

    def both(i):
        return jnp.stack([grads[0][i][0], grads[1][i][0]])
    dconv = jnp.stack([grads[0][0][:3], grads[1][0][:3]])
    dsinks = jnp.stack([grads[0][1][0, :N_HEADS], grads[1][1][0, :N_HEADS]])
    part = _pack_small(d, both(2), both(3), both(4), dg_final[0], dconv, dsinks, loss_tile[0, 0])

    def spread(shard):
        return lax.dynamic_update_slice(jnp.zeros((2, 3, CONV_CH), F32), shard, (0, 0, chip * conv_n))
    zero = jnp.zeros((), F32)
    packs = [_pack_small(d, a, b, c_, e, spread(f), g_, zero) for a, b, c_, e, f, g_ in (
        (g_mix, g_group, g_mlp, g_final, conv_w, sinks),
        (m_g_mix, m_g_group, m_g_mlp, m_g_final, m_conv_w, m_sinks),
        (v_g_mix, v_g_group, v_g_mlp, v_g_final, v_conv_w, v_sinks))]
    small = [_unpack_small(p, x.shape[2]) for p in _small_sync(part, *packs)]

    def shard_of(full):
        return lax.dynamic_slice(full, (0, 0, chip * conv_n), (2, 3, conv_n))
    small = [(s[0], s[1], s[2], s[3], shard_of(s[4]), s[5], s[6]) for s in small]
    loss = small[0][6]

    big = [_adamw(w, own[i], other[i], m, v, pos_arr, "adamw_%d" % i) for i, (w, m, v) in enumerate((
        (w_in, m_w_in, v_w_in), (w_o, m_w_o, v_w_o), (w_ff_in, m_w_ff_in, v_w_ff_in),
        (w_ff_out, m_w_ff_out, v_w_ff_out)))]

    def ordered(kind):
        b = [big[i][kind] for i in range(N_BIG)]
        s = small[kind]
        return [b[0], s[4], s[5], s[0], s[1], b[1], s[2], b[2], b[3], s[3]]

    return (loss, dx[None], *ordered(0), *ordered(1), *ordered(2), *ordered(3))
```

```python
import functools

import jax
import jax.numpy as jnp
from jax import lax
from jax.experimental import pallas as pl
from jax.experimental.pallas import tpu as pltpu

HEAD_DIM = 64
N_HEADS = 6
C_GROUP = 3
A_WIDTH = N_HEADS * HEAD_DIM
C_KV_WIDTH = 2 * HEAD_DIM
CONV_CH = 256
ZA_W = 3 * A_WIDTH
ZB_W = 3 * CONV_CH
ZC_W = A_WIDTH + 2 * C_KV_WIDTH
IN_WIDTH = ZA_W + ZB_W + ZC_W
MIX_WIDTH = A_WIDTH + CONV_CH + A_WIDTH
DILATIONS = (1, 4, 16)
A_MAX_DIST = 128
C_MAX_DIST = 127
TQ = 128
EPS = 1e-6
SCALE = HEAD_DIM ** -0.5
NEG = -1e30
HALO = 8

ADAM_LR = 0.001
ADAM_B1 = 0.9
ADAM_B2 = 0.999
ADAM_EPS = 1e-08
ADAM_WD = 0.01
ADAM_STEP = 10

BF = jnp.bfloat16
F32 = jnp.float32
MESH = pl.DeviceIdType.MESH
VMEM_LIMIT = 56 * 1024 * 1024


def _cparams(*sem):
    return pltpu.CompilerParams(dimension_semantics=sem, vmem_limit_bytes=VMEM_LIMIT)


def _nt(a, b):
    return lax.dot_general(a, b, (((1,), (1,)), ((), ())), preferred_element_type=F32)


def _tn(a, b):
    return lax.dot_general(a, b, (((0,), (0,)), ((), ())), preferred_element_type=F32)


def _nn(a, b):
    return jnp.dot(a, b, preferred_element_type=F32)


def _rows(tb, w):
    return pl.BlockSpec((tb, w), lambda i: (i, 0))


def _whole(shape):
    return pl.BlockSpec(shape, lambda *_: (0,) * len(shape))


def _layer(shape, l):
    return pl.BlockSpec((None,) + shape, lambda *_: (l,) + (0,) * len(shape))


def _rms_scale(v):
    return lax.rsqrt(jnp.mean(v * v, axis=-1, keepdims=True) + EPS)


def _norm_bwd(dxhat, xhat, r):
    return r * (dxhat - xhat * jnp.mean(dxhat * xhat, axis=-1, keepdims=True))


def _qkv_fwd(x, g, w_all, l, tb):
    s, d = x.shape

    def body(x_ref, g_ref, w_ref, h_ref, za_ref, zb_ref, zc_ref):
        xv = x_ref[...]
        h = ((xv * _rms_scale(xv)) * g_ref[...]).astype(BF)
        h_ref[...] = h
        z = jnp.concatenate([_nn(h, w_ref[k]) for k in range(N_CHIPS)], axis=1)
        za_ref[...] = z[:, :ZA_W].astype(BF)
        zb_ref[...] = z[:, ZA_W:ZA_W + ZB_W]
        zc_ref[...] = z[:, ZA_W + ZB_W:].astype(BF)

    return pl.pallas_call(
        body, grid=(s // tb,), name="qkv_fwd",
        in_specs=[_rows(tb, d), _whole((1, d)), _layer((N_CHIPS, d, IN_WIDTH // N_CHIPS), l)],
        out_specs=[_rows(tb, d), _rows(tb, ZA_W), _rows(tb, ZB_W), _rows(tb, ZC_W)],
        out_shape=[jax.ShapeDtypeStruct((s, d), BF), jax.ShapeDtypeStruct((s, ZA_W), BF),
                   jax.ShapeDtypeStruct((s, ZB_W), F32), jax.ShapeDtypeStruct((s, ZC_W), BF)],
        compiler_params=_cparams("parallel"),
    )(x, g, w_all)


def _band_mask(b, max_dist):
    qi = lax.broadcasted_iota(jnp.int32, (TQ, 2 * TQ), 0)
    kj = lax.broadcasted_iota(jnp.int32, (TQ, 2 * TQ), 1)
    dist = TQ + qi - kj
    return (dist >= 0) & (dist <= max_dist) & ((kj >= TQ) | (b > 0))


def _attn_specs(dil, zw, kw, kcol, vcol):
    nq, nk = zw // A_WIDTH, zw // kw
    q = pl.BlockSpec((TQ, A_WIDTH), lambda r, b: (b, r * nq))
    kp = pl.BlockSpec((TQ, kw), lambda r, b: (jnp.maximum(b - 1, 0), r * nk + kcol))
    kc = pl.BlockSpec((TQ, kw), lambda r, b: (b, r * nk + kcol))
    vp = pl.BlockSpec((TQ, kw), lambda r, b: (jnp.maximum(b - 1, 0), r * nk + vcol))
    vc = pl.BlockSpec((TQ, kw), lambda r, b: (b, r * nk + vcol))
    return [q, kp, kc, vp, vc]


def _head_spec(w=A_WIDTH):
    return pl.BlockSpec((TQ, w), lambda r, b: (b, r))


def _hs(h):
    return slice(h * HEAD_DIM, (h + 1) * HEAD_DIM)


def _attn_fwd(z, dil, kw, kcol, vcol, n_rep, max_dist, state, sink, last, name):
    s, zw = z.shape
    sub = s // dil
    zv = z.reshape(sub, dil * zw)
    have_state, have_sink = state is not None, sink is not None

    def body(*refs):
        q_ref, kp_ref, kc_ref, vp_ref, vc_ref = refs[:5]
        pos = 5
        if have_state:
            acc_in, m_in, l_in = refs[pos:pos + 3]
            pos += 3
        if have_sink:
            sink_ref = refs[pos]
            pos += 1
        outs = refs[pos:]
        mask = _band_mask(pl.program_id(1), max_dist)
        for h in range(N_HEADS):
            kh = h // n_rep
            q = q_ref[:, _hs(h)]
            k2 = jnp.concatenate([kp_ref[:, _hs(kh)], kc_ref[:, _hs(kh)]], axis=0)
            v2 = jnp.concatenate([vp_ref[:, _hs(kh)], vc_ref[:, _hs(kh)]], axis=0)
            sc = jnp.where(mask, _nt(q, k2) * SCALE, NEG)
            m_new = jnp.max(sc, axis=1, keepdims=True)
            if have_sink:
                sk = sink_ref[0:1, h:h + 1]
                m_new = jnp.maximum(m_new, sk)
            if have_state:
                m_old = m_in[:, h * HEAD_DIM:h * HEAD_DIM + 1]
                m_new = jnp.maximum(m_new, m_old)
            p = jnp.exp(sc - m_new)
            l_new = jnp.sum(p, axis=1, keepdims=True)
            acc = _nn(p.astype(BF), v2)
            if have_state:
                alpha = jnp.exp(m_old - m_new)
                l_new = l_new + alpha * l_in[:, h * HEAD_DIM:h * HEAD_DIM + 1]
                acc = acc + alpha * acc_in[:, _hs(h)]
            if have_sink:
                l_new = l_new + jnp.exp(sk - m_new)
            if last:
                outs[0][:, _hs(h)] = acc / l_new
                outs[1][:, _hs(h)] = jnp.broadcast_to(m_new + jnp.log(l_new), (TQ, HEAD_DIM))
            else:
                outs[0][:, _hs(h)] = acc
                outs[1][:, _hs(h)] = jnp.broadcast_to(m_new, (TQ, HEAD_DIM))
                outs[2][:, _hs(h)] = jnp.broadcast_to(l_new, (TQ, HEAD_DIM))

    args = [zv] * 5
    in_specs = _attn_specs(dil, zw, kw, kcol, vcol)
    if have_state:
        args += [a.reshape(sub, dil * A_WIDTH) for a in state]
        in_specs += [_head_spec()] * 3
    if have_sink:
        args.append(sink)
        in_specs.append(_whole((HALO, 128)))
    n_out = 2 if last else 3
    res = pl.pallas_call(
        body, grid=(dil, sub // TQ), name=name, in_specs=in_specs,
        out_specs=[_head_spec()] * n_out,
        out_shape=[jax.ShapeDtypeStruct((sub, dil * A_WIDTH), F32)] * n_out,
        compiler_params=_cparams("parallel", "parallel"),
    )(*args)
    return [a.reshape(s, A_WIDTH) for a in res]


def _shift_down(v, n, halo):
    rows = v.shape[0]
    out = pltpu.roll(v, n, 0)
    row = lax.broadcasted_iota(jnp.int32, v.shape, 0)
    for t in range(n):
        out = jnp.where(row == t, halo[HALO - n + t:HALO - n + t + 1, :], out)
    return out


def _shift_up(v, n, halo):
    rows = v.shape[0]
    out = pltpu.roll(v, rows - n, 0)
    row = lax.broadcasted_iota(jnp.int32, v.shape, 0)
    for t in range(n):
        out = jnp.where(row == rows - n + t, halo[t:t + 1, :], out)
    return out


def _conv_parts(zb, zb_prev, cw):
    gb, gc, xb = zb[:, :CONV_CH], zb[:, CONV_CH:2 * CONV_CH], zb[:, 2 * CONV_CH:]
    u = gc * xb
    uh = zb_prev[:, CONV_CH:2 * CONV_CH] * zb_prev[:, 2 * CONV_CH:]
    u1 = _shift_down(u, 1, uh)
    u2 = _shift_down(u, 2, uh)
    c = cw[0:1, :] * u2 + cw[1:2, :] * u1 + cw[2:3, :] * u
    return gb, gc, xb, u, u1, u2, c


def _prev_halo(tb, w):
    return pl.BlockSpec((HALO, w), lambda i: (jnp.maximum(i * (tb // HALO) - 1, 0), 0))


def _next_halo(tb, w, nblk):
    return pl.BlockSpec((HALO, w), lambda i: (jnp.minimum((i + 1) * (tb // HALO), nblk * (tb // HALO) - 1), 0))


def _mix_fwd(x, ya, yc, zb, cw, gg, wo_all, l, tb):
    s, d = x.shape

    def body(x_ref, ya_ref, yc_ref, zb_ref, zbp_ref, cw_ref, gg_ref, wo_ref, x1_ref, yb_ref):
        i = pl.program_id(0)
        zbp = jnp.where(i > 0, zbp_ref[...], 0.0)
        gb, _, _, _, _, _, c = _conv_parts(zb_ref[...], zbp, cw_ref[...])
        yb = gb * c
        yb_ref[...] = yb
        ya, yc = ya_ref[...], yc_ref[...]
        n = jnp.concatenate([ya * _rms_scale(ya), yb * _rms_scale(yb), yc * _rms_scale(yc)], axis=1)
        n = (n * gg_ref[...]).astype(BF)
        x1_ref[...] = x_ref[...] + _nn(n, wo_ref[...].reshape(MIX_WIDTH, d))

    return pl.pallas_call(
        body, grid=(s // tb,), name="mix_fwd",
        in_specs=[_rows(tb, d), _rows(tb, A_WIDTH), _rows(tb, A_WIDTH), _rows(tb, ZB_W), _prev_halo(tb, ZB_W),
                  _whole((HALO, CONV_CH)), _whole((1, MIX_WIDTH)), _layer((N_CHIPS, MIX_WIDTH // N_CHIPS, d), l)],
        out_specs=[_rows(tb, d), _rows(tb, CONV_CH)],
        out_shape=[jax.ShapeDtypeStruct((s, d), F32), jax.ShapeDtypeStruct((s, CONV_CH), F32)],
        compiler_params=_cparams("parallel"),
    )(x, ya, yc, zb, zb, cw, gg, wo_all)


def _mlp_fwd(x1, g, w1_all, w2_all, l, tb, tf):
    s, d = x1.shape
    ff = w1_all.shape[1] * w1_all.shape[3]
    nj = ff // tf

    def body(x_ref, g_ref, w1_ref, w2_ref, x2_ref, h2_ref, ap_ref, acc):
        j = pl.program_id(1)

        @pl.when(j == 0)
        def _():
            xv = x_ref[...]
            h2_ref[...] = ((xv * _rms_scale(xv)) * g_ref[...]).astype(BF)
            acc[...] = jnp.zeros_like(acc)

        ap = _nn(h2_ref[...], w1_ref[...])
        ap_ref[...] = ap.astype(BF)
        a = jnp.square(jnp.maximum(ap, 0.0)).astype(BF)
        acc[...] += _nn(a, w2_ref[...])

        @pl.when(j == nj - 1)
        def _():
            x2_ref[...] = x_ref[...] + acc[...]

    return pl.pallas_call(
        body, grid=(s // tb, nj), name="mlp_fwd",
        in_specs=[pl.BlockSpec((tb, d), lambda i, j: (i, 0)), _whole((1, d)),
                  pl.BlockSpec((None, None, d, tf), lambda i, j: (l, j, 0, 0)),
                  pl.BlockSpec((None, None, tf, d), lambda i, j: (l, j, 0, 0))],
        out_specs=[pl.BlockSpec((tb, d), lambda i, j: (i, 0)), pl.BlockSpec((tb, d), lambda i, j: (i, 0)),
                   pl.BlockSpec((tb, tf), lambda i, j: (i, j))],
        out_shape=[jax.ShapeDtypeStruct((s, d), F32), jax.ShapeDtypeStruct((s, d), BF),
                   jax.ShapeDtypeStruct((s, ff), BF)],
        scratch_shapes=[pltpu.VMEM((tb, d), F32)],
        compiler_params=_cparams("parallel", "arbitrary"),
    )(x1, g, w1_all, w2_all)


def _loss_head(x, g, tgt, tb):
    s, d = x.shape

    def body(x_ref, g_ref, t_ref, dx_ref, loss_ref, dg_ref):
        i = pl.program_id(0)

        @pl.when(i == 0)
        def _():
            loss_ref[...] = jnp.zeros_like(loss_ref)
            dg_ref[...] = jnp.zeros_like(dg_ref)

        xv = x_ref[...]
        r = _rms_scale(xv)
        xhat = xv * r
        err = xhat * g_ref[...] - t_ref[...]
        part = jnp.sum(jnp.mean(jnp.square(err), axis=-1, keepdims=True), axis=0, keepdims=True)
        loss_ref[...] += 0.5 * part
        dy = err * (1.0 / d)
        dg_ref[...] += jnp.sum(dy * xhat, axis=0, keepdims=True)
        dx_ref[...] = _norm_bwd(dy * g_ref[...], xhat, r)

    return pl.pallas_call(
        body, grid=(s // tb,), name="loss_head",
        in_specs=[_rows(tb, d), _whole((1, d)), _rows(tb, d)],
        out_specs=[_rows(tb, d), _whole((HALO, 128)), _whole((HALO, d))],
        out_shape=[jax.ShapeDtypeStruct((s, d), F32), jax.ShapeDtypeStruct((HALO, 128), F32),
                   jax.ShapeDtypeStruct((HALO, d), F32)],
        compiler_params=_cparams("arbitrary"),
    )(x, g, tgt)


def _mlp_bwd(dx2, x1, ap, g, w1_all, w2_all, l, tb, tf):
    s, d = x1.shape
    ff = ap.shape[1]
    nj = ff // tf

    def body(dx2_ref, x1_ref, ap_ref, g_ref, w1_ref, w2_ref, dx1_ref, dap_ref, dg_ref, acc):
        i, j = pl.program_id(0), pl.program_id(1)

        @pl.when((i == 0) & (j == 0))
        def _():
            dg_ref[...] = jnp.zeros_like(dg_ref)

        @pl.when(j == 0)
        def _():
            acc[...] = jnp.zeros_like(acc)

        da = _nt(dx2_ref[...].astype(BF), w2_ref[...])
        dap = (da * (2.0 * jnp.maximum(ap_ref[...].astype(F32), 0.0))).astype(BF)
        dap_ref[...] = dap
        acc[...] += _nt(dap, w1_ref[...])

        @pl.when(j == nj - 1)
        def _():
            xv = x1_ref[...]
            r = _rms_scale(xv)
            xhat = xv * r
            dh = acc[...]
            dg_ref[...] += jnp.sum(dh * xhat, axis=0, keepdims=True)
            dx1_ref[...] = dx2_ref[...] + _norm_bwd(dh * g_ref[...], xhat, r)

    return pl.pallas_call(
        body, grid=(s // tb, nj), name="mlp_bwd",
        in_specs=[pl.BlockSpec((tb, d), lambda i, j: (i, 0)), pl.BlockSpec((tb, d), lambda i, j: (i, 0)),
                  pl.BlockSpec((tb, tf), lambda i, j: (i, j)),
                  _whole((1, d)), pl.BlockSpec((None, None, d, tf), lambda i, j: (l, j, 0, 0)),
                  pl.BlockSpec((None, None, tf, d), lambda i, j: (l, j, 0, 0))],
        out_specs=[pl.BlockSpec((tb, d), lambda i, j: (i, 0)), pl.BlockSpec((tb, tf), lambda i, j: (i, j)),
                   _whole((HALO, d))],
        out_shape=[jax.ShapeDtypeStruct((s, d), F32), jax.ShapeDtypeStruct((s, ff), BF),
                   jax.ShapeDtypeStruct((HALO, d), F32)],
        scratch_shapes=[pltpu.VMEM((tb, d), F32)],
        compiler_params=_cparams("arbitrary", "arbitrary"),
    )(dx2, x1, ap, g, w1_all, w2_all)


def _wgrad(a, b, tm, tn, ts, name, relu2=False):
    s, m = a.shape
    n = b.shape[1]
    ns = s // ts

    def body(a_ref, b_ref, o_ref, acc):
        k = pl.program_id(2)

        @pl.when(k == 0)
        def _():
            acc[...] = jnp.zeros_like(acc)

        av = a_ref[...]
        if relu2:
            av = jnp.square(jnp.maximum(av.astype(F32), 0.0)).astype(BF)
        acc[...] += _tn(av, b_ref[...].astype(BF))

        @pl.when(k == ns - 1)
        def _():
            o_ref[...] = acc[...].astype(BF)

    return pl.pallas_call(
        body, grid=(m // tm, n // tn, ns), name=name,
        in_specs=[pl.BlockSpec((ts, tm), lambda i, j, k: (k, i)), pl.BlockSpec((ts, tn), lambda i, j, k: (k, j))],
        out_specs=pl.BlockSpec((tm, tn), lambda i, j, k: (i, j)),
        out_shape=jax.ShapeDtypeStruct((m, n), BF),
        scratch_shapes=[pltpu.VMEM((tm, tn), F32)],
        compiler_params=_cparams("parallel", "parallel", "arbitrary"),
    )(a, b)


def _mix_bwd(dx1, ya, yb, yc, gg, wo_all, l, tb):
    s, d = dx1.shape

    def body(dx_ref, ya_ref, yb_ref, yc_ref, gg_ref, wo_ref, n_ref, dya_ref, dyc_ref, da_ref, dc_ref, dyb_ref, dg_ref):
        i = pl.program_id(0)

        @pl.when(i == 0)
        def _():
            dg_ref[...] = jnp.zeros_like(dg_ref)

        dn = _nt(dx_ref[...].astype(BF), wo_ref[...].reshape(MIX_WIDTH, d))
        ys = [ya_ref[...], yb_ref[...], yc_ref[...]]
        rs = [_rms_scale(v) for v in ys]
        nhat = jnp.concatenate([v * r for v, r in zip(ys, rs)], axis=1)
        gg = gg_ref[...]
        n_ref[...] = (nhat * gg).astype(BF)
        dg_ref[...] += jnp.sum(dn * nhat, axis=0, keepdims=True)
        dnh = dn * gg
        bounds = [(0, A_WIDTH), (A_WIDTH, A_WIDTH + CONV_CH), (A_WIDTH + CONV_CH, MIX_WIDTH)]
        dys = [_norm_bwd(dnh[:, lo:hi], nhat[:, lo:hi], r) for (lo, hi), r in zip(bounds, rs)]
        dyb_ref[...] = dys[1]
        for dy, y, dy_ref, dd_ref in ((dys[0], ys[0], dya_ref, da_ref), (dys[2], ys[2], dyc_ref, dc_ref)):
            dy_ref[...] = dy.astype(BF)
            t = dy * y
            for h in range(N_HEADS):
                dd_ref[:, _hs(h)] = jnp.broadcast_to(jnp.sum(t[:, _hs(h)], axis=1, keepdims=True), (tb, HEAD_DIM))

    return pl.pallas_call(
        body, grid=(s // tb,), name="mix_bwd",
        in_specs=[_rows(tb, d), _rows(tb, A_WIDTH), _rows(tb, CONV_CH), _rows(tb, A_WIDTH), _whole((1, MIX_WIDTH)),
                  _layer((N_CHIPS, MIX_WIDTH // N_CHIPS, d), l)],
        out_specs=[_rows(tb, MIX_WIDTH), _rows(tb, A_WIDTH), _rows(tb, A_WIDTH), _rows(tb, A_WIDTH),
                   _rows(tb, A_WIDTH), _rows(tb, CONV_CH), _whole((HALO, MIX_WIDTH))],
        out_shape=[jax.ShapeDtypeStruct((s, MIX_WIDTH), BF), jax.ShapeDtypeStruct((s, A_WIDTH), BF),
                   jax.ShapeDtypeStruct((s, A_WIDTH), BF), jax.ShapeDtypeStruct((s, A_WIDTH), F32),
                   jax.ShapeDtypeStruct((s, A_WIDTH), F32), jax.ShapeDtypeStruct((s, CONV_CH), F32),
                   jax.ShapeDtypeStruct((HALO, MIX_WIDTH), F32)],
        compiler_params=_cparams("arbitrary"),
    )(dx1, ya, yb, yc, gg, wo_all)


def _attn_bwd(z, dy, lse, dd, dil, kw, kcol, vcol, n_rep, max_dist, sink, name):
    s, zw = z.shape
    sub = s // dil
    zv = z.reshape(sub, dil * zw)
    have_sink = sink is not None
    n_kv = N_HEADS // n_rep

    def body(*refs):
        q_ref, kp_ref, kc_ref, vp_ref, vc_ref, dy_ref, lse_ref, dd_ref = refs[:8]
        pos = 8
        if have_sink:
            sink_ref = refs[pos]
            pos += 1
        dq_ref, dkp_ref, dkc_ref, dvp_ref, dvc_ref = refs[pos:pos + 5]
        b = pl.program_id(1)
        mask = _band_mask(b, max_dist)
        if have_sink:
            dsink_ref = refs[pos + 5]

            @pl.when(b == 0)
            def _():
                dsink_ref[...] = jnp.zeros_like(dsink_ref)

            row = lax.broadcasted_iota(jnp.int32, (HALO, 128), 0)
            lane = lax.broadcasted_iota(jnp.int32, (HALO, 128), 1)
        for kh in range(n_kv):
            k2 = jnp.concatenate([kp_ref[:, _hs(kh)], kc_ref[:, _hs(kh)]], axis=0)
            v2 = jnp.concatenate([vp_ref[:, _hs(kh)], vc_ref[:, _hs(kh)]], axis=0)
            dk2 = jnp.zeros((2 * TQ, HEAD_DIM), F32)
            dv2 = jnp.zeros((2 * TQ, HEAD_DIM), F32)
            for h in range(kh * n_rep, (kh + 1) * n_rep):
                q = q_ref[:, _hs(h)]
                lse_h = lse_ref[:, h * HEAD_DIM:h * HEAD_DIM + 1]
                dd_h = dd_ref[:, h * HEAD_DIM:h * HEAD_DIM + 1]
                dyh = dy_ref[:, _hs(h)]
                sc = jnp.where(mask, _nt(q, k2) * SCALE, NEG)
                p = jnp.exp(sc - lse_h)
                dp = _nt(dyh, v2)
                ds = ((p * (dp - dd_h)) * SCALE).astype(BF)
                dq_ref[:, _hs(h)] = _nn(ds, k2)
                dk2 = dk2 + _tn(ds, q)
                dv2 = dv2 + _tn(p.astype(BF), dyh)
                if have_sink:
                    sk = sink_ref[0:1, h:h + 1]
                    val = -jnp.sum(jnp.exp(sk - lse_h) * dd_h, axis=0, keepdims=True)
                    dsink_ref[...] += jnp.where((row == 0) & (lane == h), val, 0.0)
            dkp_ref[:, _hs(kh)] = dk2[:TQ]
            dkc_ref[:, _hs(kh)] = dk2[TQ:]
            dvp_ref[:, _hs(kh)] = dv2[:TQ]
            dvc_ref[:, _hs(kh)] = dv2[TQ:]

    args = [zv] * 5 + [a.reshape(sub, dil * A_WIDTH) for a in (dy, lse, dd)]
    in_specs = _attn_specs(dil, zw, kw, kcol, vcol) + [_head_spec()] * 3
    out_specs = [_head_spec()] + [_head_spec(kw)] * 4
    out_shape = [jax.ShapeDtypeStruct((sub, dil * A_WIDTH), F32)] + [jax.ShapeDtypeStruct((sub, dil * kw), F32)] * 4
    if have_sink:
        args.append(sink)
        in_specs.append(_whole((HALO, 128)))
        out_specs.append(_whole((HALO, 128)))
        out_shape.append(jax.ShapeDtypeStruct((HALO, 128), F32))
    res = pl.pallas_call(
        body, grid=(dil, sub // TQ), name=name, in_specs=in_specs, out_specs=out_specs, out_shape=out_shape,
        compiler_params=_cparams("arbitrary", "arbitrary"),
    )(*args)
    outs = [res[0].reshape(s, A_WIDTH)] + [a.reshape(s, kw) for a in res[1:5]]
    return outs + list(res[5:])


def _dz_assemble(parts_a, parts_c, dyb, zb, cw):
    s = zb.shape[0]
    nb = s // TQ

    def shifted(w, dil):
        return pl.BlockSpec((TQ, w), lambda i: (jnp.minimum(i + dil, nb - 1), 0))

    args, in_specs = [], []
    for dil, (dq, dkp, dkc, dvp, dvc) in zip(DILATIONS + (1,), parts_a + [parts_c]):
        w = dkp.shape[1]
        args += [dq, dkp, dkc, dvp, dvc]
        in_specs += [_rows(TQ, A_WIDTH), shifted(w, dil), _rows(TQ, w), shifted(w, dil), _rows(TQ, w)]
    args += [dyb, dyb, zb, zb, zb, cw]
    in_specs += [_rows(TQ, CONV_CH), _next_halo(TQ, CONV_CH, nb), _rows(TQ, ZB_W), _prev_halo(TQ, ZB_W),
                 _next_halo(TQ, ZB_W, nb), _whole((HALO, CONV_CH))]
    n_att = 20

    def body(*refs):
        att = refs[:n_att]
        dyb_ref, dybn_ref, zb_ref, zbp_ref, zbn_ref, cw_ref, dz_ref, dcw_ref = refs[n_att:]
        i = pl.program_id(0)

        @pl.when(i == 0)
        def _():
            dcw_ref[...] = jnp.zeros_like(dcw_ref)

        dq = jnp.zeros((TQ, A_WIDTH), F32)
        dk = jnp.zeros((TQ, A_WIDTH), F32)
        dv = jnp.zeros((TQ, A_WIDTH), F32)
        for p, dil in enumerate(DILATIONS):
            dq_r, dkp_r, dkc_r, dvp_r, dvc_r = att[5 * p:5 * p + 5]
            live = i + dil < nb
            dq = dq + dq_r[...]
            dk = dk + dkc_r[...] + jnp.where(live, dkp_r[...], 0.0)
            dv = dv + dvc_r[...] + jnp.where(live, dvp_r[...], 0.0)
        dz_ref[:, 0:A_WIDTH] = dq.astype(BF)
        dz_ref[:, A_WIDTH:2 * A_WIDTH] = dk.astype(BF)
        dz_ref[:, 2 * A_WIDTH:ZA_W] = dv.astype(BF)
        dq_r, dkp_r, dkc_r, dvp_r, dvc_r = att[15:20]
        live = i + 1 < nb
        c0 = ZA_W + ZB_W
        dz_ref[:, c0:c0 + A_WIDTH] = dq_r[...].astype(BF)
        dz_ref[:, c0 + A_WIDTH:c0 + A_WIDTH + C_KV_WIDTH] = (dkc_r[...] + jnp.where(live, dkp_r[...], 0.0)).astype(BF)
        dz_ref[:, c0 + A_WIDTH + C_KV_WIDTH:IN_WIDTH] = (dvc_r[...] + jnp.where(live, dvp_r[...], 0.0)).astype(BF)

        cw = cw_ref[...]
        zbp = jnp.where(i > 0, zbp_ref[...], 0.0)
        gb, gc, xb, u, u1, u2, c = _conv_parts(zb_ref[...], zbp, cw)
        dyb = dyb_ref[...]
        dcv = dyb * gb
        dcn = jnp.where(i + 1 < nb, dybn_ref[...] * zbn_ref[:, :CONV_CH], 0.0)
        du = cw[2:3, :] * dcv + cw[1:2, :] * _shift_up(dcv, 1, dcn) + cw[0:1, :] * _shift_up(dcv, 2, dcn)
        dz_ref[:, ZA_W:ZA_W + CONV_CH] = (dyb * c).astype(BF)
        dz_ref[:, ZA_W + CONV_CH:ZA_W + 2 * CONV_CH] = (du * xb).astype(BF)
        dz_ref[:, ZA_W + 2 * CONV_CH:c0] = (du * gc).astype(BF)
        row = lax.broadcasted_iota(jnp.int32, (HALO, CONV_CH), 0)
        upd = jnp.zeros((HALO, CONV_CH), F32)
        for t, uu in enumerate((u2, u1, u)):
            upd = jnp.where(row == t, jnp.sum(dcv * uu, axis=0, keepdims=True), upd)
        dcw_ref[...] += upd

    return pl.pallas_call(
        body, grid=(nb,), name="dz_assemble", in_specs=in_specs,
        out_specs=[_rows(TQ, IN_WIDTH), _whole((HALO, CONV_CH))],
        out_shape=[jax.ShapeDtypeStruct((s, IN_WIDTH), BF), jax.ShapeDtypeStruct((HALO, CONV_CH), F32)],
        compiler_params=_cparams("arbitrary"),
    )(*args)


def _qkv_bwd(dz, dx1, x, g, w_all, l, tb):
    s, d = x.shape

    def body(dz_ref, dx1_ref, x_ref, g_ref, w_ref, dx_ref, dg_ref):
        i = pl.program_id(0)

        @pl.when(i == 0)
        def _():
            dg_ref[...] = jnp.zeros_like(dg_ref)

        n = IN_WIDTH // N_CHIPS
        dh = _nt(dz_ref[:, 0:n], w_ref[0])
        for k in range(1, N_CHIPS):
            dh = dh + _nt(dz_ref[:, k * n:(k + 1) * n], w_ref[k])
        xv = x_ref[...]
        r = _rms_scale(xv)
        xhat = xv * r
        dg_ref[...] += jnp.sum(dh * xhat, axis=0, keepdims=True)
        dx_ref[...] = dx1_ref[...] + _norm_bwd(dh * g_ref[...], xhat, r)

    return pl.pallas_call(
        body, grid=(s // tb,), name="qkv_bwd",
        in_specs=[_rows(tb, IN_WIDTH), _rows(tb, d), _rows(tb, d), _whole((1, d)),
                  _layer((N_CHIPS, d, IN_WIDTH // N_CHIPS), l)],
        out_specs=[_rows(tb, d), _whole((HALO, d))],
        out_shape=[jax.ShapeDtypeStruct((s, d), F32), jax.ShapeDtypeStruct((HALO, d), F32)],
        compiler_params=_cparams("arbitrary"),
    )(dz, dx1, x, g, w_all)


def _tile_rows(rows):
    return jnp.pad(rows, ((0, HALO - rows.shape[0]), (0, 0)))


def _local_step(x, tgt, fetch, ff, sinks, g_mix, g_group, g_mlp, g_final, emit):
    s, d = x.shape
    depth = g_mix.shape[0]
    tb = min(512, s)
    tf = ff // N_CHIPS
    saved = []
    for l in range(depth):
        w_in, _, _, _, conv_w = fetch(0, l, x)
        cw = _tile_rows(conv_w[l])
        sk = jnp.pad(sinks[l].reshape(1, N_HEADS), ((0, HALO - 1), (0, 128 - N_HEADS)))
        h, za, zb, zc = _qkv_fwd(x, g_mix[l][None], w_in, l, tb)
        state = None
        for p, dil in enumerate(DILATIONS):
            state = _attn_fwd(za, dil, A_WIDTH, 1, 2, 1, A_MAX_DIST, state, None, p == len(DILATIONS) - 1,
                              "attn_a_fwd_%d" % dil)
        ya, lse_a = state
        yc, lse_c = _attn_fwd(zc, 1, C_KV_WIDTH, 3, 4, C_GROUP, C_MAX_DIST, None, sk, True, "attn_c_fwd")
        w_in, w_o, w1, w2, _ = fetch(1, l, yc)
        x1, yb = _mix_fwd(x, ya, yc, zb, cw, g_group[l][None], w_o, l, tb)
        x2, h2, ap = _mlp_fwd(x1, g_mlp[l][None], w1, w2, l, tb, tf)
        saved.append((x, h, za, zb, zc, ya, lse_a, yc, lse_c, yb, x1, h2, ap, cw, sk))
        x = x2
    dx, loss_tile, dg_final = _loss_head(x, g_final[None], tgt, tb)
    grads = [None] * depth
    tok = jnp.zeros((), F32)
    for l in reversed(range(depth)):
        x0, h, za, zb, zc, ya, lse_a, yc, lse_c, yb, x1, h2, ap, cw, sk = saved[l]
        dx1, dap, dg_mlp = _mlp_bwd(dx, x1, ap, g_mlp[l][None] + tok, w1, w2, l, tb, tf)
        tok = emit(l, 3, _wgrad(ap, dx, min(1024, ff), d, tb, "wgrad_ff_out", relu2=True))
        tok = tok + emit(l, 2, _wgrad(h2, dap, d, min(1024, ff), tb, "wgrad_ff_in"))
        n, dya, dyc, dd_a, dd_c, dyb, dg_group = _mix_bwd(dx1, ya, yb, yc, g_group[l][None] + tok, w_o, l, tb)
        tok = emit(l, 1, _wgrad(n, dx1, MIX_WIDTH, d, tb, "wgrad_o"))
        cw = cw + tok
        parts_a = [_attn_bwd(za, dya, lse_a, dd_a, dil, A_WIDTH, 1, 2, 1, A_MAX_DIST, None, "attn_a_bwd_%d" % dil)
                   for dil in DILATIONS]
        *parts_c, dsink = _attn_bwd(zc, dyc, lse_c, dd_c, 1, C_KV_WIDTH, 3, 4, C_GROUP, C_MAX_DIST, sk, "attn_c_bwd")
        dz, dcw = _dz_assemble(parts_a, parts_c, dyb, zb, cw)
        dx, dg_mix = _qkv_bwd(dz, dx1, x0, g_mix[l][None], w_in, l, tb)
        tok = emit(l, 0, _wgrad(h, dz, d, IN_WIDTH // 4, tb, "wgrad_in"))
        grads[l] = (dcw, dsink, dg_mix, dg_group, dg_mlp)
    return loss_tile, dx, grads, dg_final


ANY = pl.BlockSpec(memory_space=pl.ANY)
SHARD_AXES = (2, 1, 2, 1)
N_BIG = len(SHARD_AXES)
N_CHIPS = 4
N_DEV = 8


def _mesh_pos():
    return lax.axis_index("x"), lax.axis_index("y"), lax.axis_index("c")


def _flip(v, bit):
    return 1 - v if bit else v


def _place_shard(shard, chip_arr, name):
    _, rows, cols = shard.shape
    tr = min(256, rows)

    def body(chip_ref, x_ref, o_ref):
        o_ref[...] = x_ref[...].astype(BF)

    return pl.pallas_call(
        body, name=name,
        grid_spec=pltpu.PrefetchScalarGridSpec(
            num_scalar_prefetch=1, grid=(2, rows // tr),
            in_specs=[pl.BlockSpec((None, tr, cols), lambda l, i, chip: (l, i, 0))],
            out_specs=pl.BlockSpec((None, None, tr, cols), lambda l, i, chip: (l, chip[0], i, 0))),
        out_shape=jax.ShapeDtypeStruct((2, N_CHIPS, rows, cols), BF),
        compiler_params=_cparams("parallel", "parallel"),
    )(chip_arr, shard)


HBM = pl.BlockSpec(memory_space=pltpu.HBM)
SEM = pl.BlockSpec(memory_space=pltpu.SEMAPHORE)
EFFECT = pltpu.SideEffectType.DATAFLOW_SIDE_EFFECTING

GATHER_GROUPS = (((0, 0),), ((1, 0), (2, 0), (3, 0)), ((0, 1),), ((1, 1), (2, 1), (3, 1)))
GATHER_STARTS = ((0, 1), (2, 3))


def _gather_copies(arrs, group, send_sems, recv_sems):
    x, y, c = _mesh_pos()
    me = 2 * x + y
    out = []
    for i, (w, layer) in enumerate(group):
        mine = arrs[w].at[layer, me]
        for j, (qx, qy) in enumerate([(1 - x, y), (x, 1 - y), (1 - x, 1 - y)]):
            landed = arrs[w].at[layer, 2 * qx + qy]
            out.append(tuple(pltpu.make_async_remote_copy(
                src_ref=piece, dst_ref=piece, send_sem=send_sems.at[i * 3 + j], recv_sem=recv_sems.at[i * 3 + j],
                device_id=(qx, qy, c), device_id_type=MESH) for piece in (mine, landed)))
    return out


def _conv_copies(conv_src, conv_dst, send_sems, recv_sems):
    x, y, c = _mesh_pos()
    out = []
    for j, (qx, qy) in enumerate([(1 - x, y), (x, 1 - y), (1 - x, 1 - y)]):
        out.append(tuple(pltpu.make_async_remote_copy(
            src_ref=conv_src, dst_ref=conv_dst.at[q], send_sem=send_sems.at[j], recv_sem=recv_sems.at[j],
            device_id=(qx, qy, c), device_id_type=MESH) for q in (2 * x + y, 2 * qx + qy)))
    return out


def _gather_start(groups, arrs, conv, name):
    n_sems = 2 * (len(groups) + (conv is not None))

    def body(*refs):
        mats = refs[:N_BIG]
        sems = refs[n_op:n_op + n_sems]
        if conv is not None:
            for cp, _ in _conv_copies(refs[N_BIG], refs[N_BIG + 1], sems[-2], sems[-1]):
                cp.start()
        for k, g in enumerate(groups):
            for cp, _ in _gather_copies(mats, GATHER_GROUPS[g], sems[2 * k], sems[2 * k + 1]):
                cp.start()

    sem_shapes = []
    for n in [len(GATHER_GROUPS[g]) for g in groups] + ([1] if conv is not None else []):
        sem_shapes += [pltpu.SemaphoreType.DMA((3 * n,))] * 2
    operands = list(arrs) + ([] if conv is None else list(conv))
    n_op = len(operands)
    res = pl.pallas_call(
        body, name=name,
        out_shape=tuple(sem_shapes) + tuple(pltpu.HBM(a.shape, a.dtype) for a in operands),
        in_specs=(HBM,) * n_op, out_specs=(SEM,) * n_sems + (HBM,) * n_op,
        input_output_aliases={i: n_sems + i for i in range(n_op)},
        compiler_params=pltpu.CompilerParams(has_side_effects=EFFECT),
    )(*[pltpu.with_memory_space_constraint(a, pltpu.HBM) for a in operands])
    return res[:n_sems], list(res[n_sems:])


def _gather_wait(k, sems, arrs, conv, after, name):
    group = GATHER_GROUPS[k]
    mats = sorted({w for w, _ in group})
    n_conv = 0 if conv is None else 2

    def body(*refs):
        local = refs[:len(mats)]
        arrs_ref = [None] * N_BIG
        for w, ref in zip(mats, local):
            arrs_ref[w] = ref
        pos = len(mats) + n_conv
        copies = _gather_copies(arrs_ref, group, refs[pos], refs[pos + 1])
        if conv is not None:
            copies += _conv_copies(refs[len(mats)], refs[len(mats) + 1], refs[pos + 2], refs[pos + 3])
        for send, recv in copies:
            recv.wait_recv()
            send.wait_send()

    operands = [arrs[w] for w in mats] + ([] if conv is None else [conv[1], conv[2]])
    sem_ops = list(sems) + ([] if conv is None else list(conv[0]))
    n_op = len(operands)
    res = pl.pallas_call(
        body, name=name, out_shape=tuple(pltpu.HBM(a.shape, a.dtype) for a in operands),
        in_specs=(HBM,) * n_op + (SEM,) * len(sem_ops) + (ANY,), out_specs=(HBM,) * n_op,
        input_output_aliases={i: i for i in range(n_op)},
        compiler_params=pltpu.CompilerParams(has_side_effects=EFFECT),
    )(*operands, *sem_ops, after)
    arrs = list(arrs)
    for w, a in zip(mats, res):
        arrs[w] = a
    return arrs, (res[-1] if conv is not None else None)


def _grad_shard(ref, w, chip, n):
    start = pl.multiple_of(chip * n, 128)
    if SHARD_AXES[w] == 2:
        return ref.at[:, pl.ds(start, n)]
    return ref.at[pl.ds(start, n), :]


def _slot_shape(g, w):
    shape = list(g.shape)
    shape[SHARD_AXES[w] - 1] //= N_CHIPS
    return (N_DEV - 1,) + tuple(shape)


def _scatter_copies(g_ref, land_ref, send_sems, recv_sems, layer, w):
    x, y, c = _mesh_pos()
    n = g_ref.shape[SHARD_AXES[w] - 1] // N_CHIPS
    out = []
    for r in range(1, N_DEV):
        tx, ty, tc = _flip(x, r & 4), _flip(y, r & 2), _flip(c, r & 1)
        cp = pltpu.make_async_remote_copy(
            src_ref=_grad_shard(g_ref, w, 2 * tx + ty, n), dst_ref=land_ref.at[r - 1], send_sem=send_sems.at[r - 1],
            recv_sem=recv_sems.at[r - 1], device_id=(tx, ty, tc), device_id_type=MESH)
        out.append((cp, (c != layer) if r & 1 else (c == layer)))
    return out


def _scatter_start(g, land, layer, w, name):
    def body(g_ref, land_ref, send_sems, recv_sems, g_thru, land_thru, token):
        for cp, mine in _scatter_copies(g_ref, land_ref, send_sems, recv_sems, layer, w):
            @pl.when(mine)
            def _():
                cp.start()
        token[...] = jnp.zeros_like(token)

    return pl.pallas_call(
        body, name=name,
        out_shape=(pltpu.SemaphoreType.DMA((N_DEV - 1,)), pltpu.SemaphoreType.DMA((N_DEV - 1,)),
                   pltpu.HBM(g.shape, g.dtype), pltpu.HBM(land.shape, land.dtype),
                   jax.ShapeDtypeStruct((HALO, 128), F32)),
        in_specs=(HBM, HBM), out_specs=(SEM, SEM, HBM, HBM, pl.BlockSpec(memory_space=pltpu.VMEM)),
        input_output_aliases={0: 2, 1: 3}, compiler_params=pltpu.CompilerParams(has_side_effects=EFFECT),
    )(pltpu.with_memory_space_constraint(g, pltpu.HBM), pltpu.with_memory_space_constraint(land, pltpu.HBM))


def _scatter_wait(started, land, after, w, name):
    def body(g0_ref, g1_ref, land_ref, ss0, rs0, ss1, rs1, after_ref, g0_out, g1_out, land_out):
        c = lax.axis_index("c")
        for layer, g_ref, ss, rs in ((0, g0_ref, ss0, rs0), (1, g1_ref, ss1, rs1)):
            for cp, mine in _scatter_copies(g_ref, land_ref, ss, rs, layer, w):
                @pl.when(mine)
                def _():
                    cp.wait_send()

                @pl.when(c == layer)
                def _():
                    cp.wait_recv()

    (ss0, rs0, g0), (ss1, rs1, g1) = started
    return pl.pallas_call(
        body, name=name,
        out_shape=(pltpu.HBM(g0.shape, g0.dtype), pltpu.HBM(g1.shape, g1.dtype), pltpu.HBM(land.shape, land.dtype)),
        in_specs=(HBM, HBM, HBM, SEM, SEM, SEM, SEM, ANY), out_specs=(HBM, HBM, HBM),
        input_output_aliases={0: 0, 1: 1, 2: 2}, compiler_params=pltpu.CompilerParams(has_side_effects=EFFECT),
    )(g0, g1, land, ss0, rs0, ss1, rs1, after)


def _sum_slots(g0, g1, slots, w, pos_arr, name):
    _, rows, cols = slots.shape
    tr = min(256, rows)
    nr = rows // tr
    if SHARD_AXES[w] == 2:
        own = pl.BlockSpec((tr, cols), lambda i, pos: (i, pos[0]))
    else:
        own = pl.BlockSpec((tr, cols), lambda i, pos: (pos[0] * nr + i, 0))

    def body(pos_ref, own0_ref, own1_ref, s_ref, o_ref):
        acc = jnp.where(pos_ref[1] == 0, own0_ref[...], own1_ref[...]).astype(F32)
        for r in range(N_DEV - 1):
            acc = acc + s_ref[r].astype(F32)
        o_ref[...] = acc

    return pl.pallas_call(
        body, name=name,
        grid_spec=pltpu.PrefetchScalarGridSpec(
            num_scalar_prefetch=1, grid=(nr,),
            in_specs=[own, own, pl.BlockSpec((N_DEV - 1, tr, cols), lambda i, pos: (0, i, 0))],
            out_specs=pl.BlockSpec((tr, cols), lambda i, pos: (i, 0))),
        out_shape=jax.ShapeDtypeStruct((rows, cols), F32), compiler_params=_cparams("parallel"),
    )(pos_arr, g0, g1, slots)


def _swap_layers(halves):
    def body(*refs):
        srcs, dsts = refs[:N_BIG], refs[N_BIG:2 * N_BIG]
        send_sems, recv_sems = refs[2 * N_BIG:]
        x, y, c = _mesh_pos()
        sends = [pltpu.make_async_remote_copy(src_ref=srcs[w], dst_ref=dsts[w], send_sem=send_sems.at[w],
                                              recv_sem=recv_sems.at[w], device_id=(x, y, 1 - c), device_id_type=MESH)
                 for w in range(N_BIG)]
        for cp in sends:
            cp.start()
        for cp in sends:
            cp.wait_recv()
        for cp in sends:
            cp.wait_send()

    return pl.pallas_call(
        body, name="swap_layers", in_specs=[ANY] * N_BIG, out_specs=[ANY] * N_BIG,
        out_shape=[jax.ShapeDtypeStruct(h.shape, h.dtype) for h in halves],
        scratch_shapes=[pltpu.SemaphoreType.DMA((N_BIG,)), pltpu.SemaphoreType.DMA((N_BIG,))],
    )(*halves)


def _adamw_math(w, g, m, v):
    m = ADAM_B1 * m + (1.0 - ADAM_B1) * g
    v = ADAM_B2 * v + (1.0 - ADAM_B2) * jnp.square(g)
    m_hat = m / (1.0 - ADAM_B1 ** ADAM_STEP)
    v_hat = v / (1.0 - ADAM_B2 ** ADAM_STEP)
    delta = -ADAM_LR * (m_hat / (jnp.sqrt(v_hat) + ADAM_EPS) + ADAM_WD * w)
    return delta, m, v


def _adamw(w, g_own, g_other, m, v, pos_arr, name):
    shape = w.shape
    _, rows, cols = shape
    tr = min(256, rows)

    def body(pos_ref, w_ref, own_ref, other_ref, m_ref, v_ref, g_ref, d_ref, m2_ref, v2_ref):
        g = jnp.where(pl.program_id(0) == pos_ref[1], own_ref[...], other_ref[...])
        g_ref[...] = g
        d_ref[...], m2_ref[...], v2_ref[...] = _adamw_math(w_ref[...], g, m_ref[...], v_ref[...])

    full = pl.BlockSpec((None, tr, cols), lambda l, i, pos: (l, i, 0))
    half = pl.BlockSpec((tr, cols), lambda l, i, pos: (i, 0))
    return pl.pallas_call(
        body, name=name,
        grid_spec=pltpu.PrefetchScalarGridSpec(
            num_scalar_prefetch=1, grid=(2, rows // tr),
            in_specs=[full, half, half, full, full], out_specs=[full] * 4),
        out_shape=[jax.ShapeDtypeStruct(shape, F32)] * 4, compiler_params=_cparams("parallel", "parallel"),
    )(pos_arr, w, g_own, g_other, m, v)


def _small_sync(part, w, m, v):
    rows, cols = part.shape

    def body(p_ref, w_ref, m_ref, v_ref, g_ref, d_ref, m2_ref, v2_ref, slots, send_sems, recv_sems):
        x, y, c = _mesh_pos()
        me = 4 * x + 2 * y + c
        slots[me] = p_ref[...]
        sends = []
        for r in range(1, N_DEV):
            to = (_flip(x, r & 4), _flip(y, r & 2), _flip(c, r & 1))
            sends.append(pltpu.make_async_remote_copy(
                src_ref=p_ref, dst_ref=slots.at[me], send_sem=send_sems.at[r - 1], recv_sem=recv_sems.at[r - 1],
                device_id=to, device_id_type=MESH))
        for cp in sends:
            cp.start()
        for cp in sends:
            cp.wait_recv()
        for cp in sends:
            cp.wait_send()
        g = slots[0]
        for i in range(1, N_DEV):
            g = g + slots[i]
        g_ref[...] = g
        d_ref[...], m2_ref[...], v2_ref[...] = _adamw_math(w_ref[...], g, m_ref[...], v_ref[...])

    vm = pl.BlockSpec(memory_space=pltpu.VMEM)
    return pl.pallas_call(
        body, name="small_sync", in_specs=[vm] * 4, out_specs=[vm] * 4,
        out_shape=[jax.ShapeDtypeStruct((rows, cols), F32)] * 4,
        scratch_shapes=[pltpu.VMEM((N_DEV, rows, cols), F32), pltpu.SemaphoreType.DMA((N_DEV - 1,)),
                        pltpu.SemaphoreType.DMA((N_DEV - 1,))],
    )(part, w, m, v)


def _pack_small(d, g_mix, g_group, g_mlp, g_final, conv_full, sinks, scalar):
    def part(rows):
        return jnp.pad(rows, ((0, HALO - rows.shape[0]), (0, d - rows.shape[1])))
    return jnp.concatenate([part(g_mix), part(g_group), part(g_mlp), part(g_final[None]),
                            part(conv_full.reshape(6, CONV_CH)), part(sinks.reshape(2, N_HEADS)),
                            part(scalar.reshape(1, 1))], axis=0)


def _unpack_small(p, dm):
    return (p[0:2, :dm], p[8:10, :MIX_WIDTH], p[16:18, :dm], p[24, :dm], p[32:38, :CONV_CH].reshape(2, 3, CONV_CH),
            p[40:42, :N_HEADS].reshape(2, 2, C_GROUP), p[48, 0])


def kernel(x, w_in, conv_w, sinks, g_mix, g_group, w_o, g_mlp, w_ff_in, w_ff_out, g_final, loss_target, m_w_in, m_conv_w, m_sinks, m_g_mix, m_g_group, m_w_o, m_g_mlp, m_w_ff_in, m_w_ff_out, m_g_final, v_w_in, v_conv_w, v_sinks, v_g_mix, v_g_group, v_w_o, v_g_mlp, v_w_ff_in, v_w_ff_out, v_g_final):
    d = max(x.shape[2], MIX_WIDTH)
    chip = 2 * lax.axis_index("x") + lax.axis_index("y")
    conv_n = conv_w.shape[2]

    pos_arr = jnp.stack([chip, lax.axis_index("c")]).astype(jnp.int32)
    placed = [_place_shard(w, pos_arr[:1], "place_shard_%d" % i)
              for i, w in enumerate((w_in, w_o, w_ff_in, w_ff_out))]
    conv_tile = jnp.pad(conv_w.reshape(6, conv_n), ((0, HALO - 6), (0, 128 - conv_n)))
    sems0, thru = _gather_start(GATHER_STARTS[0], placed,
                                (conv_tile, lax.empty((N_CHIPS,) + conv_tile.shape, conv_tile.dtype)), "gather_start_0")
    full = {"arrs": thru[:N_BIG], "conv": None, "sems": list(sems0[:4])}

    def fetch(stage, layer, after):
        k = 2 * layer + stage
        sems = full["sems"][2 * k:2 * k + 2]
        if k == 0:
            full["arrs"], land = _gather_wait(0, sems, full["arrs"], (sems0[-2:], thru[N_BIG], thru[N_BIG + 1]),
                                              after, "gather_wait_0")
            conv_all = lax.dynamic_update_slice(land, conv_tile[None], (chip, 0, 0))
            full["conv"] = conv_all[:, :6, :conv_n].reshape(N_CHIPS, 2, 3, conv_n).transpose(1, 2, 0, 3).reshape(
                2, 3, CONV_CH)
        else:
            full["arrs"], _ = _gather_wait(k, sems, full["arrs"], None, after, "gather_wait_%d" % k)
        if k == 1:
            sems1, full["arrs"] = _gather_start(GATHER_STARTS[1], full["arrs"], None, "gather_start_1")
            full["sems"] += list(sems1)
        return (*full["arrs"], full["conv"])

    lands, started = [None] * N_BIG, {}

    def emit(layer, w, g):
        if lands[w] is None:
            lands[w] = lax.empty(_slot_shape(g, w), g.dtype)
        *started[layer, w], lands[w], token = _scatter_start(g, lands[w], layer, w, "scatter_start_%d_%d" % (layer, w))
        return token[0, 0]

    loss_tile, dx, grads, dg_final = _local_step(x[0], loss_target[0], fetch, w_ff_in.shape[2] * N_CHIPS,
                                                 sinks, g_mix, g_group, g_mlp, g_final, emit)

    own = []
    for w in range(N_BIG):
        g0, g1, slots = _scatter_wait((started[0, w], started[1, w]), lands[w], dx, w, "scatter_wait_%d" % w)
        own.append(_sum_slots(g0, g1, slots, w, pos_arr, "sum_slots_%d" % w))
    other = _swap_layers(own)

    def both(i):
        return jnp.stack([grads[0][i][0], grads[1][i][0]])
    dconv = jnp.stack([grads[0][0][:3], grads[1][0][:3]])
    dsinks = jnp.stack([grads[0][1][0, :N_HEADS], grads[1][1][0, :N_HEADS]])
    part = _pack_small(d, both(2), both(3), both(4), dg_final[0], dconv, dsinks, loss_tile[0, 0])

    def spread(shard):
        return lax.dynamic_update_slice(jnp.zeros((2, 3, CONV_CH), F32), shard, (0, 0, chip * conv_n))
    zero = jnp.zeros((), F32)
    packs = [_pack_small(d, a, b, c_, e, spread(f), g_, zero) for a, b, c_, e, f, g_ in (
        (g_mix, g_group, g_mlp, g_final, conv_w, sinks),
        (m_g_mix, m_g_group, m_g_mlp, m_g_final, m_conv_w, m_sinks),
        (v_g_mix, v_g_group, v_g_mlp, v_g_final, v_conv_w, v_sinks))]
    small = [_unpack_small(p, x.shape[2]) for p in _small_sync(part, *packs)]

    def shard_of(full):
        return lax.dynamic_slice(full, (0, 0, chip * conv_n), (2, 3, conv_n))
    small = [(s[0], s[1], s[2], s[3], shard_of(s[4]), s[5], s[6]) for s in small]
    loss = small[0][6]

    big = [_adamw(w, own[i], other[i], m, v, pos_arr, "adamw_%d" % i) for i, (w, m, v) in enumerate((
        (w_in, m_w_in, v_w_in), (w_o, m_w_o, v_w_o), (w_ff_in, m_w_ff_in, v_w_ff_in),
        (w_ff_out, m_w_ff_out, v_w_ff_out)))]

    def ordered(kind):
        b = [big[i][kind] for i in range(N_BIG)]
        s = small[kind]
        return [b[0], s[4], s[5], s[0], s[1], b[1], s[2], b[2], b[3], s[3]]

    return (loss, dx[None], *ordered(0), *ordered(1), *ordered(2), *ordered(3))
```

```python
import functools

import jax
import jax.numpy as jnp
from jax import lax
from jax.experimental import pallas as pl
from jax.experimental.pallas import tpu as pltpu

HEAD_DIM = 64
N_HEADS = 6
C_GROUP = 3
A_WIDTH = N_HEADS * HEAD_DIM
C_KV_WIDTH = 2 * HEAD_DIM
CONV_CH = 256
ZA_W = 3 * A_WIDTH
ZB_W = 3 * CONV_CH
ZC_W = A_WIDTH + 2 * C_KV_WIDTH
IN_WIDTH = ZA_W + ZB_W + ZC_W
MIX_WIDTH = A_WIDTH + CONV_CH + A_WIDTH
DILATIONS = (1, 4, 16)
A_MAX_DIST = 128
C_MAX_DIST = 127
TQ = 128
EPS = 1e-6
SCALE = HEAD_DIM ** -0.5
NEG = -1e30
HALO = 8

ADAM_LR = 0.001
ADAM_B1 = 0.9
ADAM_B2 = 0.999
ADAM_EPS = 1e-08
ADAM_WD = 0.01
ADAM_STEP = 10

BF = jnp.bfloat16
F32 = jnp.float32
MESH = pl.DeviceIdType.MESH
VMEM_LIMIT = 56 * 1024 * 1024


def _cparams(*sem):
    return pltpu.CompilerParams(dimension_semantics=sem, vmem_limit_bytes=VMEM_LIMIT)


def _nt(a, b):
    return lax.dot_general(a, b, (((1,), (1,)), ((), ())), preferred_element_type=F32)


def _tn(a, b):
    return lax.dot_general(a, b, (((0,), (0,)), ((), ())), preferred_element_type=F32)


def _nn(a, b):
    return jnp.dot(a, b, preferred_element_type=F32)


def _rows(tb, w):
    return pl.BlockSpec((tb, w), lambda i: (i, 0))


def _whole(shape):
    return pl.BlockSpec(shape, lambda *_: (0,) * len(shape))


def _layer(shape, l):
    return pl.BlockSpec((None,) + shape, lambda *_: (l,) + (0,) * len(shape))


def _rms_scale(v):
    return lax.rsqrt(jnp.mean(v * v, axis=-1, keepdims=True) + EPS)


def _norm_bwd(dxhat, xhat, r):
    return r * (dxhat - xhat * jnp.mean(dxhat * xhat, axis=-1, keepdims=True))


def _qkv_fwd(x, g, w_all, l, tb):
    s, d = x.shape

    def body(x_ref, g_ref, w_ref, h_ref, za_ref, zb_ref, zc_ref):
        xv = x_ref[...]
        h = ((xv * _rms_scale(xv)) * g_ref[...]).astype(BF)
        h_ref[...] = h
        z = jnp.concatenate([_nn(h, w_ref[k]) for k in range(N_CHIPS)], axis=1)
        za_ref[...] = z[:, :ZA_W]
        zb_ref[...] = z[:, ZA_W:ZA_W + ZB_W]
        zc_ref[...] = z[:, ZA_W + ZB_W:]

    return pl.pallas_call(
        body, grid=(s // tb,), name="qkv_fwd",
        in_specs=[_rows(tb, d), _whole((1, d)), _layer((N_CHIPS, d, IN_WIDTH // N_CHIPS), l)],
        out_specs=[_rows(tb, d), _rows(tb, ZA_W), _rows(tb, ZB_W), _rows(tb, ZC_W)],
        out_shape=[jax.ShapeDtypeStruct((s, d), BF), jax.ShapeDtypeStruct((s, ZA_W), F32),
                   jax.ShapeDtypeStruct((s, ZB_W), F32), jax.ShapeDtypeStruct((s, ZC_W), F32)],
        compiler_params=_cparams("parallel"),
    )(x, g, w_all)


N_STRIPS = 16


def _strips(a):
    s, w = a.shape
    return a.reshape(4, 4, s // N_STRIPS, w)


def _p_grid(s, dil):
    na = s // N_STRIPS
    return {16: (4, 4, na // TQ), 4: (4, na // 32), 1: (na // 8,)}[dil]


def _p_spec(dil, cw, col, prev=False):
    def blk(j):
        return jnp.maximum(j - 1, 0) if prev else j
    if dil == 16:
        return pl.BlockSpec((None, None, TQ, cw), lambda f, e, j: (f, e, blk(j), col))
    if dil == 4:
        return pl.BlockSpec((None, 4, 32, cw), lambda f, j: (f, 0, blk(j), col))
    return pl.BlockSpec((4, 4, 8, cw), lambda j: (0, 0, blk(j), col))


def _block_pos(i, dil):
    if dil == 16:
        return i
    if dil == 4:
        return 4 * (i % 32) + i // 32
    return 16 * (i % 8) + 4 * ((i // 8) % 4) + i // 32


def _band_mask(b, dil, max_dist):
    qi = _block_pos(lax.broadcasted_iota(jnp.int32, (TQ, 2 * TQ), 0), dil)
    col = lax.broadcasted_iota(jnp.int32, (TQ, 2 * TQ), 1)
    cur = col >= TQ
    dist = qi - _block_pos(col % TQ, dil) + jnp.where(cur, 0, TQ)
    return (dist >= 0) & (dist <= max_dist) & (cur | (b > 0))


def _hs(h):
    return slice(h * HEAD_DIM, (h + 1) * HEAD_DIM)


def _ld(ref, cols):
    v = ref[..., cols]
    return v.reshape(TQ, v.shape[-1])


def _st(ref, cols, val):
    ref[..., cols] = val.reshape(ref.shape[:-1] + (val.shape[-1],))


def _attn_fwd(z, dil, kw, kcol, vcol, n_rep, max_dist, state, sink, last, name):
    s, zw = z.shape
    grid = _p_grid(s, dil)
    have_state, have_sink = state is not None, sink is not None

    def body(*refs):
        q_ref, kp_ref, kc_ref, vp_ref, vc_ref = refs[:5]
        pos = 5
        if have_state:
            acc_in, m_in, l_in = refs[pos:pos + 3]
            pos += 3
        if have_sink:
            sink_ref = refs[pos]
            pos += 1
        outs = refs[pos:]
        mask = _band_mask(pl.program_id(len(grid) - 1), dil, max_dist)
        for h in range(N_HEADS):
            kh = h // n_rep
            q = _ld(q_ref, _hs(h)).astype(BF)
            k2 = jnp.concatenate([_ld(kp_ref, _hs(kh)), _ld(kc_ref, _hs(kh))], axis=0).astype(BF)
            v2 = jnp.concatenate([_ld(vp_ref, _hs(kh)), _ld(vc_ref, _hs(kh))], axis=0).astype(BF)
            sc = jnp.where(mask, _nt(q, k2) * SCALE, NEG)
            m_new = jnp.max(sc, axis=1, keepdims=True)
            if have_sink:
                sk = sink_ref[0:1, h:h + 1]
                m_new = jnp.maximum(m_new, sk)
            if have_state:
                m_old = _ld(m_in, slice(h * HEAD_DIM, h * HEAD_DIM + 1))
                m_new = jnp.maximum(m_new, m_old)
            p = jnp.exp(sc - m_new)
            l_new = jnp.sum(p, axis=1, keepdims=True)
            acc = _nn(p.astype(BF), v2)
            if have_state:
                alpha = jnp.exp(m_old - m_new)
                l_new = l_new + alpha * _ld(l_in, slice(h * HEAD_DIM, h * HEAD_DIM + 1))
                acc = acc + alpha * _ld(acc_in, _hs(h))
            if have_sink:
                l_new = l_new + jnp.exp(sk - m_new)
            if last:
                _st(outs[0], _hs(h), acc / l_new)
                _st(outs[1], _hs(h), jnp.broadcast_to(m_new + jnp.log(l_new), (TQ, HEAD_DIM)))
            else:
                _st(outs[0], _hs(h), acc)
                _st(outs[1], _hs(h), jnp.broadcast_to(m_new, (TQ, HEAD_DIM)))
                _st(outs[2], _hs(h), jnp.broadcast_to(l_new, (TQ, HEAD_DIM)))

    args = [_strips(z)] * 5
    in_specs = [_p_spec(dil, A_WIDTH, 0), _p_spec(dil, kw, kcol, True), _p_spec(dil, kw, kcol),
                _p_spec(dil, kw, vcol, True), _p_spec(dil, kw, vcol)]
    if have_state:
        args += [_strips(a) for a in state]
        in_specs += [_p_spec(dil, A_WIDTH, 0)] * 3
    if have_sink:
        args.append(sink)
        in_specs.append(_whole((HALO, 128)))
    n_out = 2 if last else 3
    res = pl.pallas_call(
        body, grid=grid, name=name, in_specs=in_specs,
        out_specs=[_p_spec(dil, A_WIDTH, 0)] * n_out,
        out_shape=[jax.ShapeDtypeStruct((4, 4, s // N_STRIPS, A_WIDTH), F32)] * n_out,
        compiler_params=_cparams(*(("parallel",) * len(grid))),
    )(*args)
    return [a.reshape(s, A_WIDTH) for a in res]


def _shift_down(v, n, halo):
    rows = v.shape[0]
    out = pltpu.roll(v, n, 0)
    row = lax.broadcasted_iota(jnp.int32, v.shape, 0)
    for t in range(n):
        out = jnp.where(row == t, halo[HALO - n + t:HALO - n + t + 1, :], out)
    return out


def _shift_up(v, n, halo):
    rows = v.shape[0]
    out = pltpu.roll(v, rows - n, 0)
    row = lax.broadcasted_iota(jnp.int32, v.shape, 0)
    for t in range(n):
        out = jnp.where(row == rows - n + t, halo[t:t + 1, :], out)
    return out


def _strip(v, b):
    return v[b % 4, b // 4]


def _conv_strips(zb, prev, cw):
    gb = [_strip(zb, b)[:, :CONV_CH] for b in range(N_STRIPS)]
    gc = [_strip(zb, b)[:, CONV_CH:2 * CONV_CH] for b in range(N_STRIPS)]
    xb = [_strip(zb, b)[:, 2 * CONV_CH:] for b in range(N_STRIPS)]
    u = [g * v for g, v in zip(gc, xb)]
    uh = prev[:, :, CONV_CH:2 * CONV_CH] * prev[:, :, 2 * CONV_CH:]
    wrapped = {14: _shift_down(u[14], 1, uh[2]), 15: _shift_down(u[15], 1, uh[3])}
    u1 = [u[b - 1] if b >= 1 else wrapped[15] for b in range(N_STRIPS)]
    u2 = [u[b - 2] if b >= 2 else wrapped[14 + b] for b in range(N_STRIPS)]
    c = [cw[0:1, :] * u2[b] + cw[1:2, :] * u1[b] + cw[2:3, :] * u[b] for b in range(N_STRIPS)]
    return gb, gc, xb, u, u1, u2, c


def _strip_rows(ta, w):
    return pl.BlockSpec((4, 4, ta, w), lambda i: (0, 0, i, 0))


def _prev_rows(ta, w):
    return pl.BlockSpec((4, None, HALO, w), lambda i: (0, 3, jnp.maximum(i * (ta // HALO) - 1, 0), 0))


def _next_rows(ta, w, nblk):
    return pl.BlockSpec((4, None, HALO, w),
                        lambda i: (0, 0, jnp.minimum((i + 1) * (ta // HALO), nblk * (ta // HALO) - 1), 0))


def _mix_fwd(x, ya, yc, zb, cw, gg, wo_all, l, tb):
    s, d = x.shape
    ta = tb // N_STRIPS

    def body(x_ref, ya_ref, yc_ref, zb_ref, zbp_ref, cw_ref, gg_ref, wo_ref, x1_ref, yb_ref):
        i = pl.program_id(0)
        prev = jnp.where(i > 0, zbp_ref[...], 0.0)
        gb, _, _, _, _, _, c = _conv_strips(zb_ref[...], prev, cw_ref[...])
        for b in range(N_STRIPS):
            yb_ref[b % 4, b // 4] = gb[b] * c[b]
        yb = yb_ref[...].reshape(tb, CONV_CH)
        ya, yc = ya_ref[...].reshape(tb, A_WIDTH), yc_ref[...].reshape(tb, A_WIDTH)
        n = jnp.concatenate([ya * _rms_scale(ya), yb * _rms_scale(yb), yc * _rms_scale(yc)], axis=1)
        n = (n * gg_ref[...]).astype(BF)
        x1 = x_ref[...].reshape(tb, d) + _nn(n, wo_ref[...].reshape(MIX_WIDTH, d))
        x1_ref[...] = x1.reshape(4, 4, ta, d)

    res = pl.pallas_call(
        body, grid=(s // tb,), name="mix_fwd",
        in_specs=[_strip_rows(ta, d), _strip_rows(ta, A_WIDTH), _strip_rows(ta, A_WIDTH), _strip_rows(ta, ZB_W),
                  _prev_rows(ta, ZB_W), _whole((HALO, CONV_CH)), _whole((1, MIX_WIDTH)),
                  _layer((N_CHIPS, MIX_WIDTH // N_CHIPS, d), l)],
        out_specs=[_strip_rows(ta, d), _strip_rows(ta, CONV_CH)],
        out_shape=[jax.ShapeDtypeStruct((4, 4, s // N_STRIPS, d), F32),
                   jax.ShapeDtypeStruct((4, 4, s // N_STRIPS, CONV_CH), F32)],
        compiler_params=_cparams("parallel"),
    )(_strips(x), _strips(ya), _strips(yc), _strips(zb), _strips(zb), cw, gg, wo_all)
    return res[0].reshape(s, d), res[1].reshape(s, CONV_CH)


def _mlp_fwd(x1, g, w1_all, w2_all, l, tb, tf):
    s, d = x1.shape
    ff = w1_all.shape[1] * w1_all.shape[3]
    nj = ff // tf

    def body(x_ref, g_ref, w1_ref, w2_ref, x2_ref, h2_ref, ap_ref, acc):
        j = pl.program_id(1)

        @pl.when(j == 0)
        def _():
            xv = x_ref[...]
            h2_ref[...] = ((xv * _rms_scale(xv)) * g_ref[...]).astype(BF)
            acc[...] = jnp.zeros_like(acc)

        ap = _nn(h2_ref[...], w1_ref[...])
        ap_ref[...] = ap.astype(BF)
        a = jnp.square(jnp.maximum(ap, 0.0)).astype(BF)
        acc[...] += _nn(a, w2_ref[...])

        @pl.when(j == nj - 1)
        def _():
            x2_ref[...] = x_ref[...] + acc[...]

    return pl.pallas_call(
        body, grid=(s // tb, nj), name="mlp_fwd",
        in_specs=[pl.BlockSpec((tb, d), lambda i, j: (i, 0)), _whole((1, d)),
                  pl.BlockSpec((None, None, d, tf), lambda i, j: (l, j, 0, 0)),
                  pl.BlockSpec((None, None, tf, d), lambda i, j: (l, j, 0, 0))],
        out_specs=[pl.BlockSpec((tb, d), lambda i, j: (i, 0)), pl.BlockSpec((tb, d), lambda i, j: (i, 0)),
                   pl.BlockSpec((tb, tf), lambda i, j: (i, j))],
        out_shape=[jax.ShapeDtypeStruct((s, d), F32), jax.ShapeDtypeStruct((s, d), BF),
                   jax.ShapeDtypeStruct((s, ff), BF)],
        scratch_shapes=[pltpu.VMEM((tb, d), F32)],
        compiler_params=_cparams("parallel", "arbitrary"),
    )(x1, g, w1_all, w2_all)


def _loss_head(x, g, tgt, tb):
    s, d = x.shape

    def body(x_ref, g_ref, t_ref, dx_ref, loss_ref, dg_ref):
        i = pl.program_id(0)

        @pl.when(i == 0)
        def _():
            loss_ref[...] = jnp.zeros_like(loss_ref)
            dg_ref[...] = jnp.zeros_like(dg_ref)

        xv = x_ref[...]
        r = _rms_scale(xv)
        xhat = xv * r
        err = xhat * g_ref[...] - t_ref[...]
        part = jnp.sum(jnp.mean(jnp.square(err), axis=-1, keepdims=True), axis=0, keepdims=True)
        loss_ref[...] += 0.5 * part
        dy = err * (1.0 / d)
        dg_ref[...] += jnp.sum(dy * xhat, axis=0, keepdims=True)
        dx_ref[...] = _norm_bwd(dy * g_ref[...], xhat, r)

    return pl.pallas_call(
        body, grid=(s // tb,), name="loss_head",
        in_specs=[_rows(tb, d), _whole((1, d)), _rows(tb, d)],
        out_specs=[_rows(tb, d), _whole((HALO, 128)), _whole((HALO, d))],
        out_shape=[jax.ShapeDtypeStruct((s, d), F32), jax.ShapeDtypeStruct((HALO, 128), F32),
                   jax.ShapeDtypeStruct((HALO, d), F32)],
        compiler_params=_cparams("arbitrary"),
    )(x, g, tgt)


def _mlp_bwd(dx2, x1, ap, g, w1_all, w2_all, l, tb, tf):
    s, d = x1.shape
    ff = ap.shape[1]
    nj = ff // tf

    def body(dx2_ref, x1_ref, ap_ref, g_ref, w1_ref, w2_ref, dx1_ref, dap_ref, dg_ref, acc):
        i, j = pl.program_id(0), pl.program_id(1)

        @pl.when((i == 0) & (j == 0))
        def _():
            dg_ref[...] = jnp.zeros_like(dg_ref)

        @pl.when(j == 0)
        def _():
            acc[...] = jnp.zeros_like(acc)

        da = _nt(dx2_ref[...].astype(BF), w2_ref[...])
        dap = (da * (2.0 * jnp.maximum(ap_ref[...].astype(F32), 0.0))).astype(BF)
        dap_ref[...] = dap
        acc[...] += _nt(dap, w1_ref[...])

        @pl.when(j == nj - 1)
        def _():
            xv = x1_ref[...]
            r = _rms_scale(xv)
            xhat = xv * r
            dh = acc[...]
            dg_ref[...] += jnp.sum(dh * xhat, axis=0, keepdims=True)
            dx1_ref[...] = dx2_ref[...] + _norm_bwd(dh * g_ref[...], xhat, r)

    return pl.pallas_call(
        body, grid=(s // tb, nj), name="mlp_bwd",
        in_specs=[pl.BlockSpec((tb, d), lambda i, j: (i, 0)), pl.BlockSpec((tb, d), lambda i, j: (i, 0)),
                  pl.BlockSpec((tb, tf), lambda i, j: (i, j)),
                  _whole((1, d)), pl.BlockSpec((None, None, d, tf), lambda i, j: (l, j, 0, 0)),
                  pl.BlockSpec((None, None, tf, d), lambda i, j: (l, j, 0, 0))],
        out_specs=[pl.BlockSpec((tb, d), lambda i, j: (i, 0)), pl.BlockSpec((tb, tf), lambda i, j: (i, j)),
                   _whole((HALO, d))],
        out_shape=[jax.ShapeDtypeStruct((s, d), F32), jax.ShapeDtypeStruct((s, ff), BF),
                   jax.ShapeDtypeStruct((HALO, d), F32)],
        scratch_shapes=[pltpu.VMEM((tb, d), F32)],
        compiler_params=_cparams("arbitrary", "arbitrary"),
    )(dx2, x1, ap, g, w1_all, w2_all)


def _wgrad(a, b, tm, tn, ts, name, relu2=False):
    s, m = a.shape
    n = b.shape[1]
    ns = s // ts

    def body(a_ref, b_ref, o_ref, acc):
        k = pl.program_id(2)

        @pl.when(k == 0)
        def _():
            acc[...] = jnp.zeros_like(acc)

        av = a_ref[...]
        if relu2:
            av = jnp.square(jnp.maximum(av.astype(F32), 0.0)).astype(BF)
        acc[...] += _tn(av, b_ref[...].astype(BF))

        @pl.when(k == ns - 1)
        def _():
            o_ref[...] = acc[...].astype(BF)

    return pl.pallas_call(
        body, grid=(m // tm, n // tn, ns), name=name,
        in_specs=[pl.BlockSpec((ts, tm), lambda i, j, k: (k, i)), pl.BlockSpec((ts, tn), lambda i, j, k: (k, j))],
        out_specs=pl.BlockSpec((tm, tn), lambda i, j, k: (i, j)),
        out_shape=jax.ShapeDtypeStruct((m, n), BF),
        scratch_shapes=[pltpu.VMEM((tm, tn), F32)],
        compiler_params=_cparams("parallel", "parallel", "arbitrary"),
    )(a, b)


def _mix_bwd(dx1, ya, yb, yc, gg, wo_all, l, tb):
    s, d = dx1.shape

    def body(dx_ref, ya_ref, yb_ref, yc_ref, gg_ref, wo_ref, n_ref, dya_ref, dyc_ref, da_ref, dc_ref, dyb_ref, dg_ref):
        i = pl.program_id(0)

        @pl.when(i == 0)
        def _():
            dg_ref[...] = jnp.zeros_like(dg_ref)

        dn = _nt(dx_ref[...].astype(BF), wo_ref[...].reshape(MIX_WIDTH, d))
        ys = [ya_ref[...], yb_ref[...], yc_ref[...]]
        rs = [_rms_scale(v) for v in ys]
        nhat = jnp.concatenate([v * r for v, r in zip(ys, rs)], axis=1)
        gg = gg_ref[...]
        n_ref[...] = (nhat * gg).astype(BF)
        dg_ref[...] += jnp.sum(dn * nhat, axis=0, keepdims=True)
        dnh = dn * gg
        bounds = [(0, A_WIDTH), (A_WIDTH, A_WIDTH + CONV_CH), (A_WIDTH + CONV_CH, MIX_WIDTH)]
        dys = [_norm_bwd(dnh[:, lo:hi], nhat[:, lo:hi], r) for (lo, hi), r in zip(bounds, rs)]
        dyb_ref[...] = dys[1]
        for dy, y, dy_ref, dd_ref in ((dys[0], ys[0], dya_ref, da_ref), (dys[2], ys[2], dyc_ref, dc_ref)):
            dy_ref[...] = dy
            t = dy * y
            for h in range(N_HEADS):
                dd_ref[:, _hs(h)] = jnp.broadcast_to(jnp.sum(t[:, _hs(h)], axis=1, keepdims=True), (tb, HEAD_DIM))

    return pl.pallas_call(
        body, grid=(s // tb,), name="mix_bwd",
        in_specs=[_rows(tb, d), _rows(tb, A_WIDTH), _rows(tb, CONV_CH), _rows(tb, A_WIDTH), _whole((1, MIX_WIDTH)),
                  _layer((N_CHIPS, MIX_WIDTH // N_CHIPS, d), l)],
        out_specs=[_rows(tb, MIX_WIDTH), _rows(tb, A_WIDTH), _rows(tb, A_WIDTH), _rows(tb, A_WIDTH),
                   _rows(tb, A_WIDTH), _rows(tb, CONV_CH), _whole((HALO, MIX_WIDTH))],
        out_shape=[jax.ShapeDtypeStruct((s, MIX_WIDTH), BF), jax.ShapeDtypeStruct((s, A_WIDTH), F32),
                   jax.ShapeDtypeStruct((s, A_WIDTH), F32), jax.ShapeDtypeStruct((s, A_WIDTH), F32),
                   jax.ShapeDtypeStruct((s, A_WIDTH), F32), jax.ShapeDtypeStruct((s, CONV_CH), F32),
                   jax.ShapeDtypeStruct((HALO, MIX_WIDTH), F32)],
        compiler_params=_cparams("arbitrary"),
    )(dx1, ya, yb, yc, gg, wo_all)


def _attn_bwd(z, dy, lse, dd, dil, kw, kcol, vcol, n_rep, max_dist, sink, name):
    s, zw = z.shape
    grid = _p_grid(s, dil)
    have_sink = sink is not None
    n_kv = N_HEADS // n_rep

    def body(*refs):
        q_ref, kp_ref, kc_ref, vp_ref, vc_ref, dy_ref, lse_ref, dd_ref = refs[:8]
        pos = 8
        if have_sink:
            sink_ref = refs[pos]
            pos += 1
        dq_ref, dkp_ref, dkc_ref, dvp_ref, dvc_ref = refs[pos:pos + 5]
        b = pl.program_id(len(grid) - 1)
        mask = _band_mask(b, dil, max_dist)
        if have_sink:
            dsink_ref = refs[pos + 5]

            @pl.when(b == 0)
            def _():
                dsink_ref[...] = jnp.zeros_like(dsink_ref)

            row = lax.broadcasted_iota(jnp.int32, (HALO, 128), 0)
            lane = lax.broadcasted_iota(jnp.int32, (HALO, 128), 1)
        for kh in range(n_kv):
            k2 = jnp.concatenate([_ld(kp_ref, _hs(kh)), _ld(kc_ref, _hs(kh))], axis=0).astype(BF)
            v2 = jnp.concatenate([_ld(vp_ref, _hs(kh)), _ld(vc_ref, _hs(kh))], axis=0).astype(BF)
            dk2 = jnp.zeros((2 * TQ, HEAD_DIM), F32)
            dv2 = jnp.zeros((2 * TQ, HEAD_DIM), F32)
            for h in range(kh * n_rep, (kh + 1) * n_rep):
                q = _ld(q_ref, _hs(h)).astype(BF)
                lse_h = _ld(lse_ref, slice(h * HEAD_DIM, h * HEAD_DIM + 1))
                dd_h = _ld(dd_ref, slice(h * HEAD_DIM, h * HEAD_DIM + 1))
                dyh = _ld(dy_ref, _hs(h)).astype(BF)
                sc = jnp.where(mask, _nt(q, k2) * SCALE, NEG)
                p = jnp.exp(sc - lse_h)
                dp = _nt(dyh, v2)
                ds = ((p * (dp - dd_h)) * SCALE).astype(BF)
                _st(dq_ref, _hs(h), _nn(ds, k2))
                dk2 = dk2 + _tn(ds, q)
                dv2 = dv2 + _tn(p.astype(BF), dyh)
                if have_sink:
                    sk = sink_ref[0:1, h:h + 1]
                    val = -jnp.sum(jnp.exp(sk - lse_h) * dd_h, axis=0, keepdims=True)
                    dsink_ref[...] += jnp.where((row == 0) & (lane == h), val, 0.0)
            _st(dkp_ref, _hs(kh), dk2[:TQ])
            _st(dkc_ref, _hs(kh), dk2[TQ:])
            _st(dvp_ref, _hs(kh), dv2[:TQ])
            _st(dvc_ref, _hs(kh), dv2[TQ:])

    args = [_strips(z)] * 5 + [_strips(a) for a in (dy, lse, dd)]
    in_specs = [_p_spec(dil, A_WIDTH, 0), _p_spec(dil, kw, kcol, True), _p_spec(dil, kw, kcol),
                _p_spec(dil, kw, vcol, True), _p_spec(dil, kw, vcol)] + [_p_spec(dil, A_WIDTH, 0)] * 3
    out_specs = [_p_spec(dil, A_WIDTH, 0)] + [_p_spec(dil, kw, 0)] * 4
    na = s // N_STRIPS
    out_shape = [jax.ShapeDtypeStruct((4, 4, na, A_WIDTH), F32)] + [jax.ShapeDtypeStruct((4, 4, na, kw), F32)] * 4
    if have_sink:
        args.append(sink)
        in_specs.append(_whole((HALO, 128)))
        out_specs.append(_whole((HALO, 128)))
        out_shape.append(jax.ShapeDtypeStruct((HALO, 128), F32))
    res = pl.pallas_call(
        body, grid=grid, name=name, in_specs=in_specs, out_specs=out_specs, out_shape=out_shape,
        compiler_params=_cparams(*(("arbitrary",) * len(grid))),
    )(*args)
    outs = [res[0].reshape(s, A_WIDTH)] + [a.reshape(s, kw) for a in res[1:5]]
    return outs + list(res[5:])


DZ_TA = 16


def _dz_assemble(parts_a, parts_c, dyb, zb, cw):
    s = zb.shape[0]
    na = s // N_STRIPS
    nb = na // DZ_TA

    def ahead(w, k):
        return pl.BlockSpec((4, 4, DZ_TA, w), lambda i: (0, 0, jnp.minimum(i + k, nb - 1), 0))

    args, in_specs = [], []
    for dil, (dq, dkp, dkc, dvp, dvc) in zip(DILATIONS + (1,), parts_a + [parts_c]):
        w = dkp.shape[1]
        here = _strip_rows(DZ_TA, w)
        if dil == 1:
            args += [dq, dkp, dkp, dkc, dvp, dvp, dvc]
            in_specs += [_strip_rows(DZ_TA, A_WIDTH), here, ahead(w, 1), here, here, ahead(w, 1), here]
        else:
            k = 8 * dil // DZ_TA
            args += [dq, dkp, dkc, dvp, dvc]
            in_specs += [_strip_rows(DZ_TA, A_WIDTH), ahead(w, k), here, ahead(w, k), here]
    n_att = len(args)
    args = [_strips(a) for a in args] + [_strips(dyb), _strips(dyb), _strips(zb), _strips(zb), _strips(zb), cw]
    in_specs += [_strip_rows(DZ_TA, CONV_CH), _next_rows(DZ_TA, CONV_CH, nb), _strip_rows(DZ_TA, ZB_W),
                 _prev_rows(DZ_TA, ZB_W), _next_rows(DZ_TA, ZB_W, nb), _whole((HALO, CONV_CH))]

    def body(*refs):
        att = list(refs[:n_att])
        dyb_ref, dybn_ref, zb_ref, zbp_ref, zbn_ref, cw_ref, dz_ref, dcw_ref = refs[n_att:]
        i = pl.program_id(0)

        @pl.when(i == 0)
        def _():
            dcw_ref[...] = jnp.zeros_like(dcw_ref)

        def shifted(dil):
            if dil == 1:
                dq_r, kp0, kp1, dkc_r, vp0, vp1, dvc_r = [att.pop(0) for _ in range(7)]
                live = i + 1 < nb
                half = DZ_TA // 2
                dkp = jnp.concatenate([kp0[:, :, half:, :], jnp.where(live, kp1[:, :, :half, :], 0.0)], axis=2)
                dvp = jnp.concatenate([vp0[:, :, half:, :], jnp.where(live, vp1[:, :, :half, :], 0.0)], axis=2)
            else:
                dq_r, dkp_r, dkc_r, dvp_r, dvc_r = [att.pop(0) for _ in range(5)]
                live = i + 8 * dil // DZ_TA < nb
                dkp, dvp = jnp.where(live, dkp_r[...], 0.0), jnp.where(live, dvp_r[...], 0.0)
            return dq_r[...], dkc_r[...] + dkp, dvc_r[...] + dvp

        dq, dk, dv = shifted(DILATIONS[0])
        for dil in DILATIONS[1:]:
            dq2, dk2, dv2 = shifted(dil)
            dq, dk, dv = dq + dq2, dk + dk2, dv + dv2
        dz_ref[:, :, :, 0:A_WIDTH] = dq.astype(BF)
        dz_ref[:, :, :, A_WIDTH:2 * A_WIDTH] = dk.astype(BF)
        dz_ref[:, :, :, 2 * A_WIDTH:ZA_W] = dv.astype(BF)
        dq, dk, dv = shifted(1)
        c0 = ZA_W + ZB_W
        dz_ref[:, :, :, c0:c0 + A_WIDTH] = dq.astype(BF)
        dz_ref[:, :, :, c0 + A_WIDTH:c0 + A_WIDTH + C_KV_WIDTH] = dk.astype(BF)
        dz_ref[:, :, :, c0 + A_WIDTH + C_KV_WIDTH:IN_WIDTH] = dv.astype(BF)

        cw = cw_ref[...]
        prev = jnp.where(i > 0, zbp_ref[...], 0.0)
        gb, gc, xb, u, u1, u2, c = _conv_strips(zb_ref[...], prev, cw)
        dyb = dyb_ref[...]
        dc = [_strip(dyb, b) * gb[b] for b in range(N_STRIPS)]
        dcn = jnp.where(i + 1 < nb, dybn_ref[...] * zbn_ref[:, :, :CONV_CH], 0.0)
        wrapped = [_shift_up(dc[0], 1, dcn[0]), _shift_up(dc[1], 1, dcn[1])]
        upd = [jnp.zeros((1, CONV_CH), F32)] * 3
        for b in range(N_STRIPS):
            dc1 = dc[b + 1] if b + 1 < N_STRIPS else wrapped[0]
            dc2 = dc[b + 2] if b + 2 < N_STRIPS else wrapped[b + 2 - N_STRIPS]
            du = cw[2:3, :] * dc[b] + cw[1:2, :] * dc1 + cw[0:1, :] * dc2
            f, e = b % 4, b // 4
            dz_ref[f, e, :, ZA_W:ZA_W + CONV_CH] = (_strip(dyb, b) * c[b]).astype(BF)
            dz_ref[f, e, :, ZA_W + CONV_CH:ZA_W + 2 * CONV_CH] = (du * xb[b]).astype(BF)
            dz_ref[f, e, :, ZA_W + 2 * CONV_CH:c0] = (du * gc[b]).astype(BF)
            for t, uu in enumerate((u2[b], u1[b], u[b])):
                upd[t] = upd[t] + jnp.sum(dc[b] * uu, axis=0, keepdims=True)
        row = lax.broadcasted_iota(jnp.int32, (HALO, CONV_CH), 0)
        tile = jnp.zeros((HALO, CONV_CH), F32)
        for t in range(3):
            tile = jnp.where(row == t, upd[t], tile)
        dcw_ref[...] += tile

    dz, dcw = pl.pallas_call(
        body, grid=(nb,), name="dz_assemble", in_specs=in_specs,
        out_specs=[_strip_rows(DZ_TA, IN_WIDTH), _whole((HALO, CONV_CH))],
        out_shape=[jax.ShapeDtypeStruct((4, 4, na, IN_WIDTH), BF), jax.ShapeDtypeStruct((HALO, CONV_CH), F32)],
        compiler_params=_cparams("arbitrary"),
    )(*args)
    return dz.reshape(s, IN_WIDTH), dcw


def _qkv_bwd(dz, dx1, x, g, w_all, l, tb):
    s, d = x.shape

    def body(dz_ref, dx1_ref, x_ref, g_ref, w_ref, dx_ref, dg_ref):
        i = pl.program_id(0)

        @pl.when(i == 0)
        def _():
            dg_ref[...] = jnp.zeros_like(dg_ref)

        n = IN_WIDTH // N_CHIPS
        dh = _nt(dz_ref[:, 0:n], w_ref[0])
        for k in range(1, N_CHIPS):
            dh = dh + _nt(dz_ref[:, k * n:(k + 1) * n], w_ref[k])
        xv = x_ref[...]
        r = _rms_scale(xv)
        xhat = xv * r
        dg_ref[...] += jnp.sum(dh * xhat, axis=0, keepdims=True)
        dx_ref[...] = dx1_ref[...] + _norm_bwd(dh * g_ref[...], xhat, r)

    return pl.pallas_call(
        body, grid=(s // tb,), name="qkv_bwd",
        in_specs=[_rows(tb, IN_WIDTH), _rows(tb, d), _rows(tb, d), _whole((1, d)),
                  _layer((N_CHIPS, d, IN_WIDTH // N_CHIPS), l)],
        out_specs=[_rows(tb, d), _whole((HALO, d))],
        out_shape=[jax.ShapeDtypeStruct((s, d), F32), jax.ShapeDtypeStruct((HALO, d), F32)],
        compiler_params=_cparams("arbitrary"),
    )(dz, dx1, x, g, w_all)


def _tile_rows(rows):
    return jnp.pad(rows, ((0, HALO - rows.shape[0]), (0, 0)))


def _to_strips(a):
    s, d = a.shape
    return a.reshape(s // N_STRIPS, 4, 4, d).transpose(2, 1, 0, 3).reshape(s, d)


def _from_strips(a):
    s, d = a.shape
    return a.reshape(4, 4, s // N_STRIPS, d).transpose(2, 1, 0, 3).reshape(s, d)


def _local_step(x, tgt, fetch, ff, sinks, g_mix, g_group, g_mlp, g_final, emit):
    s, d = x.shape
    depth = g_mix.shape[0]
    tb = min(512, s)
    tf = ff // N_CHIPS
    saved = []
    for l in range(depth):
        w_in, _, _, _, conv_w = fetch(0, l, x)
        cw = _tile_rows(conv_w[l])
        sk = jnp.pad(sinks[l].reshape(1, N_HEADS), ((0, HALO - 1), (0, 128 - N_HEADS)))
        h, za, zb, zc = _qkv_fwd(x, g_mix[l][None], w_in, l, tb)
        state = None
        for p, dil in enumerate(DILATIONS):
            state = _attn_fwd(za, dil, A_WIDTH, 1, 2, 1, A_MAX_DIST, state, None, p == len(DILATIONS) - 1,
                              "attn_a_fwd_%d" % dil)
        ya, lse_a = state
        yc, lse_c = _attn_fwd(zc, 1, C_KV_WIDTH, 3, 4, C_GROUP, C_MAX_DIST, None, sk, True, "attn_c_fwd")
        w_in, w_o, w1, w2, _ = fetch(1, l, yc)
        x1, yb = _mix_fwd(x, ya, yc, zb, cw, g_group[l][None], w_o, l, tb)
        x2, h2, ap = _mlp_fwd(x1, g_mlp[l][None], w1, w2, l, tb, tf)
        saved.append((x, h, za, zb, zc, ya, lse_a, yc, lse_c, yb, x1, h2, ap, cw, sk))
        x = x2
    dx, loss_tile, dg_final = _loss_head(x, g_final[None], tgt, tb)
    grads = [None] * depth
    tok = jnp.zeros((), F32)
    for l in reversed(range(depth)):
        x0, h, za, zb, zc, ya, lse_a, yc, lse_c, yb, x1, h2, ap, cw, sk = saved[l]
        dx1, dap, dg_mlp = _mlp_bwd(dx, x1, ap, g_mlp[l][None] + tok, w1, w2, l, tb, tf)
        tok = emit(l, 3, _wgrad(ap, dx, min(1024, ff), d, tb, "wgrad_ff_out", relu2=True))
        tok = tok + emit(l, 2, _wgrad(h2, dap, d, min(1024, ff), tb, "wgrad_ff_in"))
        n, dya, dyc, dd_a, dd_c, dyb, dg_group = _mix_bwd(dx1, ya, yb, yc, g_group[l][None] + tok, w_o, l, tb)
        tok = emit(l, 1, _wgrad(n, dx1, MIX_WIDTH, d, tb, "wgrad_o"))
        cw = cw + tok
        parts_a = [_attn_bwd(za, dya, lse_a, dd_a, dil, A_WIDTH, 1, 2, 1, A_MAX_DIST, None, "attn_a_bwd_%d" % dil)
                   for dil in DILATIONS]
        *parts_c, dsink = _attn_bwd(zc, dyc, lse_c, dd_c, 1, C_KV_WIDTH, 3, 4, C_GROUP, C_MAX_DIST, sk, "attn_c_bwd")
        dz, dcw = _dz_assemble(parts_a, parts_c, dyb, zb, cw)
        dx, dg_mix = _qkv_bwd(dz, dx1, x0, g_mix[l][None], w_in, l, tb)
        tok = emit(l, 0, _wgrad(h, dz, d, IN_WIDTH // 4, tb, "wgrad_in"))
        grads[l] = (dcw, dsink, dg_mix, dg_group, dg_mlp)
    return loss_tile, dx, grads, dg_final


ANY = pl.BlockSpec(memory_space=pl.ANY)
SHARD_AXES = (2, 1, 2, 1)
N_BIG = len(SHARD_AXES)
N_CHIPS = 4
N_DEV = 8


def _mesh_pos():
    return lax.axis_index("x"), lax.axis_index("y"), lax.axis_index("c")


def _flip(v, bit):
    return 1 - v if bit else v


def _place_shard(shard, chip_arr, name):
    _, rows, cols = shard.shape
    tr = min(256, rows)

    def body(chip_ref, x_ref, o_ref):
        o_ref[...] = x_ref[...].astype(BF)

    return pl.pallas_call(
        body, name=name,
        grid_spec=pltpu.PrefetchScalarGridSpec(
            num_scalar_prefetch=1, grid=(2, rows // tr),
            in_specs=[pl.BlockSpec((None, tr, cols), lambda l, i, chip: (l, i, 0))],
            out_specs=pl.BlockSpec((None, None, tr, cols), lambda l, i, chip: (l, chip[0], i, 0))),
        out_shape=jax.ShapeDtypeStruct((2, N_CHIPS, rows, cols), BF),
        compiler_params=_cparams("parallel", "parallel"),
    )(chip_arr, shard)


HBM = pl.BlockSpec(memory_space=pltpu.HBM)
SEM = pl.BlockSpec(memory_space=pltpu.SEMAPHORE)
EFFECT = pltpu.SideEffectType.DATAFLOW_SIDE_EFFECTING

GATHER_GROUPS = (((0, 0),), ((1, 0), (2, 0), (3, 0)), ((0, 1),), ((1, 1), (2, 1), (3, 1)))
GATHER_STARTS = ((0, 1), (2, 3))


def _gather_copies(arrs, group, send_sems, recv_sems):
    x, y, c = _mesh_pos()
    me = 2 * x + y
    out = []
    for i, (w, layer) in enumerate(group):
        mine = arrs[w].at[layer, me]
        for j, (qx, qy) in enumerate([(1 - x, y), (x, 1 - y), (1 - x, 1 - y)]):
            landed = arrs[w].at[layer, 2 * qx + qy]
            out.append(tuple(pltpu.make_async_remote_copy(
                src_ref=piece, dst_ref=piece, send_sem=send_sems.at[i * 3 + j], recv_sem=recv_sems.at[i * 3 + j],
                device_id=(qx, qy, c), device_id_type=MESH) for piece in (mine, landed)))
    return out


def _conv_copies(conv_src, conv_dst, send_sems, recv_sems):
    x, y, c = _mesh_pos()
    out = []
    for j, (qx, qy) in enumerate([(1 - x, y), (x, 1 - y), (1 - x, 1 - y)]):
        out.append(tuple(pltpu.make_async_remote_copy(
            src_ref=conv_src, dst_ref=conv_dst.at[q], send_sem=send_sems.at[j], recv_sem=recv_sems.at[j],
            device_id=(qx, qy, c), device_id_type=MESH) for q in (2 * x + y, 2 * qx + qy)))
    return out


def _gather_start(groups, arrs, conv, name):
    n_sems = 2 * (len(groups) + (conv is not None))

    def body(*refs):
        mats = refs[:N_BIG]
        sems = refs[n_op:n_op + n_sems]
        if conv is not None:
            for cp, _ in _conv_copies(refs[N_BIG], refs[N_BIG + 1], sems[-2], sems[-1]):
                cp.start()
        for k, g in enumerate(groups):
            for cp, _ in _gather_copies(mats, GATHER_GROUPS[g], sems[2 * k], sems[2 * k + 1]):
                cp.start()

    sem_shapes = []
    for n in [len(GATHER_GROUPS[g]) for g in groups] + ([1] if conv is not None else []):
        sem_shapes += [pltpu.SemaphoreType.DMA((3 * n,))] * 2
    operands = list(arrs) + ([] if conv is None else list(conv))
    n_op = len(operands)
    res = pl.pallas_call(
        body, name=name,
        out_shape=tuple(sem_shapes) + tuple(pltpu.HBM(a.shape, a.dtype) for a in operands),
        in_specs=(HBM,) * n_op, out_specs=(SEM,) * n_sems + (HBM,) * n_op,
        input_output_aliases={i: n_sems + i for i in range(n_op)},
        compiler_params=pltpu.CompilerParams(has_side_effects=EFFECT),
    )(*[pltpu.with_memory_space_constraint(a, pltpu.HBM) for a in operands])
    return res[:n_sems], list(res[n_sems:])


def _gather_wait(k, sems, arrs, conv, after, name):
    group = GATHER_GROUPS[k]
    mats = sorted({w for w, _ in group})
    n_conv = 0 if conv is None else 2

    def body(*refs):
        local = refs[:len(mats)]
        arrs_ref = [None] * N_BIG
        for w, ref in zip(mats, local):
            arrs_ref[w] = ref
        pos = len(mats) + n_conv
        copies = _gather_copies(arrs_ref, group, refs[pos], refs[pos + 1])
        if conv is not None:
            copies += _conv_copies(refs[len(mats)], refs[len(mats) + 1], refs[pos + 2], refs[pos + 3])
        for send, recv in copies:
            recv.wait_recv()
            send.wait_send()

    operands = [arrs[w] for w in mats] + ([] if conv is None else [conv[1], conv[2]])
    sem_ops = list(sems) + ([] if conv is None else list(conv[0]))
    n_op = len(operands)
    res = pl.pallas_call(
        body, name=name, out_shape=tuple(pltpu.HBM(a.shape, a.dtype) for a in operands),
        in_specs=(HBM,) * n_op + (SEM,) * len(sem_ops) + (ANY,), out_specs=(HBM,) * n_op,
        input_output_aliases={i: i for i in range(n_op)},
        compiler_params=pltpu.CompilerParams(has_side_effects=EFFECT),
    )(*operands, *sem_ops, after)
    arrs = list(arrs)
    for w, a in zip(mats, res):
        arrs[w] = a
    return arrs, (res[-1] if conv is not None else None)


def _grad_shard(ref, w, chip, n):
    start = pl.multiple_of(chip * n, 128)
    if SHARD_AXES[w] == 2:
        return ref.at[:, pl.ds(start, n)]
    return ref.at[pl.ds(start, n), :]


def _slot_shape(g, w):
    shape = list(g.shape)
    shape[SHARD_AXES[w] - 1] //= N_CHIPS
    return (N_DEV - 1,) + tuple(shape)


def _scatter_copies(g_ref, land_ref, send_sems, recv_sems, layer, w):
    x, y, c = _mesh_pos()
    n = g_ref.shape[SHARD_AXES[w] - 1] // N_CHIPS
    out = []
    for r in range(1, N_DEV):
        tx, ty, tc = _flip(x, r & 4), _flip(y, r & 2), _flip(c, r & 1)
        cp = pltpu.make_async_remote_copy(
            src_ref=_grad_shard(g_ref, w, 2 * tx + ty, n), dst_ref=land_ref.at[r - 1], send_sem=send_sems.at[r - 1],
            recv_sem=recv_sems.at[r - 1], device_id=(tx, ty, tc), device_id_type=MESH)
        out.append((cp, (c != layer) if r & 1 else (c == layer)))
    return out


def _scatter_start(g, land, layer, w, name):
    def body(g_ref, land_ref, send_sems, recv_sems, g_thru, land_thru, token):
        for cp, mine in _scatter_copies(g_ref, land_ref, send_sems, recv_sems, layer, w):
            @pl.when(mine)
            def _():
                cp.start()
        token[...] = jnp.zeros_like(token)

    return pl.pallas_call(
        body, name=name,
        out_shape=(pltpu.SemaphoreType.DMA((N_DEV - 1,)), pltpu.SemaphoreType.DMA((N_DEV - 1,)),
                   pltpu.HBM(g.shape, g.dtype), pltpu.HBM(land.shape, land.dtype),
                   jax.ShapeDtypeStruct((HALO, 128), F32)),
        in_specs=(HBM, HBM), out_specs=(SEM, SEM, HBM, HBM, pl.BlockSpec(memory_space=pltpu.VMEM)),
        input_output_aliases={0: 2, 1: 3}, compiler_params=pltpu.CompilerParams(has_side_effects=EFFECT),
    )(pltpu.with_memory_space_constraint(g, pltpu.HBM), pltpu.with_memory_space_constraint(land, pltpu.HBM))


def _scatter_wait(started, land, after, w, name):
    def body(g0_ref, g1_ref, land_ref, ss0, rs0, ss1, rs1, after_ref, g0_out, g1_out, land_out):
        c = lax.axis_index("c")
        for layer, g_ref, ss, rs in ((0, g0_ref, ss0, rs0), (1, g1_ref, ss1, rs1)):
            for cp, mine in _scatter_copies(g_ref, land_ref, ss, rs, layer, w):
                @pl.when(mine)
                def _():
                    cp.wait_send()

                @pl.when(c == layer)
                def _():
                    cp.wait_recv()

    (ss0, rs0, g0), (ss1, rs1, g1) = started
    return pl.pallas_call(
        body, name=name,
        out_shape=(pltpu.HBM(g0.shape, g0.dtype), pltpu.HBM(g1.shape, g1.dtype), pltpu.HBM(land.shape, land.dtype)),
        in_specs=(HBM, HBM, HBM, SEM, SEM, SEM, SEM, ANY), out_specs=(HBM, HBM, HBM),
        input_output_aliases={0: 0, 1: 1, 2: 2}, compiler_params=pltpu.CompilerParams(has_side_effects=EFFECT),
    )(g0, g1, land, ss0, rs0, ss1, rs1, after)


def _sum_slots(g0, g1, slots, w, pos_arr, name):
    _, rows, cols = slots.shape
    tr = min(256, rows)
    nr = rows // tr
    if SHARD_AXES[w] == 2:
        own = pl.BlockSpec((tr, cols), lambda i, pos: (i, pos[0]))
    else:
        own = pl.BlockSpec((tr, cols), lambda i, pos: (pos[0] * nr + i, 0))

    def body(pos_ref, own0_ref, own1_ref, s_ref, o_ref):
        acc = jnp.where(pos_ref[1] == 0, own0_ref[...], own1_ref[...]).astype(F32)
        for r in range(N_DEV - 1):
            acc = acc + s_ref[r].astype(F32)
        o_ref[...] = acc

    return pl.pallas_call(
        body, name=name,
        grid_spec=pltpu.PrefetchScalarGridSpec(
            num_scalar_prefetch=1, grid=(nr,),
            in_specs=[own, own, pl.BlockSpec((N_DEV - 1, tr, cols), lambda i, pos: (0, i, 0))],
            out_specs=pl.BlockSpec((tr, cols), lambda i, pos: (i, 0))),
        out_shape=jax.ShapeDtypeStruct((rows, cols), F32), compiler_params=_cparams("parallel"),
    )(pos_arr, g0, g1, slots)


def _swap_layers(halves):
    def body(*refs):
        srcs, dsts = refs[:N_BIG], refs[N_BIG:2 * N_BIG]
        send_sems, recv_sems = refs[2 * N_BIG:]
        x, y, c = _mesh_pos()
        sends = [pltpu.make_async_remote_copy(src_ref=srcs[w], dst_ref=dsts[w], send_sem=send_sems.at[w],
                                              recv_sem=recv_sems.at[w], device_id=(x, y, 1 - c), device_id_type=MESH)
                 for w in range(N_BIG)]
        for cp in sends:
            cp.start()
        for cp in sends:
            cp.wait_recv()
        for cp in sends:
            cp.wait_send()

    return pl.pallas_call(
        body, name="swap_layers", in_specs=[ANY] * N_BIG, out_specs=[ANY] * N_BIG,
        out_shape=[jax.ShapeDtypeStruct(h.shape, h.dtype) for h in halves],
        scratch_shapes=[pltpu.SemaphoreType.DMA((N_BIG,)), pltpu.SemaphoreType.DMA((N_BIG,))],
    )(*halves)


def _adamw_math(w, g, m, v):
    m = ADAM_B1 * m + (1.0 - ADAM_B1) * g
    v = ADAM_B2 * v + (1.0 - ADAM_B2) * jnp.square(g)
    m_hat = m / (1.0 - ADAM_B1 ** ADAM_STEP)
    v_hat = v / (1.0 - ADAM_B2 ** ADAM_STEP)
    delta = -ADAM_LR * (m_hat / (jnp.sqrt(v_hat) + ADAM_EPS) + ADAM_WD * w)
    return delta, m, v


def _adamw(w, g_own, g_other, m, v, pos_arr, name):
    shape = w.shape
    _, rows, cols = shape
    tr = min(256, rows)

    def body(pos_ref, w_ref, own_ref, other_ref, m_ref, v_ref, g_ref, d_ref, m2_ref, v2_ref):
        g = jnp.where(pl.program_id(0) == pos_ref[1], own_ref[...], other_ref[...])
        g_ref[...] = g
        d_ref[...], m2_ref[...], v2_ref[...] = _adamw_math(w_ref[...], g, m_ref[...], v_ref[...])

    full = pl.BlockSpec((None, tr, cols), lambda l, i, pos: (l, i, 0))
    half = pl.BlockSpec((tr, cols), lambda l, i, pos: (i, 0))
    return pl.pallas_call(
        body, name=name,
        grid_spec=pltpu.PrefetchScalarGridSpec(
            num_scalar_prefetch=1, grid=(2, rows // tr),
            in_specs=[full, half, half, full, full], out_specs=[full] * 4),
        out_shape=[jax.ShapeDtypeStruct(shape, F32)] * 4, compiler_params=_cparams("parallel", "parallel"),
    )(pos_arr, w, g_own, g_other, m, v)


def _small_sync(part, w, m, v):
    rows, cols = part.shape

    def body(p_ref, w_ref, m_ref, v_ref, g_ref, d_ref, m2_ref, v2_ref, slots, send_sems, recv_sems):
        x, y, c = _mesh_pos()
        me = 4 * x + 2 * y + c
        slots[me] = p_ref[...]
        sends = []
        for r in range(1, N_DEV):
            to = (_flip(x, r & 4), _flip(y, r & 2), _flip(c, r & 1))
            sends.append(pltpu.make_async_remote_copy(
                src_ref=p_ref, dst_ref=slots.at[me], send_sem=send_sems.at[r - 1], recv_sem=recv_sems.at[r - 1],
                device_id=to, device_id_type=MESH))
        for cp in sends:
            cp.start()
        for cp in sends:
            cp.wait_recv()
        for cp in sends:
            cp.wait_send()
        g = slots[0]
        for i in range(1, N_DEV):
            g = g + slots[i]
        g_ref[...] = g
        d_ref[...], m2_ref[...], v2_ref[...] = _adamw_math(w_ref[...], g, m_ref[...], v_ref[...])

    vm = pl.BlockSpec(memory_space=pltpu.VMEM)
    return pl.pallas_call(
        body, name="small_sync", in_specs=[vm] * 4, out_specs=[vm] * 4,
        out_shape=[jax.ShapeDtypeStruct((rows, cols), F32)] * 4,
        scratch_shapes=[pltpu.VMEM((N_DEV, rows, cols), F32), pltpu.SemaphoreType.DMA((N_DEV - 1,)),
                        pltpu.SemaphoreType.DMA((N_DEV - 1,))],
    )(part, w, m, v)


def _pack_small(d, g_mix, g_group, g_mlp, g_final, conv_full, sinks, scalar):
    def part(rows):
        return jnp.pad(rows, ((0, HALO - rows.shape[0]), (0, d - rows.shape[1])))
    return jnp.concatenate([part(g_mix), part(g_group), part(g_mlp), part(g_final[None]),
                            part(conv_full.reshape(6, CONV_CH)), part(sinks.reshape(2, N_HEADS)),
                            part(scalar.reshape(1, 1))], axis=0)


def _unpack_small(p, dm):
    return (p[0:2, :dm], p[8:10, :MIX_WIDTH], p[16:18, :dm], p[24, :dm], p[32:38, :CONV_CH].reshape(2, 3, CONV_CH),
            p[40:42, :N_HEADS].reshape(2, 2, C_GROUP), p[48, 0])


def kernel(x, w_in, conv_w, sinks, g_mix, g_group, w_o, g_mlp, w_ff_in, w_ff_out, g_final, loss_target, m_w_in, m_conv_w, m_sinks, m_g_mix, m_g_group, m_w_o, m_g_mlp, m_w_ff_in, m_w_ff_out, m_g_final, v_w_in, v_conv_w, v_sinks, v_g_mix, v_g_group, v_w_o, v_g_mlp, v_w_ff_in, v_w_ff_out, v_g_final):
    d = max(x.shape[2], MIX_WIDTH)
    chip = 2 * lax.axis_index("x") + lax.axis_index("y")
    conv_n = conv_w.shape[2]

    pos_arr = jnp.stack([chip, lax.axis_index("c")]).astype(jnp.int32)
    placed = [_place_shard(w, pos_arr[:1], "place_shard_%d" % i)
              for i, w in enumerate((w_in, w_o, w_ff_in, w_ff_out))]
    conv_tile = jnp.pad(conv_w.reshape(6, conv_n), ((0, HALO - 6), (0, 128 - conv_n)))
    sems0, thru = _gather_start(GATHER_STARTS[0], placed,
                                (conv_tile, lax.empty((N_CHIPS,) + conv_tile.shape, conv_tile.dtype)), "gather_start_0")
    full = {"arrs": thru[:N_BIG], "conv": None, "sems": list(sems0[:4])}

    def fetch(stage, layer, after):
        k = 2 * layer + stage
        sems = full["sems"][2 * k:2 * k + 2]
        if k == 0:
            full["arrs"], land = _gather_wait(0, sems, full["arrs"], (sems0[-2:], thru[N_BIG], thru[N_BIG + 1]),
                                              after, "gather_wait_0")
            conv_all = lax.dynamic_update_slice(land, conv_tile[None], (chip, 0, 0))
            full["conv"] = conv_all[:, :6, :conv_n].reshape(N_CHIPS, 2, 3, conv_n).transpose(1, 2, 0, 3).reshape(
                2, 3, CONV_CH)
        else:
            full["arrs"], _ = _gather_wait(k, sems, full["arrs"], None, after, "gather_wait_%d" % k)
        if k == 1:
            sems1, full["arrs"] = _gather_start(GATHER_STARTS[1], full["arrs"], None, "gather_start_1")
            full["sems"] += list(sems1)
        return (*full["arrs"], full["conv"])

    lands, started = [None] * N_BIG, {}

    def emit(layer, w, g):
        if lands[w] is None:
            lands[w] = lax.empty(_slot_shape(g, w), g.dtype)
        *started[layer, w], lands[w], token = _scatter_start(g, lands[w], layer, w, "scatter_start_%d_%d" % (layer, w))
        return token[0, 0]

    loss_tile, dx, grads, dg_final = _local_step(_to_strips(x[0]), _to_strips(loss_target[0]), fetch, w_ff_in.shape[2] * N_CHIPS,
                                                 sinks, g_mix, g_group, g_mlp, g_final, emit)

    own = []
    for w in range(N_BIG):
        g0, g1, slots = _scatter_wait((started[0, w], started[1, w]), lands[w], dx, w, "scatter_wait_%d" % w)
        own.append(_sum_slots(g0, g1, slots, w, pos_arr, "sum_slots_%d" % w))
    other = _swap_layers(own)

    def both(i):
        return jnp.stack([grads[0][i][0], grads[1][i][0]])
    dconv = jnp.stack([grads[0][0][:3], grads[1][0][:3]])
    dsinks = jnp.stack([grads[0][1][0, :N_HEADS], grads[1][1][0, :N_HEADS]])
    part = _pack_small(d, both(2), both(3), both(4), dg_final[0], dconv, dsinks, loss_tile[0, 0])

    def spread(shard):
        return lax.dynamic_update_slice(jnp.zeros((2, 3, CONV_CH), F32), shard, (0, 0, chip * conv_n))
    zero = jnp.zeros((), F32)
    packs = [_pack_small(d, a, b, c_, e, spread(f), g_, zero) for a, b, c_, e, f, g_ in (
        (g_mix, g_group, g_mlp, g_final, conv_w, sinks),
        (m_g_mix, m_g_group, m_g_mlp, m_g_final, m_conv_w, m_sinks),
        (v_g_mix, v_g_group, v_g_mlp, v_g_final, v_conv_w, v_sinks))]
    small = [_unpack_small(p, x.shape[2]) for p in _small_sync(part, *packs)]

    def shard_of(full):
        return lax.dynamic_slice(full, (0, 0, chip * conv_n), (2, 3, conv_n))
    small = [(s[0], s[1], s[2], s[3], shard_of(s[4]), s[5], s[6]) for s in small]
    loss = small[0][6]

    big = [_adamw(w, own[i], other[i], m, v, pos_arr, "adamw_%d" % i) for i, (w, m, v) in enumerate((
        (w_in, m_w_in, v_w_in), (w_o, m_w_o, v_w_o), (w_ff_in, m_w_ff_in, v_w_ff_in),
        (w_ff_out, m_w_ff_out, v_w_ff_out)))]

    def ordered(kind):
        b = [big[i][kind] for i in range(N_BIG)]
        s = small[kind]
        return [b[0], s[4], s[5], s[0], s[1], b[1], s[2], b[2], b[3], s[3]]

    return (loss, _from_strips(dx)[None], *ordered(0), *ordered(1), *ordered(2), *ordered(3))
```

```python
import functools

import jax
import jax.numpy as jnp
from jax import lax
from jax.experimental import pallas as pl
from jax.experimental.pallas import tpu as pltpu

HEAD_DIM = 64
N_HEADS = 6
C_GROUP = 3
A_WIDTH = N_HEADS * HEAD_DIM
C_KV_WIDTH = 2 * HEAD_DIM
CONV_CH = 256
ZA_W = 3 * A_WIDTH
ZB_W = 3 * CONV_CH
ZC_W = A_WIDTH + 2 * C_KV_WIDTH
IN_WIDTH = ZA_W + ZB_W + ZC_W
MIX_WIDTH = A_WIDTH + CONV_CH + A_WIDTH
DILATIONS = (1, 4, 16)
A_MAX_DIST = 128
C_MAX_DIST = 127
TQ = 128
EPS = 1e-6
SCALE = HEAD_DIM ** -0.5
NEG = -1e30
HALO = 8

ADAM_LR = 0.001
ADAM_B1 = 0.9
ADAM_B2 = 0.999
ADAM_EPS = 1e-08
ADAM_WD = 0.01
ADAM_STEP = 10

BF = jnp.bfloat16
F32 = jnp.float32
MESH = pl.DeviceIdType.MESH
VMEM_LIMIT = 56 * 1024 * 1024


def _cparams(*sem):
    return pltpu.CompilerParams(dimension_semantics=sem, vmem_limit_bytes=VMEM_LIMIT)


def _nt(a, b):
    return lax.dot_general(a, b, (((1,), (1,)), ((), ())), preferred_element_type=F32)


def _tn(a, b):
    return lax.dot_general(a, b, (((0,), (0,)), ((), ())), preferred_element_type=F32)


def _nn(a, b):
    return jnp.dot(a, b, preferred_element_type=F32)


def _rows(tb, w):
    return pl.BlockSpec((tb, w), lambda i: (i, 0))


def _whole(shape):
    return pl.BlockSpec(shape, lambda *_: (0,) * len(shape))


def _layer(shape, l):
    return pl.BlockSpec((None,) + shape, lambda *_: (l,) + (0,) * len(shape))


def _rms_scale(v):
    return lax.rsqrt(jnp.mean(v * v, axis=-1, keepdims=True) + EPS)


def _norm_bwd(dxhat, xhat, r):
    return r * (dxhat - xhat * jnp.mean(dxhat * xhat, axis=-1, keepdims=True))


def _qkv_fwd(x, g, w_all, l, tb):
    s, d = x.shape

    def body(x_ref, g_ref, w_ref, h_ref, za_ref, zb_ref, zc_ref):
        xv = x_ref[...]
        h = ((xv * _rms_scale(xv)) * g_ref[...]).astype(BF)
        h_ref[...] = h
        z = jnp.concatenate([_nn(h, w_ref[k]) for k in range(N_CHIPS)], axis=1)
        za_ref[...] = z[:, :ZA_W]
        zb_ref[...] = z[:, ZA_W:ZA_W + ZB_W]
        zc_ref[...] = z[:, ZA_W + ZB_W:]

    return pl.pallas_call(
        body, grid=(s // tb,), name="qkv_fwd",
        in_specs=[_rows(tb, d), _whole((1, d)), _layer((N_CHIPS, d, IN_WIDTH // N_CHIPS), l)],
        out_specs=[_rows(tb, d), _rows(tb, ZA_W), _rows(tb, ZB_W), _rows(tb, ZC_W)],
        out_shape=[jax.ShapeDtypeStruct((s, d), BF), jax.ShapeDtypeStruct((s, ZA_W), F32),
                   jax.ShapeDtypeStruct((s, ZB_W), F32), jax.ShapeDtypeStruct((s, ZC_W), F32)],
        compiler_params=_cparams("parallel"),
    )(x, g, w_all)


N_STRIPS = 16


def _strips(a):
    s, w = a.shape
    return a.reshape(4, 4, s // N_STRIPS, w)


def _p_grid(s, dil):
    na = s // N_STRIPS
    return {16: (4, 4, na // TQ), 4: (4, na // 32), 1: (na // 8,)}[dil]


def _p_spec(dil, cw, col, prev=False):
    def blk(j):
        return jnp.maximum(j - 1, 0) if prev else j
    if dil == 16:
        return pl.BlockSpec((None, None, TQ, cw), lambda f, e, j: (f, e, blk(j), col))
    if dil == 4:
        return pl.BlockSpec((None, 4, 32, cw), lambda f, j: (f, 0, blk(j), col))
    return pl.BlockSpec((4, 4, 8, cw), lambda j: (0, 0, blk(j), col))


def _block_pos(i, dil):
    if dil == 16:
        return i
    if dil == 4:
        return 4 * (i % 32) + i // 32
    return 16 * (i % 8) + 4 * ((i // 8) % 4) + i // 32


def _band_mask(b, dil, max_dist):
    qi = _block_pos(lax.broadcasted_iota(jnp.int32, (TQ, 2 * TQ), 0), dil)
    col = lax.broadcasted_iota(jnp.int32, (TQ, 2 * TQ), 1)
    cur = col >= TQ
    dist = qi - _block_pos(col % TQ, dil) + jnp.where(cur, 0, TQ)
    return (dist >= 0) & (dist <= max_dist) & (cur | (b > 0))


def _hs(h):
    return slice(h * HEAD_DIM, (h + 1) * HEAD_DIM)


def _ld(ref, cols):
    v = ref[..., cols]
    return v.reshape(TQ, v.shape[-1])


def _st(ref, cols, val):
    ref[..., cols] = val.reshape(ref.shape[:-1] + (val.shape[-1],))


def _attn_fwd(z, dil, kw, kcol, vcol, n_rep, max_dist, sink, name):
    s, zw = z.shape
    grid = _p_grid(s, dil)
    have_sink = sink is not None

    def body(*refs):
        q_ref, kp_ref, kc_ref, vp_ref, vc_ref = refs[:5]
        pos = 5
        if have_sink:
            sink_ref = refs[pos]
            pos += 1
        outs = refs[pos:]
        mask = _band_mask(pl.program_id(len(grid) - 1), dil, max_dist)
        for h in range(N_HEADS):
            kh = h // n_rep
            q = _ld(q_ref, _hs(h)).astype(BF)
            k2 = jnp.concatenate([_ld(kp_ref, _hs(kh)), _ld(kc_ref, _hs(kh))], axis=0).astype(BF)
            v2 = jnp.concatenate([_ld(vp_ref, _hs(kh)), _ld(vc_ref, _hs(kh))], axis=0).astype(BF)
            sc = jnp.where(mask, _nt(q, k2) * SCALE, NEG)
            m_new = jnp.max(sc, axis=1, keepdims=True)
            if have_sink:
                sk = sink_ref[0:1, h:h + 1]
                m_new = jnp.maximum(m_new, sk)
            p = jnp.exp(sc - m_new)
            l_new = jnp.sum(p, axis=1, keepdims=True)
            if have_sink:
                l_new = l_new + jnp.exp(sk - m_new)
            _st(outs[0], _hs(h), _nn(p.astype(BF), v2))
            _st(outs[1], _hs(h), jnp.broadcast_to(m_new, (TQ, HEAD_DIM)))
            _st(outs[2], _hs(h), jnp.broadcast_to(l_new, (TQ, HEAD_DIM)))

    args = [_strips(z)] * 5
    in_specs = [_p_spec(dil, A_WIDTH, 0), _p_spec(dil, kw, kcol, True), _p_spec(dil, kw, kcol),
                _p_spec(dil, kw, vcol, True), _p_spec(dil, kw, vcol)]
    if have_sink:
        args.append(sink)
        in_specs.append(_whole((HALO, 128)))
    n_out = 3
    res = pl.pallas_call(
        body, grid=grid, name=name, in_specs=in_specs,
        out_specs=[_p_spec(dil, A_WIDTH, 0)] * n_out,
        out_shape=[jax.ShapeDtypeStruct((4, 4, s // N_STRIPS, A_WIDTH), F32)] * n_out,
        compiler_params=_cparams(*(("parallel",) * len(grid))),
    )(*args)
    return [a.reshape(s, A_WIDTH) for a in res]


def _attn_merge(parts_a, part_c, tb):
    s = part_c[0].shape[0]
    n_a = len(parts_a)

    def body(*refs):
        ins, (ya_ref, lsea_ref, yc_ref, lsec_ref) = refs[:3 * n_a + 3], refs[3 * n_a + 3:]
        ms = [ins[3 * p + 1][...] for p in range(n_a)]
        m = functools.reduce(jnp.maximum, ms)
        acc = jnp.zeros_like(m)
        l = jnp.zeros_like(m)
        for p in range(n_a):
            w = jnp.exp(ms[p] - m)
            acc = acc + w * ins[3 * p][...]
            l = l + w * ins[3 * p + 2][...]
        ya_ref[...] = acc / l
        lsea_ref[...] = m + jnp.log(l)
        acc_c, m_c, l_c = [r[...] for r in ins[3 * n_a:]]
        yc_ref[...] = acc_c / l_c
        lsec_ref[...] = m_c + jnp.log(l_c)

    return pl.pallas_call(
        body, grid=(s // tb,), name="attn_merge", in_specs=[_rows(tb, A_WIDTH)] * (3 * n_a + 3),
        out_specs=[_rows(tb, A_WIDTH)] * 4, out_shape=[jax.ShapeDtypeStruct((s, A_WIDTH), F32)] * 4,
        compiler_params=_cparams("parallel"),
    )(*[a for part in parts_a + [part_c] for a in part])


def _shift_down(v, n, halo):
    rows = v.shape[0]
    out = pltpu.roll(v, n, 0)
    row = lax.broadcasted_iota(jnp.int32, v.shape, 0)
    for t in range(n):
        out = jnp.where(row == t, halo[HALO - n + t:HALO - n + t + 1, :], out)
    return out


def _shift_up(v, n, halo):
    rows = v.shape[0]
    out = pltpu.roll(v, rows - n, 0)
    row = lax.broadcasted_iota(jnp.int32, v.shape, 0)
    for t in range(n):
        out = jnp.where(row == rows - n + t, halo[t:t + 1, :], out)
    return out


def _strip(v, b):
    return v[b % 4, b // 4]


def _conv_strips(zb, prev, cw):
    gb = [_strip(zb, b)[:, :CONV_CH] for b in range(N_STRIPS)]
    gc = [_strip(zb, b)[:, CONV_CH:2 * CONV_CH] for b in range(N_STRIPS)]
    xb = [_strip(zb, b)[:, 2 * CONV_CH:] for b in range(N_STRIPS)]
    u = [g * v for g, v in zip(gc, xb)]
    uh = prev[:, :, CONV_CH:2 * CONV_CH] * prev[:, :, 2 * CONV_CH:]
    wrapped = {14: _shift_down(u[14], 1, uh[2]), 15: _shift_down(u[15], 1, uh[3])}
    u1 = [u[b - 1] if b >= 1 else wrapped[15] for b in range(N_STRIPS)]
    u2 = [u[b - 2] if b >= 2 else wrapped[14 + b] for b in range(N_STRIPS)]
    c = [cw[0:1, :] * u2[b] + cw[1:2, :] * u1[b] + cw[2:3, :] * u[b] for b in range(N_STRIPS)]
    return gb, gc, xb, u, u1, u2, c


def _strip_rows(ta, w):
    return pl.BlockSpec((4, 4, ta, w), lambda i: (0, 0, i, 0))


def _prev_rows(ta, w):
    return pl.BlockSpec((4, None, HALO, w), lambda i: (0, 3, jnp.maximum(i * (ta // HALO) - 1, 0), 0))


def _next_rows(ta, w, nblk):
    return pl.BlockSpec((4, None, HALO, w),
                        lambda i: (0, 0, jnp.minimum((i + 1) * (ta // HALO), nblk * (ta // HALO) - 1), 0))


def _mix_fwd(x, ya, yc, zb, cw, gg, wo_all, l, tb):
    s, d = x.shape
    ta = tb // N_STRIPS

    def body(x_ref, ya_ref, yc_ref, zb_ref, zbp_ref, cw_ref, gg_ref, wo_ref, x1_ref, yb_ref):
        i = pl.program_id(0)
        prev = jnp.where(i > 0, zbp_ref[...], 0.0)
        gb, _, _, _, _, _, c = _conv_strips(zb_ref[...], prev, cw_ref[...])
        for b in range(N_STRIPS):
            yb_ref[b % 4, b // 4] = gb[b] * c[b]
        yb = yb_ref[...].reshape(tb, CONV_CH)
        ya, yc = ya_ref[...].reshape(tb, A_WIDTH), yc_ref[...].reshape(tb, A_WIDTH)
        n = jnp.concatenate([ya * _rms_scale(ya), yb * _rms_scale(yb), yc * _rms_scale(yc)], axis=1)
        n = (n * gg_ref[...]).astype(BF)
        x1 = x_ref[...].reshape(tb, d) + _nn(n, wo_ref[...].reshape(MIX_WIDTH, d))
        x1_ref[...] = x1.reshape(4, 4, ta, d)

    res = pl.pallas_call(
        body, grid=(s // tb,), name="mix_fwd",
        in_specs=[_strip_rows(ta, d), _strip_rows(ta, A_WIDTH), _strip_rows(ta, A_WIDTH), _strip_rows(ta, ZB_W),
                  _prev_rows(ta, ZB_W), _whole((HALO, CONV_CH)), _whole((1, MIX_WIDTH)),
                  _layer((N_CHIPS, MIX_WIDTH // N_CHIPS, d), l)],
        out_specs=[_strip_rows(ta, d), _strip_rows(ta, CONV_CH)],
        out_shape=[jax.ShapeDtypeStruct((4, 4, s // N_STRIPS, d), F32),
                   jax.ShapeDtypeStruct((4, 4, s // N_STRIPS, CONV_CH), F32)],
        compiler_params=_cparams("parallel"),
    )(_strips(x), _strips(ya), _strips(yc), _strips(zb), _strips(zb), cw, gg, wo_all)
    return res[0].reshape(s, d), res[1].reshape(s, CONV_CH)


def _mlp_fwd(x1, g, w1_all, w2_all, l, tb, tf):
    s, d = x1.shape
    ff = w1_all.shape[1] * w1_all.shape[3]
    nj = ff // tf

    def body(x_ref, g_ref, w1_ref, w2_ref, x2_ref, h2_ref, ap_ref, acc):
        j = pl.program_id(1)

        @pl.when(j == 0)
        def _():
            xv = x_ref[...]
            h2_ref[...] = ((xv * _rms_scale(xv)) * g_ref[...]).astype(BF)
            acc[...] = jnp.zeros_like(acc)

        ap = _nn(h2_ref[...], w1_ref[...])
        ap_ref[...] = ap.astype(BF)
        a = jnp.square(jnp.maximum(ap, 0.0)).astype(BF)
        acc[...] += _nn(a, w2_ref[...])

        @pl.when(j == nj - 1)
        def _():
            x2_ref[...] = x_ref[...] + acc[...]

    return pl.pallas_call(
        body, grid=(s // tb, nj), name="mlp_fwd",
        in_specs=[pl.BlockSpec((tb, d), lambda i, j: (i, 0)), _whole((1, d)),
                  pl.BlockSpec((None, None, d, tf), lambda i, j: (l, j, 0, 0)),
                  pl.BlockSpec((None, None, tf, d), lambda i, j: (l, j, 0, 0))],
        out_specs=[pl.BlockSpec((tb, d), lambda i, j: (i, 0)), pl.BlockSpec((tb, d), lambda i, j: (i, 0)),
                   pl.BlockSpec((tb, tf), lambda i, j: (i, j))],
        out_shape=[jax.ShapeDtypeStruct((s, d), F32), jax.ShapeDtypeStruct((s, d), BF),
                   jax.ShapeDtypeStruct((s, ff), BF)],
        scratch_shapes=[pltpu.VMEM((tb, d), F32)],
        compiler_params=_cparams("parallel", "arbitrary"),
    )(x1, g, w1_all, w2_all)


def _loss_head(x, g, tgt, tb):
    s, d = x.shape

    def body(x_ref, g_ref, t_ref, dx_ref, loss_ref, dg_ref):
        i = pl.program_id(0)

        @pl.when(i == 0)
        def _():
            loss_ref[...] = jnp.zeros_like(loss_ref)
            dg_ref[...] = jnp.zeros_like(dg_ref)

        xv = x_ref[...]
        r = _rms_scale(xv)
        xhat = xv * r
        err = xhat * g_ref[...] - t_ref[...]
        part = jnp.sum(jnp.mean(jnp.square(err), axis=-1, keepdims=True), axis=0, keepdims=True)
        loss_ref[...] += 0.5 * part
        dy = err * (1.0 / d)
        dg_ref[...] += jnp.sum(dy * xhat, axis=0, keepdims=True)
        dx_ref[...] = _norm_bwd(dy * g_ref[...], xhat, r)

    return pl.pallas_call(
        body, grid=(s // tb,), name="loss_head",
        in_specs=[_rows(tb, d), _whole((1, d)), _rows(tb, d)],
        out_specs=[_rows(tb, d), _whole((HALO, 128)), _whole((HALO, d))],
        out_shape=[jax.ShapeDtypeStruct((s, d), F32), jax.ShapeDtypeStruct((HALO, 128), F32),
                   jax.ShapeDtypeStruct((HALO, d), F32)],
        compiler_params=_cparams("arbitrary"),
    )(x, g, tgt)


def _mlp_bwd(dx2, x1, ap, g, w1_all, w2_all, l, tb, tf):
    s, d = x1.shape
    ff = ap.shape[1]
    nj = ff // tf

    def body(dx2_ref, x1_ref, ap_ref, g_ref, w1_ref, w2_ref, dx1_ref, dap_ref, dg_ref, acc):
        i, j = pl.program_id(0), pl.program_id(1)

        @pl.when((i == 0) & (j == 0))
        def _():
            dg_ref[...] = jnp.zeros_like(dg_ref)

        @pl.when(j == 0)
        def _():
            acc[...] = jnp.zeros_like(acc)

        da = _nt(dx2_ref[...].astype(BF), w2_ref[...])
        dap = (da * (2.0 * jnp.maximum(ap_ref[...].astype(F32), 0.0))).astype(BF)
        dap_ref[...] = dap
        acc[...] += _nt(dap, w1_ref[...])

        @pl.when(j == nj - 1)
        def _():
            xv = x1_ref[...]
            r = _rms_scale(xv)
            xhat = xv * r
            dh = acc[...]
            dg_ref[...] += jnp.sum(dh * xhat, axis=0, keepdims=True)
            dx1_ref[...] = dx2_ref[...] + _norm_bwd(dh * g_ref[...], xhat, r)

    return pl.pallas_call(
        body, grid=(s // tb, nj), name="mlp_bwd",
        in_specs=[pl.BlockSpec((tb, d), lambda i, j: (i, 0)), pl.BlockSpec((tb, d), lambda i, j: (i, 0)),
                  pl.BlockSpec((tb, tf), lambda i, j: (i, j)),
                  _whole((1, d)), pl.BlockSpec((None, None, d, tf), lambda i, j: (l, j, 0, 0)),
                  pl.BlockSpec((None, None, tf, d), lambda i, j: (l, j, 0, 0))],
        out_specs=[pl.BlockSpec((tb, d), lambda i, j: (i, 0)), pl.BlockSpec((tb, tf), lambda i, j: (i, j)),
                   _whole((HALO, d))],
        out_shape=[jax.ShapeDtypeStruct((s, d), F32), jax.ShapeDtypeStruct((s, ff), BF),
                   jax.ShapeDtypeStruct((HALO, d), F32)],
        scratch_shapes=[pltpu.VMEM((tb, d), F32)],
        compiler_params=_cparams("arbitrary", "arbitrary"),
    )(dx2, x1, ap, g, w1_all, w2_all)


def _wgrad(a, b, tm, tn, ts, name, relu2=False):
    s, m = a.shape
    n = b.shape[1]
    ns = s // ts

    def body(a_ref, b_ref, o_ref, acc):
        k = pl.program_id(2)

        @pl.when(k == 0)
        def _():
            acc[...] = jnp.zeros_like(acc)

        av = a_ref[...]
        if relu2:
            av = jnp.square(jnp.maximum(av.astype(F32), 0.0)).astype(BF)
        acc[...] += _tn(av, b_ref[...].astype(BF))

        @pl.when(k == ns - 1)
        def _():
            o_ref[...] = acc[...].astype(BF)

    return pl.pallas_call(
        body, grid=(m // tm, n // tn, ns), name=name,
        in_specs=[pl.BlockSpec((ts, tm), lambda i, j, k: (k, i)), pl.BlockSpec((ts, tn), lambda i, j, k: (k, j))],
        out_specs=pl.BlockSpec((tm, tn), lambda i, j, k: (i, j)),
        out_shape=jax.ShapeDtypeStruct((m, n), BF),
        scratch_shapes=[pltpu.VMEM((tm, tn), F32)],
        compiler_params=_cparams("parallel", "parallel", "arbitrary"),
    )(a, b)


def _mix_bwd(dx1, ya, yb, yc, gg, wo_all, l, tb):
    s, d = dx1.shape

    def body(dx_ref, ya_ref, yb_ref, yc_ref, gg_ref, wo_ref, n_ref, dya_ref, dyc_ref, da_ref, dc_ref, dyb_ref, dg_ref):
        i = pl.program_id(0)

        @pl.when(i == 0)
        def _():
            dg_ref[...] = jnp.zeros_like(dg_ref)

        dn = _nt(dx_ref[...].astype(BF), wo_ref[...].reshape(MIX_WIDTH, d))
        ys = [ya_ref[...], yb_ref[...], yc_ref[...]]
        rs = [_rms_scale(v) for v in ys]
        nhat = jnp.concatenate([v * r for v, r in zip(ys, rs)], axis=1)
        gg = gg_ref[...]
        n_ref[...] = (nhat * gg).astype(BF)
        dg_ref[...] += jnp.sum(dn * nhat, axis=0, keepdims=True)
        dnh = dn * gg
        bounds = [(0, A_WIDTH), (A_WIDTH, A_WIDTH + CONV_CH), (A_WIDTH + CONV_CH, MIX_WIDTH)]
        dys = [_norm_bwd(dnh[:, lo:hi], nhat[:, lo:hi], r) for (lo, hi), r in zip(bounds, rs)]
        dyb_ref[...] = dys[1]
        for dy, y, dy_ref, dd_ref in ((dys[0], ys[0], dya_ref, da_ref), (dys[2], ys[2], dyc_ref, dc_ref)):
            dy_ref[...] = dy
            t = dy * y
            for h in range(N_HEADS):
                dd_ref[:, _hs(h)] = jnp.broadcast_to(jnp.sum(t[:, _hs(h)], axis=1, keepdims=True), (tb, HEAD_DIM))

    return pl.pallas_call(
        body, grid=(s // tb,), name="mix_bwd",
        in_specs=[_rows(tb, d), _rows(tb, A_WIDTH), _rows(tb, CONV_CH), _rows(tb, A_WIDTH), _whole((1, MIX_WIDTH)),
                  _layer((N_CHIPS, MIX_WIDTH // N_CHIPS, d), l)],
        out_specs=[_rows(tb, MIX_WIDTH), _rows(tb, A_WIDTH), _rows(tb, A_WIDTH), _rows(tb, A_WIDTH),
                   _rows(tb, A_WIDTH), _rows(tb, CONV_CH), _whole((HALO, MIX_WIDTH))],
        out_shape=[jax.ShapeDtypeStruct((s, MIX_WIDTH), BF), jax.ShapeDtypeStruct((s, A_WIDTH), F32),
                   jax.ShapeDtypeStruct((s, A_WIDTH), F32), jax.ShapeDtypeStruct((s, A_WIDTH), F32),
                   jax.ShapeDtypeStruct((s, A_WIDTH), F32), jax.ShapeDtypeStruct((s, CONV_CH), F32),
                   jax.ShapeDtypeStruct((HALO, MIX_WIDTH), F32)],
        compiler_params=_cparams("arbitrary"),
    )(dx1, ya, yb, yc, gg, wo_all)


def _attn_bwd(z, dy, lse, dd, dil, kw, kcol, vcol, n_rep, max_dist, sink, name):
    s, zw = z.shape
    grid = _p_grid(s, dil)
    have_sink = sink is not None
    n_kv = N_HEADS // n_rep

    def body(*refs):
        q_ref, kp_ref, kc_ref, vp_ref, vc_ref, dy_ref, lse_ref, dd_ref = refs[:8]
        pos = 8
        if have_sink:
            sink_ref = refs[pos]
            pos += 1
        dq_ref, dkp_ref, dkc_ref, dvp_ref, dvc_ref = refs[pos:pos + 5]
        b = pl.program_id(len(grid) - 1)
        mask = _band_mask(b, dil, max_dist)
        if have_sink:
            dsink_ref = refs[pos + 5]

            @pl.when(b == 0)
            def _():
                dsink_ref[...] = jnp.zeros_like(dsink_ref)

            row = lax.broadcasted_iota(jnp.int32, (HALO, 128), 0)
            lane = lax.broadcasted_iota(jnp.int32, (HALO, 128), 1)
        for kh in range(n_kv):
            k2 = jnp.concatenate([_ld(kp_ref, _hs(kh)), _ld(kc_ref, _hs(kh))], axis=0).astype(BF)
            v2 = jnp.concatenate([_ld(vp_ref, _hs(kh)), _ld(vc_ref, _hs(kh))], axis=0).astype(BF)
            dk2 = jnp.zeros((2 * TQ, HEAD_DIM), F32)
            dv2 = jnp.zeros((2 * TQ, HEAD_DIM), F32)
            for h in range(kh * n_rep, (kh + 1) * n_rep):
                q = _ld(q_ref, _hs(h)).astype(BF)
                lse_h = _ld(lse_ref, slice(h * HEAD_DIM, h * HEAD_DIM + 1))
                dd_h = _ld(dd_ref, slice(h * HEAD_DIM, h * HEAD_DIM + 1))
                dyh = _ld(dy_ref, _hs(h)).astype(BF)
                sc = jnp.where(mask, _nt(q, k2) * SCALE, NEG)
                p = jnp.exp(sc - lse_h)
                dp = _nt(dyh, v2)
                ds = ((p * (dp - dd_h)) * SCALE).astype(BF)
                _st(dq_ref, _hs(h), _nn(ds, k2))
                dk2 = dk2 + _tn(ds, q)
                dv2 = dv2 + _tn(p.astype(BF), dyh)
                if have_sink:
                    sk = sink_ref[0:1, h:h + 1]
                    val = -jnp.sum(jnp.exp(sk - lse_h) * dd_h, axis=0, keepdims=True)
                    dsink_ref[...] += jnp.where((row == 0) & (lane == h), val, 0.0)
            _st(dkp_ref, _hs(kh), dk2[:TQ])
            _st(dkc_ref, _hs(kh), dk2[TQ:])
            _st(dvp_ref, _hs(kh), dv2[:TQ])
            _st(dvc_ref, _hs(kh), dv2[TQ:])

    args = [_strips(z)] * 5 + [_strips(a) for a in (dy, lse, dd)]
    in_specs = [_p_spec(dil, A_WIDTH, 0), _p_spec(dil, kw, kcol, True), _p_spec(dil, kw, kcol),
                _p_spec(dil, kw, vcol, True), _p_spec(dil, kw, vcol)] + [_p_spec(dil, A_WIDTH, 0)] * 3
    out_specs = [_p_spec(dil, A_WIDTH, 0)] + [_p_spec(dil, kw, 0)] * 4
    na = s // N_STRIPS
    out_shape = [jax.ShapeDtypeStruct((4, 4, na, A_WIDTH), F32)] + [jax.ShapeDtypeStruct((4, 4, na, kw), F32)] * 4
    if have_sink:
        args.append(sink)
        in_specs.append(_whole((HALO, 128)))
        out_specs.append(_whole((HALO, 128)))
        out_shape.append(jax.ShapeDtypeStruct((HALO, 128), F32))
    res = pl.pallas_call(
        body, grid=grid, name=name, in_specs=in_specs, out_specs=out_specs, out_shape=out_shape,
        compiler_params=_cparams(*(("arbitrary",) * len(grid))),
    )(*args)
    outs = [res[0].reshape(s, A_WIDTH)] + [a.reshape(s, kw) for a in res[1:5]]
    return outs + list(res[5:])


DZ_TA = 16


def _dz_assemble(parts_a, parts_c, dyb, zb, cw):
    s = zb.shape[0]
    na = s // N_STRIPS
    nb = na // DZ_TA

    def ahead(w, k):
        return pl.BlockSpec((4, 4, DZ_TA, w), lambda i: (0, 0, jnp.minimum(i + k, nb - 1), 0))

    args, in_specs = [], []
    for dil, (dq, dkp, dkc, dvp, dvc) in zip(DILATIONS + (1,), parts_a + [parts_c]):
        w = dkp.shape[1]
        here = _strip_rows(DZ_TA, w)
        if dil == 1:
            args += [dq, dkp, dkp, dkc, dvp, dvp, dvc]
            in_specs += [_strip_rows(DZ_TA, A_WIDTH), here, ahead(w, 1), here, here, ahead(w, 1), here]
        else:
            k = 8 * dil // DZ_TA
            args += [dq, dkp, dkc, dvp, dvc]
            in_specs += [_strip_rows(DZ_TA, A_WIDTH), ahead(w, k), here, ahead(w, k), here]
    n_att = len(args)
    args = [_strips(a) for a in args] + [_strips(dyb), _strips(dyb), _strips(zb), _strips(zb), _strips(zb), cw]
    in_specs += [_strip_rows(DZ_TA, CONV_CH), _next_rows(DZ_TA, CONV_CH, nb), _strip_rows(DZ_TA, ZB_W),
                 _prev_rows(DZ_TA, ZB_W), _next_rows(DZ_TA, ZB_W, nb), _whole((HALO, CONV_CH))]

    def body(*refs):
        att = list(refs[:n_att])
        dyb_ref, dybn_ref, zb_ref, zbp_ref, zbn_ref, cw_ref, dz_ref, dcw_ref = refs[n_att:]
        i = pl.program_id(0)

        @pl.when(i == 0)
        def _():
            dcw_ref[...] = jnp.zeros_like(dcw_ref)

        def shifted(dil):
            if dil == 1:
                dq_r, kp0, kp1, dkc_r, vp0, vp1, dvc_r = [att.pop(0) for _ in range(7)]
                live = i + 1 < nb
                half = DZ_TA // 2
                dkp = jnp.concatenate([kp0[:, :, half:, :], jnp.where(live, kp1[:, :, :half, :], 0.0)], axis=2)
                dvp = jnp.concatenate([vp0[:, :, half:, :], jnp.where(live, vp1[:, :, :half, :], 0.0)], axis=2)
            else:
                dq_r, dkp_r, dkc_r, dvp_r, dvc_r = [att.pop(0) for _ in range(5)]
                live = i + 8 * dil // DZ_TA < nb
                dkp, dvp = jnp.where(live, dkp_r[...], 0.0), jnp.where(live, dvp_r[...], 0.0)
            return dq_r[...], dkc_r[...] + dkp, dvc_r[...] + dvp

        dq, dk, dv = shifted(DILATIONS[0])
        for dil in DILATIONS[1:]:
            dq2, dk2, dv2 = shifted(dil)
            dq, dk, dv = dq + dq2, dk + dk2, dv + dv2
        dz_ref[:, :, :, 0:A_WIDTH] = dq.astype(BF)
        dz_ref[:, :, :, A_WIDTH:2 * A_WIDTH] = dk.astype(BF)
        dz_ref[:, :, :, 2 * A_WIDTH:ZA_W] = dv.astype(BF)
        dq, dk, dv = shifted(1)
        c0 = ZA_W + ZB_W
        dz_ref[:, :, :, c0:c0 + A_WIDTH] = dq.astype(BF)
        dz_ref[:, :, :, c0 + A_WIDTH:c0 + A_WIDTH + C_KV_WIDTH] = dk.astype(BF)
        dz_ref[:, :, :, c0 + A_WIDTH + C_KV_WIDTH:IN_WIDTH] = dv.astype(BF)

        cw = cw_ref[...]
        prev = jnp.where(i > 0, zbp_ref[...], 0.0)
        gb, gc, xb, u, u1, u2, c = _conv_strips(zb_ref[...], prev, cw)
        dyb = dyb_ref[...]
        dc = [_strip(dyb, b) * gb[b] for b in range(N_STRIPS)]
        dcn = jnp.where(i + 1 < nb, dybn_ref[...] * zbn_ref[:, :, :CONV_CH], 0.0)
        wrapped = [_shift_up(dc[0], 1, dcn[0]), _shift_up(dc[1], 1, dcn[1])]
        upd = [jnp.zeros((1, CONV_CH), F32)] * 3
        for b in range(N_STRIPS):
            dc1 = dc[b + 1] if b + 1 < N_STRIPS else wrapped[0]
            dc2 = dc[b + 2] if b + 2 < N_STRIPS else wrapped[b + 2 - N_STRIPS]
            du = cw[2:3, :] * dc[b] + cw[1:2, :] * dc1 + cw[0:1, :] * dc2
            f, e = b % 4, b // 4
            dz_ref[f, e, :, ZA_W:ZA_W + CONV_CH] = (_strip(dyb, b) * c[b]).astype(BF)
            dz_ref[f, e, :, ZA_W + CONV_CH:ZA_W + 2 * CONV_CH] = (du * xb[b]).astype(BF)
            dz_ref[f, e, :, ZA_W + 2 * CONV_CH:c0] = (du * gc[b]).astype(BF)
            for t, uu in enumerate((u2[b], u1[b], u[b])):
                upd[t] = upd[t] + jnp.sum(dc[b] * uu, axis=0, keepdims=True)
        row = lax.broadcasted_iota(jnp.int32, (HALO, CONV_CH), 0)
        tile = jnp.zeros((HALO, CONV_CH), F32)
        for t in range(3):
            tile = jnp.where(row == t, upd[t], tile)
        dcw_ref[...] += tile

    dz, dcw = pl.pallas_call(
        body, grid=(nb,), name="dz_assemble", in_specs=in_specs,
        out_specs=[_strip_rows(DZ_TA, IN_WIDTH), _whole((HALO, CONV_CH))],
        out_shape=[jax.ShapeDtypeStruct((4, 4, na, IN_WIDTH), BF), jax.ShapeDtypeStruct((HALO, CONV_CH), F32)],
        compiler_params=_cparams("arbitrary"),
    )(*args)
    return dz.reshape(s, IN_WIDTH), dcw


def _qkv_bwd(dz, dx1, x, g, w_all, l, tb):
    s, d = x.shape

    def body(dz_ref, dx1_ref, x_ref, g_ref, w_ref, dx_ref, dg_ref):
        i = pl.program_id(0)

        @pl.when(i == 0)
        def _():
            dg_ref[...] = jnp.zeros_like(dg_ref)

        n = IN_WIDTH // N_CHIPS
        dh = _nt(dz_ref[:, 0:n], w_ref[0])
        for k in range(1, N_CHIPS):
            dh = dh + _nt(dz_ref[:, k * n:(k + 1) * n], w_ref[k])
        xv = x_ref[...]
        r = _rms_scale(xv)
        xhat = xv * r
        dg_ref[...] += jnp.sum(dh * xhat, axis=0, keepdims=True)
        dx_ref[...] = dx1_ref[...] + _norm_bwd(dh * g_ref[...], xhat, r)

    return pl.pallas_call(
        body, grid=(s // tb,), name="qkv_bwd",
        in_specs=[_rows(tb, IN_WIDTH), _rows(tb, d), _rows(tb, d), _whole((1, d)),
                  _layer((N_CHIPS, d, IN_WIDTH // N_CHIPS), l)],
        out_specs=[_rows(tb, d), _whole((HALO, d))],
        out_shape=[jax.ShapeDtypeStruct((s, d), F32), jax.ShapeDtypeStruct((HALO, d), F32)],
        compiler_params=_cparams("arbitrary"),
    )(dz, dx1, x, g, w_all)


def _tile_rows(rows):
    return jnp.pad(rows, ((0, HALO - rows.shape[0]), (0, 0)))


def _to_strips(a):
    s, d = a.shape
    return a.reshape(s // N_STRIPS, 4, 4, d).transpose(2, 1, 0, 3).reshape(s, d)


def _from_strips(a):
    s, d = a.shape
    return a.reshape(4, 4, s // N_STRIPS, d).transpose(2, 1, 0, 3).reshape(s, d)


def _local_step(x, tgt, fetch, ff, sinks, g_mix, g_group, g_mlp, g_final, emit):
    s, d = x.shape
    depth = g_mix.shape[0]
    tb = min(512, s)
    tf = ff // N_CHIPS
    saved = []
    for l in range(depth):
        w_in, _, _, _, conv_w = fetch(0, l, x)
        cw = _tile_rows(conv_w[l])
        sk = jnp.pad(sinks[l].reshape(1, N_HEADS), ((0, HALO - 1), (0, 128 - N_HEADS)))
        h, za, zb, zc = _qkv_fwd(x, g_mix[l][None], w_in, l, tb)
        parts_a = [_attn_fwd(za, dil, A_WIDTH, 1, 2, 1, A_MAX_DIST, None, "attn_a_fwd_%d" % dil) for dil in DILATIONS]
        part_c = _attn_fwd(zc, 1, C_KV_WIDTH, 3, 4, C_GROUP, C_MAX_DIST, sk, "attn_c_fwd")
        ya, lse_a, yc, lse_c = _attn_merge(parts_a, part_c, tb)
        w_in, w_o, w1, w2, _ = fetch(1, l, yc)
        x1, yb = _mix_fwd(x, ya, yc, zb, cw, g_group[l][None], w_o, l, tb)
        x2, h2, ap = _mlp_fwd(x1, g_mlp[l][None], w1, w2, l, tb, tf)
        saved.append((x, h, za, zb, zc, ya, lse_a, yc, lse_c, yb, x1, h2, ap, cw, sk))
        x = x2
    dx, loss_tile, dg_final = _loss_head(x, g_final[None], tgt, tb)
    grads = [None] * depth
    tok = jnp.zeros((), F32)
    for l in reversed(range(depth)):
        x0, h, za, zb, zc, ya, lse_a, yc, lse_c, yb, x1, h2, ap, cw, sk = saved[l]
        dx1, dap, dg_mlp = _mlp_bwd(dx, x1, ap, g_mlp[l][None] + tok, w1, w2, l, tb, tf)
        tok = emit(l, 3, _wgrad(ap, dx, min(1024, ff), d, tb, "wgrad_ff_out", relu2=True))
        tok = tok + emit(l, 2, _wgrad(h2, dap, d, min(1024, ff), tb, "wgrad_ff_in"))
        n, dya, dyc, dd_a, dd_c, dyb, dg_group = _mix_bwd(dx1, ya, yb, yc, g_group[l][None] + tok, w_o, l, tb)
        tok = emit(l, 1, _wgrad(n, dx1, MIX_WIDTH, d, tb, "wgrad_o"))
        cw = cw + tok
        parts_a = [_attn_bwd(za, dya, lse_a, dd_a, dil, A_WIDTH, 1, 2, 1, A_MAX_DIST, None, "attn_a_bwd_%d" % dil)
                   for dil in DILATIONS]
        *parts_c, dsink = _attn_bwd(zc, dyc, lse_c, dd_c, 1, C_KV_WIDTH, 3, 4, C_GROUP, C_MAX_DIST, sk, "attn_c_bwd")
        dz, dcw = _dz_assemble(parts_a, parts_c, dyb, zb, cw)
        dx, dg_mix = _qkv_bwd(dz, dx1, x0, g_mix[l][None], w_in, l, tb)
        tok = emit(l, 0, _wgrad(h, dz, d, IN_WIDTH // 4, tb, "wgrad_in"))
        grads[l] = (dcw, dsink, dg_mix, dg_group, dg_mlp)
    return loss_tile, dx, grads, dg_final


ANY = pl.BlockSpec(memory_space=pl.ANY)
SHARD_AXES = (2, 1, 2, 1)
N_BIG = len(SHARD_AXES)
N_CHIPS = 4
N_DEV = 8


def _mesh_pos():
    return lax.axis_index("x"), lax.axis_index("y"), lax.axis_index("c")


def _flip(v, bit):
    return 1 - v if bit else v


def _place_shard(shard, chip_arr, name):
    _, rows, cols = shard.shape
    tr = min(256, rows)

    def body(chip_ref, x_ref, o_ref):
        o_ref[...] = x_ref[...].astype(BF)

    return pl.pallas_call(
        body, name=name,
        grid_spec=pltpu.PrefetchScalarGridSpec(
            num_scalar_prefetch=1, grid=(2, rows // tr),
            in_specs=[pl.BlockSpec((None, tr, cols), lambda l, i, chip: (l, i, 0))],
            out_specs=pl.BlockSpec((None, None, tr, cols), lambda l, i, chip: (l, chip[0], i, 0))),
        out_shape=jax.ShapeDtypeStruct((2, N_CHIPS, rows, cols), BF),
        compiler_params=_cparams("parallel", "parallel"),
    )(chip_arr, shard)


HBM = pl.BlockSpec(memory_space=pltpu.HBM)
SEM = pl.BlockSpec(memory_space=pltpu.SEMAPHORE)
EFFECT = pltpu.SideEffectType.DATAFLOW_SIDE_EFFECTING

GATHER_GROUPS = (((0, 0),), ((1, 0), (2, 0), (3, 0)), ((0, 1),), ((1, 1), (2, 1), (3, 1)))
GATHER_STARTS = ((0, 1), (2, 3))


def _gather_copies(arrs, group, send_sems, recv_sems):
    x, y, c = _mesh_pos()
    me = 2 * x + y
    out = []
    for i, (w, layer) in enumerate(group):
        mine = arrs[w].at[layer, me]
        for j, (qx, qy) in enumerate([(1 - x, y), (x, 1 - y), (1 - x, 1 - y)]):
            landed = arrs[w].at[layer, 2 * qx + qy]
            out.append(tuple(pltpu.make_async_remote_copy(
                src_ref=piece, dst_ref=piece, send_sem=send_sems.at[i * 3 + j], recv_sem=recv_sems.at[i * 3 + j],
                device_id=(qx, qy, c), device_id_type=MESH) for piece in (mine, landed)))
    return out


def _conv_copies(conv_src, conv_dst, send_sems, recv_sems):
    x, y, c = _mesh_pos()
    out = []
    for j, (qx, qy) in enumerate([(1 - x, y), (x, 1 - y), (1 - x, 1 - y)]):
        out.append(tuple(pltpu.make_async_remote_copy(
            src_ref=conv_src, dst_ref=conv_dst.at[q], send_sem=send_sems.at[j], recv_sem=recv_sems.at[j],
            device_id=(qx, qy, c), device_id_type=MESH) for q in (2 * x + y, 2 * qx + qy)))
    return out


def _gather_start(groups, arrs, conv, name):
    n_sems = 2 * (len(groups) + (conv is not None))

    def body(*refs):
        mats = refs[:N_BIG]
        sems = refs[n_op:n_op + n_sems]
        if conv is not None:
            for cp, _ in _conv_copies(refs[N_BIG], refs[N_BIG + 1], sems[-2], sems[-1]):
                cp.start()
        for k, g in enumerate(groups):
            for cp, _ in _gather_copies(mats, GATHER_GROUPS[g], sems[2 * k], sems[2 * k + 1]):
                cp.start()

    sem_shapes = []
    for n in [len(GATHER_GROUPS[g]) for g in groups] + ([1] if conv is not None else []):
        sem_shapes += [pltpu.SemaphoreType.DMA((3 * n,))] * 2
    operands = list(arrs) + ([] if conv is None else list(conv))
    n_op = len(operands)
    res = pl.pallas_call(
        body, name=name,
        out_shape=tuple(sem_shapes) + tuple(pltpu.HBM(a.shape, a.dtype) for a in operands),
        in_specs=(HBM,) * n_op, out_specs=(SEM,) * n_sems + (HBM,) * n_op,
        input_output_aliases={i: n_sems + i for i in range(n_op)},
        compiler_params=pltpu.CompilerParams(has_side_effects=EFFECT),
    )(*[pltpu.with_memory_space_constraint(a, pltpu.HBM) for a in operands])
    return res[:n_sems], list(res[n_sems:])


def _gather_wait(k, sems, arrs, conv, after, name):
    group = GATHER_GROUPS[k]
    mats = sorted({w for w, _ in group})
    n_conv = 0 if conv is None else 2

    def body(*refs):
        local = refs[:len(mats)]
        arrs_ref = [None] * N_BIG
        for w, ref in zip(mats, local):
            arrs_ref[w] = ref
        pos = len(mats) + n_conv
        copies = _gather_copies(arrs_ref, group, refs[pos], refs[pos + 1])
        if conv is not None:
            copies += _conv_copies(refs[len(mats)], refs[len(mats) + 1], refs[pos + 2], refs[pos + 3])
        for send, recv in copies:
            recv.wait_recv()
            send.wait_send()

    operands = [arrs[w] for w in mats] + ([] if conv is None else [conv[1], conv[2]])
    sem_ops = list(sems) + ([] if conv is None else list(conv[0]))
    n_op = len(operands)
    res = pl.pallas_call(
        body, name=name, out_shape=tuple(pltpu.HBM(a.shape, a.dtype) for a in operands),
        in_specs=(HBM,) * n_op + (SEM,) * len(sem_ops) + (ANY,), out_specs=(HBM,) * n_op,
        input_output_aliases={i: i for i in range(n_op)},
        compiler_params=pltpu.CompilerParams(has_side_effects=EFFECT),
    )(*operands, *sem_ops, after)
    arrs = list(arrs)
    for w, a in zip(mats, res):
        arrs[w] = a
    return arrs, (res[-1] if conv is not None else None)


def _grad_shard(ref, w, chip, n):
    start = pl.multiple_of(chip * n, 128)
    if SHARD_AXES[w] == 2:
        return ref.at[:, pl.ds(start, n)]
    return ref.at[pl.ds(start, n), :]


def _slot_shape(g, w):
    shape = list(g.shape)
    shape[SHARD_AXES[w] - 1] //= N_CHIPS
    return (N_DEV - 1,) + tuple(shape)


def _scatter_copies(g_ref, land_ref, send_sems, recv_sems, layer, w):
    x, y, c = _mesh_pos()
    n = g_ref.shape[SHARD_AXES[w] - 1] // N_CHIPS
    out = []
    for r in range(1, N_DEV):
        tx, ty, tc = _flip(x, r & 4), _flip(y, r & 2), _flip(c, r & 1)
        cp = pltpu.make_async_remote_copy(
            src_ref=_grad_shard(g_ref, w, 2 * tx + ty, n), dst_ref=land_ref.at[r - 1], send_sem=send_sems.at[r - 1],
            recv_sem=recv_sems.at[r - 1], device_id=(tx, ty, tc), device_id_type=MESH)
        out.append((cp, (c != layer) if r & 1 else (c == layer)))
    return out


def _scatter_start(g, land, layer, w, name):
    def body(g_ref, land_ref, send_sems, recv_sems, g_thru, land_thru, token):
        for cp, mine in _scatter_copies(g_ref, land_ref, send_sems, recv_sems, layer, w):
            @pl.when(mine)
            def _():
                cp.start()
        token[...] = jnp.zeros_like(token)

    return pl.pallas_call(
        body, name=name,
        out_shape=(pltpu.SemaphoreType.DMA((N_DEV - 1,)), pltpu.SemaphoreType.DMA((N_DEV - 1,)),
                   pltpu.HBM(g.shape, g.dtype), pltpu.HBM(land.shape, land.dtype),
                   jax.ShapeDtypeStruct((HALO, 128), F32)),
        in_specs=(HBM, HBM), out_specs=(SEM, SEM, HBM, HBM, pl.BlockSpec(memory_space=pltpu.VMEM)),
        input_output_aliases={0: 2, 1: 3}, compiler_params=pltpu.CompilerParams(has_side_effects=EFFECT),
    )(pltpu.with_memory_space_constraint(g, pltpu.HBM), pltpu.with_memory_space_constraint(land, pltpu.HBM))


def _scatter_wait(started, land, after, w, name):
    def body(g0_ref, g1_ref, land_ref, ss0, rs0, ss1, rs1, after_ref, g0_out, g1_out, land_out):
        c = lax.axis_index("c")
        for layer, g_ref, ss, rs in ((0, g0_ref, ss0, rs0), (1, g1_ref, ss1, rs1)):
            for cp, mine in _scatter_copies(g_ref, land_ref, ss, rs, layer, w):
                @pl.when(mine)
                def _():
                    cp.wait_send()

                @pl.when(c == layer)
                def _():
                    cp.wait_recv()

    (ss0, rs0, g0), (ss1, rs1, g1) = started
    return pl.pallas_call(
        body, name=name,
        out_shape=(pltpu.HBM(g0.shape, g0.dtype), pltpu.HBM(g1.shape, g1.dtype), pltpu.HBM(land.shape, land.dtype)),
        in_specs=(HBM, HBM, HBM, SEM, SEM, SEM, SEM, ANY), out_specs=(HBM, HBM, HBM),
        input_output_aliases={0: 0, 1: 1, 2: 2}, compiler_params=pltpu.CompilerParams(has_side_effects=EFFECT),
    )(g0, g1, land, ss0, rs0, ss1, rs1, after)


def _sum_slots(g0, g1, slots, w, pos_arr, name):
    _, rows, cols = slots.shape
    tr = min(256, rows)
    nr = rows // tr
    if SHARD_AXES[w] == 2:
        own = pl.BlockSpec((tr, cols), lambda i, pos: (i, pos[0]))
    else:
        own = pl.BlockSpec((tr, cols), lambda i, pos: (pos[0] * nr + i, 0))

    def body(pos_ref, own0_ref, own1_ref, s_ref, o_ref):
        acc = jnp.where(pos_ref[1] == 0, own0_ref[...], own1_ref[...]).astype(F32)
        for r in range(N_DEV - 1):
            acc = acc + s_ref[r].astype(F32)
        o_ref[...] = acc

    return pl.pallas_call(
        body, name=name,
        grid_spec=pltpu.PrefetchScalarGridSpec(
            num_scalar_prefetch=1, grid=(nr,),
            in_specs=[own, own, pl.BlockSpec((N_DEV - 1, tr, cols), lambda i, pos: (0, i, 0))],
            out_specs=pl.BlockSpec((tr, cols), lambda i, pos: (i, 0))),
        out_shape=jax.ShapeDtypeStruct((rows, cols), F32), compiler_params=_cparams("parallel"),
    )(pos_arr, g0, g1, slots)


def _swap_layers(halves):
    def body(*refs):
        srcs, dsts = refs[:N_BIG], refs[N_BIG:2 * N_BIG]
        send_sems, recv_sems = refs[2 * N_BIG:]
        x, y, c = _mesh_pos()
        sends = [pltpu.make_async_remote_copy(src_ref=srcs[w], dst_ref=dsts[w], send_sem=send_sems.at[w],
                                              recv_sem=recv_sems.at[w], device_id=(x, y, 1 - c), device_id_type=MESH)
                 for w in range(N_BIG)]
        for cp in sends:
            cp.start()
        for cp in sends:
            cp.wait_recv()
        for cp in sends:
            cp.wait_send()

    return pl.pallas_call(
        body, name="swap_layers", in_specs=[ANY] * N_BIG, out_specs=[ANY] * N_BIG,
        out_shape=[jax.ShapeDtypeStruct(h.shape, h.dtype) for h in halves],
        scratch_shapes=[pltpu.SemaphoreType.DMA((N_BIG,)), pltpu.SemaphoreType.DMA((N_BIG,))],
    )(*halves)


def _adamw_math(w, g, m, v):
    m = ADAM_B1 * m + (1.0 - ADAM_B1) * g
    v = ADAM_B2 * v + (1.0 - ADAM_B2) * jnp.square(g)
    m_hat = m / (1.0 - ADAM_B1 ** ADAM_STEP)
    v_hat = v / (1.0 - ADAM_B2 ** ADAM_STEP)
    delta = -ADAM_LR * (m_hat / (jnp.sqrt(v_hat) + ADAM_EPS) + ADAM_WD * w)
    return delta, m, v


def _adamw(w, g_own, g_other, m, v, pos_arr, name):
    shape = w.shape
    _, rows, cols = shape
    tr = min(256, rows)

    def body(pos_ref, w_ref, own_ref, other_ref, m_ref, v_ref, g_ref, d_ref, m2_ref, v2_ref):
        g = jnp.where(pl.program_id(0) == pos_ref[1], own_ref[...], other_ref[...])
        g_ref[...] = g
        d_ref[...], m2_ref[...], v2_ref[...] = _adamw_math(w_ref[...], g, m_ref[...], v_ref[...])

    full = pl.BlockSpec((None, tr, cols), lambda l, i, pos: (l, i, 0))
    half = pl.BlockSpec((tr, cols), lambda l, i, pos: (i, 0))
    return pl.pallas_call(
        body, name=name,
        grid_spec=pltpu.PrefetchScalarGridSpec(
            num_scalar_prefetch=1, grid=(2, rows // tr),
            in_specs=[full, half, half, full, full], out_specs=[full] * 4),
        out_shape=[jax.ShapeDtypeStruct(shape, F32)] * 4, compiler_params=_cparams("parallel", "parallel"),
    )(pos_arr, w, g_own, g_other, m, v)


def _small_sync(part, w, m, v):
    rows, cols = part.shape

    def body(p_ref, w_ref, m_ref, v_ref, g_ref, d_ref, m2_ref, v2_ref, slots, send_sems, recv_sems):
        x, y, c = _mesh_pos()
        me = 4 * x + 2 * y + c
        slots[me] = p_ref[...]
        sends = []
        for r in range(1, N_DEV):
            to = (_flip(x, r & 4), _flip(y, r & 2), _flip(c, r & 1))
            sends.append(pltpu.make_async_remote_copy(
                src_ref=p_ref, dst_ref=slots.at[me], send_sem=send_sems.at[r - 1], recv_sem=recv_sems.at[r - 1],
                device_id=to, device_id_type=MESH))
        for cp in sends:
            cp.start()
        for cp in sends:
            cp.wait_recv()
        for cp in sends:
            cp.wait_send()
        g = slots[0]
        for i in range(1, N_DEV):
            g = g + slots[i]
        g_ref[...] = g
        d_ref[...], m2_ref[...], v2_ref[...] = _adamw_math(w_ref[...], g, m_ref[...], v_ref[...])

    vm = pl.BlockSpec(memory_space=pltpu.VMEM)
    return pl.pallas_call(
        body, name="small_sync", in_specs=[vm] * 4, out_specs=[vm] * 4,
        out_shape=[jax.ShapeDtypeStruct((rows, cols), F32)] * 4,
        scratch_shapes=[pltpu.VMEM((N_DEV, rows, cols), F32), pltpu.SemaphoreType.DMA((N_DEV - 1,)),
                        pltpu.SemaphoreType.DMA((N_DEV - 1,))],
    )(part, w, m, v)


def _pack_small(d, g_mix, g_group, g_mlp, g_final, conv_full, sinks, scalar):
    def part(rows):
        return jnp.pad(rows, ((0, HALO - rows.shape[0]), (0, d - rows.shape[1])))
    return jnp.concatenate([part(g_mix), part(g_group), part(g_mlp), part(g_final[None]),
                            part(conv_full.reshape(6, CONV_CH)), part(sinks.reshape(2, N_HEADS)),
                            part(scalar.reshape(1, 1))], axis=0)


def _unpack_small(p, dm):
    return (p[0:2, :dm], p[8:10, :MIX_WIDTH], p[16:18, :dm], p[24, :dm], p[32:38, :CONV_CH].reshape(2, 3, CONV_CH),
            p[40:42, :N_HEADS].reshape(2, 2, C_GROUP), p[48, 0])


def kernel(x, w_in, conv_w, sinks, g_mix, g_group, w_o, g_mlp, w_ff_in, w_ff_out, g_final, loss_target, m_w_in, m_conv_w, m_sinks, m_g_mix, m_g_group, m_w_o, m_g_mlp, m_w_ff_in, m_w_ff_out, m_g_final, v_w_in, v_conv_w, v_sinks, v_g_mix, v_g_group, v_w_o, v_g_mlp, v_w_ff_in, v_w_ff_out, v_g_final):
    d = max(x.shape[2], MIX_WIDTH)
    chip = 2 * lax.axis_index("x") + lax.axis_index("y")
    conv_n = conv_w.shape[2]

    pos_arr = jnp.stack([chip, lax.axis_index("c")]).astype(jnp.int32)
    placed = [_place_shard(w, pos_arr[:1], "place_shard_%d" % i)
              for i, w in enumerate((w_in, w_o, w_ff_in, w_ff_out))]
    conv_tile = jnp.pad(conv_w.reshape(6, conv_n), ((0, HALO - 6), (0, 128 - conv_n)))
    sems0, thru = _gather_start(GATHER_STARTS[0], placed,
                                (conv_tile, lax.empty((N_CHIPS,) + conv_tile.shape, conv_tile.dtype)), "gather_start_0")
    full = {"arrs": thru[:N_BIG], "conv": None, "sems": list(sems0[:4])}

    def fetch(stage, layer, after):
        k = 2 * layer + stage
        sems = full["sems"][2 * k:2 * k + 2]
        if k == 0:
            full["arrs"], land = _gather_wait(0, sems, full["arrs"], (sems0[-2:], thru[N_BIG], thru[N_BIG + 1]),
                                              after, "gather_wait_0")
            conv_all = lax.dynamic_update_slice(land, conv_tile[None], (chip, 0, 0))
            full["conv"] = conv_all[:, :6, :conv_n].reshape(N_CHIPS, 2, 3, conv_n).transpose(1, 2, 0, 3).reshape(
                2, 3, CONV_CH)
        else:
            full["arrs"], _ = _gather_wait(k, sems, full["arrs"], None, after, "gather_wait_%d" % k)
        if k == 1:
            sems1, full["arrs"] = _gather_start(GATHER_STARTS[1], full["arrs"], None, "gather_start_1")
            full["sems"] += list(sems1)
        return (*full["arrs"], full["conv"])

    lands, started = [None] * N_BIG, {}

    def emit(layer, w, g):
        if lands[w] is None:
            lands[w] = lax.empty(_slot_shape(g, w), g.dtype)
        *started[layer, w], lands[w], token = _scatter_start(g, lands[w], layer, w, "scatter_start_%d_%d" % (layer, w))
        return token[0, 0]

    loss_tile, dx, grads, dg_final = _local_step(_to_strips(x[0]), _to_strips(loss_target[0]), fetch, w_ff_in.shape[2] * N_CHIPS,
                                                 sinks, g_mix, g_group, g_mlp, g_final, emit)

    own = []
    for w in range(N_BIG):
        g0, g1, slots = _scatter_wait((started[0, w], started[1, w]), lands[w], dx, w, "scatter_wait_%d" % w)
        own.append(_sum_slots(g0, g1, slots, w, pos_arr, "sum_slots_%d" % w))
    other = _swap_layers(own)

    def both(i):
        return jnp.stack([grads[0][i][0], grads[1][i][0]])
    dconv = jnp.stack([grads[0][0][:3], grads[1][0][:3]])
    dsinks = jnp.stack([grads[0][1][0, :N_HEADS], grads[1][1][0, :N_HEADS]])
    part = _pack_small(d, both(2), both(3), both(4), dg_final[0], dconv, dsinks, loss_tile[0, 0])

    def spread(shard):
        return lax.dynamic_update_slice(jnp.zeros((2, 3, CONV_CH), F32), shard, (0, 0, chip * conv_n))
    zero = jnp.zeros((), F32)
    packs = [_pack_small(d, a, b, c_, e, spread(f), g_, zero) for a, b, c_, e, f, g_ in (
        (g_mix, g_group, g_mlp, g_final, conv_w, sinks),
        (m_g_mix, m_g_group, m_g_mlp, m_g_final, m_conv_w, m_sinks),
        (v_g_mix, v_g_group, v_g_mlp, v_g_final, v_conv_w, v_sinks))]
    small = [_unpack_small(p, x.shape[2]) for p in _small_sync(part, *packs)]

    def shard_of(full):
        return lax.dynamic_slice(full, (0, 0, chip * conv_n), (2, 3, conv_n))
    small = [(s[0], s[1], s[2], s[3], shard_of(s[4]), s[5], s[6]) for s in small]
    loss = small[0][6]

    big = [_adamw(w, own[i], other[i], m, v, pos_arr, "adamw_%d" % i) for i, (w, m, v) in enumerate((
        (w_in, m_w_in, v_w_in), (w_o, m_w_o, v_w_o), (w_ff_in, m_w_ff_in, v_w_ff_in),
        (w_ff_out, m_w_ff_out, v_w_ff_out)))]

    def ordered(kind):
        b = [big[i][kind] for i in range(N_BIG)]
        s = small[kind]
        return [b[0], s[4], s[5], s[0], s[1], b[1], s[2], b[2], b[3], s[3]]

    return (loss, _from_strips(dx)[None], *ordered(0), *ordered(1), *ordered(2), *ordered(3))
```

```python
import functools

import jax
import jax.numpy as jnp
from jax import lax
from jax.experimental import pallas as pl
from jax.experimental.pallas import tpu as pltpu

HEAD_DIM = 64
N_HEADS = 6
C_GROUP = 3
A_WIDTH = N_HEADS * HEAD_DIM
C_KV_WIDTH = 2 * HEAD_DIM
CONV_CH = 256
ZA_W = 3 * A_WIDTH
ZB_W = 3 * CONV_CH
ZC_W = A_WIDTH + 2 * C_KV_WIDTH
IN_WIDTH = ZA_W + ZB_W + ZC_W
MIX_WIDTH = A_WIDTH + CONV_CH + A_WIDTH
DILATIONS = (1, 4, 16)
A_MAX_DIST = 128
C_MAX_DIST = 127
TQ = 128
EPS = 1e-6
SCALE = HEAD_DIM ** -0.5
NEG = -1e30
HALO = 8

ADAM_LR = 0.001
ADAM_B1 = 0.9
ADAM_B2 = 0.999
ADAM_EPS = 1e-08
ADAM_WD = 0.01
ADAM_STEP = 10

BF = jnp.bfloat16
F32 = jnp.float32
MESH = pl.DeviceIdType.MESH
VMEM_LIMIT = 56 * 1024 * 1024


def _cparams(*sem):
    return pltpu.CompilerParams(dimension_semantics=sem, vmem_limit_bytes=VMEM_LIMIT)


def _nt(a, b):
    return lax.dot_general(a, b, (((1,), (1,)), ((), ())), preferred_element_type=F32)


def _tn(a, b):
    return lax.dot_general(a, b, (((0,), (0,)), ((), ())), preferred_element_type=F32)


def _nn(a, b):
    return jnp.dot(a, b, preferred_element_type=F32)


def _rows(tb, w):
    return pl.BlockSpec((tb, w), lambda i: (i, 0))


def _whole(shape):
    return pl.BlockSpec(shape, lambda *_: (0,) * len(shape))


def _layer(shape, l):
    return pl.BlockSpec((None,) + shape, lambda *_: (l,) + (0,) * len(shape))


def _rms_scale(v):
    return lax.rsqrt(jnp.mean(v * v, axis=-1, keepdims=True) + EPS)


def _norm_bwd(dxhat, xhat, r):
    return r * (dxhat - xhat * jnp.mean(dxhat * xhat, axis=-1, keepdims=True))


def _qkv_fwd(x, g, w_all, l, tb):
    s, d = x.shape

    def body(x_ref, g_ref, w_ref, h_ref, za_ref, zb_ref, zc_ref):
        xv = x_ref[...]
        h = ((xv * _rms_scale(xv)) * g_ref[...]).astype(BF)
        h_ref[...] = h
        z = jnp.concatenate([_nn(h, w_ref[k]) for k in range(N_CHIPS)], axis=1)
        za_ref[...] = z[:, :ZA_W]
        zb_ref[...] = z[:, ZA_W:ZA_W + ZB_W]
        zc_ref[...] = z[:, ZA_W + ZB_W:]

    return pl.pallas_call(
        body, grid=(s // tb,), name="qkv_fwd",
        in_specs=[_rows(tb, d), _whole((1, d)), _layer((N_CHIPS, d, IN_WIDTH // N_CHIPS), l)],
        out_specs=[_rows(tb, d), _rows(tb, ZA_W), _rows(tb, ZB_W), _rows(tb, ZC_W)],
        out_shape=[jax.ShapeDtypeStruct((s, d), BF), jax.ShapeDtypeStruct((s, ZA_W), F32),
                   jax.ShapeDtypeStruct((s, ZB_W), F32), jax.ShapeDtypeStruct((s, ZC_W), F32)],
        compiler_params=_cparams("parallel"),
    )(x, g, w_all)


N_STRIPS = 16


def _strips(a):
    s, w = a.shape
    return a.reshape(4, 4, s // N_STRIPS, w)


def _p_grid(s, dil):
    na = s // N_STRIPS
    return {16: (4, 4, na // TQ), 4: (4, na // 32), 1: (na // 8,)}[dil]


def _p_spec(dil, cw, col, prev=False):
    def blk(j):
        return jnp.maximum(j - 1, 0) if prev else j
    if dil == 16:
        return pl.BlockSpec((None, None, TQ, cw), lambda f, e, j: (f, e, blk(j), col))
    if dil == 4:
        return pl.BlockSpec((None, 4, 32, cw), lambda f, j: (f, 0, blk(j), col))
    return pl.BlockSpec((4, 4, 8, cw), lambda j: (0, 0, blk(j), col))


def _block_pos(i, dil):
    if dil == 16:
        return i
    if dil == 4:
        return 4 * (i % 32) + i // 32
    return 16 * (i % 8) + 4 * ((i // 8) % 4) + i // 32


def _band_mask(b, dil, max_dist):
    qi = _block_pos(lax.broadcasted_iota(jnp.int32, (TQ, 2 * TQ), 0), dil)
    col = lax.broadcasted_iota(jnp.int32, (TQ, 2 * TQ), 1)
    cur = col >= TQ
    dist = qi - _block_pos(col % TQ, dil) + jnp.where(cur, 0, TQ)
    return (dist >= 0) & (dist <= max_dist) & (cur | (b > 0))


def _hs(h):
    return slice(h * HEAD_DIM, (h + 1) * HEAD_DIM)


def _ld(ref, cols):
    v = ref[..., cols]
    return v.reshape(TQ, v.shape[-1])


def _st(ref, cols, val):
    ref[..., cols] = val.reshape(ref.shape[:-1] + (val.shape[-1],))


def _attn_fwd(z, dil, kw, kcol, vcol, n_rep, max_dist, name):
    s, zw = z.shape
    grid = _p_grid(s, dil)

    def body(q_ref, kp_ref, kc_ref, vp_ref, vc_ref, acc_ref, m_ref, l_ref):
        mask = _band_mask(pl.program_id(len(grid) - 1), dil, max_dist)
        for kh in range(N_HEADS // n_rep):
            k2 = jnp.concatenate([_ld(kp_ref, _hs(kh)), _ld(kc_ref, _hs(kh))], axis=0).astype(BF)
            v2 = jnp.concatenate([_ld(vp_ref, _hs(kh)), _ld(vc_ref, _hs(kh))], axis=0).astype(BF)
            for h in range(kh * n_rep, (kh + 1) * n_rep):
                q = _ld(q_ref, _hs(h)).astype(BF)
                sc = jnp.where(mask, _nt(q, k2) * SCALE, NEG)
                m = jnp.max(sc, axis=1, keepdims=True)
                p = jnp.exp(sc - m)
                _st(acc_ref, _hs(h), _nn(p.astype(BF), v2))
                _st(m_ref, _hs(h), jnp.broadcast_to(m, (TQ, HEAD_DIM)))
                _st(l_ref, _hs(h), jnp.broadcast_to(jnp.sum(p, axis=1, keepdims=True), (TQ, HEAD_DIM)))

    res = pl.pallas_call(
        body, grid=grid, name=name,
        in_specs=[_p_spec(dil, A_WIDTH, 0), _p_spec(dil, kw, kcol, True), _p_spec(dil, kw, kcol),
                  _p_spec(dil, kw, vcol, True), _p_spec(dil, kw, vcol)],
        out_specs=[_p_spec(dil, A_WIDTH, 0)] * 3,
        out_shape=[jax.ShapeDtypeStruct((4, 4, s // N_STRIPS, A_WIDTH), F32)] * 3,
        compiler_params=_cparams(*(("parallel",) * len(grid))),
    )(*[_strips(z)] * 5)
    return [a.reshape(s, A_WIDTH) for a in res]


def _attn_merge(parts_a, part_c, sink_row, tb):
    s = part_c[0].shape[0]
    n_a = len(parts_a)

    def body(*refs):
        ins, sink_ref = refs[:3 * n_a + 3], refs[3 * n_a + 3]
        ya_ref, lsea_ref, yc_ref, lsec_ref = refs[3 * n_a + 4:]
        ms = [ins[3 * p + 1][...] for p in range(n_a)]
        m = functools.reduce(jnp.maximum, ms)
        acc = jnp.zeros_like(m)
        l = jnp.zeros_like(m)
        for p in range(n_a):
            w = jnp.exp(ms[p] - m)
            acc = acc + w * ins[3 * p][...]
            l = l + w * ins[3 * p + 2][...]
        ya_ref[...] = acc / l
        lsea_ref[...] = m + jnp.log(l)
        acc_c, m_c, l_c = [r[...] for r in ins[3 * n_a:]]
        sk = sink_ref[...]
        m2 = jnp.maximum(m_c, sk)
        w = jnp.exp(m_c - m2)
        l2 = w * l_c + jnp.exp(sk - m2)
        yc_ref[...] = (w * acc_c) / l2
        lsec_ref[...] = m2 + jnp.log(l2)

    return pl.pallas_call(
        body, grid=(s // tb,), name="attn_merge",
        in_specs=[_rows(tb, A_WIDTH)] * (3 * n_a + 3) + [_whole((1, A_WIDTH))],
        out_specs=[_rows(tb, A_WIDTH)] * 4, out_shape=[jax.ShapeDtypeStruct((s, A_WIDTH), F32)] * 4,
        compiler_params=_cparams("parallel"),
    )(*[a for part in parts_a + [part_c] for a in part], sink_row)


def _shift_down(v, n, halo):
    rows = v.shape[0]
    out = pltpu.roll(v, n, 0)
    row = lax.broadcasted_iota(jnp.int32, v.shape, 0)
    for t in range(n):
        out = jnp.where(row == t, halo[HALO - n + t:HALO - n + t + 1, :], out)
    return out


def _shift_up(v, n, halo):
    rows = v.shape[0]
    out = pltpu.roll(v, rows - n, 0)
    row = lax.broadcasted_iota(jnp.int32, v.shape, 0)
    for t in range(n):
        out = jnp.where(row == rows - n + t, halo[t:t + 1, :], out)
    return out


def _strip(v, b):
    return v[b % 4, b // 4]


def _conv_strips(zb, prev, cw):
    gb = [_strip(zb, b)[:, :CONV_CH] for b in range(N_STRIPS)]
    gc = [_strip(zb, b)[:, CONV_CH:2 * CONV_CH] for b in range(N_STRIPS)]
    xb = [_strip(zb, b)[:, 2 * CONV_CH:] for b in range(N_STRIPS)]
    u = [g * v for g, v in zip(gc, xb)]
    uh = prev[:, :, CONV_CH:2 * CONV_CH] * prev[:, :, 2 * CONV_CH:]
    wrapped = {14: _shift_down(u[14], 1, uh[2]), 15: _shift_down(u[15], 1, uh[3])}
    u1 = [u[b - 1] if b >= 1 else wrapped[15] for b in range(N_STRIPS)]
    u2 = [u[b - 2] if b >= 2 else wrapped[14 + b] for b in range(N_STRIPS)]
    c = [cw[0:1, :] * u2[b] + cw[1:2, :] * u1[b] + cw[2:3, :] * u[b] for b in range(N_STRIPS)]
    return gb, gc, xb, u, u1, u2, c


def _strip_rows(ta, w):
    return pl.BlockSpec((4, 4, ta, w), lambda i: (0, 0, i, 0))


def _prev_rows(ta, w):
    return pl.BlockSpec((4, None, HALO, w), lambda i: (0, 3, jnp.maximum(i * (ta // HALO) - 1, 0), 0))


def _next_rows(ta, w, nblk):
    return pl.BlockSpec((4, None, HALO, w),
                        lambda i: (0, 0, jnp.minimum((i + 1) * (ta // HALO), nblk * (ta // HALO) - 1), 0))


def _mix_fwd(x, ya, yc, zb, cw, gg, wo_all, l, tb):
    s, d = x.shape
    ta = tb // N_STRIPS

    def body(x_ref, ya_ref, yc_ref, zb_ref, zbp_ref, cw_ref, gg_ref, wo_ref, x1_ref, yb_ref):
        i = pl.program_id(0)
        prev = jnp.where(i > 0, zbp_ref[...], 0.0)
        gb, _, _, _, _, _, c = _conv_strips(zb_ref[...], prev, cw_ref[...])
        for b in range(N_STRIPS):
            yb_ref[b % 4, b // 4] = gb[b] * c[b]
        yb = yb_ref[...].reshape(tb, CONV_CH)
        ya, yc = ya_ref[...].reshape(tb, A_WIDTH), yc_ref[...].reshape(tb, A_WIDTH)
        n = jnp.concatenate([ya * _rms_scale(ya), yb * _rms_scale(yb), yc * _rms_scale(yc)], axis=1)
        n = (n * gg_ref[...]).astype(BF)
        x1 = x_ref[...].reshape(tb, d) + _nn(n, wo_ref[...].reshape(MIX_WIDTH, d))
        x1_ref[...] = x1.reshape(4, 4, ta, d)

    res = pl.pallas_call(
        body, grid=(s // tb,), name="mix_fwd",
        in_specs=[_strip_rows(ta, d), _strip_rows(ta, A_WIDTH), _strip_rows(ta, A_WIDTH), _strip_rows(ta, ZB_W),
                  _prev_rows(ta, ZB_W), _whole((HALO, CONV_CH)), _whole((1, MIX_WIDTH)),
                  _layer((N_CHIPS, MIX_WIDTH // N_CHIPS, d), l)],
        out_specs=[_strip_rows(ta, d), _strip_rows(ta, CONV_CH)],
        out_shape=[jax.ShapeDtypeStruct((4, 4, s // N_STRIPS, d), F32),
                   jax.ShapeDtypeStruct((4, 4, s // N_STRIPS, CONV_CH), F32)],
        compiler_params=_cparams("parallel"),
    )(_strips(x), _strips(ya), _strips(yc), _strips(zb), _strips(zb), cw, gg, wo_all)
    return res[0].reshape(s, d), res[1].reshape(s, CONV_CH)


def _mlp_fwd(x1, g, w1_all, w2_all, l, tb, tf):
    s, d = x1.shape
    ff = w1_all.shape[1] * w1_all.shape[3]
    nj = ff // tf

    def body(x_ref, g_ref, w1_ref, w2_ref, x2_ref, h2_ref, ap_ref, acc):
        j = pl.program_id(1)

        @pl.when(j == 0)
        def _():
            xv = x_ref[...]
            h2_ref[...] = ((xv * _rms_scale(xv)) * g_ref[...]).astype(BF)
            acc[...] = jnp.zeros_like(acc)

        ap = _nn(h2_ref[...], w1_ref[...])
        ap_ref[...] = ap.astype(BF)
        a = jnp.square(jnp.maximum(ap, 0.0)).astype(BF)
        acc[...] += _nn(a, w2_ref[...])

        @pl.when(j == nj - 1)
        def _():
            x2_ref[...] = x_ref[...] + acc[...]

    return pl.pallas_call(
        body, grid=(s // tb, nj), name="mlp_fwd",
        in_specs=[pl.BlockSpec((tb, d), lambda i, j: (i, 0)), _whole((1, d)),
                  pl.BlockSpec((None, None, d, tf), lambda i, j: (l, j, 0, 0)),
                  pl.BlockSpec((None, None, tf, d), lambda i, j: (l, j, 0, 0))],
        out_specs=[pl.BlockSpec((tb, d), lambda i, j: (i, 0)), pl.BlockSpec((tb, d), lambda i, j: (i, 0)),
                   pl.BlockSpec((tb, tf), lambda i, j: (i, j))],
        out_shape=[jax.ShapeDtypeStruct((s, d), F32), jax.ShapeDtypeStruct((s, d), BF),
                   jax.ShapeDtypeStruct((s, ff), BF)],
        scratch_shapes=[pltpu.VMEM((tb, d), F32)],
        compiler_params=_cparams("parallel", "arbitrary"),
    )(x1, g, w1_all, w2_all)


def _loss_head(x, g, tgt, tb):
    s, d = x.shape

    def body(x_ref, g_ref, t_ref, dx_ref, loss_ref, dg_ref):
        i = pl.program_id(0)

        @pl.when(i == 0)
        def _():
            loss_ref[...] = jnp.zeros_like(loss_ref)
            dg_ref[...] = jnp.zeros_like(dg_ref)

        xv = x_ref[...]
        r = _rms_scale(xv)
        xhat = xv * r
        err = xhat * g_ref[...] - t_ref[...]
        part = jnp.sum(jnp.mean(jnp.square(err), axis=-1, keepdims=True), axis=0, keepdims=True)
        loss_ref[...] += 0.5 * part
        dy = err * (1.0 / d)
        dg_ref[...] += jnp.sum(dy * xhat, axis=0, keepdims=True)
        dx_ref[...] = _norm_bwd(dy * g_ref[...], xhat, r)

    return pl.pallas_call(
        body, grid=(s // tb,), name="loss_head",
        in_specs=[_rows(tb, d), _whole((1, d)), _rows(tb, d)],
        out_specs=[_rows(tb, d), _whole((HALO, 128)), _whole((HALO, d))],
        out_shape=[jax.ShapeDtypeStruct((s, d), F32), jax.ShapeDtypeStruct((HALO, 128), F32),
                   jax.ShapeDtypeStruct((HALO, d), F32)],
        compiler_params=_cparams("arbitrary"),
    )(x, g, tgt)


def _mlp_bwd(dx2, x1, ap, g, w1_all, w2_all, l, tb, tf):
    s, d = x1.shape
    ff = ap.shape[1]
    nj = ff // tf

    def body(dx2_ref, x1_ref, ap_ref, g_ref, w1_ref, w2_ref, dx1_ref, dap_ref, dg_ref, acc):
        i, j = pl.program_id(0), pl.program_id(1)

        @pl.when((i == 0) & (j == 0))
        def _():
            dg_ref[...] = jnp.zeros_like(dg_ref)

        @pl.when(j == 0)
        def _():
            acc[...] = jnp.zeros_like(acc)

        da = _nt(dx2_ref[...].astype(BF), w2_ref[...])
        dap = (da * (2.0 * jnp.maximum(ap_ref[...].astype(F32), 0.0))).astype(BF)
        dap_ref[...] = dap
        acc[...] += _nt(dap, w1_ref[...])

        @pl.when(j == nj - 1)
        def _():
            xv = x1_ref[...]
            r = _rms_scale(xv)
            xhat = xv * r
            dh = acc[...]
            dg_ref[...] += jnp.sum(dh * xhat, axis=0, keepdims=True)
            dx1_ref[...] = dx2_ref[...] + _norm_bwd(dh * g_ref[...], xhat, r)

    return pl.pallas_call(
        body, grid=(s // tb, nj), name="mlp_bwd",
        in_specs=[pl.BlockSpec((tb, d), lambda i, j: (i, 0)), pl.BlockSpec((tb, d), lambda i, j: (i, 0)),
                  pl.BlockSpec((tb, tf), lambda i, j: (i, j)),
                  _whole((1, d)), pl.BlockSpec((None, None, d, tf), lambda i, j: (l, j, 0, 0)),
                  pl.BlockSpec((None, None, tf, d), lambda i, j: (l, j, 0, 0))],
        out_specs=[pl.BlockSpec((tb, d), lambda i, j: (i, 0)), pl.BlockSpec((tb, tf), lambda i, j: (i, j)),
                   _whole((HALO, d))],
        out_shape=[jax.ShapeDtypeStruct((s, d), F32), jax.ShapeDtypeStruct((s, ff), BF),
                   jax.ShapeDtypeStruct((HALO, d), F32)],
        scratch_shapes=[pltpu.VMEM((tb, d), F32)],
        compiler_params=_cparams("arbitrary", "arbitrary"),
    )(dx2, x1, ap, g, w1_all, w2_all)


def _wgrad(a, b, tm, tn, ts, name, relu2=False):
    s, m = a.shape
    n = b.shape[1]
    ns = s // ts

    def body(a_ref, b_ref, o_ref, acc):
        k = pl.program_id(2)

        @pl.when(k == 0)
        def _():
            acc[...] = jnp.zeros_like(acc)

        av = a_ref[...]
        if relu2:
            av = jnp.square(jnp.maximum(av.astype(F32), 0.0)).astype(BF)
        acc[...] += _tn(av, b_ref[...].astype(BF))

        @pl.when(k == ns - 1)
        def _():
            o_ref[...] = acc[...].astype(BF)

    return pl.pallas_call(
        body, grid=(m // tm, n // tn, ns), name=name,
        in_specs=[pl.BlockSpec((ts, tm), lambda i, j, k: (k, i)), pl.BlockSpec((ts, tn), lambda i, j, k: (k, j))],
        out_specs=pl.BlockSpec((tm, tn), lambda i, j, k: (i, j)),
        out_shape=jax.ShapeDtypeStruct((m, n), BF),
        scratch_shapes=[pltpu.VMEM((tm, tn), F32)],
        compiler_params=_cparams("parallel", "parallel", "arbitrary"),
    )(a, b)


def _mix_bwd(dx1, ya, yb, yc, lse_c, sink_row, gg, wo_all, l, tb):
    s, d = dx1.shape

    def body(dx_ref, ya_ref, yb_ref, yc_ref, lse_ref, sink_ref, gg_ref, wo_ref,
             n_ref, dya_ref, dyc_ref, da_ref, dc_ref, dyb_ref, dg_ref, dsink_ref):
        i = pl.program_id(0)

        @pl.when(i == 0)
        def _():
            dg_ref[...] = jnp.zeros_like(dg_ref)
            dsink_ref[...] = jnp.zeros_like(dsink_ref)

        dn = _nt(dx_ref[...].astype(BF), wo_ref[...].reshape(MIX_WIDTH, d))
        ys = [ya_ref[...], yb_ref[...], yc_ref[...]]
        rs = [_rms_scale(v) for v in ys]
        nhat = jnp.concatenate([v * r for v, r in zip(ys, rs)], axis=1)
        gg = gg_ref[...]
        n_ref[...] = (nhat * gg).astype(BF)
        dg_ref[...] += jnp.sum(dn * nhat, axis=0, keepdims=True)
        dnh = dn * gg
        bounds = [(0, A_WIDTH), (A_WIDTH, A_WIDTH + CONV_CH), (A_WIDTH + CONV_CH, MIX_WIDTH)]
        dys = [_norm_bwd(dnh[:, lo:hi], nhat[:, lo:hi], r) for (lo, hi), r in zip(bounds, rs)]
        dyb_ref[...] = dys[1]
        for dy, y, dy_ref, dd_ref in ((dys[0], ys[0], dya_ref, da_ref), (dys[2], ys[2], dyc_ref, dc_ref)):
            dy_ref[...] = dy
            t = dy * y
            for h in range(N_HEADS):
                dd_ref[:, _hs(h)] = jnp.broadcast_to(jnp.sum(t[:, _hs(h)], axis=1, keepdims=True), (tb, HEAD_DIM))
        dsink_ref[...] -= jnp.sum(jnp.exp(sink_ref[...] - lse_ref[...]) * dc_ref[...], axis=0, keepdims=True)

    return pl.pallas_call(
        body, grid=(s // tb,), name="mix_bwd",
        in_specs=[_rows(tb, d), _rows(tb, A_WIDTH), _rows(tb, CONV_CH), _rows(tb, A_WIDTH), _rows(tb, A_WIDTH),
                  _whole((1, A_WIDTH)), _whole((1, MIX_WIDTH)), _layer((N_CHIPS, MIX_WIDTH // N_CHIPS, d), l)],
        out_specs=[_rows(tb, MIX_WIDTH), _rows(tb, A_WIDTH), _rows(tb, A_WIDTH), _rows(tb, A_WIDTH),
                   _rows(tb, A_WIDTH), _rows(tb, CONV_CH), _whole((HALO, MIX_WIDTH)), _whole((HALO, A_WIDTH))],
        out_shape=[jax.ShapeDtypeStruct((s, MIX_WIDTH), BF), jax.ShapeDtypeStruct((s, A_WIDTH), F32),
                   jax.ShapeDtypeStruct((s, A_WIDTH), F32), jax.ShapeDtypeStruct((s, A_WIDTH), F32),
                   jax.ShapeDtypeStruct((s, A_WIDTH), F32), jax.ShapeDtypeStruct((s, CONV_CH), F32),
                   jax.ShapeDtypeStruct((HALO, MIX_WIDTH), F32), jax.ShapeDtypeStruct((HALO, A_WIDTH), F32)],
        compiler_params=_cparams("arbitrary"),
    )(dx1, ya, yb, yc, lse_c, sink_row, gg, wo_all)


def _attn_bwd(z, dy, lse, dd, dil, kw, kcol, vcol, n_rep, max_dist, name):
    s, zw = z.shape
    grid = _p_grid(s, dil)
    n_kv = N_HEADS // n_rep

    def body(q_ref, kp_ref, kc_ref, vp_ref, vc_ref, dy_ref, lse_ref, dd_ref, dq_ref, dkp_ref, dkc_ref, dvp_ref, dvc_ref):
        mask = _band_mask(pl.program_id(len(grid) - 1), dil, max_dist)
        for kh in range(n_kv):
            k2 = jnp.concatenate([_ld(kp_ref, _hs(kh)), _ld(kc_ref, _hs(kh))], axis=0).astype(BF)
            v2 = jnp.concatenate([_ld(vp_ref, _hs(kh)), _ld(vc_ref, _hs(kh))], axis=0).astype(BF)
            dk2 = jnp.zeros((2 * TQ, HEAD_DIM), F32)
            dv2 = jnp.zeros((2 * TQ, HEAD_DIM), F32)
            for h in range(kh * n_rep, (kh + 1) * n_rep):
                q = _ld(q_ref, _hs(h)).astype(BF)
                lse_h = _ld(lse_ref, slice(h * HEAD_DIM, h * HEAD_DIM + 1))
                dd_h = _ld(dd_ref, slice(h * HEAD_DIM, h * HEAD_DIM + 1))
                dyh = _ld(dy_ref, _hs(h)).astype(BF)
                sc = jnp.where(mask, _nt(q, k2) * SCALE, NEG)
                p = jnp.exp(sc - lse_h)
                dp = _nt(dyh, v2)
                ds = ((p * (dp - dd_h)) * SCALE).astype(BF)
                _st(dq_ref, _hs(h), _nn(ds, k2))
                dk2 = dk2 + _tn(ds, q)
                dv2 = dv2 + _tn(p.astype(BF), dyh)
            _st(dkp_ref, _hs(kh), dk2[:TQ])
            _st(dkc_ref, _hs(kh), dk2[TQ:])
            _st(dvp_ref, _hs(kh), dv2[:TQ])
            _st(dvc_ref, _hs(kh), dv2[TQ:])

    args = [_strips(z)] * 5 + [_strips(a) for a in (dy, lse, dd)]
    in_specs = [_p_spec(dil, A_WIDTH, 0), _p_spec(dil, kw, kcol, True), _p_spec(dil, kw, kcol),
                _p_spec(dil, kw, vcol, True), _p_spec(dil, kw, vcol)] + [_p_spec(dil, A_WIDTH, 0)] * 3
    out_specs = [_p_spec(dil, A_WIDTH, 0)] + [_p_spec(dil, kw, 0)] * 4
    na = s // N_STRIPS
    out_shape = [jax.ShapeDtypeStruct((4, 4, na, A_WIDTH), F32)] + [jax.ShapeDtypeStruct((4, 4, na, kw), F32)] * 4
    res = pl.pallas_call(
        body, grid=grid, name=name, in_specs=in_specs, out_specs=out_specs, out_shape=out_shape,
        compiler_params=_cparams(*(("parallel",) * len(grid))),
    )(*args)
    return [res[0].reshape(s, A_WIDTH)] + [a.reshape(s, kw) for a in res[1:]]


DZ_TA = 16


def _dz_assemble(parts_a, parts_c, dyb, zb, cw):
    s = zb.shape[0]
    na = s // N_STRIPS
    nb = na // DZ_TA

    def ahead(w, k):
        return pl.BlockSpec((4, 4, DZ_TA, w), lambda i: (0, 0, jnp.minimum(i + k, nb - 1), 0))

    args, in_specs = [], []
    for dil, (dq, dkp, dkc, dvp, dvc) in zip(DILATIONS + (1,), parts_a + [parts_c]):
        w = dkp.shape[1]
        here = _strip_rows(DZ_TA, w)
        if dil == 1:
            args += [dq, dkp, dkp, dkc, dvp, dvp, dvc]
            in_specs += [_strip_rows(DZ_TA, A_WIDTH), here, ahead(w, 1), here, here, ahead(w, 1), here]
        else:
            k = 8 * dil // DZ_TA
            args += [dq, dkp, dkc, dvp, dvc]
            in_specs += [_strip_rows(DZ_TA, A_WIDTH), ahead(w, k), here, ahead(w, k), here]
    n_att = len(args)
    args = [_strips(a) for a in args] + [_strips(dyb), _strips(dyb), _strips(zb), _strips(zb), _strips(zb), cw]
    in_specs += [_strip_rows(DZ_TA, CONV_CH), _next_rows(DZ_TA, CONV_CH, nb), _strip_rows(DZ_TA, ZB_W),
                 _prev_rows(DZ_TA, ZB_W), _next_rows(DZ_TA, ZB_W, nb), _whole((HALO, CONV_CH))]

    def body(*refs):
        att = list(refs[:n_att])
        dyb_ref, dybn_ref, zb_ref, zbp_ref, zbn_ref, cw_ref, dz_ref, dcw_ref = refs[n_att:]
        i = pl.program_id(0)

        @pl.when(i == 0)
        def _():
            dcw_ref[...] = jnp.zeros_like(dcw_ref)

        def shifted(dil):
            if dil == 1:
                dq_r, kp0, kp1, dkc_r, vp0, vp1, dvc_r = [att.pop(0) for _ in range(7)]
                live = i + 1 < nb
                half = DZ_TA // 2
                dkp = jnp.concatenate([kp0[:, :, half:, :], jnp.where(live, kp1[:, :, :half, :], 0.0)], axis=2)
                dvp = jnp.concatenate([vp0[:, :, half:, :], jnp.where(live, vp1[:, :, :half, :], 0.0)], axis=2)
            else:
                dq_r, dkp_r, dkc_r, dvp_r, dvc_r = [att.pop(0) for _ in range(5)]
                live = i + 8 * dil // DZ_TA < nb
                dkp, dvp = jnp.where(live, dkp_r[...], 0.0), jnp.where(live, dvp_r[...], 0.0)
            return dq_r[...], dkc_r[...] + dkp, dvc_r[...] + dvp

        dq, dk, dv = shifted(DILATIONS[0])
        for dil in DILATIONS[1:]:
            dq2, dk2, dv2 = shifted(dil)
            dq, dk, dv = dq + dq2, dk + dk2, dv + dv2
        dz_ref[:, :, :, 0:A_WIDTH] = dq.astype(BF)
        dz_ref[:, :, :, A_WIDTH:2 * A_WIDTH] = dk.astype(BF)
        dz_ref[:, :, :, 2 * A_WIDTH:ZA_W] = dv.astype(BF)
        dq, dk, dv = shifted(1)
        c0 = ZA_W + ZB_W
        dz_ref[:, :, :, c0:c0 + A_WIDTH] = dq.astype(BF)
        dz_ref[:, :, :, c0 + A_WIDTH:c0 + A_WIDTH + C_KV_WIDTH] = dk.astype(BF)
        dz_ref[:, :, :, c0 + A_WIDTH + C_KV_WIDTH:IN_WIDTH] = dv.astype(BF)

        cw = cw_ref[...]
        prev = jnp.where(i > 0, zbp_ref[...], 0.0)
        gb, gc, xb, u, u1, u2, c = _conv_strips(zb_ref[...], prev, cw)
        dyb = dyb_ref[...]
        dc = [_strip(dyb, b) * gb[b] for b in range(N_STRIPS)]
        dcn = jnp.where(i + 1 < nb, dybn_ref[...] * zbn_ref[:, :, :CONV_CH], 0.0)
        wrapped = [_shift_up(dc[0], 1, dcn[0]), _shift_up(dc[1], 1, dcn[1])]
        upd = [jnp.zeros((1, CONV_CH), F32)] * 3
        for b in range(N_STRIPS):
            dc1 = dc[b + 1] if b + 1 < N_STRIPS else wrapped[0]
            dc2 = dc[b + 2] if b + 2 < N_STRIPS else wrapped[b + 2 - N_STRIPS]
            du = cw[2:3, :] * dc[b] + cw[1:2, :] * dc1 + cw[0:1, :] * dc2
            f, e = b % 4, b // 4
            dz_ref[f, e, :, ZA_W:ZA_W + CONV_CH] = (_strip(dyb, b) * c[b]).astype(BF)
            dz_ref[f, e, :, ZA_W + CONV_CH:ZA_W + 2 * CONV_CH] = (du * xb[b]).astype(BF)
            dz_ref[f, e, :, ZA_W + 2 * CONV_CH:c0] = (du * gc[b]).astype(BF)
            for t, uu in enumerate((u2[b], u1[b], u[b])):
                upd[t] = upd[t] + jnp.sum(dc[b] * uu, axis=0, keepdims=True)
        row = lax.broadcasted_iota(jnp.int32, (HALO, CONV_CH), 0)
        tile = jnp.zeros((HALO, CONV_CH), F32)
        for t in range(3):
            tile = jnp.where(row == t, upd[t], tile)
        dcw_ref[...] += tile

    dz, dcw = pl.pallas_call(
        body, grid=(nb,), name="dz_assemble", in_specs=in_specs,
        out_specs=[_strip_rows(DZ_TA, IN_WIDTH), _whole((HALO, CONV_CH))],
        out_shape=[jax.ShapeDtypeStruct((4, 4, na, IN_WIDTH), BF), jax.ShapeDtypeStruct((HALO, CONV_CH), F32)],
        compiler_params=_cparams("arbitrary"),
    )(*args)
    return dz.reshape(s, IN_WIDTH), dcw


def _qkv_bwd(dz, dx1, x, g, w_all, l, tb):
    s, d = x.shape

    def body(dz_ref, dx1_ref, x_ref, g_ref, w_ref, dx_ref, dg_ref):
        i = pl.program_id(0)

        @pl.when(i == 0)
        def _():
            dg_ref[...] = jnp.zeros_like(dg_ref)

        n = IN_WIDTH // N_CHIPS
        dh = _nt(dz_ref[:, 0:n], w_ref[0])
        for k in range(1, N_CHIPS):
            dh = dh + _nt(dz_ref[:, k * n:(k + 1) * n], w_ref[k])
        xv = x_ref[...]
        r = _rms_scale(xv)
        xhat = xv * r
        dg_ref[...] += jnp.sum(dh * xhat, axis=0, keepdims=True)
        dx_ref[...] = dx1_ref[...] + _norm_bwd(dh * g_ref[...], xhat, r)

    return pl.pallas_call(
        body, grid=(s // tb,), name="qkv_bwd",
        in_specs=[_rows(tb, IN_WIDTH), _rows(tb, d), _rows(tb, d), _whole((1, d)),
                  _layer((N_CHIPS, d, IN_WIDTH // N_CHIPS), l)],
        out_specs=[_rows(tb, d), _whole((HALO, d))],
        out_shape=[jax.ShapeDtypeStruct((s, d), F32), jax.ShapeDtypeStruct((HALO, d), F32)],
        compiler_params=_cparams("arbitrary"),
    )(dz, dx1, x, g, w_all)


def _tile_rows(rows):
    return jnp.pad(rows, ((0, HALO - rows.shape[0]), (0, 0)))


def _to_strips(a):
    s, d = a.shape
    return a.reshape(s // N_STRIPS, 4, 4, d).transpose(2, 1, 0, 3).reshape(s, d)


def _from_strips(a):
    s, d = a.shape
    return a.reshape(4, 4, s // N_STRIPS, d).transpose(2, 1, 0, 3).reshape(s, d)


def _local_step(x, tgt, fetch, ff, sinks, g_mix, g_group, g_mlp, g_final, emit):
    s, d = x.shape
    depth = g_mix.shape[0]
    tb = min(512, s)
    tf = ff // N_CHIPS
    saved = []
    for l in range(depth):
        w_in, _, _, _, conv_w = fetch(0, l, x)
        cw = _tile_rows(conv_w[l])
        sk = jnp.repeat(sinks[l].reshape(N_HEADS), HEAD_DIM)[None]
        h, za, zb, zc = _qkv_fwd(x, g_mix[l][None], w_in, l, tb)
        parts_a = [_attn_fwd(za, dil, A_WIDTH, 1, 2, 1, A_MAX_DIST, "attn_a_fwd_%d" % dil) for dil in DILATIONS]
        part_c = _attn_fwd(zc, 1, C_KV_WIDTH, 3, 4, C_GROUP, C_MAX_DIST, "attn_c_fwd")
        ya, lse_a, yc, lse_c = _attn_merge(parts_a, part_c, sk, tb)
        w_in, w_o, w1, w2, _ = fetch(1, l, yc)
        x1, yb = _mix_fwd(x, ya, yc, zb, cw, g_group[l][None], w_o, l, tb)
        x2, h2, ap = _mlp_fwd(x1, g_mlp[l][None], w1, w2, l, tb, tf)
        saved.append((x, h, za, zb, zc, ya, lse_a, yc, lse_c, yb, x1, h2, ap, cw, sk))
        x = x2
    dx, loss_tile, dg_final = _loss_head(x, g_final[None], tgt, tb)
    grads = [None] * depth
    tok = jnp.zeros((), F32)
    for l in reversed(range(depth)):
        x0, h, za, zb, zc, ya, lse_a, yc, lse_c, yb, x1, h2, ap, cw, sk = saved[l]
        dx1, dap, dg_mlp = _mlp_bwd(dx, x1, ap, g_mlp[l][None] + tok, w1, w2, l, tb, tf)
        tok = emit(l, 3, _wgrad(ap, dx, min(1024, ff), d, tb, "wgrad_ff_out", relu2=True))
        tok = tok + emit(l, 2, _wgrad(h2, dap, d, min(1024, ff), tb, "wgrad_ff_in"))
        n, dya, dyc, dd_a, dd_c, dyb, dg_group, dsink = _mix_bwd(dx1, ya, yb, yc, lse_c, sk, g_group[l][None] + tok,
                                                                 w_o, l, tb)
        tok = emit(l, 1, _wgrad(n, dx1, MIX_WIDTH, d, tb, "wgrad_o"))
        cw = cw + tok
        parts_a = [_attn_bwd(za, dya, lse_a, dd_a, dil, A_WIDTH, 1, 2, 1, A_MAX_DIST, "attn_a_bwd_%d" % dil)
                   for dil in DILATIONS]
        parts_c = _attn_bwd(zc, dyc, lse_c, dd_c, 1, C_KV_WIDTH, 3, 4, C_GROUP, C_MAX_DIST, "attn_c_bwd")
        dz, dcw = _dz_assemble(parts_a, parts_c, dyb, zb, cw)
        dx, dg_mix = _qkv_bwd(dz, dx1, x0, g_mix[l][None], w_in, l, tb)
        tok = emit(l, 0, _wgrad(h, dz, d, IN_WIDTH // 4, tb, "wgrad_in"))
        grads[l] = (dcw, dsink, dg_mix, dg_group, dg_mlp)
    return loss_tile, dx, grads, dg_final


ANY = pl.BlockSpec(memory_space=pl.ANY)
SHARD_AXES = (2, 1, 2, 1)
N_BIG = len(SHARD_AXES)
N_CHIPS = 4
N_DEV = 8


def _mesh_pos():
    return lax.axis_index("x"), lax.axis_index("y"), lax.axis_index("c")


def _flip(v, bit):
    return 1 - v if bit else v


def _place_shard(shard, chip_arr, name):
    _, rows, cols = shard.shape
    tr = min(256, rows)

    def body(chip_ref, x_ref, o_ref):
        o_ref[...] = x_ref[...].astype(BF)

    return pl.pallas_call(
        body, name=name,
        grid_spec=pltpu.PrefetchScalarGridSpec(
            num_scalar_prefetch=1, grid=(2, rows // tr),
            in_specs=[pl.BlockSpec((None, tr, cols), lambda l, i, chip: (l, i, 0))],
            out_specs=pl.BlockSpec((None, None, tr, cols), lambda l, i, chip: (l, chip[0], i, 0))),
        out_shape=jax.ShapeDtypeStruct((2, N_CHIPS, rows, cols), BF),
        compiler_params=_cparams("parallel", "parallel"),
    )(chip_arr, shard)


HBM = pl.BlockSpec(memory_space=pltpu.HBM)
SEM = pl.BlockSpec(memory_space=pltpu.SEMAPHORE)
EFFECT = pltpu.SideEffectType.DATAFLOW_SIDE_EFFECTING

GATHER_GROUPS = (((0, 0),), ((1, 0), (2, 0), (3, 0)), ((0, 1),), ((1, 1), (2, 1), (3, 1)))
GATHER_STARTS = ((0, 1), (2, 3))


def _gather_copies(arrs, group, send_sems, recv_sems):
    x, y, c = _mesh_pos()
    me = 2 * x + y
    out = []
    for i, (w, layer) in enumerate(group):
        mine = arrs[w].at[layer, me]
        for j, (qx, qy) in enumerate([(1 - x, y), (x, 1 - y), (1 - x, 1 - y)]):
            landed = arrs[w].at[layer, 2 * qx + qy]
            out.append(tuple(pltpu.make_async_remote_copy(
                src_ref=piece, dst_ref=piece, send_sem=send_sems.at[i * 3 + j], recv_sem=recv_sems.at[i * 3 + j],
                device_id=(qx, qy, c), device_id_type=MESH) for piece in (mine, landed)))
    return out


def _conv_copies(conv_src, conv_dst, send_sems, recv_sems):
    x, y, c = _mesh_pos()
    out = []
    for j, (qx, qy) in enumerate([(1 - x, y), (x, 1 - y), (1 - x, 1 - y)]):
        out.append(tuple(pltpu.make_async_remote_copy(
            src_ref=conv_src, dst_ref=conv_dst.at[q], send_sem=send_sems.at[j], recv_sem=recv_sems.at[j],
            device_id=(qx, qy, c), device_id_type=MESH) for q in (2 * x + y, 2 * qx + qy)))
    return out


def _gather_start(groups, arrs, conv, name):
    n_sems = 2 * (len(groups) + (conv is not None))

    def body(*refs):
        mats = refs[:N_BIG]
        sems = refs[n_op:n_op + n_sems]
        if conv is not None:
            for cp, _ in _conv_copies(refs[N_BIG], refs[N_BIG + 1], sems[-2], sems[-1]):
                cp.start()
        for k, g in enumerate(groups):
            for cp, _ in _gather_copies(mats, GATHER_GROUPS[g], sems[2 * k], sems[2 * k + 1]):
                cp.start()

    sem_shapes = []
    for n in [len(GATHER_GROUPS[g]) for g in groups] + ([1] if conv is not None else []):
        sem_shapes += [pltpu.SemaphoreType.DMA((3 * n,))] * 2
    operands = list(arrs) + ([] if conv is None else list(conv))
    n_op = len(operands)
    res = pl.pallas_call(
        body, name=name,
        out_shape=tuple(sem_shapes) + tuple(pltpu.HBM(a.shape, a.dtype) for a in operands),
        in_specs=(HBM,) * n_op, out_specs=(SEM,) * n_sems + (HBM,) * n_op,
        input_output_aliases={i: n_sems + i for i in range(n_op)},
        compiler_params=pltpu.CompilerParams(has_side_effects=EFFECT),
    )(*[pltpu.with_memory_space_constraint(a, pltpu.HBM) for a in operands])
    return res[:n_sems], list(res[n_sems:])


def _gather_wait(k, sems, arrs, conv, after, name):
    group = GATHER_GROUPS[k]
    mats = sorted({w for w, _ in group})
    n_conv = 0 if conv is None else 2

    def body(*refs):
        local = refs[:len(mats)]
        arrs_ref = [None] * N_BIG
        for w, ref in zip(mats, local):
            arrs_ref[w] = ref
        pos = len(mats) + n_conv
        copies = _gather_copies(arrs_ref, group, refs[pos], refs[pos + 1])
        if conv is not None:
            copies += _conv_copies(refs[len(mats)], refs[len(mats) + 1], refs[pos + 2], refs[pos + 3])
        for send, recv in copies:
            recv.wait_recv()
            send.wait_send()

    operands = [arrs[w] for w in mats] + ([] if conv is None else [conv[1], conv[2]])
    sem_ops = list(sems) + ([] if conv is None else list(conv[0]))
    n_op = len(operands)
    res = pl.pallas_call(
        body, name=name, out_shape=tuple(pltpu.HBM(a.shape, a.dtype) for a in operands),
        in_specs=(HBM,) * n_op + (SEM,) * len(sem_ops) + (ANY,), out_specs=(HBM,) * n_op,
        input_output_aliases={i: i for i in range(n_op)},
        compiler_params=pltpu.CompilerParams(has_side_effects=EFFECT),
    )(*operands, *sem_ops, after)
    arrs = list(arrs)
    for w, a in zip(mats, res):
        arrs[w] = a
    return arrs, (res[-1] if conv is not None else None)


def _grad_shard(ref, w, chip, n):
    start = pl.multiple_of(chip * n, 128)
    if SHARD_AXES[w] == 2:
        return ref.at[:, pl.ds(start, n)]
    return ref.at[pl.ds(start, n), :]


def _slot_shape(g, w):
    shape = list(g.shape)
    shape[SHARD_AXES[w] - 1] //= N_CHIPS
    return (N_DEV - 1,) + tuple(shape)


def _scatter_copies(g_ref, land_ref, send_sems, recv_sems, layer, w):
    x, y, c = _mesh_pos()
    n = g_ref.shape[SHARD_AXES[w] - 1] // N_CHIPS
    out = []
    for r in range(1, N_DEV):
        tx, ty, tc = _flip(x, r & 4), _flip(y, r & 2), _flip(c, r & 1)
        cp = pltpu.make_async_remote_copy(
            src_ref=_grad_shard(g_ref, w, 2 * tx + ty, n), dst_ref=land_ref.at[r - 1], send_sem=send_sems.at[r - 1],
            recv_sem=recv_sems.at[r - 1], device_id=(tx, ty, tc), device_id_type=MESH)
        out.append((cp, (c != layer) if r & 1 else (c == layer)))
    return out


def _scatter_start(g, land, layer, w, name):
    def body(g_ref, land_ref, send_sems, recv_sems, g_thru, land_thru, token):
        for cp, mine in _scatter_copies(g_ref, land_ref, send_sems, recv_sems, layer, w):
            @pl.when(mine)
            def _():
                cp.start()
        token[...] = jnp.zeros_like(token)

    return pl.pallas_call(
        body, name=name,
        out_shape=(pltpu.SemaphoreType.DMA((N_DEV - 1,)), pltpu.SemaphoreType.DMA((N_DEV - 1,)),
                   pltpu.HBM(g.shape, g.dtype), pltpu.HBM(land.shape, land.dtype),
                   jax.ShapeDtypeStruct((HALO, 128), F32)),
        in_specs=(HBM, HBM), out_specs=(SEM, SEM, HBM, HBM, pl.BlockSpec(memory_space=pltpu.VMEM)),
        input_output_aliases={0: 2, 1: 3}, compiler_params=pltpu.CompilerParams(has_side_effects=EFFECT),
    )(pltpu.with_memory_space_constraint(g, pltpu.HBM), pltpu.with_memory_space_constraint(land, pltpu.HBM))


def _scatter_wait(started, land, after, w, name):
    def body(g0_ref, g1_ref, land_ref, ss0, rs0, ss1, rs1, after_ref, g0_out, g1_out, land_out):
        c = lax.axis_index("c")
        for layer, g_ref, ss, rs in ((0, g0_ref, ss0, rs0), (1, g1_ref, ss1, rs1)):
            for cp, mine in _scatter_copies(g_ref, land_ref, ss, rs, layer, w):
                @pl.when(mine)
                def _():
                    cp.wait_send()

                @pl.when(c == layer)
                def _():
                    cp.wait_recv()

    (ss0, rs0, g0), (ss1, rs1, g1) = started
    return pl.pallas_call(
        body, name=name,
        out_shape=(pltpu.HBM(g0.shape, g0.dtype), pltpu.HBM(g1.shape, g1.dtype), pltpu.HBM(land.shape, land.dtype)),
        in_specs=(HBM, HBM, HBM, SEM, SEM, SEM, SEM, ANY), out_specs=(HBM, HBM, HBM),
        input_output_aliases={0: 0, 1: 1, 2: 2}, compiler_params=pltpu.CompilerParams(has_side_effects=EFFECT),
    )(g0, g1, land, ss0, rs0, ss1, rs1, after)


def _sum_slots(g0, g1, slots, w, pos_arr, name):
    _, rows, cols = slots.shape
    tr = min(256, rows)
    nr = rows // tr
    if SHARD_AXES[w] == 2:
        own = pl.BlockSpec((tr, cols), lambda i, pos: (i, pos[0]))
    else:
        own = pl.BlockSpec((tr, cols), lambda i, pos: (pos[0] * nr + i, 0))

    def body(pos_ref, own0_ref, own1_ref, s_ref, o_ref):
        acc = jnp.where(pos_ref[1] == 0, own0_ref[...], own1_ref[...]).astype(F32)
        for r in range(N_DEV - 1):
            acc = acc + s_ref[r].astype(F32)
        o_ref[...] = acc

    return pl.pallas_call(
        body, name=name,
        grid_spec=pltpu.PrefetchScalarGridSpec(
            num_scalar_prefetch=1, grid=(nr,),
            in_specs=[own, own, pl.BlockSpec((N_DEV - 1, tr, cols), lambda i, pos: (0, i, 0))],
            out_specs=pl.BlockSpec((tr, cols), lambda i, pos: (i, 0))),
        out_shape=jax.ShapeDtypeStruct((rows, cols), F32), compiler_params=_cparams("parallel"),
    )(pos_arr, g0, g1, slots)


def _swap_layers(halves):
    def body(*refs):
        srcs, dsts = refs[:N_BIG], refs[N_BIG:2 * N_BIG]
        send_sems, recv_sems = refs[2 * N_BIG:]
        x, y, c = _mesh_pos()
        sends = [pltpu.make_async_remote_copy(src_ref=srcs[w], dst_ref=dsts[w], send_sem=send_sems.at[w],
                                              recv_sem=recv_sems.at[w], device_id=(x, y, 1 - c), device_id_type=MESH)
                 for w in range(N_BIG)]
        for cp in sends:
            cp.start()
        for cp in sends:
            cp.wait_recv()
        for cp in sends:
            cp.wait_send()

    return pl.pallas_call(
        body, name="swap_layers", in_specs=[ANY] * N_BIG, out_specs=[ANY] * N_BIG,
        out_shape=[jax.ShapeDtypeStruct(h.shape, h.dtype) for h in halves],
        scratch_shapes=[pltpu.SemaphoreType.DMA((N_BIG,)), pltpu.SemaphoreType.DMA((N_BIG,))],
    )(*halves)


def _adamw_math(w, g, m, v):
    m = ADAM_B1 * m + (1.0 - ADAM_B1) * g
    v = ADAM_B2 * v + (1.0 - ADAM_B2) * jnp.square(g)
    m_hat = m / (1.0 - ADAM_B1 ** ADAM_STEP)
    v_hat = v / (1.0 - ADAM_B2 ** ADAM_STEP)
    delta = -ADAM_LR * (m_hat / (jnp.sqrt(v_hat) + ADAM_EPS) + ADAM_WD * w)
    return delta, m, v


def _adamw(w, g_own, g_other, m, v, pos_arr, name):
    shape = w.shape
    _, rows, cols = shape
    tr = min(256, rows)

    def body(pos_ref, w_ref, own_ref, other_ref, m_ref, v_ref, g_ref, d_ref, m2_ref, v2_ref):
        g = jnp.where(pl.program_id(0) == pos_ref[1], own_ref[...], other_ref[...])
        g_ref[...] = g
        d_ref[...], m2_ref[...], v2_ref[...] = _adamw_math(w_ref[...], g, m_ref[...], v_ref[...])

    full = pl.BlockSpec((None, tr, cols), lambda l, i, pos: (l, i, 0))
    half = pl.BlockSpec((tr, cols), lambda l, i, pos: (i, 0))
    return pl.pallas_call(
        body, name=name,
        grid_spec=pltpu.PrefetchScalarGridSpec(
            num_scalar_prefetch=1, grid=(2, rows // tr),
            in_specs=[full, half, half, full, full], out_specs=[full] * 4),
        out_shape=[jax.ShapeDtypeStruct(shape, F32)] * 4, compiler_params=_cparams("parallel", "parallel"),
    )(pos_arr, w, g_own, g_other, m, v)


def _small_sync(part, w, m, v):
    rows, cols = part.shape

    def body(p_ref, w_ref, m_ref, v_ref, g_ref, d_ref, m2_ref, v2_ref, slots, send_sems, recv_sems):
        x, y, c = _mesh_pos()
        me = 4 * x + 2 * y + c
        slots[me] = p_ref[...]
        sends = []
        for r in range(1, N_DEV):
            to = (_flip(x, r & 4), _flip(y, r & 2), _flip(c, r & 1))
            sends.append(pltpu.make_async_remote_copy(
                src_ref=p_ref, dst_ref=slots.at[me], send_sem=send_sems.at[r - 1], recv_sem=recv_sems.at[r - 1],
                device_id=to, device_id_type=MESH))
        for cp in sends:
            cp.start()
        for cp in sends:
            cp.wait_recv()
        for cp in sends:
            cp.wait_send()
        g = slots[0]
        for i in range(1, N_DEV):
            g = g + slots[i]
        g_ref[...] = g
        d_ref[...], m2_ref[...], v2_ref[...] = _adamw_math(w_ref[...], g, m_ref[...], v_ref[...])

    vm = pl.BlockSpec(memory_space=pltpu.VMEM)
    return pl.pallas_call(
        body, name="small_sync", in_specs=[vm] * 4, out_specs=[vm] * 4,
        out_shape=[jax.ShapeDtypeStruct((rows, cols), F32)] * 4,
        scratch_shapes=[pltpu.VMEM((N_DEV, rows, cols), F32), pltpu.SemaphoreType.DMA((N_DEV - 1,)),
                        pltpu.SemaphoreType.DMA((N_DEV - 1,))],
    )(part, w, m, v)


def _pack_small(d, g_mix, g_group, g_mlp, g_final, conv_full, sinks, scalar):
    def part(rows):
        return jnp.pad(rows, ((0, HALO - rows.shape[0]), (0, d - rows.shape[1])))
    return jnp.concatenate([part(g_mix), part(g_group), part(g_mlp), part(g_final[None]),
                            part(conv_full.reshape(6, CONV_CH)), part(sinks.reshape(2, N_HEADS)),
                            part(scalar.reshape(1, 1))], axis=0)


def _unpack_small(p, dm):
    return (p[0:2, :dm], p[8:10, :MIX_WIDTH], p[16:18, :dm], p[24, :dm], p[32:38, :CONV_CH].reshape(2, 3, CONV_CH),
            p[40:42, :N_HEADS].reshape(2, 2, C_GROUP), p[48, 0])


def kernel(x, w_in, conv_w, sinks, g_mix, g_group, w_o, g_mlp, w_ff_in, w_ff_out, g_final, loss_target, m_w_in, m_conv_w, m_sinks, m_g_mix, m_g_group, m_w_o, m_g_mlp, m_w_ff_in, m_w_ff_out, m_g_final, v_w_in, v_conv_w, v_sinks, v_g_mix, v_g_group, v_w_o, v_g_mlp, v_w_ff_in, v_w_ff_out, v_g_final):
    d = max(x.shape[2], MIX_WIDTH)
    chip = 2 * lax.axis_index("x") + lax.axis_index("y")
    conv_n = conv_w.shape[2]

    pos_arr = jnp.stack([chip, lax.axis_index("c")]).astype(jnp.int32)
    placed = [_place_shard(w, pos_arr[:1], "place_shard_%d" % i)
              for i, w in enumerate((w_in, w_o, w_ff_in, w_ff_out))]
    conv_tile = jnp.pad(conv_w.reshape(6, conv_n), ((0, HALO - 6), (0, 128 - conv_n)))
    sems0, thru = _gather_start(GATHER_STARTS[0], placed,
                                (conv_tile, lax.empty((N_CHIPS,) + conv_tile.shape, conv_tile.dtype)), "gather_start_0")
    full = {"arrs": thru[:N_BIG], "conv": None, "sems": list(sems0[:4])}

    def fetch(stage, layer, after):
        k = 2 * layer + stage
        sems = full["sems"][2 * k:2 * k + 2]
        if k == 0:
            full["arrs"], land = _gather_wait(0, sems, full["arrs"], (sems0[-2:], thru[N_BIG], thru[N_BIG + 1]),
                                              after, "gather_wait_0")
            conv_all = lax.dynamic_update_slice(land, conv_tile[None], (chip, 0, 0))
            full["conv"] = conv_all[:, :6, :conv_n].reshape(N_CHIPS, 2, 3, conv_n).transpose(1, 2, 0, 3).reshape(
                2, 3, CONV_CH)
        else:
            full["arrs"], _ = _gather_wait(k, sems, full["arrs"], None, after, "gather_wait_%d" % k)
        if k == 1:
            sems1, full["arrs"] = _gather_start(GATHER_STARTS[1], full["arrs"], None, "gather_start_1")
            full["sems"] += list(sems1)
        return (*full["arrs"], full["conv"])

    lands, started = [None] * N_BIG, {}

    def emit(layer, w, g):
        if lands[w] is None:
            lands[w] = lax.empty(_slot_shape(g, w), g.dtype)
        *started[layer, w], lands[w], token = _scatter_start(g, lands[w], layer, w, "scatter_start_%d_%d" % (layer, w))
        return token[0, 0]

    loss_tile, dx, grads, dg_final = _local_step(_to_strips(x[0]), _to_strips(loss_target[0]), fetch, w_ff_in.shape[2] * N_CHIPS,
                                                 sinks, g_mix, g_group, g_mlp, g_final, emit)

    own = []
    for w in range(N_BIG):
        g0, g1, slots = _scatter_wait((started[0, w], started[1, w]), lands[w], dx, w, "scatter_wait_%d" % w)
        own.append(_sum_slots(g0, g1, slots, w, pos_arr, "sum_slots_%d" % w))
    other = _swap_layers(own)

    def both(i):
        return jnp.stack([grads[0][i][0], grads[1][i][0]])
    dconv = jnp.stack([grads[0][0][:3], grads[1][0][:3]])
    dsinks = jnp.stack([grads[0][1][0, ::HEAD_DIM], grads[1][1][0, ::HEAD_DIM]])
    part = _pack_small(d, both(2), both(3), both(4), dg_final[0], dconv, dsinks, loss_tile[0, 0])

    def spread(shard):
        return lax.dynamic_update_slice(jnp.zeros((2, 3, CONV_CH), F32), shard, (0, 0, chip * conv_n))
    zero = jnp.zeros((), F32)
    packs = [_pack_small(d, a, b, c_, e, spread(f), g_, zero) for a, b, c_, e, f, g_ in (
        (g_mix, g_group, g_mlp, g_final, conv_w, sinks),
        (m_g_mix, m_g_group, m_g_mlp, m_g_final, m_conv_w, m_sinks),
        (v_g_mix, v_g_group, v_g_mlp, v_g_final, v_conv_w, v_sinks))]
    small = [_unpack_small(p, x.shape[2]) for p in _small_sync(part, *packs)]

    def shard_of(full):
        return lax.dynamic_slice(full, (0, 0, chip * conv_n), (2, 3, conv_n))
    small = [(s[0], s[1], s[2], s[3], shard_of(s[4]), s[5], s[6]) for s in small]
    loss = small[0][6]

    big = [_adamw(w, own[i], other[i], m, v, pos_arr, "adamw_%d" % i) for i, (w, m, v) in enumerate((
        (w_in, m_w_in, v_w_in), (w_o, m_w_o, v_w_o), (w_ff_in, m_w_ff_in, v_w_ff_in),
        (w_ff_out, m_w_ff_out, v_w_ff_out)))]

    def ordered(kind):
        b = [big[i][kind] for i in range(N_BIG)]
        s = small[kind]
        return [b[0], s[4], s[5], s[0], s[1], b[1], s[2], b[2], b[3], s[3]]

    return (loss, _from_strips(dx)[None], *ordered(0), *ordered(1), *ordered(2), *ordered(3))
```

```python
import functools

import jax
import jax.numpy as jnp
from jax import lax
from jax.experimental import pallas as pl
from jax.experimental.pallas import tpu as pltpu

HEAD_DIM = 64
N_HEADS = 6
C_GROUP = 3
A_WIDTH = N_HEADS * HEAD_DIM
C_KV_WIDTH = 2 * HEAD_DIM
CONV_CH = 256
ZA_W = 3 * A_WIDTH
ZB_W = 3 * CONV_CH
ZC_W = A_WIDTH + 2 * C_KV_WIDTH
IN_WIDTH = ZA_W + ZB_W + ZC_W
MIX_WIDTH = A_WIDTH + CONV_CH + A_WIDTH
DILATIONS = (1, 4, 16)
A_MAX_DIST = 128
C_MAX_DIST = 127
TQ = 128
EPS = 1e-6
SCALE = HEAD_DIM ** -0.5
NEG = -1e30
HALO = 8

ADAM_LR = 0.001
ADAM_B1 = 0.9
ADAM_B2 = 0.999
ADAM_EPS = 1e-08
ADAM_WD = 0.01
ADAM_STEP = 10

BF = jnp.bfloat16
F32 = jnp.float32
MESH = pl.DeviceIdType.MESH
VMEM_LIMIT = 56 * 1024 * 1024


def _cparams(*sem):
    return pltpu.CompilerParams(dimension_semantics=sem, vmem_limit_bytes=VMEM_LIMIT)


def _nt(a, b):
    return lax.dot_general(a, b, (((1,), (1,)), ((), ())), preferred_element_type=F32)


def _tn(a, b):
    return lax.dot_general(a, b, (((0,), (0,)), ((), ())), preferred_element_type=F32)


def _nn(a, b):
    return jnp.dot(a, b, preferred_element_type=F32)


def _rows(tb, w):
    return pl.BlockSpec((tb, w), lambda i: (i, 0))


def _whole(shape):
    return pl.BlockSpec(shape, lambda *_: (0,) * len(shape))


def _layer(shape, l):
    return pl.BlockSpec((None,) + shape, lambda *_: (l,) + (0,) * len(shape))


def _rms_scale(v):
    return lax.rsqrt(jnp.mean(v * v, axis=-1, keepdims=True) + EPS)


def _norm_bwd(dxhat, xhat, r):
    return r * (dxhat - xhat * jnp.mean(dxhat * xhat, axis=-1, keepdims=True))


def _qkv_fwd(x, g, w_all, l, tb):
    s, d = x.shape

    def body(x_ref, g_ref, w_ref, h_ref, za_ref, zb_ref, zc_ref):
        xv = x_ref[...]
        h = ((xv * _rms_scale(xv)) * g_ref[...]).astype(BF)
        h_ref[...] = h
        z = jnp.concatenate([_nn(h, w_ref[k]) for k in range(N_CHIPS)], axis=1)
        za_ref[...] = z[:, :ZA_W]
        zb_ref[...] = z[:, ZA_W:ZA_W + ZB_W]
        zc_ref[...] = z[:, ZA_W + ZB_W:]

    return pl.pallas_call(
        body, grid=(s // tb,), name="qkv_fwd",
        in_specs=[_rows(tb, d), _whole((1, d)), _layer((N_CHIPS, d, IN_WIDTH // N_CHIPS), l)],
        out_specs=[_rows(tb, d), _rows(tb, ZA_W), _rows(tb, ZB_W), _rows(tb, ZC_W)],
        out_shape=[jax.ShapeDtypeStruct((s, d), BF), jax.ShapeDtypeStruct((s, ZA_W), F32),
                   jax.ShapeDtypeStruct((s, ZB_W), F32), jax.ShapeDtypeStruct((s, ZC_W), F32)],
        compiler_params=_cparams("parallel"),
    )(x, g, w_all)


N_STRIPS = 16


def _strips(a):
    s, w = a.shape
    return a.reshape(4, 4, s // N_STRIPS, w)


def _p_grid(s, dil):
    na = s // N_STRIPS
    return {16: (4, 4, na // TQ), 4: (4, na // 32), 1: (na // 8,)}[dil]


def _p_spec(dil, cw, col, prev=False):
    def blk(j):
        return jnp.maximum(j - 1, 0) if prev else j
    if dil == 16:
        return pl.BlockSpec((None, None, TQ, cw), lambda f, e, j: (f, e, blk(j), col))
    if dil == 4:
        return pl.BlockSpec((None, 4, 32, cw), lambda f, j: (f, 0, blk(j), col))
    return pl.BlockSpec((4, 4, 8, cw), lambda j: (0, 0, blk(j), col))


def _block_pos(i, dil):
    if dil == 16:
        return i
    if dil == 4:
        return 4 * (i % 32) + i // 32
    return 16 * (i % 8) + 4 * ((i // 8) % 4) + i // 32


def _band_mask(b, dil, max_dist):
    qi = _block_pos(lax.broadcasted_iota(jnp.int32, (TQ, 2 * TQ), 0), dil)
    col = lax.broadcasted_iota(jnp.int32, (TQ, 2 * TQ), 1)
    cur = col >= TQ
    dist = qi - _block_pos(col % TQ, dil) + jnp.where(cur, 0, TQ)
    return (dist >= 0) & (dist <= max_dist) & (cur | (b > 0))


def _hs(h):
    return slice(h * HEAD_DIM, (h + 1) * HEAD_DIM)


def _ld(ref, cols):
    v = ref[..., cols]
    return v.reshape(TQ, v.shape[-1])


def _st(ref, cols, val):
    ref[..., cols] = val.reshape(ref.shape[:-1] + (val.shape[-1],))


def _attn_fwd(z, dil, kw, kcol, vcol, n_rep, max_dist, name):
    s, zw = z.shape
    grid = _p_grid(s, dil)

    def body(q_ref, kp_ref, kc_ref, vp_ref, vc_ref, acc_ref, m_ref, l_ref):
        mask = _band_mask(pl.program_id(len(grid) - 1), dil, max_dist)
        for kh in range(N_HEADS // n_rep):
            k2 = jnp.concatenate([_ld(kp_ref, _hs(kh)), _ld(kc_ref, _hs(kh))], axis=0).astype(BF)
            v2 = jnp.concatenate([_ld(vp_ref, _hs(kh)), _ld(vc_ref, _hs(kh))], axis=0).astype(BF)
            for h in range(kh * n_rep, (kh + 1) * n_rep):
                q = _ld(q_ref, _hs(h)).astype(BF)
                sc = jnp.where(mask, _nt(q, k2) * SCALE, NEG)
                m = jnp.max(sc, axis=1, keepdims=True)
                p = jnp.exp(sc - m)
                _st(acc_ref, _hs(h), _nn(p.astype(BF), v2))
                _st(m_ref, _hs(h), jnp.broadcast_to(m, (TQ, HEAD_DIM)))
                _st(l_ref, _hs(h), jnp.broadcast_to(jnp.sum(p, axis=1, keepdims=True), (TQ, HEAD_DIM)))

    res = pl.pallas_call(
        body, grid=grid, name=name,
        in_specs=[_p_spec(dil, A_WIDTH, 0), _p_spec(dil, kw, kcol, True), _p_spec(dil, kw, kcol),
                  _p_spec(dil, kw, vcol, True), _p_spec(dil, kw, vcol)],
        out_specs=[_p_spec(dil, A_WIDTH, 0)] * 3,
        out_shape=[jax.ShapeDtypeStruct((4, 4, s // N_STRIPS, A_WIDTH), F32)] * 3,
        compiler_params=_cparams(*(("parallel",) * len(grid))),
    )(*[_strips(z)] * 5)
    return [a.reshape(s, A_WIDTH) for a in res]


def _attn_merge(parts_a, part_c, sink_row, tb):
    s = part_c[0].shape[0]
    n_a = len(parts_a)

    def body(*refs):
        ins, sink_ref = refs[:3 * n_a + 3], refs[3 * n_a + 3]
        ya_ref, lsea_ref, yc_ref, lsec_ref = refs[3 * n_a + 4:]
        ms = [ins[3 * p + 1][...] for p in range(n_a)]
        m = functools.reduce(jnp.maximum, ms)
        acc = jnp.zeros_like(m)
        l = jnp.zeros_like(m)
        for p in range(n_a):
            w = jnp.exp(ms[p] - m)
            acc = acc + w * ins[3 * p][...]
            l = l + w * ins[3 * p + 2][...]
        ya_ref[...] = acc / l
        lsea_ref[...] = m + jnp.log(l)
        acc_c, m_c, l_c = [r[...] for r in ins[3 * n_a:]]
        sk = sink_ref[...]
        m2 = jnp.maximum(m_c, sk)
        w = jnp.exp(m_c - m2)
        l2 = w * l_c + jnp.exp(sk - m2)
        yc_ref[...] = (w * acc_c) / l2
        lsec_ref[...] = m2 + jnp.log(l2)

    return pl.pallas_call(
        body, grid=(s // tb,), name="attn_merge",
        in_specs=[_rows(tb, A_WIDTH)] * (3 * n_a + 3) + [_whole((1, A_WIDTH))],
        out_specs=[_rows(tb, A_WIDTH)] * 4, out_shape=[jax.ShapeDtypeStruct((s, A_WIDTH), F32)] * 4,
        compiler_params=_cparams("parallel"),
    )(*[a for part in parts_a + [part_c] for a in part], sink_row)


def _shift_down(v, n, halo):
    rows = v.shape[0]
    out = pltpu.roll(v, n, 0)
    row = lax.broadcasted_iota(jnp.int32, v.shape, 0)
    for t in range(n):
        out = jnp.where(row == t, halo[HALO - n + t:HALO - n + t + 1, :], out)
    return out


def _shift_up(v, n, halo):
    rows = v.shape[0]
    out = pltpu.roll(v, rows - n, 0)
    row = lax.broadcasted_iota(jnp.int32, v.shape, 0)
    for t in range(n):
        out = jnp.where(row == rows - n + t, halo[t:t + 1, :], out)
    return out


def _strip(v, b):
    return v[b % 4, b // 4]


def _conv_strips(zb, prev, cw):
    gb = [_strip(zb, b)[:, :CONV_CH] for b in range(N_STRIPS)]
    gc = [_strip(zb, b)[:, CONV_CH:2 * CONV_CH] for b in range(N_STRIPS)]
    xb = [_strip(zb, b)[:, 2 * CONV_CH:] for b in range(N_STRIPS)]
    u = [g * v for g, v in zip(gc, xb)]
    uh = prev[:, :, CONV_CH:2 * CONV_CH] * prev[:, :, 2 * CONV_CH:]
    wrapped = {14: _shift_down(u[14], 1, uh[2]), 15: _shift_down(u[15], 1, uh[3])}
    u1 = [u[b - 1] if b >= 1 else wrapped[15] for b in range(N_STRIPS)]
    u2 = [u[b - 2] if b >= 2 else wrapped[14 + b] for b in range(N_STRIPS)]
    c = [cw[0:1, :] * u2[b] + cw[1:2, :] * u1[b] + cw[2:3, :] * u[b] for b in range(N_STRIPS)]
    return gb, gc, xb, u, u1, u2, c


def _strip_rows(ta, w):
    return pl.BlockSpec((4, 4, ta, w), lambda i: (0, 0, i, 0))


def _prev_rows(ta, w):
    return pl.BlockSpec((4, None, HALO, w), lambda i: (0, 3, jnp.maximum(i * (ta // HALO) - 1, 0), 0))


def _next_rows(ta, w, nblk):
    return pl.BlockSpec((4, None, HALO, w),
                        lambda i: (0, 0, jnp.minimum((i + 1) * (ta // HALO), nblk * (ta // HALO) - 1), 0))


def _mix_fwd(x, ya, yc, zb, cw, gg, wo_all, l, tb):
    s, d = x.shape
    ta = tb // N_STRIPS

    def body(x_ref, ya_ref, yc_ref, zb_ref, zbp_ref, cw_ref, gg_ref, wo_ref, x1_ref, yb_ref):
        i = pl.program_id(0)
        prev = jnp.where(i > 0, zbp_ref[...], 0.0)
        gb, _, _, _, _, _, c = _conv_strips(zb_ref[...], prev, cw_ref[...])
        for b in range(N_STRIPS):
            yb_ref[b % 4, b // 4] = gb[b] * c[b]
        yb = yb_ref[...].reshape(tb, CONV_CH)
        ya, yc = ya_ref[...].reshape(tb, A_WIDTH), yc_ref[...].reshape(tb, A_WIDTH)
        n = jnp.concatenate([ya * _rms_scale(ya), yb * _rms_scale(yb), yc * _rms_scale(yc)], axis=1)
        n = (n * gg_ref[...]).astype(BF)
        x1 = x_ref[...].reshape(tb, d) + _nn(n, wo_ref[...].reshape(MIX_WIDTH, d))
        x1_ref[...] = x1.reshape(4, 4, ta, d)

    res = pl.pallas_call(
        body, grid=(s // tb,), name="mix_fwd",
        in_specs=[_strip_rows(ta, d), _strip_rows(ta, A_WIDTH), _strip_rows(ta, A_WIDTH), _strip_rows(ta, ZB_W),
                  _prev_rows(ta, ZB_W), _whole((HALO, CONV_CH)), _whole((1, MIX_WIDTH)),
                  _layer((N_CHIPS, MIX_WIDTH // N_CHIPS, d), l)],
        out_specs=[_strip_rows(ta, d), _strip_rows(ta, CONV_CH)],
        out_shape=[jax.ShapeDtypeStruct((4, 4, s // N_STRIPS, d), F32),
                   jax.ShapeDtypeStruct((4, 4, s // N_STRIPS, CONV_CH), F32)],
        compiler_params=_cparams("parallel"),
    )(_strips(x), _strips(ya), _strips(yc), _strips(zb), _strips(zb), cw, gg, wo_all)
    return res[0].reshape(s, d), res[1].reshape(s, CONV_CH)


def _mlp_fwd(x1, g, w1_all, w2_all, l, tb, tf):
    s, d = x1.shape
    ff = w1_all.shape[1] * w1_all.shape[3]
    nj = ff // tf

    def body(x_ref, g_ref, w1_ref, w2_ref, x2_ref, h2_ref, ap_ref, acc):
        j = pl.program_id(1)

        @pl.when(j == 0)
        def _():
            xv = x_ref[...]
            h2_ref[...] = ((xv * _rms_scale(xv)) * g_ref[...]).astype(BF)
            acc[...] = jnp.zeros_like(acc)

        ap = _nn(h2_ref[...], w1_ref[...])
        ap_ref[...] = ap.astype(BF)
        a = jnp.square(jnp.maximum(ap, 0.0)).astype(BF)
        acc[...] += _nn(a, w2_ref[...])

        @pl.when(j == nj - 1)
        def _():
            x2_ref[...] = x_ref[...] + acc[...]

    return pl.pallas_call(
        body, grid=(s // tb, nj), name="mlp_fwd",
        in_specs=[pl.BlockSpec((tb, d), lambda i, j: (i, 0)), _whole((1, d)),
                  pl.BlockSpec((None, None, d, tf), lambda i, j: (l, j, 0, 0)),
                  pl.BlockSpec((None, None, tf, d), lambda i, j: (l, j, 0, 0))],
        out_specs=[pl.BlockSpec((tb, d), lambda i, j: (i, 0)), pl.BlockSpec((tb, d), lambda i, j: (i, 0)),
                   pl.BlockSpec((tb, tf), lambda i, j: (i, j))],
        out_shape=[jax.ShapeDtypeStruct((s, d), F32), jax.ShapeDtypeStruct((s, d), BF),
                   jax.ShapeDtypeStruct((s, ff), BF)],
        scratch_shapes=[pltpu.VMEM((tb, d), F32)],
        compiler_params=_cparams("parallel", "arbitrary"),
    )(x1, g, w1_all, w2_all)


def _loss_head(x, g, tgt, tb):
    s, d = x.shape

    def body(x_ref, g_ref, t_ref, dx_ref, loss_ref, dg_ref):
        i = pl.program_id(0)

        @pl.when(i == 0)
        def _():
            loss_ref[...] = jnp.zeros_like(loss_ref)
            dg_ref[...] = jnp.zeros_like(dg_ref)

        xv = x_ref[...]
        r = _rms_scale(xv)
        xhat = xv * r
        err = xhat * g_ref[...] - t_ref[...]
        part = jnp.sum(jnp.mean(jnp.square(err), axis=-1, keepdims=True), axis=0, keepdims=True)
        loss_ref[...] += 0.5 * part
        dy = err * (1.0 / d)
        dg_ref[...] += jnp.sum(dy * xhat, axis=0, keepdims=True)
        dx_ref[...] = _norm_bwd(dy * g_ref[...], xhat, r)

    return pl.pallas_call(
        body, grid=(s // tb,), name="loss_head",
        in_specs=[_rows(tb, d), _whole((1, d)), _rows(tb, d)],
        out_specs=[_rows(tb, d), _whole((HALO, 128)), _whole((HALO, d))],
        out_shape=[jax.ShapeDtypeStruct((s, d), F32), jax.ShapeDtypeStruct((HALO, 128), F32),
                   jax.ShapeDtypeStruct((HALO, d), F32)],
        compiler_params=_cparams("arbitrary"),
    )(x, g, tgt)


def _mlp_bwd(dx2, x1, ap, g, w1_all, w2_all, l, tb, tf):
    s, d = x1.shape
    ff = ap.shape[1]
    nj = ff // tf

    def body(dx2_ref, x1_ref, ap_ref, g_ref, w1_ref, w2_ref, dx1_ref, dap_ref, dg_ref, acc):
        i, j = pl.program_id(0), pl.program_id(1)

        @pl.when((i == 0) & (j == 0))
        def _():
            dg_ref[...] = jnp.zeros_like(dg_ref)

        @pl.when(j == 0)
        def _():
            acc[...] = jnp.zeros_like(acc)

        da = _nt(dx2_ref[...].astype(BF), w2_ref[...])
        dap = (da * (2.0 * jnp.maximum(ap_ref[...].astype(F32), 0.0))).astype(BF)
        dap_ref[...] = dap
        acc[...] += _nt(dap, w1_ref[...])

        @pl.when(j == nj - 1)
        def _():
            xv = x1_ref[...]
            r = _rms_scale(xv)
            xhat = xv * r
            dh = acc[...]
            dg_ref[...] += jnp.sum(dh * xhat, axis=0, keepdims=True)
            dx1_ref[...] = dx2_ref[...] + _norm_bwd(dh * g_ref[...], xhat, r)

    return pl.pallas_call(
        body, grid=(s // tb, nj), name="mlp_bwd",
        in_specs=[pl.BlockSpec((tb, d), lambda i, j: (i, 0)), pl.BlockSpec((tb, d), lambda i, j: (i, 0)),
                  pl.BlockSpec((tb, tf), lambda i, j: (i, j)),
                  _whole((1, d)), pl.BlockSpec((None, None, d, tf), lambda i, j: (l, j, 0, 0)),
                  pl.BlockSpec((None, None, tf, d), lambda i, j: (l, j, 0, 0))],
        out_specs=[pl.BlockSpec((tb, d), lambda i, j: (i, 0)), pl.BlockSpec((tb, tf), lambda i, j: (i, j)),
                   _whole((HALO, d))],
        out_shape=[jax.ShapeDtypeStruct((s, d), F32), jax.ShapeDtypeStruct((s, ff), BF),
                   jax.ShapeDtypeStruct((HALO, d), F32)],
        scratch_shapes=[pltpu.VMEM((tb, d), F32)],
        compiler_params=_cparams("arbitrary", "arbitrary"),
    )(dx2, x1, ap, g, w1_all, w2_all)


def _wgrad(a, b, tm, tn, ts, name, relu2=False):
    s, m = a.shape
    n = b.shape[1]
    ns = s // ts

    def body(a_ref, b_ref, o_ref, acc):
        k = pl.program_id(2)

        @pl.when(k == 0)
        def _():
            acc[...] = jnp.zeros_like(acc)

        av = a_ref[...]
        if relu2:
            av = jnp.square(jnp.maximum(av.astype(F32), 0.0)).astype(BF)
        acc[...] += _tn(av, b_ref[...].astype(BF))

        @pl.when(k == ns - 1)
        def _():
            o_ref[...] = acc[...].astype(BF)

    return pl.pallas_call(
        body, grid=(m // tm, n // tn, ns), name=name,
        in_specs=[pl.BlockSpec((ts, tm), lambda i, j, k: (k, i)), pl.BlockSpec((ts, tn), lambda i, j, k: (k, j))],
        out_specs=pl.BlockSpec((tm, tn), lambda i, j, k: (i, j)),
        out_shape=jax.ShapeDtypeStruct((m, n), BF),
        scratch_shapes=[pltpu.VMEM((tm, tn), F32)],
        compiler_params=_cparams("parallel", "parallel", "arbitrary"),
    )(a, b)


def _mix_bwd(dx1, ya, yb, yc, lse_c, sink_row, gg, wo_all, l, tb):
    s, d = dx1.shape

    def body(dx_ref, ya_ref, yb_ref, yc_ref, lse_ref, sink_ref, gg_ref, wo_ref,
             n_ref, dya_ref, dyc_ref, da_ref, dc_ref, dyb_ref, dg_ref, dsink_ref):
        i = pl.program_id(0)

        @pl.when(i == 0)
        def _():
            dg_ref[...] = jnp.zeros_like(dg_ref)
            dsink_ref[...] = jnp.zeros_like(dsink_ref)

        dn = _nt(dx_ref[...].astype(BF), wo_ref[...].reshape(MIX_WIDTH, d))
        ys = [ya_ref[...], yb_ref[...], yc_ref[...]]
        rs = [_rms_scale(v) for v in ys]
        nhat = jnp.concatenate([v * r for v, r in zip(ys, rs)], axis=1)
        gg = gg_ref[...]
        n_ref[...] = (nhat * gg).astype(BF)
        dg_ref[...] += jnp.sum(dn * nhat, axis=0, keepdims=True)
        dnh = dn * gg
        bounds = [(0, A_WIDTH), (A_WIDTH, A_WIDTH + CONV_CH), (A_WIDTH + CONV_CH, MIX_WIDTH)]
        dys = [_norm_bwd(dnh[:, lo:hi], nhat[:, lo:hi], r) for (lo, hi), r in zip(bounds, rs)]
        dyb_ref[...] = dys[1]
        for dy, y, dy_ref, dd_ref in ((dys[0], ys[0], dya_ref, da_ref), (dys[2], ys[2], dyc_ref, dc_ref)):
            dy_ref[...] = dy
            t = dy * y
            for h in range(N_HEADS):
                dd_ref[:, _hs(h)] = jnp.broadcast_to(jnp.sum(t[:, _hs(h)], axis=1, keepdims=True), (tb, HEAD_DIM))
        dsink_ref[...] -= jnp.sum(jnp.exp(sink_ref[...] - lse_ref[...]) * dc_ref[...], axis=0, keepdims=True)

    return pl.pallas_call(
        body, grid=(s // tb,), name="mix_bwd",
        in_specs=[_rows(tb, d), _rows(tb, A_WIDTH), _rows(tb, CONV_CH), _rows(tb, A_WIDTH), _rows(tb, A_WIDTH),
                  _whole((1, A_WIDTH)), _whole((1, MIX_WIDTH)), _layer((N_CHIPS, MIX_WIDTH // N_CHIPS, d), l)],
        out_specs=[_rows(tb, MIX_WIDTH), _rows(tb, A_WIDTH), _rows(tb, A_WIDTH), _rows(tb, A_WIDTH),
                   _rows(tb, A_WIDTH), _rows(tb, CONV_CH), _whole((HALO, MIX_WIDTH)), _whole((HALO, A_WIDTH))],
        out_shape=[jax.ShapeDtypeStruct((s, MIX_WIDTH), BF), jax.ShapeDtypeStruct((s, A_WIDTH), F32),
                   jax.ShapeDtypeStruct((s, A_WIDTH), F32), jax.ShapeDtypeStruct((s, A_WIDTH), F32),
                   jax.ShapeDtypeStruct((s, A_WIDTH), F32), jax.ShapeDtypeStruct((s, CONV_CH), F32),
                   jax.ShapeDtypeStruct((HALO, MIX_WIDTH), F32), jax.ShapeDtypeStruct((HALO, A_WIDTH), F32)],
        compiler_params=_cparams("arbitrary"),
    )(dx1, ya, yb, yc, lse_c, sink_row, gg, wo_all)


def _attn_bwd(z, dy, lse, dd, dil, kw, kcol, vcol, n_rep, max_dist, name):
    s, zw = z.shape
    grid = _p_grid(s, dil)
    n_kv = N_HEADS // n_rep

    def body(q_ref, kp_ref, kc_ref, vp_ref, vc_ref, dy_ref, lse_ref, dd_ref, dq_ref, dkp_ref, dkc_ref, dvp_ref, dvc_ref):
        mask = _band_mask(pl.program_id(len(grid) - 1), dil, max_dist)
        for kh in range(n_kv):
            k2 = jnp.concatenate([_ld(kp_ref, _hs(kh)), _ld(kc_ref, _hs(kh))], axis=0).astype(BF)
            v2 = jnp.concatenate([_ld(vp_ref, _hs(kh)), _ld(vc_ref, _hs(kh))], axis=0).astype(BF)
            dk2 = jnp.zeros((2 * TQ, HEAD_DIM), F32)
            dv2 = jnp.zeros((2 * TQ, HEAD_DIM), F32)
            for h in range(kh * n_rep, (kh + 1) * n_rep):
                q = _ld(q_ref, _hs(h)).astype(BF)
                lse_h = _ld(lse_ref, slice(h * HEAD_DIM, h * HEAD_DIM + 1))
                dd_h = _ld(dd_ref, slice(h * HEAD_DIM, h * HEAD_DIM + 1))
                dyh = _ld(dy_ref, _hs(h)).astype(BF)
                sc = jnp.where(mask, _nt(q, k2) * SCALE, NEG)
                p = jnp.exp(sc - lse_h)
                dp = _nt(dyh, v2)
                ds = ((p * (dp - dd_h)) * SCALE).astype(BF)
                _st(dq_ref, _hs(h), _nn(ds, k2))
                dk2 = dk2 + _tn(ds, q)
                dv2 = dv2 + _tn(p.astype(BF), dyh)
            _st(dkp_ref, _hs(kh), dk2[:TQ])
            _st(dkc_ref, _hs(kh), dk2[TQ:])
            _st(dvp_ref, _hs(kh), dv2[:TQ])
            _st(dvc_ref, _hs(kh), dv2[TQ:])

    args = [_strips(z)] * 5 + [_strips(a) for a in (dy, lse, dd)]
    in_specs = [_p_spec(dil, A_WIDTH, 0), _p_spec(dil, kw, kcol, True), _p_spec(dil, kw, kcol),
                _p_spec(dil, kw, vcol, True), _p_spec(dil, kw, vcol)] + [_p_spec(dil, A_WIDTH, 0)] * 3
    out_specs = [_p_spec(dil, A_WIDTH, 0)] + [_p_spec(dil, kw, 0)] * 4
    na = s // N_STRIPS
    out_shape = [jax.ShapeDtypeStruct((4, 4, na, A_WIDTH), F32)] + [jax.ShapeDtypeStruct((4, 4, na, kw), F32)] * 4
    res = pl.pallas_call(
        body, grid=grid, name=name, in_specs=in_specs, out_specs=out_specs, out_shape=out_shape,
        compiler_params=_cparams(*(("parallel",) * len(grid))),
    )(*args)
    return [res[0].reshape(s, A_WIDTH)] + [a.reshape(s, kw) for a in res[1:]]


DZ_TA = 16


def _dz_assemble(parts_a, parts_c, dyb, zb, cw):
    s = zb.shape[0]
    na = s // N_STRIPS
    nb = na // DZ_TA

    def ahead(w, k):
        return pl.BlockSpec((4, 4, DZ_TA, w), lambda i: (0, 0, jnp.minimum(i + k, nb - 1), 0))

    args, in_specs = [], []
    for dil, (dq, dkp, dkc, dvp, dvc) in zip(DILATIONS + (1,), parts_a + [parts_c]):
        w = dkp.shape[1]
        here = _strip_rows(DZ_TA, w)
        if dil == 1:
            args += [dq, dkp, dkp, dkc, dvp, dvp, dvc]
            in_specs += [_strip_rows(DZ_TA, A_WIDTH), here, ahead(w, 1), here, here, ahead(w, 1), here]
        else:
            k = 8 * dil // DZ_TA
            args += [dq, dkp, dkc, dvp, dvc]
            in_specs += [_strip_rows(DZ_TA, A_WIDTH), ahead(w, k), here, ahead(w, k), here]
    n_att = len(args)
    args = [_strips(a) for a in args] + [_strips(dyb), _strips(dyb), _strips(zb), _strips(zb), _strips(zb), cw]
    in_specs += [_strip_rows(DZ_TA, CONV_CH), _next_rows(DZ_TA, CONV_CH, nb), _strip_rows(DZ_TA, ZB_W),
                 _prev_rows(DZ_TA, ZB_W), _next_rows(DZ_TA, ZB_W, nb), _whole((HALO, CONV_CH))]

    def body(*refs):
        att = list(refs[:n_att])
        dyb_ref, dybn_ref, zb_ref, zbp_ref, zbn_ref, cw_ref, dz_ref, dcw_ref = refs[n_att:]
        i = pl.program_id(0)

        @pl.when(i == 0)
        def _():
            dcw_ref[...] = jnp.zeros_like(dcw_ref)

        def shifted(dil):
            if dil == 1:
                dq_r, kp0, kp1, dkc_r, vp0, vp1, dvc_r = [att.pop(0) for _ in range(7)]
                live = i + 1 < nb
                half = DZ_TA // 2
                dkp = jnp.concatenate([kp0[:, :, half:, :], jnp.where(live, kp1[:, :, :half, :], 0.0)], axis=2)
                dvp = jnp.concatenate([vp0[:, :, half:, :], jnp.where(live, vp1[:, :, :half, :], 0.0)], axis=2)
            else:
                dq_r, dkp_r, dkc_r, dvp_r, dvc_r = [att.pop(0) for _ in range(5)]
                live = i + 8 * dil // DZ_TA < nb
                dkp, dvp = jnp.where(live, dkp_r[...], 0.0), jnp.where(live, dvp_r[...], 0.0)
            return dq_r[...], dkc_r[...] + dkp, dvc_r[...] + dvp

        dq, dk, dv = shifted(DILATIONS[0])
        for dil in DILATIONS[1:]:
            dq2, dk2, dv2 = shifted(dil)
            dq, dk, dv = dq + dq2, dk + dk2, dv + dv2
        dz_ref[:, :, :, 0:A_WIDTH] = dq.astype(BF)
        dz_ref[:, :, :, A_WIDTH:2 * A_WIDTH] = dk.astype(BF)
        dz_ref[:, :, :, 2 * A_WIDTH:ZA_W] = dv.astype(BF)
        dq, dk, dv = shifted(1)
        c0 = ZA_W + ZB_W
        dz_ref[:, :, :, c0:c0 + A_WIDTH] = dq.astype(BF)
        dz_ref[:, :, :, c0 + A_WIDTH:c0 + A_WIDTH + C_KV_WIDTH] = dk.astype(BF)
        dz_ref[:, :, :, c0 + A_WIDTH + C_KV_WIDTH:IN_WIDTH] = dv.astype(BF)

        cw = cw_ref[...]
        prev = jnp.where(i > 0, zbp_ref[...], 0.0)
        gb, gc, xb, u, u1, u2, c = _conv_strips(zb_ref[...], prev, cw)
        dyb = dyb_ref[...]
        dc = [_strip(dyb, b) * gb[b] for b in range(N_STRIPS)]
        dcn = jnp.where(i + 1 < nb, dybn_ref[...] * zbn_ref[:, :, :CONV_CH], 0.0)
        wrapped = [_shift_up(dc[0], 1, dcn[0]), _shift_up(dc[1], 1, dcn[1])]
        upd = [jnp.zeros((1, CONV_CH), F32)] * 3
        for b in range(N_STRIPS):
            dc1 = dc[b + 1] if b + 1 < N_STRIPS else wrapped[0]
            dc2 = dc[b + 2] if b + 2 < N_STRIPS else wrapped[b + 2 - N_STRIPS]
            du = cw[2:3, :] * dc[b] + cw[1:2, :] * dc1 + cw[0:1, :] * dc2
            f, e = b % 4, b // 4
            dz_ref[f, e, :, ZA_W:ZA_W + CONV_CH] = (_strip(dyb, b) * c[b]).astype(BF)
            dz_ref[f, e, :, ZA_W + CONV_CH:ZA_W + 2 * CONV_CH] = (du * xb[b]).astype(BF)
            dz_ref[f, e, :, ZA_W + 2 * CONV_CH:c0] = (du * gc[b]).astype(BF)
            for t, uu in enumerate((u2[b], u1[b], u[b])):
                upd[t] = upd[t] + jnp.sum(dc[b] * uu, axis=0, keepdims=True)
        row = lax.broadcasted_iota(jnp.int32, (HALO, CONV_CH), 0)
        tile = jnp.zeros((HALO, CONV_CH), F32)
        for t in range(3):
            tile = jnp.where(row == t, upd[t], tile)
        dcw_ref[...] += tile

    dz, dcw = pl.pallas_call(
        body, grid=(nb,), name="dz_assemble", in_specs=in_specs,
        out_specs=[_strip_rows(DZ_TA, IN_WIDTH), _whole((HALO, CONV_CH))],
        out_shape=[jax.ShapeDtypeStruct((4, 4, na, IN_WIDTH), BF), jax.ShapeDtypeStruct((HALO, CONV_CH), F32)],
        compiler_params=_cparams("arbitrary"),
    )(*args)
    return dz.reshape(s, IN_WIDTH), dcw


def _qkv_bwd(dz, dx1, x, g, w_all, l, tb):
    s, d = x.shape

    def body(dz_ref, dx1_ref, x_ref, g_ref, w_ref, dx_ref, dg_ref):
        i = pl.program_id(0)

        @pl.when(i == 0)
        def _():
            dg_ref[...] = jnp.zeros_like(dg_ref)

        n = IN_WIDTH // N_CHIPS
        dh = _nt(dz_ref[:, 0:n], w_ref[0])
        for k in range(1, N_CHIPS):
            dh = dh + _nt(dz_ref[:, k * n:(k + 1) * n], w_ref[k])
        xv = x_ref[...]
        r = _rms_scale(xv)
        xhat = xv * r
        dg_ref[...] += jnp.sum(dh * xhat, axis=0, keepdims=True)
        dx_ref[...] = dx1_ref[...] + _norm_bwd(dh * g_ref[...], xhat, r)

    return pl.pallas_call(
        body, grid=(s // tb,), name="qkv_bwd",
        in_specs=[_rows(tb, IN_WIDTH), _rows(tb, d), _rows(tb, d), _whole((1, d)),
                  _layer((N_CHIPS, d, IN_WIDTH // N_CHIPS), l)],
        out_specs=[_rows(tb, d), _whole((HALO, d))],
        out_shape=[jax.ShapeDtypeStruct((s, d), F32), jax.ShapeDtypeStruct((HALO, d), F32)],
        compiler_params=_cparams("arbitrary"),
    )(dz, dx1, x, g, w_all)


def _tile_rows(rows):
    return jnp.pad(rows, ((0, HALO - rows.shape[0]), (0, 0)))


def _to_strips(a):
    s, d = a.shape
    return a.reshape(s // N_STRIPS, 4, 4, d).transpose(2, 1, 0, 3).reshape(s, d)


def _from_strips(a):
    s, d = a.shape
    return a.reshape(4, 4, s // N_STRIPS, d).transpose(2, 1, 0, 3).reshape(s, d)


def _local_step(x, tgt, fetch, ff, sinks, g_mix, g_group, g_mlp, g_final, emit):
    s, d = x.shape
    depth = g_mix.shape[0]
    tb = min(512, s)
    tf = ff // N_CHIPS
    ts = min(1024, s)
    saved = []
    for l in range(depth):
        w_in, _, _, _, conv_w = fetch(0, l, x)
        cw = _tile_rows(conv_w[l])
        sk = jnp.repeat(sinks[l].reshape(N_HEADS), HEAD_DIM)[None]
        h, za, zb, zc = _qkv_fwd(x, g_mix[l][None], w_in, l, tb)
        parts_a = [_attn_fwd(za, dil, A_WIDTH, 1, 2, 1, A_MAX_DIST, "attn_a_fwd_%d" % dil) for dil in DILATIONS]
        part_c = _attn_fwd(zc, 1, C_KV_WIDTH, 3, 4, C_GROUP, C_MAX_DIST, "attn_c_fwd")
        ya, lse_a, yc, lse_c = _attn_merge(parts_a, part_c, sk, tb)
        w_in, w_o, w1, w2, _ = fetch(1, l, yc)
        x1, yb = _mix_fwd(x, ya, yc, zb, cw, g_group[l][None], w_o, l, tb)
        x2, h2, ap = _mlp_fwd(x1, g_mlp[l][None], w1, w2, l, tb, tf)
        saved.append((x, h, za, zb, zc, ya, lse_a, yc, lse_c, yb, x1, h2, ap, cw, sk))
        x = x2
    dx, loss_tile, dg_final = _loss_head(x, g_final[None], tgt, tb)
    grads = [None] * depth
    tok = jnp.zeros((), F32)
    for l in reversed(range(depth)):
        x0, h, za, zb, zc, ya, lse_a, yc, lse_c, yb, x1, h2, ap, cw, sk = saved[l]
        dx1, dap, dg_mlp = _mlp_bwd(dx, x1, ap, g_mlp[l][None] + tok, w1, w2, l, tb, tf)
        tok = emit(l, 3, _wgrad(ap, dx, min(1024, ff), d, ts, "wgrad_ff_out", relu2=True))
        tok = tok + emit(l, 2, _wgrad(h2, dap, d, min(1024, ff), ts, "wgrad_ff_in"))
        n, dya, dyc, dd_a, dd_c, dyb, dg_group, dsink = _mix_bwd(dx1, ya, yb, yc, lse_c, sk, g_group[l][None] + tok,
                                                                 w_o, l, tb)
        tok = emit(l, 1, _wgrad(n, dx1, MIX_WIDTH, d, ts, "wgrad_o"))
        cw = cw + tok
        parts_a = [_attn_bwd(za, dya, lse_a, dd_a, dil, A_WIDTH, 1, 2, 1, A_MAX_DIST, "attn_a_bwd_%d" % dil)
                   for dil in DILATIONS]
        parts_c = _attn_bwd(zc, dyc, lse_c, dd_c, 1, C_KV_WIDTH, 3, 4, C_GROUP, C_MAX_DIST, "attn_c_bwd")
        dz, dcw = _dz_assemble(parts_a, parts_c, dyb, zb, cw)
        dx, dg_mix = _qkv_bwd(dz, dx1, x0, g_mix[l][None], w_in, l, tb)
        tok = emit(l, 0, _wgrad(h, dz, d, IN_WIDTH // 4, ts, "wgrad_in"))
        grads[l] = (dcw, dsink, dg_mix, dg_group, dg_mlp)
    return loss_tile, dx, grads, dg_final


ANY = pl.BlockSpec(memory_space=pl.ANY)
SHARD_AXES = (2, 1, 2, 1)
N_BIG = len(SHARD_AXES)
N_CHIPS = 4
N_DEV = 8


def _mesh_pos():
    return lax.axis_index("x"), lax.axis_index("y"), lax.axis_index("c")


def _flip(v, bit):
    return 1 - v if bit else v


def _place_shard(shard, chip_arr, name):
    _, rows, cols = shard.shape
    tr = min(256, rows)

    def body(chip_ref, x_ref, o_ref):
        o_ref[...] = x_ref[...].astype(BF)

    return pl.pallas_call(
        body, name=name,
        grid_spec=pltpu.PrefetchScalarGridSpec(
            num_scalar_prefetch=1, grid=(2, rows // tr),
            in_specs=[pl.BlockSpec((None, tr, cols), lambda l, i, chip: (l, i, 0))],
            out_specs=pl.BlockSpec((None, None, tr, cols), lambda l, i, chip: (l, chip[0], i, 0))),
        out_shape=jax.ShapeDtypeStruct((2, N_CHIPS, rows, cols), BF),
        compiler_params=_cparams("parallel", "parallel"),
    )(chip_arr, shard)


HBM = pl.BlockSpec(memory_space=pltpu.HBM)
SEM = pl.BlockSpec(memory_space=pltpu.SEMAPHORE)
EFFECT = pltpu.SideEffectType.DATAFLOW_SIDE_EFFECTING

GATHER_GROUPS = (((0, 0),), ((1, 0), (2, 0), (3, 0)), ((0, 1),), ((1, 1), (2, 1), (3, 1)))
GATHER_STARTS = ((0,), (1,), (2, 3))


def _gather_copies(arrs, group, send_sems, recv_sems):
    x, y, c = _mesh_pos()
    me = 2 * x + y
    out = []
    for i, (w, layer) in enumerate(group):
        mine = arrs[w].at[layer, me]
        for j, (qx, qy) in enumerate([(1 - x, y), (x, 1 - y), (1 - x, 1 - y)]):
            landed = arrs[w].at[layer, 2 * qx + qy]
            out.append(tuple(pltpu.make_async_remote_copy(
                src_ref=piece, dst_ref=piece, send_sem=send_sems.at[i * 3 + j], recv_sem=recv_sems.at[i * 3 + j],
                device_id=(qx, qy, c), device_id_type=MESH) for piece in (mine, landed)))
    return out


def _conv_copies(conv_src, conv_dst, send_sems, recv_sems):
    x, y, c = _mesh_pos()
    out = []
    for j, (qx, qy) in enumerate([(1 - x, y), (x, 1 - y), (1 - x, 1 - y)]):
        out.append(tuple(pltpu.make_async_remote_copy(
            src_ref=conv_src, dst_ref=conv_dst.at[q], send_sem=send_sems.at[j], recv_sem=recv_sems.at[j],
            device_id=(qx, qy, c), device_id_type=MESH) for q in (2 * x + y, 2 * qx + qy)))
    return out


def _gather_start(groups, arrs, conv, name):
    n_sems = 2 * (len(groups) + (conv is not None))
    mats = sorted({w for g in groups for w, _ in GATHER_GROUPS[g]})

    def body(*refs):
        arrs_ref = [None] * N_BIG
        for w, ref in zip(mats, refs):
            arrs_ref[w] = ref
        sems = refs[n_op:n_op + n_sems]
        if conv is not None:
            for cp, _ in _conv_copies(refs[len(mats)], refs[len(mats) + 1], sems[-2], sems[-1]):
                cp.start()
        for k, g in enumerate(groups):
            for cp, _ in _gather_copies(arrs_ref, GATHER_GROUPS[g], sems[2 * k], sems[2 * k + 1]):
                cp.start()

    sem_shapes = []
    for n in [len(GATHER_GROUPS[g]) for g in groups] + ([1] if conv is not None else []):
        sem_shapes += [pltpu.SemaphoreType.DMA((3 * n,))] * 2
    operands = [arrs[w] for w in mats] + ([] if conv is None else list(conv))
    n_op = len(operands)
    res = pl.pallas_call(
        body, name=name,
        out_shape=tuple(sem_shapes) + tuple(pltpu.HBM(a.shape, a.dtype) for a in operands),
        in_specs=(HBM,) * n_op, out_specs=(SEM,) * n_sems + (HBM,) * n_op,
        input_output_aliases={i: n_sems + i for i in range(n_op)},
        compiler_params=pltpu.CompilerParams(has_side_effects=EFFECT),
    )(*[pltpu.with_memory_space_constraint(a, pltpu.HBM) for a in operands])
    arrs = list(arrs)
    for w, a in zip(mats, res[n_sems:]):
        arrs[w] = a
    return res[:n_sems], arrs, list(res[n_sems + len(mats):])


def _gather_wait(k, sems, arrs, conv, after, name):
    group = GATHER_GROUPS[k]
    mats = sorted({w for w, _ in group})
    n_conv = 0 if conv is None else 2

    def body(*refs):
        local = refs[:len(mats)]
        arrs_ref = [None] * N_BIG
        for w, ref in zip(mats, local):
            arrs_ref[w] = ref
        pos = len(mats) + n_conv
        copies = _gather_copies(arrs_ref, group, refs[pos], refs[pos + 1])
        if conv is not None:
            copies += _conv_copies(refs[len(mats)], refs[len(mats) + 1], refs[pos + 2], refs[pos + 3])
        for send, recv in copies:
            recv.wait_recv()
            send.wait_send()

    operands = [arrs[w] for w in mats] + ([] if conv is None else [conv[1], conv[2]])
    sem_ops = list(sems) + ([] if conv is None else list(conv[0]))
    n_op = len(operands)
    res = pl.pallas_call(
        body, name=name, out_shape=tuple(pltpu.HBM(a.shape, a.dtype) for a in operands),
        in_specs=(HBM,) * n_op + (SEM,) * len(sem_ops) + (ANY,), out_specs=(HBM,) * n_op,
        input_output_aliases={i: i for i in range(n_op)},
        compiler_params=pltpu.CompilerParams(has_side_effects=EFFECT),
    )(*operands, *sem_ops, after)
    arrs = list(arrs)
    for w, a in zip(mats, res):
        arrs[w] = a
    return arrs, (res[-1] if conv is not None else None)


def _grad_shard(ref, w, chip, n):
    start = pl.multiple_of(chip * n, 128)
    if SHARD_AXES[w] == 2:
        return ref.at[:, pl.ds(start, n)]
    return ref.at[pl.ds(start, n), :]


def _slot_shape(g, w):
    shape = list(g.shape)
    shape[SHARD_AXES[w] - 1] //= N_CHIPS
    return (N_DEV - 1,) + tuple(shape)


def _scatter_copies(g_ref, land_ref, send_sems, recv_sems, layer, w):
    x, y, c = _mesh_pos()
    n = g_ref.shape[SHARD_AXES[w] - 1] // N_CHIPS
    out = []
    for r in range(1, N_DEV):
        tx, ty, tc = _flip(x, r & 4), _flip(y, r & 2), _flip(c, r & 1)
        cp = pltpu.make_async_remote_copy(
            src_ref=_grad_shard(g_ref, w, 2 * tx + ty, n), dst_ref=land_ref.at[r - 1], send_sem=send_sems.at[r - 1],
            recv_sem=recv_sems.at[r - 1], device_id=(tx, ty, tc), device_id_type=MESH)
        out.append((cp, (c != layer) if r & 1 else (c == layer)))
    return out


def _scatter_start(g, land, layer, w, name):
    def body(g_ref, land_ref, send_sems, recv_sems, g_thru, land_thru, token):
        for cp, mine in _scatter_copies(g_ref, land_ref, send_sems, recv_sems, layer, w):
            @pl.when(mine)
            def _():
                cp.start()
        token[...] = jnp.zeros_like(token)

    return pl.pallas_call(
        body, name=name,
        out_shape=(pltpu.SemaphoreType.DMA((N_DEV - 1,)), pltpu.SemaphoreType.DMA((N_DEV - 1,)),
                   pltpu.HBM(g.shape, g.dtype), pltpu.HBM(land.shape, land.dtype),
                   jax.ShapeDtypeStruct((HALO, 128), F32)),
        in_specs=(HBM, HBM), out_specs=(SEM, SEM, HBM, HBM, pl.BlockSpec(memory_space=pltpu.VMEM)),
        input_output_aliases={0: 2, 1: 3}, compiler_params=pltpu.CompilerParams(has_side_effects=EFFECT),
    )(pltpu.with_memory_space_constraint(g, pltpu.HBM), pltpu.with_memory_space_constraint(land, pltpu.HBM))


def _scatter_wait(started, land, after, w, name):
    def body(g0_ref, g1_ref, land_ref, ss0, rs0, ss1, rs1, after_ref, g0_out, g1_out, land_out):
        c = lax.axis_index("c")
        for layer, g_ref, ss, rs in ((0, g0_ref, ss0, rs0), (1, g1_ref, ss1, rs1)):
            for cp, mine in _scatter_copies(g_ref, land_ref, ss, rs, layer, w):
                @pl.when(mine)
                def _():
                    cp.wait_send()

                @pl.when(c == layer)
                def _():
                    cp.wait_recv()

    (ss0, rs0, g0), (ss1, rs1, g1) = started
    return pl.pallas_call(
        body, name=name,
        out_shape=(pltpu.HBM(g0.shape, g0.dtype), pltpu.HBM(g1.shape, g1.dtype), pltpu.HBM(land.shape, land.dtype)),
        in_specs=(HBM, HBM, HBM, SEM, SEM, SEM, SEM, ANY), out_specs=(HBM, HBM, HBM),
        input_output_aliases={0: 0, 1: 1, 2: 2}, compiler_params=pltpu.CompilerParams(has_side_effects=EFFECT),
    )(g0, g1, land, ss0, rs0, ss1, rs1, after)


def _sum_slots(g0, g1, slots, w, pos_arr, name):
    _, rows, cols = slots.shape
    tr = min(256, rows)
    nr = rows // tr
    if SHARD_AXES[w] == 2:
        own = pl.BlockSpec((tr, cols), lambda i, pos: (i, pos[0]))
    else:
        own = pl.BlockSpec((tr, cols), lambda i, pos: (pos[0] * nr + i, 0))

    def body(pos_ref, own0_ref, own1_ref, s_ref, o_ref):
        acc = jnp.where(pos_ref[1] == 0, own0_ref[...], own1_ref[...]).astype(F32)
        for r in range(N_DEV - 1):
            acc = acc + s_ref[r].astype(F32)
        o_ref[...] = acc

    return pl.pallas_call(
        body, name=name,
        grid_spec=pltpu.PrefetchScalarGridSpec(
            num_scalar_prefetch=1, grid=(nr,),
            in_specs=[own, own, pl.BlockSpec((N_DEV - 1, tr, cols), lambda i, pos: (0, i, 0))],
            out_specs=pl.BlockSpec((tr, cols), lambda i, pos: (i, 0))),
        out_shape=jax.ShapeDtypeStruct((rows, cols), F32), compiler_params=_cparams("parallel"),
    )(pos_arr, g0, g1, slots)


def _swap_layers(halves):
    def body(*refs):
        srcs, dsts = refs[:N_BIG], refs[N_BIG:2 * N_BIG]
        send_sems, recv_sems = refs[2 * N_BIG:]
        x, y, c = _mesh_pos()
        sends = [pltpu.make_async_remote_copy(src_ref=srcs[w], dst_ref=dsts[w], send_sem=send_sems.at[w],
                                              recv_sem=recv_sems.at[w], device_id=(x, y, 1 - c), device_id_type=MESH)
                 for w in range(N_BIG)]
        for cp in sends:
            cp.start()
        for cp in sends:
            cp.wait_recv()
        for cp in sends:
            cp.wait_send()

    return pl.pallas_call(
        body, name="swap_layers", in_specs=[ANY] * N_BIG, out_specs=[ANY] * N_BIG,
        out_shape=[jax.ShapeDtypeStruct(h.shape, h.dtype) for h in halves],
        scratch_shapes=[pltpu.SemaphoreType.DMA((N_BIG,)), pltpu.SemaphoreType.DMA((N_BIG,))],
    )(*halves)


def _adamw_math(w, g, m, v):
    m = ADAM_B1 * m + (1.0 - ADAM_B1) * g
    v = ADAM_B2 * v + (1.0 - ADAM_B2) * jnp.square(g)
    m_hat = m / (1.0 - ADAM_B1 ** ADAM_STEP)
    v_hat = v / (1.0 - ADAM_B2 ** ADAM_STEP)
    delta = -ADAM_LR * (m_hat / (jnp.sqrt(v_hat) + ADAM_EPS) + ADAM_WD * w)
    return delta, m, v


def _adamw(w, g_own, g_other, m, v, pos_arr, name):
    shape = w.shape
    _, rows, cols = shape
    tr = min(256, rows)

    def body(pos_ref, w_ref, own_ref, other_ref, m_ref, v_ref, g_ref, d_ref, m2_ref, v2_ref):
        g = jnp.where(pl.program_id(0) == pos_ref[1], own_ref[...], other_ref[...])
        g_ref[...] = g
        d_ref[...], m2_ref[...], v2_ref[...] = _adamw_math(w_ref[...], g, m_ref[...], v_ref[...])

    full = pl.BlockSpec((None, tr, cols), lambda l, i, pos: (l, i, 0))
    half = pl.BlockSpec((tr, cols), lambda l, i, pos: (i, 0))
    return pl.pallas_call(
        body, name=name,
        grid_spec=pltpu.PrefetchScalarGridSpec(
            num_scalar_prefetch=1, grid=(2, rows // tr),
            in_specs=[full, half, half, full, full], out_specs=[full] * 4),
        out_shape=[jax.ShapeDtypeStruct(shape, F32)] * 4, compiler_params=_cparams("parallel", "parallel"),
    )(pos_arr, w, g_own, g_other, m, v)


def _small_sync(part, w, m, v):
    rows, cols = part.shape

    def body(p_ref, w_ref, m_ref, v_ref, g_ref, d_ref, m2_ref, v2_ref, slots, send_sems, recv_sems):
        x, y, c = _mesh_pos()
        me = 4 * x + 2 * y + c
        slots[me] = p_ref[...]
        sends = []
        for r in range(1, N_DEV):
            to = (_flip(x, r & 4), _flip(y, r & 2), _flip(c, r & 1))
            sends.append(pltpu.make_async_remote_copy(
                src_ref=p_ref, dst_ref=slots.at[me], send_sem=send_sems.at[r - 1], recv_sem=recv_sems.at[r - 1],
                device_id=to, device_id_type=MESH))
        for cp in sends:
            cp.start()
        for cp in sends:
            cp.wait_recv()
        for cp in sends:
            cp.wait_send()
        g = slots[0]
        for i in range(1, N_DEV):
            g = g + slots[i]
        g_ref[...] = g
        d_ref[...], m2_ref[...], v2_ref[...] = _adamw_math(w_ref[...], g, m_ref[...], v_ref[...])

    vm = pl.BlockSpec(memory_space=pltpu.VMEM)
    return pl.pallas_call(
        body, name="small_sync", in_specs=[vm] * 4, out_specs=[vm] * 4,
        out_shape=[jax.ShapeDtypeStruct((rows, cols), F32)] * 4,
        scratch_shapes=[pltpu.VMEM((N_DEV, rows, cols), F32), pltpu.SemaphoreType.DMA((N_DEV - 1,)),
                        pltpu.SemaphoreType.DMA((N_DEV - 1,))],
    )(part, w, m, v)


def _pack_small(d, g_mix, g_group, g_mlp, g_final, conv_full, sinks, scalar):
    def part(rows):
        return jnp.pad(rows, ((0, HALO - rows.shape[0]), (0, d - rows.shape[1])))
    return jnp.concatenate([part(g_mix), part(g_group), part(g_mlp), part(g_final[None]),
                            part(conv_full.reshape(6, CONV_CH)), part(sinks.reshape(2, N_HEADS)),
                            part(scalar.reshape(1, 1))], axis=0)


def _unpack_small(p, dm):
    return (p[0:2, :dm], p[8:10, :MIX_WIDTH], p[16:18, :dm], p[24, :dm], p[32:38, :CONV_CH].reshape(2, 3, CONV_CH),
            p[40:42, :N_HEADS].reshape(2, 2, C_GROUP), p[48, 0])


def kernel(x, w_in, conv_w, sinks, g_mix, g_group, w_o, g_mlp, w_ff_in, w_ff_out, g_final, loss_target, m_w_in, m_conv_w, m_sinks, m_g_mix, m_g_group, m_w_o, m_g_mlp, m_w_ff_in, m_w_ff_out, m_g_final, v_w_in, v_conv_w, v_sinks, v_g_mix, v_g_group, v_w_o, v_g_mlp, v_w_ff_in, v_w_ff_out, v_g_final):
    d = max(x.shape[2], MIX_WIDTH)
    chip = 2 * lax.axis_index("x") + lax.axis_index("y")
    conv_n = conv_w.shape[2]

    pos_arr = jnp.stack([chip, lax.axis_index("c")]).astype(jnp.int32)
    shards = (w_in, w_o, w_ff_in, w_ff_out)
    conv_tile = jnp.pad(conv_w.reshape(6, conv_n), ((0, HALO - 6), (0, 128 - conv_n)))
    placed = [_place_shard(w_in, pos_arr[:1], "place_shard_0"), None, None, None]
    sems_a, placed, conv_thru = _gather_start(
        GATHER_STARTS[0], placed, (conv_tile, lax.empty((N_CHIPS,) + conv_tile.shape, conv_tile.dtype)),
        "gather_start_0")
    for i in range(1, N_BIG):
        placed[i] = _place_shard(shards[i], pos_arr[:1], "place_shard_%d" % i)
    sems_b, placed, _ = _gather_start(GATHER_STARTS[1], placed, None, "gather_start_1")
    full = {"arrs": placed, "conv": None, "sems": list(sems_a[:2]) + list(sems_b)}

    def fetch(stage, layer, after):
        k = 2 * layer + stage
        sems = full["sems"][2 * k:2 * k + 2]
        if k == 0:
            full["arrs"], land = _gather_wait(0, sems, full["arrs"], (sems_a[-2:], *conv_thru), after, "gather_wait_0")
            conv_all = lax.dynamic_update_slice(land, conv_tile[None], (chip, 0, 0))
            full["conv"] = conv_all[:, :6, :conv_n].reshape(N_CHIPS, 2, 3, conv_n).transpose(1, 2, 0, 3).reshape(
                2, 3, CONV_CH)
        else:
            full["arrs"], _ = _gather_wait(k, sems, full["arrs"], None, after, "gather_wait_%d" % k)
        if k == 1:
            sems_c, full["arrs"], _ = _gather_start(GATHER_STARTS[2], full["arrs"], None, "gather_start_2")
            full["sems"] += list(sems_c)
        return (*full["arrs"], full["conv"])

    lands, started = [None] * N_BIG, {}

    def emit(layer, w, g):
        if lands[w] is None:
            lands[w] = lax.empty(_slot_shape(g, w), g.dtype)
        *started[layer, w], lands[w], token = _scatter_start(g, lands[w], layer, w, "scatter_start_%d_%d" % (layer, w))
        return token[0, 0]

    loss_tile, dx, grads, dg_final = _local_step(_to_strips(x[0]), _to_strips(loss_target[0]), fetch, w_ff_in.shape[2] * N_CHIPS,
                                                 sinks, g_mix, g_group, g_mlp, g_final, emit)

    own = []
    for w in range(N_BIG):
        g0, g1, slots = _scatter_wait((started[0, w], started[1, w]), lands[w], dx, w, "scatter_wait_%d" % w)
        own.append(_sum_slots(g0, g1, slots, w, pos_arr, "sum_slots_%d" % w))
    other = _swap_layers(own)

    def both(i):
        return jnp.stack([grads[0][i][0], grads[1][i][0]])
    dconv = jnp.stack([grads[0][0][:3], grads[1][0][:3]])
    dsinks = jnp.stack([grads[0][1][0, ::HEAD_DIM], grads[1][1][0, ::HEAD_DIM]])
    part = _pack_small(d, both(2), both(3), both(4), dg_final[0], dconv, dsinks, loss_tile[0, 0])

    def spread(shard):
        return lax.dynamic_update_slice(jnp.zeros((2, 3, CONV_CH), F32), shard, (0, 0, chip * conv_n))
    zero = jnp.zeros((), F32)
    packs = [_pack_small(d, a, b, c_, e, spread(f), g_, zero) for a, b, c_, e, f, g_ in (
        (g_mix, g_group, g_mlp, g_final, conv_w, sinks),
        (m_g_mix, m_g_group, m_g_mlp, m_g_final, m_conv_w, m_sinks),
        (v_g_mix, v_g_group, v_g_mlp, v_g_final, v_conv_w, v_sinks))]
    small = [_unpack_small(p, x.shape[2]) for p in _small_sync(part, *packs)]

    def shard_of(full):
        return lax.dynamic_slice(full, (0, 0, chip * conv_n), (2, 3, conv_n))
    small = [(s[0], s[1], s[2], s[3], shard_of(s[4]), s[5], s[6]) for s in small]
    loss = small[0][6]

    big = [_adamw(w, own[i], other[i], m, v, pos_arr, "adamw_%d" % i) for i, (w, m, v) in enumerate((
        (w_in, m_w_in, v_w_in), (w_o, m_w_o, v_w_o), (w_ff_in, m_w_ff_in, v_w_ff_in),
        (w_ff_out, m_w_ff_out, v_w_ff_out)))]

    def ordered(kind):
        b = [big[i][kind] for i in range(N_BIG)]
        s = small[kind]
        return [b[0], s[4], s[5], s[0], s[1], b[1], s[2], b[2], b[3], s[3]]

    return (loss, _from_strips(dx)[None], *ordered(0), *ordered(1), *ordered(2), *ordered(3))
```

```python
import functools

import jax
import jax.numpy as jnp
from jax import lax
from jax.experimental import pallas as pl
from jax.experimental.pallas import tpu as pltpu

HEAD_DIM = 64
N_HEADS = 6
C_GROUP = 3
A_WIDTH = N_HEADS * HEAD_DIM
C_KV_WIDTH = 2 * HEAD_DIM
CONV_CH = 256
ZA_W = 3 * A_WIDTH
ZB_W = 3 * CONV_CH
ZC_W = A_WIDTH + 2 * C_KV_WIDTH
IN_WIDTH = ZA_W + ZB_W + ZC_W
MIX_WIDTH = A_WIDTH + CONV_CH + A_WIDTH
DILATIONS = (1, 4, 16)
A_MAX_DIST = 128
C_MAX_DIST = 127
TQ = 128
EPS = 1e-6
SCALE = HEAD_DIM ** -0.5
NEG = -1e30
HALO = 8

ADAM_LR = 0.001
ADAM_B1 = 0.9
ADAM_B2 = 0.999
ADAM_EPS = 1e-08
ADAM_WD = 0.01
ADAM_STEP = 10

BF = jnp.bfloat16
F32 = jnp.float32
MESH = pl.DeviceIdType.MESH
VMEM_LIMIT = 56 * 1024 * 1024


def _cparams(*sem):
    return pltpu.CompilerParams(dimension_semantics=sem, vmem_limit_bytes=VMEM_LIMIT)


def _nt(a, b):
    return lax.dot_general(a, b, (((1,), (1,)), ((), ())), preferred_element_type=F32)


def _tn(a, b):
    return lax.dot_general(a, b, (((0,), (0,)), ((), ())), preferred_element_type=F32)


def _nn(a, b):
    return jnp.dot(a, b, preferred_element_type=F32)


def _rows(tb, w):
    return pl.BlockSpec((tb, w), lambda i: (i, 0))


def _whole(shape):
    return pl.BlockSpec(shape, lambda *_: (0,) * len(shape))


def _layer(shape, l):
    return pl.BlockSpec((None,) + shape, lambda *_: (l,) + (0,) * len(shape))


def _rms_scale(v):
    return lax.rsqrt(jnp.mean(v * v, axis=-1, keepdims=True) + EPS)


def _norm_bwd(dxhat, xhat, r):
    return r * (dxhat - xhat * jnp.mean(dxhat * xhat, axis=-1, keepdims=True))


def _qkv_fwd(x, g, w_all, l, tb):
    s, d = x.shape

    def body(x_ref, g_ref, w_ref, h_ref, za_ref, zb_ref, zc_ref):
        xv = x_ref[...]
        h = ((xv * _rms_scale(xv)) * g_ref[...]).astype(BF)
        h_ref[...] = h
        z = jnp.concatenate([_nn(h, w_ref[k]) for k in range(N_CHIPS)], axis=1)
        za_ref[...] = z[:, :ZA_W]
        zb_ref[...] = z[:, ZA_W:ZA_W + ZB_W]
        zc_ref[...] = z[:, ZA_W + ZB_W:]

    return pl.pallas_call(
        body, grid=(s // tb,), name="qkv_fwd",
        in_specs=[_rows(tb, d), _whole((1, d)), _layer((N_CHIPS, d, IN_WIDTH // N_CHIPS), l)],
        out_specs=[_rows(tb, d), _rows(tb, ZA_W), _rows(tb, ZB_W), _rows(tb, ZC_W)],
        out_shape=[jax.ShapeDtypeStruct((s, d), BF), jax.ShapeDtypeStruct((s, ZA_W), F32),
                   jax.ShapeDtypeStruct((s, ZB_W), F32), jax.ShapeDtypeStruct((s, ZC_W), F32)],
        compiler_params=_cparams("parallel"),
    )(x, g, w_all)


N_STRIPS = 16


def _strips(a):
    s, w = a.shape
    return a.reshape(4, 4, s // N_STRIPS, w)


def _p_grid(s, dil):
    na = s // N_STRIPS
    return {16: (4, 4, na // TQ), 4: (4, na // 32), 1: (na // 8,)}[dil]


def _p_spec(dil, cw, col, prev=False):
    def blk(j):
        return jnp.maximum(j - 1, 0) if prev else j
    if dil == 16:
        return pl.BlockSpec((None, None, TQ, cw), lambda f, e, j: (f, e, blk(j), col))
    if dil == 4:
        return pl.BlockSpec((None, 4, 32, cw), lambda f, j: (f, 0, blk(j), col))
    return pl.BlockSpec((4, 4, 8, cw), lambda j: (0, 0, blk(j), col))


def _block_pos(i, dil):
    if dil == 16:
        return i
    if dil == 4:
        return 4 * (i % 32) + i // 32
    return 16 * (i % 8) + 4 * ((i // 8) % 4) + i // 32


def _band_mask(b, dil, max_dist):
    qi = _block_pos(lax.broadcasted_iota(jnp.int32, (TQ, 2 * TQ), 0), dil)
    col = lax.broadcasted_iota(jnp.int32, (TQ, 2 * TQ), 1)
    cur = col >= TQ
    dist = qi - _block_pos(col % TQ, dil) + jnp.where(cur, 0, TQ)
    return (dist >= 0) & (dist <= max_dist) & (cur | (b > 0))


def _hs(h):
    return slice(h * HEAD_DIM, (h + 1) * HEAD_DIM)


def _ld(ref, cols):
    v = ref[..., cols]
    return v.reshape(TQ, v.shape[-1])


def _st(ref, cols, val):
    ref[..., cols] = val.reshape(ref.shape[:-1] + (val.shape[-1],))


def _attn_fwd(z, dil, kw, kcol, vcol, n_rep, max_dist, name):
    s, zw = z.shape
    grid = _p_grid(s, dil)

    def body(q_ref, kp_ref, kc_ref, vp_ref, vc_ref, acc_ref, m_ref, l_ref):
        mask = _band_mask(pl.program_id(len(grid) - 1), dil, max_dist)
        for kh in range(N_HEADS // n_rep):
            k2 = jnp.concatenate([_ld(kp_ref, _hs(kh)), _ld(kc_ref, _hs(kh))], axis=0).astype(BF)
            v2 = jnp.concatenate([_ld(vp_ref, _hs(kh)), _ld(vc_ref, _hs(kh))], axis=0).astype(BF)
            for h in range(kh * n_rep, (kh + 1) * n_rep):
                q = _ld(q_ref, _hs(h)).astype(BF)
                sc = jnp.where(mask, _nt(q, k2) * SCALE, NEG)
                m = jnp.max(sc, axis=1, keepdims=True)
                p = jnp.exp(sc - m)
                _st(acc_ref, _hs(h), _nn(p.astype(BF), v2))
                _st(m_ref, _hs(h), jnp.broadcast_to(m, (TQ, HEAD_DIM)))
                _st(l_ref, _hs(h), jnp.broadcast_to(jnp.sum(p, axis=1, keepdims=True), (TQ, HEAD_DIM)))

    res = pl.pallas_call(
        body, grid=grid, name=name,
        in_specs=[_p_spec(dil, A_WIDTH, 0), _p_spec(dil, kw, kcol, True), _p_spec(dil, kw, kcol),
                  _p_spec(dil, kw, vcol, True), _p_spec(dil, kw, vcol)],
        out_specs=[_p_spec(dil, A_WIDTH, 0)] * 3,
        out_shape=[jax.ShapeDtypeStruct((4, 4, s // N_STRIPS, A_WIDTH), F32)] * 3,
        compiler_params=_cparams(*(("parallel",) * len(grid))),
    )(*[_strips(z)] * 5)
    return [a.reshape(s, A_WIDTH) for a in res]


def _attn_merge(parts_a, part_c, sink_row, tb):
    s = part_c[0].shape[0]
    n_a = len(parts_a)

    def body(*refs):
        ins, sink_ref = refs[:3 * n_a + 3], refs[3 * n_a + 3]
        ya_ref, lsea_ref, yc_ref, lsec_ref = refs[3 * n_a + 4:]
        ms = [ins[3 * p + 1][...] for p in range(n_a)]
        m = functools.reduce(jnp.maximum, ms)
        acc = jnp.zeros_like(m)
        l = jnp.zeros_like(m)
        for p in range(n_a):
            w = jnp.exp(ms[p] - m)
            acc = acc + w * ins[3 * p][...]
            l = l + w * ins[3 * p + 2][...]
        ya_ref[...] = acc / l
        lsea_ref[...] = m + jnp.log(l)
        acc_c, m_c, l_c = [r[...] for r in ins[3 * n_a:]]
        sk = sink_ref[...]
        m2 = jnp.maximum(m_c, sk)
        w = jnp.exp(m_c - m2)
        l2 = w * l_c + jnp.exp(sk - m2)
        yc_ref[...] = (w * acc_c) / l2
        lsec_ref[...] = m2 + jnp.log(l2)

    return pl.pallas_call(
        body, grid=(s // tb,), name="attn_merge",
        in_specs=[_rows(tb, A_WIDTH)] * (3 * n_a + 3) + [_whole((1, A_WIDTH))],
        out_specs=[_rows(tb, A_WIDTH)] * 4, out_shape=[jax.ShapeDtypeStruct((s, A_WIDTH), F32)] * 4,
        compiler_params=_cparams("parallel"),
    )(*[a for part in parts_a + [part_c] for a in part], sink_row)


def _shift_down(v, n, halo):
    rows = v.shape[0]
    out = pltpu.roll(v, n, 0)
    row = lax.broadcasted_iota(jnp.int32, v.shape, 0)
    for t in range(n):
        out = jnp.where(row == t, halo[HALO - n + t:HALO - n + t + 1, :], out)
    return out


def _shift_up(v, n, halo):
    rows = v.shape[0]
    out = pltpu.roll(v, rows - n, 0)
    row = lax.broadcasted_iota(jnp.int32, v.shape, 0)
    for t in range(n):
        out = jnp.where(row == rows - n + t, halo[t:t + 1, :], out)
    return out


def _strip(v, b):
    return v[b % 4, b // 4]


def _conv_strips(zb, prev, cw):
    gb = [_strip(zb, b)[:, :CONV_CH] for b in range(N_STRIPS)]
    gc = [_strip(zb, b)[:, CONV_CH:2 * CONV_CH] for b in range(N_STRIPS)]
    xb = [_strip(zb, b)[:, 2 * CONV_CH:] for b in range(N_STRIPS)]
    u = [g * v for g, v in zip(gc, xb)]
    uh = prev[:, :, CONV_CH:2 * CONV_CH] * prev[:, :, 2 * CONV_CH:]
    wrapped = {14: _shift_down(u[14], 1, uh[2]), 15: _shift_down(u[15], 1, uh[3])}
    u1 = [u[b - 1] if b >= 1 else wrapped[15] for b in range(N_STRIPS)]
    u2 = [u[b - 2] if b >= 2 else wrapped[14 + b] for b in range(N_STRIPS)]
    c = [cw[0:1, :] * u2[b] + cw[1:2, :] * u1[b] + cw[2:3, :] * u[b] for b in range(N_STRIPS)]
    return gb, gc, xb, u, u1, u2, c


def _strip_rows(ta, w):
    return pl.BlockSpec((4, 4, ta, w), lambda i: (0, 0, i, 0))


def _prev_rows(ta, w):
    return pl.BlockSpec((4, None, HALO, w), lambda i: (0, 3, jnp.maximum(i * (ta // HALO) - 1, 0), 0))


def _next_rows(ta, w, nblk):
    return pl.BlockSpec((4, None, HALO, w),
                        lambda i: (0, 0, jnp.minimum((i + 1) * (ta // HALO), nblk * (ta // HALO) - 1), 0))


def _mix_fwd(x, ya, yc, zb, cw, gg, wo_all, l, tb):
    s, d = x.shape
    ta = tb // N_STRIPS

    def body(x_ref, ya_ref, yc_ref, zb_ref, zbp_ref, cw_ref, gg_ref, wo_ref, x1_ref, yb_ref):
        i = pl.program_id(0)
        prev = jnp.where(i > 0, zbp_ref[...], 0.0)
        gb, _, _, _, _, _, c = _conv_strips(zb_ref[...], prev, cw_ref[...])
        for b in range(N_STRIPS):
            yb_ref[b % 4, b // 4] = gb[b] * c[b]
        yb = yb_ref[...].reshape(tb, CONV_CH)
        ya, yc = ya_ref[...].reshape(tb, A_WIDTH), yc_ref[...].reshape(tb, A_WIDTH)
        n = jnp.concatenate([ya * _rms_scale(ya), yb * _rms_scale(yb), yc * _rms_scale(yc)], axis=1)
        n = (n * gg_ref[...]).astype(BF)
        x1 = x_ref[...].reshape(tb, d) + _nn(n, wo_ref[...].reshape(MIX_WIDTH, d))
        x1_ref[...] = x1.reshape(4, 4, ta, d)

    res = pl.pallas_call(
        body, grid=(s // tb,), name="mix_fwd",
        in_specs=[_strip_rows(ta, d), _strip_rows(ta, A_WIDTH), _strip_rows(ta, A_WIDTH), _strip_rows(ta, ZB_W),
                  _prev_rows(ta, ZB_W), _whole((HALO, CONV_CH)), _whole((1, MIX_WIDTH)),
                  _layer((N_CHIPS, MIX_WIDTH // N_CHIPS, d), l)],
        out_specs=[_strip_rows(ta, d), _strip_rows(ta, CONV_CH)],
        out_shape=[jax.ShapeDtypeStruct((4, 4, s // N_STRIPS, d), F32),
                   jax.ShapeDtypeStruct((4, 4, s // N_STRIPS, CONV_CH), F32)],
        compiler_params=_cparams("parallel"),
    )(_strips(x), _strips(ya), _strips(yc), _strips(zb), _strips(zb), cw, gg, wo_all)
    return res[0].reshape(s, d), res[1].reshape(s, CONV_CH)


def _mlp_fwd(x1, g, w1_all, w2_all, l, tb, tf):
    s, d = x1.shape
    ff = w1_all.shape[1] * w1_all.shape[3]
    nj = ff // tf

    def body(x_ref, g_ref, w1_ref, w2_ref, x2_ref, h2_ref, ap_ref, acc):
        j = pl.program_id(1)

        @pl.when(j == 0)
        def _():
            xv = x_ref[...]
            h2_ref[...] = ((xv * _rms_scale(xv)) * g_ref[...]).astype(BF)
            acc[...] = jnp.zeros_like(acc)

        ap = _nn(h2_ref[...], w1_ref[...])
        ap_ref[...] = ap.astype(BF)
        a = jnp.square(jnp.maximum(ap, 0.0)).astype(BF)
        acc[...] += _nn(a, w2_ref[...])

        @pl.when(j == nj - 1)
        def _():
            x2_ref[...] = x_ref[...] + acc[...]

    return pl.pallas_call(
        body, grid=(s // tb, nj), name="mlp_fwd",
        in_specs=[pl.BlockSpec((tb, d), lambda i, j: (i, 0)), _whole((1, d)),
                  pl.BlockSpec((None, None, d, tf), lambda i, j: (l, j, 0, 0)),
                  pl.BlockSpec((None, None, tf, d), lambda i, j: (l, j, 0, 0))],
        out_specs=[pl.BlockSpec((tb, d), lambda i, j: (i, 0)), pl.BlockSpec((tb, d), lambda i, j: (i, 0)),
                   pl.BlockSpec((tb, tf), lambda i, j: (i, j))],
        out_shape=[jax.ShapeDtypeStruct((s, d), F32), jax.ShapeDtypeStruct((s, d), BF),
                   jax.ShapeDtypeStruct((s, ff), BF)],
        scratch_shapes=[pltpu.VMEM((tb, d), F32)],
        compiler_params=_cparams("parallel", "arbitrary"),
    )(x1, g, w1_all, w2_all)


def _loss_head(x, g, tgt, tb):
    s, d = x.shape

    def body(x_ref, g_ref, t_ref, dx_ref, loss_ref, dg_ref):
        i = pl.program_id(0)

        @pl.when(i == 0)
        def _():
            loss_ref[...] = jnp.zeros_like(loss_ref)
            dg_ref[...] = jnp.zeros_like(dg_ref)

        xv = x_ref[...]
        r = _rms_scale(xv)
        xhat = xv * r
        err = xhat * g_ref[...] - t_ref[...]
        part = jnp.sum(jnp.mean(jnp.square(err), axis=-1, keepdims=True), axis=0, keepdims=True)
        loss_ref[...] += 0.5 * part
        dy = err * (1.0 / d)
        dg_ref[...] += jnp.sum(dy * xhat, axis=0, keepdims=True)
        dx_ref[...] = _norm_bwd(dy * g_ref[...], xhat, r)

    return pl.pallas_call(
        body, grid=(s // tb,), name="loss_head",
        in_specs=[_rows(tb, d), _whole((1, d)), _rows(tb, d)],
        out_specs=[_rows(tb, d), _whole((HALO, 128)), _whole((HALO, d))],
        out_shape=[jax.ShapeDtypeStruct((s, d), F32), jax.ShapeDtypeStruct((HALO, 128), F32),
                   jax.ShapeDtypeStruct((HALO, d), F32)],
        compiler_params=_cparams("arbitrary"),
    )(x, g, tgt)


def _mlp_bwd(dx2, x1, ap, g, w1_all, w2_all, l, tb, tf):
    s, d = x1.shape
    ff = ap.shape[1]
    nj = ff // tf

    def body(dx2_ref, x1_ref, ap_ref, g_ref, w1_ref, w2_ref, dx1_ref, dap_ref, dg_ref, acc):
        i, j = pl.program_id(0), pl.program_id(1)

        @pl.when((i == 0) & (j == 0))
        def _():
            dg_ref[...] = jnp.zeros_like(dg_ref)

        @pl.when(j == 0)
        def _():
            acc[...] = jnp.zeros_like(acc)

        da = _nt(dx2_ref[...].astype(BF), w2_ref[...])
        dap = (da * (2.0 * jnp.maximum(ap_ref[...].astype(F32), 0.0))).astype(BF)
        dap_ref[...] = dap
        acc[...] += _nt(dap, w1_ref[...])

        @pl.when(j == nj - 1)
        def _():
            xv = x1_ref[...]
            r = _rms_scale(xv)
            xhat = xv * r
            dh = acc[...]
            dg_ref[...] += jnp.sum(dh * xhat, axis=0, keepdims=True)
            dx1_ref[...] = dx2_ref[...] + _norm_bwd(dh * g_ref[...], xhat, r)

    return pl.pallas_call(
        body, grid=(s // tb, nj), name="mlp_bwd",
        in_specs=[pl.BlockSpec((tb, d), lambda i, j: (i, 0)), pl.BlockSpec((tb, d), lambda i, j: (i, 0)),
                  pl.BlockSpec((tb, tf), lambda i, j: (i, j)),
                  _whole((1, d)), pl.BlockSpec((None, None, d, tf), lambda i, j: (l, j, 0, 0)),
                  pl.BlockSpec((None, None, tf, d), lambda i, j: (l, j, 0, 0))],
        out_specs=[pl.BlockSpec((tb, d), lambda i, j: (i, 0)), pl.BlockSpec((tb, tf), lambda i, j: (i, j)),
                   _whole((HALO, d))],
        out_shape=[jax.ShapeDtypeStruct((s, d), F32), jax.ShapeDtypeStruct((s, ff), BF),
                   jax.ShapeDtypeStruct((HALO, d), F32)],
        scratch_shapes=[pltpu.VMEM((tb, d), F32)],
        compiler_params=_cparams("arbitrary", "arbitrary"),
    )(dx2, x1, ap, g, w1_all, w2_all)


def _wgrad(a, b, tm, tn, ts, name, relu2=False):
    s, m = a.shape
    n = b.shape[1]
    ns = s // ts

    def body(a_ref, b_ref, o_ref, acc):
        k = pl.program_id(2)

        @pl.when(k == 0)
        def _():
            acc[...] = jnp.zeros_like(acc)

        av = a_ref[...]
        if relu2:
            av = jnp.square(jnp.maximum(av.astype(F32), 0.0)).astype(BF)
        acc[...] += _tn(av, b_ref[...].astype(BF))

        @pl.when(k == ns - 1)
        def _():
            o_ref[...] = acc[...].astype(BF)

    return pl.pallas_call(
        body, grid=(m // tm, n // tn, ns), name=name,
        in_specs=[pl.BlockSpec((ts, tm), lambda i, j, k: (k, i)), pl.BlockSpec((ts, tn), lambda i, j, k: (k, j))],
        out_specs=pl.BlockSpec((tm, tn), lambda i, j, k: (i, j)),
        out_shape=jax.ShapeDtypeStruct((m, n), BF),
        scratch_shapes=[pltpu.VMEM((tm, tn), F32)],
        compiler_params=_cparams("parallel", "parallel", "arbitrary"),
    )(a, b)


def _mix_bwd(dx1, ya, yb, yc, lse_c, sink_row, gg, wo_all, l, tb):
    s, d = dx1.shape

    def body(dx_ref, ya_ref, yb_ref, yc_ref, lse_ref, sink_ref, gg_ref, wo_ref,
             n_ref, dya_ref, dyc_ref, da_ref, dc_ref, dyb_ref, dg_ref, dsink_ref):
        i = pl.program_id(0)

        @pl.when(i == 0)
        def _():
            dg_ref[...] = jnp.zeros_like(dg_ref)
            dsink_ref[...] = jnp.zeros_like(dsink_ref)

        dn = _nt(dx_ref[...].astype(BF), wo_ref[...].reshape(MIX_WIDTH, d))
        ys = [ya_ref[...], yb_ref[...], yc_ref[...]]
        rs = [_rms_scale(v) for v in ys]
        nhat = jnp.concatenate([v * r for v, r in zip(ys, rs)], axis=1)
        gg = gg_ref[...]
        n_ref[...] = (nhat * gg).astype(BF)
        dg_ref[...] += jnp.sum(dn * nhat, axis=0, keepdims=True)
        dnh = dn * gg
        bounds = [(0, A_WIDTH), (A_WIDTH, A_WIDTH + CONV_CH), (A_WIDTH + CONV_CH, MIX_WIDTH)]
        dys = [_norm_bwd(dnh[:, lo:hi], nhat[:, lo:hi], r) for (lo, hi), r in zip(bounds, rs)]
        dyb_ref[...] = dys[1]
        for dy, y, dy_ref, dd_ref in ((dys[0], ys[0], dya_ref, da_ref), (dys[2], ys[2], dyc_ref, dc_ref)):
            dy_ref[...] = dy
            t = dy * y
            for h in range(N_HEADS):
                dd_ref[:, _hs(h)] = jnp.broadcast_to(jnp.sum(t[:, _hs(h)], axis=1, keepdims=True), (tb, HEAD_DIM))
        dsink_ref[...] -= jnp.sum(jnp.exp(sink_ref[...] - lse_ref[...]) * dc_ref[...], axis=0, keepdims=True)

    return pl.pallas_call(
        body, grid=(s // tb,), name="mix_bwd",
        in_specs=[_rows(tb, d), _rows(tb, A_WIDTH), _rows(tb, CONV_CH), _rows(tb, A_WIDTH), _rows(tb, A_WIDTH),
                  _whole((1, A_WIDTH)), _whole((1, MIX_WIDTH)), _layer((N_CHIPS, MIX_WIDTH // N_CHIPS, d), l)],
        out_specs=[_rows(tb, MIX_WIDTH), _rows(tb, A_WIDTH), _rows(tb, A_WIDTH), _rows(tb, A_WIDTH),
                   _rows(tb, A_WIDTH), _rows(tb, CONV_CH), _whole((HALO, MIX_WIDTH)), _whole((HALO, A_WIDTH))],
        out_shape=[jax.ShapeDtypeStruct((s, MIX_WIDTH), BF), jax.ShapeDtypeStruct((s, A_WIDTH), F32),
                   jax.ShapeDtypeStruct((s, A_WIDTH), F32), jax.ShapeDtypeStruct((s, A_WIDTH), F32),
                   jax.ShapeDtypeStruct((s, A_WIDTH), F32), jax.ShapeDtypeStruct((s, CONV_CH), F32),
                   jax.ShapeDtypeStruct((HALO, MIX_WIDTH), F32), jax.ShapeDtypeStruct((HALO, A_WIDTH), F32)],
        compiler_params=_cparams("arbitrary"),
    )(dx1, ya, yb, yc, lse_c, sink_row, gg, wo_all)


def _attn_bwd(z, dy, lse, dd, dil, kw, kcol, vcol, n_rep, max_dist, name):
    s, zw = z.shape
    grid = _p_grid(s, dil)
    n_kv = N_HEADS // n_rep

    def body(q_ref, kp_ref, kc_ref, vp_ref, vc_ref, dy_ref, lse_ref, dd_ref, dq_ref, dkp_ref, dkc_ref, dvp_ref, dvc_ref):
        mask = _band_mask(pl.program_id(len(grid) - 1), dil, max_dist)
        for kh in range(n_kv):
            k2 = jnp.concatenate([_ld(kp_ref, _hs(kh)), _ld(kc_ref, _hs(kh))], axis=0).astype(BF)
            v2 = jnp.concatenate([_ld(vp_ref, _hs(kh)), _ld(vc_ref, _hs(kh))], axis=0).astype(BF)
            dk2 = jnp.zeros((2 * TQ, HEAD_DIM), F32)
            dv2 = jnp.zeros((2 * TQ, HEAD_DIM), F32)
            for h in range(kh * n_rep, (kh + 1) * n_rep):
                q = _ld(q_ref, _hs(h)).astype(BF)
                lse_h = _ld(lse_ref, slice(h * HEAD_DIM, h * HEAD_DIM + 1))
                dd_h = _ld(dd_ref, slice(h * HEAD_DIM, h * HEAD_DIM + 1))
                dyh = _ld(dy_ref, _hs(h)).astype(BF)
                sc = jnp.where(mask, _nt(q, k2) * SCALE, NEG)
                p = jnp.exp(sc - lse_h)
                dp = _nt(dyh, v2)
                ds = ((p * (dp - dd_h)) * SCALE).astype(BF)
                _st(dq_ref, _hs(h), _nn(ds, k2))
                dk2 = dk2 + _tn(ds, q)
                dv2 = dv2 + _tn(p.astype(BF), dyh)
            _st(dkp_ref, _hs(kh), dk2[:TQ])
            _st(dkc_ref, _hs(kh), dk2[TQ:])
            _st(dvp_ref, _hs(kh), dv2[:TQ])
            _st(dvc_ref, _hs(kh), dv2[TQ:])

    args = [_strips(z)] * 5 + [_strips(a) for a in (dy, lse, dd)]
    in_specs = [_p_spec(dil, A_WIDTH, 0), _p_spec(dil, kw, kcol, True), _p_spec(dil, kw, kcol),
                _p_spec(dil, kw, vcol, True), _p_spec(dil, kw, vcol)] + [_p_spec(dil, A_WIDTH, 0)] * 3
    out_specs = [_p_spec(dil, A_WIDTH, 0)] + [_p_spec(dil, kw, 0)] * 4
    na = s // N_STRIPS
    out_shape = [jax.ShapeDtypeStruct((4, 4, na, A_WIDTH), F32)] + [jax.ShapeDtypeStruct((4, 4, na, kw), F32)] * 4
    res = pl.pallas_call(
        body, grid=grid, name=name, in_specs=in_specs, out_specs=out_specs, out_shape=out_shape,
        compiler_params=_cparams(*(("parallel",) * len(grid))),
    )(*args)
    return [res[0].reshape(s, A_WIDTH)] + [a.reshape(s, kw) for a in res[1:]]


DZ_TA = 16


def _dz_assemble(parts_a, parts_c, dyb, zb, cw):
    s = zb.shape[0]
    na = s // N_STRIPS
    nb = na // DZ_TA

    def ahead(w, k):
        return pl.BlockSpec((4, 4, DZ_TA, w), lambda i: (0, 0, jnp.minimum(i + k, nb - 1), 0))

    args, in_specs = [], []
    for dil, (dq, dkp, dkc, dvp, dvc) in zip(DILATIONS + (1,), parts_a + [parts_c]):
        w = dkp.shape[1]
        here = _strip_rows(DZ_TA, w)
        if dil == 1:
            args += [dq, dkp, dkp, dkc, dvp, dvp, dvc]
            in_specs += [_strip_rows(DZ_TA, A_WIDTH), here, ahead(w, 1), here, here, ahead(w, 1), here]
        else:
            k = 8 * dil // DZ_TA
            args += [dq, dkp, dkc, dvp, dvc]
            in_specs += [_strip_rows(DZ_TA, A_WIDTH), ahead(w, k), here, ahead(w, k), here]
    n_att = len(args)
    args = [_strips(a) for a in args] + [_strips(dyb), _strips(dyb), _strips(zb), _strips(zb), _strips(zb), cw]
    in_specs += [_strip_rows(DZ_TA, CONV_CH), _next_rows(DZ_TA, CONV_CH, nb), _strip_rows(DZ_TA, ZB_W),
                 _prev_rows(DZ_TA, ZB_W), _next_rows(DZ_TA, ZB_W, nb), _whole((HALO, CONV_CH))]

    def body(*refs):
        att = list(refs[:n_att])
        dyb_ref, dybn_ref, zb_ref, zbp_ref, zbn_ref, cw_ref, dz_ref, dcw_ref = refs[n_att:]
        i = pl.program_id(0)

        @pl.when(i == 0)
        def _():
            dcw_ref[...] = jnp.zeros_like(dcw_ref)

        def shifted(dil):
            if dil == 1:
                dq_r, kp0, kp1, dkc_r, vp0, vp1, dvc_r = [att.pop(0) for _ in range(7)]
                live = i + 1 < nb
                half = DZ_TA // 2
                dkp = jnp.concatenate([kp0[:, :, half:, :], jnp.where(live, kp1[:, :, :half, :], 0.0)], axis=2)
                dvp = jnp.concatenate([vp0[:, :, half:, :], jnp.where(live, vp1[:, :, :half, :], 0.0)], axis=2)
            else:
                dq_r, dkp_r, dkc_r, dvp_r, dvc_r = [att.pop(0) for _ in range(5)]
                live = i + 8 * dil // DZ_TA < nb
                dkp, dvp = jnp.where(live, dkp_r[...], 0.0), jnp.where(live, dvp_r[...], 0.0)
            return dq_r[...], dkc_r[...] + dkp, dvc_r[...] + dvp

        dq, dk, dv = shifted(DILATIONS[0])
        for dil in DILATIONS[1:]:
            dq2, dk2, dv2 = shifted(dil)
            dq, dk, dv = dq + dq2, dk + dk2, dv + dv2
        dz_ref[:, :, :, 0:A_WIDTH] = dq.astype(BF)
        dz_ref[:, :, :, A_WIDTH:2 * A_WIDTH] = dk.astype(BF)
        dz_ref[:, :, :, 2 * A_WIDTH:ZA_W] = dv.astype(BF)
        dq, dk, dv = shifted(1)
        c0 = ZA_W + ZB_W
        dz_ref[:, :, :, c0:c0 + A_WIDTH] = dq.astype(BF)
        dz_ref[:, :, :, c0 + A_WIDTH:c0 + A_WIDTH + C_KV_WIDTH] = dk.astype(BF)
        dz_ref[:, :, :, c0 + A_WIDTH + C_KV_WIDTH:IN_WIDTH] = dv.astype(BF)

        cw = cw_ref[...]
        prev = jnp.where(i > 0, zbp_ref[...], 0.0)
        gb, gc, xb, u, u1, u2, c = _conv_strips(zb_ref[...], prev, cw)
        dyb = dyb_ref[...]
        dc = [_strip(dyb, b) * gb[b] for b in range(N_STRIPS)]
        dcn = jnp.where(i + 1 < nb, dybn_ref[...] * zbn_ref[:, :, :CONV_CH], 0.0)
        wrapped = [_shift_up(dc[0], 1, dcn[0]), _shift_up(dc[1], 1, dcn[1])]
        upd = [jnp.zeros((1, CONV_CH), F32)] * 3
        for b in range(N_STRIPS):
            dc1 = dc[b + 1] if b + 1 < N_STRIPS else wrapped[0]
            dc2 = dc[b + 2] if b + 2 < N_STRIPS else wrapped[b + 2 - N_STRIPS]
            du = cw[2:3, :] * dc[b] + cw[1:2, :] * dc1 + cw[0:1, :] * dc2
            f, e = b % 4, b // 4
            dz_ref[f, e, :, ZA_W:ZA_W + CONV_CH] = (_strip(dyb, b) * c[b]).astype(BF)
            dz_ref[f, e, :, ZA_W + CONV_CH:ZA_W + 2 * CONV_CH] = (du * xb[b]).astype(BF)
            dz_ref[f, e, :, ZA_W + 2 * CONV_CH:c0] = (du * gc[b]).astype(BF)
            for t, uu in enumerate((u2[b], u1[b], u[b])):
                upd[t] = upd[t] + jnp.sum(dc[b] * uu, axis=0, keepdims=True)
        row = lax.broadcasted_iota(jnp.int32, (HALO, CONV_CH), 0)
        tile = jnp.zeros((HALO, CONV_CH), F32)
        for t in range(3):
            tile = jnp.where(row == t, upd[t], tile)
        dcw_ref[...] += tile

    dz, dcw = pl.pallas_call(
        body, grid=(nb,), name="dz_assemble", in_specs=in_specs,
        out_specs=[_strip_rows(DZ_TA, IN_WIDTH), _whole((HALO, CONV_CH))],
        out_shape=[jax.ShapeDtypeStruct((4, 4, na, IN_WIDTH), BF), jax.ShapeDtypeStruct((HALO, CONV_CH), F32)],
        compiler_params=_cparams("arbitrary"),
    )(*args)
    return dz.reshape(s, IN_WIDTH), dcw


def _qkv_bwd(dz, dx1, x, g, w_all, l, tb):
    s, d = x.shape

    def body(dz_ref, dx1_ref, x_ref, g_ref, w_ref, dx_ref, dg_ref):
        i = pl.program_id(0)

        @pl.when(i == 0)
        def _():
            dg_ref[...] = jnp.zeros_like(dg_ref)

        n = IN_WIDTH // N_CHIPS
        dh = _nt(dz_ref[:, 0:n], w_ref[0])
        for k in range(1, N_CHIPS):
            dh = dh + _nt(dz_ref[:, k * n:(k + 1) * n], w_ref[k])
        xv = x_ref[...]
        r = _rms_scale(xv)
        xhat = xv * r
        dg_ref[...] += jnp.sum(dh * xhat, axis=0, keepdims=True)
        dx_ref[...] = dx1_ref[...] + _norm_bwd(dh * g_ref[...], xhat, r)

    return pl.pallas_call(
        body, grid=(s // tb,), name="qkv_bwd",
        in_specs=[_rows(tb, IN_WIDTH), _rows(tb, d), _rows(tb, d), _whole((1, d)),
                  _layer((N_CHIPS, d, IN_WIDTH // N_CHIPS), l)],
        out_specs=[_rows(tb, d), _whole((HALO, d))],
        out_shape=[jax.ShapeDtypeStruct((s, d), F32), jax.ShapeDtypeStruct((HALO, d), F32)],
        compiler_params=_cparams("arbitrary"),
    )(dz, dx1, x, g, w_all)


def _tile_rows(rows):
    return jnp.pad(rows, ((0, HALO - rows.shape[0]), (0, 0)))


def _to_strips(a):
    s, d = a.shape
    return a.reshape(s // N_STRIPS, 4, 4, d).transpose(2, 1, 0, 3).reshape(s, d)


def _from_strips(a):
    s, d = a.shape
    return a.reshape(4, 4, s // N_STRIPS, d).transpose(2, 1, 0, 3).reshape(s, d)


def _local_step(x, tgt, fetch, ff, sinks, g_mix, g_group, g_mlp, g_final, emit):
    s, d = x.shape
    depth = g_mix.shape[0]
    tb = min(512, s)
    tf = ff // N_CHIPS
    ts = min(1024, s)
    saved = []
    for l in range(depth):
        w_in, _, _, _, conv_w = fetch(0, l, x)
        cw = _tile_rows(conv_w[l])
        sk = jnp.repeat(sinks[l].reshape(N_HEADS), HEAD_DIM)[None]
        h, za, zb, zc = _qkv_fwd(x, g_mix[l][None], w_in, l, tb)
        parts_a = [_attn_fwd(za, dil, A_WIDTH, 1, 2, 1, A_MAX_DIST, "attn_a_fwd_%d" % dil) for dil in DILATIONS]
        part_c = _attn_fwd(zc, 1, C_KV_WIDTH, 3, 4, C_GROUP, C_MAX_DIST, "attn_c_fwd")
        ya, lse_a, yc, lse_c = _attn_merge(parts_a, part_c, sk, tb)
        w_in, w_o, w1, w2, _ = fetch(1, l, yc)
        x1, yb = _mix_fwd(x, ya, yc, zb, cw, g_group[l][None], w_o, l, tb)
        x2, h2, ap = _mlp_fwd(x1, g_mlp[l][None], w1, w2, l, ts, tf)
        saved.append((x, h, za, zb, zc, ya, lse_a, yc, lse_c, yb, x1, h2, ap, cw, sk))
        x = x2
    dx, loss_tile, dg_final = _loss_head(x, g_final[None], tgt, tb)
    grads = [None] * depth
    tok = jnp.zeros((), F32)
    for l in reversed(range(depth)):
        x0, h, za, zb, zc, ya, lse_a, yc, lse_c, yb, x1, h2, ap, cw, sk = saved[l]
        dx1, dap, dg_mlp = _mlp_bwd(dx, x1, ap, g_mlp[l][None] + tok, w1, w2, l, ts, tf)
        tok = emit(l, 3, _wgrad(ap, dx, min(1024, ff), d, ts, "wgrad_ff_out", relu2=True))
        tok = tok + emit(l, 2, _wgrad(h2, dap, d, min(1024, ff), ts, "wgrad_ff_in"))
        n, dya, dyc, dd_a, dd_c, dyb, dg_group, dsink = _mix_bwd(dx1, ya, yb, yc, lse_c, sk, g_group[l][None] + tok,
                                                                 w_o, l, tb)
        tok = emit(l, 1, _wgrad(n, dx1, MIX_WIDTH, d, ts, "wgrad_o"))
        cw = cw + tok
        parts_a = [_attn_bwd(za, dya, lse_a, dd_a, dil, A_WIDTH, 1, 2, 1, A_MAX_DIST, "attn_a_bwd_%d" % dil)
                   for dil in DILATIONS]
        parts_c = _attn_bwd(zc, dyc, lse_c, dd_c, 1, C_KV_WIDTH, 3, 4, C_GROUP, C_MAX_DIST, "attn_c_bwd")
        dz, dcw = _dz_assemble(parts_a, parts_c, dyb, zb, cw)
        dx, dg_mix = _qkv_bwd(dz, dx1, x0, g_mix[l][None], w_in, l, tb)
        tok = emit(l, 0, _wgrad(h, dz, d, IN_WIDTH // 4, ts, "wgrad_in"))
        grads[l] = (dcw, dsink, dg_mix, dg_group, dg_mlp)
    return loss_tile, dx, grads, dg_final


ANY = pl.BlockSpec(memory_space=pl.ANY)
SHARD_AXES = (2, 1, 2, 1)
N_BIG = len(SHARD_AXES)
N_CHIPS = 4
N_DEV = 8


def _mesh_pos():
    return lax.axis_index("x"), lax.axis_index("y"), lax.axis_index("c")


def _flip(v, bit):
    return 1 - v if bit else v


def _place_shard(shard, chip_arr, name):
    _, rows, cols = shard.shape
    tr = min(256, rows)

    def body(chip_ref, x_ref, o_ref):
        o_ref[...] = x_ref[...].astype(BF)

    return pl.pallas_call(
        body, name=name,
        grid_spec=pltpu.PrefetchScalarGridSpec(
            num_scalar_prefetch=1, grid=(2, rows // tr),
            in_specs=[pl.BlockSpec((None, tr, cols), lambda l, i, chip: (l, i, 0))],
            out_specs=pl.BlockSpec((None, None, tr, cols), lambda l, i, chip: (l, chip[0], i, 0))),
        out_shape=jax.ShapeDtypeStruct((2, N_CHIPS, rows, cols), BF),
        compiler_params=_cparams("parallel", "parallel"),
    )(chip_arr, shard)


HBM = pl.BlockSpec(memory_space=pltpu.HBM)
SEM = pl.BlockSpec(memory_space=pltpu.SEMAPHORE)
EFFECT = pltpu.SideEffectType.DATAFLOW_SIDE_EFFECTING

GATHER_GROUPS = (((0, 0),), ((1, 0), (2, 0), (3, 0)), ((0, 1),), ((1, 1), (2, 1), (3, 1)))
GATHER_STARTS = ((0,), (1,), (2, 3))


def _gather_copies(arrs, group, send_sems, recv_sems):
    x, y, c = _mesh_pos()
    me = 2 * x + y
    out = []
    for i, (w, layer) in enumerate(group):
        mine = arrs[w].at[layer, me]
        for j, (qx, qy) in enumerate([(1 - x, y), (x, 1 - y), (1 - x, 1 - y)]):
            landed = arrs[w].at[layer, 2 * qx + qy]
            out.append(tuple(pltpu.make_async_remote_copy(
                src_ref=piece, dst_ref=piece, send_sem=send_sems.at[i * 3 + j], recv_sem=recv_sems.at[i * 3 + j],
                device_id=(qx, qy, c), device_id_type=MESH) for piece in (mine, landed)))
    return out


def _conv_copies(conv_src, conv_dst, send_sems, recv_sems):
    x, y, c = _mesh_pos()
    out = []
    for j, (qx, qy) in enumerate([(1 - x, y), (x, 1 - y), (1 - x, 1 - y)]):
        out.append(tuple(pltpu.make_async_remote_copy(
            src_ref=conv_src, dst_ref=conv_dst.at[q], send_sem=send_sems.at[j], recv_sem=recv_sems.at[j],
            device_id=(qx, qy, c), device_id_type=MESH) for q in (2 * x + y, 2 * qx + qy)))
    return out


def _gather_start(groups, arrs, conv, name):
    n_sems = 2 * (len(groups) + (conv is not None))
    mats = sorted({w for g in groups for w, _ in GATHER_GROUPS[g]})

    def body(*refs):
        arrs_ref = [None] * N_BIG
        for w, ref in zip(mats, refs):
            arrs_ref[w] = ref
        sems = refs[n_op:n_op + n_sems]
        if conv is not None:
            for cp, _ in _conv_copies(refs[len(mats)], refs[len(mats) + 1], sems[-2], sems[-1]):
                cp.start()
        for k, g in enumerate(groups):
            for cp, _ in _gather_copies(arrs_ref, GATHER_GROUPS[g], sems[2 * k], sems[2 * k + 1]):
                cp.start()

    sem_shapes = []
    for n in [len(GATHER_GROUPS[g]) for g in groups] + ([1] if conv is not None else []):
        sem_shapes += [pltpu.SemaphoreType.DMA((3 * n,))] * 2
    operands = [arrs[w] for w in mats] + ([] if conv is None else list(conv))
    n_op = len(operands)
    res = pl.pallas_call(
        body, name=name,
        out_shape=tuple(sem_shapes) + tuple(pltpu.HBM(a.shape, a.dtype) for a in operands),
        in_specs=(HBM,) * n_op, out_specs=(SEM,) * n_sems + (HBM,) * n_op,
        input_output_aliases={i: n_sems + i for i in range(n_op)},
        compiler_params=pltpu.CompilerParams(has_side_effects=EFFECT),
    )(*[pltpu.with_memory_space_constraint(a, pltpu.HBM) for a in operands])
    arrs = list(arrs)
    for w, a in zip(mats, res[n_sems:]):
        arrs[w] = a
    return res[:n_sems], arrs, list(res[n_sems + len(mats):])


def _gather_wait(k, sems, arrs, conv, after, name):
    group = GATHER_GROUPS[k]
    mats = sorted({w for w, _ in group})
    n_conv = 0 if conv is None else 2

    def body(*refs):
        local = refs[:len(mats)]
        arrs_ref = [None] * N_BIG
        for w, ref in zip(mats, local):
            arrs_ref[w] = ref
        pos = len(mats) + n_conv
        copies = _gather_copies(arrs_ref, group, refs[pos], refs[pos + 1])
        if conv is not None:
            copies += _conv_copies(refs[len(mats)], refs[len(mats) + 1], refs[pos + 2], refs[pos + 3])
        for send, recv in copies:
            recv.wait_recv()
            send.wait_send()

    operands = [arrs[w] for w in mats] + ([] if conv is None else [conv[1], conv[2]])
    sem_ops = list(sems) + ([] if conv is None else list(conv[0]))
    n_op = len(operands)
    res = pl.pallas_call(
        body, name=name, out_shape=tuple(pltpu.HBM(a.shape, a.dtype) for a in operands),
        in_specs=(HBM,) * n_op + (SEM,) * len(sem_ops) + (ANY,), out_specs=(HBM,) * n_op,
        input_output_aliases={i: i for i in range(n_op)},
        compiler_params=pltpu.CompilerParams(has_side_effects=EFFECT),
    )(*operands, *sem_ops, after)
    arrs = list(arrs)
    for w, a in zip(mats, res):
        arrs[w] = a
    return arrs, (res[-1] if conv is not None else None)


def _grad_shard(ref, w, chip, n):
    start = pl.multiple_of(chip * n, 128)
    if SHARD_AXES[w] == 2:
        return ref.at[:, pl.ds(start, n)]
    return ref.at[pl.ds(start, n), :]


def _slot_shape(g, w):
    shape = list(g.shape)
    shape[SHARD_AXES[w] - 1] //= N_CHIPS
    return (N_DEV - 1,) + tuple(shape)


def _scatter_copies(g_ref, land_ref, send_sems, recv_sems, layer, w):
    x, y, c = _mesh_pos()
    n = g_ref.shape[SHARD_AXES[w] - 1] // N_CHIPS
    out = []
    for r in range(1, N_DEV):
        tx, ty, tc = _flip(x, r & 4), _flip(y, r & 2), _flip(c, r & 1)
        cp = pltpu.make_async_remote_copy(
            src_ref=_grad_shard(g_ref, w, 2 * tx + ty, n), dst_ref=land_ref.at[r - 1], send_sem=send_sems.at[r - 1],
            recv_sem=recv_sems.at[r - 1], device_id=(tx, ty, tc), device_id_type=MESH)
        out.append((cp, (c != layer) if r & 1 else (c == layer)))
    return out


def _scatter_start(g, land, layer, w, name):
    def body(g_ref, land_ref, send_sems, recv_sems, g_thru, land_thru, token):
        for cp, mine in _scatter_copies(g_ref, land_ref, send_sems, recv_sems, layer, w):
            @pl.when(mine)
            def _():
                cp.start()
        token[...] = jnp.zeros_like(token)

    return pl.pallas_call(
        body, name=name,
        out_shape=(pltpu.SemaphoreType.DMA((N_DEV - 1,)), pltpu.SemaphoreType.DMA((N_DEV - 1,)),
                   pltpu.HBM(g.shape, g.dtype), pltpu.HBM(land.shape, land.dtype),
                   jax.ShapeDtypeStruct((HALO, 128), F32)),
        in_specs=(HBM, HBM), out_specs=(SEM, SEM, HBM, HBM, pl.BlockSpec(memory_space=pltpu.VMEM)),
        input_output_aliases={0: 2, 1: 3}, compiler_params=pltpu.CompilerParams(has_side_effects=EFFECT),
    )(pltpu.with_memory_space_constraint(g, pltpu.HBM), pltpu.with_memory_space_constraint(land, pltpu.HBM))


def _scatter_wait(started, land, after, w, name):
    def body(g0_ref, g1_ref, land_ref, ss0, rs0, ss1, rs1, after_ref, g0_out, g1_out, land_out):
        c = lax.axis_index("c")
        for layer, g_ref, ss, rs in ((0, g0_ref, ss0, rs0), (1, g1_ref, ss1, rs1)):
            for cp, mine in _scatter_copies(g_ref, land_ref, ss, rs, layer, w):
                @pl.when(mine)
                def _():
                    cp.wait_send()

                @pl.when(c == layer)
                def _():
                    cp.wait_recv()

    (ss0, rs0, g0), (ss1, rs1, g1) = started
    return pl.pallas_call(
        body, name=name,
        out_shape=(pltpu.HBM(g0.shape, g0.dtype), pltpu.HBM(g1.shape, g1.dtype), pltpu.HBM(land.shape, land.dtype)),
        in_specs=(HBM, HBM, HBM, SEM, SEM, SEM, SEM, ANY), out_specs=(HBM, HBM, HBM),
        input_output_aliases={0: 0, 1: 1, 2: 2}, compiler_params=pltpu.CompilerParams(has_side_effects=EFFECT),
    )(g0, g1, land, ss0, rs0, ss1, rs1, after)


def _sum_slots(g0, g1, slots, w, pos_arr, name):
    _, rows, cols = slots.shape
    tr = min(256, rows)
    nr = rows // tr
    if SHARD_AXES[w] == 2:
        own = pl.BlockSpec((tr, cols), lambda i, pos: (i, pos[0]))
    else:
        own = pl.BlockSpec((tr, cols), lambda i, pos: (pos[0] * nr + i, 0))

    def body(pos_ref, own0_ref, own1_ref, s_ref, o_ref):
        acc = jnp.where(pos_ref[1] == 0, own0_ref[...], own1_ref[...]).astype(F32)
        for r in range(N_DEV - 1):
            acc = acc + s_ref[r].astype(F32)
        o_ref[...] = acc

    return pl.pallas_call(
        body, name=name,
        grid_spec=pltpu.PrefetchScalarGridSpec(
            num_scalar_prefetch=1, grid=(nr,),
            in_specs=[own, own, pl.BlockSpec((N_DEV - 1, tr, cols), lambda i, pos: (0, i, 0))],
            out_specs=pl.BlockSpec((tr, cols), lambda i, pos: (i, 0))),
        out_shape=jax.ShapeDtypeStruct((rows, cols), F32), compiler_params=_cparams("parallel"),
    )(pos_arr, g0, g1, slots)


def _swap_layers(halves):
    def body(*refs):
        srcs, dsts = refs[:N_BIG], refs[N_BIG:2 * N_BIG]
        send_sems, recv_sems = refs[2 * N_BIG:]
        x, y, c = _mesh_pos()
        sends = [pltpu.make_async_remote_copy(src_ref=srcs[w], dst_ref=dsts[w], send_sem=send_sems.at[w],
                                              recv_sem=recv_sems.at[w], device_id=(x, y, 1 - c), device_id_type=MESH)
                 for w in range(N_BIG)]
        for cp in sends:
            cp.start()
        for cp in sends:
            cp.wait_recv()
        for cp in sends:
            cp.wait_send()

    return pl.pallas_call(
        body, name="swap_layers", in_specs=[ANY] * N_BIG, out_specs=[ANY] * N_BIG,
        out_shape=[jax.ShapeDtypeStruct(h.shape, h.dtype) for h in halves],
        scratch_shapes=[pltpu.SemaphoreType.DMA((N_BIG,)), pltpu.SemaphoreType.DMA((N_BIG,))],
    )(*halves)


def _adamw_math(w, g, m, v):
    m = ADAM_B1 * m + (1.0 - ADAM_B1) * g
    v = ADAM_B2 * v + (1.0 - ADAM_B2) * jnp.square(g)
    m_hat = m / (1.0 - ADAM_B1 ** ADAM_STEP)
    v_hat = v / (1.0 - ADAM_B2 ** ADAM_STEP)
    delta = -ADAM_LR * (m_hat / (jnp.sqrt(v_hat) + ADAM_EPS) + ADAM_WD * w)
    return delta, m, v


def _adamw(w, g_own, g_other, m, v, pos_arr, name):
    shape = w.shape
    _, rows, cols = shape
    tr = min(256, rows)

    def body(pos_ref, w_ref, own_ref, other_ref, m_ref, v_ref, g_ref, d_ref, m2_ref, v2_ref):
        g = jnp.where(pl.program_id(0) == pos_ref[1], own_ref[...], other_ref[...])
        g_ref[...] = g
        d_ref[...], m2_ref[...], v2_ref[...] = _adamw_math(w_ref[...], g, m_ref[...], v_ref[...])

    full = pl.BlockSpec((None, tr, cols), lambda l, i, pos: (l, i, 0))
    half = pl.BlockSpec((tr, cols), lambda l, i, pos: (i, 0))
    return pl.pallas_call(
        body, name=name,
        grid_spec=pltpu.PrefetchScalarGridSpec(
            num_scalar_prefetch=1, grid=(2, rows // tr),
            in_specs=[full, half, half, full, full], out_specs=[full] * 4),
        out_shape=[jax.ShapeDtypeStruct(shape, F32)] * 4, compiler_params=_cparams("parallel", "parallel"),
    )(pos_arr, w, g_own, g_other, m, v)


def _small_sync(part, w, m, v):
    rows, cols = part.shape

    def body(p_ref, w_ref, m_ref, v_ref, g_ref, d_ref, m2_ref, v2_ref, slots, send_sems, recv_sems):
        x, y, c = _mesh_pos()
        me = 4 * x + 2 * y + c
        slots[me] = p_ref[...]
        sends = []
        for r in range(1, N_DEV):
            to = (_flip(x, r & 4), _flip(y, r & 2), _flip(c, r & 1))
            sends.append(pltpu.make_async_remote_copy(
                src_ref=p_ref, dst_ref=slots.at[me], send_sem=send_sems.at[r - 1], recv_sem=recv_sems.at[r - 1],
                device_id=to, device_id_type=MESH))
        for cp in sends:
            cp.start()
        for cp in sends:
            cp.wait_recv()
        for cp in sends:
            cp.wait_send()
        g = slots[0]
        for i in range(1, N_DEV):
            g = g + slots[i]
        g_ref[...] = g
        d_ref[...], m2_ref[...], v2_ref[...] = _adamw_math(w_ref[...], g, m_ref[...], v_ref[...])

    vm = pl.BlockSpec(memory_space=pltpu.VMEM)
    return pl.pallas_call(
        body, name="small_sync", in_specs=[vm] * 4, out_specs=[vm] * 4,
        out_shape=[jax.ShapeDtypeStruct((rows, cols), F32)] * 4,
        scratch_shapes=[pltpu.VMEM((N_DEV, rows, cols), F32), pltpu.SemaphoreType.DMA((N_DEV - 1,)),
                        pltpu.SemaphoreType.DMA((N_DEV - 1,))],
    )(part, w, m, v)


def _pack_small(d, g_mix, g_group, g_mlp, g_final, conv_full, sinks, scalar):
    def part(rows):
        return jnp.pad(rows, ((0, HALO - rows.shape[0]), (0, d - rows.shape[1])))
    return jnp.concatenate([part(g_mix), part(g_group), part(g_mlp), part(g_final[None]),
                            part(conv_full.reshape(6, CONV_CH)), part(sinks.reshape(2, N_HEADS)),
                            part(scalar.reshape(1, 1))], axis=0)


def _unpack_small(p, dm):
    return (p[0:2, :dm], p[8:10, :MIX_WIDTH], p[16:18, :dm], p[24, :dm], p[32:38, :CONV_CH].reshape(2, 3, CONV_CH),
            p[40:42, :N_HEADS].reshape(2, 2, C_GROUP), p[48, 0])


def kernel(x, w_in, conv_w, sinks, g_mix, g_group, w_o, g_mlp, w_ff_in, w_ff_out, g_final, loss_target, m_w_in, m_conv_w, m_sinks, m_g_mix, m_g_group, m_w_o, m_g_mlp, m_w_ff_in, m_w_ff_out, m_g_final, v_w_in, v_conv_w, v_sinks, v_g_mix, v_g_group, v_w_o, v_g_mlp, v_w_ff_in, v_w_ff_out, v_g_final):
    d = max(x.shape[2], MIX_WIDTH)
    chip = 2 * lax.axis_index("x") + lax.axis_index("y")
    conv_n = conv_w.shape[2]

    pos_arr = jnp.stack([chip, lax.axis_index("c")]).astype(jnp.int32)
    shards = (w_in, w_o, w_ff_in, w_ff_out)
    conv_tile = jnp.pad(conv_w.reshape(6, conv_n), ((0, HALO - 6), (0, 128 - conv_n)))
    placed = [_place_shard(w_in, pos_arr[:1], "place_shard_0"), None, None, None]
    sems_a, placed, conv_thru = _gather_start(
        GATHER_STARTS[0], placed, (conv_tile, lax.empty((N_CHIPS,) + conv_tile.shape, conv_tile.dtype)),
        "gather_start_0")
    for i in range(1, N_BIG):
        placed[i] = _place_shard(shards[i], pos_arr[:1], "place_shard_%d" % i)
    sems_b, placed, _ = _gather_start(GATHER_STARTS[1], placed, None, "gather_start_1")
    full = {"arrs": placed, "conv": None, "sems": list(sems_a[:2]) + list(sems_b)}

    def fetch(stage, layer, after):
        k = 2 * layer + stage
        sems = full["sems"][2 * k:2 * k + 2]
        if k == 0:
            full["arrs"], land = _gather_wait(0, sems, full["arrs"], (sems_a[-2:], *conv_thru), after, "gather_wait_0")
            conv_all = lax.dynamic_update_slice(land, conv_tile[None], (chip, 0, 0))
            full["conv"] = conv_all[:, :6, :conv_n].reshape(N_CHIPS, 2, 3, conv_n).transpose(1, 2, 0, 3).reshape(
                2, 3, CONV_CH)
        else:
            full["arrs"], _ = _gather_wait(k, sems, full["arrs"], None, after, "gather_wait_%d" % k)
        if k == 1:
            sems_c, full["arrs"], _ = _gather_start(GATHER_STARTS[2], full["arrs"], None, "gather_start_2")
            full["sems"] += list(sems_c)
        return (*full["arrs"], full["conv"])

    lands, started = [None] * N_BIG, {}

    def emit(layer, w, g):
        if lands[w] is None:
            lands[w] = lax.empty(_slot_shape(g, w), g.dtype)
        *started[layer, w], lands[w], token = _scatter_start(g, lands[w], layer, w, "scatter_start_%d_%d" % (layer, w))
        return token[0, 0]

    loss_tile, dx, grads, dg_final = _local_step(_to_strips(x[0]), _to_strips(loss_target[0]), fetch, w_ff_in.shape[2] * N_CHIPS,
                                                 sinks, g_mix, g_group, g_mlp, g_final, emit)

    own = []
    for w in range(N_BIG):
        g0, g1, slots = _scatter_wait((started[0, w], started[1, w]), lands[w], dx, w, "scatter_wait_%d" % w)
        own.append(_sum_slots(g0, g1, slots, w, pos_arr, "sum_slots_%d" % w))
    other = _swap_layers(own)

    def both(i):
        return jnp.stack([grads[0][i][0], grads[1][i][0]])
    dconv = jnp.stack([grads[0][0][:3], grads[1][0][:3]])
    dsinks = jnp.stack([grads[0][1][0, ::HEAD_DIM], grads[1][1][0, ::HEAD_DIM]])
    part = _pack_small(d, both(2), both(3), both(4), dg_final[0], dconv, dsinks, loss_tile[0, 0])

    def spread(shard):
        return lax.dynamic_update_slice(jnp.zeros((2, 3, CONV_CH), F32), shard, (0, 0, chip * conv_n))
    zero = jnp.zeros((), F32)
    packs = [_pack_small(d, a, b, c_, e, spread(f), g_, zero) for a, b, c_, e, f, g_ in (
        (g_mix, g_group, g_mlp, g_final, conv_w, sinks),
        (m_g_mix, m_g_group, m_g_mlp, m_g_final, m_conv_w, m_sinks),
        (v_g_mix, v_g_group, v_g_mlp, v_g_final, v_conv_w, v_sinks))]
    small = [_unpack_small(p, x.shape[2]) for p in _small_sync(part, *packs)]

    def shard_of(full):
        return lax.dynamic_slice(full, (0, 0, chip * conv_n), (2, 3, conv_n))
    small = [(s[0], s[1], s[2], s[3], shard_of(s[4]), s[5], s[6]) for s in small]
    loss = small[0][6]

    big = [_adamw(w, own[i], other[i], m, v, pos_arr, "adamw_%d" % i) for i, (w, m, v) in enumerate((
        (w_in, m_w_in, v_w_in), (w_o, m_w_o, v_w_o), (w_ff_in, m_w_ff_in, v_w_ff_in),
        (w_ff_out, m_w_ff_out, v_w_ff_out)))]

    def ordered(kind):
        b = [big[i][kind] for i in range(N_BIG)]
        s = small[kind]
        return [b[0], s[4], s[5], s[0], s[1], b[1], s[2], b[2], b[3], s[3]]

    return (loss, _from_strips(dx)[None], *ordered(0), *ordered(1), *ordered(2), *ordered(3))
```

```python
import functools

import jax
import jax.numpy as jnp
from jax import lax
from jax.experimental import pallas as pl
from jax.experimental.pallas import tpu as pltpu

HEAD_DIM = 64
N_HEADS = 6
C_GROUP = 3
A_WIDTH = N_HEADS * HEAD_DIM
C_KV_WIDTH = 2 * HEAD_DIM
CONV_CH = 256
ZA_W = 3 * A_WIDTH
ZB_W = 3 * CONV_CH
ZC_W = A_WIDTH + 2 * C_KV_WIDTH
IN_WIDTH = ZA_W + ZB_W + ZC_W
MIX_WIDTH = A_WIDTH + CONV_CH + A_WIDTH
DILATIONS = (1, 4, 16)
A_MAX_DIST = 128
C_MAX_DIST = 127
TQ = 128
EPS = 1e-6
SCALE = HEAD_DIM ** -0.5
NEG = -1e30
HALO = 8

ADAM_LR = 0.001
ADAM_B1 = 0.9
ADAM_B2 = 0.999
ADAM_EPS = 1e-08
ADAM_WD = 0.01
ADAM_STEP = 10

BF = jnp.bfloat16
F32 = jnp.float32
MESH = pl.DeviceIdType.MESH
VMEM_LIMIT = 56 * 1024 * 1024


def _cparams(*sem):
    return pltpu.CompilerParams(dimension_semantics=sem, vmem_limit_bytes=VMEM_LIMIT)


def _nt(a, b):
    return lax.dot_general(a, b, (((1,), (1,)), ((), ())), preferred_element_type=F32)


def _tn(a, b):
    return lax.dot_general(a, b, (((0,), (0,)), ((), ())), preferred_element_type=F32)


def _nn(a, b):
    return jnp.dot(a, b, preferred_element_type=F32)


def _rows(tb, w):
    return pl.BlockSpec((tb, w), lambda i: (i, 0))


def _whole(shape):
    return pl.BlockSpec(shape, lambda *_: (0,) * len(shape))


def _layer(shape, l):
    return pl.BlockSpec((None,) + shape, lambda *_: (l,) + (0,) * len(shape))


def _rms_scale(v):
    return lax.rsqrt(jnp.mean(v * v, axis=-1, keepdims=True) + EPS)


def _norm_bwd(dxhat, xhat, r):
    return r * (dxhat - xhat * jnp.mean(dxhat * xhat, axis=-1, keepdims=True))


def _qkv_fwd(x, g, w_all, l, tb):
    s, d = x.shape

    def body(x_ref, g_ref, w_ref, h_ref, za_ref, zb_ref, zc_ref):
        xv = x_ref[...]
        h = ((xv * _rms_scale(xv)) * g_ref[...]).astype(BF)
        h_ref[...] = h
        z = jnp.concatenate([_nn(h, w_ref[k]) for k in range(N_CHIPS)], axis=1)
        za_ref[...] = z[:, :ZA_W]
        zb_ref[...] = z[:, ZA_W:ZA_W + ZB_W]
        zc_ref[...] = z[:, ZA_W + ZB_W:]

    return pl.pallas_call(
        body, grid=(s // tb,), name="qkv_fwd",
        in_specs=[_rows(tb, d), _whole((1, d)), _layer((N_CHIPS, d, IN_WIDTH // N_CHIPS), l)],
        out_specs=[_rows(tb, d), _rows(tb, ZA_W), _rows(tb, ZB_W), _rows(tb, ZC_W)],
        out_shape=[jax.ShapeDtypeStruct((s, d), BF), jax.ShapeDtypeStruct((s, ZA_W), F32),
                   jax.ShapeDtypeStruct((s, ZB_W), F32), jax.ShapeDtypeStruct((s, ZC_W), F32)],
        compiler_params=_cparams("parallel"),
    )(x, g, w_all)


N_STRIPS = 16


def _strips(a):
    s, w = a.shape
    return a.reshape(4, 4, s // N_STRIPS, w)


def _p_grid(s, dil):
    na = s // N_STRIPS
    return {16: (4, 4, na // TQ), 4: (4, na // 32), 1: (na // 8,)}[dil]


def _p_spec(dil, cw, col, prev=False):
    def blk(j):
        return jnp.maximum(j - 1, 0) if prev else j
    if dil == 16:
        return pl.BlockSpec((None, None, TQ, cw), lambda f, e, j: (f, e, blk(j), col))
    if dil == 4:
        return pl.BlockSpec((None, 4, 32, cw), lambda f, j: (f, 0, blk(j), col))
    return pl.BlockSpec((4, 4, 8, cw), lambda j: (0, 0, blk(j), col))


def _block_pos(i, dil):
    if dil == 16:
        return i
    if dil == 4:
        return 4 * (i % 32) + i // 32
    return 16 * (i % 8) + 4 * ((i // 8) % 4) + i // 32


def _band_mask(b, dil, max_dist):
    qi = _block_pos(lax.broadcasted_iota(jnp.int32, (TQ, 2 * TQ), 0), dil)
    col = lax.broadcasted_iota(jnp.int32, (TQ, 2 * TQ), 1)
    cur = col >= TQ
    dist = qi - _block_pos(col % TQ, dil) + jnp.where(cur, 0, TQ)
    return (dist >= 0) & (dist <= max_dist) & (cur | (b > 0))


def _hs(h):
    return slice(h * HEAD_DIM, (h + 1) * HEAD_DIM)


def _ld(ref, cols):
    v = ref[..., cols]
    return v.reshape(TQ, v.shape[-1])


def _st(ref, cols, val):
    ref[..., cols] = val.reshape(ref.shape[:-1] + (val.shape[-1],))


def _attn_fwd(z, dil, kw, kcol, vcol, n_rep, max_dist, name):
    s, zw = z.shape
    grid = _p_grid(s, dil)

    def body(q_ref, kp_ref, kc_ref, vp_ref, vc_ref, acc_ref, m_ref, l_ref):
        mask = _band_mask(pl.program_id(len(grid) - 1), dil, max_dist)
        for kh in range(N_HEADS // n_rep):
            k2 = jnp.concatenate([_ld(kp_ref, _hs(kh)), _ld(kc_ref, _hs(kh))], axis=0).astype(BF)
            v2 = jnp.concatenate([_ld(vp_ref, _hs(kh)), _ld(vc_ref, _hs(kh))], axis=0).astype(BF)
            for h in range(kh * n_rep, (kh + 1) * n_rep):
                q = _ld(q_ref, _hs(h)).astype(BF)
                sc = jnp.where(mask, _nt(q, k2) * SCALE, NEG)
                m = jnp.max(sc, axis=1, keepdims=True)
                p = jnp.exp(sc - m)
                _st(acc_ref, _hs(h), _nn(p.astype(BF), v2))
                _st(m_ref, _hs(h), jnp.broadcast_to(m, (TQ, HEAD_DIM)))
                _st(l_ref, _hs(h), jnp.broadcast_to(jnp.sum(p, axis=1, keepdims=True), (TQ, HEAD_DIM)))

    res = pl.pallas_call(
        body, grid=grid, name=name,
        in_specs=[_p_spec(dil, A_WIDTH, 0), _p_spec(dil, kw, kcol, True), _p_spec(dil, kw, kcol),
                  _p_spec(dil, kw, vcol, True), _p_spec(dil, kw, vcol)],
        out_specs=[_p_spec(dil, A_WIDTH, 0)] * 3,
        out_shape=[jax.ShapeDtypeStruct((4, 4, s // N_STRIPS, A_WIDTH), F32)] * 3,
        compiler_params=_cparams(*(("parallel",) * len(grid))),
    )(*[_strips(z)] * 5)
    return [a.reshape(s, A_WIDTH) for a in res]


def _attn_merge(parts_a, part_c, sink_row, tb):
    s = part_c[0].shape[0]
    n_a = len(parts_a)

    def body(*refs):
        ins, sink_ref = refs[:3 * n_a + 3], refs[3 * n_a + 3]
        ya_ref, lsea_ref, yc_ref, lsec_ref = refs[3 * n_a + 4:]
        ms = [ins[3 * p + 1][...] for p in range(n_a)]
        m = functools.reduce(jnp.maximum, ms)
        acc = jnp.zeros_like(m)
        l = jnp.zeros_like(m)
        for p in range(n_a):
            w = jnp.exp(ms[p] - m)
            acc = acc + w * ins[3 * p][...]
            l = l + w * ins[3 * p + 2][...]
        ya_ref[...] = acc / l
        lsea_ref[...] = m + jnp.log(l)
        acc_c, m_c, l_c = [r[...] for r in ins[3 * n_a:]]
        sk = sink_ref[...]
        m2 = jnp.maximum(m_c, sk)
        w = jnp.exp(m_c - m2)
        l2 = w * l_c + jnp.exp(sk - m2)
        yc_ref[...] = (w * acc_c) / l2
        lsec_ref[...] = m2 + jnp.log(l2)

    return pl.pallas_call(
        body, grid=(s // tb,), name="attn_merge",
        in_specs=[_rows(tb, A_WIDTH)] * (3 * n_a + 3) + [_whole((1, A_WIDTH))],
        out_specs=[_rows(tb, A_WIDTH)] * 4, out_shape=[jax.ShapeDtypeStruct((s, A_WIDTH), F32)] * 4,
        compiler_params=_cparams("parallel"),
    )(*[a for part in parts_a + [part_c] for a in part], sink_row)


def _shift_down(v, n, halo):
    rows = v.shape[0]
    out = pltpu.roll(v, n, 0)
    row = lax.broadcasted_iota(jnp.int32, v.shape, 0)
    for t in range(n):
        out = jnp.where(row == t, halo[HALO - n + t:HALO - n + t + 1, :], out)
    return out


def _shift_up(v, n, halo):
    rows = v.shape[0]
    out = pltpu.roll(v, rows - n, 0)
    row = lax.broadcasted_iota(jnp.int32, v.shape, 0)
    for t in range(n):
        out = jnp.where(row == rows - n + t, halo[t:t + 1, :], out)
    return out


def _strip(v, b):
    return v[b % 4, b // 4]


def _conv_strips(zb, prev, cw):
    gb = [_strip(zb, b)[:, :CONV_CH] for b in range(N_STRIPS)]
    gc = [_strip(zb, b)[:, CONV_CH:2 * CONV_CH] for b in range(N_STRIPS)]
    xb = [_strip(zb, b)[:, 2 * CONV_CH:] for b in range(N_STRIPS)]
    u = [g * v for g, v in zip(gc, xb)]
    uh = prev[:, :, CONV_CH:2 * CONV_CH] * prev[:, :, 2 * CONV_CH:]
    wrapped = {14: _shift_down(u[14], 1, uh[2]), 15: _shift_down(u[15], 1, uh[3])}
    u1 = [u[b - 1] if b >= 1 else wrapped[15] for b in range(N_STRIPS)]
    u2 = [u[b - 2] if b >= 2 else wrapped[14 + b] for b in range(N_STRIPS)]
    c = [cw[0:1, :] * u2[b] + cw[1:2, :] * u1[b] + cw[2:3, :] * u[b] for b in range(N_STRIPS)]
    return gb, gc, xb, u, u1, u2, c


def _strip_rows(ta, w):
    return pl.BlockSpec((4, 4, ta, w), lambda i: (0, 0, i, 0))


def _prev_rows(ta, w):
    return pl.BlockSpec((4, None, HALO, w), lambda i: (0, 3, jnp.maximum(i * (ta // HALO) - 1, 0), 0))


def _next_rows(ta, w, nblk):
    return pl.BlockSpec((4, None, HALO, w),
                        lambda i: (0, 0, jnp.minimum((i + 1) * (ta // HALO), nblk * (ta // HALO) - 1), 0))


def _mix_fwd(x, ya, yc, zb, cw, gg, wo_all, l, tb):
    s, d = x.shape
    ta = tb // N_STRIPS

    def body(x_ref, ya_ref, yc_ref, zb_ref, zbp_ref, cw_ref, gg_ref, wo_ref, x1_ref, yb_ref):
        i = pl.program_id(0)
        prev = jnp.where(i > 0, zbp_ref[...], 0.0)
        gb, _, _, _, _, _, c = _conv_strips(zb_ref[...], prev, cw_ref[...])
        for b in range(N_STRIPS):
            yb_ref[b % 4, b // 4] = gb[b] * c[b]
        yb = yb_ref[...].reshape(tb, CONV_CH)
        ya, yc = ya_ref[...].reshape(tb, A_WIDTH), yc_ref[...].reshape(tb, A_WIDTH)
        n = jnp.concatenate([ya * _rms_scale(ya), yb * _rms_scale(yb), yc * _rms_scale(yc)], axis=1)
        n = (n * gg_ref[...]).astype(BF)
        x1 = x_ref[...].reshape(tb, d) + _nn(n, wo_ref[...].reshape(MIX_WIDTH, d))
        x1_ref[...] = x1.reshape(4, 4, ta, d)

    res = pl.pallas_call(
        body, grid=(s // tb,), name="mix_fwd",
        in_specs=[_strip_rows(ta, d), _strip_rows(ta, A_WIDTH), _strip_rows(ta, A_WIDTH), _strip_rows(ta, ZB_W),
                  _prev_rows(ta, ZB_W), _whole((HALO, CONV_CH)), _whole((1, MIX_WIDTH)),
                  _layer((N_CHIPS, MIX_WIDTH // N_CHIPS, d), l)],
        out_specs=[_strip_rows(ta, d), _strip_rows(ta, CONV_CH)],
        out_shape=[jax.ShapeDtypeStruct((4, 4, s // N_STRIPS, d), F32),
                   jax.ShapeDtypeStruct((4, 4, s // N_STRIPS, CONV_CH), F32)],
        compiler_params=_cparams("parallel"),
    )(_strips(x), _strips(ya), _strips(yc), _strips(zb), _strips(zb), cw, gg, wo_all)
    return res[0].reshape(s, d), res[1].reshape(s, CONV_CH)


def _mlp_fwd(x1, g, w1_all, w2_all, l, tb, tf):
    s, d = x1.shape
    ff = w1_all.shape[1] * w1_all.shape[3]
    nj = ff // tf

    def body(x_ref, g_ref, w1_ref, w2_ref, x2_ref, h2_ref, ap_ref, acc):
        j = pl.program_id(1)

        @pl.when(j == 0)
        def _():
            xv = x_ref[...]
            h2_ref[...] = ((xv * _rms_scale(xv)) * g_ref[...]).astype(BF)
            acc[...] = jnp.zeros_like(acc)

        ap = _nn(h2_ref[...], w1_ref[...])
        ap_ref[...] = ap.astype(BF)
        a = jnp.square(jnp.maximum(ap, 0.0)).astype(BF)
        acc[...] += _nn(a, w2_ref[...])

        @pl.when(j == nj - 1)
        def _():
            x2_ref[...] = x_ref[...] + acc[...]

    return pl.pallas_call(
        body, grid=(s // tb, nj), name="mlp_fwd",
        in_specs=[pl.BlockSpec((tb, d), lambda i, j: (i, 0)), _whole((1, d)),
                  pl.BlockSpec((None, None, d, tf), lambda i, j: (l, j, 0, 0)),
                  pl.BlockSpec((None, None, tf, d), lambda i, j: (l, j, 0, 0))],
        out_specs=[pl.BlockSpec((tb, d), lambda i, j: (i, 0)), pl.BlockSpec((tb, d), lambda i, j: (i, 0)),
                   pl.BlockSpec((tb, tf), lambda i, j: (i, j))],
        out_shape=[jax.ShapeDtypeStruct((s, d), F32), jax.ShapeDtypeStruct((s, d), BF),
                   jax.ShapeDtypeStruct((s, ff), BF)],
        scratch_shapes=[pltpu.VMEM((tb, d), F32)],
        compiler_params=_cparams("parallel", "arbitrary"),
    )(x1, g, w1_all, w2_all)


def _loss_head(x, g, tgt, tb):
    s, d = x.shape

    def body(x_ref, g_ref, t_ref, dx_ref, loss_ref, dg_ref):
        i = pl.program_id(0)

        @pl.when(i == 0)
        def _():
            loss_ref[...] = jnp.zeros_like(loss_ref)
            dg_ref[...] = jnp.zeros_like(dg_ref)

        xv = x_ref[...]
        r = _rms_scale(xv)
        xhat = xv * r
        err = xhat * g_ref[...] - t_ref[...]
        part = jnp.sum(jnp.mean(jnp.square(err), axis=-1, keepdims=True), axis=0, keepdims=True)
        loss_ref[...] += 0.5 * part
        dy = err * (1.0 / d)
        dg_ref[...] += jnp.sum(dy * xhat, axis=0, keepdims=True)
        dx_ref[...] = _norm_bwd(dy * g_ref[...], xhat, r)

    return pl.pallas_call(
        body, grid=(s // tb,), name="loss_head",
        in_specs=[_rows(tb, d), _whole((1, d)), _rows(tb, d)],
        out_specs=[_rows(tb, d), _whole((HALO, 128)), _whole((HALO, d))],
        out_shape=[jax.ShapeDtypeStruct((s, d), F32), jax.ShapeDtypeStruct((HALO, 128), F32),
                   jax.ShapeDtypeStruct((HALO, d), F32)],
        compiler_params=_cparams("arbitrary"),
    )(x, g, tgt)


def _mlp_bwd(dx2, x1, ap, g, w1_all, w2_all, l, tb, tf):
    s, d = x1.shape
    ff = ap.shape[1]
    nj = ff // tf

    def body(dx2_ref, x1_ref, ap_ref, g_ref, w1_ref, w2_ref, dx1_ref, dap_ref, dg_ref, acc):
        i, j = pl.program_id(0), pl.program_id(1)

        @pl.when((i == 0) & (j == 0))
        def _():
            dg_ref[...] = jnp.zeros_like(dg_ref)

        @pl.when(j == 0)
        def _():
            acc[...] = jnp.zeros_like(acc)

        da = _nt(dx2_ref[...].astype(BF), w2_ref[...])
        dap = (da * (2.0 * jnp.maximum(ap_ref[...].astype(F32), 0.0))).astype(BF)
        dap_ref[...] = dap
        acc[...] += _nt(dap, w1_ref[...])

        @pl.when(j == nj - 1)
        def _():
            xv = x1_ref[...]
            r = _rms_scale(xv)
            xhat = xv * r
            dh = acc[...]
            dg_ref[...] += jnp.sum(dh * xhat, axis=0, keepdims=True)
            dx1_ref[...] = dx2_ref[...] + _norm_bwd(dh * g_ref[...], xhat, r)

    return pl.pallas_call(
        body, grid=(s // tb, nj), name="mlp_bwd",
        in_specs=[pl.BlockSpec((tb, d), lambda i, j: (i, 0)), pl.BlockSpec((tb, d), lambda i, j: (i, 0)),
                  pl.BlockSpec((tb, tf), lambda i, j: (i, j)),
                  _whole((1, d)), pl.BlockSpec((None, None, d, tf), lambda i, j: (l, j, 0, 0)),
                  pl.BlockSpec((None, None, tf, d), lambda i, j: (l, j, 0, 0))],
        out_specs=[pl.BlockSpec((tb, d), lambda i, j: (i, 0)), pl.BlockSpec((tb, tf), lambda i, j: (i, j)),
                   _whole((HALO, d))],
        out_shape=[jax.ShapeDtypeStruct((s, d), F32), jax.ShapeDtypeStruct((s, ff), BF),
                   jax.ShapeDtypeStruct((HALO, d), F32)],
        scratch_shapes=[pltpu.VMEM((tb, d), F32)],
        compiler_params=_cparams("arbitrary", "arbitrary"),
    )(dx2, x1, ap, g, w1_all, w2_all)


def _wgrad(a, b, tm, tn, ts, name, relu2=False):
    s, m = a.shape
    n = b.shape[1]
    ns = s // ts

    def body(a_ref, b_ref, o_ref, acc):
        k = pl.program_id(2)

        @pl.when(k == 0)
        def _():
            acc[...] = jnp.zeros_like(acc)

        av = a_ref[...]
        if relu2:
            av = jnp.square(jnp.maximum(av.astype(F32), 0.0)).astype(BF)
        acc[...] += _tn(av, b_ref[...].astype(BF))

        @pl.when(k == ns - 1)
        def _():
            o_ref[...] = acc[...].astype(BF)

    return pl.pallas_call(
        body, grid=(m // tm, n // tn, ns), name=name,
        in_specs=[pl.BlockSpec((ts, tm), lambda i, j, k: (k, i)), pl.BlockSpec((ts, tn), lambda i, j, k: (k, j))],
        out_specs=pl.BlockSpec((tm, tn), lambda i, j, k: (i, j)),
        out_shape=jax.ShapeDtypeStruct((m, n), BF),
        scratch_shapes=[pltpu.VMEM((tm, tn), F32)],
        compiler_params=_cparams("parallel", "parallel", "arbitrary"),
    )(a, b)


def _mix_bwd(dx1, ya, yb, yc, lse_c, sink_row, gg, wo_all, l, tb):
    s, d = dx1.shape

    def body(dx_ref, ya_ref, yb_ref, yc_ref, lse_ref, sink_ref, gg_ref, wo_ref,
             n_ref, dya_ref, dyc_ref, da_ref, dc_ref, dyb_ref, dg_ref, dsink_ref):
        i = pl.program_id(0)

        @pl.when(i == 0)
        def _():
            dg_ref[...] = jnp.zeros_like(dg_ref)
            dsink_ref[...] = jnp.zeros_like(dsink_ref)

        dn = _nt(dx_ref[...].astype(BF), wo_ref[...].reshape(MIX_WIDTH, d))
        ys = [ya_ref[...], yb_ref[...], yc_ref[...]]
        rs = [_rms_scale(v) for v in ys]
        nhat = jnp.concatenate([v * r for v, r in zip(ys, rs)], axis=1)
        gg = gg_ref[...]
        n_ref[...] = (nhat * gg).astype(BF)
        dg_ref[...] += jnp.sum(dn * nhat, axis=0, keepdims=True)
        dnh = dn * gg
        bounds = [(0, A_WIDTH), (A_WIDTH, A_WIDTH + CONV_CH), (A_WIDTH + CONV_CH, MIX_WIDTH)]
        dys = [_norm_bwd(dnh[:, lo:hi], nhat[:, lo:hi], r) for (lo, hi), r in zip(bounds, rs)]
        dyb_ref[...] = dys[1]
        for dy, y, dy_ref, dd_ref in ((dys[0], ys[0], dya_ref, da_ref), (dys[2], ys[2], dyc_ref, dc_ref)):
            dy_ref[...] = dy
            t = dy * y
            for h in range(N_HEADS):
                dd_ref[:, _hs(h)] = jnp.broadcast_to(jnp.sum(t[:, _hs(h)], axis=1, keepdims=True), (tb, HEAD_DIM))
        dsink_ref[...] -= jnp.sum(jnp.exp(sink_ref[...] - lse_ref[...]) * dc_ref[...], axis=0, keepdims=True)

    return pl.pallas_call(
        body, grid=(s // tb,), name="mix_bwd",
        in_specs=[_rows(tb, d), _rows(tb, A_WIDTH), _rows(tb, CONV_CH), _rows(tb, A_WIDTH), _rows(tb, A_WIDTH),
                  _whole((1, A_WIDTH)), _whole((1, MIX_WIDTH)), _layer((N_CHIPS, MIX_WIDTH // N_CHIPS, d), l)],
        out_specs=[_rows(tb, MIX_WIDTH), _rows(tb, A_WIDTH), _rows(tb, A_WIDTH), _rows(tb, A_WIDTH),
                   _rows(tb, A_WIDTH), _rows(tb, CONV_CH), _whole((HALO, MIX_WIDTH)), _whole((HALO, A_WIDTH))],
        out_shape=[jax.ShapeDtypeStruct((s, MIX_WIDTH), BF), jax.ShapeDtypeStruct((s, A_WIDTH), F32),
                   jax.ShapeDtypeStruct((s, A_WIDTH), F32), jax.ShapeDtypeStruct((s, A_WIDTH), F32),
                   jax.ShapeDtypeStruct((s, A_WIDTH), F32), jax.ShapeDtypeStruct((s, CONV_CH), F32),
                   jax.ShapeDtypeStruct((HALO, MIX_WIDTH), F32), jax.ShapeDtypeStruct((HALO, A_WIDTH), F32)],
        compiler_params=_cparams("arbitrary"),
    )(dx1, ya, yb, yc, lse_c, sink_row, gg, wo_all)


def _attn_bwd(z, dy, lse, dd, dil, kw, kcol, vcol, n_rep, max_dist, name):
    s, zw = z.shape
    grid = _p_grid(s, dil)
    n_kv = N_HEADS // n_rep

    def body(q_ref, kp_ref, kc_ref, vp_ref, vc_ref, dy_ref, lse_ref, dd_ref, dq_ref, dkp_ref, dkc_ref, dvp_ref, dvc_ref):
        mask = _band_mask(pl.program_id(len(grid) - 1), dil, max_dist)
        for kh in range(n_kv):
            k2 = jnp.concatenate([_ld(kp_ref, _hs(kh)), _ld(kc_ref, _hs(kh))], axis=0).astype(BF)
            v2 = jnp.concatenate([_ld(vp_ref, _hs(kh)), _ld(vc_ref, _hs(kh))], axis=0).astype(BF)
            dk2 = jnp.zeros((2 * TQ, HEAD_DIM), F32)
            dv2 = jnp.zeros((2 * TQ, HEAD_DIM), F32)
            for h in range(kh * n_rep, (kh + 1) * n_rep):
                q = _ld(q_ref, _hs(h)).astype(BF)
                lse_h = _ld(lse_ref, slice(h * HEAD_DIM, h * HEAD_DIM + 1))
                dd_h = _ld(dd_ref, slice(h * HEAD_DIM, h * HEAD_DIM + 1))
                dyh = _ld(dy_ref, _hs(h)).astype(BF)
                sc = jnp.where(mask, _nt(q, k2) * SCALE, NEG)
                p = jnp.exp(sc - lse_h)
                dp = _nt(dyh, v2)
                ds = ((p * (dp - dd_h)) * SCALE).astype(BF)
                _st(dq_ref, _hs(h), _nn(ds, k2))
                dk2 = dk2 + _tn(ds, q)
                dv2 = dv2 + _tn(p.astype(BF), dyh)
            _st(dkp_ref, _hs(kh), dk2[:TQ])
            _st(dkc_ref, _hs(kh), dk2[TQ:])
            _st(dvp_ref, _hs(kh), dv2[:TQ])
            _st(dvc_ref, _hs(kh), dv2[TQ:])

    args = [_strips(z)] * 5 + [_strips(a) for a in (dy, lse, dd)]
    in_specs = [_p_spec(dil, A_WIDTH, 0), _p_spec(dil, kw, kcol, True), _p_spec(dil, kw, kcol),
                _p_spec(dil, kw, vcol, True), _p_spec(dil, kw, vcol)] + [_p_spec(dil, A_WIDTH, 0)] * 3
    out_specs = [_p_spec(dil, A_WIDTH, 0)] + [_p_spec(dil, kw, 0)] * 4
    na = s // N_STRIPS
    out_shape = [jax.ShapeDtypeStruct((4, 4, na, A_WIDTH), F32)] + [jax.ShapeDtypeStruct((4, 4, na, kw), F32)] * 4
    res = pl.pallas_call(
        body, grid=grid, name=name, in_specs=in_specs, out_specs=out_specs, out_shape=out_shape,
        compiler_params=_cparams(*(("parallel",) * len(grid))),
    )(*args)
    return [res[0].reshape(s, A_WIDTH)] + [a.reshape(s, kw) for a in res[1:]]


DZ_TA = 16


def _dz_assemble(parts_a, parts_c, dyb, zb, cw):
    s = zb.shape[0]
    na = s // N_STRIPS
    nb = na // DZ_TA

    def ahead(w, k):
        return pl.BlockSpec((4, 4, DZ_TA, w), lambda i: (0, 0, jnp.minimum(i + k, nb - 1), 0))

    args, in_specs = [], []
    for dil, (dq, dkp, dkc, dvp, dvc) in zip(DILATIONS + (1,), parts_a + [parts_c]):
        w = dkp.shape[1]
        here = _strip_rows(DZ_TA, w)
        if dil == 1:
            args += [dq, dkp, dkp, dkc, dvp, dvp, dvc]
            in_specs += [_strip_rows(DZ_TA, A_WIDTH), here, ahead(w, 1), here, here, ahead(w, 1), here]
        else:
            k = 8 * dil // DZ_TA
            args += [dq, dkp, dkc, dvp, dvc]
            in_specs += [_strip_rows(DZ_TA, A_WIDTH), ahead(w, k), here, ahead(w, k), here]
    n_att = len(args)
    args = [_strips(a) for a in args] + [_strips(dyb), _strips(dyb), _strips(zb), _strips(zb), _strips(zb), cw]
    in_specs += [_strip_rows(DZ_TA, CONV_CH), _next_rows(DZ_TA, CONV_CH, nb), _strip_rows(DZ_TA, ZB_W),
                 _prev_rows(DZ_TA, ZB_W), _next_rows(DZ_TA, ZB_W, nb), _whole((HALO, CONV_CH))]

    def body(*refs):
        att = list(refs[:n_att])
        dyb_ref, dybn_ref, zb_ref, zbp_ref, zbn_ref, cw_ref, dz_ref, dcw_ref = refs[n_att:]
        i = pl.program_id(0)

        @pl.when(i == 0)
        def _():
            dcw_ref[...] = jnp.zeros_like(dcw_ref)

        def shifted(dil):
            if dil == 1:
                dq_r, kp0, kp1, dkc_r, vp0, vp1, dvc_r = [att.pop(0) for _ in range(7)]
                live = i + 1 < nb
                half = DZ_TA // 2
                dkp = jnp.concatenate([kp0[:, :, half:, :], jnp.where(live, kp1[:, :, :half, :], 0.0)], axis=2)
                dvp = jnp.concatenate([vp0[:, :, half:, :], jnp.where(live, vp1[:, :, :half, :], 0.0)], axis=2)
            else:
                dq_r, dkp_r, dkc_r, dvp_r, dvc_r = [att.pop(0) for _ in range(5)]
                live = i + 8 * dil // DZ_TA < nb
                dkp, dvp = jnp.where(live, dkp_r[...], 0.0), jnp.where(live, dvp_r[...], 0.0)
            return dq_r[...], dkc_r[...] + dkp, dvc_r[...] + dvp

        dq, dk, dv = shifted(DILATIONS[0])
        for dil in DILATIONS[1:]:
            dq2, dk2, dv2 = shifted(dil)
            dq, dk, dv = dq + dq2, dk + dk2, dv + dv2
        dz_ref[:, :, :, 0:A_WIDTH] = dq.astype(BF)
        dz_ref[:, :, :, A_WIDTH:2 * A_WIDTH] = dk.astype(BF)
        dz_ref[:, :, :, 2 * A_WIDTH:ZA_W] = dv.astype(BF)
        dq, dk, dv = shifted(1)
        c0 = ZA_W + ZB_W
        dz_ref[:, :, :, c0:c0 + A_WIDTH] = dq.astype(BF)
        dz_ref[:, :, :, c0 + A_WIDTH:c0 + A_WIDTH + C_KV_WIDTH] = dk.astype(BF)
        dz_ref[:, :, :, c0 + A_WIDTH + C_KV_WIDTH:IN_WIDTH] = dv.astype(BF)

        cw = cw_ref[...]
        prev = jnp.where(i > 0, zbp_ref[...], 0.0)
        gb, gc, xb, u, u1, u2, c = _conv_strips(zb_ref[...], prev, cw)
        dyb = dyb_ref[...]
        dc = [_strip(dyb, b) * gb[b] for b in range(N_STRIPS)]
        dcn = jnp.where(i + 1 < nb, dybn_ref[...] * zbn_ref[:, :, :CONV_CH], 0.0)
        wrapped = [_shift_up(dc[0], 1, dcn[0]), _shift_up(dc[1], 1, dcn[1])]
        upd = [jnp.zeros((1, CONV_CH), F32)] * 3
        for b in range(N_STRIPS):
            dc1 = dc[b + 1] if b + 1 < N_STRIPS else wrapped[0]
            dc2 = dc[b + 2] if b + 2 < N_STRIPS else wrapped[b + 2 - N_STRIPS]
            du = cw[2:3, :] * dc[b] + cw[1:2, :] * dc1 + cw[0:1, :] * dc2
            f, e = b % 4, b // 4
            dz_ref[f, e, :, ZA_W:ZA_W + CONV_CH] = (_strip(dyb, b) * c[b]).astype(BF)
            dz_ref[f, e, :, ZA_W + CONV_CH:ZA_W + 2 * CONV_CH] = (du * xb[b]).astype(BF)
            dz_ref[f, e, :, ZA_W + 2 * CONV_CH:c0] = (du * gc[b]).astype(BF)
            for t, uu in enumerate((u2[b], u1[b], u[b])):
                upd[t] = upd[t] + jnp.sum(dc[b] * uu, axis=0, keepdims=True)
        row = lax.broadcasted_iota(jnp.int32, (HALO, CONV_CH), 0)
        tile = jnp.zeros((HALO, CONV_CH), F32)
        for t in range(3):
            tile = jnp.where(row == t, upd[t], tile)
        dcw_ref[...] += tile

    dz, dcw = pl.pallas_call(
        body, grid=(nb,), name="dz_assemble", in_specs=in_specs,
        out_specs=[_strip_rows(DZ_TA, IN_WIDTH), _whole((HALO, CONV_CH))],
        out_shape=[jax.ShapeDtypeStruct((4, 4, na, IN_WIDTH), BF), jax.ShapeDtypeStruct((HALO, CONV_CH), F32)],
        compiler_params=_cparams("arbitrary"),
    )(*args)
    return dz.reshape(s, IN_WIDTH), dcw


def _qkv_bwd(dz, dx1, x, g, w_all, l, tb):
    s, d = x.shape

    def body(dz_ref, dx1_ref, x_ref, g_ref, w_ref, dx_ref, dg_ref):
        i = pl.program_id(0)

        @pl.when(i == 0)
        def _():
            dg_ref[...] = jnp.zeros_like(dg_ref)

        n = IN_WIDTH // N_CHIPS
        dh = _nt(dz_ref[:, 0:n], w_ref[0])
        for k in range(1, N_CHIPS):
            dh = dh + _nt(dz_ref[:, k * n:(k + 1) * n], w_ref[k])
        xv = x_ref[...]
        r = _rms_scale(xv)
        xhat = xv * r
        dg_ref[...] += jnp.sum(dh * xhat, axis=0, keepdims=True)
        dx_ref[...] = dx1_ref[...] + _norm_bwd(dh * g_ref[...], xhat, r)

    return pl.pallas_call(
        body, grid=(s // tb,), name="qkv_bwd",
        in_specs=[_rows(tb, IN_WIDTH), _rows(tb, d), _rows(tb, d), _whole((1, d)),
                  _layer((N_CHIPS, d, IN_WIDTH // N_CHIPS), l)],
        out_specs=[_rows(tb, d), _whole((HALO, d))],
        out_shape=[jax.ShapeDtypeStruct((s, d), F32), jax.ShapeDtypeStruct((HALO, d), F32)],
        compiler_params=_cparams("arbitrary"),
    )(dz, dx1, x, g, w_all)


def _tile_rows(rows):
    return jnp.pad(rows, ((0, HALO - rows.shape[0]), (0, 0)))


def _to_strips(a, after, name):
    s, d = a.shape
    na = s // N_STRIPS
    ta = min(32, na)

    def body(a_ref, after_ref, o_ref):
        for b in range(N_STRIPS):
            o_ref[b % 4, b // 4] = a_ref[:, b, :]

    return pl.pallas_call(
        body, grid=(na // ta,), name=name,
        in_specs=[pl.BlockSpec((ta, N_STRIPS, d), lambda i: (i, 0, 0)), ANY], out_specs=_strip_rows(ta, d),
        out_shape=jax.ShapeDtypeStruct((4, 4, na, d), a.dtype), compiler_params=_cparams("parallel"),
    )(a.reshape(na, N_STRIPS, d), after).reshape(s, d)


def _from_strips(a, name):
    s, d = a.shape
    na = s // N_STRIPS
    ta = min(32, na)

    def body(a_ref, o_ref):
        for b in range(N_STRIPS):
            o_ref[:, b, :] = a_ref[b % 4, b // 4]

    return pl.pallas_call(
        body, grid=(na // ta,), name=name, in_specs=[_strip_rows(ta, d)],
        out_specs=pl.BlockSpec((ta, N_STRIPS, d), lambda i: (i, 0, 0)),
        out_shape=jax.ShapeDtypeStruct((na, N_STRIPS, d), a.dtype), compiler_params=_cparams("parallel"),
    )(_strips(a)).reshape(s, d)


def _local_step(x, tgt, fetch, ff, sinks, g_mix, g_group, g_mlp, g_final, emit):
    s, d = x.shape
    depth = g_mix.shape[0]
    tb = min(512, s)
    tf = ff // N_CHIPS
    ts = min(1024, s)
    saved = []
    for l in range(depth):
        w_in, _, _, _, conv_w = fetch(0, l, x)
        cw = _tile_rows(conv_w[l])
        sk = jnp.repeat(sinks[l].reshape(N_HEADS), HEAD_DIM)[None]
        h, za, zb, zc = _qkv_fwd(x, g_mix[l][None], w_in, l, tb)
        parts_a = [_attn_fwd(za, dil, A_WIDTH, 1, 2, 1, A_MAX_DIST, "attn_a_fwd_%d" % dil) for dil in DILATIONS]
        part_c = _attn_fwd(zc, 1, C_KV_WIDTH, 3, 4, C_GROUP, C_MAX_DIST, "attn_c_fwd")
        ya, lse_a, yc, lse_c = _attn_merge(parts_a, part_c, sk, tb)
        w_in, w_o, w1, w2, _ = fetch(1, l, yc)
        x1, yb = _mix_fwd(x, ya, yc, zb, cw, g_group[l][None], w_o, l, tb)
        x2, h2, ap = _mlp_fwd(x1, g_mlp[l][None], w1, w2, l, ts, tf)
        saved.append((x, h, za, zb, zc, ya, lse_a, yc, lse_c, yb, x1, h2, ap, cw, sk))
        x = x2
    dx, loss_tile, dg_final = _loss_head(x, g_final[None], tgt, tb)
    grads = [None] * depth
    tok = jnp.zeros((), F32)
    for l in reversed(range(depth)):
        x0, h, za, zb, zc, ya, lse_a, yc, lse_c, yb, x1, h2, ap, cw, sk = saved[l]
        dx1, dap, dg_mlp = _mlp_bwd(dx, x1, ap, g_mlp[l][None] + tok, w1, w2, l, ts, tf)
        tok = emit(l, 3, _wgrad(ap, dx, min(1024, ff), d, ts, "wgrad_ff_out", relu2=True))
        tok = tok + emit(l, 2, _wgrad(h2, dap, d, min(1024, ff), ts, "wgrad_ff_in"))
        n, dya, dyc, dd_a, dd_c, dyb, dg_group, dsink = _mix_bwd(dx1, ya, yb, yc, lse_c, sk, g_group[l][None] + tok,
                                                                 w_o, l, tb)
        tok = emit(l, 1, _wgrad(n, dx1, MIX_WIDTH, d, ts, "wgrad_o"))
        cw = cw + tok
        parts_a = [_attn_bwd(za, dya, lse_a, dd_a, dil, A_WIDTH, 1, 2, 1, A_MAX_DIST, "attn_a_bwd_%d" % dil)
                   for dil in DILATIONS]
        parts_c = _attn_bwd(zc, dyc, lse_c, dd_c, 1, C_KV_WIDTH, 3, 4, C_GROUP, C_MAX_DIST, "attn_c_bwd")
        dz, dcw = _dz_assemble(parts_a, parts_c, dyb, zb, cw)
        dx, dg_mix = _qkv_bwd(dz, dx1, x0, g_mix[l][None], w_in, l, tb)
        tok = emit(l, 0, _wgrad(h, dz, d, IN_WIDTH // 4, ts, "wgrad_in"))
        grads[l] = (dcw, dsink, dg_mix, dg_group, dg_mlp)
    return loss_tile, dx, grads, dg_final


ANY = pl.BlockSpec(memory_space=pl.ANY)
SHARD_AXES = (2, 1, 2, 1)
N_BIG = len(SHARD_AXES)
N_CHIPS = 4
N_DEV = 8


def _mesh_pos():
    return lax.axis_index("x"), lax.axis_index("y"), lax.axis_index("c")


def _flip(v, bit):
    return 1 - v if bit else v


def _place_shard(shard, chip_arr, name):
    _, rows, cols = shard.shape
    tr = min(256, rows)

    def body(chip_ref, x_ref, o_ref):
        o_ref[...] = x_ref[...].astype(BF)

    return pl.pallas_call(
        body, name=name,
        grid_spec=pltpu.PrefetchScalarGridSpec(
            num_scalar_prefetch=1, grid=(2, rows // tr),
            in_specs=[pl.BlockSpec((None, tr, cols), lambda l, i, chip: (l, i, 0))],
            out_specs=pl.BlockSpec((None, None, tr, cols), lambda l, i, chip: (l, chip[0], i, 0))),
        out_shape=jax.ShapeDtypeStruct((2, N_CHIPS, rows, cols), BF),
        compiler_params=_cparams("parallel", "parallel"),
    )(chip_arr, shard)


HBM = pl.BlockSpec(memory_space=pltpu.HBM)
SEM = pl.BlockSpec(memory_space=pltpu.SEMAPHORE)
EFFECT = pltpu.SideEffectType.DATAFLOW_SIDE_EFFECTING

GATHER_GROUPS = (((0, 0),), ((1, 0), (2, 0), (3, 0)), ((0, 1),), ((1, 1), (2, 1), (3, 1)))
GATHER_STARTS = ((0,), (1,), (2, 3))


def _gather_copies(arrs, group, send_sems, recv_sems):
    x, y, c = _mesh_pos()
    me = 2 * x + y
    out = []
    for i, (w, layer) in enumerate(group):
        mine = arrs[w].at[layer, me]
        for j, (qx, qy) in enumerate([(1 - x, y), (x, 1 - y), (1 - x, 1 - y)]):
            landed = arrs[w].at[layer, 2 * qx + qy]
            out.append(tuple(pltpu.make_async_remote_copy(
                src_ref=piece, dst_ref=piece, send_sem=send_sems.at[i * 3 + j], recv_sem=recv_sems.at[i * 3 + j],
                device_id=(qx, qy, c), device_id_type=MESH) for piece in (mine, landed)))
    return out


def _conv_copies(conv_src, conv_dst, send_sems, recv_sems):
    x, y, c = _mesh_pos()
    out = []
    for j, (qx, qy) in enumerate([(1 - x, y), (x, 1 - y), (1 - x, 1 - y)]):
        out.append(tuple(pltpu.make_async_remote_copy(
            src_ref=conv_src, dst_ref=conv_dst.at[q], send_sem=send_sems.at[j], recv_sem=recv_sems.at[j],
            device_id=(qx, qy, c), device_id_type=MESH) for q in (2 * x + y, 2 * qx + qy)))
    return out


def _gather_start(groups, arrs, conv, name):
    n_sems = 2 * (len(groups) + (conv is not None))
    mats = sorted({w for g in groups for w, _ in GATHER_GROUPS[g]})

    def body(*refs):
        arrs_ref = [None] * N_BIG
        for w, ref in zip(mats, refs):
            arrs_ref[w] = ref
        sems = refs[n_op:n_op + n_sems]
        if conv is not None:
            for cp, _ in _conv_copies(refs[len(mats)], refs[len(mats) + 1], sems[-2], sems[-1]):
                cp.start()
        for k, g in enumerate(groups):
            for cp, _ in _gather_copies(arrs_ref, GATHER_GROUPS[g], sems[2 * k], sems[2 * k + 1]):
                cp.start()

    sem_shapes = []
    for n in [len(GATHER_GROUPS[g]) for g in groups] + ([1] if conv is not None else []):
        sem_shapes += [pltpu.SemaphoreType.DMA((3 * n,))] * 2
    operands = [arrs[w] for w in mats] + ([] if conv is None else list(conv))
    n_op = len(operands)
    res = pl.pallas_call(
        body, name=name,
        out_shape=tuple(sem_shapes) + tuple(pltpu.HBM(a.shape, a.dtype) for a in operands),
        in_specs=(HBM,) * n_op, out_specs=(SEM,) * n_sems + (HBM,) * n_op,
        input_output_aliases={i: n_sems + i for i in range(n_op)},
        compiler_params=pltpu.CompilerParams(has_side_effects=EFFECT),
    )(*[pltpu.with_memory_space_constraint(a, pltpu.HBM) for a in operands])
    arrs = list(arrs)
    for w, a in zip(mats, res[n_sems:]):
        arrs[w] = a
    return res[:n_sems], arrs, list(res[n_sems + len(mats):])


def _gather_wait(k, sems, arrs, conv, after, name):
    group = GATHER_GROUPS[k]
    mats = sorted({w for w, _ in group})
    n_conv = 0 if conv is None else 2

    def body(*refs):
        local = refs[:len(mats)]
        arrs_ref = [None] * N_BIG
        for w, ref in zip(mats, local):
            arrs_ref[w] = ref
        pos = len(mats) + n_conv
        copies = _gather_copies(arrs_ref, group, refs[pos], refs[pos + 1])
        if conv is not None:
            copies += _conv_copies(refs[len(mats)], refs[len(mats) + 1], refs[pos + 2], refs[pos + 3])
        for send, recv in copies:
            recv.wait_recv()
            send.wait_send()

    operands = [arrs[w] for w in mats] + ([] if conv is None else [conv[1], conv[2]])
    sem_ops = list(sems) + ([] if conv is None else list(conv[0]))
    n_op = len(operands)
    res = pl.pallas_call(
        body, name=name, out_shape=tuple(pltpu.HBM(a.shape, a.dtype) for a in operands),
        in_specs=(HBM,) * n_op + (SEM,) * len(sem_ops) + (ANY,), out_specs=(HBM,) * n_op,
        input_output_aliases={i: i for i in range(n_op)},
        compiler_params=pltpu.CompilerParams(has_side_effects=EFFECT),
    )(*operands, *sem_ops, after)
    arrs = list(arrs)
    for w, a in zip(mats, res):
        arrs[w] = a
    return arrs, (res[-1] if conv is not None else None)


def _grad_shard(ref, w, chip, n):
    start = pl.multiple_of(chip * n, 128)
    if SHARD_AXES[w] == 2:
        return ref.at[:, pl.ds(start, n)]
    return ref.at[pl.ds(start, n), :]


def _slot_shape(g, w):
    shape = list(g.shape)
    shape[SHARD_AXES[w] - 1] //= N_CHIPS
    return (N_DEV - 1,) + tuple(shape)


def _scatter_copies(g_ref, land_ref, send_sems, recv_sems, layer, w):
    x, y, c = _mesh_pos()
    n = g_ref.shape[SHARD_AXES[w] - 1] // N_CHIPS
    out = []
    for r in range(1, N_DEV):
        tx, ty, tc = _flip(x, r & 4), _flip(y, r & 2), _flip(c, r & 1)
        cp = pltpu.make_async_remote_copy(
            src_ref=_grad_shard(g_ref, w, 2 * tx + ty, n), dst_ref=land_ref.at[r - 1], send_sem=send_sems.at[r - 1],
            recv_sem=recv_sems.at[r - 1], device_id=(tx, ty, tc), device_id_type=MESH)
        out.append((cp, (c != layer) if r & 1 else (c == layer)))
    return out


def _scatter_start(g, land, layer, w, name):
    def body(g_ref, land_ref, send_sems, recv_sems, g_thru, land_thru, token):
        for cp, mine in _scatter_copies(g_ref, land_ref, send_sems, recv_sems, layer, w):
            @pl.when(mine)
            def _():
                cp.start()
        token[...] = jnp.zeros_like(token)

    return pl.pallas_call(
        body, name=name,
        out_shape=(pltpu.SemaphoreType.DMA((N_DEV - 1,)), pltpu.SemaphoreType.DMA((N_DEV - 1,)),
                   pltpu.HBM(g.shape, g.dtype), pltpu.HBM(land.shape, land.dtype),
                   jax.ShapeDtypeStruct((HALO, 128), F32)),
        in_specs=(HBM, HBM), out_specs=(SEM, SEM, HBM, HBM, pl.BlockSpec(memory_space=pltpu.VMEM)),
        input_output_aliases={0: 2, 1: 3}, compiler_params=pltpu.CompilerParams(has_side_effects=EFFECT),
    )(pltpu.with_memory_space_constraint(g, pltpu.HBM), pltpu.with_memory_space_constraint(land, pltpu.HBM))


def _scatter_wait(started, land, after, w, name):
    def body(g0_ref, g1_ref, land_ref, ss0, rs0, ss1, rs1, after_ref, g0_out, g1_out, land_out):
        c = lax.axis_index("c")
        for layer, g_ref, ss, rs in ((0, g0_ref, ss0, rs0), (1, g1_ref, ss1, rs1)):
            for cp, mine in _scatter_copies(g_ref, land_ref, ss, rs, layer, w):
                @pl.when(mine)
                def _():
                    cp.wait_send()

                @pl.when(c == layer)
                def _():
                    cp.wait_recv()

    (ss0, rs0, g0), (ss1, rs1, g1) = started
    return pl.pallas_call(
        body, name=name,
        out_shape=(pltpu.HBM(g0.shape, g0.dtype), pltpu.HBM(g1.shape, g1.dtype), pltpu.HBM(land.shape, land.dtype)),
        in_specs=(HBM, HBM, HBM, SEM, SEM, SEM, SEM, ANY), out_specs=(HBM, HBM, HBM),
        input_output_aliases={0: 0, 1: 1, 2: 2}, compiler_params=pltpu.CompilerParams(has_side_effects=EFFECT),
    )(g0, g1, land, ss0, rs0, ss1, rs1, after)


def _sum_slots(g0, g1, slots, w, pos_arr, name):
    _, rows, cols = slots.shape
    tr = min(256, rows)
    nr = rows // tr
    if SHARD_AXES[w] == 2:
        own = pl.BlockSpec((tr, cols), lambda i, pos: (i, pos[0]))
    else:
        own = pl.BlockSpec((tr, cols), lambda i, pos: (pos[0] * nr + i, 0))

    def body(pos_ref, own0_ref, own1_ref, s_ref, o_ref):
        acc = jnp.where(pos_ref[1] == 0, own0_ref[...], own1_ref[...]).astype(F32)
        for r in range(N_DEV - 1):
            acc = acc + s_ref[r].astype(F32)
        o_ref[...] = acc

    return pl.pallas_call(
        body, name=name,
        grid_spec=pltpu.PrefetchScalarGridSpec(
            num_scalar_prefetch=1, grid=(nr,),
            in_specs=[own, own, pl.BlockSpec((N_DEV - 1, tr, cols), lambda i, pos: (0, i, 0))],
            out_specs=pl.BlockSpec((tr, cols), lambda i, pos: (i, 0))),
        out_shape=jax.ShapeDtypeStruct((rows, cols), F32), compiler_params=_cparams("parallel"),
    )(pos_arr, g0, g1, slots)


def _swap_layers(halves):
    def body(*refs):
        srcs, dsts = refs[:N_BIG], refs[N_BIG:2 * N_BIG]
        send_sems, recv_sems = refs[2 * N_BIG:]
        x, y, c = _mesh_pos()
        sends = [pltpu.make_async_remote_copy(src_ref=srcs[w], dst_ref=dsts[w], send_sem=send_sems.at[w],
                                              recv_sem=recv_sems.at[w], device_id=(x, y, 1 - c), device_id_type=MESH)
                 for w in range(N_BIG)]
        for cp in sends:
            cp.start()
        for cp in sends:
            cp.wait_recv()
        for cp in sends:
            cp.wait_send()

    return pl.pallas_call(
        body, name="swap_layers", in_specs=[ANY] * N_BIG, out_specs=[ANY] * N_BIG,
        out_shape=[jax.ShapeDtypeStruct(h.shape, h.dtype) for h in halves],
        scratch_shapes=[pltpu.SemaphoreType.DMA((N_BIG,)), pltpu.SemaphoreType.DMA((N_BIG,))],
    )(*halves)


def _adamw_math(w, g, m, v):
    m = ADAM_B1 * m + (1.0 - ADAM_B1) * g
    v = ADAM_B2 * v + (1.0 - ADAM_B2) * jnp.square(g)
    m_hat = m / (1.0 - ADAM_B1 ** ADAM_STEP)
    v_hat = v / (1.0 - ADAM_B2 ** ADAM_STEP)
    delta = -ADAM_LR * (m_hat / (jnp.sqrt(v_hat) + ADAM_EPS) + ADAM_WD * w)
    return delta, m, v


def _adamw(w, g_own, g_other, m, v, pos_arr, name):
    shape = w.shape
    _, rows, cols = shape
    tr = min(256, rows)

    def body(pos_ref, w_ref, own_ref, other_ref, m_ref, v_ref, g_ref, d_ref, m2_ref, v2_ref):
        g = jnp.where(pl.program_id(0) == pos_ref[1], own_ref[...], other_ref[...])
        g_ref[...] = g
        d_ref[...], m2_ref[...], v2_ref[...] = _adamw_math(w_ref[...], g, m_ref[...], v_ref[...])

    full = pl.BlockSpec((None, tr, cols), lambda l, i, pos: (l, i, 0))
    half = pl.BlockSpec((tr, cols), lambda l, i, pos: (i, 0))
    return pl.pallas_call(
        body, name=name,
        grid_spec=pltpu.PrefetchScalarGridSpec(
            num_scalar_prefetch=1, grid=(2, rows // tr),
            in_specs=[full, half, half, full, full], out_specs=[full] * 4),
        out_shape=[jax.ShapeDtypeStruct(shape, F32)] * 4, compiler_params=_cparams("parallel", "parallel"),
    )(pos_arr, w, g_own, g_other, m, v)


def _small_sync(part, w, m, v):
    rows, cols = part.shape

    def body(p_ref, w_ref, m_ref, v_ref, g_ref, d_ref, m2_ref, v2_ref, slots, send_sems, recv_sems):
        x, y, c = _mesh_pos()
        me = 4 * x + 2 * y + c
        slots[me] = p_ref[...]
        sends = []
        for r in range(1, N_DEV):
            to = (_flip(x, r & 4), _flip(y, r & 2), _flip(c, r & 1))
            sends.append(pltpu.make_async_remote_copy(
                src_ref=p_ref, dst_ref=slots.at[me], send_sem=send_sems.at[r - 1], recv_sem=recv_sems.at[r - 1],
                device_id=to, device_id_type=MESH))
        for cp in sends:
            cp.start()
        for cp in sends:
            cp.wait_recv()
        for cp in sends:
            cp.wait_send()
        g = slots[0]
        for i in range(1, N_DEV):
            g = g + slots[i]
        g_ref[...] = g
        d_ref[...], m2_ref[...], v2_ref[...] = _adamw_math(w_ref[...], g, m_ref[...], v_ref[...])

    vm = pl.BlockSpec(memory_space=pltpu.VMEM)
    return pl.pallas_call(
        body, name="small_sync", in_specs=[vm] * 4, out_specs=[vm] * 4,
        out_shape=[jax.ShapeDtypeStruct((rows, cols), F32)] * 4,
        scratch_shapes=[pltpu.VMEM((N_DEV, rows, cols), F32), pltpu.SemaphoreType.DMA((N_DEV - 1,)),
                        pltpu.SemaphoreType.DMA((N_DEV - 1,))],
    )(part, w, m, v)


def _pack_small(d, g_mix, g_group, g_mlp, g_final, conv_full, sinks, scalar):
    def part(rows):
        return jnp.pad(rows, ((0, HALO - rows.shape[0]), (0, d - rows.shape[1])))
    return jnp.concatenate([part(g_mix), part(g_group), part(g_mlp), part(g_final[None]),
                            part(conv_full.reshape(6, CONV_CH)), part(sinks.reshape(2, N_HEADS)),
                            part(scalar.reshape(1, 1))], axis=0)


def _unpack_small(p, dm):
    return (p[0:2, :dm], p[8:10, :MIX_WIDTH], p[16:18, :dm], p[24, :dm], p[32:38, :CONV_CH].reshape(2, 3, CONV_CH),
            p[40:42, :N_HEADS].reshape(2, 2, C_GROUP), p[48, 0])


def kernel(x, w_in, conv_w, sinks, g_mix, g_group, w_o, g_mlp, w_ff_in, w_ff_out, g_final, loss_target, m_w_in, m_conv_w, m_sinks, m_g_mix, m_g_group, m_w_o, m_g_mlp, m_w_ff_in, m_w_ff_out, m_g_final, v_w_in, v_conv_w, v_sinks, v_g_mix, v_g_group, v_w_o, v_g_mlp, v_w_ff_in, v_w_ff_out, v_g_final):
    d = max(x.shape[2], MIX_WIDTH)
    chip = 2 * lax.axis_index("x") + lax.axis_index("y")
    conv_n = conv_w.shape[2]

    pos_arr = jnp.stack([chip, lax.axis_index("c")]).astype(jnp.int32)
    shards = (w_in, w_o, w_ff_in, w_ff_out)
    conv_tile = jnp.pad(conv_w.reshape(6, conv_n), ((0, HALO - 6), (0, 128 - conv_n)))
    placed = [_place_shard(w_in, pos_arr[:1], "place_shard_0"), None, None, None]
    sems_a, placed, conv_thru = _gather_start(
        GATHER_STARTS[0], placed, (conv_tile, lax.empty((N_CHIPS,) + conv_tile.shape, conv_tile.dtype)),
        "gather_start_0")
    for i in range(1, N_BIG):
        placed[i] = _place_shard(shards[i], pos_arr[:1], "place_shard_%d" % i)
    sems_b, placed, _ = _gather_start(GATHER_STARTS[1], placed, None, "gather_start_1")
    full = {"arrs": placed, "conv": None, "sems": list(sems_a[:2]) + list(sems_b)}

    def fetch(stage, layer, after):
        k = 2 * layer + stage
        sems = full["sems"][2 * k:2 * k + 2]
        if k == 0:
            full["arrs"], land = _gather_wait(0, sems, full["arrs"], (sems_a[-2:], *conv_thru), after, "gather_wait_0")
            conv_all = lax.dynamic_update_slice(land, conv_tile[None], (chip, 0, 0))
            full["conv"] = conv_all[:, :6, :conv_n].reshape(N_CHIPS, 2, 3, conv_n).transpose(1, 2, 0, 3).reshape(
                2, 3, CONV_CH)
        else:
            full["arrs"], _ = _gather_wait(k, sems, full["arrs"], None, after, "gather_wait_%d" % k)
        if k == 1:
            sems_c, full["arrs"], _ = _gather_start(GATHER_STARTS[2], full["arrs"], None, "gather_start_2")
            full["sems"] += list(sems_c)
        return (*full["arrs"], full["conv"])

    lands, started = [None] * N_BIG, {}

    def emit(layer, w, g):
        if lands[w] is None:
            lands[w] = lax.empty(_slot_shape(g, w), g.dtype)
        *started[layer, w], lands[w], token = _scatter_start(g, lands[w], layer, w, "scatter_start_%d_%d" % (layer, w))
        return token[0, 0]

    loss_tile, dx, grads, dg_final = _local_step(_to_strips(x[0], placed[0], "to_strips_x"),
                                                 _to_strips(loss_target[0], placed[0], "to_strips_target"), fetch,
                                                 w_ff_in.shape[2] * N_CHIPS,
                                                 sinks, g_mix, g_group, g_mlp, g_final, emit)

    own = []
    for w in range(N_BIG):
        g0, g1, slots = _scatter_wait((started[0, w], started[1, w]), lands[w], dx, w, "scatter_wait_%d" % w)
        own.append(_sum_slots(g0, g1, slots, w, pos_arr, "sum_slots_%d" % w))
    other = _swap_layers(own)

    def both(i):
        return jnp.stack([grads[0][i][0], grads[1][i][0]])
    dconv = jnp.stack([grads[0][0][:3], grads[1][0][:3]])
    dsinks = jnp.stack([grads[0][1][0, ::HEAD_DIM], grads[1][1][0, ::HEAD_DIM]])
    part = _pack_small(d, both(2), both(3), both(4), dg_final[0], dconv, dsinks, loss_tile[0, 0])

    def spread(shard):
        return lax.dynamic_update_slice(jnp.zeros((2, 3, CONV_CH), F32), shard, (0, 0, chip * conv_n))
    zero = jnp.zeros((), F32)
    packs = [_pack_small(d, a, b, c_, e, spread(f), g_, zero) for a, b, c_, e, f, g_ in (
        (g_mix, g_group, g_mlp, g_final, conv_w, sinks),
        (m_g_mix, m_g_group, m_g_mlp, m_g_final, m_conv_w, m_sinks),
        (v_g_mix, v_g_group, v_g_mlp, v_g_final, v_conv_w, v_sinks))]
    small = [_unpack_small(p, x.shape[2]) for p in _small_sync(part, *packs)]

    def shard_of(full):
        return lax.dynamic_slice(full, (0, 0, chip * conv_n), (2, 3, conv_n))
    small = [(s[0], s[1], s[2], s[3], shard_of(s[4]), s[5], s[6]) for s in small]
    loss = small[0][6]

    big = [_adamw(w, own[i], other[i], m, v, pos_arr, "adamw_%d" % i) for i, (w, m, v) in enumerate((
        (w_in, m_w_in, v_w_in), (w_o, m_w_o, v_w_o), (w_ff_in, m_w_ff_in, v_w_ff_in),
        (w_ff_out, m_w_ff_out, v_w_ff_out)))]

    def ordered(kind):
        b = [big[i][kind] for i in range(N_BIG)]
        s = small[kind]
        return [b[0], s[4], s[5], s[0], s[1], b[1], s[2], b[2], b[3], s[3]]

    return (loss, _from_strips(dx, "from_strips_dx")[None], *ordered(0), *ordered(1), *ordered(2), *ordered(3))
```

```python
import functools

import jax
import jax.numpy as jnp
from jax import lax
from jax.experimental import pallas as pl
from jax.experimental.pallas import tpu as pltpu

HEAD_DIM = 64
N_HEADS = 6
C_GROUP = 3
A_WIDTH = N_HEADS * HEAD_DIM
C_KV_WIDTH = 2 * HEAD_DIM
CONV_CH = 256
ZA_W = 3 * A_WIDTH
ZB_W = 3 * CONV_CH
ZC_W = A_WIDTH + 2 * C_KV_WIDTH
IN_WIDTH = ZA_W + ZB_W + ZC_W
MIX_WIDTH = A_WIDTH + CONV_CH + A_WIDTH
DILATIONS = (1, 4, 16)
A_MAX_DIST = 128
C_MAX_DIST = 127
TQ = 128
EPS = 1e-6
SCALE = HEAD_DIM ** -0.5
NEG = -1e30
HALO = 8

ADAM_LR = 0.001
ADAM_B1 = 0.9
ADAM_B2 = 0.999
ADAM_EPS = 1e-08
ADAM_WD = 0.01
ADAM_STEP = 10

BF = jnp.bfloat16
F32 = jnp.float32
MESH = pl.DeviceIdType.MESH
VMEM_LIMIT = 56 * 1024 * 1024


def _cparams(*sem):
    return pltpu.CompilerParams(dimension_semantics=sem, vmem_limit_bytes=VMEM_LIMIT)


def _nt(a, b):
    return lax.dot_general(a, b, (((1,), (1,)), ((), ())), preferred_element_type=F32)


def _tn(a, b):
    return lax.dot_general(a, b, (((0,), (0,)), ((), ())), preferred_element_type=F32)


def _nn(a, b):
    return jnp.dot(a, b, preferred_element_type=F32)


def _rows(tb, w):
    return pl.BlockSpec((tb, w), lambda i: (i, 0))


def _whole(shape):
    return pl.BlockSpec(shape, lambda *_: (0,) * len(shape))


def _layer(shape, l):
    return pl.BlockSpec((None,) + shape, lambda *_: (l,) + (0,) * len(shape))


def _rms_scale(v):
    return lax.rsqrt(jnp.mean(v * v, axis=-1, keepdims=True) + EPS)


def _norm_bwd(dxhat, xhat, r):
    return r * (dxhat - xhat * jnp.mean(dxhat * xhat, axis=-1, keepdims=True))


def _qkv_fwd(x, g, w_all, l, tb):
    s, d = x.shape

    def body(x_ref, g_ref, w_ref, h_ref, za_ref, zb_ref, zc_ref):
        xv = x_ref[...]
        h = ((xv * _rms_scale(xv)) * g_ref[...]).astype(BF)
        h_ref[...] = h
        z = jnp.concatenate([_nn(h, w_ref[k]) for k in range(N_CHIPS)], axis=1)
        za_ref[...] = z[:, :ZA_W]
        zb_ref[...] = z[:, ZA_W:ZA_W + ZB_W]
        zc_ref[...] = z[:, ZA_W + ZB_W:]

    return pl.pallas_call(
        body, grid=(s // tb,), name="qkv_fwd",
        in_specs=[_rows(tb, d), _whole((1, d)), _layer((N_CHIPS, d, IN_WIDTH // N_CHIPS), l)],
        out_specs=[_rows(tb, d), _rows(tb, ZA_W), _rows(tb, ZB_W), _rows(tb, ZC_W)],
        out_shape=[jax.ShapeDtypeStruct((s, d), BF), jax.ShapeDtypeStruct((s, ZA_W), F32),
                   jax.ShapeDtypeStruct((s, ZB_W), F32), jax.ShapeDtypeStruct((s, ZC_W), F32)],
        compiler_params=_cparams("parallel"),
    )(x, g, w_all)


N_STRIPS = 16


def _strips(a):
    s, w = a.shape
    return a.reshape(4, 4, s // N_STRIPS, w)


def _p_grid(s, dil):
    na = s // N_STRIPS
    return {16: (4, 4, na // TQ), 4: (4, na // 32), 1: (na // 8,)}[dil]


def _p_spec(dil, cw, col, prev=False):
    def blk(j):
        return jnp.maximum(j - 1, 0) if prev else j
    if dil == 16:
        return pl.BlockSpec((None, None, TQ, cw), lambda f, e, j: (f, e, blk(j), col))
    if dil == 4:
        return pl.BlockSpec((None, 4, 32, cw), lambda f, j: (f, 0, blk(j), col))
    return pl.BlockSpec((4, 4, 8, cw), lambda j: (0, 0, blk(j), col))


def _block_pos(i, dil):
    if dil == 16:
        return i
    if dil == 4:
        return 4 * (i % 32) + i // 32
    return 16 * (i % 8) + 4 * ((i // 8) % 4) + i // 32


def _band_mask(b, dil, max_dist):
    qi = _block_pos(lax.broadcasted_iota(jnp.int32, (TQ, 2 * TQ), 0), dil)
    col = lax.broadcasted_iota(jnp.int32, (TQ, 2 * TQ), 1)
    cur = col >= TQ
    dist = qi - _block_pos(col % TQ, dil) + jnp.where(cur, 0, TQ)
    return (dist >= 0) & (dist <= max_dist) & (cur | (b > 0))


def _hs(h):
    return slice(h * HEAD_DIM, (h + 1) * HEAD_DIM)


def _ld(ref, cols):
    v = ref[..., cols]
    return v.reshape(TQ, v.shape[-1])


def _st(ref, cols, val):
    ref[..., cols] = val.reshape(ref.shape[:-1] + (val.shape[-1],))


def _attn_fwd(z, dil, kw, kcol, vcol, n_rep, max_dist, name):
    s, zw = z.shape
    grid = _p_grid(s, dil)

    def body(q_ref, kp_ref, kc_ref, vp_ref, vc_ref, acc_ref, m_ref, l_ref):
        mask = _band_mask(pl.program_id(len(grid) - 1), dil, max_dist)
        for kh in range(N_HEADS // n_rep):
            k2 = jnp.concatenate([_ld(kp_ref, _hs(kh)), _ld(kc_ref, _hs(kh))], axis=0).astype(BF)
            v2 = jnp.concatenate([_ld(vp_ref, _hs(kh)), _ld(vc_ref, _hs(kh))], axis=0).astype(BF)
            for h in range(kh * n_rep, (kh + 1) * n_rep):
                q = _ld(q_ref, _hs(h)).astype(BF)
                sc = jnp.where(mask, _nt(q, k2) * SCALE, NEG)
                m = jnp.max(sc, axis=1, keepdims=True)
                p = jnp.exp(sc - m)
                _st(acc_ref, _hs(h), _nn(p.astype(BF), v2))
                _st(m_ref, _hs(h), jnp.broadcast_to(m, (TQ, HEAD_DIM)))
                _st(l_ref, _hs(h), jnp.broadcast_to(jnp.sum(p, axis=1, keepdims=True), (TQ, HEAD_DIM)))

    res = pl.pallas_call(
        body, grid=grid, name=name,
        in_specs=[_p_spec(dil, A_WIDTH, 0), _p_spec(dil, kw, kcol, True), _p_spec(dil, kw, kcol),
                  _p_spec(dil, kw, vcol, True), _p_spec(dil, kw, vcol)],
        out_specs=[_p_spec(dil, A_WIDTH, 0)] * 3,
        out_shape=[jax.ShapeDtypeStruct((4, 4, s // N_STRIPS, A_WIDTH), F32)] * 3,
        compiler_params=_cparams(*(("parallel",) * len(grid))),
    )(*[_strips(z)] * 5)
    return [a.reshape(s, A_WIDTH) for a in res]


def _attn_merge(parts_a, part_c, sink_row, tb):
    s = part_c[0].shape[0]
    n_a = len(parts_a)

    def body(*refs):
        ins, sink_ref = refs[:3 * n_a + 3], refs[3 * n_a + 3]
        ya_ref, lsea_ref, yc_ref, lsec_ref = refs[3 * n_a + 4:]
        ms = [ins[3 * p + 1][...] for p in range(n_a)]
        m = functools.reduce(jnp.maximum, ms)
        acc = jnp.zeros_like(m)
        l = jnp.zeros_like(m)
        for p in range(n_a):
            w = jnp.exp(ms[p] - m)
            acc = acc + w * ins[3 * p][...]
            l = l + w * ins[3 * p + 2][...]
        ya_ref[...] = acc / l
        lsea_ref[...] = m + jnp.log(l)
        acc_c, m_c, l_c = [r[...] for r in ins[3 * n_a:]]
        sk = sink_ref[...]
        m2 = jnp.maximum(m_c, sk)
        w = jnp.exp(m_c - m2)
        l2 = w * l_c + jnp.exp(sk - m2)
        yc_ref[...] = (w * acc_c) / l2
        lsec_ref[...] = m2 + jnp.log(l2)

    return pl.pallas_call(
        body, grid=(s // tb,), name="attn_merge",
        in_specs=[_rows(tb, A_WIDTH)] * (3 * n_a + 3) + [_whole((1, A_WIDTH))],
        out_specs=[_rows(tb, A_WIDTH)] * 4, out_shape=[jax.ShapeDtypeStruct((s, A_WIDTH), F32)] * 4,
        compiler_params=_cparams("parallel"),
    )(*[a for part in parts_a + [part_c] for a in part], sink_row)


def _shift_down(v, n, halo):
    rows = v.shape[0]
    out = pltpu.roll(v, n, 0)
    row = lax.broadcasted_iota(jnp.int32, v.shape, 0)
    for t in range(n):
        out = jnp.where(row == t, halo[HALO - n + t:HALO - n + t + 1, :], out)
    return out


def _shift_up(v, n, halo):
    rows = v.shape[0]
    out = pltpu.roll(v, rows - n, 0)
    row = lax.broadcasted_iota(jnp.int32, v.shape, 0)
    for t in range(n):
        out = jnp.where(row == rows - n + t, halo[t:t + 1, :], out)
    return out


def _strip(v, b):
    return v[b % 4, b // 4]


def _conv_strips(zb, prev, cw):
    gb = [_strip(zb, b)[:, :CONV_CH] for b in range(N_STRIPS)]
    gc = [_strip(zb, b)[:, CONV_CH:2 * CONV_CH] for b in range(N_STRIPS)]
    xb = [_strip(zb, b)[:, 2 * CONV_CH:] for b in range(N_STRIPS)]
    u = [g * v for g, v in zip(gc, xb)]
    uh = prev[:, :, CONV_CH:2 * CONV_CH] * prev[:, :, 2 * CONV_CH:]
    wrapped = {14: _shift_down(u[14], 1, uh[2]), 15: _shift_down(u[15], 1, uh[3])}
    u1 = [u[b - 1] if b >= 1 else wrapped[15] for b in range(N_STRIPS)]
    u2 = [u[b - 2] if b >= 2 else wrapped[14 + b] for b in range(N_STRIPS)]
    c = [cw[0:1, :] * u2[b] + cw[1:2, :] * u1[b] + cw[2:3, :] * u[b] for b in range(N_STRIPS)]
    return gb, gc, xb, u, u1, u2, c


def _strip_rows(ta, w):
    return pl.BlockSpec((4, 4, ta, w), lambda i: (0, 0, i, 0))


def _prev_rows(ta, w):
    return pl.BlockSpec((4, None, HALO, w), lambda i: (0, 3, jnp.maximum(i * (ta // HALO) - 1, 0), 0))


def _next_rows(ta, w, nblk):
    return pl.BlockSpec((4, None, HALO, w),
                        lambda i: (0, 0, jnp.minimum((i + 1) * (ta // HALO), nblk * (ta // HALO) - 1), 0))


def _mix_fwd(x, ya, yc, zb, cw, gg, wo_all, l, tb):
    s, d = x.shape
    ta = tb // N_STRIPS

    def body(x_ref, ya_ref, yc_ref, zb_ref, zbp_ref, cw_ref, gg_ref, wo_ref, x1_ref, yb_ref):
        i = pl.program_id(0)
        prev = jnp.where(i > 0, zbp_ref[...], 0.0)
        gb, _, _, _, _, _, c = _conv_strips(zb_ref[...], prev, cw_ref[...])
        for b in range(N_STRIPS):
            yb_ref[b % 4, b // 4] = gb[b] * c[b]
        yb = yb_ref[...].reshape(tb, CONV_CH)
        ya, yc = ya_ref[...].reshape(tb, A_WIDTH), yc_ref[...].reshape(tb, A_WIDTH)
        n = jnp.concatenate([ya * _rms_scale(ya), yb * _rms_scale(yb), yc * _rms_scale(yc)], axis=1)
        n = (n * gg_ref[...]).astype(BF)
        x1 = x_ref[...].reshape(tb, d) + _nn(n, wo_ref[...].reshape(MIX_WIDTH, d))
        x1_ref[...] = x1.reshape(4, 4, ta, d)

    res = pl.pallas_call(
        body, grid=(s // tb,), name="mix_fwd",
        in_specs=[_strip_rows(ta, d), _strip_rows(ta, A_WIDTH), _strip_rows(ta, A_WIDTH), _strip_rows(ta, ZB_W),
                  _prev_rows(ta, ZB_W), _whole((HALO, CONV_CH)), _whole((1, MIX_WIDTH)),
                  _layer((N_CHIPS, MIX_WIDTH // N_CHIPS, d), l)],
        out_specs=[_strip_rows(ta, d), _strip_rows(ta, CONV_CH)],
        out_shape=[jax.ShapeDtypeStruct((4, 4, s // N_STRIPS, d), F32),
                   jax.ShapeDtypeStruct((4, 4, s // N_STRIPS, CONV_CH), F32)],
        compiler_params=_cparams("parallel"),
    )(_strips(x), _strips(ya), _strips(yc), _strips(zb), _strips(zb), cw, gg, wo_all)
    return res[0].reshape(s, d), res[1].reshape(s, CONV_CH)


def _mlp_fwd(x1, g, w1_all, w2_all, l, tb, tf):
    s, d = x1.shape
    ff = w1_all.shape[1] * w1_all.shape[3]
    nj = ff // tf

    def body(x_ref, g_ref, w1_ref, w2_ref, x2_ref, h2_ref, ap_ref, acc):
        j = pl.program_id(1)

        @pl.when(j == 0)
        def _():
            xv = x_ref[...]
            h2_ref[...] = ((xv * _rms_scale(xv)) * g_ref[...]).astype(BF)
            acc[...] = jnp.zeros_like(acc)

        ap = _nn(h2_ref[...], w1_ref[...])
        ap_ref[...] = ap.astype(BF)
        a = jnp.square(jnp.maximum(ap, 0.0)).astype(BF)
        acc[...] += _nn(a, w2_ref[...])

        @pl.when(j == nj - 1)
        def _():
            x2_ref[...] = x_ref[...] + acc[...]

    return pl.pallas_call(
        body, grid=(s // tb, nj), name="mlp_fwd",
        in_specs=[pl.BlockSpec((tb, d), lambda i, j: (i, 0)), _whole((1, d)),
                  pl.BlockSpec((None, None, d, tf), lambda i, j: (l, j, 0, 0)),
                  pl.BlockSpec((None, None, tf, d), lambda i, j: (l, j, 0, 0))],
        out_specs=[pl.BlockSpec((tb, d), lambda i, j: (i, 0)), pl.BlockSpec((tb, d), lambda i, j: (i, 0)),
                   pl.BlockSpec((tb, tf), lambda i, j: (i, j))],
        out_shape=[jax.ShapeDtypeStruct((s, d), F32), jax.ShapeDtypeStruct((s, d), BF),
                   jax.ShapeDtypeStruct((s, ff), BF)],
        scratch_shapes=[pltpu.VMEM((tb, d), F32)],
        compiler_params=_cparams("parallel", "arbitrary"),
    )(x1, g, w1_all, w2_all)


def _loss_head(x, g, tgt, tb):
    s, d = x.shape

    def body(x_ref, g_ref, t_ref, dx_ref, loss_ref, dg_ref):
        i = pl.program_id(0)

        @pl.when(i == 0)
        def _():
            loss_ref[...] = jnp.zeros_like(loss_ref)
            dg_ref[...] = jnp.zeros_like(dg_ref)

        xv = x_ref[...]
        r = _rms_scale(xv)
        xhat = xv * r
        err = xhat * g_ref[...] - t_ref[...]
        part = jnp.sum(jnp.mean(jnp.square(err), axis=-1, keepdims=True), axis=0, keepdims=True)
        loss_ref[...] += 0.5 * part
        dy = err * (1.0 / d)
        dg_ref[...] += jnp.sum(dy * xhat, axis=0, keepdims=True)
        dx_ref[...] = _norm_bwd(dy * g_ref[...], xhat, r)

    return pl.pallas_call(
        body, grid=(s // tb,), name="loss_head",
        in_specs=[_rows(tb, d), _whole((1, d)), _rows(tb, d)],
        out_specs=[_rows(tb, d), _whole((HALO, 128)), _whole((HALO, d))],
        out_shape=[jax.ShapeDtypeStruct((s, d), F32), jax.ShapeDtypeStruct((HALO, 128), F32),
                   jax.ShapeDtypeStruct((HALO, d), F32)],
        compiler_params=_cparams("arbitrary"),
    )(x, g, tgt)


def _mlp_bwd(dx2, x1, ap, g, w1_all, w2_all, l, tb, tf):
    s, d = x1.shape
    ff = ap.shape[1]
    nj = ff // tf

    def body(dx2_ref, x1_ref, ap_ref, g_ref, w1_ref, w2_ref, dx1_ref, dap_ref, dg_ref, acc):
        i, j = pl.program_id(0), pl.program_id(1)

        @pl.when((i == 0) & (j == 0))
        def _():
            dg_ref[...] = jnp.zeros_like(dg_ref)

        @pl.when(j == 0)
        def _():
            acc[...] = jnp.zeros_like(acc)

        da = _nt(dx2_ref[...].astype(BF), w2_ref[...])
        dap = (da * (2.0 * jnp.maximum(ap_ref[...].astype(F32), 0.0))).astype(BF)
        dap_ref[...] = dap
        acc[...] += _nt(dap, w1_ref[...])

        @pl.when(j == nj - 1)
        def _():
            xv = x1_ref[...]
            r = _rms_scale(xv)
            xhat = xv * r
            dh = acc[...]
            dg_ref[...] += jnp.sum(dh * xhat, axis=0, keepdims=True)
            dx1_ref[...] = dx2_ref[...] + _norm_bwd(dh * g_ref[...], xhat, r)

    return pl.pallas_call(
        body, grid=(s // tb, nj), name="mlp_bwd",
        in_specs=[pl.BlockSpec((tb, d), lambda i, j: (i, 0)), pl.BlockSpec((tb, d), lambda i, j: (i, 0)),
                  pl.BlockSpec((tb, tf), lambda i, j: (i, j)),
                  _whole((1, d)), pl.BlockSpec((None, None, d, tf), lambda i, j: (l, j, 0, 0)),
                  pl.BlockSpec((None, None, tf, d), lambda i, j: (l, j, 0, 0))],
        out_specs=[pl.BlockSpec((tb, d), lambda i, j: (i, 0)), pl.BlockSpec((tb, tf), lambda i, j: (i, j)),
                   _whole((HALO, d))],
        out_shape=[jax.ShapeDtypeStruct((s, d), F32), jax.ShapeDtypeStruct((s, ff), BF),
                   jax.ShapeDtypeStruct((HALO, d), F32)],
        scratch_shapes=[pltpu.VMEM((tb, d), F32)],
        compiler_params=_cparams("arbitrary", "arbitrary"),
    )(dx2, x1, ap, g, w1_all, w2_all)


def _wgrad(a, b, tm, tn, ts, name, relu2=False):
    s, m = a.shape
    n = b.shape[1]
    ns = s // ts

    def body(a_ref, b_ref, o_ref, acc):
        k = pl.program_id(2)

        @pl.when(k == 0)
        def _():
            acc[...] = jnp.zeros_like(acc)

        av = a_ref[...]
        if relu2:
            av = jnp.square(jnp.maximum(av.astype(F32), 0.0)).astype(BF)
        acc[...] += _tn(av, b_ref[...].astype(BF))

        @pl.when(k == ns - 1)
        def _():
            o_ref[...] = acc[...].astype(BF)

    return pl.pallas_call(
        body, grid=(m // tm, n // tn, ns), name=name,
        in_specs=[pl.BlockSpec((ts, tm), lambda i, j, k: (k, i)), pl.BlockSpec((ts, tn), lambda i, j, k: (k, j))],
        out_specs=pl.BlockSpec((tm, tn), lambda i, j, k: (i, j)),
        out_shape=jax.ShapeDtypeStruct((m, n), BF),
        scratch_shapes=[pltpu.VMEM((tm, tn), F32)],
        compiler_params=_cparams("parallel", "parallel", "arbitrary"),
    )(a, b)


def _mix_bwd(dx1, ya, yb, yc, lse_c, sink_row, gg, wo_all, l, tb):
    s, d = dx1.shape

    def body(dx_ref, ya_ref, yb_ref, yc_ref, lse_ref, sink_ref, gg_ref, wo_ref,
             n_ref, dya_ref, dyc_ref, da_ref, dc_ref, dyb_ref, dg_ref, dsink_ref):
        i = pl.program_id(0)

        @pl.when(i == 0)
        def _():
            dg_ref[...] = jnp.zeros_like(dg_ref)
            dsink_ref[...] = jnp.zeros_like(dsink_ref)

        dn = _nt(dx_ref[...].astype(BF), wo_ref[...].reshape(MIX_WIDTH, d))
        ys = [ya_ref[...], yb_ref[...], yc_ref[...]]
        rs = [_rms_scale(v) for v in ys]
        nhat = jnp.concatenate([v * r for v, r in zip(ys, rs)], axis=1)
        gg = gg_ref[...]
        n_ref[...] = (nhat * gg).astype(BF)
        dg_ref[...] += jnp.sum(dn * nhat, axis=0, keepdims=True)
        dnh = dn * gg
        bounds = [(0, A_WIDTH), (A_WIDTH, A_WIDTH + CONV_CH), (A_WIDTH + CONV_CH, MIX_WIDTH)]
        dys = [_norm_bwd(dnh[:, lo:hi], nhat[:, lo:hi], r) for (lo, hi), r in zip(bounds, rs)]
        dyb_ref[...] = dys[1]
        for dy, y, dy_ref, dd_ref in ((dys[0], ys[0], dya_ref, da_ref), (dys[2], ys[2], dyc_ref, dc_ref)):
            dy_ref[...] = dy
            t = dy * y
            for h in range(N_HEADS):
                dd_ref[:, _hs(h)] = jnp.broadcast_to(jnp.sum(t[:, _hs(h)], axis=1, keepdims=True), (tb, HEAD_DIM))
        dsink_ref[...] -= jnp.sum(jnp.exp(sink_ref[...] - lse_ref[...]) * dc_ref[...], axis=0, keepdims=True)

    return pl.pallas_call(
        body, grid=(s // tb,), name="mix_bwd",
        in_specs=[_rows(tb, d), _rows(tb, A_WIDTH), _rows(tb, CONV_CH), _rows(tb, A_WIDTH), _rows(tb, A_WIDTH),
                  _whole((1, A_WIDTH)), _whole((1, MIX_WIDTH)), _layer((N_CHIPS, MIX_WIDTH // N_CHIPS, d), l)],
        out_specs=[_rows(tb, MIX_WIDTH), _rows(tb, A_WIDTH), _rows(tb, A_WIDTH), _rows(tb, A_WIDTH),
                   _rows(tb, A_WIDTH), _rows(tb, CONV_CH), _whole((HALO, MIX_WIDTH)), _whole((HALO, A_WIDTH))],
        out_shape=[jax.ShapeDtypeStruct((s, MIX_WIDTH), BF), jax.ShapeDtypeStruct((s, A_WIDTH), F32),
                   jax.ShapeDtypeStruct((s, A_WIDTH), F32), jax.ShapeDtypeStruct((s, A_WIDTH), F32),
                   jax.ShapeDtypeStruct((s, A_WIDTH), F32), jax.ShapeDtypeStruct((s, CONV_CH), F32),
                   jax.ShapeDtypeStruct((HALO, MIX_WIDTH), F32), jax.ShapeDtypeStruct((HALO, A_WIDTH), F32)],
        compiler_params=_cparams("arbitrary"),
    )(dx1, ya, yb, yc, lse_c, sink_row, gg, wo_all)


def _attn_bwd(z, dy, lse, dd, dil, kw, kcol, vcol, n_rep, max_dist, name):
    s, zw = z.shape
    grid = _p_grid(s, dil)
    n_kv = N_HEADS // n_rep

    def body(q_ref, kp_ref, kc_ref, vp_ref, vc_ref, dy_ref, lse_ref, dd_ref, dq_ref, dkp_ref, dkc_ref, dvp_ref, dvc_ref):
        mask = _band_mask(pl.program_id(len(grid) - 1), dil, max_dist)
        for kh in range(n_kv):
            k2 = jnp.concatenate([_ld(kp_ref, _hs(kh)), _ld(kc_ref, _hs(kh))], axis=0).astype(BF)
            v2 = jnp.concatenate([_ld(vp_ref, _hs(kh)), _ld(vc_ref, _hs(kh))], axis=0).astype(BF)
            dk2 = jnp.zeros((2 * TQ, HEAD_DIM), F32)
            dv2 = jnp.zeros((2 * TQ, HEAD_DIM), F32)
            for h in range(kh * n_rep, (kh + 1) * n_rep):
                q = _ld(q_ref, _hs(h)).astype(BF)
                lse_h = _ld(lse_ref, slice(h * HEAD_DIM, h * HEAD_DIM + 1))
                dd_h = _ld(dd_ref, slice(h * HEAD_DIM, h * HEAD_DIM + 1))
                dyh = _ld(dy_ref, _hs(h)).astype(BF)
                sc = jnp.where(mask, _nt(q, k2) * SCALE, NEG)
                p = jnp.exp(sc - lse_h)
                dp = _nt(dyh, v2)
                ds = ((p * (dp - dd_h)) * SCALE).astype(BF)
                _st(dq_ref, _hs(h), _nn(ds, k2))
                dk2 = dk2 + _tn(ds, q)
                dv2 = dv2 + _tn(p.astype(BF), dyh)
            _st(dkp_ref, _hs(kh), dk2[:TQ])
            _st(dkc_ref, _hs(kh), dk2[TQ:])
            _st(dvp_ref, _hs(kh), dv2[:TQ])
            _st(dvc_ref, _hs(kh), dv2[TQ:])

    args = [_strips(z)] * 5 + [_strips(a) for a in (dy, lse, dd)]
    in_specs = [_p_spec(dil, A_WIDTH, 0), _p_spec(dil, kw, kcol, True), _p_spec(dil, kw, kcol),
                _p_spec(dil, kw, vcol, True), _p_spec(dil, kw, vcol)] + [_p_spec(dil, A_WIDTH, 0)] * 3
    out_specs = [_p_spec(dil, A_WIDTH, 0)] + [_p_spec(dil, kw, 0)] * 4
    na = s // N_STRIPS
    out_shape = [jax.ShapeDtypeStruct((4, 4, na, A_WIDTH), F32)] + [jax.ShapeDtypeStruct((4, 4, na, kw), F32)] * 4
    res = pl.pallas_call(
        body, grid=grid, name=name, in_specs=in_specs, out_specs=out_specs, out_shape=out_shape,
        compiler_params=_cparams(*(("parallel",) * len(grid))),
    )(*args)
    return [res[0].reshape(s, A_WIDTH)] + [a.reshape(s, kw) for a in res[1:]]


DZ_TA = 16


def _dz_assemble(parts_a, parts_c, dyb, zb, cw):
    s = zb.shape[0]
    na = s // N_STRIPS
    nb = na // DZ_TA

    def ahead(w, k):
        return pl.BlockSpec((4, 4, DZ_TA, w), lambda i: (0, 0, jnp.minimum(i + k, nb - 1), 0))

    args, in_specs = [], []
    for dil, (dq, dkp, dkc, dvp, dvc) in zip(DILATIONS + (1,), parts_a + [parts_c]):
        w = dkp.shape[1]
        here = _strip_rows(DZ_TA, w)
        if dil == 1:
            args += [dq, dkp, dkp, dkc, dvp, dvp, dvc]
            in_specs += [_strip_rows(DZ_TA, A_WIDTH), here, ahead(w, 1), here, here, ahead(w, 1), here]
        else:
            k = 8 * dil // DZ_TA
            args += [dq, dkp, dkc, dvp, dvc]
            in_specs += [_strip_rows(DZ_TA, A_WIDTH), ahead(w, k), here, ahead(w, k), here]
    n_att = len(args)
    args = [_strips(a) for a in args] + [_strips(dyb), _strips(dyb), _strips(zb), _strips(zb), _strips(zb), cw]
    in_specs += [_strip_rows(DZ_TA, CONV_CH), _next_rows(DZ_TA, CONV_CH, nb), _strip_rows(DZ_TA, ZB_W),
                 _prev_rows(DZ_TA, ZB_W), _next_rows(DZ_TA, ZB_W, nb), _whole((HALO, CONV_CH))]

    def body(*refs):
        att = list(refs[:n_att])
        dyb_ref, dybn_ref, zb_ref, zbp_ref, zbn_ref, cw_ref, dz_ref, dcw_ref = refs[n_att:]
        i = pl.program_id(0)

        @pl.when(i == 0)
        def _():
            dcw_ref[...] = jnp.zeros_like(dcw_ref)

        def shifted(dil):
            if dil == 1:
                dq_r, kp0, kp1, dkc_r, vp0, vp1, dvc_r = [att.pop(0) for _ in range(7)]
                live = i + 1 < nb
                half = DZ_TA // 2
                dkp = jnp.concatenate([kp0[:, :, half:, :], jnp.where(live, kp1[:, :, :half, :], 0.0)], axis=2)
                dvp = jnp.concatenate([vp0[:, :, half:, :], jnp.where(live, vp1[:, :, :half, :], 0.0)], axis=2)
            else:
                dq_r, dkp_r, dkc_r, dvp_r, dvc_r = [att.pop(0) for _ in range(5)]
                live = i + 8 * dil // DZ_TA < nb
                dkp, dvp = jnp.where(live, dkp_r[...], 0.0), jnp.where(live, dvp_r[...], 0.0)
            return dq_r[...], dkc_r[...] + dkp, dvc_r[...] + dvp

        dq, dk, dv = shifted(DILATIONS[0])
        for dil in DILATIONS[1:]:
            dq2, dk2, dv2 = shifted(dil)
            dq, dk, dv = dq + dq2, dk + dk2, dv + dv2
        dz_ref[:, :, :, 0:A_WIDTH] = dq.astype(BF)
        dz_ref[:, :, :, A_WIDTH:2 * A_WIDTH] = dk.astype(BF)
        dz_ref[:, :, :, 2 * A_WIDTH:ZA_W] = dv.astype(BF)
        dq, dk, dv = shifted(1)
        c0 = ZA_W + ZB_W
        dz_ref[:, :, :, c0:c0 + A_WIDTH] = dq.astype(BF)
        dz_ref[:, :, :, c0 + A_WIDTH:c0 + A_WIDTH + C_KV_WIDTH] = dk.astype(BF)
        dz_ref[:, :, :, c0 + A_WIDTH + C_KV_WIDTH:IN_WIDTH] = dv.astype(BF)

        cw = cw_ref[...]
        prev = jnp.where(i > 0, zbp_ref[...], 0.0)
        gb, gc, xb, u, u1, u2, c = _conv_strips(zb_ref[...], prev, cw)
        dyb = dyb_ref[...]
        dc = [_strip(dyb, b) * gb[b] for b in range(N_STRIPS)]
        dcn = jnp.where(i + 1 < nb, dybn_ref[...] * zbn_ref[:, :, :CONV_CH], 0.0)
        wrapped = [_shift_up(dc[0], 1, dcn[0]), _shift_up(dc[1], 1, dcn[1])]
        upd = [jnp.zeros((1, CONV_CH), F32)] * 3
        for b in range(N_STRIPS):
            dc1 = dc[b + 1] if b + 1 < N_STRIPS else wrapped[0]
            dc2 = dc[b + 2] if b + 2 < N_STRIPS else wrapped[b + 2 - N_STRIPS]
            du = cw[2:3, :] * dc[b] + cw[1:2, :] * dc1 + cw[0:1, :] * dc2
            f, e = b % 4, b // 4
            dz_ref[f, e, :, ZA_W:ZA_W + CONV_CH] = (_strip(dyb, b) * c[b]).astype(BF)
            dz_ref[f, e, :, ZA_W + CONV_CH:ZA_W + 2 * CONV_CH] = (du * xb[b]).astype(BF)
            dz_ref[f, e, :, ZA_W + 2 * CONV_CH:c0] = (du * gc[b]).astype(BF)
            for t, uu in enumerate((u2[b], u1[b], u[b])):
                upd[t] = upd[t] + jnp.sum(dc[b] * uu, axis=0, keepdims=True)
        row = lax.broadcasted_iota(jnp.int32, (HALO, CONV_CH), 0)
        tile = jnp.zeros((HALO, CONV_CH), F32)
        for t in range(3):
            tile = jnp.where(row == t, upd[t], tile)
        dcw_ref[...] += tile

    dz, dcw = pl.pallas_call(
        body, grid=(nb,), name="dz_assemble", in_specs=in_specs,
        out_specs=[_strip_rows(DZ_TA, IN_WIDTH), _whole((HALO, CONV_CH))],
        out_shape=[jax.ShapeDtypeStruct((4, 4, na, IN_WIDTH), BF), jax.ShapeDtypeStruct((HALO, CONV_CH), F32)],
        compiler_params=_cparams("arbitrary"),
    )(*args)
    return dz.reshape(s, IN_WIDTH), dcw


def _qkv_bwd(dz, dx1, x, g, w_all, l, tb):
    s, d = x.shape

    def body(dz_ref, dx1_ref, x_ref, g_ref, w_ref, dx_ref, dg_ref):
        i = pl.program_id(0)

        @pl.when(i == 0)
        def _():
            dg_ref[...] = jnp.zeros_like(dg_ref)

        n = IN_WIDTH // N_CHIPS
        dh = _nt(dz_ref[:, 0:n], w_ref[0])
        for k in range(1, N_CHIPS):
            dh = dh + _nt(dz_ref[:, k * n:(k + 1) * n], w_ref[k])
        xv = x_ref[...]
        r = _rms_scale(xv)
        xhat = xv * r
        dg_ref[...] += jnp.sum(dh * xhat, axis=0, keepdims=True)
        dx_ref[...] = dx1_ref[...] + _norm_bwd(dh * g_ref[...], xhat, r)

    return pl.pallas_call(
        body, grid=(s // tb,), name="qkv_bwd",
        in_specs=[_rows(tb, IN_WIDTH), _rows(tb, d), _rows(tb, d), _whole((1, d)),
                  _layer((N_CHIPS, d, IN_WIDTH // N_CHIPS), l)],
        out_specs=[_rows(tb, d), _whole((HALO, d))],
        out_shape=[jax.ShapeDtypeStruct((s, d), F32), jax.ShapeDtypeStruct((HALO, d), F32)],
        compiler_params=_cparams("arbitrary"),
    )(dz, dx1, x, g, w_all)


def _tile_rows(rows):
    return jnp.pad(rows, ((0, HALO - rows.shape[0]), (0, 0)))


def _to_strips(a, after, name):
    s, d = a.shape
    na = s // N_STRIPS
    ta = min(32, na)

    def body(a_ref, after_ref, o_ref):
        for b in range(N_STRIPS):
            o_ref[b % 4, b // 4] = a_ref[:, b, :]

    return pl.pallas_call(
        body, grid=(na // ta,), name=name,
        in_specs=[pl.BlockSpec((ta, N_STRIPS, d), lambda i: (i, 0, 0)), ANY], out_specs=_strip_rows(ta, d),
        out_shape=jax.ShapeDtypeStruct((4, 4, na, d), a.dtype), compiler_params=_cparams("parallel"),
    )(a.reshape(na, N_STRIPS, d), after).reshape(s, d)


def _from_strips(a, name):
    s, d = a.shape
    na = s // N_STRIPS
    ta = min(32, na)

    def body(a_ref, o_ref):
        for b in range(N_STRIPS):
            o_ref[:, b, :] = a_ref[b % 4, b // 4]

    return pl.pallas_call(
        body, grid=(na // ta,), name=name, in_specs=[_strip_rows(ta, d)],
        out_specs=pl.BlockSpec((ta, N_STRIPS, d), lambda i: (i, 0, 0)),
        out_shape=jax.ShapeDtypeStruct((na, N_STRIPS, d), a.dtype), compiler_params=_cparams("parallel"),
    )(_strips(a)).reshape(s, d)


def _local_step(x, tgt, fetch, ff, sinks, g_mix, g_group, g_mlp, g_final, emit):
    s, d = x.shape
    depth = g_mix.shape[0]
    tb = min(512, s)
    tf = ff // N_CHIPS
    ts = min(1024, s)
    saved = []
    for l in range(depth):
        w_in, _, _, _, conv_w = fetch(0, l, x)
        cw = _tile_rows(conv_w[l])
        sk = jnp.repeat(sinks[l].reshape(N_HEADS), HEAD_DIM)[None]
        h, za, zb, zc = _qkv_fwd(x, g_mix[l][None], w_in, l, tb)
        parts_a = [_attn_fwd(za, dil, A_WIDTH, 1, 2, 1, A_MAX_DIST, "attn_a_fwd_%d" % dil) for dil in DILATIONS]
        part_c = _attn_fwd(zc, 1, C_KV_WIDTH, 3, 4, C_GROUP, C_MAX_DIST, "attn_c_fwd")
        ya, lse_a, yc, lse_c = _attn_merge(parts_a, part_c, sk, tb)
        w_in, w_o, w1, w2, _ = fetch(1, l, yc)
        x1, yb = _mix_fwd(x, ya, yc, zb, cw, g_group[l][None], w_o, l, tb)
        x2, h2, ap = _mlp_fwd(x1, g_mlp[l][None], w1, w2, l, ts, tf)
        saved.append((x, h, za, zb, zc, ya, lse_a, yc, lse_c, yb, x1, h2, ap, cw, sk))
        x = x2
    dx, loss_tile, dg_final = _loss_head(x, g_final[None], tgt, tb)
    grads = [None] * depth
    tok = jnp.zeros((), F32)
    for l in reversed(range(depth)):
        x0, h, za, zb, zc, ya, lse_a, yc, lse_c, yb, x1, h2, ap, cw, sk = saved[l]
        dx1, dap, dg_mlp = _mlp_bwd(dx, x1, ap, g_mlp[l][None] + tok, w1, w2, l, ts, tf)
        tok = emit(l, 3, _wgrad(ap, dx, min(1024, ff), d, ts, "wgrad_ff_out", relu2=True))
        tok = tok + emit(l, 2, _wgrad(h2, dap, d, min(1024, ff), ts, "wgrad_ff_in"))
        n, dya, dyc, dd_a, dd_c, dyb, dg_group, dsink = _mix_bwd(dx1, ya, yb, yc, lse_c, sk, g_group[l][None] + tok,
                                                                 w_o, l, tb)
        tok = emit(l, 1, _wgrad(n, dx1, MIX_WIDTH, d, ts, "wgrad_o"))
        cw = cw + tok
        parts_a = [_attn_bwd(za, dya, lse_a, dd_a, dil, A_WIDTH, 1, 2, 1, A_MAX_DIST, "attn_a_bwd_%d" % dil)
                   for dil in DILATIONS]
        parts_c = _attn_bwd(zc, dyc, lse_c, dd_c, 1, C_KV_WIDTH, 3, 4, C_GROUP, C_MAX_DIST, "attn_c_bwd")
        dz, dcw = _dz_assemble(parts_a, parts_c, dyb, zb, cw)
        dx, dg_mix = _qkv_bwd(dz, dx1, x0, g_mix[l][None], w_in, l, tb)
        tok = emit(l, 0, _wgrad(h, dz, d, IN_WIDTH // 4, ts, "wgrad_in"))
        grads[l] = (dcw, dsink, dg_mix, dg_group, dg_mlp)
    return loss_tile, dx, grads, dg_final


ANY = pl.BlockSpec(memory_space=pl.ANY)
SHARD_AXES = (2, 1, 2, 1)
N_BIG = len(SHARD_AXES)
N_CHIPS = 4
N_DEV = 8


def _mesh_pos():
    return lax.axis_index("x"), lax.axis_index("y"), lax.axis_index("c")


def _flip(v, bit):
    return 1 - v if bit else v


def _place_shard(shard, chip_arr, name):
    _, rows, cols = shard.shape
    tr = min(256, rows)

    def body(chip_ref, x_ref, o_ref):
        o_ref[...] = x_ref[...].astype(BF)

    return pl.pallas_call(
        body, name=name,
        grid_spec=pltpu.PrefetchScalarGridSpec(
            num_scalar_prefetch=1, grid=(2, rows // tr),
            in_specs=[pl.BlockSpec((None, tr, cols), lambda l, i, chip: (l, i, 0))],
            out_specs=pl.BlockSpec((None, None, tr, cols), lambda l, i, chip: (l, chip[0], i, 0))),
        out_shape=jax.ShapeDtypeStruct((2, N_CHIPS, rows, cols), BF),
        compiler_params=_cparams("parallel", "parallel"),
    )(chip_arr, shard)


HBM = pl.BlockSpec(memory_space=pltpu.HBM)
SEM = pl.BlockSpec(memory_space=pltpu.SEMAPHORE)
EFFECT = pltpu.SideEffectType.DATAFLOW_SIDE_EFFECTING

GATHER_GROUPS = (((0, 0),), ((1, 0), (2, 0), (3, 0)), ((0, 1),), ((1, 1), (2, 1), (3, 1)))
GATHER_STARTS = ((0,), (1,), (2, 3))


def _gather_copies(arrs, group, send_sems, recv_sems):
    x, y, c = _mesh_pos()
    me = 2 * x + y
    out = []
    for i, (w, layer) in enumerate(group):
        mine = arrs[w].at[layer, me]
        for j, (qx, qy) in enumerate([(1 - x, y), (x, 1 - y), (1 - x, 1 - y)]):
            landed = arrs[w].at[layer, 2 * qx + qy]
            out.append(tuple(pltpu.make_async_remote_copy(
                src_ref=piece, dst_ref=piece, send_sem=send_sems.at[i * 3 + j], recv_sem=recv_sems.at[i * 3 + j],
                device_id=(qx, qy, c), device_id_type=MESH) for piece in (mine, landed)))
    return out


def _conv_copies(conv_src, conv_dst, send_sems, recv_sems):
    x, y, c = _mesh_pos()
    out = []
    for j, (qx, qy) in enumerate([(1 - x, y), (x, 1 - y), (1 - x, 1 - y)]):
        out.append(tuple(pltpu.make_async_remote_copy(
            src_ref=conv_src, dst_ref=conv_dst.at[q], send_sem=send_sems.at[j], recv_sem=recv_sems.at[j],
            device_id=(qx, qy, c), device_id_type=MESH) for q in (2 * x + y, 2 * qx + qy)))
    return out


def _gather_start(groups, arrs, conv, name):
    n_sems = 2 * (len(groups) + (conv is not None))
    mats = sorted({w for g in groups for w, _ in GATHER_GROUPS[g]})

    def body(*refs):
        arrs_ref = [None] * N_BIG
        for w, ref in zip(mats, refs):
            arrs_ref[w] = ref
        sems = refs[n_op:n_op + n_sems]
        if conv is not None:
            for cp, _ in _conv_copies(refs[len(mats)], refs[len(mats) + 1], sems[-2], sems[-1]):
                cp.start()
        for k, g in enumerate(groups):
            for cp, _ in _gather_copies(arrs_ref, GATHER_GROUPS[g], sems[2 * k], sems[2 * k + 1]):
                cp.start()

    sem_shapes = []
    for n in [len(GATHER_GROUPS[g]) for g in groups] + ([1] if conv is not None else []):
        sem_shapes += [pltpu.SemaphoreType.DMA((3 * n,))] * 2
    operands = [arrs[w] for w in mats] + ([] if conv is None else list(conv))
    n_op = len(operands)
    res = pl.pallas_call(
        body, name=name,
        out_shape=tuple(sem_shapes) + tuple(pltpu.HBM(a.shape, a.dtype) for a in operands),
        in_specs=(HBM,) * n_op, out_specs=(SEM,) * n_sems + (HBM,) * n_op,
        input_output_aliases={i: n_sems + i for i in range(n_op)},
        compiler_params=pltpu.CompilerParams(has_side_effects=EFFECT),
    )(*[pltpu.with_memory_space_constraint(a, pltpu.HBM) for a in operands])
    arrs = list(arrs)
    for w, a in zip(mats, res[n_sems:]):
        arrs[w] = a
    return res[:n_sems], arrs, list(res[n_sems + len(mats):])


def _gather_wait(k, sems, arrs, conv, after, name):
    group = GATHER_GROUPS[k]
    mats = sorted({w for w, _ in group})
    n_conv = 0 if conv is None else 2

    def body(*refs):
        local = refs[:len(mats)]
        arrs_ref = [None] * N_BIG
        for w, ref in zip(mats, local):
            arrs_ref[w] = ref
        pos = len(mats) + n_conv
        copies = _gather_copies(arrs_ref, group, refs[pos], refs[pos + 1])
        if conv is not None:
            copies += _conv_copies(refs[len(mats)], refs[len(mats) + 1], refs[pos + 2], refs[pos + 3])
        for send, recv in copies:
            recv.wait_recv()
            send.wait_send()

    operands = [arrs[w] for w in mats] + ([] if conv is None else [conv[1], conv[2]])
    sem_ops = list(sems) + ([] if conv is None else list(conv[0]))
    n_op = len(operands)
    res = pl.pallas_call(
        body, name=name, out_shape=tuple(pltpu.HBM(a.shape, a.dtype) for a in operands),
        in_specs=(HBM,) * n_op + (SEM,) * len(sem_ops) + (ANY,), out_specs=(HBM,) * n_op,
        input_output_aliases={i: i for i in range(n_op)},
        compiler_params=pltpu.CompilerParams(has_side_effects=EFFECT),
    )(*operands, *sem_ops, after)
    arrs = list(arrs)
    for w, a in zip(mats, res):
        arrs[w] = a
    return arrs, (res[-1] if conv is not None else None)


def _grad_shard(ref, w, chip, n):
    start = pl.multiple_of(chip * n, 128)
    if SHARD_AXES[w] == 2:
        return ref.at[:, pl.ds(start, n)]
    return ref.at[pl.ds(start, n), :]


def _slot_shape(g, w):
    shape = list(g.shape)
    shape[SHARD_AXES[w] - 1] //= N_CHIPS
    return (N_DEV - 1,) + tuple(shape)


def _scatter_copies(g_ref, land_ref, send_sems, recv_sems, layer, w):
    x, y, c = _mesh_pos()
    n = g_ref.shape[SHARD_AXES[w] - 1] // N_CHIPS
    out = []
    for r in range(1, N_DEV):
        tx, ty, tc = _flip(x, r & 4), _flip(y, r & 2), _flip(c, r & 1)
        cp = pltpu.make_async_remote_copy(
            src_ref=_grad_shard(g_ref, w, 2 * tx + ty, n), dst_ref=land_ref.at[r - 1], send_sem=send_sems.at[r - 1],
            recv_sem=recv_sems.at[r - 1], device_id=(tx, ty, tc), device_id_type=MESH)
        out.append((cp, (c != layer) if r & 1 else (c == layer)))
    return out


def _scatter_start(g, land, layer, w, name):
    def body(g_ref, land_ref, send_sems, recv_sems, g_thru, land_thru, token):
        for cp, mine in _scatter_copies(g_ref, land_ref, send_sems, recv_sems, layer, w):
            @pl.when(mine)
            def _():
                cp.start()
        token[...] = jnp.zeros_like(token)

    return pl.pallas_call(
        body, name=name,
        out_shape=(pltpu.SemaphoreType.DMA((N_DEV - 1,)), pltpu.SemaphoreType.DMA((N_DEV - 1,)),
                   pltpu.HBM(g.shape, g.dtype), pltpu.HBM(land.shape, land.dtype),
                   jax.ShapeDtypeStruct((HALO, 128), F32)),
        in_specs=(HBM, HBM), out_specs=(SEM, SEM, HBM, HBM, pl.BlockSpec(memory_space=pltpu.VMEM)),
        input_output_aliases={0: 2, 1: 3}, compiler_params=pltpu.CompilerParams(has_side_effects=EFFECT),
    )(pltpu.with_memory_space_constraint(g, pltpu.HBM), pltpu.with_memory_space_constraint(land, pltpu.HBM))


def _scatter_wait(started, land, after, w, name):
    def body(g0_ref, g1_ref, land_ref, ss0, rs0, ss1, rs1, after_ref, g0_out, g1_out, land_out):
        c = lax.axis_index("c")
        for layer, g_ref, ss, rs in ((0, g0_ref, ss0, rs0), (1, g1_ref, ss1, rs1)):
            for cp, mine in _scatter_copies(g_ref, land_ref, ss, rs, layer, w):
                @pl.when(mine)
                def _():
                    cp.wait_send()

                @pl.when(c == layer)
                def _():
                    cp.wait_recv()

    (ss0, rs0, g0), (ss1, rs1, g1) = started
    return pl.pallas_call(
        body, name=name,
        out_shape=(pltpu.HBM(g0.shape, g0.dtype), pltpu.HBM(g1.shape, g1.dtype), pltpu.HBM(land.shape, land.dtype)),
        in_specs=(HBM, HBM, HBM, SEM, SEM, SEM, SEM, ANY), out_specs=(HBM, HBM, HBM),
        input_output_aliases={0: 0, 1: 1, 2: 2}, compiler_params=pltpu.CompilerParams(has_side_effects=EFFECT),
    )(g0, g1, land, ss0, rs0, ss1, rs1, after)


def _sum_slots(g0, g1, slots, w, pos_arr, name):
    _, rows, cols = slots.shape
    tr = min(256, rows)
    nr = rows // tr
    if SHARD_AXES[w] == 2:
        own = pl.BlockSpec((tr, cols), lambda i, pos: (i, pos[0]))
    else:
        own = pl.BlockSpec((tr, cols), lambda i, pos: (pos[0] * nr + i, 0))

    def body(pos_ref, own0_ref, own1_ref, s_ref, o_ref):
        acc = jnp.where(pos_ref[1] == 0, own0_ref[...], own1_ref[...]).astype(F32)
        for r in range(N_DEV - 1):
            acc = acc + s_ref[r].astype(F32)
        o_ref[...] = acc

    return pl.pallas_call(
        body, name=name,
        grid_spec=pltpu.PrefetchScalarGridSpec(
            num_scalar_prefetch=1, grid=(nr,),
            in_specs=[own, own, pl.BlockSpec((N_DEV - 1, tr, cols), lambda i, pos: (0, i, 0))],
            out_specs=pl.BlockSpec((tr, cols), lambda i, pos: (i, 0))),
        out_shape=jax.ShapeDtypeStruct((rows, cols), F32), compiler_params=_cparams("parallel"),
    )(pos_arr, g0, g1, slots)


def _swap_layers(halves):
    def body(*refs):
        srcs, dsts = refs[:N_BIG], refs[N_BIG:2 * N_BIG]
        send_sems, recv_sems = refs[2 * N_BIG:]
        x, y, c = _mesh_pos()
        sends = [pltpu.make_async_remote_copy(src_ref=srcs[w], dst_ref=dsts[w], send_sem=send_sems.at[w],
                                              recv_sem=recv_sems.at[w], device_id=(x, y, 1 - c), device_id_type=MESH)
                 for w in range(N_BIG)]
        for cp in sends:
            cp.start()
        for cp in sends:
            cp.wait_recv()
        for cp in sends:
            cp.wait_send()

    return pl.pallas_call(
        body, name="swap_layers", in_specs=[ANY] * N_BIG, out_specs=[ANY] * N_BIG,
        out_shape=[jax.ShapeDtypeStruct(h.shape, h.dtype) for h in halves],
        scratch_shapes=[pltpu.SemaphoreType.DMA((N_BIG,)), pltpu.SemaphoreType.DMA((N_BIG,))],
    )(*halves)


def _adamw_math(w, g, m, v):
    m = ADAM_B1 * m + (1.0 - ADAM_B1) * g
    v = ADAM_B2 * v + (1.0 - ADAM_B2) * jnp.square(g)
    m_hat = m / (1.0 - ADAM_B1 ** ADAM_STEP)
    v_hat = v / (1.0 - ADAM_B2 ** ADAM_STEP)
    delta = -ADAM_LR * (m_hat / (jnp.sqrt(v_hat) + ADAM_EPS) + ADAM_WD * w)
    return delta, m, v


def _adamw(w, g_own, g_other, m, v, pos_arr, name):
    shape = w.shape
    _, rows, cols = shape
    tr = min(256, rows)

    def body(pos_ref, w_ref, own_ref, other_ref, m_ref, v_ref, g_ref, d_ref, m2_ref, v2_ref):
        g = jnp.where(pl.program_id(0) == pos_ref[1], own_ref[...], other_ref[...])
        g_ref[...] = g
        d_ref[...], m2_ref[...], v2_ref[...] = _adamw_math(w_ref[...], g, m_ref[...], v_ref[...])

    full = pl.BlockSpec((None, tr, cols), lambda l, i, pos: (l, i, 0))
    half = pl.BlockSpec((tr, cols), lambda l, i, pos: (i, 0))
    return pl.pallas_call(
        body, name=name,
        grid_spec=pltpu.PrefetchScalarGridSpec(
            num_scalar_prefetch=1, grid=(2, rows // tr),
            in_specs=[full, half, half, full, full], out_specs=[full] * 4),
        out_shape=[jax.ShapeDtypeStruct(shape, F32)] * 4, compiler_params=_cparams("parallel", "parallel"),
    )(pos_arr, w, g_own, g_other, m, v)


def _small_sync(part, w, m, v):
    rows, cols = part.shape

    def body(p_ref, w_ref, m_ref, v_ref, g_ref, d_ref, m2_ref, v2_ref, slots, send_sems, recv_sems):
        x, y, c = _mesh_pos()
        me = 4 * x + 2 * y + c
        slots[me] = p_ref[...]
        sends = []
        for r in range(1, N_DEV):
            to = (_flip(x, r & 4), _flip(y, r & 2), _flip(c, r & 1))
            sends.append(pltpu.make_async_remote_copy(
                src_ref=p_ref, dst_ref=slots.at[me], send_sem=send_sems.at[r - 1], recv_sem=recv_sems.at[r - 1],
                device_id=to, device_id_type=MESH))
        for cp in sends:
            cp.start()
        for cp in sends:
            cp.wait_recv()
        for cp in sends:
            cp.wait_send()
        g = slots[0]
        for i in range(1, N_DEV):
            g = g + slots[i]
        g_ref[...] = g
        d_ref[...], m2_ref[...], v2_ref[...] = _adamw_math(w_ref[...], g, m_ref[...], v_ref[...])

    vm = pl.BlockSpec(memory_space=pltpu.VMEM)
    return pl.pallas_call(
        body, name="small_sync", in_specs=[vm] * 4, out_specs=[vm] * 4,
        out_shape=[jax.ShapeDtypeStruct((rows, cols), F32)] * 4,
        scratch_shapes=[pltpu.VMEM((N_DEV, rows, cols), F32), pltpu.SemaphoreType.DMA((N_DEV - 1,)),
                        pltpu.SemaphoreType.DMA((N_DEV - 1,))],
    )(part, w, m, v)


def _pack_small(d, g_mix, g_group, g_mlp, g_final, conv_full, sinks, scalar):
    def part(rows):
        return jnp.pad(rows, ((0, HALO - rows.shape[0]), (0, d - rows.shape[1])))
    return jnp.concatenate([part(g_mix), part(g_group), part(g_mlp), part(g_final[None]),
                            part(conv_full.reshape(6, CONV_CH)), part(sinks.reshape(2, N_HEADS)),
                            part(scalar.reshape(1, 1))], axis=0)


def _unpack_small(p, dm):
    return (p[0:2, :dm], p[8:10, :MIX_WIDTH], p[16:18, :dm], p[24, :dm], p[32:38, :CONV_CH].reshape(2, 3, CONV_CH),
            p[40:42, :N_HEADS].reshape(2, 2, C_GROUP), p[48, 0])


def kernel(x, w_in, conv_w, sinks, g_mix, g_group, w_o, g_mlp, w_ff_in, w_ff_out, g_final, loss_target, m_w_in, m_conv_w, m_sinks, m_g_mix, m_g_group, m_w_o, m_g_mlp, m_w_ff_in, m_w_ff_out, m_g_final, v_w_in, v_conv_w, v_sinks, v_g_mix, v_g_group, v_w_o, v_g_mlp, v_w_ff_in, v_w_ff_out, v_g_final):
    d = max(x.shape[2], MIX_WIDTH)
    chip = 2 * lax.axis_index("x") + lax.axis_index("y")
    conv_n = conv_w.shape[2]

    pos_arr = jnp.stack([chip, lax.axis_index("c")]).astype(jnp.int32)
    shards = (w_in, w_o, w_ff_in, w_ff_out)
    conv_tile = jnp.pad(conv_w.reshape(6, conv_n), ((0, HALO - 6), (0, 128 - conv_n)))
    placed = [_place_shard(w_in, pos_arr[:1], "place_shard_0"), None, None, None]
    sems_a, placed, conv_thru = _gather_start(
        GATHER_STARTS[0], placed, (conv_tile, lax.empty((N_CHIPS,) + conv_tile.shape, conv_tile.dtype)),
        "gather_start_0")
    for i in range(1, N_BIG):
        placed[i] = _place_shard(shards[i], pos_arr[:1], "place_shard_%d" % i)
    sems_b, placed, _ = _gather_start(GATHER_STARTS[1], placed, None, "gather_start_1")
    full = {"arrs": placed, "conv": None, "sems": list(sems_a[:2]) + list(sems_b)}

    def fetch(stage, layer, after):
        k = 2 * layer + stage
        sems = full["sems"][2 * k:2 * k + 2]
        if k == 0:
            full["arrs"], land = _gather_wait(0, sems, full["arrs"], (sems_a[-2:], *conv_thru), after, "gather_wait_0")
            conv_all = lax.dynamic_update_slice(land, conv_tile[None], (chip, 0, 0))
            full["conv"] = conv_all[:, :6, :conv_n].reshape(N_CHIPS, 2, 3, conv_n).transpose(1, 2, 0, 3).reshape(
                2, 3, CONV_CH)
        else:
            full["arrs"], _ = _gather_wait(k, sems, full["arrs"], None, after, "gather_wait_%d" % k)
        if k == 1:
            sems_c, full["arrs"], _ = _gather_start(GATHER_STARTS[2], full["arrs"], None, "gather_start_2")
            full["sems"] += list(sems_c)
        return (*full["arrs"], full["conv"])

    lands, started = [None] * N_BIG, {}

    def emit(layer, w, g):
        if lands[w] is None:
            lands[w] = lax.empty(_slot_shape(g, w), g.dtype)
        *started[layer, w], lands[w], token = _scatter_start(g, lands[w], layer, w, "scatter_start_%d_%d" % (layer, w))
        return token[0, 0]

    loss_tile, dx, grads, dg_final = _local_step(_to_strips(x[0], placed[1], "to_strips_x"),
                                                 _to_strips(loss_target[0], placed[1], "to_strips_target"), fetch,
                                                 w_ff_in.shape[2] * N_CHIPS,
                                                 sinks, g_mix, g_group, g_mlp, g_final, emit)

    own = []
    for w in range(N_BIG):
        g0, g1, slots = _scatter_wait((started[0, w], started[1, w]), lands[w], dx, w, "scatter_wait_%d" % w)
        own.append(_sum_slots(g0, g1, slots, w, pos_arr, "sum_slots_%d" % w))
    other = _swap_layers(own)

    def both(i):
        return jnp.stack([grads[0][i][0], grads[1][i][0]])
    dconv = jnp.stack([grads[0][0][:3], grads[1][0][:3]])
    dsinks = jnp.stack([grads[0][1][0, ::HEAD_DIM], grads[1][1][0, ::HEAD_DIM]])
    part = _pack_small(d, both(2), both(3), both(4), dg_final[0], dconv, dsinks, loss_tile[0, 0])

    def spread(shard):
        return lax.dynamic_update_slice(jnp.zeros((2, 3, CONV_CH), F32), shard, (0, 0, chip * conv_n))
    zero = jnp.zeros((), F32)
    packs = [_pack_small(d, a, b, c_, e, spread(f), g_, zero) for a, b, c_, e, f, g_ in (
        (g_mix, g_group, g_mlp, g_final, conv_w, sinks),
        (m_g_mix, m_g_group, m_g_mlp, m_g_final, m_conv_w, m_sinks),
        (v_g_mix, v_g_group, v_g_mlp, v_g_final, v_conv_w, v_sinks))]
    small = [_unpack_small(p, x.shape[2]) for p in _small_sync(part, *packs)]

    def shard_of(full):
        return lax.dynamic_slice(full, (0, 0, chip * conv_n), (2, 3, conv_n))
    small = [(s[0], s[1], s[2], s[3], shard_of(s[4]), s[5], s[6]) for s in small]
    loss = small[0][6]

    big = [_adamw(w, own[i], other[i], m, v, pos_arr, "adamw_%d" % i) for i, (w, m, v) in enumerate((
        (w_in, m_w_in, v_w_in), (w_o, m_w_o, v_w_o), (w_ff_in, m_w_ff_in, v_w_ff_in),
        (w_ff_out, m_w_ff_out, v_w_ff_out)))]

    def ordered(kind):
        b = [big[i][kind] for i in range(N_BIG)]
        s = small[kind]
        return [b[0], s[4], s[5], s[0], s[1], b[1], s[2], b[2], b[3], s[3]]

    return (loss, _from_strips(dx, "from_strips_dx")[None], *ordered(0), *ordered(1), *ordered(2), *ordered(3))
```

```python
import functools

import jax
import jax.numpy as jnp
from jax import lax
from jax.experimental import pallas as pl
from jax.experimental.pallas import tpu as pltpu

HEAD_DIM = 64
N_HEADS = 6
C_GROUP = 3
A_WIDTH = N_HEADS * HEAD_DIM
C_KV_WIDTH = 2 * HEAD_DIM
CONV_CH = 256
ZA_W = 3 * A_WIDTH
ZB_W = 3 * CONV_CH
ZC_W = A_WIDTH + 2 * C_KV_WIDTH
IN_WIDTH = ZA_W + ZB_W + ZC_W
MIX_WIDTH = A_WIDTH + CONV_CH + A_WIDTH
DILATIONS = (1, 4, 16)
A_MAX_DIST = 128
C_MAX_DIST = 127
TQ = 128
EPS = 1e-6
SCALE = HEAD_DIM ** -0.5
NEG = -1e30
HALO = 8

ADAM_LR = 0.001
ADAM_B1 = 0.9
ADAM_B2 = 0.999
ADAM_EPS = 1e-08
ADAM_WD = 0.01
ADAM_STEP = 10

BF = jnp.bfloat16
F32 = jnp.float32
MESH = pl.DeviceIdType.MESH
VMEM_LIMIT = 56 * 1024 * 1024


def _cparams(*sem):
    return pltpu.CompilerParams(dimension_semantics=sem, vmem_limit_bytes=VMEM_LIMIT)


def _nt(a, b):
    return lax.dot_general(a, b, (((1,), (1,)), ((), ())), preferred_element_type=F32)


def _tn(a, b):
    return lax.dot_general(a, b, (((0,), (0,)), ((), ())), preferred_element_type=F32)


def _nn(a, b):
    return jnp.dot(a, b, preferred_element_type=F32)


def _rows(tb, w):
    return pl.BlockSpec((tb, w), lambda i: (i, 0))


def _whole(shape):
    return pl.BlockSpec(shape, lambda *_: (0,) * len(shape))


def _layer(shape, l):
    return pl.BlockSpec((None,) + shape, lambda *_: (l,) + (0,) * len(shape))


def _rms_scale(v):
    return lax.rsqrt(jnp.mean(v * v, axis=-1, keepdims=True) + EPS)


def _norm_bwd(dxhat, xhat, r):
    return r * (dxhat - xhat * jnp.mean(dxhat * xhat, axis=-1, keepdims=True))


def _qkv_fwd(x, g, w_all, l, tb):
    s, d = x.shape

    def body(x_ref, g_ref, w_ref, h_ref, za_ref, zb_ref, zc_ref):
        xv = x_ref[...]
        h = ((xv * _rms_scale(xv)) * g_ref[...]).astype(BF)
        h_ref[...] = h
        z = jnp.concatenate([_nn(h, w_ref[k]) for k in range(N_CHIPS)], axis=1)
        za_ref[...] = z[:, :ZA_W]
        zb_ref[...] = z[:, ZA_W:ZA_W + ZB_W]
        zc_ref[...] = z[:, ZA_W + ZB_W:]

    return pl.pallas_call(
        body, grid=(s // tb,), name="qkv_fwd",
        in_specs=[_rows(tb, d), _whole((1, d)), _layer((N_CHIPS, d, IN_WIDTH // N_CHIPS), l)],
        out_specs=[_rows(tb, d), _rows(tb, ZA_W), _rows(tb, ZB_W), _rows(tb, ZC_W)],
        out_shape=[jax.ShapeDtypeStruct((s, d), BF), jax.ShapeDtypeStruct((s, ZA_W), F32),
                   jax.ShapeDtypeStruct((s, ZB_W), F32), jax.ShapeDtypeStruct((s, ZC_W), F32)],
        compiler_params=_cparams("parallel"),
    )(x, g, w_all)


N_STRIPS = 16


def _strips(a):
    s, w = a.shape
    return a.reshape(4, 4, s // N_STRIPS, w)


def _p_grid(s, dil):
    na = s // N_STRIPS
    return {16: (4, 4, na // TQ), 4: (4, na // 32), 1: (na // 8,)}[dil]


def _p_spec(dil, cw, col, prev=False):
    def blk(j):
        return jnp.maximum(j - 1, 0) if prev else j
    if dil == 16:
        return pl.BlockSpec((None, None, TQ, cw), lambda f, e, j: (f, e, blk(j), col))
    if dil == 4:
        return pl.BlockSpec((None, 4, 32, cw), lambda f, j: (f, 0, blk(j), col))
    return pl.BlockSpec((4, 4, 8, cw), lambda j: (0, 0, blk(j), col))


def _block_pos(i, dil):
    if dil == 16:
        return i
    if dil == 4:
        return 4 * (i % 32) + i // 32
    return 16 * (i % 8) + 4 * ((i // 8) % 4) + i // 32


def _band_mask(b, dil, max_dist):
    qi = _block_pos(lax.broadcasted_iota(jnp.int32, (TQ, 2 * TQ), 0), dil)
    col = lax.broadcasted_iota(jnp.int32, (TQ, 2 * TQ), 1)
    cur = col >= TQ
    dist = qi - _block_pos(col % TQ, dil) + jnp.where(cur, 0, TQ)
    return (dist >= 0) & (dist <= max_dist) & (cur | (b > 0))


def _hs(h):
    return slice(h * HEAD_DIM, (h + 1) * HEAD_DIM)


def _ld(ref, cols):
    v = ref[..., cols]
    return v.reshape(TQ, v.shape[-1])


def _st(ref, cols, val):
    ref[..., cols] = val.reshape(ref.shape[:-1] + (val.shape[-1],))


def _attn_fwd(z, dil, kw, kcol, vcol, n_rep, max_dist, name):
    s, zw = z.shape
    grid = _p_grid(s, dil)

    def body(q_ref, kp_ref, kc_ref, vp_ref, vc_ref, acc_ref, m_ref, l_ref):
        mask = _band_mask(pl.program_id(len(grid) - 1), dil, max_dist)
        for kh in range(N_HEADS // n_rep):
            k2 = jnp.concatenate([_ld(kp_ref, _hs(kh)), _ld(kc_ref, _hs(kh))], axis=0).astype(BF)
            v2 = jnp.concatenate([_ld(vp_ref, _hs(kh)), _ld(vc_ref, _hs(kh))], axis=0).astype(BF)
            for h in range(kh * n_rep, (kh + 1) * n_rep):
                q = _ld(q_ref, _hs(h)).astype(BF)
                sc = jnp.where(mask, _nt(q, k2) * SCALE, NEG)
                m = jnp.max(sc, axis=1, keepdims=True)
                p = jnp.exp(sc - m)
                _st(acc_ref, _hs(h), _nn(p.astype(BF), v2))
                _st(m_ref, _hs(h), jnp.broadcast_to(m, (TQ, HEAD_DIM)))
                _st(l_ref, _hs(h), jnp.broadcast_to(jnp.sum(p, axis=1, keepdims=True), (TQ, HEAD_DIM)))

    res = pl.pallas_call(
        body, grid=grid, name=name,
        in_specs=[_p_spec(dil, A_WIDTH, 0), _p_spec(dil, kw, kcol, True), _p_spec(dil, kw, kcol),
                  _p_spec(dil, kw, vcol, True), _p_spec(dil, kw, vcol)],
        out_specs=[_p_spec(dil, A_WIDTH, 0)] * 3,
        out_shape=[jax.ShapeDtypeStruct((4, 4, s // N_STRIPS, A_WIDTH), F32)] * 3,
        compiler_params=_cparams(*(("parallel",) * len(grid))),
    )(*[_strips(z)] * 5)
    return [a.reshape(s, A_WIDTH) for a in res]


def _attn_merge(parts_a, part_c, sink_row, tb):
    s = part_c[0].shape[0]
    n_a = len(parts_a)

    def body(*refs):
        ins, sink_ref = refs[:3 * n_a + 3], refs[3 * n_a + 3]
        ya_ref, lsea_ref, yc_ref, lsec_ref = refs[3 * n_a + 4:]
        ms = [ins[3 * p + 1][...] for p in range(n_a)]
        m = functools.reduce(jnp.maximum, ms)
        acc = jnp.zeros_like(m)
        l = jnp.zeros_like(m)
        for p in range(n_a):
            w = jnp.exp(ms[p] - m)
            acc = acc + w * ins[3 * p][...]
            l = l + w * ins[3 * p + 2][...]
        ya_ref[...] = acc / l
        lsea_ref[...] = m + jnp.log(l)
        acc_c, m_c, l_c = [r[...] for r in ins[3 * n_a:]]
        sk = sink_ref[...]
        m2 = jnp.maximum(m_c, sk)
        w = jnp.exp(m_c - m2)
        l2 = w * l_c + jnp.exp(sk - m2)
        yc_ref[...] = (w * acc_c) / l2
        lsec_ref[...] = m2 + jnp.log(l2)

    return pl.pallas_call(
        body, grid=(s // tb,), name="attn_merge",
        in_specs=[_rows(tb, A_WIDTH)] * (3 * n_a + 3) + [_whole((1, A_WIDTH))],
        out_specs=[_rows(tb, A_WIDTH)] * 4, out_shape=[jax.ShapeDtypeStruct((s, A_WIDTH), F32)] * 4,
        compiler_params=_cparams("parallel"),
    )(*[a for part in parts_a + [part_c] for a in part], sink_row)


def _shift_down(v, n, halo):
    rows = v.shape[0]
    out = pltpu.roll(v, n, 0)
    row = lax.broadcasted_iota(jnp.int32, v.shape, 0)
    for t in range(n):
        out = jnp.where(row == t, halo[HALO - n + t:HALO - n + t + 1, :], out)
    return out


def _shift_up(v, n, halo):
    rows = v.shape[0]
    out = pltpu.roll(v, rows - n, 0)
    row = lax.broadcasted_iota(jnp.int32, v.shape, 0)
    for t in range(n):
        out = jnp.where(row == rows - n + t, halo[t:t + 1, :], out)
    return out


def _strip(v, b):
    return v[b % 4, b // 4]


def _conv_strips(zb, prev, cw):
    gb = [_strip(zb, b)[:, :CONV_CH] for b in range(N_STRIPS)]
    gc = [_strip(zb, b)[:, CONV_CH:2 * CONV_CH] for b in range(N_STRIPS)]
    xb = [_strip(zb, b)[:, 2 * CONV_CH:] for b in range(N_STRIPS)]
    u = [g * v for g, v in zip(gc, xb)]
    uh = prev[:, :, CONV_CH:2 * CONV_CH] * prev[:, :, 2 * CONV_CH:]
    wrapped = {14: _shift_down(u[14], 1, uh[2]), 15: _shift_down(u[15], 1, uh[3])}
    u1 = [u[b - 1] if b >= 1 else wrapped[15] for b in range(N_STRIPS)]
    u2 = [u[b - 2] if b >= 2 else wrapped[14 + b] for b in range(N_STRIPS)]
    c = [cw[0:1, :] * u2[b] + cw[1:2, :] * u1[b] + cw[2:3, :] * u[b] for b in range(N_STRIPS)]
    return gb, gc, xb, u, u1, u2, c


def _strip_rows(ta, w):
    return pl.BlockSpec((4, 4, ta, w), lambda i: (0, 0, i, 0))


def _prev_rows(ta, w):
    return pl.BlockSpec((4, None, HALO, w), lambda i: (0, 3, jnp.maximum(i * (ta // HALO) - 1, 0), 0))


def _next_rows(ta, w, nblk):
    return pl.BlockSpec((4, None, HALO, w),
                        lambda i: (0, 0, jnp.minimum((i + 1) * (ta // HALO), nblk * (ta // HALO) - 1), 0))


def _mix_fwd(x, ya, yc, zb, cw, gg, wo_all, l, tb):
    s, d = x.shape
    ta = tb // N_STRIPS

    def body(x_ref, ya_ref, yc_ref, zb_ref, zbp_ref, cw_ref, gg_ref, wo_ref, x1_ref, yb_ref):
        i = pl.program_id(0)
        prev = jnp.where(i > 0, zbp_ref[...], 0.0)
        gb, _, _, _, _, _, c = _conv_strips(zb_ref[...], prev, cw_ref[...])
        for b in range(N_STRIPS):
            yb_ref[b % 4, b // 4] = gb[b] * c[b]
        yb = yb_ref[...].reshape(tb, CONV_CH)
        ya, yc = ya_ref[...].reshape(tb, A_WIDTH), yc_ref[...].reshape(tb, A_WIDTH)
        n = jnp.concatenate([ya * _rms_scale(ya), yb * _rms_scale(yb), yc * _rms_scale(yc)], axis=1)
        n = (n * gg_ref[...]).astype(BF)
        x1 = x_ref[...].reshape(tb, d) + _nn(n, wo_ref[...].reshape(MIX_WIDTH, d))
        x1_ref[...] = x1.reshape(4, 4, ta, d)

    res = pl.pallas_call(
        body, grid=(s // tb,), name="mix_fwd",
        in_specs=[_strip_rows(ta, d), _strip_rows(ta, A_WIDTH), _strip_rows(ta, A_WIDTH), _strip_rows(ta, ZB_W),
                  _prev_rows(ta, ZB_W), _whole((HALO, CONV_CH)), _whole((1, MIX_WIDTH)),
                  _layer((N_CHIPS, MIX_WIDTH // N_CHIPS, d), l)],
        out_specs=[_strip_rows(ta, d), _strip_rows(ta, CONV_CH)],
        out_shape=[jax.ShapeDtypeStruct((4, 4, s // N_STRIPS, d), F32),
                   jax.ShapeDtypeStruct((4, 4, s // N_STRIPS, CONV_CH), F32)],
        compiler_params=_cparams("parallel"),
    )(_strips(x), _strips(ya), _strips(yc), _strips(zb), _strips(zb), cw, gg, wo_all)
    return res[0].reshape(s, d), res[1].reshape(s, CONV_CH)


def _mlp_fwd(x1, g, w1_all, w2_all, l, tb, tf):
    s, d = x1.shape
    ff = w1_all.shape[1] * w1_all.shape[3]
    nj = ff // tf

    def body(x_ref, g_ref, w1_ref, w2_ref, x2_ref, h2_ref, ap_ref, acc):
        j = pl.program_id(1)

        @pl.when(j == 0)
        def _():
            xv = x_ref[...]
            h2_ref[...] = ((xv * _rms_scale(xv)) * g_ref[...]).astype(BF)
            acc[...] = jnp.zeros_like(acc)

        ap = _nn(h2_ref[...], w1_ref[...])
        ap_ref[...] = ap.astype(BF)
        a = jnp.square(jnp.maximum(ap, 0.0)).astype(BF)
        acc[...] += _nn(a, w2_ref[...])

        @pl.when(j == nj - 1)
        def _():
            x2_ref[...] = x_ref[...] + acc[...]

    return pl.pallas_call(
        body, grid=(s // tb, nj), name="mlp_fwd",
        in_specs=[pl.BlockSpec((tb, d), lambda i, j: (i, 0)), _whole((1, d)),
                  pl.BlockSpec((None, None, d, tf), lambda i, j: (l, j, 0, 0)),
                  pl.BlockSpec((None, None, tf, d), lambda i, j: (l, j, 0, 0))],
        out_specs=[pl.BlockSpec((tb, d), lambda i, j: (i, 0)), pl.BlockSpec((tb, d), lambda i, j: (i, 0)),
                   pl.BlockSpec((tb, tf), lambda i, j: (i, j))],
        out_shape=[jax.ShapeDtypeStruct((s, d), F32), jax.ShapeDtypeStruct((s, d), BF),
                   jax.ShapeDtypeStruct((s, ff), BF)],
        scratch_shapes=[pltpu.VMEM((tb, d), F32)],
        compiler_params=_cparams("parallel", "arbitrary"),
    )(x1, g, w1_all, w2_all)


def _loss_head(x, g, tgt, tb):
    s, d = x.shape

    def body(x_ref, g_ref, t_ref, dx_ref, loss_ref, dg_ref):
        i = pl.program_id(0)

        @pl.when(i == 0)
        def _():
            loss_ref[...] = jnp.zeros_like(loss_ref)
            dg_ref[...] = jnp.zeros_like(dg_ref)

        xv = x_ref[...]
        r = _rms_scale(xv)
        xhat = xv * r
        err = xhat * g_ref[...] - t_ref[...]
        part = jnp.sum(jnp.mean(jnp.square(err), axis=-1, keepdims=True), axis=0, keepdims=True)
        loss_ref[...] += 0.5 * part
        dy = err * (1.0 / d)
        dg_ref[...] += jnp.sum(dy * xhat, axis=0, keepdims=True)
        dx_ref[...] = _norm_bwd(dy * g_ref[...], xhat, r)

    return pl.pallas_call(
        body, grid=(s // tb,), name="loss_head",
        in_specs=[_rows(tb, d), _whole((1, d)), _rows(tb, d)],
        out_specs=[_rows(tb, d), _whole((HALO, 128)), _whole((HALO, d))],
        out_shape=[jax.ShapeDtypeStruct((s, d), F32), jax.ShapeDtypeStruct((HALO, 128), F32),
                   jax.ShapeDtypeStruct((HALO, d), F32)],
        compiler_params=_cparams("arbitrary"),
    )(x, g, tgt)


def _mlp_bwd(dx2, x1, ap, g, w1_all, w2_all, l, tb, tf):
    s, d = x1.shape
    ff = ap.shape[1]
    nj = ff // tf

    def body(dx2_ref, x1_ref, ap_ref, g_ref, w1_ref, w2_ref, dx1_ref, dap_ref, dg_ref, acc):
        i, j = pl.program_id(0), pl.program_id(1)

        @pl.when((i == 0) & (j == 0))
        def _():
            dg_ref[...] = jnp.zeros_like(dg_ref)

        @pl.when(j == 0)
        def _():
            acc[...] = jnp.zeros_like(acc)

        da = _nt(dx2_ref[...].astype(BF), w2_ref[...])
        dap = (da * (2.0 * jnp.maximum(ap_ref[...].astype(F32), 0.0))).astype(BF)
        dap_ref[...] = dap
        acc[...] += _nt(dap, w1_ref[...])

        @pl.when(j == nj - 1)
        def _():
            xv = x1_ref[...]
            r = _rms_scale(xv)
            xhat = xv * r
            dh = acc[...]
            dg_ref[...] += jnp.sum(dh * xhat, axis=0, keepdims=True)
            dx1_ref[...] = dx2_ref[...] + _norm_bwd(dh * g_ref[...], xhat, r)

    return pl.pallas_call(
        body, grid=(s // tb, nj), name="mlp_bwd",
        in_specs=[pl.BlockSpec((tb, d), lambda i, j: (i, 0)), pl.BlockSpec((tb, d), lambda i, j: (i, 0)),
                  pl.BlockSpec((tb, tf), lambda i, j: (i, j)),
                  _whole((1, d)), pl.BlockSpec((None, None, d, tf), lambda i, j: (l, j, 0, 0)),
                  pl.BlockSpec((None, None, tf, d), lambda i, j: (l, j, 0, 0))],
        out_specs=[pl.BlockSpec((tb, d), lambda i, j: (i, 0)), pl.BlockSpec((tb, tf), lambda i, j: (i, j)),
                   _whole((HALO, d))],
        out_shape=[jax.ShapeDtypeStruct((s, d), F32), jax.ShapeDtypeStruct((s, ff), BF),
                   jax.ShapeDtypeStruct((HALO, d), F32)],
        scratch_shapes=[pltpu.VMEM((tb, d), F32)],
        compiler_params=_cparams("arbitrary", "arbitrary"),
    )(dx2, x1, ap, g, w1_all, w2_all)


def _wgrad(a, b, tm, tn, ts, name, relu2=False):
    s, m = a.shape
    n = b.shape[1]
    ns = s // ts

    def body(a_ref, b_ref, o_ref, acc):
        k = pl.program_id(2)

        @pl.when(k == 0)
        def _():
            acc[...] = jnp.zeros_like(acc)

        av = a_ref[...]
        if relu2:
            av = jnp.square(jnp.maximum(av.astype(F32), 0.0)).astype(BF)
        acc[...] += _tn(av, b_ref[...].astype(BF))

        @pl.when(k == ns - 1)
        def _():
            o_ref[...] = acc[...].astype(BF)

    return pl.pallas_call(
        body, grid=(m // tm, n // tn, ns), name=name,
        in_specs=[pl.BlockSpec((ts, tm), lambda i, j, k: (k, i)), pl.BlockSpec((ts, tn), lambda i, j, k: (k, j))],
        out_specs=pl.BlockSpec((tm, tn), lambda i, j, k: (i, j)),
        out_shape=jax.ShapeDtypeStruct((m, n), BF),
        scratch_shapes=[pltpu.VMEM((tm, tn), F32)],
        compiler_params=_cparams("parallel", "parallel", "arbitrary"),
    )(a, b)


def _mix_bwd(dx1, ya, yb, yc, lse_c, sink_row, gg, wo_all, l, tb):
    s, d = dx1.shape

    def body(dx_ref, ya_ref, yb_ref, yc_ref, lse_ref, sink_ref, gg_ref, wo_ref,
             n_ref, dya_ref, dyc_ref, da_ref, dc_ref, dyb_ref, dg_ref, dsink_ref):
        i = pl.program_id(0)

        @pl.when(i == 0)
        def _():
            dg_ref[...] = jnp.zeros_like(dg_ref)
            dsink_ref[...] = jnp.zeros_like(dsink_ref)

        dn = _nt(dx_ref[...].astype(BF), wo_ref[...].reshape(MIX_WIDTH, d))
        ys = [ya_ref[...], yb_ref[...], yc_ref[...]]
        rs = [_rms_scale(v) for v in ys]
        nhat = jnp.concatenate([v * r for v, r in zip(ys, rs)], axis=1)
        gg = gg_ref[...]
        n_ref[...] = (nhat * gg).astype(BF)
        dg_ref[...] += jnp.sum(dn * nhat, axis=0, keepdims=True)
        dnh = dn * gg
        bounds = [(0, A_WIDTH), (A_WIDTH, A_WIDTH + CONV_CH), (A_WIDTH + CONV_CH, MIX_WIDTH)]
        dys = [_norm_bwd(dnh[:, lo:hi], nhat[:, lo:hi], r) for (lo, hi), r in zip(bounds, rs)]
        dyb_ref[...] = dys[1]
        for dy, y, dy_ref, dd_ref in ((dys[0], ys[0], dya_ref, da_ref), (dys[2], ys[2], dyc_ref, dc_ref)):
            dy_ref[...] = dy
            t = dy * y
            for h in range(N_HEADS):
                dd_ref[:, _hs(h)] = jnp.broadcast_to(jnp.sum(t[:, _hs(h)], axis=1, keepdims=True), (tb, HEAD_DIM))
        dsink_ref[...] -= jnp.sum(jnp.exp(sink_ref[...] - lse_ref[...]) * dc_ref[...], axis=0, keepdims=True)

    return pl.pallas_call(
        body, grid=(s // tb,), name="mix_bwd",
        in_specs=[_rows(tb, d), _rows(tb, A_WIDTH), _rows(tb, CONV_CH), _rows(tb, A_WIDTH), _rows(tb, A_WIDTH),
                  _whole((1, A_WIDTH)), _whole((1, MIX_WIDTH)), _layer((N_CHIPS, MIX_WIDTH // N_CHIPS, d), l)],
        out_specs=[_rows(tb, MIX_WIDTH), _rows(tb, A_WIDTH), _rows(tb, A_WIDTH), _rows(tb, A_WIDTH),
                   _rows(tb, A_WIDTH), _rows(tb, CONV_CH), _whole((HALO, MIX_WIDTH)), _whole((HALO, A_WIDTH))],
        out_shape=[jax.ShapeDtypeStruct((s, MIX_WIDTH), BF), jax.ShapeDtypeStruct((s, A_WIDTH), F32),
                   jax.ShapeDtypeStruct((s, A_WIDTH), F32), jax.ShapeDtypeStruct((s, A_WIDTH), F32),
                   jax.ShapeDtypeStruct((s, A_WIDTH), F32), jax.ShapeDtypeStruct((s, CONV_CH), F32),
                   jax.ShapeDtypeStruct((HALO, MIX_WIDTH), F32), jax.ShapeDtypeStruct((HALO, A_WIDTH), F32)],
        compiler_params=_cparams("arbitrary"),
    )(dx1, ya, yb, yc, lse_c, sink_row, gg, wo_all)


def _attn_bwd(z, dy, lse, dd, dil, kw, kcol, vcol, n_rep, max_dist, name):
    s, zw = z.shape
    grid = _p_grid(s, dil)
    n_kv = N_HEADS // n_rep

    def body(q_ref, kp_ref, kc_ref, vp_ref, vc_ref, dy_ref, lse_ref, dd_ref, dq_ref, dkp_ref, dkc_ref, dvp_ref, dvc_ref):
        mask = _band_mask(pl.program_id(len(grid) - 1), dil, max_dist)
        for kh in range(n_kv):
            k2 = jnp.concatenate([_ld(kp_ref, _hs(kh)), _ld(kc_ref, _hs(kh))], axis=0).astype(BF)
            v2 = jnp.concatenate([_ld(vp_ref, _hs(kh)), _ld(vc_ref, _hs(kh))], axis=0).astype(BF)
            dk2 = jnp.zeros((2 * TQ, HEAD_DIM), F32)
            dv2 = jnp.zeros((2 * TQ, HEAD_DIM), F32)
            for h in range(kh * n_rep, (kh + 1) * n_rep):
                q = _ld(q_ref, _hs(h)).astype(BF)
                lse_h = _ld(lse_ref, slice(h * HEAD_DIM, h * HEAD_DIM + 1))
                dd_h = _ld(dd_ref, slice(h * HEAD_DIM, h * HEAD_DIM + 1))
                dyh = _ld(dy_ref, _hs(h)).astype(BF)
                sc = jnp.where(mask, _nt(q, k2) * SCALE, NEG)
                p = jnp.exp(sc - lse_h)
                dp = _nt(dyh, v2)
                ds = ((p * (dp - dd_h)) * SCALE).astype(BF)
                _st(dq_ref, _hs(h), _nn(ds, k2))
                dk2 = dk2 + _tn(ds, q)
                dv2 = dv2 + _tn(p.astype(BF), dyh)
            _st(dkp_ref, _hs(kh), dk2[:TQ])
            _st(dkc_ref, _hs(kh), dk2[TQ:])
            _st(dvp_ref, _hs(kh), dv2[:TQ])
            _st(dvc_ref, _hs(kh), dv2[TQ:])

    args = [_strips(z)] * 5 + [_strips(a) for a in (dy, lse, dd)]
    in_specs = [_p_spec(dil, A_WIDTH, 0), _p_spec(dil, kw, kcol, True), _p_spec(dil, kw, kcol),
                _p_spec(dil, kw, vcol, True), _p_spec(dil, kw, vcol)] + [_p_spec(dil, A_WIDTH, 0)] * 3
    out_specs = [_p_spec(dil, A_WIDTH, 0)] + [_p_spec(dil, kw, 0)] * 4
    na = s // N_STRIPS
    out_shape = [jax.ShapeDtypeStruct((4, 4, na, A_WIDTH), F32)] + [jax.ShapeDtypeStruct((4, 4, na, kw), F32)] * 4
    res = pl.pallas_call(
        body, grid=grid, name=name, in_specs=in_specs, out_specs=out_specs, out_shape=out_shape,
        compiler_params=_cparams(*(("parallel",) * len(grid))),
    )(*args)
    return [res[0].reshape(s, A_WIDTH)] + [a.reshape(s, kw) for a in res[1:]]


DZ_TA = 16


def _dz_assemble(parts_a, parts_c, dyb, zb, cw):
    s = zb.shape[0]
    na = s // N_STRIPS
    nb = na // DZ_TA

    def ahead(w, k):
        return pl.BlockSpec((4, 4, DZ_TA, w), lambda i: (0, 0, jnp.minimum(i + k, nb - 1), 0))

    args, in_specs = [], []
    for dil, (dq, dkp, dkc, dvp, dvc) in zip(DILATIONS + (1,), parts_a + [parts_c]):
        w = dkp.shape[1]
        here = _strip_rows(DZ_TA, w)
        if dil == 1:
            args += [dq, dkp, dkp, dkc, dvp, dvp, dvc]
            in_specs += [_strip_rows(DZ_TA, A_WIDTH), here, ahead(w, 1), here, here, ahead(w, 1), here]
        else:
            k = 8 * dil // DZ_TA
            args += [dq, dkp, dkc, dvp, dvc]
            in_specs += [_strip_rows(DZ_TA, A_WIDTH), ahead(w, k), here, ahead(w, k), here]
    n_att = len(args)
    args = [_strips(a) for a in args] + [_strips(dyb), _strips(dyb), _strips(zb), _strips(zb), _strips(zb), cw]
    in_specs += [_strip_rows(DZ_TA, CONV_CH), _next_rows(DZ_TA, CONV_CH, nb), _strip_rows(DZ_TA, ZB_W),
                 _prev_rows(DZ_TA, ZB_W), _next_rows(DZ_TA, ZB_W, nb), _whole((HALO, CONV_CH))]

    def body(*refs):
        att = list(refs[:n_att])
        dyb_ref, dybn_ref, zb_ref, zbp_ref, zbn_ref, cw_ref, dz_ref, dcw_ref = refs[n_att:]
        i = pl.program_id(0)

        @pl.when(i == 0)
        def _():
            dcw_ref[...] = jnp.zeros_like(dcw_ref)

        def shifted(dil):
            if dil == 1:
                dq_r, kp0, kp1, dkc_r, vp0, vp1, dvc_r = [att.pop(0) for _ in range(7)]
                live = i + 1 < nb
                half = DZ_TA // 2
                dkp = jnp.concatenate([kp0[:, :, half:, :], jnp.where(live, kp1[:, :, :half, :], 0.0)], axis=2)
                dvp = jnp.concatenate([vp0[:, :, half:, :], jnp.where(live, vp1[:, :, :half, :], 0.0)], axis=2)
            else:
                dq_r, dkp_r, dkc_r, dvp_r, dvc_r = [att.pop(0) for _ in range(5)]
                live = i + 8 * dil // DZ_TA < nb
                dkp, dvp = jnp.where(live, dkp_r[...], 0.0), jnp.where(live, dvp_r[...], 0.0)
            return dq_r[...], dkc_r[...] + dkp, dvc_r[...] + dvp

        dq, dk, dv = shifted(DILATIONS[0])
        for dil in DILATIONS[1:]:
            dq2, dk2, dv2 = shifted(dil)
            dq, dk, dv = dq + dq2, dk + dk2, dv + dv2
        dz_ref[:, :, :, 0:A_WIDTH] = dq.astype(BF)
        dz_ref[:, :, :, A_WIDTH:2 * A_WIDTH] = dk.astype(BF)
        dz_ref[:, :, :, 2 * A_WIDTH:ZA_W] = dv.astype(BF)
        dq, dk, dv = shifted(1)
        c0 = ZA_W + ZB_W
        dz_ref[:, :, :, c0:c0 + A_WIDTH] = dq.astype(BF)
        dz_ref[:, :, :, c0 + A_WIDTH:c0 + A_WIDTH + C_KV_WIDTH] = dk.astype(BF)
        dz_ref[:, :, :, c0 + A_WIDTH + C_KV_WIDTH:IN_WIDTH] = dv.astype(BF)

        cw = cw_ref[...]
        prev = jnp.where(i > 0, zbp_ref[...], 0.0)
        gb, gc, xb, u, u1, u2, c = _conv_strips(zb_ref[...], prev, cw)
        dyb = dyb_ref[...]
        dc = [_strip(dyb, b) * gb[b] for b in range(N_STRIPS)]
        dcn = jnp.where(i + 1 < nb, dybn_ref[...] * zbn_ref[:, :, :CONV_CH], 0.0)
        wrapped = [_shift_up(dc[0], 1, dcn[0]), _shift_up(dc[1], 1, dcn[1])]
        upd = [jnp.zeros((1, CONV_CH), F32)] * 3
        for b in range(N_STRIPS):
            dc1 = dc[b + 1] if b + 1 < N_STRIPS else wrapped[0]
            dc2 = dc[b + 2] if b + 2 < N_STRIPS else wrapped[b + 2 - N_STRIPS]
            du = cw[2:3, :] * dc[b] + cw[1:2, :] * dc1 + cw[0:1, :] * dc2
            f, e = b % 4, b // 4
            dz_ref[f, e, :, ZA_W:ZA_W + CONV_CH] = (_strip(dyb, b) * c[b]).astype(BF)
            dz_ref[f, e, :, ZA_W + CONV_CH:ZA_W + 2 * CONV_CH] = (du * xb[b]).astype(BF)
            dz_ref[f, e, :, ZA_W + 2 * CONV_CH:c0] = (du * gc[b]).astype(BF)
            for t, uu in enumerate((u2[b], u1[b], u[b])):
                upd[t] = upd[t] + jnp.sum(dc[b] * uu, axis=0, keepdims=True)
        row = lax.broadcasted_iota(jnp.int32, (HALO, CONV_CH), 0)
        tile = jnp.zeros((HALO, CONV_CH), F32)
        for t in range(3):
            tile = jnp.where(row == t, upd[t], tile)
        dcw_ref[...] += tile

    dz, dcw = pl.pallas_call(
        body, grid=(nb,), name="dz_assemble", in_specs=in_specs,
        out_specs=[_strip_rows(DZ_TA, IN_WIDTH), _whole((HALO, CONV_CH))],
        out_shape=[jax.ShapeDtypeStruct((4, 4, na, IN_WIDTH), BF), jax.ShapeDtypeStruct((HALO, CONV_CH), F32)],
        compiler_params=_cparams("arbitrary"),
    )(*args)
    return dz.reshape(s, IN_WIDTH), dcw


def _qkv_bwd(dz, dx1, x, g, w_all, l, tb):
    s, d = x.shape

    def body(dz_ref, dx1_ref, x_ref, g_ref, w_ref, dx_ref, dg_ref):
        i = pl.program_id(0)

        @pl.when(i == 0)
        def _():
            dg_ref[...] = jnp.zeros_like(dg_ref)

        n = IN_WIDTH // N_CHIPS
        dh = _nt(dz_ref[:, 0:n], w_ref[0])
        for k in range(1, N_CHIPS):
            dh = dh + _nt(dz_ref[:, k * n:(k + 1) * n], w_ref[k])
        xv = x_ref[...]
        r = _rms_scale(xv)
        xhat = xv * r
        dg_ref[...] += jnp.sum(dh * xhat, axis=0, keepdims=True)
        dx_ref[...] = dx1_ref[...] + _norm_bwd(dh * g_ref[...], xhat, r)

    return pl.pallas_call(
        body, grid=(s // tb,), name="qkv_bwd",
        in_specs=[_rows(tb, IN_WIDTH), _rows(tb, d), _rows(tb, d), _whole((1, d)),
                  _layer((N_CHIPS, d, IN_WIDTH // N_CHIPS), l)],
        out_specs=[_rows(tb, d), _whole((HALO, d))],
        out_shape=[jax.ShapeDtypeStruct((s, d), F32), jax.ShapeDtypeStruct((HALO, d), F32)],
        compiler_params=_cparams("arbitrary"),
    )(dz, dx1, x, g, w_all)


def _tile_rows(rows):
    return jnp.pad(rows, ((0, HALO - rows.shape[0]), (0, 0)))


def _to_strips(a, after, name):
    s, d = a.shape
    na = s // N_STRIPS
    ta = min(32, na)

    def body(a_ref, after_ref, o_ref):
        for b in range(N_STRIPS):
            o_ref[b % 4, b // 4] = a_ref[:, b, :]

    return pl.pallas_call(
        body, grid=(na // ta,), name=name,
        in_specs=[pl.BlockSpec((ta, N_STRIPS, d), lambda i: (i, 0, 0)), ANY], out_specs=_strip_rows(ta, d),
        out_shape=jax.ShapeDtypeStruct((4, 4, na, d), a.dtype), compiler_params=_cparams("parallel"),
    )(a.reshape(na, N_STRIPS, d), after).reshape(s, d)


def _from_strips(a, name):
    s, d = a.shape
    na = s // N_STRIPS
    ta = min(32, na)

    def body(a_ref, o_ref):
        for b in range(N_STRIPS):
            o_ref[:, b, :] = a_ref[b % 4, b // 4]

    return pl.pallas_call(
        body, grid=(na // ta,), name=name, in_specs=[_strip_rows(ta, d)],
        out_specs=pl.BlockSpec((ta, N_STRIPS, d), lambda i: (i, 0, 0)),
        out_shape=jax.ShapeDtypeStruct((na, N_STRIPS, d), a.dtype), compiler_params=_cparams("parallel"),
    )(_strips(a)).reshape(s, d)


def _local_step(x, tgt, fetch, ff, sinks, g_mix, g_group, g_mlp, g_final, emit):
    s, d = x.shape
    depth = g_mix.shape[0]
    tb = min(512, s)
    tf = ff // N_CHIPS
    ts = min(1024, s)
    saved = []
    for l in range(depth):
        w_in, _, _, _, conv_w = fetch(0, l, x)
        cw = _tile_rows(conv_w[l])
        sk = jnp.repeat(sinks[l].reshape(N_HEADS), HEAD_DIM)[None]
        h, za, zb, zc = _qkv_fwd(x, g_mix[l][None], w_in, l, tb)
        parts_a = [_attn_fwd(za, dil, A_WIDTH, 1, 2, 1, A_MAX_DIST, "attn_a_fwd_%d" % dil) for dil in DILATIONS]
        part_c = _attn_fwd(zc, 1, C_KV_WIDTH, 3, 4, C_GROUP, C_MAX_DIST, "attn_c_fwd")
        ya, lse_a, yc, lse_c = _attn_merge(parts_a, part_c, sk, tb)
        w_in, w_o, w1, w2, _ = fetch(1, l, yc)
        x1, yb = _mix_fwd(x, ya, yc, zb, cw, g_group[l][None], w_o, l, tb)
        x2, h2, ap = _mlp_fwd(x1, g_mlp[l][None], w1, w2, l, ts, tf)
        saved.append((x, h, za, zb, zc, ya, lse_a, yc, lse_c, yb, x1, h2, ap, cw, sk))
        x = x2
    dx, loss_tile, dg_final = _loss_head(x, g_final[None], tgt, tb)
    grads = [None] * depth
    tok = jnp.zeros((), F32)
    for l in reversed(range(depth)):
        x0, h, za, zb, zc, ya, lse_a, yc, lse_c, yb, x1, h2, ap, cw, sk = saved[l]
        dx1, dap, dg_mlp = _mlp_bwd(dx, x1, ap, g_mlp[l][None] + tok, w1, w2, l, ts, tf)
        tok = emit(l, 3, _wgrad(ap, dx, min(1024, ff), d, ts, "wgrad_ff_out", relu2=True))
        tok = tok + emit(l, 2, _wgrad(h2, dap, d, min(1024, ff), ts, "wgrad_ff_in"))
        n, dya, dyc, dd_a, dd_c, dyb, dg_group, dsink = _mix_bwd(dx1, ya, yb, yc, lse_c, sk, g_group[l][None] + tok,
                                                                 w_o, l, tb)
        tok = emit(l, 1, _wgrad(n, dx1, MIX_WIDTH, d, ts, "wgrad_o"))
        cw = cw + tok
        parts_a = [_attn_bwd(za, dya, lse_a, dd_a, dil, A_WIDTH, 1, 2, 1, A_MAX_DIST, "attn_a_bwd_%d" % dil)
                   for dil in DILATIONS]
        parts_c = _attn_bwd(zc, dyc, lse_c, dd_c, 1, C_KV_WIDTH, 3, 4, C_GROUP, C_MAX_DIST, "attn_c_bwd")
        dz, dcw = _dz_assemble(parts_a, parts_c, dyb, zb, cw)
        dx, dg_mix = _qkv_bwd(dz, dx1, x0, g_mix[l][None], w_in, l, tb)
        tok = emit(l, 0, _wgrad(h, dz, d, IN_WIDTH // 4, ts, "wgrad_in"))
        grads[l] = (dcw, dsink, dg_mix, dg_group, dg_mlp)
    return loss_tile, dx, grads, dg_final


ANY = pl.BlockSpec(memory_space=pl.ANY)
SHARD_AXES = (2, 1, 2, 1)
N_BIG = len(SHARD_AXES)
N_CHIPS = 4
N_DEV = 8


def _mesh_pos():
    return lax.axis_index("x"), lax.axis_index("y"), lax.axis_index("c")


def _flip(v, bit):
    return 1 - v if bit else v


def _place_shard(shard, chip_arr, name):
    _, rows, cols = shard.shape
    tr = min(256, rows)

    def body(chip_ref, x_ref, o_ref):
        o_ref[...] = x_ref[...].astype(BF)

    return pl.pallas_call(
        body, name=name,
        grid_spec=pltpu.PrefetchScalarGridSpec(
            num_scalar_prefetch=1, grid=(2, rows // tr),
            in_specs=[pl.BlockSpec((None, tr, cols), lambda l, i, chip: (l, i, 0))],
            out_specs=pl.BlockSpec((None, None, tr, cols), lambda l, i, chip: (l, chip[0], i, 0))),
        out_shape=jax.ShapeDtypeStruct((2, N_CHIPS, rows, cols), BF),
        compiler_params=_cparams("parallel", "parallel"),
    )(chip_arr, shard)


HBM = pl.BlockSpec(memory_space=pltpu.HBM)
SEM = pl.BlockSpec(memory_space=pltpu.SEMAPHORE)
EFFECT = pltpu.SideEffectType.DATAFLOW_SIDE_EFFECTING

GATHER_GROUPS = (((0, 0),), ((1, 0), (2, 0), (3, 0)), ((0, 1),), ((1, 1), (2, 1), (3, 1)))
GATHER_STARTS = ((0,), (1,), (2, 3))


def _gather_copies(arrs, group, send_sems, recv_sems):
    x, y, c = _mesh_pos()
    me = 2 * x + y
    out = []
    for i, (w, layer) in enumerate(group):
        mine = arrs[w].at[layer, me]
        for j, (qx, qy) in enumerate([(1 - x, y), (x, 1 - y), (1 - x, 1 - y)]):
            landed = arrs[w].at[layer, 2 * qx + qy]
            out.append(tuple(pltpu.make_async_remote_copy(
                src_ref=piece, dst_ref=piece, send_sem=send_sems.at[i * 3 + j], recv_sem=recv_sems.at[i * 3 + j],
                device_id=(qx, qy, c), device_id_type=MESH) for piece in (mine, landed)))
    return out


def _conv_copies(conv_src, conv_dst, send_sems, recv_sems):
    x, y, c = _mesh_pos()
    out = []
    for j, (qx, qy) in enumerate([(1 - x, y), (x, 1 - y), (1 - x, 1 - y)]):
        out.append(tuple(pltpu.make_async_remote_copy(
            src_ref=conv_src, dst_ref=conv_dst.at[q], send_sem=send_sems.at[j], recv_sem=recv_sems.at[j],
            device_id=(qx, qy, c), device_id_type=MESH) for q in (2 * x + y, 2 * qx + qy)))
    return out


def _gather_start(groups, arrs, conv, name, after=None):
    n_sems = 2 * (len(groups) + (conv is not None))
    mats = sorted({w for g in groups for w, _ in GATHER_GROUPS[g]})

    def body(*refs):
        arrs_ref = [None] * N_BIG
        for w, ref in zip(mats, refs):
            arrs_ref[w] = ref
        sems = refs[n_in:n_in + n_sems]
        if conv is not None:
            for cp, _ in _conv_copies(refs[len(mats)], refs[len(mats) + 1], sems[-2], sems[-1]):
                cp.start()
        for k, g in enumerate(groups):
            for cp, _ in _gather_copies(arrs_ref, GATHER_GROUPS[g], sems[2 * k], sems[2 * k + 1]):
                cp.start()

    sem_shapes = []
    for n in [len(GATHER_GROUPS[g]) for g in groups] + ([1] if conv is not None else []):
        sem_shapes += [pltpu.SemaphoreType.DMA((3 * n,))] * 2
    operands = [arrs[w] for w in mats] + ([] if conv is None else list(conv))
    n_op = len(operands)
    n_in = n_op + (after is not None)
    res = pl.pallas_call(
        body, name=name,
        out_shape=tuple(sem_shapes) + tuple(pltpu.HBM(a.shape, a.dtype) for a in operands),
        in_specs=(HBM,) * n_op + (ANY,) * (n_in - n_op), out_specs=(SEM,) * n_sems + (HBM,) * n_op,
        input_output_aliases={i: n_sems + i for i in range(n_op)},
        compiler_params=pltpu.CompilerParams(has_side_effects=EFFECT),
    )(*[pltpu.with_memory_space_constraint(a, pltpu.HBM) for a in operands], *([] if after is None else [after]))
    arrs = list(arrs)
    for w, a in zip(mats, res[n_sems:]):
        arrs[w] = a
    return res[:n_sems], arrs, list(res[n_sems + len(mats):])


def _gather_wait(k, sems, arrs, conv, after, name):
    group = GATHER_GROUPS[k]
    mats = sorted({w for w, _ in group})
    n_conv = 0 if conv is None else 2

    def body(*refs):
        local = refs[:len(mats)]
        arrs_ref = [None] * N_BIG
        for w, ref in zip(mats, local):
            arrs_ref[w] = ref
        pos = len(mats) + n_conv
        copies = _gather_copies(arrs_ref, group, refs[pos], refs[pos + 1])
        if conv is not None:
            copies += _conv_copies(refs[len(mats)], refs[len(mats) + 1], refs[pos + 2], refs[pos + 3])
        for send, recv in copies:
            recv.wait_recv()
            send.wait_send()

    operands = [arrs[w] for w in mats] + ([] if conv is None else [conv[1], conv[2]])
    sem_ops = list(sems) + ([] if conv is None else list(conv[0]))
    n_op = len(operands)
    res = pl.pallas_call(
        body, name=name, out_shape=tuple(pltpu.HBM(a.shape, a.dtype) for a in operands),
        in_specs=(HBM,) * n_op + (SEM,) * len(sem_ops) + (ANY,), out_specs=(HBM,) * n_op,
        input_output_aliases={i: i for i in range(n_op)},
        compiler_params=pltpu.CompilerParams(has_side_effects=EFFECT),
    )(*operands, *sem_ops, after)
    arrs = list(arrs)
    for w, a in zip(mats, res):
        arrs[w] = a
    return arrs, (res[-1] if conv is not None else None)


def _grad_shard(ref, w, chip, n):
    start = pl.multiple_of(chip * n, 128)
    if SHARD_AXES[w] == 2:
        return ref.at[:, pl.ds(start, n)]
    return ref.at[pl.ds(start, n), :]


def _slot_shape(g, w):
    shape = list(g.shape)
    shape[SHARD_AXES[w] - 1] //= N_CHIPS
    return (N_DEV - 1,) + tuple(shape)


def _scatter_copies(g_ref, land_ref, send_sems, recv_sems, layer, w):
    x, y, c = _mesh_pos()
    n = g_ref.shape[SHARD_AXES[w] - 1] // N_CHIPS
    out = []
    for r in range(1, N_DEV):
        tx, ty, tc = _flip(x, r & 4), _flip(y, r & 2), _flip(c, r & 1)
        cp = pltpu.make_async_remote_copy(
            src_ref=_grad_shard(g_ref, w, 2 * tx + ty, n), dst_ref=land_ref.at[r - 1], send_sem=send_sems.at[r - 1],
            recv_sem=recv_sems.at[r - 1], device_id=(tx, ty, tc), device_id_type=MESH)
        out.append((cp, (c != layer) if r & 1 else (c == layer)))
    return out


def _scatter_start(g, land, layer, w, name):
    def body(g_ref, land_ref, send_sems, recv_sems, g_thru, land_thru, token):
        for cp, mine in _scatter_copies(g_ref, land_ref, send_sems, recv_sems, layer, w):
            @pl.when(mine)
            def _():
                cp.start()
        token[...] = jnp.zeros_like(token)

    return pl.pallas_call(
        body, name=name,
        out_shape=(pltpu.SemaphoreType.DMA((N_DEV - 1,)), pltpu.SemaphoreType.DMA((N_DEV - 1,)),
                   pltpu.HBM(g.shape, g.dtype), pltpu.HBM(land.shape, land.dtype),
                   jax.ShapeDtypeStruct((HALO, 128), F32)),
        in_specs=(HBM, HBM), out_specs=(SEM, SEM, HBM, HBM, pl.BlockSpec(memory_space=pltpu.VMEM)),
        input_output_aliases={0: 2, 1: 3}, compiler_params=pltpu.CompilerParams(has_side_effects=EFFECT),
    )(pltpu.with_memory_space_constraint(g, pltpu.HBM), pltpu.with_memory_space_constraint(land, pltpu.HBM))


def _scatter_wait(started, land, after, w, name):
    def body(g0_ref, g1_ref, land_ref, ss0, rs0, ss1, rs1, after_ref, g0_out, g1_out, land_out):
        c = lax.axis_index("c")
        for layer, g_ref, ss, rs in ((0, g0_ref, ss0, rs0), (1, g1_ref, ss1, rs1)):
            for cp, mine in _scatter_copies(g_ref, land_ref, ss, rs, layer, w):
                @pl.when(mine)
                def _():
                    cp.wait_send()

                @pl.when(c == layer)
                def _():
                    cp.wait_recv()

    (ss0, rs0, g0), (ss1, rs1, g1) = started
    return pl.pallas_call(
        body, name=name,
        out_shape=(pltpu.HBM(g0.shape, g0.dtype), pltpu.HBM(g1.shape, g1.dtype), pltpu.HBM(land.shape, land.dtype)),
        in_specs=(HBM, HBM, HBM, SEM, SEM, SEM, SEM, ANY), out_specs=(HBM, HBM, HBM),
        input_output_aliases={0: 0, 1: 1, 2: 2}, compiler_params=pltpu.CompilerParams(has_side_effects=EFFECT),
    )(g0, g1, land, ss0, rs0, ss1, rs1, after)


def _sum_slots(g0, g1, slots, w, pos_arr, name):
    _, rows, cols = slots.shape
    tr = min(256, rows)
    nr = rows // tr
    if SHARD_AXES[w] == 2:
        own = pl.BlockSpec((tr, cols), lambda i, pos: (i, pos[0]))
    else:
        own = pl.BlockSpec((tr, cols), lambda i, pos: (pos[0] * nr + i, 0))

    def body(pos_ref, own0_ref, own1_ref, s_ref, o_ref):
        acc = jnp.where(pos_ref[1] == 0, own0_ref[...], own1_ref[...]).astype(F32)
        for r in range(N_DEV - 1):
            acc = acc + s_ref[r].astype(F32)
        o_ref[...] = acc

    return pl.pallas_call(
        body, name=name,
        grid_spec=pltpu.PrefetchScalarGridSpec(
            num_scalar_prefetch=1, grid=(nr,),
            in_specs=[own, own, pl.BlockSpec((N_DEV - 1, tr, cols), lambda i, pos: (0, i, 0))],
            out_specs=pl.BlockSpec((tr, cols), lambda i, pos: (i, 0))),
        out_shape=jax.ShapeDtypeStruct((rows, cols), F32), compiler_params=_cparams("parallel"),
    )(pos_arr, g0, g1, slots)


def _swap_layers(halves):
    def body(*refs):
        srcs, dsts = refs[:N_BIG], refs[N_BIG:2 * N_BIG]
        send_sems, recv_sems = refs[2 * N_BIG:]
        x, y, c = _mesh_pos()
        sends = [pltpu.make_async_remote_copy(src_ref=srcs[w], dst_ref=dsts[w], send_sem=send_sems.at[w],
                                              recv_sem=recv_sems.at[w], device_id=(x, y, 1 - c), device_id_type=MESH)
                 for w in range(N_BIG)]
        for cp in sends:
            cp.start()
        for cp in sends:
            cp.wait_recv()
        for cp in sends:
            cp.wait_send()

    return pl.pallas_call(
        body, name="swap_layers", in_specs=[ANY] * N_BIG, out_specs=[ANY] * N_BIG,
        out_shape=[jax.ShapeDtypeStruct(h.shape, h.dtype) for h in halves],
        scratch_shapes=[pltpu.SemaphoreType.DMA((N_BIG,)), pltpu.SemaphoreType.DMA((N_BIG,))],
    )(*halves)


def _adamw_math(w, g, m, v):
    m = ADAM_B1 * m + (1.0 - ADAM_B1) * g
    v = ADAM_B2 * v + (1.0 - ADAM_B2) * jnp.square(g)
    m_hat = m / (1.0 - ADAM_B1 ** ADAM_STEP)
    v_hat = v / (1.0 - ADAM_B2 ** ADAM_STEP)
    delta = -ADAM_LR * (m_hat / (jnp.sqrt(v_hat) + ADAM_EPS) + ADAM_WD * w)
    return delta, m, v


def _adamw(w, g_own, g_other, m, v, pos_arr, name):
    shape = w.shape
    _, rows, cols = shape
    tr = min(256, rows)

    def body(pos_ref, w_ref, own_ref, other_ref, m_ref, v_ref, g_ref, d_ref, m2_ref, v2_ref):
        g = jnp.where(pl.program_id(0) == pos_ref[1], own_ref[...], other_ref[...])
        g_ref[...] = g
        d_ref[...], m2_ref[...], v2_ref[...] = _adamw_math(w_ref[...], g, m_ref[...], v_ref[...])

    full = pl.BlockSpec((None, tr, cols), lambda l, i, pos: (l, i, 0))
    half = pl.BlockSpec((tr, cols), lambda l, i, pos: (i, 0))
    return pl.pallas_call(
        body, name=name,
        grid_spec=pltpu.PrefetchScalarGridSpec(
            num_scalar_prefetch=1, grid=(2, rows // tr),
            in_specs=[full, half, half, full, full], out_specs=[full] * 4),
        out_shape=[jax.ShapeDtypeStruct(shape, F32)] * 4, compiler_params=_cparams("parallel", "parallel"),
    )(pos_arr, w, g_own, g_other, m, v)


def _small_sync(part, w, m, v):
    rows, cols = part.shape

    def body(p_ref, w_ref, m_ref, v_ref, g_ref, d_ref, m2_ref, v2_ref, slots, send_sems, recv_sems):
        x, y, c = _mesh_pos()
        me = 4 * x + 2 * y + c
        slots[me] = p_ref[...]
        sends = []
        for r in range(1, N_DEV):
            to = (_flip(x, r & 4), _flip(y, r & 2), _flip(c, r & 1))
            sends.append(pltpu.make_async_remote_copy(
                src_ref=p_ref, dst_ref=slots.at[me], send_sem=send_sems.at[r - 1], recv_sem=recv_sems.at[r - 1],
                device_id=to, device_id_type=MESH))
        for cp in sends:
            cp.start()
        for cp in sends:
            cp.wait_recv()
        for cp in sends:
            cp.wait_send()
        g = slots[0]
        for i in range(1, N_DEV):
            g = g + slots[i]
        g_ref[...] = g
        d_ref[...], m2_ref[...], v2_ref[...] = _adamw_math(w_ref[...], g, m_ref[...], v_ref[...])

    vm = pl.BlockSpec(memory_space=pltpu.VMEM)
    return pl.pallas_call(
        body, name="small_sync", in_specs=[vm] * 4, out_specs=[vm] * 4,
        out_shape=[jax.ShapeDtypeStruct((rows, cols), F32)] * 4,
        scratch_shapes=[pltpu.VMEM((N_DEV, rows, cols), F32), pltpu.SemaphoreType.DMA((N_DEV - 1,)),
                        pltpu.SemaphoreType.DMA((N_DEV - 1,))],
    )(part, w, m, v)


def _pack_small(d, g_mix, g_group, g_mlp, g_final, conv_full, sinks, scalar):
    def part(rows):
        return jnp.pad(rows, ((0, HALO - rows.shape[0]), (0, d - rows.shape[1])))
    return jnp.concatenate([part(g_mix), part(g_group), part(g_mlp), part(g_final[None]),
                            part(conv_full.reshape(6, CONV_CH)), part(sinks.reshape(2, N_HEADS)),
                            part(scalar.reshape(1, 1))], axis=0)


def _unpack_small(p, dm):
    return (p[0:2, :dm], p[8:10, :MIX_WIDTH], p[16:18, :dm], p[24, :dm], p[32:38, :CONV_CH].reshape(2, 3, CONV_CH),
            p[40:42, :N_HEADS].reshape(2, 2, C_GROUP), p[48, 0])


def kernel(x, w_in, conv_w, sinks, g_mix, g_group, w_o, g_mlp, w_ff_in, w_ff_out, g_final, loss_target, m_w_in, m_conv_w, m_sinks, m_g_mix, m_g_group, m_w_o, m_g_mlp, m_w_ff_in, m_w_ff_out, m_g_final, v_w_in, v_conv_w, v_sinks, v_g_mix, v_g_group, v_w_o, v_g_mlp, v_w_ff_in, v_w_ff_out, v_g_final):
    d = max(x.shape[2], MIX_WIDTH)
    chip = 2 * lax.axis_index("x") + lax.axis_index("y")
    conv_n = conv_w.shape[2]

    pos_arr = jnp.stack([chip, lax.axis_index("c")]).astype(jnp.int32)
    shards = (w_in, w_o, w_ff_in, w_ff_out)
    conv_tile = jnp.pad(conv_w.reshape(6, conv_n), ((0, HALO - 6), (0, 128 - conv_n)))
    placed = [_place_shard(w_in, pos_arr[:1], "place_shard_0"), None, None, None]
    sems_a, placed, conv_thru = _gather_start(
        GATHER_STARTS[0], placed, (conv_tile, lax.empty((N_CHIPS,) + conv_tile.shape, conv_tile.dtype)),
        "gather_start_0")
    for i in range(1, N_BIG):
        placed[i] = _place_shard(shards[i], pos_arr[:1], "place_shard_%d" % i)
    full = {"arrs": placed, "conv": None, "sems": list(sems_a[:2])}

    def fetch(stage, layer, after):
        k = 2 * layer + stage
        sems = full["sems"][2 * k:2 * k + 2]
        if k == 0:
            full["arrs"], land = _gather_wait(0, sems, full["arrs"], (sems_a[-2:], *conv_thru), after, "gather_wait_0")
            conv_all = lax.dynamic_update_slice(land, conv_tile[None], (chip, 0, 0))
            full["conv"] = conv_all[:, :6, :conv_n].reshape(N_CHIPS, 2, 3, conv_n).transpose(1, 2, 0, 3).reshape(
                2, 3, CONV_CH)
            sems_b, full["arrs"], _ = _gather_start(GATHER_STARTS[1], full["arrs"], None, "gather_start_1",
                                                    after=full["arrs"][0])
            full["sems"] += list(sems_b)
        else:
            full["arrs"], _ = _gather_wait(k, sems, full["arrs"], None, after, "gather_wait_%d" % k)
        if k == 1:
            sems_c, full["arrs"], _ = _gather_start(GATHER_STARTS[2], full["arrs"], None, "gather_start_2")
            full["sems"] += list(sems_c)
        return (*full["arrs"], full["conv"])

    lands, started = [None] * N_BIG, {}

    def emit(layer, w, g):
        if lands[w] is None:
            lands[w] = lax.empty(_slot_shape(g, w), g.dtype)
        *started[layer, w], lands[w], token = _scatter_start(g, lands[w], layer, w, "scatter_start_%d_%d" % (layer, w))
        return token[0, 0]

    loss_tile, dx, grads, dg_final = _local_step(_to_strips(x[0], placed[0], "to_strips_x"),
                                                 _to_strips(loss_target[0], placed[0], "to_strips_target"), fetch,
                                                 w_ff_in.shape[2] * N_CHIPS,
                                                 sinks, g_mix, g_group, g_mlp, g_final, emit)

    own = []
    for w in range(N_BIG):
        g0, g1, slots = _scatter_wait((started[0, w], started[1, w]), lands[w], dx, w, "scatter_wait_%d" % w)
        own.append(_sum_slots(g0, g1, slots, w, pos_arr, "sum_slots_%d" % w))
    other = _swap_layers(own)

    def both(i):
        return jnp.stack([grads[0][i][0], grads[1][i][0]])
    dconv = jnp.stack([grads[0][0][:3], grads[1][0][:3]])
    dsinks = jnp.stack([grads[0][1][0, ::HEAD_DIM], grads[1][1][0, ::HEAD_DIM]])
    part = _pack_small(d, both(2), both(3), both(4), dg_final[0], dconv, dsinks, loss_tile[0, 0])

    def spread(shard):
        return lax.dynamic_update_slice(jnp.zeros((2, 3, CONV_CH), F32), shard, (0, 0, chip * conv_n))
    zero = jnp.zeros((), F32)
    packs = [_pack_small(d, a, b, c_, e, spread(f), g_, zero) for a, b, c_, e, f, g_ in (
        (g_mix, g_group, g_mlp, g_final, conv_w, sinks),
        (m_g_mix, m_g_group, m_g_mlp, m_g_final, m_conv_w, m_sinks),
        (v_g_mix, v_g_group, v_g_mlp, v_g_final, v_conv_w, v_sinks))]
    small = [_unpack_small(p, x.shape[2]) for p in _small_sync(part, *packs)]

    def shard_of(full):
        return lax.dynamic_slice(full, (0, 0, chip * conv_n), (2, 3, conv_n))
    small = [(s[0], s[1], s[2], s[3], shard_of(s[4]), s[5], s[6]) for s in small]
    loss = small[0][6]

    big = [_adamw(w, own[i], other[i], m, v, pos_arr, "adamw_%d" % i) for i, (w, m, v) in enumerate((
        (w_in, m_w_in, v_w_in), (w_o, m_w_o, v_w_o), (w_ff_in, m_w_ff_in, v_w_ff_in),
        (w_ff_out, m_w_ff_out, v_w_ff_out)))]

    def ordered(kind):
        b = [big[i][kind] for i in range(N_BIG)]
        s = small[kind]
        return [b[0], s[4], s[5], s[0], s[1], b[1], s[2], b[2], b[3], s[3]]

    return (loss, _from_strips(dx, "from_strips_dx")[None], *ordered(0), *ordered(1), *ordered(2), *ordered(3))
```

```python
import functools

import jax
import jax.numpy as jnp
from jax import lax
from jax.experimental import pallas as pl
from jax.experimental.pallas import tpu as pltpu

HEAD_DIM = 64
N_HEADS = 6
C_GROUP = 3
A_WIDTH = N_HEADS * HEAD_DIM
C_KV_WIDTH = 2 * HEAD_DIM
CONV_CH = 256
ZA_W = 3 * A_WIDTH
ZB_W = 3 * CONV_CH
ZC_W = A_WIDTH + 2 * C_KV_WIDTH
IN_WIDTH = ZA_W + ZB_W + ZC_W
MIX_WIDTH = A_WIDTH + CONV_CH + A_WIDTH
DILATIONS = (1, 4, 16)
A_MAX_DIST = 128
C_MAX_DIST = 127
TQ = 128
EPS = 1e-6
SCALE = HEAD_DIM ** -0.5
NEG = -1e30
HALO = 8

ADAM_LR = 0.001
ADAM_B1 = 0.9
ADAM_B2 = 0.999
ADAM_EPS = 1e-08
ADAM_WD = 0.01
ADAM_STEP = 10

BF = jnp.bfloat16
F32 = jnp.float32
MESH = pl.DeviceIdType.MESH
VMEM_LIMIT = 56 * 1024 * 1024


def _cparams(*sem):
    return pltpu.CompilerParams(dimension_semantics=sem, vmem_limit_bytes=VMEM_LIMIT)


def _nt(a, b):
    return lax.dot_general(a, b, (((1,), (1,)), ((), ())), preferred_element_type=F32)


def _tn(a, b):
    return lax.dot_general(a, b, (((0,), (0,)), ((), ())), preferred_element_type=F32)


def _nn(a, b):
    return jnp.dot(a, b, preferred_element_type=F32)


def _rows(tb, w):
    return pl.BlockSpec((tb, w), lambda i: (i, 0))


def _whole(shape):
    return pl.BlockSpec(shape, lambda *_: (0,) * len(shape))


def _layer(shape, l):
    return pl.BlockSpec((None,) + shape, lambda *_: (l,) + (0,) * len(shape))


def _rms_scale(v):
    return lax.rsqrt(jnp.mean(v * v, axis=-1, keepdims=True) + EPS)


def _norm_bwd(dxhat, xhat, r):
    return r * (dxhat - xhat * jnp.mean(dxhat * xhat, axis=-1, keepdims=True))


def _qkv_fwd(x, g, w_all, l, tb):
    s, d = x.shape

    def body(x_ref, g_ref, w_ref, h_ref, za_ref, zb_ref, zc_ref):
        xv = x_ref[...]
        h = ((xv * _rms_scale(xv)) * g_ref[...]).astype(BF)
        h_ref[...] = h
        z = jnp.concatenate([_nn(h, w_ref[k]) for k in range(N_CHIPS)], axis=1)
        za_ref[...] = z[:, :ZA_W]
        zb_ref[...] = z[:, ZA_W:ZA_W + ZB_W]
        zc_ref[...] = z[:, ZA_W + ZB_W:]

    return pl.pallas_call(
        body, grid=(s // tb,), name="qkv_fwd",
        in_specs=[_rows(tb, d), _whole((1, d)), _layer((N_CHIPS, d, IN_WIDTH // N_CHIPS), l)],
        out_specs=[_rows(tb, d), _rows(tb, ZA_W), _rows(tb, ZB_W), _rows(tb, ZC_W)],
        out_shape=[jax.ShapeDtypeStruct((s, d), BF), jax.ShapeDtypeStruct((s, ZA_W), F32),
                   jax.ShapeDtypeStruct((s, ZB_W), F32), jax.ShapeDtypeStruct((s, ZC_W), F32)],
        compiler_params=_cparams("parallel"),
    )(x, g, w_all)


N_STRIPS = 16


def _strips(a):
    s, w = a.shape
    return a.reshape(4, 4, s // N_STRIPS, w)


def _p_grid(s, dil):
    na = s // N_STRIPS
    return {16: (4, 4, na // TQ), 4: (4, na // 32), 1: (na // 8,)}[dil]


def _p_spec(dil, cw, col, prev=False):
    def blk(j):
        return jnp.maximum(j - 1, 0) if prev else j
    if dil == 16:
        return pl.BlockSpec((None, None, TQ, cw), lambda f, e, j: (f, e, blk(j), col))
    if dil == 4:
        return pl.BlockSpec((None, 4, 32, cw), lambda f, j: (f, 0, blk(j), col))
    return pl.BlockSpec((4, 4, 8, cw), lambda j: (0, 0, blk(j), col))


def _block_pos(i, dil):
    if dil == 16:
        return i
    if dil == 4:
        return 4 * (i % 32) + i // 32
    return 16 * (i % 8) + 4 * ((i // 8) % 4) + i // 32


def _band_mask(b, dil, max_dist):
    qi = _block_pos(lax.broadcasted_iota(jnp.int32, (TQ, 2 * TQ), 0), dil)
    col = lax.broadcasted_iota(jnp.int32, (TQ, 2 * TQ), 1)
    cur = col >= TQ
    dist = qi - _block_pos(col % TQ, dil) + jnp.where(cur, 0, TQ)
    return (dist >= 0) & (dist <= max_dist) & (cur | (b > 0))


def _hs(h):
    return slice(h * HEAD_DIM, (h + 1) * HEAD_DIM)


def _ld(ref, cols):
    v = ref[..., cols]
    return v.reshape(TQ, v.shape[-1])


def _st(ref, cols, val):
    ref[..., cols] = val.reshape(ref.shape[:-1] + (val.shape[-1],))


def _attn_fwd(z, dil, kw, kcol, vcol, n_rep, max_dist, name):
    s, zw = z.shape
    grid = _p_grid(s, dil)

    def body(q_ref, kp_ref, kc_ref, vp_ref, vc_ref, acc_ref, m_ref, l_ref):
        mask = _band_mask(pl.program_id(len(grid) - 1), dil, max_dist)
        for kh in range(N_HEADS // n_rep):
            k2 = jnp.concatenate([_ld(kp_ref, _hs(kh)), _ld(kc_ref, _hs(kh))], axis=0).astype(BF)
            v2 = jnp.concatenate([_ld(vp_ref, _hs(kh)), _ld(vc_ref, _hs(kh))], axis=0).astype(BF)
            for h in range(kh * n_rep, (kh + 1) * n_rep):
                q = _ld(q_ref, _hs(h)).astype(BF)
                sc = jnp.where(mask, _nt(q, k2) * SCALE, NEG)
                m = jnp.max(sc, axis=1, keepdims=True)
                p = jnp.exp(sc - m)
                _st(acc_ref, _hs(h), _nn(p.astype(BF), v2))
                _st(m_ref, _hs(h), jnp.broadcast_to(m, (TQ, HEAD_DIM)))
                _st(l_ref, _hs(h), jnp.broadcast_to(jnp.sum(p, axis=1, keepdims=True), (TQ, HEAD_DIM)))

    res = pl.pallas_call(
        body, grid=grid, name=name,
        in_specs=[_p_spec(dil, A_WIDTH, 0), _p_spec(dil, kw, kcol, True), _p_spec(dil, kw, kcol),
                  _p_spec(dil, kw, vcol, True), _p_spec(dil, kw, vcol)],
        out_specs=[_p_spec(dil, A_WIDTH, 0)] * 3,
        out_shape=[jax.ShapeDtypeStruct((4, 4, s // N_STRIPS, A_WIDTH), F32)] * 3,
        compiler_params=_cparams(*(("parallel",) * len(grid))),
    )(*[_strips(z)] * 5)
    return [a.reshape(s, A_WIDTH) for a in res]


def _attn_merge(parts_a, part_c, sink_row, tb):
    s = part_c[0].shape[0]
    n_a = len(parts_a)

    def body(*refs):
        ins, sink_ref = refs[:3 * n_a + 3], refs[3 * n_a + 3]
        ya_ref, lsea_ref, yc_ref, lsec_ref = refs[3 * n_a + 4:]
        ms = [ins[3 * p + 1][...] for p in range(n_a)]
        m = functools.reduce(jnp.maximum, ms)
        acc = jnp.zeros_like(m)
        l = jnp.zeros_like(m)
        for p in range(n_a):
            w = jnp.exp(ms[p] - m)
            acc = acc + w * ins[3 * p][...]
            l = l + w * ins[3 * p + 2][...]
        ya_ref[...] = acc / l
        lsea_ref[...] = m + jnp.log(l)
        acc_c, m_c, l_c = [r[...] for r in ins[3 * n_a:]]
        sk = sink_ref[...]
        m2 = jnp.maximum(m_c, sk)
        w = jnp.exp(m_c - m2)
        l2 = w * l_c + jnp.exp(sk - m2)
        yc_ref[...] = (w * acc_c) / l2
        lsec_ref[...] = m2 + jnp.log(l2)

    return pl.pallas_call(
        body, grid=(s // tb,), name="attn_merge",
        in_specs=[_rows(tb, A_WIDTH)] * (3 * n_a + 3) + [_whole((1, A_WIDTH))],
        out_specs=[_rows(tb, A_WIDTH)] * 4, out_shape=[jax.ShapeDtypeStruct((s, A_WIDTH), F32)] * 4,
        compiler_params=_cparams("parallel"),
    )(*[a for part in parts_a + [part_c] for a in part], sink_row)


def _shift_down(v, n, halo):
    rows = v.shape[0]
    out = pltpu.roll(v, n, 0)
    row = lax.broadcasted_iota(jnp.int32, v.shape, 0)
    for t in range(n):
        out = jnp.where(row == t, halo[HALO - n + t:HALO - n + t + 1, :], out)
    return out


def _shift_up(v, n, halo):
    rows = v.shape[0]
    out = pltpu.roll(v, rows - n, 0)
    row = lax.broadcasted_iota(jnp.int32, v.shape, 0)
    for t in range(n):
        out = jnp.where(row == rows - n + t, halo[t:t + 1, :], out)
    return out


def _strip(v, b):
    return v[b % 4, b // 4]


def _conv_strips(zb, prev, cw):
    gb = [_strip(zb, b)[:, :CONV_CH] for b in range(N_STRIPS)]
    gc = [_strip(zb, b)[:, CONV_CH:2 * CONV_CH] for b in range(N_STRIPS)]
    xb = [_strip(zb, b)[:, 2 * CONV_CH:] for b in range(N_STRIPS)]
    u = [g * v for g, v in zip(gc, xb)]
    uh = prev[:, :, CONV_CH:2 * CONV_CH] * prev[:, :, 2 * CONV_CH:]
    wrapped = {14: _shift_down(u[14], 1, uh[2]), 15: _shift_down(u[15], 1, uh[3])}
    u1 = [u[b - 1] if b >= 1 else wrapped[15] for b in range(N_STRIPS)]
    u2 = [u[b - 2] if b >= 2 else wrapped[14 + b] for b in range(N_STRIPS)]
    c = [cw[0:1, :] * u2[b] + cw[1:2, :] * u1[b] + cw[2:3, :] * u[b] for b in range(N_STRIPS)]
    return gb, gc, xb, u, u1, u2, c


def _strip_rows(ta, w):
    return pl.BlockSpec((4, 4, ta, w), lambda i: (0, 0, i, 0))


def _prev_rows(ta, w):
    return pl.BlockSpec((4, None, HALO, w), lambda i: (0, 3, jnp.maximum(i * (ta // HALO) - 1, 0), 0))


def _next_rows(ta, w, nblk):
    return pl.BlockSpec((4, None, HALO, w),
                        lambda i: (0, 0, jnp.minimum((i + 1) * (ta // HALO), nblk * (ta // HALO) - 1), 0))


def _mix_fwd(x, ya, yc, zb, cw, gg, wo_all, l, tb):
    s, d = x.shape
    ta = tb // N_STRIPS

    def body(x_ref, ya_ref, yc_ref, zb_ref, zbp_ref, cw_ref, gg_ref, wo_ref, x1_ref, yb_ref):
        i = pl.program_id(0)
        prev = jnp.where(i > 0, zbp_ref[...], 0.0)
        gb, _, _, _, _, _, c = _conv_strips(zb_ref[...], prev, cw_ref[...])
        for b in range(N_STRIPS):
            yb_ref[b % 4, b // 4] = gb[b] * c[b]
        yb = yb_ref[...].reshape(tb, CONV_CH)
        ya, yc = ya_ref[...].reshape(tb, A_WIDTH), yc_ref[...].reshape(tb, A_WIDTH)
        n = jnp.concatenate([ya * _rms_scale(ya), yb * _rms_scale(yb), yc * _rms_scale(yc)], axis=1)
        n = (n * gg_ref[...]).astype(BF)
        x1 = x_ref[...].reshape(tb, d) + _nn(n, wo_ref[...].reshape(MIX_WIDTH, d))
        x1_ref[...] = x1.reshape(4, 4, ta, d)

    res = pl.pallas_call(
        body, grid=(s // tb,), name="mix_fwd",
        in_specs=[_strip_rows(ta, d), _strip_rows(ta, A_WIDTH), _strip_rows(ta, A_WIDTH), _strip_rows(ta, ZB_W),
                  _prev_rows(ta, ZB_W), _whole((HALO, CONV_CH)), _whole((1, MIX_WIDTH)),
                  _layer((N_CHIPS, MIX_WIDTH // N_CHIPS, d), l)],
        out_specs=[_strip_rows(ta, d), _strip_rows(ta, CONV_CH)],
        out_shape=[jax.ShapeDtypeStruct((4, 4, s // N_STRIPS, d), F32),
                   jax.ShapeDtypeStruct((4, 4, s // N_STRIPS, CONV_CH), F32)],
        compiler_params=_cparams("parallel"),
    )(_strips(x), _strips(ya), _strips(yc), _strips(zb), _strips(zb), cw, gg, wo_all)
    return res[0].reshape(s, d), res[1].reshape(s, CONV_CH)


def _mlp_fwd(x1, g, w1_all, w2_all, l, tb, tf):
    s, d = x1.shape
    ff = w1_all.shape[1] * w1_all.shape[3]
    nj = ff // tf

    def body(x_ref, g_ref, w1_ref, w2_ref, x2_ref, h2_ref, ap_ref, acc):
        j = pl.program_id(1)

        @pl.when(j == 0)
        def _():
            xv = x_ref[...]
            h2_ref[...] = ((xv * _rms_scale(xv)) * g_ref[...]).astype(BF)
            acc[...] = jnp.zeros_like(acc)

        ap = _nn(h2_ref[...], w1_ref[...])
        ap_ref[...] = ap.astype(BF)
        a = jnp.square(jnp.maximum(ap, 0.0)).astype(BF)
        acc[...] += _nn(a, w2_ref[...])

        @pl.when(j == nj - 1)
        def _():
            x2_ref[...] = x_ref[...] + acc[...]

    return pl.pallas_call(
        body, grid=(s // tb, nj), name="mlp_fwd",
        in_specs=[pl.BlockSpec((tb, d), lambda i, j: (i, 0)), _whole((1, d)),
                  pl.BlockSpec((None, None, d, tf), lambda i, j: (l, j, 0, 0)),
                  pl.BlockSpec((None, None, tf, d), lambda i, j: (l, j, 0, 0))],
        out_specs=[pl.BlockSpec((tb, d), lambda i, j: (i, 0)), pl.BlockSpec((tb, d), lambda i, j: (i, 0)),
                   pl.BlockSpec((tb, tf), lambda i, j: (i, j))],
        out_shape=[jax.ShapeDtypeStruct((s, d), F32), jax.ShapeDtypeStruct((s, d), BF),
                   jax.ShapeDtypeStruct((s, ff), BF)],
        scratch_shapes=[pltpu.VMEM((tb, d), F32)],
        compiler_params=_cparams("parallel", "arbitrary"),
    )(x1, g, w1_all, w2_all)


def _loss_head(x, g, tgt, tb):
    s, d = x.shape

    def body(x_ref, g_ref, t_ref, dx_ref, loss_ref, dg_ref):
        i = pl.program_id(0)

        @pl.when(i == 0)
        def _():
            loss_ref[...] = jnp.zeros_like(loss_ref)
            dg_ref[...] = jnp.zeros_like(dg_ref)

        xv = x_ref[...]
        r = _rms_scale(xv)
        xhat = xv * r
        err = xhat * g_ref[...] - t_ref[...]
        part = jnp.sum(jnp.mean(jnp.square(err), axis=-1, keepdims=True), axis=0, keepdims=True)
        loss_ref[...] += 0.5 * part
        dy = err * (1.0 / d)
        dg_ref[...] += jnp.sum(dy * xhat, axis=0, keepdims=True)
        dx_ref[...] = _norm_bwd(dy * g_ref[...], xhat, r)

    return pl.pallas_call(
        body, grid=(s // tb,), name="loss_head",
        in_specs=[_rows(tb, d), _whole((1, d)), _rows(tb, d)],
        out_specs=[_rows(tb, d), _whole((HALO, 128)), _whole((HALO, d))],
        out_shape=[jax.ShapeDtypeStruct((s, d), F32), jax.ShapeDtypeStruct((HALO, 128), F32),
                   jax.ShapeDtypeStruct((HALO, d), F32)],
        compiler_params=_cparams("arbitrary"),
    )(x, g, tgt)


def _mlp_bwd(dx2, x1, ap, g, w1_all, w2_all, l, tb, tf):
    s, d = x1.shape
    ff = ap.shape[1]
    nj = ff // tf

    def body(dx2_ref, x1_ref, ap_ref, g_ref, w1_ref, w2_ref, dx1_ref, dap_ref, dg_ref, acc):
        i, j = pl.program_id(0), pl.program_id(1)

        @pl.when((i == 0) & (j == 0))
        def _():
            dg_ref[...] = jnp.zeros_like(dg_ref)

        @pl.when(j == 0)
        def _():
            acc[...] = jnp.zeros_like(acc)

        da = _nt(dx2_ref[...].astype(BF), w2_ref[...])
        dap = (da * (2.0 * jnp.maximum(ap_ref[...].astype(F32), 0.0))).astype(BF)
        dap_ref[...] = dap
        acc[...] += _nt(dap, w1_ref[...])

        @pl.when(j == nj - 1)
        def _():
            xv = x1_ref[...]
            r = _rms_scale(xv)
            xhat = xv * r
            dh = acc[...]
            dg_ref[...] += jnp.sum(dh * xhat, axis=0, keepdims=True)
            dx1_ref[...] = dx2_ref[...] + _norm_bwd(dh * g_ref[...], xhat, r)

    return pl.pallas_call(
        body, grid=(s // tb, nj), name="mlp_bwd",
        in_specs=[pl.BlockSpec((tb, d), lambda i, j: (i, 0)), pl.BlockSpec((tb, d), lambda i, j: (i, 0)),
                  pl.BlockSpec((tb, tf), lambda i, j: (i, j)),
                  _whole((1, d)), pl.BlockSpec((None, None, d, tf), lambda i, j: (l, j, 0, 0)),
                  pl.BlockSpec((None, None, tf, d), lambda i, j: (l, j, 0, 0))],
        out_specs=[pl.BlockSpec((tb, d), lambda i, j: (i, 0)), pl.BlockSpec((tb, tf), lambda i, j: (i, j)),
                   _whole((HALO, d))],
        out_shape=[jax.ShapeDtypeStruct((s, d), F32), jax.ShapeDtypeStruct((s, ff), BF),
                   jax.ShapeDtypeStruct((HALO, d), F32)],
        scratch_shapes=[pltpu.VMEM((tb, d), F32)],
        compiler_params=_cparams("arbitrary", "arbitrary"),
    )(dx2, x1, ap, g, w1_all, w2_all)


def _wgrad(a, b, tm, tn, ts, name, relu2=False):
    s, m = a.shape
    n = b.shape[1]
    ns = s // ts

    def body(a_ref, b_ref, o_ref, acc):
        k = pl.program_id(2)

        @pl.when(k == 0)
        def _():
            acc[...] = jnp.zeros_like(acc)

        av = a_ref[...]
        if relu2:
            av = jnp.square(jnp.maximum(av.astype(F32), 0.0)).astype(BF)
        acc[...] += _tn(av, b_ref[...].astype(BF))

        @pl.when(k == ns - 1)
        def _():
            o_ref[...] = acc[...].astype(BF)

    return pl.pallas_call(
        body, grid=(m // tm, n // tn, ns), name=name,
        in_specs=[pl.BlockSpec((ts, tm), lambda i, j, k: (k, i)), pl.BlockSpec((ts, tn), lambda i, j, k: (k, j))],
        out_specs=pl.BlockSpec((tm, tn), lambda i, j, k: (i, j)),
        out_shape=jax.ShapeDtypeStruct((m, n), BF),
        scratch_shapes=[pltpu.VMEM((tm, tn), F32)],
        compiler_params=_cparams("parallel", "parallel", "arbitrary"),
    )(a, b)


def _mix_bwd(dx1, ya, yb, yc, lse_c, sink_row, gg, wo_all, l, tb):
    s, d = dx1.shape

    def body(dx_ref, ya_ref, yb_ref, yc_ref, lse_ref, sink_ref, gg_ref, wo_ref,
             n_ref, dya_ref, dyc_ref, da_ref, dc_ref, dyb_ref, dg_ref, dsink_ref):
        i = pl.program_id(0)

        @pl.when(i == 0)
        def _():
            dg_ref[...] = jnp.zeros_like(dg_ref)
            dsink_ref[...] = jnp.zeros_like(dsink_ref)

        dn = _nt(dx_ref[...].astype(BF), wo_ref[...].reshape(MIX_WIDTH, d))
        ys = [ya_ref[...], yb_ref[...], yc_ref[...]]
        rs = [_rms_scale(v) for v in ys]
        nhat = jnp.concatenate([v * r for v, r in zip(ys, rs)], axis=1)
        gg = gg_ref[...]
        n_ref[...] = (nhat * gg).astype(BF)
        dg_ref[...] += jnp.sum(dn * nhat, axis=0, keepdims=True)
        dnh = dn * gg
        bounds = [(0, A_WIDTH), (A_WIDTH, A_WIDTH + CONV_CH), (A_WIDTH + CONV_CH, MIX_WIDTH)]
        dys = [_norm_bwd(dnh[:, lo:hi], nhat[:, lo:hi], r) for (lo, hi), r in zip(bounds, rs)]
        dyb_ref[...] = dys[1]
        for dy, y, dy_ref, dd_ref in ((dys[0], ys[0], dya_ref, da_ref), (dys[2], ys[2], dyc_ref, dc_ref)):
            dy_ref[...] = dy
            t = dy * y
            for h in range(N_HEADS):
                dd_ref[:, _hs(h)] = jnp.broadcast_to(jnp.sum(t[:, _hs(h)], axis=1, keepdims=True), (tb, HEAD_DIM))
        dsink_ref[...] -= jnp.sum(jnp.exp(sink_ref[...] - lse_ref[...]) * dc_ref[...], axis=0, keepdims=True)

    return pl.pallas_call(
        body, grid=(s // tb,), name="mix_bwd",
        in_specs=[_rows(tb, d), _rows(tb, A_WIDTH), _rows(tb, CONV_CH), _rows(tb, A_WIDTH), _rows(tb, A_WIDTH),
                  _whole((1, A_WIDTH)), _whole((1, MIX_WIDTH)), _layer((N_CHIPS, MIX_WIDTH // N_CHIPS, d), l)],
        out_specs=[_rows(tb, MIX_WIDTH), _rows(tb, A_WIDTH), _rows(tb, A_WIDTH), _rows(tb, A_WIDTH),
                   _rows(tb, A_WIDTH), _rows(tb, CONV_CH), _whole((HALO, MIX_WIDTH)), _whole((HALO, A_WIDTH))],
        out_shape=[jax.ShapeDtypeStruct((s, MIX_WIDTH), BF), jax.ShapeDtypeStruct((s, A_WIDTH), F32),
                   jax.ShapeDtypeStruct((s, A_WIDTH), F32), jax.ShapeDtypeStruct((s, A_WIDTH), F32),
                   jax.ShapeDtypeStruct((s, A_WIDTH), F32), jax.ShapeDtypeStruct((s, CONV_CH), F32),
                   jax.ShapeDtypeStruct((HALO, MIX_WIDTH), F32), jax.ShapeDtypeStruct((HALO, A_WIDTH), F32)],
        compiler_params=_cparams("arbitrary"),
    )(dx1, ya, yb, yc, lse_c, sink_row, gg, wo_all)


def _attn_bwd(z, dy, lse, dd, dil, kw, kcol, vcol, n_rep, max_dist, name):
    s, zw = z.shape
    grid = _p_grid(s, dil)
    n_kv = N_HEADS // n_rep

    def body(q_ref, kp_ref, kc_ref, vp_ref, vc_ref, dy_ref, lse_ref, dd_ref, dq_ref, dkp_ref, dkc_ref, dvp_ref, dvc_ref):
        mask = _band_mask(pl.program_id(len(grid) - 1), dil, max_dist)
        for kh in range(n_kv):
            k2 = jnp.concatenate([_ld(kp_ref, _hs(kh)), _ld(kc_ref, _hs(kh))], axis=0).astype(BF)
            v2 = jnp.concatenate([_ld(vp_ref, _hs(kh)), _ld(vc_ref, _hs(kh))], axis=0).astype(BF)
            dk2 = jnp.zeros((2 * TQ, HEAD_DIM), F32)
            dv2 = jnp.zeros((2 * TQ, HEAD_DIM), F32)
            for h in range(kh * n_rep, (kh + 1) * n_rep):
                q = _ld(q_ref, _hs(h)).astype(BF)
                lse_h = _ld(lse_ref, slice(h * HEAD_DIM, h * HEAD_DIM + 1))
                dd_h = _ld(dd_ref, slice(h * HEAD_DIM, h * HEAD_DIM + 1))
                dyh = _ld(dy_ref, _hs(h)).astype(BF)
                sc = jnp.where(mask, _nt(q, k2) * SCALE, NEG)
                p = jnp.exp(sc - lse_h)
                dp = _nt(dyh, v2)
                ds = ((p * (dp - dd_h)) * SCALE).astype(BF)
                _st(dq_ref, _hs(h), _nn(ds, k2))
                dk2 = dk2 + _tn(ds, q)
                dv2 = dv2 + _tn(p.astype(BF), dyh)
            _st(dkp_ref, _hs(kh), dk2[:TQ])
            _st(dkc_ref, _hs(kh), dk2[TQ:])
            _st(dvp_ref, _hs(kh), dv2[:TQ])
            _st(dvc_ref, _hs(kh), dv2[TQ:])

    args = [_strips(z)] * 5 + [_strips(a) for a in (dy, lse, dd)]
    in_specs = [_p_spec(dil, A_WIDTH, 0), _p_spec(dil, kw, kcol, True), _p_spec(dil, kw, kcol),
                _p_spec(dil, kw, vcol, True), _p_spec(dil, kw, vcol)] + [_p_spec(dil, A_WIDTH, 0)] * 3
    out_specs = [_p_spec(dil, A_WIDTH, 0)] + [_p_spec(dil, kw, 0)] * 4
    na = s // N_STRIPS
    out_shape = [jax.ShapeDtypeStruct((4, 4, na, A_WIDTH), F32)] + [jax.ShapeDtypeStruct((4, 4, na, kw), F32)] * 4
    res = pl.pallas_call(
        body, grid=grid, name=name, in_specs=in_specs, out_specs=out_specs, out_shape=out_shape,
        compiler_params=_cparams(*(("parallel",) * len(grid))),
    )(*args)
    return [res[0].reshape(s, A_WIDTH)] + [a.reshape(s, kw) for a in res[1:]]


DZ_TA = 16


def _dz_assemble(parts_a, parts_c, dyb, zb, cw):
    s = zb.shape[0]
    na = s // N_STRIPS
    nb = na // DZ_TA

    def ahead(w, k):
        return pl.BlockSpec((4, 4, DZ_TA, w), lambda i: (0, 0, jnp.minimum(i + k, nb - 1), 0))

    args, in_specs = [], []
    for dil, (dq, dkp, dkc, dvp, dvc) in zip(DILATIONS + (1,), parts_a + [parts_c]):
        w = dkp.shape[1]
        here = _strip_rows(DZ_TA, w)
        if dil == 1:
            args += [dq, dkp, dkp, dkc, dvp, dvp, dvc]
            in_specs += [_strip_rows(DZ_TA, A_WIDTH), here, ahead(w, 1), here, here, ahead(w, 1), here]
        else:
            k = 8 * dil // DZ_TA
            args += [dq, dkp, dkc, dvp, dvc]
            in_specs += [_strip_rows(DZ_TA, A_WIDTH), ahead(w, k), here, ahead(w, k), here]
    n_att = len(args)
    args = [_strips(a) for a in args] + [_strips(dyb), _strips(dyb), _strips(zb), _strips(zb), _strips(zb), cw]
    in_specs += [_strip_rows(DZ_TA, CONV_CH), _next_rows(DZ_TA, CONV_CH, nb), _strip_rows(DZ_TA, ZB_W),
                 _prev_rows(DZ_TA, ZB_W), _next_rows(DZ_TA, ZB_W, nb), _whole((HALO, CONV_CH))]

    def body(*refs):
        att = list(refs[:n_att])
        dyb_ref, dybn_ref, zb_ref, zbp_ref, zbn_ref, cw_ref, dz_ref, dcw_ref = refs[n_att:]
        i = pl.program_id(0)

        @pl.when(i == 0)
        def _():
            dcw_ref[...] = jnp.zeros_like(dcw_ref)

        def shifted(dil):
            if dil == 1:
                dq_r, kp0, kp1, dkc_r, vp0, vp1, dvc_r = [att.pop(0) for _ in range(7)]
                live = i + 1 < nb
                half = DZ_TA // 2
                dkp = jnp.concatenate([kp0[:, :, half:, :], jnp.where(live, kp1[:, :, :half, :], 0.0)], axis=2)
                dvp = jnp.concatenate([vp0[:, :, half:, :], jnp.where(live, vp1[:, :, :half, :], 0.0)], axis=2)
            else:
                dq_r, dkp_r, dkc_r, dvp_r, dvc_r = [att.pop(0) for _ in range(5)]
                live = i + 8 * dil // DZ_TA < nb
                dkp, dvp = jnp.where(live, dkp_r[...], 0.0), jnp.where(live, dvp_r[...], 0.0)
            return dq_r[...], dkc_r[...] + dkp, dvc_r[...] + dvp

        dq, dk, dv = shifted(DILATIONS[0])
        for dil in DILATIONS[1:]:
            dq2, dk2, dv2 = shifted(dil)
            dq, dk, dv = dq + dq2, dk + dk2, dv + dv2
        dz_ref[:, :, :, 0:A_WIDTH] = dq.astype(BF)
        dz_ref[:, :, :, A_WIDTH:2 * A_WIDTH] = dk.astype(BF)
        dz_ref[:, :, :, 2 * A_WIDTH:ZA_W] = dv.astype(BF)
        dq, dk, dv = shifted(1)
        c0 = ZA_W + ZB_W
        dz_ref[:, :, :, c0:c0 + A_WIDTH] = dq.astype(BF)
        dz_ref[:, :, :, c0 + A_WIDTH:c0 + A_WIDTH + C_KV_WIDTH] = dk.astype(BF)
        dz_ref[:, :, :, c0 + A_WIDTH + C_KV_WIDTH:IN_WIDTH] = dv.astype(BF)

        cw = cw_ref[...]
        prev = jnp.where(i > 0, zbp_ref[...], 0.0)
        gb, gc, xb, u, u1, u2, c = _conv_strips(zb_ref[...], prev, cw)
        dyb = dyb_ref[...]
        dc = [_strip(dyb, b) * gb[b] for b in range(N_STRIPS)]
        dcn = jnp.where(i + 1 < nb, dybn_ref[...] * zbn_ref[:, :, :CONV_CH], 0.0)
        wrapped = [_shift_up(dc[0], 1, dcn[0]), _shift_up(dc[1], 1, dcn[1])]
        upd = [jnp.zeros((1, CONV_CH), F32)] * 3
        for b in range(N_STRIPS):
            dc1 = dc[b + 1] if b + 1 < N_STRIPS else wrapped[0]
            dc2 = dc[b + 2] if b + 2 < N_STRIPS else wrapped[b + 2 - N_STRIPS]
            du = cw[2:3, :] * dc[b] + cw[1:2, :] * dc1 + cw[0:1, :] * dc2
            f, e = b % 4, b // 4
            dz_ref[f, e, :, ZA_W:ZA_W + CONV_CH] = (_strip(dyb, b) * c[b]).astype(BF)
            dz_ref[f, e, :, ZA_W + CONV_CH:ZA_W + 2 * CONV_CH] = (du * xb[b]).astype(BF)
            dz_ref[f, e, :, ZA_W + 2 * CONV_CH:c0] = (du * gc[b]).astype(BF)
            for t, uu in enumerate((u2[b], u1[b], u[b])):
                upd[t] = upd[t] + jnp.sum(dc[b] * uu, axis=0, keepdims=True)
        row = lax.broadcasted_iota(jnp.int32, (HALO, CONV_CH), 0)
        tile = jnp.zeros((HALO, CONV_CH), F32)
        for t in range(3):
            tile = jnp.where(row == t, upd[t], tile)
        dcw_ref[...] += tile

    dz, dcw = pl.pallas_call(
        body, grid=(nb,), name="dz_assemble", in_specs=in_specs,
        out_specs=[_strip_rows(DZ_TA, IN_WIDTH), _whole((HALO, CONV_CH))],
        out_shape=[jax.ShapeDtypeStruct((4, 4, na, IN_WIDTH), BF), jax.ShapeDtypeStruct((HALO, CONV_CH), F32)],
        compiler_params=_cparams("arbitrary"),
    )(*args)
    return dz.reshape(s, IN_WIDTH), dcw


def _qkv_bwd(dz, dx1, x, g, w_all, l, tb):
    s, d = x.shape

    def body(dz_ref, dx1_ref, x_ref, g_ref, w_ref, dx_ref, dg_ref):
        i = pl.program_id(0)

        @pl.when(i == 0)
        def _():
            dg_ref[...] = jnp.zeros_like(dg_ref)

        n = IN_WIDTH // N_CHIPS
        dh = _nt(dz_ref[:, 0:n], w_ref[0])
        for k in range(1, N_CHIPS):
            dh = dh + _nt(dz_ref[:, k * n:(k + 1) * n], w_ref[k])
        xv = x_ref[...]
        r = _rms_scale(xv)
        xhat = xv * r
        dg_ref[...] += jnp.sum(dh * xhat, axis=0, keepdims=True)
        dx_ref[...] = dx1_ref[...] + _norm_bwd(dh * g_ref[...], xhat, r)

    return pl.pallas_call(
        body, grid=(s // tb,), name="qkv_bwd",
        in_specs=[_rows(tb, IN_WIDTH), _rows(tb, d), _rows(tb, d), _whole((1, d)),
                  _layer((N_CHIPS, d, IN_WIDTH // N_CHIPS), l)],
        out_specs=[_rows(tb, d), _whole((HALO, d))],
        out_shape=[jax.ShapeDtypeStruct((s, d), F32), jax.ShapeDtypeStruct((HALO, d), F32)],
        compiler_params=_cparams("arbitrary"),
    )(dz, dx1, x, g, w_all)


def _tile_rows(rows):
    return jnp.pad(rows, ((0, HALO - rows.shape[0]), (0, 0)))


def _to_strips(a, after, name):
    s, d = a.shape
    na = s // N_STRIPS
    ta = min(32, na)

    def body(a_ref, after_ref, o_ref):
        for b in range(N_STRIPS):
            o_ref[b % 4, b // 4] = a_ref[:, b, :]

    return pl.pallas_call(
        body, grid=(na // ta,), name=name,
        in_specs=[pl.BlockSpec((ta, N_STRIPS, d), lambda i: (i, 0, 0)), ANY], out_specs=_strip_rows(ta, d),
        out_shape=jax.ShapeDtypeStruct((4, 4, na, d), a.dtype), compiler_params=_cparams("parallel"),
    )(a.reshape(na, N_STRIPS, d), after).reshape(s, d)


def _from_strips(a, name):
    s, d = a.shape
    na = s // N_STRIPS
    ta = min(32, na)

    def body(a_ref, o_ref):
        for b in range(N_STRIPS):
            o_ref[:, b, :] = a_ref[b % 4, b // 4]

    return pl.pallas_call(
        body, grid=(na // ta,), name=name, in_specs=[_strip_rows(ta, d)],
        out_specs=pl.BlockSpec((ta, N_STRIPS, d), lambda i: (i, 0, 0)),
        out_shape=jax.ShapeDtypeStruct((na, N_STRIPS, d), a.dtype), compiler_params=_cparams("parallel"),
    )(_strips(a)).reshape(s, d)


def _local_step(x, tgt, fetch, ff, sinks, g_mix, g_group, g_mlp, g_final, emit):
    s, d = x.shape
    depth = g_mix.shape[0]
    tb = min(512, s)
    tf = ff // N_CHIPS
    ts = min(1024, s)
    saved = []
    for l in range(depth):
        w_in, _, _, _, conv_w = fetch(0, l, x)
        cw = _tile_rows(conv_w[l])
        sk = jnp.repeat(sinks[l].reshape(N_HEADS), HEAD_DIM)[None]
        h, za, zb, zc = _qkv_fwd(x, g_mix[l][None], w_in, l, tb)
        parts_a = [_attn_fwd(za, dil, A_WIDTH, 1, 2, 1, A_MAX_DIST, "attn_a_fwd_%d" % dil) for dil in DILATIONS]
        part_c = _attn_fwd(zc, 1, C_KV_WIDTH, 3, 4, C_GROUP, C_MAX_DIST, "attn_c_fwd")
        ya, lse_a, yc, lse_c = _attn_merge(parts_a, part_c, sk, tb)
        w_in, w_o, w1, w2, _ = fetch(1, l, yc)
        x1, yb = _mix_fwd(x, ya, yc, zb, cw, g_group[l][None], w_o, l, tb)
        x2, h2, ap = _mlp_fwd(x1, g_mlp[l][None], w1, w2, l, ts, tf)
        saved.append((x, h, za, zb, zc, ya, lse_a, yc, lse_c, yb, x1, h2, ap, cw, sk))
        x = x2
    dx, loss_tile, dg_final = _loss_head(x, g_final[None], tgt, tb)
    grads = [None] * depth
    tok = jnp.zeros((), F32)
    for l in reversed(range(depth)):
        x0, h, za, zb, zc, ya, lse_a, yc, lse_c, yb, x1, h2, ap, cw, sk = saved[l]
        dx1, dap, dg_mlp = _mlp_bwd(dx, x1, ap, g_mlp[l][None] + tok, w1, w2, l, ts, tf)
        tok = emit(l, 3, _wgrad(ap, dx, min(1024, ff), d, ts, "wgrad_ff_out", relu2=True))
        tok = tok + emit(l, 2, _wgrad(h2, dap, d, min(1024, ff), ts, "wgrad_ff_in"))
        n, dya, dyc, dd_a, dd_c, dyb, dg_group, dsink = _mix_bwd(dx1, ya, yb, yc, lse_c, sk, g_group[l][None] + tok,
                                                                 w_o, l, tb)
        tok = emit(l, 1, _wgrad(n, dx1, MIX_WIDTH, d, ts, "wgrad_o"))
        cw = cw + tok
        parts_a = [_attn_bwd(za, dya, lse_a, dd_a, dil, A_WIDTH, 1, 2, 1, A_MAX_DIST, "attn_a_bwd_%d" % dil)
                   for dil in DILATIONS]
        parts_c = _attn_bwd(zc, dyc, lse_c, dd_c, 1, C_KV_WIDTH, 3, 4, C_GROUP, C_MAX_DIST, "attn_c_bwd")
        dz, dcw = _dz_assemble(parts_a, parts_c, dyb, zb, cw)
        dx, dg_mix = _qkv_bwd(dz, dx1, x0, g_mix[l][None], w_in, l, tb)
        tok = emit(l, 0, _wgrad(h, dz, d, IN_WIDTH // 4, ts, "wgrad_in"))
        grads[l] = (dcw, dsink, dg_mix, dg_group, dg_mlp)
    return loss_tile, dx, grads, dg_final


ANY = pl.BlockSpec(memory_space=pl.ANY)
SHARD_AXES = (2, 1, 2, 1)
N_BIG = len(SHARD_AXES)
N_CHIPS = 4
N_DEV = 8


def _mesh_pos():
    return lax.axis_index("x"), lax.axis_index("y"), lax.axis_index("c")


def _flip(v, bit):
    return 1 - v if bit else v


def _place_shard(shard, chip_arr, name):
    _, rows, cols = shard.shape
    tr = min(256, rows)

    def body(chip_ref, x_ref, o_ref):
        o_ref[...] = x_ref[...].astype(BF)

    return pl.pallas_call(
        body, name=name,
        grid_spec=pltpu.PrefetchScalarGridSpec(
            num_scalar_prefetch=1, grid=(2, rows // tr),
            in_specs=[pl.BlockSpec((None, tr, cols), lambda l, i, chip: (l, i, 0))],
            out_specs=pl.BlockSpec((None, None, tr, cols), lambda l, i, chip: (l, chip[0], i, 0))),
        out_shape=jax.ShapeDtypeStruct((2, N_CHIPS, rows, cols), BF),
        compiler_params=_cparams("parallel", "parallel"),
    )(chip_arr, shard)


HBM = pl.BlockSpec(memory_space=pltpu.HBM)
SEM = pl.BlockSpec(memory_space=pltpu.SEMAPHORE)
EFFECT = pltpu.SideEffectType.DATAFLOW_SIDE_EFFECTING

GATHER_GROUPS = (((0, 0),), ((1, 0), (2, 0), (3, 0)), ((0, 1),), ((1, 1), (2, 1), (3, 1)))
GATHER_STARTS = ((0,), (1,), (2, 3))


def _gather_copies(arrs, group, send_sems, recv_sems):
    x, y, c = _mesh_pos()
    me = 2 * x + y
    out = []
    for i, (w, layer) in enumerate(group):
        mine = arrs[w].at[layer, me]
        for j, (qx, qy) in enumerate([(1 - x, y), (x, 1 - y), (1 - x, 1 - y)]):
            landed = arrs[w].at[layer, 2 * qx + qy]
            out.append(tuple(pltpu.make_async_remote_copy(
                src_ref=piece, dst_ref=piece, send_sem=send_sems.at[i * 3 + j], recv_sem=recv_sems.at[i * 3 + j],
                device_id=(qx, qy, c), device_id_type=MESH) for piece in (mine, landed)))
    return out


def _conv_copies(conv_src, conv_dst, send_sems, recv_sems):
    x, y, c = _mesh_pos()
    out = []
    for j, (qx, qy) in enumerate([(1 - x, y), (x, 1 - y), (1 - x, 1 - y)]):
        out.append(tuple(pltpu.make_async_remote_copy(
            src_ref=conv_src, dst_ref=conv_dst.at[q], send_sem=send_sems.at[j], recv_sem=recv_sems.at[j],
            device_id=(qx, qy, c), device_id_type=MESH) for q in (2 * x + y, 2 * qx + qy)))
    return out


def _gather_start(groups, arrs, conv, name, through=None):
    n_sems = 2 * (len(groups) + (conv is not None))
    mats = sorted({w for g in groups for w, _ in GATHER_GROUPS[g]})

    def body(*refs):
        arrs_ref = [None] * N_BIG
        for w, ref in zip(mats, refs):
            arrs_ref[w] = ref
        sems = refs[n_in:n_in + n_sems]
        if conv is not None:
            for cp, _ in _conv_copies(refs[len(mats)], refs[len(mats) + 1], sems[-2], sems[-1]):
                cp.start()
        for k, g in enumerate(groups):
            for cp, _ in _gather_copies(arrs_ref, GATHER_GROUPS[g], sems[2 * k], sems[2 * k + 1]):
                cp.start()

    sem_shapes = []
    for n in [len(GATHER_GROUPS[g]) for g in groups] + ([1] if conv is not None else []):
        sem_shapes += [pltpu.SemaphoreType.DMA((3 * n,))] * 2
    operands = [arrs[w] for w in mats] + ([] if conv is None else list(conv)) + ([] if through is None else [through])
    n_in = len(operands)
    res = pl.pallas_call(
        body, name=name,
        out_shape=tuple(sem_shapes) + tuple(pltpu.HBM(a.shape, a.dtype) for a in operands),
        in_specs=(HBM,) * n_in, out_specs=(SEM,) * n_sems + (HBM,) * n_in,
        input_output_aliases={i: n_sems + i for i in range(n_in)},
        compiler_params=pltpu.CompilerParams(has_side_effects=EFFECT),
    )(*[pltpu.with_memory_space_constraint(a, pltpu.HBM) for a in operands])
    arrs = list(arrs)
    for w, a in zip(mats, res[n_sems:]):
        arrs[w] = a
    return res[:n_sems], arrs, list(res[n_sems + len(mats):])


def _gather_wait(k, sems, arrs, conv, after, name):
    group = GATHER_GROUPS[k]
    mats = sorted({w for w, _ in group})
    n_conv = 0 if conv is None else 2

    def body(*refs):
        local = refs[:len(mats)]
        arrs_ref = [None] * N_BIG
        for w, ref in zip(mats, local):
            arrs_ref[w] = ref
        pos = len(mats) + n_conv
        copies = _gather_copies(arrs_ref, group, refs[pos], refs[pos + 1])
        if conv is not None:
            copies += _conv_copies(refs[len(mats)], refs[len(mats) + 1], refs[pos + 2], refs[pos + 3])
        for send, recv in copies:
            recv.wait_recv()
            send.wait_send()

    operands = [arrs[w] for w in mats] + ([] if conv is None else [conv[1], conv[2]])
    sem_ops = list(sems) + ([] if conv is None else list(conv[0]))
    n_op = len(operands)
    res = pl.pallas_call(
        body, name=name, out_shape=tuple(pltpu.HBM(a.shape, a.dtype) for a in operands),
        in_specs=(HBM,) * n_op + (SEM,) * len(sem_ops) + (ANY,), out_specs=(HBM,) * n_op,
        input_output_aliases={i: i for i in range(n_op)},
        compiler_params=pltpu.CompilerParams(has_side_effects=EFFECT),
    )(*operands, *sem_ops, after)
    arrs = list(arrs)
    for w, a in zip(mats, res):
        arrs[w] = a
    return arrs, (res[-1] if conv is not None else None)


def _grad_shard(ref, w, chip, n):
    start = pl.multiple_of(chip * n, 128)
    if SHARD_AXES[w] == 2:
        return ref.at[:, pl.ds(start, n)]
    return ref.at[pl.ds(start, n), :]


def _slot_shape(g, w):
    shape = list(g.shape)
    shape[SHARD_AXES[w] - 1] //= N_CHIPS
    return (N_DEV - 1,) + tuple(shape)


def _scatter_copies(g_ref, land_ref, send_sems, recv_sems, layer, w):
    x, y, c = _mesh_pos()
    n = g_ref.shape[SHARD_AXES[w] - 1] // N_CHIPS
    out = []
    for r in range(1, N_DEV):
        tx, ty, tc = _flip(x, r & 4), _flip(y, r & 2), _flip(c, r & 1)
        cp = pltpu.make_async_remote_copy(
            src_ref=_grad_shard(g_ref, w, 2 * tx + ty, n), dst_ref=land_ref.at[r - 1], send_sem=send_sems.at[r - 1],
            recv_sem=recv_sems.at[r - 1], device_id=(tx, ty, tc), device_id_type=MESH)
        out.append((cp, (c != layer) if r & 1 else (c == layer)))
    return out


def _scatter_start(g, land, layer, w, name):
    def body(g_ref, land_ref, send_sems, recv_sems, g_thru, land_thru, token):
        for cp, mine in _scatter_copies(g_ref, land_ref, send_sems, recv_sems, layer, w):
            @pl.when(mine)
            def _():
                cp.start()
        token[...] = jnp.zeros_like(token)

    return pl.pallas_call(
        body, name=name,
        out_shape=(pltpu.SemaphoreType.DMA((N_DEV - 1,)), pltpu.SemaphoreType.DMA((N_DEV - 1,)),
                   pltpu.HBM(g.shape, g.dtype), pltpu.HBM(land.shape, land.dtype),
                   jax.ShapeDtypeStruct((HALO, 128), F32)),
        in_specs=(HBM, HBM), out_specs=(SEM, SEM, HBM, HBM, pl.BlockSpec(memory_space=pltpu.VMEM)),
        input_output_aliases={0: 2, 1: 3}, compiler_params=pltpu.CompilerParams(has_side_effects=EFFECT),
    )(pltpu.with_memory_space_constraint(g, pltpu.HBM), pltpu.with_memory_space_constraint(land, pltpu.HBM))


def _scatter_wait(started, land, after, w, name):
    def body(g0_ref, g1_ref, land_ref, ss0, rs0, ss1, rs1, after_ref, g0_out, g1_out, land_out):
        c = lax.axis_index("c")
        for layer, g_ref, ss, rs in ((0, g0_ref, ss0, rs0), (1, g1_ref, ss1, rs1)):
            for cp, mine in _scatter_copies(g_ref, land_ref, ss, rs, layer, w):
                @pl.when(mine)
                def _():
                    cp.wait_send()

                @pl.when(c == layer)
                def _():
                    cp.wait_recv()

    (ss0, rs0, g0), (ss1, rs1, g1) = started
    return pl.pallas_call(
        body, name=name,
        out_shape=(pltpu.HBM(g0.shape, g0.dtype), pltpu.HBM(g1.shape, g1.dtype), pltpu.HBM(land.shape, land.dtype)),
        in_specs=(HBM, HBM, HBM, SEM, SEM, SEM, SEM, ANY), out_specs=(HBM, HBM, HBM),
        input_output_aliases={0: 0, 1: 1, 2: 2}, compiler_params=pltpu.CompilerParams(has_side_effects=EFFECT),
    )(g0, g1, land, ss0, rs0, ss1, rs1, after)


def _sum_slots(g0, g1, slots, w, pos_arr, name):
    _, rows, cols = slots.shape
    tr = min(256, rows)
    nr = rows // tr
    if SHARD_AXES[w] == 2:
        own = pl.BlockSpec((tr, cols), lambda i, pos: (i, pos[0]))
    else:
        own = pl.BlockSpec((tr, cols), lambda i, pos: (pos[0] * nr + i, 0))

    def body(pos_ref, own0_ref, own1_ref, s_ref, o_ref):
        acc = jnp.where(pos_ref[1] == 0, own0_ref[...], own1_ref[...]).astype(F32)
        for r in range(N_DEV - 1):
            acc = acc + s_ref[r].astype(F32)
        o_ref[...] = acc

    return pl.pallas_call(
        body, name=name,
        grid_spec=pltpu.PrefetchScalarGridSpec(
            num_scalar_prefetch=1, grid=(nr,),
            in_specs=[own, own, pl.BlockSpec((N_DEV - 1, tr, cols), lambda i, pos: (0, i, 0))],
            out_specs=pl.BlockSpec((tr, cols), lambda i, pos: (i, 0))),
        out_shape=jax.ShapeDtypeStruct((rows, cols), F32), compiler_params=_cparams("parallel"),
    )(pos_arr, g0, g1, slots)


def _swap_layers(halves):
    def body(*refs):
        srcs, dsts = refs[:N_BIG], refs[N_BIG:2 * N_BIG]
        send_sems, recv_sems = refs[2 * N_BIG:]
        x, y, c = _mesh_pos()
        sends = [pltpu.make_async_remote_copy(src_ref=srcs[w], dst_ref=dsts[w], send_sem=send_sems.at[w],
                                              recv_sem=recv_sems.at[w], device_id=(x, y, 1 - c), device_id_type=MESH)
                 for w in range(N_BIG)]
        for cp in sends:
            cp.start()
        for cp in sends:
            cp.wait_recv()
        for cp in sends:
            cp.wait_send()

    return pl.pallas_call(
        body, name="swap_layers", in_specs=[ANY] * N_BIG, out_specs=[ANY] * N_BIG,
        out_shape=[jax.ShapeDtypeStruct(h.shape, h.dtype) for h in halves],
        scratch_shapes=[pltpu.SemaphoreType.DMA((N_BIG,)), pltpu.SemaphoreType.DMA((N_BIG,))],
    )(*halves)


def _adamw_math(w, g, m, v):
    m = ADAM_B1 * m + (1.0 - ADAM_B1) * g
    v = ADAM_B2 * v + (1.0 - ADAM_B2) * jnp.square(g)
    m_hat = m / (1.0 - ADAM_B1 ** ADAM_STEP)
    v_hat = v / (1.0 - ADAM_B2 ** ADAM_STEP)
    delta = -ADAM_LR * (m_hat / (jnp.sqrt(v_hat) + ADAM_EPS) + ADAM_WD * w)
    return delta, m, v


def _adamw(w, g_own, g_other, m, v, pos_arr, name):
    shape = w.shape
    _, rows, cols = shape
    tr = min(256, rows)

    def body(pos_ref, w_ref, own_ref, other_ref, m_ref, v_ref, g_ref, d_ref, m2_ref, v2_ref):
        g = jnp.where(pl.program_id(0) == pos_ref[1], own_ref[...], other_ref[...])
        g_ref[...] = g
        d_ref[...], m2_ref[...], v2_ref[...] = _adamw_math(w_ref[...], g, m_ref[...], v_ref[...])

    full = pl.BlockSpec((None, tr, cols), lambda l, i, pos: (l, i, 0))
    half = pl.BlockSpec((tr, cols), lambda l, i, pos: (i, 0))
    return pl.pallas_call(
        body, name=name,
        grid_spec=pltpu.PrefetchScalarGridSpec(
            num_scalar_prefetch=1, grid=(2, rows // tr),
            in_specs=[full, half, half, full, full], out_specs=[full] * 4),
        out_shape=[jax.ShapeDtypeStruct(shape, F32)] * 4, compiler_params=_cparams("parallel", "parallel"),
    )(pos_arr, w, g_own, g_other, m, v)


def _small_sync(part, w, m, v):
    rows, cols = part.shape

    def body(p_ref, w_ref, m_ref, v_ref, g_ref, d_ref, m2_ref, v2_ref, slots, send_sems, recv_sems):
        x, y, c = _mesh_pos()
        me = 4 * x + 2 * y + c
        slots[me] = p_ref[...]
        sends = []
        for r in range(1, N_DEV):
            to = (_flip(x, r & 4), _flip(y, r & 2), _flip(c, r & 1))
            sends.append(pltpu.make_async_remote_copy(
                src_ref=p_ref, dst_ref=slots.at[me], send_sem=send_sems.at[r - 1], recv_sem=recv_sems.at[r - 1],
                device_id=to, device_id_type=MESH))
        for cp in sends:
            cp.start()
        for cp in sends:
            cp.wait_recv()
        for cp in sends:
            cp.wait_send()
        g = slots[0]
        for i in range(1, N_DEV):
            g = g + slots[i]
        g_ref[...] = g
        d_ref[...], m2_ref[...], v2_ref[...] = _adamw_math(w_ref[...], g, m_ref[...], v_ref[...])

    vm = pl.BlockSpec(memory_space=pltpu.VMEM)
    return pl.pallas_call(
        body, name="small_sync", in_specs=[vm] * 4, out_specs=[vm] * 4,
        out_shape=[jax.ShapeDtypeStruct((rows, cols), F32)] * 4,
        scratch_shapes=[pltpu.VMEM((N_DEV, rows, cols), F32), pltpu.SemaphoreType.DMA((N_DEV - 1,)),
                        pltpu.SemaphoreType.DMA((N_DEV - 1,))],
    )(part, w, m, v)


def _pack_small(d, g_mix, g_group, g_mlp, g_final, conv_full, sinks, scalar):
    def part(rows):
        return jnp.pad(rows, ((0, HALO - rows.shape[0]), (0, d - rows.shape[1])))
    return jnp.concatenate([part(g_mix), part(g_group), part(g_mlp), part(g_final[None]),
                            part(conv_full.reshape(6, CONV_CH)), part(sinks.reshape(2, N_HEADS)),
                            part(scalar.reshape(1, 1))], axis=0)


def _unpack_small(p, dm):
    return (p[0:2, :dm], p[8:10, :MIX_WIDTH], p[16:18, :dm], p[24, :dm], p[32:38, :CONV_CH].reshape(2, 3, CONV_CH),
            p[40:42, :N_HEADS].reshape(2, 2, C_GROUP), p[48, 0])


def kernel(x, w_in, conv_w, sinks, g_mix, g_group, w_o, g_mlp, w_ff_in, w_ff_out, g_final, loss_target, m_w_in, m_conv_w, m_sinks, m_g_mix, m_g_group, m_w_o, m_g_mlp, m_w_ff_in, m_w_ff_out, m_g_final, v_w_in, v_conv_w, v_sinks, v_g_mix, v_g_group, v_w_o, v_g_mlp, v_w_ff_in, v_w_ff_out, v_g_final):
    d = max(x.shape[2], MIX_WIDTH)
    chip = 2 * lax.axis_index("x") + lax.axis_index("y")
    conv_n = conv_w.shape[2]

    pos_arr = jnp.stack([chip, lax.axis_index("c")]).astype(jnp.int32)
    shards = (w_in, w_o, w_ff_in, w_ff_out)
    conv_tile = jnp.pad(conv_w.reshape(6, conv_n), ((0, HALO - 6), (0, 128 - conv_n)))
    placed = [_place_shard(w_in, pos_arr[:1], "place_shard_0"), None, None, None]
    sems_a, placed, conv_thru = _gather_start(
        GATHER_STARTS[0], placed, (conv_tile, lax.empty((N_CHIPS,) + conv_tile.shape, conv_tile.dtype)),
        "gather_start_0")
    for i in range(1, N_BIG):
        placed[i] = _place_shard(shards[i], pos_arr[:1], "place_shard_%d" % i)
    full = {"arrs": placed, "conv": None, "sems": list(sems_a[:2])}

    def fetch(stage, layer, after):
        k = 2 * layer + stage
        sems = full["sems"][2 * k:2 * k + 2]
        if k == 0:
            full["arrs"], land = _gather_wait(0, sems, full["arrs"], (sems_a[-2:], *conv_thru), after, "gather_wait_0")
            conv_all = lax.dynamic_update_slice(land, conv_tile[None], (chip, 0, 0))
            full["conv"] = conv_all[:, :6, :conv_n].reshape(N_CHIPS, 2, 3, conv_n).transpose(1, 2, 0, 3).reshape(
                2, 3, CONV_CH)
            sems_b, full["arrs"], rest = _gather_start(GATHER_STARTS[1], full["arrs"], None, "gather_start_1",
                                                       through=full["arrs"][0])
            full["arrs"][0] = rest[-1]
            full["sems"] += list(sems_b)
        else:
            full["arrs"], _ = _gather_wait(k, sems, full["arrs"], None, after, "gather_wait_%d" % k)
        if k == 1:
            sems_c, full["arrs"], _ = _gather_start(GATHER_STARTS[2], full["arrs"], None, "gather_start_2")
            full["sems"] += list(sems_c)
        return (*full["arrs"], full["conv"])

    lands, started = [None] * N_BIG, {}

    def emit(layer, w, g):
        if lands[w] is None:
            lands[w] = lax.empty(_slot_shape(g, w), g.dtype)
        *started[layer, w], lands[w], token = _scatter_start(g, lands[w], layer, w, "scatter_start_%d_%d" % (layer, w))
        return token[0, 0]

    loss_tile, dx, grads, dg_final = _local_step(_to_strips(x[0], placed[0], "to_strips_x"),
                                                 _to_strips(loss_target[0], placed[0], "to_strips_target"), fetch,
                                                 w_ff_in.shape[2] * N_CHIPS,
                                                 sinks, g_mix, g_group, g_mlp, g_final, emit)

    own = []
    for w in range(N_BIG):
        g0, g1, slots = _scatter_wait((started[0, w], started[1, w]), lands[w], dx, w, "scatter_wait_%d" % w)
        own.append(_sum_slots(g0, g1, slots, w, pos_arr, "sum_slots_%d" % w))
    other = _swap_layers(own)

    def both(i):
        return jnp.stack([grads[0][i][0], grads[1][i][0]])
    dconv = jnp.stack([grads[0][0][:3], grads[1][0][:3]])
    dsinks = jnp.stack([grads[0][1][0, ::HEAD_DIM], grads[1][1][0, ::HEAD_DIM]])
    part = _pack_small(d, both(2), both(3), both(4), dg_final[0], dconv, dsinks, loss_tile[0, 0])

    def spread(shard):
        return lax.dynamic_update_slice(jnp.zeros((2, 3, CONV_CH), F32), shard, (0, 0, chip * conv_n))
    zero = jnp.zeros((), F32)
    packs = [_pack_small(d, a, b, c_, e, spread(f), g_, zero) for a, b, c_, e, f, g_ in (
        (g_mix, g_group, g_mlp, g_final, conv_w, sinks),
        (m_g_mix, m_g_group, m_g_mlp, m_g_final, m_conv_w, m_sinks),
        (v_g_mix, v_g_group, v_g_mlp, v_g_final, v_conv_w, v_sinks))]
    small = [_unpack_small(p, x.shape[2]) for p in _small_sync(part, *packs)]

    def shard_of(full):
        return lax.dynamic_slice(full, (0, 0, chip * conv_n), (2, 3, conv_n))
    small = [(s[0], s[1], s[2], s[3], shard_of(s[4]), s[5], s[6]) for s in small]
    loss = small[0][6]

    big = [_adamw(w, own[i], other[i], m, v, pos_arr, "adamw_%d" % i) for i, (w, m, v) in enumerate((
        (w_in, m_w_in, v_w_in), (w_o, m_w_o, v_w_o), (w_ff_in, m_w_ff_in, v_w_ff_in),
        (w_ff_out, m_w_ff_out, v_w_ff_out)))]

    def ordered(kind):
        b = [big[i][kind] for i in range(N_BIG)]
        s = small[kind]
        return [b[0], s[4], s[5], s[0], s[1], b[1], s[2], b[2], b[3], s[3]]

    return (loss, _from_strips(dx, "from_strips_dx")[None], *ordered(0), *ordered(1), *ordered(2), *ordered(3))
```

```python
import functools

import jax
import jax.numpy as jnp
from jax import lax
from jax.experimental import pallas as pl
from jax.experimental.pallas import tpu as pltpu

HEAD_DIM = 64
N_HEADS = 6
C_GROUP = 3
A_WIDTH = N_HEADS * HEAD_DIM
C_KV_WIDTH = 2 * HEAD_DIM
CONV_CH = 256
ZA_W = 3 * A_WIDTH
ZB_W = 3 * CONV_CH
ZC_W = A_WIDTH + 2 * C_KV_WIDTH
IN_WIDTH = ZA_W + ZB_W + ZC_W
MIX_WIDTH = A_WIDTH + CONV_CH + A_WIDTH
DILATIONS = (1, 4, 16)
A_MAX_DIST = 128
C_MAX_DIST = 127
TQ = 128
EPS = 1e-6
SCALE = HEAD_DIM ** -0.5
NEG = -1e30
HALO = 8

ADAM_LR = 0.001
ADAM_B1 = 0.9
ADAM_B2 = 0.999
ADAM_EPS = 1e-08
ADAM_WD = 0.01
ADAM_STEP = 10

BF = jnp.bfloat16
F32 = jnp.float32
MESH = pl.DeviceIdType.MESH
VMEM_LIMIT = 56 * 1024 * 1024


def _cparams(*sem):
    return pltpu.CompilerParams(dimension_semantics=sem, vmem_limit_bytes=VMEM_LIMIT)


def _nt(a, b):
    return lax.dot_general(a, b, (((1,), (1,)), ((), ())), preferred_element_type=F32)


def _tn(a, b):
    return lax.dot_general(a, b, (((0,), (0,)), ((), ())), preferred_element_type=F32)


def _nn(a, b):
    return jnp.dot(a, b, preferred_element_type=F32)


def _rows(tb, w):
    return pl.BlockSpec((tb, w), lambda i: (i, 0))


def _whole(shape):
    return pl.BlockSpec(shape, lambda *_: (0,) * len(shape))


def _layer(shape, l):
    return pl.BlockSpec((None,) + shape, lambda *_: (l,) + (0,) * len(shape))


def _rms_scale(v):
    return lax.rsqrt(jnp.mean(v * v, axis=-1, keepdims=True) + EPS)


def _norm_bwd(dxhat, xhat, r):
    return r * (dxhat - xhat * jnp.mean(dxhat * xhat, axis=-1, keepdims=True))


def _qkv_fwd(x, g, w_all, l, tb):
    s, d = x.shape

    def body(x_ref, g_ref, w_ref, h_ref, za_ref, zb_ref, zc_ref):
        xv = x_ref[...]
        h = ((xv * _rms_scale(xv)) * g_ref[...]).astype(BF)
        h_ref[...] = h
        z = jnp.concatenate([_nn(h, w_ref[k]) for k in range(N_CHIPS)], axis=1)
        za_ref[...] = z[:, :ZA_W]
        zb_ref[...] = z[:, ZA_W:ZA_W + ZB_W]
        zc_ref[...] = z[:, ZA_W + ZB_W:]

    return pl.pallas_call(
        body, grid=(s // tb,), name="qkv_fwd",
        in_specs=[_rows(tb, d), _whole((1, d)), _layer((N_CHIPS, d, IN_WIDTH // N_CHIPS), l)],
        out_specs=[_rows(tb, d), _rows(tb, ZA_W), _rows(tb, ZB_W), _rows(tb, ZC_W)],
        out_shape=[jax.ShapeDtypeStruct((s, d), BF), jax.ShapeDtypeStruct((s, ZA_W), F32),
                   jax.ShapeDtypeStruct((s, ZB_W), F32), jax.ShapeDtypeStruct((s, ZC_W), F32)],
        compiler_params=_cparams("parallel"),
    )(x, g, w_all)


N_STRIPS = 16


def _strips(a):
    s, w = a.shape
    return a.reshape(4, 4, s // N_STRIPS, w)


def _p_grid(s, dil):
    na = s // N_STRIPS
    return {16: (4, 4, na // TQ), 4: (4, na // 32), 1: (na // 8,)}[dil]


def _p_spec(dil, cw, col, prev=False):
    def blk(j):
        return jnp.maximum(j - 1, 0) if prev else j
    if dil == 16:
        return pl.BlockSpec((None, None, TQ, cw), lambda f, e, j: (f, e, blk(j), col))
    if dil == 4:
        return pl.BlockSpec((None, 4, 32, cw), lambda f, j: (f, 0, blk(j), col))
    return pl.BlockSpec((4, 4, 8, cw), lambda j: (0, 0, blk(j), col))


def _block_pos(i, dil):
    if dil == 16:
        return i
    if dil == 4:
        return 4 * (i % 32) + i // 32
    return 16 * (i % 8) + 4 * ((i // 8) % 4) + i // 32


def _band_mask(b, dil, max_dist):
    qi = _block_pos(lax.broadcasted_iota(jnp.int32, (TQ, 2 * TQ), 0), dil)
    col = lax.broadcasted_iota(jnp.int32, (TQ, 2 * TQ), 1)
    cur = col >= TQ
    dist = qi - _block_pos(col % TQ, dil) + jnp.where(cur, 0, TQ)
    return (dist >= 0) & (dist <= max_dist) & (cur | (b > 0))


def _hs(h):
    return slice(h * HEAD_DIM, (h + 1) * HEAD_DIM)


def _ld(ref, cols):
    v = ref[..., cols]
    return v.reshape(TQ, v.shape[-1])


def _st(ref, cols, val):
    ref[..., cols] = val.reshape(ref.shape[:-1] + (val.shape[-1],))


def _attn_fwd(z, dil, kw, kcol, vcol, n_rep, max_dist, name):
    s, zw = z.shape
    grid = _p_grid(s, dil)

    def body(q_ref, kp_ref, kc_ref, vp_ref, vc_ref, acc_ref, m_ref, l_ref):
        mask = _band_mask(pl.program_id(len(grid) - 1), dil, max_dist)
        for kh in range(N_HEADS // n_rep):
            k2 = jnp.concatenate([_ld(kp_ref, _hs(kh)), _ld(kc_ref, _hs(kh))], axis=0).astype(BF)
            v2 = jnp.concatenate([_ld(vp_ref, _hs(kh)), _ld(vc_ref, _hs(kh))], axis=0).astype(BF)
            for h in range(kh * n_rep, (kh + 1) * n_rep):
                q = _ld(q_ref, _hs(h)).astype(BF)
                sc = jnp.where(mask, _nt(q, k2) * SCALE, NEG)
                m = jnp.max(sc, axis=1, keepdims=True)
                p = jnp.exp(sc - m)
                _st(acc_ref, _hs(h), _nn(p.astype(BF), v2))
                _st(m_ref, _hs(h), jnp.broadcast_to(m, (TQ, HEAD_DIM)))
                _st(l_ref, _hs(h), jnp.broadcast_to(jnp.sum(p, axis=1, keepdims=True), (TQ, HEAD_DIM)))

    res = pl.pallas_call(
        body, grid=grid, name=name,
        in_specs=[_p_spec(dil, A_WIDTH, 0), _p_spec(dil, kw, kcol, True), _p_spec(dil, kw, kcol),
                  _p_spec(dil, kw, vcol, True), _p_spec(dil, kw, vcol)],
        out_specs=[_p_spec(dil, A_WIDTH, 0)] * 3,
        out_shape=[jax.ShapeDtypeStruct((4, 4, s // N_STRIPS, A_WIDTH), F32)] * 3,
        compiler_params=_cparams(*(("parallel",) * len(grid))),
    )(*[_strips(z)] * 5)
    return [a.reshape(s, A_WIDTH) for a in res]


def _attn_merge(parts_a, part_c, sink_row, tb):
    s = part_c[0].shape[0]
    n_a = len(parts_a)

    def body(*refs):
        ins, sink_ref = refs[:3 * n_a + 3], refs[3 * n_a + 3]
        ya_ref, lsea_ref, yc_ref, lsec_ref = refs[3 * n_a + 4:]
        ms = [ins[3 * p + 1][...] for p in range(n_a)]
        m = functools.reduce(jnp.maximum, ms)
        acc = jnp.zeros_like(m)
        l = jnp.zeros_like(m)
        for p in range(n_a):
            w = jnp.exp(ms[p] - m)
            acc = acc + w * ins[3 * p][...]
            l = l + w * ins[3 * p + 2][...]
        ya_ref[...] = acc / l
        lsea_ref[...] = m + jnp.log(l)
        acc_c, m_c, l_c = [r[...] for r in ins[3 * n_a:]]
        sk = sink_ref[...]
        m2 = jnp.maximum(m_c, sk)
        w = jnp.exp(m_c - m2)
        l2 = w * l_c + jnp.exp(sk - m2)
        yc_ref[...] = (w * acc_c) / l2
        lsec_ref[...] = m2 + jnp.log(l2)

    return pl.pallas_call(
        body, grid=(s // tb,), name="attn_merge",
        in_specs=[_rows(tb, A_WIDTH)] * (3 * n_a + 3) + [_whole((1, A_WIDTH))],
        out_specs=[_rows(tb, A_WIDTH)] * 4, out_shape=[jax.ShapeDtypeStruct((s, A_WIDTH), F32)] * 4,
        compiler_params=_cparams("parallel"),
    )(*[a for part in parts_a + [part_c] for a in part], sink_row)


def _shift_down(v, n, halo):
    rows = v.shape[0]
    out = pltpu.roll(v, n, 0)
    row = lax.broadcasted_iota(jnp.int32, v.shape, 0)
    for t in range(n):
        out = jnp.where(row == t, halo[HALO - n + t:HALO - n + t + 1, :], out)
    return out


def _shift_up(v, n, halo):
    rows = v.shape[0]
    out = pltpu.roll(v, rows - n, 0)
    row = lax.broadcasted_iota(jnp.int32, v.shape, 0)
    for t in range(n):
        out = jnp.where(row == rows - n + t, halo[t:t + 1, :], out)
    return out


def _strip(v, b):
    return v[b % 4, b // 4]


def _conv_strips(zb, prev, cw):
    gb = [_strip(zb, b)[:, :CONV_CH] for b in range(N_STRIPS)]
    gc = [_strip(zb, b)[:, CONV_CH:2 * CONV_CH] for b in range(N_STRIPS)]
    xb = [_strip(zb, b)[:, 2 * CONV_CH:] for b in range(N_STRIPS)]
    u = [g * v for g, v in zip(gc, xb)]
    uh = prev[:, :, CONV_CH:2 * CONV_CH] * prev[:, :, 2 * CONV_CH:]
    wrapped = {14: _shift_down(u[14], 1, uh[2]), 15: _shift_down(u[15], 1, uh[3])}
    u1 = [u[b - 1] if b >= 1 else wrapped[15] for b in range(N_STRIPS)]
    u2 = [u[b - 2] if b >= 2 else wrapped[14 + b] for b in range(N_STRIPS)]
    c = [cw[0:1, :] * u2[b] + cw[1:2, :] * u1[b] + cw[2:3, :] * u[b] for b in range(N_STRIPS)]
    return gb, gc, xb, u, u1, u2, c


def _strip_rows(ta, w):
    return pl.BlockSpec((4, 4, ta, w), lambda i: (0, 0, i, 0))


def _prev_rows(ta, w):
    return pl.BlockSpec((4, None, HALO, w), lambda i: (0, 3, jnp.maximum(i * (ta // HALO) - 1, 0), 0))


def _next_rows(ta, w, nblk):
    return pl.BlockSpec((4, None, HALO, w),
                        lambda i: (0, 0, jnp.minimum((i + 1) * (ta // HALO), nblk * (ta // HALO) - 1), 0))


def _mix_fwd(x, ya, yc, zb, cw, gg, wo_all, l, tb):
    s, d = x.shape
    ta = tb // N_STRIPS

    def body(x_ref, ya_ref, yc_ref, zb_ref, zbp_ref, cw_ref, gg_ref, wo_ref, x1_ref, yb_ref):
        i = pl.program_id(0)
        prev = jnp.where(i > 0, zbp_ref[...], 0.0)
        gb, _, _, _, _, _, c = _conv_strips(zb_ref[...], prev, cw_ref[...])
        for b in range(N_STRIPS):
            yb_ref[b % 4, b // 4] = gb[b] * c[b]
        yb = yb_ref[...].reshape(tb, CONV_CH)
        ya, yc = ya_ref[...].reshape(tb, A_WIDTH), yc_ref[...].reshape(tb, A_WIDTH)
        n = jnp.concatenate([ya * _rms_scale(ya), yb * _rms_scale(yb), yc * _rms_scale(yc)], axis=1)
        n = (n * gg_ref[...]).astype(BF)
        x1 = x_ref[...].reshape(tb, d) + _nn(n, wo_ref[...].reshape(MIX_WIDTH, d))
        x1_ref[...] = x1.reshape(4, 4, ta, d)

    res = pl.pallas_call(
        body, grid=(s // tb,), name="mix_fwd",
        in_specs=[_strip_rows(ta, d), _strip_rows(ta, A_WIDTH), _strip_rows(ta, A_WIDTH), _strip_rows(ta, ZB_W),
                  _prev_rows(ta, ZB_W), _whole((HALO, CONV_CH)), _whole((1, MIX_WIDTH)),
                  _layer((N_CHIPS, MIX_WIDTH // N_CHIPS, d), l)],
        out_specs=[_strip_rows(ta, d), _strip_rows(ta, CONV_CH)],
        out_shape=[jax.ShapeDtypeStruct((4, 4, s // N_STRIPS, d), F32),
                   jax.ShapeDtypeStruct((4, 4, s // N_STRIPS, CONV_CH), F32)],
        compiler_params=_cparams("parallel"),
    )(_strips(x), _strips(ya), _strips(yc), _strips(zb), _strips(zb), cw, gg, wo_all)
    return res[0].reshape(s, d), res[1].reshape(s, CONV_CH)


def _mlp_fwd(x1, g, w1_all, w2_all, l, tb, tf):
    s, d = x1.shape
    ff = w1_all.shape[1] * w1_all.shape[3]
    nj = ff // tf

    def body(x_ref, g_ref, w1_ref, w2_ref, x2_ref, h2_ref, ap_ref, acc):
        j = pl.program_id(1)

        @pl.when(j == 0)
        def _():
            xv = x_ref[...]
            h2_ref[...] = ((xv * _rms_scale(xv)) * g_ref[...]).astype(BF)
            acc[...] = jnp.zeros_like(acc)

        ap = _nn(h2_ref[...], w1_ref[...])
        ap_ref[...] = ap.astype(BF)
        a = jnp.square(jnp.maximum(ap, 0.0)).astype(BF)
        acc[...] += _nn(a, w2_ref[...])

        @pl.when(j == nj - 1)
        def _():
            x2_ref[...] = x_ref[...] + acc[...]

    return pl.pallas_call(
        body, grid=(s // tb, nj), name="mlp_fwd",
        in_specs=[pl.BlockSpec((tb, d), lambda i, j: (i, 0)), _whole((1, d)),
                  pl.BlockSpec((None, None, d, tf), lambda i, j: (l, j, 0, 0)),
                  pl.BlockSpec((None, None, tf, d), lambda i, j: (l, j, 0, 0))],
        out_specs=[pl.BlockSpec((tb, d), lambda i, j: (i, 0)), pl.BlockSpec((tb, d), lambda i, j: (i, 0)),
                   pl.BlockSpec((tb, tf), lambda i, j: (i, j))],
        out_shape=[jax.ShapeDtypeStruct((s, d), F32), jax.ShapeDtypeStruct((s, d), BF),
                   jax.ShapeDtypeStruct((s, ff), BF)],
        scratch_shapes=[pltpu.VMEM((tb, d), F32)],
        compiler_params=_cparams("parallel", "arbitrary"),
    )(x1, g, w1_all, w2_all)


def _loss_head(x, g, tgt, tb):
    s, d = x.shape

    def body(x_ref, g_ref, t_ref, dx_ref, loss_ref, dg_ref):
        i = pl.program_id(0)

        @pl.when(i == 0)
        def _():
            loss_ref[...] = jnp.zeros_like(loss_ref)
            dg_ref[...] = jnp.zeros_like(dg_ref)

        xv = x_ref[...]
        r = _rms_scale(xv)
        xhat = xv * r
        err = xhat * g_ref[...] - t_ref[...]
        part = jnp.sum(jnp.mean(jnp.square(err), axis=-1, keepdims=True), axis=0, keepdims=True)
        loss_ref[...] += 0.5 * part
        dy = err * (1.0 / d)
        dg_ref[...] += jnp.sum(dy * xhat, axis=0, keepdims=True)
        dx_ref[...] = _norm_bwd(dy * g_ref[...], xhat, r)

    return pl.pallas_call(
        body, grid=(s // tb,), name="loss_head",
        in_specs=[_rows(tb, d), _whole((1, d)), _rows(tb, d)],
        out_specs=[_rows(tb, d), _whole((HALO, 128)), _whole((HALO, d))],
        out_shape=[jax.ShapeDtypeStruct((s, d), F32), jax.ShapeDtypeStruct((HALO, 128), F32),
                   jax.ShapeDtypeStruct((HALO, d), F32)],
        compiler_params=_cparams("arbitrary"),
    )(x, g, tgt)


def _mlp_bwd(dx2, x1, ap, g, w1_all, w2_all, l, tb, tf):
    s, d = x1.shape
    ff = ap.shape[1]
    nj = ff // tf

    def body(dx2_ref, x1_ref, ap_ref, g_ref, w1_ref, w2_ref, dx1_ref, dap_ref, dg_ref, acc):
        i, j = pl.program_id(0), pl.program_id(1)

        @pl.when((i == 0) & (j == 0))
        def _():
            dg_ref[...] = jnp.zeros_like(dg_ref)

        @pl.when(j == 0)
        def _():
            acc[...] = jnp.zeros_like(acc)

        da = _nt(dx2_ref[...].astype(BF), w2_ref[...])
        dap = (da * (2.0 * jnp.maximum(ap_ref[...].astype(F32), 0.0))).astype(BF)
        dap_ref[...] = dap
        acc[...] += _nt(dap, w1_ref[...])

        @pl.when(j == nj - 1)
        def _():
            xv = x1_ref[...]
            r = _rms_scale(xv)
            xhat = xv * r
            dh = acc[...]
            dg_ref[...] += jnp.sum(dh * xhat, axis=0, keepdims=True)
            dx1_ref[...] = dx2_ref[...] + _norm_bwd(dh * g_ref[...], xhat, r)

    return pl.pallas_call(
        body, grid=(s // tb, nj), name="mlp_bwd",
        in_specs=[pl.BlockSpec((tb, d), lambda i, j: (i, 0)), pl.BlockSpec((tb, d), lambda i, j: (i, 0)),
                  pl.BlockSpec((tb, tf), lambda i, j: (i, j)),
                  _whole((1, d)), pl.BlockSpec((None, None, d, tf), lambda i, j: (l, j, 0, 0)),
                  pl.BlockSpec((None, None, tf, d), lambda i, j: (l, j, 0, 0))],
        out_specs=[pl.BlockSpec((tb, d), lambda i, j: (i, 0)), pl.BlockSpec((tb, tf), lambda i, j: (i, j)),
                   _whole((HALO, d))],
        out_shape=[jax.ShapeDtypeStruct((s, d), F32), jax.ShapeDtypeStruct((s, ff), BF),
                   jax.ShapeDtypeStruct((HALO, d), F32)],
        scratch_shapes=[pltpu.VMEM((tb, d), F32)],
        compiler_params=_cparams("arbitrary", "arbitrary"),
    )(dx2, x1, ap, g, w1_all, w2_all)


def _wgrad(a, b, tm, tn, ts, name, relu2=False):
    s, m = a.shape
    n = b.shape[1]
    ns = s // ts

    def body(a_ref, b_ref, o_ref, acc):
        k = pl.program_id(2)

        @pl.when(k == 0)
        def _():
            acc[...] = jnp.zeros_like(acc)

        av = a_ref[...]
        if relu2:
            av = jnp.square(jnp.maximum(av.astype(F32), 0.0)).astype(BF)
        acc[...] += _tn(av, b_ref[...].astype(BF))

        @pl.when(k == ns - 1)
        def _():
            o_ref[...] = acc[...].astype(BF)

    return pl.pallas_call(
        body, grid=(m // tm, n // tn, ns), name=name,
        in_specs=[pl.BlockSpec((ts, tm), lambda i, j, k: (k, i)), pl.BlockSpec((ts, tn), lambda i, j, k: (k, j))],
        out_specs=pl.BlockSpec((tm, tn), lambda i, j, k: (i, j)),
        out_shape=jax.ShapeDtypeStruct((m, n), BF),
        scratch_shapes=[pltpu.VMEM((tm, tn), F32)],
        compiler_params=_cparams("parallel", "parallel", "arbitrary"),
    )(a, b)


def _mix_bwd(dx1, ya, yb, yc, lse_c, sink_row, gg, wo_all, l, tb):
    s, d = dx1.shape

    def body(dx_ref, ya_ref, yb_ref, yc_ref, lse_ref, sink_ref, gg_ref, wo_ref,
             n_ref, dya_ref, dyc_ref, da_ref, dc_ref, dyb_ref, dg_ref, dsink_ref):
        i = pl.program_id(0)

        @pl.when(i == 0)
        def _():
            dg_ref[...] = jnp.zeros_like(dg_ref)
            dsink_ref[...] = jnp.zeros_like(dsink_ref)

        dn = _nt(dx_ref[...].astype(BF), wo_ref[...].reshape(MIX_WIDTH, d))
        ys = [ya_ref[...], yb_ref[...], yc_ref[...]]
        rs = [_rms_scale(v) for v in ys]
        nhat = jnp.concatenate([v * r for v, r in zip(ys, rs)], axis=1)
        gg = gg_ref[...]
        n_ref[...] = (nhat * gg).astype(BF)
        dg_ref[...] += jnp.sum(dn * nhat, axis=0, keepdims=True)
        dnh = dn * gg
        bounds = [(0, A_WIDTH), (A_WIDTH, A_WIDTH + CONV_CH), (A_WIDTH + CONV_CH, MIX_WIDTH)]
        dys = [_norm_bwd(dnh[:, lo:hi], nhat[:, lo:hi], r) for (lo, hi), r in zip(bounds, rs)]
        dyb_ref[...] = dys[1]
        for dy, y, dy_ref, dd_ref in ((dys[0], ys[0], dya_ref, da_ref), (dys[2], ys[2], dyc_ref, dc_ref)):
            dy_ref[...] = dy
            t = dy * y
            for h in range(N_HEADS):
                dd_ref[:, _hs(h)] = jnp.broadcast_to(jnp.sum(t[:, _hs(h)], axis=1, keepdims=True), (tb, HEAD_DIM))
        dsink_ref[...] -= jnp.sum(jnp.exp(sink_ref[...] - lse_ref[...]) * dc_ref[...], axis=0, keepdims=True)

    return pl.pallas_call(
        body, grid=(s // tb,), name="mix_bwd",
        in_specs=[_rows(tb, d), _rows(tb, A_WIDTH), _rows(tb, CONV_CH), _rows(tb, A_WIDTH), _rows(tb, A_WIDTH),
                  _whole((1, A_WIDTH)), _whole((1, MIX_WIDTH)), _layer((N_CHIPS, MIX_WIDTH // N_CHIPS, d), l)],
        out_specs=[_rows(tb, MIX_WIDTH), _rows(tb, A_WIDTH), _rows(tb, A_WIDTH), _rows(tb, A_WIDTH),
                   _rows(tb, A_WIDTH), _rows(tb, CONV_CH), _whole((HALO, MIX_WIDTH)), _whole((HALO, A_WIDTH))],
        out_shape=[jax.ShapeDtypeStruct((s, MIX_WIDTH), BF), jax.ShapeDtypeStruct((s, A_WIDTH), F32),
                   jax.ShapeDtypeStruct((s, A_WIDTH), F32), jax.ShapeDtypeStruct((s, A_WIDTH), F32),
                   jax.ShapeDtypeStruct((s, A_WIDTH), F32), jax.ShapeDtypeStruct((s, CONV_CH), F32),
                   jax.ShapeDtypeStruct((HALO, MIX_WIDTH), F32), jax.ShapeDtypeStruct((HALO, A_WIDTH), F32)],
        compiler_params=_cparams("arbitrary"),
    )(dx1, ya, yb, yc, lse_c, sink_row, gg, wo_all)


def _attn_bwd(z, dy, lse, dd, dil, kw, kcol, vcol, n_rep, max_dist, name):
    s, zw = z.shape
    grid = _p_grid(s, dil)
    n_kv = N_HEADS // n_rep

    def body(q_ref, kp_ref, kc_ref, vp_ref, vc_ref, dy_ref, lse_ref, dd_ref, dq_ref, dkp_ref, dkc_ref, dvp_ref, dvc_ref):
        mask = _band_mask(pl.program_id(len(grid) - 1), dil, max_dist)
        for kh in range(n_kv):
            k2 = jnp.concatenate([_ld(kp_ref, _hs(kh)), _ld(kc_ref, _hs(kh))], axis=0).astype(BF)
            v2 = jnp.concatenate([_ld(vp_ref, _hs(kh)), _ld(vc_ref, _hs(kh))], axis=0).astype(BF)
            dk2 = jnp.zeros((2 * TQ, HEAD_DIM), F32)
            dv2 = jnp.zeros((2 * TQ, HEAD_DIM), F32)
            for h in range(kh * n_rep, (kh + 1) * n_rep):
                q = _ld(q_ref, _hs(h)).astype(BF)
                lse_h = _ld(lse_ref, slice(h * HEAD_DIM, h * HEAD_DIM + 1))
                dd_h = _ld(dd_ref, slice(h * HEAD_DIM, h * HEAD_DIM + 1))
                dyh = _ld(dy_ref, _hs(h)).astype(BF)
                sc = jnp.where(mask, _nt(q, k2) * SCALE, NEG)
                p = jnp.exp(sc - lse_h)
                dp = _nt(dyh, v2)
                ds = ((p * (dp - dd_h)) * SCALE).astype(BF)
                _st(dq_ref, _hs(h), _nn(ds, k2))
                dk2 = dk2 + _tn(ds, q)
                dv2 = dv2 + _tn(p.astype(BF), dyh)
            _st(dkp_ref, _hs(kh), dk2[:TQ])
            _st(dkc_ref, _hs(kh), dk2[TQ:])
            _st(dvp_ref, _hs(kh), dv2[:TQ])
            _st(dvc_ref, _hs(kh), dv2[TQ:])

    args = [_strips(z)] * 5 + [_strips(a) for a in (dy, lse, dd)]
    in_specs = [_p_spec(dil, A_WIDTH, 0), _p_spec(dil, kw, kcol, True), _p_spec(dil, kw, kcol),
                _p_spec(dil, kw, vcol, True), _p_spec(dil, kw, vcol)] + [_p_spec(dil, A_WIDTH, 0)] * 3
    out_specs = [_p_spec(dil, A_WIDTH, 0)] + [_p_spec(dil, kw, 0)] * 4
    na = s // N_STRIPS
    out_shape = [jax.ShapeDtypeStruct((4, 4, na, A_WIDTH), F32)] + [jax.ShapeDtypeStruct((4, 4, na, kw), F32)] * 4
    res = pl.pallas_call(
        body, grid=grid, name=name, in_specs=in_specs, out_specs=out_specs, out_shape=out_shape,
        compiler_params=_cparams(*(("parallel",) * len(grid))),
    )(*args)
    return [res[0].reshape(s, A_WIDTH)] + [a.reshape(s, kw) for a in res[1:]]


DZ_TA = 16


def _dz_assemble(parts_a, parts_c, dyb, zb, cw):
    s = zb.shape[0]
    na = s // N_STRIPS
    nb = na // DZ_TA

    def ahead(w, k):
        return pl.BlockSpec((4, 4, DZ_TA, w), lambda i: (0, 0, jnp.minimum(i + k, nb - 1), 0))

    args, in_specs = [], []
    for dil, (dq, dkp, dkc, dvp, dvc) in zip(DILATIONS + (1,), parts_a + [parts_c]):
        w = dkp.shape[1]
        here = _strip_rows(DZ_TA, w)
        if dil == 1:
            args += [dq, dkp, dkp, dkc, dvp, dvp, dvc]
            in_specs += [_strip_rows(DZ_TA, A_WIDTH), here, ahead(w, 1), here, here, ahead(w, 1), here]
        else:
            k = 8 * dil // DZ_TA
            args += [dq, dkp, dkc, dvp, dvc]
            in_specs += [_strip_rows(DZ_TA, A_WIDTH), ahead(w, k), here, ahead(w, k), here]
    n_att = len(args)
    args = [_strips(a) for a in args] + [_strips(dyb), _strips(dyb), _strips(zb), _strips(zb), _strips(zb), cw]
    in_specs += [_strip_rows(DZ_TA, CONV_CH), _next_rows(DZ_TA, CONV_CH, nb), _strip_rows(DZ_TA, ZB_W),
                 _prev_rows(DZ_TA, ZB_W), _next_rows(DZ_TA, ZB_W, nb), _whole((HALO, CONV_CH))]

    def body(*refs):
        att = list(refs[:n_att])
        dyb_ref, dybn_ref, zb_ref, zbp_ref, zbn_ref, cw_ref, dz_ref, dcw_ref = refs[n_att:]
        i = pl.program_id(0)

        @pl.when(i == 0)
        def _():
            dcw_ref[...] = jnp.zeros_like(dcw_ref)

        def shifted(dil):
            if dil == 1:
                dq_r, kp0, kp1, dkc_r, vp0, vp1, dvc_r = [att.pop(0) for _ in range(7)]
                live = i + 1 < nb
                half = DZ_TA // 2
                dkp = jnp.concatenate([kp0[:, :, half:, :], jnp.where(live, kp1[:, :, :half, :], 0.0)], axis=2)
                dvp = jnp.concatenate([vp0[:, :, half:, :], jnp.where(live, vp1[:, :, :half, :], 0.0)], axis=2)
            else:
                dq_r, dkp_r, dkc_r, dvp_r, dvc_r = [att.pop(0) for _ in range(5)]
                live = i + 8 * dil // DZ_TA < nb
                dkp, dvp = jnp.where(live, dkp_r[...], 0.0), jnp.where(live, dvp_r[...], 0.0)
            return dq_r[...], dkc_r[...] + dkp, dvc_r[...] + dvp

        dq, dk, dv = shifted(DILATIONS[0])
        for dil in DILATIONS[1:]:
            dq2, dk2, dv2 = shifted(dil)
            dq, dk, dv = dq + dq2, dk + dk2, dv + dv2
        dz_ref[:, :, :, 0:A_WIDTH] = dq.astype(BF)
        dz_ref[:, :, :, A_WIDTH:2 * A_WIDTH] = dk.astype(BF)
        dz_ref[:, :, :, 2 * A_WIDTH:ZA_W] = dv.astype(BF)
        dq, dk, dv = shifted(1)
        c0 = ZA_W + ZB_W
        dz_ref[:, :, :, c0:c0 + A_WIDTH] = dq.astype(BF)
        dz_ref[:, :, :, c0 + A_WIDTH:c0 + A_WIDTH + C_KV_WIDTH] = dk.astype(BF)
        dz_ref[:, :, :, c0 + A_WIDTH + C_KV_WIDTH:IN_WIDTH] = dv.astype(BF)

        cw = cw_ref[...]
        prev = jnp.where(i > 0, zbp_ref[...], 0.0)
        gb, gc, xb, u, u1, u2, c = _conv_strips(zb_ref[...], prev, cw)
        dyb = dyb_ref[...]
        dc = [_strip(dyb, b) * gb[b] for b in range(N_STRIPS)]
        dcn = jnp.where(i + 1 < nb, dybn_ref[...] * zbn_ref[:, :, :CONV_CH], 0.0)
        wrapped = [_shift_up(dc[0], 1, dcn[0]), _shift_up(dc[1], 1, dcn[1])]
        upd = [jnp.zeros((1, CONV_CH), F32)] * 3
        for b in range(N_STRIPS):
            dc1 = dc[b + 1] if b + 1 < N_STRIPS else wrapped[0]
            dc2 = dc[b + 2] if b + 2 < N_STRIPS else wrapped[b + 2 - N_STRIPS]
            du = cw[2:3, :] * dc[b] + cw[1:2, :] * dc1 + cw[0:1, :] * dc2
            f, e = b % 4, b // 4
            dz_ref[f, e, :, ZA_W:ZA_W + CONV_CH] = (_strip(dyb, b) * c[b]).astype(BF)
            dz_ref[f, e, :, ZA_W + CONV_CH:ZA_W + 2 * CONV_CH] = (du * xb[b]).astype(BF)
            dz_ref[f, e, :, ZA_W + 2 * CONV_CH:c0] = (du * gc[b]).astype(BF)
            for t, uu in enumerate((u2[b], u1[b], u[b])):
                upd[t] = upd[t] + jnp.sum(dc[b] * uu, axis=0, keepdims=True)
        row = lax.broadcasted_iota(jnp.int32, (HALO, CONV_CH), 0)
        tile = jnp.zeros((HALO, CONV_CH), F32)
        for t in range(3):
            tile = jnp.where(row == t, upd[t], tile)
        dcw_ref[...] += tile

    dz, dcw = pl.pallas_call(
        body, grid=(nb,), name="dz_assemble", in_specs=in_specs,
        out_specs=[_strip_rows(DZ_TA, IN_WIDTH), _whole((HALO, CONV_CH))],
        out_shape=[jax.ShapeDtypeStruct((4, 4, na, IN_WIDTH), BF), jax.ShapeDtypeStruct((HALO, CONV_CH), F32)],
        compiler_params=_cparams("arbitrary"),
    )(*args)
    return dz.reshape(s, IN_WIDTH), dcw


def _qkv_bwd(dz, dx1, x, g, w_all, l, tb):
    s, d = x.shape

    def body(dz_ref, dx1_ref, x_ref, g_ref, w_ref, dx_ref, dg_ref):
        i = pl.program_id(0)

        @pl.when(i == 0)
        def _():
            dg_ref[...] = jnp.zeros_like(dg_ref)

        n = IN_WIDTH // N_CHIPS
        dh = _nt(dz_ref[:, 0:n], w_ref[0])
        for k in range(1, N_CHIPS):
            dh = dh + _nt(dz_ref[:, k * n:(k + 1) * n], w_ref[k])
        xv = x_ref[...]
        r = _rms_scale(xv)
        xhat = xv * r
        dg_ref[...] += jnp.sum(dh * xhat, axis=0, keepdims=True)
        dx_ref[...] = dx1_ref[...] + _norm_bwd(dh * g_ref[...], xhat, r)

    return pl.pallas_call(
        body, grid=(s // tb,), name="qkv_bwd",
        in_specs=[_rows(tb, IN_WIDTH), _rows(tb, d), _rows(tb, d), _whole((1, d)),
                  _layer((N_CHIPS, d, IN_WIDTH // N_CHIPS), l)],
        out_specs=[_rows(tb, d), _whole((HALO, d))],
        out_shape=[jax.ShapeDtypeStruct((s, d), F32), jax.ShapeDtypeStruct((HALO, d), F32)],
        compiler_params=_cparams("arbitrary"),
    )(dz, dx1, x, g, w_all)


def _tile_rows(rows):
    return jnp.pad(rows, ((0, HALO - rows.shape[0]), (0, 0)))


def _to_strips(a, after, name):
    s, d = a.shape
    na = s // N_STRIPS
    ta = min(32, na)

    def body(a_ref, after_ref, o_ref):
        for b in range(N_STRIPS):
            o_ref[b % 4, b // 4] = a_ref[:, b, :]

    return pl.pallas_call(
        body, grid=(na // ta,), name=name,
        in_specs=[pl.BlockSpec((ta, N_STRIPS, d), lambda i: (i, 0, 0)), ANY], out_specs=_strip_rows(ta, d),
        out_shape=jax.ShapeDtypeStruct((4, 4, na, d), a.dtype), compiler_params=_cparams("parallel"),
    )(a.reshape(na, N_STRIPS, d), after).reshape(s, d)


def _from_strips(a, name):
    s, d = a.shape
    na = s // N_STRIPS
    ta = min(32, na)

    def body(a_ref, o_ref):
        for b in range(N_STRIPS):
            o_ref[:, b, :] = a_ref[b % 4, b // 4]

    return pl.pallas_call(
        body, grid=(na // ta,), name=name, in_specs=[_strip_rows(ta, d)],
        out_specs=pl.BlockSpec((ta, N_STRIPS, d), lambda i: (i, 0, 0)),
        out_shape=jax.ShapeDtypeStruct((na, N_STRIPS, d), a.dtype), compiler_params=_cparams("parallel"),
    )(_strips(a)).reshape(s, d)


def _local_step(x, tgt, fetch, ff, sinks, g_mix, g_group, g_mlp, g_final, emit):
    s, d = x.shape
    depth = g_mix.shape[0]
    tb = min(512, s)
    tf = ff // N_CHIPS
    ts = min(1024, s)
    saved = []
    for l in range(depth):
        w_in, _, _, _, conv_w = fetch(0, l, x)
        cw = _tile_rows(conv_w[l])
        sk = jnp.repeat(sinks[l].reshape(N_HEADS), HEAD_DIM)[None]
        h, za, zb, zc = _qkv_fwd(x, g_mix[l][None], w_in, l, tb)
        parts_a = [_attn_fwd(za, dil, A_WIDTH, 1, 2, 1, A_MAX_DIST, "attn_a_fwd_%d" % dil) for dil in DILATIONS]
        part_c = _attn_fwd(zc, 1, C_KV_WIDTH, 3, 4, C_GROUP, C_MAX_DIST, "attn_c_fwd")
        ya, lse_a, yc, lse_c = _attn_merge(parts_a, part_c, sk, tb)
        w_in, w_o, w1, w2, _ = fetch(1, l, yc)
        x1, yb = _mix_fwd(x, ya, yc, zb, cw, g_group[l][None], w_o, l, tb)
        x2, h2, ap = _mlp_fwd(x1, g_mlp[l][None], w1, w2, l, ts, tf)
        saved.append((x, h, za, zb, zc, ya, lse_a, yc, lse_c, yb, x1, h2, ap, cw, sk))
        x = x2
    dx, loss_tile, dg_final = _loss_head(x, g_final[None], tgt, tb)
    grads = [None] * depth
    tok = jnp.zeros((), F32)
    for l in reversed(range(depth)):
        x0, h, za, zb, zc, ya, lse_a, yc, lse_c, yb, x1, h2, ap, cw, sk = saved[l]
        dx1, dap, dg_mlp = _mlp_bwd(dx, x1, ap, g_mlp[l][None] + tok, w1, w2, l, ts, tf)
        tok = emit(l, 3, _wgrad(ap, dx, min(1024, ff), d, ts, "wgrad_ff_out", relu2=True))
        tok = tok + emit(l, 2, _wgrad(h2, dap, d, min(1024, ff), ts, "wgrad_ff_in"))
        n, dya, dyc, dd_a, dd_c, dyb, dg_group, dsink = _mix_bwd(dx1, ya, yb, yc, lse_c, sk, g_group[l][None] + tok,
                                                                 w_o, l, tb)
        tok = emit(l, 1, _wgrad(n, dx1, MIX_WIDTH, d, ts, "wgrad_o"))
        cw = cw + tok
        parts_a = [_attn_bwd(za, dya, lse_a, dd_a, dil, A_WIDTH, 1, 2, 1, A_MAX_DIST, "attn_a_bwd_%d" % dil)
                   for dil in DILATIONS]
        parts_c = _attn_bwd(zc, dyc, lse_c, dd_c, 1, C_KV_WIDTH, 3, 4, C_GROUP, C_MAX_DIST, "attn_c_bwd")
        dz, dcw = _dz_assemble(parts_a, parts_c, dyb, zb, cw)
        tok = emit(l, 0, _wgrad(h, dz, d, IN_WIDTH // 4, ts, "wgrad_in"))
        dx, dg_mix = _qkv_bwd(dz, dx1, x0, g_mix[l][None] + tok, w_in, l, tb)
        grads[l] = (dcw, dsink, dg_mix, dg_group, dg_mlp)
    return loss_tile, dx, grads, dg_final


ANY = pl.BlockSpec(memory_space=pl.ANY)
SHARD_AXES = (2, 1, 2, 1)
N_BIG = len(SHARD_AXES)
N_CHIPS = 4
N_DEV = 8


def _mesh_pos():
    return lax.axis_index("x"), lax.axis_index("y"), lax.axis_index("c")


def _flip(v, bit):
    return 1 - v if bit else v


def _place_shard(shard, chip_arr, name):
    _, rows, cols = shard.shape
    tr = min(256, rows)

    def body(chip_ref, x_ref, o_ref):
        o_ref[...] = x_ref[...].astype(BF)

    return pl.pallas_call(
        body, name=name,
        grid_spec=pltpu.PrefetchScalarGridSpec(
            num_scalar_prefetch=1, grid=(2, rows // tr),
            in_specs=[pl.BlockSpec((None, tr, cols), lambda l, i, chip: (l, i, 0))],
            out_specs=pl.BlockSpec((None, None, tr, cols), lambda l, i, chip: (l, chip[0], i, 0))),
        out_shape=jax.ShapeDtypeStruct((2, N_CHIPS, rows, cols), BF),
        compiler_params=_cparams("parallel", "parallel"),
    )(chip_arr, shard)


HBM = pl.BlockSpec(memory_space=pltpu.HBM)
SEM = pl.BlockSpec(memory_space=pltpu.SEMAPHORE)
EFFECT = pltpu.SideEffectType.DATAFLOW_SIDE_EFFECTING

GATHER_GROUPS = (((0, 0),), ((1, 0), (2, 0), (3, 0)), ((0, 1),), ((1, 1), (2, 1), (3, 1)))
GATHER_STARTS = ((0,), (1,), (2, 3))


def _gather_copies(arrs, group, send_sems, recv_sems):
    x, y, c = _mesh_pos()
    me = 2 * x + y
    out = []
    for i, (w, layer) in enumerate(group):
        mine = arrs[w].at[layer, me]
        for j, (qx, qy) in enumerate([(1 - x, y), (x, 1 - y), (1 - x, 1 - y)]):
            landed = arrs[w].at[layer, 2 * qx + qy]
            out.append(tuple(pltpu.make_async_remote_copy(
                src_ref=piece, dst_ref=piece, send_sem=send_sems.at[i * 3 + j], recv_sem=recv_sems.at[i * 3 + j],
                device_id=(qx, qy, c), device_id_type=MESH) for piece in (mine, landed)))
    return out


def _conv_copies(conv_src, conv_dst, send_sems, recv_sems):
    x, y, c = _mesh_pos()
    out = []
    for j, (qx, qy) in enumerate([(1 - x, y), (x, 1 - y), (1 - x, 1 - y)]):
        out.append(tuple(pltpu.make_async_remote_copy(
            src_ref=conv_src, dst_ref=conv_dst.at[q], send_sem=send_sems.at[j], recv_sem=recv_sems.at[j],
            device_id=(qx, qy, c), device_id_type=MESH) for q in (2 * x + y, 2 * qx + qy)))
    return out


def _gather_start(groups, arrs, conv, name, through=None):
    n_sems = 2 * (len(groups) + (conv is not None))
    mats = sorted({w for g in groups for w, _ in GATHER_GROUPS[g]})

    def body(*refs):
        arrs_ref = [None] * N_BIG
        for w, ref in zip(mats, refs):
            arrs_ref[w] = ref
        sems = refs[n_in:n_in + n_sems]
        if conv is not None:
            for cp, _ in _conv_copies(refs[len(mats)], refs[len(mats) + 1], sems[-2], sems[-1]):
                cp.start()
        for k, g in enumerate(groups):
            for cp, _ in _gather_copies(arrs_ref, GATHER_GROUPS[g], sems[2 * k], sems[2 * k + 1]):
                cp.start()

    sem_shapes = []
    for n in [len(GATHER_GROUPS[g]) for g in groups] + ([1] if conv is not None else []):
        sem_shapes += [pltpu.SemaphoreType.DMA((3 * n,))] * 2
    operands = [arrs[w] for w in mats] + ([] if conv is None else list(conv)) + ([] if through is None else [through])
    n_in = len(operands)
    res = pl.pallas_call(
        body, name=name,
        out_shape=tuple(sem_shapes) + tuple(pltpu.HBM(a.shape, a.dtype) for a in operands),
        in_specs=(HBM,) * n_in, out_specs=(SEM,) * n_sems + (HBM,) * n_in,
        input_output_aliases={i: n_sems + i for i in range(n_in)},
        compiler_params=pltpu.CompilerParams(has_side_effects=EFFECT),
    )(*[pltpu.with_memory_space_constraint(a, pltpu.HBM) for a in operands])
    arrs = list(arrs)
    for w, a in zip(mats, res[n_sems:]):
        arrs[w] = a
    return res[:n_sems], arrs, list(res[n_sems + len(mats):])


def _gather_wait(k, sems, arrs, conv, after, name):
    group = GATHER_GROUPS[k]
    mats = sorted({w for w, _ in group})
    n_conv = 0 if conv is None else 2

    def body(*refs):
        local = refs[:len(mats)]
        arrs_ref = [None] * N_BIG
        for w, ref in zip(mats, local):
            arrs_ref[w] = ref
        pos = len(mats) + n_conv
        copies = _gather_copies(arrs_ref, group, refs[pos], refs[pos + 1])
        if conv is not None:
            copies += _conv_copies(refs[len(mats)], refs[len(mats) + 1], refs[pos + 2], refs[pos + 3])
        for send, recv in copies:
            recv.wait_recv()
            send.wait_send()

    operands = [arrs[w] for w in mats] + ([] if conv is None else [conv[1], conv[2]])
    sem_ops = list(sems) + ([] if conv is None else list(conv[0]))
    n_op = len(operands)
    res = pl.pallas_call(
        body, name=name, out_shape=tuple(pltpu.HBM(a.shape, a.dtype) for a in operands),
        in_specs=(HBM,) * n_op + (SEM,) * len(sem_ops) + (ANY,), out_specs=(HBM,) * n_op,
        input_output_aliases={i: i for i in range(n_op)},
        compiler_params=pltpu.CompilerParams(has_side_effects=EFFECT),
    )(*operands, *sem_ops, after)
    arrs = list(arrs)
    for w, a in zip(mats, res):
        arrs[w] = a
    return arrs, (res[-1] if conv is not None else None)


def _grad_shard(ref, w, chip, n):
    start = pl.multiple_of(chip * n, 128)
    if SHARD_AXES[w] == 2:
        return ref.at[:, pl.ds(start, n)]
    return ref.at[pl.ds(start, n), :]


def _slot_shape(g, w):
    shape = list(g.shape)
    shape[SHARD_AXES[w] - 1] //= N_CHIPS
    return (N_DEV - 1,) + tuple(shape)


def _scatter_copies(g_ref, land_ref, send_sems, recv_sems, layer, w):
    x, y, c = _mesh_pos()
    n = g_ref.shape[SHARD_AXES[w] - 1] // N_CHIPS
    out = []
    for r in range(1, N_DEV):
        tx, ty, tc = _flip(x, r & 4), _flip(y, r & 2), _flip(c, r & 1)
        cp = pltpu.make_async_remote_copy(
            src_ref=_grad_shard(g_ref, w, 2 * tx + ty, n), dst_ref=land_ref.at[r - 1], send_sem=send_sems.at[r - 1],
            recv_sem=recv_sems.at[r - 1], device_id=(tx, ty, tc), device_id_type=MESH)
        out.append((cp, (c != layer) if r & 1 else (c == layer)))
    return out


def _scatter_start(g, land, layer, w, name):
    def body(g_ref, land_ref, send_sems, recv_sems, g_thru, land_thru, token):
        for cp, mine in _scatter_copies(g_ref, land_ref, send_sems, recv_sems, layer, w):
            @pl.when(mine)
            def _():
                cp.start()
        token[...] = jnp.zeros_like(token)

    return pl.pallas_call(
        body, name=name,
        out_shape=(pltpu.SemaphoreType.DMA((N_DEV - 1,)), pltpu.SemaphoreType.DMA((N_DEV - 1,)),
                   pltpu.HBM(g.shape, g.dtype), pltpu.HBM(land.shape, land.dtype),
                   jax.ShapeDtypeStruct((HALO, 128), F32)),
        in_specs=(HBM, HBM), out_specs=(SEM, SEM, HBM, HBM, pl.BlockSpec(memory_space=pltpu.VMEM)),
        input_output_aliases={0: 2, 1: 3}, compiler_params=pltpu.CompilerParams(has_side_effects=EFFECT),
    )(pltpu.with_memory_space_constraint(g, pltpu.HBM), pltpu.with_memory_space_constraint(land, pltpu.HBM))


def _scatter_wait(started, land, after, w, name):
    def body(g0_ref, g1_ref, land_ref, ss0, rs0, ss1, rs1, after_ref, g0_out, g1_out, land_out):
        c = lax.axis_index("c")
        for layer, g_ref, ss, rs in ((0, g0_ref, ss0, rs0), (1, g1_ref, ss1, rs1)):
            for cp, mine in _scatter_copies(g_ref, land_ref, ss, rs, layer, w):
                @pl.when(mine)
                def _():
                    cp.wait_send()

                @pl.when(c == layer)
                def _():
                    cp.wait_recv()

    (ss0, rs0, g0), (ss1, rs1, g1) = started
    return pl.pallas_call(
        body, name=name,
        out_shape=(pltpu.HBM(g0.shape, g0.dtype), pltpu.HBM(g1.shape, g1.dtype), pltpu.HBM(land.shape, land.dtype)),
        in_specs=(HBM, HBM, HBM, SEM, SEM, SEM, SEM, ANY), out_specs=(HBM, HBM, HBM),
        input_output_aliases={0: 0, 1: 1, 2: 2}, compiler_params=pltpu.CompilerParams(has_side_effects=EFFECT),
    )(g0, g1, land, ss0, rs0, ss1, rs1, after)


def _sum_slots(g0, g1, slots, w, pos_arr, name):
    _, rows, cols = slots.shape
    tr = min(256, rows)
    nr = rows // tr
    if SHARD_AXES[w] == 2:
        own = pl.BlockSpec((tr, cols), lambda i, pos: (i, pos[0]))
    else:
        own = pl.BlockSpec((tr, cols), lambda i, pos: (pos[0] * nr + i, 0))

    def body(pos_ref, own0_ref, own1_ref, s_ref, o_ref):
        acc = jnp.where(pos_ref[1] == 0, own0_ref[...], own1_ref[...]).astype(F32)
        for r in range(N_DEV - 1):
            acc = acc + s_ref[r].astype(F32)
        o_ref[...] = acc

    return pl.pallas_call(
        body, name=name,
        grid_spec=pltpu.PrefetchScalarGridSpec(
            num_scalar_prefetch=1, grid=(nr,),
            in_specs=[own, own, pl.BlockSpec((N_DEV - 1, tr, cols), lambda i, pos: (0, i, 0))],
            out_specs=pl.BlockSpec((tr, cols), lambda i, pos: (i, 0))),
        out_shape=jax.ShapeDtypeStruct((rows, cols), F32), compiler_params=_cparams("parallel"),
    )(pos_arr, g0, g1, slots)


def _swap_layers(halves, name):
    n = len(halves)

    def body(*refs):
        srcs, dsts = refs[:n], refs[n:2 * n]
        send_sems, recv_sems = refs[2 * n:]
        x, y, c = _mesh_pos()
        sends = [pltpu.make_async_remote_copy(src_ref=srcs[w], dst_ref=dsts[w], send_sem=send_sems.at[w],
                                              recv_sem=recv_sems.at[w], device_id=(x, y, 1 - c), device_id_type=MESH)
                 for w in range(n)]
        for cp in sends:
            cp.start()
        for cp in sends:
            cp.wait_recv()
        for cp in sends:
            cp.wait_send()

    return pl.pallas_call(
        body, name=name, in_specs=[ANY] * n, out_specs=[ANY] * n,
        out_shape=[jax.ShapeDtypeStruct(h.shape, h.dtype) for h in halves],
        scratch_shapes=[pltpu.SemaphoreType.DMA((n,)), pltpu.SemaphoreType.DMA((n,))],
    )(*halves)


def _adamw_math(w, g, m, v):
    m = ADAM_B1 * m + (1.0 - ADAM_B1) * g
    v = ADAM_B2 * v + (1.0 - ADAM_B2) * jnp.square(g)
    m_hat = m / (1.0 - ADAM_B1 ** ADAM_STEP)
    v_hat = v / (1.0 - ADAM_B2 ** ADAM_STEP)
    delta = -ADAM_LR * (m_hat / (jnp.sqrt(v_hat) + ADAM_EPS) + ADAM_WD * w)
    return delta, m, v


def _adamw(w, g_own, g_other, m, v, pos_arr, name):
    shape = w.shape
    _, rows, cols = shape
    tr = min(256, rows)

    def body(pos_ref, w_ref, own_ref, other_ref, m_ref, v_ref, g_ref, d_ref, m2_ref, v2_ref):
        g = jnp.where(pl.program_id(0) == pos_ref[1], own_ref[...], other_ref[...])
        g_ref[...] = g
        d_ref[...], m2_ref[...], v2_ref[...] = _adamw_math(w_ref[...], g, m_ref[...], v_ref[...])

    full = pl.BlockSpec((None, tr, cols), lambda l, i, pos: (l, i, 0))
    half = pl.BlockSpec((tr, cols), lambda l, i, pos: (i, 0))
    return pl.pallas_call(
        body, name=name,
        grid_spec=pltpu.PrefetchScalarGridSpec(
            num_scalar_prefetch=1, grid=(2, rows // tr),
            in_specs=[full, half, half, full, full], out_specs=[full] * 4),
        out_shape=[jax.ShapeDtypeStruct(shape, F32)] * 4, compiler_params=_cparams("parallel", "parallel"),
    )(pos_arr, w, g_own, g_other, m, v)


def _small_sync(part, w, m, v):
    rows, cols = part.shape

    def body(p_ref, w_ref, m_ref, v_ref, g_ref, d_ref, m2_ref, v2_ref, slots, send_sems, recv_sems):
        x, y, c = _mesh_pos()
        me = 4 * x + 2 * y + c
        slots[me] = p_ref[...]
        sends = []
        for r in range(1, N_DEV):
            to = (_flip(x, r & 4), _flip(y, r & 2), _flip(c, r & 1))
            sends.append(pltpu.make_async_remote_copy(
                src_ref=p_ref, dst_ref=slots.at[me], send_sem=send_sems.at[r - 1], recv_sem=recv_sems.at[r - 1],
                device_id=to, device_id_type=MESH))
        for cp in sends:
            cp.start()
        for cp in sends:
            cp.wait_recv()
        for cp in sends:
            cp.wait_send()
        g = slots[0]
        for i in range(1, N_DEV):
            g = g + slots[i]
        g_ref[...] = g
        d_ref[...], m2_ref[...], v2_ref[...] = _adamw_math(w_ref[...], g, m_ref[...], v_ref[...])

    vm = pl.BlockSpec(memory_space=pltpu.VMEM)
    return pl.pallas_call(
        body, name="small_sync", in_specs=[vm] * 4, out_specs=[vm] * 4,
        out_shape=[jax.ShapeDtypeStruct((rows, cols), F32)] * 4,
        scratch_shapes=[pltpu.VMEM((N_DEV, rows, cols), F32), pltpu.SemaphoreType.DMA((N_DEV - 1,)),
                        pltpu.SemaphoreType.DMA((N_DEV - 1,))],
    )(part, w, m, v)


def _pack_small(d, g_mix, g_group, g_mlp, g_final, conv_full, sinks, scalar):
    def part(rows):
        return jnp.pad(rows, ((0, HALO - rows.shape[0]), (0, d - rows.shape[1])))
    return jnp.concatenate([part(g_mix), part(g_group), part(g_mlp), part(g_final[None]),
                            part(conv_full.reshape(6, CONV_CH)), part(sinks.reshape(2, N_HEADS)),
                            part(scalar.reshape(1, 1))], axis=0)


def _unpack_small(p, dm):
    return (p[0:2, :dm], p[8:10, :MIX_WIDTH], p[16:18, :dm], p[24, :dm], p[32:38, :CONV_CH].reshape(2, 3, CONV_CH),
            p[40:42, :N_HEADS].reshape(2, 2, C_GROUP), p[48, 0])


def kernel(x, w_in, conv_w, sinks, g_mix, g_group, w_o, g_mlp, w_ff_in, w_ff_out, g_final, loss_target, m_w_in, m_conv_w, m_sinks, m_g_mix, m_g_group, m_w_o, m_g_mlp, m_w_ff_in, m_w_ff_out, m_g_final, v_w_in, v_conv_w, v_sinks, v_g_mix, v_g_group, v_w_o, v_g_mlp, v_w_ff_in, v_w_ff_out, v_g_final):
    d = max(x.shape[2], MIX_WIDTH)
    chip = 2 * lax.axis_index("x") + lax.axis_index("y")
    conv_n = conv_w.shape[2]

    pos_arr = jnp.stack([chip, lax.axis_index("c")]).astype(jnp.int32)
    shards = (w_in, w_o, w_ff_in, w_ff_out)
    conv_tile = jnp.pad(conv_w.reshape(6, conv_n), ((0, HALO - 6), (0, 128 - conv_n)))
    placed = [_place_shard(w_in, pos_arr[:1], "place_shard_0"), None, None, None]
    sems_a, placed, conv_thru = _gather_start(
        GATHER_STARTS[0], placed, (conv_tile, lax.empty((N_CHIPS,) + conv_tile.shape, conv_tile.dtype)),
        "gather_start_0")
    for i in range(1, N_BIG):
        placed[i] = _place_shard(shards[i], pos_arr[:1], "place_shard_%d" % i)
    full = {"arrs": placed, "conv": None, "sems": list(sems_a[:2])}

    def fetch(stage, layer, after):
        k = 2 * layer + stage
        sems = full["sems"][2 * k:2 * k + 2]
        if k == 0:
            full["arrs"], land = _gather_wait(0, sems, full["arrs"], (sems_a[-2:], *conv_thru), after, "gather_wait_0")
            conv_all = lax.dynamic_update_slice(land, conv_tile[None], (chip, 0, 0))
            full["conv"] = conv_all[:, :6, :conv_n].reshape(N_CHIPS, 2, 3, conv_n).transpose(1, 2, 0, 3).reshape(
                2, 3, CONV_CH)
            sems_b, full["arrs"], rest = _gather_start(GATHER_STARTS[1], full["arrs"], None, "gather_start_1",
                                                       through=full["arrs"][0])
            full["arrs"][0] = rest[-1]
            full["sems"] += list(sems_b)
        else:
            full["arrs"], _ = _gather_wait(k, sems, full["arrs"], None, after, "gather_wait_%d" % k)
        if k == 1:
            sems_c, full["arrs"], _ = _gather_start(GATHER_STARTS[2], full["arrs"], None, "gather_start_2")
            full["sems"] += list(sems_c)
        return (*full["arrs"], full["conv"])

    lands, started = [None] * N_BIG, {}

    def emit(layer, w, g):
        if lands[w] is None:
            lands[w] = lax.empty(_slot_shape(g, w), g.dtype)
        *started[layer, w], lands[w], token = _scatter_start(g, lands[w], layer, w, "scatter_start_%d_%d" % (layer, w))
        return token[0, 0]

    loss_tile, dx, grads, dg_final = _local_step(_to_strips(x[0], placed[0], "to_strips_x"),
                                                 _to_strips(loss_target[0], placed[0], "to_strips_target"), fetch,
                                                 w_ff_in.shape[2] * N_CHIPS,
                                                 sinks, g_mix, g_group, g_mlp, g_final, emit)

    wmv = ((w_in, m_w_in, v_w_in), (w_o, m_w_o, v_w_o), (w_ff_in, m_w_ff_in, v_w_ff_in),
           (w_ff_out, m_w_ff_out, v_w_ff_out))
    big, after = [None] * N_BIG, dx
    for name, ws in (("swap_layers_rest", (1, 2, 3)), ("swap_layers_in", (0,))):
        own = []
        for w in ws:
            g0, g1, slots = _scatter_wait((started[0, w], started[1, w]), lands[w], after, w, "scatter_wait_%d" % w)
            own.append(_sum_slots(g0, g1, slots, w, pos_arr, "sum_slots_%d" % w))
        for w, mine, theirs in zip(ws, own, _swap_layers(own, name)):
            big[w] = _adamw(wmv[w][0], mine, theirs, wmv[w][1], wmv[w][2], pos_arr, "adamw_%d" % w)
        after = big[ws[-1]][1]

    def both(i):
        return jnp.stack([grads[0][i][0], grads[1][i][0]])
    dconv = jnp.stack([grads[0][0][:3], grads[1][0][:3]])
    dsinks = jnp.stack([grads[0][1][0, ::HEAD_DIM], grads[1][1][0, ::HEAD_DIM]])
    part = _pack_small(d, both(2), both(3), both(4), dg_final[0], dconv, dsinks, loss_tile[0, 0])

    def spread(shard):
        return lax.dynamic_update_slice(jnp.zeros((2, 3, CONV_CH), F32), shard, (0, 0, chip * conv_n))
    zero = jnp.zeros((), F32)
    packs = [_pack_small(d, a, b, c_, e, spread(f), g_, zero) for a, b, c_, e, f, g_ in (
        (g_mix, g_group, g_mlp, g_final, conv_w, sinks),
        (m_g_mix, m_g_group, m_g_mlp, m_g_final, m_conv_w, m_sinks),
        (v_g_mix, v_g_group, v_g_mlp, v_g_final, v_conv_w, v_sinks))]
    small = [_unpack_small(p, x.shape[2]) for p in _small_sync(part, *packs)]

    def shard_of(full):
        return lax.dynamic_slice(full, (0, 0, chip * conv_n), (2, 3, conv_n))
    small = [(s[0], s[1], s[2], s[3], shard_of(s[4]), s[5], s[6]) for s in small]
    loss = small[0][6]

    def ordered(kind):
        b = [big[i][kind] for i in range(N_BIG)]
        s = small[kind]
        return [b[0], s[4], s[5], s[0], s[1], b[1], s[2], b[2], b[3], s[3]]

    return (loss, _from_strips(dx, "from_strips_dx")[None], *ordered(0), *ordered(1), *ordered(2), *ordered(3))
```

```python
import functools

import jax
import jax.numpy as jnp
from jax import lax
from jax.experimental import pallas as pl
from jax.experimental.pallas import tpu as pltpu

HEAD_DIM = 64
N_HEADS = 6
C_GROUP = 3
A_WIDTH = N_HEADS * HEAD_DIM
C_KV_WIDTH = 2 * HEAD_DIM
CONV_CH = 256
ZA_W = 3 * A_WIDTH
ZB_W = 3 * CONV_CH
ZC_W = A_WIDTH + 2 * C_KV_WIDTH
IN_WIDTH = ZA_W + ZB_W + ZC_W
MIX_WIDTH = A_WIDTH + CONV_CH + A_WIDTH
DILATIONS = (1, 4, 16)
A_MAX_DIST = 128
C_MAX_DIST = 127
TQ = 128
EPS = 1e-6
SCALE = HEAD_DIM ** -0.5
NEG = -1e30
HALO = 8

ADAM_LR = 0.001
ADAM_B1 = 0.9
ADAM_B2 = 0.999
ADAM_EPS = 1e-08
ADAM_WD = 0.01
ADAM_STEP = 10

BF = jnp.bfloat16
F32 = jnp.float32
MESH = pl.DeviceIdType.MESH
VMEM_LIMIT = 56 * 1024 * 1024


def _cparams(*sem):
    return pltpu.CompilerParams(dimension_semantics=sem, vmem_limit_bytes=VMEM_LIMIT)


def _nt(a, b):
    return lax.dot_general(a, b, (((1,), (1,)), ((), ())), preferred_element_type=F32)


def _tn(a, b):
    return lax.dot_general(a, b, (((0,), (0,)), ((), ())), preferred_element_type=F32)


def _nn(a, b):
    return jnp.dot(a, b, preferred_element_type=F32)


def _rows(tb, w):
    return pl.BlockSpec((tb, w), lambda i: (i, 0))


def _whole(shape):
    return pl.BlockSpec(shape, lambda *_: (0,) * len(shape))


def _layer(shape, l):
    return pl.BlockSpec((None,) + shape, lambda *_: (l,) + (0,) * len(shape))


def _rms_scale(v):
    return lax.rsqrt(jnp.mean(v * v, axis=-1, keepdims=True) + EPS)


def _norm_bwd(dxhat, xhat, r):
    return r * (dxhat - xhat * jnp.mean(dxhat * xhat, axis=-1, keepdims=True))


def _qkv_fwd(x, g, w_all, l, tb):
    s, d = x.shape

    def body(x_ref, g_ref, w_ref, h_ref, za_ref, zb_ref, zc_ref):
        xv = x_ref[...]
        h = ((xv * _rms_scale(xv)) * g_ref[...]).astype(BF)
        h_ref[...] = h
        z = jnp.concatenate([_nn(h, w_ref[k]) for k in range(N_CHIPS)], axis=1)
        za_ref[...] = z[:, :ZA_W]
        zb_ref[...] = z[:, ZA_W:ZA_W + ZB_W]
        zc_ref[...] = z[:, ZA_W + ZB_W:]

    return pl.pallas_call(
        body, grid=(s // tb,), name="qkv_fwd",
        in_specs=[_rows(tb, d), _whole((1, d)), _layer((N_CHIPS, d, IN_WIDTH // N_CHIPS), l)],
        out_specs=[_rows(tb, d), _rows(tb, ZA_W), _rows(tb, ZB_W), _rows(tb, ZC_W)],
        out_shape=[jax.ShapeDtypeStruct((s, d), BF), jax.ShapeDtypeStruct((s, ZA_W), F32),
                   jax.ShapeDtypeStruct((s, ZB_W), F32), jax.ShapeDtypeStruct((s, ZC_W), F32)],
        compiler_params=_cparams("parallel"),
    )(x, g, w_all)


N_STRIPS = 16


def _strips(a):
    s, w = a.shape
    return a.reshape(4, 4, s // N_STRIPS, w)


def _p_grid(s, dil):
    na = s // N_STRIPS
    return {16: (4, 4, na // TQ), 4: (4, na // 32), 1: (na // 8,)}[dil]


def _p_spec(dil, cw, col, prev=False):
    def blk(j):
        return jnp.maximum(j - 1, 0) if prev else j
    if dil == 16:
        return pl.BlockSpec((None, None, TQ, cw), lambda f, e, j: (f, e, blk(j), col))
    if dil == 4:
        return pl.BlockSpec((None, 4, 32, cw), lambda f, j: (f, 0, blk(j), col))
    return pl.BlockSpec((4, 4, 8, cw), lambda j: (0, 0, blk(j), col))


def _block_pos(i, dil):
    if dil == 16:
        return i
    if dil == 4:
        return 4 * (i % 32) + i // 32
    return 16 * (i % 8) + 4 * ((i // 8) % 4) + i // 32


def _band_mask(b, dil, max_dist):
    qi = _block_pos(lax.broadcasted_iota(jnp.int32, (TQ, 2 * TQ), 0), dil)
    col = lax.broadcasted_iota(jnp.int32, (TQ, 2 * TQ), 1)
    cur = col >= TQ
    dist = qi - _block_pos(col % TQ, dil) + jnp.where(cur, 0, TQ)
    return (dist >= 0) & (dist <= max_dist) & (cur | (b > 0))


def _hs(h):
    return slice(h * HEAD_DIM, (h + 1) * HEAD_DIM)


def _ld(ref, cols):
    v = ref[..., cols]
    return v.reshape(TQ, v.shape[-1])


def _st(ref, cols, val):
    ref[..., cols] = val.reshape(ref.shape[:-1] + (val.shape[-1],))


def _attn_fwd(z, dil, kw, kcol, vcol, n_rep, max_dist, name):
    s, zw = z.shape
    grid = _p_grid(s, dil)

    def body(q_ref, kp_ref, kc_ref, vp_ref, vc_ref, acc_ref, m_ref, l_ref):
        mask = _band_mask(pl.program_id(len(grid) - 1), dil, max_dist)
        for kh in range(N_HEADS // n_rep):
            k2 = jnp.concatenate([_ld(kp_ref, _hs(kh)), _ld(kc_ref, _hs(kh))], axis=0).astype(BF)
            v2 = jnp.concatenate([_ld(vp_ref, _hs(kh)), _ld(vc_ref, _hs(kh))], axis=0).astype(BF)
            for h in range(kh * n_rep, (kh + 1) * n_rep):
                q = _ld(q_ref, _hs(h)).astype(BF)
                sc = jnp.where(mask, _nt(q, k2) * SCALE, NEG)
                m = jnp.max(sc, axis=1, keepdims=True)
                p = jnp.exp(sc - m)
                _st(acc_ref, _hs(h), _nn(p.astype(BF), v2))
                _st(m_ref, _hs(h), jnp.broadcast_to(m, (TQ, HEAD_DIM)))
                _st(l_ref, _hs(h), jnp.broadcast_to(jnp.sum(p, axis=1, keepdims=True), (TQ, HEAD_DIM)))

    res = pl.pallas_call(
        body, grid=grid, name=name,
        in_specs=[_p_spec(dil, A_WIDTH, 0), _p_spec(dil, kw, kcol, True), _p_spec(dil, kw, kcol),
                  _p_spec(dil, kw, vcol, True), _p_spec(dil, kw, vcol)],
        out_specs=[_p_spec(dil, A_WIDTH, 0)] * 3,
        out_shape=[jax.ShapeDtypeStruct((4, 4, s // N_STRIPS, A_WIDTH), F32)] * 3,
        compiler_params=_cparams(*(("parallel",) * len(grid))),
    )(*[_strips(z)] * 5)
    return [a.reshape(s, A_WIDTH) for a in res]


def _attn_merge(parts_a, part_c, sink_row, tb):
    s = part_c[0].shape[0]
    n_a = len(parts_a)

    def body(*refs):
        ins, sink_ref = refs[:3 * n_a + 3], refs[3 * n_a + 3]
        ya_ref, lsea_ref, yc_ref, lsec_ref = refs[3 * n_a + 4:]
        ms = [ins[3 * p + 1][...] for p in range(n_a)]
        m = functools.reduce(jnp.maximum, ms)
        acc = jnp.zeros_like(m)
        l = jnp.zeros_like(m)
        for p in range(n_a):
            w = jnp.exp(ms[p] - m)
            acc = acc + w * ins[3 * p][...]
            l = l + w * ins[3 * p + 2][...]
        ya_ref[...] = acc / l
        lsea_ref[...] = m + jnp.log(l)
        acc_c, m_c, l_c = [r[...] for r in ins[3 * n_a:]]
        sk = sink_ref[...]
        m2 = jnp.maximum(m_c, sk)
        w = jnp.exp(m_c - m2)
        l2 = w * l_c + jnp.exp(sk - m2)
        yc_ref[...] = (w * acc_c) / l2
        lsec_ref[...] = m2 + jnp.log(l2)

    return pl.pallas_call(
        body, grid=(s // tb,), name="attn_merge",
        in_specs=[_rows(tb, A_WIDTH)] * (3 * n_a + 3) + [_whole((1, A_WIDTH))],
        out_specs=[_rows(tb, A_WIDTH)] * 4, out_shape=[jax.ShapeDtypeStruct((s, A_WIDTH), F32)] * 4,
        compiler_params=_cparams("parallel"),
    )(*[a for part in parts_a + [part_c] for a in part], sink_row)


def _shift_down(v, n, halo):
    rows = v.shape[0]
    out = pltpu.roll(v, n, 0)
    row = lax.broadcasted_iota(jnp.int32, v.shape, 0)
    for t in range(n):
        out = jnp.where(row == t, halo[HALO - n + t:HALO - n + t + 1, :], out)
    return out


def _shift_up(v, n, halo):
    rows = v.shape[0]
    out = pltpu.roll(v, rows - n, 0)
    row = lax.broadcasted_iota(jnp.int32, v.shape, 0)
    for t in range(n):
        out = jnp.where(row == rows - n + t, halo[t:t + 1, :], out)
    return out


def _strip(v, b):
    return v[b % 4, b // 4]


def _conv_strips(zb, prev, cw):
    gb = [_strip(zb, b)[:, :CONV_CH] for b in range(N_STRIPS)]
    gc = [_strip(zb, b)[:, CONV_CH:2 * CONV_CH] for b in range(N_STRIPS)]
    xb = [_strip(zb, b)[:, 2 * CONV_CH:] for b in range(N_STRIPS)]
    u = [g * v for g, v in zip(gc, xb)]
    uh = prev[:, :, CONV_CH:2 * CONV_CH] * prev[:, :, 2 * CONV_CH:]
    wrapped = {14: _shift_down(u[14], 1, uh[2]), 15: _shift_down(u[15], 1, uh[3])}
    u1 = [u[b - 1] if b >= 1 else wrapped[15] for b in range(N_STRIPS)]
    u2 = [u[b - 2] if b >= 2 else wrapped[14 + b] for b in range(N_STRIPS)]
    c = [cw[0:1, :] * u2[b] + cw[1:2, :] * u1[b] + cw[2:3, :] * u[b] for b in range(N_STRIPS)]
    return gb, gc, xb, u, u1, u2, c


def _strip_rows(ta, w):
    return pl.BlockSpec((4, 4, ta, w), lambda i: (0, 0, i, 0))


def _prev_rows(ta, w):
    return pl.BlockSpec((4, None, HALO, w), lambda i: (0, 3, jnp.maximum(i * (ta // HALO) - 1, 0), 0))


def _next_rows(ta, w, nblk):
    return pl.BlockSpec((4, None, HALO, w),
                        lambda i: (0, 0, jnp.minimum((i + 1) * (ta // HALO), nblk * (ta // HALO) - 1), 0))


def _mix_fwd(x, ya, yc, zb, cw, gg, wo_all, l, tb):
    s, d = x.shape
    ta = tb // N_STRIPS

    def body(x_ref, ya_ref, yc_ref, zb_ref, zbp_ref, cw_ref, gg_ref, wo_ref, x1_ref, yb_ref):
        i = pl.program_id(0)
        prev = jnp.where(i > 0, zbp_ref[...], 0.0)
        gb, _, _, _, _, _, c = _conv_strips(zb_ref[...], prev, cw_ref[...])
        for b in range(N_STRIPS):
            yb_ref[b % 4, b // 4] = gb[b] * c[b]
        yb = yb_ref[...].reshape(tb, CONV_CH)
        ya, yc = ya_ref[...].reshape(tb, A_WIDTH), yc_ref[...].reshape(tb, A_WIDTH)
        n = jnp.concatenate([ya * _rms_scale(ya), yb * _rms_scale(yb), yc * _rms_scale(yc)], axis=1)
        n = (n * gg_ref[...]).astype(BF)
        x1 = x_ref[...].reshape(tb, d) + _nn(n, wo_ref[...].reshape(MIX_WIDTH, d))
        x1_ref[...] = x1.reshape(4, 4, ta, d)

    res = pl.pallas_call(
        body, grid=(s // tb,), name="mix_fwd",
        in_specs=[_strip_rows(ta, d), _strip_rows(ta, A_WIDTH), _strip_rows(ta, A_WIDTH), _strip_rows(ta, ZB_W),
                  _prev_rows(ta, ZB_W), _whole((HALO, CONV_CH)), _whole((1, MIX_WIDTH)),
                  _layer((N_CHIPS, MIX_WIDTH // N_CHIPS, d), l)],
        out_specs=[_strip_rows(ta, d), _strip_rows(ta, CONV_CH)],
        out_shape=[jax.ShapeDtypeStruct((4, 4, s // N_STRIPS, d), F32),
                   jax.ShapeDtypeStruct((4, 4, s // N_STRIPS, CONV_CH), F32)],
        compiler_params=_cparams("parallel"),
    )(_strips(x), _strips(ya), _strips(yc), _strips(zb), _strips(zb), cw, gg, wo_all)
    return res[0].reshape(s, d), res[1].reshape(s, CONV_CH)


def _mlp_fwd(x1, g, w1_all, w2_all, l, tb, tf):
    s, d = x1.shape
    ff = w1_all.shape[1] * w1_all.shape[3]
    nj = ff // tf

    def body(x_ref, g_ref, w1_ref, w2_ref, x2_ref, h2_ref, ap_ref, acc):
        j = pl.program_id(1)

        @pl.when(j == 0)
        def _():
            xv = x_ref[...]
            h2_ref[...] = ((xv * _rms_scale(xv)) * g_ref[...]).astype(BF)
            acc[...] = jnp.zeros_like(acc)

        ap = _nn(h2_ref[...], w1_ref[...])
        ap_ref[...] = ap.astype(BF)
        a = jnp.square(jnp.maximum(ap, 0.0)).astype(BF)
        acc[...] += _nn(a, w2_ref[...])

        @pl.when(j == nj - 1)
        def _():
            x2_ref[...] = x_ref[...] + acc[...]

    return pl.pallas_call(
        body, grid=(s // tb, nj), name="mlp_fwd",
        in_specs=[pl.BlockSpec((tb, d), lambda i, j: (i, 0)), _whole((1, d)),
                  pl.BlockSpec((None, None, d, tf), lambda i, j: (l, j, 0, 0)),
                  pl.BlockSpec((None, None, tf, d), lambda i, j: (l, j, 0, 0))],
        out_specs=[pl.BlockSpec((tb, d), lambda i, j: (i, 0)), pl.BlockSpec((tb, d), lambda i, j: (i, 0)),
                   pl.BlockSpec((tb, tf), lambda i, j: (i, j))],
        out_shape=[jax.ShapeDtypeStruct((s, d), F32), jax.ShapeDtypeStruct((s, d), BF),
                   jax.ShapeDtypeStruct((s, ff), BF)],
        scratch_shapes=[pltpu.VMEM((tb, d), F32)],
        compiler_params=_cparams("parallel", "arbitrary"),
    )(x1, g, w1_all, w2_all)


def _loss_head(x, g, tgt, tb):
    s, d = x.shape

    def body(x_ref, g_ref, t_ref, dx_ref, loss_ref, dg_ref):
        i = pl.program_id(0)

        @pl.when(i == 0)
        def _():
            loss_ref[...] = jnp.zeros_like(loss_ref)
            dg_ref[...] = jnp.zeros_like(dg_ref)

        xv = x_ref[...]
        r = _rms_scale(xv)
        xhat = xv * r
        err = xhat * g_ref[...] - t_ref[...]
        part = jnp.sum(jnp.mean(jnp.square(err), axis=-1, keepdims=True), axis=0, keepdims=True)
        loss_ref[...] += 0.5 * part
        dy = err * (1.0 / d)
        dg_ref[...] += jnp.sum(dy * xhat, axis=0, keepdims=True)
        dx_ref[...] = _norm_bwd(dy * g_ref[...], xhat, r)

    return pl.pallas_call(
        body, grid=(s // tb,), name="loss_head",
        in_specs=[_rows(tb, d), _whole((1, d)), _rows(tb, d)],
        out_specs=[_rows(tb, d), _whole((HALO, 128)), _whole((HALO, d))],
        out_shape=[jax.ShapeDtypeStruct((s, d), F32), jax.ShapeDtypeStruct((HALO, 128), F32),
                   jax.ShapeDtypeStruct((HALO, d), F32)],
        compiler_params=_cparams("arbitrary"),
    )(x, g, tgt)


def _mlp_bwd(dx2, x1, ap, g, w1_all, w2_all, l, tb, tf):
    s, d = x1.shape
    ff = ap.shape[1]
    nj = ff // tf

    def body(dx2_ref, x1_ref, ap_ref, g_ref, w1_ref, w2_ref, dx1_ref, dap_ref, dg_ref, acc):
        i, j = pl.program_id(0), pl.program_id(1)

        @pl.when((i == 0) & (j == 0))
        def _():
            dg_ref[...] = jnp.zeros_like(dg_ref)

        @pl.when(j == 0)
        def _():
            acc[...] = jnp.zeros_like(acc)

        da = _nt(dx2_ref[...].astype(BF), w2_ref[...])
        dap = (da * (2.0 * jnp.maximum(ap_ref[...].astype(F32), 0.0))).astype(BF)
        dap_ref[...] = dap
        acc[...] += _nt(dap, w1_ref[...])

        @pl.when(j == nj - 1)
        def _():
            xv = x1_ref[...]
            r = _rms_scale(xv)
            xhat = xv * r
            dh = acc[...]
            dg_ref[...] += jnp.sum(dh * xhat, axis=0, keepdims=True)
            dx1_ref[...] = dx2_ref[...] + _norm_bwd(dh * g_ref[...], xhat, r)

    return pl.pallas_call(
        body, grid=(s // tb, nj), name="mlp_bwd",
        in_specs=[pl.BlockSpec((tb, d), lambda i, j: (i, 0)), pl.BlockSpec((tb, d), lambda i, j: (i, 0)),
                  pl.BlockSpec((tb, tf), lambda i, j: (i, j)),
                  _whole((1, d)), pl.BlockSpec((None, None, d, tf), lambda i, j: (l, j, 0, 0)),
                  pl.BlockSpec((None, None, tf, d), lambda i, j: (l, j, 0, 0))],
        out_specs=[pl.BlockSpec((tb, d), lambda i, j: (i, 0)), pl.BlockSpec((tb, tf), lambda i, j: (i, j)),
                   _whole((HALO, d))],
        out_shape=[jax.ShapeDtypeStruct((s, d), F32), jax.ShapeDtypeStruct((s, ff), BF),
                   jax.ShapeDtypeStruct((HALO, d), F32)],
        scratch_shapes=[pltpu.VMEM((tb, d), F32)],
        compiler_params=_cparams("arbitrary", "arbitrary"),
    )(dx2, x1, ap, g, w1_all, w2_all)


def _wgrad(a, b, tm, tn, ts, name, relu2=False):
    s, m = a.shape
    n = b.shape[1]
    ns = s // ts

    def body(a_ref, b_ref, o_ref, acc):
        k = pl.program_id(2)

        @pl.when(k == 0)
        def _():
            acc[...] = jnp.zeros_like(acc)

        av = a_ref[...]
        if relu2:
            av = jnp.square(jnp.maximum(av.astype(F32), 0.0)).astype(BF)
        acc[...] += _tn(av, b_ref[...].astype(BF))

        @pl.when(k == ns - 1)
        def _():
            o_ref[...] = acc[...].astype(BF)

    return pl.pallas_call(
        body, grid=(m // tm, n // tn, ns), name=name,
        in_specs=[pl.BlockSpec((ts, tm), lambda i, j, k: (k, i)), pl.BlockSpec((ts, tn), lambda i, j, k: (k, j))],
        out_specs=pl.BlockSpec((tm, tn), lambda i, j, k: (i, j)),
        out_shape=jax.ShapeDtypeStruct((m, n), BF),
        scratch_shapes=[pltpu.VMEM((tm, tn), F32)],
        compiler_params=_cparams("parallel", "parallel", "arbitrary"),
    )(a, b)


def _mix_bwd(dx1, ya, yb, yc, lse_c, sink_row, gg, wo_all, l, tb):
    s, d = dx1.shape

    def body(dx_ref, ya_ref, yb_ref, yc_ref, lse_ref, sink_ref, gg_ref, wo_ref,
             n_ref, dya_ref, dyc_ref, da_ref, dc_ref, dyb_ref, dg_ref, dsink_ref):
        i = pl.program_id(0)

        @pl.when(i == 0)
        def _():
            dg_ref[...] = jnp.zeros_like(dg_ref)
            dsink_ref[...] = jnp.zeros_like(dsink_ref)

        dn = _nt(dx_ref[...].astype(BF), wo_ref[...].reshape(MIX_WIDTH, d))
        ys = [ya_ref[...], yb_ref[...], yc_ref[...]]
        rs = [_rms_scale(v) for v in ys]
        nhat = jnp.concatenate([v * r for v, r in zip(ys, rs)], axis=1)
        gg = gg_ref[...]
        n_ref[...] = (nhat * gg).astype(BF)
        dg_ref[...] += jnp.sum(dn * nhat, axis=0, keepdims=True)
        dnh = dn * gg
        bounds = [(0, A_WIDTH), (A_WIDTH, A_WIDTH + CONV_CH), (A_WIDTH + CONV_CH, MIX_WIDTH)]
        dys = [_norm_bwd(dnh[:, lo:hi], nhat[:, lo:hi], r) for (lo, hi), r in zip(bounds, rs)]
        dyb_ref[...] = dys[1]
        for dy, y, dy_ref, dd_ref in ((dys[0], ys[0], dya_ref, da_ref), (dys[2], ys[2], dyc_ref, dc_ref)):
            dy_ref[...] = dy
            t = dy * y
            for h in range(N_HEADS):
                dd_ref[:, _hs(h)] = jnp.broadcast_to(jnp.sum(t[:, _hs(h)], axis=1, keepdims=True), (tb, HEAD_DIM))
        dsink_ref[...] -= jnp.sum(jnp.exp(sink_ref[...] - lse_ref[...]) * dc_ref[...], axis=0, keepdims=True)

    return pl.pallas_call(
        body, grid=(s // tb,), name="mix_bwd",
        in_specs=[_rows(tb, d), _rows(tb, A_WIDTH), _rows(tb, CONV_CH), _rows(tb, A_WIDTH), _rows(tb, A_WIDTH),
                  _whole((1, A_WIDTH)), _whole((1, MIX_WIDTH)), _layer((N_CHIPS, MIX_WIDTH // N_CHIPS, d), l)],
        out_specs=[_rows(tb, MIX_WIDTH), _rows(tb, A_WIDTH), _rows(tb, A_WIDTH), _rows(tb, A_WIDTH),
                   _rows(tb, A_WIDTH), _rows(tb, CONV_CH), _whole((HALO, MIX_WIDTH)), _whole((HALO, A_WIDTH))],
        out_shape=[jax.ShapeDtypeStruct((s, MIX_WIDTH), BF), jax.ShapeDtypeStruct((s, A_WIDTH), F32),
                   jax.ShapeDtypeStruct((s, A_WIDTH), F32), jax.ShapeDtypeStruct((s, A_WIDTH), F32),
                   jax.ShapeDtypeStruct((s, A_WIDTH), F32), jax.ShapeDtypeStruct((s, CONV_CH), F32),
                   jax.ShapeDtypeStruct((HALO, MIX_WIDTH), F32), jax.ShapeDtypeStruct((HALO, A_WIDTH), F32)],
        compiler_params=_cparams("arbitrary"),
    )(dx1, ya, yb, yc, lse_c, sink_row, gg, wo_all)


def _attn_bwd(z, dy, lse, dd, dil, kw, kcol, vcol, n_rep, max_dist, name):
    s, zw = z.shape
    grid = _p_grid(s, dil)
    n_kv = N_HEADS // n_rep

    def body(q_ref, kp_ref, kc_ref, vp_ref, vc_ref, dy_ref, lse_ref, dd_ref, dq_ref, dkp_ref, dkc_ref, dvp_ref, dvc_ref):
        mask = _band_mask(pl.program_id(len(grid) - 1), dil, max_dist)
        for kh in range(n_kv):
            k2 = jnp.concatenate([_ld(kp_ref, _hs(kh)), _ld(kc_ref, _hs(kh))], axis=0).astype(BF)
            v2 = jnp.concatenate([_ld(vp_ref, _hs(kh)), _ld(vc_ref, _hs(kh))], axis=0).astype(BF)
            dk2 = jnp.zeros((2 * TQ, HEAD_DIM), F32)
            dv2 = jnp.zeros((2 * TQ, HEAD_DIM), F32)
            for h in range(kh * n_rep, (kh + 1) * n_rep):
                q = _ld(q_ref, _hs(h)).astype(BF)
                lse_h = _ld(lse_ref, slice(h * HEAD_DIM, h * HEAD_DIM + 1))
                dd_h = _ld(dd_ref, slice(h * HEAD_DIM, h * HEAD_DIM + 1))
                dyh = _ld(dy_ref, _hs(h)).astype(BF)
                sc = jnp.where(mask, _nt(q, k2) * SCALE, NEG)
                p = jnp.exp(sc - lse_h)
                dp = _nt(dyh, v2)
                ds = ((p * (dp - dd_h)) * SCALE).astype(BF)
                _st(dq_ref, _hs(h), _nn(ds, k2))
                dk2 = dk2 + _tn(ds, q)
                dv2 = dv2 + _tn(p.astype(BF), dyh)
            _st(dkp_ref, _hs(kh), dk2[:TQ])
            _st(dkc_ref, _hs(kh), dk2[TQ:])
            _st(dvp_ref, _hs(kh), dv2[:TQ])
            _st(dvc_ref, _hs(kh), dv2[TQ:])

    args = [_strips(z)] * 5 + [_strips(a) for a in (dy, lse, dd)]
    in_specs = [_p_spec(dil, A_WIDTH, 0), _p_spec(dil, kw, kcol, True), _p_spec(dil, kw, kcol),
                _p_spec(dil, kw, vcol, True), _p_spec(dil, kw, vcol)] + [_p_spec(dil, A_WIDTH, 0)] * 3
    out_specs = [_p_spec(dil, A_WIDTH, 0)] + [_p_spec(dil, kw, 0)] * 4
    na = s // N_STRIPS
    out_shape = [jax.ShapeDtypeStruct((4, 4, na, A_WIDTH), F32)] + [jax.ShapeDtypeStruct((4, 4, na, kw), F32)] * 4
    res = pl.pallas_call(
        body, grid=grid, name=name, in_specs=in_specs, out_specs=out_specs, out_shape=out_shape,
        compiler_params=_cparams(*(("parallel",) * len(grid))),
    )(*args)
    return [res[0].reshape(s, A_WIDTH)] + [a.reshape(s, kw) for a in res[1:]]


DZ_TA = 16


def _dz_assemble(parts_a, parts_c, dyb, zb, cw):
    s = zb.shape[0]
    na = s // N_STRIPS
    nb = na // DZ_TA

    def ahead(w, k):
        return pl.BlockSpec((4, 4, DZ_TA, w), lambda i: (0, 0, jnp.minimum(i + k, nb - 1), 0))

    args, in_specs = [], []
    for dil, (dq, dkp, dkc, dvp, dvc) in zip(DILATIONS + (1,), parts_a + [parts_c]):
        w = dkp.shape[1]
        here = _strip_rows(DZ_TA, w)
        if dil == 1:
            args += [dq, dkp, dkp, dkc, dvp, dvp, dvc]
            in_specs += [_strip_rows(DZ_TA, A_WIDTH), here, ahead(w, 1), here, here, ahead(w, 1), here]
        else:
            k = 8 * dil // DZ_TA
            args += [dq, dkp, dkc, dvp, dvc]
            in_specs += [_strip_rows(DZ_TA, A_WIDTH), ahead(w, k), here, ahead(w, k), here]
    n_att = len(args)
    args = [_strips(a) for a in args] + [_strips(dyb), _strips(dyb), _strips(zb), _strips(zb), _strips(zb), cw]
    in_specs += [_strip_rows(DZ_TA, CONV_CH), _next_rows(DZ_TA, CONV_CH, nb), _strip_rows(DZ_TA, ZB_W),
                 _prev_rows(DZ_TA, ZB_W), _next_rows(DZ_TA, ZB_W, nb), _whole((HALO, CONV_CH))]

    def body(*refs):
        att = list(refs[:n_att])
        dyb_ref, dybn_ref, zb_ref, zbp_ref, zbn_ref, cw_ref, dz_ref, dcw_ref = refs[n_att:]
        i = pl.program_id(0)

        @pl.when(i == 0)
        def _():
            dcw_ref[...] = jnp.zeros_like(dcw_ref)

        def shifted(dil):
            if dil == 1:
                dq_r, kp0, kp1, dkc_r, vp0, vp1, dvc_r = [att.pop(0) for _ in range(7)]
                live = i + 1 < nb
                half = DZ_TA // 2
                dkp = jnp.concatenate([kp0[:, :, half:, :], jnp.where(live, kp1[:, :, :half, :], 0.0)], axis=2)
                dvp = jnp.concatenate([vp0[:, :, half:, :], jnp.where(live, vp1[:, :, :half, :], 0.0)], axis=2)
            else:
                dq_r, dkp_r, dkc_r, dvp_r, dvc_r = [att.pop(0) for _ in range(5)]
                live = i + 8 * dil // DZ_TA < nb
                dkp, dvp = jnp.where(live, dkp_r[...], 0.0), jnp.where(live, dvp_r[...], 0.0)
            return dq_r[...], dkc_r[...] + dkp, dvc_r[...] + dvp

        dq, dk, dv = shifted(DILATIONS[0])
        for dil in DILATIONS[1:]:
            dq2, dk2, dv2 = shifted(dil)
            dq, dk, dv = dq + dq2, dk + dk2, dv + dv2
        dz_ref[:, :, :, 0:A_WIDTH] = dq.astype(BF)
        dz_ref[:, :, :, A_WIDTH:2 * A_WIDTH] = dk.astype(BF)
        dz_ref[:, :, :, 2 * A_WIDTH:ZA_W] = dv.astype(BF)
        dq, dk, dv = shifted(1)
        c0 = ZA_W + ZB_W
        dz_ref[:, :, :, c0:c0 + A_WIDTH] = dq.astype(BF)
        dz_ref[:, :, :, c0 + A_WIDTH:c0 + A_WIDTH + C_KV_WIDTH] = dk.astype(BF)
        dz_ref[:, :, :, c0 + A_WIDTH + C_KV_WIDTH:IN_WIDTH] = dv.astype(BF)

        cw = cw_ref[...]
        prev = jnp.where(i > 0, zbp_ref[...], 0.0)
        gb, gc, xb, u, u1, u2, c = _conv_strips(zb_ref[...], prev, cw)
        dyb = dyb_ref[...]
        dc = [_strip(dyb, b) * gb[b] for b in range(N_STRIPS)]
        dcn = jnp.where(i + 1 < nb, dybn_ref[...] * zbn_ref[:, :, :CONV_CH], 0.0)
        wrapped = [_shift_up(dc[0], 1, dcn[0]), _shift_up(dc[1], 1, dcn[1])]
        upd = [jnp.zeros((1, CONV_CH), F32)] * 3
        for b in range(N_STRIPS):
            dc1 = dc[b + 1] if b + 1 < N_STRIPS else wrapped[0]
            dc2 = dc[b + 2] if b + 2 < N_STRIPS else wrapped[b + 2 - N_STRIPS]
            du = cw[2:3, :] * dc[b] + cw[1:2, :] * dc1 + cw[0:1, :] * dc2
            f, e = b % 4, b // 4
            dz_ref[f, e, :, ZA_W:ZA_W + CONV_CH] = (_strip(dyb, b) * c[b]).astype(BF)
            dz_ref[f, e, :, ZA_W + CONV_CH:ZA_W + 2 * CONV_CH] = (du * xb[b]).astype(BF)
            dz_ref[f, e, :, ZA_W + 2 * CONV_CH:c0] = (du * gc[b]).astype(BF)
            for t, uu in enumerate((u2[b], u1[b], u[b])):
                upd[t] = upd[t] + jnp.sum(dc[b] * uu, axis=0, keepdims=True)
        row = lax.broadcasted_iota(jnp.int32, (HALO, CONV_CH), 0)
        tile = jnp.zeros((HALO, CONV_CH), F32)
        for t in range(3):
            tile = jnp.where(row == t, upd[t], tile)
        dcw_ref[...] += tile

    dz, dcw = pl.pallas_call(
        body, grid=(nb,), name="dz_assemble", in_specs=in_specs,
        out_specs=[_strip_rows(DZ_TA, IN_WIDTH), _whole((HALO, CONV_CH))],
        out_shape=[jax.ShapeDtypeStruct((4, 4, na, IN_WIDTH), BF), jax.ShapeDtypeStruct((HALO, CONV_CH), F32)],
        compiler_params=_cparams("arbitrary"),
    )(*args)
    return dz.reshape(s, IN_WIDTH), dcw


def _qkv_bwd(dz, dx1, x, g, w_all, l, tb):
    s, d = x.shape

    def body(dz_ref, dx1_ref, x_ref, g_ref, w_ref, dx_ref, dg_ref):
        i = pl.program_id(0)

        @pl.when(i == 0)
        def _():
            dg_ref[...] = jnp.zeros_like(dg_ref)

        n = IN_WIDTH // N_CHIPS
        dh = _nt(dz_ref[:, 0:n], w_ref[0])
        for k in range(1, N_CHIPS):
            dh = dh + _nt(dz_ref[:, k * n:(k + 1) * n], w_ref[k])
        xv = x_ref[...]
        r = _rms_scale(xv)
        xhat = xv * r
        dg_ref[...] += jnp.sum(dh * xhat, axis=0, keepdims=True)
        dx_ref[...] = dx1_ref[...] + _norm_bwd(dh * g_ref[...], xhat, r)

    return pl.pallas_call(
        body, grid=(s // tb,), name="qkv_bwd",
        in_specs=[_rows(tb, IN_WIDTH), _rows(tb, d), _rows(tb, d), _whole((1, d)),
                  _layer((N_CHIPS, d, IN_WIDTH // N_CHIPS), l)],
        out_specs=[_rows(tb, d), _whole((HALO, d))],
        out_shape=[jax.ShapeDtypeStruct((s, d), F32), jax.ShapeDtypeStruct((HALO, d), F32)],
        compiler_params=_cparams("arbitrary"),
    )(dz, dx1, x, g, w_all)


def _tile_rows(rows):
    return jnp.pad(rows, ((0, HALO - rows.shape[0]), (0, 0)))


def _to_strips(a, after, name):
    s, d = a.shape
    na = s // N_STRIPS
    ta = min(32, na)

    def body(a_ref, after_ref, o_ref):
        for b in range(N_STRIPS):
            o_ref[b % 4, b // 4] = a_ref[:, b, :]

    return pl.pallas_call(
        body, grid=(na // ta,), name=name,
        in_specs=[pl.BlockSpec((ta, N_STRIPS, d), lambda i: (i, 0, 0)), ANY], out_specs=_strip_rows(ta, d),
        out_shape=jax.ShapeDtypeStruct((4, 4, na, d), a.dtype), compiler_params=_cparams("parallel"),
    )(a.reshape(na, N_STRIPS, d), after).reshape(s, d)


def _from_strips(a, name):
    s, d = a.shape
    na = s // N_STRIPS
    ta = min(32, na)

    def body(a_ref, o_ref):
        for b in range(N_STRIPS):
            o_ref[:, b, :] = a_ref[b % 4, b // 4]

    return pl.pallas_call(
        body, grid=(na // ta,), name=name, in_specs=[_strip_rows(ta, d)],
        out_specs=pl.BlockSpec((ta, N_STRIPS, d), lambda i: (i, 0, 0)),
        out_shape=jax.ShapeDtypeStruct((na, N_STRIPS, d), a.dtype), compiler_params=_cparams("parallel"),
    )(_strips(a)).reshape(s, d)


def _local_step(x, tgt, fetch, ff, sinks, g_mix, g_group, g_mlp, g_final, emit):
    s, d = x.shape
    depth = g_mix.shape[0]
    tb = min(512, s)
    tf = ff // N_CHIPS
    ts = min(1024, s)
    saved = []
    for l in range(depth):
        w_in, _, _, _, conv_w = fetch(0, l, x)
        cw = _tile_rows(conv_w[l])
        sk = jnp.repeat(sinks[l].reshape(N_HEADS), HEAD_DIM)[None]
        h, za, zb, zc = _qkv_fwd(x, g_mix[l][None], w_in, l, tb)
        parts_a = [_attn_fwd(za, dil, A_WIDTH, 1, 2, 1, A_MAX_DIST, "attn_a_fwd_%d" % dil) for dil in DILATIONS]
        part_c = _attn_fwd(zc, 1, C_KV_WIDTH, 3, 4, C_GROUP, C_MAX_DIST, "attn_c_fwd")
        ya, lse_a, yc, lse_c = _attn_merge(parts_a, part_c, sk, tb)
        w_in, w_o, w1, w2, _ = fetch(1, l, yc)
        x1, yb = _mix_fwd(x, ya, yc, zb, cw, g_group[l][None], w_o, l, tb)
        x2, h2, ap = _mlp_fwd(x1, g_mlp[l][None], w1, w2, l, ts, tf)
        saved.append((x, h, za, zb, zc, ya, lse_a, yc, lse_c, yb, x1, h2, ap, cw, sk))
        x = x2
    dx, loss_tile, dg_final = _loss_head(x, g_final[None], tgt, tb)
    grads = [None] * depth
    tok = jnp.zeros((), F32)
    for l in reversed(range(depth)):
        x0, h, za, zb, zc, ya, lse_a, yc, lse_c, yb, x1, h2, ap, cw, sk = saved[l]
        dx1, dap, dg_mlp = _mlp_bwd(dx, x1, ap, g_mlp[l][None] + tok, w1, w2, l, ts, tf)
        tok = emit(l, 3, _wgrad(ap, dx, min(1024, ff), d, ts, "wgrad_ff_out", relu2=True))
        tok = tok + emit(l, 2, _wgrad(h2, dap, d, min(1024, ff), 2 * ts, "wgrad_ff_in"))
        n, dya, dyc, dd_a, dd_c, dyb, dg_group, dsink = _mix_bwd(dx1, ya, yb, yc, lse_c, sk, g_group[l][None] + tok,
                                                                 w_o, l, tb)
        tok = emit(l, 1, _wgrad(n, dx1, MIX_WIDTH, d, ts, "wgrad_o"))
        cw = cw + tok
        parts_a = [_attn_bwd(za, dya, lse_a, dd_a, dil, A_WIDTH, 1, 2, 1, A_MAX_DIST, "attn_a_bwd_%d" % dil)
                   for dil in DILATIONS]
        parts_c = _attn_bwd(zc, dyc, lse_c, dd_c, 1, C_KV_WIDTH, 3, 4, C_GROUP, C_MAX_DIST, "attn_c_bwd")
        dz, dcw = _dz_assemble(parts_a, parts_c, dyb, zb, cw)
        tok = emit(l, 0, _wgrad(h, dz, d, IN_WIDTH // 4, 2 * ts, "wgrad_in"))
        dx, dg_mix = _qkv_bwd(dz, dx1, x0, g_mix[l][None] + tok, w_in, l, tb)
        grads[l] = (dcw, dsink, dg_mix, dg_group, dg_mlp)
    return loss_tile, dx, grads, dg_final


ANY = pl.BlockSpec(memory_space=pl.ANY)
SHARD_AXES = (2, 1, 2, 1)
N_BIG = len(SHARD_AXES)
N_CHIPS = 4
N_DEV = 8


def _mesh_pos():
    return lax.axis_index("x"), lax.axis_index("y"), lax.axis_index("c")


def _flip(v, bit):
    return 1 - v if bit else v


def _place_shard(shard, chip_arr, name):
    _, rows, cols = shard.shape
    tr = min(256, rows)

    def body(chip_ref, x_ref, o_ref):
        o_ref[...] = x_ref[...].astype(BF)

    return pl.pallas_call(
        body, name=name,
        grid_spec=pltpu.PrefetchScalarGridSpec(
            num_scalar_prefetch=1, grid=(2, rows // tr),
            in_specs=[pl.BlockSpec((None, tr, cols), lambda l, i, chip: (l, i, 0))],
            out_specs=pl.BlockSpec((None, None, tr, cols), lambda l, i, chip: (l, chip[0], i, 0))),
        out_shape=jax.ShapeDtypeStruct((2, N_CHIPS, rows, cols), BF),
        compiler_params=_cparams("parallel", "parallel"),
    )(chip_arr, shard)


HBM = pl.BlockSpec(memory_space=pltpu.HBM)
SEM = pl.BlockSpec(memory_space=pltpu.SEMAPHORE)
EFFECT = pltpu.SideEffectType.DATAFLOW_SIDE_EFFECTING

GATHER_GROUPS = (((0, 0),), ((1, 0), (2, 0), (3, 0)), ((0, 1),), ((1, 1), (2, 1), (3, 1)))
GATHER_STARTS = ((0,), (1,), (2, 3))


def _gather_copies(arrs, group, send_sems, recv_sems):
    x, y, c = _mesh_pos()
    me = 2 * x + y
    out = []
    for i, (w, layer) in enumerate(group):
        mine = arrs[w].at[layer, me]
        for j, (qx, qy) in enumerate([(1 - x, y), (x, 1 - y), (1 - x, 1 - y)]):
            landed = arrs[w].at[layer, 2 * qx + qy]
            out.append(tuple(pltpu.make_async_remote_copy(
                src_ref=piece, dst_ref=piece, send_sem=send_sems.at[i * 3 + j], recv_sem=recv_sems.at[i * 3 + j],
                device_id=(qx, qy, c), device_id_type=MESH) for piece in (mine, landed)))
    return out


def _conv_copies(conv_src, conv_dst, send_sems, recv_sems):
    x, y, c = _mesh_pos()
    out = []
    for j, (qx, qy) in enumerate([(1 - x, y), (x, 1 - y), (1 - x, 1 - y)]):
        out.append(tuple(pltpu.make_async_remote_copy(
            src_ref=conv_src, dst_ref=conv_dst.at[q], send_sem=send_sems.at[j], recv_sem=recv_sems.at[j],
            device_id=(qx, qy, c), device_id_type=MESH) for q in (2 * x + y, 2 * qx + qy)))
    return out


def _gather_start(groups, arrs, conv, name, through=None):
    n_sems = 2 * (len(groups) + (conv is not None))
    mats = sorted({w for g in groups for w, _ in GATHER_GROUPS[g]})

    def body(*refs):
        arrs_ref = [None] * N_BIG
        for w, ref in zip(mats, refs):
            arrs_ref[w] = ref
        sems = refs[n_in:n_in + n_sems]
        if conv is not None:
            for cp, _ in _conv_copies(refs[len(mats)], refs[len(mats) + 1], sems[-2], sems[-1]):
                cp.start()
        for k, g in enumerate(groups):
            for cp, _ in _gather_copies(arrs_ref, GATHER_GROUPS[g], sems[2 * k], sems[2 * k + 1]):
                cp.start()

    sem_shapes = []
    for n in [len(GATHER_GROUPS[g]) for g in groups] + ([1] if conv is not None else []):
        sem_shapes += [pltpu.SemaphoreType.DMA((3 * n,))] * 2
    operands = [arrs[w] for w in mats] + ([] if conv is None else list(conv)) + ([] if through is None else [through])
    n_in = len(operands)
    res = pl.pallas_call(
        body, name=name,
        out_shape=tuple(sem_shapes) + tuple(pltpu.HBM(a.shape, a.dtype) for a in operands),
        in_specs=(HBM,) * n_in, out_specs=(SEM,) * n_sems + (HBM,) * n_in,
        input_output_aliases={i: n_sems + i for i in range(n_in)},
        compiler_params=pltpu.CompilerParams(has_side_effects=EFFECT),
    )(*[pltpu.with_memory_space_constraint(a, pltpu.HBM) for a in operands])
    arrs = list(arrs)
    for w, a in zip(mats, res[n_sems:]):
        arrs[w] = a
    return res[:n_sems], arrs, list(res[n_sems + len(mats):])


def _gather_wait(k, sems, arrs, conv, after, name):
    group = GATHER_GROUPS[k]
    mats = sorted({w for w, _ in group})
    n_conv = 0 if conv is None else 2

    def body(*refs):
        local = refs[:len(mats)]
        arrs_ref = [None] * N_BIG
        for w, ref in zip(mats, local):
            arrs_ref[w] = ref
        pos = len(mats) + n_conv
        copies = _gather_copies(arrs_ref, group, refs[pos], refs[pos + 1])
        if conv is not None:
            copies += _conv_copies(refs[len(mats)], refs[len(mats) + 1], refs[pos + 2], refs[pos + 3])
        for send, recv in copies:
            recv.wait_recv()
            send.wait_send()

    operands = [arrs[w] for w in mats] + ([] if conv is None else [conv[1], conv[2]])
    sem_ops = list(sems) + ([] if conv is None else list(conv[0]))
    n_op = len(operands)
    res = pl.pallas_call(
        body, name=name, out_shape=tuple(pltpu.HBM(a.shape, a.dtype) for a in operands),
        in_specs=(HBM,) * n_op + (SEM,) * len(sem_ops) + (ANY,), out_specs=(HBM,) * n_op,
        input_output_aliases={i: i for i in range(n_op)},
        compiler_params=pltpu.CompilerParams(has_side_effects=EFFECT),
    )(*operands, *sem_ops, after)
    arrs = list(arrs)
    for w, a in zip(mats, res):
        arrs[w] = a
    return arrs, (res[-1] if conv is not None else None)


def _grad_shard(ref, w, chip, n):
    start = pl.multiple_of(chip * n, 128)
    if SHARD_AXES[w] == 2:
        return ref.at[:, pl.ds(start, n)]
    return ref.at[pl.ds(start, n), :]


def _slot_shape(g, w):
    shape = list(g.shape)
    shape[SHARD_AXES[w] - 1] //= N_CHIPS
    return (N_DEV - 1,) + tuple(shape)


def _scatter_copies(g_ref, land_ref, send_sems, recv_sems, layer, w):
    x, y, c = _mesh_pos()
    n = g_ref.shape[SHARD_AXES[w] - 1] // N_CHIPS
    out = []
    for r in range(1, N_DEV):
        tx, ty, tc = _flip(x, r & 4), _flip(y, r & 2), _flip(c, r & 1)
        cp = pltpu.make_async_remote_copy(
            src_ref=_grad_shard(g_ref, w, 2 * tx + ty, n), dst_ref=land_ref.at[r - 1], send_sem=send_sems.at[r - 1],
            recv_sem=recv_sems.at[r - 1], device_id=(tx, ty, tc), device_id_type=MESH)
        out.append((cp, (c != layer) if r & 1 else (c == layer)))
    return out


def _scatter_start(g, land, layer, w, name):
    def body(g_ref, land_ref, send_sems, recv_sems, g_thru, land_thru, token):
        for cp, mine in _scatter_copies(g_ref, land_ref, send_sems, recv_sems, layer, w):
            @pl.when(mine)
            def _():
                cp.start()
        token[...] = jnp.zeros_like(token)

    return pl.pallas_call(
        body, name=name,
        out_shape=(pltpu.SemaphoreType.DMA((N_DEV - 1,)), pltpu.SemaphoreType.DMA((N_DEV - 1,)),
                   pltpu.HBM(g.shape, g.dtype), pltpu.HBM(land.shape, land.dtype),
                   jax.ShapeDtypeStruct((HALO, 128), F32)),
        in_specs=(HBM, HBM), out_specs=(SEM, SEM, HBM, HBM, pl.BlockSpec(memory_space=pltpu.VMEM)),
        input_output_aliases={0: 2, 1: 3}, compiler_params=pltpu.CompilerParams(has_side_effects=EFFECT),
    )(pltpu.with_memory_space_constraint(g, pltpu.HBM), pltpu.with_memory_space_constraint(land, pltpu.HBM))


def _scatter_wait(started, land, after, w, name):
    def body(g0_ref, g1_ref, land_ref, ss0, rs0, ss1, rs1, after_ref, g0_out, g1_out, land_out):
        c = lax.axis_index("c")
        for layer, g_ref, ss, rs in ((0, g0_ref, ss0, rs0), (1, g1_ref, ss1, rs1)):
            for cp, mine in _scatter_copies(g_ref, land_ref, ss, rs, layer, w):
                @pl.when(mine)
                def _():
                    cp.wait_send()

                @pl.when(c == layer)
                def _():
                    cp.wait_recv()

    (ss0, rs0, g0), (ss1, rs1, g1) = started
    return pl.pallas_call(
        body, name=name,
        out_shape=(pltpu.HBM(g0.shape, g0.dtype), pltpu.HBM(g1.shape, g1.dtype), pltpu.HBM(land.shape, land.dtype)),
        in_specs=(HBM, HBM, HBM, SEM, SEM, SEM, SEM, ANY), out_specs=(HBM, HBM, HBM),
        input_output_aliases={0: 0, 1: 1, 2: 2}, compiler_params=pltpu.CompilerParams(has_side_effects=EFFECT),
    )(g0, g1, land, ss0, rs0, ss1, rs1, after)


def _sum_slots(g0, g1, slots, w, pos_arr, name):
    _, rows, cols = slots.shape
    tr = min(256, rows)
    nr = rows // tr
    if SHARD_AXES[w] == 2:
        own = pl.BlockSpec((tr, cols), lambda i, pos: (i, pos[0]))
    else:
        own = pl.BlockSpec((tr, cols), lambda i, pos: (pos[0] * nr + i, 0))

    def body(pos_ref, own0_ref, own1_ref, s_ref, o_ref):
        acc = jnp.where(pos_ref[1] == 0, own0_ref[...], own1_ref[...]).astype(F32)
        for r in range(N_DEV - 1):
            acc = acc + s_ref[r].astype(F32)
        o_ref[...] = acc

    return pl.pallas_call(
        body, name=name,
        grid_spec=pltpu.PrefetchScalarGridSpec(
            num_scalar_prefetch=1, grid=(nr,),
            in_specs=[own, own, pl.BlockSpec((N_DEV - 1, tr, cols), lambda i, pos: (0, i, 0))],
            out_specs=pl.BlockSpec((tr, cols), lambda i, pos: (i, 0))),
        out_shape=jax.ShapeDtypeStruct((rows, cols), F32), compiler_params=_cparams("parallel"),
    )(pos_arr, g0, g1, slots)


def _swap_layers(halves, name):
    n = len(halves)

    def body(*refs):
        srcs, dsts = refs[:n], refs[n:2 * n]
        send_sems, recv_sems = refs[2 * n:]
        x, y, c = _mesh_pos()
        sends = [pltpu.make_async_remote_copy(src_ref=srcs[w], dst_ref=dsts[w], send_sem=send_sems.at[w],
                                              recv_sem=recv_sems.at[w], device_id=(x, y, 1 - c), device_id_type=MESH)
                 for w in range(n)]
        for cp in sends:
            cp.start()
        for cp in sends:
            cp.wait_recv()
        for cp in sends:
            cp.wait_send()

    return pl.pallas_call(
        body, name=name, in_specs=[ANY] * n, out_specs=[ANY] * n,
        out_shape=[jax.ShapeDtypeStruct(h.shape, h.dtype) for h in halves],
        scratch_shapes=[pltpu.SemaphoreType.DMA((n,)), pltpu.SemaphoreType.DMA((n,))],
    )(*halves)


def _adamw_math(w, g, m, v):
    m = ADAM_B1 * m + (1.0 - ADAM_B1) * g
    v = ADAM_B2 * v + (1.0 - ADAM_B2) * jnp.square(g)
    m_hat = m / (1.0 - ADAM_B1 ** ADAM_STEP)
    v_hat = v / (1.0 - ADAM_B2 ** ADAM_STEP)
    delta = -ADAM_LR * (m_hat / (jnp.sqrt(v_hat) + ADAM_EPS) + ADAM_WD * w)
    return delta, m, v


def _adamw(w, g_own, g_other, m, v, pos_arr, name):
    shape = w.shape
    _, rows, cols = shape
    tr = min(256, rows)

    def body(pos_ref, w_ref, own_ref, other_ref, m_ref, v_ref, g_ref, d_ref, m2_ref, v2_ref):
        g = jnp.where(pl.program_id(0) == pos_ref[1], own_ref[...], other_ref[...])
        g_ref[...] = g
        d_ref[...], m2_ref[...], v2_ref[...] = _adamw_math(w_ref[...], g, m_ref[...], v_ref[...])

    full = pl.BlockSpec((None, tr, cols), lambda l, i, pos: (l, i, 0))
    half = pl.BlockSpec((tr, cols), lambda l, i, pos: (i, 0))
    return pl.pallas_call(
        body, name=name,
        grid_spec=pltpu.PrefetchScalarGridSpec(
            num_scalar_prefetch=1, grid=(2, rows // tr),
            in_specs=[full, half, half, full, full], out_specs=[full] * 4),
        out_shape=[jax.ShapeDtypeStruct(shape, F32)] * 4, compiler_params=_cparams("parallel", "parallel"),
    )(pos_arr, w, g_own, g_other, m, v)


def _small_sync(part, w, m, v):
    rows, cols = part.shape

    def body(p_ref, w_ref, m_ref, v_ref, g_ref, d_ref, m2_ref, v2_ref, slots, send_sems, recv_sems):
        x, y, c = _mesh_pos()
        me = 4 * x + 2 * y + c
        slots[me] = p_ref[...]
        sends = []
        for r in range(1, N_DEV):
            to = (_flip(x, r & 4), _flip(y, r & 2), _flip(c, r & 1))
            sends.append(pltpu.make_async_remote_copy(
                src_ref=p_ref, dst_ref=slots.at[me], send_sem=send_sems.at[r - 1], recv_sem=recv_sems.at[r - 1],
                device_id=to, device_id_type=MESH))
        for cp in sends:
            cp.start()
        for cp in sends:
            cp.wait_recv()
        for cp in sends:
            cp.wait_send()
        g = slots[0]
        for i in range(1, N_DEV):
            g = g + slots[i]
        g_ref[...] = g
        d_ref[...], m2_ref[...], v2_ref[...] = _adamw_math(w_ref[...], g, m_ref[...], v_ref[...])

    vm = pl.BlockSpec(memory_space=pltpu.VMEM)
    return pl.pallas_call(
        body, name="small_sync", in_specs=[vm] * 4, out_specs=[vm] * 4,
        out_shape=[jax.ShapeDtypeStruct((rows, cols), F32)] * 4,
        scratch_shapes=[pltpu.VMEM((N_DEV, rows, cols), F32), pltpu.SemaphoreType.DMA((N_DEV - 1,)),
                        pltpu.SemaphoreType.DMA((N_DEV - 1,))],
    )(part, w, m, v)


PACK_W = 256


def _pack_rows(n):
    return -(-n // (HALO * PACK_W)) * HALO


def _pack_small(parts):
    out = []
    for a in parts:
        flat = a.reshape(-1)
        out.append(jnp.pad(flat, (0, _pack_rows(flat.size) * PACK_W - flat.size)).reshape(-1, PACK_W))
    return jnp.concatenate(out, axis=0)


def _unpack_small(p, shapes):
    out, row = [], 0
    for shape in shapes:
        n = 1
        for k in shape:
            n *= k
        out.append(p[row:row + _pack_rows(n)].reshape(-1)[:n].reshape(shape))
        row += _pack_rows(n)
    return out


def kernel(x, w_in, conv_w, sinks, g_mix, g_group, w_o, g_mlp, w_ff_in, w_ff_out, g_final, loss_target, m_w_in, m_conv_w, m_sinks, m_g_mix, m_g_group, m_w_o, m_g_mlp, m_w_ff_in, m_w_ff_out, m_g_final, v_w_in, v_conv_w, v_sinks, v_g_mix, v_g_group, v_w_o, v_g_mlp, v_w_ff_in, v_w_ff_out, v_g_final):
    chip = 2 * lax.axis_index("x") + lax.axis_index("y")
    conv_n = conv_w.shape[2]

    pos_arr = jnp.stack([chip, lax.axis_index("c")]).astype(jnp.int32)
    shards = (w_in, w_o, w_ff_in, w_ff_out)
    conv_tile = jnp.pad(conv_w.reshape(6, conv_n), ((0, HALO - 6), (0, 128 - conv_n)))
    placed = [_place_shard(w_in, pos_arr[:1], "place_shard_0"), None, None, None]
    sems_a, placed, conv_thru = _gather_start(
        GATHER_STARTS[0], placed, (conv_tile, lax.empty((N_CHIPS,) + conv_tile.shape, conv_tile.dtype)),
        "gather_start_0")
    for i in range(1, N_BIG):
        placed[i] = _place_shard(shards[i], pos_arr[:1], "place_shard_%d" % i)
    full = {"arrs": placed, "conv": None, "sems": list(sems_a[:2])}

    def fetch(stage, layer, after):
        k = 2 * layer + stage
        sems = full["sems"][2 * k:2 * k + 2]
        if k == 0:
            full["arrs"], land = _gather_wait(0, sems, full["arrs"], (sems_a[-2:], *conv_thru), after, "gather_wait_0")
            conv_all = lax.dynamic_update_slice(land, conv_tile[None], (chip, 0, 0))
            full["conv"] = conv_all[:, :6, :conv_n].reshape(N_CHIPS, 2, 3, conv_n).transpose(1, 2, 0, 3).reshape(
                2, 3, CONV_CH)
            sems_b, full["arrs"], rest = _gather_start(GATHER_STARTS[1], full["arrs"], None, "gather_start_1",
                                                       through=full["arrs"][0])
            full["arrs"][0] = rest[-1]
            full["sems"] += list(sems_b)
        else:
            full["arrs"], _ = _gather_wait(k, sems, full["arrs"], None, after, "gather_wait_%d" % k)
        if k == 1:
            sems_c, full["arrs"], _ = _gather_start(GATHER_STARTS[2], full["arrs"], None, "gather_start_2")
            full["sems"] += list(sems_c)
        return (*full["arrs"], full["conv"])

    lands, started = [None] * N_BIG, {}

    def emit(layer, w, g):
        if lands[w] is None:
            lands[w] = lax.empty(_slot_shape(g, w), g.dtype)
        *started[layer, w], lands[w], token = _scatter_start(g, lands[w], layer, w, "scatter_start_%d_%d" % (layer, w))
        return token[0, 0]

    loss_tile, dx, grads, dg_final = _local_step(_to_strips(x[0], placed[0], "to_strips_x"),
                                                 _to_strips(loss_target[0], placed[0], "to_strips_target"), fetch,
                                                 w_ff_in.shape[2] * N_CHIPS,
                                                 sinks, g_mix, g_group, g_mlp, g_final, emit)

    wmv = ((w_in, m_w_in, v_w_in), (w_o, m_w_o, v_w_o), (w_ff_in, m_w_ff_in, v_w_ff_in),
           (w_ff_out, m_w_ff_out, v_w_ff_out))
    big, after = [None] * N_BIG, dx
    for name, ws in (("swap_layers_rest", (1, 2, 3)), ("swap_layers_in", (0,))):
        own = []
        for w in ws:
            g0, g1, slots = _scatter_wait((started[0, w], started[1, w]), lands[w], after, w, "scatter_wait_%d" % w)
            own.append(_sum_slots(g0, g1, slots, w, pos_arr, "sum_slots_%d" % w))
        for w, mine, theirs in zip(ws, own, _swap_layers(own, name)):
            big[w] = _adamw(wmv[w][0], mine, theirs, wmv[w][1], wmv[w][2], pos_arr, "adamw_%d" % w)
        after = big[ws[-1]][1]

    def both(i):
        return jnp.stack([grads[0][i][0], grads[1][i][0]])
    dconv = jnp.stack([grads[0][0][:3], grads[1][0][:3]])
    dsinks = jnp.stack([grads[0][1][0, ::HEAD_DIM], grads[1][1][0, ::HEAD_DIM]])
    part = _pack_small([both(2), both(3), both(4), dg_final[0], dconv, dsinks, loss_tile[0, 0]])

    def spread(shard):
        return lax.dynamic_update_slice(jnp.zeros((2, 3, CONV_CH), F32), shard, (0, 0, chip * conv_n))
    zero = jnp.zeros((), F32)
    packs = [_pack_small([a, b, c_, e, spread(f), g_, zero]) for a, b, c_, e, f, g_ in (
        (g_mix, g_group, g_mlp, g_final, conv_w, sinks),
        (m_g_mix, m_g_group, m_g_mlp, m_g_final, m_conv_w, m_sinks),
        (v_g_mix, v_g_group, v_g_mlp, v_g_final, v_conv_w, v_sinks))]
    shapes = [g_mix.shape, g_group.shape, g_mlp.shape, g_final.shape, (2, 3, CONV_CH), sinks.shape, ()]
    small = [_unpack_small(p, shapes) for p in _small_sync(part, *packs)]

    def shard_of(full):
        return lax.dynamic_slice(full, (0, 0, chip * conv_n), (2, 3, conv_n))
    small = [(s[0], s[1], s[2], s[3], shard_of(s[4]), s[5], s[6]) for s in small]
    loss = small[0][6]

    def ordered(kind):
        b = [big[i][kind] for i in range(N_BIG)]
        s = small[kind]
        return [b[0], s[4], s[5], s[0], s[1], b[1], s[2], b[2], b[3], s[3]]

    return (loss, _from_strips(dx, "from_strips_dx")[None], *ordered(0), *ordered(1), *ordered(2), *ordered(3))
```

```python
import functools

import jax
import jax.numpy as jnp
from jax import lax
from jax.experimental import pallas as pl
from jax.experimental.pallas import tpu as pltpu

HEAD_DIM = 64
N_HEADS = 6
C_GROUP = 3
A_WIDTH = N_HEADS * HEAD_DIM
C_KV_WIDTH = 2 * HEAD_DIM
CONV_CH = 256
ZA_W = 3 * A_WIDTH
ZB_W = 3 * CONV_CH
ZC_W = A_WIDTH + 2 * C_KV_WIDTH
IN_WIDTH = ZA_W + ZB_W + ZC_W
MIX_WIDTH = A_WIDTH + CONV_CH + A_WIDTH
DILATIONS = (1, 4, 16)
A_MAX_DIST = 128
C_MAX_DIST = 127
TQ = 128
EPS = 1e-6
SCALE = HEAD_DIM ** -0.5
NEG = -1e30
HALO = 8

ADAM_LR = 0.001
ADAM_B1 = 0.9
ADAM_B2 = 0.999
ADAM_EPS = 1e-08
ADAM_WD = 0.01
ADAM_STEP = 10

BF = jnp.bfloat16
F32 = jnp.float32
MESH = pl.DeviceIdType.MESH
VMEM_LIMIT = 56 * 1024 * 1024


def _cparams(*sem):
    return pltpu.CompilerParams(dimension_semantics=sem, vmem_limit_bytes=VMEM_LIMIT)


def _nt(a, b):
    return lax.dot_general(a, b, (((1,), (1,)), ((), ())), preferred_element_type=F32)


def _tn(a, b):
    return lax.dot_general(a, b, (((0,), (0,)), ((), ())), preferred_element_type=F32)


def _nn(a, b):
    return jnp.dot(a, b, preferred_element_type=F32)


def _rows(tb, w):
    return pl.BlockSpec((tb, w), lambda i: (i, 0))


def _whole(shape):
    return pl.BlockSpec(shape, lambda *_: (0,) * len(shape))


def _layer(shape, l):
    return pl.BlockSpec((None,) + shape, lambda *_: (l,) + (0,) * len(shape))


def _rms_scale(v):
    return lax.rsqrt(jnp.mean(v * v, axis=-1, keepdims=True) + EPS)


def _norm_bwd(dxhat, xhat, r):
    return r * (dxhat - xhat * jnp.mean(dxhat * xhat, axis=-1, keepdims=True))


def _qkv_fwd(x, g, w_all, l, tb):
    s, d = x.shape

    def body(x_ref, g_ref, w_ref, h_ref, za_ref, zb_ref, zc_ref):
        xv = x_ref[...]
        h = ((xv * _rms_scale(xv)) * g_ref[...]).astype(BF)
        h_ref[...] = h
        z = jnp.concatenate([_nn(h, w_ref[k]) for k in range(N_CHIPS)], axis=1)
        za_ref[...] = z[:, :ZA_W]
        zb_ref[...] = z[:, ZA_W:ZA_W + ZB_W]
        zc_ref[...] = z[:, ZA_W + ZB_W:]

    return pl.pallas_call(
        body, grid=(s // tb,), name="qkv_fwd",
        in_specs=[_rows(tb, d), _whole((1, d)), _layer((N_CHIPS, d, IN_WIDTH // N_CHIPS), l)],
        out_specs=[_rows(tb, d), _rows(tb, ZA_W), _rows(tb, ZB_W), _rows(tb, ZC_W)],
        out_shape=[jax.ShapeDtypeStruct((s, d), BF), jax.ShapeDtypeStruct((s, ZA_W), F32),
                   jax.ShapeDtypeStruct((s, ZB_W), F32), jax.ShapeDtypeStruct((s, ZC_W), F32)],
        compiler_params=_cparams("parallel"),
    )(x, g, w_all)


N_STRIPS = 16


def _strips(a):
    s, w = a.shape
    return a.reshape(4, 4, s // N_STRIPS, w)


def _p_grid(s, dil):
    na = s // N_STRIPS
    return {16: (4, 4, na // TQ), 4: (4, na // 32), 1: (na // 8,)}[dil]


def _p_spec(dil, cw, col, prev=False):
    def blk(j):
        return jnp.maximum(j - 1, 0) if prev else j
    if dil == 16:
        return pl.BlockSpec((None, None, TQ, cw), lambda f, e, j: (f, e, blk(j), col))
    if dil == 4:
        return pl.BlockSpec((None, 4, 32, cw), lambda f, j: (f, 0, blk(j), col))
    return pl.BlockSpec((4, 4, 8, cw), lambda j: (0, 0, blk(j), col))


def _block_pos(i, dil):
    if dil == 16:
        return i
    if dil == 4:
        return 4 * (i % 32) + i // 32
    return 16 * (i % 8) + 4 * ((i // 8) % 4) + i // 32


def _band_mask(b, dil, max_dist):
    qi = _block_pos(lax.broadcasted_iota(jnp.int32, (TQ, 2 * TQ), 0), dil)
    col = lax.broadcasted_iota(jnp.int32, (TQ, 2 * TQ), 1)
    cur = col >= TQ
    dist = qi - _block_pos(col % TQ, dil) + jnp.where(cur, 0, TQ)
    return (dist >= 0) & (dist <= max_dist) & (cur | (b > 0))


def _hs(h):
    return slice(h * HEAD_DIM, (h + 1) * HEAD_DIM)


def _ld(ref, cols):
    v = ref[..., cols]
    return v.reshape(TQ, v.shape[-1])


def _st(ref, cols, val):
    ref[..., cols] = val.reshape(ref.shape[:-1] + (val.shape[-1],))


def _attn_fwd(z, dil, kw, kcol, vcol, n_rep, max_dist, name):
    s, zw = z.shape
    grid = _p_grid(s, dil)

    def body(q_ref, kp_ref, kc_ref, vp_ref, vc_ref, acc_ref, m_ref, l_ref):
        mask = _band_mask(pl.program_id(len(grid) - 1), dil, max_dist)
        for kh in range(N_HEADS // n_rep):
            k2 = jnp.concatenate([_ld(kp_ref, _hs(kh)), _ld(kc_ref, _hs(kh))], axis=0).astype(BF)
            v2 = jnp.concatenate([_ld(vp_ref, _hs(kh)), _ld(vc_ref, _hs(kh))], axis=0).astype(BF)
            for h in range(kh * n_rep, (kh + 1) * n_rep):
                q = (_ld(q_ref, _hs(h)) * SCALE).astype(BF)
                sc = jnp.where(mask, _nt(q, k2), NEG)
                m = jnp.max(sc, axis=1, keepdims=True)
                p = jnp.exp(sc - m)
                _st(acc_ref, _hs(h), _nn(p.astype(BF), v2))
                _st(m_ref, _hs(h), jnp.broadcast_to(m, (TQ, HEAD_DIM)))
                _st(l_ref, _hs(h), jnp.broadcast_to(jnp.sum(p, axis=1, keepdims=True), (TQ, HEAD_DIM)))

    res = pl.pallas_call(
        body, grid=grid, name=name,
        in_specs=[_p_spec(dil, A_WIDTH, 0), _p_spec(dil, kw, kcol, True), _p_spec(dil, kw, kcol),
                  _p_spec(dil, kw, vcol, True), _p_spec(dil, kw, vcol)],
        out_specs=[_p_spec(dil, A_WIDTH, 0)] * 3,
        out_shape=[jax.ShapeDtypeStruct((4, 4, s // N_STRIPS, A_WIDTH), F32)] * 3,
        compiler_params=_cparams(*(("parallel",) * len(grid))),
    )(*[_strips(z)] * 5)
    return [a.reshape(s, A_WIDTH) for a in res]


def _attn_merge(parts_a, part_c, sink_row, tb):
    s = part_c[0].shape[0]
    n_a = len(parts_a)

    def body(*refs):
        ins, sink_ref = refs[:3 * n_a + 3], refs[3 * n_a + 3]
        ya_ref, lsea_ref, yc_ref, lsec_ref = refs[3 * n_a + 4:]
        ms = [ins[3 * p + 1][...] for p in range(n_a)]
        m = functools.reduce(jnp.maximum, ms)
        acc = jnp.zeros_like(m)
        l = jnp.zeros_like(m)
        for p in range(n_a):
            w = jnp.exp(ms[p] - m)
            acc = acc + w * ins[3 * p][...]
            l = l + w * ins[3 * p + 2][...]
        ya_ref[...] = acc / l
        lsea_ref[...] = m + jnp.log(l)
        acc_c, m_c, l_c = [r[...] for r in ins[3 * n_a:]]
        sk = sink_ref[...]
        m2 = jnp.maximum(m_c, sk)
        w = jnp.exp(m_c - m2)
        l2 = w * l_c + jnp.exp(sk - m2)
        yc_ref[...] = (w * acc_c) / l2
        lsec_ref[...] = m2 + jnp.log(l2)

    return pl.pallas_call(
        body, grid=(s // tb,), name="attn_merge",
        in_specs=[_rows(tb, A_WIDTH)] * (3 * n_a + 3) + [_whole((1, A_WIDTH))],
        out_specs=[_rows(tb, A_WIDTH)] * 4, out_shape=[jax.ShapeDtypeStruct((s, A_WIDTH), F32)] * 4,
        compiler_params=_cparams("parallel"),
    )(*[a for part in parts_a + [part_c] for a in part], sink_row)


def _shift_down(v, n, halo):
    rows = v.shape[0]
    out = pltpu.roll(v, n, 0)
    row = lax.broadcasted_iota(jnp.int32, v.shape, 0)
    for t in range(n):
        out = jnp.where(row == t, halo[HALO - n + t:HALO - n + t + 1, :], out)
    return out


def _shift_up(v, n, halo):
    rows = v.shape[0]
    out = pltpu.roll(v, rows - n, 0)
    row = lax.broadcasted_iota(jnp.int32, v.shape, 0)
    for t in range(n):
        out = jnp.where(row == rows - n + t, halo[t:t + 1, :], out)
    return out


def _strip(v, b):
    return v[b % 4, b // 4]


def _conv_strips(zb, prev, cw):
    gb = [_strip(zb, b)[:, :CONV_CH] for b in range(N_STRIPS)]
    gc = [_strip(zb, b)[:, CONV_CH:2 * CONV_CH] for b in range(N_STRIPS)]
    xb = [_strip(zb, b)[:, 2 * CONV_CH:] for b in range(N_STRIPS)]
    u = [g * v for g, v in zip(gc, xb)]
    uh = prev[:, :, CONV_CH:2 * CONV_CH] * prev[:, :, 2 * CONV_CH:]
    wrapped = {14: _shift_down(u[14], 1, uh[2]), 15: _shift_down(u[15], 1, uh[3])}
    u1 = [u[b - 1] if b >= 1 else wrapped[15] for b in range(N_STRIPS)]
    u2 = [u[b - 2] if b >= 2 else wrapped[14 + b] for b in range(N_STRIPS)]
    c = [cw[0:1, :] * u2[b] + cw[1:2, :] * u1[b] + cw[2:3, :] * u[b] for b in range(N_STRIPS)]
    return gb, gc, xb, u, u1, u2, c


def _strip_rows(ta, w):
    return pl.BlockSpec((4, 4, ta, w), lambda i: (0, 0, i, 0))


def _prev_rows(ta, w):
    return pl.BlockSpec((4, None, HALO, w), lambda i: (0, 3, jnp.maximum(i * (ta // HALO) - 1, 0), 0))


def _next_rows(ta, w, nblk):
    return pl.BlockSpec((4, None, HALO, w),
                        lambda i: (0, 0, jnp.minimum((i + 1) * (ta // HALO), nblk * (ta // HALO) - 1), 0))


def _mix_fwd(x, ya, yc, zb, cw, gg, wo_all, l, tb):
    s, d = x.shape
    ta = tb // N_STRIPS

    def body(x_ref, ya_ref, yc_ref, zb_ref, zbp_ref, cw_ref, gg_ref, wo_ref, x1_ref, yb_ref):
        i = pl.program_id(0)
        prev = jnp.where(i > 0, zbp_ref[...], 0.0)
        gb, _, _, _, _, _, c = _conv_strips(zb_ref[...], prev, cw_ref[...])
        for b in range(N_STRIPS):
            yb_ref[b % 4, b // 4] = gb[b] * c[b]
        yb = yb_ref[...].reshape(tb, CONV_CH)
        ya, yc = ya_ref[...].reshape(tb, A_WIDTH), yc_ref[...].reshape(tb, A_WIDTH)
        n = jnp.concatenate([ya * _rms_scale(ya), yb * _rms_scale(yb), yc * _rms_scale(yc)], axis=1)
        n = (n * gg_ref[...]).astype(BF)
        x1 = x_ref[...].reshape(tb, d) + _nn(n, wo_ref[...].reshape(MIX_WIDTH, d))
        x1_ref[...] = x1.reshape(4, 4, ta, d)

    res = pl.pallas_call(
        body, grid=(s // tb,), name="mix_fwd",
        in_specs=[_strip_rows(ta, d), _strip_rows(ta, A_WIDTH), _strip_rows(ta, A_WIDTH), _strip_rows(ta, ZB_W),
                  _prev_rows(ta, ZB_W), _whole((HALO, CONV_CH)), _whole((1, MIX_WIDTH)),
                  _layer((N_CHIPS, MIX_WIDTH // N_CHIPS, d), l)],
        out_specs=[_strip_rows(ta, d), _strip_rows(ta, CONV_CH)],
        out_shape=[jax.ShapeDtypeStruct((4, 4, s // N_STRIPS, d), F32),
                   jax.ShapeDtypeStruct((4, 4, s // N_STRIPS, CONV_CH), F32)],
        compiler_params=_cparams("parallel"),
    )(_strips(x), _strips(ya), _strips(yc), _strips(zb), _strips(zb), cw, gg, wo_all)
    return res[0].reshape(s, d), res[1].reshape(s, CONV_CH)


def _mlp_fwd(x1, g, w1_all, w2_all, l, tb, tf):
    s, d = x1.shape
    ff = w1_all.shape[1] * w1_all.shape[3]
    nj = ff // tf

    def body(x_ref, g_ref, w1_ref, w2_ref, x2_ref, h2_ref, ap_ref, acc):
        j = pl.program_id(1)

        @pl.when(j == 0)
        def _():
            xv = x_ref[...]
            h2_ref[...] = ((xv * _rms_scale(xv)) * g_ref[...]).astype(BF)
            acc[...] = jnp.zeros_like(acc)

        ap = _nn(h2_ref[...], w1_ref[...])
        ap_ref[...] = ap.astype(BF)
        a = jnp.square(jnp.maximum(ap, 0.0)).astype(BF)
        acc[...] += _nn(a, w2_ref[...])

        @pl.when(j == nj - 1)
        def _():
            x2_ref[...] = x_ref[...] + acc[...]

    return pl.pallas_call(
        body, grid=(s // tb, nj), name="mlp_fwd",
        in_specs=[pl.BlockSpec((tb, d), lambda i, j: (i, 0)), _whole((1, d)),
                  pl.BlockSpec((None, None, d, tf), lambda i, j: (l, j, 0, 0)),
                  pl.BlockSpec((None, None, tf, d), lambda i, j: (l, j, 0, 0))],
        out_specs=[pl.BlockSpec((tb, d), lambda i, j: (i, 0)), pl.BlockSpec((tb, d), lambda i, j: (i, 0)),
                   pl.BlockSpec((tb, tf), lambda i, j: (i, j))],
        out_shape=[jax.ShapeDtypeStruct((s, d), F32), jax.ShapeDtypeStruct((s, d), BF),
                   jax.ShapeDtypeStruct((s, ff), BF)],
        scratch_shapes=[pltpu.VMEM((tb, d), F32)],
        compiler_params=_cparams("parallel", "arbitrary"),
    )(x1, g, w1_all, w2_all)


def _loss_head(x, g, tgt, tb):
    s, d = x.shape

    def body(x_ref, g_ref, t_ref, dx_ref, loss_ref, dg_ref):
        i = pl.program_id(0)

        @pl.when(i == 0)
        def _():
            loss_ref[...] = jnp.zeros_like(loss_ref)
            dg_ref[...] = jnp.zeros_like(dg_ref)

        xv = x_ref[...]
        r = _rms_scale(xv)
        xhat = xv * r
        err = xhat * g_ref[...] - t_ref[...]
        part = jnp.sum(jnp.mean(jnp.square(err), axis=-1, keepdims=True), axis=0, keepdims=True)
        loss_ref[...] += 0.5 * part
        dy = err * (1.0 / d)
        dg_ref[...] += jnp.sum(dy * xhat, axis=0, keepdims=True)
        dx_ref[...] = _norm_bwd(dy * g_ref[...], xhat, r)

    return pl.pallas_call(
        body, grid=(s // tb,), name="loss_head",
        in_specs=[_rows(tb, d), _whole((1, d)), _rows(tb, d)],
        out_specs=[_rows(tb, d), _whole((HALO, 128)), _whole((HALO, d))],
        out_shape=[jax.ShapeDtypeStruct((s, d), F32), jax.ShapeDtypeStruct((HALO, 128), F32),
                   jax.ShapeDtypeStruct((HALO, d), F32)],
        compiler_params=_cparams("arbitrary"),
    )(x, g, tgt)


def _mlp_bwd(dx2, x1, ap, g, w1_all, w2_all, l, tb, tf):
    s, d = x1.shape
    ff = ap.shape[1]
    nj = ff // tf

    def body(dx2_ref, x1_ref, ap_ref, g_ref, w1_ref, w2_ref, dx1_ref, dap_ref, dg_ref, acc):
        i, j = pl.program_id(0), pl.program_id(1)

        @pl.when((i == 0) & (j == 0))
        def _():
            dg_ref[...] = jnp.zeros_like(dg_ref)

        @pl.when(j == 0)
        def _():
            acc[...] = jnp.zeros_like(acc)

        da = _nt(dx2_ref[...].astype(BF), w2_ref[...])
        dap = (da * (2.0 * jnp.maximum(ap_ref[...].astype(F32), 0.0))).astype(BF)
        dap_ref[...] = dap
        acc[...] += _nt(dap, w1_ref[...])

        @pl.when(j == nj - 1)
        def _():
            xv = x1_ref[...]
            r = _rms_scale(xv)
            xhat = xv * r
            dh = acc[...]
            dg_ref[...] += jnp.sum(dh * xhat, axis=0, keepdims=True)
            dx1_ref[...] = dx2_ref[...] + _norm_bwd(dh * g_ref[...], xhat, r)

    return pl.pallas_call(
        body, grid=(s // tb, nj), name="mlp_bwd",
        in_specs=[pl.BlockSpec((tb, d), lambda i, j: (i, 0)), pl.BlockSpec((tb, d), lambda i, j: (i, 0)),
                  pl.BlockSpec((tb, tf), lambda i, j: (i, j)),
                  _whole((1, d)), pl.BlockSpec((None, None, d, tf), lambda i, j: (l, j, 0, 0)),
                  pl.BlockSpec((None, None, tf, d), lambda i, j: (l, j, 0, 0))],
        out_specs=[pl.BlockSpec((tb, d), lambda i, j: (i, 0)), pl.BlockSpec((tb, tf), lambda i, j: (i, j)),
                   _whole((HALO, d))],
        out_shape=[jax.ShapeDtypeStruct((s, d), F32), jax.ShapeDtypeStruct((s, ff), BF),
                   jax.ShapeDtypeStruct((HALO, d), F32)],
        scratch_shapes=[pltpu.VMEM((tb, d), F32)],
        compiler_params=_cparams("arbitrary", "arbitrary"),
    )(dx2, x1, ap, g, w1_all, w2_all)


def _wgrad(a, b, tm, tn, ts, name, relu2=False):
    s, m = a.shape
    n = b.shape[1]
    ns = s // ts

    def body(a_ref, b_ref, o_ref, acc):
        k = pl.program_id(2)

        @pl.when(k == 0)
        def _():
            acc[...] = jnp.zeros_like(acc)

        av = a_ref[...]
        if relu2:
            av = jnp.square(jnp.maximum(av.astype(F32), 0.0)).astype(BF)
        acc[...] += _tn(av, b_ref[...].astype(BF))

        @pl.when(k == ns - 1)
        def _():
            o_ref[...] = acc[...].astype(BF)

    return pl.pallas_call(
        body, grid=(m // tm, n // tn, ns), name=name,
        in_specs=[pl.BlockSpec((ts, tm), lambda i, j, k: (k, i)), pl.BlockSpec((ts, tn), lambda i, j, k: (k, j))],
        out_specs=pl.BlockSpec((tm, tn), lambda i, j, k: (i, j)),
        out_shape=jax.ShapeDtypeStruct((m, n), BF),
        scratch_shapes=[pltpu.VMEM((tm, tn), F32)],
        compiler_params=_cparams("parallel", "parallel", "arbitrary"),
    )(a, b)


def _mix_bwd(dx1, ya, yb, yc, lse_c, sink_row, gg, wo_all, l, tb):
    s, d = dx1.shape

    def body(dx_ref, ya_ref, yb_ref, yc_ref, lse_ref, sink_ref, gg_ref, wo_ref,
             n_ref, dya_ref, dyc_ref, da_ref, dc_ref, dyb_ref, dg_ref, dsink_ref):
        i = pl.program_id(0)

        @pl.when(i == 0)
        def _():
            dg_ref[...] = jnp.zeros_like(dg_ref)
            dsink_ref[...] = jnp.zeros_like(dsink_ref)

        dn = _nt(dx_ref[...].astype(BF), wo_ref[...].reshape(MIX_WIDTH, d))
        ys = [ya_ref[...], yb_ref[...], yc_ref[...]]
        rs = [_rms_scale(v) for v in ys]
        nhat = jnp.concatenate([v * r for v, r in zip(ys, rs)], axis=1)
        gg = gg_ref[...]
        n_ref[...] = (nhat * gg).astype(BF)
        dg_ref[...] += jnp.sum(dn * nhat, axis=0, keepdims=True)
        dnh = dn * gg
        bounds = [(0, A_WIDTH), (A_WIDTH, A_WIDTH + CONV_CH), (A_WIDTH + CONV_CH, MIX_WIDTH)]
        dys = [_norm_bwd(dnh[:, lo:hi], nhat[:, lo:hi], r) for (lo, hi), r in zip(bounds, rs)]
        dyb_ref[...] = dys[1]
        for dy, y, dy_ref, dd_ref in ((dys[0], ys[0], dya_ref, da_ref), (dys[2], ys[2], dyc_ref, dc_ref)):
            dy_ref[...] = dy
            t = dy * y
            for h in range(N_HEADS):
                dd_ref[:, _hs(h)] = jnp.broadcast_to(jnp.sum(t[:, _hs(h)], axis=1, keepdims=True), (tb, HEAD_DIM))
        dsink_ref[...] -= jnp.sum(jnp.exp(sink_ref[...] - lse_ref[...]) * dc_ref[...], axis=0, keepdims=True)

    return pl.pallas_call(
        body, grid=(s // tb,), name="mix_bwd",
        in_specs=[_rows(tb, d), _rows(tb, A_WIDTH), _rows(tb, CONV_CH), _rows(tb, A_WIDTH), _rows(tb, A_WIDTH),
                  _whole((1, A_WIDTH)), _whole((1, MIX_WIDTH)), _layer((N_CHIPS, MIX_WIDTH // N_CHIPS, d), l)],
        out_specs=[_rows(tb, MIX_WIDTH), _rows(tb, A_WIDTH), _rows(tb, A_WIDTH), _rows(tb, A_WIDTH),
                   _rows(tb, A_WIDTH), _rows(tb, CONV_CH), _whole((HALO, MIX_WIDTH)), _whole((HALO, A_WIDTH))],
        out_shape=[jax.ShapeDtypeStruct((s, MIX_WIDTH), BF), jax.ShapeDtypeStruct((s, A_WIDTH), F32),
                   jax.ShapeDtypeStruct((s, A_WIDTH), F32), jax.ShapeDtypeStruct((s, A_WIDTH), F32),
                   jax.ShapeDtypeStruct((s, A_WIDTH), F32), jax.ShapeDtypeStruct((s, CONV_CH), F32),
                   jax.ShapeDtypeStruct((HALO, MIX_WIDTH), F32), jax.ShapeDtypeStruct((HALO, A_WIDTH), F32)],
        compiler_params=_cparams("arbitrary"),
    )(dx1, ya, yb, yc, lse_c, sink_row, gg, wo_all)


def _attn_bwd(z, dy, lse, dd, dil, kw, kcol, vcol, n_rep, max_dist, name):
    s, zw = z.shape
    grid = _p_grid(s, dil)
    n_kv = N_HEADS // n_rep
    dt = F32 if dil == 1 else BF

    def body(q_ref, kp_ref, kc_ref, vp_ref, vc_ref, dy_ref, lse_ref, dd_ref, dq_ref, dkp_ref, dkc_ref, dvp_ref, dvc_ref):
        mask = _band_mask(pl.program_id(len(grid) - 1), dil, max_dist)
        for kh in range(n_kv):
            k2 = jnp.concatenate([_ld(kp_ref, _hs(kh)), _ld(kc_ref, _hs(kh))], axis=0).astype(BF)
            v2 = jnp.concatenate([_ld(vp_ref, _hs(kh)), _ld(vc_ref, _hs(kh))], axis=0).astype(BF)
            dk2 = jnp.zeros((2 * TQ, HEAD_DIM), F32)
            dv2 = jnp.zeros((2 * TQ, HEAD_DIM), F32)
            for h in range(kh * n_rep, (kh + 1) * n_rep):
                q = (_ld(q_ref, _hs(h)) * SCALE).astype(BF)
                lse_h = _ld(lse_ref, slice(h * HEAD_DIM, h * HEAD_DIM + 1))
                dd_h = _ld(dd_ref, slice(h * HEAD_DIM, h * HEAD_DIM + 1))
                dyh = _ld(dy_ref, _hs(h)).astype(BF)
                sc = jnp.where(mask, _nt(q, k2), NEG)
                p = jnp.exp(sc - lse_h)
                dp = _nt(dyh, v2)
                ds = (p * (dp - dd_h)).astype(BF)
                _st(dq_ref, _hs(h), (_nn(ds, k2) * SCALE).astype(dq_ref.dtype))
                dk2 = dk2 + _tn(ds, q)
                dv2 = dv2 + _tn(p.astype(BF), dyh)
            _st(dkp_ref, _hs(kh), dk2[:TQ].astype(dt))
            _st(dkc_ref, _hs(kh), dk2[TQ:].astype(dt))
            _st(dvp_ref, _hs(kh), dv2[:TQ].astype(dt))
            _st(dvc_ref, _hs(kh), dv2[TQ:].astype(dt))

    args = [_strips(z)] * 5 + [_strips(a) for a in (dy, lse, dd)]
    in_specs = [_p_spec(dil, A_WIDTH, 0), _p_spec(dil, kw, kcol, True), _p_spec(dil, kw, kcol),
                _p_spec(dil, kw, vcol, True), _p_spec(dil, kw, vcol)] + [_p_spec(dil, A_WIDTH, 0)] * 3
    out_specs = [_p_spec(dil, A_WIDTH, 0)] + [_p_spec(dil, kw, 0)] * 4
    na = s // N_STRIPS
    out_shape = [jax.ShapeDtypeStruct((4, 4, na, A_WIDTH), dt)] + [jax.ShapeDtypeStruct((4, 4, na, kw), dt)] * 4
    res = pl.pallas_call(
        body, grid=grid, name=name, in_specs=in_specs, out_specs=out_specs, out_shape=out_shape,
        compiler_params=_cparams(*(("parallel",) * len(grid))),
    )(*args)
    return [res[0].reshape(s, A_WIDTH)] + [a.reshape(s, kw) for a in res[1:]]


DZ_TA = 16


def _dz_assemble(parts_a, parts_c, dyb, zb, cw):
    s = zb.shape[0]
    na = s // N_STRIPS
    nb = na // DZ_TA

    def ahead(w, k):
        return pl.BlockSpec((4, 4, DZ_TA, w), lambda i: (0, 0, jnp.minimum(i + k, nb - 1), 0))

    args, in_specs = [], []
    for dil, (dq, dkp, dkc, dvp, dvc) in zip(DILATIONS + (1,), parts_a + [parts_c]):
        w = dkp.shape[1]
        here = _strip_rows(DZ_TA, w)
        if dil == 1:
            args += [dq, dkp, dkp, dkc, dvp, dvp, dvc]
            in_specs += [_strip_rows(DZ_TA, A_WIDTH), here, ahead(w, 1), here, here, ahead(w, 1), here]
        else:
            k = 8 * dil // DZ_TA
            args += [dq, dkp, dkc, dvp, dvc]
            in_specs += [_strip_rows(DZ_TA, A_WIDTH), ahead(w, k), here, ahead(w, k), here]
    n_att = len(args)
    args = [_strips(a) for a in args] + [_strips(dyb), _strips(dyb), _strips(zb), _strips(zb), _strips(zb), cw]
    in_specs += [_strip_rows(DZ_TA, CONV_CH), _next_rows(DZ_TA, CONV_CH, nb), _strip_rows(DZ_TA, ZB_W),
                 _prev_rows(DZ_TA, ZB_W), _next_rows(DZ_TA, ZB_W, nb), _whole((HALO, CONV_CH))]

    def body(*refs):
        att = list(refs[:n_att])
        dyb_ref, dybn_ref, zb_ref, zbp_ref, zbn_ref, cw_ref, dz_ref, dcw_ref = refs[n_att:]
        i = pl.program_id(0)

        @pl.when(i == 0)
        def _():
            dcw_ref[...] = jnp.zeros_like(dcw_ref)

        def shifted(dil):
            if dil == 1:
                dq_r, kp0, kp1, dkc_r, vp0, vp1, dvc_r = [att.pop(0) for _ in range(7)]
                live = i + 1 < nb
                half = DZ_TA // 2
                dkp = jnp.concatenate([kp0[:, :, half:, :], jnp.where(live, kp1[:, :, :half, :], 0.0)], axis=2)
                dvp = jnp.concatenate([vp0[:, :, half:, :], jnp.where(live, vp1[:, :, :half, :], 0.0)], axis=2)
            else:
                dq_r, dkp_r, dkc_r, dvp_r, dvc_r = [att.pop(0) for _ in range(5)]
                live = i + 8 * dil // DZ_TA < nb
                dkp = jnp.where(live, dkp_r[...].astype(F32), 0.0)
                dvp = jnp.where(live, dvp_r[...].astype(F32), 0.0)
            return dq_r[...].astype(F32), dkc_r[...].astype(F32) + dkp, dvc_r[...].astype(F32) + dvp

        dq, dk, dv = shifted(DILATIONS[0])
        for dil in DILATIONS[1:]:
            dq2, dk2, dv2 = shifted(dil)
            dq, dk, dv = dq + dq2, dk + dk2, dv + dv2
        dz_ref[:, :, :, 0:A_WIDTH] = dq.astype(BF)
        dz_ref[:, :, :, A_WIDTH:2 * A_WIDTH] = dk.astype(BF)
        dz_ref[:, :, :, 2 * A_WIDTH:ZA_W] = dv.astype(BF)
        dq, dk, dv = shifted(1)
        c0 = ZA_W + ZB_W
        dz_ref[:, :, :, c0:c0 + A_WIDTH] = dq.astype(BF)
        dz_ref[:, :, :, c0 + A_WIDTH:c0 + A_WIDTH + C_KV_WIDTH] = dk.astype(BF)
        dz_ref[:, :, :, c0 + A_WIDTH + C_KV_WIDTH:IN_WIDTH] = dv.astype(BF)

        cw = cw_ref[...]
        prev = jnp.where(i > 0, zbp_ref[...], 0.0)
        gb, gc, xb, u, u1, u2, c = _conv_strips(zb_ref[...], prev, cw)
        dyb = dyb_ref[...]
        dc = [_strip(dyb, b) * gb[b] for b in range(N_STRIPS)]
        dcn = jnp.where(i + 1 < nb, dybn_ref[...] * zbn_ref[:, :, :CONV_CH], 0.0)
        wrapped = [_shift_up(dc[0], 1, dcn[0]), _shift_up(dc[1], 1, dcn[1])]
        upd = [jnp.zeros((1, CONV_CH), F32)] * 3
        for b in range(N_STRIPS):
            dc1 = dc[b + 1] if b + 1 < N_STRIPS else wrapped[0]
            dc2 = dc[b + 2] if b + 2 < N_STRIPS else wrapped[b + 2 - N_STRIPS]
            du = cw[2:3, :] * dc[b] + cw[1:2, :] * dc1 + cw[0:1, :] * dc2
            f, e = b % 4, b // 4
            dz_ref[f, e, :, ZA_W:ZA_W + CONV_CH] = (_strip(dyb, b) * c[b]).astype(BF)
            dz_ref[f, e, :, ZA_W + CONV_CH:ZA_W + 2 * CONV_CH] = (du * xb[b]).astype(BF)
            dz_ref[f, e, :, ZA_W + 2 * CONV_CH:c0] = (du * gc[b]).astype(BF)
            for t, uu in enumerate((u2[b], u1[b], u[b])):
                upd[t] = upd[t] + jnp.sum(dc[b] * uu, axis=0, keepdims=True)
        row = lax.broadcasted_iota(jnp.int32, (HALO, CONV_CH), 0)
        tile = jnp.zeros((HALO, CONV_CH), F32)
        for t in range(3):
            tile = jnp.where(row == t, upd[t], tile)
        dcw_ref[...] += tile

    dz, dcw = pl.pallas_call(
        body, grid=(nb,), name="dz_assemble", in_specs=in_specs,
        out_specs=[_strip_rows(DZ_TA, IN_WIDTH), _whole((HALO, CONV_CH))],
        out_shape=[jax.ShapeDtypeStruct((4, 4, na, IN_WIDTH), BF), jax.ShapeDtypeStruct((HALO, CONV_CH), F32)],
        compiler_params=_cparams("arbitrary"),
    )(*args)
    return dz.reshape(s, IN_WIDTH), dcw


def _qkv_bwd(dz, dx1, x, g, w_all, l, tb, tokens_out):
    s, d = x.shape
    na, ta = s // N_STRIPS, tb // N_STRIPS

    def body(dz_ref, dx1_ref, x_ref, g_ref, w_ref, dx_ref, dg_ref):
        i = pl.program_id(0)

        @pl.when(i == 0)
        def _():
            dg_ref[...] = jnp.zeros_like(dg_ref)

        n = IN_WIDTH // N_CHIPS
        dz = dz_ref[...].reshape(tb, IN_WIDTH)
        dh = _nt(dz[:, 0:n], w_ref[0])
        for k in range(1, N_CHIPS):
            dh = dh + _nt(dz[:, k * n:(k + 1) * n], w_ref[k])
        xv = x_ref[...].reshape(tb, d)
        r = _rms_scale(xv)
        xhat = xv * r
        dg_ref[...] += jnp.sum(dh * xhat, axis=0, keepdims=True)
        dx = (dx1_ref[...].reshape(tb, d) + _norm_bwd(dh * g_ref[...], xhat, r)).reshape(4, 4, ta, d)
        if tokens_out:
            for b in range(N_STRIPS):
                dx_ref[:, b, :] = _strip(dx, b)
        else:
            dx_ref[...] = dx

    if tokens_out:
        dx_spec, dx_shape = pl.BlockSpec((ta, N_STRIPS, d), lambda i: (i, 0, 0)), (na, N_STRIPS, d)
    else:
        dx_spec, dx_shape = _strip_rows(ta, d), (4, 4, na, d)
    dx, dg = pl.pallas_call(
        body, grid=(s // tb,), name="qkv_bwd",
        in_specs=[_strip_rows(ta, IN_WIDTH), _strip_rows(ta, d), _strip_rows(ta, d), _whole((1, d)),
                  _layer((N_CHIPS, d, IN_WIDTH // N_CHIPS), l)],
        out_specs=[dx_spec, _whole((HALO, d))],
        out_shape=[jax.ShapeDtypeStruct(dx_shape, F32), jax.ShapeDtypeStruct((HALO, d), F32)],
        compiler_params=_cparams("arbitrary"),
    )(_strips(dz), _strips(dx1), _strips(x), g, w_all)
    return dx.reshape(s, d), dg


def _tile_rows(rows):
    return jnp.pad(rows, ((0, HALO - rows.shape[0]), (0, 0)))


def _to_strips(a, after, name):
    s, d = a.shape
    na = s // N_STRIPS
    ta = min(32, na)

    def body(a_ref, after_ref, o_ref):
        for b in range(N_STRIPS):
            o_ref[b % 4, b // 4] = a_ref[:, b, :]

    return pl.pallas_call(
        body, grid=(na // ta,), name=name,
        in_specs=[pl.BlockSpec((ta, N_STRIPS, d), lambda i: (i, 0, 0)), ANY], out_specs=_strip_rows(ta, d),
        out_shape=jax.ShapeDtypeStruct((4, 4, na, d), a.dtype), compiler_params=_cparams("parallel"),
    )(a.reshape(na, N_STRIPS, d), after).reshape(s, d)


def _local_step(x, tgt, fetch, ff, sinks, g_mix, g_group, g_mlp, g_final, emit):
    s, d = x.shape
    depth = g_mix.shape[0]
    tb = min(512, s)
    tf = ff // N_CHIPS
    ts = min(1024, s)
    saved = []
    for l in range(depth):
        w_in, _, _, _, conv_w = fetch(0, l, x)
        cw = _tile_rows(conv_w[l])
        sk = jnp.repeat(sinks[l].reshape(N_HEADS), HEAD_DIM)[None]
        h, za, zb, zc = _qkv_fwd(x, g_mix[l][None], w_in, l, tb)
        parts_a = [_attn_fwd(za, dil, A_WIDTH, 1, 2, 1, A_MAX_DIST, "attn_a_fwd_%d" % dil) for dil in DILATIONS]
        part_c = _attn_fwd(zc, 1, C_KV_WIDTH, 3, 4, C_GROUP, C_MAX_DIST, "attn_c_fwd")
        ya, lse_a, yc, lse_c = _attn_merge(parts_a, part_c, sk, tb)
        w_in, w_o, w1, w2, _ = fetch(1, l, yc)
        x1, yb = _mix_fwd(x, ya, yc, zb, cw, g_group[l][None], w_o, l, tb)
        x2, h2, ap = _mlp_fwd(x1, g_mlp[l][None], w1, w2, l, ts, tf)
        saved.append((x, h, za, zb, zc, ya, lse_a, yc, lse_c, yb, x1, h2, ap, cw, sk))
        x = x2
    dx, loss_tile, dg_final = _loss_head(x, g_final[None], tgt, tb)
    grads = [None] * depth
    tok = jnp.zeros((), F32)
    for l in reversed(range(depth)):
        x0, h, za, zb, zc, ya, lse_a, yc, lse_c, yb, x1, h2, ap, cw, sk = saved[l]
        dx1, dap, dg_mlp = _mlp_bwd(dx, x1, ap, g_mlp[l][None] + tok, w1, w2, l, ts, tf)
        tok = emit(l, 3, _wgrad(ap, dx, min(1024, ff), d, ts, "wgrad_ff_out", relu2=True))
        tok = tok + emit(l, 2, _wgrad(h2, dap, d, min(1024, ff), 2 * ts, "wgrad_ff_in"))
        n, dya, dyc, dd_a, dd_c, dyb, dg_group, dsink = _mix_bwd(dx1, ya, yb, yc, lse_c, sk, g_group[l][None] + tok,
                                                                 w_o, l, tb)
        tok = emit(l, 1, _wgrad(n, dx1, MIX_WIDTH, d, ts, "wgrad_o"))
        cw = cw + tok
        parts_a = [_attn_bwd(za, dya, lse_a, dd_a, dil, A_WIDTH, 1, 2, 1, A_MAX_DIST, "attn_a_bwd_%d" % dil)
                   for dil in DILATIONS]
        parts_c = _attn_bwd(zc, dyc, lse_c, dd_c, 1, C_KV_WIDTH, 3, 4, C_GROUP, C_MAX_DIST, "attn_c_bwd")
        dz, dcw = _dz_assemble(parts_a, parts_c, dyb, zb, cw)
        tok = emit(l, 0, _wgrad(h, dz, d, IN_WIDTH // 4, 2 * ts, "wgrad_in"))
        dx, dg_mix = _qkv_bwd(dz, dx1, x0, g_mix[l][None] + tok, w_in, l, tb, l == 0)
        grads[l] = (dcw, dsink, dg_mix, dg_group, dg_mlp)
    return loss_tile, dx, grads, dg_final


ANY = pl.BlockSpec(memory_space=pl.ANY)
SHARD_AXES = (2, 1, 2, 1)
N_BIG = len(SHARD_AXES)
N_CHIPS = 4
N_DEV = 8


def _mesh_pos():
    return lax.axis_index("x"), lax.axis_index("y"), lax.axis_index("c")


def _flip(v, bit):
    return 1 - v if bit else v


def _place_shard(shard, chip_arr, name):
    _, rows, cols = shard.shape
    tr = min(256, rows)

    def body(chip_ref, x_ref, o_ref):
        o_ref[...] = x_ref[...].astype(BF)

    return pl.pallas_call(
        body, name=name,
        grid_spec=pltpu.PrefetchScalarGridSpec(
            num_scalar_prefetch=1, grid=(2, rows // tr),
            in_specs=[pl.BlockSpec((None, tr, cols), lambda l, i, chip: (l, i, 0))],
            out_specs=pl.BlockSpec((None, None, tr, cols), lambda l, i, chip: (l, chip[0], i, 0))),
        out_shape=jax.ShapeDtypeStruct((2, N_CHIPS, rows, cols), BF),
        compiler_params=_cparams("parallel", "parallel"),
    )(chip_arr, shard)


HBM = pl.BlockSpec(memory_space=pltpu.HBM)
SEM = pl.BlockSpec(memory_space=pltpu.SEMAPHORE)
EFFECT = pltpu.SideEffectType.DATAFLOW_SIDE_EFFECTING

GATHER_GROUPS = (((0, 0),), ((1, 0), (2, 0), (3, 0)), ((0, 1),), ((1, 1), (2, 1), (3, 1)))
GATHER_STARTS = ((0,), (1,), (2, 3))


def _gather_copies(arrs, group, send_sems, recv_sems):
    x, y, c = _mesh_pos()
    me = 2 * x + y
    out = []
    for i, (w, layer) in enumerate(group):
        mine = arrs[w].at[layer, me]
        for j, (qx, qy) in enumerate([(1 - x, y), (x, 1 - y), (1 - x, 1 - y)]):
            landed = arrs[w].at[layer, 2 * qx + qy]
            out.append(tuple(pltpu.make_async_remote_copy(
                src_ref=piece, dst_ref=piece, send_sem=send_sems.at[i * 3 + j], recv_sem=recv_sems.at[i * 3 + j],
                device_id=(qx, qy, c), device_id_type=MESH) for piece in (mine, landed)))
    return out


def _conv_copies(conv_src, conv_dst, send_sems, recv_sems):
    x, y, c = _mesh_pos()
    out = []
    for j, (qx, qy) in enumerate([(1 - x, y), (x, 1 - y), (1 - x, 1 - y)]):
        out.append(tuple(pltpu.make_async_remote_copy(
            src_ref=conv_src, dst_ref=conv_dst.at[q], send_sem=send_sems.at[j], recv_sem=recv_sems.at[j],
            device_id=(qx, qy, c), device_id_type=MESH) for q in (2 * x + y, 2 * qx + qy)))
    return out


def _gather_start(groups, arrs, conv, name, through=None):
    n_sems = 2 * (len(groups) + (conv is not None))
    mats = sorted({w for g in groups for w, _ in GATHER_GROUPS[g]})

    def body(*refs):
        arrs_ref = [None] * N_BIG
        for w, ref in zip(mats, refs):
            arrs_ref[w] = ref
        sems = refs[n_in:n_in + n_sems]
        if conv is not None:
            for cp, _ in _conv_copies(refs[len(mats)], refs[len(mats) + 1], sems[-2], sems[-1]):
                cp.start()
        for k, g in enumerate(groups):
            for cp, _ in _gather_copies(arrs_ref, GATHER_GROUPS[g], sems[2 * k], sems[2 * k + 1]):
                cp.start()

    sem_shapes = []
    for n in [len(GATHER_GROUPS[g]) for g in groups] + ([1] if conv is not None else []):
        sem_shapes += [pltpu.SemaphoreType.DMA((3 * n,))] * 2
    operands = [arrs[w] for w in mats] + ([] if conv is None else list(conv)) + ([] if through is None else [through])
    n_in = len(operands)
    res = pl.pallas_call(
        body, name=name,
        out_shape=tuple(sem_shapes) + tuple(pltpu.HBM(a.shape, a.dtype) for a in operands),
        in_specs=(HBM,) * n_in, out_specs=(SEM,) * n_sems + (HBM,) * n_in,
        input_output_aliases={i: n_sems + i for i in range(n_in)},
        compiler_params=pltpu.CompilerParams(has_side_effects=EFFECT),
    )(*[pltpu.with_memory_space_constraint(a, pltpu.HBM) for a in operands])
    arrs = list(arrs)
    for w, a in zip(mats, res[n_sems:]):
        arrs[w] = a
    return res[:n_sems], arrs, list(res[n_sems + len(mats):])


def _gather_wait(k, sems, arrs, conv, after, name):
    group = GATHER_GROUPS[k]
    mats = sorted({w for w, _ in group})
    n_conv = 0 if conv is None else 2

    def body(*refs):
        local = refs[:len(mats)]
        arrs_ref = [None] * N_BIG
        for w, ref in zip(mats, local):
            arrs_ref[w] = ref
        pos = len(mats) + n_conv
        copies = _gather_copies(arrs_ref, group, refs[pos], refs[pos + 1])
        if conv is not None:
            copies += _conv_copies(refs[len(mats)], refs[len(mats) + 1], refs[pos + 2], refs[pos + 3])
        for send, recv in copies:
            recv.wait_recv()
            send.wait_send()

    operands = [arrs[w] for w in mats] + ([] if conv is None else [conv[1], conv[2]])
    sem_ops = list(sems) + ([] if conv is None else list(conv[0]))
    n_op = len(operands)
    res = pl.pallas_call(
        body, name=name, out_shape=tuple(pltpu.HBM(a.shape, a.dtype) for a in operands),
        in_specs=(HBM,) * n_op + (SEM,) * len(sem_ops) + (ANY,), out_specs=(HBM,) * n_op,
        input_output_aliases={i: i for i in range(n_op)},
        compiler_params=pltpu.CompilerParams(has_side_effects=EFFECT),
    )(*operands, *sem_ops, after)
    arrs = list(arrs)
    for w, a in zip(mats, res):
        arrs[w] = a
    return arrs, (res[-1] if conv is not None else None)


def _grad_shard(ref, w, chip, n):
    start = pl.multiple_of(chip * n, 128)
    if SHARD_AXES[w] == 2:
        return ref.at[:, pl.ds(start, n)]
    return ref.at[pl.ds(start, n), :]


def _slot_shape(g, w):
    shape = list(g.shape)
    shape[SHARD_AXES[w] - 1] //= N_CHIPS
    return (N_DEV - 1,) + tuple(shape)


def _scatter_copies(g_ref, land_ref, send_sems, recv_sems, layer, w):
    x, y, c = _mesh_pos()
    n = g_ref.shape[SHARD_AXES[w] - 1] // N_CHIPS
    out = []
    for r in range(1, N_DEV):
        tx, ty, tc = _flip(x, r & 4), _flip(y, r & 2), _flip(c, r & 1)
        cp = pltpu.make_async_remote_copy(
            src_ref=_grad_shard(g_ref, w, 2 * tx + ty, n), dst_ref=land_ref.at[r - 1], send_sem=send_sems.at[r - 1],
            recv_sem=recv_sems.at[r - 1], device_id=(tx, ty, tc), device_id_type=MESH)
        out.append((cp, (c != layer) if r & 1 else (c == layer)))
    return out


def _scatter_start(g, land, layer, w, name):
    def body(g_ref, land_ref, send_sems, recv_sems, g_thru, land_thru, token):
        for cp, mine in _scatter_copies(g_ref, land_ref, send_sems, recv_sems, layer, w):
            @pl.when(mine)
            def _():
                cp.start()
        token[...] = jnp.zeros_like(token)

    return pl.pallas_call(
        body, name=name,
        out_shape=(pltpu.SemaphoreType.DMA((N_DEV - 1,)), pltpu.SemaphoreType.DMA((N_DEV - 1,)),
                   pltpu.HBM(g.shape, g.dtype), pltpu.HBM(land.shape, land.dtype),
                   jax.ShapeDtypeStruct((HALO, 128), F32)),
        in_specs=(HBM, HBM), out_specs=(SEM, SEM, HBM, HBM, pl.BlockSpec(memory_space=pltpu.VMEM)),
        input_output_aliases={0: 2, 1: 3}, compiler_params=pltpu.CompilerParams(has_side_effects=EFFECT),
    )(pltpu.with_memory_space_constraint(g, pltpu.HBM), pltpu.with_memory_space_constraint(land, pltpu.HBM))


def _scatter_wait(started, land, after, w, name):
    def body(g0_ref, g1_ref, land_ref, ss0, rs0, ss1, rs1, after_ref, g0_out, g1_out, land_out):
        c = lax.axis_index("c")
        for layer, g_ref, ss, rs in ((0, g0_ref, ss0, rs0), (1, g1_ref, ss1, rs1)):
            for cp, mine in _scatter_copies(g_ref, land_ref, ss, rs, layer, w):
                @pl.when(mine)
                def _():
                    cp.wait_send()

                @pl.when(c == layer)
                def _():
                    cp.wait_recv()

    (ss0, rs0, g0), (ss1, rs1, g1) = started
    return pl.pallas_call(
        body, name=name,
        out_shape=(pltpu.HBM(g0.shape, g0.dtype), pltpu.HBM(g1.shape, g1.dtype), pltpu.HBM(land.shape, land.dtype)),
        in_specs=(HBM, HBM, HBM, SEM, SEM, SEM, SEM, ANY), out_specs=(HBM, HBM, HBM),
        input_output_aliases={0: 0, 1: 1, 2: 2}, compiler_params=pltpu.CompilerParams(has_side_effects=EFFECT),
    )(g0, g1, land, ss0, rs0, ss1, rs1, after)


def _sum_slots(g0, g1, slots, w, pos_arr, name):
    _, rows, cols = slots.shape
    tr = min(256, rows)
    nr = rows // tr
    if SHARD_AXES[w] == 2:
        own = pl.BlockSpec((tr, cols), lambda i, pos: (i, pos[0]))
    else:
        own = pl.BlockSpec((tr, cols), lambda i, pos: (pos[0] * nr + i, 0))

    def body(pos_ref, own0_ref, own1_ref, s_ref, o_ref):
        acc = jnp.where(pos_ref[1] == 0, own0_ref[...], own1_ref[...]).astype(F32)
        for r in range(N_DEV - 1):
            acc = acc + s_ref[r].astype(F32)
        o_ref[...] = acc

    return pl.pallas_call(
        body, name=name,
        grid_spec=pltpu.PrefetchScalarGridSpec(
            num_scalar_prefetch=1, grid=(nr,),
            in_specs=[own, own, pl.BlockSpec((N_DEV - 1, tr, cols), lambda i, pos: (0, i, 0))],
            out_specs=pl.BlockSpec((tr, cols), lambda i, pos: (i, 0))),
        out_shape=jax.ShapeDtypeStruct((rows, cols), F32), compiler_params=_cparams("parallel"),
    )(pos_arr, g0, g1, slots)


def _swap_layers(halves, name):
    n = len(halves)

    def body(*refs):
        srcs, dsts = refs[:n], refs[n:2 * n]
        send_sems, recv_sems = refs[2 * n:]
        x, y, c = _mesh_pos()
        sends = [pltpu.make_async_remote_copy(src_ref=srcs[w], dst_ref=dsts[w], send_sem=send_sems.at[w],
                                              recv_sem=recv_sems.at[w], device_id=(x, y, 1 - c), device_id_type=MESH)
                 for w in range(n)]
        for cp in sends:
            cp.start()
        for cp in sends:
            cp.wait_recv()
        for cp in sends:
            cp.wait_send()

    return pl.pallas_call(
        body, name=name, in_specs=[ANY] * n, out_specs=[ANY] * n,
        out_shape=[jax.ShapeDtypeStruct(h.shape, h.dtype) for h in halves],
        scratch_shapes=[pltpu.SemaphoreType.DMA((n,)), pltpu.SemaphoreType.DMA((n,))],
    )(*halves)


def _adamw_math(w, g, m, v):
    m = ADAM_B1 * m + (1.0 - ADAM_B1) * g
    v = ADAM_B2 * v + (1.0 - ADAM_B2) * jnp.square(g)
    m_hat = m / (1.0 - ADAM_B1 ** ADAM_STEP)
    v_hat = v / (1.0 - ADAM_B2 ** ADAM_STEP)
    delta = -ADAM_LR * (m_hat / (jnp.sqrt(v_hat) + ADAM_EPS) + ADAM_WD * w)
    return delta, m, v


def _adamw(w, g_own, g_other, m, v, pos_arr, name):
    shape = w.shape
    _, rows, cols = shape
    tr = min(256, rows)

    def body(pos_ref, w_ref, own_ref, other_ref, m_ref, v_ref, g_ref, d_ref, m2_ref, v2_ref):
        g = jnp.where(pl.program_id(0) == pos_ref[1], own_ref[...], other_ref[...])
        g_ref[...] = g
        d_ref[...], m2_ref[...], v2_ref[...] = _adamw_math(w_ref[...], g, m_ref[...], v_ref[...])

    full = pl.BlockSpec((None, tr, cols), lambda l, i, pos: (l, i, 0))
    half = pl.BlockSpec((tr, cols), lambda l, i, pos: (i, 0))
    return pl.pallas_call(
        body, name=name,
        grid_spec=pltpu.PrefetchScalarGridSpec(
            num_scalar_prefetch=1, grid=(2, rows // tr),
            in_specs=[full, half, half, full, full], out_specs=[full] * 4),
        out_shape=[jax.ShapeDtypeStruct(shape, F32)] * 4, compiler_params=_cparams("parallel", "parallel"),
    )(pos_arr, w, g_own, g_other, m, v)


def _small_sync(part, w, m, v):
    rows, cols = part.shape

    def body(p_ref, w_ref, m_ref, v_ref, g_ref, d_ref, m2_ref, v2_ref, slots, send_sems, recv_sems):
        x, y, c = _mesh_pos()
        me = 4 * x + 2 * y + c
        slots[me] = p_ref[...]
        sends = []
        for r in range(1, N_DEV):
            to = (_flip(x, r & 4), _flip(y, r & 2), _flip(c, r & 1))
            sends.append(pltpu.make_async_remote_copy(
                src_ref=p_ref, dst_ref=slots.at[me], send_sem=send_sems.at[r - 1], recv_sem=recv_sems.at[r - 1],
                device_id=to, device_id_type=MESH))
        for cp in sends:
            cp.start()
        for cp in sends:
            cp.wait_recv()
        for cp in sends:
            cp.wait_send()
        g = slots[0]
        for i in range(1, N_DEV):
            g = g + slots[i]
        g_ref[...] = g
        d_ref[...], m2_ref[...], v2_ref[...] = _adamw_math(w_ref[...], g, m_ref[...], v_ref[...])

    vm = pl.BlockSpec(memory_space=pltpu.VMEM)
    return pl.pallas_call(
        body, name="small_sync", in_specs=[vm] * 4, out_specs=[vm] * 4,
        out_shape=[jax.ShapeDtypeStruct((rows, cols), F32)] * 4,
        scratch_shapes=[pltpu.VMEM((N_DEV, rows, cols), F32), pltpu.SemaphoreType.DMA((N_DEV - 1,)),
                        pltpu.SemaphoreType.DMA((N_DEV - 1,))],
    )(part, w, m, v)


PACK_W = 256


def _pack_rows(n):
    return -(-n // (HALO * PACK_W)) * HALO


def _pack_small(parts):
    out = []
    for a in parts:
        flat = a.reshape(-1)
        out.append(jnp.pad(flat, (0, _pack_rows(flat.size) * PACK_W - flat.size)).reshape(-1, PACK_W))
    return jnp.concatenate(out, axis=0)


def _unpack_small(p, shapes):
    out, row = [], 0
    for shape in shapes:
        n = 1
        for k in shape:
            n *= k
        out.append(p[row:row + _pack_rows(n)].reshape(-1)[:n].reshape(shape))
        row += _pack_rows(n)
    return out


def kernel(x, w_in, conv_w, sinks, g_mix, g_group, w_o, g_mlp, w_ff_in, w_ff_out, g_final, loss_target, m_w_in, m_conv_w, m_sinks, m_g_mix, m_g_group, m_w_o, m_g_mlp, m_w_ff_in, m_w_ff_out, m_g_final, v_w_in, v_conv_w, v_sinks, v_g_mix, v_g_group, v_w_o, v_g_mlp, v_w_ff_in, v_w_ff_out, v_g_final):
    chip = 2 * lax.axis_index("x") + lax.axis_index("y")
    conv_n = conv_w.shape[2]

    pos_arr = jnp.stack([chip, lax.axis_index("c")]).astype(jnp.int32)
    shards = (w_in, w_o, w_ff_in, w_ff_out)
    conv_tile = jnp.pad(conv_w.reshape(6, conv_n), ((0, HALO - 6), (0, 128 - conv_n)))
    placed = [_place_shard(w_in, pos_arr[:1], "place_shard_0"), None, None, None]
    sems_a, placed, conv_thru = _gather_start(
        GATHER_STARTS[0], placed, (conv_tile, lax.empty((N_CHIPS,) + conv_tile.shape, conv_tile.dtype)),
        "gather_start_0")
    for i in range(1, N_BIG):
        placed[i] = _place_shard(shards[i], pos_arr[:1], "place_shard_%d" % i)
    full = {"arrs": placed, "conv": None, "sems": list(sems_a[:2])}

    def fetch(stage, layer, after):
        k = 2 * layer + stage
        sems = full["sems"][2 * k:2 * k + 2]
        if k == 0:
            full["arrs"], land = _gather_wait(0, sems, full["arrs"], (sems_a[-2:], *conv_thru), after, "gather_wait_0")
            conv_all = lax.dynamic_update_slice(land, conv_tile[None], (chip, 0, 0))
            full["conv"] = conv_all[:, :6, :conv_n].reshape(N_CHIPS, 2, 3, conv_n).transpose(1, 2, 0, 3).reshape(
                2, 3, CONV_CH)
            sems_b, full["arrs"], rest = _gather_start(GATHER_STARTS[1], full["arrs"], None, "gather_start_1",
                                                       through=full["arrs"][0])
            full["arrs"][0] = rest[-1]
            full["sems"] += list(sems_b)
        else:
            full["arrs"], _ = _gather_wait(k, sems, full["arrs"], None, after, "gather_wait_%d" % k)
        if k == 1:
            sems_c, full["arrs"], _ = _gather_start(GATHER_STARTS[2], full["arrs"], None, "gather_start_2")
            full["sems"] += list(sems_c)
        return (*full["arrs"], full["conv"])

    lands, started = [None] * N_BIG, {}

    def emit(layer, w, g):
        if lands[w] is None:
            lands[w] = lax.empty(_slot_shape(g, w), g.dtype)
        *started[layer, w], lands[w], token = _scatter_start(g, lands[w], layer, w, "scatter_start_%d_%d" % (layer, w))
        return token[0, 0]

    loss_tile, dx, grads, dg_final = _local_step(_to_strips(x[0], placed[0], "to_strips_x"),
                                                 _to_strips(loss_target[0], placed[0], "to_strips_target"), fetch,
                                                 w_ff_in.shape[2] * N_CHIPS,
                                                 sinks, g_mix, g_group, g_mlp, g_final, emit)

    wmv = ((w_in, m_w_in, v_w_in), (w_o, m_w_o, v_w_o), (w_ff_in, m_w_ff_in, v_w_ff_in),
           (w_ff_out, m_w_ff_out, v_w_ff_out))
    big, after = [None] * N_BIG, dx
    for name, ws in (("swap_layers_rest", (1, 2, 3)), ("swap_layers_in", (0,))):
        own = []
        for w in ws:
            g0, g1, slots = _scatter_wait((started[0, w], started[1, w]), lands[w], after, w, "scatter_wait_%d" % w)
            own.append(_sum_slots(g0, g1, slots, w, pos_arr, "sum_slots_%d" % w))
        for w, mine, theirs in zip(ws, own, _swap_layers(own, name)):
            big[w] = _adamw(wmv[w][0], mine, theirs, wmv[w][1], wmv[w][2], pos_arr, "adamw_%d" % w)
        after = big[ws[-1]][1]

    def both(i):
        return jnp.stack([grads[0][i][0], grads[1][i][0]])
    dconv = jnp.stack([grads[0][0][:3], grads[1][0][:3]])
    dsinks = jnp.stack([grads[0][1][0, ::HEAD_DIM], grads[1][1][0, ::HEAD_DIM]])
    part = _pack_small([both(2), both(3), both(4), dg_final[0], dconv, dsinks, loss_tile[0, 0]])

    def spread(shard):
        return lax.dynamic_update_slice(jnp.zeros((2, 3, CONV_CH), F32), shard, (0, 0, chip * conv_n))
    zero = jnp.zeros((), F32)
    packs = [_pack_small([a, b, c_, e, spread(f), g_, zero]) for a, b, c_, e, f, g_ in (
        (g_mix, g_group, g_mlp, g_final, conv_w, sinks),
        (m_g_mix, m_g_group, m_g_mlp, m_g_final, m_conv_w, m_sinks),
        (v_g_mix, v_g_group, v_g_mlp, v_g_final, v_conv_w, v_sinks))]
    shapes = [g_mix.shape, g_group.shape, g_mlp.shape, g_final.shape, (2, 3, CONV_CH), sinks.shape, ()]
    small = [_unpack_small(p, shapes) for p in _small_sync(part, *packs)]

    def shard_of(full):
        return lax.dynamic_slice(full, (0, 0, chip * conv_n), (2, 3, conv_n))
    small = [(s[0], s[1], s[2], s[3], shard_of(s[4]), s[5], s[6]) for s in small]
    loss = small[0][6]

    def ordered(kind):
        b = [big[i][kind] for i in range(N_BIG)]
        s = small[kind]
        return [b[0], s[4], s[5], s[0], s[1], b[1], s[2], b[2], b[3], s[3]]

    return (loss, dx[None], *ordered(0), *ordered(1), *ordered(2), *ordered(3))
```

```python
import functools

import jax
import jax.numpy as jnp
from jax import lax
from jax.experimental import pallas as pl
from jax.experimental.pallas import tpu as pltpu

HEAD_DIM = 64
N_HEADS = 6
C_GROUP = 3
A_WIDTH = N_HEADS * HEAD_DIM
C_KV_WIDTH = 2 * HEAD_DIM
CONV_CH = 256
ZA_W = 3 * A_WIDTH
ZB_W = 3 * CONV_CH
ZC_W = A_WIDTH + 2 * C_KV_WIDTH
IN_WIDTH = ZA_W + ZB_W + ZC_W
MIX_WIDTH = A_WIDTH + CONV_CH + A_WIDTH
DILATIONS = (1, 4, 16)
A_MAX_DIST = 128
C_MAX_DIST = 127
TQ = 128
EPS = 1e-6
SCALE = HEAD_DIM ** -0.5
NEG = -1e30
HALO = 8

ADAM_LR = 0.001
ADAM_B1 = 0.9
ADAM_B2 = 0.999
ADAM_EPS = 1e-08
ADAM_WD = 0.01
ADAM_STEP = 10

BF = jnp.bfloat16
F32 = jnp.float32
MESH = pl.DeviceIdType.MESH
VMEM_LIMIT = 56 * 1024 * 1024


def _cparams(*sem):
    return pltpu.CompilerParams(dimension_semantics=sem, vmem_limit_bytes=VMEM_LIMIT)


def _nt(a, b):
    return lax.dot_general(a, b, (((1,), (1,)), ((), ())), preferred_element_type=F32)


def _tn(a, b):
    return lax.dot_general(a, b, (((0,), (0,)), ((), ())), preferred_element_type=F32)


def _nn(a, b):
    return jnp.dot(a, b, preferred_element_type=F32)


def _rows(tb, w):
    return pl.BlockSpec((tb, w), lambda i: (i, 0))


def _whole(shape):
    return pl.BlockSpec(shape, lambda *_: (0,) * len(shape))


def _layer(shape, l):
    return pl.BlockSpec((None,) + shape, lambda *_: (l,) + (0,) * len(shape))


def _rms_scale(v):
    return lax.rsqrt(jnp.mean(v * v, axis=-1, keepdims=True) + EPS)


def _norm_bwd(dxhat, xhat, r):
    return r * (dxhat - xhat * jnp.mean(dxhat * xhat, axis=-1, keepdims=True))


def _qkv_fwd(x, g, w_all, l, tb):
    s, d = x.shape

    def body(x_ref, g_ref, w_ref, h_ref, za_ref, zb_ref, zc_ref):
        xv = x_ref[...]
        h = ((xv * _rms_scale(xv)) * g_ref[...]).astype(BF)
        h_ref[...] = h
        z = jnp.concatenate([_nn(h, w_ref[k]) for k in range(N_CHIPS)], axis=1)
        za_ref[...] = z[:, :ZA_W]
        zb_ref[...] = z[:, ZA_W:ZA_W + ZB_W]
        zc_ref[...] = z[:, ZA_W + ZB_W:]

    return pl.pallas_call(
        body, grid=(s // tb,), name="qkv_fwd",
        in_specs=[_rows(tb, d), _whole((1, d)), _layer((N_CHIPS, d, IN_WIDTH // N_CHIPS), l)],
        out_specs=[_rows(tb, d), _rows(tb, ZA_W), _rows(tb, ZB_W), _rows(tb, ZC_W)],
        out_shape=[jax.ShapeDtypeStruct((s, d), BF), jax.ShapeDtypeStruct((s, ZA_W), F32),
                   jax.ShapeDtypeStruct((s, ZB_W), F32), jax.ShapeDtypeStruct((s, ZC_W), F32)],
        compiler_params=_cparams("parallel"),
    )(x, g, w_all)


N_STRIPS = 16


def _strips(a):
    s, w = a.shape
    return a.reshape(4, 4, s // N_STRIPS, w)


def _p_grid(s, dil):
    na = s // N_STRIPS
    return {16: (4, 4, na // TQ), 4: (4, na // 32), 1: (na // 8,)}[dil]


def _p_spec(dil, cw, col, prev=False):
    def blk(j):
        return jnp.maximum(j - 1, 0) if prev else j
    if dil == 16:
        return pl.BlockSpec((None, None, TQ, cw), lambda f, e, j: (f, e, blk(j), col))
    if dil == 4:
        return pl.BlockSpec((None, 4, 32, cw), lambda f, j: (f, 0, blk(j), col))
    return pl.BlockSpec((4, 4, 8, cw), lambda j: (0, 0, blk(j), col))


def _block_pos(i, dil):
    if dil == 16:
        return i
    if dil == 4:
        return 4 * (i % 32) + i // 32
    return 16 * (i % 8) + 4 * ((i // 8) % 4) + i // 32


def _band_mask(b, dil, max_dist):
    qi = _block_pos(lax.broadcasted_iota(jnp.int32, (TQ, 2 * TQ), 0), dil)
    col = lax.broadcasted_iota(jnp.int32, (TQ, 2 * TQ), 1)
    cur = col >= TQ
    dist = qi - _block_pos(col % TQ, dil) + jnp.where(cur, 0, TQ)
    return (dist >= 0) & (dist <= max_dist) & (cur | (b > 0))


def _hs(h):
    return slice(h * HEAD_DIM, (h + 1) * HEAD_DIM)


def _ld(ref, cols):
    v = ref[..., cols]
    return v.reshape(TQ, v.shape[-1])


def _st(ref, cols, val):
    ref[..., cols] = val.reshape(ref.shape[:-1] + (val.shape[-1],))


def _attn_fwd(z, dil, kw, kcol, vcol, n_rep, max_dist, name):
    s, zw = z.shape
    grid = _p_grid(s, dil)

    def body(q_ref, kp_ref, kc_ref, vp_ref, vc_ref, acc_ref, m_ref, l_ref):
        mask = _band_mask(pl.program_id(len(grid) - 1), dil, max_dist)
        scs, v2s = [], []
        for kh in range(N_HEADS // n_rep):
            k2 = jnp.concatenate([_ld(kp_ref, _hs(kh)), _ld(kc_ref, _hs(kh))], axis=0).astype(BF)
            v2s.append(jnp.concatenate([_ld(vp_ref, _hs(kh)), _ld(vc_ref, _hs(kh))], axis=0).astype(BF))
            for h in range(kh * n_rep, (kh + 1) * n_rep):
                q = (_ld(q_ref, _hs(h)) * SCALE).astype(BF)
                scs.append(jnp.where(mask, _nt(q, k2), NEG))
        for h, sc in enumerate(scs):
            m = jnp.max(sc, axis=1, keepdims=True)
            p = jnp.exp(sc - m)
            _st(acc_ref, _hs(h), _nn(p.astype(BF), v2s[h // n_rep]))
            _st(m_ref, _hs(h), jnp.broadcast_to(m, (TQ, HEAD_DIM)))
            _st(l_ref, _hs(h), jnp.broadcast_to(jnp.sum(p, axis=1, keepdims=True), (TQ, HEAD_DIM)))

    res = pl.pallas_call(
        body, grid=grid, name=name,
        in_specs=[_p_spec(dil, A_WIDTH, 0), _p_spec(dil, kw, kcol, True), _p_spec(dil, kw, kcol),
                  _p_spec(dil, kw, vcol, True), _p_spec(dil, kw, vcol)],
        out_specs=[_p_spec(dil, A_WIDTH, 0)] * 3,
        out_shape=[jax.ShapeDtypeStruct((4, 4, s // N_STRIPS, A_WIDTH), F32)] * 3,
        compiler_params=_cparams(*(("parallel",) * len(grid))),
    )(*[_strips(z)] * 5)
    return [a.reshape(s, A_WIDTH) for a in res]


def _attn_merge(parts_a, part_c, sink_row, tb):
    s = part_c[0].shape[0]
    n_a = len(parts_a)

    def body(*refs):
        ins, sink_ref = refs[:3 * n_a + 3], refs[3 * n_a + 3]
        ya_ref, lsea_ref, yc_ref, lsec_ref = refs[3 * n_a + 4:]
        ms = [ins[3 * p + 1][...] for p in range(n_a)]
        m = functools.reduce(jnp.maximum, ms)
        acc = jnp.zeros_like(m)
        l = jnp.zeros_like(m)
        for p in range(n_a):
            w = jnp.exp(ms[p] - m)
            acc = acc + w * ins[3 * p][...]
            l = l + w * ins[3 * p + 2][...]
        ya_ref[...] = acc / l
        lsea_ref[...] = m + jnp.log(l)
        acc_c, m_c, l_c = [r[...] for r in ins[3 * n_a:]]
        sk = sink_ref[...]
        m2 = jnp.maximum(m_c, sk)
        w = jnp.exp(m_c - m2)
        l2 = w * l_c + jnp.exp(sk - m2)
        yc_ref[...] = (w * acc_c) / l2
        lsec_ref[...] = m2 + jnp.log(l2)

    return pl.pallas_call(
        body, grid=(s // tb,), name="attn_merge",
        in_specs=[_rows(tb, A_WIDTH)] * (3 * n_a + 3) + [_whole((1, A_WIDTH))],
        out_specs=[_rows(tb, A_WIDTH)] * 4, out_shape=[jax.ShapeDtypeStruct((s, A_WIDTH), F32)] * 4,
        compiler_params=_cparams("parallel"),
    )(*[a for part in parts_a + [part_c] for a in part], sink_row)


def _shift_down(v, n, halo):
    rows = v.shape[0]
    out = pltpu.roll(v, n, 0)
    row = lax.broadcasted_iota(jnp.int32, v.shape, 0)
    for t in range(n):
        out = jnp.where(row == t, halo[HALO - n + t:HALO - n + t + 1, :], out)
    return out


def _shift_up(v, n, halo):
    rows = v.shape[0]
    out = pltpu.roll(v, rows - n, 0)
    row = lax.broadcasted_iota(jnp.int32, v.shape, 0)
    for t in range(n):
        out = jnp.where(row == rows - n + t, halo[t:t + 1, :], out)
    return out


def _strip(v, b):
    return v[b % 4, b // 4]


def _conv_strips(zb, prev, cw):
    gb = [_strip(zb, b)[:, :CONV_CH] for b in range(N_STRIPS)]
    gc = [_strip(zb, b)[:, CONV_CH:2 * CONV_CH] for b in range(N_STRIPS)]
    xb = [_strip(zb, b)[:, 2 * CONV_CH:] for b in range(N_STRIPS)]
    u = [g * v for g, v in zip(gc, xb)]
    uh = prev[:, :, CONV_CH:2 * CONV_CH] * prev[:, :, 2 * CONV_CH:]
    wrapped = {14: _shift_down(u[14], 1, uh[2]), 15: _shift_down(u[15], 1, uh[3])}
    u1 = [u[b - 1] if b >= 1 else wrapped[15] for b in range(N_STRIPS)]
    u2 = [u[b - 2] if b >= 2 else wrapped[14 + b] for b in range(N_STRIPS)]
    c = [cw[0:1, :] * u2[b] + cw[1:2, :] * u1[b] + cw[2:3, :] * u[b] for b in range(N_STRIPS)]
    return gb, gc, xb, u, u1, u2, c


def _strip_rows(ta, w):
    return pl.BlockSpec((4, 4, ta, w), lambda i: (0, 0, i, 0))


def _prev_rows(ta, w):
    return pl.BlockSpec((4, None, HALO, w), lambda i: (0, 3, jnp.maximum(i * (ta // HALO) - 1, 0), 0))


def _next_rows(ta, w, nblk):
    return pl.BlockSpec((4, None, HALO, w),
                        lambda i: (0, 0, jnp.minimum((i + 1) * (ta // HALO), nblk * (ta // HALO) - 1), 0))


def _mix_fwd(x, ya, yc, zb, cw, gg, wo_all, l, tb):
    s, d = x.shape
    ta = tb // N_STRIPS

    def body(x_ref, ya_ref, yc_ref, zb_ref, zbp_ref, cw_ref, gg_ref, wo_ref, x1_ref, yb_ref):
        i = pl.program_id(0)
        prev = jnp.where(i > 0, zbp_ref[...], 0.0)
        gb, _, _, _, _, _, c = _conv_strips(zb_ref[...], prev, cw_ref[...])
        for b in range(N_STRIPS):
            yb_ref[b % 4, b // 4] = gb[b] * c[b]
        yb = yb_ref[...].reshape(tb, CONV_CH)
        ya, yc = ya_ref[...].reshape(tb, A_WIDTH), yc_ref[...].reshape(tb, A_WIDTH)
        n = jnp.concatenate([ya * _rms_scale(ya), yb * _rms_scale(yb), yc * _rms_scale(yc)], axis=1)
        n = (n * gg_ref[...]).astype(BF)
        x1 = x_ref[...].reshape(tb, d) + _nn(n, wo_ref[...].reshape(MIX_WIDTH, d))
        x1_ref[...] = x1.reshape(4, 4, ta, d)

    res = pl.pallas_call(
        body, grid=(s // tb,), name="mix_fwd",
        in_specs=[_strip_rows(ta, d), _strip_rows(ta, A_WIDTH), _strip_rows(ta, A_WIDTH), _strip_rows(ta, ZB_W),
                  _prev_rows(ta, ZB_W), _whole((HALO, CONV_CH)), _whole((1, MIX_WIDTH)),
                  _layer((N_CHIPS, MIX_WIDTH // N_CHIPS, d), l)],
        out_specs=[_strip_rows(ta, d), _strip_rows(ta, CONV_CH)],
        out_shape=[jax.ShapeDtypeStruct((4, 4, s // N_STRIPS, d), F32),
                   jax.ShapeDtypeStruct((4, 4, s // N_STRIPS, CONV_CH), F32)],
        compiler_params=_cparams("parallel"),
    )(_strips(x), _strips(ya), _strips(yc), _strips(zb), _strips(zb), cw, gg, wo_all)
    return res[0].reshape(s, d), res[1].reshape(s, CONV_CH)


def _mlp_fwd(x1, g, w1_all, w2_all, l, tb, tf):
    s, d = x1.shape
    ff = w1_all.shape[1] * w1_all.shape[3]
    nj = ff // tf

    def body(x_ref, g_ref, w1_ref, w2_ref, x2_ref, h2_ref, ap_ref, acc):
        j = pl.program_id(1)

        @pl.when(j == 0)
        def _():
            xv = x_ref[...]
            h2_ref[...] = ((xv * _rms_scale(xv)) * g_ref[...]).astype(BF)
            acc[...] = jnp.zeros_like(acc)

        ap = _nn(h2_ref[...], w1_ref[...])
        ap_ref[...] = ap.astype(BF)
        a = jnp.square(jnp.maximum(ap, 0.0)).astype(BF)
        acc[...] += _nn(a, w2_ref[...])

        @pl.when(j == nj - 1)
        def _():
            x2_ref[...] = x_ref[...] + acc[...]

    return pl.pallas_call(
        body, grid=(s // tb, nj), name="mlp_fwd",
        in_specs=[pl.BlockSpec((tb, d), lambda i, j: (i, 0)), _whole((1, d)),
                  pl.BlockSpec((None, None, d, tf), lambda i, j: (l, j, 0, 0)),
                  pl.BlockSpec((None, None, tf, d), lambda i, j: (l, j, 0, 0))],
        out_specs=[pl.BlockSpec((tb, d), lambda i, j: (i, 0)), pl.BlockSpec((tb, d), lambda i, j: (i, 0)),
                   pl.BlockSpec((tb, tf), lambda i, j: (i, j))],
        out_shape=[jax.ShapeDtypeStruct((s, d), F32), jax.ShapeDtypeStruct((s, d), BF),
                   jax.ShapeDtypeStruct((s, ff), BF)],
        scratch_shapes=[pltpu.VMEM((tb, d), F32)],
        compiler_params=_cparams("parallel", "arbitrary"),
    )(x1, g, w1_all, w2_all)


def _loss_head(x, g, tgt, tb):
    s, d = x.shape

    def body(x_ref, g_ref, t_ref, dx_ref, loss_ref, dg_ref):
        i = pl.program_id(0)

        @pl.when(i == 0)
        def _():
            loss_ref[...] = jnp.zeros_like(loss_ref)
            dg_ref[...] = jnp.zeros_like(dg_ref)

        xv = x_ref[...]
        r = _rms_scale(xv)
        xhat = xv * r
        err = xhat * g_ref[...] - t_ref[...]
        part = jnp.sum(jnp.mean(jnp.square(err), axis=-1, keepdims=True), axis=0, keepdims=True)
        loss_ref[...] += 0.5 * part
        dy = err * (1.0 / d)
        dg_ref[...] += jnp.sum(dy * xhat, axis=0, keepdims=True)
        dx_ref[...] = _norm_bwd(dy * g_ref[...], xhat, r)

    return pl.pallas_call(
        body, grid=(s // tb,), name="loss_head",
        in_specs=[_rows(tb, d), _whole((1, d)), _rows(tb, d)],
        out_specs=[_rows(tb, d), _whole((HALO, 128)), _whole((HALO, d))],
        out_shape=[jax.ShapeDtypeStruct((s, d), F32), jax.ShapeDtypeStruct((HALO, 128), F32),
                   jax.ShapeDtypeStruct((HALO, d), F32)],
        compiler_params=_cparams("arbitrary"),
    )(x, g, tgt)


def _mlp_bwd(dx2, x1, ap, g, w1_all, w2_all, l, tb, tf):
    s, d = x1.shape
    ff = ap.shape[1]
    nj = ff // tf

    def body(dx2_ref, x1_ref, ap_ref, g_ref, w1_ref, w2_ref, dx1_ref, dap_ref, dg_ref, acc):
        i, j = pl.program_id(0), pl.program_id(1)

        @pl.when((i == 0) & (j == 0))
        def _():
            dg_ref[...] = jnp.zeros_like(dg_ref)

        @pl.when(j == 0)
        def _():
            acc[...] = jnp.zeros_like(acc)

        da = _nt(dx2_ref[...].astype(BF), w2_ref[...])
        dap = (da * (2.0 * jnp.maximum(ap_ref[...].astype(F32), 0.0))).astype(BF)
        dap_ref[...] = dap
        acc[...] += _nt(dap, w1_ref[...])

        @pl.when(j == nj - 1)
        def _():
            xv = x1_ref[...]
            r = _rms_scale(xv)
            xhat = xv * r
            dh = acc[...]
            dg_ref[...] += jnp.sum(dh * xhat, axis=0, keepdims=True)
            dx1_ref[...] = dx2_ref[...] + _norm_bwd(dh * g_ref[...], xhat, r)

    return pl.pallas_call(
        body, grid=(s // tb, nj), name="mlp_bwd",
        in_specs=[pl.BlockSpec((tb, d), lambda i, j: (i, 0)), pl.BlockSpec((tb, d), lambda i, j: (i, 0)),
                  pl.BlockSpec((tb, tf), lambda i, j: (i, j)),
                  _whole((1, d)), pl.BlockSpec((None, None, d, tf), lambda i, j: (l, j, 0, 0)),
                  pl.BlockSpec((None, None, tf, d), lambda i, j: (l, j, 0, 0))],
        out_specs=[pl.BlockSpec((tb, d), lambda i, j: (i, 0)), pl.BlockSpec((tb, tf), lambda i, j: (i, j)),
                   _whole((HALO, d))],
        out_shape=[jax.ShapeDtypeStruct((s, d), F32), jax.ShapeDtypeStruct((s, ff), BF),
                   jax.ShapeDtypeStruct((HALO, d), F32)],
        scratch_shapes=[pltpu.VMEM((tb, d), F32)],
        compiler_params=_cparams("arbitrary", "arbitrary"),
    )(dx2, x1, ap, g, w1_all, w2_all)


def _wgrad(a, b, tm, tn, ts, name, relu2=False):
    s, m = a.shape
    n = b.shape[1]
    ns = s // ts

    def body(a_ref, b_ref, o_ref, acc):
        k = pl.program_id(2)

        @pl.when(k == 0)
        def _():
            acc[...] = jnp.zeros_like(acc)

        av = a_ref[...]
        if relu2:
            av = jnp.square(jnp.maximum(av.astype(F32), 0.0)).astype(BF)
        acc[...] += _tn(av, b_ref[...].astype(BF))

        @pl.when(k == ns - 1)
        def _():
            o_ref[...] = acc[...].astype(BF)

    return pl.pallas_call(
        body, grid=(m // tm, n // tn, ns), name=name,
        in_specs=[pl.BlockSpec((ts, tm), lambda i, j, k: (k, i)), pl.BlockSpec((ts, tn), lambda i, j, k: (k, j))],
        out_specs=pl.BlockSpec((tm, tn), lambda i, j, k: (i, j)),
        out_shape=jax.ShapeDtypeStruct((m, n), BF),
        scratch_shapes=[pltpu.VMEM((tm, tn), F32)],
        compiler_params=_cparams("parallel", "parallel", "arbitrary"),
    )(a, b)


def _mix_bwd(dx1, ya, yb, yc, lse_c, sink_row, gg, wo_all, l, tb):
    s, d = dx1.shape

    def body(dx_ref, ya_ref, yb_ref, yc_ref, lse_ref, sink_ref, gg_ref, wo_ref,
             n_ref, dya_ref, dyc_ref, da_ref, dc_ref, dyb_ref, dg_ref, dsink_ref):
        i = pl.program_id(0)

        @pl.when(i == 0)
        def _():
            dg_ref[...] = jnp.zeros_like(dg_ref)
            dsink_ref[...] = jnp.zeros_like(dsink_ref)

        dn = _nt(dx_ref[...].astype(BF), wo_ref[...].reshape(MIX_WIDTH, d))
        ys = [ya_ref[...], yb_ref[...], yc_ref[...]]
        rs = [_rms_scale(v) for v in ys]
        nhat = jnp.concatenate([v * r for v, r in zip(ys, rs)], axis=1)
        gg = gg_ref[...]
        n_ref[...] = (nhat * gg).astype(BF)
        dg_ref[...] += jnp.sum(dn * nhat, axis=0, keepdims=True)
        dnh = dn * gg
        bounds = [(0, A_WIDTH), (A_WIDTH, A_WIDTH + CONV_CH), (A_WIDTH + CONV_CH, MIX_WIDTH)]
        dys = [_norm_bwd(dnh[:, lo:hi], nhat[:, lo:hi], r) for (lo, hi), r in zip(bounds, rs)]
        dyb_ref[...] = dys[1]
        for dy, y, dy_ref, dd_ref in ((dys[0], ys[0], dya_ref, da_ref), (dys[2], ys[2], dyc_ref, dc_ref)):
            dy_ref[...] = dy
            t = dy * y
            for h in range(N_HEADS):
                dd_ref[:, _hs(h)] = jnp.broadcast_to(jnp.sum(t[:, _hs(h)], axis=1, keepdims=True), (tb, HEAD_DIM))
        dsink_ref[...] -= jnp.sum(jnp.exp(sink_ref[...] - lse_ref[...]) * dc_ref[...], axis=0, keepdims=True)

    return pl.pallas_call(
        body, grid=(s // tb,), name="mix_bwd",
        in_specs=[_rows(tb, d), _rows(tb, A_WIDTH), _rows(tb, CONV_CH), _rows(tb, A_WIDTH), _rows(tb, A_WIDTH),
                  _whole((1, A_WIDTH)), _whole((1, MIX_WIDTH)), _layer((N_CHIPS, MIX_WIDTH // N_CHIPS, d), l)],
        out_specs=[_rows(tb, MIX_WIDTH), _rows(tb, A_WIDTH), _rows(tb, A_WIDTH), _rows(tb, A_WIDTH),
                   _rows(tb, A_WIDTH), _rows(tb, CONV_CH), _whole((HALO, MIX_WIDTH)), _whole((HALO, A_WIDTH))],
        out_shape=[jax.ShapeDtypeStruct((s, MIX_WIDTH), BF), jax.ShapeDtypeStruct((s, A_WIDTH), F32),
                   jax.ShapeDtypeStruct((s, A_WIDTH), F32), jax.ShapeDtypeStruct((s, A_WIDTH), F32),
                   jax.ShapeDtypeStruct((s, A_WIDTH), F32), jax.ShapeDtypeStruct((s, CONV_CH), F32),
                   jax.ShapeDtypeStruct((HALO, MIX_WIDTH), F32), jax.ShapeDtypeStruct((HALO, A_WIDTH), F32)],
        compiler_params=_cparams("arbitrary"),
    )(dx1, ya, yb, yc, lse_c, sink_row, gg, wo_all)


def _attn_bwd(z, dy, lse, dd, dil, kw, kcol, vcol, n_rep, max_dist, name):
    s, zw = z.shape
    grid = _p_grid(s, dil)
    n_kv = N_HEADS // n_rep
    dt = F32 if dil == 1 else BF

    def body(q_ref, kp_ref, kc_ref, vp_ref, vc_ref, dy_ref, lse_ref, dd_ref, dq_ref, dkp_ref, dkc_ref, dvp_ref, dvc_ref):
        mask = _band_mask(pl.program_id(len(grid) - 1), dil, max_dist)
        k2s, qs, dys, scs, dps = [], [], [], [], []
        for kh in range(n_kv):
            k2s.append(jnp.concatenate([_ld(kp_ref, _hs(kh)), _ld(kc_ref, _hs(kh))], axis=0).astype(BF))
            v2 = jnp.concatenate([_ld(vp_ref, _hs(kh)), _ld(vc_ref, _hs(kh))], axis=0).astype(BF)
            for h in range(kh * n_rep, (kh + 1) * n_rep):
                qs.append((_ld(q_ref, _hs(h)) * SCALE).astype(BF))
                dys.append(_ld(dy_ref, _hs(h)).astype(BF))
                scs.append(jnp.where(mask, _nt(qs[h], k2s[kh]), NEG))
                dps.append(_nt(dys[h], v2))
        for kh in range(n_kv):
            k2 = k2s[kh]
            dk2 = jnp.zeros((2 * TQ, HEAD_DIM), F32)
            dv2 = jnp.zeros((2 * TQ, HEAD_DIM), F32)
            for h in range(kh * n_rep, (kh + 1) * n_rep):
                lse_h = _ld(lse_ref, slice(h * HEAD_DIM, h * HEAD_DIM + 1))
                dd_h = _ld(dd_ref, slice(h * HEAD_DIM, h * HEAD_DIM + 1))
                p = jnp.exp(scs[h] - lse_h)
                ds = (p * (dps[h] - dd_h)).astype(BF)
                _st(dq_ref, _hs(h), (_nn(ds, k2) * SCALE).astype(dq_ref.dtype))
                dk2 = dk2 + _tn(ds, qs[h])
                dv2 = dv2 + _tn(p.astype(BF), dys[h])
            _st(dkp_ref, _hs(kh), dk2[:TQ].astype(dt))
            _st(dkc_ref, _hs(kh), dk2[TQ:].astype(dt))
            _st(dvp_ref, _hs(kh), dv2[:TQ].astype(dt))
            _st(dvc_ref, _hs(kh), dv2[TQ:].astype(dt))

    args = [_strips(z)] * 5 + [_strips(a) for a in (dy, lse, dd)]
    in_specs = [_p_spec(dil, A_WIDTH, 0), _p_spec(dil, kw, kcol, True), _p_spec(dil, kw, kcol),
                _p_spec(dil, kw, vcol, True), _p_spec(dil, kw, vcol)] + [_p_spec(dil, A_WIDTH, 0)] * 3
    out_specs = [_p_spec(dil, A_WIDTH, 0)] + [_p_spec(dil, kw, 0)] * 4
    na = s // N_STRIPS
    out_shape = [jax.ShapeDtypeStruct((4, 4, na, A_WIDTH), dt)] + [jax.ShapeDtypeStruct((4, 4, na, kw), dt)] * 4
    res = pl.pallas_call(
        body, grid=grid, name=name, in_specs=in_specs, out_specs=out_specs, out_shape=out_shape,
        compiler_params=_cparams(*(("parallel",) * len(grid))),
    )(*args)
    return [res[0].reshape(s, A_WIDTH)] + [a.reshape(s, kw) for a in res[1:]]


DZ_TA = 16


def _dz_assemble(parts_a, parts_c, dyb, zb, cw):
    s = zb.shape[0]
    na = s // N_STRIPS
    nb = na // DZ_TA

    def ahead(w, k):
        return pl.BlockSpec((4, 4, DZ_TA, w), lambda i: (0, 0, jnp.minimum(i + k, nb - 1), 0))

    args, in_specs = [], []
    for dil, (dq, dkp, dkc, dvp, dvc) in zip(DILATIONS + (1,), parts_a + [parts_c]):
        w = dkp.shape[1]
        here = _strip_rows(DZ_TA, w)
        if dil == 1:
            args += [dq, dkp, dkp, dkc, dvp, dvp, dvc]
            in_specs += [_strip_rows(DZ_TA, A_WIDTH), here, ahead(w, 1), here, here, ahead(w, 1), here]
        else:
            k = 8 * dil // DZ_TA
            args += [dq, dkp, dkc, dvp, dvc]
            in_specs += [_strip_rows(DZ_TA, A_WIDTH), ahead(w, k), here, ahead(w, k), here]
    n_att = len(args)
    args = [_strips(a) for a in args] + [_strips(dyb), _strips(dyb), _strips(zb), _strips(zb), _strips(zb), cw]
    in_specs += [_strip_rows(DZ_TA, CONV_CH), _next_rows(DZ_TA, CONV_CH, nb), _strip_rows(DZ_TA, ZB_W),
                 _prev_rows(DZ_TA, ZB_W), _next_rows(DZ_TA, ZB_W, nb), _whole((HALO, CONV_CH))]

    def body(*refs):
        att = list(refs[:n_att])
        dyb_ref, dybn_ref, zb_ref, zbp_ref, zbn_ref, cw_ref, dz_ref, dcw_ref = refs[n_att:]
        i = pl.program_id(0)

        @pl.when(i == 0)
        def _():
            dcw_ref[...] = jnp.zeros_like(dcw_ref)

        def shifted(dil):
            if dil == 1:
                dq_r, kp0, kp1, dkc_r, vp0, vp1, dvc_r = [att.pop(0) for _ in range(7)]
                live = i + 1 < nb
                half = DZ_TA // 2
                dkp = jnp.concatenate([kp0[:, :, half:, :], jnp.where(live, kp1[:, :, :half, :], 0.0)], axis=2)
                dvp = jnp.concatenate([vp0[:, :, half:, :], jnp.where(live, vp1[:, :, :half, :], 0.0)], axis=2)
            else:
                dq_r, dkp_r, dkc_r, dvp_r, dvc_r = [att.pop(0) for _ in range(5)]
                live = i + 8 * dil // DZ_TA < nb
                dkp = jnp.where(live, dkp_r[...].astype(F32), 0.0)
                dvp = jnp.where(live, dvp_r[...].astype(F32), 0.0)
            return dq_r[...].astype(F32), dkc_r[...].astype(F32) + dkp, dvc_r[...].astype(F32) + dvp

        dq, dk, dv = shifted(DILATIONS[0])
        for dil in DILATIONS[1:]:
            dq2, dk2, dv2 = shifted(dil)
            dq, dk, dv = dq + dq2, dk + dk2, dv + dv2
        dz_ref[:, :, :, 0:A_WIDTH] = dq.astype(BF)
        dz_ref[:, :, :, A_WIDTH:2 * A_WIDTH] = dk.astype(BF)
        dz_ref[:, :, :, 2 * A_WIDTH:ZA_W] = dv.astype(BF)
        dq, dk, dv = shifted(1)
        c0 = ZA_W + ZB_W
        dz_ref[:, :, :, c0:c0 + A_WIDTH] = dq.astype(BF)
        dz_ref[:, :, :, c0 + A_WIDTH:c0 + A_WIDTH + C_KV_WIDTH] = dk.astype(BF)
        dz_ref[:, :, :, c0 + A_WIDTH + C_KV_WIDTH:IN_WIDTH] = dv.astype(BF)

        cw = cw_ref[...]
        prev = jnp.where(i > 0, zbp_ref[...], 0.0)
        gb, gc, xb, u, u1, u2, c = _conv_strips(zb_ref[...], prev, cw)
        dyb = dyb_ref[...]
        dc = [_strip(dyb, b) * gb[b] for b in range(N_STRIPS)]
        dcn = jnp.where(i + 1 < nb, dybn_ref[...] * zbn_ref[:, :, :CONV_CH], 0.0)
        wrapped = [_shift_up(dc[0], 1, dcn[0]), _shift_up(dc[1], 1, dcn[1])]
        upd = [jnp.zeros((1, CONV_CH), F32)] * 3
        for b in range(N_STRIPS):
            dc1 = dc[b + 1] if b + 1 < N_STRIPS else wrapped[0]
            dc2 = dc[b + 2] if b + 2 < N_STRIPS else wrapped[b + 2 - N_STRIPS]
            du = cw[2:3, :] * dc[b] + cw[1:2, :] * dc1 + cw[0:1, :] * dc2
            f, e = b % 4, b // 4
            dz_ref[f, e, :, ZA_W:ZA_W + CONV_CH] = (_strip(dyb, b) * c[b]).astype(BF)
            dz_ref[f, e, :, ZA_W + CONV_CH:ZA_W + 2 * CONV_CH] = (du * xb[b]).astype(BF)
            dz_ref[f, e, :, ZA_W + 2 * CONV_CH:c0] = (du * gc[b]).astype(BF)
            for t, uu in enumerate((u2[b], u1[b], u[b])):
                upd[t] = upd[t] + jnp.sum(dc[b] * uu, axis=0, keepdims=True)
        row = lax.broadcasted_iota(jnp.int32, (HALO, CONV_CH), 0)
        tile = jnp.zeros((HALO, CONV_CH), F32)
        for t in range(3):
            tile = jnp.where(row == t, upd[t], tile)
        dcw_ref[...] += tile

    dz, dcw = pl.pallas_call(
        body, grid=(nb,), name="dz_assemble", in_specs=in_specs,
        out_specs=[_strip_rows(DZ_TA, IN_WIDTH), _whole((HALO, CONV_CH))],
        out_shape=[jax.ShapeDtypeStruct((4, 4, na, IN_WIDTH), BF), jax.ShapeDtypeStruct((HALO, CONV_CH), F32)],
        compiler_params=_cparams("arbitrary"),
    )(*args)
    return dz.reshape(s, IN_WIDTH), dcw


def _qkv_bwd(dz, dx1, x, g, w_all, l, tb, tokens_out):
    s, d = x.shape
    na, ta = s // N_STRIPS, tb // N_STRIPS

    def body(dz_ref, dx1_ref, x_ref, g_ref, w_ref, dx_ref, dg_ref):
        i = pl.program_id(0)

        @pl.when(i == 0)
        def _():
            dg_ref[...] = jnp.zeros_like(dg_ref)

        n = IN_WIDTH // N_CHIPS
        dz = dz_ref[...].reshape(tb, IN_WIDTH)
        dh = _nt(dz[:, 0:n], w_ref[0])
        for k in range(1, N_CHIPS):
            dh = dh + _nt(dz[:, k * n:(k + 1) * n], w_ref[k])
        xv = x_ref[...].reshape(tb, d)
        r = _rms_scale(xv)
        xhat = xv * r
        dg_ref[...] += jnp.sum(dh * xhat, axis=0, keepdims=True)
        dx = (dx1_ref[...].reshape(tb, d) + _norm_bwd(dh * g_ref[...], xhat, r)).reshape(4, 4, ta, d)
        if tokens_out:
            for b in range(N_STRIPS):
                dx_ref[:, b, :] = _strip(dx, b)
        else:
            dx_ref[...] = dx

    if tokens_out:
        dx_spec, dx_shape = pl.BlockSpec((ta, N_STRIPS, d), lambda i: (i, 0, 0)), (na, N_STRIPS, d)
    else:
        dx_spec, dx_shape = _strip_rows(ta, d), (4, 4, na, d)
    dx, dg = pl.pallas_call(
        body, grid=(s // tb,), name="qkv_bwd",
        in_specs=[_strip_rows(ta, IN_WIDTH), _strip_rows(ta, d), _strip_rows(ta, d), _whole((1, d)),
                  _layer((N_CHIPS, d, IN_WIDTH // N_CHIPS), l)],
        out_specs=[dx_spec, _whole((HALO, d))],
        out_shape=[jax.ShapeDtypeStruct(dx_shape, F32), jax.ShapeDtypeStruct((HALO, d), F32)],
        compiler_params=_cparams("arbitrary"),
    )(_strips(dz), _strips(dx1), _strips(x), g, w_all)
    return dx.reshape(s, d), dg


def _tile_rows(rows):
    return jnp.pad(rows, ((0, HALO - rows.shape[0]), (0, 0)))


def _to_strips(a, after, name):
    s, d = a.shape
    na = s // N_STRIPS
    ta = min(32, na)

    def body(a_ref, after_ref, o_ref):
        for b in range(N_STRIPS):
            o_ref[b % 4, b // 4] = a_ref[:, b, :]

    return pl.pallas_call(
        body, grid=(na // ta,), name=name,
        in_specs=[pl.BlockSpec((ta, N_STRIPS, d), lambda i: (i, 0, 0)), ANY], out_specs=_strip_rows(ta, d),
        out_shape=jax.ShapeDtypeStruct((4, 4, na, d), a.dtype), compiler_params=_cparams("parallel"),
    )(a.reshape(na, N_STRIPS, d), after).reshape(s, d)


def _local_step(x, tgt, fetch, ff, sinks, g_mix, g_group, g_mlp, g_final, emit):
    s, d = x.shape
    depth = g_mix.shape[0]
    tb = min(512, s)
    tf = ff // N_CHIPS
    ts = min(1024, s)
    saved = []
    for l in range(depth):
        w_in, _, _, _, conv_w = fetch(0, l, x)
        cw = _tile_rows(conv_w[l])
        sk = jnp.repeat(sinks[l].reshape(N_HEADS), HEAD_DIM)[None]
        h, za, zb, zc = _qkv_fwd(x, g_mix[l][None], w_in, l, tb)
        parts_a = [_attn_fwd(za, dil, A_WIDTH, 1, 2, 1, A_MAX_DIST, "attn_a_fwd_%d" % dil) for dil in DILATIONS]
        part_c = _attn_fwd(zc, 1, C_KV_WIDTH, 3, 4, C_GROUP, C_MAX_DIST, "attn_c_fwd")
        ya, lse_a, yc, lse_c = _attn_merge(parts_a, part_c, sk, tb)
        w_in, w_o, w1, w2, _ = fetch(1, l, yc)
        x1, yb = _mix_fwd(x, ya, yc, zb, cw, g_group[l][None], w_o, l, tb)
        x2, h2, ap = _mlp_fwd(x1, g_mlp[l][None], w1, w2, l, ts, tf)
        saved.append((x, h, za, zb, zc, ya, lse_a, yc, lse_c, yb, x1, h2, ap, cw, sk))
        x = x2
    dx, loss_tile, dg_final = _loss_head(x, g_final[None], tgt, tb)
    grads = [None] * depth
    tok = jnp.zeros((), F32)
    for l in reversed(range(depth)):
        x0, h, za, zb, zc, ya, lse_a, yc, lse_c, yb, x1, h2, ap, cw, sk = saved[l]
        dx1, dap, dg_mlp = _mlp_bwd(dx, x1, ap, g_mlp[l][None] + tok, w1, w2, l, ts, tf)
        tok = emit(l, 3, _wgrad(ap, dx, min(1024, ff), d, ts, "wgrad_ff_out", relu2=True))
        tok = tok + emit(l, 2, _wgrad(h2, dap, d, min(1024, ff), 2 * ts, "wgrad_ff_in"))
        n, dya, dyc, dd_a, dd_c, dyb, dg_group, dsink = _mix_bwd(dx1, ya, yb, yc, lse_c, sk, g_group[l][None] + tok,
                                                                 w_o, l, tb)
        tok = emit(l, 1, _wgrad(n, dx1, MIX_WIDTH, d, ts, "wgrad_o"))
        cw = cw + tok
        parts_a = [_attn_bwd(za, dya, lse_a, dd_a, dil, A_WIDTH, 1, 2, 1, A_MAX_DIST, "attn_a_bwd_%d" % dil)
                   for dil in DILATIONS]
        parts_c = _attn_bwd(zc, dyc, lse_c, dd_c, 1, C_KV_WIDTH, 3, 4, C_GROUP, C_MAX_DIST, "attn_c_bwd")
        dz, dcw = _dz_assemble(parts_a, parts_c, dyb, zb, cw)
        tok = emit(l, 0, _wgrad(h, dz, d, IN_WIDTH // 4, 2 * ts, "wgrad_in"))
        dx, dg_mix = _qkv_bwd(dz, dx1, x0, g_mix[l][None] + tok, w_in, l, tb, l == 0)
        grads[l] = (dcw, dsink, dg_mix, dg_group, dg_mlp)
    return loss_tile, dx, grads, dg_final


ANY = pl.BlockSpec(memory_space=pl.ANY)
SHARD_AXES = (2, 1, 2, 1)
N_BIG = len(SHARD_AXES)
N_CHIPS = 4
N_DEV = 8


def _mesh_pos():
    return lax.axis_index("x"), lax.axis_index("y"), lax.axis_index("c")


def _flip(v, bit):
    return 1 - v if bit else v


def _place_shard(shard, chip_arr, name):
    _, rows, cols = shard.shape
    tr = min(256, rows)

    def body(chip_ref, x_ref, o_ref):
        o_ref[...] = x_ref[...].astype(BF)

    return pl.pallas_call(
        body, name=name,
        grid_spec=pltpu.PrefetchScalarGridSpec(
            num_scalar_prefetch=1, grid=(2, rows // tr),
            in_specs=[pl.BlockSpec((None, tr, cols), lambda l, i, chip: (l, i, 0))],
            out_specs=pl.BlockSpec((None, None, tr, cols), lambda l, i, chip: (l, chip[0], i, 0))),
        out_shape=jax.ShapeDtypeStruct((2, N_CHIPS, rows, cols), BF),
        compiler_params=_cparams("parallel", "parallel"),
    )(chip_arr, shard)


HBM = pl.BlockSpec(memory_space=pltpu.HBM)
SEM = pl.BlockSpec(memory_space=pltpu.SEMAPHORE)
EFFECT = pltpu.SideEffectType.DATAFLOW_SIDE_EFFECTING

GATHER_GROUPS = (((0, 0),), ((1, 0), (2, 0), (3, 0)), ((0, 1),), ((1, 1), (2, 1), (3, 1)))
GATHER_STARTS = ((0,), (1,), (2, 3))


def _gather_copies(arrs, group, send_sems, recv_sems):
    x, y, c = _mesh_pos()
    me = 2 * x + y
    out = []
    for i, (w, layer) in enumerate(group):
        mine = arrs[w].at[layer, me]
        for j, (qx, qy) in enumerate([(1 - x, y), (x, 1 - y), (1 - x, 1 - y)]):
            landed = arrs[w].at[layer, 2 * qx + qy]
            out.append(tuple(pltpu.make_async_remote_copy(
                src_ref=piece, dst_ref=piece, send_sem=send_sems.at[i * 3 + j], recv_sem=recv_sems.at[i * 3 + j],
                device_id=(qx, qy, c), device_id_type=MESH) for piece in (mine, landed)))
    return out


def _conv_copies(conv_src, conv_dst, send_sems, recv_sems):
    x, y, c = _mesh_pos()
    out = []
    for j, (qx, qy) in enumerate([(1 - x, y), (x, 1 - y), (1 - x, 1 - y)]):
        out.append(tuple(pltpu.make_async_remote_copy(
            src_ref=conv_src, dst_ref=conv_dst.at[q], send_sem=send_sems.at[j], recv_sem=recv_sems.at[j],
            device_id=(qx, qy, c), device_id_type=MESH) for q in (2 * x + y, 2 * qx + qy)))
    return out


def _gather_start(groups, arrs, conv, name, through=None):
    n_sems = 2 * (len(groups) + (conv is not None))
    mats = sorted({w for g in groups for w, _ in GATHER_GROUPS[g]})

    def body(*refs):
        arrs_ref = [None] * N_BIG
        for w, ref in zip(mats, refs):
            arrs_ref[w] = ref
        sems = refs[n_in:n_in + n_sems]
        if conv is not None:
            for cp, _ in _conv_copies(refs[len(mats)], refs[len(mats) + 1], sems[-2], sems[-1]):
                cp.start()
        for k, g in enumerate(groups):
            for cp, _ in _gather_copies(arrs_ref, GATHER_GROUPS[g], sems[2 * k], sems[2 * k + 1]):
                cp.start()

    sem_shapes = []
    for n in [len(GATHER_GROUPS[g]) for g in groups] + ([1] if conv is not None else []):
        sem_shapes += [pltpu.SemaphoreType.DMA((3 * n,))] * 2
    operands = [arrs[w] for w in mats] + ([] if conv is None else list(conv)) + ([] if through is None else [through])
    n_in = len(operands)
    res = pl.pallas_call(
        body, name=name,
        out_shape=tuple(sem_shapes) + tuple(pltpu.HBM(a.shape, a.dtype) for a in operands),
        in_specs=(HBM,) * n_in, out_specs=(SEM,) * n_sems + (HBM,) * n_in,
        input_output_aliases={i: n_sems + i for i in range(n_in)},
        compiler_params=pltpu.CompilerParams(has_side_effects=EFFECT),
    )(*[pltpu.with_memory_space_constraint(a, pltpu.HBM) for a in operands])
    arrs = list(arrs)
    for w, a in zip(mats, res[n_sems:]):
        arrs[w] = a
    return res[:n_sems], arrs, list(res[n_sems + len(mats):])


def _gather_wait(k, sems, arrs, conv, after, name):
    group = GATHER_GROUPS[k]
    mats = sorted({w for w, _ in group})
    n_conv = 0 if conv is None else 2

    def body(*refs):
        local = refs[:len(mats)]
        arrs_ref = [None] * N_BIG
        for w, ref in zip(mats, local):
            arrs_ref[w] = ref
        pos = len(mats) + n_conv
        copies = _gather_copies(arrs_ref, group, refs[pos], refs[pos + 1])
        if conv is not None:
            copies += _conv_copies(refs[len(mats)], refs[len(mats) + 1], refs[pos + 2], refs[pos + 3])
        for send, recv in copies:
            recv.wait_recv()
            send.wait_send()

    operands = [arrs[w] for w in mats] + ([] if conv is None else [conv[1], conv[2]])
    sem_ops = list(sems) + ([] if conv is None else list(conv[0]))
    n_op = len(operands)
    res = pl.pallas_call(
        body, name=name, out_shape=tuple(pltpu.HBM(a.shape, a.dtype) for a in operands),
        in_specs=(HBM,) * n_op + (SEM,) * len(sem_ops) + (ANY,), out_specs=(HBM,) * n_op,
        input_output_aliases={i: i for i in range(n_op)},
        compiler_params=pltpu.CompilerParams(has_side_effects=EFFECT),
    )(*operands, *sem_ops, after)
    arrs = list(arrs)
    for w, a in zip(mats, res):
        arrs[w] = a
    return arrs, (res[-1] if conv is not None else None)


def _grad_shard(ref, w, chip, n):
    start = pl.multiple_of(chip * n, 128)
    if SHARD_AXES[w] == 2:
        return ref.at[:, pl.ds(start, n)]
    return ref.at[pl.ds(start, n), :]


def _slot_shape(g, w):
    shape = list(g.shape)
    shape[SHARD_AXES[w] - 1] //= N_CHIPS
    return (N_DEV - 1,) + tuple(shape)


def _scatter_copies(g_ref, land_ref, send_sems, recv_sems, layer, w):
    x, y, c = _mesh_pos()
    n = g_ref.shape[SHARD_AXES[w] - 1] // N_CHIPS
    out = []
    for r in range(1, N_DEV):
        tx, ty, tc = _flip(x, r & 4), _flip(y, r & 2), _flip(c, r & 1)
        cp = pltpu.make_async_remote_copy(
            src_ref=_grad_shard(g_ref, w, 2 * tx + ty, n), dst_ref=land_ref.at[r - 1], send_sem=send_sems.at[r - 1],
            recv_sem=recv_sems.at[r - 1], device_id=(tx, ty, tc), device_id_type=MESH)
        out.append((cp, (c != layer) if r & 1 else (c == layer)))
    return out


def _scatter_start(g, land, layer, w, name):
    def body(g_ref, land_ref, send_sems, recv_sems, g_thru, land_thru, token):
        for cp, mine in _scatter_copies(g_ref, land_ref, send_sems, recv_sems, layer, w):
            @pl.when(mine)
            def _():
                cp.start()
        token[...] = jnp.zeros_like(token)

    return pl.pallas_call(
        body, name=name,
        out_shape=(pltpu.SemaphoreType.DMA((N_DEV - 1,)), pltpu.SemaphoreType.DMA((N_DEV - 1,)),
                   pltpu.HBM(g.shape, g.dtype), pltpu.HBM(land.shape, land.dtype),
                   jax.ShapeDtypeStruct((HALO, 128), F32)),
        in_specs=(HBM, HBM), out_specs=(SEM, SEM, HBM, HBM, pl.BlockSpec(memory_space=pltpu.VMEM)),
        input_output_aliases={0: 2, 1: 3}, compiler_params=pltpu.CompilerParams(has_side_effects=EFFECT),
    )(pltpu.with_memory_space_constraint(g, pltpu.HBM), pltpu.with_memory_space_constraint(land, pltpu.HBM))


def _scatter_wait(started, land, after, w, name):
    def body(g0_ref, g1_ref, land_ref, ss0, rs0, ss1, rs1, after_ref, g0_out, g1_out, land_out):
        c = lax.axis_index("c")
        for layer, g_ref, ss, rs in ((0, g0_ref, ss0, rs0), (1, g1_ref, ss1, rs1)):
            for cp, mine in _scatter_copies(g_ref, land_ref, ss, rs, layer, w):
                @pl.when(mine)
                def _():
                    cp.wait_send()

                @pl.when(c == layer)
                def _():
                    cp.wait_recv()

    (ss0, rs0, g0), (ss1, rs1, g1) = started
    return pl.pallas_call(
        body, name=name,
        out_shape=(pltpu.HBM(g0.shape, g0.dtype), pltpu.HBM(g1.shape, g1.dtype), pltpu.HBM(land.shape, land.dtype)),
        in_specs=(HBM, HBM, HBM, SEM, SEM, SEM, SEM, ANY), out_specs=(HBM, HBM, HBM),
        input_output_aliases={0: 0, 1: 1, 2: 2}, compiler_params=pltpu.CompilerParams(has_side_effects=EFFECT),
    )(g0, g1, land, ss0, rs0, ss1, rs1, after)


def _sum_slots(g0, g1, slots, w, pos_arr, name):
    _, rows, cols = slots.shape
    tr = min(256, rows)
    nr = rows // tr
    if SHARD_AXES[w] == 2:
        own = pl.BlockSpec((tr, cols), lambda i, pos: (i, pos[0]))
    else:
        own = pl.BlockSpec((tr, cols), lambda i, pos: (pos[0] * nr + i, 0))

    def body(pos_ref, own0_ref, own1_ref, s_ref, o_ref):
        acc = jnp.where(pos_ref[1] == 0, own0_ref[...], own1_ref[...]).astype(F32)
        for r in range(N_DEV - 1):
            acc = acc + s_ref[r].astype(F32)
        o_ref[...] = acc

    return pl.pallas_call(
        body, name=name,
        grid_spec=pltpu.PrefetchScalarGridSpec(
            num_scalar_prefetch=1, grid=(nr,),
            in_specs=[own, own, pl.BlockSpec((N_DEV - 1, tr, cols), lambda i, pos: (0, i, 0))],
            out_specs=pl.BlockSpec((tr, cols), lambda i, pos: (i, 0))),
        out_shape=jax.ShapeDtypeStruct((rows, cols), F32), compiler_params=_cparams("parallel"),
    )(pos_arr, g0, g1, slots)


def _swap_layers(halves, name):
    n = len(halves)

    def body(*refs):
        srcs, dsts = refs[:n], refs[n:2 * n]
        send_sems, recv_sems = refs[2 * n:]
        x, y, c = _mesh_pos()
        sends = [pltpu.make_async_remote_copy(src_ref=srcs[w], dst_ref=dsts[w], send_sem=send_sems.at[w],
                                              recv_sem=recv_sems.at[w], device_id=(x, y, 1 - c), device_id_type=MESH)
                 for w in range(n)]
        for cp in sends:
            cp.start()
        for cp in sends:
            cp.wait_recv()
        for cp in sends:
            cp.wait_send()

    return pl.pallas_call(
        body, name=name, in_specs=[ANY] * n, out_specs=[ANY] * n,
        out_shape=[jax.ShapeDtypeStruct(h.shape, h.dtype) for h in halves],
        scratch_shapes=[pltpu.SemaphoreType.DMA((n,)), pltpu.SemaphoreType.DMA((n,))],
    )(*halves)


def _adamw_math(w, g, m, v):
    m = ADAM_B1 * m + (1.0 - ADAM_B1) * g
    v = ADAM_B2 * v + (1.0 - ADAM_B2) * jnp.square(g)
    m_hat = m / (1.0 - ADAM_B1 ** ADAM_STEP)
    v_hat = v / (1.0 - ADAM_B2 ** ADAM_STEP)
    delta = -ADAM_LR * (m_hat / (jnp.sqrt(v_hat) + ADAM_EPS) + ADAM_WD * w)
    return delta, m, v


def _adamw(w, g_own, g_other, m, v, pos_arr, name):
    shape = w.shape
    _, rows, cols = shape
    tr = min(256, rows)

    def body(pos_ref, w_ref, own_ref, other_ref, m_ref, v_ref, g_ref, d_ref, m2_ref, v2_ref):
        g = jnp.where(pl.program_id(0) == pos_ref[1], own_ref[...], other_ref[...])
        g_ref[...] = g
        d_ref[...], m2_ref[...], v2_ref[...] = _adamw_math(w_ref[...], g, m_ref[...], v_ref[...])

    full = pl.BlockSpec((None, tr, cols), lambda l, i, pos: (l, i, 0))
    half = pl.BlockSpec((tr, cols), lambda l, i, pos: (i, 0))
    return pl.pallas_call(
        body, name=name,
        grid_spec=pltpu.PrefetchScalarGridSpec(
            num_scalar_prefetch=1, grid=(2, rows // tr),
            in_specs=[full, half, half, full, full], out_specs=[full] * 4),
        out_shape=[jax.ShapeDtypeStruct(shape, F32)] * 4, compiler_params=_cparams("parallel", "parallel"),
    )(pos_arr, w, g_own, g_other, m, v)


def _small_sync(part, w, m, v):
    rows, cols = part.shape

    def body(p_ref, w_ref, m_ref, v_ref, g_ref, d_ref, m2_ref, v2_ref, slots, send_sems, recv_sems):
        x, y, c = _mesh_pos()
        me = 4 * x + 2 * y + c
        slots[me] = p_ref[...]
        sends = []
        for r in range(1, N_DEV):
            to = (_flip(x, r & 4), _flip(y, r & 2), _flip(c, r & 1))
            sends.append(pltpu.make_async_remote_copy(
                src_ref=p_ref, dst_ref=slots.at[me], send_sem=send_sems.at[r - 1], recv_sem=recv_sems.at[r - 1],
                device_id=to, device_id_type=MESH))
        for cp in sends:
            cp.start()
        for cp in sends:
            cp.wait_recv()
        for cp in sends:
            cp.wait_send()
        g = slots[0]
        for i in range(1, N_DEV):
            g = g + slots[i]
        g_ref[...] = g
        d_ref[...], m2_ref[...], v2_ref[...] = _adamw_math(w_ref[...], g, m_ref[...], v_ref[...])

    vm = pl.BlockSpec(memory_space=pltpu.VMEM)
    return pl.pallas_call(
        body, name="small_sync", in_specs=[vm] * 4, out_specs=[vm] * 4,
        out_shape=[jax.ShapeDtypeStruct((rows, cols), F32)] * 4,
        scratch_shapes=[pltpu.VMEM((N_DEV, rows, cols), F32), pltpu.SemaphoreType.DMA((N_DEV - 1,)),
                        pltpu.SemaphoreType.DMA((N_DEV - 1,))],
    )(part, w, m, v)


PACK_W = 256


def _pack_rows(n):
    return -(-n // (HALO * PACK_W)) * HALO


def _pack_small(parts):
    out = []
    for a in parts:
        flat = a.reshape(-1)
        out.append(jnp.pad(flat, (0, _pack_rows(flat.size) * PACK_W - flat.size)).reshape(-1, PACK_W))
    return jnp.concatenate(out, axis=0)


def _unpack_small(p, shapes):
    out, row = [], 0
    for shape in shapes:
        n = 1
        for k in shape:
            n *= k
        out.append(p[row:row + _pack_rows(n)].reshape(-1)[:n].reshape(shape))
        row += _pack_rows(n)
    return out


def kernel(x, w_in, conv_w, sinks, g_mix, g_group, w_o, g_mlp, w_ff_in, w_ff_out, g_final, loss_target, m_w_in, m_conv_w, m_sinks, m_g_mix, m_g_group, m_w_o, m_g_mlp, m_w_ff_in, m_w_ff_out, m_g_final, v_w_in, v_conv_w, v_sinks, v_g_mix, v_g_group, v_w_o, v_g_mlp, v_w_ff_in, v_w_ff_out, v_g_final):
    chip = 2 * lax.axis_index("x") + lax.axis_index("y")
    conv_n = conv_w.shape[2]

    pos_arr = jnp.stack([chip, lax.axis_index("c")]).astype(jnp.int32)
    shards = (w_in, w_o, w_ff_in, w_ff_out)
    conv_tile = jnp.pad(conv_w.reshape(6, conv_n), ((0, HALO - 6), (0, 128 - conv_n)))
    placed = [_place_shard(w_in, pos_arr[:1], "place_shard_0"), None, None, None]
    sems_a, placed, conv_thru = _gather_start(
        GATHER_STARTS[0], placed, (conv_tile, lax.empty((N_CHIPS,) + conv_tile.shape, conv_tile.dtype)),
        "gather_start_0")
    for i in range(1, N_BIG):
        placed[i] = _place_shard(shards[i], pos_arr[:1], "place_shard_%d" % i)
    full = {"arrs": placed, "conv": None, "sems": list(sems_a[:2])}

    def fetch(stage, layer, after):
        k = 2 * layer + stage
        sems = full["sems"][2 * k:2 * k + 2]
        if k == 0:
            full["arrs"], land = _gather_wait(0, sems, full["arrs"], (sems_a[-2:], *conv_thru), after, "gather_wait_0")
            conv_all = lax.dynamic_update_slice(land, conv_tile[None], (chip, 0, 0))
            full["conv"] = conv_all[:, :6, :conv_n].reshape(N_CHIPS, 2, 3, conv_n).transpose(1, 2, 0, 3).reshape(
                2, 3, CONV_CH)
            sems_b, full["arrs"], rest = _gather_start(GATHER_STARTS[1], full["arrs"], None, "gather_start_1",
                                                       through=full["arrs"][0])
            full["arrs"][0] = rest[-1]
            full["sems"] += list(sems_b)
        else:
            full["arrs"], _ = _gather_wait(k, sems, full["arrs"], None, after, "gather_wait_%d" % k)
        if k == 1:
            sems_c, full["arrs"], _ = _gather_start(GATHER_STARTS[2], full["arrs"], None, "gather_start_2")
            full["sems"] += list(sems_c)
        return (*full["arrs"], full["conv"])

    lands, started = [None] * N_BIG, {}

    def emit(layer, w, g):
        if lands[w] is None:
            lands[w] = lax.empty(_slot_shape(g, w), g.dtype)
        *started[layer, w], lands[w], token = _scatter_start(g, lands[w], layer, w, "scatter_start_%d_%d" % (layer, w))
        return token[0, 0]

    loss_tile, dx, grads, dg_final = _local_step(_to_strips(x[0], placed[0], "to_strips_x"),
                                                 _to_strips(loss_target[0], placed[0], "to_strips_target"), fetch,
                                                 w_ff_in.shape[2] * N_CHIPS,
                                                 sinks, g_mix, g_group, g_mlp, g_final, emit)

    wmv = ((w_in, m_w_in, v_w_in), (w_o, m_w_o, v_w_o), (w_ff_in, m_w_ff_in, v_w_ff_in),
           (w_ff_out, m_w_ff_out, v_w_ff_out))
    big, after = [None] * N_BIG, dx
    for name, ws in (("swap_layers_rest", (1, 2, 3)), ("swap_layers_in", (0,))):
        own = []
        for w in ws:
            g0, g1, slots = _scatter_wait((started[0, w], started[1, w]), lands[w], after, w, "scatter_wait_%d" % w)
            own.append(_sum_slots(g0, g1, slots, w, pos_arr, "sum_slots_%d" % w))
        for w, mine, theirs in zip(ws, own, _swap_layers(own, name)):
            big[w] = _adamw(wmv[w][0], mine, theirs, wmv[w][1], wmv[w][2], pos_arr, "adamw_%d" % w)
        after = big[ws[-1]][1]

    def both(i):
        return jnp.stack([grads[0][i][0], grads[1][i][0]])
    dconv = jnp.stack([grads[0][0][:3], grads[1][0][:3]])
    dsinks = jnp.stack([grads[0][1][0, ::HEAD_DIM], grads[1][1][0, ::HEAD_DIM]])
    part = _pack_small([both(2), both(3), both(4), dg_final[0], dconv, dsinks, loss_tile[0, 0]])

    def spread(shard):
        return lax.dynamic_update_slice(jnp.zeros((2, 3, CONV_CH), F32), shard, (0, 0, chip * conv_n))
    zero = jnp.zeros((), F32)
    packs = [_pack_small([a, b, c_, e, spread(f), g_, zero]) for a, b, c_, e, f, g_ in (
        (g_mix, g_group, g_mlp, g_final, conv_w, sinks),
        (m_g_mix, m_g_group, m_g_mlp, m_g_final, m_conv_w, m_sinks),
        (v_g_mix, v_g_group, v_g_mlp, v_g_final, v_conv_w, v_sinks))]
    shapes = [g_mix.shape, g_group.shape, g_mlp.shape, g_final.shape, (2, 3, CONV_CH), sinks.shape, ()]
    small = [_unpack_small(p, shapes) for p in _small_sync(part, *packs)]

    def shard_of(full):
        return lax.dynamic_slice(full, (0, 0, chip * conv_n), (2, 3, conv_n))
    small = [(s[0], s[1], s[2], s[3], shard_of(s[4]), s[5], s[6]) for s in small]
    loss = small[0][6]

    def ordered(kind):
        b = [big[i][kind] for i in range(N_BIG)]
        s = small[kind]
        return [b[0], s[4], s[5], s[0], s[1], b[1], s[2], b[2], b[3], s[3]]

    return (loss, dx[None], *ordered(0), *ordered(1), *ordered(2), *ordered(3))
```

```python
import functools

import jax
import jax.numpy as jnp
from jax import lax
from jax.experimental import pallas as pl
from jax.experimental.pallas import tpu as pltpu

HEAD_DIM = 64
N_HEADS = 6
C_GROUP = 3
A_WIDTH = N_HEADS * HEAD_DIM
C_KV_WIDTH = 2 * HEAD_DIM
CONV_CH = 256
ZA_W = 3 * A_WIDTH
ZB_W = 3 * CONV_CH
ZC_W = A_WIDTH + 2 * C_KV_WIDTH
IN_WIDTH = ZA_W + ZB_W + ZC_W
MIX_WIDTH = A_WIDTH + CONV_CH + A_WIDTH
DILATIONS = (1, 4, 16)
A_MAX_DIST = 128
C_MAX_DIST = 127
TQ = 128
EPS = 1e-6
SCALE = HEAD_DIM ** -0.5
NEG = -1e30
HALO = 8

ADAM_LR = 0.001
ADAM_B1 = 0.9
ADAM_B2 = 0.999
ADAM_EPS = 1e-08
ADAM_WD = 0.01
ADAM_STEP = 10

BF = jnp.bfloat16
F32 = jnp.float32
MESH = pl.DeviceIdType.MESH
VMEM_LIMIT = 56 * 1024 * 1024


def _cparams(*sem):
    return pltpu.CompilerParams(dimension_semantics=sem, vmem_limit_bytes=VMEM_LIMIT)


def _nt(a, b):
    return lax.dot_general(a, b, (((1,), (1,)), ((), ())), preferred_element_type=F32)


def _tn(a, b):
    return lax.dot_general(a, b, (((0,), (0,)), ((), ())), preferred_element_type=F32)


def _nn(a, b):
    return jnp.dot(a, b, preferred_element_type=F32)


def _rows(tb, w):
    return pl.BlockSpec((tb, w), lambda i: (i, 0))


def _whole(shape):
    return pl.BlockSpec(shape, lambda *_: (0,) * len(shape))


def _layer(shape, l):
    return pl.BlockSpec((None,) + shape, lambda *_: (l,) + (0,) * len(shape))


def _rms_scale(v):
    return lax.rsqrt(jnp.mean(v * v, axis=-1, keepdims=True) + EPS)


def _norm_bwd(dxhat, xhat, r):
    return r * (dxhat - xhat * jnp.mean(dxhat * xhat, axis=-1, keepdims=True))


def _qkv_fwd(x, g, w_all, l, tb):
    s, d = x.shape

    def body(x_ref, g_ref, w_ref, h_ref, za_ref, zb_ref, zc_ref):
        xv = x_ref[...]
        h = ((xv * _rms_scale(xv)) * g_ref[...]).astype(BF)
        h_ref[...] = h
        z = jnp.concatenate([_nn(h, w_ref[k]) for k in range(N_CHIPS)], axis=1)
        za_ref[...] = z[:, :ZA_W]
        zb_ref[...] = z[:, ZA_W:ZA_W + ZB_W]
        zc_ref[...] = z[:, ZA_W + ZB_W:]

    return pl.pallas_call(
        body, grid=(s // tb,), name="qkv_fwd",
        in_specs=[_rows(tb, d), _whole((1, d)), _layer((N_CHIPS, d, IN_WIDTH // N_CHIPS), l)],
        out_specs=[_rows(tb, d), _rows(tb, ZA_W), _rows(tb, ZB_W), _rows(tb, ZC_W)],
        out_shape=[jax.ShapeDtypeStruct((s, d), BF), jax.ShapeDtypeStruct((s, ZA_W), F32),
                   jax.ShapeDtypeStruct((s, ZB_W), F32), jax.ShapeDtypeStruct((s, ZC_W), F32)],
        compiler_params=_cparams("parallel"),
    )(x, g, w_all)


N_STRIPS = 16


def _strips(a):
    s, w = a.shape
    return a.reshape(4, 4, s // N_STRIPS, w)


def _p_grid(s, dil):
    na = s // N_STRIPS
    return {16: (4, 4, na // TQ), 4: (4, na // 32), 1: (na // 8,)}[dil]


def _p_spec(dil, cw, col, prev=False):
    def blk(j):
        return jnp.maximum(j - 1, 0) if prev else j
    if dil == 16:
        return pl.BlockSpec((None, None, TQ, cw), lambda f, e, j: (f, e, blk(j), col))
    if dil == 4:
        return pl.BlockSpec((None, 4, 32, cw), lambda f, j: (f, 0, blk(j), col))
    return pl.BlockSpec((4, 4, 8, cw), lambda j: (0, 0, blk(j), col))


def _block_pos(i, dil):
    if dil == 16:
        return i
    if dil == 4:
        return 4 * (i % 32) + i // 32
    return 16 * (i % 8) + 4 * ((i // 8) % 4) + i // 32


def _band_mask(b, dil, max_dist):
    qi = _block_pos(lax.broadcasted_iota(jnp.int32, (TQ, 2 * TQ), 0), dil)
    col = lax.broadcasted_iota(jnp.int32, (TQ, 2 * TQ), 1)
    cur = col >= TQ
    dist = qi - _block_pos(col % TQ, dil) + jnp.where(cur, 0, TQ)
    return (dist >= 0) & (dist <= max_dist) & (cur | (b > 0))


def _hs(h):
    return slice(h * HEAD_DIM, (h + 1) * HEAD_DIM)


def _ld(ref, cols):
    v = ref[..., cols]
    return v.reshape(TQ, v.shape[-1])


def _st(ref, cols, val):
    ref[..., cols] = val.reshape(ref.shape[:-1] + (val.shape[-1],))


def _attn_fwd(z, dil, kw, kcol, vcol, n_rep, max_dist, name):
    s, zw = z.shape
    grid = _p_grid(s, dil)

    def body(q_ref, kp_ref, kc_ref, vp_ref, vc_ref, o_ref, lse_ref):
        mask = _band_mask(pl.program_id(len(grid) - 1), dil, max_dist)
        scs, v2s = [], []
        for kh in range(N_HEADS // n_rep):
            k2 = jnp.concatenate([_ld(kp_ref, _hs(kh)), _ld(kc_ref, _hs(kh))], axis=0).astype(BF)
            v2s.append(jnp.concatenate([_ld(vp_ref, _hs(kh)), _ld(vc_ref, _hs(kh))], axis=0).astype(BF))
            for h in range(kh * n_rep, (kh + 1) * n_rep):
                q = (_ld(q_ref, _hs(h)) * SCALE).astype(BF)
                scs.append(jnp.where(mask, _nt(q, k2), NEG))
        for h, sc in enumerate(scs):
            m = jnp.max(sc, axis=1, keepdims=True)
            p = jnp.exp(sc - m)
            l = jnp.sum(p, axis=1, keepdims=True)
            _st(o_ref, _hs(h), _nn(p.astype(BF), v2s[h // n_rep]) / l)
            _st(lse_ref, _hs(h), jnp.broadcast_to(m + jnp.log(l), (TQ, HEAD_DIM)))

    res = pl.pallas_call(
        body, grid=grid, name=name,
        in_specs=[_p_spec(dil, A_WIDTH, 0), _p_spec(dil, kw, kcol, True), _p_spec(dil, kw, kcol),
                  _p_spec(dil, kw, vcol, True), _p_spec(dil, kw, vcol)],
        out_specs=[_p_spec(dil, A_WIDTH, 0)] * 2,
        out_shape=[jax.ShapeDtypeStruct((4, 4, s // N_STRIPS, A_WIDTH), F32)] * 2,
        compiler_params=_cparams(*(("parallel",) * len(grid))),
    )(*[_strips(z)] * 5)
    return [a.reshape(s, A_WIDTH) for a in res]


def _attn_merge(parts_a, part_c, sink_row, tb):
    s = part_c[0].shape[0]
    n_a = len(parts_a)

    def body(*refs):
        ins, sink_ref = refs[:2 * n_a + 2], refs[2 * n_a + 2]
        ya_ref, lsea_ref, yc_ref, lsec_ref = refs[2 * n_a + 3:]
        lses = [ins[2 * p + 1][...] for p in range(n_a)]
        m = functools.reduce(jnp.maximum, lses)
        ws = [jnp.exp(v - m) for v in lses]
        l = functools.reduce(jnp.add, ws)
        ya_ref[...] = functools.reduce(jnp.add, [w * ins[2 * p][...] for p, w in enumerate(ws)]) / l
        lsea_ref[...] = m + jnp.log(l)
        o_c, lse_c = [r[...] for r in ins[2 * n_a:]]
        sk = sink_ref[...]
        m2 = jnp.maximum(lse_c, sk)
        w = jnp.exp(lse_c - m2)
        l2 = w + jnp.exp(sk - m2)
        yc_ref[...] = o_c * (w / l2)
        lsec_ref[...] = m2 + jnp.log(l2)

    return pl.pallas_call(
        body, grid=(s // tb,), name="attn_merge",
        in_specs=[_rows(tb, A_WIDTH)] * (2 * n_a + 2) + [_whole((1, A_WIDTH))],
        out_specs=[_rows(tb, A_WIDTH)] * 4, out_shape=[jax.ShapeDtypeStruct((s, A_WIDTH), F32)] * 4,
        compiler_params=_cparams("parallel"),
    )(*[a for part in parts_a + [part_c] for a in part], sink_row)


def _shift_down(v, n, halo):
    rows = v.shape[0]
    out = pltpu.roll(v, n, 0)
    row = lax.broadcasted_iota(jnp.int32, v.shape, 0)
    for t in range(n):
        out = jnp.where(row == t, halo[HALO - n + t:HALO - n + t + 1, :], out)
    return out


def _shift_up(v, n, halo):
    rows = v.shape[0]
    out = pltpu.roll(v, rows - n, 0)
    row = lax.broadcasted_iota(jnp.int32, v.shape, 0)
    for t in range(n):
        out = jnp.where(row == rows - n + t, halo[t:t + 1, :], out)
    return out


def _strip(v, b):
    return v[b % 4, b // 4]


def _conv_strips(zb, prev, cw):
    gb = [_strip(zb, b)[:, :CONV_CH] for b in range(N_STRIPS)]
    gc = [_strip(zb, b)[:, CONV_CH:2 * CONV_CH] for b in range(N_STRIPS)]
    xb = [_strip(zb, b)[:, 2 * CONV_CH:] for b in range(N_STRIPS)]
    u = [g * v for g, v in zip(gc, xb)]
    uh = prev[:, :, CONV_CH:2 * CONV_CH] * prev[:, :, 2 * CONV_CH:]
    wrapped = {14: _shift_down(u[14], 1, uh[2]), 15: _shift_down(u[15], 1, uh[3])}
    u1 = [u[b - 1] if b >= 1 else wrapped[15] for b in range(N_STRIPS)]
    u2 = [u[b - 2] if b >= 2 else wrapped[14 + b] for b in range(N_STRIPS)]
    c = [cw[0:1, :] * u2[b] + cw[1:2, :] * u1[b] + cw[2:3, :] * u[b] for b in range(N_STRIPS)]
    return gb, gc, xb, u, u1, u2, c


def _strip_rows(ta, w):
    return pl.BlockSpec((4, 4, ta, w), lambda i: (0, 0, i, 0))


def _prev_rows(ta, w):
    return pl.BlockSpec((4, None, HALO, w), lambda i: (0, 3, jnp.maximum(i * (ta // HALO) - 1, 0), 0))


def _next_rows(ta, w, nblk):
    return pl.BlockSpec((4, None, HALO, w),
                        lambda i: (0, 0, jnp.minimum((i + 1) * (ta // HALO), nblk * (ta // HALO) - 1), 0))


def _mix_fwd(x, ya, yc, zb, cw, gg, wo_all, l, tb):
    s, d = x.shape
    ta = tb // N_STRIPS

    def body(x_ref, ya_ref, yc_ref, zb_ref, zbp_ref, cw_ref, gg_ref, wo_ref, x1_ref, yb_ref):
        i = pl.program_id(0)
        prev = jnp.where(i > 0, zbp_ref[...], 0.0)
        gb, _, _, _, _, _, c = _conv_strips(zb_ref[...], prev, cw_ref[...])
        for b in range(N_STRIPS):
            yb_ref[b % 4, b // 4] = gb[b] * c[b]
        yb = yb_ref[...].reshape(tb, CONV_CH)
        ya, yc = ya_ref[...].reshape(tb, A_WIDTH), yc_ref[...].reshape(tb, A_WIDTH)
        n = jnp.concatenate([ya * _rms_scale(ya), yb * _rms_scale(yb), yc * _rms_scale(yc)], axis=1)
        n = (n * gg_ref[...]).astype(BF)
        x1 = x_ref[...].reshape(tb, d) + _nn(n, wo_ref[...].reshape(MIX_WIDTH, d))
        x1_ref[...] = x1.reshape(4, 4, ta, d)

    res = pl.pallas_call(
        body, grid=(s // tb,), name="mix_fwd",
        in_specs=[_strip_rows(ta, d), _strip_rows(ta, A_WIDTH), _strip_rows(ta, A_WIDTH), _strip_rows(ta, ZB_W),
                  _prev_rows(ta, ZB_W), _whole((HALO, CONV_CH)), _whole((1, MIX_WIDTH)),
                  _layer((N_CHIPS, MIX_WIDTH // N_CHIPS, d), l)],
        out_specs=[_strip_rows(ta, d), _strip_rows(ta, CONV_CH)],
        out_shape=[jax.ShapeDtypeStruct((4, 4, s // N_STRIPS, d), F32),
                   jax.ShapeDtypeStruct((4, 4, s // N_STRIPS, CONV_CH), F32)],
        compiler_params=_cparams("parallel"),
    )(_strips(x), _strips(ya), _strips(yc), _strips(zb), _strips(zb), cw, gg, wo_all)
    return res[0].reshape(s, d), res[1].reshape(s, CONV_CH)


def _mlp_fwd(x1, g, w1_all, w2_all, l, tb, tf):
    s, d = x1.shape
    ff = w1_all.shape[1] * w1_all.shape[3]
    nj = ff // tf

    def body(x_ref, g_ref, w1_ref, w2_ref, x2_ref, h2_ref, ap_ref, acc):
        j = pl.program_id(1)

        @pl.when(j == 0)
        def _():
            xv = x_ref[...]
            h2_ref[...] = ((xv * _rms_scale(xv)) * g_ref[...]).astype(BF)
            acc[...] = jnp.zeros_like(acc)

        ap = _nn(h2_ref[...], w1_ref[...])
        ap_ref[...] = ap.astype(BF)
        a = jnp.square(jnp.maximum(ap, 0.0)).astype(BF)
        acc[...] += _nn(a, w2_ref[...])

        @pl.when(j == nj - 1)
        def _():
            x2_ref[...] = x_ref[...] + acc[...]

    return pl.pallas_call(
        body, grid=(s // tb, nj), name="mlp_fwd",
        in_specs=[pl.BlockSpec((tb, d), lambda i, j: (i, 0)), _whole((1, d)),
                  pl.BlockSpec((None, None, d, tf), lambda i, j: (l, j, 0, 0)),
                  pl.BlockSpec((None, None, tf, d), lambda i, j: (l, j, 0, 0))],
        out_specs=[pl.BlockSpec((tb, d), lambda i, j: (i, 0)), pl.BlockSpec((tb, d), lambda i, j: (i, 0)),
                   pl.BlockSpec((tb, tf), lambda i, j: (i, j))],
        out_shape=[jax.ShapeDtypeStruct((s, d), F32), jax.ShapeDtypeStruct((s, d), BF),
                   jax.ShapeDtypeStruct((s, ff), BF)],
        scratch_shapes=[pltpu.VMEM((tb, d), F32)],
        compiler_params=_cparams("parallel", "arbitrary"),
    )(x1, g, w1_all, w2_all)


def _loss_head(x, g, tgt, tb):
    s, d = x.shape

    def body(x_ref, g_ref, t_ref, dx_ref, loss_ref, dg_ref):
        i = pl.program_id(0)

        @pl.when(i == 0)
        def _():
            loss_ref[...] = jnp.zeros_like(loss_ref)
            dg_ref[...] = jnp.zeros_like(dg_ref)

        xv = x_ref[...]
        r = _rms_scale(xv)
        xhat = xv * r
        err = xhat * g_ref[...] - t_ref[...]
        part = jnp.sum(jnp.mean(jnp.square(err), axis=-1, keepdims=True), axis=0, keepdims=True)
        loss_ref[...] += 0.5 * part
        dy = err * (1.0 / d)
        dg_ref[...] += jnp.sum(dy * xhat, axis=0, keepdims=True)
        dx_ref[...] = _norm_bwd(dy * g_ref[...], xhat, r)

    return pl.pallas_call(
        body, grid=(s // tb,), name="loss_head",
        in_specs=[_rows(tb, d), _whole((1, d)), _rows(tb, d)],
        out_specs=[_rows(tb, d), _whole((HALO, 128)), _whole((HALO, d))],
        out_shape=[jax.ShapeDtypeStruct((s, d), F32), jax.ShapeDtypeStruct((HALO, 128), F32),
                   jax.ShapeDtypeStruct((HALO, d), F32)],
        compiler_params=_cparams("arbitrary"),
    )(x, g, tgt)


def _mlp_bwd(dx2, x1, ap, g, w1_all, w2_all, l, tb, tf):
    s, d = x1.shape
    ff = ap.shape[1]
    nj = ff // tf

    def body(dx2_ref, x1_ref, ap_ref, g_ref, w1_ref, w2_ref, dx1_ref, dap_ref, dg_ref, acc):
        i, j = pl.program_id(0), pl.program_id(1)

        @pl.when((i == 0) & (j == 0))
        def _():
            dg_ref[...] = jnp.zeros_like(dg_ref)

        @pl.when(j == 0)
        def _():
            acc[...] = jnp.zeros_like(acc)

        da = _nt(dx2_ref[...].astype(BF), w2_ref[...])
        dap = (da * (2.0 * jnp.maximum(ap_ref[...].astype(F32), 0.0))).astype(BF)
        dap_ref[...] = dap
        acc[...] += _nt(dap, w1_ref[...])

        @pl.when(j == nj - 1)
        def _():
            xv = x1_ref[...]
            r = _rms_scale(xv)
            xhat = xv * r
            dh = acc[...]
            dg_ref[...] += jnp.sum(dh * xhat, axis=0, keepdims=True)
            dx1_ref[...] = dx2_ref[...] + _norm_bwd(dh * g_ref[...], xhat, r)

    return pl.pallas_call(
        body, grid=(s // tb, nj), name="mlp_bwd",
        in_specs=[pl.BlockSpec((tb, d), lambda i, j: (i, 0)), pl.BlockSpec((tb, d), lambda i, j: (i, 0)),
                  pl.BlockSpec((tb, tf), lambda i, j: (i, j)),
                  _whole((1, d)), pl.BlockSpec((None, None, d, tf), lambda i, j: (l, j, 0, 0)),
                  pl.BlockSpec((None, None, tf, d), lambda i, j: (l, j, 0, 0))],
        out_specs=[pl.BlockSpec((tb, d), lambda i, j: (i, 0)), pl.BlockSpec((tb, tf), lambda i, j: (i, j)),
                   _whole((HALO, d))],
        out_shape=[jax.ShapeDtypeStruct((s, d), F32), jax.ShapeDtypeStruct((s, ff), BF),
                   jax.ShapeDtypeStruct((HALO, d), F32)],
        scratch_shapes=[pltpu.VMEM((tb, d), F32)],
        compiler_params=_cparams("arbitrary", "arbitrary"),
    )(dx2, x1, ap, g, w1_all, w2_all)


def _wgrad(a, b, tm, tn, ts, name, relu2=False):
    s, m = a.shape
    n = b.shape[1]
    ns = s // ts

    def body(a_ref, b_ref, o_ref, acc):
        k = pl.program_id(2)

        @pl.when(k == 0)
        def _():
            acc[...] = jnp.zeros_like(acc)

        av = a_ref[...]
        if relu2:
            av = jnp.square(jnp.maximum(av.astype(F32), 0.0)).astype(BF)
        acc[...] += _tn(av, b_ref[...].astype(BF))

        @pl.when(k == ns - 1)
        def _():
            o_ref[...] = acc[...].astype(BF)

    return pl.pallas_call(
        body, grid=(m // tm, n // tn, ns), name=name,
        in_specs=[pl.BlockSpec((ts, tm), lambda i, j, k: (k, i)), pl.BlockSpec((ts, tn), lambda i, j, k: (k, j))],
        out_specs=pl.BlockSpec((tm, tn), lambda i, j, k: (i, j)),
        out_shape=jax.ShapeDtypeStruct((m, n), BF),
        scratch_shapes=[pltpu.VMEM((tm, tn), F32)],
        compiler_params=_cparams("parallel", "parallel", "arbitrary"),
    )(a, b)


def _mix_bwd(dx1, ya, yb, yc, lse_c, sink_row, gg, wo_all, l, tb):
    s, d = dx1.shape

    def body(dx_ref, ya_ref, yb_ref, yc_ref, lse_ref, sink_ref, gg_ref, wo_ref,
             n_ref, dya_ref, dyc_ref, da_ref, dc_ref, dyb_ref, dg_ref, dsink_ref):
        i = pl.program_id(0)

        @pl.when(i == 0)
        def _():
            dg_ref[...] = jnp.zeros_like(dg_ref)
            dsink_ref[...] = jnp.zeros_like(dsink_ref)

        dn = _nt(dx_ref[...].astype(BF), wo_ref[...].reshape(MIX_WIDTH, d))
        ys = [ya_ref[...], yb_ref[...], yc_ref[...]]
        rs = [_rms_scale(v) for v in ys]
        nhat = jnp.concatenate([v * r for v, r in zip(ys, rs)], axis=1)
        gg = gg_ref[...]
        n_ref[...] = (nhat * gg).astype(BF)
        dg_ref[...] += jnp.sum(dn * nhat, axis=0, keepdims=True)
        dnh = dn * gg
        bounds = [(0, A_WIDTH), (A_WIDTH, A_WIDTH + CONV_CH), (A_WIDTH + CONV_CH, MIX_WIDTH)]
        dys = [_norm_bwd(dnh[:, lo:hi], nhat[:, lo:hi], r) for (lo, hi), r in zip(bounds, rs)]
        dyb_ref[...] = dys[1]
        for dy, y, dy_ref, dd_ref in ((dys[0], ys[0], dya_ref, da_ref), (dys[2], ys[2], dyc_ref, dc_ref)):
            dy_ref[...] = dy
            t = dy * y
            for h in range(N_HEADS):
                dd_ref[:, _hs(h)] = jnp.broadcast_to(jnp.sum(t[:, _hs(h)], axis=1, keepdims=True), (tb, HEAD_DIM))
        dsink_ref[...] -= jnp.sum(jnp.exp(sink_ref[...] - lse_ref[...]) * dc_ref[...], axis=0, keepdims=True)

    return pl.pallas_call(
        body, grid=(s // tb,), name="mix_bwd",
        in_specs=[_rows(tb, d), _rows(tb, A_WIDTH), _rows(tb, CONV_CH), _rows(tb, A_WIDTH), _rows(tb, A_WIDTH),
                  _whole((1, A_WIDTH)), _whole((1, MIX_WIDTH)), _layer((N_CHIPS, MIX_WIDTH // N_CHIPS, d), l)],
        out_specs=[_rows(tb, MIX_WIDTH), _rows(tb, A_WIDTH), _rows(tb, A_WIDTH), _rows(tb, A_WIDTH),
                   _rows(tb, A_WIDTH), _rows(tb, CONV_CH), _whole((HALO, MIX_WIDTH)), _whole((HALO, A_WIDTH))],
        out_shape=[jax.ShapeDtypeStruct((s, MIX_WIDTH), BF), jax.ShapeDtypeStruct((s, A_WIDTH), F32),
                   jax.ShapeDtypeStruct((s, A_WIDTH), F32), jax.ShapeDtypeStruct((s, A_WIDTH), F32),
                   jax.ShapeDtypeStruct((s, A_WIDTH), F32), jax.ShapeDtypeStruct((s, CONV_CH), F32),
                   jax.ShapeDtypeStruct((HALO, MIX_WIDTH), F32), jax.ShapeDtypeStruct((HALO, A_WIDTH), F32)],
        compiler_params=_cparams("arbitrary"),
    )(dx1, ya, yb, yc, lse_c, sink_row, gg, wo_all)


def _attn_bwd(z, dy, lse, dd, dil, kw, kcol, vcol, n_rep, max_dist, name):
    s, zw = z.shape
    grid = _p_grid(s, dil)
    n_kv = N_HEADS // n_rep
    dt = F32 if dil == 1 else BF

    def body(q_ref, kp_ref, kc_ref, vp_ref, vc_ref, dy_ref, lse_ref, dd_ref, dq_ref, dkp_ref, dkc_ref, dvp_ref, dvc_ref):
        mask = _band_mask(pl.program_id(len(grid) - 1), dil, max_dist)
        k2s, qs, dys, scs, dps = [], [], [], [], []
        for kh in range(n_kv):
            k2s.append(jnp.concatenate([_ld(kp_ref, _hs(kh)), _ld(kc_ref, _hs(kh))], axis=0).astype(BF))
            v2 = jnp.concatenate([_ld(vp_ref, _hs(kh)), _ld(vc_ref, _hs(kh))], axis=0).astype(BF)
            for h in range(kh * n_rep, (kh + 1) * n_rep):
                qs.append((_ld(q_ref, _hs(h)) * SCALE).astype(BF))
                dys.append(_ld(dy_ref, _hs(h)).astype(BF))
                scs.append(jnp.where(mask, _nt(qs[h], k2s[kh]), NEG))
                dps.append(_nt(dys[h], v2))
        for kh in range(n_kv):
            k2 = k2s[kh]
            dk2 = jnp.zeros((2 * TQ, HEAD_DIM), F32)
            dv2 = jnp.zeros((2 * TQ, HEAD_DIM), F32)
            for h in range(kh * n_rep, (kh + 1) * n_rep):
                lse_h = _ld(lse_ref, slice(h * HEAD_DIM, h * HEAD_DIM + 1))
                dd_h = _ld(dd_ref, slice(h * HEAD_DIM, h * HEAD_DIM + 1))
                p = jnp.exp(scs[h] - lse_h)
                ds = (p * (dps[h] - dd_h)).astype(BF)
                _st(dq_ref, _hs(h), (_nn(ds, k2) * SCALE).astype(dq_ref.dtype))
                dk2 = dk2 + _tn(ds, qs[h])
                dv2 = dv2 + _tn(p.astype(BF), dys[h])
            _st(dkp_ref, _hs(kh), dk2[:TQ].astype(dt))
            _st(dkc_ref, _hs(kh), dk2[TQ:].astype(dt))
            _st(dvp_ref, _hs(kh), dv2[:TQ].astype(dt))
            _st(dvc_ref, _hs(kh), dv2[TQ:].astype(dt))

    args = [_strips(z)] * 5 + [_strips(a) for a in (dy, lse, dd)]
    in_specs = [_p_spec(dil, A_WIDTH, 0), _p_spec(dil, kw, kcol, True), _p_spec(dil, kw, kcol),
                _p_spec(dil, kw, vcol, True), _p_spec(dil, kw, vcol)] + [_p_spec(dil, A_WIDTH, 0)] * 3
    out_specs = [_p_spec(dil, A_WIDTH, 0)] + [_p_spec(dil, kw, 0)] * 4
    na = s // N_STRIPS
    out_shape = [jax.ShapeDtypeStruct((4, 4, na, A_WIDTH), dt)] + [jax.ShapeDtypeStruct((4, 4, na, kw), dt)] * 4
    res = pl.pallas_call(
        body, grid=grid, name=name, in_specs=in_specs, out_specs=out_specs, out_shape=out_shape,
        compiler_params=_cparams(*(("parallel",) * len(grid))),
    )(*args)
    return [res[0].reshape(s, A_WIDTH)] + [a.reshape(s, kw) for a in res[1:]]


DZ_TA = 16


def _dz_assemble(parts_a, parts_c, dyb, zb, cw):
    s = zb.shape[0]
    na = s // N_STRIPS
    nb = na // DZ_TA

    def ahead(w, k):
        return pl.BlockSpec((4, 4, DZ_TA, w), lambda i: (0, 0, jnp.minimum(i + k, nb - 1), 0))

    args, in_specs = [], []
    for dil, (dq, dkp, dkc, dvp, dvc) in zip(DILATIONS + (1,), parts_a + [parts_c]):
        w = dkp.shape[1]
        here = _strip_rows(DZ_TA, w)
        if dil == 1:
            args += [dq, dkp, dkp, dkc, dvp, dvp, dvc]
            in_specs += [_strip_rows(DZ_TA, A_WIDTH), here, ahead(w, 1), here, here, ahead(w, 1), here]
        else:
            k = 8 * dil // DZ_TA
            args += [dq, dkp, dkc, dvp, dvc]
            in_specs += [_strip_rows(DZ_TA, A_WIDTH), ahead(w, k), here, ahead(w, k), here]
    n_att = len(args)
    args = [_strips(a) for a in args] + [_strips(dyb), _strips(dyb), _strips(zb), _strips(zb), _strips(zb), cw]
    in_specs += [_strip_rows(DZ_TA, CONV_CH), _next_rows(DZ_TA, CONV_CH, nb), _strip_rows(DZ_TA, ZB_W),
                 _prev_rows(DZ_TA, ZB_W), _next_rows(DZ_TA, ZB_W, nb), _whole((HALO, CONV_CH))]

    def body(*refs):
        att = list(refs[:n_att])
        dyb_ref, dybn_ref, zb_ref, zbp_ref, zbn_ref, cw_ref, dz_ref, dcw_ref = refs[n_att:]
        i = pl.program_id(0)

        @pl.when(i == 0)
        def _():
            dcw_ref[...] = jnp.zeros_like(dcw_ref)

        def shifted(dil):
            if dil == 1:
                dq_r, kp0, kp1, dkc_r, vp0, vp1, dvc_r = [att.pop(0) for _ in range(7)]
                live = i + 1 < nb
                half = DZ_TA // 2
                dkp = jnp.concatenate([kp0[:, :, half:, :], jnp.where(live, kp1[:, :, :half, :], 0.0)], axis=2)
                dvp = jnp.concatenate([vp0[:, :, half:, :], jnp.where(live, vp1[:, :, :half, :], 0.0)], axis=2)
            else:
                dq_r, dkp_r, dkc_r, dvp_r, dvc_r = [att.pop(0) for _ in range(5)]
                live = i + 8 * dil // DZ_TA < nb
                dkp = jnp.where(live, dkp_r[...].astype(F32), 0.0)
                dvp = jnp.where(live, dvp_r[...].astype(F32), 0.0)
            return dq_r[...].astype(F32), dkc_r[...].astype(F32) + dkp, dvc_r[...].astype(F32) + dvp

        dq, dk, dv = shifted(DILATIONS[0])
        for dil in DILATIONS[1:]:
            dq2, dk2, dv2 = shifted(dil)
            dq, dk, dv = dq + dq2, dk + dk2, dv + dv2
        dz_ref[:, :, :, 0:A_WIDTH] = dq.astype(BF)
        dz_ref[:, :, :, A_WIDTH:2 * A_WIDTH] = dk.astype(BF)
        dz_ref[:, :, :, 2 * A_WIDTH:ZA_W] = dv.astype(BF)
        dq, dk, dv = shifted(1)
        c0 = ZA_W + ZB_W
        dz_ref[:, :, :, c0:c0 + A_WIDTH] = dq.astype(BF)
        dz_ref[:, :, :, c0 + A_WIDTH:c0 + A_WIDTH + C_KV_WIDTH] = dk.astype(BF)
        dz_ref[:, :, :, c0 + A_WIDTH + C_KV_WIDTH:IN_WIDTH] = dv.astype(BF)

        cw = cw_ref[...]
        prev = jnp.where(i > 0, zbp_ref[...], 0.0)
        gb, gc, xb, u, u1, u2, c = _conv_strips(zb_ref[...], prev, cw)
        dyb = dyb_ref[...]
        dc = [_strip(dyb, b) * gb[b] for b in range(N_STRIPS)]
        dcn = jnp.where(i + 1 < nb, dybn_ref[...] * zbn_ref[:, :, :CONV_CH], 0.0)
        wrapped = [_shift_up(dc[0], 1, dcn[0]), _shift_up(dc[1], 1, dcn[1])]
        upd = [jnp.zeros((1, CONV_CH), F32)] * 3
        for b in range(N_STRIPS):
            dc1 = dc[b + 1] if b + 1 < N_STRIPS else wrapped[0]
            dc2 = dc[b + 2] if b + 2 < N_STRIPS else wrapped[b + 2 - N_STRIPS]
            du = cw[2:3, :] * dc[b] + cw[1:2, :] * dc1 + cw[0:1, :] * dc2
            f, e = b % 4, b // 4
            dz_ref[f, e, :, ZA_W:ZA_W + CONV_CH] = (_strip(dyb, b) * c[b]).astype(BF)
            dz_ref[f, e, :, ZA_W + CONV_CH:ZA_W + 2 * CONV_CH] = (du * xb[b]).astype(BF)
            dz_ref[f, e, :, ZA_W + 2 * CONV_CH:c0] = (du * gc[b]).astype(BF)
            for t, uu in enumerate((u2[b], u1[b], u[b])):
                upd[t] = upd[t] + jnp.sum(dc[b] * uu, axis=0, keepdims=True)
        row = lax.broadcasted_iota(jnp.int32, (HALO, CONV_CH), 0)
        tile = jnp.zeros((HALO, CONV_CH), F32)
        for t in range(3):
            tile = jnp.where(row == t, upd[t], tile)
        dcw_ref[...] += tile

    dz, dcw = pl.pallas_call(
        body, grid=(nb,), name="dz_assemble", in_specs=in_specs,
        out_specs=[_strip_rows(DZ_TA, IN_WIDTH), _whole((HALO, CONV_CH))],
        out_shape=[jax.ShapeDtypeStruct((4, 4, na, IN_WIDTH), BF), jax.ShapeDtypeStruct((HALO, CONV_CH), F32)],
        compiler_params=_cparams("arbitrary"),
    )(*args)
    return dz.reshape(s, IN_WIDTH), dcw


def _qkv_bwd(dz, dx1, x, g, w_all, l, tb, tokens_out):
    s, d = x.shape
    na, ta = s // N_STRIPS, tb // N_STRIPS

    def body(dz_ref, dx1_ref, x_ref, g_ref, w_ref, dx_ref, dg_ref):
        i = pl.program_id(0)

        @pl.when(i == 0)
        def _():
            dg_ref[...] = jnp.zeros_like(dg_ref)

        n = IN_WIDTH // N_CHIPS
        dz = dz_ref[...].reshape(tb, IN_WIDTH)
        dh = _nt(dz[:, 0:n], w_ref[0])
        for k in range(1, N_CHIPS):
            dh = dh + _nt(dz[:, k * n:(k + 1) * n], w_ref[k])
        xv = x_ref[...].reshape(tb, d)
        r = _rms_scale(xv)
        xhat = xv * r
        dg_ref[...] += jnp.sum(dh * xhat, axis=0, keepdims=True)
        dx = (dx1_ref[...].reshape(tb, d) + _norm_bwd(dh * g_ref[...], xhat, r)).reshape(4, 4, ta, d)
        if tokens_out:
            for b in range(N_STRIPS):
                dx_ref[:, b, :] = _strip(dx, b)
        else:
            dx_ref[...] = dx

    if tokens_out:
        dx_spec, dx_shape = pl.BlockSpec((ta, N_STRIPS, d), lambda i: (i, 0, 0)), (na, N_STRIPS, d)
    else:
        dx_spec, dx_shape = _strip_rows(ta, d), (4, 4, na, d)
    dx, dg = pl.pallas_call(
        body, grid=(s // tb,), name="qkv_bwd",
        in_specs=[_strip_rows(ta, IN_WIDTH), _strip_rows(ta, d), _strip_rows(ta, d), _whole((1, d)),
                  _layer((N_CHIPS, d, IN_WIDTH // N_CHIPS), l)],
        out_specs=[dx_spec, _whole((HALO, d))],
        out_shape=[jax.ShapeDtypeStruct(dx_shape, F32), jax.ShapeDtypeStruct((HALO, d), F32)],
        compiler_params=_cparams("arbitrary"),
    )(_strips(dz), _strips(dx1), _strips(x), g, w_all)
    return dx.reshape(s, d), dg


def _tile_rows(rows):
    return jnp.pad(rows, ((0, HALO - rows.shape[0]), (0, 0)))


def _to_strips(a, after, name):
    s, d = a.shape
    na = s // N_STRIPS
    ta = min(32, na)

    def body(a_ref, after_ref, o_ref):
        for b in range(N_STRIPS):
            o_ref[b % 4, b // 4] = a_ref[:, b, :]

    return pl.pallas_call(
        body, grid=(na // ta,), name=name,
        in_specs=[pl.BlockSpec((ta, N_STRIPS, d), lambda i: (i, 0, 0)), ANY], out_specs=_strip_rows(ta, d),
        out_shape=jax.ShapeDtypeStruct((4, 4, na, d), a.dtype), compiler_params=_cparams("parallel"),
    )(a.reshape(na, N_STRIPS, d), after).reshape(s, d)


def _local_step(x, tgt, fetch, ff, sinks, g_mix, g_group, g_mlp, g_final, emit):
    s, d = x.shape
    depth = g_mix.shape[0]
    tb = min(512, s)
    tf = ff // N_CHIPS
    ts = min(1024, s)
    saved = []
    for l in range(depth):
        w_in, _, _, _, conv_w = fetch(0, l, x)
        cw = _tile_rows(conv_w[l])
        sk = jnp.repeat(sinks[l].reshape(N_HEADS), HEAD_DIM)[None]
        h, za, zb, zc = _qkv_fwd(x, g_mix[l][None], w_in, l, tb)
        parts_a = [_attn_fwd(za, dil, A_WIDTH, 1, 2, 1, A_MAX_DIST, "attn_a_fwd_%d" % dil) for dil in DILATIONS]
        part_c = _attn_fwd(zc, 1, C_KV_WIDTH, 3, 4, C_GROUP, C_MAX_DIST, "attn_c_fwd")
        ya, lse_a, yc, lse_c = _attn_merge(parts_a, part_c, sk, tb)
        w_in, w_o, w1, w2, _ = fetch(1, l, yc)
        x1, yb = _mix_fwd(x, ya, yc, zb, cw, g_group[l][None], w_o, l, tb)
        x2, h2, ap = _mlp_fwd(x1, g_mlp[l][None], w1, w2, l, ts, tf)
        saved.append((x, h, za, zb, zc, ya, lse_a, yc, lse_c, yb, x1, h2, ap, cw, sk))
        x = x2
    dx, loss_tile, dg_final = _loss_head(x, g_final[None], tgt, tb)
    grads = [None] * depth
    tok = jnp.zeros((), F32)
    for l in reversed(range(depth)):
        x0, h, za, zb, zc, ya, lse_a, yc, lse_c, yb, x1, h2, ap, cw, sk = saved[l]
        dx1, dap, dg_mlp = _mlp_bwd(dx, x1, ap, g_mlp[l][None] + tok, w1, w2, l, ts, tf)
        tok = emit(l, 3, _wgrad(ap, dx, min(1024, ff), d, ts, "wgrad_ff_out", relu2=True))
        tok = tok + emit(l, 2, _wgrad(h2, dap, d, min(1024, ff), 2 * ts, "wgrad_ff_in"))
        n, dya, dyc, dd_a, dd_c, dyb, dg_group, dsink = _mix_bwd(dx1, ya, yb, yc, lse_c, sk, g_group[l][None] + tok,
                                                                 w_o, l, tb)
        tok = emit(l, 1, _wgrad(n, dx1, MIX_WIDTH, d, ts, "wgrad_o"))
        cw = cw + tok
        parts_a = [_attn_bwd(za, dya, lse_a, dd_a, dil, A_WIDTH, 1, 2, 1, A_MAX_DIST, "attn_a_bwd_%d" % dil)
                   for dil in DILATIONS]
        parts_c = _attn_bwd(zc, dyc, lse_c, dd_c, 1, C_KV_WIDTH, 3, 4, C_GROUP, C_MAX_DIST, "attn_c_bwd")
        dz, dcw = _dz_assemble(parts_a, parts_c, dyb, zb, cw)
        tok = emit(l, 0, _wgrad(h, dz, d, IN_WIDTH // 4, 2 * ts, "wgrad_in"))
        dx, dg_mix = _qkv_bwd(dz, dx1, x0, g_mix[l][None] + tok, w_in, l, tb, l == 0)
        grads[l] = (dcw, dsink, dg_mix, dg_group, dg_mlp)
    return loss_tile, dx, grads, dg_final


ANY = pl.BlockSpec(memory_space=pl.ANY)
SHARD_AXES = (2, 1, 2, 1)
N_BIG = len(SHARD_AXES)
N_CHIPS = 4
N_DEV = 8


def _mesh_pos():
    return lax.axis_index("x"), lax.axis_index("y"), lax.axis_index("c")


def _flip(v, bit):
    return 1 - v if bit else v


def _place_shard(shard, chip_arr, name):
    _, rows, cols = shard.shape
    tr = min(256, rows)

    def body(chip_ref, x_ref, o_ref):
        o_ref[...] = x_ref[...].astype(BF)

    return pl.pallas_call(
        body, name=name,
        grid_spec=pltpu.PrefetchScalarGridSpec(
            num_scalar_prefetch=1, grid=(2, rows // tr),
            in_specs=[pl.BlockSpec((None, tr, cols), lambda l, i, chip: (l, i, 0))],
            out_specs=pl.BlockSpec((None, None, tr, cols), lambda l, i, chip: (l, chip[0], i, 0))),
        out_shape=jax.ShapeDtypeStruct((2, N_CHIPS, rows, cols), BF),
        compiler_params=_cparams("parallel", "parallel"),
    )(chip_arr, shard)


HBM = pl.BlockSpec(memory_space=pltpu.HBM)
SEM = pl.BlockSpec(memory_space=pltpu.SEMAPHORE)
EFFECT = pltpu.SideEffectType.DATAFLOW_SIDE_EFFECTING

GATHER_GROUPS = (((0, 0),), ((1, 0), (2, 0), (3, 0)), ((0, 1),), ((1, 1), (2, 1), (3, 1)))
GATHER_STARTS = ((0,), (1,), (2, 3))


def _gather_copies(arrs, group, send_sems, recv_sems):
    x, y, c = _mesh_pos()
    me = 2 * x + y
    out = []
    for i, (w, layer) in enumerate(group):
        mine = arrs[w].at[layer, me]
        for j, (qx, qy) in enumerate([(1 - x, y), (x, 1 - y), (1 - x, 1 - y)]):
            landed = arrs[w].at[layer, 2 * qx + qy]
            out.append(tuple(pltpu.make_async_remote_copy(
                src_ref=piece, dst_ref=piece, send_sem=send_sems.at[i * 3 + j], recv_sem=recv_sems.at[i * 3 + j],
                device_id=(qx, qy, c), device_id_type=MESH) for piece in (mine, landed)))
    return out


def _conv_copies(conv_src, conv_dst, send_sems, recv_sems):
    x, y, c = _mesh_pos()
    out = []
    for j, (qx, qy) in enumerate([(1 - x, y), (x, 1 - y), (1 - x, 1 - y)]):
        out.append(tuple(pltpu.make_async_remote_copy(
            src_ref=conv_src, dst_ref=conv_dst.at[q], send_sem=send_sems.at[j], recv_sem=recv_sems.at[j],
            device_id=(qx, qy, c), device_id_type=MESH) for q in (2 * x + y, 2 * qx + qy)))
    return out


def _gather_start(groups, arrs, conv, name, through=None):
    n_sems = 2 * (len(groups) + (conv is not None))
    mats = sorted({w for g in groups for w, _ in GATHER_GROUPS[g]})

    def body(*refs):
        arrs_ref = [None] * N_BIG
        for w, ref in zip(mats, refs):
            arrs_ref[w] = ref
        sems = refs[n_in:n_in + n_sems]
        if conv is not None:
            for cp, _ in _conv_copies(refs[len(mats)], refs[len(mats) + 1], sems[-2], sems[-1]):
                cp.start()
        for k, g in enumerate(groups):
            for cp, _ in _gather_copies(arrs_ref, GATHER_GROUPS[g], sems[2 * k], sems[2 * k + 1]):
                cp.start()

    sem_shapes = []
    for n in [len(GATHER_GROUPS[g]) for g in groups] + ([1] if conv is not None else []):
        sem_shapes += [pltpu.SemaphoreType.DMA((3 * n,))] * 2
    operands = [arrs[w] for w in mats] + ([] if conv is None else list(conv)) + ([] if through is None else [through])
    n_in = len(operands)
    res = pl.pallas_call(
        body, name=name,
        out_shape=tuple(sem_shapes) + tuple(pltpu.HBM(a.shape, a.dtype) for a in operands),
        in_specs=(HBM,) * n_in, out_specs=(SEM,) * n_sems + (HBM,) * n_in,
        input_output_aliases={i: n_sems + i for i in range(n_in)},
        compiler_params=pltpu.CompilerParams(has_side_effects=EFFECT),
    )(*[pltpu.with_memory_space_constraint(a, pltpu.HBM) for a in operands])
    arrs = list(arrs)
    for w, a in zip(mats, res[n_sems:]):
        arrs[w] = a
    return res[:n_sems], arrs, list(res[n_sems + len(mats):])


def _gather_wait(k, sems, arrs, conv, after, name):
    group = GATHER_GROUPS[k]
    mats = sorted({w for w, _ in group})
    n_conv = 0 if conv is None else 2

    def body(*refs):
        local = refs[:len(mats)]
        arrs_ref = [None] * N_BIG
        for w, ref in zip(mats, local):
            arrs_ref[w] = ref
        pos = len(mats) + n_conv
        copies = _gather_copies(arrs_ref, group, refs[pos], refs[pos + 1])
        if conv is not None:
            copies += _conv_copies(refs[len(mats)], refs[len(mats) + 1], refs[pos + 2], refs[pos + 3])
        for send, recv in copies:
            recv.wait_recv()
            send.wait_send()

    operands = [arrs[w] for w in mats] + ([] if conv is None else [conv[1], conv[2]])
    sem_ops = list(sems) + ([] if conv is None else list(conv[0]))
    n_op = len(operands)
    res = pl.pallas_call(
        body, name=name, out_shape=tuple(pltpu.HBM(a.shape, a.dtype) for a in operands),
        in_specs=(HBM,) * n_op + (SEM,) * len(sem_ops) + (ANY,), out_specs=(HBM,) * n_op,
        input_output_aliases={i: i for i in range(n_op)},
        compiler_params=pltpu.CompilerParams(has_side_effects=EFFECT),
    )(*operands, *sem_ops, after)
    arrs = list(arrs)
    for w, a in zip(mats, res):
        arrs[w] = a
    return arrs, (res[-1] if conv is not None else None)


def _grad_shard(ref, w, chip, n):
    start = pl.multiple_of(chip * n, 128)
    if SHARD_AXES[w] == 2:
        return ref.at[:, pl.ds(start, n)]
    return ref.at[pl.ds(start, n), :]


def _slot_shape(g, w):
    shape = list(g.shape)
    shape[SHARD_AXES[w] - 1] //= N_CHIPS
    return (N_DEV - 1,) + tuple(shape)


def _scatter_copies(g_ref, land_ref, send_sems, recv_sems, layer, w):
    x, y, c = _mesh_pos()
    n = g_ref.shape[SHARD_AXES[w] - 1] // N_CHIPS
    out = []
    for r in range(1, N_DEV):
        tx, ty, tc = _flip(x, r & 4), _flip(y, r & 2), _flip(c, r & 1)
        cp = pltpu.make_async_remote_copy(
            src_ref=_grad_shard(g_ref, w, 2 * tx + ty, n), dst_ref=land_ref.at[r - 1], send_sem=send_sems.at[r - 1],
            recv_sem=recv_sems.at[r - 1], device_id=(tx, ty, tc), device_id_type=MESH)
        out.append((cp, (c != layer) if r & 1 else (c == layer)))
    return out


def _scatter_start(g, land, layer, w, name):
    def body(g_ref, land_ref, send_sems, recv_sems, g_thru, land_thru, token):
        for cp, mine in _scatter_copies(g_ref, land_ref, send_sems, recv_sems, layer, w):
            @pl.when(mine)
            def _():
                cp.start()
        token[...] = jnp.zeros_like(token)

    return pl.pallas_call(
        body, name=name,
        out_shape=(pltpu.SemaphoreType.DMA((N_DEV - 1,)), pltpu.SemaphoreType.DMA((N_DEV - 1,)),
                   pltpu.HBM(g.shape, g.dtype), pltpu.HBM(land.shape, land.dtype),
                   jax.ShapeDtypeStruct((HALO, 128), F32)),
        in_specs=(HBM, HBM), out_specs=(SEM, SEM, HBM, HBM, pl.BlockSpec(memory_space=pltpu.VMEM)),
        input_output_aliases={0: 2, 1: 3}, compiler_params=pltpu.CompilerParams(has_side_effects=EFFECT),
    )(pltpu.with_memory_space_constraint(g, pltpu.HBM), pltpu.with_memory_space_constraint(land, pltpu.HBM))


def _scatter_wait(started, land, after, w, name):
    def body(g0_ref, g1_ref, land_ref, ss0, rs0, ss1, rs1, after_ref, g0_out, g1_out, land_out):
        c = lax.axis_index("c")
        for layer, g_ref, ss, rs in ((0, g0_ref, ss0, rs0), (1, g1_ref, ss1, rs1)):
            for cp, mine in _scatter_copies(g_ref, land_ref, ss, rs, layer, w):
                @pl.when(mine)
                def _():
                    cp.wait_send()

                @pl.when(c == layer)
                def _():
                    cp.wait_recv()

    (ss0, rs0, g0), (ss1, rs1, g1) = started
    return pl.pallas_call(
        body, name=name,
        out_shape=(pltpu.HBM(g0.shape, g0.dtype), pltpu.HBM(g1.shape, g1.dtype), pltpu.HBM(land.shape, land.dtype)),
        in_specs=(HBM, HBM, HBM, SEM, SEM, SEM, SEM, ANY), out_specs=(HBM, HBM, HBM),
        input_output_aliases={0: 0, 1: 1, 2: 2}, compiler_params=pltpu.CompilerParams(has_side_effects=EFFECT),
    )(g0, g1, land, ss0, rs0, ss1, rs1, after)


def _sum_slots(g0, g1, slots, w, pos_arr, name):
    _, rows, cols = slots.shape
    tr = min(256, rows)
    nr = rows // tr
    if SHARD_AXES[w] == 2:
        own = pl.BlockSpec((tr, cols), lambda i, pos: (i, pos[0]))
    else:
        own = pl.BlockSpec((tr, cols), lambda i, pos: (pos[0] * nr + i, 0))

    def body(pos_ref, own0_ref, own1_ref, s_ref, o_ref):
        acc = jnp.where(pos_ref[1] == 0, own0_ref[...], own1_ref[...]).astype(F32)
        for r in range(N_DEV - 1):
            acc = acc + s_ref[r].astype(F32)
        o_ref[...] = acc

    return pl.pallas_call(
        body, name=name,
        grid_spec=pltpu.PrefetchScalarGridSpec(
            num_scalar_prefetch=1, grid=(nr,),
            in_specs=[own, own, pl.BlockSpec((N_DEV - 1, tr, cols), lambda i, pos: (0, i, 0))],
            out_specs=pl.BlockSpec((tr, cols), lambda i, pos: (i, 0))),
        out_shape=jax.ShapeDtypeStruct((rows, cols), F32), compiler_params=_cparams("parallel"),
    )(pos_arr, g0, g1, slots)


def _swap_layers(halves, name):
    n = len(halves)

    def body(*refs):
        srcs, dsts = refs[:n], refs[n:2 * n]
        send_sems, recv_sems = refs[2 * n:]
        x, y, c = _mesh_pos()
        sends = [pltpu.make_async_remote_copy(src_ref=srcs[w], dst_ref=dsts[w], send_sem=send_sems.at[w],
                                              recv_sem=recv_sems.at[w], device_id=(x, y, 1 - c), device_id_type=MESH)
                 for w in range(n)]
        for cp in sends:
            cp.start()
        for cp in sends:
            cp.wait_recv()
        for cp in sends:
            cp.wait_send()

    return pl.pallas_call(
        body, name=name, in_specs=[ANY] * n, out_specs=[ANY] * n,
        out_shape=[jax.ShapeDtypeStruct(h.shape, h.dtype) for h in halves],
        scratch_shapes=[pltpu.SemaphoreType.DMA((n,)), pltpu.SemaphoreType.DMA((n,))],
    )(*halves)


def _adamw_math(w, g, m, v):
    m = ADAM_B1 * m + (1.0 - ADAM_B1) * g
    v = ADAM_B2 * v + (1.0 - ADAM_B2) * jnp.square(g)
    m_hat = m / (1.0 - ADAM_B1 ** ADAM_STEP)
    v_hat = v / (1.0 - ADAM_B2 ** ADAM_STEP)
    delta = -ADAM_LR * (m_hat / (jnp.sqrt(v_hat) + ADAM_EPS) + ADAM_WD * w)
    return delta, m, v


def _adamw(w, g_own, g_other, m, v, pos_arr, name):
    shape = w.shape
    _, rows, cols = shape
    tr = min(256, rows)

    def body(pos_ref, w_ref, own_ref, other_ref, m_ref, v_ref, g_ref, d_ref, m2_ref, v2_ref):
        g = jnp.where(pl.program_id(0) == pos_ref[1], own_ref[...], other_ref[...])
        g_ref[...] = g
        d_ref[...], m2_ref[...], v2_ref[...] = _adamw_math(w_ref[...], g, m_ref[...], v_ref[...])

    full = pl.BlockSpec((None, tr, cols), lambda l, i, pos: (l, i, 0))
    half = pl.BlockSpec((tr, cols), lambda l, i, pos: (i, 0))
    return pl.pallas_call(
        body, name=name,
        grid_spec=pltpu.PrefetchScalarGridSpec(
            num_scalar_prefetch=1, grid=(2, rows // tr),
            in_specs=[full, half, half, full, full], out_specs=[full] * 4),
        out_shape=[jax.ShapeDtypeStruct(shape, F32)] * 4, compiler_params=_cparams("parallel", "parallel"),
    )(pos_arr, w, g_own, g_other, m, v)


def _small_sync(part, w, m, v):
    rows, cols = part.shape

    def body(p_ref, w_ref, m_ref, v_ref, g_ref, d_ref, m2_ref, v2_ref, slots, send_sems, recv_sems):
        x, y, c = _mesh_pos()
        me = 4 * x + 2 * y + c
        slots[me] = p_ref[...]
        sends = []
        for r in range(1, N_DEV):
            to = (_flip(x, r & 4), _flip(y, r & 2), _flip(c, r & 1))
            sends.append(pltpu.make_async_remote_copy(
                src_ref=p_ref, dst_ref=slots.at[me], send_sem=send_sems.at[r - 1], recv_sem=recv_sems.at[r - 1],
                device_id=to, device_id_type=MESH))
        for cp in sends:
            cp.start()
        for cp in sends:
            cp.wait_recv()
        for cp in sends:
            cp.wait_send()
        g = slots[0]
        for i in range(1, N_DEV):
            g = g + slots[i]
        g_ref[...] = g
        d_ref[...], m2_ref[...], v2_ref[...] = _adamw_math(w_ref[...], g, m_ref[...], v_ref[...])

    vm = pl.BlockSpec(memory_space=pltpu.VMEM)
    return pl.pallas_call(
        body, name="small_sync", in_specs=[vm] * 4, out_specs=[vm] * 4,
        out_shape=[jax.ShapeDtypeStruct((rows, cols), F32)] * 4,
        scratch_shapes=[pltpu.VMEM((N_DEV, rows, cols), F32), pltpu.SemaphoreType.DMA((N_DEV - 1,)),
                        pltpu.SemaphoreType.DMA((N_DEV - 1,))],
    )(part, w, m, v)


PACK_W = 256


def _pack_rows(n):
    return -(-n // (HALO * PACK_W)) * HALO


def _pack_small(parts):
    out = []
    for a in parts:
        flat = a.reshape(-1)
        out.append(jnp.pad(flat, (0, _pack_rows(flat.size) * PACK_W - flat.size)).reshape(-1, PACK_W))
    return jnp.concatenate(out, axis=0)


def _unpack_small(p, shapes):
    out, row = [], 0
    for shape in shapes:
        n = 1
        for k in shape:
            n *= k
        out.append(p[row:row + _pack_rows(n)].reshape(-1)[:n].reshape(shape))
        row += _pack_rows(n)
    return out


def kernel(x, w_in, conv_w, sinks, g_mix, g_group, w_o, g_mlp, w_ff_in, w_ff_out, g_final, loss_target, m_w_in, m_conv_w, m_sinks, m_g_mix, m_g_group, m_w_o, m_g_mlp, m_w_ff_in, m_w_ff_out, m_g_final, v_w_in, v_conv_w, v_sinks, v_g_mix, v_g_group, v_w_o, v_g_mlp, v_w_ff_in, v_w_ff_out, v_g_final):
    chip = 2 * lax.axis_index("x") + lax.axis_index("y")
    conv_n = conv_w.shape[2]

    pos_arr = jnp.stack([chip, lax.axis_index("c")]).astype(jnp.int32)
    shards = (w_in, w_o, w_ff_in, w_ff_out)
    conv_tile = jnp.pad(conv_w.reshape(6, conv_n), ((0, HALO - 6), (0, 128 - conv_n)))
    placed = [_place_shard(w_in, pos_arr[:1], "place_shard_0"), None, None, None]
    sems_a, placed, conv_thru = _gather_start(
        GATHER_STARTS[0], placed, (conv_tile, lax.empty((N_CHIPS,) + conv_tile.shape, conv_tile.dtype)),
        "gather_start_0")
    for i in range(1, N_BIG):
        placed[i] = _place_shard(shards[i], pos_arr[:1], "place_shard_%d" % i)
    full = {"arrs": placed, "conv": None, "sems": list(sems_a[:2])}

    def fetch(stage, layer, after):
        k = 2 * layer + stage
        sems = full["sems"][2 * k:2 * k + 2]
        if k == 0:
            full["arrs"], land = _gather_wait(0, sems, full["arrs"], (sems_a[-2:], *conv_thru), after, "gather_wait_0")
            conv_all = lax.dynamic_update_slice(land, conv_tile[None], (chip, 0, 0))
            full["conv"] = conv_all[:, :6, :conv_n].reshape(N_CHIPS, 2, 3, conv_n).transpose(1, 2, 0, 3).reshape(
                2, 3, CONV_CH)
            sems_b, full["arrs"], rest = _gather_start(GATHER_STARTS[1], full["arrs"], None, "gather_start_1",
                                                       through=full["arrs"][0])
            full["arrs"][0] = rest[-1]
            full["sems"] += list(sems_b)
        else:
            full["arrs"], _ = _gather_wait(k, sems, full["arrs"], None, after, "gather_wait_%d" % k)
        if k == 1:
            sems_c, full["arrs"], _ = _gather_start(GATHER_STARTS[2], full["arrs"], None, "gather_start_2")
            full["sems"] += list(sems_c)
        return (*full["arrs"], full["conv"])

    lands, started = [None] * N_BIG, {}

    def emit(layer, w, g):
        if lands[w] is None:
            lands[w] = lax.empty(_slot_shape(g, w), g.dtype)
        *started[layer, w], lands[w], token = _scatter_start(g, lands[w], layer, w, "scatter_start_%d_%d" % (layer, w))
        return token[0, 0]

    loss_tile, dx, grads, dg_final = _local_step(_to_strips(x[0], placed[0], "to_strips_x"),
                                                 _to_strips(loss_target[0], placed[0], "to_strips_target"), fetch,
                                                 w_ff_in.shape[2] * N_CHIPS,
                                                 sinks, g_mix, g_group, g_mlp, g_final, emit)

    wmv = ((w_in, m_w_in, v_w_in), (w_o, m_w_o, v_w_o), (w_ff_in, m_w_ff_in, v_w_ff_in),
           (w_ff_out, m_w_ff_out, v_w_ff_out))
    big, after = [None] * N_BIG, dx
    for name, ws in (("swap_layers_rest", (1, 2, 3)), ("swap_layers_in", (0,))):
        own = []
        for w in ws:
            g0, g1, slots = _scatter_wait((started[0, w], started[1, w]), lands[w], after, w, "scatter_wait_%d" % w)
            own.append(_sum_slots(g0, g1, slots, w, pos_arr, "sum_slots_%d" % w))
        for w, mine, theirs in zip(ws, own, _swap_layers(own, name)):
            big[w] = _adamw(wmv[w][0], mine, theirs, wmv[w][1], wmv[w][2], pos_arr, "adamw_%d" % w)
        after = big[ws[-1]][1]

    def both(i):
        return jnp.stack([grads[0][i][0], grads[1][i][0]])
    dconv = jnp.stack([grads[0][0][:3], grads[1][0][:3]])
    dsinks = jnp.stack([grads[0][1][0, ::HEAD_DIM], grads[1][1][0, ::HEAD_DIM]])
    part = _pack_small([both(2), both(3), both(4), dg_final[0], dconv, dsinks, loss_tile[0, 0]])

    def spread(shard):
        return lax.dynamic_update_slice(jnp.zeros((2, 3, CONV_CH), F32), shard, (0, 0, chip * conv_n))
    zero = jnp.zeros((), F32)
    packs = [_pack_small([a, b, c_, e, spread(f), g_, zero]) for a, b, c_, e, f, g_ in (
        (g_mix, g_group, g_mlp, g_final, conv_w, sinks),
        (m_g_mix, m_g_group, m_g_mlp, m_g_final, m_conv_w, m_sinks),
        (v_g_mix, v_g_group, v_g_mlp, v_g_final, v_conv_w, v_sinks))]
    shapes = [g_mix.shape, g_group.shape, g_mlp.shape, g_final.shape, (2, 3, CONV_CH), sinks.shape, ()]
    small = [_unpack_small(p, shapes) for p in _small_sync(part, *packs)]

    def shard_of(full):
        return lax.dynamic_slice(full, (0, 0, chip * conv_n), (2, 3, conv_n))
    small = [(s[0], s[1], s[2], s[3], shard_of(s[4]), s[5], s[6]) for s in small]
    loss = small[0][6]

    def ordered(kind):
        b = [big[i][kind] for i in range(N_BIG)]
        s = small[kind]
        return [b[0], s[4], s[5], s[0], s[1], b[1], s[2], b[2], b[3], s[3]]

    return (loss, dx[None], *ordered(0), *ordered(1), *ordered(2), *ordered(3))
```

```python
import functools

import jax
import jax.numpy as jnp
from jax import lax
from jax.experimental import pallas as pl
from jax.experimental.pallas import tpu as pltpu

HEAD_DIM = 64
N_HEADS = 6
C_GROUP = 3
A_WIDTH = N_HEADS * HEAD_DIM
C_KV_WIDTH = 2 * HEAD_DIM
CONV_CH = 256
ZA_W = 3 * A_WIDTH
ZB_W = 3 * CONV_CH
ZC_W = A_WIDTH + 2 * C_KV_WIDTH
IN_WIDTH = ZA_W + ZB_W + ZC_W
MIX_WIDTH = A_WIDTH + CONV_CH + A_WIDTH
DILATIONS = (1, 4, 16)
A_MAX_DIST = 128
C_MAX_DIST = 127
TQ = 128
EPS = 1e-6
SCALE = HEAD_DIM ** -0.5
NEG = -1e30
HALO = 8

ADAM_LR = 0.001
ADAM_B1 = 0.9
ADAM_B2 = 0.999
ADAM_EPS = 1e-08
ADAM_WD = 0.01
ADAM_STEP = 10

BF = jnp.bfloat16
F32 = jnp.float32
MESH = pl.DeviceIdType.MESH
VMEM_LIMIT = 56 * 1024 * 1024


def _cparams(*sem):
    return pltpu.CompilerParams(dimension_semantics=sem, vmem_limit_bytes=VMEM_LIMIT)


def _nt(a, b):
    return lax.dot_general(a, b, (((1,), (1,)), ((), ())), preferred_element_type=F32)


def _tn(a, b):
    return lax.dot_general(a, b, (((0,), (0,)), ((), ())), preferred_element_type=F32)


def _nn(a, b):
    return jnp.dot(a, b, preferred_element_type=F32)


def _rows(tb, w):
    return pl.BlockSpec((tb, w), lambda i: (i, 0))


def _whole(shape):
    return pl.BlockSpec(shape, lambda *_: (0,) * len(shape))


def _layer(shape, l):
    return pl.BlockSpec((None,) + shape, lambda *_: (l,) + (0,) * len(shape))


def _rms_scale(v):
    return lax.rsqrt(jnp.mean(v * v, axis=-1, keepdims=True) + EPS)


def _norm_bwd(dxhat, xhat, r):
    return r * (dxhat - xhat * jnp.mean(dxhat * xhat, axis=-1, keepdims=True))


def _qkv_fwd(x, g, w_all, l, tb):
    s, d = x.shape

    def body(x_ref, g_ref, w_ref, h_ref, za_ref, zb_ref, zc_ref):
        xv = x_ref[...]
        h = ((xv * _rms_scale(xv)) * g_ref[...]).astype(BF)
        h_ref[...] = h
        z = jnp.concatenate([_nn(h, w_ref[k]) for k in range(N_CHIPS)], axis=1)
        za_ref[...] = z[:, :ZA_W]
        zb_ref[...] = z[:, ZA_W:ZA_W + ZB_W]
        zc_ref[...] = z[:, ZA_W + ZB_W:]

    return pl.pallas_call(
        body, grid=(s // tb,), name="qkv_fwd",
        in_specs=[_rows(tb, d), _whole((1, d)), _layer((N_CHIPS, d, IN_WIDTH // N_CHIPS), l)],
        out_specs=[_rows(tb, d), _rows(tb, ZA_W), _rows(tb, ZB_W), _rows(tb, ZC_W)],
        out_shape=[jax.ShapeDtypeStruct((s, d), BF), jax.ShapeDtypeStruct((s, ZA_W), F32),
                   jax.ShapeDtypeStruct((s, ZB_W), F32), jax.ShapeDtypeStruct((s, ZC_W), F32)],
        compiler_params=_cparams("parallel"),
    )(x, g, w_all)


N_STRIPS = 16


def _strips(a):
    s, w = a.shape
    return a.reshape(4, 4, s // N_STRIPS, w)


def _p_grid(s, dil):
    na = s // N_STRIPS
    return {16: (4, 4, na // TQ), 4: (4, na // 32), 1: (na // 8,)}[dil]


def _p_spec(dil, cw, col, prev=False):
    def blk(j):
        return jnp.maximum(j - 1, 0) if prev else j
    if dil == 16:
        return pl.BlockSpec((None, None, TQ, cw), lambda f, e, j: (f, e, blk(j), col))
    if dil == 4:
        return pl.BlockSpec((None, 4, 32, cw), lambda f, j: (f, 0, blk(j), col))
    return pl.BlockSpec((4, 4, 8, cw), lambda j: (0, 0, blk(j), col))


def _block_pos(i, dil):
    if dil == 16:
        return i
    if dil == 4:
        return 4 * (i % 32) + i // 32
    return 16 * (i % 8) + 4 * ((i // 8) % 4) + i // 32


def _band_mask(b, dil, max_dist):
    qi = _block_pos(lax.broadcasted_iota(jnp.int32, (TQ, 2 * TQ), 0), dil)
    col = lax.broadcasted_iota(jnp.int32, (TQ, 2 * TQ), 1)
    cur = col >= TQ
    dist = qi - _block_pos(col % TQ, dil) + jnp.where(cur, 0, TQ)
    return (dist >= 0) & (dist <= max_dist) & (cur | (b > 0))


def _hs(h):
    return slice(h * HEAD_DIM, (h + 1) * HEAD_DIM)


def _ld(ref, cols):
    v = ref[..., cols]
    return v.reshape(TQ, v.shape[-1])


def _st(ref, cols, val):
    ref[..., cols] = val.reshape(ref.shape[:-1] + (val.shape[-1],))


def _attn_fwd(z, dil, kw, kcol, vcol, n_rep, max_dist, name):
    s, zw = z.shape
    grid = _p_grid(s, dil)

    def body(q_ref, kp_ref, kc_ref, vp_ref, vc_ref, o_ref, lse_ref):
        mask = _band_mask(pl.program_id(len(grid) - 1), dil, max_dist)
        scs, v2s = [], []
        for kh in range(N_HEADS // n_rep):
            k2 = jnp.concatenate([_ld(kp_ref, _hs(kh)), _ld(kc_ref, _hs(kh))], axis=0).astype(BF)
            v2s.append(jnp.concatenate([_ld(vp_ref, _hs(kh)), _ld(vc_ref, _hs(kh))], axis=0).astype(BF))
            for h in range(kh * n_rep, (kh + 1) * n_rep):
                q = (_ld(q_ref, _hs(h)) * SCALE).astype(BF)
                scs.append(jnp.where(mask, _nt(q, k2), NEG))
        for h, sc in enumerate(scs):
            m = jnp.max(sc, axis=1, keepdims=True)
            p = jnp.exp(sc - m)
            l = jnp.sum(p, axis=1, keepdims=True)
            _st(o_ref, _hs(h), _nn(p.astype(BF), v2s[h // n_rep]) / l)
            _st(lse_ref, _hs(h), jnp.broadcast_to(m + jnp.log(l), (TQ, HEAD_DIM)))

    res = pl.pallas_call(
        body, grid=grid, name=name,
        in_specs=[_p_spec(dil, A_WIDTH, 0), _p_spec(dil, kw, kcol, True), _p_spec(dil, kw, kcol),
                  _p_spec(dil, kw, vcol, True), _p_spec(dil, kw, vcol)],
        out_specs=[_p_spec(dil, A_WIDTH, 0)] * 2,
        out_shape=[jax.ShapeDtypeStruct((4, 4, s // N_STRIPS, A_WIDTH), F32)] * 2,
        compiler_params=_cparams(*(("parallel",) * len(grid))),
    )(*[_strips(z)] * 5)
    return [a.reshape(s, A_WIDTH) for a in res]


def _attn_merge(parts_a, part_c, sink_row, tb):
    s = part_c[0].shape[0]
    n_a = len(parts_a)

    def body(*refs):
        ins, sink_ref = refs[:2 * n_a + 2], refs[2 * n_a + 2]
        ya_ref, lsea_ref, yc_ref, lsec_ref = refs[2 * n_a + 3:]
        lses = [ins[2 * p + 1][...] for p in range(n_a)]
        m = functools.reduce(jnp.maximum, lses)
        ws = [jnp.exp(v - m) for v in lses]
        l = functools.reduce(jnp.add, ws)
        ya_ref[...] = functools.reduce(jnp.add, [w * ins[2 * p][...] for p, w in enumerate(ws)]) / l
        lsea_ref[...] = m + jnp.log(l)
        o_c, lse_c = [r[...] for r in ins[2 * n_a:]]
        sk = sink_ref[...]
        m2 = jnp.maximum(lse_c, sk)
        w = jnp.exp(lse_c - m2)
        l2 = w + jnp.exp(sk - m2)
        yc_ref[...] = o_c * (w / l2)
        lsec_ref[...] = m2 + jnp.log(l2)

    return pl.pallas_call(
        body, grid=(s // tb,), name="attn_merge",
        in_specs=[_rows(tb, A_WIDTH)] * (2 * n_a + 2) + [_whole((1, A_WIDTH))],
        out_specs=[_rows(tb, A_WIDTH)] * 4, out_shape=[jax.ShapeDtypeStruct((s, A_WIDTH), F32)] * 4,
        compiler_params=_cparams("parallel"),
    )(*[a for part in parts_a + [part_c] for a in part], sink_row)


def _shift_down(v, n, halo):
    rows = v.shape[0]
    out = pltpu.roll(v, n, 0)
    row = lax.broadcasted_iota(jnp.int32, v.shape, 0)
    for t in range(n):
        out = jnp.where(row == t, halo[HALO - n + t:HALO - n + t + 1, :], out)
    return out


def _shift_up(v, n, halo):
    rows = v.shape[0]
    out = pltpu.roll(v, rows - n, 0)
    row = lax.broadcasted_iota(jnp.int32, v.shape, 0)
    for t in range(n):
        out = jnp.where(row == rows - n + t, halo[t:t + 1, :], out)
    return out


def _strip(v, b):
    return v[b % 4, b // 4]


def _conv_strips(zb, prev, cw):
    gb = [_strip(zb, b)[:, :CONV_CH] for b in range(N_STRIPS)]
    gc = [_strip(zb, b)[:, CONV_CH:2 * CONV_CH] for b in range(N_STRIPS)]
    xb = [_strip(zb, b)[:, 2 * CONV_CH:] for b in range(N_STRIPS)]
    u = [g * v for g, v in zip(gc, xb)]
    uh = prev[:, :, CONV_CH:2 * CONV_CH] * prev[:, :, 2 * CONV_CH:]
    wrapped = {14: _shift_down(u[14], 1, uh[2]), 15: _shift_down(u[15], 1, uh[3])}
    u1 = [u[b - 1] if b >= 1 else wrapped[15] for b in range(N_STRIPS)]
    u2 = [u[b - 2] if b >= 2 else wrapped[14 + b] for b in range(N_STRIPS)]
    c = [cw[0:1, :] * u2[b] + cw[1:2, :] * u1[b] + cw[2:3, :] * u[b] for b in range(N_STRIPS)]
    return gb, gc, xb, u, u1, u2, c


def _strip_rows(ta, w):
    return pl.BlockSpec((4, 4, ta, w), lambda i: (0, 0, i, 0))


def _prev_rows(ta, w):
    return pl.BlockSpec((4, None, HALO, w), lambda i: (0, 3, jnp.maximum(i * (ta // HALO) - 1, 0), 0))


def _next_rows(ta, w, nblk):
    return pl.BlockSpec((4, None, HALO, w),
                        lambda i: (0, 0, jnp.minimum((i + 1) * (ta // HALO), nblk * (ta // HALO) - 1), 0))


def _mix_fwd(x, ya, yc, zb, cw, gg, wo_all, l, tb):
    s, d = x.shape
    ta = tb // N_STRIPS

    def body(x_ref, ya_ref, yc_ref, zb_ref, zbp_ref, cw_ref, gg_ref, wo_ref, x1_ref, yb_ref):
        i = pl.program_id(0)
        prev = jnp.where(i > 0, zbp_ref[...], 0.0)
        gb, _, _, _, _, _, c = _conv_strips(zb_ref[...], prev, cw_ref[...])
        for b in range(N_STRIPS):
            yb_ref[b % 4, b // 4] = gb[b] * c[b]
        yb = yb_ref[...].reshape(tb, CONV_CH)
        ya, yc = ya_ref[...].reshape(tb, A_WIDTH), yc_ref[...].reshape(tb, A_WIDTH)
        n = jnp.concatenate([ya * _rms_scale(ya), yb * _rms_scale(yb), yc * _rms_scale(yc)], axis=1)
        n = (n * gg_ref[...]).astype(BF)
        x1 = x_ref[...].reshape(tb, d) + _nn(n, wo_ref[...].reshape(MIX_WIDTH, d))
        x1_ref[...] = x1.reshape(4, 4, ta, d)

    res = pl.pallas_call(
        body, grid=(s // tb,), name="mix_fwd",
        in_specs=[_strip_rows(ta, d), _strip_rows(ta, A_WIDTH), _strip_rows(ta, A_WIDTH), _strip_rows(ta, ZB_W),
                  _prev_rows(ta, ZB_W), _whole((HALO, CONV_CH)), _whole((1, MIX_WIDTH)),
                  _layer((N_CHIPS, MIX_WIDTH // N_CHIPS, d), l)],
        out_specs=[_strip_rows(ta, d), _strip_rows(ta, CONV_CH)],
        out_shape=[jax.ShapeDtypeStruct((4, 4, s // N_STRIPS, d), F32),
                   jax.ShapeDtypeStruct((4, 4, s // N_STRIPS, CONV_CH), F32)],
        compiler_params=_cparams("parallel"),
    )(_strips(x), _strips(ya), _strips(yc), _strips(zb), _strips(zb), cw, gg, wo_all)
    return res[0].reshape(s, d), res[1].reshape(s, CONV_CH)


def _mlp_fwd(x1, g, w1_all, w2_all, l, tb, tf):
    s, d = x1.shape
    ff = w1_all.shape[1] * w1_all.shape[3]
    nj = ff // tf

    def body(x_ref, g_ref, w1_ref, w2_ref, x2_ref, h2_ref, ap_ref, acc):
        j = pl.program_id(1)

        @pl.when(j == 0)
        def _():
            xv = x_ref[...]
            h2_ref[...] = ((xv * _rms_scale(xv)) * g_ref[...]).astype(BF)
            acc[...] = jnp.zeros_like(acc)

        ap = _nn(h2_ref[...], w1_ref[...])
        ap_ref[...] = ap.astype(BF)
        a = jnp.square(jnp.maximum(ap, 0.0)).astype(BF)
        acc[...] += _nn(a, w2_ref[...])

        @pl.when(j == nj - 1)
        def _():
            x2_ref[...] = x_ref[...] + acc[...]

    return pl.pallas_call(
        body, grid=(s // tb, nj), name="mlp_fwd",
        in_specs=[pl.BlockSpec((tb, d), lambda i, j: (i, 0)), _whole((1, d)),
                  pl.BlockSpec((None, None, d, tf), lambda i, j: (l, j, 0, 0)),
                  pl.BlockSpec((None, None, tf, d), lambda i, j: (l, j, 0, 0))],
        out_specs=[pl.BlockSpec((tb, d), lambda i, j: (i, 0)), pl.BlockSpec((tb, d), lambda i, j: (i, 0)),
                   pl.BlockSpec((tb, tf), lambda i, j: (i, j))],
        out_shape=[jax.ShapeDtypeStruct((s, d), F32), jax.ShapeDtypeStruct((s, d), BF),
                   jax.ShapeDtypeStruct((s, ff), BF)],
        scratch_shapes=[pltpu.VMEM((tb, d), F32)],
        compiler_params=_cparams("parallel", "arbitrary"),
    )(x1, g, w1_all, w2_all)


def _loss_head(x, g, tgt, tb):
    s, d = x.shape

    def body(x_ref, g_ref, t_ref, dx_ref, loss_ref, dg_ref):
        i = pl.program_id(0)

        @pl.when(i == 0)
        def _():
            loss_ref[...] = jnp.zeros_like(loss_ref)
            dg_ref[...] = jnp.zeros_like(dg_ref)

        xv = x_ref[...]
        r = _rms_scale(xv)
        xhat = xv * r
        err = xhat * g_ref[...] - t_ref[...]
        part = jnp.sum(jnp.mean(jnp.square(err), axis=-1, keepdims=True), axis=0, keepdims=True)
        loss_ref[...] += 0.5 * part
        dy = err * (1.0 / d)
        dg_ref[...] += jnp.sum(dy * xhat, axis=0, keepdims=True)
        dx_ref[...] = _norm_bwd(dy * g_ref[...], xhat, r)

    return pl.pallas_call(
        body, grid=(s // tb,), name="loss_head",
        in_specs=[_rows(tb, d), _whole((1, d)), _rows(tb, d)],
        out_specs=[_rows(tb, d), _whole((HALO, 128)), _whole((HALO, d))],
        out_shape=[jax.ShapeDtypeStruct((s, d), F32), jax.ShapeDtypeStruct((HALO, 128), F32),
                   jax.ShapeDtypeStruct((HALO, d), F32)],
        compiler_params=_cparams("arbitrary"),
    )(x, g, tgt)


def _mlp_bwd(dx2, x1, ap, g, w1_all, w2_all, l, tb, tf):
    s, d = x1.shape
    ff = ap.shape[1]
    nj = ff // tf

    def body(dx2_ref, x1_ref, ap_ref, g_ref, w1_ref, w2_ref, dx1_ref, dap_ref, dg_ref, acc):
        i, j = pl.program_id(0), pl.program_id(1)

        @pl.when((i == 0) & (j == 0))
        def _():
            dg_ref[...] = jnp.zeros_like(dg_ref)

        @pl.when(j == 0)
        def _():
            acc[...] = jnp.zeros_like(acc)

        da = _nt(dx2_ref[...].astype(BF), w2_ref[...])
        dap = (da * (2.0 * jnp.maximum(ap_ref[...].astype(F32), 0.0))).astype(BF)
        dap_ref[...] = dap
        acc[...] += _nt(dap, w1_ref[...])

        @pl.when(j == nj - 1)
        def _():
            xv = x1_ref[...]
            r = _rms_scale(xv)
            xhat = xv * r
            dh = acc[...]
            dg_ref[...] += jnp.sum(dh * xhat, axis=0, keepdims=True)
            dx1_ref[...] = dx2_ref[...] + _norm_bwd(dh * g_ref[...], xhat, r)

    return pl.pallas_call(
        body, grid=(s // tb, nj), name="mlp_bwd",
        in_specs=[pl.BlockSpec((tb, d), lambda i, j: (i, 0)), pl.BlockSpec((tb, d), lambda i, j: (i, 0)),
                  pl.BlockSpec((tb, tf), lambda i, j: (i, j)),
                  _whole((1, d)), pl.BlockSpec((None, None, d, tf), lambda i, j: (l, j, 0, 0)),
                  pl.BlockSpec((None, None, tf, d), lambda i, j: (l, j, 0, 0))],
        out_specs=[pl.BlockSpec((tb, d), lambda i, j: (i, 0)), pl.BlockSpec((tb, tf), lambda i, j: (i, j)),
                   _whole((HALO, d))],
        out_shape=[jax.ShapeDtypeStruct((s, d), F32), jax.ShapeDtypeStruct((s, ff), BF),
                   jax.ShapeDtypeStruct((HALO, d), F32)],
        scratch_shapes=[pltpu.VMEM((tb, d), F32)],
        compiler_params=_cparams("arbitrary", "arbitrary"),
    )(dx2, x1, ap, g, w1_all, w2_all)


def _wgrad(a, b, tm, tn, ts, name, relu2=False):
    s, m = a.shape
    n = b.shape[1]
    ns = s // ts

    def body(a_ref, b_ref, o_ref, acc):
        k = pl.program_id(2)

        @pl.when(k == 0)
        def _():
            acc[...] = jnp.zeros_like(acc)

        av = a_ref[...]
        if relu2:
            av = jnp.square(jnp.maximum(av.astype(F32), 0.0)).astype(BF)
        acc[...] += _tn(av, b_ref[...].astype(BF))

        @pl.when(k == ns - 1)
        def _():
            o_ref[...] = acc[...].astype(BF)

    return pl.pallas_call(
        body, grid=(m // tm, n // tn, ns), name=name,
        in_specs=[pl.BlockSpec((ts, tm), lambda i, j, k: (k, i)), pl.BlockSpec((ts, tn), lambda i, j, k: (k, j))],
        out_specs=pl.BlockSpec((tm, tn), lambda i, j, k: (i, j)),
        out_shape=jax.ShapeDtypeStruct((m, n), BF),
        scratch_shapes=[pltpu.VMEM((tm, tn), F32)],
        compiler_params=_cparams("parallel", "parallel", "arbitrary"),
    )(a, b)


def _mix_bwd(dx1, ya, yb, yc, lse_c, sink_row, gg, wo_all, l, tb):
    s, d = dx1.shape

    def body(dx_ref, ya_ref, yb_ref, yc_ref, lse_ref, sink_ref, gg_ref, wo_ref,
             n_ref, dya_ref, dyc_ref, da_ref, dc_ref, dyb_ref, dg_ref, dsink_ref):
        i = pl.program_id(0)

        @pl.when(i == 0)
        def _():
            dg_ref[...] = jnp.zeros_like(dg_ref)
            dsink_ref[...] = jnp.zeros_like(dsink_ref)

        dn = _nt(dx_ref[...].astype(BF), wo_ref[...].reshape(MIX_WIDTH, d))
        ys = [ya_ref[...], yb_ref[...], yc_ref[...]]
        rs = [_rms_scale(v) for v in ys]
        nhat = jnp.concatenate([v * r for v, r in zip(ys, rs)], axis=1)
        gg = gg_ref[...]
        n_ref[...] = (nhat * gg).astype(BF)
        dg_ref[...] += jnp.sum(dn * nhat, axis=0, keepdims=True)
        dnh = dn * gg
        bounds = [(0, A_WIDTH), (A_WIDTH, A_WIDTH + CONV_CH), (A_WIDTH + CONV_CH, MIX_WIDTH)]
        dys = [_norm_bwd(dnh[:, lo:hi], nhat[:, lo:hi], r) for (lo, hi), r in zip(bounds, rs)]
        dyb_ref[...] = dys[1]
        for dy, y, dy_ref, dd_ref in ((dys[0], ys[0], dya_ref, da_ref), (dys[2], ys[2], dyc_ref, dc_ref)):
            dy_ref[...] = dy
            t = dy * y
            for h in range(N_HEADS):
                dd_ref[:, _hs(h)] = jnp.broadcast_to(jnp.sum(t[:, _hs(h)], axis=1, keepdims=True), (tb, HEAD_DIM))
        dsink_ref[...] -= jnp.sum(jnp.exp(sink_ref[...] - lse_ref[...]) * dc_ref[...], axis=0, keepdims=True)

    return pl.pallas_call(
        body, grid=(s // tb,), name="mix_bwd",
        in_specs=[_rows(tb, d), _rows(tb, A_WIDTH), _rows(tb, CONV_CH), _rows(tb, A_WIDTH), _rows(tb, A_WIDTH),
                  _whole((1, A_WIDTH)), _whole((1, MIX_WIDTH)), _layer((N_CHIPS, MIX_WIDTH // N_CHIPS, d), l)],
        out_specs=[_rows(tb, MIX_WIDTH), _rows(tb, A_WIDTH), _rows(tb, A_WIDTH), _rows(tb, A_WIDTH),
                   _rows(tb, A_WIDTH), _rows(tb, CONV_CH), _whole((HALO, MIX_WIDTH)), _whole((HALO, A_WIDTH))],
        out_shape=[jax.ShapeDtypeStruct((s, MIX_WIDTH), BF), jax.ShapeDtypeStruct((s, A_WIDTH), F32),
                   jax.ShapeDtypeStruct((s, A_WIDTH), F32), jax.ShapeDtypeStruct((s, A_WIDTH), F32),
                   jax.ShapeDtypeStruct((s, A_WIDTH), F32), jax.ShapeDtypeStruct((s, CONV_CH), F32),
                   jax.ShapeDtypeStruct((HALO, MIX_WIDTH), F32), jax.ShapeDtypeStruct((HALO, A_WIDTH), F32)],
        compiler_params=_cparams("arbitrary"),
    )(dx1, ya, yb, yc, lse_c, sink_row, gg, wo_all)


def _attn_bwd(z, dy, lse, dd, dil, kw, kcol, vcol, n_rep, max_dist, name):
    s, zw = z.shape
    grid = _p_grid(s, dil)
    n_kv = N_HEADS // n_rep
    dt = F32 if dil == 1 else BF

    def body(q_ref, kp_ref, kc_ref, vp_ref, vc_ref, dy_ref, lse_ref, dd_ref, dq_ref, dkp_ref, dkc_ref, dvp_ref, dvc_ref):
        mask = _band_mask(pl.program_id(len(grid) - 1), dil, max_dist)
        k2s, qs, dys, scs, dps = [], [], [], [], []
        for kh in range(n_kv):
            k2s.append(jnp.concatenate([_ld(kp_ref, _hs(kh)), _ld(kc_ref, _hs(kh))], axis=0).astype(BF))
            v2 = jnp.concatenate([_ld(vp_ref, _hs(kh)), _ld(vc_ref, _hs(kh))], axis=0).astype(BF)
            for h in range(kh * n_rep, (kh + 1) * n_rep):
                qs.append((_ld(q_ref, _hs(h)) * SCALE).astype(BF))
                dys.append(_ld(dy_ref, _hs(h)).astype(BF))
                scs.append(jnp.where(mask, _nt(qs[h], k2s[kh]), NEG))
                dps.append(_nt(dys[h], v2))
        for kh in range(n_kv):
            k2 = k2s[kh]
            dk2 = jnp.zeros((2 * TQ, HEAD_DIM), F32)
            dv2 = jnp.zeros((2 * TQ, HEAD_DIM), F32)
            for h in range(kh * n_rep, (kh + 1) * n_rep):
                lse_h = _ld(lse_ref, slice(h * HEAD_DIM, h * HEAD_DIM + 1))
                dd_h = _ld(dd_ref, slice(h * HEAD_DIM, h * HEAD_DIM + 1))
                p = jnp.exp(scs[h] - lse_h)
                ds = (p * (dps[h] - dd_h)).astype(BF)
                _st(dq_ref, _hs(h), (_nn(ds, k2) * SCALE).astype(dq_ref.dtype))
                dk2 = dk2 + _tn(ds, qs[h])
                dv2 = dv2 + _tn(p.astype(BF), dys[h])
            _st(dkp_ref, _hs(kh), dk2[:TQ].astype(dt))
            _st(dkc_ref, _hs(kh), dk2[TQ:].astype(dt))
            _st(dvp_ref, _hs(kh), dv2[:TQ].astype(dt))
            _st(dvc_ref, _hs(kh), dv2[TQ:].astype(dt))

    args = [_strips(z)] * 5 + [_strips(a) for a in (dy, lse, dd)]
    in_specs = [_p_spec(dil, A_WIDTH, 0), _p_spec(dil, kw, kcol, True), _p_spec(dil, kw, kcol),
                _p_spec(dil, kw, vcol, True), _p_spec(dil, kw, vcol)] + [_p_spec(dil, A_WIDTH, 0)] * 3
    out_specs = [_p_spec(dil, A_WIDTH, 0)] + [_p_spec(dil, kw, 0)] * 4
    na = s // N_STRIPS
    out_shape = [jax.ShapeDtypeStruct((4, 4, na, A_WIDTH), dt)] + [jax.ShapeDtypeStruct((4, 4, na, kw), dt)] * 4
    res = pl.pallas_call(
        body, grid=grid, name=name, in_specs=in_specs, out_specs=out_specs, out_shape=out_shape,
        compiler_params=_cparams(*(("parallel",) * len(grid))),
    )(*args)
    return [res[0].reshape(s, A_WIDTH)] + [a.reshape(s, kw) for a in res[1:]]


DZ_TA = 16


def _dz_assemble(parts_a, parts_c, dyb, zb, cw):
    s = zb.shape[0]
    na = s // N_STRIPS
    nb = na // DZ_TA

    def ahead(w, k):
        return pl.BlockSpec((4, 4, DZ_TA, w), lambda i: (0, 0, jnp.minimum(i + k, nb - 1), 0))

    args, in_specs = [], []
    for dil, (dq, dkp, dkc, dvp, dvc) in zip(DILATIONS + (1,), parts_a + [parts_c]):
        w = dkp.shape[1]
        here = _strip_rows(DZ_TA, w)
        if dil == 1:
            args += [dq, dkp, dkp, dkc, dvp, dvp, dvc]
            in_specs += [_strip_rows(DZ_TA, A_WIDTH), here, ahead(w, 1), here, here, ahead(w, 1), here]
        else:
            k = 8 * dil // DZ_TA
            args += [dq, dkp, dkc, dvp, dvc]
            in_specs += [_strip_rows(DZ_TA, A_WIDTH), ahead(w, k), here, ahead(w, k), here]
    n_att = len(args)
    args = [_strips(a) for a in args] + [_strips(dyb), _strips(dyb), _strips(zb), _strips(zb), _strips(zb), cw]
    in_specs += [_strip_rows(DZ_TA, CONV_CH), _next_rows(DZ_TA, CONV_CH, nb), _strip_rows(DZ_TA, ZB_W),
                 _prev_rows(DZ_TA, ZB_W), _next_rows(DZ_TA, ZB_W, nb), _whole((HALO, CONV_CH))]

    def body(*refs):
        att = list(refs[:n_att])
        dyb_ref, dybn_ref, zb_ref, zbp_ref, zbn_ref, cw_ref, dz_ref, dcw_ref = refs[n_att:]
        i = pl.program_id(0)

        @pl.when(i == 0)
        def _():
            dcw_ref[...] = jnp.zeros_like(dcw_ref)

        def shifted(dil):
            if dil == 1:
                dq_r, kp0, kp1, dkc_r, vp0, vp1, dvc_r = [att.pop(0) for _ in range(7)]
                live = i + 1 < nb
                half = DZ_TA // 2
                dkp = jnp.concatenate([kp0[:, :, half:, :], jnp.where(live, kp1[:, :, :half, :], 0.0)], axis=2)
                dvp = jnp.concatenate([vp0[:, :, half:, :], jnp.where(live, vp1[:, :, :half, :], 0.0)], axis=2)
            else:
                dq_r, dkp_r, dkc_r, dvp_r, dvc_r = [att.pop(0) for _ in range(5)]
                live = i + 8 * dil // DZ_TA < nb
                dkp = jnp.where(live, dkp_r[...].astype(F32), 0.0)
                dvp = jnp.where(live, dvp_r[...].astype(F32), 0.0)
            return dq_r[...].astype(F32), dkc_r[...].astype(F32) + dkp, dvc_r[...].astype(F32) + dvp

        dq, dk, dv = shifted(DILATIONS[0])
        for dil in DILATIONS[1:]:
            dq2, dk2, dv2 = shifted(dil)
            dq, dk, dv = dq + dq2, dk + dk2, dv + dv2
        dz_ref[:, :, :, 0:A_WIDTH] = dq.astype(BF)
        dz_ref[:, :, :, A_WIDTH:2 * A_WIDTH] = dk.astype(BF)
        dz_ref[:, :, :, 2 * A_WIDTH:ZA_W] = dv.astype(BF)
        dq, dk, dv = shifted(1)
        c0 = ZA_W + ZB_W
        dz_ref[:, :, :, c0:c0 + A_WIDTH] = dq.astype(BF)
        dz_ref[:, :, :, c0 + A_WIDTH:c0 + A_WIDTH + C_KV_WIDTH] = dk.astype(BF)
        dz_ref[:, :, :, c0 + A_WIDTH + C_KV_WIDTH:IN_WIDTH] = dv.astype(BF)

        cw = cw_ref[...]
        prev = jnp.where(i > 0, zbp_ref[...], 0.0)
        gb, gc, xb, u, u1, u2, c = _conv_strips(zb_ref[...], prev, cw)
        dyb = dyb_ref[...]
        dc = [_strip(dyb, b) * gb[b] for b in range(N_STRIPS)]
        dcn = jnp.where(i + 1 < nb, dybn_ref[...] * zbn_ref[:, :, :CONV_CH], 0.0)
        wrapped = [_shift_up(dc[0], 1, dcn[0]), _shift_up(dc[1], 1, dcn[1])]
        upd = [jnp.zeros((1, CONV_CH), F32)] * 3
        for b in range(N_STRIPS):
            dc1 = dc[b + 1] if b + 1 < N_STRIPS else wrapped[0]
            dc2 = dc[b + 2] if b + 2 < N_STRIPS else wrapped[b + 2 - N_STRIPS]
            du = cw[2:3, :] * dc[b] + cw[1:2, :] * dc1 + cw[0:1, :] * dc2
            f, e = b % 4, b // 4
            dz_ref[f, e, :, ZA_W:ZA_W + CONV_CH] = (_strip(dyb, b) * c[b]).astype(BF)
            dz_ref[f, e, :, ZA_W + CONV_CH:ZA_W + 2 * CONV_CH] = (du * xb[b]).astype(BF)
            dz_ref[f, e, :, ZA_W + 2 * CONV_CH:c0] = (du * gc[b]).astype(BF)
            for t, uu in enumerate((u2[b], u1[b], u[b])):
                upd[t] = upd[t] + jnp.sum(dc[b] * uu, axis=0, keepdims=True)
        row = lax.broadcasted_iota(jnp.int32, (HALO, CONV_CH), 0)
        tile = jnp.zeros((HALO, CONV_CH), F32)
        for t in range(3):
            tile = jnp.where(row == t, upd[t], tile)
        dcw_ref[...] += tile

    dz, dcw = pl.pallas_call(
        body, grid=(nb,), name="dz_assemble", in_specs=in_specs,
        out_specs=[_strip_rows(DZ_TA, IN_WIDTH), _whole((HALO, CONV_CH))],
        out_shape=[jax.ShapeDtypeStruct((4, 4, na, IN_WIDTH), BF), jax.ShapeDtypeStruct((HALO, CONV_CH), F32)],
        compiler_params=_cparams("arbitrary"),
    )(*args)
    return dz.reshape(s, IN_WIDTH), dcw


def _qkv_bwd(dz, dx1, x, g, w_all, l, tb, tokens_out):
    s, d = x.shape
    na, ta = s // N_STRIPS, tb // N_STRIPS

    def body(dz_ref, dx1_ref, x_ref, g_ref, w_ref, dx_ref, dg_ref):
        i = pl.program_id(0)

        @pl.when(i == 0)
        def _():
            dg_ref[...] = jnp.zeros_like(dg_ref)

        n = IN_WIDTH // N_CHIPS
        dz = dz_ref[...].reshape(tb, IN_WIDTH)
        dh = _nt(dz[:, 0:n], w_ref[0])
        for k in range(1, N_CHIPS):
            dh = dh + _nt(dz[:, k * n:(k + 1) * n], w_ref[k])
        xv = x_ref[...].reshape(tb, d)
        r = _rms_scale(xv)
        xhat = xv * r
        dg_ref[...] += jnp.sum(dh * xhat, axis=0, keepdims=True)
        dx = (dx1_ref[...].reshape(tb, d) + _norm_bwd(dh * g_ref[...], xhat, r)).reshape(4, 4, ta, d)
        if tokens_out:
            for b in range(N_STRIPS):
                dx_ref[:, b, :] = _strip(dx, b)
        else:
            dx_ref[...] = dx

    if tokens_out:
        dx_spec, dx_shape = pl.BlockSpec((ta, N_STRIPS, d), lambda i: (i, 0, 0)), (na, N_STRIPS, d)
    else:
        dx_spec, dx_shape = _strip_rows(ta, d), (4, 4, na, d)
    dx, dg = pl.pallas_call(
        body, grid=(s // tb,), name="qkv_bwd",
        in_specs=[_strip_rows(ta, IN_WIDTH), _strip_rows(ta, d), _strip_rows(ta, d), _whole((1, d)),
                  _layer((N_CHIPS, d, IN_WIDTH // N_CHIPS), l)],
        out_specs=[dx_spec, _whole((HALO, d))],
        out_shape=[jax.ShapeDtypeStruct(dx_shape, F32), jax.ShapeDtypeStruct((HALO, d), F32)],
        compiler_params=_cparams("arbitrary"),
    )(_strips(dz), _strips(dx1), _strips(x), g, w_all)
    return dx.reshape(s, d), dg


def _tile_rows(rows):
    return jnp.pad(rows, ((0, HALO - rows.shape[0]), (0, 0)))


def _to_strips(a, after, name):
    s, d = a.shape
    na = s // N_STRIPS
    ta = min(32, na)

    def body(a_ref, *rest):
        for b in range(N_STRIPS):
            rest[-1][b % 4, b // 4] = a_ref[:, b, :]

    return pl.pallas_call(
        body, grid=(na // ta,), name=name,
        in_specs=[pl.BlockSpec((ta, N_STRIPS, d), lambda i: (i, 0, 0))] + [ANY] * len(after),
        out_specs=_strip_rows(ta, d),
        out_shape=jax.ShapeDtypeStruct((4, 4, na, d), a.dtype), compiler_params=_cparams("parallel"),
    )(a.reshape(na, N_STRIPS, d), *after).reshape(s, d)


def _local_step(x, tgt, fetch, ff, sinks, g_mix, g_group, g_mlp, g_final, emit):
    s, d = x.shape
    depth = g_mix.shape[0]
    tb = min(512, s)
    tf = ff // N_CHIPS
    ts = min(1024, s)
    saved = []
    for l in range(depth):
        w_in, _, _, _, conv_w = fetch(0, l, x)
        cw = _tile_rows(conv_w[l])
        sk = jnp.repeat(sinks[l].reshape(N_HEADS), HEAD_DIM)[None]
        h, za, zb, zc = _qkv_fwd(x, g_mix[l][None], w_in, l, tb)
        parts_a = [_attn_fwd(za, dil, A_WIDTH, 1, 2, 1, A_MAX_DIST, "attn_a_fwd_%d" % dil) for dil in DILATIONS]
        part_c = _attn_fwd(zc, 1, C_KV_WIDTH, 3, 4, C_GROUP, C_MAX_DIST, "attn_c_fwd")
        ya, lse_a, yc, lse_c = _attn_merge(parts_a, part_c, sk, tb)
        w_in, w_o, w1, w2, _ = fetch(1, l, yc)
        x1, yb = _mix_fwd(x, ya, yc, zb, cw, g_group[l][None], w_o, l, tb)
        x2, h2, ap = _mlp_fwd(x1, g_mlp[l][None], w1, w2, l, ts, tf)
        saved.append((x, h, za, zb, zc, ya, lse_a, yc, lse_c, yb, x1, h2, ap, cw, sk))
        x = x2
    dx, loss_tile, dg_final = _loss_head(x, g_final[None], tgt, tb)
    grads = [None] * depth
    tok = jnp.zeros((), F32)
    for l in reversed(range(depth)):
        x0, h, za, zb, zc, ya, lse_a, yc, lse_c, yb, x1, h2, ap, cw, sk = saved[l]
        dx1, dap, dg_mlp = _mlp_bwd(dx, x1, ap, g_mlp[l][None] + tok, w1, w2, l, ts, tf)
        tok = emit(l, 3, _wgrad(ap, dx, min(1024, ff), d, ts, "wgrad_ff_out", relu2=True))
        tok = tok + emit(l, 2, _wgrad(h2, dap, d, min(1024, ff), 2 * ts, "wgrad_ff_in"))
        n, dya, dyc, dd_a, dd_c, dyb, dg_group, dsink = _mix_bwd(dx1, ya, yb, yc, lse_c, sk, g_group[l][None] + tok,
                                                                 w_o, l, tb)
        tok = emit(l, 1, _wgrad(n, dx1, MIX_WIDTH, d, ts, "wgrad_o"))
        cw = cw + tok
        parts_a = [_attn_bwd(za, dya, lse_a, dd_a, dil, A_WIDTH, 1, 2, 1, A_MAX_DIST, "attn_a_bwd_%d" % dil)
                   for dil in DILATIONS]
        parts_c = _attn_bwd(zc, dyc, lse_c, dd_c, 1, C_KV_WIDTH, 3, 4, C_GROUP, C_MAX_DIST, "attn_c_bwd")
        dz, dcw = _dz_assemble(parts_a, parts_c, dyb, zb, cw)
        tok = emit(l, 0, _wgrad(h, dz, d, IN_WIDTH // 4, 2 * ts, "wgrad_in"))
        dx, dg_mix = _qkv_bwd(dz, dx1, x0, g_mix[l][None] + tok, w_in, l, tb, l == 0)
        grads[l] = (dcw, dsink, dg_mix, dg_group, dg_mlp)
    return loss_tile, dx, grads, dg_final


ANY = pl.BlockSpec(memory_space=pl.ANY)
SHARD_AXES = (2, 1, 2, 1)
N_BIG = len(SHARD_AXES)
N_CHIPS = 4
N_DEV = 8


def _mesh_pos():
    return lax.axis_index("x"), lax.axis_index("y"), lax.axis_index("c")


def _flip(v, bit):
    return 1 - v if bit else v


def _place_shard(shard, chip_arr, name):
    _, rows, cols = shard.shape
    tr = min(256, rows)

    def body(chip_ref, x_ref, o_ref):
        o_ref[...] = x_ref[...].astype(BF)

    return pl.pallas_call(
        body, name=name,
        grid_spec=pltpu.PrefetchScalarGridSpec(
            num_scalar_prefetch=1, grid=(2, rows // tr),
            in_specs=[pl.BlockSpec((None, tr, cols), lambda l, i, chip: (l, i, 0))],
            out_specs=pl.BlockSpec((None, None, tr, cols), lambda l, i, chip: (l, chip[0], i, 0))),
        out_shape=jax.ShapeDtypeStruct((2, N_CHIPS, rows, cols), BF),
        compiler_params=_cparams("parallel", "parallel"),
    )(chip_arr, shard)


HBM = pl.BlockSpec(memory_space=pltpu.HBM)
SEM = pl.BlockSpec(memory_space=pltpu.SEMAPHORE)
EFFECT = pltpu.SideEffectType.DATAFLOW_SIDE_EFFECTING

GATHER_GROUPS = (((0, 0),), ((1, 0), (2, 0), (3, 0)), ((0, 1),), ((1, 1), (2, 1), (3, 1)))
GATHER_STARTS = ((0,), (1,), (2, 3))


def _gather_copies(arrs, group, send_sems, recv_sems):
    x, y, c = _mesh_pos()
    me = 2 * x + y
    out = []
    for i, (w, layer) in enumerate(group):
        mine = arrs[w].at[layer, me]
        for j, (qx, qy) in enumerate([(1 - x, y), (x, 1 - y), (1 - x, 1 - y)]):
            landed = arrs[w].at[layer, 2 * qx + qy]
            out.append(tuple(pltpu.make_async_remote_copy(
                src_ref=piece, dst_ref=piece, send_sem=send_sems.at[i * 3 + j], recv_sem=recv_sems.at[i * 3 + j],
                device_id=(qx, qy, c), device_id_type=MESH) for piece in (mine, landed)))
    return out


def _conv_copies(conv_src, conv_dst, send_sems, recv_sems):
    x, y, c = _mesh_pos()
    out = []
    for j, (qx, qy) in enumerate([(1 - x, y), (x, 1 - y), (1 - x, 1 - y)]):
        out.append(tuple(pltpu.make_async_remote_copy(
            src_ref=conv_src, dst_ref=conv_dst.at[q], send_sem=send_sems.at[j], recv_sem=recv_sems.at[j],
            device_id=(qx, qy, c), device_id_type=MESH) for q in (2 * x + y, 2 * qx + qy)))
    return out


def _gather_start(groups, arrs, conv, name, through=None):
    n_sems = 2 * (len(groups) + (conv is not None))
    mats = sorted({w for g in groups for w, _ in GATHER_GROUPS[g]})

    def body(*refs):
        arrs_ref = [None] * N_BIG
        for w, ref in zip(mats, refs):
            arrs_ref[w] = ref
        sems = refs[n_in:n_in + n_sems]
        if conv is not None:
            for cp, _ in _conv_copies(refs[len(mats)], refs[len(mats) + 1], sems[-2], sems[-1]):
                cp.start()
        for k, g in enumerate(groups):
            for cp, _ in _gather_copies(arrs_ref, GATHER_GROUPS[g], sems[2 * k], sems[2 * k + 1]):
                cp.start()

    sem_shapes = []
    for n in [len(GATHER_GROUPS[g]) for g in groups] + ([1] if conv is not None else []):
        sem_shapes += [pltpu.SemaphoreType.DMA((3 * n,))] * 2
    operands = [arrs[w] for w in mats] + ([] if conv is None else list(conv)) + ([] if through is None else [through])
    n_in = len(operands)
    res = pl.pallas_call(
        body, name=name,
        out_shape=tuple(sem_shapes) + tuple(pltpu.HBM(a.shape, a.dtype) for a in operands),
        in_specs=(HBM,) * n_in, out_specs=(SEM,) * n_sems + (HBM,) * n_in,
        input_output_aliases={i: n_sems + i for i in range(n_in)},
        compiler_params=pltpu.CompilerParams(has_side_effects=EFFECT),
    )(*[pltpu.with_memory_space_constraint(a, pltpu.HBM) for a in operands])
    arrs = list(arrs)
    for w, a in zip(mats, res[n_sems:]):
        arrs[w] = a
    return res[:n_sems], arrs, list(res[n_sems + len(mats):])


def _gather_wait(k, sems, arrs, conv, after, name):
    group = GATHER_GROUPS[k]
    mats = sorted({w for w, _ in group})
    n_conv = 0 if conv is None else 2

    def body(*refs):
        local = refs[:len(mats)]
        arrs_ref = [None] * N_BIG
        for w, ref in zip(mats, local):
            arrs_ref[w] = ref
        pos = len(mats) + n_conv
        copies = _gather_copies(arrs_ref, group, refs[pos], refs[pos + 1])
        if conv is not None:
            copies += _conv_copies(refs[len(mats)], refs[len(mats) + 1], refs[pos + 2], refs[pos + 3])
        for send, recv in copies:
            recv.wait_recv()
            send.wait_send()

    operands = [arrs[w] for w in mats] + ([] if conv is None else [conv[1], conv[2]])
    sem_ops = list(sems) + ([] if conv is None else list(conv[0]))
    n_op = len(operands)
    res = pl.pallas_call(
        body, name=name, out_shape=tuple(pltpu.HBM(a.shape, a.dtype) for a in operands),
        in_specs=(HBM,) * n_op + (SEM,) * len(sem_ops) + (ANY,) * len(after), out_specs=(HBM,) * n_op,
        input_output_aliases={i: i for i in range(n_op)},
        compiler_params=pltpu.CompilerParams(has_side_effects=EFFECT),
    )(*operands, *sem_ops, *after)
    arrs = list(arrs)
    for w, a in zip(mats, res):
        arrs[w] = a
    return arrs, (res[-1] if conv is not None else None)


def _grad_shard(ref, w, chip, n):
    start = pl.multiple_of(chip * n, 128)
    if SHARD_AXES[w] == 2:
        return ref.at[:, pl.ds(start, n)]
    return ref.at[pl.ds(start, n), :]


def _slot_shape(g, w):
    shape = list(g.shape)
    shape[SHARD_AXES[w] - 1] //= N_CHIPS
    return (N_DEV - 1,) + tuple(shape)


def _scatter_copies(g_ref, land_ref, send_sems, recv_sems, layer, w):
    x, y, c = _mesh_pos()
    n = g_ref.shape[SHARD_AXES[w] - 1] // N_CHIPS
    out = []
    for r in range(1, N_DEV):
        tx, ty, tc = _flip(x, r & 4), _flip(y, r & 2), _flip(c, r & 1)
        cp = pltpu.make_async_remote_copy(
            src_ref=_grad_shard(g_ref, w, 2 * tx + ty, n), dst_ref=land_ref.at[r - 1], send_sem=send_sems.at[r - 1],
            recv_sem=recv_sems.at[r - 1], device_id=(tx, ty, tc), device_id_type=MESH)
        out.append((cp, (c != layer) if r & 1 else (c == layer)))
    return out


def _scatter_start(g, land, layer, w, name):
    def body(g_ref, land_ref, send_sems, recv_sems, g_thru, land_thru, token):
        for cp, mine in _scatter_copies(g_ref, land_ref, send_sems, recv_sems, layer, w):
            @pl.when(mine)
            def _():
                cp.start()
        token[...] = jnp.zeros_like(token)

    return pl.pallas_call(
        body, name=name,
        out_shape=(pltpu.SemaphoreType.DMA((N_DEV - 1,)), pltpu.SemaphoreType.DMA((N_DEV - 1,)),
                   pltpu.HBM(g.shape, g.dtype), pltpu.HBM(land.shape, land.dtype),
                   jax.ShapeDtypeStruct((HALO, 128), F32)),
        in_specs=(HBM, HBM), out_specs=(SEM, SEM, HBM, HBM, pl.BlockSpec(memory_space=pltpu.VMEM)),
        input_output_aliases={0: 2, 1: 3}, compiler_params=pltpu.CompilerParams(has_side_effects=EFFECT),
    )(pltpu.with_memory_space_constraint(g, pltpu.HBM), pltpu.with_memory_space_constraint(land, pltpu.HBM))


def _scatter_wait(started, land, after, w, name):
    def body(g0_ref, g1_ref, land_ref, ss0, rs0, ss1, rs1, after_ref, g0_out, g1_out, land_out):
        c = lax.axis_index("c")
        for layer, g_ref, ss, rs in ((0, g0_ref, ss0, rs0), (1, g1_ref, ss1, rs1)):
            for cp, mine in _scatter_copies(g_ref, land_ref, ss, rs, layer, w):
                @pl.when(mine)
                def _():
                    cp.wait_send()

                @pl.when(c == layer)
                def _():
                    cp.wait_recv()

    (ss0, rs0, g0), (ss1, rs1, g1) = started
    return pl.pallas_call(
        body, name=name,
        out_shape=(pltpu.HBM(g0.shape, g0.dtype), pltpu.HBM(g1.shape, g1.dtype), pltpu.HBM(land.shape, land.dtype)),
        in_specs=(HBM, HBM, HBM, SEM, SEM, SEM, SEM, ANY), out_specs=(HBM, HBM, HBM),
        input_output_aliases={0: 0, 1: 1, 2: 2}, compiler_params=pltpu.CompilerParams(has_side_effects=EFFECT),
    )(g0, g1, land, ss0, rs0, ss1, rs1, after)


def _sum_slots(g0, g1, slots, w, pos_arr, name):
    _, rows, cols = slots.shape
    tr = min(256, rows)
    nr = rows // tr
    if SHARD_AXES[w] == 2:
        own = pl.BlockSpec((tr, cols), lambda i, pos: (i, pos[0]))
    else:
        own = pl.BlockSpec((tr, cols), lambda i, pos: (pos[0] * nr + i, 0))

    def body(pos_ref, own0_ref, own1_ref, s_ref, o_ref):
        acc = jnp.where(pos_ref[1] == 0, own0_ref[...], own1_ref[...]).astype(F32)
        for r in range(N_DEV - 1):
            acc = acc + s_ref[r].astype(F32)
        o_ref[...] = acc

    return pl.pallas_call(
        body, name=name,
        grid_spec=pltpu.PrefetchScalarGridSpec(
            num_scalar_prefetch=1, grid=(nr,),
            in_specs=[own, own, pl.BlockSpec((N_DEV - 1, tr, cols), lambda i, pos: (0, i, 0))],
            out_specs=pl.BlockSpec((tr, cols), lambda i, pos: (i, 0))),
        out_shape=jax.ShapeDtypeStruct((rows, cols), F32), compiler_params=_cparams("parallel"),
    )(pos_arr, g0, g1, slots)


def _swap_layers(halves, name):
    n = len(halves)

    def body(*refs):
        srcs, dsts = refs[:n], refs[n:2 * n]
        send_sems, recv_sems = refs[2 * n:]
        x, y, c = _mesh_pos()
        sends = [pltpu.make_async_remote_copy(src_ref=srcs[w], dst_ref=dsts[w], send_sem=send_sems.at[w],
                                              recv_sem=recv_sems.at[w], device_id=(x, y, 1 - c), device_id_type=MESH)
                 for w in range(n)]
        for cp in sends:
            cp.start()
        for cp in sends:
            cp.wait_recv()
        for cp in sends:
            cp.wait_send()

    return pl.pallas_call(
        body, name=name, in_specs=[ANY] * n, out_specs=[ANY] * n,
        out_shape=[jax.ShapeDtypeStruct(h.shape, h.dtype) for h in halves],
        scratch_shapes=[pltpu.SemaphoreType.DMA((n,)), pltpu.SemaphoreType.DMA((n,))],
    )(*halves)


def _adamw_math(w, g, m, v):
    m = ADAM_B1 * m + (1.0 - ADAM_B1) * g
    v = ADAM_B2 * v + (1.0 - ADAM_B2) * jnp.square(g)
    m_hat = m / (1.0 - ADAM_B1 ** ADAM_STEP)
    v_hat = v / (1.0 - ADAM_B2 ** ADAM_STEP)
    delta = -ADAM_LR * (m_hat / (jnp.sqrt(v_hat) + ADAM_EPS) + ADAM_WD * w)
    return delta, m, v


def _adamw(w, g_own, g_other, m, v, pos_arr, name):
    shape = w.shape
    _, rows, cols = shape
    tr = min(256, rows)

    def body(pos_ref, w_ref, own_ref, other_ref, m_ref, v_ref, g_ref, d_ref, m2_ref, v2_ref):
        g = jnp.where(pl.program_id(0) == pos_ref[1], own_ref[...], other_ref[...])
        g_ref[...] = g
        d_ref[...], m2_ref[...], v2_ref[...] = _adamw_math(w_ref[...], g, m_ref[...], v_ref[...])

    full = pl.BlockSpec((None, tr, cols), lambda l, i, pos: (l, i, 0))
    half = pl.BlockSpec((tr, cols), lambda l, i, pos: (i, 0))
    return pl.pallas_call(
        body, name=name,
        grid_spec=pltpu.PrefetchScalarGridSpec(
            num_scalar_prefetch=1, grid=(2, rows // tr),
            in_specs=[full, half, half, full, full], out_specs=[full] * 4),
        out_shape=[jax.ShapeDtypeStruct(shape, F32)] * 4, compiler_params=_cparams("parallel", "parallel"),
    )(pos_arr, w, g_own, g_other, m, v)


def _small_sync(part, w, m, v):
    rows, cols = part.shape

    def body(p_ref, w_ref, m_ref, v_ref, g_ref, d_ref, m2_ref, v2_ref, slots, send_sems, recv_sems):
        x, y, c = _mesh_pos()
        me = 4 * x + 2 * y + c
        slots[me] = p_ref[...]
        sends = []
        for r in range(1, N_DEV):
            to = (_flip(x, r & 4), _flip(y, r & 2), _flip(c, r & 1))
            sends.append(pltpu.make_async_remote_copy(
                src_ref=p_ref, dst_ref=slots.at[me], send_sem=send_sems.at[r - 1], recv_sem=recv_sems.at[r - 1],
                device_id=to, device_id_type=MESH))
        for cp in sends:
            cp.start()
        for cp in sends:
            cp.wait_recv()
        for cp in sends:
            cp.wait_send()
        g = slots[0]
        for i in range(1, N_DEV):
            g = g + slots[i]
        g_ref[...] = g
        d_ref[...], m2_ref[...], v2_ref[...] = _adamw_math(w_ref[...], g, m_ref[...], v_ref[...])

    vm = pl.BlockSpec(memory_space=pltpu.VMEM)
    return pl.pallas_call(
        body, name="small_sync", in_specs=[vm] * 4, out_specs=[vm] * 4,
        out_shape=[jax.ShapeDtypeStruct((rows, cols), F32)] * 4,
        scratch_shapes=[pltpu.VMEM((N_DEV, rows, cols), F32), pltpu.SemaphoreType.DMA((N_DEV - 1,)),
                        pltpu.SemaphoreType.DMA((N_DEV - 1,))],
    )(part, w, m, v)


PACK_W = 256


def _pack_rows(n):
    return -(-n // (HALO * PACK_W)) * HALO


def _pack_small(parts):
    out = []
    for a in parts:
        flat = a.reshape(-1)
        out.append(jnp.pad(flat, (0, _pack_rows(flat.size) * PACK_W - flat.size)).reshape(-1, PACK_W))
    return jnp.concatenate(out, axis=0)


def _unpack_small(p, shapes):
    out, row = [], 0
    for shape in shapes:
        n = 1
        for k in shape:
            n *= k
        out.append(p[row:row + _pack_rows(n)].reshape(-1)[:n].reshape(shape))
        row += _pack_rows(n)
    return out


def kernel(x, w_in, conv_w, sinks, g_mix, g_group, w_o, g_mlp, w_ff_in, w_ff_out, g_final, loss_target, m_w_in, m_conv_w, m_sinks, m_g_mix, m_g_group, m_w_o, m_g_mlp, m_w_ff_in, m_w_ff_out, m_g_final, v_w_in, v_conv_w, v_sinks, v_g_mix, v_g_group, v_w_o, v_g_mlp, v_w_ff_in, v_w_ff_out, v_g_final):
    chip = 2 * lax.axis_index("x") + lax.axis_index("y")
    conv_n = conv_w.shape[2]

    pos_arr = jnp.stack([chip, lax.axis_index("c")]).astype(jnp.int32)
    shards = (w_in, w_o, w_ff_in, w_ff_out)
    conv_tile = jnp.pad(conv_w.reshape(6, conv_n), ((0, HALO - 6), (0, 128 - conv_n)))
    placed = [_place_shard(w_in, pos_arr[:1], "place_shard_0"), None, None, None]
    sems_a, placed, conv_thru = _gather_start(
        GATHER_STARTS[0], placed, (conv_tile, lax.empty((N_CHIPS,) + conv_tile.shape, conv_tile.dtype)),
        "gather_start_0")
    for i in range(1, N_BIG):
        placed[i] = _place_shard(shards[i], pos_arr[:1], "place_shard_%d" % i)
    full = {"arrs": placed, "conv": None, "sems": list(sems_a[:2])}
    target = _to_strips(loss_target[0], placed[:1], "to_strips_target")

    def fetch(stage, layer, after):
        k = 2 * layer + stage
        sems = full["sems"][2 * k:2 * k + 2]
        if k == 0:
            full["arrs"], land = _gather_wait(0, sems, full["arrs"], (sems_a[-2:], *conv_thru), (after, target),
                                              "gather_wait_0")
            conv_all = lax.dynamic_update_slice(land, conv_tile[None], (chip, 0, 0))
            full["conv"] = conv_all[:, :6, :conv_n].reshape(N_CHIPS, 2, 3, conv_n).transpose(1, 2, 0, 3).reshape(
                2, 3, CONV_CH)
            sems_b, full["arrs"], rest = _gather_start(GATHER_STARTS[1], full["arrs"], None, "gather_start_1",
                                                       through=full["arrs"][0])
            full["arrs"][0] = rest[-1]
            full["sems"] += list(sems_b)
        else:
            full["arrs"], _ = _gather_wait(k, sems, full["arrs"], None, (after,), "gather_wait_%d" % k)
        if k == 1:
            sems_c, full["arrs"], _ = _gather_start(GATHER_STARTS[2], full["arrs"], None, "gather_start_2")
            full["sems"] += list(sems_c)
        return (*full["arrs"], full["conv"])

    lands, started = [None] * N_BIG, {}

    def emit(layer, w, g):
        if lands[w] is None:
            lands[w] = lax.empty(_slot_shape(g, w), g.dtype)
        *started[layer, w], lands[w], token = _scatter_start(g, lands[w], layer, w, "scatter_start_%d_%d" % (layer, w))
        return token[0, 0]

    loss_tile, dx, grads, dg_final = _local_step(_to_strips(x[0], placed, "to_strips_x"), target, fetch,
                                                 w_ff_in.shape[2] * N_CHIPS,
                                                 sinks, g_mix, g_group, g_mlp, g_final, emit)

    wmv = ((w_in, m_w_in, v_w_in), (w_o, m_w_o, v_w_o), (w_ff_in, m_w_ff_in, v_w_ff_in),
           (w_ff_out, m_w_ff_out, v_w_ff_out))
    big, after = [None] * N_BIG, dx
    for name, ws in (("swap_layers_rest", (1, 2, 3)), ("swap_layers_in", (0,))):
        own = []
        for w in ws:
            g0, g1, slots = _scatter_wait((started[0, w], started[1, w]), lands[w], after, w, "scatter_wait_%d" % w)
            own.append(_sum_slots(g0, g1, slots, w, pos_arr, "sum_slots_%d" % w))
        for w, mine, theirs in zip(ws, own, _swap_layers(own, name)):
            big[w] = _adamw(wmv[w][0], mine, theirs, wmv[w][1], wmv[w][2], pos_arr, "adamw_%d" % w)
        after = big[ws[-1]][1]

    def both(i):
        return jnp.stack([grads[0][i][0], grads[1][i][0]])
    dconv = jnp.stack([grads[0][0][:3], grads[1][0][:3]])
    dsinks = jnp.stack([grads[0][1][0, ::HEAD_DIM], grads[1][1][0, ::HEAD_DIM]])
    part = _pack_small([both(2), both(3), both(4), dg_final[0], dconv, dsinks, loss_tile[0, 0]])

    def spread(shard):
        return lax.dynamic_update_slice(jnp.zeros((2, 3, CONV_CH), F32), shard, (0, 0, chip * conv_n))
    zero = jnp.zeros((), F32)
    packs = [_pack_small([a, b, c_, e, spread(f), g_, zero]) for a, b, c_, e, f, g_ in (
        (g_mix, g_group, g_mlp, g_final, conv_w, sinks),
        (m_g_mix, m_g_group, m_g_mlp, m_g_final, m_conv_w, m_sinks),
        (v_g_mix, v_g_group, v_g_mlp, v_g_final, v_conv_w, v_sinks))]
    shapes = [g_mix.shape, g_group.shape, g_mlp.shape, g_final.shape, (2, 3, CONV_CH), sinks.shape, ()]
    small = [_unpack_small(p, shapes) for p in _small_sync(part, *packs)]

    def shard_of(full):
        return lax.dynamic_slice(full, (0, 0, chip * conv_n), (2, 3, conv_n))
    small = [(s[0], s[1], s[2], s[3], shard_of(s[4]), s[5], s[6]) for s in small]
    loss = small[0][6]

    def ordered(kind):
        b = [big[i][kind] for i in range(N_BIG)]
        s = small[kind]
        return [b[0], s[4], s[5], s[0], s[1], b[1], s[2], b[2], b[3], s[3]]

    return (loss, dx[None], *ordered(0), *ordered(1), *ordered(2), *ordered(3))
```

```python
import functools

import jax
import jax.numpy as jnp
from jax import lax
from jax.experimental import pallas as pl
from jax.experimental.pallas import tpu as pltpu

HEAD_DIM = 64
N_HEADS = 6
C_GROUP = 3
A_WIDTH = N_HEADS * HEAD_DIM
C_KV_WIDTH = 2 * HEAD_DIM
CONV_CH = 256
ZA_W = 3 * A_WIDTH
ZB_W = 3 * CONV_CH
ZC_W = A_WIDTH + 2 * C_KV_WIDTH
IN_WIDTH = ZA_W + ZB_W + ZC_W
MIX_WIDTH = A_WIDTH + CONV_CH + A_WIDTH
DILATIONS = (1, 4, 16)
A_MAX_DIST = 128
C_MAX_DIST = 127
TQ = 128
EPS = 1e-6
SCALE = HEAD_DIM ** -0.5
NEG = -1e30
HALO = 8

ADAM_LR = 0.001
ADAM_B1 = 0.9
ADAM_B2 = 0.999
ADAM_EPS = 1e-08
ADAM_WD = 0.01
ADAM_STEP = 10

BF = jnp.bfloat16
F32 = jnp.float32
MESH = pl.DeviceIdType.MESH
VMEM_LIMIT = 56 * 1024 * 1024


def _cparams(*sem):
    return pltpu.CompilerParams(dimension_semantics=sem, vmem_limit_bytes=VMEM_LIMIT)


def _nt(a, b):
    return lax.dot_general(a, b, (((1,), (1,)), ((), ())), preferred_element_type=F32)


def _tn(a, b):
    return lax.dot_general(a, b, (((0,), (0,)), ((), ())), preferred_element_type=F32)


def _nn(a, b):
    return jnp.dot(a, b, preferred_element_type=F32)


def _rows(tb, w):
    return pl.BlockSpec((tb, w), lambda i: (i, 0))


def _whole(shape):
    return pl.BlockSpec(shape, lambda *_: (0,) * len(shape))


def _layer(shape, l):
    return pl.BlockSpec((None,) + shape, lambda *_: (l,) + (0,) * len(shape))


def _rms_scale(v):
    return lax.rsqrt(jnp.mean(v * v, axis=-1, keepdims=True) + EPS)


def _norm_bwd(dxhat, xhat, r):
    return r * (dxhat - xhat * jnp.mean(dxhat * xhat, axis=-1, keepdims=True))


def _qkv_fwd(x, g, w_all, l, tb):
    s, d = x.shape

    def body(x_ref, g_ref, w_ref, h_ref, za_ref, zb_ref, zc_ref):
        xv = x_ref[...]
        h = ((xv * _rms_scale(xv)) * g_ref[...]).astype(BF)
        h_ref[...] = h
        z = jnp.concatenate([_nn(h, w_ref[k]) for k in range(N_CHIPS)], axis=1)
        za_ref[...] = z[:, :ZA_W]
        zb_ref[...] = z[:, ZA_W:ZA_W + ZB_W]
        zc_ref[...] = z[:, ZA_W + ZB_W:]

    return pl.pallas_call(
        body, grid=(s // tb,), name="qkv_fwd",
        in_specs=[_rows(tb, d), _whole((1, d)), _layer((N_CHIPS, d, IN_WIDTH // N_CHIPS), l)],
        out_specs=[_rows(tb, d), _rows(tb, ZA_W), _rows(tb, ZB_W), _rows(tb, ZC_W)],
        out_shape=[jax.ShapeDtypeStruct((s, d), BF), jax.ShapeDtypeStruct((s, ZA_W), F32),
                   jax.ShapeDtypeStruct((s, ZB_W), F32), jax.ShapeDtypeStruct((s, ZC_W), F32)],
        compiler_params=_cparams("parallel"),
    )(x, g, w_all)


N_STRIPS = 16


def _strips(a):
    s, w = a.shape
    return a.reshape(4, 4, s // N_STRIPS, w)


P_ROWS = {16: TQ, 4: 32, 1: 8}


def _p_sub(s, dil):
    return 2 if (s // dil // TQ) % 2 == 0 else 1


def _p_grid(s, dil):
    nb = s // dil // TQ // _p_sub(s, dil)
    return {16: (4, 4, nb), 4: (4, nb), 1: (nb,)}[dil]


def _p_spec(dil, cw, col, n_sub, prev=False):
    rows = P_ROWS[dil] * (1 if prev else n_sub)

    def blk(j):
        return jnp.maximum(n_sub * j - 1, 0) if prev else j
    if dil == 16:
        return pl.BlockSpec((None, None, rows, cw), lambda f, e, j: (f, e, blk(j), col))
    if dil == 4:
        return pl.BlockSpec((None, 4, rows, cw), lambda f, j: (f, 0, blk(j), col))
    return pl.BlockSpec((4, 4, rows, cw), lambda j: (0, 0, blk(j), col))


def _block_pos(i, dil):
    if dil == 16:
        return i
    if dil == 4:
        return 4 * (i % 32) + i // 32
    return 16 * (i % 8) + 4 * ((i // 8) % 4) + i // 32


def _band_mask(b, dil, max_dist):
    qi = _block_pos(lax.broadcasted_iota(jnp.int32, (TQ, 2 * TQ), 0), dil)
    col = lax.broadcasted_iota(jnp.int32, (TQ, 2 * TQ), 1)
    cur = col >= TQ
    dist = qi - _block_pos(col % TQ, dil) + jnp.where(cur, 0, TQ)
    return (dist >= 0) & (dist <= max_dist) & (cur | (b > 0))


def _hs(h):
    return slice(h * HEAD_DIM, (h + 1) * HEAD_DIM)


def _ld(ref, cols, rows=slice(None)):
    v = ref[..., rows, cols]
    return v.reshape(TQ, v.shape[-1])


def _st(ref, cols, val, rows=slice(None)):
    lead = ref.shape[:-2] + (ref.shape[-2] if rows == slice(None) else rows.stop - rows.start,)
    ref[..., rows, cols] = val.reshape(lead + (val.shape[-1],))


def _attn_fwd(z, dil, kw, kcol, vcol, n_rep, max_dist, name):
    s, zw = z.shape
    grid, n_sub = _p_grid(s, dil), _p_sub(s, dil)

    def body(q_ref, kp_ref, kc_ref, vp_ref, vc_ref, o_ref, lse_ref):
        for t in range(n_sub):
            rows = slice(t * P_ROWS[dil], (t + 1) * P_ROWS[dil])
            before = (slice(None),) if t == 0 else (slice((t - 1) * P_ROWS[dil], t * P_ROWS[dil]),)
            kb_ref, vb_ref = (kp_ref, vp_ref) if t == 0 else (kc_ref, vc_ref)
            mask = _band_mask(n_sub * pl.program_id(len(grid) - 1) if t == 0 else 1, dil, max_dist)
            scs, v2s = [], []
            for kh in range(N_HEADS // n_rep):
                k2 = jnp.concatenate([_ld(kb_ref, _hs(kh), *before), _ld(kc_ref, _hs(kh), rows)], axis=0).astype(BF)
                v2s.append(jnp.concatenate([_ld(vb_ref, _hs(kh), *before), _ld(vc_ref, _hs(kh), rows)],
                                           axis=0).astype(BF))
                for h in range(kh * n_rep, (kh + 1) * n_rep):
                    q = (_ld(q_ref, _hs(h), rows) * SCALE).astype(BF)
                    scs.append(jnp.where(mask, _nt(q, k2), NEG))
            for h, sc in enumerate(scs):
                m = jnp.max(sc, axis=1, keepdims=True)
                p = jnp.exp(sc - m)
                l = jnp.sum(p, axis=1, keepdims=True)
                _st(o_ref, _hs(h), _nn(p.astype(BF), v2s[h // n_rep]) / l, rows)
                _st(lse_ref, _hs(h), jnp.broadcast_to(m + jnp.log(l), (TQ, HEAD_DIM)), rows)

    res = pl.pallas_call(
        body, grid=grid, name=name,
        in_specs=[_p_spec(dil, A_WIDTH, 0, n_sub), _p_spec(dil, kw, kcol, n_sub, True), _p_spec(dil, kw, kcol, n_sub),
                  _p_spec(dil, kw, vcol, n_sub, True), _p_spec(dil, kw, vcol, n_sub)],
        out_specs=[_p_spec(dil, A_WIDTH, 0, n_sub)] * 2,
        out_shape=[jax.ShapeDtypeStruct((4, 4, s // N_STRIPS, A_WIDTH), F32)] * 2,
        compiler_params=_cparams(*(("parallel",) * len(grid))),
    )(*[_strips(z)] * 5)
    return [a.reshape(s, A_WIDTH) for a in res]


def _attn_merge(parts_a, part_c, sink_row, tb):
    s = part_c[0].shape[0]
    n_a = len(parts_a)

    def body(*refs):
        ins, sink_ref = refs[:2 * n_a + 2], refs[2 * n_a + 2]
        ya_ref, lsea_ref, yc_ref, lsec_ref = refs[2 * n_a + 3:]
        lses = [ins[2 * p + 1][...] for p in range(n_a)]
        m = functools.reduce(jnp.maximum, lses)
        ws = [jnp.exp(v - m) for v in lses]
        l = functools.reduce(jnp.add, ws)
        ya_ref[...] = functools.reduce(jnp.add, [w * ins[2 * p][...] for p, w in enumerate(ws)]) / l
        lsea_ref[...] = m + jnp.log(l)
        o_c, lse_c = [r[...] for r in ins[2 * n_a:]]
        sk = sink_ref[...]
        m2 = jnp.maximum(lse_c, sk)
        w = jnp.exp(lse_c - m2)
        l2 = w + jnp.exp(sk - m2)
        yc_ref[...] = o_c * (w / l2)
        lsec_ref[...] = m2 + jnp.log(l2)

    return pl.pallas_call(
        body, grid=(s // tb,), name="attn_merge",
        in_specs=[_rows(tb, A_WIDTH)] * (2 * n_a + 2) + [_whole((1, A_WIDTH))],
        out_specs=[_rows(tb, A_WIDTH)] * 4, out_shape=[jax.ShapeDtypeStruct((s, A_WIDTH), F32)] * 4,
        compiler_params=_cparams("parallel"),
    )(*[a for part in parts_a + [part_c] for a in part], sink_row)


def _shift_down(v, n, halo):
    rows = v.shape[0]
    out = pltpu.roll(v, n, 0)
    row = lax.broadcasted_iota(jnp.int32, v.shape, 0)
    for t in range(n):
        out = jnp.where(row == t, halo[HALO - n + t:HALO - n + t + 1, :], out)
    return out


def _shift_up(v, n, halo):
    rows = v.shape[0]
    out = pltpu.roll(v, rows - n, 0)
    row = lax.broadcasted_iota(jnp.int32, v.shape, 0)
    for t in range(n):
        out = jnp.where(row == rows - n + t, halo[t:t + 1, :], out)
    return out


def _strip(v, b):
    return v[b % 4, b // 4]


def _conv_strips(zb, prev, cw):
    gb = [_strip(zb, b)[:, :CONV_CH] for b in range(N_STRIPS)]
    gc = [_strip(zb, b)[:, CONV_CH:2 * CONV_CH] for b in range(N_STRIPS)]
    xb = [_strip(zb, b)[:, 2 * CONV_CH:] for b in range(N_STRIPS)]
    u = [g * v for g, v in zip(gc, xb)]
    uh = prev[:, :, CONV_CH:2 * CONV_CH] * prev[:, :, 2 * CONV_CH:]
    wrapped = {14: _shift_down(u[14], 1, uh[2]), 15: _shift_down(u[15], 1, uh[3])}
    u1 = [u[b - 1] if b >= 1 else wrapped[15] for b in range(N_STRIPS)]
    u2 = [u[b - 2] if b >= 2 else wrapped[14 + b] for b in range(N_STRIPS)]
    c = [cw[0:1, :] * u2[b] + cw[1:2, :] * u1[b] + cw[2:3, :] * u[b] for b in range(N_STRIPS)]
    return gb, gc, xb, u, u1, u2, c


def _strip_rows(ta, w):
    return pl.BlockSpec((4, 4, ta, w), lambda i: (0, 0, i, 0))


def _prev_rows(ta, w):
    return pl.BlockSpec((4, None, HALO, w), lambda i: (0, 3, jnp.maximum(i * (ta // HALO) - 1, 0), 0))


def _next_rows(ta, w, nblk):
    return pl.BlockSpec((4, None, HALO, w),
                        lambda i: (0, 0, jnp.minimum((i + 1) * (ta // HALO), nblk * (ta // HALO) - 1), 0))


def _mix_fwd(x, ya, yc, zb, cw, gg, wo_all, l, tb):
    s, d = x.shape
    ta = tb // N_STRIPS

    def body(x_ref, ya_ref, yc_ref, zb_ref, zbp_ref, cw_ref, gg_ref, wo_ref, x1_ref, yb_ref):
        i = pl.program_id(0)
        prev = jnp.where(i > 0, zbp_ref[...], 0.0)
        gb, _, _, _, _, _, c = _conv_strips(zb_ref[...], prev, cw_ref[...])
        for b in range(N_STRIPS):
            yb_ref[b % 4, b // 4] = gb[b] * c[b]
        yb = yb_ref[...].reshape(tb, CONV_CH)
        ya, yc = ya_ref[...].reshape(tb, A_WIDTH), yc_ref[...].reshape(tb, A_WIDTH)
        n = jnp.concatenate([ya * _rms_scale(ya), yb * _rms_scale(yb), yc * _rms_scale(yc)], axis=1)
        n = (n * gg_ref[...]).astype(BF)
        x1 = x_ref[...].reshape(tb, d) + _nn(n, wo_ref[...].reshape(MIX_WIDTH, d))
        x1_ref[...] = x1.reshape(4, 4, ta, d)

    res = pl.pallas_call(
        body, grid=(s // tb,), name="mix_fwd",
        in_specs=[_strip_rows(ta, d), _strip_rows(ta, A_WIDTH), _strip_rows(ta, A_WIDTH), _strip_rows(ta, ZB_W),
                  _prev_rows(ta, ZB_W), _whole((HALO, CONV_CH)), _whole((1, MIX_WIDTH)),
                  _layer((N_CHIPS, MIX_WIDTH // N_CHIPS, d), l)],
        out_specs=[_strip_rows(ta, d), _strip_rows(ta, CONV_CH)],
        out_shape=[jax.ShapeDtypeStruct((4, 4, s // N_STRIPS, d), F32),
                   jax.ShapeDtypeStruct((4, 4, s // N_STRIPS, CONV_CH), F32)],
        compiler_params=_cparams("parallel"),
    )(_strips(x), _strips(ya), _strips(yc), _strips(zb), _strips(zb), cw, gg, wo_all)
    return res[0].reshape(s, d), res[1].reshape(s, CONV_CH)


def _mlp_fwd(x1, g, w1_all, w2_all, l, tb, tf):
    s, d = x1.shape
    ff = w1_all.shape[1] * w1_all.shape[3]
    nj = ff // tf

    def body(x_ref, g_ref, w1_ref, w2_ref, x2_ref, h2_ref, ap_ref, acc):
        j = pl.program_id(1)

        @pl.when(j == 0)
        def _():
            xv = x_ref[...]
            h2_ref[...] = ((xv * _rms_scale(xv)) * g_ref[...]).astype(BF)
            acc[...] = jnp.zeros_like(acc)

        ap = _nn(h2_ref[...], w1_ref[...])
        ap_ref[...] = ap.astype(BF)
        a = jnp.square(jnp.maximum(ap, 0.0)).astype(BF)
        acc[...] += _nn(a, w2_ref[...])

        @pl.when(j == nj - 1)
        def _():
            x2_ref[...] = x_ref[...] + acc[...]

    return pl.pallas_call(
        body, grid=(s // tb, nj), name="mlp_fwd",
        in_specs=[pl.BlockSpec((tb, d), lambda i, j: (i, 0)), _whole((1, d)),
                  pl.BlockSpec((None, None, d, tf), lambda i, j: (l, j, 0, 0)),
                  pl.BlockSpec((None, None, tf, d), lambda i, j: (l, j, 0, 0))],
        out_specs=[pl.BlockSpec((tb, d), lambda i, j: (i, 0)), pl.BlockSpec((tb, d), lambda i, j: (i, 0)),
                   pl.BlockSpec((tb, tf), lambda i, j: (i, j))],
        out_shape=[jax.ShapeDtypeStruct((s, d), F32), jax.ShapeDtypeStruct((s, d), BF),
                   jax.ShapeDtypeStruct((s, ff), BF)],
        scratch_shapes=[pltpu.VMEM((tb, d), F32)],
        compiler_params=_cparams("parallel", "arbitrary"),
    )(x1, g, w1_all, w2_all)


def _loss_head(x, g, tgt, tb):
    s, d = x.shape

    def body(x_ref, g_ref, t_ref, dx_ref, loss_ref, dg_ref):
        i = pl.program_id(0)

        @pl.when(i == 0)
        def _():
            loss_ref[...] = jnp.zeros_like(loss_ref)
            dg_ref[...] = jnp.zeros_like(dg_ref)

        xv = x_ref[...]
        r = _rms_scale(xv)
        xhat = xv * r
        err = xhat * g_ref[...] - t_ref[...]
        part = jnp.sum(jnp.mean(jnp.square(err), axis=-1, keepdims=True), axis=0, keepdims=True)
        loss_ref[...] += 0.5 * part
        dy = err * (1.0 / d)
        dg_ref[...] += jnp.sum(dy * xhat, axis=0, keepdims=True)
        dx_ref[...] = _norm_bwd(dy * g_ref[...], xhat, r)

    return pl.pallas_call(
        body, grid=(s // tb,), name="loss_head",
        in_specs=[_rows(tb, d), _whole((1, d)), _rows(tb, d)],
        out_specs=[_rows(tb, d), _whole((HALO, 128)), _whole((HALO, d))],
        out_shape=[jax.ShapeDtypeStruct((s, d), F32), jax.ShapeDtypeStruct((HALO, 128), F32),
                   jax.ShapeDtypeStruct((HALO, d), F32)],
        compiler_params=_cparams("arbitrary"),
    )(x, g, tgt)


def _mlp_bwd(dx2, x1, ap, g, w1_all, w2_all, l, tb, tf):
    s, d = x1.shape
    ff = ap.shape[1]
    nj = ff // tf

    def body(dx2_ref, x1_ref, ap_ref, g_ref, w1_ref, w2_ref, dx1_ref, dap_ref, dg_ref, acc):
        i, j = pl.program_id(0), pl.program_id(1)

        @pl.when((i == 0) & (j == 0))
        def _():
            dg_ref[...] = jnp.zeros_like(dg_ref)

        @pl.when(j == 0)
        def _():
            acc[...] = jnp.zeros_like(acc)

        da = _nt(dx2_ref[...].astype(BF), w2_ref[...])
        dap = (da * (2.0 * jnp.maximum(ap_ref[...].astype(F32), 0.0))).astype(BF)
        dap_ref[...] = dap
        acc[...] += _nt(dap, w1_ref[...])

        @pl.when(j == nj - 1)
        def _():
            xv = x1_ref[...]
            r = _rms_scale(xv)
            xhat = xv * r
            dh = acc[...]
            dg_ref[...] += jnp.sum(dh * xhat, axis=0, keepdims=True)
            dx1_ref[...] = dx2_ref[...] + _norm_bwd(dh * g_ref[...], xhat, r)

    return pl.pallas_call(
        body, grid=(s // tb, nj), name="mlp_bwd",
        in_specs=[pl.BlockSpec((tb, d), lambda i, j: (i, 0)), pl.BlockSpec((tb, d), lambda i, j: (i, 0)),
                  pl.BlockSpec((tb, tf), lambda i, j: (i, j)),
                  _whole((1, d)), pl.BlockSpec((None, None, d, tf), lambda i, j: (l, j, 0, 0)),
                  pl.BlockSpec((None, None, tf, d), lambda i, j: (l, j, 0, 0))],
        out_specs=[pl.BlockSpec((tb, d), lambda i, j: (i, 0)), pl.BlockSpec((tb, tf), lambda i, j: (i, j)),
                   _whole((HALO, d))],
        out_shape=[jax.ShapeDtypeStruct((s, d), F32), jax.ShapeDtypeStruct((s, ff), BF),
                   jax.ShapeDtypeStruct((HALO, d), F32)],
        scratch_shapes=[pltpu.VMEM((tb, d), F32)],
        compiler_params=_cparams("arbitrary", "arbitrary"),
    )(dx2, x1, ap, g, w1_all, w2_all)


def _wgrad(a, b, tm, tn, ts, name, relu2=False):
    s, m = a.shape
    n = b.shape[1]
    ns = s // ts

    def body(a_ref, b_ref, o_ref, acc):
        k = pl.program_id(2)

        @pl.when(k == 0)
        def _():
            acc[...] = jnp.zeros_like(acc)

        av = a_ref[...]
        if relu2:
            av = jnp.square(jnp.maximum(av.astype(F32), 0.0)).astype(BF)
        acc[...] += _tn(av, b_ref[...].astype(BF))

        @pl.when(k == ns - 1)
        def _():
            o_ref[...] = acc[...].astype(BF)

    return pl.pallas_call(
        body, grid=(m // tm, n // tn, ns), name=name,
        in_specs=[pl.BlockSpec((ts, tm), lambda i, j, k: (k, i)), pl.BlockSpec((ts, tn), lambda i, j, k: (k, j))],
        out_specs=pl.BlockSpec((tm, tn), lambda i, j, k: (i, j)),
        out_shape=jax.ShapeDtypeStruct((m, n), BF),
        scratch_shapes=[pltpu.VMEM((tm, tn), F32)],
        compiler_params=_cparams("parallel", "parallel", "arbitrary"),
    )(a, b)


def _mix_bwd(dx1, ya, yb, yc, lse_c, sink_row, gg, wo_all, l, tb):
    s, d = dx1.shape

    def body(dx_ref, ya_ref, yb_ref, yc_ref, lse_ref, sink_ref, gg_ref, wo_ref,
             n_ref, dya_ref, dyc_ref, da_ref, dc_ref, dyb_ref, dg_ref, dsink_ref):
        i = pl.program_id(0)

        @pl.when(i == 0)
        def _():
            dg_ref[...] = jnp.zeros_like(dg_ref)
            dsink_ref[...] = jnp.zeros_like(dsink_ref)

        dn = _nt(dx_ref[...].astype(BF), wo_ref[...].reshape(MIX_WIDTH, d))
        ys = [ya_ref[...], yb_ref[...], yc_ref[...]]
        rs = [_rms_scale(v) for v in ys]
        nhat = jnp.concatenate([v * r for v, r in zip(ys, rs)], axis=1)
        gg = gg_ref[...]
        n_ref[...] = (nhat * gg).astype(BF)
        dg_ref[...] += jnp.sum(dn * nhat, axis=0, keepdims=True)
        dnh = dn * gg
        bounds = [(0, A_WIDTH), (A_WIDTH, A_WIDTH + CONV_CH), (A_WIDTH + CONV_CH, MIX_WIDTH)]
        dys = [_norm_bwd(dnh[:, lo:hi], nhat[:, lo:hi], r) for (lo, hi), r in zip(bounds, rs)]
        dyb_ref[...] = dys[1]
        for dy, y, dy_ref, dd_ref in ((dys[0], ys[0], dya_ref, da_ref), (dys[2], ys[2], dyc_ref, dc_ref)):
            dy_ref[...] = dy
            t = dy * y
            for h in range(N_HEADS):
                dd_ref[:, _hs(h)] = jnp.broadcast_to(jnp.sum(t[:, _hs(h)], axis=1, keepdims=True), (tb, HEAD_DIM))
        dsink_ref[...] -= jnp.sum(jnp.exp(sink_ref[...] - lse_ref[...]) * dc_ref[...], axis=0, keepdims=True)

    return pl.pallas_call(
        body, grid=(s // tb,), name="mix_bwd",
        in_specs=[_rows(tb, d), _rows(tb, A_WIDTH), _rows(tb, CONV_CH), _rows(tb, A_WIDTH), _rows(tb, A_WIDTH),
                  _whole((1, A_WIDTH)), _whole((1, MIX_WIDTH)), _layer((N_CHIPS, MIX_WIDTH // N_CHIPS, d), l)],
        out_specs=[_rows(tb, MIX_WIDTH), _rows(tb, A_WIDTH), _rows(tb, A_WIDTH), _rows(tb, A_WIDTH),
                   _rows(tb, A_WIDTH), _rows(tb, CONV_CH), _whole((HALO, MIX_WIDTH)), _whole((HALO, A_WIDTH))],
        out_shape=[jax.ShapeDtypeStruct((s, MIX_WIDTH), BF), jax.ShapeDtypeStruct((s, A_WIDTH), F32),
                   jax.ShapeDtypeStruct((s, A_WIDTH), F32), jax.ShapeDtypeStruct((s, A_WIDTH), F32),
                   jax.ShapeDtypeStruct((s, A_WIDTH), F32), jax.ShapeDtypeStruct((s, CONV_CH), F32),
                   jax.ShapeDtypeStruct((HALO, MIX_WIDTH), F32), jax.ShapeDtypeStruct((HALO, A_WIDTH), F32)],
        compiler_params=_cparams("arbitrary"),
    )(dx1, ya, yb, yc, lse_c, sink_row, gg, wo_all)


def _attn_bwd(z, dy, lse, dd, dil, kw, kcol, vcol, n_rep, max_dist, name):
    s, zw = z.shape
    grid, n_sub = _p_grid(s, dil), _p_sub(s, dil)
    n_kv = N_HEADS // n_rep
    dt = F32 if dil == 1 else BF

    def body(q_ref, kp_ref, kc_ref, vp_ref, vc_ref, dy_ref, lse_ref, dd_ref, dq_ref, dkp_ref, dkc_ref, dvp_ref, dvc_ref):
        for t in range(n_sub):
            rows = slice(t * P_ROWS[dil], (t + 1) * P_ROWS[dil])
            before = (slice(None),) if t == 0 else (slice((t - 1) * P_ROWS[dil], t * P_ROWS[dil]),)
            kb_ref, vb_ref = (kp_ref, vp_ref) if t == 0 else (kc_ref, vc_ref)
            mask = _band_mask(n_sub * pl.program_id(len(grid) - 1) if t == 0 else 1, dil, max_dist)
            k2s, qs, dys, scs, dps = [], [], [], [], []
            for kh in range(n_kv):
                k2s.append(jnp.concatenate([_ld(kb_ref, _hs(kh), *before), _ld(kc_ref, _hs(kh), rows)],
                                           axis=0).astype(BF))
                v2 = jnp.concatenate([_ld(vb_ref, _hs(kh), *before), _ld(vc_ref, _hs(kh), rows)], axis=0).astype(BF)
                for h in range(kh * n_rep, (kh + 1) * n_rep):
                    qs.append((_ld(q_ref, _hs(h), rows) * SCALE).astype(BF))
                    dys.append(_ld(dy_ref, _hs(h), rows).astype(BF))
                    scs.append(jnp.where(mask, _nt(qs[h], k2s[kh]), NEG))
                    dps.append(_nt(dys[h], v2))
            for kh in range(n_kv):
                k2 = k2s[kh]
                dk2 = jnp.zeros((2 * TQ, HEAD_DIM), F32)
                dv2 = jnp.zeros((2 * TQ, HEAD_DIM), F32)
                for h in range(kh * n_rep, (kh + 1) * n_rep):
                    lse_h = _ld(lse_ref, slice(h * HEAD_DIM, h * HEAD_DIM + 1), rows)
                    dd_h = _ld(dd_ref, slice(h * HEAD_DIM, h * HEAD_DIM + 1), rows)
                    p = jnp.exp(scs[h] - lse_h)
                    ds = (p * (dps[h] - dd_h)).astype(BF)
                    _st(dq_ref, _hs(h), (_nn(ds, k2) * SCALE).astype(dt), rows)
                    dk2 = dk2 + _tn(ds, qs[h])
                    dv2 = dv2 + _tn(p.astype(BF), dys[h])
                _st(dkp_ref, _hs(kh), dk2[:TQ].astype(dt), rows)
                _st(dkc_ref, _hs(kh), dk2[TQ:].astype(dt), rows)
                _st(dvp_ref, _hs(kh), dv2[:TQ].astype(dt), rows)
                _st(dvc_ref, _hs(kh), dv2[TQ:].astype(dt), rows)

    args = [_strips(z)] * 5 + [_strips(a) for a in (dy, lse, dd)]
    pair = _p_spec(dil, A_WIDTH, 0, n_sub)
    in_specs = [pair, _p_spec(dil, kw, kcol, n_sub, True), _p_spec(dil, kw, kcol, n_sub),
                _p_spec(dil, kw, vcol, n_sub, True), _p_spec(dil, kw, vcol, n_sub)] + [pair] * 3
    out_specs = [pair] + [_p_spec(dil, kw, 0, n_sub)] * 4
    na = s // N_STRIPS
    out_shape = [jax.ShapeDtypeStruct((4, 4, na, A_WIDTH), dt)] + [jax.ShapeDtypeStruct((4, 4, na, kw), dt)] * 4
    res = pl.pallas_call(
        body, grid=grid, name=name, in_specs=in_specs, out_specs=out_specs, out_shape=out_shape,
        compiler_params=_cparams(*(("parallel",) * len(grid))),
    )(*args)
    return [res[0].reshape(s, A_WIDTH)] + [a.reshape(s, kw) for a in res[1:]]


DZ_TA = 16


def _dz_assemble(parts_a, parts_c, dyb, zb, cw):
    s = zb.shape[0]
    na = s // N_STRIPS
    nb = na // DZ_TA

    def ahead(w, k):
        return pl.BlockSpec((4, 4, DZ_TA, w), lambda i: (0, 0, jnp.minimum(i + k, nb - 1), 0))

    args, in_specs = [], []
    for dil, (dq, dkp, dkc, dvp, dvc) in zip(DILATIONS + (1,), parts_a + [parts_c]):
        w = dkp.shape[1]
        here = _strip_rows(DZ_TA, w)
        if dil == 1:
            args += [dq, dkp, dkp, dkc, dvp, dvp, dvc]
            in_specs += [_strip_rows(DZ_TA, A_WIDTH), here, ahead(w, 1), here, here, ahead(w, 1), here]
        else:
            k = 8 * dil // DZ_TA
            args += [dq, dkp, dkc, dvp, dvc]
            in_specs += [_strip_rows(DZ_TA, A_WIDTH), ahead(w, k), here, ahead(w, k), here]
    n_att = len(args)
    args = [_strips(a) for a in args] + [_strips(dyb), _strips(dyb), _strips(zb), _strips(zb), _strips(zb), cw]
    in_specs += [_strip_rows(DZ_TA, CONV_CH), _next_rows(DZ_TA, CONV_CH, nb), _strip_rows(DZ_TA, ZB_W),
                 _prev_rows(DZ_TA, ZB_W), _next_rows(DZ_TA, ZB_W, nb), _whole((HALO, CONV_CH))]

    def body(*refs):
        att = list(refs[:n_att])
        dyb_ref, dybn_ref, zb_ref, zbp_ref, zbn_ref, cw_ref, dz_ref, dcw_ref = refs[n_att:]
        i = pl.program_id(0)

        @pl.when(i == 0)
        def _():
            dcw_ref[...] = jnp.zeros_like(dcw_ref)

        def shifted(dil):
            if dil == 1:
                dq_r, kp0, kp1, dkc_r, vp0, vp1, dvc_r = [att.pop(0) for _ in range(7)]
                live = i + 1 < nb
                half = DZ_TA // 2
                dkp = jnp.concatenate([kp0[:, :, half:, :], jnp.where(live, kp1[:, :, :half, :], 0.0)], axis=2)
                dvp = jnp.concatenate([vp0[:, :, half:, :], jnp.where(live, vp1[:, :, :half, :], 0.0)], axis=2)
            else:
                dq_r, dkp_r, dkc_r, dvp_r, dvc_r = [att.pop(0) for _ in range(5)]
                live = i + 8 * dil // DZ_TA < nb
                dkp = jnp.where(live, dkp_r[...].astype(F32), 0.0)
                dvp = jnp.where(live, dvp_r[...].astype(F32), 0.0)
            return dq_r[...].astype(F32), dkc_r[...].astype(F32) + dkp, dvc_r[...].astype(F32) + dvp

        dq, dk, dv = shifted(DILATIONS[0])
        for dil in DILATIONS[1:]:
            dq2, dk2, dv2 = shifted(dil)
            dq, dk, dv = dq + dq2, dk + dk2, dv + dv2
        dz_ref[:, :, :, 0:A_WIDTH] = dq.astype(BF)
        dz_ref[:, :, :, A_WIDTH:2 * A_WIDTH] = dk.astype(BF)
        dz_ref[:, :, :, 2 * A_WIDTH:ZA_W] = dv.astype(BF)
        dq, dk, dv = shifted(1)
        c0 = ZA_W + ZB_W
        dz_ref[:, :, :, c0:c0 + A_WIDTH] = dq.astype(BF)
        dz_ref[:, :, :, c0 + A_WIDTH:c0 + A_WIDTH + C_KV_WIDTH] = dk.astype(BF)
        dz_ref[:, :, :, c0 + A_WIDTH + C_KV_WIDTH:IN_WIDTH] = dv.astype(BF)

        cw = cw_ref[...]
        prev = jnp.where(i > 0, zbp_ref[...], 0.0)
        gb, gc, xb, u, u1, u2, c = _conv_strips(zb_ref[...], prev, cw)
        dyb = dyb_ref[...]
        dc = [_strip(dyb, b) * gb[b] for b in range(N_STRIPS)]
        dcn = jnp.where(i + 1 < nb, dybn_ref[...] * zbn_ref[:, :, :CONV_CH], 0.0)
        wrapped = [_shift_up(dc[0], 1, dcn[0]), _shift_up(dc[1], 1, dcn[1])]
        upd = [jnp.zeros((1, CONV_CH), F32)] * 3
        for b in range(N_STRIPS):
            dc1 = dc[b + 1] if b + 1 < N_STRIPS else wrapped[0]
            dc2 = dc[b + 2] if b + 2 < N_STRIPS else wrapped[b + 2 - N_STRIPS]
            du = cw[2:3, :] * dc[b] + cw[1:2, :] * dc1 + cw[0:1, :] * dc2
            f, e = b % 4, b // 4
            dz_ref[f, e, :, ZA_W:ZA_W + CONV_CH] = (_strip(dyb, b) * c[b]).astype(BF)
            dz_ref[f, e, :, ZA_W + CONV_CH:ZA_W + 2 * CONV_CH] = (du * xb[b]).astype(BF)
            dz_ref[f, e, :, ZA_W + 2 * CONV_CH:c0] = (du * gc[b]).astype(BF)
            for t, uu in enumerate((u2[b], u1[b], u[b])):
                upd[t] = upd[t] + jnp.sum(dc[b] * uu, axis=0, keepdims=True)
        row = lax.broadcasted_iota(jnp.int32, (HALO, CONV_CH), 0)
        tile = jnp.zeros((HALO, CONV_CH), F32)
        for t in range(3):
            tile = jnp.where(row == t, upd[t], tile)
        dcw_ref[...] += tile

    dz, dcw = pl.pallas_call(
        body, grid=(nb,), name="dz_assemble", in_specs=in_specs,
        out_specs=[_strip_rows(DZ_TA, IN_WIDTH), _whole((HALO, CONV_CH))],
        out_shape=[jax.ShapeDtypeStruct((4, 4, na, IN_WIDTH), BF), jax.ShapeDtypeStruct((HALO, CONV_CH), F32)],
        compiler_params=_cparams("arbitrary"),
    )(*args)
    return dz.reshape(s, IN_WIDTH), dcw


def _qkv_bwd(dz, dx1, x, g, w_all, l, tb, tokens_out):
    s, d = x.shape
    na, ta = s // N_STRIPS, tb // N_STRIPS

    def body(dz_ref, dx1_ref, x_ref, g_ref, w_ref, dx_ref, dg_ref):
        i = pl.program_id(0)

        @pl.when(i == 0)
        def _():
            dg_ref[...] = jnp.zeros_like(dg_ref)

        n = IN_WIDTH // N_CHIPS
        dz = dz_ref[...].reshape(tb, IN_WIDTH)
        dh = _nt(dz[:, 0:n], w_ref[0])
        for k in range(1, N_CHIPS):
            dh = dh + _nt(dz[:, k * n:(k + 1) * n], w_ref[k])
        xv = x_ref[...].reshape(tb, d)
        r = _rms_scale(xv)
        xhat = xv * r
        dg_ref[...] += jnp.sum(dh * xhat, axis=0, keepdims=True)
        dx = (dx1_ref[...].reshape(tb, d) + _norm_bwd(dh * g_ref[...], xhat, r)).reshape(4, 4, ta, d)
        if tokens_out:
            for b in range(N_STRIPS):
                dx_ref[:, b, :] = _strip(dx, b)
        else:
            dx_ref[...] = dx

    if tokens_out:
        dx_spec, dx_shape = pl.BlockSpec((ta, N_STRIPS, d), lambda i: (i, 0, 0)), (na, N_STRIPS, d)
    else:
        dx_spec, dx_shape = _strip_rows(ta, d), (4, 4, na, d)
    dx, dg = pl.pallas_call(
        body, grid=(s // tb,), name="qkv_bwd",
        in_specs=[_strip_rows(ta, IN_WIDTH), _strip_rows(ta, d), _strip_rows(ta, d), _whole((1, d)),
                  _layer((N_CHIPS, d, IN_WIDTH // N_CHIPS), l)],
        out_specs=[dx_spec, _whole((HALO, d))],
        out_shape=[jax.ShapeDtypeStruct(dx_shape, F32), jax.ShapeDtypeStruct((HALO, d), F32)],
        compiler_params=_cparams("arbitrary"),
    )(_strips(dz), _strips(dx1), _strips(x), g, w_all)
    return dx.reshape(s, d), dg


def _tile_rows(rows):
    return jnp.pad(rows, ((0, HALO - rows.shape[0]), (0, 0)))


def _to_strips(a, after, name):
    s, d = a.shape
    na = s // N_STRIPS
    ta = min(32, na)

    def body(a_ref, *rest):
        for b in range(N_STRIPS):
            rest[-1][b % 4, b // 4] = a_ref[:, b, :]

    return pl.pallas_call(
        body, grid=(na // ta,), name=name,
        in_specs=[pl.BlockSpec((ta, N_STRIPS, d), lambda i: (i, 0, 0))] + [ANY] * len(after),
        out_specs=_strip_rows(ta, d),
        out_shape=jax.ShapeDtypeStruct((4, 4, na, d), a.dtype), compiler_params=_cparams("parallel"),
    )(a.reshape(na, N_STRIPS, d), *after).reshape(s, d)


def _local_step(x, tgt, fetch, ff, sinks, g_mix, g_group, g_mlp, g_final, emit):
    s, d = x.shape
    depth = g_mix.shape[0]
    tb = min(512, s)
    tf = ff // N_CHIPS
    ts = min(1024, s)
    saved = []
    for l in range(depth):
        w_in, _, _, _, conv_w = fetch(0, l, x)
        cw = _tile_rows(conv_w[l])
        sk = jnp.repeat(sinks[l].reshape(N_HEADS), HEAD_DIM)[None]
        h, za, zb, zc = _qkv_fwd(x, g_mix[l][None], w_in, l, tb)
        parts_a = [_attn_fwd(za, dil, A_WIDTH, 1, 2, 1, A_MAX_DIST, "attn_a_fwd_%d" % dil) for dil in DILATIONS]
        part_c = _attn_fwd(zc, 1, C_KV_WIDTH, 3, 4, C_GROUP, C_MAX_DIST, "attn_c_fwd")
        ya, lse_a, yc, lse_c = _attn_merge(parts_a, part_c, sk, tb)
        w_in, w_o, w1, w2, _ = fetch(1, l, yc)
        x1, yb = _mix_fwd(x, ya, yc, zb, cw, g_group[l][None], w_o, l, tb)
        x2, h2, ap = _mlp_fwd(x1, g_mlp[l][None], w1, w2, l, ts, tf)
        saved.append((x, h, za, zb, zc, ya, lse_a, yc, lse_c, yb, x1, h2, ap, cw, sk))
        x = x2
    dx, loss_tile, dg_final = _loss_head(x, g_final[None], tgt, tb)
    grads = [None] * depth
    tok = jnp.zeros((), F32)
    for l in reversed(range(depth)):
        x0, h, za, zb, zc, ya, lse_a, yc, lse_c, yb, x1, h2, ap, cw, sk = saved[l]
        dx1, dap, dg_mlp = _mlp_bwd(dx, x1, ap, g_mlp[l][None] + tok, w1, w2, l, ts, tf)
        tok = emit(l, 3, _wgrad(ap, dx, min(1024, ff), d, ts, "wgrad_ff_out", relu2=True))
        tok = tok + emit(l, 2, _wgrad(h2, dap, d, min(1024, ff), 2 * ts, "wgrad_ff_in"))
        n, dya, dyc, dd_a, dd_c, dyb, dg_group, dsink = _mix_bwd(dx1, ya, yb, yc, lse_c, sk, g_group[l][None] + tok,
                                                                 w_o, l, tb)
        tok = emit(l, 1, _wgrad(n, dx1, MIX_WIDTH, d, ts, "wgrad_o"))
        cw = cw + tok
        parts_a = [_attn_bwd(za, dya, lse_a, dd_a, dil, A_WIDTH, 1, 2, 1, A_MAX_DIST, "attn_a_bwd_%d" % dil)
                   for dil in DILATIONS]
        parts_c = _attn_bwd(zc, dyc, lse_c, dd_c, 1, C_KV_WIDTH, 3, 4, C_GROUP, C_MAX_DIST, "attn_c_bwd")
        dz, dcw = _dz_assemble(parts_a, parts_c, dyb, zb, cw)
        tok = emit(l, 0, _wgrad(h, dz, d, IN_WIDTH // 4, 2 * ts, "wgrad_in"))
        dx, dg_mix = _qkv_bwd(dz, dx1, x0, g_mix[l][None] + tok, w_in, l, tb, l == 0)
        grads[l] = (dcw, dsink, dg_mix, dg_group, dg_mlp)
    return loss_tile, dx, grads, dg_final


ANY = pl.BlockSpec(memory_space=pl.ANY)
SHARD_AXES = (2, 1, 2, 1)
N_BIG = len(SHARD_AXES)
N_CHIPS = 4
N_DEV = 8


def _mesh_pos():
    return lax.axis_index("x"), lax.axis_index("y"), lax.axis_index("c")


def _flip(v, bit):
    return 1 - v if bit else v


def _place_shard(shard, chip_arr, name):
    _, rows, cols = shard.shape
    tr = min(256, rows)

    def body(chip_ref, x_ref, o_ref):
        o_ref[...] = x_ref[...].astype(BF)

    return pl.pallas_call(
        body, name=name,
        grid_spec=pltpu.PrefetchScalarGridSpec(
            num_scalar_prefetch=1, grid=(2, rows // tr),
            in_specs=[pl.BlockSpec((None, tr, cols), lambda l, i, chip: (l, i, 0))],
            out_specs=pl.BlockSpec((None, None, tr, cols), lambda l, i, chip: (l, chip[0], i, 0))),
        out_shape=jax.ShapeDtypeStruct((2, N_CHIPS, rows, cols), BF),
        compiler_params=_cparams("parallel", "parallel"),
    )(chip_arr, shard)


HBM = pl.BlockSpec(memory_space=pltpu.HBM)
SEM = pl.BlockSpec(memory_space=pltpu.SEMAPHORE)
EFFECT = pltpu.SideEffectType.DATAFLOW_SIDE_EFFECTING

GATHER_GROUPS = (((0, 0),), ((1, 0), (2, 0), (3, 0)), ((0, 1),), ((1, 1), (2, 1), (3, 1)))
GATHER_STARTS = ((0,), (1,), (2, 3))


def _gather_copies(arrs, group, send_sems, recv_sems):
    x, y, c = _mesh_pos()
    me = 2 * x + y
    out = []
    for i, (w, layer) in enumerate(group):
        mine = arrs[w].at[layer, me]
        for j, (qx, qy) in enumerate([(1 - x, y), (x, 1 - y), (1 - x, 1 - y)]):
            landed = arrs[w].at[layer, 2 * qx + qy]
            out.append(tuple(pltpu.make_async_remote_copy(
                src_ref=piece, dst_ref=piece, send_sem=send_sems.at[i * 3 + j], recv_sem=recv_sems.at[i * 3 + j],
                device_id=(qx, qy, c), device_id_type=MESH) for piece in (mine, landed)))
    return out


def _conv_copies(conv_src, conv_dst, send_sems, recv_sems):
    x, y, c = _mesh_pos()
    out = []
    for j, (qx, qy) in enumerate([(1 - x, y), (x, 1 - y), (1 - x, 1 - y)]):
        out.append(tuple(pltpu.make_async_remote_copy(
            src_ref=conv_src, dst_ref=conv_dst.at[q], send_sem=send_sems.at[j], recv_sem=recv_sems.at[j],
            device_id=(qx, qy, c), device_id_type=MESH) for q in (2 * x + y, 2 * qx + qy)))
    return out


def _gather_start(groups, arrs, conv, name, through=None):
    n_sems = 2 * (len(groups) + (conv is not None))
    mats = sorted({w for g in groups for w, _ in GATHER_GROUPS[g]})

    def body(*refs):
        arrs_ref = [None] * N_BIG
        for w, ref in zip(mats, refs):
            arrs_ref[w] = ref
        sems = refs[n_in:n_in + n_sems]
        if conv is not None:
            for cp, _ in _conv_copies(refs[len(mats)], refs[len(mats) + 1], sems[-2], sems[-1]):
                cp.start()
        for k, g in enumerate(groups):
            for cp, _ in _gather_copies(arrs_ref, GATHER_GROUPS[g], sems[2 * k], sems[2 * k + 1]):
                cp.start()

    sem_shapes = []
    for n in [len(GATHER_GROUPS[g]) for g in groups] + ([1] if conv is not None else []):
        sem_shapes += [pltpu.SemaphoreType.DMA((3 * n,))] * 2
    operands = [arrs[w] for w in mats] + ([] if conv is None else list(conv)) + ([] if through is None else [through])
    n_in = len(operands)
    res = pl.pallas_call(
        body, name=name,
        out_shape=tuple(sem_shapes) + tuple(pltpu.HBM(a.shape, a.dtype) for a in operands),
        in_specs=(HBM,) * n_in, out_specs=(SEM,) * n_sems + (HBM,) * n_in,
        input_output_aliases={i: n_sems + i for i in range(n_in)},
        compiler_params=pltpu.CompilerParams(has_side_effects=EFFECT),
    )(*[pltpu.with_memory_space_constraint(a, pltpu.HBM) for a in operands])
    arrs = list(arrs)
    for w, a in zip(mats, res[n_sems:]):
        arrs[w] = a
    return res[:n_sems], arrs, list(res[n_sems + len(mats):])


def _gather_wait(k, sems, arrs, conv, after, name):
    group = GATHER_GROUPS[k]
    mats = sorted({w for w, _ in group})
    n_conv = 0 if conv is None else 2

    def body(*refs):
        local = refs[:len(mats)]
        arrs_ref = [None] * N_BIG
        for w, ref in zip(mats, local):
            arrs_ref[w] = ref
        pos = len(mats) + n_conv
        copies = _gather_copies(arrs_ref, group, refs[pos], refs[pos + 1])
        if conv is not None:
            copies += _conv_copies(refs[len(mats)], refs[len(mats) + 1], refs[pos + 2], refs[pos + 3])
        for send, recv in copies:
            recv.wait_recv()
            send.wait_send()

    operands = [arrs[w] for w in mats] + ([] if conv is None else [conv[1], conv[2]])
    sem_ops = list(sems) + ([] if conv is None else list(conv[0]))
    n_op = len(operands)
    res = pl.pallas_call(
        body, name=name, out_shape=tuple(pltpu.HBM(a.shape, a.dtype) for a in operands),
        in_specs=(HBM,) * n_op + (SEM,) * len(sem_ops) + (ANY,) * len(after), out_specs=(HBM,) * n_op,
        input_output_aliases={i: i for i in range(n_op)},
        compiler_params=pltpu.CompilerParams(has_side_effects=EFFECT),
    )(*operands, *sem_ops, *after)
    arrs = list(arrs)
    for w, a in zip(mats, res):
        arrs[w] = a
    return arrs, (res[-1] if conv is not None else None)


def _grad_shard(ref, w, chip, n):
    start = pl.multiple_of(chip * n, 128)
    if SHARD_AXES[w] == 2:
        return ref.at[:, pl.ds(start, n)]
    return ref.at[pl.ds(start, n), :]


def _slot_shape(g, w):
    shape = list(g.shape)
    shape[SHARD_AXES[w] - 1] //= N_CHIPS
    return (N_DEV - 1,) + tuple(shape)


def _scatter_copies(g_ref, land_ref, send_sems, recv_sems, layer, w):
    x, y, c = _mesh_pos()
    n = g_ref.shape[SHARD_AXES[w] - 1] // N_CHIPS
    out = []
    for r in range(1, N_DEV):
        tx, ty, tc = _flip(x, r & 4), _flip(y, r & 2), _flip(c, r & 1)
        cp = pltpu.make_async_remote_copy(
            src_ref=_grad_shard(g_ref, w, 2 * tx + ty, n), dst_ref=land_ref.at[r - 1], send_sem=send_sems.at[r - 1],
            recv_sem=recv_sems.at[r - 1], device_id=(tx, ty, tc), device_id_type=MESH)
        out.append((cp, (c != layer) if r & 1 else (c == layer)))
    return out


def _scatter_start(g, land, layer, w, name):
    def body(g_ref, land_ref, send_sems, recv_sems, g_thru, land_thru, token):
        for cp, mine in _scatter_copies(g_ref, land_ref, send_sems, recv_sems, layer, w):
            @pl.when(mine)
            def _():
                cp.start()
        token[...] = jnp.zeros_like(token)

    return pl.pallas_call(
        body, name=name,
        out_shape=(pltpu.SemaphoreType.DMA((N_DEV - 1,)), pltpu.SemaphoreType.DMA((N_DEV - 1,)),
                   pltpu.HBM(g.shape, g.dtype), pltpu.HBM(land.shape, land.dtype),
                   jax.ShapeDtypeStruct((HALO, 128), F32)),
        in_specs=(HBM, HBM), out_specs=(SEM, SEM, HBM, HBM, pl.BlockSpec(memory_space=pltpu.VMEM)),
        input_output_aliases={0: 2, 1: 3}, compiler_params=pltpu.CompilerParams(has_side_effects=EFFECT),
    )(pltpu.with_memory_space_constraint(g, pltpu.HBM), pltpu.with_memory_space_constraint(land, pltpu.HBM))


def _scatter_wait(started, land, after, w, name):
    def body(g0_ref, g1_ref, land_ref, ss0, rs0, ss1, rs1, after_ref, g0_out, g1_out, land_out):
        c = lax.axis_index("c")
        for layer, g_ref, ss, rs in ((0, g0_ref, ss0, rs0), (1, g1_ref, ss1, rs1)):
            for cp, mine in _scatter_copies(g_ref, land_ref, ss, rs, layer, w):
                @pl.when(mine)
                def _():
                    cp.wait_send()

                @pl.when(c == layer)
                def _():
                    cp.wait_recv()

    (ss0, rs0, g0), (ss1, rs1, g1) = started
    return pl.pallas_call(
        body, name=name,
        out_shape=(pltpu.HBM(g0.shape, g0.dtype), pltpu.HBM(g1.shape, g1.dtype), pltpu.HBM(land.shape, land.dtype)),
        in_specs=(HBM, HBM, HBM, SEM, SEM, SEM, SEM, ANY), out_specs=(HBM, HBM, HBM),
        input_output_aliases={0: 0, 1: 1, 2: 2}, compiler_params=pltpu.CompilerParams(has_side_effects=EFFECT),
    )(g0, g1, land, ss0, rs0, ss1, rs1, after)


def _sum_slots(g0, g1, slots, w, pos_arr, name):
    _, rows, cols = slots.shape
    tr = min(256, rows)
    nr = rows // tr
    if SHARD_AXES[w] == 2:
        own = pl.BlockSpec((tr, cols), lambda i, pos: (i, pos[0]))
    else:
        own = pl.BlockSpec((tr, cols), lambda i, pos: (pos[0] * nr + i, 0))

    def body(pos_ref, own0_ref, own1_ref, s_ref, o_ref):
        acc = jnp.where(pos_ref[1] == 0, own0_ref[...], own1_ref[...]).astype(F32)
        for r in range(N_DEV - 1):
            acc = acc + s_ref[r].astype(F32)
        o_ref[...] = acc

    return pl.pallas_call(
        body, name=name,
        grid_spec=pltpu.PrefetchScalarGridSpec(
            num_scalar_prefetch=1, grid=(nr,),
            in_specs=[own, own, pl.BlockSpec((N_DEV - 1, tr, cols), lambda i, pos: (0, i, 0))],
            out_specs=pl.BlockSpec((tr, cols), lambda i, pos: (i, 0))),
        out_shape=jax.ShapeDtypeStruct((rows, cols), F32), compiler_params=_cparams("parallel"),
    )(pos_arr, g0, g1, slots)


def _swap_layers(halves, name):
    n = len(halves)

    def body(*refs):
        srcs, dsts = refs[:n], refs[n:2 * n]
        send_sems, recv_sems = refs[2 * n:]
        x, y, c = _mesh_pos()
        sends = [pltpu.make_async_remote_copy(src_ref=srcs[w], dst_ref=dsts[w], send_sem=send_sems.at[w],
                                              recv_sem=recv_sems.at[w], device_id=(x, y, 1 - c), device_id_type=MESH)
                 for w in range(n)]
        for cp in sends:
            cp.start()
        for cp in sends:
            cp.wait_recv()
        for cp in sends:
            cp.wait_send()

    return pl.pallas_call(
        body, name=name, in_specs=[ANY] * n, out_specs=[ANY] * n,
        out_shape=[jax.ShapeDtypeStruct(h.shape, h.dtype) for h in halves],
        scratch_shapes=[pltpu.SemaphoreType.DMA((n,)), pltpu.SemaphoreType.DMA((n,))],
    )(*halves)


def _adamw_math(w, g, m, v):
    m = ADAM_B1 * m + (1.0 - ADAM_B1) * g
    v = ADAM_B2 * v + (1.0 - ADAM_B2) * jnp.square(g)
    m_hat = m / (1.0 - ADAM_B1 ** ADAM_STEP)
    v_hat = v / (1.0 - ADAM_B2 ** ADAM_STEP)
    delta = -ADAM_LR * (m_hat / (jnp.sqrt(v_hat) + ADAM_EPS) + ADAM_WD * w)
    return delta, m, v


def _adamw(w, g_own, g_other, m, v, pos_arr, name):
    shape = w.shape
    _, rows, cols = shape
    tr = min(256, rows)

    def body(pos_ref, w_ref, own_ref, other_ref, m_ref, v_ref, g_ref, d_ref, m2_ref, v2_ref):
        g = jnp.where(pl.program_id(0) == pos_ref[1], own_ref[...], other_ref[...])
        g_ref[...] = g
        d_ref[...], m2_ref[...], v2_ref[...] = _adamw_math(w_ref[...], g, m_ref[...], v_ref[...])

    full = pl.BlockSpec((None, tr, cols), lambda l, i, pos: (l, i, 0))
    half = pl.BlockSpec((tr, cols), lambda l, i, pos: (i, 0))
    return pl.pallas_call(
        body, name=name,
        grid_spec=pltpu.PrefetchScalarGridSpec(
            num_scalar_prefetch=1, grid=(2, rows // tr),
            in_specs=[full, half, half, full, full], out_specs=[full] * 4),
        out_shape=[jax.ShapeDtypeStruct(shape, F32)] * 4, compiler_params=_cparams("parallel", "parallel"),
    )(pos_arr, w, g_own, g_other, m, v)


def _small_sync(part, w, m, v):
    rows, cols = part.shape

    def body(p_ref, w_ref, m_ref, v_ref, g_ref, d_ref, m2_ref, v2_ref, slots, send_sems, recv_sems):
        x, y, c = _mesh_pos()
        me = 4 * x + 2 * y + c
        slots[me] = p_ref[...]
        sends = []
        for r in range(1, N_DEV):
            to = (_flip(x, r & 4), _flip(y, r & 2), _flip(c, r & 1))
            sends.append(pltpu.make_async_remote_copy(
                src_ref=p_ref, dst_ref=slots.at[me], send_sem=send_sems.at[r - 1], recv_sem=recv_sems.at[r - 1],
                device_id=to, device_id_type=MESH))
        for cp in sends:
            cp.start()
        for cp in sends:
            cp.wait_recv()
        for cp in sends:
            cp.wait_send()
        g = slots[0]
        for i in range(1, N_DEV):
            g = g + slots[i]
        g_ref[...] = g
        d_ref[...], m2_ref[...], v2_ref[...] = _adamw_math(w_ref[...], g, m_ref[...], v_ref[...])

    vm = pl.BlockSpec(memory_space=pltpu.VMEM)
    return pl.pallas_call(
        body, name="small_sync", in_specs=[vm] * 4, out_specs=[vm] * 4,
        out_shape=[jax.ShapeDtypeStruct((rows, cols), F32)] * 4,
        scratch_shapes=[pltpu.VMEM((N_DEV, rows, cols), F32), pltpu.SemaphoreType.DMA((N_DEV - 1,)),
                        pltpu.SemaphoreType.DMA((N_DEV - 1,))],
    )(part, w, m, v)


PACK_W = 256


def _pack_rows(n):
    return -(-n // (HALO * PACK_W)) * HALO


def _pack_small(parts):
    out = []
    for a in parts:
        flat = a.reshape(-1)
        out.append(jnp.pad(flat, (0, _pack_rows(flat.size) * PACK_W - flat.size)).reshape(-1, PACK_W))
    return jnp.concatenate(out, axis=0)


def _unpack_small(p, shapes):
    out, row = [], 0
    for shape in shapes:
        n = 1
        for k in shape:
            n *= k
        out.append(p[row:row + _pack_rows(n)].reshape(-1)[:n].reshape(shape))
        row += _pack_rows(n)
    return out


def kernel(x, w_in, conv_w, sinks, g_mix, g_group, w_o, g_mlp, w_ff_in, w_ff_out, g_final, loss_target, m_w_in, m_conv_w, m_sinks, m_g_mix, m_g_group, m_w_o, m_g_mlp, m_w_ff_in, m_w_ff_out, m_g_final, v_w_in, v_conv_w, v_sinks, v_g_mix, v_g_group, v_w_o, v_g_mlp, v_w_ff_in, v_w_ff_out, v_g_final):
    chip = 2 * lax.axis_index("x") + lax.axis_index("y")
    conv_n = conv_w.shape[2]

    pos_arr = jnp.stack([chip, lax.axis_index("c")]).astype(jnp.int32)
    shards = (w_in, w_o, w_ff_in, w_ff_out)
    conv_tile = jnp.pad(conv_w.reshape(6, conv_n), ((0, HALO - 6), (0, 128 - conv_n)))
    placed = [_place_shard(w_in, pos_arr[:1], "place_shard_0"), None, None, None]
    sems_a, placed, conv_thru = _gather_start(
        GATHER_STARTS[0], placed, (conv_tile, lax.empty((N_CHIPS,) + conv_tile.shape, conv_tile.dtype)),
        "gather_start_0")
    for i in range(1, N_BIG):
        placed[i] = _place_shard(shards[i], pos_arr[:1], "place_shard_%d" % i)
    full = {"arrs": placed, "conv": None, "sems": list(sems_a[:2])}
    target = _to_strips(loss_target[0], placed[:1], "to_strips_target")

    def fetch(stage, layer, after):
        k = 2 * layer + stage
        sems = full["sems"][2 * k:2 * k + 2]
        if k == 0:
            full["arrs"], land = _gather_wait(0, sems, full["arrs"], (sems_a[-2:], *conv_thru), (after, target),
                                              "gather_wait_0")
            conv_all = lax.dynamic_update_slice(land, conv_tile[None], (chip, 0, 0))
            full["conv"] = conv_all[:, :6, :conv_n].reshape(N_CHIPS, 2, 3, conv_n).transpose(1, 2, 0, 3).reshape(
                2, 3, CONV_CH)
            sems_b, full["arrs"], rest = _gather_start(GATHER_STARTS[1], full["arrs"], None, "gather_start_1",
                                                       through=full["arrs"][0])
            full["arrs"][0] = rest[-1]
            full["sems"] += list(sems_b)
        else:
            full["arrs"], _ = _gather_wait(k, sems, full["arrs"], None, (after,), "gather_wait_%d" % k)
        if k == 1:
            sems_c, full["arrs"], _ = _gather_start(GATHER_STARTS[2], full["arrs"], None, "gather_start_2")
            full["sems"] += list(sems_c)
        return (*full["arrs"], full["conv"])

    lands, started = [None] * N_BIG, {}

    def emit(layer, w, g):
        if lands[w] is None:
            lands[w] = lax.empty(_slot_shape(g, w), g.dtype)
        *started[layer, w], lands[w], token = _scatter_start(g, lands[w], layer, w, "scatter_start_%d_%d" % (layer, w))
        return token[0, 0]

    loss_tile, dx, grads, dg_final = _local_step(_to_strips(x[0], placed, "to_strips_x"), target, fetch,
                                                 w_ff_in.shape[2] * N_CHIPS,
                                                 sinks, g_mix, g_group, g_mlp, g_final, emit)

    wmv = ((w_in, m_w_in, v_w_in), (w_o, m_w_o, v_w_o), (w_ff_in, m_w_ff_in, v_w_ff_in),
           (w_ff_out, m_w_ff_out, v_w_ff_out))
    big, after = [None] * N_BIG, dx
    for name, ws in (("swap_layers_rest", (1, 2, 3)), ("swap_layers_in", (0,))):
        own = []
        for w in ws:
            g0, g1, slots = _scatter_wait((started[0, w], started[1, w]), lands[w], after, w, "scatter_wait_%d" % w)
            own.append(_sum_slots(g0, g1, slots, w, pos_arr, "sum_slots_%d" % w))
        for w, mine, theirs in zip(ws, own, _swap_layers(own, name)):
            big[w] = _adamw(wmv[w][0], mine, theirs, wmv[w][1], wmv[w][2], pos_arr, "adamw_%d" % w)
        after = big[ws[-1]][1]

    def both(i):
        return jnp.stack([grads[0][i][0], grads[1][i][0]])
    dconv = jnp.stack([grads[0][0][:3], grads[1][0][:3]])
    dsinks = jnp.stack([grads[0][1][0, ::HEAD_DIM], grads[1][1][0, ::HEAD_DIM]])
    part = _pack_small([both(2), both(3), both(4), dg_final[0], dconv, dsinks, loss_tile[0, 0]])

    def spread(shard):
        return lax.dynamic_update_slice(jnp.zeros((2, 3, CONV_CH), F32), shard, (0, 0, chip * conv_n))
    zero = jnp.zeros((), F32)
    packs = [_pack_small([a, b, c_, e, spread(f), g_, zero]) for a, b, c_, e, f, g_ in (
        (g_mix, g_group, g_mlp, g_final, conv_w, sinks),
        (m_g_mix, m_g_group, m_g_mlp, m_g_final, m_conv_w, m_sinks),
        (v_g_mix, v_g_group, v_g_mlp, v_g_final, v_conv_w, v_sinks))]
    shapes = [g_mix.shape, g_group.shape, g_mlp.shape, g_final.shape, (2, 3, CONV_CH), sinks.shape, ()]
    small = [_unpack_small(p, shapes) for p in _small_sync(part, *packs)]

    def shard_of(full):
        return lax.dynamic_slice(full, (0, 0, chip * conv_n), (2, 3, conv_n))
    small = [(s[0], s[1], s[2], s[3], shard_of(s[4]), s[5], s[6]) for s in small]
    loss = small[0][6]

    def ordered(kind):
        b = [big[i][kind] for i in range(N_BIG)]
        s = small[kind]
        return [b[0], s[4], s[5], s[0], s[1], b[1], s[2], b[2], b[3], s[3]]

    return (loss, dx[None], *ordered(0), *ordered(1), *ordered(2), *ordered(3))
```

```python
import functools

import jax
import jax.numpy as jnp
from jax import lax
from jax.experimental import pallas as pl
from jax.experimental.pallas import tpu as pltpu

HEAD_DIM = 64
N_HEADS = 6
C_GROUP = 3
A_WIDTH = N_HEADS * HEAD_DIM
C_KV_WIDTH = 2 * HEAD_DIM
CONV_CH = 256
ZA_W = 3 * A_WIDTH
ZB_W = 3 * CONV_CH
ZC_W = A_WIDTH + 2 * C_KV_WIDTH
IN_WIDTH = ZA_W + ZB_W + ZC_W
MIX_WIDTH = A_WIDTH + CONV_CH + A_WIDTH
DILATIONS = (1, 4, 16)
A_MAX_DIST = 128
C_MAX_DIST = 127
TQ = 128
EPS = 1e-6
SCALE = HEAD_DIM ** -0.5
NEG = -1e30
HALO = 8

ADAM_LR = 0.001
ADAM_B1 = 0.9
ADAM_B2 = 0.999
ADAM_EPS = 1e-08
ADAM_WD = 0.01
ADAM_STEP = 10

BF = jnp.bfloat16
F32 = jnp.float32
MESH = pl.DeviceIdType.MESH
VMEM_LIMIT = 56 * 1024 * 1024


def _cparams(*sem):
    return pltpu.CompilerParams(dimension_semantics=sem, vmem_limit_bytes=VMEM_LIMIT)


def _nt(a, b):
    return lax.dot_general(a, b, (((1,), (1,)), ((), ())), preferred_element_type=F32)


def _tn(a, b):
    return lax.dot_general(a, b, (((0,), (0,)), ((), ())), preferred_element_type=F32)


def _nn(a, b):
    return jnp.dot(a, b, preferred_element_type=F32)


def _rows(tb, w):
    return pl.BlockSpec((tb, w), lambda i: (i, 0))


def _whole(shape):
    return pl.BlockSpec(shape, lambda *_: (0,) * len(shape))


def _layer(shape, l):
    return pl.BlockSpec((None,) + shape, lambda *_: (l,) + (0,) * len(shape))


def _rms_scale(v):
    return lax.rsqrt(jnp.mean(v * v, axis=-1, keepdims=True) + EPS)


def _norm_bwd(dxhat, xhat, r):
    return r * (dxhat - xhat * jnp.mean(dxhat * xhat, axis=-1, keepdims=True))


def _qkv_fwd(x, g, w_all, l, tb):
    s, d = x.shape

    def body(x_ref, g_ref, w_ref, h_ref, za_ref, zb_ref, zc_ref):
        xv = x_ref[...]
        h = ((xv * _rms_scale(xv)) * g_ref[...]).astype(BF)
        h_ref[...] = h
        z = jnp.concatenate([_nn(h, w_ref[k]) for k in range(N_CHIPS)], axis=1)
        za_ref[...] = z[:, :ZA_W]
        zb_ref[...] = z[:, ZA_W:ZA_W + ZB_W]
        zc_ref[...] = z[:, ZA_W + ZB_W:]

    return pl.pallas_call(
        body, grid=(s // tb,), name="qkv_fwd",
        in_specs=[_rows(tb, d), _whole((1, d)), _layer((N_CHIPS, d, IN_WIDTH // N_CHIPS), l)],
        out_specs=[_rows(tb, d), _rows(tb, ZA_W), _rows(tb, ZB_W), _rows(tb, ZC_W)],
        out_shape=[jax.ShapeDtypeStruct((s, d), BF), jax.ShapeDtypeStruct((s, ZA_W), F32),
                   jax.ShapeDtypeStruct((s, ZB_W), F32), jax.ShapeDtypeStruct((s, ZC_W), F32)],
        compiler_params=_cparams("parallel"),
    )(x, g, w_all)


N_STRIPS = 16


def _strips(a):
    s, w = a.shape
    return a.reshape(4, 4, s // N_STRIPS, w)


P_ROWS = {16: TQ, 4: 32, 1: 8}


def _p_sub(s, dil):
    return 2 if (s // dil // TQ) % 2 == 0 else 1


def _p_grid(s, dil, n_sub):
    nb = s // dil // TQ // n_sub
    return {16: (4, 4, nb), 4: (4, nb), 1: (nb,)}[dil]


def _p_spec(dil, cw, col, n_sub, prev=False):
    rows = P_ROWS[dil] * (1 if prev else n_sub)

    def blk(j):
        return jnp.maximum(n_sub * j - 1, 0) if prev else j
    if dil == 16:
        return pl.BlockSpec((None, None, rows, cw), lambda f, e, j: (f, e, blk(j), col))
    if dil == 4:
        return pl.BlockSpec((None, 4, rows, cw), lambda f, j: (f, 0, blk(j), col))
    return pl.BlockSpec((4, 4, rows, cw), lambda j: (0, 0, blk(j), col))


def _block_pos(i, dil):
    if dil == 16:
        return i
    if dil == 4:
        return 4 * (i % 32) + i // 32
    return 16 * (i % 8) + 4 * ((i // 8) % 4) + i // 32


def _band_mask(b, dil, max_dist):
    qi = _block_pos(lax.broadcasted_iota(jnp.int32, (TQ, 2 * TQ), 0), dil)
    col = lax.broadcasted_iota(jnp.int32, (TQ, 2 * TQ), 1)
    cur = col >= TQ
    dist = qi - _block_pos(col % TQ, dil) + jnp.where(cur, 0, TQ)
    return (dist >= 0) & (dist <= max_dist) & (cur | (b > 0))


def _hs(h):
    return slice(h * HEAD_DIM, (h + 1) * HEAD_DIM)


def _ld(ref, cols, rows=slice(None)):
    v = ref[..., rows, cols]
    return v.reshape(TQ, v.shape[-1])


def _st(ref, cols, val, rows=slice(None)):
    lead = ref.shape[:-2] + (ref.shape[-2] if rows == slice(None) else rows.stop - rows.start,)
    ref[..., rows, cols] = val.reshape(lead + (val.shape[-1],))


def _attn_fwd(z, dil, kw, kcol, vcol, n_rep, max_dist, name):
    s, zw = z.shape
    n_sub = _p_sub(s, dil)
    grid = _p_grid(s, dil, n_sub)

    def body(q_ref, kp_ref, kc_ref, vp_ref, vc_ref, o_ref, lse_ref):
        for t in range(n_sub):
            rows = slice(t * P_ROWS[dil], (t + 1) * P_ROWS[dil])
            before = (slice(None),) if t == 0 else (slice((t - 1) * P_ROWS[dil], t * P_ROWS[dil]),)
            kb_ref, vb_ref = (kp_ref, vp_ref) if t == 0 else (kc_ref, vc_ref)
            mask = _band_mask(n_sub * pl.program_id(len(grid) - 1) if t == 0 else 1, dil, max_dist)
            scs, v2s = [], []
            for kh in range(N_HEADS // n_rep):
                k2 = jnp.concatenate([_ld(kb_ref, _hs(kh), *before), _ld(kc_ref, _hs(kh), rows)], axis=0).astype(BF)
                v2s.append(jnp.concatenate([_ld(vb_ref, _hs(kh), *before), _ld(vc_ref, _hs(kh), rows)],
                                           axis=0).astype(BF))
                for h in range(kh * n_rep, (kh + 1) * n_rep):
                    q = (_ld(q_ref, _hs(h), rows) * SCALE).astype(BF)
                    scs.append(jnp.where(mask, _nt(q, k2), NEG))
            for h, sc in enumerate(scs):
                m = jnp.max(sc, axis=1, keepdims=True)
                p = jnp.exp(sc - m)
                l = jnp.sum(p, axis=1, keepdims=True)
                _st(o_ref, _hs(h), _nn(p.astype(BF), v2s[h // n_rep]) / l, rows)
                _st(lse_ref, _hs(h), jnp.broadcast_to(m + jnp.log(l), (TQ, HEAD_DIM)), rows)

    res = pl.pallas_call(
        body, grid=grid, name=name,
        in_specs=[_p_spec(dil, A_WIDTH, 0, n_sub), _p_spec(dil, kw, kcol, n_sub, True), _p_spec(dil, kw, kcol, n_sub),
                  _p_spec(dil, kw, vcol, n_sub, True), _p_spec(dil, kw, vcol, n_sub)],
        out_specs=[_p_spec(dil, A_WIDTH, 0, n_sub)] * 2,
        out_shape=[jax.ShapeDtypeStruct((4, 4, s // N_STRIPS, A_WIDTH), F32)] * 2,
        compiler_params=_cparams(*(("parallel",) * len(grid))),
    )(*[_strips(z)] * 5)
    return [a.reshape(s, A_WIDTH) for a in res]


def _attn_merge(parts_a, part_c, sink_row, tb):
    s = part_c[0].shape[0]
    n_a = len(parts_a)

    def body(*refs):
        ins, sink_ref = refs[:2 * n_a + 2], refs[2 * n_a + 2]
        ya_ref, lsea_ref, yc_ref, lsec_ref = refs[2 * n_a + 3:]
        lses = [ins[2 * p + 1][...] for p in range(n_a)]
        m = functools.reduce(jnp.maximum, lses)
        ws = [jnp.exp(v - m) for v in lses]
        l = functools.reduce(jnp.add, ws)
        ya_ref[...] = functools.reduce(jnp.add, [w * ins[2 * p][...] for p, w in enumerate(ws)]) / l
        lsea_ref[...] = m + jnp.log(l)
        o_c, lse_c = [r[...] for r in ins[2 * n_a:]]
        sk = sink_ref[...]
        m2 = jnp.maximum(lse_c, sk)
        w = jnp.exp(lse_c - m2)
        l2 = w + jnp.exp(sk - m2)
        yc_ref[...] = o_c * (w / l2)
        lsec_ref[...] = m2 + jnp.log(l2)

    return pl.pallas_call(
        body, grid=(s // tb,), name="attn_merge",
        in_specs=[_rows(tb, A_WIDTH)] * (2 * n_a + 2) + [_whole((1, A_WIDTH))],
        out_specs=[_rows(tb, A_WIDTH)] * 4, out_shape=[jax.ShapeDtypeStruct((s, A_WIDTH), F32)] * 4,
        compiler_params=_cparams("parallel"),
    )(*[a for part in parts_a + [part_c] for a in part], sink_row)


def _shift_down(v, n, halo):
    rows = v.shape[0]
    out = pltpu.roll(v, n, 0)
    row = lax.broadcasted_iota(jnp.int32, v.shape, 0)
    for t in range(n):
        out = jnp.where(row == t, halo[HALO - n + t:HALO - n + t + 1, :], out)
    return out


def _shift_up(v, n, halo):
    rows = v.shape[0]
    out = pltpu.roll(v, rows - n, 0)
    row = lax.broadcasted_iota(jnp.int32, v.shape, 0)
    for t in range(n):
        out = jnp.where(row == rows - n + t, halo[t:t + 1, :], out)
    return out


def _strip(v, b):
    return v[b % 4, b // 4]


def _conv_strips(zb, prev, cw):
    gb = [_strip(zb, b)[:, :CONV_CH] for b in range(N_STRIPS)]
    gc = [_strip(zb, b)[:, CONV_CH:2 * CONV_CH] for b in range(N_STRIPS)]
    xb = [_strip(zb, b)[:, 2 * CONV_CH:] for b in range(N_STRIPS)]
    u = [g * v for g, v in zip(gc, xb)]
    uh = prev[:, :, CONV_CH:2 * CONV_CH] * prev[:, :, 2 * CONV_CH:]
    wrapped = {14: _shift_down(u[14], 1, uh[2]), 15: _shift_down(u[15], 1, uh[3])}
    u1 = [u[b - 1] if b >= 1 else wrapped[15] for b in range(N_STRIPS)]
    u2 = [u[b - 2] if b >= 2 else wrapped[14 + b] for b in range(N_STRIPS)]
    c = [cw[0:1, :] * u2[b] + cw[1:2, :] * u1[b] + cw[2:3, :] * u[b] for b in range(N_STRIPS)]
    return gb, gc, xb, u, u1, u2, c


def _strip_rows(ta, w):
    return pl.BlockSpec((4, 4, ta, w), lambda i: (0, 0, i, 0))


def _prev_rows(ta, w):
    return pl.BlockSpec((4, None, HALO, w), lambda i: (0, 3, jnp.maximum(i * (ta // HALO) - 1, 0), 0))


def _next_rows(ta, w, nblk):
    return pl.BlockSpec((4, None, HALO, w),
                        lambda i: (0, 0, jnp.minimum((i + 1) * (ta // HALO), nblk * (ta // HALO) - 1), 0))


def _mix_fwd(x, ya, yc, zb, cw, gg, wo_all, l, tb):
    s, d = x.shape
    ta = tb // N_STRIPS

    def body(x_ref, ya_ref, yc_ref, zb_ref, zbp_ref, cw_ref, gg_ref, wo_ref, x1_ref, yb_ref):
        i = pl.program_id(0)
        prev = jnp.where(i > 0, zbp_ref[...], 0.0)
        gb, _, _, _, _, _, c = _conv_strips(zb_ref[...], prev, cw_ref[...])
        for b in range(N_STRIPS):
            yb_ref[b % 4, b // 4] = gb[b] * c[b]
        yb = yb_ref[...].reshape(tb, CONV_CH)
        ya, yc = ya_ref[...].reshape(tb, A_WIDTH), yc_ref[...].reshape(tb, A_WIDTH)
        n = jnp.concatenate([ya * _rms_scale(ya), yb * _rms_scale(yb), yc * _rms_scale(yc)], axis=1)
        n = (n * gg_ref[...]).astype(BF)
        x1 = x_ref[...].reshape(tb, d) + _nn(n, wo_ref[...].reshape(MIX_WIDTH, d))
        x1_ref[...] = x1.reshape(4, 4, ta, d)

    res = pl.pallas_call(
        body, grid=(s // tb,), name="mix_fwd",
        in_specs=[_strip_rows(ta, d), _strip_rows(ta, A_WIDTH), _strip_rows(ta, A_WIDTH), _strip_rows(ta, ZB_W),
                  _prev_rows(ta, ZB_W), _whole((HALO, CONV_CH)), _whole((1, MIX_WIDTH)),
                  _layer((N_CHIPS, MIX_WIDTH // N_CHIPS, d), l)],
        out_specs=[_strip_rows(ta, d), _strip_rows(ta, CONV_CH)],
        out_shape=[jax.ShapeDtypeStruct((4, 4, s // N_STRIPS, d), F32),
                   jax.ShapeDtypeStruct((4, 4, s // N_STRIPS, CONV_CH), F32)],
        compiler_params=_cparams("parallel"),
    )(_strips(x), _strips(ya), _strips(yc), _strips(zb), _strips(zb), cw, gg, wo_all)
    return res[0].reshape(s, d), res[1].reshape(s, CONV_CH)


def _mlp_fwd(x1, g, w1_all, w2_all, l, tb, tf):
    s, d = x1.shape
    ff = w1_all.shape[1] * w1_all.shape[3]
    nj = ff // tf

    def body(x_ref, g_ref, w1_ref, w2_ref, x2_ref, h2_ref, ap_ref, acc):
        j = pl.program_id(1)

        @pl.when(j == 0)
        def _():
            xv = x_ref[...]
            h2_ref[...] = ((xv * _rms_scale(xv)) * g_ref[...]).astype(BF)
            acc[...] = jnp.zeros_like(acc)

        ap = _nn(h2_ref[...], w1_ref[...])
        ap_ref[...] = ap.astype(BF)
        a = jnp.square(jnp.maximum(ap, 0.0)).astype(BF)
        acc[...] += _nn(a, w2_ref[...])

        @pl.when(j == nj - 1)
        def _():
            x2_ref[...] = x_ref[...] + acc[...]

    return pl.pallas_call(
        body, grid=(s // tb, nj), name="mlp_fwd",
        in_specs=[pl.BlockSpec((tb, d), lambda i, j: (i, 0)), _whole((1, d)),
                  pl.BlockSpec((None, None, d, tf), lambda i, j: (l, j, 0, 0)),
                  pl.BlockSpec((None, None, tf, d), lambda i, j: (l, j, 0, 0))],
        out_specs=[pl.BlockSpec((tb, d), lambda i, j: (i, 0)), pl.BlockSpec((tb, d), lambda i, j: (i, 0)),
                   pl.BlockSpec((tb, tf), lambda i, j: (i, j))],
        out_shape=[jax.ShapeDtypeStruct((s, d), F32), jax.ShapeDtypeStruct((s, d), BF),
                   jax.ShapeDtypeStruct((s, ff), BF)],
        scratch_shapes=[pltpu.VMEM((tb, d), F32)],
        compiler_params=_cparams("parallel", "arbitrary"),
    )(x1, g, w1_all, w2_all)


def _loss_head(x, g, tgt, tb):
    s, d = x.shape

    def body(x_ref, g_ref, t_ref, dx_ref, loss_ref, dg_ref):
        i = pl.program_id(0)

        @pl.when(i == 0)
        def _():
            loss_ref[...] = jnp.zeros_like(loss_ref)
            dg_ref[...] = jnp.zeros_like(dg_ref)

        xv = x_ref[...]
        r = _rms_scale(xv)
        xhat = xv * r
        err = xhat * g_ref[...] - t_ref[...]
        part = jnp.sum(jnp.mean(jnp.square(err), axis=-1, keepdims=True), axis=0, keepdims=True)
        loss_ref[...] += 0.5 * part
        dy = err * (1.0 / d)
        dg_ref[...] += jnp.sum(dy * xhat, axis=0, keepdims=True)
        dx_ref[...] = _norm_bwd(dy * g_ref[...], xhat, r)

    return pl.pallas_call(
        body, grid=(s // tb,), name="loss_head",
        in_specs=[_rows(tb, d), _whole((1, d)), _rows(tb, d)],
        out_specs=[_rows(tb, d), _whole((HALO, 128)), _whole((HALO, d))],
        out_shape=[jax.ShapeDtypeStruct((s, d), F32), jax.ShapeDtypeStruct((HALO, 128), F32),
                   jax.ShapeDtypeStruct((HALO, d), F32)],
        compiler_params=_cparams("arbitrary"),
    )(x, g, tgt)


def _mlp_bwd(dx2, x1, ap, g, w1_all, w2_all, l, tb, tf):
    s, d = x1.shape
    ff = ap.shape[1]
    nj = ff // tf

    def body(dx2_ref, x1_ref, ap_ref, g_ref, w1_ref, w2_ref, dx1_ref, dap_ref, dg_ref, acc):
        i, j = pl.program_id(0), pl.program_id(1)

        @pl.when((i == 0) & (j == 0))
        def _():
            dg_ref[...] = jnp.zeros_like(dg_ref)

        @pl.when(j == 0)
        def _():
            acc[...] = jnp.zeros_like(acc)

        da = _nt(dx2_ref[...].astype(BF), w2_ref[...])
        dap = (da * (2.0 * jnp.maximum(ap_ref[...].astype(F32), 0.0))).astype(BF)
        dap_ref[...] = dap
        acc[...] += _nt(dap, w1_ref[...])

        @pl.when(j == nj - 1)
        def _():
            xv = x1_ref[...]
            r = _rms_scale(xv)
            xhat = xv * r
            dh = acc[...]
            dg_ref[...] += jnp.sum(dh * xhat, axis=0, keepdims=True)
            dx1_ref[...] = dx2_ref[...] + _norm_bwd(dh * g_ref[...], xhat, r)

    return pl.pallas_call(
        body, grid=(s // tb, nj), name="mlp_bwd",
        in_specs=[pl.BlockSpec((tb, d), lambda i, j: (i, 0)), pl.BlockSpec((tb, d), lambda i, j: (i, 0)),
                  pl.BlockSpec((tb, tf), lambda i, j: (i, j)),
                  _whole((1, d)), pl.BlockSpec((None, None, d, tf), lambda i, j: (l, j, 0, 0)),
                  pl.BlockSpec((None, None, tf, d), lambda i, j: (l, j, 0, 0))],
        out_specs=[pl.BlockSpec((tb, d), lambda i, j: (i, 0)), pl.BlockSpec((tb, tf), lambda i, j: (i, j)),
                   _whole((HALO, d))],
        out_shape=[jax.ShapeDtypeStruct((s, d), F32), jax.ShapeDtypeStruct((s, ff), BF),
                   jax.ShapeDtypeStruct((HALO, d), F32)],
        scratch_shapes=[pltpu.VMEM((tb, d), F32)],
        compiler_params=_cparams("arbitrary", "arbitrary"),
    )(dx2, x1, ap, g, w1_all, w2_all)


def _wgrad(a, b, tm, tn, ts, name, relu2=False):
    s, m = a.shape
    n = b.shape[1]
    ns = s // ts

    def body(a_ref, b_ref, o_ref, acc):
        k = pl.program_id(2)

        @pl.when(k == 0)
        def _():
            acc[...] = jnp.zeros_like(acc)

        av = a_ref[...]
        if relu2:
            av = jnp.square(jnp.maximum(av.astype(F32), 0.0)).astype(BF)
        acc[...] += _tn(av, b_ref[...].astype(BF))

        @pl.when(k == ns - 1)
        def _():
            o_ref[...] = acc[...].astype(BF)

    return pl.pallas_call(
        body, grid=(m // tm, n // tn, ns), name=name,
        in_specs=[pl.BlockSpec((ts, tm), lambda i, j, k: (k, i)), pl.BlockSpec((ts, tn), lambda i, j, k: (k, j))],
        out_specs=pl.BlockSpec((tm, tn), lambda i, j, k: (i, j)),
        out_shape=jax.ShapeDtypeStruct((m, n), BF),
        scratch_shapes=[pltpu.VMEM((tm, tn), F32)],
        compiler_params=_cparams("parallel", "parallel", "arbitrary"),
    )(a, b)


def _mix_bwd(dx1, ya, yb, yc, lse_c, sink_row, gg, wo_all, l, tb):
    s, d = dx1.shape

    def body(dx_ref, ya_ref, yb_ref, yc_ref, lse_ref, sink_ref, gg_ref, wo_ref,
             n_ref, dya_ref, dyc_ref, da_ref, dc_ref, dyb_ref, dg_ref, dsink_ref):
        i = pl.program_id(0)

        @pl.when(i == 0)
        def _():
            dg_ref[...] = jnp.zeros_like(dg_ref)
            dsink_ref[...] = jnp.zeros_like(dsink_ref)

        dn = _nt(dx_ref[...].astype(BF), wo_ref[...].reshape(MIX_WIDTH, d))
        ys = [ya_ref[...], yb_ref[...], yc_ref[...]]
        rs = [_rms_scale(v) for v in ys]
        nhat = jnp.concatenate([v * r for v, r in zip(ys, rs)], axis=1)
        gg = gg_ref[...]
        n_ref[...] = (nhat * gg).astype(BF)
        dg_ref[...] += jnp.sum(dn * nhat, axis=0, keepdims=True)
        dnh = dn * gg
        bounds = [(0, A_WIDTH), (A_WIDTH, A_WIDTH + CONV_CH), (A_WIDTH + CONV_CH, MIX_WIDTH)]
        dys = [_norm_bwd(dnh[:, lo:hi], nhat[:, lo:hi], r) for (lo, hi), r in zip(bounds, rs)]
        dyb_ref[...] = dys[1]
        for dy, y, dy_ref, dd_ref in ((dys[0], ys[0], dya_ref, da_ref), (dys[2], ys[2], dyc_ref, dc_ref)):
            dy_ref[...] = dy
            t = dy * y
            for h in range(N_HEADS):
                dd_ref[:, _hs(h)] = jnp.broadcast_to(jnp.sum(t[:, _hs(h)], axis=1, keepdims=True), (tb, HEAD_DIM))
        dsink_ref[...] -= jnp.sum(jnp.exp(sink_ref[...] - lse_ref[...]) * dc_ref[...], axis=0, keepdims=True)

    return pl.pallas_call(
        body, grid=(s // tb,), name="mix_bwd",
        in_specs=[_rows(tb, d), _rows(tb, A_WIDTH), _rows(tb, CONV_CH), _rows(tb, A_WIDTH), _rows(tb, A_WIDTH),
                  _whole((1, A_WIDTH)), _whole((1, MIX_WIDTH)), _layer((N_CHIPS, MIX_WIDTH // N_CHIPS, d), l)],
        out_specs=[_rows(tb, MIX_WIDTH), _rows(tb, A_WIDTH), _rows(tb, A_WIDTH), _rows(tb, A_WIDTH),
                   _rows(tb, A_WIDTH), _rows(tb, CONV_CH), _whole((HALO, MIX_WIDTH)), _whole((HALO, A_WIDTH))],
        out_shape=[jax.ShapeDtypeStruct((s, MIX_WIDTH), BF), jax.ShapeDtypeStruct((s, A_WIDTH), F32),
                   jax.ShapeDtypeStruct((s, A_WIDTH), F32), jax.ShapeDtypeStruct((s, A_WIDTH), F32),
                   jax.ShapeDtypeStruct((s, A_WIDTH), F32), jax.ShapeDtypeStruct((s, CONV_CH), F32),
                   jax.ShapeDtypeStruct((HALO, MIX_WIDTH), F32), jax.ShapeDtypeStruct((HALO, A_WIDTH), F32)],
        compiler_params=_cparams("arbitrary"),
    )(dx1, ya, yb, yc, lse_c, sink_row, gg, wo_all)


def _attn_bwd(z, dy, lse, dd, dil, kw, kcol, vcol, n_rep, max_dist, name):
    s, zw = z.shape
    n_sub = _p_sub(s, dil) if n_rep == 1 else 1
    grid = _p_grid(s, dil, n_sub)
    n_kv = N_HEADS // n_rep
    dt = F32 if dil == 1 else BF

    def body(q_ref, kp_ref, kc_ref, vp_ref, vc_ref, dy_ref, lse_ref, dd_ref, dq_ref, dkp_ref, dkc_ref, dvp_ref, dvc_ref):
        for t in range(n_sub):
            rows = slice(t * P_ROWS[dil], (t + 1) * P_ROWS[dil])
            before = (slice(None),) if t == 0 else (slice((t - 1) * P_ROWS[dil], t * P_ROWS[dil]),)
            kb_ref, vb_ref = (kp_ref, vp_ref) if t == 0 else (kc_ref, vc_ref)
            mask = _band_mask(n_sub * pl.program_id(len(grid) - 1) if t == 0 else 1, dil, max_dist)
            k2s, qs, dys, scs, dps = [], [], [], [], []
            for kh in range(n_kv):
                k2s.append(jnp.concatenate([_ld(kb_ref, _hs(kh), *before), _ld(kc_ref, _hs(kh), rows)],
                                           axis=0).astype(BF))
                v2 = jnp.concatenate([_ld(vb_ref, _hs(kh), *before), _ld(vc_ref, _hs(kh), rows)], axis=0).astype(BF)
                for h in range(kh * n_rep, (kh + 1) * n_rep):
                    qs.append((_ld(q_ref, _hs(h), rows) * SCALE).astype(BF))
                    dys.append(_ld(dy_ref, _hs(h), rows).astype(BF))
                    scs.append(jnp.where(mask, _nt(qs[h], k2s[kh]), NEG))
                    dps.append(_nt(dys[h], v2))
            for kh in range(n_kv):
                k2 = k2s[kh]
                dk2 = jnp.zeros((2 * TQ, HEAD_DIM), F32)
                dv2 = jnp.zeros((2 * TQ, HEAD_DIM), F32)
                for h in range(kh * n_rep, (kh + 1) * n_rep):
                    lse_h = _ld(lse_ref, slice(h * HEAD_DIM, h * HEAD_DIM + 1), rows)
                    dd_h = _ld(dd_ref, slice(h * HEAD_DIM, h * HEAD_DIM + 1), rows)
                    p = jnp.exp(scs[h] - lse_h)
                    ds = (p * (dps[h] - dd_h)).astype(BF)
                    _st(dq_ref, _hs(h), (_nn(ds, k2) * SCALE).astype(dt), rows)
                    dk2 = dk2 + _tn(ds, qs[h])
                    dv2 = dv2 + _tn(p.astype(BF), dys[h])
                _st(dkp_ref, _hs(kh), dk2[:TQ].astype(dt), rows)
                _st(dkc_ref, _hs(kh), dk2[TQ:].astype(dt), rows)
                _st(dvp_ref, _hs(kh), dv2[:TQ].astype(dt), rows)
                _st(dvc_ref, _hs(kh), dv2[TQ:].astype(dt), rows)

    args = [_strips(z)] * 5 + [_strips(a) for a in (dy, lse, dd)]
    pair = _p_spec(dil, A_WIDTH, 0, n_sub)
    in_specs = [pair, _p_spec(dil, kw, kcol, n_sub, True), _p_spec(dil, kw, kcol, n_sub),
                _p_spec(dil, kw, vcol, n_sub, True), _p_spec(dil, kw, vcol, n_sub)] + [pair] * 3
    out_specs = [pair] + [_p_spec(dil, kw, 0, n_sub)] * 4
    na = s // N_STRIPS
    out_shape = [jax.ShapeDtypeStruct((4, 4, na, A_WIDTH), dt)] + [jax.ShapeDtypeStruct((4, 4, na, kw), dt)] * 4
    res = pl.pallas_call(
        body, grid=grid, name=name, in_specs=in_specs, out_specs=out_specs, out_shape=out_shape,
        compiler_params=_cparams(*(("parallel",) * len(grid))),
    )(*args)
    return [res[0].reshape(s, A_WIDTH)] + [a.reshape(s, kw) for a in res[1:]]


DZ_TA = 16


def _dz_assemble(parts_a, parts_c, dyb, zb, cw):
    s = zb.shape[0]
    na = s // N_STRIPS
    nb = na // DZ_TA

    def ahead(w, k):
        return pl.BlockSpec((4, 4, DZ_TA, w), lambda i: (0, 0, jnp.minimum(i + k, nb - 1), 0))

    args, in_specs = [], []
    for dil, (dq, dkp, dkc, dvp, dvc) in zip(DILATIONS + (1,), parts_a + [parts_c]):
        w = dkp.shape[1]
        here = _strip_rows(DZ_TA, w)
        if dil == 1:
            args += [dq, dkp, dkp, dkc, dvp, dvp, dvc]
            in_specs += [_strip_rows(DZ_TA, A_WIDTH), here, ahead(w, 1), here, here, ahead(w, 1), here]
        else:
            k = 8 * dil // DZ_TA
            args += [dq, dkp, dkc, dvp, dvc]
            in_specs += [_strip_rows(DZ_TA, A_WIDTH), ahead(w, k), here, ahead(w, k), here]
    n_att = len(args)
    args = [_strips(a) for a in args] + [_strips(dyb), _strips(dyb), _strips(zb), _strips(zb), _strips(zb), cw]
    in_specs += [_strip_rows(DZ_TA, CONV_CH), _next_rows(DZ_TA, CONV_CH, nb), _strip_rows(DZ_TA, ZB_W),
                 _prev_rows(DZ_TA, ZB_W), _next_rows(DZ_TA, ZB_W, nb), _whole((HALO, CONV_CH))]

    def body(*refs):
        att = list(refs[:n_att])
        dyb_ref, dybn_ref, zb_ref, zbp_ref, zbn_ref, cw_ref, dz_ref, dcw_ref = refs[n_att:]
        i = pl.program_id(0)

        @pl.when(i == 0)
        def _():
            dcw_ref[...] = jnp.zeros_like(dcw_ref)

        def shifted(dil):
            if dil == 1:
                dq_r, kp0, kp1, dkc_r, vp0, vp1, dvc_r = [att.pop(0) for _ in range(7)]
                live = i + 1 < nb
                half = DZ_TA // 2
                dkp = jnp.concatenate([kp0[:, :, half:, :], jnp.where(live, kp1[:, :, :half, :], 0.0)], axis=2)
                dvp = jnp.concatenate([vp0[:, :, half:, :], jnp.where(live, vp1[:, :, :half, :], 0.0)], axis=2)
            else:
                dq_r, dkp_r, dkc_r, dvp_r, dvc_r = [att.pop(0) for _ in range(5)]
                live = i + 8 * dil // DZ_TA < nb
                dkp = jnp.where(live, dkp_r[...].astype(F32), 0.0)
                dvp = jnp.where(live, dvp_r[...].astype(F32), 0.0)
            return dq_r[...].astype(F32), dkc_r[...].astype(F32) + dkp, dvc_r[...].astype(F32) + dvp

        dq, dk, dv = shifted(DILATIONS[0])
        for dil in DILATIONS[1:]:
            dq2, dk2, dv2 = shifted(dil)
            dq, dk, dv = dq + dq2, dk + dk2, dv + dv2
        dz_ref[:, :, :, 0:A_WIDTH] = dq.astype(BF)
        dz_ref[:, :, :, A_WIDTH:2 * A_WIDTH] = dk.astype(BF)
        dz_ref[:, :, :, 2 * A_WIDTH:ZA_W] = dv.astype(BF)
        dq, dk, dv = shifted(1)
        c0 = ZA_W + ZB_W
        dz_ref[:, :, :, c0:c0 + A_WIDTH] = dq.astype(BF)
        dz_ref[:, :, :, c0 + A_WIDTH:c0 + A_WIDTH + C_KV_WIDTH] = dk.astype(BF)
        dz_ref[:, :, :, c0 + A_WIDTH + C_KV_WIDTH:IN_WIDTH] = dv.astype(BF)

        cw = cw_ref[...]
        prev = jnp.where(i > 0, zbp_ref[...], 0.0)
        gb, gc, xb, u, u1, u2, c = _conv_strips(zb_ref[...], prev, cw)
        dyb = dyb_ref[...]
        dc = [_strip(dyb, b) * gb[b] for b in range(N_STRIPS)]
        dcn = jnp.where(i + 1 < nb, dybn_ref[...] * zbn_ref[:, :, :CONV_CH], 0.0)
        wrapped = [_shift_up(dc[0], 1, dcn[0]), _shift_up(dc[1], 1, dcn[1])]
        upd = [jnp.zeros((1, CONV_CH), F32)] * 3
        for b in range(N_STRIPS):
            dc1 = dc[b + 1] if b + 1 < N_STRIPS else wrapped[0]
            dc2 = dc[b + 2] if b + 2 < N_STRIPS else wrapped[b + 2 - N_STRIPS]
            du = cw[2:3, :] * dc[b] + cw[1:2, :] * dc1 + cw[0:1, :] * dc2
            f, e = b % 4, b // 4
            dz_ref[f, e, :, ZA_W:ZA_W + CONV_CH] = (_strip(dyb, b) * c[b]).astype(BF)
            dz_ref[f, e, :, ZA_W + CONV_CH:ZA_W + 2 * CONV_CH] = (du * xb[b]).astype(BF)
            dz_ref[f, e, :, ZA_W + 2 * CONV_CH:c0] = (du * gc[b]).astype(BF)
            for t, uu in enumerate((u2[b], u1[b], u[b])):
                upd[t] = upd[t] + jnp.sum(dc[b] * uu, axis=0, keepdims=True)
        row = lax.broadcasted_iota(jnp.int32, (HALO, CONV_CH), 0)
        tile = jnp.zeros((HALO, CONV_CH), F32)
        for t in range(3):
            tile = jnp.where(row == t, upd[t], tile)
        dcw_ref[...] += tile

    dz, dcw = pl.pallas_call(
        body, grid=(nb,), name="dz_assemble", in_specs=in_specs,
        out_specs=[_strip_rows(DZ_TA, IN_WIDTH), _whole((HALO, CONV_CH))],
        out_shape=[jax.ShapeDtypeStruct((4, 4, na, IN_WIDTH), BF), jax.ShapeDtypeStruct((HALO, CONV_CH), F32)],
        compiler_params=_cparams("arbitrary"),
    )(*args)
    return dz.reshape(s, IN_WIDTH), dcw


def _qkv_bwd(dz, dx1, x, g, w_all, l, tb, tokens_out):
    s, d = x.shape
    na, ta = s // N_STRIPS, tb // N_STRIPS

    def body(dz_ref, dx1_ref, x_ref, g_ref, w_ref, dx_ref, dg_ref):
        i = pl.program_id(0)

        @pl.when(i == 0)
        def _():
            dg_ref[...] = jnp.zeros_like(dg_ref)

        n = IN_WIDTH // N_CHIPS
        dz = dz_ref[...].reshape(tb, IN_WIDTH)
        dh = _nt(dz[:, 0:n], w_ref[0])
        for k in range(1, N_CHIPS):
            dh = dh + _nt(dz[:, k * n:(k + 1) * n], w_ref[k])
        xv = x_ref[...].reshape(tb, d)
        r = _rms_scale(xv)
        xhat = xv * r
        dg_ref[...] += jnp.sum(dh * xhat, axis=0, keepdims=True)
        dx = (dx1_ref[...].reshape(tb, d) + _norm_bwd(dh * g_ref[...], xhat, r)).reshape(4, 4, ta, d)
        if tokens_out:
            for b in range(N_STRIPS):
                dx_ref[:, b, :] = _strip(dx, b)
        else:
            dx_ref[...] = dx

    if tokens_out:
        dx_spec, dx_shape = pl.BlockSpec((ta, N_STRIPS, d), lambda i: (i, 0, 0)), (na, N_STRIPS, d)
    else:
        dx_spec, dx_shape = _strip_rows(ta, d), (4, 4, na, d)
    dx, dg = pl.pallas_call(
        body, grid=(s // tb,), name="qkv_bwd",
        in_specs=[_strip_rows(ta, IN_WIDTH), _strip_rows(ta, d), _strip_rows(ta, d), _whole((1, d)),
                  _layer((N_CHIPS, d, IN_WIDTH // N_CHIPS), l)],
        out_specs=[dx_spec, _whole((HALO, d))],
        out_shape=[jax.ShapeDtypeStruct(dx_shape, F32), jax.ShapeDtypeStruct((HALO, d), F32)],
        compiler_params=_cparams("arbitrary"),
    )(_strips(dz), _strips(dx1), _strips(x), g, w_all)
    return dx.reshape(s, d), dg


def _tile_rows(rows):
    return jnp.pad(rows, ((0, HALO - rows.shape[0]), (0, 0)))


def _to_strips(a, after, name):
    s, d = a.shape
    na = s // N_STRIPS
    ta = min(32, na)

    def body(a_ref, *rest):
        for b in range(N_STRIPS):
            rest[-1][b % 4, b // 4] = a_ref[:, b, :]

    return pl.pallas_call(
        body, grid=(na // ta,), name=name,
        in_specs=[pl.BlockSpec((ta, N_STRIPS, d), lambda i: (i, 0, 0))] + [ANY] * len(after),
        out_specs=_strip_rows(ta, d),
        out_shape=jax.ShapeDtypeStruct((4, 4, na, d), a.dtype), compiler_params=_cparams("parallel"),
    )(a.reshape(na, N_STRIPS, d), *after).reshape(s, d)


def _local_step(x, tgt, fetch, ff, sinks, g_mix, g_group, g_mlp, g_final, emit):
    s, d = x.shape
    depth = g_mix.shape[0]
    tb = min(512, s)
    tf = ff // N_CHIPS
    ts = min(1024, s)
    saved = []
    for l in range(depth):
        w_in, _, _, _, conv_w = fetch(0, l, x)
        cw = _tile_rows(conv_w[l])
        sk = jnp.repeat(sinks[l].reshape(N_HEADS), HEAD_DIM)[None]
        h, za, zb, zc = _qkv_fwd(x, g_mix[l][None], w_in, l, tb)
        parts_a = [_attn_fwd(za, dil, A_WIDTH, 1, 2, 1, A_MAX_DIST, "attn_a_fwd_%d" % dil) for dil in DILATIONS]
        part_c = _attn_fwd(zc, 1, C_KV_WIDTH, 3, 4, C_GROUP, C_MAX_DIST, "attn_c_fwd")
        ya, lse_a, yc, lse_c = _attn_merge(parts_a, part_c, sk, tb)
        w_in, w_o, w1, w2, _ = fetch(1, l, yc)
        x1, yb = _mix_fwd(x, ya, yc, zb, cw, g_group[l][None], w_o, l, tb)
        x2, h2, ap = _mlp_fwd(x1, g_mlp[l][None], w1, w2, l, ts, tf)
        saved.append((x, h, za, zb, zc, ya, lse_a, yc, lse_c, yb, x1, h2, ap, cw, sk))
        x = x2
    dx, loss_tile, dg_final = _loss_head(x, g_final[None], tgt, tb)
    grads = [None] * depth
    tok = jnp.zeros((), F32)
    for l in reversed(range(depth)):
        x0, h, za, zb, zc, ya, lse_a, yc, lse_c, yb, x1, h2, ap, cw, sk = saved[l]
        dx1, dap, dg_mlp = _mlp_bwd(dx, x1, ap, g_mlp[l][None] + tok, w1, w2, l, ts, tf)
        tok = emit(l, 3, _wgrad(ap, dx, min(1024, ff), d, ts, "wgrad_ff_out", relu2=True))
        tok = tok + emit(l, 2, _wgrad(h2, dap, d, min(1024, ff), 2 * ts, "wgrad_ff_in"))
        n, dya, dyc, dd_a, dd_c, dyb, dg_group, dsink = _mix_bwd(dx1, ya, yb, yc, lse_c, sk, g_group[l][None] + tok,
                                                                 w_o, l, tb)
        tok = emit(l, 1, _wgrad(n, dx1, MIX_WIDTH, d, ts, "wgrad_o"))
        cw = cw + tok
        parts_a = [_attn_bwd(za, dya, lse_a, dd_a, dil, A_WIDTH, 1, 2, 1, A_MAX_DIST, "attn_a_bwd_%d" % dil)
                   for dil in DILATIONS]
        parts_c = _attn_bwd(zc, dyc, lse_c, dd_c, 1, C_KV_WIDTH, 3, 4, C_GROUP, C_MAX_DIST, "attn_c_bwd")
        dz, dcw = _dz_assemble(parts_a, parts_c, dyb, zb, cw)
        tok = emit(l, 0, _wgrad(h, dz, d, IN_WIDTH // 4, 2 * ts, "wgrad_in"))
        dx, dg_mix = _qkv_bwd(dz, dx1, x0, g_mix[l][None] + tok, w_in, l, tb, l == 0)
        grads[l] = (dcw, dsink, dg_mix, dg_group, dg_mlp)
    return loss_tile, dx, grads, dg_final


ANY = pl.BlockSpec(memory_space=pl.ANY)
SHARD_AXES = (2, 1, 2, 1)
N_BIG = len(SHARD_AXES)
N_CHIPS = 4
N_DEV = 8


def _mesh_pos():
    return lax.axis_index("x"), lax.axis_index("y"), lax.axis_index("c")


def _flip(v, bit):
    return 1 - v if bit else v


def _place_shard(shard, chip_arr, name):
    _, rows, cols = shard.shape
    tr = min(256, rows)

    def body(chip_ref, x_ref, o_ref):
        o_ref[...] = x_ref[...].astype(BF)

    return pl.pallas_call(
        body, name=name,
        grid_spec=pltpu.PrefetchScalarGridSpec(
            num_scalar_prefetch=1, grid=(2, rows // tr),
            in_specs=[pl.BlockSpec((None, tr, cols), lambda l, i, chip: (l, i, 0))],
            out_specs=pl.BlockSpec((None, None, tr, cols), lambda l, i, chip: (l, chip[0], i, 0))),
        out_shape=jax.ShapeDtypeStruct((2, N_CHIPS, rows, cols), BF),
        compiler_params=_cparams("parallel", "parallel"),
    )(chip_arr, shard)


HBM = pl.BlockSpec(memory_space=pltpu.HBM)
SEM = pl.BlockSpec(memory_space=pltpu.SEMAPHORE)
EFFECT = pltpu.SideEffectType.DATAFLOW_SIDE_EFFECTING

GATHER_GROUPS = (((0, 0),), ((1, 0), (2, 0), (3, 0)), ((0, 1),), ((1, 1), (2, 1), (3, 1)))
GATHER_STARTS = ((0,), (1,), (2, 3))


def _gather_copies(arrs, group, send_sems, recv_sems):
    x, y, c = _mesh_pos()
    me = 2 * x + y
    out = []
    for i, (w, layer) in enumerate(group):
        mine = arrs[w].at[layer, me]
        for j, (qx, qy) in enumerate([(1 - x, y), (x, 1 - y), (1 - x, 1 - y)]):
            landed = arrs[w].at[layer, 2 * qx + qy]
            out.append(tuple(pltpu.make_async_remote_copy(
                src_ref=piece, dst_ref=piece, send_sem=send_sems.at[i * 3 + j], recv_sem=recv_sems.at[i * 3 + j],
                device_id=(qx, qy, c), device_id_type=MESH) for piece in (mine, landed)))
    return out


def _conv_copies(conv_src, conv_dst, send_sems, recv_sems):
    x, y, c = _mesh_pos()
    out = []
    for j, (qx, qy) in enumerate([(1 - x, y), (x, 1 - y), (1 - x, 1 - y)]):
        out.append(tuple(pltpu.make_async_remote_copy(
            src_ref=conv_src, dst_ref=conv_dst.at[q], send_sem=send_sems.at[j], recv_sem=recv_sems.at[j],
            device_id=(qx, qy, c), device_id_type=MESH) for q in (2 * x + y, 2 * qx + qy)))
    return out


def _gather_start(groups, arrs, conv, name, through=None):
    n_sems = 2 * (len(groups) + (conv is not None))
    mats = sorted({w for g in groups for w, _ in GATHER_GROUPS[g]})

    def body(*refs):
        arrs_ref = [None] * N_BIG
        for w, ref in zip(mats, refs):
            arrs_ref[w] = ref
        sems = refs[n_in:n_in + n_sems]
        if conv is not None:
            for cp, _ in _conv_copies(refs[len(mats)], refs[len(mats) + 1], sems[-2], sems[-1]):
                cp.start()
        for k, g in enumerate(groups):
            for cp, _ in _gather_copies(arrs_ref, GATHER_GROUPS[g], sems[2 * k], sems[2 * k + 1]):
                cp.start()

    sem_shapes = []
    for n in [len(GATHER_GROUPS[g]) for g in groups] + ([1] if conv is not None else []):
        sem_shapes += [pltpu.SemaphoreType.DMA((3 * n,))] * 2
    operands = [arrs[w] for w in mats] + ([] if conv is None else list(conv)) + ([] if through is None else [through])
    n_in = len(operands)
    res = pl.pallas_call(
        body, name=name,
        out_shape=tuple(sem_shapes) + tuple(pltpu.HBM(a.shape, a.dtype) for a in operands),
        in_specs=(HBM,) * n_in, out_specs=(SEM,) * n_sems + (HBM,) * n_in,
        input_output_aliases={i: n_sems + i for i in range(n_in)},
        compiler_params=pltpu.CompilerParams(has_side_effects=EFFECT),
    )(*[pltpu.with_memory_space_constraint(a, pltpu.HBM) for a in operands])
    arrs = list(arrs)
    for w, a in zip(mats, res[n_sems:]):
        arrs[w] = a
    return res[:n_sems], arrs, list(res[n_sems + len(mats):])


def _gather_wait(k, sems, arrs, conv, after, name):
    group = GATHER_GROUPS[k]
    mats = sorted({w for w, _ in group})
    n_conv = 0 if conv is None else 2

    def body(*refs):
        local = refs[:len(mats)]
        arrs_ref = [None] * N_BIG
        for w, ref in zip(mats, local):
            arrs_ref[w] = ref
        pos = len(mats) + n_conv
        copies = _gather_copies(arrs_ref, group, refs[pos], refs[pos + 1])
        if conv is not None:
            copies += _conv_copies(refs[len(mats)], refs[len(mats) + 1], refs[pos + 2], refs[pos + 3])
        for send, recv in copies:
            recv.wait_recv()
            send.wait_send()

    operands = [arrs[w] for w in mats] + ([] if conv is None else [conv[1], conv[2]])
    sem_ops = list(sems) + ([] if conv is None else list(conv[0]))
    n_op = len(operands)
    res = pl.pallas_call(
        body, name=name, out_shape=tuple(pltpu.HBM(a.shape, a.dtype) for a in operands),
        in_specs=(HBM,) * n_op + (SEM,) * len(sem_ops) + (ANY,) * len(after), out_specs=(HBM,) * n_op,
        input_output_aliases={i: i for i in range(n_op)},
        compiler_params=pltpu.CompilerParams(has_side_effects=EFFECT),
    )(*operands, *sem_ops, *after)
    arrs = list(arrs)
    for w, a in zip(mats, res):
        arrs[w] = a
    return arrs, (res[-1] if conv is not None else None)


def _grad_shard(ref, w, chip, n):
    start = pl.multiple_of(chip * n, 128)
    if SHARD_AXES[w] == 2:
        return ref.at[:, pl.ds(start, n)]
    return ref.at[pl.ds(start, n), :]


def _slot_shape(g, w):
    shape = list(g.shape)
    shape[SHARD_AXES[w] - 1] //= N_CHIPS
    return (N_DEV - 1,) + tuple(shape)


def _scatter_copies(g_ref, land_ref, send_sems, recv_sems, layer, w):
    x, y, c = _mesh_pos()
    n = g_ref.shape[SHARD_AXES[w] - 1] // N_CHIPS
    out = []
    for r in range(1, N_DEV):
        tx, ty, tc = _flip(x, r & 4), _flip(y, r & 2), _flip(c, r & 1)
        cp = pltpu.make_async_remote_copy(
            src_ref=_grad_shard(g_ref, w, 2 * tx + ty, n), dst_ref=land_ref.at[r - 1], send_sem=send_sems.at[r - 1],
            recv_sem=recv_sems.at[r - 1], device_id=(tx, ty, tc), device_id_type=MESH)
        out.append((cp, (c != layer) if r & 1 else (c == layer)))
    return out


def _scatter_start(items, layer, name):
    n = len(items)

    def body(*refs):
        for i, (w, _, _) in enumerate(items):
            g_ref, land_ref = refs[2 * i], refs[2 * i + 1]
            send_sems, recv_sems = refs[2 * n + 2 * i], refs[2 * n + 2 * i + 1]
            for cp, mine in _scatter_copies(g_ref, land_ref, send_sems, recv_sems, layer, w):
                @pl.when(mine)
                def _():
                    cp.start()
        refs[-1][...] = jnp.zeros_like(refs[-1])

    operands = [a for _, g, land in items for a in (g, land)]
    res = pl.pallas_call(
        body, name=name,
        out_shape=(pltpu.SemaphoreType.DMA((N_DEV - 1,)),) * (2 * n)
        + tuple(pltpu.HBM(a.shape, a.dtype) for a in operands) + (jax.ShapeDtypeStruct((HALO, 128), F32),),
        in_specs=(HBM,) * (2 * n),
        out_specs=(SEM,) * (2 * n) + (HBM,) * (2 * n) + (pl.BlockSpec(memory_space=pltpu.VMEM),),
        input_output_aliases={i: 2 * n + i for i in range(2 * n)},
        compiler_params=pltpu.CompilerParams(has_side_effects=EFFECT),
    )(*[pltpu.with_memory_space_constraint(a, pltpu.HBM) for a in operands])
    return [(res[2 * i], res[2 * i + 1], res[2 * n + 2 * i], res[2 * n + 2 * i + 1]) for i in range(n)], res[-1]


def _scatter_wait(started, land, after, w, name):
    def body(g0_ref, g1_ref, land_ref, ss0, rs0, ss1, rs1, after_ref, g0_out, g1_out, land_out):
        c = lax.axis_index("c")
        for layer, g_ref, ss, rs in ((0, g0_ref, ss0, rs0), (1, g1_ref, ss1, rs1)):
            for cp, mine in _scatter_copies(g_ref, land_ref, ss, rs, layer, w):
                @pl.when(mine)
                def _():
                    cp.wait_send()

                @pl.when(c == layer)
                def _():
                    cp.wait_recv()

    (ss0, rs0, g0), (ss1, rs1, g1) = started
    return pl.pallas_call(
        body, name=name,
        out_shape=(pltpu.HBM(g0.shape, g0.dtype), pltpu.HBM(g1.shape, g1.dtype), pltpu.HBM(land.shape, land.dtype)),
        in_specs=(HBM, HBM, HBM, SEM, SEM, SEM, SEM, ANY), out_specs=(HBM, HBM, HBM),
        input_output_aliases={0: 0, 1: 1, 2: 2}, compiler_params=pltpu.CompilerParams(has_side_effects=EFFECT),
    )(g0, g1, land, ss0, rs0, ss1, rs1, after)


def _sum_slots(g0, g1, slots, w, pos_arr, name):
    _, rows, cols = slots.shape
    tr = min(256, rows)
    nr = rows // tr
    if SHARD_AXES[w] == 2:
        own = pl.BlockSpec((tr, cols), lambda i, pos: (i, pos[0]))
    else:
        own = pl.BlockSpec((tr, cols), lambda i, pos: (pos[0] * nr + i, 0))

    def body(pos_ref, own0_ref, own1_ref, s_ref, o_ref):
        acc = jnp.where(pos_ref[1] == 0, own0_ref[...], own1_ref[...]).astype(F32)
        for r in range(N_DEV - 1):
            acc = acc + s_ref[r].astype(F32)
        o_ref[...] = acc

    return pl.pallas_call(
        body, name=name,
        grid_spec=pltpu.PrefetchScalarGridSpec(
            num_scalar_prefetch=1, grid=(nr,),
            in_specs=[own, own, pl.BlockSpec((N_DEV - 1, tr, cols), lambda i, pos: (0, i, 0))],
            out_specs=pl.BlockSpec((tr, cols), lambda i, pos: (i, 0))),
        out_shape=jax.ShapeDtypeStruct((rows, cols), F32), compiler_params=_cparams("parallel"),
    )(pos_arr, g0, g1, slots)


def _swap_layers(halves, name):
    n = len(halves)

    def body(*refs):
        srcs, dsts = refs[:n], refs[n:2 * n]
        send_sems, recv_sems = refs[2 * n:]
        x, y, c = _mesh_pos()
        sends = [pltpu.make_async_remote_copy(src_ref=srcs[w], dst_ref=dsts[w], send_sem=send_sems.at[w],
                                              recv_sem=recv_sems.at[w], device_id=(x, y, 1 - c), device_id_type=MESH)
                 for w in range(n)]
        for cp in sends:
            cp.start()
        for cp in sends:
            cp.wait_recv()
        for cp in sends:
            cp.wait_send()

    return pl.pallas_call(
        body, name=name, in_specs=[ANY] * n, out_specs=[ANY] * n,
        out_shape=[jax.ShapeDtypeStruct(h.shape, h.dtype) for h in halves],
        scratch_shapes=[pltpu.SemaphoreType.DMA((n,)), pltpu.SemaphoreType.DMA((n,))],
    )(*halves)


def _adamw_math(w, g, m, v):
    m = ADAM_B1 * m + (1.0 - ADAM_B1) * g
    v = ADAM_B2 * v + (1.0 - ADAM_B2) * jnp.square(g)
    m_hat = m / (1.0 - ADAM_B1 ** ADAM_STEP)
    v_hat = v / (1.0 - ADAM_B2 ** ADAM_STEP)
    delta = -ADAM_LR * (m_hat / (jnp.sqrt(v_hat) + ADAM_EPS) + ADAM_WD * w)
    return delta, m, v


def _adamw(w, g_own, g_other, m, v, pos_arr, name):
    shape = w.shape
    _, rows, cols = shape
    tr = min(256, rows)

    def body(pos_ref, w_ref, own_ref, other_ref, m_ref, v_ref, g_ref, d_ref, m2_ref, v2_ref):
        g = jnp.where(pl.program_id(0) == pos_ref[1], own_ref[...], other_ref[...])
        g_ref[...] = g
        d_ref[...], m2_ref[...], v2_ref[...] = _adamw_math(w_ref[...], g, m_ref[...], v_ref[...])

    full = pl.BlockSpec((None, tr, cols), lambda l, i, pos: (l, i, 0))
    half = pl.BlockSpec((tr, cols), lambda l, i, pos: (i, 0))
    return pl.pallas_call(
        body, name=name,
        grid_spec=pltpu.PrefetchScalarGridSpec(
            num_scalar_prefetch=1, grid=(2, rows // tr),
            in_specs=[full, half, half, full, full], out_specs=[full] * 4),
        out_shape=[jax.ShapeDtypeStruct(shape, F32)] * 4, compiler_params=_cparams("parallel", "parallel"),
    )(pos_arr, w, g_own, g_other, m, v)


def _small_sync(part, w, m, v):
    rows, cols = part.shape

    def body(p_ref, w_ref, m_ref, v_ref, g_ref, d_ref, m2_ref, v2_ref, slots, send_sems, recv_sems):
        x, y, c = _mesh_pos()
        me = 4 * x + 2 * y + c
        slots[me] = p_ref[...]
        sends = []
        for r in range(1, N_DEV):
            to = (_flip(x, r & 4), _flip(y, r & 2), _flip(c, r & 1))
            sends.append(pltpu.make_async_remote_copy(
                src_ref=p_ref, dst_ref=slots.at[me], send_sem=send_sems.at[r - 1], recv_sem=recv_sems.at[r - 1],
                device_id=to, device_id_type=MESH))
        for cp in sends:
            cp.start()
        for cp in sends:
            cp.wait_recv()
        for cp in sends:
            cp.wait_send()
        g = slots[0]
        for i in range(1, N_DEV):
            g = g + slots[i]
        g_ref[...] = g
        d_ref[...], m2_ref[...], v2_ref[...] = _adamw_math(w_ref[...], g, m_ref[...], v_ref[...])

    vm = pl.BlockSpec(memory_space=pltpu.VMEM)
    return pl.pallas_call(
        body, name="small_sync", in_specs=[vm] * 4, out_specs=[vm] * 4,
        out_shape=[jax.ShapeDtypeStruct((rows, cols), F32)] * 4,
        scratch_shapes=[pltpu.VMEM((N_DEV, rows, cols), F32), pltpu.SemaphoreType.DMA((N_DEV - 1,)),
                        pltpu.SemaphoreType.DMA((N_DEV - 1,))],
    )(part, w, m, v)


PACK_W = 256


def _pack_rows(n):
    return -(-n // (HALO * PACK_W)) * HALO


def _pack_small(parts):
    out = []
    for a in parts:
        flat = a.reshape(-1)
        out.append(jnp.pad(flat, (0, _pack_rows(flat.size) * PACK_W - flat.size)).reshape(-1, PACK_W))
    return jnp.concatenate(out, axis=0)


def _unpack_small(p, shapes):
    out, row = [], 0
    for shape in shapes:
        n = 1
        for k in shape:
            n *= k
        out.append(p[row:row + _pack_rows(n)].reshape(-1)[:n].reshape(shape))
        row += _pack_rows(n)
    return out


def kernel(x, w_in, conv_w, sinks, g_mix, g_group, w_o, g_mlp, w_ff_in, w_ff_out, g_final, loss_target, m_w_in, m_conv_w, m_sinks, m_g_mix, m_g_group, m_w_o, m_g_mlp, m_w_ff_in, m_w_ff_out, m_g_final, v_w_in, v_conv_w, v_sinks, v_g_mix, v_g_group, v_w_o, v_g_mlp, v_w_ff_in, v_w_ff_out, v_g_final):
    chip = 2 * lax.axis_index("x") + lax.axis_index("y")
    conv_n = conv_w.shape[2]

    pos_arr = jnp.stack([chip, lax.axis_index("c")]).astype(jnp.int32)
    shards = (w_in, w_o, w_ff_in, w_ff_out)
    conv_tile = jnp.pad(conv_w.reshape(6, conv_n), ((0, HALO - 6), (0, 128 - conv_n)))
    placed = [_place_shard(w_in, pos_arr[:1], "place_shard_0"), None, None, None]
    sems_a, placed, conv_thru = _gather_start(
        GATHER_STARTS[0], placed, (conv_tile, lax.empty((N_CHIPS,) + conv_tile.shape, conv_tile.dtype)),
        "gather_start_0")
    for i in range(1, N_BIG):
        placed[i] = _place_shard(shards[i], pos_arr[:1], "place_shard_%d" % i)
    full = {"arrs": placed, "conv": None, "sems": list(sems_a[:2])}
    target = _to_strips(loss_target[0], placed[:1], "to_strips_target")

    def fetch(stage, layer, after):
        k = 2 * layer + stage
        sems = full["sems"][2 * k:2 * k + 2]
        if k == 0:
            full["arrs"], land = _gather_wait(0, sems, full["arrs"], (sems_a[-2:], *conv_thru), (after, target),
                                              "gather_wait_0")
            conv_all = lax.dynamic_update_slice(land, conv_tile[None], (chip, 0, 0))
            full["conv"] = conv_all[:, :6, :conv_n].reshape(N_CHIPS, 2, 3, conv_n).transpose(1, 2, 0, 3).reshape(
                2, 3, CONV_CH)
            sems_b, full["arrs"], rest = _gather_start(GATHER_STARTS[1], full["arrs"], None, "gather_start_1",
                                                       through=full["arrs"][0])
            full["arrs"][0] = rest[-1]
            full["sems"] += list(sems_b)
        else:
            full["arrs"], _ = _gather_wait(k, sems, full["arrs"], None, (after,), "gather_wait_%d" % k)
        if k == 1:
            sems_c, full["arrs"], _ = _gather_start(GATHER_STARTS[2], full["arrs"], None, "gather_start_2")
            full["sems"] += list(sems_c)
        return (*full["arrs"], full["conv"])

    lands, started, pending = [None] * N_BIG, {}, []

    def emit(layer, w, g):
        if lands[w] is None:
            lands[w] = lax.empty(_slot_shape(g, w), g.dtype)
        pending.append((w, g, lands[w]))
        if not (w == 0 or (layer == 0 and w == 1)):
            return jnp.zeros((), F32)
        name = "scatter_start_%d_%d" % (layer, len(pending))
        done, token = _scatter_start(list(pending), layer, name)
        for (w_i, _, _), (ss, rs, g_thru, land) in zip(pending, done):
            started[layer, w_i], lands[w_i] = (ss, rs, g_thru), land
        pending.clear()
        return token[0, 0]

    loss_tile, dx, grads, dg_final = _local_step(_to_strips(x[0], placed, "to_strips_x"), target, fetch,
                                                 w_ff_in.shape[2] * N_CHIPS,
                                                 sinks, g_mix, g_group, g_mlp, g_final, emit)

    wmv = ((w_in, m_w_in, v_w_in), (w_o, m_w_o, v_w_o), (w_ff_in, m_w_ff_in, v_w_ff_in),
           (w_ff_out, m_w_ff_out, v_w_ff_out))
    big, after = [None] * N_BIG, dx
    for name, ws in (("swap_layers_rest", (1, 2, 3)), ("swap_layers_in", (0,))):
        own = []
        for w in ws:
            g0, g1, slots = _scatter_wait((started[0, w], started[1, w]), lands[w], after, w, "scatter_wait_%d" % w)
            own.append(_sum_slots(g0, g1, slots, w, pos_arr, "sum_slots_%d" % w))
        for w, mine, theirs in zip(ws, own, _swap_layers(own, name)):
            big[w] = _adamw(wmv[w][0], mine, theirs, wmv[w][1], wmv[w][2], pos_arr, "adamw_%d" % w)
        after = big[ws[-1]][1]

    def both(i):
        return jnp.stack([grads[0][i][0], grads[1][i][0]])
    dconv = jnp.stack([grads[0][0][:3], grads[1][0][:3]])
    dsinks = jnp.stack([grads[0][1][0, ::HEAD_DIM], grads[1][1][0, ::HEAD_DIM]])
    part = _pack_small([both(2), both(3), both(4), dg_final[0], dconv, dsinks, loss_tile[0, 0]])

    def spread(shard):
        return lax.dynamic_update_slice(jnp.zeros((2, 3, CONV_CH), F32), shard, (0, 0, chip * conv_n))
    zero = jnp.zeros((), F32)
    packs = [_pack_small([a, b, c_, e, spread(f), g_, zero]) for a, b, c_, e, f, g_ in (
        (g_mix, g_group, g_mlp, g_final, conv_w, sinks),
        (m_g_mix, m_g_group, m_g_mlp, m_g_final, m_conv_w, m_sinks),
        (v_g_mix, v_g_group, v_g_mlp, v_g_final, v_conv_w, v_sinks))]
    shapes = [g_mix.shape, g_group.shape, g_mlp.shape, g_final.shape, (2, 3, CONV_CH), sinks.shape, ()]
    small = [_unpack_small(p, shapes) for p in _small_sync(part, *packs)]

    def shard_of(full):
        return lax.dynamic_slice(full, (0, 0, chip * conv_n), (2, 3, conv_n))
    small = [(s[0], s[1], s[2], s[3], shard_of(s[4]), s[5], s[6]) for s in small]
    loss = small[0][6]

    def ordered(kind):
        b = [big[i][kind] for i in range(N_BIG)]
        s = small[kind]
        return [b[0], s[4], s[5], s[0], s[1], b[1], s[2], b[2], b[3], s[3]]

    return (loss, dx[None], *ordered(0), *ordered(1), *ordered(2), *ordered(3))
```

```python
import functools

import jax
import jax.numpy as jnp
from jax import lax
from jax.experimental import pallas as pl
from jax.experimental.pallas import tpu as pltpu

HEAD_DIM = 64
N_HEADS = 6
C_GROUP = 3
A_WIDTH = N_HEADS * HEAD_DIM
C_KV_WIDTH = 2 * HEAD_DIM
CONV_CH = 256
ZA_W = 3 * A_WIDTH
ZB_W = 3 * CONV_CH
ZC_W = A_WIDTH + 2 * C_KV_WIDTH
IN_WIDTH = ZA_W + ZB_W + ZC_W
MIX_WIDTH = A_WIDTH + CONV_CH + A_WIDTH
DILATIONS = (1, 4, 16)
A_MAX_DIST = 128
C_MAX_DIST = 127
TQ = 128
EPS = 1e-6
SCALE = HEAD_DIM ** -0.5
NEG = -1e30
HALO = 8

ADAM_LR = 0.001
ADAM_B1 = 0.9
ADAM_B2 = 0.999
ADAM_EPS = 1e-08
ADAM_WD = 0.01
ADAM_STEP = 10

BF = jnp.bfloat16
F32 = jnp.float32
MESH = pl.DeviceIdType.MESH
VMEM_LIMIT = 56 * 1024 * 1024


def _cparams(*sem):
    return pltpu.CompilerParams(dimension_semantics=sem, vmem_limit_bytes=VMEM_LIMIT)


def _nt(a, b):
    return lax.dot_general(a, b, (((1,), (1,)), ((), ())), preferred_element_type=F32)


def _tn(a, b):
    return lax.dot_general(a, b, (((0,), (0,)), ((), ())), preferred_element_type=F32)


def _nn(a, b):
    return jnp.dot(a, b, preferred_element_type=F32)


def _rows(tb, w):
    return pl.BlockSpec((tb, w), lambda i: (i, 0))


def _whole(shape):
    return pl.BlockSpec(shape, lambda *_: (0,) * len(shape))


def _layer(shape, l):
    return pl.BlockSpec((None,) + shape, lambda *_: (l,) + (0,) * len(shape))


def _rms_scale(v):
    return lax.rsqrt(jnp.mean(v * v, axis=-1, keepdims=True) + EPS)


def _norm_bwd(dxhat, xhat, r):
    return r * (dxhat - xhat * jnp.mean(dxhat * xhat, axis=-1, keepdims=True))


def _qkv_fwd(x, g, w_all, l, tb):
    s, d = x.shape

    def body(x_ref, g_ref, w_ref, h_ref, za_ref, zb_ref, zc_ref):
        xv = x_ref[...]
        h = ((xv * _rms_scale(xv)) * g_ref[...]).astype(BF)
        h_ref[...] = h
        z = jnp.concatenate([_nn(h, w_ref[k]) for k in range(N_CHIPS)], axis=1)
        za_ref[...] = z[:, :ZA_W]
        zb_ref[...] = z[:, ZA_W:ZA_W + ZB_W]
        zc_ref[...] = z[:, ZA_W + ZB_W:]

    return pl.pallas_call(
        body, grid=(s // tb,), name="qkv_fwd",
        in_specs=[_rows(tb, d), _whole((1, d)), _layer((N_CHIPS, d, IN_WIDTH // N_CHIPS), l)],
        out_specs=[_rows(tb, d), _rows(tb, ZA_W), _rows(tb, ZB_W), _rows(tb, ZC_W)],
        out_shape=[jax.ShapeDtypeStruct((s, d), BF), jax.ShapeDtypeStruct((s, ZA_W), F32),
                   jax.ShapeDtypeStruct((s, ZB_W), F32), jax.ShapeDtypeStruct((s, ZC_W), F32)],
        compiler_params=_cparams("parallel"),
    )(x, g, w_all)


N_STRIPS = 16


def _strips(a):
    s, w = a.shape
    return a.reshape(4, 4, s // N_STRIPS, w)


P_ROWS = {16: TQ, 4: 32, 1: 8}


def _p_sub(s, dil, most):
    while (s // dil // TQ) % most:
        most //= 2
    return most


def _p_grid(s, dil, n_sub):
    nb = s // dil // TQ // n_sub
    return {16: (4, 4, nb), 4: (4, nb), 1: (nb,)}[dil]


def _p_spec(dil, cw, col, n_sub, prev=False):
    rows = P_ROWS[dil] * (1 if prev else n_sub)

    def blk(j):
        return jnp.maximum(n_sub * j - 1, 0) if prev else j
    if dil == 16:
        return pl.BlockSpec((None, None, rows, cw), lambda f, e, j: (f, e, blk(j), col))
    if dil == 4:
        return pl.BlockSpec((None, 4, rows, cw), lambda f, j: (f, 0, blk(j), col))
    return pl.BlockSpec((4, 4, rows, cw), lambda j: (0, 0, blk(j), col))


def _block_pos(i, dil):
    if dil == 16:
        return i
    if dil == 4:
        return 4 * (i % 32) + i // 32
    return 16 * (i % 8) + 4 * ((i // 8) % 4) + i // 32


def _band_mask(b, dil, max_dist):
    qi = _block_pos(lax.broadcasted_iota(jnp.int32, (TQ, 2 * TQ), 0), dil)
    col = lax.broadcasted_iota(jnp.int32, (TQ, 2 * TQ), 1)
    cur = col >= TQ
    dist = qi - _block_pos(col % TQ, dil) + jnp.where(cur, 0, TQ)
    return (dist >= 0) & (dist <= max_dist) & (cur | (b > 0))


def _hs(h):
    return slice(h * HEAD_DIM, (h + 1) * HEAD_DIM)


def _ld(ref, cols, rows=slice(None)):
    v = ref[..., rows, cols]
    return v.reshape(TQ, v.shape[-1])


def _st(ref, cols, val, rows=slice(None)):
    lead = ref.shape[:-2] + (ref.shape[-2] if rows == slice(None) else rows.stop - rows.start,)
    ref[..., rows, cols] = val.reshape(lead + (val.shape[-1],))


def _attn_fwd(z, dil, kw, kcol, vcol, n_rep, max_dist, name):
    s, zw = z.shape
    n_sub = _p_sub(s, dil, 4)
    grid = _p_grid(s, dil, n_sub)

    def body(q_ref, kp_ref, kc_ref, vp_ref, vc_ref, o_ref, lse_ref):
        for t in range(n_sub):
            rows = slice(t * P_ROWS[dil], (t + 1) * P_ROWS[dil])
            before = (slice(None),) if t == 0 else (slice((t - 1) * P_ROWS[dil], t * P_ROWS[dil]),)
            kb_ref, vb_ref = (kp_ref, vp_ref) if t == 0 else (kc_ref, vc_ref)
            mask = _band_mask(n_sub * pl.program_id(len(grid) - 1) if t == 0 else 1, dil, max_dist)
            scs, v2s = [], []
            for kh in range(N_HEADS // n_rep):
                k2 = jnp.concatenate([_ld(kb_ref, _hs(kh), *before), _ld(kc_ref, _hs(kh), rows)], axis=0).astype(BF)
                v2s.append(jnp.concatenate([_ld(vb_ref, _hs(kh), *before), _ld(vc_ref, _hs(kh), rows)],
                                           axis=0).astype(BF))
                for h in range(kh * n_rep, (kh + 1) * n_rep):
                    q = (_ld(q_ref, _hs(h), rows) * SCALE).astype(BF)
                    scs.append(jnp.where(mask, _nt(q, k2), NEG))
            for h, sc in enumerate(scs):
                m = jnp.max(sc, axis=1, keepdims=True)
                p = jnp.exp(sc - m)
                l = jnp.sum(p, axis=1, keepdims=True)
                _st(o_ref, _hs(h), _nn(p.astype(BF), v2s[h // n_rep]) / l, rows)
                _st(lse_ref, _hs(h), jnp.broadcast_to(m + jnp.log(l), (TQ, HEAD_DIM)), rows)

    res = pl.pallas_call(
        body, grid=grid, name=name,
        in_specs=[_p_spec(dil, A_WIDTH, 0, n_sub), _p_spec(dil, kw, kcol, n_sub, True), _p_spec(dil, kw, kcol, n_sub),
                  _p_spec(dil, kw, vcol, n_sub, True), _p_spec(dil, kw, vcol, n_sub)],
        out_specs=[_p_spec(dil, A_WIDTH, 0, n_sub)] * 2,
        out_shape=[jax.ShapeDtypeStruct((4, 4, s // N_STRIPS, A_WIDTH), F32)] * 2,
        compiler_params=_cparams(*(("parallel",) * len(grid))),
    )(*[_strips(z)] * 5)
    return [a.reshape(s, A_WIDTH) for a in res]


def _attn_merge(parts_a, part_c, sink_row, tb):
    s = part_c[0].shape[0]
    n_a = len(parts_a)

    def body(*refs):
        ins, sink_ref = refs[:2 * n_a + 2], refs[2 * n_a + 2]
        ya_ref, lsea_ref, yc_ref, lsec_ref = refs[2 * n_a + 3:]
        lses = [ins[2 * p + 1][...] for p in range(n_a)]
        m = functools.reduce(jnp.maximum, lses)
        ws = [jnp.exp(v - m) for v in lses]
        l = functools.reduce(jnp.add, ws)
        ya_ref[...] = functools.reduce(jnp.add, [w * ins[2 * p][...] for p, w in enumerate(ws)]) / l
        lsea_ref[...] = m + jnp.log(l)
        o_c, lse_c = [r[...] for r in ins[2 * n_a:]]
        sk = sink_ref[...]
        m2 = jnp.maximum(lse_c, sk)
        w = jnp.exp(lse_c - m2)
        l2 = w + jnp.exp(sk - m2)
        yc_ref[...] = o_c * (w / l2)
        lsec_ref[...] = m2 + jnp.log(l2)

    return pl.pallas_call(
        body, grid=(s // tb,), name="attn_merge",
        in_specs=[_rows(tb, A_WIDTH)] * (2 * n_a + 2) + [_whole((1, A_WIDTH))],
        out_specs=[_rows(tb, A_WIDTH)] * 4, out_shape=[jax.ShapeDtypeStruct((s, A_WIDTH), F32)] * 4,
        compiler_params=_cparams("parallel"),
    )(*[a for part in parts_a + [part_c] for a in part], sink_row)


def _shift_down(v, n, halo):
    rows = v.shape[0]
    out = pltpu.roll(v, n, 0)
    row = lax.broadcasted_iota(jnp.int32, v.shape, 0)
    for t in range(n):
        out = jnp.where(row == t, halo[HALO - n + t:HALO - n + t + 1, :], out)
    return out


def _shift_up(v, n, halo):
    rows = v.shape[0]
    out = pltpu.roll(v, rows - n, 0)
    row = lax.broadcasted_iota(jnp.int32, v.shape, 0)
    for t in range(n):
        out = jnp.where(row == rows - n + t, halo[t:t + 1, :], out)
    return out


def _strip(v, b):
    return v[b % 4, b // 4]


def _conv_strips(zb, prev, cw):
    gb = [_strip(zb, b)[:, :CONV_CH] for b in range(N_STRIPS)]
    gc = [_strip(zb, b)[:, CONV_CH:2 * CONV_CH] for b in range(N_STRIPS)]
    xb = [_strip(zb, b)[:, 2 * CONV_CH:] for b in range(N_STRIPS)]
    u = [g * v for g, v in zip(gc, xb)]
    uh = prev[:, :, CONV_CH:2 * CONV_CH] * prev[:, :, 2 * CONV_CH:]
    wrapped = {14: _shift_down(u[14], 1, uh[2]), 15: _shift_down(u[15], 1, uh[3])}
    u1 = [u[b - 1] if b >= 1 else wrapped[15] for b in range(N_STRIPS)]
    u2 = [u[b - 2] if b >= 2 else wrapped[14 + b] for b in range(N_STRIPS)]
    c = [cw[0:1, :] * u2[b] + cw[1:2, :] * u1[b] + cw[2:3, :] * u[b] for b in range(N_STRIPS)]
    return gb, gc, xb, u, u1, u2, c


def _strip_rows(ta, w):
    return pl.BlockSpec((4, 4, ta, w), lambda i: (0, 0, i, 0))


def _prev_rows(ta, w):
    return pl.BlockSpec((4, None, HALO, w), lambda i: (0, 3, jnp.maximum(i * (ta // HALO) - 1, 0), 0))


def _next_rows(ta, w, nblk):
    return pl.BlockSpec((4, None, HALO, w),
                        lambda i: (0, 0, jnp.minimum((i + 1) * (ta // HALO), nblk * (ta // HALO) - 1), 0))


def _mix_fwd(x, ya, yc, zb, cw, gg, wo_all, l, tb):
    s, d = x.shape
    ta = tb // N_STRIPS

    def body(x_ref, ya_ref, yc_ref, zb_ref, zbp_ref, cw_ref, gg_ref, wo_ref, x1_ref, yb_ref):
        i = pl.program_id(0)
        prev = jnp.where(i > 0, zbp_ref[...], 0.0)
        gb, _, _, _, _, _, c = _conv_strips(zb_ref[...], prev, cw_ref[...])
        for b in range(N_STRIPS):
            yb_ref[b % 4, b // 4] = gb[b] * c[b]
        yb = yb_ref[...].reshape(tb, CONV_CH)
        ya, yc = ya_ref[...].reshape(tb, A_WIDTH), yc_ref[...].reshape(tb, A_WIDTH)
        n = jnp.concatenate([ya * _rms_scale(ya), yb * _rms_scale(yb), yc * _rms_scale(yc)], axis=1)
        n = (n * gg_ref[...]).astype(BF)
        x1 = x_ref[...].reshape(tb, d) + _nn(n, wo_ref[...].reshape(MIX_WIDTH, d))
        x1_ref[...] = x1.reshape(4, 4, ta, d)

    res = pl.pallas_call(
        body, grid=(s // tb,), name="mix_fwd",
        in_specs=[_strip_rows(ta, d), _strip_rows(ta, A_WIDTH), _strip_rows(ta, A_WIDTH), _strip_rows(ta, ZB_W),
                  _prev_rows(ta, ZB_W), _whole((HALO, CONV_CH)), _whole((1, MIX_WIDTH)),
                  _layer((N_CHIPS, MIX_WIDTH // N_CHIPS, d), l)],
        out_specs=[_strip_rows(ta, d), _strip_rows(ta, CONV_CH)],
        out_shape=[jax.ShapeDtypeStruct((4, 4, s // N_STRIPS, d), F32),
                   jax.ShapeDtypeStruct((4, 4, s // N_STRIPS, CONV_CH), F32)],
        compiler_params=_cparams("parallel"),
    )(_strips(x), _strips(ya), _strips(yc), _strips(zb), _strips(zb), cw, gg, wo_all)
    return res[0].reshape(s, d), res[1].reshape(s, CONV_CH)


def _mlp_fwd(x1, g, w1_all, w2_all, l, tb, tf):
    s, d = x1.shape
    ff = w1_all.shape[1] * w1_all.shape[3]
    nj = ff // tf

    def body(x_ref, g_ref, w1_ref, w2_ref, x2_ref, h2_ref, ap_ref, acc):
        j = pl.program_id(1)

        @pl.when(j == 0)
        def _():
            xv = x_ref[...]
            h2_ref[...] = ((xv * _rms_scale(xv)) * g_ref[...]).astype(BF)
            acc[...] = jnp.zeros_like(acc)

        ap = _nn(h2_ref[...], w1_ref[...])
        ap_ref[...] = ap.astype(BF)
        a = jnp.square(jnp.maximum(ap, 0.0)).astype(BF)
        acc[...] += _nn(a, w2_ref[...])

        @pl.when(j == nj - 1)
        def _():
            x2_ref[...] = x_ref[...] + acc[...]

    return pl.pallas_call(
        body, grid=(s // tb, nj), name="mlp_fwd",
        in_specs=[pl.BlockSpec((tb, d), lambda i, j: (i, 0)), _whole((1, d)),
                  pl.BlockSpec((None, None, d, tf), lambda i, j: (l, j, 0, 0)),
                  pl.BlockSpec((None, None, tf, d), lambda i, j: (l, j, 0, 0))],
        out_specs=[pl.BlockSpec((tb, d), lambda i, j: (i, 0)), pl.BlockSpec((tb, d), lambda i, j: (i, 0)),
                   pl.BlockSpec((tb, tf), lambda i, j: (i, j))],
        out_shape=[jax.ShapeDtypeStruct((s, d), F32), jax.ShapeDtypeStruct((s, d), BF),
                   jax.ShapeDtypeStruct((s, ff), BF)],
        scratch_shapes=[pltpu.VMEM((tb, d), F32)],
        compiler_params=_cparams("parallel", "arbitrary"),
    )(x1, g, w1_all, w2_all)


def _loss_head(x, g, tgt, tb):
    s, d = x.shape

    def body(x_ref, g_ref, t_ref, dx_ref, loss_ref, dg_ref):
        i = pl.program_id(0)

        @pl.when(i == 0)
        def _():
            loss_ref[...] = jnp.zeros_like(loss_ref)
            dg_ref[...] = jnp.zeros_like(dg_ref)

        xv = x_ref[...]
        r = _rms_scale(xv)
        xhat = xv * r
        err = xhat * g_ref[...] - t_ref[...]
        part = jnp.sum(jnp.mean(jnp.square(err), axis=-1, keepdims=True), axis=0, keepdims=True)
        loss_ref[...] += 0.5 * part
        dy = err * (1.0 / d)
        dg_ref[...] += jnp.sum(dy * xhat, axis=0, keepdims=True)
        dx_ref[...] = _norm_bwd(dy * g_ref[...], xhat, r)

    return pl.pallas_call(
        body, grid=(s // tb,), name="loss_head",
        in_specs=[_rows(tb, d), _whole((1, d)), _rows(tb, d)],
        out_specs=[_rows(tb, d), _whole((HALO, 128)), _whole((HALO, d))],
        out_shape=[jax.ShapeDtypeStruct((s, d), F32), jax.ShapeDtypeStruct((HALO, 128), F32),
                   jax.ShapeDtypeStruct((HALO, d), F32)],
        compiler_params=_cparams("arbitrary"),
    )(x, g, tgt)


def _mlp_bwd(dx2, x1, ap, g, w1_all, w2_all, l, tb, tf):
    s, d = x1.shape
    ff = ap.shape[1]
    nj = ff // tf

    def body(dx2_ref, x1_ref, ap_ref, g_ref, w1_ref, w2_ref, dx1_ref, dap_ref, dg_ref, acc):
        i, j = pl.program_id(0), pl.program_id(1)

        @pl.when((i == 0) & (j == 0))
        def _():
            dg_ref[...] = jnp.zeros_like(dg_ref)

        @pl.when(j == 0)
        def _():
            acc[...] = jnp.zeros_like(acc)

        da = _nt(dx2_ref[...].astype(BF), w2_ref[...])
        dap = (da * (2.0 * jnp.maximum(ap_ref[...].astype(F32), 0.0))).astype(BF)
        dap_ref[...] = dap
        acc[...] += _nt(dap, w1_ref[...])

        @pl.when(j == nj - 1)
        def _():
            xv = x1_ref[...]
            r = _rms_scale(xv)
            xhat = xv * r
            dh = acc[...]
            dg_ref[...] += jnp.sum(dh * xhat, axis=0, keepdims=True)
            dx1_ref[...] = dx2_ref[...] + _norm_bwd(dh * g_ref[...], xhat, r)

    return pl.pallas_call(
        body, grid=(s // tb, nj), name="mlp_bwd",
        in_specs=[pl.BlockSpec((tb, d), lambda i, j: (i, 0)), pl.BlockSpec((tb, d), lambda i, j: (i, 0)),
                  pl.BlockSpec((tb, tf), lambda i, j: (i, j)),
                  _whole((1, d)), pl.BlockSpec((None, None, d, tf), lambda i, j: (l, j, 0, 0)),
                  pl.BlockSpec((None, None, tf, d), lambda i, j: (l, j, 0, 0))],
        out_specs=[pl.BlockSpec((tb, d), lambda i, j: (i, 0)), pl.BlockSpec((tb, tf), lambda i, j: (i, j)),
                   _whole((HALO, d))],
        out_shape=[jax.ShapeDtypeStruct((s, d), F32), jax.ShapeDtypeStruct((s, ff), BF),
                   jax.ShapeDtypeStruct((HALO, d), F32)],
        scratch_shapes=[pltpu.VMEM((tb, d), F32)],
        compiler_params=_cparams("arbitrary", "arbitrary"),
    )(dx2, x1, ap, g, w1_all, w2_all)


def _wgrad(a, b, tm, tn, ts, name, relu2=False):
    s, m = a.shape
    n = b.shape[1]
    ns = s // ts

    def body(a_ref, b_ref, o_ref, acc):
        k = pl.program_id(2)

        @pl.when(k == 0)
        def _():
            acc[...] = jnp.zeros_like(acc)

        av = a_ref[...]
        if relu2:
            av = jnp.square(jnp.maximum(av.astype(F32), 0.0)).astype(BF)
        acc[...] += _tn(av, b_ref[...].astype(BF))

        @pl.when(k == ns - 1)
        def _():
            o_ref[...] = acc[...].astype(BF)

    return pl.pallas_call(
        body, grid=(m // tm, n // tn, ns), name=name,
        in_specs=[pl.BlockSpec((ts, tm), lambda i, j, k: (k, i)), pl.BlockSpec((ts, tn), lambda i, j, k: (k, j))],
        out_specs=pl.BlockSpec((tm, tn), lambda i, j, k: (i, j)),
        out_shape=jax.ShapeDtypeStruct((m, n), BF),
        scratch_shapes=[pltpu.VMEM((tm, tn), F32)],
        compiler_params=_cparams("parallel", "parallel", "arbitrary"),
    )(a, b)


def _mix_bwd(dx1, ya, yb, yc, lse_c, sink_row, gg, wo_all, l, tb):
    s, d = dx1.shape

    def body(dx_ref, ya_ref, yb_ref, yc_ref, lse_ref, sink_ref, gg_ref, wo_ref,
             n_ref, dya_ref, dyc_ref, da_ref, dc_ref, dyb_ref, dg_ref, dsink_ref):
        i = pl.program_id(0)

        @pl.when(i == 0)
        def _():
            dg_ref[...] = jnp.zeros_like(dg_ref)
            dsink_ref[...] = jnp.zeros_like(dsink_ref)

        dn = _nt(dx_ref[...].astype(BF), wo_ref[...].reshape(MIX_WIDTH, d))
        ys = [ya_ref[...], yb_ref[...], yc_ref[...]]
        rs = [_rms_scale(v) for v in ys]
        nhat = jnp.concatenate([v * r for v, r in zip(ys, rs)], axis=1)
        gg = gg_ref[...]
        n_ref[...] = (nhat * gg).astype(BF)
        dg_ref[...] += jnp.sum(dn * nhat, axis=0, keepdims=True)
        dnh = dn * gg
        bounds = [(0, A_WIDTH), (A_WIDTH, A_WIDTH + CONV_CH), (A_WIDTH + CONV_CH, MIX_WIDTH)]
        dys = [_norm_bwd(dnh[:, lo:hi], nhat[:, lo:hi], r) for (lo, hi), r in zip(bounds, rs)]
        dyb_ref[...] = dys[1]
        for dy, y, dy_ref, dd_ref in ((dys[0], ys[0], dya_ref, da_ref), (dys[2], ys[2], dyc_ref, dc_ref)):
            dy_ref[...] = dy
            t = dy * y
            for h in range(N_HEADS):
                dd_ref[:, _hs(h)] = jnp.broadcast_to(jnp.sum(t[:, _hs(h)], axis=1, keepdims=True), (tb, HEAD_DIM))
        dsink_ref[...] -= jnp.sum(jnp.exp(sink_ref[...] - lse_ref[...]) * dc_ref[...], axis=0, keepdims=True)

    return pl.pallas_call(
        body, grid=(s // tb,), name="mix_bwd",
        in_specs=[_rows(tb, d), _rows(tb, A_WIDTH), _rows(tb, CONV_CH), _rows(tb, A_WIDTH), _rows(tb, A_WIDTH),
                  _whole((1, A_WIDTH)), _whole((1, MIX_WIDTH)), _layer((N_CHIPS, MIX_WIDTH // N_CHIPS, d), l)],
        out_specs=[_rows(tb, MIX_WIDTH), _rows(tb, A_WIDTH), _rows(tb, A_WIDTH), _rows(tb, A_WIDTH),
                   _rows(tb, A_WIDTH), _rows(tb, CONV_CH), _whole((HALO, MIX_WIDTH)), _whole((HALO, A_WIDTH))],
        out_shape=[jax.ShapeDtypeStruct((s, MIX_WIDTH), BF), jax.ShapeDtypeStruct((s, A_WIDTH), F32),
                   jax.ShapeDtypeStruct((s, A_WIDTH), F32), jax.ShapeDtypeStruct((s, A_WIDTH), F32),
                   jax.ShapeDtypeStruct((s, A_WIDTH), F32), jax.ShapeDtypeStruct((s, CONV_CH), F32),
                   jax.ShapeDtypeStruct((HALO, MIX_WIDTH), F32), jax.ShapeDtypeStruct((HALO, A_WIDTH), F32)],
        compiler_params=_cparams("arbitrary"),
    )(dx1, ya, yb, yc, lse_c, sink_row, gg, wo_all)


def _attn_bwd(z, dy, lse, dd, dil, kw, kcol, vcol, n_rep, max_dist, name):
    s, zw = z.shape
    n_sub = _p_sub(s, dil, 2) if n_rep == 1 else 1
    grid = _p_grid(s, dil, n_sub)
    n_kv = N_HEADS // n_rep
    dt = F32 if dil == 1 else BF

    def body(q_ref, kp_ref, kc_ref, vp_ref, vc_ref, dy_ref, lse_ref, dd_ref, dq_ref, dkp_ref, dkc_ref, dvp_ref, dvc_ref):
        for t in range(n_sub):
            rows = slice(t * P_ROWS[dil], (t + 1) * P_ROWS[dil])
            before = (slice(None),) if t == 0 else (slice((t - 1) * P_ROWS[dil], t * P_ROWS[dil]),)
            kb_ref, vb_ref = (kp_ref, vp_ref) if t == 0 else (kc_ref, vc_ref)
            mask = _band_mask(n_sub * pl.program_id(len(grid) - 1) if t == 0 else 1, dil, max_dist)
            k2s, qs, dys, scs, dps = [], [], [], [], []
            for kh in range(n_kv):
                k2s.append(jnp.concatenate([_ld(kb_ref, _hs(kh), *before), _ld(kc_ref, _hs(kh), rows)],
                                           axis=0).astype(BF))
                v2 = jnp.concatenate([_ld(vb_ref, _hs(kh), *before), _ld(vc_ref, _hs(kh), rows)], axis=0).astype(BF)
                for h in range(kh * n_rep, (kh + 1) * n_rep):
                    qs.append((_ld(q_ref, _hs(h), rows) * SCALE).astype(BF))
                    dys.append(_ld(dy_ref, _hs(h), rows).astype(BF))
                    scs.append(jnp.where(mask, _nt(qs[h], k2s[kh]), NEG))
                    dps.append(_nt(dys[h], v2))
            for kh in range(n_kv):
                k2 = k2s[kh]
                dk2 = jnp.zeros((2 * TQ, HEAD_DIM), F32)
                dv2 = jnp.zeros((2 * TQ, HEAD_DIM), F32)
                for h in range(kh * n_rep, (kh + 1) * n_rep):
                    lse_h = _ld(lse_ref, slice(h * HEAD_DIM, h * HEAD_DIM + 1), rows)
                    dd_h = _ld(dd_ref, slice(h * HEAD_DIM, h * HEAD_DIM + 1), rows)
                    p = jnp.exp(scs[h] - lse_h)
                    ds = (p * (dps[h] - dd_h)).astype(BF)
                    _st(dq_ref, _hs(h), (_nn(ds, k2) * SCALE).astype(dt), rows)
                    dk2 = dk2 + _tn(ds, qs[h])
                    dv2 = dv2 + _tn(p.astype(BF), dys[h])
                _st(dkp_ref, _hs(kh), dk2[:TQ].astype(dt), rows)
                _st(dkc_ref, _hs(kh), dk2[TQ:].astype(dt), rows)
                _st(dvp_ref, _hs(kh), dv2[:TQ].astype(dt), rows)
                _st(dvc_ref, _hs(kh), dv2[TQ:].astype(dt), rows)

    args = [_strips(z)] * 5 + [_strips(a) for a in (dy, lse, dd)]
    pair = _p_spec(dil, A_WIDTH, 0, n_sub)
    in_specs = [pair, _p_spec(dil, kw, kcol, n_sub, True), _p_spec(dil, kw, kcol, n_sub),
                _p_spec(dil, kw, vcol, n_sub, True), _p_spec(dil, kw, vcol, n_sub)] + [pair] * 3
    out_specs = [pair] + [_p_spec(dil, kw, 0, n_sub)] * 4
    na = s // N_STRIPS
    out_shape = [jax.ShapeDtypeStruct((4, 4, na, A_WIDTH), dt)] + [jax.ShapeDtypeStruct((4, 4, na, kw), dt)] * 4
    res = pl.pallas_call(
        body, grid=grid, name=name, in_specs=in_specs, out_specs=out_specs, out_shape=out_shape,
        compiler_params=_cparams(*(("parallel",) * len(grid))),
    )(*args)
    return [res[0].reshape(s, A_WIDTH)] + [a.reshape(s, kw) for a in res[1:]]


DZ_TA = 16


def _dz_assemble(parts_a, parts_c, dyb, zb, cw):
    s = zb.shape[0]
    na = s // N_STRIPS
    nb = na // DZ_TA

    def ahead(w, k):
        return pl.BlockSpec((4, 4, DZ_TA, w), lambda i: (0, 0, jnp.minimum(i + k, nb - 1), 0))

    args, in_specs = [], []
    for dil, (dq, dkp, dkc, dvp, dvc) in zip(DILATIONS + (1,), parts_a + [parts_c]):
        w = dkp.shape[1]
        here = _strip_rows(DZ_TA, w)
        if dil == 1:
            args += [dq, dkp, dkp, dkc, dvp, dvp, dvc]
            in_specs += [_strip_rows(DZ_TA, A_WIDTH), here, ahead(w, 1), here, here, ahead(w, 1), here]
        else:
            k = 8 * dil // DZ_TA
            args += [dq, dkp, dkc, dvp, dvc]
            in_specs += [_strip_rows(DZ_TA, A_WIDTH), ahead(w, k), here, ahead(w, k), here]
    n_att = len(args)
    args = [_strips(a) for a in args] + [_strips(dyb), _strips(dyb), _strips(zb), _strips(zb), _strips(zb), cw]
    in_specs += [_strip_rows(DZ_TA, CONV_CH), _next_rows(DZ_TA, CONV_CH, nb), _strip_rows(DZ_TA, ZB_W),
                 _prev_rows(DZ_TA, ZB_W), _next_rows(DZ_TA, ZB_W, nb), _whole((HALO, CONV_CH))]

    def body(*refs):
        att = list(refs[:n_att])
        dyb_ref, dybn_ref, zb_ref, zbp_ref, zbn_ref, cw_ref, dz_ref, dcw_ref = refs[n_att:]
        i = pl.program_id(0)

        @pl.when(i == 0)
        def _():
            dcw_ref[...] = jnp.zeros_like(dcw_ref)

        def shifted(dil):
            if dil == 1:
                dq_r, kp0, kp1, dkc_r, vp0, vp1, dvc_r = [att.pop(0) for _ in range(7)]
                live = i + 1 < nb
                half = DZ_TA // 2
                dkp = jnp.concatenate([kp0[:, :, half:, :], jnp.where(live, kp1[:, :, :half, :], 0.0)], axis=2)
                dvp = jnp.concatenate([vp0[:, :, half:, :], jnp.where(live, vp1[:, :, :half, :], 0.0)], axis=2)
            else:
                dq_r, dkp_r, dkc_r, dvp_r, dvc_r = [att.pop(0) for _ in range(5)]
                live = i + 8 * dil // DZ_TA < nb
                dkp = jnp.where(live, dkp_r[...].astype(F32), 0.0)
                dvp = jnp.where(live, dvp_r[...].astype(F32), 0.0)
            return dq_r[...].astype(F32), dkc_r[...].astype(F32) + dkp, dvc_r[...].astype(F32) + dvp

        dq, dk, dv = shifted(DILATIONS[0])
        for dil in DILATIONS[1:]:
            dq2, dk2, dv2 = shifted(dil)
            dq, dk, dv = dq + dq2, dk + dk2, dv + dv2
        dz_ref[:, :, :, 0:A_WIDTH] = dq.astype(BF)
        dz_ref[:, :, :, A_WIDTH:2 * A_WIDTH] = dk.astype(BF)
        dz_ref[:, :, :, 2 * A_WIDTH:ZA_W] = dv.astype(BF)
        dq, dk, dv = shifted(1)
        c0 = ZA_W + ZB_W
        dz_ref[:, :, :, c0:c0 + A_WIDTH] = dq.astype(BF)
        dz_ref[:, :, :, c0 + A_WIDTH:c0 + A_WIDTH + C_KV_WIDTH] = dk.astype(BF)
        dz_ref[:, :, :, c0 + A_WIDTH + C_KV_WIDTH:IN_WIDTH] = dv.astype(BF)

        cw = cw_ref[...]
        prev = jnp.where(i > 0, zbp_ref[...], 0.0)
        gb, gc, xb, u, u1, u2, c = _conv_strips(zb_ref[...], prev, cw)
        dyb = dyb_ref[...]
        dc = [_strip(dyb, b) * gb[b] for b in range(N_STRIPS)]
        dcn = jnp.where(i + 1 < nb, dybn_ref[...] * zbn_ref[:, :, :CONV_CH], 0.0)
        wrapped = [_shift_up(dc[0], 1, dcn[0]), _shift_up(dc[1], 1, dcn[1])]
        upd = [jnp.zeros((1, CONV_CH), F32)] * 3
        for b in range(N_STRIPS):
            dc1 = dc[b + 1] if b + 1 < N_STRIPS else wrapped[0]
            dc2 = dc[b + 2] if b + 2 < N_STRIPS else wrapped[b + 2 - N_STRIPS]
            du = cw[2:3, :] * dc[b] + cw[1:2, :] * dc1 + cw[0:1, :] * dc2
            f, e = b % 4, b // 4
            dz_ref[f, e, :, ZA_W:ZA_W + CONV_CH] = (_strip(dyb, b) * c[b]).astype(BF)
            dz_ref[f, e, :, ZA_W + CONV_CH:ZA_W + 2 * CONV_CH] = (du * xb[b]).astype(BF)
            dz_ref[f, e, :, ZA_W + 2 * CONV_CH:c0] = (du * gc[b]).astype(BF)
            for t, uu in enumerate((u2[b], u1[b], u[b])):
                upd[t] = upd[t] + jnp.sum(dc[b] * uu, axis=0, keepdims=True)
        row = lax.broadcasted_iota(jnp.int32, (HALO, CONV_CH), 0)
        tile = jnp.zeros((HALO, CONV_CH), F32)
        for t in range(3):
            tile = jnp.where(row == t, upd[t], tile)
        dcw_ref[...] += tile

    dz, dcw = pl.pallas_call(
        body, grid=(nb,), name="dz_assemble", in_specs=in_specs,
        out_specs=[_strip_rows(DZ_TA, IN_WIDTH), _whole((HALO, CONV_CH))],
        out_shape=[jax.ShapeDtypeStruct((4, 4, na, IN_WIDTH), BF), jax.ShapeDtypeStruct((HALO, CONV_CH), F32)],
        compiler_params=_cparams("arbitrary"),
    )(*args)
    return dz.reshape(s, IN_WIDTH), dcw


def _qkv_bwd(dz, dx1, x, g, w_all, l, tb, tokens_out):
    s, d = x.shape
    na, ta = s // N_STRIPS, tb // N_STRIPS

    def body(dz_ref, dx1_ref, x_ref, g_ref, w_ref, dx_ref, dg_ref):
        i = pl.program_id(0)

        @pl.when(i == 0)
        def _():
            dg_ref[...] = jnp.zeros_like(dg_ref)

        n = IN_WIDTH // N_CHIPS
        dz = dz_ref[...].reshape(tb, IN_WIDTH)
        dh = _nt(dz[:, 0:n], w_ref[0])
        for k in range(1, N_CHIPS):
            dh = dh + _nt(dz[:, k * n:(k + 1) * n], w_ref[k])
        xv = x_ref[...].reshape(tb, d)
        r = _rms_scale(xv)
        xhat = xv * r
        dg_ref[...] += jnp.sum(dh * xhat, axis=0, keepdims=True)
        dx = (dx1_ref[...].reshape(tb, d) + _norm_bwd(dh * g_ref[...], xhat, r)).reshape(4, 4, ta, d)
        if tokens_out:
            for b in range(N_STRIPS):
                dx_ref[:, b, :] = _strip(dx, b)
        else:
            dx_ref[...] = dx

    if tokens_out:
        dx_spec, dx_shape = pl.BlockSpec((ta, N_STRIPS, d), lambda i: (i, 0, 0)), (na, N_STRIPS, d)
    else:
        dx_spec, dx_shape = _strip_rows(ta, d), (4, 4, na, d)
    dx, dg = pl.pallas_call(
        body, grid=(s // tb,), name="qkv_bwd",
        in_specs=[_strip_rows(ta, IN_WIDTH), _strip_rows(ta, d), _strip_rows(ta, d), _whole((1, d)),
                  _layer((N_CHIPS, d, IN_WIDTH // N_CHIPS), l)],
        out_specs=[dx_spec, _whole((HALO, d))],
        out_shape=[jax.ShapeDtypeStruct(dx_shape, F32), jax.ShapeDtypeStruct((HALO, d), F32)],
        compiler_params=_cparams("arbitrary"),
    )(_strips(dz), _strips(dx1), _strips(x), g, w_all)
    return dx.reshape(s, d), dg


def _tile_rows(rows):
    return jnp.pad(rows, ((0, HALO - rows.shape[0]), (0, 0)))


def _to_strips(a, after, name):
    s, d = a.shape
    na = s // N_STRIPS
    ta = min(32, na)

    def body(a_ref, *rest):
        for b in range(N_STRIPS):
            rest[-1][b % 4, b // 4] = a_ref[:, b, :]

    return pl.pallas_call(
        body, grid=(na // ta,), name=name,
        in_specs=[pl.BlockSpec((ta, N_STRIPS, d), lambda i: (i, 0, 0))] + [ANY] * len(after),
        out_specs=_strip_rows(ta, d),
        out_shape=jax.ShapeDtypeStruct((4, 4, na, d), a.dtype), compiler_params=_cparams("parallel"),
    )(a.reshape(na, N_STRIPS, d), *after).reshape(s, d)


def _local_step(x, tgt, fetch, ff, sinks, g_mix, g_group, g_mlp, g_final, emit):
    s, d = x.shape
    depth = g_mix.shape[0]
    tb = min(512, s)
    tf = ff // N_CHIPS
    ts = min(1024, s)
    saved = []
    for l in range(depth):
        w_in, _, _, _, conv_w = fetch(0, l, x)
        cw = _tile_rows(conv_w[l])
        sk = jnp.repeat(sinks[l].reshape(N_HEADS), HEAD_DIM)[None]
        h, za, zb, zc = _qkv_fwd(x, g_mix[l][None], w_in, l, tb)
        parts_a = [_attn_fwd(za, dil, A_WIDTH, 1, 2, 1, A_MAX_DIST, "attn_a_fwd_%d" % dil) for dil in DILATIONS]
        part_c = _attn_fwd(zc, 1, C_KV_WIDTH, 3, 4, C_GROUP, C_MAX_DIST, "attn_c_fwd")
        ya, lse_a, yc, lse_c = _attn_merge(parts_a, part_c, sk, tb)
        w_in, w_o, w1, w2, _ = fetch(1, l, yc)
        x1, yb = _mix_fwd(x, ya, yc, zb, cw, g_group[l][None], w_o, l, tb)
        x2, h2, ap = _mlp_fwd(x1, g_mlp[l][None], w1, w2, l, ts, tf)
        saved.append((x, h, za, zb, zc, ya, lse_a, yc, lse_c, yb, x1, h2, ap, cw, sk))
        x = x2
    dx, loss_tile, dg_final = _loss_head(x, g_final[None], tgt, tb)
    grads = [None] * depth
    tok = jnp.zeros((), F32)
    for l in reversed(range(depth)):
        x0, h, za, zb, zc, ya, lse_a, yc, lse_c, yb, x1, h2, ap, cw, sk = saved[l]
        dx1, dap, dg_mlp = _mlp_bwd(dx, x1, ap, g_mlp[l][None] + tok, w1, w2, l, ts, tf)
        tok = emit(l, 3, _wgrad(ap, dx, min(1024, ff), d, ts, "wgrad_ff_out", relu2=True))
        tok = tok + emit(l, 2, _wgrad(h2, dap, d, min(1024, ff), 2 * ts, "wgrad_ff_in"))
        n, dya, dyc, dd_a, dd_c, dyb, dg_group, dsink = _mix_bwd(dx1, ya, yb, yc, lse_c, sk, g_group[l][None] + tok,
                                                                 w_o, l, tb)
        tok = emit(l, 1, _wgrad(n, dx1, MIX_WIDTH, d, ts, "wgrad_o"))
        cw = cw + tok
        parts_a = [_attn_bwd(za, dya, lse_a, dd_a, dil, A_WIDTH, 1, 2, 1, A_MAX_DIST, "attn_a_bwd_%d" % dil)
                   for dil in DILATIONS]
        parts_c = _attn_bwd(zc, dyc, lse_c, dd_c, 1, C_KV_WIDTH, 3, 4, C_GROUP, C_MAX_DIST, "attn_c_bwd")
        dz, dcw = _dz_assemble(parts_a, parts_c, dyb, zb, cw)
        tok = emit(l, 0, _wgrad(h, dz, d, IN_WIDTH // 4, 2 * ts, "wgrad_in"))
        dx, dg_mix = _qkv_bwd(dz, dx1, x0, g_mix[l][None] + tok, w_in, l, tb, l == 0)
        grads[l] = (dcw, dsink, dg_mix, dg_group, dg_mlp)
    return loss_tile, dx, grads, dg_final


ANY = pl.BlockSpec(memory_space=pl.ANY)
SHARD_AXES = (2, 1, 2, 1)
N_BIG = len(SHARD_AXES)
N_CHIPS = 4
N_DEV = 8


def _mesh_pos():
    return lax.axis_index("x"), lax.axis_index("y"), lax.axis_index("c")


def _flip(v, bit):
    return 1 - v if bit else v


def _place_shard(shard, chip_arr, name):
    _, rows, cols = shard.shape
    tr = min(256, rows)

    def body(chip_ref, x_ref, o_ref):
        o_ref[...] = x_ref[...].astype(BF)

    return pl.pallas_call(
        body, name=name,
        grid_spec=pltpu.PrefetchScalarGridSpec(
            num_scalar_prefetch=1, grid=(2, rows // tr),
            in_specs=[pl.BlockSpec((None, tr, cols), lambda l, i, chip: (l, i, 0))],
            out_specs=pl.BlockSpec((None, None, tr, cols), lambda l, i, chip: (l, chip[0], i, 0))),
        out_shape=jax.ShapeDtypeStruct((2, N_CHIPS, rows, cols), BF),
        compiler_params=_cparams("parallel", "parallel"),
    )(chip_arr, shard)


HBM = pl.BlockSpec(memory_space=pltpu.HBM)
SEM = pl.BlockSpec(memory_space=pltpu.SEMAPHORE)
EFFECT = pltpu.SideEffectType.DATAFLOW_SIDE_EFFECTING

GATHER_GROUPS = (((0, 0),), ((1, 0), (2, 0), (3, 0)), ((0, 1),), ((1, 1), (2, 1), (3, 1)))
GATHER_STARTS = ((0,), (1,), (2, 3))


def _gather_copies(arrs, group, send_sems, recv_sems):
    x, y, c = _mesh_pos()
    me = 2 * x + y
    out = []
    for i, (w, layer) in enumerate(group):
        mine = arrs[w].at[layer, me]
        for j, (qx, qy) in enumerate([(1 - x, y), (x, 1 - y), (1 - x, 1 - y)]):
            landed = arrs[w].at[layer, 2 * qx + qy]
            out.append(tuple(pltpu.make_async_remote_copy(
                src_ref=piece, dst_ref=piece, send_sem=send_sems.at[i * 3 + j], recv_sem=recv_sems.at[i * 3 + j],
                device_id=(qx, qy, c), device_id_type=MESH) for piece in (mine, landed)))
    return out


def _conv_copies(conv_src, conv_dst, send_sems, recv_sems):
    x, y, c = _mesh_pos()
    out = []
    for j, (qx, qy) in enumerate([(1 - x, y), (x, 1 - y), (1 - x, 1 - y)]):
        out.append(tuple(pltpu.make_async_remote_copy(
            src_ref=conv_src, dst_ref=conv_dst.at[q], send_sem=send_sems.at[j], recv_sem=recv_sems.at[j],
            device_id=(qx, qy, c), device_id_type=MESH) for q in (2 * x + y, 2 * qx + qy)))
    return out


def _gather_start(groups, arrs, conv, name, through=None):
    n_sems = 2 * (len(groups) + (conv is not None))
    mats = sorted({w for g in groups for w, _ in GATHER_GROUPS[g]})

    def body(*refs):
        arrs_ref = [None] * N_BIG
        for w, ref in zip(mats, refs):
            arrs_ref[w] = ref
        sems = refs[n_in:n_in + n_sems]
        if conv is not None:
            for cp, _ in _conv_copies(refs[len(mats)], refs[len(mats) + 1], sems[-2], sems[-1]):
                cp.start()
        for k, g in enumerate(groups):
            for cp, _ in _gather_copies(arrs_ref, GATHER_GROUPS[g], sems[2 * k], sems[2 * k + 1]):
                cp.start()

    sem_shapes = []
    for n in [len(GATHER_GROUPS[g]) for g in groups] + ([1] if conv is not None else []):
        sem_shapes += [pltpu.SemaphoreType.DMA((3 * n,))] * 2
    operands = [arrs[w] for w in mats] + ([] if conv is None else list(conv)) + ([] if through is None else [through])
    n_in = len(operands)
    res = pl.pallas_call(
        body, name=name,
        out_shape=tuple(sem_shapes) + tuple(pltpu.HBM(a.shape, a.dtype) for a in operands),
        in_specs=(HBM,) * n_in, out_specs=(SEM,) * n_sems + (HBM,) * n_in,
        input_output_aliases={i: n_sems + i for i in range(n_in)},
        compiler_params=pltpu.CompilerParams(has_side_effects=EFFECT),
    )(*[pltpu.with_memory_space_constraint(a, pltpu.HBM) for a in operands])
    arrs = list(arrs)
    for w, a in zip(mats, res[n_sems:]):
        arrs[w] = a
    return res[:n_sems], arrs, list(res[n_sems + len(mats):])


def _gather_wait(k, sems, arrs, conv, after, name):
    group = GATHER_GROUPS[k]
    mats = sorted({w for w, _ in group})
    n_conv = 0 if conv is None else 2

    def body(*refs):
        local = refs[:len(mats)]
        arrs_ref = [None] * N_BIG
        for w, ref in zip(mats, local):
            arrs_ref[w] = ref
        pos = len(mats) + n_conv
        copies = _gather_copies(arrs_ref, group, refs[pos], refs[pos + 1])
        if conv is not None:
            copies += _conv_copies(refs[len(mats)], refs[len(mats) + 1], refs[pos + 2], refs[pos + 3])
        for send, recv in copies:
            recv.wait_recv()
            send.wait_send()

    operands = [arrs[w] for w in mats] + ([] if conv is None else [conv[1], conv[2]])
    sem_ops = list(sems) + ([] if conv is None else list(conv[0]))
    n_op = len(operands)
    res = pl.pallas_call(
        body, name=name, out_shape=tuple(pltpu.HBM(a.shape, a.dtype) for a in operands),
        in_specs=(HBM,) * n_op + (SEM,) * len(sem_ops) + (ANY,) * len(after), out_specs=(HBM,) * n_op,
        input_output_aliases={i: i for i in range(n_op)},
        compiler_params=pltpu.CompilerParams(has_side_effects=EFFECT),
    )(*operands, *sem_ops, *after)
    arrs = list(arrs)
    for w, a in zip(mats, res):
        arrs[w] = a
    return arrs, (res[-1] if conv is not None else None)


def _grad_shard(ref, w, chip, n):
    start = pl.multiple_of(chip * n, 128)
    if SHARD_AXES[w] == 2:
        return ref.at[:, pl.ds(start, n)]
    return ref.at[pl.ds(start, n), :]


def _slot_shape(g, w):
    shape = list(g.shape)
    shape[SHARD_AXES[w] - 1] //= N_CHIPS
    return (N_DEV - 1,) + tuple(shape)


def _scatter_copies(g_ref, land_ref, send_sems, recv_sems, layer, w):
    x, y, c = _mesh_pos()
    n = g_ref.shape[SHARD_AXES[w] - 1] // N_CHIPS
    out = []
    for r in range(1, N_DEV):
        tx, ty, tc = _flip(x, r & 4), _flip(y, r & 2), _flip(c, r & 1)
        cp = pltpu.make_async_remote_copy(
            src_ref=_grad_shard(g_ref, w, 2 * tx + ty, n), dst_ref=land_ref.at[r - 1], send_sem=send_sems.at[r - 1],
            recv_sem=recv_sems.at[r - 1], device_id=(tx, ty, tc), device_id_type=MESH)
        out.append((cp, (c != layer) if r & 1 else (c == layer)))
    return out


def _scatter_start(items, layer, name):
    n = len(items)

    def body(*refs):
        for i, (w, _, _) in enumerate(items):
            g_ref, land_ref = refs[2 * i], refs[2 * i + 1]
            send_sems, recv_sems = refs[2 * n + 2 * i], refs[2 * n + 2 * i + 1]
            for cp, mine in _scatter_copies(g_ref, land_ref, send_sems, recv_sems, layer, w):
                @pl.when(mine)
                def _():
                    cp.start()
        refs[-1][...] = jnp.zeros_like(refs[-1])

    operands = [a for _, g, land in items for a in (g, land)]
    res = pl.pallas_call(
        body, name=name,
        out_shape=(pltpu.SemaphoreType.DMA((N_DEV - 1,)),) * (2 * n)
        + tuple(pltpu.HBM(a.shape, a.dtype) for a in operands) + (jax.ShapeDtypeStruct((HALO, 128), F32),),
        in_specs=(HBM,) * (2 * n),
        out_specs=(SEM,) * (2 * n) + (HBM,) * (2 * n) + (pl.BlockSpec(memory_space=pltpu.VMEM),),
        input_output_aliases={i: 2 * n + i for i in range(2 * n)},
        compiler_params=pltpu.CompilerParams(has_side_effects=EFFECT),
    )(*[pltpu.with_memory_space_constraint(a, pltpu.HBM) for a in operands])
    return [(res[2 * i], res[2 * i + 1], res[2 * n + 2 * i], res[2 * n + 2 * i + 1]) for i in range(n)], res[-1]


def _scatter_wait(started, land, after, w, name):
    def body(g0_ref, g1_ref, land_ref, ss0, rs0, ss1, rs1, after_ref, g0_out, g1_out, land_out):
        c = lax.axis_index("c")
        for layer, g_ref, ss, rs in ((0, g0_ref, ss0, rs0), (1, g1_ref, ss1, rs1)):
            for cp, mine in _scatter_copies(g_ref, land_ref, ss, rs, layer, w):
                @pl.when(mine)
                def _():
                    cp.wait_send()

                @pl.when(c == layer)
                def _():
                    cp.wait_recv()

    (ss0, rs0, g0), (ss1, rs1, g1) = started
    return pl.pallas_call(
        body, name=name,
        out_shape=(pltpu.HBM(g0.shape, g0.dtype), pltpu.HBM(g1.shape, g1.dtype), pltpu.HBM(land.shape, land.dtype)),
        in_specs=(HBM, HBM, HBM, SEM, SEM, SEM, SEM, ANY), out_specs=(HBM, HBM, HBM),
        input_output_aliases={0: 0, 1: 1, 2: 2}, compiler_params=pltpu.CompilerParams(has_side_effects=EFFECT),
    )(g0, g1, land, ss0, rs0, ss1, rs1, after)


def _sum_slots(g0, g1, slots, w, pos_arr, name):
    _, rows, cols = slots.shape
    tr = min(256, rows)
    nr = rows // tr
    if SHARD_AXES[w] == 2:
        own = pl.BlockSpec((tr, cols), lambda i, pos: (i, pos[0]))
    else:
        own = pl.BlockSpec((tr, cols), lambda i, pos: (pos[0] * nr + i, 0))

    def body(pos_ref, own0_ref, own1_ref, s_ref, o_ref):
        acc = jnp.where(pos_ref[1] == 0, own0_ref[...], own1_ref[...]).astype(F32)
        for r in range(N_DEV - 1):
            acc = acc + s_ref[r].astype(F32)
        o_ref[...] = acc

    return pl.pallas_call(
        body, name=name,
        grid_spec=pltpu.PrefetchScalarGridSpec(
            num_scalar_prefetch=1, grid=(nr,),
            in_specs=[own, own, pl.BlockSpec((N_DEV - 1, tr, cols), lambda i, pos: (0, i, 0))],
            out_specs=pl.BlockSpec((tr, cols), lambda i, pos: (i, 0))),
        out_shape=jax.ShapeDtypeStruct((rows, cols), F32), compiler_params=_cparams("parallel"),
    )(pos_arr, g0, g1, slots)


def _swap_layers(halves, name):
    n = len(halves)

    def body(*refs):
        srcs, dsts = refs[:n], refs[n:2 * n]
        send_sems, recv_sems = refs[2 * n:]
        x, y, c = _mesh_pos()
        sends = [pltpu.make_async_remote_copy(src_ref=srcs[w], dst_ref=dsts[w], send_sem=send_sems.at[w],
                                              recv_sem=recv_sems.at[w], device_id=(x, y, 1 - c), device_id_type=MESH)
                 for w in range(n)]
        for cp in sends:
            cp.start()
        for cp in sends:
            cp.wait_recv()
        for cp in sends:
            cp.wait_send()

    return pl.pallas_call(
        body, name=name, in_specs=[ANY] * n, out_specs=[ANY] * n,
        out_shape=[jax.ShapeDtypeStruct(h.shape, h.dtype) for h in halves],
        scratch_shapes=[pltpu.SemaphoreType.DMA((n,)), pltpu.SemaphoreType.DMA((n,))],
    )(*halves)


def _adamw_math(w, g, m, v):
    m = ADAM_B1 * m + (1.0 - ADAM_B1) * g
    v = ADAM_B2 * v + (1.0 - ADAM_B2) * jnp.square(g)
    m_hat = m / (1.0 - ADAM_B1 ** ADAM_STEP)
    v_hat = v / (1.0 - ADAM_B2 ** ADAM_STEP)
    delta = -ADAM_LR * (m_hat / (jnp.sqrt(v_hat) + ADAM_EPS) + ADAM_WD * w)
    return delta, m, v


def _adamw(w, g_own, g_other, m, v, pos_arr, name):
    shape = w.shape
    _, rows, cols = shape
    tr = min(256, rows)

    def body(pos_ref, w_ref, own_ref, other_ref, m_ref, v_ref, g_ref, d_ref, m2_ref, v2_ref):
        g = jnp.where(pl.program_id(0) == pos_ref[1], own_ref[...], other_ref[...])
        g_ref[...] = g
        d_ref[...], m2_ref[...], v2_ref[...] = _adamw_math(w_ref[...], g, m_ref[...], v_ref[...])

    full = pl.BlockSpec((None, tr, cols), lambda l, i, pos: (l, i, 0))
    own = pl.BlockSpec((tr, cols), lambda l, i, pos: (jnp.where(l == pos[1], i, 0), 0))
    other = pl.BlockSpec((tr, cols), lambda l, i, pos: (jnp.where(l == pos[1], 0, i), 0))
    return pl.pallas_call(
        body, name=name,
        grid_spec=pltpu.PrefetchScalarGridSpec(
            num_scalar_prefetch=1, grid=(2, rows // tr),
            in_specs=[full, own, other, full, full], out_specs=[full] * 4),
        out_shape=[jax.ShapeDtypeStruct(shape, F32)] * 4, compiler_params=_cparams("parallel", "parallel"),
    )(pos_arr, w, g_own, g_other, m, v)


def _small_sync(part, w, m, v):
    rows, cols = part.shape

    def body(p_ref, w_ref, m_ref, v_ref, g_ref, d_ref, m2_ref, v2_ref, slots, send_sems, recv_sems):
        x, y, c = _mesh_pos()
        me = 4 * x + 2 * y + c
        slots[me] = p_ref[...]
        sends = []
        for r in range(1, N_DEV):
            to = (_flip(x, r & 4), _flip(y, r & 2), _flip(c, r & 1))
            sends.append(pltpu.make_async_remote_copy(
                src_ref=p_ref, dst_ref=slots.at[me], send_sem=send_sems.at[r - 1], recv_sem=recv_sems.at[r - 1],
                device_id=to, device_id_type=MESH))
        for cp in sends:
            cp.start()
        for cp in sends:
            cp.wait_recv()
        for cp in sends:
            cp.wait_send()
        g = slots[0]
        for i in range(1, N_DEV):
            g = g + slots[i]
        g_ref[...] = g
        d_ref[...], m2_ref[...], v2_ref[...] = _adamw_math(w_ref[...], g, m_ref[...], v_ref[...])

    vm = pl.BlockSpec(memory_space=pltpu.VMEM)
    return pl.pallas_call(
        body, name="small_sync", in_specs=[vm] * 4, out_specs=[vm] * 4,
        out_shape=[jax.ShapeDtypeStruct((rows, cols), F32)] * 4,
        scratch_shapes=[pltpu.VMEM((N_DEV, rows, cols), F32), pltpu.SemaphoreType.DMA((N_DEV - 1,)),
                        pltpu.SemaphoreType.DMA((N_DEV - 1,))],
    )(part, w, m, v)


PACK_W = 256


def _pack_rows(n):
    return -(-n // (HALO * PACK_W)) * HALO


def _pack_small(parts):
    out = []
    for a in parts:
        flat = a.reshape(-1)
        out.append(jnp.pad(flat, (0, _pack_rows(flat.size) * PACK_W - flat.size)).reshape(-1, PACK_W))
    return jnp.concatenate(out, axis=0)


def _unpack_small(p, shapes):
    out, row = [], 0
    for shape in shapes:
        n = 1
        for k in shape:
            n *= k
        out.append(p[row:row + _pack_rows(n)].reshape(-1)[:n].reshape(shape))
        row += _pack_rows(n)
    return out


def kernel(x, w_in, conv_w, sinks, g_mix, g_group, w_o, g_mlp, w_ff_in, w_ff_out, g_final, loss_target, m_w_in, m_conv_w, m_sinks, m_g_mix, m_g_group, m_w_o, m_g_mlp, m_w_ff_in, m_w_ff_out, m_g_final, v_w_in, v_conv_w, v_sinks, v_g_mix, v_g_group, v_w_o, v_g_mlp, v_w_ff_in, v_w_ff_out, v_g_final):
    chip = 2 * lax.axis_index("x") + lax.axis_index("y")
    conv_n = conv_w.shape[2]

    pos_arr = jnp.stack([chip, lax.axis_index("c")]).astype(jnp.int32)
    shards = (w_in, w_o, w_ff_in, w_ff_out)
    conv_tile = jnp.pad(conv_w.reshape(6, conv_n), ((0, HALO - 6), (0, 128 - conv_n)))
    placed = [_place_shard(w_in, pos_arr[:1], "place_shard_0"), None, None, None]
    sems_a, placed, conv_thru = _gather_start(
        GATHER_STARTS[0], placed, (conv_tile, lax.empty((N_CHIPS,) + conv_tile.shape, conv_tile.dtype)),
        "gather_start_0")
    for i in range(1, N_BIG):
        placed[i] = _place_shard(shards[i], pos_arr[:1], "place_shard_%d" % i)
    full = {"arrs": placed, "conv": None, "sems": list(sems_a[:2])}
    target = _to_strips(loss_target[0], placed[:1], "to_strips_target")

    def fetch(stage, layer, after):
        k = 2 * layer + stage
        sems = full["sems"][2 * k:2 * k + 2]
        if k == 0:
            full["arrs"], land = _gather_wait(0, sems, full["arrs"], (sems_a[-2:], *conv_thru), (after, target),
                                              "gather_wait_0")
            conv_all = lax.dynamic_update_slice(land, conv_tile[None], (chip, 0, 0))
            full["conv"] = conv_all[:, :6, :conv_n].reshape(N_CHIPS, 2, 3, conv_n).transpose(1, 2, 0, 3).reshape(
                2, 3, CONV_CH)
            sems_b, full["arrs"], rest = _gather_start(GATHER_STARTS[1], full["arrs"], None, "gather_start_1",
                                                       through=full["arrs"][0])
            full["arrs"][0] = rest[-1]
            full["sems"] += list(sems_b)
        else:
            full["arrs"], _ = _gather_wait(k, sems, full["arrs"], None, (after,), "gather_wait_%d" % k)
        if k == 1:
            sems_c, full["arrs"], _ = _gather_start(GATHER_STARTS[2], full["arrs"], None, "gather_start_2")
            full["sems"] += list(sems_c)
        return (*full["arrs"], full["conv"])

    lands, started, pending = [None] * N_BIG, {}, []

    def emit(layer, w, g):
        if lands[w] is None:
            lands[w] = lax.empty(_slot_shape(g, w), g.dtype)
        pending.append((w, g, lands[w]))
        if not (w == 0 or (layer == 0 and w == 1)):
            return jnp.zeros((), F32)
        name = "scatter_start_%d_%d" % (layer, len(pending))
        done, token = _scatter_start(list(pending), layer, name)
        for (w_i, _, _), (ss, rs, g_thru, land) in zip(pending, done):
            started[layer, w_i], lands[w_i] = (ss, rs, g_thru), land
        pending.clear()
        return token[0, 0]

    loss_tile, dx, grads, dg_final = _local_step(_to_strips(x[0], placed, "to_strips_x"), target, fetch,
                                                 w_ff_in.shape[2] * N_CHIPS,
                                                 sinks, g_mix, g_group, g_mlp, g_final, emit)

    wmv = ((w_in, m_w_in, v_w_in), (w_o, m_w_o, v_w_o), (w_ff_in, m_w_ff_in, v_w_ff_in),
           (w_ff_out, m_w_ff_out, v_w_ff_out))
    big, after = [None] * N_BIG, dx
    for name, ws in (("swap_layers_rest", (1, 2, 3)), ("swap_layers_in", (0,))):
        own = []
        for w in ws:
            g0, g1, slots = _scatter_wait((started[0, w], started[1, w]), lands[w], after, w, "scatter_wait_%d" % w)
            own.append(_sum_slots(g0, g1, slots, w, pos_arr, "sum_slots_%d" % w))
        for w, mine, theirs in zip(ws, own, _swap_layers(own, name)):
            big[w] = _adamw(wmv[w][0], mine, theirs, wmv[w][1], wmv[w][2], pos_arr, "adamw_%d" % w)
        after = big[ws[-1]][1]

    def both(i):
        return jnp.stack([grads[0][i][0], grads[1][i][0]])
    dconv = jnp.stack([grads[0][0][:3], grads[1][0][:3]])
    dsinks = jnp.stack([grads[0][1][0, ::HEAD_DIM], grads[1][1][0, ::HEAD_DIM]])
    part = _pack_small([both(2), both(3), both(4), dg_final[0], dconv, dsinks, loss_tile[0, 0]])

    def spread(shard):
        return lax.dynamic_update_slice(jnp.zeros((2, 3, CONV_CH), F32), shard, (0, 0, chip * conv_n))
    zero = jnp.zeros((), F32)
    packs = [_pack_small([a, b, c_, e, spread(f), g_, zero]) for a, b, c_, e, f, g_ in (
        (g_mix, g_group, g_mlp, g_final, conv_w, sinks),
        (m_g_mix, m_g_group, m_g_mlp, m_g_final, m_conv_w, m_sinks),
        (v_g_mix, v_g_group, v_g_mlp, v_g_final, v_conv_w, v_sinks))]
    shapes = [g_mix.shape, g_group.shape, g_mlp.shape, g_final.shape, (2, 3, CONV_CH), sinks.shape, ()]
    small = [_unpack_small(p, shapes) for p in _small_sync(part, *packs)]

    def shard_of(full):
        return lax.dynamic_slice(full, (0, 0, chip * conv_n), (2, 3, conv_n))
    small = [(s[0], s[1], s[2], s[3], shard_of(s[4]), s[5], s[6]) for s in small]
    loss = small[0][6]

    def ordered(kind):
        b = [big[i][kind] for i in range(N_BIG)]
        s = small[kind]
        return [b[0], s[4], s[5], s[0], s[1], b[1], s[2], b[2], b[3], s[3]]

    return (loss, dx[None], *ordered(0), *ordered(1), *ordered(2), *ordered(3))
```

```python
import functools

import jax
import jax.numpy as jnp
from jax import lax
from jax.experimental import pallas as pl
from jax.experimental.pallas import tpu as pltpu

HEAD_DIM = 64
N_HEADS = 6
C_GROUP = 3
A_WIDTH = N_HEADS * HEAD_DIM
C_KV_WIDTH = 2 * HEAD_DIM
CONV_CH = 256
ZA_W = 3 * A_WIDTH
ZB_W = 3 * CONV_CH
ZC_W = A_WIDTH + 2 * C_KV_WIDTH
IN_WIDTH = ZA_W + ZB_W + ZC_W
MIX_WIDTH = A_WIDTH + CONV_CH + A_WIDTH
DILATIONS = (1, 4, 16)
A_MAX_DIST = 128
C_MAX_DIST = 127
TQ = 128
EPS = 1e-6
SCALE = HEAD_DIM ** -0.5
NEG = -1e30
HALO = 8

ADAM_LR = 0.001
ADAM_B1 = 0.9
ADAM_B2 = 0.999
ADAM_EPS = 1e-08
ADAM_WD = 0.01
ADAM_STEP = 10

BF = jnp.bfloat16
F32 = jnp.float32
MESH = pl.DeviceIdType.MESH
VMEM_LIMIT = 56 * 1024 * 1024


def _cparams(*sem):
    return pltpu.CompilerParams(dimension_semantics=sem, vmem_limit_bytes=VMEM_LIMIT)


def _nt(a, b):
    return lax.dot_general(a, b, (((1,), (1,)), ((), ())), preferred_element_type=F32)


def _tn(a, b):
    return lax.dot_general(a, b, (((0,), (0,)), ((), ())), preferred_element_type=F32)


def _nn(a, b):
    return jnp.dot(a, b, preferred_element_type=F32)


def _rows(tb, w):
    return pl.BlockSpec((tb, w), lambda i: (i, 0))


def _whole(shape):
    return pl.BlockSpec(shape, lambda *_: (0,) * len(shape))


def _layer(shape, l):
    return pl.BlockSpec((None,) + shape, lambda *_: (l,) + (0,) * len(shape))


def _rms_scale(v):
    return lax.rsqrt(jnp.mean(v * v, axis=-1, keepdims=True) + EPS)


def _norm_bwd(dxhat, xhat, r):
    return r * (dxhat - xhat * jnp.mean(dxhat * xhat, axis=-1, keepdims=True))


def _qkv_fwd(x, g, w_all, l, tb):
    s, d = x.shape

    def body(x_ref, g_ref, w_ref, h_ref, za_ref, zb_ref, zc_ref):
        xv = x_ref[...]
        h = ((xv * _rms_scale(xv)) * g_ref[...]).astype(BF)
        h_ref[...] = h
        z = jnp.concatenate([_nn(h, w_ref[k]) for k in range(N_CHIPS)], axis=1)
        za_ref[...] = z[:, :ZA_W]
        zb_ref[...] = z[:, ZA_W:ZA_W + ZB_W]
        zc_ref[...] = z[:, ZA_W + ZB_W:]

    return pl.pallas_call(
        body, grid=(s // tb,), name="qkv_fwd",
        in_specs=[_rows(tb, d), _whole((1, d)), _layer((N_CHIPS, d, IN_WIDTH // N_CHIPS), l)],
        out_specs=[_rows(tb, d), _rows(tb, ZA_W), _rows(tb, ZB_W), _rows(tb, ZC_W)],
        out_shape=[jax.ShapeDtypeStruct((s, d), BF), jax.ShapeDtypeStruct((s, ZA_W), F32),
                   jax.ShapeDtypeStruct((s, ZB_W), F32), jax.ShapeDtypeStruct((s, ZC_W), F32)],
        compiler_params=_cparams("parallel"),
    )(x, g, w_all)


N_STRIPS = 16


def _strips(a):
    s, w = a.shape
    return a.reshape(4, 4, s // N_STRIPS, w)


P_ROWS = {16: TQ, 4: 32, 1: 8}


def _p_sub(s, dil, most):
    while (s // dil // TQ) % most:
        most //= 2
    return most


def _p_grid(s, dil, n_sub):
    nb = s // dil // TQ // n_sub
    return {16: (4, 4, nb), 4: (4, nb), 1: (nb,)}[dil]


def _p_spec(dil, cw, col, n_sub, prev=False):
    rows = P_ROWS[dil] * (1 if prev else n_sub)

    def blk(j):
        return jnp.maximum(n_sub * j - 1, 0) if prev else j
    if dil == 16:
        return pl.BlockSpec((None, None, rows, cw), lambda f, e, j: (f, e, blk(j), col))
    if dil == 4:
        return pl.BlockSpec((None, 4, rows, cw), lambda f, j: (f, 0, blk(j), col))
    return pl.BlockSpec((4, 4, rows, cw), lambda j: (0, 0, blk(j), col))


def _block_pos(i, dil):
    if dil == 16:
        return i
    if dil == 4:
        return 4 * (i % 32) + i // 32
    return 16 * (i % 8) + 4 * ((i // 8) % 4) + i // 32


def _band_mask(b, dil, max_dist):
    qi = _block_pos(lax.broadcasted_iota(jnp.int32, (TQ, 2 * TQ), 0), dil)
    col = lax.broadcasted_iota(jnp.int32, (TQ, 2 * TQ), 1)
    cur = col >= TQ
    dist = qi - _block_pos(col % TQ, dil) + jnp.where(cur, 0, TQ)
    return (dist >= 0) & (dist <= max_dist) & (cur | (b > 0))


def _hs(h):
    return slice(h * HEAD_DIM, (h + 1) * HEAD_DIM)


def _ld(ref, cols, rows=slice(None)):
    v = ref[..., rows, cols]
    return v.reshape(TQ, v.shape[-1])


def _st(ref, cols, val, rows=slice(None)):
    lead = ref.shape[:-2] + (ref.shape[-2] if rows == slice(None) else rows.stop - rows.start,)
    ref[..., rows, cols] = val.reshape(lead + (val.shape[-1],))


def _attn_fwd(z, dil, kw, kcol, vcol, n_rep, max_dist, name):
    s, zw = z.shape
    n_sub = _p_sub(s, dil, 2)
    grid = _p_grid(s, dil, n_sub)

    def body(q_ref, kp_ref, kc_ref, vp_ref, vc_ref, o_ref, lse_ref):
        for t in range(n_sub):
            rows = slice(t * P_ROWS[dil], (t + 1) * P_ROWS[dil])
            before = (slice(None),) if t == 0 else (slice((t - 1) * P_ROWS[dil], t * P_ROWS[dil]),)
            kb_ref, vb_ref = (kp_ref, vp_ref) if t == 0 else (kc_ref, vc_ref)
            mask = _band_mask(n_sub * pl.program_id(len(grid) - 1) if t == 0 else 1, dil, max_dist)
            scs, v2s = [], []
            for kh in range(N_HEADS // n_rep):
                k2 = jnp.concatenate([_ld(kb_ref, _hs(kh), *before), _ld(kc_ref, _hs(kh), rows)], axis=0).astype(BF)
                v2s.append(jnp.concatenate([_ld(vb_ref, _hs(kh), *before), _ld(vc_ref, _hs(kh), rows)],
                                           axis=0).astype(BF))
                for h in range(kh * n_rep, (kh + 1) * n_rep):
                    q = (_ld(q_ref, _hs(h), rows) * SCALE).astype(BF)
                    scs.append(jnp.where(mask, _nt(q, k2), NEG))
            for h, sc in enumerate(scs):
                m = jnp.max(sc, axis=1, keepdims=True)
                p = jnp.exp(sc - m)
                l = jnp.sum(p, axis=1, keepdims=True)
                _st(o_ref, _hs(h), _nn(p.astype(BF), v2s[h // n_rep]) / l, rows)
                _st(lse_ref, _hs(h), jnp.broadcast_to(m + jnp.log(l), (TQ, HEAD_DIM)), rows)

    res = pl.pallas_call(
        body, grid=grid, name=name,
        in_specs=[_p_spec(dil, A_WIDTH, 0, n_sub), _p_spec(dil, kw, kcol, n_sub, True), _p_spec(dil, kw, kcol, n_sub),
                  _p_spec(dil, kw, vcol, n_sub, True), _p_spec(dil, kw, vcol, n_sub)],
        out_specs=[_p_spec(dil, A_WIDTH, 0, n_sub)] * 2,
        out_shape=[jax.ShapeDtypeStruct((4, 4, s // N_STRIPS, A_WIDTH), F32)] * 2,
        compiler_params=_cparams(*(("parallel",) * len(grid))),
    )(*[_strips(z)] * 5)
    return [a.reshape(s, A_WIDTH) for a in res]


def _attn_merge(parts_a, part_c, sink_row, tb):
    s = part_c[0].shape[0]
    n_a = len(parts_a)

    def body(*refs):
        ins, sink_ref = refs[:2 * n_a + 2], refs[2 * n_a + 2]
        ya_ref, lsea_ref, yc_ref, lsec_ref = refs[2 * n_a + 3:]
        lses = [ins[2 * p + 1][...] for p in range(n_a)]
        m = functools.reduce(jnp.maximum, lses)
        ws = [jnp.exp(v - m) for v in lses]
        l = functools.reduce(jnp.add, ws)
        ya_ref[...] = functools.reduce(jnp.add, [w * ins[2 * p][...] for p, w in enumerate(ws)]) / l
        lsea_ref[...] = m + jnp.log(l)
        o_c, lse_c = [r[...] for r in ins[2 * n_a:]]
        sk = sink_ref[...]
        m2 = jnp.maximum(lse_c, sk)
        w = jnp.exp(lse_c - m2)
        l2 = w + jnp.exp(sk - m2)
        yc_ref[...] = o_c * (w / l2)
        lsec_ref[...] = m2 + jnp.log(l2)

    return pl.pallas_call(
        body, grid=(s // tb,), name="attn_merge",
        in_specs=[_rows(tb, A_WIDTH)] * (2 * n_a + 2) + [_whole((1, A_WIDTH))],
        out_specs=[_rows(tb, A_WIDTH)] * 4, out_shape=[jax.ShapeDtypeStruct((s, A_WIDTH), F32)] * 4,
        compiler_params=_cparams("parallel"),
    )(*[a for part in parts_a + [part_c] for a in part], sink_row)


def _shift_down(v, n, halo):
    rows = v.shape[0]
    out = pltpu.roll(v, n, 0)
    row = lax.broadcasted_iota(jnp.int32, v.shape, 0)
    for t in range(n):
        out = jnp.where(row == t, halo[HALO - n + t:HALO - n + t + 1, :], out)
    return out


def _shift_up(v, n, halo):
    rows = v.shape[0]
    out = pltpu.roll(v, rows - n, 0)
    row = lax.broadcasted_iota(jnp.int32, v.shape, 0)
    for t in range(n):
        out = jnp.where(row == rows - n + t, halo[t:t + 1, :], out)
    return out


def _strip(v, b):
    return v[b % 4, b // 4]


def _conv_strips(zb, prev, cw):
    gb = [_strip(zb, b)[:, :CONV_CH] for b in range(N_STRIPS)]
    gc = [_strip(zb, b)[:, CONV_CH:2 * CONV_CH] for b in range(N_STRIPS)]
    xb = [_strip(zb, b)[:, 2 * CONV_CH:] for b in range(N_STRIPS)]
    u = [g * v for g, v in zip(gc, xb)]
    uh = prev[:, :, CONV_CH:2 * CONV_CH] * prev[:, :, 2 * CONV_CH:]
    wrapped = {14: _shift_down(u[14], 1, uh[2]), 15: _shift_down(u[15], 1, uh[3])}
    u1 = [u[b - 1] if b >= 1 else wrapped[15] for b in range(N_STRIPS)]
    u2 = [u[b - 2] if b >= 2 else wrapped[14 + b] for b in range(N_STRIPS)]
    c = [cw[0:1, :] * u2[b] + cw[1:2, :] * u1[b] + cw[2:3, :] * u[b] for b in range(N_STRIPS)]
    return gb, gc, xb, u, u1, u2, c


def _strip_rows(ta, w):
    return pl.BlockSpec((4, 4, ta, w), lambda i: (0, 0, i, 0))


def _prev_rows(ta, w):
    return pl.BlockSpec((4, None, HALO, w), lambda i: (0, 3, jnp.maximum(i * (ta // HALO) - 1, 0), 0))


def _next_rows(ta, w, nblk):
    return pl.BlockSpec((4, None, HALO, w),
                        lambda i: (0, 0, jnp.minimum((i + 1) * (ta // HALO), nblk * (ta // HALO) - 1), 0))


def _mix_fwd(x, ya, yc, zb, cw, gg, wo_all, l, tb):
    s, d = x.shape
    ta = tb // N_STRIPS

    def body(x_ref, ya_ref, yc_ref, zb_ref, zbp_ref, cw_ref, gg_ref, wo_ref, x1_ref, yb_ref):
        i = pl.program_id(0)
        prev = jnp.where(i > 0, zbp_ref[...], 0.0)
        gb, _, _, _, _, _, c = _conv_strips(zb_ref[...], prev, cw_ref[...])
        for b in range(N_STRIPS):
            yb_ref[b % 4, b // 4] = gb[b] * c[b]
        yb = yb_ref[...].reshape(tb, CONV_CH)
        ya, yc = ya_ref[...].reshape(tb, A_WIDTH), yc_ref[...].reshape(tb, A_WIDTH)
        n = jnp.concatenate([ya * _rms_scale(ya), yb * _rms_scale(yb), yc * _rms_scale(yc)], axis=1)
        n = (n * gg_ref[...]).astype(BF)
        x1 = x_ref[...].reshape(tb, d) + _nn(n, wo_ref[...].reshape(MIX_WIDTH, d))
        x1_ref[...] = x1.reshape(4, 4, ta, d)

    res = pl.pallas_call(
        body, grid=(s // tb,), name="mix_fwd",
        in_specs=[_strip_rows(ta, d), _strip_rows(ta, A_WIDTH), _strip_rows(ta, A_WIDTH), _strip_rows(ta, ZB_W),
                  _prev_rows(ta, ZB_W), _whole((HALO, CONV_CH)), _whole((1, MIX_WIDTH)),
                  _layer((N_CHIPS, MIX_WIDTH // N_CHIPS, d), l)],
        out_specs=[_strip_rows(ta, d), _strip_rows(ta, CONV_CH)],
        out_shape=[jax.ShapeDtypeStruct((4, 4, s // N_STRIPS, d), F32),
                   jax.ShapeDtypeStruct((4, 4, s // N_STRIPS, CONV_CH), F32)],
        compiler_params=_cparams("parallel"),
    )(_strips(x), _strips(ya), _strips(yc), _strips(zb), _strips(zb), cw, gg, wo_all)
    return res[0].reshape(s, d), res[1].reshape(s, CONV_CH)


def _mlp_fwd(x1, g, w1_all, w2_all, l, tb, tf):
    s, d = x1.shape
    ff = w1_all.shape[1] * w1_all.shape[3]
    nj = ff // tf

    def body(x_ref, g_ref, w1_ref, w2_ref, x2_ref, h2_ref, ap_ref, acc):
        j = pl.program_id(1)

        @pl.when(j == 0)
        def _():
            xv = x_ref[...]
            h2_ref[...] = ((xv * _rms_scale(xv)) * g_ref[...]).astype(BF)
            acc[...] = jnp.zeros_like(acc)

        ap = _nn(h2_ref[...], w1_ref[...])
        ap_ref[...] = ap.astype(BF)
        a = jnp.square(jnp.maximum(ap, 0.0)).astype(BF)
        acc[...] += _nn(a, w2_ref[...])

        @pl.when(j == nj - 1)
        def _():
            x2_ref[...] = x_ref[...] + acc[...]

    return pl.pallas_call(
        body, grid=(s // tb, nj), name="mlp_fwd",
        in_specs=[pl.BlockSpec((tb, d), lambda i, j: (i, 0)), _whole((1, d)),
                  pl.BlockSpec((None, None, d, tf), lambda i, j: (l, j, 0, 0)),
                  pl.BlockSpec((None, None, tf, d), lambda i, j: (l, j, 0, 0))],
        out_specs=[pl.BlockSpec((tb, d), lambda i, j: (i, 0)), pl.BlockSpec((tb, d), lambda i, j: (i, 0)),
                   pl.BlockSpec((tb, tf), lambda i, j: (i, j))],
        out_shape=[jax.ShapeDtypeStruct((s, d), F32), jax.ShapeDtypeStruct((s, d), BF),
                   jax.ShapeDtypeStruct((s, ff), BF)],
        scratch_shapes=[pltpu.VMEM((tb, d), F32)],
        compiler_params=_cparams("parallel", "arbitrary"),
    )(x1, g, w1_all, w2_all)


def _loss_head(x, g, tgt, tb):
    s, d = x.shape

    def body(x_ref, g_ref, t_ref, dx_ref, loss_ref, dg_ref):
        i = pl.program_id(0)

        @pl.when(i == 0)
        def _():
            loss_ref[...] = jnp.zeros_like(loss_ref)
            dg_ref[...] = jnp.zeros_like(dg_ref)

        xv = x_ref[...]
        r = _rms_scale(xv)
        xhat = xv * r
        err = xhat * g_ref[...] - t_ref[...]
        part = jnp.sum(jnp.mean(jnp.square(err), axis=-1, keepdims=True), axis=0, keepdims=True)
        loss_ref[...] += 0.5 * part
        dy = err * (1.0 / d)
        dg_ref[...] += jnp.sum(dy * xhat, axis=0, keepdims=True)
        dx_ref[...] = _norm_bwd(dy * g_ref[...], xhat, r)

    return pl.pallas_call(
        body, grid=(s // tb,), name="loss_head",
        in_specs=[_rows(tb, d), _whole((1, d)), _rows(tb, d)],
        out_specs=[_rows(tb, d), _whole((HALO, 128)), _whole((HALO, d))],
        out_shape=[jax.ShapeDtypeStruct((s, d), F32), jax.ShapeDtypeStruct((HALO, 128), F32),
                   jax.ShapeDtypeStruct((HALO, d), F32)],
        compiler_params=_cparams("arbitrary"),
    )(x, g, tgt)


def _mlp_bwd(dx2, x1, ap, g, w1_all, w2_all, l, tb, tf):
    s, d = x1.shape
    ff = ap.shape[1]
    nj = ff // tf

    def body(dx2_ref, x1_ref, ap_ref, g_ref, w1_ref, w2_ref, dx1_ref, dap_ref, dg_ref, acc):
        i, j = pl.program_id(0), pl.program_id(1)

        @pl.when((i == 0) & (j == 0))
        def _():
            dg_ref[...] = jnp.zeros_like(dg_ref)

        @pl.when(j == 0)
        def _():
            acc[...] = jnp.zeros_like(acc)

        da = _nt(dx2_ref[...].astype(BF), w2_ref[...])
        dap = (da * (2.0 * jnp.maximum(ap_ref[...].astype(F32), 0.0))).astype(BF)
        dap_ref[...] = dap
        acc[...] += _nt(dap, w1_ref[...])

        @pl.when(j == nj - 1)
        def _():
            xv = x1_ref[...]
            r = _rms_scale(xv)
            xhat = xv * r
            dh = acc[...]
            dg_ref[...] += jnp.sum(dh * xhat, axis=0, keepdims=True)
            dx1_ref[...] = dx2_ref[...] + _norm_bwd(dh * g_ref[...], xhat, r)

    return pl.pallas_call(
        body, grid=(s // tb, nj), name="mlp_bwd",
        in_specs=[pl.BlockSpec((tb, d), lambda i, j: (i, 0)), pl.BlockSpec((tb, d), lambda i, j: (i, 0)),
                  pl.BlockSpec((tb, tf), lambda i, j: (i, j)),
                  _whole((1, d)), pl.BlockSpec((None, None, d, tf), lambda i, j: (l, j, 0, 0)),
                  pl.BlockSpec((None, None, tf, d), lambda i, j: (l, j, 0, 0))],
        out_specs=[pl.BlockSpec((tb, d), lambda i, j: (i, 0)), pl.BlockSpec((tb, tf), lambda i, j: (i, j)),
                   _whole((HALO, d))],
        out_shape=[jax.ShapeDtypeStruct((s, d), F32), jax.ShapeDtypeStruct((s, ff), BF),
                   jax.ShapeDtypeStruct((HALO, d), F32)],
        scratch_shapes=[pltpu.VMEM((tb, d), F32)],
        compiler_params=_cparams("arbitrary", "arbitrary"),
    )(dx2, x1, ap, g, w1_all, w2_all)


def _wgrad(a, b, tm, tn, ts, name, relu2=False):
    s, m = a.shape
    n = b.shape[1]
    ns = s // ts

    def body(a_ref, b_ref, o_ref, acc):
        k = pl.program_id(2)

        @pl.when(k == 0)
        def _():
            acc[...] = jnp.zeros_like(acc)

        av = a_ref[...]
        if relu2:
            av = jnp.square(jnp.maximum(av.astype(F32), 0.0)).astype(BF)
        acc[...] += _tn(av, b_ref[...].astype(BF))

        @pl.when(k == ns - 1)
        def _():
            o_ref[...] = acc[...].astype(BF)

    return pl.pallas_call(
        body, grid=(m // tm, n // tn, ns), name=name,
        in_specs=[pl.BlockSpec((ts, tm), lambda i, j, k: (k, i)), pl.BlockSpec((ts, tn), lambda i, j, k: (k, j))],
        out_specs=pl.BlockSpec((tm, tn), lambda i, j, k: (i, j)),
        out_shape=jax.ShapeDtypeStruct((m, n), BF),
        scratch_shapes=[pltpu.VMEM((tm, tn), F32)],
        compiler_params=_cparams("parallel", "parallel", "arbitrary"),
    )(a, b)


def _mix_bwd(dx1, ya, yb, yc, lse_c, sink_row, gg, wo_all, l, tb):
    s, d = dx1.shape

    def body(dx_ref, ya_ref, yb_ref, yc_ref, lse_ref, sink_ref, gg_ref, wo_ref,
             n_ref, dya_ref, dyc_ref, da_ref, dc_ref, dyb_ref, dg_ref, dsink_ref):
        i = pl.program_id(0)

        @pl.when(i == 0)
        def _():
            dg_ref[...] = jnp.zeros_like(dg_ref)
            dsink_ref[...] = jnp.zeros_like(dsink_ref)

        dn = _nt(dx_ref[...].astype(BF), wo_ref[...].reshape(MIX_WIDTH, d))
        ys = [ya_ref[...], yb_ref[...], yc_ref[...]]
        rs = [_rms_scale(v) for v in ys]
        nhat = jnp.concatenate([v * r for v, r in zip(ys, rs)], axis=1)
        gg = gg_ref[...]
        n_ref[...] = (nhat * gg).astype(BF)
        dg_ref[...] += jnp.sum(dn * nhat, axis=0, keepdims=True)
        dnh = dn * gg
        bounds = [(0, A_WIDTH), (A_WIDTH, A_WIDTH + CONV_CH), (A_WIDTH + CONV_CH, MIX_WIDTH)]
        dys = [_norm_bwd(dnh[:, lo:hi], nhat[:, lo:hi], r) for (lo, hi), r in zip(bounds, rs)]
        dyb_ref[...] = dys[1]
        for dy, y, dy_ref, dd_ref in ((dys[0], ys[0], dya_ref, da_ref), (dys[2], ys[2], dyc_ref, dc_ref)):
            dy_ref[...] = dy
            t = dy * y
            for h in range(N_HEADS):
                dd_ref[:, _hs(h)] = jnp.broadcast_to(jnp.sum(t[:, _hs(h)], axis=1, keepdims=True), (tb, HEAD_DIM))
        dsink_ref[...] -= jnp.sum(jnp.exp(sink_ref[...] - lse_ref[...]) * dc_ref[...], axis=0, keepdims=True)

    return pl.pallas_call(
        body, grid=(s // tb,), name="mix_bwd",
        in_specs=[_rows(tb, d), _rows(tb, A_WIDTH), _rows(tb, CONV_CH), _rows(tb, A_WIDTH), _rows(tb, A_WIDTH),
                  _whole((1, A_WIDTH)), _whole((1, MIX_WIDTH)), _layer((N_CHIPS, MIX_WIDTH // N_CHIPS, d), l)],
        out_specs=[_rows(tb, MIX_WIDTH), _rows(tb, A_WIDTH), _rows(tb, A_WIDTH), _rows(tb, A_WIDTH),
                   _rows(tb, A_WIDTH), _rows(tb, CONV_CH), _whole((HALO, MIX_WIDTH)), _whole((HALO, A_WIDTH))],
        out_shape=[jax.ShapeDtypeStruct((s, MIX_WIDTH), BF), jax.ShapeDtypeStruct((s, A_WIDTH), F32),
                   jax.ShapeDtypeStruct((s, A_WIDTH), F32), jax.ShapeDtypeStruct((s, A_WIDTH), F32),
                   jax.ShapeDtypeStruct((s, A_WIDTH), F32), jax.ShapeDtypeStruct((s, CONV_CH), F32),
                   jax.ShapeDtypeStruct((HALO, MIX_WIDTH), F32), jax.ShapeDtypeStruct((HALO, A_WIDTH), F32)],
        compiler_params=_cparams("arbitrary"),
    )(dx1, ya, yb, yc, lse_c, sink_row, gg, wo_all)


def _attn_bwd(z, dy, lse, dd, dil, kw, kcol, vcol, n_rep, max_dist, name):
    s, zw = z.shape
    n_sub = _p_sub(s, dil, 2) if n_rep == 1 else 1
    grid = _p_grid(s, dil, n_sub)
    n_kv = N_HEADS // n_rep
    dt = F32 if dil == 1 else BF

    def body(q_ref, kp_ref, kc_ref, vp_ref, vc_ref, dy_ref, lse_ref, dd_ref, dq_ref, dkp_ref, dkc_ref, dvp_ref, dvc_ref):
        for t in range(n_sub):
            rows = slice(t * P_ROWS[dil], (t + 1) * P_ROWS[dil])
            before = (slice(None),) if t == 0 else (slice((t - 1) * P_ROWS[dil], t * P_ROWS[dil]),)
            kb_ref, vb_ref = (kp_ref, vp_ref) if t == 0 else (kc_ref, vc_ref)
            mask = _band_mask(n_sub * pl.program_id(len(grid) - 1) if t == 0 else 1, dil, max_dist)
            k2s, qs, dys, scs, dps = [], [], [], [], []
            for kh in range(n_kv):
                k2s.append(jnp.concatenate([_ld(kb_ref, _hs(kh), *before), _ld(kc_ref, _hs(kh), rows)],
                                           axis=0).astype(BF))
                v2 = jnp.concatenate([_ld(vb_ref, _hs(kh), *before), _ld(vc_ref, _hs(kh), rows)], axis=0).astype(BF)
                for h in range(kh * n_rep, (kh + 1) * n_rep):
                    qs.append((_ld(q_ref, _hs(h), rows) * SCALE).astype(BF))
                    dys.append(_ld(dy_ref, _hs(h), rows).astype(BF))
                    scs.append(jnp.where(mask, _nt(qs[h], k2s[kh]), NEG))
                    dps.append(_nt(dys[h], v2))
            for kh in range(n_kv):
                k2 = k2s[kh]
                dk2 = jnp.zeros((2 * TQ, HEAD_DIM), F32)
                dv2 = jnp.zeros((2 * TQ, HEAD_DIM), F32)
                for h in range(kh * n_rep, (kh + 1) * n_rep):
                    lse_h = _ld(lse_ref, slice(h * HEAD_DIM, h * HEAD_DIM + 1), rows)
                    dd_h = _ld(dd_ref, slice(h * HEAD_DIM, h * HEAD_DIM + 1), rows)
                    p = jnp.exp(scs[h] - lse_h)
                    ds = (p * (dps[h] - dd_h)).astype(BF)
                    _st(dq_ref, _hs(h), (_nn(ds, k2) * SCALE).astype(dt), rows)
                    dk2 = dk2 + _tn(ds, qs[h])
                    dv2 = dv2 + _tn(p.astype(BF), dys[h])
                _st(dkp_ref, _hs(kh), dk2[:TQ].astype(dt), rows)
                _st(dkc_ref, _hs(kh), dk2[TQ:].astype(dt), rows)
                _st(dvp_ref, _hs(kh), dv2[:TQ].astype(dt), rows)
                _st(dvc_ref, _hs(kh), dv2[TQ:].astype(dt), rows)

    args = [_strips(z)] * 5 + [_strips(a) for a in (dy, lse, dd)]
    pair = _p_spec(dil, A_WIDTH, 0, n_sub)
    in_specs = [pair, _p_spec(dil, kw, kcol, n_sub, True), _p_spec(dil, kw, kcol, n_sub),
                _p_spec(dil, kw, vcol, n_sub, True), _p_spec(dil, kw, vcol, n_sub)] + [pair] * 3
    out_specs = [pair] + [_p_spec(dil, kw, 0, n_sub)] * 4
    na = s // N_STRIPS
    out_shape = [jax.ShapeDtypeStruct((4, 4, na, A_WIDTH), dt)] + [jax.ShapeDtypeStruct((4, 4, na, kw), dt)] * 4
    res = pl.pallas_call(
        body, grid=grid, name=name, in_specs=in_specs, out_specs=out_specs, out_shape=out_shape,
        compiler_params=_cparams(*(("parallel",) * len(grid))),
    )(*args)
    return [res[0].reshape(s, A_WIDTH)] + [a.reshape(s, kw) for a in res[1:]]


DZ_TA = 16


def _dz_assemble(parts_a, parts_c, dyb, zb, cw):
    s = zb.shape[0]
    na = s // N_STRIPS
    nb = na // DZ_TA

    def ahead(w, k):
        return pl.BlockSpec((4, 4, DZ_TA, w), lambda i: (0, 0, jnp.minimum(i + k, nb - 1), 0))

    args, in_specs = [], []
    for dil, (dq, dkp, dkc, dvp, dvc) in zip(DILATIONS + (1,), parts_a + [parts_c]):
        w = dkp.shape[1]
        here = _strip_rows(DZ_TA, w)
        if dil == 1:
            args += [dq, dkp, dkp, dkc, dvp, dvp, dvc]
            in_specs += [_strip_rows(DZ_TA, A_WIDTH), here, ahead(w, 1), here, here, ahead(w, 1), here]
        else:
            k = 8 * dil // DZ_TA
            args += [dq, dkp, dkc, dvp, dvc]
            in_specs += [_strip_rows(DZ_TA, A_WIDTH), ahead(w, k), here, ahead(w, k), here]
    n_att = len(args)
    args = [_strips(a) for a in args] + [_strips(dyb), _strips(dyb), _strips(zb), _strips(zb), _strips(zb), cw]
    in_specs += [_strip_rows(DZ_TA, CONV_CH), _next_rows(DZ_TA, CONV_CH, nb), _strip_rows(DZ_TA, ZB_W),
                 _prev_rows(DZ_TA, ZB_W), _next_rows(DZ_TA, ZB_W, nb), _whole((HALO, CONV_CH))]

    def body(*refs):
        att = list(refs[:n_att])
        dyb_ref, dybn_ref, zb_ref, zbp_ref, zbn_ref, cw_ref, dz_ref, dcw_ref = refs[n_att:]
        i = pl.program_id(0)

        @pl.when(i == 0)
        def _():
            dcw_ref[...] = jnp.zeros_like(dcw_ref)

        def shifted(dil):
            if dil == 1:
                dq_r, kp0, kp1, dkc_r, vp0, vp1, dvc_r = [att.pop(0) for _ in range(7)]
                live = i + 1 < nb
                half = DZ_TA // 2
                dkp = jnp.concatenate([kp0[:, :, half:, :], jnp.where(live, kp1[:, :, :half, :], 0.0)], axis=2)
                dvp = jnp.concatenate([vp0[:, :, half:, :], jnp.where(live, vp1[:, :, :half, :], 0.0)], axis=2)
            else:
                dq_r, dkp_r, dkc_r, dvp_r, dvc_r = [att.pop(0) for _ in range(5)]
                live = i + 8 * dil // DZ_TA < nb
                dkp = jnp.where(live, dkp_r[...].astype(F32), 0.0)
                dvp = jnp.where(live, dvp_r[...].astype(F32), 0.0)
            return dq_r[...].astype(F32), dkc_r[...].astype(F32) + dkp, dvc_r[...].astype(F32) + dvp

        dq, dk, dv = shifted(DILATIONS[0])
        for dil in DILATIONS[1:]:
            dq2, dk2, dv2 = shifted(dil)
            dq, dk, dv = dq + dq2, dk + dk2, dv + dv2
        dz_ref[:, :, :, 0:A_WIDTH] = dq.astype(BF)
        dz_ref[:, :, :, A_WIDTH:2 * A_WIDTH] = dk.astype(BF)
        dz_ref[:, :, :, 2 * A_WIDTH:ZA_W] = dv.astype(BF)
        dq, dk, dv = shifted(1)
        c0 = ZA_W + ZB_W
        dz_ref[:, :, :, c0:c0 + A_WIDTH] = dq.astype(BF)
        dz_ref[:, :, :, c0 + A_WIDTH:c0 + A_WIDTH + C_KV_WIDTH] = dk.astype(BF)
        dz_ref[:, :, :, c0 + A_WIDTH + C_KV_WIDTH:IN_WIDTH] = dv.astype(BF)

        cw = cw_ref[...]
        prev = jnp.where(i > 0, zbp_ref[...], 0.0)
        gb, gc, xb, u, u1, u2, c = _conv_strips(zb_ref[...], prev, cw)
        dyb = dyb_ref[...]
        dc = [_strip(dyb, b) * gb[b] for b in range(N_STRIPS)]
        dcn = jnp.where(i + 1 < nb, dybn_ref[...] * zbn_ref[:, :, :CONV_CH], 0.0)
        wrapped = [_shift_up(dc[0], 1, dcn[0]), _shift_up(dc[1], 1, dcn[1])]
        upd = [jnp.zeros((1, CONV_CH), F32)] * 3
        for b in range(N_STRIPS):
            dc1 = dc[b + 1] if b + 1 < N_STRIPS else wrapped[0]
            dc2 = dc[b + 2] if b + 2 < N_STRIPS else wrapped[b + 2 - N_STRIPS]
            du = cw[2:3, :] * dc[b] + cw[1:2, :] * dc1 + cw[0:1, :] * dc2
            f, e = b % 4, b // 4
            dz_ref[f, e, :, ZA_W:ZA_W + CONV_CH] = (_strip(dyb, b) * c[b]).astype(BF)
            dz_ref[f, e, :, ZA_W + CONV_CH:ZA_W + 2 * CONV_CH] = (du * xb[b]).astype(BF)
            dz_ref[f, e, :, ZA_W + 2 * CONV_CH:c0] = (du * gc[b]).astype(BF)
            for t, uu in enumerate((u2[b], u1[b], u[b])):
                upd[t] = upd[t] + jnp.sum(dc[b] * uu, axis=0, keepdims=True)
        row = lax.broadcasted_iota(jnp.int32, (HALO, CONV_CH), 0)
        tile = jnp.zeros((HALO, CONV_CH), F32)
        for t in range(3):
            tile = jnp.where(row == t, upd[t], tile)
        dcw_ref[...] += tile

    dz, dcw = pl.pallas_call(
        body, grid=(nb,), name="dz_assemble", in_specs=in_specs,
        out_specs=[_strip_rows(DZ_TA, IN_WIDTH), _whole((HALO, CONV_CH))],
        out_shape=[jax.ShapeDtypeStruct((4, 4, na, IN_WIDTH), BF), jax.ShapeDtypeStruct((HALO, CONV_CH), F32)],
        compiler_params=_cparams("arbitrary"),
    )(*args)
    return dz.reshape(s, IN_WIDTH), dcw


def _qkv_bwd(dz, dx1, x, g, w_all, l, tb, tokens_out):
    s, d = x.shape
    na, ta = s // N_STRIPS, tb // N_STRIPS

    def body(dz_ref, dx1_ref, x_ref, g_ref, w_ref, dx_ref, dg_ref):
        i = pl.program_id(0)

        @pl.when(i == 0)
        def _():
            dg_ref[...] = jnp.zeros_like(dg_ref)

        n = IN_WIDTH // N_CHIPS
        dz = dz_ref[...].reshape(tb, IN_WIDTH)
        dh = _nt(dz[:, 0:n], w_ref[0])
        for k in range(1, N_CHIPS):
            dh = dh + _nt(dz[:, k * n:(k + 1) * n], w_ref[k])
        xv = x_ref[...].reshape(tb, d)
        r = _rms_scale(xv)
        xhat = xv * r
        dg_ref[...] += jnp.sum(dh * xhat, axis=0, keepdims=True)
        dx = (dx1_ref[...].reshape(tb, d) + _norm_bwd(dh * g_ref[...], xhat, r)).reshape(4, 4, ta, d)
        if tokens_out:
            for b in range(N_STRIPS):
                dx_ref[:, b, :] = _strip(dx, b)
        else:
            dx_ref[...] = dx

    if tokens_out:
        dx_spec, dx_shape = pl.BlockSpec((ta, N_STRIPS, d), lambda i: (i, 0, 0)), (na, N_STRIPS, d)
    else:
        dx_spec, dx_shape = _strip_rows(ta, d), (4, 4, na, d)
    dx, dg = pl.pallas_call(
        body, grid=(s // tb,), name="qkv_bwd",
        in_specs=[_strip_rows(ta, IN_WIDTH), _strip_rows(ta, d), _strip_rows(ta, d), _whole((1, d)),
                  _layer((N_CHIPS, d, IN_WIDTH // N_CHIPS), l)],
        out_specs=[dx_spec, _whole((HALO, d))],
        out_shape=[jax.ShapeDtypeStruct(dx_shape, F32), jax.ShapeDtypeStruct((HALO, d), F32)],
        compiler_params=_cparams("arbitrary"),
    )(_strips(dz), _strips(dx1), _strips(x), g, w_all)
    return dx.reshape(s, d), dg


def _tile_rows(rows):
    return jnp.pad(rows, ((0, HALO - rows.shape[0]), (0, 0)))


def _to_strips(a, after, name):
    s, d = a.shape
    na = s // N_STRIPS
    ta = min(32, na)

    def body(a_ref, *rest):
        for b in range(N_STRIPS):
            rest[-1][b % 4, b // 4] = a_ref[:, b, :]

    return pl.pallas_call(
        body, grid=(na // ta,), name=name,
        in_specs=[pl.BlockSpec((ta, N_STRIPS, d), lambda i: (i, 0, 0))] + [ANY] * len(after),
        out_specs=_strip_rows(ta, d),
        out_shape=jax.ShapeDtypeStruct((4, 4, na, d), a.dtype), compiler_params=_cparams("parallel"),
    )(a.reshape(na, N_STRIPS, d), *after).reshape(s, d)


def _local_step(x, tgt, fetch, ff, sinks, g_mix, g_group, g_mlp, g_final, emit):
    s, d = x.shape
    depth = g_mix.shape[0]
    tb = min(512, s)
    tf = ff // N_CHIPS
    ts = min(1024, s)
    saved = []
    for l in range(depth):
        w_in, _, _, _, conv_w = fetch(0, l, x)
        cw = _tile_rows(conv_w[l])
        sk = jnp.repeat(sinks[l].reshape(N_HEADS), HEAD_DIM)[None]
        h, za, zb, zc = _qkv_fwd(x, g_mix[l][None], w_in, l, tb)
        parts_a = [_attn_fwd(za, dil, A_WIDTH, 1, 2, 1, A_MAX_DIST, "attn_a_fwd_%d" % dil) for dil in DILATIONS]
        part_c = _attn_fwd(zc, 1, C_KV_WIDTH, 3, 4, C_GROUP, C_MAX_DIST, "attn_c_fwd")
        ya, lse_a, yc, lse_c = _attn_merge(parts_a, part_c, sk, tb)
        w_in, w_o, w1, w2, _ = fetch(1, l, yc)
        x1, yb = _mix_fwd(x, ya, yc, zb, cw, g_group[l][None], w_o, l, tb)
        x2, h2, ap = _mlp_fwd(x1, g_mlp[l][None], w1, w2, l, ts, tf)
        saved.append((x, h, za, zb, zc, ya, lse_a, yc, lse_c, yb, x1, h2, ap, cw, sk))
        x = x2
    dx, loss_tile, dg_final = _loss_head(x, g_final[None], tgt, tb)
    grads = [None] * depth
    tok = jnp.zeros((), F32)
    for l in reversed(range(depth)):
        x0, h, za, zb, zc, ya, lse_a, yc, lse_c, yb, x1, h2, ap, cw, sk = saved[l]
        dx1, dap, dg_mlp = _mlp_bwd(dx, x1, ap, g_mlp[l][None] + tok, w1, w2, l, ts, tf)
        tok = emit(l, 3, _wgrad(ap, dx, min(1024, ff), d, ts, "wgrad_ff_out", relu2=True))
        tok = tok + emit(l, 2, _wgrad(h2, dap, d, min(1024, ff), 2 * ts, "wgrad_ff_in"))
        n, dya, dyc, dd_a, dd_c, dyb, dg_group, dsink = _mix_bwd(dx1, ya, yb, yc, lse_c, sk, g_group[l][None] + tok,
                                                                 w_o, l, tb)
        tok = emit(l, 1, _wgrad(n, dx1, MIX_WIDTH, d, ts, "wgrad_o"))
        cw = cw + tok
        parts_a = [_attn_bwd(za, dya, lse_a, dd_a, dil, A_WIDTH, 1, 2, 1, A_MAX_DIST, "attn_a_bwd_%d" % dil)
                   for dil in DILATIONS]
        parts_c = _attn_bwd(zc, dyc, lse_c, dd_c, 1, C_KV_WIDTH, 3, 4, C_GROUP, C_MAX_DIST, "attn_c_bwd")
        dz, dcw = _dz_assemble(parts_a, parts_c, dyb, zb, cw)
        tok = emit(l, 0, _wgrad(h, dz, d, IN_WIDTH // 4, 2 * ts, "wgrad_in"))
        dx, dg_mix = _qkv_bwd(dz, dx1, x0, g_mix[l][None] + tok, w_in, l, tb, l == 0)
        grads[l] = (dcw, dsink, dg_mix, dg_group, dg_mlp)
    return loss_tile, dx, grads, dg_final


ANY = pl.BlockSpec(memory_space=pl.ANY)
SHARD_AXES = (2, 1, 2, 1)
N_BIG = len(SHARD_AXES)
N_CHIPS = 4
N_DEV = 8


def _mesh_pos():
    return lax.axis_index("x"), lax.axis_index("y"), lax.axis_index("c")


def _flip(v, bit):
    return 1 - v if bit else v


def _place_shard(shard, chip_arr, name):
    _, rows, cols = shard.shape
    tr = min(256, rows)

    def body(chip_ref, x_ref, o_ref):
        o_ref[...] = x_ref[...].astype(BF)

    return pl.pallas_call(
        body, name=name,
        grid_spec=pltpu.PrefetchScalarGridSpec(
            num_scalar_prefetch=1, grid=(2, rows // tr),
            in_specs=[pl.BlockSpec((None, tr, cols), lambda l, i, chip: (l, i, 0))],
            out_specs=pl.BlockSpec((None, None, tr, cols), lambda l, i, chip: (l, chip[0], i, 0))),
        out_shape=jax.ShapeDtypeStruct((2, N_CHIPS, rows, cols), BF),
        compiler_params=_cparams("parallel", "parallel"),
    )(chip_arr, shard)


HBM = pl.BlockSpec(memory_space=pltpu.HBM)
SEM = pl.BlockSpec(memory_space=pltpu.SEMAPHORE)
EFFECT = pltpu.SideEffectType.DATAFLOW_SIDE_EFFECTING

GATHER_GROUPS = (((0, 0),), ((1, 0), (2, 0), (3, 0)), ((0, 1),), ((1, 1), (2, 1), (3, 1)))
GATHER_STARTS = ((0,), (1,), (2, 3))


def _gather_copies(arrs, group, send_sems, recv_sems):
    x, y, c = _mesh_pos()
    me = 2 * x + y
    out = []
    for i, (w, layer) in enumerate(group):
        mine = arrs[w].at[layer, me]
        for j, (qx, qy) in enumerate([(1 - x, y), (x, 1 - y), (1 - x, 1 - y)]):
            landed = arrs[w].at[layer, 2 * qx + qy]
            out.append(tuple(pltpu.make_async_remote_copy(
                src_ref=piece, dst_ref=piece, send_sem=send_sems.at[i * 3 + j], recv_sem=recv_sems.at[i * 3 + j],
                device_id=(qx, qy, c), device_id_type=MESH) for piece in (mine, landed)))
    return out


def _conv_copies(conv_src, conv_dst, send_sems, recv_sems):
    x, y, c = _mesh_pos()
    out = []
    for j, (qx, qy) in enumerate([(1 - x, y), (x, 1 - y), (1 - x, 1 - y)]):
        out.append(tuple(pltpu.make_async_remote_copy(
            src_ref=conv_src, dst_ref=conv_dst.at[q], send_sem=send_sems.at[j], recv_sem=recv_sems.at[j],
            device_id=(qx, qy, c), device_id_type=MESH) for q in (2 * x + y, 2 * qx + qy)))
    return out


def _gather_start(groups, arrs, conv, name, through=None):
    n_sems = 2 * (len(groups) + (conv is not None))
    mats = sorted({w for g in groups for w, _ in GATHER_GROUPS[g]})

    def body(*refs):
        arrs_ref = [None] * N_BIG
        for w, ref in zip(mats, refs):
            arrs_ref[w] = ref
        sems = refs[n_in:n_in + n_sems]
        if conv is not None:
            for cp, _ in _conv_copies(refs[len(mats)], refs[len(mats) + 1], sems[-2], sems[-1]):
                cp.start()
        for k, g in enumerate(groups):
            for cp, _ in _gather_copies(arrs_ref, GATHER_GROUPS[g], sems[2 * k], sems[2 * k + 1]):
                cp.start()

    sem_shapes = []
    for n in [len(GATHER_GROUPS[g]) for g in groups] + ([1] if conv is not None else []):
        sem_shapes += [pltpu.SemaphoreType.DMA((3 * n,))] * 2
    operands = [arrs[w] for w in mats] + ([] if conv is None else list(conv)) + ([] if through is None else [through])
    n_in = len(operands)
    res = pl.pallas_call(
        body, name=name,
        out_shape=tuple(sem_shapes) + tuple(pltpu.HBM(a.shape, a.dtype) for a in operands),
        in_specs=(HBM,) * n_in, out_specs=(SEM,) * n_sems + (HBM,) * n_in,
        input_output_aliases={i: n_sems + i for i in range(n_in)},
        compiler_params=pltpu.CompilerParams(has_side_effects=EFFECT),
    )(*[pltpu.with_memory_space_constraint(a, pltpu.HBM) for a in operands])
    arrs = list(arrs)
    for w, a in zip(mats, res[n_sems:]):
        arrs[w] = a
    return res[:n_sems], arrs, list(res[n_sems + len(mats):])


def _gather_wait(k, sems, arrs, conv, after, name):
    group = GATHER_GROUPS[k]
    mats = sorted({w for w, _ in group})
    n_conv = 0 if conv is None else 2

    def body(*refs):
        local = refs[:len(mats)]
        arrs_ref = [None] * N_BIG
        for w, ref in zip(mats, local):
            arrs_ref[w] = ref
        pos = len(mats) + n_conv
        copies = _gather_copies(arrs_ref, group, refs[pos], refs[pos + 1])
        if conv is not None:
            copies += _conv_copies(refs[len(mats)], refs[len(mats) + 1], refs[pos + 2], refs[pos + 3])
        for send, recv in copies:
            recv.wait_recv()
            send.wait_send()

    operands = [arrs[w] for w in mats] + ([] if conv is None else [conv[1], conv[2]])
    sem_ops = list(sems) + ([] if conv is None else list(conv[0]))
    n_op = len(operands)
    res = pl.pallas_call(
        body, name=name, out_shape=tuple(pltpu.HBM(a.shape, a.dtype) for a in operands),
        in_specs=(HBM,) * n_op + (SEM,) * len(sem_ops) + (ANY,) * len(after), out_specs=(HBM,) * n_op,
        input_output_aliases={i: i for i in range(n_op)},
        compiler_params=pltpu.CompilerParams(has_side_effects=EFFECT),
    )(*operands, *sem_ops, *after)
    arrs = list(arrs)
    for w, a in zip(mats, res):
        arrs[w] = a
    return arrs, (res[-1] if conv is not None else None)


def _grad_shard(ref, w, chip, n):
    start = pl.multiple_of(chip * n, 128)
    if SHARD_AXES[w] == 2:
        return ref.at[:, pl.ds(start, n)]
    return ref.at[pl.ds(start, n), :]


def _slot_shape(g, w):
    shape = list(g.shape)
    shape[SHARD_AXES[w] - 1] //= N_CHIPS
    return (N_DEV - 1,) + tuple(shape)


def _scatter_copies(g_ref, land_ref, send_sems, recv_sems, layer, w):
    x, y, c = _mesh_pos()
    n = g_ref.shape[SHARD_AXES[w] - 1] // N_CHIPS
    out = []
    for r in range(1, N_DEV):
        tx, ty, tc = _flip(x, r & 4), _flip(y, r & 2), _flip(c, r & 1)
        cp = pltpu.make_async_remote_copy(
            src_ref=_grad_shard(g_ref, w, 2 * tx + ty, n), dst_ref=land_ref.at[r - 1], send_sem=send_sems.at[r - 1],
            recv_sem=recv_sems.at[r - 1], device_id=(tx, ty, tc), device_id_type=MESH)
        out.append((cp, (c != layer) if r & 1 else (c == layer)))
    return out


def _scatter_start(items, layer, name):
    n = len(items)

    def body(*refs):
        for i, (w, _, _) in enumerate(items):
            g_ref, land_ref = refs[2 * i], refs[2 * i + 1]
            send_sems, recv_sems = refs[2 * n + 2 * i], refs[2 * n + 2 * i + 1]
            for cp, mine in _scatter_copies(g_ref, land_ref, send_sems, recv_sems, layer, w):
                @pl.when(mine)
                def _():
                    cp.start()
        refs[-1][...] = jnp.zeros_like(refs[-1])

    operands = [a for _, g, land in items for a in (g, land)]
    res = pl.pallas_call(
        body, name=name,
        out_shape=(pltpu.SemaphoreType.DMA((N_DEV - 1,)),) * (2 * n)
        + tuple(pltpu.HBM(a.shape, a.dtype) for a in operands) + (jax.ShapeDtypeStruct((HALO, 128), F32),),
        in_specs=(HBM,) * (2 * n),
        out_specs=(SEM,) * (2 * n) + (HBM,) * (2 * n) + (pl.BlockSpec(memory_space=pltpu.VMEM),),
        input_output_aliases={i: 2 * n + i for i in range(2 * n)},
        compiler_params=pltpu.CompilerParams(has_side_effects=EFFECT),
    )(*[pltpu.with_memory_space_constraint(a, pltpu.HBM) for a in operands])
    return [(res[2 * i], res[2 * i + 1], res[2 * n + 2 * i], res[2 * n + 2 * i + 1]) for i in range(n)], res[-1]


def _scatter_wait(started, land, after, w, name):
    def body(g0_ref, g1_ref, land_ref, ss0, rs0, ss1, rs1, after_ref, g0_out, g1_out, land_out):
        c = lax.axis_index("c")
        for layer, g_ref, ss, rs in ((0, g0_ref, ss0, rs0), (1, g1_ref, ss1, rs1)):
            for cp, mine in _scatter_copies(g_ref, land_ref, ss, rs, layer, w):
                @pl.when(mine)
                def _():
                    cp.wait_send()

                @pl.when(c == layer)
                def _():
                    cp.wait_recv()

    (ss0, rs0, g0), (ss1, rs1, g1) = started
    return pl.pallas_call(
        body, name=name,
        out_shape=(pltpu.HBM(g0.shape, g0.dtype), pltpu.HBM(g1.shape, g1.dtype), pltpu.HBM(land.shape, land.dtype)),
        in_specs=(HBM, HBM, HBM, SEM, SEM, SEM, SEM, ANY), out_specs=(HBM, HBM, HBM),
        input_output_aliases={0: 0, 1: 1, 2: 2}, compiler_params=pltpu.CompilerParams(has_side_effects=EFFECT),
    )(g0, g1, land, ss0, rs0, ss1, rs1, after)


def _sum_slots(g0, g1, slots, w, pos_arr, name):
    _, rows, cols = slots.shape
    tr = min(256, rows)
    nr = rows // tr
    if SHARD_AXES[w] == 2:
        own = pl.BlockSpec((tr, cols), lambda i, pos: (i, pos[0]))
    else:
        own = pl.BlockSpec((tr, cols), lambda i, pos: (pos[0] * nr + i, 0))

    def body(pos_ref, own0_ref, own1_ref, s_ref, o_ref):
        acc = jnp.where(pos_ref[1] == 0, own0_ref[...], own1_ref[...]).astype(F32)
        for r in range(N_DEV - 1):
            acc = acc + s_ref[r].astype(F32)
        o_ref[...] = acc

    return pl.pallas_call(
        body, name=name,
        grid_spec=pltpu.PrefetchScalarGridSpec(
            num_scalar_prefetch=1, grid=(nr,),
            in_specs=[own, own, pl.BlockSpec((N_DEV - 1, tr, cols), lambda i, pos: (0, i, 0))],
            out_specs=pl.BlockSpec((tr, cols), lambda i, pos: (i, 0))),
        out_shape=jax.ShapeDtypeStruct((rows, cols), F32), compiler_params=_cparams("parallel"),
    )(pos_arr, g0, g1, slots)


def _swap_copies(refs, n):
    x, y, c = _mesh_pos()
    return [pltpu.make_async_remote_copy(src_ref=refs[w], dst_ref=refs[n + w], send_sem=refs[2 * n].at[w],
                                         recv_sem=refs[2 * n + 1].at[w], device_id=(x, y, 1 - c), device_id_type=MESH)
            for w in range(n)]


def _swap_start(halves, name):
    n = len(halves)

    def body(*refs):
        for cp in _swap_copies(refs, n):
            cp.start()

    operands = list(halves) + [lax.empty(h.shape, h.dtype) for h in halves]
    res = pl.pallas_call(
        body, name=name,
        out_shape=(pltpu.SemaphoreType.DMA((n,)),) * 2 + tuple(pltpu.HBM(a.shape, a.dtype) for a in operands),
        in_specs=(HBM,) * (2 * n), out_specs=(SEM,) * 2 + (HBM,) * (2 * n),
        input_output_aliases={i: 2 + i for i in range(2 * n)},
        compiler_params=pltpu.CompilerParams(has_side_effects=EFFECT),
    )(*[pltpu.with_memory_space_constraint(a, pltpu.HBM) for a in operands])
    return res[0], res[1], list(res[2:2 + n]), list(res[2 + n:])


def _swap_wait(send_sems, recv_sems, halves, lands, after, name):
    n = len(halves)

    def body(*refs):
        for cp in _swap_copies(refs, n):
            cp.wait_send()
            cp.wait_recv()

    operands = list(halves) + list(lands)
    res = pl.pallas_call(
        body, name=name, out_shape=tuple(pltpu.HBM(a.shape, a.dtype) for a in operands),
        in_specs=(HBM,) * (2 * n) + (SEM, SEM, ANY), out_specs=(HBM,) * (2 * n),
        input_output_aliases={i: i for i in range(2 * n)},
        compiler_params=pltpu.CompilerParams(has_side_effects=EFFECT),
    )(*operands, send_sems, recv_sems, after)
    return list(res[n:])


def _adamw_math(w, g, m, v):
    m = ADAM_B1 * m + (1.0 - ADAM_B1) * g
    v = ADAM_B2 * v + (1.0 - ADAM_B2) * jnp.square(g)
    m_hat = m / (1.0 - ADAM_B1 ** ADAM_STEP)
    v_hat = v / (1.0 - ADAM_B2 ** ADAM_STEP)
    delta = -ADAM_LR * (m_hat / (jnp.sqrt(v_hat) + ADAM_EPS) + ADAM_WD * w)
    return delta, m, v


def _adamw(w, g, m, v, filled, pos_arr, name):
    shape = w.shape
    _, rows, cols = shape
    tr = min(256, rows)

    def body(pos_ref, w_ref, g_ref, m_ref, v_ref, *rest):
        go_ref, d_ref, m2_ref, v2_ref = rest[-4:]
        g = g_ref[...]
        go_ref[...] = g
        d_ref[...], m2_ref[...], v2_ref[...] = _adamw_math(w_ref[...], g, m_ref[...], v_ref[...])

    def layer(pos):
        return pos[1] if filled is None else 1 - pos[1]

    full = pl.BlockSpec((None, tr, cols), lambda i, pos: (layer(pos), i, 0))
    half = pl.BlockSpec((tr, cols), lambda i, pos: (i, 0))
    n_in = 5
    return pl.pallas_call(
        body, name=name,
        grid_spec=pltpu.PrefetchScalarGridSpec(
            num_scalar_prefetch=1, grid=(rows // tr,),
            in_specs=[full, half, full, full] + ([] if filled is None else [ANY] * 4), out_specs=[full] * 4),
        out_shape=[jax.ShapeDtypeStruct(shape, F32)] * 4,
        input_output_aliases={} if filled is None else {n_in + k: k for k in range(4)},
        compiler_params=_cparams("parallel"),
    )(pos_arr, w, g, m, v, *([] if filled is None else filled))


def _small_sync(part, w, m, v):
    rows, cols = part.shape

    def body(p_ref, w_ref, m_ref, v_ref, g_ref, d_ref, m2_ref, v2_ref, slots, send_sems, recv_sems):
        x, y, c = _mesh_pos()
        me = 4 * x + 2 * y + c
        slots[me] = p_ref[...]
        sends = []
        for r in range(1, N_DEV):
            to = (_flip(x, r & 4), _flip(y, r & 2), _flip(c, r & 1))
            sends.append(pltpu.make_async_remote_copy(
                src_ref=p_ref, dst_ref=slots.at[me], send_sem=send_sems.at[r - 1], recv_sem=recv_sems.at[r - 1],
                device_id=to, device_id_type=MESH))
        for cp in sends:
            cp.start()
        for cp in sends:
            cp.wait_recv()
        for cp in sends:
            cp.wait_send()
        g = slots[0]
        for i in range(1, N_DEV):
            g = g + slots[i]
        g_ref[...] = g
        d_ref[...], m2_ref[...], v2_ref[...] = _adamw_math(w_ref[...], g, m_ref[...], v_ref[...])

    vm = pl.BlockSpec(memory_space=pltpu.VMEM)
    return pl.pallas_call(
        body, name="small_sync", in_specs=[vm] * 4, out_specs=[vm] * 4,
        out_shape=[jax.ShapeDtypeStruct((rows, cols), F32)] * 4,
        scratch_shapes=[pltpu.VMEM((N_DEV, rows, cols), F32), pltpu.SemaphoreType.DMA((N_DEV - 1,)),
                        pltpu.SemaphoreType.DMA((N_DEV - 1,))],
    )(part, w, m, v)


PACK_W = 256


def _pack_rows(n):
    return -(-n // (HALO * PACK_W)) * HALO


def _pack_small(parts):
    out = []
    for a in parts:
        flat = a.reshape(-1)
        out.append(jnp.pad(flat, (0, _pack_rows(flat.size) * PACK_W - flat.size)).reshape(-1, PACK_W))
    return jnp.concatenate(out, axis=0)


def _unpack_small(p, shapes):
    out, row = [], 0
    for shape in shapes:
        n = 1
        for k in shape:
            n *= k
        out.append(p[row:row + _pack_rows(n)].reshape(-1)[:n].reshape(shape))
        row += _pack_rows(n)
    return out


def kernel(x, w_in, conv_w, sinks, g_mix, g_group, w_o, g_mlp, w_ff_in, w_ff_out, g_final, loss_target, m_w_in, m_conv_w, m_sinks, m_g_mix, m_g_group, m_w_o, m_g_mlp, m_w_ff_in, m_w_ff_out, m_g_final, v_w_in, v_conv_w, v_sinks, v_g_mix, v_g_group, v_w_o, v_g_mlp, v_w_ff_in, v_w_ff_out, v_g_final):
    chip = 2 * lax.axis_index("x") + lax.axis_index("y")
    conv_n = conv_w.shape[2]

    pos_arr = jnp.stack([chip, lax.axis_index("c")]).astype(jnp.int32)
    shards = (w_in, w_o, w_ff_in, w_ff_out)
    conv_tile = jnp.pad(conv_w.reshape(6, conv_n), ((0, HALO - 6), (0, 128 - conv_n)))
    placed = [_place_shard(w_in, pos_arr[:1], "place_shard_0"), None, None, None]
    sems_a, placed, conv_thru = _gather_start(
        GATHER_STARTS[0], placed, (conv_tile, lax.empty((N_CHIPS,) + conv_tile.shape, conv_tile.dtype)),
        "gather_start_0")
    for i in range(1, N_BIG):
        placed[i] = _place_shard(shards[i], pos_arr[:1], "place_shard_%d" % i)
    full = {"arrs": placed, "conv": None, "sems": list(sems_a[:2])}
    target = _to_strips(loss_target[0], placed[:1], "to_strips_target")

    def fetch(stage, layer, after):
        k = 2 * layer + stage
        sems = full["sems"][2 * k:2 * k + 2]
        if k == 0:
            full["arrs"], land = _gather_wait(0, sems, full["arrs"], (sems_a[-2:], *conv_thru), (after, target),
                                              "gather_wait_0")
            conv_all = lax.dynamic_update_slice(land, conv_tile[None], (chip, 0, 0))
            full["conv"] = conv_all[:, :6, :conv_n].reshape(N_CHIPS, 2, 3, conv_n).transpose(1, 2, 0, 3).reshape(
                2, 3, CONV_CH)
            sems_b, full["arrs"], rest = _gather_start(GATHER_STARTS[1], full["arrs"], None, "gather_start_1",
                                                       through=full["arrs"][0])
            full["arrs"][0] = rest[-1]
            full["sems"] += list(sems_b)
        else:
            full["arrs"], _ = _gather_wait(k, sems, full["arrs"], None, (after,), "gather_wait_%d" % k)
        if k == 1:
            sems_c, full["arrs"], _ = _gather_start(GATHER_STARTS[2], full["arrs"], None, "gather_start_2")
            full["sems"] += list(sems_c)
        return (*full["arrs"], full["conv"])

    lands, started, pending = [None] * N_BIG, {}, []

    def emit(layer, w, g):
        if lands[w] is None:
            lands[w] = lax.empty(_slot_shape(g, w), g.dtype)
        pending.append((w, g, lands[w]))
        if not (w == 0 or (layer == 0 and w == 1)):
            return jnp.zeros((), F32)
        name = "scatter_start_%d_%d" % (layer, len(pending))
        done, token = _scatter_start(list(pending), layer, name)
        for (w_i, _, _), (ss, rs, g_thru, land) in zip(pending, done):
            started[layer, w_i], lands[w_i] = (ss, rs, g_thru), land
        pending.clear()
        return token[0, 0]

    loss_tile, dx, grads, dg_final = _local_step(_to_strips(x[0], placed, "to_strips_x"), target, fetch,
                                                 w_ff_in.shape[2] * N_CHIPS,
                                                 sinks, g_mix, g_group, g_mlp, g_final, emit)

    wmv = ((w_in, m_w_in, v_w_in), (w_o, m_w_o, v_w_o), (w_ff_in, m_w_ff_in, v_w_ff_in),
           (w_ff_out, m_w_ff_out, v_w_ff_out))
    big, after = [None] * N_BIG, dx
    for name, ws in (("swap_rest", (1, 2, 3)), ("swap_in", (0,))):
        own = []
        for w in ws:
            g0, g1, slots = _scatter_wait((started[0, w], started[1, w]), lands[w], after, w, "scatter_wait_%d" % w)
            own.append(_sum_slots(g0, g1, slots, w, pos_arr, "sum_slots_%d" % w))
        send_sems, recv_sems, own, zones = _swap_start(own, name + "_start")
        for w, g in zip(ws, own):
            big[w] = _adamw(wmv[w][0], g, wmv[w][1], wmv[w][2], None, pos_arr, "adamw_own_%d" % w)
        theirs = _swap_wait(send_sems, recv_sems, own, zones, big[ws[-1]][1], name + "_wait")
        for w, g in zip(ws, theirs):
            big[w] = _adamw(wmv[w][0], g, wmv[w][1], wmv[w][2], big[w], pos_arr, "adamw_other_%d" % w)
        after = big[ws[-1]][1]

    def both(i):
        return jnp.stack([grads[0][i][0], grads[1][i][0]])
    dconv = jnp.stack([grads[0][0][:3], grads[1][0][:3]])
    dsinks = jnp.stack([grads[0][1][0, ::HEAD_DIM], grads[1][1][0, ::HEAD_DIM]])
    part = _pack_small([both(2), both(3), both(4), dg_final[0], dconv, dsinks, loss_tile[0, 0]])

    def spread(shard):
        return lax.dynamic_update_slice(jnp.zeros((2, 3, CONV_CH), F32), shard, (0, 0, chip * conv_n))
    zero = jnp.zeros((), F32)
    packs = [_pack_small([a, b, c_, e, spread(f), g_, zero]) for a, b, c_, e, f, g_ in (
        (g_mix, g_group, g_mlp, g_final, conv_w, sinks),
        (m_g_mix, m_g_group, m_g_mlp, m_g_final, m_conv_w, m_sinks),
        (v_g_mix, v_g_group, v_g_mlp, v_g_final, v_conv_w, v_sinks))]
    shapes = [g_mix.shape, g_group.shape, g_mlp.shape, g_final.shape, (2, 3, CONV_CH), sinks.shape, ()]
    small = [_unpack_small(p, shapes) for p in _small_sync(part, *packs)]

    def shard_of(full):
        return lax.dynamic_slice(full, (0, 0, chip * conv_n), (2, 3, conv_n))
    small = [(s[0], s[1], s[2], s[3], shard_of(s[4]), s[5], s[6]) for s in small]
    loss = small[0][6]

    def ordered(kind):
        b = [big[i][kind] for i in range(N_BIG)]
        s = small[kind]
        return [b[0], s[4], s[5], s[0], s[1], b[1], s[2], b[2], b[3], s[3]]

    return (loss, dx[None], *ordered(0), *ordered(1), *ordered(2), *ordered(3))
```

```python
import functools

import jax
import jax.numpy as jnp
from jax import lax
from jax.experimental import pallas as pl
from jax.experimental.pallas import tpu as pltpu

HEAD_DIM = 64
N_HEADS = 6
C_GROUP = 3
A_WIDTH = N_HEADS * HEAD_DIM
C_KV_WIDTH = 2 * HEAD_DIM
CONV_CH = 256
ZA_W = 3 * A_WIDTH
ZB_W = 3 * CONV_CH
ZC_W = A_WIDTH + 2 * C_KV_WIDTH
IN_WIDTH = ZA_W + ZB_W + ZC_W
MIX_WIDTH = A_WIDTH + CONV_CH + A_WIDTH
DILATIONS = (1, 4, 16)
A_MAX_DIST = 128
C_MAX_DIST = 127
TQ = 128
EPS = 1e-6
SCALE = HEAD_DIM ** -0.5
NEG = -1e30
HALO = 8

ADAM_LR = 0.001
ADAM_B1 = 0.9
ADAM_B2 = 0.999
ADAM_EPS = 1e-08
ADAM_WD = 0.01
ADAM_STEP = 10

BF = jnp.bfloat16
F32 = jnp.float32
MESH = pl.DeviceIdType.MESH
VMEM_LIMIT = 56 * 1024 * 1024


def _cparams(*sem):
    return pltpu.CompilerParams(dimension_semantics=sem, vmem_limit_bytes=VMEM_LIMIT)


def _nt(a, b):
    return lax.dot_general(a, b, (((1,), (1,)), ((), ())), preferred_element_type=F32)


def _tn(a, b):
    return lax.dot_general(a, b, (((0,), (0,)), ((), ())), preferred_element_type=F32)


def _nn(a, b):
    return jnp.dot(a, b, preferred_element_type=F32)


def _rows(tb, w):
    return pl.BlockSpec((tb, w), lambda i: (i, 0))


def _whole(shape):
    return pl.BlockSpec(shape, lambda *_: (0,) * len(shape))


def _layer(shape, l):
    return pl.BlockSpec((None,) + shape, lambda *_: (l,) + (0,) * len(shape))


def _rms_scale(v):
    return lax.rsqrt(jnp.mean(v * v, axis=-1, keepdims=True) + EPS)


def _norm_bwd(dxhat, xhat, r):
    return r * (dxhat - xhat * jnp.mean(dxhat * xhat, axis=-1, keepdims=True))


def _qkv_fwd(x, g, w_all, l, tb):
    s, d = x.shape

    def body(x_ref, g_ref, w_ref, h_ref, za_ref, zb_ref, zc_ref):
        xv = x_ref[...]
        h = ((xv * _rms_scale(xv)) * g_ref[...]).astype(BF)
        h_ref[...] = h
        z = jnp.concatenate([_nn(h, w_ref[k]) for k in range(N_CHIPS)], axis=1)
        za_ref[...] = z[:, :ZA_W]
        zb_ref[...] = z[:, ZA_W:ZA_W + ZB_W]
        zc_ref[...] = z[:, ZA_W + ZB_W:]

    return pl.pallas_call(
        body, grid=(s // tb,), name="qkv_fwd",
        in_specs=[_rows(tb, d), _whole((1, d)), _layer((N_CHIPS, d, IN_WIDTH // N_CHIPS), l)],
        out_specs=[_rows(tb, d), _rows(tb, ZA_W), _rows(tb, ZB_W), _rows(tb, ZC_W)],
        out_shape=[jax.ShapeDtypeStruct((s, d), BF), jax.ShapeDtypeStruct((s, ZA_W), F32),
                   jax.ShapeDtypeStruct((s, ZB_W), F32), jax.ShapeDtypeStruct((s, ZC_W), F32)],
        compiler_params=_cparams("parallel"),
    )(x, g, w_all)


N_STRIPS = 16


def _strips(a):
    s, w = a.shape
    return a.reshape(4, 4, s // N_STRIPS, w)


P_ROWS = {16: TQ, 4: 32, 1: 8}


def _p_sub(s, dil, most):
    while (s // dil // TQ) % most:
        most //= 2
    return most


def _p_grid(s, dil, n_sub):
    nb = s // dil // TQ // n_sub
    return {16: (4, 4, nb), 4: (4, nb), 1: (nb,)}[dil]


def _p_spec(dil, cw, col, n_sub, prev=False):
    rows = P_ROWS[dil] * (1 if prev else n_sub)

    def blk(j):
        return jnp.maximum(n_sub * j - 1, 0) if prev else j
    if dil == 16:
        return pl.BlockSpec((None, None, rows, cw), lambda f, e, j: (f, e, blk(j), col))
    if dil == 4:
        return pl.BlockSpec((None, 4, rows, cw), lambda f, j: (f, 0, blk(j), col))
    return pl.BlockSpec((4, 4, rows, cw), lambda j: (0, 0, blk(j), col))


def _block_pos(i, dil):
    if dil == 16:
        return i
    if dil == 4:
        return 4 * (i % 32) + i // 32
    return 16 * (i % 8) + 4 * ((i // 8) % 4) + i // 32


def _band_mask(b, dil, max_dist):
    qi = _block_pos(lax.broadcasted_iota(jnp.int32, (TQ, 2 * TQ), 0), dil)
    col = lax.broadcasted_iota(jnp.int32, (TQ, 2 * TQ), 1)
    cur = col >= TQ
    dist = qi - _block_pos(col % TQ, dil) + jnp.where(cur, 0, TQ)
    return (dist >= 0) & (dist <= max_dist) & (cur | (b > 0))


def _hs(h):
    return slice(h * HEAD_DIM, (h + 1) * HEAD_DIM)


def _ld(ref, cols, rows=slice(None)):
    v = ref[..., rows, cols]
    return v.reshape(TQ, v.shape[-1])


def _st(ref, cols, val, rows=slice(None)):
    lead = ref.shape[:-2] + (ref.shape[-2] if rows == slice(None) else rows.stop - rows.start,)
    ref[..., rows, cols] = val.reshape(lead + (val.shape[-1],))


def _attn_fwd(z, dil, kw, kcol, vcol, n_rep, max_dist, name):
    s, zw = z.shape
    n_sub = _p_sub(s, dil, 2)
    grid = _p_grid(s, dil, n_sub)

    def body(q_ref, kp_ref, kc_ref, vp_ref, vc_ref, o_ref, lse_ref):
        for t in range(n_sub):
            rows = slice(t * P_ROWS[dil], (t + 1) * P_ROWS[dil])
            before = (slice(None),) if t == 0 else (slice((t - 1) * P_ROWS[dil], t * P_ROWS[dil]),)
            kb_ref, vb_ref = (kp_ref, vp_ref) if t == 0 else (kc_ref, vc_ref)
            mask = _band_mask(n_sub * pl.program_id(len(grid) - 1) if t == 0 else 1, dil, max_dist)
            scs, v2s = [], []
            for kh in range(N_HEADS // n_rep):
                k2 = jnp.concatenate([_ld(kb_ref, _hs(kh), *before), _ld(kc_ref, _hs(kh), rows)], axis=0).astype(BF)
                v2s.append(jnp.concatenate([_ld(vb_ref, _hs(kh), *before), _ld(vc_ref, _hs(kh), rows)],
                                           axis=0).astype(BF))
                for h in range(kh * n_rep, (kh + 1) * n_rep):
                    q = (_ld(q_ref, _hs(h), rows) * SCALE).astype(BF)
                    scs.append(jnp.where(mask, _nt(q, k2), NEG))
            for h, sc in enumerate(scs):
                m = jnp.max(sc, axis=1, keepdims=True)
                p = jnp.exp(sc - m)
                l = jnp.sum(p, axis=1, keepdims=True)
                _st(o_ref, _hs(h), _nn(p.astype(BF), v2s[h // n_rep]) / l, rows)
                _st(lse_ref, _hs(h), jnp.broadcast_to(m + jnp.log(l), (TQ, HEAD_DIM)), rows)

    res = pl.pallas_call(
        body, grid=grid, name=name,
        in_specs=[_p_spec(dil, A_WIDTH, 0, n_sub), _p_spec(dil, kw, kcol, n_sub, True), _p_spec(dil, kw, kcol, n_sub),
                  _p_spec(dil, kw, vcol, n_sub, True), _p_spec(dil, kw, vcol, n_sub)],
        out_specs=[_p_spec(dil, A_WIDTH, 0, n_sub)] * 2,
        out_shape=[jax.ShapeDtypeStruct((4, 4, s // N_STRIPS, A_WIDTH), F32)] * 2,
        compiler_params=_cparams(*(("parallel",) * len(grid))),
    )(*[_strips(z)] * 5)
    return [a.reshape(s, A_WIDTH) for a in res]


def _attn_merge(parts_a, part_c, sink_row, tb):
    s = part_c[0].shape[0]
    n_a = len(parts_a)

    def body(*refs):
        ins, sink_ref = refs[:2 * n_a + 2], refs[2 * n_a + 2]
        ya_ref, lsea_ref, yc_ref, lsec_ref = refs[2 * n_a + 3:]
        lses = [ins[2 * p + 1][...] for p in range(n_a)]
        m = functools.reduce(jnp.maximum, lses)
        ws = [jnp.exp(v - m) for v in lses]
        l = functools.reduce(jnp.add, ws)
        ya_ref[...] = functools.reduce(jnp.add, [w * ins[2 * p][...] for p, w in enumerate(ws)]) / l
        lsea_ref[...] = m + jnp.log(l)
        o_c, lse_c = [r[...] for r in ins[2 * n_a:]]
        sk = sink_ref[...]
        m2 = jnp.maximum(lse_c, sk)
        w = jnp.exp(lse_c - m2)
        l2 = w + jnp.exp(sk - m2)
        yc_ref[...] = o_c * (w / l2)
        lsec_ref[...] = m2 + jnp.log(l2)

    return pl.pallas_call(
        body, grid=(s // tb,), name="attn_merge",
        in_specs=[_rows(tb, A_WIDTH)] * (2 * n_a + 2) + [_whole((1, A_WIDTH))],
        out_specs=[_rows(tb, A_WIDTH)] * 4, out_shape=[jax.ShapeDtypeStruct((s, A_WIDTH), F32)] * 4,
        compiler_params=_cparams("parallel"),
    )(*[a for part in parts_a + [part_c] for a in part], sink_row)


def _shift_down(v, n, halo):
    rows = v.shape[0]
    out = pltpu.roll(v, n, 0)
    row = lax.broadcasted_iota(jnp.int32, v.shape, 0)
    for t in range(n):
        out = jnp.where(row == t, halo[HALO - n + t:HALO - n + t + 1, :], out)
    return out


def _shift_up(v, n, halo):
    rows = v.shape[0]
    out = pltpu.roll(v, rows - n, 0)
    row = lax.broadcasted_iota(jnp.int32, v.shape, 0)
    for t in range(n):
        out = jnp.where(row == rows - n + t, halo[t:t + 1, :], out)
    return out


def _strip(v, b):
    return v[b % 4, b // 4]


def _conv_strips(zb, prev, cw):
    gb = [_strip(zb, b)[:, :CONV_CH] for b in range(N_STRIPS)]
    gc = [_strip(zb, b)[:, CONV_CH:2 * CONV_CH] for b in range(N_STRIPS)]
    xb = [_strip(zb, b)[:, 2 * CONV_CH:] for b in range(N_STRIPS)]
    u = [g * v for g, v in zip(gc, xb)]
    uh = prev[:, :, CONV_CH:2 * CONV_CH] * prev[:, :, 2 * CONV_CH:]
    wrapped = {14: _shift_down(u[14], 1, uh[2]), 15: _shift_down(u[15], 1, uh[3])}
    u1 = [u[b - 1] if b >= 1 else wrapped[15] for b in range(N_STRIPS)]
    u2 = [u[b - 2] if b >= 2 else wrapped[14 + b] for b in range(N_STRIPS)]
    c = [cw[0:1, :] * u2[b] + cw[1:2, :] * u1[b] + cw[2:3, :] * u[b] for b in range(N_STRIPS)]
    return gb, gc, xb, u, u1, u2, c


def _strip_rows(ta, w):
    return pl.BlockSpec((4, 4, ta, w), lambda i: (0, 0, i, 0))


def _prev_rows(ta, w):
    return pl.BlockSpec((4, None, HALO, w), lambda i: (0, 3, jnp.maximum(i * (ta // HALO) - 1, 0), 0))


def _next_rows(ta, w, nblk):
    return pl.BlockSpec((4, None, HALO, w),
                        lambda i: (0, 0, jnp.minimum((i + 1) * (ta // HALO), nblk * (ta // HALO) - 1), 0))


def _mix_fwd(x, ya, yc, zb, cw, gg, wo_all, l, tb):
    s, d = x.shape
    ta = tb // N_STRIPS

    def body(x_ref, ya_ref, yc_ref, zb_ref, zbp_ref, cw_ref, gg_ref, wo_ref, x1_ref, yb_ref):
        i = pl.program_id(0)
        prev = jnp.where(i > 0, zbp_ref[...], 0.0)
        gb, _, _, _, _, _, c = _conv_strips(zb_ref[...], prev, cw_ref[...])
        for b in range(N_STRIPS):
            yb_ref[b % 4, b // 4] = gb[b] * c[b]
        yb = yb_ref[...].reshape(tb, CONV_CH)
        ya, yc = ya_ref[...].reshape(tb, A_WIDTH), yc_ref[...].reshape(tb, A_WIDTH)
        n = jnp.concatenate([ya * _rms_scale(ya), yb * _rms_scale(yb), yc * _rms_scale(yc)], axis=1)
        n = (n * gg_ref[...]).astype(BF)
        x1 = x_ref[...].reshape(tb, d) + _nn(n, wo_ref[...].reshape(MIX_WIDTH, d))
        x1_ref[...] = x1.reshape(4, 4, ta, d)

    res = pl.pallas_call(
        body, grid=(s // tb,), name="mix_fwd",
        in_specs=[_strip_rows(ta, d), _strip_rows(ta, A_WIDTH), _strip_rows(ta, A_WIDTH), _strip_rows(ta, ZB_W),
                  _prev_rows(ta, ZB_W), _whole((HALO, CONV_CH)), _whole((1, MIX_WIDTH)),
                  _layer((N_CHIPS, MIX_WIDTH // N_CHIPS, d), l)],
        out_specs=[_strip_rows(ta, d), _strip_rows(ta, CONV_CH)],
        out_shape=[jax.ShapeDtypeStruct((4, 4, s // N_STRIPS, d), F32),
                   jax.ShapeDtypeStruct((4, 4, s // N_STRIPS, CONV_CH), F32)],
        compiler_params=_cparams("parallel"),
    )(_strips(x), _strips(ya), _strips(yc), _strips(zb), _strips(zb), cw, gg, wo_all)
    return res[0].reshape(s, d), res[1].reshape(s, CONV_CH)


def _mlp_fwd(x1, g, w1_all, w2_all, l, tb, tf):
    s, d = x1.shape
    ff = w1_all.shape[1] * w1_all.shape[3]
    nj = ff // tf

    def body(x_ref, g_ref, w1_ref, w2_ref, x2_ref, h2_ref, ap_ref, acc):
        j = pl.program_id(1)

        @pl.when(j == 0)
        def _():
            xv = x_ref[...]
            h2_ref[...] = ((xv * _rms_scale(xv)) * g_ref[...]).astype(BF)
            acc[...] = jnp.zeros_like(acc)

        ap = _nn(h2_ref[...], w1_ref[...])
        ap_ref[...] = ap.astype(BF)
        a = jnp.square(jnp.maximum(ap, 0.0)).astype(BF)
        acc[...] += _nn(a, w2_ref[...])

        @pl.when(j == nj - 1)
        def _():
            x2_ref[...] = x_ref[...] + acc[...]

    return pl.pallas_call(
        body, grid=(s // tb, nj), name="mlp_fwd",
        in_specs=[pl.BlockSpec((tb, d), lambda i, j: (i, 0)), _whole((1, d)),
                  pl.BlockSpec((None, None, d, tf), lambda i, j: (l, j, 0, 0)),
                  pl.BlockSpec((None, None, tf, d), lambda i, j: (l, j, 0, 0))],
        out_specs=[pl.BlockSpec((tb, d), lambda i, j: (i, 0)), pl.BlockSpec((tb, d), lambda i, j: (i, 0)),
                   pl.BlockSpec((tb, tf), lambda i, j: (i, j))],
        out_shape=[jax.ShapeDtypeStruct((s, d), F32), jax.ShapeDtypeStruct((s, d), BF),
                   jax.ShapeDtypeStruct((s, ff), BF)],
        scratch_shapes=[pltpu.VMEM((tb, d), F32)],
        compiler_params=_cparams("parallel", "arbitrary"),
    )(x1, g, w1_all, w2_all)


def _loss_head(x, g, tgt, tb):
    s, d = x.shape

    def body(x_ref, g_ref, t_ref, dx_ref, loss_ref, dg_ref):
        i = pl.program_id(0)

        @pl.when(i == 0)
        def _():
            loss_ref[...] = jnp.zeros_like(loss_ref)
            dg_ref[...] = jnp.zeros_like(dg_ref)

        xv = x_ref[...]
        r = _rms_scale(xv)
        xhat = xv * r
        err = xhat * g_ref[...] - t_ref[...]
        part = jnp.sum(jnp.mean(jnp.square(err), axis=-1, keepdims=True), axis=0, keepdims=True)
        loss_ref[...] += 0.5 * part
        dy = err * (1.0 / d)
        dg_ref[...] += jnp.sum(dy * xhat, axis=0, keepdims=True)
        dx_ref[...] = _norm_bwd(dy * g_ref[...], xhat, r)

    return pl.pallas_call(
        body, grid=(s // tb,), name="loss_head",
        in_specs=[_rows(tb, d), _whole((1, d)), _rows(tb, d)],
        out_specs=[_rows(tb, d), _whole((HALO, 128)), _whole((HALO, d))],
        out_shape=[jax.ShapeDtypeStruct((s, d), F32), jax.ShapeDtypeStruct((HALO, 128), F32),
                   jax.ShapeDtypeStruct((HALO, d), F32)],
        compiler_params=_cparams("arbitrary"),
    )(x, g, tgt)


def _mlp_bwd(dx2, x1, ap, g, w1_all, w2_all, l, tb, tf):
    s, d = x1.shape
    ff = ap.shape[1]
    nj = ff // tf

    def body(dx2_ref, x1_ref, ap_ref, g_ref, w1_ref, w2_ref, dx1_ref, dap_ref, dg_ref, acc):
        i, j = pl.program_id(0), pl.program_id(1)

        @pl.when((i == 0) & (j == 0))
        def _():
            dg_ref[...] = jnp.zeros_like(dg_ref)

        @pl.when(j == 0)
        def _():
            acc[...] = jnp.zeros_like(acc)

        da = _nt(dx2_ref[...].astype(BF), w2_ref[...])
        dap = (da * (2.0 * jnp.maximum(ap_ref[...].astype(F32), 0.0))).astype(BF)
        dap_ref[...] = dap
        acc[...] += _nt(dap, w1_ref[...])

        @pl.when(j == nj - 1)
        def _():
            xv = x1_ref[...]
            r = _rms_scale(xv)
            xhat = xv * r
            dh = acc[...]
            dg_ref[...] += jnp.sum(dh * xhat, axis=0, keepdims=True)
            dx1_ref[...] = dx2_ref[...] + _norm_bwd(dh * g_ref[...], xhat, r)

    return pl.pallas_call(
        body, grid=(s // tb, nj), name="mlp_bwd",
        in_specs=[pl.BlockSpec((tb, d), lambda i, j: (i, 0)), pl.BlockSpec((tb, d), lambda i, j: (i, 0)),
                  pl.BlockSpec((tb, tf), lambda i, j: (i, j)),
                  _whole((1, d)), pl.BlockSpec((None, None, d, tf), lambda i, j: (l, j, 0, 0)),
                  pl.BlockSpec((None, None, tf, d), lambda i, j: (l, j, 0, 0))],
        out_specs=[pl.BlockSpec((tb, d), lambda i, j: (i, 0)), pl.BlockSpec((tb, tf), lambda i, j: (i, j)),
                   _whole((HALO, d))],
        out_shape=[jax.ShapeDtypeStruct((s, d), F32), jax.ShapeDtypeStruct((s, ff), BF),
                   jax.ShapeDtypeStruct((HALO, d), F32)],
        scratch_shapes=[pltpu.VMEM((tb, d), F32)],
        compiler_params=_cparams("arbitrary", "arbitrary"),
    )(dx2, x1, ap, g, w1_all, w2_all)


def _wgrad(a, b, tm, tn, ts, name, relu2=False):
    s, m = a.shape
    n = b.shape[1]
    ns = s // ts

    def body(a_ref, b_ref, o_ref, acc):
        k = pl.program_id(2)

        @pl.when(k == 0)
        def _():
            acc[...] = jnp.zeros_like(acc)

        av = a_ref[...]
        if relu2:
            av = jnp.square(jnp.maximum(av.astype(F32), 0.0)).astype(BF)
        acc[...] += _tn(av, b_ref[...].astype(BF))

        @pl.when(k == ns - 1)
        def _():
            o_ref[...] = acc[...].astype(BF)

    return pl.pallas_call(
        body, grid=(m // tm, n // tn, ns), name=name,
        in_specs=[pl.BlockSpec((ts, tm), lambda i, j, k: (k, i)), pl.BlockSpec((ts, tn), lambda i, j, k: (k, j))],
        out_specs=pl.BlockSpec((tm, tn), lambda i, j, k: (i, j)),
        out_shape=jax.ShapeDtypeStruct((m, n), BF),
        scratch_shapes=[pltpu.VMEM((tm, tn), F32)],
        compiler_params=_cparams("parallel", "parallel", "arbitrary"),
    )(a, b)


def _mix_bwd(dx1, ya, yb, yc, lse_c, sink_row, gg, wo_all, l, tb):
    s, d = dx1.shape

    def body(dx_ref, ya_ref, yb_ref, yc_ref, lse_ref, sink_ref, gg_ref, wo_ref,
             n_ref, dya_ref, dyc_ref, da_ref, dc_ref, dyb_ref, dg_ref, dsink_ref):
        i = pl.program_id(0)

        @pl.when(i == 0)
        def _():
            dg_ref[...] = jnp.zeros_like(dg_ref)
            dsink_ref[...] = jnp.zeros_like(dsink_ref)

        dn = _nt(dx_ref[...].astype(BF), wo_ref[...].reshape(MIX_WIDTH, d))
        ys = [ya_ref[...], yb_ref[...], yc_ref[...]]
        rs = [_rms_scale(v) for v in ys]
        nhat = jnp.concatenate([v * r for v, r in zip(ys, rs)], axis=1)
        gg = gg_ref[...]
        n_ref[...] = (nhat * gg).astype(BF)
        dg_ref[...] += jnp.sum(dn * nhat, axis=0, keepdims=True)
        dnh = dn * gg
        bounds = [(0, A_WIDTH), (A_WIDTH, A_WIDTH + CONV_CH), (A_WIDTH + CONV_CH, MIX_WIDTH)]
        dys = [_norm_bwd(dnh[:, lo:hi], nhat[:, lo:hi], r) for (lo, hi), r in zip(bounds, rs)]
        dyb_ref[...] = dys[1]
        for dy, y, dy_ref, dd_ref in ((dys[0], ys[0], dya_ref, da_ref), (dys[2], ys[2], dyc_ref, dc_ref)):
            dy_ref[...] = dy
            t = dy * y
            for h in range(N_HEADS):
                dd_ref[:, _hs(h)] = jnp.broadcast_to(jnp.sum(t[:, _hs(h)], axis=1, keepdims=True), (tb, HEAD_DIM))
        dsink_ref[...] -= jnp.sum(jnp.exp(sink_ref[...] - lse_ref[...]) * dc_ref[...], axis=0, keepdims=True)

    return pl.pallas_call(
        body, grid=(s // tb,), name="mix_bwd",
        in_specs=[_rows(tb, d), _rows(tb, A_WIDTH), _rows(tb, CONV_CH), _rows(tb, A_WIDTH), _rows(tb, A_WIDTH),
                  _whole((1, A_WIDTH)), _whole((1, MIX_WIDTH)), _layer((N_CHIPS, MIX_WIDTH // N_CHIPS, d), l)],
        out_specs=[_rows(tb, MIX_WIDTH), _rows(tb, A_WIDTH), _rows(tb, A_WIDTH), _rows(tb, A_WIDTH),
                   _rows(tb, A_WIDTH), _rows(tb, CONV_CH), _whole((HALO, MIX_WIDTH)), _whole((HALO, A_WIDTH))],
        out_shape=[jax.ShapeDtypeStruct((s, MIX_WIDTH), BF), jax.ShapeDtypeStruct((s, A_WIDTH), F32),
                   jax.ShapeDtypeStruct((s, A_WIDTH), F32), jax.ShapeDtypeStruct((s, A_WIDTH), F32),
                   jax.ShapeDtypeStruct((s, A_WIDTH), F32), jax.ShapeDtypeStruct((s, CONV_CH), F32),
                   jax.ShapeDtypeStruct((HALO, MIX_WIDTH), F32), jax.ShapeDtypeStruct((HALO, A_WIDTH), F32)],
        compiler_params=_cparams("arbitrary"),
    )(dx1, ya, yb, yc, lse_c, sink_row, gg, wo_all)


def _attn_bwd(z, dy, lse, dd, dil, kw, kcol, vcol, n_rep, max_dist, name):
    s, zw = z.shape
    n_sub = _p_sub(s, dil, 2) if n_rep == 1 else 1
    grid = _p_grid(s, dil, n_sub)
    n_kv = N_HEADS // n_rep
    dt = F32 if dil == 1 else BF

    def body(q_ref, kp_ref, kc_ref, vp_ref, vc_ref, dy_ref, lse_ref, dd_ref, dq_ref, dkp_ref, dkc_ref, dvp_ref, dvc_ref):
        for t in range(n_sub):
            rows = slice(t * P_ROWS[dil], (t + 1) * P_ROWS[dil])
            before = (slice(None),) if t == 0 else (slice((t - 1) * P_ROWS[dil], t * P_ROWS[dil]),)
            kb_ref, vb_ref = (kp_ref, vp_ref) if t == 0 else (kc_ref, vc_ref)
            mask = _band_mask(n_sub * pl.program_id(len(grid) - 1) if t == 0 else 1, dil, max_dist)
            k2s, qs, dys, scs, dps = [], [], [], [], []
            for kh in range(n_kv):
                k2s.append(jnp.concatenate([_ld(kb_ref, _hs(kh), *before), _ld(kc_ref, _hs(kh), rows)],
                                           axis=0).astype(BF))
                v2 = jnp.concatenate([_ld(vb_ref, _hs(kh), *before), _ld(vc_ref, _hs(kh), rows)], axis=0).astype(BF)
                for h in range(kh * n_rep, (kh + 1) * n_rep):
                    qs.append((_ld(q_ref, _hs(h), rows) * SCALE).astype(BF))
                    dys.append(_ld(dy_ref, _hs(h), rows).astype(BF))
                    scs.append(jnp.where(mask, _nt(qs[h], k2s[kh]), NEG))
                    dps.append(_nt(dys[h], v2))
            for kh in range(n_kv):
                k2 = k2s[kh]
                dk2 = jnp.zeros((2 * TQ, HEAD_DIM), F32)
                dv2 = jnp.zeros((2 * TQ, HEAD_DIM), F32)
                for h in range(kh * n_rep, (kh + 1) * n_rep):
                    lse_h = _ld(lse_ref, slice(h * HEAD_DIM, h * HEAD_DIM + 1), rows)
                    dd_h = _ld(dd_ref, slice(h * HEAD_DIM, h * HEAD_DIM + 1), rows)
                    p = jnp.exp(scs[h] - lse_h)
                    ds = (p * (dps[h] - dd_h)).astype(BF)
                    _st(dq_ref, _hs(h), (_nn(ds, k2) * SCALE).astype(dt), rows)
                    dk2 = dk2 + _tn(ds, qs[h])
                    dv2 = dv2 + _tn(p.astype(BF), dys[h])
                _st(dkp_ref, _hs(kh), dk2[:TQ].astype(dt), rows)
                _st(dkc_ref, _hs(kh), dk2[TQ:].astype(dt), rows)
                _st(dvp_ref, _hs(kh), dv2[:TQ].astype(dt), rows)
                _st(dvc_ref, _hs(kh), dv2[TQ:].astype(dt), rows)

    args = [_strips(z)] * 5 + [_strips(a) for a in (dy, lse, dd)]
    pair = _p_spec(dil, A_WIDTH, 0, n_sub)
    in_specs = [pair, _p_spec(dil, kw, kcol, n_sub, True), _p_spec(dil, kw, kcol, n_sub),
                _p_spec(dil, kw, vcol, n_sub, True), _p_spec(dil, kw, vcol, n_sub)] + [pair] * 3
    out_specs = [pair] + [_p_spec(dil, kw, 0, n_sub)] * 4
    na = s // N_STRIPS
    out_shape = [jax.ShapeDtypeStruct((4, 4, na, A_WIDTH), dt)] + [jax.ShapeDtypeStruct((4, 4, na, kw), dt)] * 4
    res = pl.pallas_call(
        body, grid=grid, name=name, in_specs=in_specs, out_specs=out_specs, out_shape=out_shape,
        compiler_params=_cparams(*(("parallel",) * len(grid))),
    )(*args)
    return [res[0].reshape(s, A_WIDTH)] + [a.reshape(s, kw) for a in res[1:]]


DZ_TA = 16


def _dz_assemble(parts_a, parts_c, dyb, zb, cw):
    s = zb.shape[0]
    na = s // N_STRIPS
    nb = na // DZ_TA

    def ahead(w, k):
        return pl.BlockSpec((4, 4, DZ_TA, w), lambda i: (0, 0, jnp.minimum(i + k, nb - 1), 0))

    args, in_specs = [], []
    for dil, (dq, dkp, dkc, dvp, dvc) in zip(DILATIONS + (1,), parts_a + [parts_c]):
        w = dkp.shape[1]
        here = _strip_rows(DZ_TA, w)
        if dil == 1:
            args += [dq, dkp, dkp, dkc, dvp, dvp, dvc]
            in_specs += [_strip_rows(DZ_TA, A_WIDTH), here, ahead(w, 1), here, here, ahead(w, 1), here]
        else:
            k = 8 * dil // DZ_TA
            args += [dq, dkp, dkc, dvp, dvc]
            in_specs += [_strip_rows(DZ_TA, A_WIDTH), ahead(w, k), here, ahead(w, k), here]
    n_att = len(args)
    args = [_strips(a) for a in args] + [_strips(dyb), _strips(dyb), _strips(zb), _strips(zb), _strips(zb), cw]
    in_specs += [_strip_rows(DZ_TA, CONV_CH), _next_rows(DZ_TA, CONV_CH, nb), _strip_rows(DZ_TA, ZB_W),
                 _prev_rows(DZ_TA, ZB_W), _next_rows(DZ_TA, ZB_W, nb), _whole((HALO, CONV_CH))]

    def body(*refs):
        att = list(refs[:n_att])
        dyb_ref, dybn_ref, zb_ref, zbp_ref, zbn_ref, cw_ref, dz_ref, dcw_ref = refs[n_att:]
        i = pl.program_id(0)

        @pl.when(i == 0)
        def _():
            dcw_ref[...] = jnp.zeros_like(dcw_ref)

        def shifted(dil):
            if dil == 1:
                dq_r, kp0, kp1, dkc_r, vp0, vp1, dvc_r = [att.pop(0) for _ in range(7)]
                live = i + 1 < nb
                half = DZ_TA // 2
                dkp = jnp.concatenate([kp0[:, :, half:, :], jnp.where(live, kp1[:, :, :half, :], 0.0)], axis=2)
                dvp = jnp.concatenate([vp0[:, :, half:, :], jnp.where(live, vp1[:, :, :half, :], 0.0)], axis=2)
            else:
                dq_r, dkp_r, dkc_r, dvp_r, dvc_r = [att.pop(0) for _ in range(5)]
                live = i + 8 * dil // DZ_TA < nb
                dkp = jnp.where(live, dkp_r[...].astype(F32), 0.0)
                dvp = jnp.where(live, dvp_r[...].astype(F32), 0.0)
            return dq_r[...].astype(F32), dkc_r[...].astype(F32) + dkp, dvc_r[...].astype(F32) + dvp

        dq, dk, dv = shifted(DILATIONS[0])
        for dil in DILATIONS[1:]:
            dq2, dk2, dv2 = shifted(dil)
            dq, dk, dv = dq + dq2, dk + dk2, dv + dv2
        dz_ref[:, :, :, 0:A_WIDTH] = dq.astype(BF)
        dz_ref[:, :, :, A_WIDTH:2 * A_WIDTH] = dk.astype(BF)
        dz_ref[:, :, :, 2 * A_WIDTH:ZA_W] = dv.astype(BF)
        dq, dk, dv = shifted(1)
        c0 = ZA_W + ZB_W
        dz_ref[:, :, :, c0:c0 + A_WIDTH] = dq.astype(BF)
        dz_ref[:, :, :, c0 + A_WIDTH:c0 + A_WIDTH + C_KV_WIDTH] = dk.astype(BF)
        dz_ref[:, :, :, c0 + A_WIDTH + C_KV_WIDTH:IN_WIDTH] = dv.astype(BF)

        cw = cw_ref[...]
        prev = jnp.where(i > 0, zbp_ref[...], 0.0)
        gb, gc, xb, u, u1, u2, c = _conv_strips(zb_ref[...], prev, cw)
        dyb = dyb_ref[...]
        dc = [_strip(dyb, b) * gb[b] for b in range(N_STRIPS)]
        dcn = jnp.where(i + 1 < nb, dybn_ref[...] * zbn_ref[:, :, :CONV_CH], 0.0)
        wrapped = [_shift_up(dc[0], 1, dcn[0]), _shift_up(dc[1], 1, dcn[1])]
        upd = [jnp.zeros((1, CONV_CH), F32)] * 3
        for b in range(N_STRIPS):
            dc1 = dc[b + 1] if b + 1 < N_STRIPS else wrapped[0]
            dc2 = dc[b + 2] if b + 2 < N_STRIPS else wrapped[b + 2 - N_STRIPS]
            du = cw[2:3, :] * dc[b] + cw[1:2, :] * dc1 + cw[0:1, :] * dc2
            f, e = b % 4, b // 4
            dz_ref[f, e, :, ZA_W:ZA_W + CONV_CH] = (_strip(dyb, b) * c[b]).astype(BF)
            dz_ref[f, e, :, ZA_W + CONV_CH:ZA_W + 2 * CONV_CH] = (du * xb[b]).astype(BF)
            dz_ref[f, e, :, ZA_W + 2 * CONV_CH:c0] = (du * gc[b]).astype(BF)
            for t, uu in enumerate((u2[b], u1[b], u[b])):
                upd[t] = upd[t] + jnp.sum(dc[b] * uu, axis=0, keepdims=True)
        row = lax.broadcasted_iota(jnp.int32, (HALO, CONV_CH), 0)
        tile = jnp.zeros((HALO, CONV_CH), F32)
        for t in range(3):
            tile = jnp.where(row == t, upd[t], tile)
        dcw_ref[...] += tile

    dz, dcw = pl.pallas_call(
        body, grid=(nb,), name="dz_assemble", in_specs=in_specs,
        out_specs=[_strip_rows(DZ_TA, IN_WIDTH), _whole((HALO, CONV_CH))],
        out_shape=[jax.ShapeDtypeStruct((4, 4, na, IN_WIDTH), BF), jax.ShapeDtypeStruct((HALO, CONV_CH), F32)],
        compiler_params=_cparams("arbitrary"),
    )(*args)
    return dz.reshape(s, IN_WIDTH), dcw


def _qkv_bwd(dz, dx1, x, g, w_all, l, tb, tokens_out):
    s, d = x.shape
    na, ta = s // N_STRIPS, tb // N_STRIPS

    def body(dz_ref, dx1_ref, x_ref, g_ref, w_ref, dx_ref, dg_ref):
        i = pl.program_id(0)

        @pl.when(i == 0)
        def _():
            dg_ref[...] = jnp.zeros_like(dg_ref)

        n = IN_WIDTH // N_CHIPS
        dz = dz_ref[...].reshape(tb, IN_WIDTH)
        dh = _nt(dz[:, 0:n], w_ref[0])
        for k in range(1, N_CHIPS):
            dh = dh + _nt(dz[:, k * n:(k + 1) * n], w_ref[k])
        xv = x_ref[...].reshape(tb, d)
        r = _rms_scale(xv)
        xhat = xv * r
        dg_ref[...] += jnp.sum(dh * xhat, axis=0, keepdims=True)
        dx = (dx1_ref[...].reshape(tb, d) + _norm_bwd(dh * g_ref[...], xhat, r)).reshape(4, 4, ta, d)
        if tokens_out:
            for b in range(N_STRIPS):
                dx_ref[:, b, :] = _strip(dx, b)
        else:
            dx_ref[...] = dx

    if tokens_out:
        dx_spec, dx_shape = pl.BlockSpec((ta, N_STRIPS, d), lambda i: (i, 0, 0)), (na, N_STRIPS, d)
    else:
        dx_spec, dx_shape = _strip_rows(ta, d), (4, 4, na, d)
    dx, dg = pl.pallas_call(
        body, grid=(s // tb,), name="qkv_bwd",
        in_specs=[_strip_rows(ta, IN_WIDTH), _strip_rows(ta, d), _strip_rows(ta, d), _whole((1, d)),
                  _layer((N_CHIPS, d, IN_WIDTH // N_CHIPS), l)],
        out_specs=[dx_spec, _whole((HALO, d))],
        out_shape=[jax.ShapeDtypeStruct(dx_shape, F32), jax.ShapeDtypeStruct((HALO, d), F32)],
        compiler_params=_cparams("arbitrary"),
    )(_strips(dz), _strips(dx1), _strips(x), g, w_all)
    return dx.reshape(s, d), dg


def _tile_rows(rows):
    return jnp.pad(rows, ((0, HALO - rows.shape[0]), (0, 0)))


def _to_strips(a, after, name):
    s, d = a.shape
    na = s // N_STRIPS
    ta = min(32, na)

    def body(a_ref, *rest):
        for b in range(N_STRIPS):
            rest[-1][b % 4, b // 4] = a_ref[:, b, :]

    return pl.pallas_call(
        body, grid=(na // ta,), name=name,
        in_specs=[pl.BlockSpec((ta, N_STRIPS, d), lambda i: (i, 0, 0))] + [ANY] * len(after),
        out_specs=_strip_rows(ta, d),
        out_shape=jax.ShapeDtypeStruct((4, 4, na, d), a.dtype), compiler_params=_cparams("parallel"),
    )(a.reshape(na, N_STRIPS, d), *after).reshape(s, d)


def _local_step(x, tgt, fetch, ff, sinks, g_mix, g_group, g_mlp, g_final, emit):
    s, d = x.shape
    depth = g_mix.shape[0]
    tb = min(512, s)
    tf = ff // N_CHIPS
    ts = min(1024, s)
    saved = []
    for l in range(depth):
        w_in, _, _, _, conv_w = fetch(0, l, x)
        cw = _tile_rows(conv_w[l])
        sk = jnp.repeat(sinks[l].reshape(N_HEADS), HEAD_DIM)[None]
        h, za, zb, zc = _qkv_fwd(x, g_mix[l][None], w_in, l, tb)
        parts_a = [_attn_fwd(za, dil, A_WIDTH, 1, 2, 1, A_MAX_DIST, "attn_a_fwd_%d" % dil) for dil in DILATIONS]
        part_c = _attn_fwd(zc, 1, C_KV_WIDTH, 3, 4, C_GROUP, C_MAX_DIST, "attn_c_fwd")
        ya, lse_a, yc, lse_c = _attn_merge(parts_a, part_c, sk, ts)
        w_in, w_o, w1, w2, _ = fetch(1, l, yc)
        x1, yb = _mix_fwd(x, ya, yc, zb, cw, g_group[l][None], w_o, l, tb)
        x2, h2, ap = _mlp_fwd(x1, g_mlp[l][None], w1, w2, l, ts, tf)
        saved.append((x, h, za, zb, zc, ya, lse_a, yc, lse_c, yb, x1, h2, ap, cw, sk))
        x = x2
    dx, loss_tile, dg_final = _loss_head(x, g_final[None], tgt, ts)
    grads = [None] * depth
    tok = jnp.zeros((), F32)
    for l in reversed(range(depth)):
        x0, h, za, zb, zc, ya, lse_a, yc, lse_c, yb, x1, h2, ap, cw, sk = saved[l]
        dx1, dap, dg_mlp = _mlp_bwd(dx, x1, ap, g_mlp[l][None] + tok, w1, w2, l, ts, tf)
        tok = emit(l, 3, _wgrad(ap, dx, min(1024, ff), d, ts, "wgrad_ff_out", relu2=True))
        tok = tok + emit(l, 2, _wgrad(h2, dap, d, min(1024, ff), 2 * ts, "wgrad_ff_in"))
        n, dya, dyc, dd_a, dd_c, dyb, dg_group, dsink = _mix_bwd(dx1, ya, yb, yc, lse_c, sk, g_group[l][None] + tok,
                                                                 w_o, l, tb)
        tok = emit(l, 1, _wgrad(n, dx1, MIX_WIDTH, d, ts, "wgrad_o"))
        cw = cw + tok
        parts_a = [_attn_bwd(za, dya, lse_a, dd_a, dil, A_WIDTH, 1, 2, 1, A_MAX_DIST, "attn_a_bwd_%d" % dil)
                   for dil in DILATIONS]
        parts_c = _attn_bwd(zc, dyc, lse_c, dd_c, 1, C_KV_WIDTH, 3, 4, C_GROUP, C_MAX_DIST, "attn_c_bwd")
        dz, dcw = _dz_assemble(parts_a, parts_c, dyb, zb, cw)
        tok = emit(l, 0, _wgrad(h, dz, d, IN_WIDTH // 4, 2 * ts, "wgrad_in"))
        dx, dg_mix = _qkv_bwd(dz, dx1, x0, g_mix[l][None] + tok, w_in, l, tb, l == 0)
        grads[l] = (dcw, dsink, dg_mix, dg_group, dg_mlp)
    return loss_tile, dx, grads, dg_final


ANY = pl.BlockSpec(memory_space=pl.ANY)
SHARD_AXES = (2, 1, 2, 1)
N_BIG = len(SHARD_AXES)
N_CHIPS = 4
N_DEV = 8


def _mesh_pos():
    return lax.axis_index("x"), lax.axis_index("y"), lax.axis_index("c")


def _flip(v, bit):
    return 1 - v if bit else v


def _place_shard(shard, chip_arr, name):
    _, rows, cols = shard.shape
    tr = min(256, rows)

    def body(chip_ref, x_ref, o_ref):
        o_ref[...] = x_ref[...].astype(BF)

    return pl.pallas_call(
        body, name=name,
        grid_spec=pltpu.PrefetchScalarGridSpec(
            num_scalar_prefetch=1, grid=(2, rows // tr),
            in_specs=[pl.BlockSpec((None, tr, cols), lambda l, i, chip: (l, i, 0))],
            out_specs=pl.BlockSpec((None, None, tr, cols), lambda l, i, chip: (l, chip[0], i, 0))),
        out_shape=jax.ShapeDtypeStruct((2, N_CHIPS, rows, cols), BF),
        compiler_params=_cparams("parallel", "parallel"),
    )(chip_arr, shard)


HBM = pl.BlockSpec(memory_space=pltpu.HBM)
SEM = pl.BlockSpec(memory_space=pltpu.SEMAPHORE)
EFFECT = pltpu.SideEffectType.DATAFLOW_SIDE_EFFECTING

GATHER_GROUPS = (((0, 0),), ((1, 0), (2, 0), (3, 0)), ((0, 1),), ((1, 1), (2, 1), (3, 1)))
GATHER_STARTS = ((0,), (1,), (2, 3))


def _gather_copies(arrs, group, send_sems, recv_sems):
    x, y, c = _mesh_pos()
    me = 2 * x + y
    out = []
    for i, (w, layer) in enumerate(group):
        mine = arrs[w].at[layer, me]
        for j, (qx, qy) in enumerate([(1 - x, y), (x, 1 - y), (1 - x, 1 - y)]):
            landed = arrs[w].at[layer, 2 * qx + qy]
            out.append(tuple(pltpu.make_async_remote_copy(
                src_ref=piece, dst_ref=piece, send_sem=send_sems.at[i * 3 + j], recv_sem=recv_sems.at[i * 3 + j],
                device_id=(qx, qy, c), device_id_type=MESH) for piece in (mine, landed)))
    return out


def _conv_copies(conv_src, conv_dst, send_sems, recv_sems):
    x, y, c = _mesh_pos()
    out = []
    for j, (qx, qy) in enumerate([(1 - x, y), (x, 1 - y), (1 - x, 1 - y)]):
        out.append(tuple(pltpu.make_async_remote_copy(
            src_ref=conv_src, dst_ref=conv_dst.at[q], send_sem=send_sems.at[j], recv_sem=recv_sems.at[j],
            device_id=(qx, qy, c), device_id_type=MESH) for q in (2 * x + y, 2 * qx + qy)))
    return out


def _gather_start(groups, arrs, conv, name, through=None):
    n_sems = 2 * (len(groups) + (conv is not None))
    mats = sorted({w for g in groups for w, _ in GATHER_GROUPS[g]})

    def body(*refs):
        arrs_ref = [None] * N_BIG
        for w, ref in zip(mats, refs):
            arrs_ref[w] = ref
        sems = refs[n_in:n_in + n_sems]
        if conv is not None:
            for cp, _ in _conv_copies(refs[len(mats)], refs[len(mats) + 1], sems[-2], sems[-1]):
                cp.start()
        for k, g in enumerate(groups):
            for cp, _ in _gather_copies(arrs_ref, GATHER_GROUPS[g], sems[2 * k], sems[2 * k + 1]):
                cp.start()

    sem_shapes = []
    for n in [len(GATHER_GROUPS[g]) for g in groups] + ([1] if conv is not None else []):
        sem_shapes += [pltpu.SemaphoreType.DMA((3 * n,))] * 2
    operands = [arrs[w] for w in mats] + ([] if conv is None else list(conv)) + ([] if through is None else [through])
    n_in = len(operands)
    res = pl.pallas_call(
        body, name=name,
        out_shape=tuple(sem_shapes) + tuple(pltpu.HBM(a.shape, a.dtype) for a in operands),
        in_specs=(HBM,) * n_in, out_specs=(SEM,) * n_sems + (HBM,) * n_in,
        input_output_aliases={i: n_sems + i for i in range(n_in)},
        compiler_params=pltpu.CompilerParams(has_side_effects=EFFECT),
    )(*[pltpu.with_memory_space_constraint(a, pltpu.HBM) for a in operands])
    arrs = list(arrs)
    for w, a in zip(mats, res[n_sems:]):
        arrs[w] = a
    return res[:n_sems], arrs, list(res[n_sems + len(mats):])


def _gather_wait(k, sems, arrs, conv, after, name):
    group = GATHER_GROUPS[k]
    mats = sorted({w for w, _ in group})
    n_conv = 0 if conv is None else 2

    def body(*refs):
        local = refs[:len(mats)]
        arrs_ref = [None] * N_BIG
        for w, ref in zip(mats, local):
            arrs_ref[w] = ref
        pos = len(mats) + n_conv
        copies = _gather_copies(arrs_ref, group, refs[pos], refs[pos + 1])
        if conv is not None:
            copies += _conv_copies(refs[len(mats)], refs[len(mats) + 1], refs[pos + 2], refs[pos + 3])
        for send, recv in copies:
            recv.wait_recv()
            send.wait_send()

    operands = [arrs[w] for w in mats] + ([] if conv is None else [conv[1], conv[2]])
    sem_ops = list(sems) + ([] if conv is None else list(conv[0]))
    n_op = len(operands)
    res = pl.pallas_call(
        body, name=name, out_shape=tuple(pltpu.HBM(a.shape, a.dtype) for a in operands),
        in_specs=(HBM,) * n_op + (SEM,) * len(sem_ops) + (ANY,) * len(after), out_specs=(HBM,) * n_op,
        input_output_aliases={i: i for i in range(n_op)},
        compiler_params=pltpu.CompilerParams(has_side_effects=EFFECT),
    )(*operands, *sem_ops, *after)
    arrs = list(arrs)
    for w, a in zip(mats, res):
        arrs[w] = a
    return arrs, (res[-1] if conv is not None else None)


def _grad_shard(ref, w, chip, n):
    start = pl.multiple_of(chip * n, 128)
    if SHARD_AXES[w] == 2:
        return ref.at[:, pl.ds(start, n)]
    return ref.at[pl.ds(start, n), :]


def _slot_shape(g, w):
    shape = list(g.shape)
    shape[SHARD_AXES[w] - 1] //= N_CHIPS
    return (N_DEV - 1,) + tuple(shape)


def _scatter_copies(g_ref, land_ref, send_sems, recv_sems, layer, w):
    x, y, c = _mesh_pos()
    n = g_ref.shape[SHARD_AXES[w] - 1] // N_CHIPS
    out = []
    for r in range(1, N_DEV):
        tx, ty, tc = _flip(x, r & 4), _flip(y, r & 2), _flip(c, r & 1)
        cp = pltpu.make_async_remote_copy(
            src_ref=_grad_shard(g_ref, w, 2 * tx + ty, n), dst_ref=land_ref.at[r - 1], send_sem=send_sems.at[r - 1],
            recv_sem=recv_sems.at[r - 1], device_id=(tx, ty, tc), device_id_type=MESH)
        out.append((cp, (c != layer) if r & 1 else (c == layer)))
    return out


def _scatter_start(items, layer, name):
    n = len(items)

    def body(*refs):
        for i, (w, _, _) in enumerate(items):
            g_ref, land_ref = refs[2 * i], refs[2 * i + 1]
            send_sems, recv_sems = refs[2 * n + 2 * i], refs[2 * n + 2 * i + 1]
            for cp, mine in _scatter_copies(g_ref, land_ref, send_sems, recv_sems, layer, w):
                @pl.when(mine)
                def _():
                    cp.start()
        refs[-1][...] = jnp.zeros_like(refs[-1])

    operands = [a for _, g, land in items for a in (g, land)]
    res = pl.pallas_call(
        body, name=name,
        out_shape=(pltpu.SemaphoreType.DMA((N_DEV - 1,)),) * (2 * n)
        + tuple(pltpu.HBM(a.shape, a.dtype) for a in operands) + (jax.ShapeDtypeStruct((HALO, 128), F32),),
        in_specs=(HBM,) * (2 * n),
        out_specs=(SEM,) * (2 * n) + (HBM,) * (2 * n) + (pl.BlockSpec(memory_space=pltpu.VMEM),),
        input_output_aliases={i: 2 * n + i for i in range(2 * n)},
        compiler_params=pltpu.CompilerParams(has_side_effects=EFFECT),
    )(*[pltpu.with_memory_space_constraint(a, pltpu.HBM) for a in operands])
    return [(res[2 * i], res[2 * i + 1], res[2 * n + 2 * i], res[2 * n + 2 * i + 1]) for i in range(n)], res[-1]


def _scatter_wait(started, land, after, w, name):
    def body(g0_ref, g1_ref, land_ref, ss0, rs0, ss1, rs1, after_ref, g0_out, g1_out, land_out):
        c = lax.axis_index("c")
        for layer, g_ref, ss, rs in ((0, g0_ref, ss0, rs0), (1, g1_ref, ss1, rs1)):
            for cp, mine in _scatter_copies(g_ref, land_ref, ss, rs, layer, w):
                @pl.when(mine)
                def _():
                    cp.wait_send()

                @pl.when(c == layer)
                def _():
                    cp.wait_recv()

    (ss0, rs0, g0), (ss1, rs1, g1) = started
    return pl.pallas_call(
        body, name=name,
        out_shape=(pltpu.HBM(g0.shape, g0.dtype), pltpu.HBM(g1.shape, g1.dtype), pltpu.HBM(land.shape, land.dtype)),
        in_specs=(HBM, HBM, HBM, SEM, SEM, SEM, SEM, ANY), out_specs=(HBM, HBM, HBM),
        input_output_aliases={0: 0, 1: 1, 2: 2}, compiler_params=pltpu.CompilerParams(has_side_effects=EFFECT),
    )(g0, g1, land, ss0, rs0, ss1, rs1, after)


def _sum_slots(g0, g1, slots, w, pos_arr, name):
    _, rows, cols = slots.shape
    tr = min(256, rows)
    nr = rows // tr
    if SHARD_AXES[w] == 2:
        own = pl.BlockSpec((tr, cols), lambda i, pos: (i, pos[0]))
    else:
        own = pl.BlockSpec((tr, cols), lambda i, pos: (pos[0] * nr + i, 0))

    def body(pos_ref, own0_ref, own1_ref, s_ref, o_ref):
        acc = jnp.where(pos_ref[1] == 0, own0_ref[...], own1_ref[...]).astype(F32)
        for r in range(N_DEV - 1):
            acc = acc + s_ref[r].astype(F32)
        o_ref[...] = acc

    return pl.pallas_call(
        body, name=name,
        grid_spec=pltpu.PrefetchScalarGridSpec(
            num_scalar_prefetch=1, grid=(nr,),
            in_specs=[own, own, pl.BlockSpec((N_DEV - 1, tr, cols), lambda i, pos: (0, i, 0))],
            out_specs=pl.BlockSpec((tr, cols), lambda i, pos: (i, 0))),
        out_shape=jax.ShapeDtypeStruct((rows, cols), F32), compiler_params=_cparams("parallel"),
    )(pos_arr, g0, g1, slots)


def _swap_copies(refs, n):
    x, y, c = _mesh_pos()
    return [pltpu.make_async_remote_copy(src_ref=refs[w], dst_ref=refs[n + w], send_sem=refs[2 * n].at[w],
                                         recv_sem=refs[2 * n + 1].at[w], device_id=(x, y, 1 - c), device_id_type=MESH)
            for w in range(n)]


def _swap_start(halves, name):
    n = len(halves)

    def body(*refs):
        for cp in _swap_copies(refs, n):
            cp.start()

    operands = list(halves) + [lax.empty(h.shape, h.dtype) for h in halves]
    res = pl.pallas_call(
        body, name=name,
        out_shape=(pltpu.SemaphoreType.DMA((n,)),) * 2 + tuple(pltpu.HBM(a.shape, a.dtype) for a in operands),
        in_specs=(HBM,) * (2 * n), out_specs=(SEM,) * 2 + (HBM,) * (2 * n),
        input_output_aliases={i: 2 + i for i in range(2 * n)},
        compiler_params=pltpu.CompilerParams(has_side_effects=EFFECT),
    )(*[pltpu.with_memory_space_constraint(a, pltpu.HBM) for a in operands])
    return res[0], res[1], list(res[2:2 + n]), list(res[2 + n:])


def _swap_wait(send_sems, recv_sems, halves, lands, after, name):
    n = len(halves)

    def body(*refs):
        for cp in _swap_copies(refs, n):
            cp.wait_send()
            cp.wait_recv()

    operands = list(halves) + list(lands)
    res = pl.pallas_call(
        body, name=name, out_shape=tuple(pltpu.HBM(a.shape, a.dtype) for a in operands),
        in_specs=(HBM,) * (2 * n) + (SEM, SEM, ANY), out_specs=(HBM,) * (2 * n),
        input_output_aliases={i: i for i in range(2 * n)},
        compiler_params=pltpu.CompilerParams(has_side_effects=EFFECT),
    )(*operands, send_sems, recv_sems, after)
    return list(res[n:])


def _adamw_math(w, g, m, v):
    m = ADAM_B1 * m + (1.0 - ADAM_B1) * g
    v = ADAM_B2 * v + (1.0 - ADAM_B2) * jnp.square(g)
    m_hat = m / (1.0 - ADAM_B1 ** ADAM_STEP)
    v_hat = v / (1.0 - ADAM_B2 ** ADAM_STEP)
    delta = -ADAM_LR * (m_hat / (jnp.sqrt(v_hat) + ADAM_EPS) + ADAM_WD * w)
    return delta, m, v


def _adamw(w, g, m, v, filled, pos_arr, name):
    shape = w.shape
    _, rows, cols = shape
    tr = min(256, rows)

    def body(pos_ref, w_ref, g_ref, m_ref, v_ref, *rest):
        go_ref, d_ref, m2_ref, v2_ref = rest[-4:]
        g = g_ref[...]
        go_ref[...] = g
        d_ref[...], m2_ref[...], v2_ref[...] = _adamw_math(w_ref[...], g, m_ref[...], v_ref[...])

    def layer(pos):
        return pos[1] if filled is None else 1 - pos[1]

    full = pl.BlockSpec((None, tr, cols), lambda i, pos: (layer(pos), i, 0))
    half = pl.BlockSpec((tr, cols), lambda i, pos: (i, 0))
    n_in = 5
    return pl.pallas_call(
        body, name=name,
        grid_spec=pltpu.PrefetchScalarGridSpec(
            num_scalar_prefetch=1, grid=(rows // tr,),
            in_specs=[full, half, full, full] + ([] if filled is None else [ANY] * 4), out_specs=[full] * 4),
        out_shape=[jax.ShapeDtypeStruct(shape, F32)] * 4,
        input_output_aliases={} if filled is None else {n_in + k: k for k in range(4)},
        compiler_params=_cparams("parallel"),
    )(pos_arr, w, g, m, v, *([] if filled is None else filled))


def _small_sync(part, w, m, v):
    rows, cols = part.shape

    def body(p_ref, w_ref, m_ref, v_ref, g_ref, d_ref, m2_ref, v2_ref, slots, send_sems, recv_sems):
        x, y, c = _mesh_pos()
        me = 4 * x + 2 * y + c
        slots[me] = p_ref[...]
        sends = []
        for r in range(1, N_DEV):
            to = (_flip(x, r & 4), _flip(y, r & 2), _flip(c, r & 1))
            sends.append(pltpu.make_async_remote_copy(
                src_ref=p_ref, dst_ref=slots.at[me], send_sem=send_sems.at[r - 1], recv_sem=recv_sems.at[r - 1],
                device_id=to, device_id_type=MESH))
        for cp in sends:
            cp.start()
        for cp in sends:
            cp.wait_recv()
        for cp in sends:
            cp.wait_send()
        g = slots[0]
        for i in range(1, N_DEV):
            g = g + slots[i]
        g_ref[...] = g
        d_ref[...], m2_ref[...], v2_ref[...] = _adamw_math(w_ref[...], g, m_ref[...], v_ref[...])

    vm = pl.BlockSpec(memory_space=pltpu.VMEM)
    return pl.pallas_call(
        body, name="small_sync", in_specs=[vm] * 4, out_specs=[vm] * 4,
        out_shape=[jax.ShapeDtypeStruct((rows, cols), F32)] * 4,
        scratch_shapes=[pltpu.VMEM((N_DEV, rows, cols), F32), pltpu.SemaphoreType.DMA((N_DEV - 1,)),
                        pltpu.SemaphoreType.DMA((N_DEV - 1,))],
    )(part, w, m, v)


PACK_W = 256


def _pack_rows(n):
    return -(-n // (HALO * PACK_W)) * HALO


def _pack_small(parts):
    out = []
    for a in parts:
        flat = a.reshape(-1)
        out.append(jnp.pad(flat, (0, _pack_rows(flat.size) * PACK_W - flat.size)).reshape(-1, PACK_W))
    return jnp.concatenate(out, axis=0)


def _unpack_small(p, shapes):
    out, row = [], 0
    for shape in shapes:
        n = 1
        for k in shape:
            n *= k
        out.append(p[row:row + _pack_rows(n)].reshape(-1)[:n].reshape(shape))
        row += _pack_rows(n)
    return out


def kernel(x, w_in, conv_w, sinks, g_mix, g_group, w_o, g_mlp, w_ff_in, w_ff_out, g_final, loss_target, m_w_in, m_conv_w, m_sinks, m_g_mix, m_g_group, m_w_o, m_g_mlp, m_w_ff_in, m_w_ff_out, m_g_final, v_w_in, v_conv_w, v_sinks, v_g_mix, v_g_group, v_w_o, v_g_mlp, v_w_ff_in, v_w_ff_out, v_g_final):
    chip = 2 * lax.axis_index("x") + lax.axis_index("y")
    conv_n = conv_w.shape[2]

    pos_arr = jnp.stack([chip, lax.axis_index("c")]).astype(jnp.int32)
    shards = (w_in, w_o, w_ff_in, w_ff_out)
    conv_tile = jnp.pad(conv_w.reshape(6, conv_n), ((0, HALO - 6), (0, 128 - conv_n)))
    placed = [_place_shard(w_in, pos_arr[:1], "place_shard_0"), None, None, None]
    sems_a, placed, conv_thru = _gather_start(
        GATHER_STARTS[0], placed, (conv_tile, lax.empty((N_CHIPS,) + conv_tile.shape, conv_tile.dtype)),
        "gather_start_0")
    for i in range(1, N_BIG):
        placed[i] = _place_shard(shards[i], pos_arr[:1], "place_shard_%d" % i)
    full = {"arrs": placed, "conv": None, "sems": list(sems_a[:2])}
    target = _to_strips(loss_target[0], placed[:1], "to_strips_target")

    def fetch(stage, layer, after):
        k = 2 * layer + stage
        sems = full["sems"][2 * k:2 * k + 2]
        if k == 0:
            full["arrs"], land = _gather_wait(0, sems, full["arrs"], (sems_a[-2:], *conv_thru), (after, target),
                                              "gather_wait_0")
            conv_all = lax.dynamic_update_slice(land, conv_tile[None], (chip, 0, 0))
            full["conv"] = conv_all[:, :6, :conv_n].reshape(N_CHIPS, 2, 3, conv_n).transpose(1, 2, 0, 3).reshape(
                2, 3, CONV_CH)
            sems_b, full["arrs"], rest = _gather_start(GATHER_STARTS[1], full["arrs"], None, "gather_start_1",
                                                       through=full["arrs"][0])
            full["arrs"][0] = rest[-1]
            full["sems"] += list(sems_b)
        else:
            full["arrs"], _ = _gather_wait(k, sems, full["arrs"], None, (after,), "gather_wait_%d" % k)
        if k == 1:
            sems_c, full["arrs"], _ = _gather_start(GATHER_STARTS[2], full["arrs"], None, "gather_start_2")
            full["sems"] += list(sems_c)
        return (*full["arrs"], full["conv"])

    lands, started, pending = [None] * N_BIG, {}, []

    def emit(layer, w, g):
        if lands[w] is None:
            lands[w] = lax.empty(_slot_shape(g, w), g.dtype)
        pending.append((w, g, lands[w]))
        if not (w == 0 or (layer == 0 and w == 1)):
            return jnp.zeros((), F32)
        name = "scatter_start_%d_%d" % (layer, len(pending))
        done, token = _scatter_start(list(pending), layer, name)
        for (w_i, _, _), (ss, rs, g_thru, land) in zip(pending, done):
            started[layer, w_i], lands[w_i] = (ss, rs, g_thru), land
        pending.clear()
        return token[0, 0]

    loss_tile, dx, grads, dg_final = _local_step(_to_strips(x[0], placed, "to_strips_x"), target, fetch,
                                                 w_ff_in.shape[2] * N_CHIPS,
                                                 sinks, g_mix, g_group, g_mlp, g_final, emit)

    wmv = ((w_in, m_w_in, v_w_in), (w_o, m_w_o, v_w_o), (w_ff_in, m_w_ff_in, v_w_ff_in),
           (w_ff_out, m_w_ff_out, v_w_ff_out))
    big, after = [None] * N_BIG, dx
    for name, ws in (("swap_rest", (1, 2, 3)), ("swap_in", (0,))):
        own = []
        for w in ws:
            g0, g1, slots = _scatter_wait((started[0, w], started[1, w]), lands[w], after, w, "scatter_wait_%d" % w)
            own.append(_sum_slots(g0, g1, slots, w, pos_arr, "sum_slots_%d" % w))
        send_sems, recv_sems, own, zones = _swap_start(own, name + "_start")
        for w, g in zip(ws, own):
            big[w] = _adamw(wmv[w][0], g, wmv[w][1], wmv[w][2], None, pos_arr, "adamw_own_%d" % w)
        theirs = _swap_wait(send_sems, recv_sems, own, zones, big[ws[-1]][1], name + "_wait")
        for w, g in zip(ws, theirs):
            big[w] = _adamw(wmv[w][0], g, wmv[w][1], wmv[w][2], big[w], pos_arr, "adamw_other_%d" % w)
        after = big[ws[-1]][1]

    def both(i):
        return jnp.stack([grads[0][i][0], grads[1][i][0]])
    dconv = jnp.stack([grads[0][0][:3], grads[1][0][:3]])
    dsinks = jnp.stack([grads[0][1][0, ::HEAD_DIM], grads[1][1][0, ::HEAD_DIM]])
    part = _pack_small([both(2), both(3), both(4), dg_final[0], dconv, dsinks, loss_tile[0, 0]])

    def spread(shard):
        return lax.dynamic_update_slice(jnp.zeros((2, 3, CONV_CH), F32), shard, (0, 0, chip * conv_n))
    zero = jnp.zeros((), F32)
    packs = [_pack_small([a, b, c_, e, spread(f), g_, zero]) for a, b, c_, e, f, g_ in (
        (g_mix, g_group, g_mlp, g_final, conv_w, sinks),
        (m_g_mix, m_g_group, m_g_mlp, m_g_final, m_conv_w, m_sinks),
        (v_g_mix, v_g_group, v_g_mlp, v_g_final, v_conv_w, v_sinks))]
    shapes = [g_mix.shape, g_group.shape, g_mlp.shape, g_final.shape, (2, 3, CONV_CH), sinks.shape, ()]
    small = [_unpack_small(p, shapes) for p in _small_sync(part, *packs)]

    def shard_of(full):
        return lax.dynamic_slice(full, (0, 0, chip * conv_n), (2, 3, conv_n))
    small = [(s[0], s[1], s[2], s[3], shard_of(s[4]), s[5], s[6]) for s in small]
    loss = small[0][6]

    def ordered(kind):
        b = [big[i][kind] for i in range(N_BIG)]
        s = small[kind]
        return [b[0], s[4], s[5], s[0], s[1], b[1], s[2], b[2], b[3], s[3]]

    return (loss, dx[None], *ordered(0), *ordered(1), *ordered(2), *ordered(3))
```

```python
import functools

import jax
import jax.numpy as jnp
from jax import lax
from jax.experimental import pallas as pl
from jax.experimental.pallas import tpu as pltpu

HEAD_DIM = 64
N_HEADS = 6
C_GROUP = 3
A_WIDTH = N_HEADS * HEAD_DIM
C_KV_WIDTH = 2 * HEAD_DIM
CONV_CH = 256
ZA_W = 3 * A_WIDTH
ZB_W = 3 * CONV_CH
ZC_W = A_WIDTH + 2 * C_KV_WIDTH
IN_WIDTH = ZA_W + ZB_W + ZC_W
MIX_WIDTH = A_WIDTH + CONV_CH + A_WIDTH
DILATIONS = (1, 4, 16)
A_MAX_DIST = 128
C_MAX_DIST = 127
TQ = 128
EPS = 1e-6
SCALE = HEAD_DIM ** -0.5
NEG = -1e30
HALO = 8

ADAM_LR = 0.001
ADAM_B1 = 0.9
ADAM_B2 = 0.999
ADAM_EPS = 1e-08
ADAM_WD = 0.01
ADAM_STEP = 10

BF = jnp.bfloat16
F32 = jnp.float32
MESH = pl.DeviceIdType.MESH
VMEM_LIMIT = 56 * 1024 * 1024


def _cparams(*sem):
    return pltpu.CompilerParams(dimension_semantics=sem, vmem_limit_bytes=VMEM_LIMIT)


def _nt(a, b):
    return lax.dot_general(a, b, (((1,), (1,)), ((), ())), preferred_element_type=F32)


def _tn(a, b):
    return lax.dot_general(a, b, (((0,), (0,)), ((), ())), preferred_element_type=F32)


def _nn(a, b):
    return jnp.dot(a, b, preferred_element_type=F32)


def _rows(tb, w):
    return pl.BlockSpec((tb, w), lambda i: (i, 0))


def _whole(shape):
    return pl.BlockSpec(shape, lambda *_: (0,) * len(shape))


def _layer(shape, l):
    return pl.BlockSpec((None,) + shape, lambda *_: (l,) + (0,) * len(shape))


def _rms_scale(v):
    return lax.rsqrt(jnp.mean(v * v, axis=-1, keepdims=True) + EPS)


def _norm_bwd(dxhat, xhat, r):
    return r * (dxhat - xhat * jnp.mean(dxhat * xhat, axis=-1, keepdims=True))


def _qkv_fwd(x, g, w_all, l, tb):
    s, d = x.shape

    def body(x_ref, g_ref, w_ref, h_ref, za_ref, zb_ref, zc_ref):
        xv = x_ref[...]
        h = ((xv * _rms_scale(xv)) * g_ref[...]).astype(BF)
        h_ref[...] = h
        z = jnp.concatenate([_nn(h, w_ref[k]) for k in range(N_CHIPS)], axis=1)
        za_ref[...] = z[:, :ZA_W]
        zb_ref[...] = z[:, ZA_W:ZA_W + ZB_W]
        zc_ref[...] = z[:, ZA_W + ZB_W:]

    return pl.pallas_call(
        body, grid=(s // tb,), name="qkv_fwd",
        in_specs=[_rows(tb, d), _whole((1, d)), _layer((N_CHIPS, d, IN_WIDTH // N_CHIPS), l)],
        out_specs=[_rows(tb, d), _rows(tb, ZA_W), _rows(tb, ZB_W), _rows(tb, ZC_W)],
        out_shape=[jax.ShapeDtypeStruct((s, d), BF), jax.ShapeDtypeStruct((s, ZA_W), F32),
                   jax.ShapeDtypeStruct((s, ZB_W), F32), jax.ShapeDtypeStruct((s, ZC_W), F32)],
        compiler_params=_cparams("parallel"),
    )(x, g, w_all)


N_STRIPS = 16


def _strips(a):
    s, w = a.shape
    return a.reshape(4, 4, s // N_STRIPS, w)


P_ROWS = {16: TQ, 4: 32, 1: 8}


def _p_sub(s, dil, most):
    while (s // dil // TQ) % most:
        most //= 2
    return most


def _p_grid(s, dil, n_sub):
    nb = s // dil // TQ // n_sub
    return {16: (4, 4, nb), 4: (4, nb), 1: (nb,)}[dil]


def _p_spec(dil, cw, col, n_sub, prev=False):
    rows = P_ROWS[dil] * (1 if prev else n_sub)

    def blk(j):
        return jnp.maximum(n_sub * j - 1, 0) if prev else j
    if dil == 16:
        return pl.BlockSpec((None, None, rows, cw), lambda f, e, j: (f, e, blk(j), col))
    if dil == 4:
        return pl.BlockSpec((None, 4, rows, cw), lambda f, j: (f, 0, blk(j), col))
    return pl.BlockSpec((4, 4, rows, cw), lambda j: (0, 0, blk(j), col))


def _block_pos(i, dil):
    if dil == 16:
        return i
    if dil == 4:
        return 4 * (i % 32) + i // 32
    return 16 * (i % 8) + 4 * ((i // 8) % 4) + i // 32


def _band_mask(b, dil, max_dist):
    qi = _block_pos(lax.broadcasted_iota(jnp.int32, (TQ, 2 * TQ), 0), dil)
    col = lax.broadcasted_iota(jnp.int32, (TQ, 2 * TQ), 1)
    cur = col >= TQ
    dist = qi - _block_pos(col % TQ, dil) + jnp.where(cur, 0, TQ)
    return (dist >= 0) & (dist <= max_dist) & (cur | (b > 0))


def _hs(h):
    return slice(h * HEAD_DIM, (h + 1) * HEAD_DIM)


def _ld(ref, cols, rows=slice(None)):
    v = ref[..., rows, cols]
    return v.reshape(TQ, v.shape[-1])


def _st(ref, cols, val, rows=slice(None)):
    lead = ref.shape[:-2] + (ref.shape[-2] if rows == slice(None) else rows.stop - rows.start,)
    ref[..., rows, cols] = val.reshape(lead + (val.shape[-1],))


def _attn_fwd(z, dil, kw, kcol, vcol, n_rep, max_dist, name):
    s, zw = z.shape
    n_sub = _p_sub(s, dil, 2)
    grid = _p_grid(s, dil, n_sub)

    def body(q_ref, kp_ref, kc_ref, vp_ref, vc_ref, o_ref, lse_ref):
        for t in range(n_sub):
            rows = slice(t * P_ROWS[dil], (t + 1) * P_ROWS[dil])
            before = (slice(None),) if t == 0 else (slice((t - 1) * P_ROWS[dil], t * P_ROWS[dil]),)
            kb_ref, vb_ref = (kp_ref, vp_ref) if t == 0 else (kc_ref, vc_ref)
            mask = _band_mask(n_sub * pl.program_id(len(grid) - 1) if t == 0 else 1, dil, max_dist)
            scs, v2s = [], []
            for kh in range(N_HEADS // n_rep):
                k2 = jnp.concatenate([_ld(kb_ref, _hs(kh), *before), _ld(kc_ref, _hs(kh), rows)], axis=0).astype(BF)
                v2s.append(jnp.concatenate([_ld(vb_ref, _hs(kh), *before), _ld(vc_ref, _hs(kh), rows)],
                                           axis=0).astype(BF))
                for h in range(kh * n_rep, (kh + 1) * n_rep):
                    q = (_ld(q_ref, _hs(h), rows) * SCALE).astype(BF)
                    scs.append(jnp.where(mask, _nt(q, k2), NEG))
            for h, sc in enumerate(scs):
                m = jnp.max(sc, axis=1, keepdims=True)
                p = jnp.exp(sc - m)
                l = jnp.sum(p, axis=1, keepdims=True)
                _st(o_ref, _hs(h), _nn(p.astype(BF), v2s[h // n_rep]) / l, rows)
                _st(lse_ref, _hs(h), jnp.broadcast_to(m + jnp.log(l), (TQ, HEAD_DIM)), rows)

    res = pl.pallas_call(
        body, grid=grid, name=name,
        in_specs=[_p_spec(dil, A_WIDTH, 0, n_sub), _p_spec(dil, kw, kcol, n_sub, True), _p_spec(dil, kw, kcol, n_sub),
                  _p_spec(dil, kw, vcol, n_sub, True), _p_spec(dil, kw, vcol, n_sub)],
        out_specs=[_p_spec(dil, A_WIDTH, 0, n_sub)] * 2,
        out_shape=[jax.ShapeDtypeStruct((4, 4, s // N_STRIPS, A_WIDTH), F32)] * 2,
        compiler_params=_cparams(*(("parallel",) * len(grid))),
    )(*[_strips(z)] * 5)
    return [a.reshape(s, A_WIDTH) for a in res]


def _attn_merge(parts_a, part_c, sink_row, tb):
    s = part_c[0].shape[0]
    n_a = len(parts_a)

    def body(*refs):
        ins, sink_ref = refs[:2 * n_a + 2], refs[2 * n_a + 2]
        ya_ref, lsea_ref, yc_ref, lsec_ref = refs[2 * n_a + 3:]
        lses = [ins[2 * p + 1][...] for p in range(n_a)]
        m = functools.reduce(jnp.maximum, lses)
        ws = [jnp.exp(v - m) for v in lses]
        l = functools.reduce(jnp.add, ws)
        ya_ref[...] = functools.reduce(jnp.add, [w * ins[2 * p][...] for p, w in enumerate(ws)]) / l
        lsea_ref[...] = m + jnp.log(l)
        o_c, lse_c = [r[...] for r in ins[2 * n_a:]]
        sk = sink_ref[...]
        m2 = jnp.maximum(lse_c, sk)
        w = jnp.exp(lse_c - m2)
        l2 = w + jnp.exp(sk - m2)
        yc_ref[...] = o_c * (w / l2)
        lsec_ref[...] = m2 + jnp.log(l2)

    return pl.pallas_call(
        body, grid=(s // tb,), name="attn_merge",
        in_specs=[_rows(tb, A_WIDTH)] * (2 * n_a + 2) + [_whole((1, A_WIDTH))],
        out_specs=[_rows(tb, A_WIDTH)] * 4, out_shape=[jax.ShapeDtypeStruct((s, A_WIDTH), F32)] * 4,
        compiler_params=_cparams("parallel"),
    )(*[a for part in parts_a + [part_c] for a in part], sink_row)


def _shift_down(v, n, halo):
    rows = v.shape[0]
    out = pltpu.roll(v, n, 0)
    row = lax.broadcasted_iota(jnp.int32, v.shape, 0)
    for t in range(n):
        out = jnp.where(row == t, halo[HALO - n + t:HALO - n + t + 1, :], out)
    return out


def _shift_up(v, n, halo):
    rows = v.shape[0]
    out = pltpu.roll(v, rows - n, 0)
    row = lax.broadcasted_iota(jnp.int32, v.shape, 0)
    for t in range(n):
        out = jnp.where(row == rows - n + t, halo[t:t + 1, :], out)
    return out


def _strip(v, b):
    return v[b % 4, b // 4]


def _conv_strips(zb, prev, cw):
    gb = [_strip(zb, b)[:, :CONV_CH] for b in range(N_STRIPS)]
    gc = [_strip(zb, b)[:, CONV_CH:2 * CONV_CH] for b in range(N_STRIPS)]
    xb = [_strip(zb, b)[:, 2 * CONV_CH:] for b in range(N_STRIPS)]
    u = [g * v for g, v in zip(gc, xb)]
    uh = prev[:, :, CONV_CH:2 * CONV_CH] * prev[:, :, 2 * CONV_CH:]
    wrapped = {14: _shift_down(u[14], 1, uh[2]), 15: _shift_down(u[15], 1, uh[3])}
    u1 = [u[b - 1] if b >= 1 else wrapped[15] for b in range(N_STRIPS)]
    u2 = [u[b - 2] if b >= 2 else wrapped[14 + b] for b in range(N_STRIPS)]
    c = [cw[0:1, :] * u2[b] + cw[1:2, :] * u1[b] + cw[2:3, :] * u[b] for b in range(N_STRIPS)]
    return gb, gc, xb, u, u1, u2, c


def _strip_rows(ta, w):
    return pl.BlockSpec((4, 4, ta, w), lambda i: (0, 0, i, 0))


def _prev_rows(ta, w):
    return pl.BlockSpec((4, None, HALO, w), lambda i: (0, 3, jnp.maximum(i * (ta // HALO) - 1, 0), 0))


def _next_rows(ta, w, nblk):
    return pl.BlockSpec((4, None, HALO, w),
                        lambda i: (0, 0, jnp.minimum((i + 1) * (ta // HALO), nblk * (ta // HALO) - 1), 0))


def _mix_fwd(x, ya, yc, zb, cw, gg, wo_all, l, tb):
    s, d = x.shape
    ta = tb // N_STRIPS

    def body(x_ref, ya_ref, yc_ref, zb_ref, zbp_ref, cw_ref, gg_ref, wo_ref, x1_ref, yb_ref):
        i = pl.program_id(0)
        prev = jnp.where(i > 0, zbp_ref[...], 0.0)
        gb, _, _, _, _, _, c = _conv_strips(zb_ref[...], prev, cw_ref[...])
        for b in range(N_STRIPS):
            yb_ref[b % 4, b // 4] = gb[b] * c[b]
        yb = yb_ref[...].reshape(tb, CONV_CH)
        ya, yc = ya_ref[...].reshape(tb, A_WIDTH), yc_ref[...].reshape(tb, A_WIDTH)
        n = jnp.concatenate([ya * _rms_scale(ya), yb * _rms_scale(yb), yc * _rms_scale(yc)], axis=1)
        n = (n * gg_ref[...]).astype(BF)
        x1 = x_ref[...].reshape(tb, d) + _nn(n, wo_ref[...].reshape(MIX_WIDTH, d))
        x1_ref[...] = x1.reshape(4, 4, ta, d)

    res = pl.pallas_call(
        body, grid=(s // tb,), name="mix_fwd",
        in_specs=[_strip_rows(ta, d), _strip_rows(ta, A_WIDTH), _strip_rows(ta, A_WIDTH), _strip_rows(ta, ZB_W),
                  _prev_rows(ta, ZB_W), _whole((HALO, CONV_CH)), _whole((1, MIX_WIDTH)),
                  _layer((N_CHIPS, MIX_WIDTH // N_CHIPS, d), l)],
        out_specs=[_strip_rows(ta, d), _strip_rows(ta, CONV_CH)],
        out_shape=[jax.ShapeDtypeStruct((4, 4, s // N_STRIPS, d), F32),
                   jax.ShapeDtypeStruct((4, 4, s // N_STRIPS, CONV_CH), F32)],
        compiler_params=_cparams("parallel"),
    )(_strips(x), _strips(ya), _strips(yc), _strips(zb), _strips(zb), cw, gg, wo_all)
    return res[0].reshape(s, d), res[1].reshape(s, CONV_CH)


def _mlp_fwd(x1, g, w1_all, w2_all, l, tb, tf):
    s, d = x1.shape
    ff = w1_all.shape[1] * w1_all.shape[3]
    nj = ff // tf

    def body(x_ref, g_ref, w1_ref, w2_ref, x2_ref, h2_ref, ap_ref, acc):
        j = pl.program_id(1)

        @pl.when(j == 0)
        def _():
            xv = x_ref[...]
            h2_ref[...] = ((xv * _rms_scale(xv)) * g_ref[...]).astype(BF)
            acc[...] = jnp.zeros_like(acc)

        ap = _nn(h2_ref[...], w1_ref[...])
        ap_ref[...] = ap.astype(BF)
        a = jnp.square(jnp.maximum(ap, 0.0)).astype(BF)
        acc[...] += _nn(a, w2_ref[...])

        @pl.when(j == nj - 1)
        def _():
            x2_ref[...] = x_ref[...] + acc[...]

    return pl.pallas_call(
        body, grid=(s // tb, nj), name="mlp_fwd",
        in_specs=[pl.BlockSpec((tb, d), lambda i, j: (i, 0)), _whole((1, d)),
                  pl.BlockSpec((None, None, d, tf), lambda i, j: (l, j, 0, 0)),
                  pl.BlockSpec((None, None, tf, d), lambda i, j: (l, j, 0, 0))],
        out_specs=[pl.BlockSpec((tb, d), lambda i, j: (i, 0)), pl.BlockSpec((tb, d), lambda i, j: (i, 0)),
                   pl.BlockSpec((tb, tf), lambda i, j: (i, j))],
        out_shape=[jax.ShapeDtypeStruct((s, d), F32), jax.ShapeDtypeStruct((s, d), BF),
                   jax.ShapeDtypeStruct((s, ff), BF)],
        scratch_shapes=[pltpu.VMEM((tb, d), F32)],
        compiler_params=_cparams("parallel", "arbitrary"),
    )(x1, g, w1_all, w2_all)


def _loss_head(x, g, tgt, tb):
    s, d = x.shape

    def body(x_ref, g_ref, t_ref, dx_ref, loss_ref, dg_ref):
        i = pl.program_id(0)

        @pl.when(i == 0)
        def _():
            loss_ref[...] = jnp.zeros_like(loss_ref)
            dg_ref[...] = jnp.zeros_like(dg_ref)

        xv = x_ref[...]
        r = _rms_scale(xv)
        xhat = xv * r
        err = xhat * g_ref[...] - t_ref[...]
        part = jnp.sum(jnp.mean(jnp.square(err), axis=-1, keepdims=True), axis=0, keepdims=True)
        loss_ref[...] += 0.5 * part
        dy = err * (1.0 / d)
        dg_ref[...] += jnp.sum(dy * xhat, axis=0, keepdims=True)
        dx_ref[...] = _norm_bwd(dy * g_ref[...], xhat, r)

    return pl.pallas_call(
        body, grid=(s // tb,), name="loss_head",
        in_specs=[_rows(tb, d), _whole((1, d)), _rows(tb, d)],
        out_specs=[_rows(tb, d), _whole((HALO, 128)), _whole((HALO, d))],
        out_shape=[jax.ShapeDtypeStruct((s, d), F32), jax.ShapeDtypeStruct((HALO, 128), F32),
                   jax.ShapeDtypeStruct((HALO, d), F32)],
        compiler_params=_cparams("arbitrary"),
    )(x, g, tgt)


def _mlp_bwd(dx2, x1, ap, g, w1_all, w2_all, l, tb, tf):
    s, d = x1.shape
    ff = ap.shape[1]
    nj = ff // tf

    def body(dx2_ref, x1_ref, ap_ref, g_ref, w1_ref, w2_ref, dx1_ref, dap_ref, dg_ref, acc):
        i, j = pl.program_id(0), pl.program_id(1)

        @pl.when((i == 0) & (j == 0))
        def _():
            dg_ref[...] = jnp.zeros_like(dg_ref)

        @pl.when(j == 0)
        def _():
            acc[...] = jnp.zeros_like(acc)

        da = _nt(dx2_ref[...].astype(BF), w2_ref[...])
        dap = (da * (2.0 * jnp.maximum(ap_ref[...].astype(F32), 0.0))).astype(BF)
        dap_ref[...] = dap
        acc[...] += _nt(dap, w1_ref[...])

        @pl.when(j == nj - 1)
        def _():
            xv = x1_ref[...]
            r = _rms_scale(xv)
            xhat = xv * r
            dh = acc[...]
            dg_ref[...] += jnp.sum(dh * xhat, axis=0, keepdims=True)
            dx1_ref[...] = dx2_ref[...] + _norm_bwd(dh * g_ref[...], xhat, r)

    return pl.pallas_call(
        body, grid=(s // tb, nj), name="mlp_bwd",
        in_specs=[pl.BlockSpec((tb, d), lambda i, j: (i, 0)), pl.BlockSpec((tb, d), lambda i, j: (i, 0)),
                  pl.BlockSpec((tb, tf), lambda i, j: (i, j)),
                  _whole((1, d)), pl.BlockSpec((None, None, d, tf), lambda i, j: (l, j, 0, 0)),
                  pl.BlockSpec((None, None, tf, d), lambda i, j: (l, j, 0, 0))],
        out_specs=[pl.BlockSpec((tb, d), lambda i, j: (i, 0)), pl.BlockSpec((tb, tf), lambda i, j: (i, j)),
                   _whole((HALO, d))],
        out_shape=[jax.ShapeDtypeStruct((s, d), F32), jax.ShapeDtypeStruct((s, ff), BF),
                   jax.ShapeDtypeStruct((HALO, d), F32)],
        scratch_shapes=[pltpu.VMEM((tb, d), F32)],
        compiler_params=_cparams("arbitrary", "arbitrary"),
    )(dx2, x1, ap, g, w1_all, w2_all)


def _wgrad(a, b, tm, tn, ts, name, relu2=False):
    s, m = a.shape
    n = b.shape[1]
    ns = s // ts

    def body(a_ref, b_ref, o_ref, acc):
        k = pl.program_id(2)

        @pl.when(k == 0)
        def _():
            acc[...] = jnp.zeros_like(acc)

        av = a_ref[...]
        if relu2:
            av = jnp.square(jnp.maximum(av.astype(F32), 0.0)).astype(BF)
        acc[...] += _tn(av, b_ref[...].astype(BF))

        @pl.when(k == ns - 1)
        def _():
            o_ref[...] = acc[...].astype(BF)

    return pl.pallas_call(
        body, grid=(m // tm, n // tn, ns), name=name,
        in_specs=[pl.BlockSpec((ts, tm), lambda i, j, k: (k, i)), pl.BlockSpec((ts, tn), lambda i, j, k: (k, j))],
        out_specs=pl.BlockSpec((tm, tn), lambda i, j, k: (i, j)),
        out_shape=jax.ShapeDtypeStruct((m, n), BF),
        scratch_shapes=[pltpu.VMEM((tm, tn), F32)],
        compiler_params=_cparams("parallel", "parallel", "arbitrary"),
    )(a, b)


def _mix_bwd(dx1, ya, yb, yc, lse_c, sink_row, gg, wo_all, l, tb):
    s, d = dx1.shape

    def body(dx_ref, ya_ref, yb_ref, yc_ref, lse_ref, sink_ref, gg_ref, wo_ref,
             n_ref, dya_ref, dyc_ref, da_ref, dc_ref, dyb_ref, dg_ref, dsink_ref):
        i = pl.program_id(0)

        @pl.when(i == 0)
        def _():
            dg_ref[...] = jnp.zeros_like(dg_ref)
            dsink_ref[...] = jnp.zeros_like(dsink_ref)

        dn = _nt(dx_ref[...].astype(BF), wo_ref[...].reshape(MIX_WIDTH, d))
        ys = [ya_ref[...], yb_ref[...], yc_ref[...]]
        rs = [_rms_scale(v) for v in ys]
        nhat = jnp.concatenate([v * r for v, r in zip(ys, rs)], axis=1)
        gg = gg_ref[...]
        n_ref[...] = (nhat * gg).astype(BF)
        dg_ref[...] += jnp.sum(dn * nhat, axis=0, keepdims=True)
        dnh = dn * gg
        bounds = [(0, A_WIDTH), (A_WIDTH, A_WIDTH + CONV_CH), (A_WIDTH + CONV_CH, MIX_WIDTH)]
        dys = [_norm_bwd(dnh[:, lo:hi], nhat[:, lo:hi], r) for (lo, hi), r in zip(bounds, rs)]
        dyb_ref[...] = dys[1]
        for dy, y, dy_ref, dd_ref in ((dys[0], ys[0], dya_ref, da_ref), (dys[2], ys[2], dyc_ref, dc_ref)):
            dy_ref[...] = dy
            t = dy * y
            for h in range(N_HEADS):
                dd_ref[:, _hs(h)] = jnp.broadcast_to(jnp.sum(t[:, _hs(h)], axis=1, keepdims=True), (tb, HEAD_DIM))
        dsink_ref[...] -= jnp.sum(jnp.exp(sink_ref[...] - lse_ref[...]) * dc_ref[...], axis=0, keepdims=True)

    return pl.pallas_call(
        body, grid=(s // tb,), name="mix_bwd",
        in_specs=[_rows(tb, d), _rows(tb, A_WIDTH), _rows(tb, CONV_CH), _rows(tb, A_WIDTH), _rows(tb, A_WIDTH),
                  _whole((1, A_WIDTH)), _whole((1, MIX_WIDTH)), _layer((N_CHIPS, MIX_WIDTH // N_CHIPS, d), l)],
        out_specs=[_rows(tb, MIX_WIDTH), _rows(tb, A_WIDTH), _rows(tb, A_WIDTH), _rows(tb, A_WIDTH),
                   _rows(tb, A_WIDTH), _rows(tb, CONV_CH), _whole((HALO, MIX_WIDTH)), _whole((HALO, A_WIDTH))],
        out_shape=[jax.ShapeDtypeStruct((s, MIX_WIDTH), BF), jax.ShapeDtypeStruct((s, A_WIDTH), F32),
                   jax.ShapeDtypeStruct((s, A_WIDTH), F32), jax.ShapeDtypeStruct((s, A_WIDTH), F32),
                   jax.ShapeDtypeStruct((s, A_WIDTH), F32), jax.ShapeDtypeStruct((s, CONV_CH), F32),
                   jax.ShapeDtypeStruct((HALO, MIX_WIDTH), F32), jax.ShapeDtypeStruct((HALO, A_WIDTH), F32)],
        compiler_params=_cparams("arbitrary"),
    )(dx1, ya, yb, yc, lse_c, sink_row, gg, wo_all)


def _attn_bwd(z, dy, lse, dd, dil, kw, kcol, vcol, n_rep, max_dist, name):
    s, zw = z.shape
    n_sub = _p_sub(s, dil, 2) if n_rep == 1 else 1
    grid = _p_grid(s, dil, n_sub)
    n_kv = N_HEADS // n_rep
    dt = F32 if dil == 1 else BF

    def body(q_ref, kp_ref, kc_ref, vp_ref, vc_ref, dy_ref, lse_ref, dd_ref, dq_ref, dkp_ref, dkc_ref, dvp_ref, dvc_ref):
        for t in range(n_sub):
            rows = slice(t * P_ROWS[dil], (t + 1) * P_ROWS[dil])
            before = (slice(None),) if t == 0 else (slice((t - 1) * P_ROWS[dil], t * P_ROWS[dil]),)
            kb_ref, vb_ref = (kp_ref, vp_ref) if t == 0 else (kc_ref, vc_ref)
            mask = _band_mask(n_sub * pl.program_id(len(grid) - 1) if t == 0 else 1, dil, max_dist)
            k2s, qs, dys, scs, dps = [], [], [], [], []
            for kh in range(n_kv):
                k2s.append(jnp.concatenate([_ld(kb_ref, _hs(kh), *before), _ld(kc_ref, _hs(kh), rows)],
                                           axis=0).astype(BF))
                v2 = jnp.concatenate([_ld(vb_ref, _hs(kh), *before), _ld(vc_ref, _hs(kh), rows)], axis=0).astype(BF)
                for h in range(kh * n_rep, (kh + 1) * n_rep):
                    qs.append((_ld(q_ref, _hs(h), rows) * SCALE).astype(BF))
                    dys.append(_ld(dy_ref, _hs(h), rows).astype(BF))
                    scs.append(jnp.where(mask, _nt(qs[h], k2s[kh]), NEG))
                    dps.append(_nt(dys[h], v2))
            for kh in range(n_kv):
                k2 = k2s[kh]
                dk2 = jnp.zeros((2 * TQ, HEAD_DIM), F32)
                dv2 = jnp.zeros((2 * TQ, HEAD_DIM), F32)
                for h in range(kh * n_rep, (kh + 1) * n_rep):
                    lse_h = _ld(lse_ref, slice(h * HEAD_DIM, h * HEAD_DIM + 1), rows)
                    dd_h = _ld(dd_ref, slice(h * HEAD_DIM, h * HEAD_DIM + 1), rows)
                    p = jnp.exp(scs[h] - lse_h)
                    ds = (p * (dps[h] - dd_h)).astype(BF)
                    _st(dq_ref, _hs(h), (_nn(ds, k2) * SCALE).astype(dt), rows)
                    dk2 = dk2 + _tn(ds, qs[h])
                    dv2 = dv2 + _tn(p.astype(BF), dys[h])
                _st(dkp_ref, _hs(kh), dk2[:TQ].astype(dt), rows)
                _st(dkc_ref, _hs(kh), dk2[TQ:].astype(dt), rows)
                _st(dvp_ref, _hs(kh), dv2[:TQ].astype(dt), rows)
                _st(dvc_ref, _hs(kh), dv2[TQ:].astype(dt), rows)

    args = [_strips(z)] * 5 + [_strips(a) for a in (dy, lse, dd)]
    pair = _p_spec(dil, A_WIDTH, 0, n_sub)
    in_specs = [pair, _p_spec(dil, kw, kcol, n_sub, True), _p_spec(dil, kw, kcol, n_sub),
                _p_spec(dil, kw, vcol, n_sub, True), _p_spec(dil, kw, vcol, n_sub)] + [pair] * 3
    out_specs = [pair] + [_p_spec(dil, kw, 0, n_sub)] * 4
    na = s // N_STRIPS
    out_shape = [jax.ShapeDtypeStruct((4, 4, na, A_WIDTH), dt)] + [jax.ShapeDtypeStruct((4, 4, na, kw), dt)] * 4
    res = pl.pallas_call(
        body, grid=grid, name=name, in_specs=in_specs, out_specs=out_specs, out_shape=out_shape,
        compiler_params=_cparams(*(("parallel",) * len(grid))),
    )(*args)
    return [res[0].reshape(s, A_WIDTH)] + [a.reshape(s, kw) for a in res[1:]]


DZ_TA = 16


def _dz_assemble(parts_a, parts_c, dyb, zb, cw):
    s = zb.shape[0]
    na = s // N_STRIPS
    nb = na // DZ_TA

    def ahead(w, k):
        return pl.BlockSpec((4, 4, DZ_TA, w), lambda i: (0, 0, jnp.minimum(i + k, nb - 1), 0))

    args, in_specs = [], []
    for dil, (dq, dkp, dkc, dvp, dvc) in zip(DILATIONS + (1,), parts_a + [parts_c]):
        w = dkp.shape[1]
        here = _strip_rows(DZ_TA, w)
        if dil == 1:
            args += [dq, dkp, dkp, dkc, dvp, dvp, dvc]
            in_specs += [_strip_rows(DZ_TA, A_WIDTH), here, ahead(w, 1), here, here, ahead(w, 1), here]
        else:
            k = 8 * dil // DZ_TA
            args += [dq, dkp, dkc, dvp, dvc]
            in_specs += [_strip_rows(DZ_TA, A_WIDTH), ahead(w, k), here, ahead(w, k), here]
    n_att = len(args)
    args = [_strips(a) for a in args] + [_strips(dyb), _strips(dyb), _strips(zb), _strips(zb), _strips(zb), cw]
    in_specs += [_strip_rows(DZ_TA, CONV_CH), _next_rows(DZ_TA, CONV_CH, nb), _strip_rows(DZ_TA, ZB_W),
                 _prev_rows(DZ_TA, ZB_W), _next_rows(DZ_TA, ZB_W, nb), _whole((HALO, CONV_CH))]

    def body(*refs):
        att = list(refs[:n_att])
        dyb_ref, dybn_ref, zb_ref, zbp_ref, zbn_ref, cw_ref, dz_ref, dcw_ref = refs[n_att:]
        i = pl.program_id(0)

        @pl.when(i == 0)
        def _():
            dcw_ref[...] = jnp.zeros_like(dcw_ref)

        def shifted(dil):
            if dil == 1:
                dq_r, kp0, kp1, dkc_r, vp0, vp1, dvc_r = [att.pop(0) for _ in range(7)]
                live = i + 1 < nb
                half = DZ_TA // 2
                dkp = jnp.concatenate([kp0[:, :, half:, :], jnp.where(live, kp1[:, :, :half, :], 0.0)], axis=2)
                dvp = jnp.concatenate([vp0[:, :, half:, :], jnp.where(live, vp1[:, :, :half, :], 0.0)], axis=2)
            else:
                dq_r, dkp_r, dkc_r, dvp_r, dvc_r = [att.pop(0) for _ in range(5)]
                live = i + 8 * dil // DZ_TA < nb
                dkp = jnp.where(live, dkp_r[...].astype(F32), 0.0)
                dvp = jnp.where(live, dvp_r[...].astype(F32), 0.0)
            return dq_r[...].astype(F32), dkc_r[...].astype(F32) + dkp, dvc_r[...].astype(F32) + dvp

        dq, dk, dv = shifted(DILATIONS[0])
        for dil in DILATIONS[1:]:
            dq2, dk2, dv2 = shifted(dil)
            dq, dk, dv = dq + dq2, dk + dk2, dv + dv2
        dz_ref[:, :, :, 0:A_WIDTH] = dq.astype(BF)
        dz_ref[:, :, :, A_WIDTH:2 * A_WIDTH] = dk.astype(BF)
        dz_ref[:, :, :, 2 * A_WIDTH:ZA_W] = dv.astype(BF)
        dq, dk, dv = shifted(1)
        c0 = ZA_W + ZB_W
        dz_ref[:, :, :, c0:c0 + A_WIDTH] = dq.astype(BF)
        dz_ref[:, :, :, c0 + A_WIDTH:c0 + A_WIDTH + C_KV_WIDTH] = dk.astype(BF)
        dz_ref[:, :, :, c0 + A_WIDTH + C_KV_WIDTH:IN_WIDTH] = dv.astype(BF)

        cw = cw_ref[...]
        prev = jnp.where(i > 0, zbp_ref[...], 0.0)
        gb, gc, xb, u, u1, u2, c = _conv_strips(zb_ref[...], prev, cw)
        dyb = dyb_ref[...]
        dc = [_strip(dyb, b) * gb[b] for b in range(N_STRIPS)]
        dcn = jnp.where(i + 1 < nb, dybn_ref[...] * zbn_ref[:, :, :CONV_CH], 0.0)
        wrapped = [_shift_up(dc[0], 1, dcn[0]), _shift_up(dc[1], 1, dcn[1])]
        upd = [jnp.zeros((1, CONV_CH), F32)] * 3
        for b in range(N_STRIPS):
            dc1 = dc[b + 1] if b + 1 < N_STRIPS else wrapped[0]
            dc2 = dc[b + 2] if b + 2 < N_STRIPS else wrapped[b + 2 - N_STRIPS]
            du = cw[2:3, :] * dc[b] + cw[1:2, :] * dc1 + cw[0:1, :] * dc2
            f, e = b % 4, b // 4
            dz_ref[f, e, :, ZA_W:ZA_W + CONV_CH] = (_strip(dyb, b) * c[b]).astype(BF)
            dz_ref[f, e, :, ZA_W + CONV_CH:ZA_W + 2 * CONV_CH] = (du * xb[b]).astype(BF)
            dz_ref[f, e, :, ZA_W + 2 * CONV_CH:c0] = (du * gc[b]).astype(BF)
            for t, uu in enumerate((u2[b], u1[b], u[b])):
                upd[t] = upd[t] + jnp.sum(dc[b] * uu, axis=0, keepdims=True)
        row = lax.broadcasted_iota(jnp.int32, (HALO, CONV_CH), 0)
        tile = jnp.zeros((HALO, CONV_CH), F32)
        for t in range(3):
            tile = jnp.where(row == t, upd[t], tile)
        dcw_ref[...] += tile

    dz, dcw = pl.pallas_call(
        body, grid=(nb,), name="dz_assemble", in_specs=in_specs,
        out_specs=[_strip_rows(DZ_TA, IN_WIDTH), _whole((HALO, CONV_CH))],
        out_shape=[jax.ShapeDtypeStruct((4, 4, na, IN_WIDTH), BF), jax.ShapeDtypeStruct((HALO, CONV_CH), F32)],
        compiler_params=_cparams("arbitrary"),
    )(*args)
    return dz.reshape(s, IN_WIDTH), dcw


def _qkv_bwd(dz, dx1, x, g, w_all, l, tb, tokens_out):
    s, d = x.shape
    na, ta = s // N_STRIPS, tb // N_STRIPS

    def body(dz_ref, dx1_ref, x_ref, g_ref, w_ref, dx_ref, dg_ref):
        i = pl.program_id(0)

        @pl.when(i == 0)
        def _():
            dg_ref[...] = jnp.zeros_like(dg_ref)

        n = IN_WIDTH // N_CHIPS
        dz = dz_ref[...].reshape(tb, IN_WIDTH)
        dh = _nt(dz[:, 0:n], w_ref[0])
        for k in range(1, N_CHIPS):
            dh = dh + _nt(dz[:, k * n:(k + 1) * n], w_ref[k])
        xv = x_ref[...].reshape(tb, d)
        r = _rms_scale(xv)
        xhat = xv * r
        dg_ref[...] += jnp.sum(dh * xhat, axis=0, keepdims=True)
        dx = (dx1_ref[...].reshape(tb, d) + _norm_bwd(dh * g_ref[...], xhat, r)).reshape(4, 4, ta, d)
        if tokens_out:
            for b in range(N_STRIPS):
                dx_ref[:, b, :] = _strip(dx, b)
        else:
            dx_ref[...] = dx

    if tokens_out:
        dx_spec, dx_shape = pl.BlockSpec((ta, N_STRIPS, d), lambda i: (i, 0, 0)), (na, N_STRIPS, d)
    else:
        dx_spec, dx_shape = _strip_rows(ta, d), (4, 4, na, d)
    dx, dg = pl.pallas_call(
        body, grid=(s // tb,), name="qkv_bwd",
        in_specs=[_strip_rows(ta, IN_WIDTH), _strip_rows(ta, d), _strip_rows(ta, d), _whole((1, d)),
                  _layer((N_CHIPS, d, IN_WIDTH // N_CHIPS), l)],
        out_specs=[dx_spec, _whole((HALO, d))],
        out_shape=[jax.ShapeDtypeStruct(dx_shape, F32), jax.ShapeDtypeStruct((HALO, d), F32)],
        compiler_params=_cparams("arbitrary"),
    )(_strips(dz), _strips(dx1), _strips(x), g, w_all)
    return dx.reshape(s, d), dg


def _tile_rows(rows):
    return jnp.pad(rows, ((0, HALO - rows.shape[0]), (0, 0)))


def _to_strips(a, after, name):
    s, d = a.shape
    na = s // N_STRIPS
    ta = min(32, na)

    def body(a_ref, *rest):
        for b in range(N_STRIPS):
            rest[-1][b % 4, b // 4] = a_ref[:, b, :]

    return pl.pallas_call(
        body, grid=(na // ta,), name=name,
        in_specs=[pl.BlockSpec((ta, N_STRIPS, d), lambda i: (i, 0, 0))] + [ANY] * len(after),
        out_specs=_strip_rows(ta, d),
        out_shape=jax.ShapeDtypeStruct((4, 4, na, d), a.dtype), compiler_params=_cparams("parallel"),
    )(a.reshape(na, N_STRIPS, d), *after).reshape(s, d)


def _local_step(x, tgt, fetch, ff, sinks, g_mix, g_group, g_mlp, g_final, emit):
    s, d = x.shape
    depth = g_mix.shape[0]
    tb = min(512, s)
    tf = ff // N_CHIPS
    ts = min(1024, s)
    saved = []
    for l in range(depth):
        w_in, _, _, _, conv_w = fetch(0, l, x)
        cw = _tile_rows(conv_w[l])
        sk = jnp.repeat(sinks[l].reshape(N_HEADS), HEAD_DIM)[None]
        h, za, zb, zc = _qkv_fwd(x, g_mix[l][None], w_in, l, tb)
        parts_a = [_attn_fwd(za, dil, A_WIDTH, 1, 2, 1, A_MAX_DIST, "attn_a_fwd_%d" % dil) for dil in DILATIONS]
        part_c = _attn_fwd(zc, 1, C_KV_WIDTH, 3, 4, C_GROUP, C_MAX_DIST, "attn_c_fwd")
        ya, lse_a, yc, lse_c = _attn_merge(parts_a, part_c, sk, ts)
        w_in, w_o, w1, w2, _ = fetch(1, l, yc)
        x1, yb = _mix_fwd(x, ya, yc, zb, cw, g_group[l][None], w_o, l, ts)
        x2, h2, ap = _mlp_fwd(x1, g_mlp[l][None], w1, w2, l, ts, tf)
        saved.append((x, h, za, zb, zc, ya, lse_a, yc, lse_c, yb, x1, h2, ap, cw, sk))
        x = x2
    dx, loss_tile, dg_final = _loss_head(x, g_final[None], tgt, ts)
    grads = [None] * depth
    tok = jnp.zeros((), F32)
    for l in reversed(range(depth)):
        x0, h, za, zb, zc, ya, lse_a, yc, lse_c, yb, x1, h2, ap, cw, sk = saved[l]
        dx1, dap, dg_mlp = _mlp_bwd(dx, x1, ap, g_mlp[l][None] + tok, w1, w2, l, ts, tf)
        tok = emit(l, 3, _wgrad(ap, dx, min(1024, ff), d, 2 * ts, "wgrad_ff_out", relu2=True))
        tok = tok + emit(l, 2, _wgrad(h2, dap, d, min(1024, ff), 2 * ts, "wgrad_ff_in"))
        n, dya, dyc, dd_a, dd_c, dyb, dg_group, dsink = _mix_bwd(dx1, ya, yb, yc, lse_c, sk, g_group[l][None] + tok,
                                                                 w_o, l, tb)
        tok = emit(l, 1, _wgrad(n, dx1, MIX_WIDTH, d, ts, "wgrad_o"))
        cw = cw + tok
        parts_a = [_attn_bwd(za, dya, lse_a, dd_a, dil, A_WIDTH, 1, 2, 1, A_MAX_DIST, "attn_a_bwd_%d" % dil)
                   for dil in DILATIONS]
        parts_c = _attn_bwd(zc, dyc, lse_c, dd_c, 1, C_KV_WIDTH, 3, 4, C_GROUP, C_MAX_DIST, "attn_c_bwd")
        dz, dcw = _dz_assemble(parts_a, parts_c, dyb, zb, cw)
        tok = emit(l, 0, _wgrad(h, dz, d, IN_WIDTH // 4, 2 * ts, "wgrad_in"))
        dx, dg_mix = _qkv_bwd(dz, dx1, x0, g_mix[l][None] + tok, w_in, l, tb, l == 0)
        grads[l] = (dcw, dsink, dg_mix, dg_group, dg_mlp)
    return loss_tile, dx, grads, dg_final


ANY = pl.BlockSpec(memory_space=pl.ANY)
SHARD_AXES = (2, 1, 2, 1)
N_BIG = len(SHARD_AXES)
N_CHIPS = 4
N_DEV = 8


def _mesh_pos():
    return lax.axis_index("x"), lax.axis_index("y"), lax.axis_index("c")


def _flip(v, bit):
    return 1 - v if bit else v


def _place_shard(shard, chip_arr, name):
    _, rows, cols = shard.shape
    tr = min(256, rows)

    def body(chip_ref, x_ref, o_ref):
        o_ref[...] = x_ref[...].astype(BF)

    return pl.pallas_call(
        body, name=name,
        grid_spec=pltpu.PrefetchScalarGridSpec(
            num_scalar_prefetch=1, grid=(2, rows // tr),
            in_specs=[pl.BlockSpec((None, tr, cols), lambda l, i, chip: (l, i, 0))],
            out_specs=pl.BlockSpec((None, None, tr, cols), lambda l, i, chip: (l, chip[0], i, 0))),
        out_shape=jax.ShapeDtypeStruct((2, N_CHIPS, rows, cols), BF),
        compiler_params=_cparams("parallel", "parallel"),
    )(chip_arr, shard)


HBM = pl.BlockSpec(memory_space=pltpu.HBM)
SEM = pl.BlockSpec(memory_space=pltpu.SEMAPHORE)
EFFECT = pltpu.SideEffectType.DATAFLOW_SIDE_EFFECTING

GATHER_GROUPS = (((0, 0),), ((1, 0), (2, 0), (3, 0)), ((0, 1),), ((1, 1), (2, 1), (3, 1)))
GATHER_STARTS = ((0,), (1,), (2, 3))


def _gather_copies(arrs, group, send_sems, recv_sems):
    x, y, c = _mesh_pos()
    me = 2 * x + y
    out = []
    for i, (w, layer) in enumerate(group):
        mine = arrs[w].at[layer, me]
        for j, (qx, qy) in enumerate([(1 - x, y), (x, 1 - y), (1 - x, 1 - y)]):
            landed = arrs[w].at[layer, 2 * qx + qy]
            out.append(tuple(pltpu.make_async_remote_copy(
                src_ref=piece, dst_ref=piece, send_sem=send_sems.at[i * 3 + j], recv_sem=recv_sems.at[i * 3 + j],
                device_id=(qx, qy, c), device_id_type=MESH) for piece in (mine, landed)))
    return out


def _conv_copies(conv_src, conv_dst, send_sems, recv_sems):
    x, y, c = _mesh_pos()
    out = []
    for j, (qx, qy) in enumerate([(1 - x, y), (x, 1 - y), (1 - x, 1 - y)]):
        out.append(tuple(pltpu.make_async_remote_copy(
            src_ref=conv_src, dst_ref=conv_dst.at[q], send_sem=send_sems.at[j], recv_sem=recv_sems.at[j],
            device_id=(qx, qy, c), device_id_type=MESH) for q in (2 * x + y, 2 * qx + qy)))
    return out


def _gather_start(groups, arrs, conv, name, through=None):
    n_sems = 2 * (len(groups) + (conv is not None))
    mats = sorted({w for g in groups for w, _ in GATHER_GROUPS[g]})

    def body(*refs):
        arrs_ref = [None] * N_BIG
        for w, ref in zip(mats, refs):
            arrs_ref[w] = ref
        sems = refs[n_in:n_in + n_sems]
        if conv is not None:
            for cp, _ in _conv_copies(refs[len(mats)], refs[len(mats) + 1], sems[-2], sems[-1]):
                cp.start()
        for k, g in enumerate(groups):
            for cp, _ in _gather_copies(arrs_ref, GATHER_GROUPS[g], sems[2 * k], sems[2 * k + 1]):
                cp.start()

    sem_shapes = []
    for n in [len(GATHER_GROUPS[g]) for g in groups] + ([1] if conv is not None else []):
        sem_shapes += [pltpu.SemaphoreType.DMA((3 * n,))] * 2
    operands = [arrs[w] for w in mats] + ([] if conv is None else list(conv)) + ([] if through is None else [through])
    n_in = len(operands)
    res = pl.pallas_call(
        body, name=name,
        out_shape=tuple(sem_shapes) + tuple(pltpu.HBM(a.shape, a.dtype) for a in operands),
        in_specs=(HBM,) * n_in, out_specs=(SEM,) * n_sems + (HBM,) * n_in,
        input_output_aliases={i: n_sems + i for i in range(n_in)},
        compiler_params=pltpu.CompilerParams(has_side_effects=EFFECT),
    )(*[pltpu.with_memory_space_constraint(a, pltpu.HBM) for a in operands])
    arrs = list(arrs)
    for w, a in zip(mats, res[n_sems:]):
        arrs[w] = a
    return res[:n_sems], arrs, list(res[n_sems + len(mats):])


def _gather_wait(k, sems, arrs, conv, after, name):
    group = GATHER_GROUPS[k]
    mats = sorted({w for w, _ in group})
    n_conv = 0 if conv is None else 2

    def body(*refs):
        local = refs[:len(mats)]
        arrs_ref = [None] * N_BIG
        for w, ref in zip(mats, local):
            arrs_ref[w] = ref
        pos = len(mats) + n_conv
        copies = _gather_copies(arrs_ref, group, refs[pos], refs[pos + 1])
        if conv is not None:
            copies += _conv_copies(refs[len(mats)], refs[len(mats) + 1], refs[pos + 2], refs[pos + 3])
        for send, recv in copies:
            recv.wait_recv()
            send.wait_send()

    operands = [arrs[w] for w in mats] + ([] if conv is None else [conv[1], conv[2]])
    sem_ops = list(sems) + ([] if conv is None else list(conv[0]))
    n_op = len(operands)
    res = pl.pallas_call(
        body, name=name, out_shape=tuple(pltpu.HBM(a.shape, a.dtype) for a in operands),
        in_specs=(HBM,) * n_op + (SEM,) * len(sem_ops) + (ANY,) * len(after), out_specs=(HBM,) * n_op,
        input_output_aliases={i: i for i in range(n_op)},
        compiler_params=pltpu.CompilerParams(has_side_effects=EFFECT),
    )(*operands, *sem_ops, *after)
    arrs = list(arrs)
    for w, a in zip(mats, res):
        arrs[w] = a
    return arrs, (res[-1] if conv is not None else None)


def _grad_shard(ref, w, chip, n):
    start = pl.multiple_of(chip * n, 128)
    if SHARD_AXES[w] == 2:
        return ref.at[:, pl.ds(start, n)]
    return ref.at[pl.ds(start, n), :]


def _slot_shape(g, w):
    shape = list(g.shape)
    shape[SHARD_AXES[w] - 1] //= N_CHIPS
    return (N_DEV - 1,) + tuple(shape)


def _scatter_copies(g_ref, land_ref, send_sems, recv_sems, layer, w):
    x, y, c = _mesh_pos()
    n = g_ref.shape[SHARD_AXES[w] - 1] // N_CHIPS
    out = []
    for r in range(1, N_DEV):
        tx, ty, tc = _flip(x, r & 4), _flip(y, r & 2), _flip(c, r & 1)
        cp = pltpu.make_async_remote_copy(
            src_ref=_grad_shard(g_ref, w, 2 * tx + ty, n), dst_ref=land_ref.at[r - 1], send_sem=send_sems.at[r - 1],
            recv_sem=recv_sems.at[r - 1], device_id=(tx, ty, tc), device_id_type=MESH)
        out.append((cp, (c != layer) if r & 1 else (c == layer)))
    return out


def _scatter_start(items, layer, name):
    n = len(items)

    def body(*refs):
        for i, (w, _, _) in enumerate(items):
            g_ref, land_ref = refs[2 * i], refs[2 * i + 1]
            send_sems, recv_sems = refs[2 * n + 2 * i], refs[2 * n + 2 * i + 1]
            for cp, mine in _scatter_copies(g_ref, land_ref, send_sems, recv_sems, layer, w):
                @pl.when(mine)
                def _():
                    cp.start()
        refs[-1][...] = jnp.zeros_like(refs[-1])

    operands = [a for _, g, land in items for a in (g, land)]
    res = pl.pallas_call(
        body, name=name,
        out_shape=(pltpu.SemaphoreType.DMA((N_DEV - 1,)),) * (2 * n)
        + tuple(pltpu.HBM(a.shape, a.dtype) for a in operands) + (jax.ShapeDtypeStruct((HALO, 128), F32),),
        in_specs=(HBM,) * (2 * n),
        out_specs=(SEM,) * (2 * n) + (HBM,) * (2 * n) + (pl.BlockSpec(memory_space=pltpu.VMEM),),
        input_output_aliases={i: 2 * n + i for i in range(2 * n)},
        compiler_params=pltpu.CompilerParams(has_side_effects=EFFECT),
    )(*[pltpu.with_memory_space_constraint(a, pltpu.HBM) for a in operands])
    return [(res[2 * i], res[2 * i + 1], res[2 * n + 2 * i], res[2 * n + 2 * i + 1]) for i in range(n)], res[-1]


def _scatter_wait(started, land, after, w, name):
    def body(g0_ref, g1_ref, land_ref, ss0, rs0, ss1, rs1, after_ref, g0_out, g1_out, land_out):
        c = lax.axis_index("c")
        for layer, g_ref, ss, rs in ((0, g0_ref, ss0, rs0), (1, g1_ref, ss1, rs1)):
            for cp, mine in _scatter_copies(g_ref, land_ref, ss, rs, layer, w):
                @pl.when(mine)
                def _():
                    cp.wait_send()

                @pl.when(c == layer)
                def _():
                    cp.wait_recv()

    (ss0, rs0, g0), (ss1, rs1, g1) = started
    return pl.pallas_call(
        body, name=name,
        out_shape=(pltpu.HBM(g0.shape, g0.dtype), pltpu.HBM(g1.shape, g1.dtype), pltpu.HBM(land.shape, land.dtype)),
        in_specs=(HBM, HBM, HBM, SEM, SEM, SEM, SEM, ANY), out_specs=(HBM, HBM, HBM),
        input_output_aliases={0: 0, 1: 1, 2: 2}, compiler_params=pltpu.CompilerParams(has_side_effects=EFFECT),
    )(g0, g1, land, ss0, rs0, ss1, rs1, after)


def _sum_slots(g0, g1, slots, w, pos_arr, name):
    _, rows, cols = slots.shape
    tr = min(512, rows)
    nr = rows // tr
    if SHARD_AXES[w] == 2:
        own = pl.BlockSpec((tr, cols), lambda i, pos: (i, pos[0]))
    else:
        own = pl.BlockSpec((tr, cols), lambda i, pos: (pos[0] * nr + i, 0))

    def body(pos_ref, own0_ref, own1_ref, s_ref, o_ref):
        acc = jnp.where(pos_ref[1] == 0, own0_ref[...], own1_ref[...]).astype(F32)
        for r in range(N_DEV - 1):
            acc = acc + s_ref[r].astype(F32)
        o_ref[...] = acc

    return pl.pallas_call(
        body, name=name,
        grid_spec=pltpu.PrefetchScalarGridSpec(
            num_scalar_prefetch=1, grid=(nr,),
            in_specs=[own, own, pl.BlockSpec((N_DEV - 1, tr, cols), lambda i, pos: (0, i, 0))],
            out_specs=pl.BlockSpec((tr, cols), lambda i, pos: (i, 0))),
        out_shape=jax.ShapeDtypeStruct((rows, cols), F32), compiler_params=_cparams("parallel"),
    )(pos_arr, g0, g1, slots)


def _swap_copies(refs, n):
    x, y, c = _mesh_pos()
    return [pltpu.make_async_remote_copy(src_ref=refs[w], dst_ref=refs[n + w], send_sem=refs[2 * n].at[w],
                                         recv_sem=refs[2 * n + 1].at[w], device_id=(x, y, 1 - c), device_id_type=MESH)
            for w in range(n)]


def _swap_start(halves, name):
    n = len(halves)

    def body(*refs):
        for cp in _swap_copies(refs, n):
            cp.start()

    operands = list(halves) + [lax.empty(h.shape, h.dtype) for h in halves]
    res = pl.pallas_call(
        body, name=name,
        out_shape=(pltpu.SemaphoreType.DMA((n,)),) * 2 + tuple(pltpu.HBM(a.shape, a.dtype) for a in operands),
        in_specs=(HBM,) * (2 * n), out_specs=(SEM,) * 2 + (HBM,) * (2 * n),
        input_output_aliases={i: 2 + i for i in range(2 * n)},
        compiler_params=pltpu.CompilerParams(has_side_effects=EFFECT),
    )(*[pltpu.with_memory_space_constraint(a, pltpu.HBM) for a in operands])
    return res[0], res[1], list(res[2:2 + n]), list(res[2 + n:])


def _swap_wait(send_sems, recv_sems, halves, lands, after, name):
    n = len(halves)

    def body(*refs):
        for cp in _swap_copies(refs, n):
            cp.wait_send()
            cp.wait_recv()

    operands = list(halves) + list(lands)
    res = pl.pallas_call(
        body, name=name, out_shape=tuple(pltpu.HBM(a.shape, a.dtype) for a in operands),
        in_specs=(HBM,) * (2 * n) + (SEM, SEM, ANY), out_specs=(HBM,) * (2 * n),
        input_output_aliases={i: i for i in range(2 * n)},
        compiler_params=pltpu.CompilerParams(has_side_effects=EFFECT),
    )(*operands, send_sems, recv_sems, after)
    return list(res[n:])


def _adamw_math(w, g, m, v):
    m = ADAM_B1 * m + (1.0 - ADAM_B1) * g
    v = ADAM_B2 * v + (1.0 - ADAM_B2) * jnp.square(g)
    m_hat = m / (1.0 - ADAM_B1 ** ADAM_STEP)
    v_hat = v / (1.0 - ADAM_B2 ** ADAM_STEP)
    delta = -ADAM_LR * (m_hat / (jnp.sqrt(v_hat) + ADAM_EPS) + ADAM_WD * w)
    return delta, m, v


def _adamw(w, g, m, v, filled, pos_arr, name):
    shape = w.shape
    _, rows, cols = shape
    tr = min(256, rows)

    def body(pos_ref, w_ref, g_ref, m_ref, v_ref, *rest):
        go_ref, d_ref, m2_ref, v2_ref = rest[-4:]
        g = g_ref[...]
        go_ref[...] = g
        d_ref[...], m2_ref[...], v2_ref[...] = _adamw_math(w_ref[...], g, m_ref[...], v_ref[...])

    def layer(pos):
        return pos[1] if filled is None else 1 - pos[1]

    full = pl.BlockSpec((None, tr, cols), lambda i, pos: (layer(pos), i, 0))
    half = pl.BlockSpec((tr, cols), lambda i, pos: (i, 0))
    n_in = 5
    return pl.pallas_call(
        body, name=name,
        grid_spec=pltpu.PrefetchScalarGridSpec(
            num_scalar_prefetch=1, grid=(rows // tr,),
            in_specs=[full, half, full, full] + ([] if filled is None else [ANY] * 4), out_specs=[full] * 4),
        out_shape=[jax.ShapeDtypeStruct(shape, F32)] * 4,
        input_output_aliases={} if filled is None else {n_in + k: k for k in range(4)},
        compiler_params=_cparams("parallel"),
    )(pos_arr, w, g, m, v, *([] if filled is None else filled))


def _small_sync(part, w, m, v):
    rows, cols = part.shape

    def body(p_ref, w_ref, m_ref, v_ref, g_ref, d_ref, m2_ref, v2_ref, slots, send_sems, recv_sems):
        x, y, c = _mesh_pos()
        me = 4 * x + 2 * y + c
        slots[me] = p_ref[...]
        sends = []
        for r in range(1, N_DEV):
            to = (_flip(x, r & 4), _flip(y, r & 2), _flip(c, r & 1))
            sends.append(pltpu.make_async_remote_copy(
                src_ref=p_ref, dst_ref=slots.at[me], send_sem=send_sems.at[r - 1], recv_sem=recv_sems.at[r - 1],
                device_id=to, device_id_type=MESH))
        for cp in sends:
            cp.start()
        for cp in sends:
            cp.wait_recv()
        for cp in sends:
            cp.wait_send()
        g = slots[0]
        for i in range(1, N_DEV):
            g = g + slots[i]
        g_ref[...] = g
        d_ref[...], m2_ref[...], v2_ref[...] = _adamw_math(w_ref[...], g, m_ref[...], v_ref[...])

    vm = pl.BlockSpec(memory_space=pltpu.VMEM)
    return pl.pallas_call(
        body, name="small_sync", in_specs=[vm] * 4, out_specs=[vm] * 4,
        out_shape=[jax.ShapeDtypeStruct((rows, cols), F32)] * 4,
        scratch_shapes=[pltpu.VMEM((N_DEV, rows, cols), F32), pltpu.SemaphoreType.DMA((N_DEV - 1,)),
                        pltpu.SemaphoreType.DMA((N_DEV - 1,))],
    )(part, w, m, v)


PACK_W = 256


def _pack_rows(n):
    return -(-n // (HALO * PACK_W)) * HALO


def _pack_small(parts):
    out = []
    for a in parts:
        flat = a.reshape(-1)
        out.append(jnp.pad(flat, (0, _pack_rows(flat.size) * PACK_W - flat.size)).reshape(-1, PACK_W))
    return jnp.concatenate(out, axis=0)


def _unpack_small(p, shapes):
    out, row = [], 0
    for shape in shapes:
        n = 1
        for k in shape:
            n *= k
        out.append(p[row:row + _pack_rows(n)].reshape(-1)[:n].reshape(shape))
        row += _pack_rows(n)
    return out


def kernel(x, w_in, conv_w, sinks, g_mix, g_group, w_o, g_mlp, w_ff_in, w_ff_out, g_final, loss_target, m_w_in, m_conv_w, m_sinks, m_g_mix, m_g_group, m_w_o, m_g_mlp, m_w_ff_in, m_w_ff_out, m_g_final, v_w_in, v_conv_w, v_sinks, v_g_mix, v_g_group, v_w_o, v_g_mlp, v_w_ff_in, v_w_ff_out, v_g_final):
    chip = 2 * lax.axis_index("x") + lax.axis_index("y")
    conv_n = conv_w.shape[2]

    pos_arr = jnp.stack([chip, lax.axis_index("c")]).astype(jnp.int32)
    shards = (w_in, w_o, w_ff_in, w_ff_out)
    conv_tile = jnp.pad(conv_w.reshape(6, conv_n), ((0, HALO - 6), (0, 128 - conv_n)))
    placed = [_place_shard(w_in, pos_arr[:1], "place_shard_0"), None, None, None]
    sems_a, placed, conv_thru = _gather_start(
        GATHER_STARTS[0], placed, (conv_tile, lax.empty((N_CHIPS,) + conv_tile.shape, conv_tile.dtype)),
        "gather_start_0")
    for i in range(1, N_BIG):
        placed[i] = _place_shard(shards[i], pos_arr[:1], "place_shard_%d" % i)
    full = {"arrs": placed, "conv": None, "sems": list(sems_a[:2])}
    target = _to_strips(loss_target[0], placed[:1], "to_strips_target")

    def fetch(stage, layer, after):
        k = 2 * layer + stage
        sems = full["sems"][2 * k:2 * k + 2]
        if k == 0:
            full["arrs"], land = _gather_wait(0, sems, full["arrs"], (sems_a[-2:], *conv_thru), (after, target),
                                              "gather_wait_0")
            conv_all = lax.dynamic_update_slice(land, conv_tile[None], (chip, 0, 0))
            full["conv"] = conv_all[:, :6, :conv_n].reshape(N_CHIPS, 2, 3, conv_n).transpose(1, 2, 0, 3).reshape(
                2, 3, CONV_CH)
            sems_b, full["arrs"], rest = _gather_start(GATHER_STARTS[1], full["arrs"], None, "gather_start_1",
                                                       through=full["arrs"][0])
            full["arrs"][0] = rest[-1]
            full["sems"] += list(sems_b)
        else:
            full["arrs"], _ = _gather_wait(k, sems, full["arrs"], None, (after,), "gather_wait_%d" % k)
        if k == 1:
            sems_c, full["arrs"], _ = _gather_start(GATHER_STARTS[2], full["arrs"], None, "gather_start_2")
            full["sems"] += list(sems_c)
        return (*full["arrs"], full["conv"])

    lands, started, pending = [None] * N_BIG, {}, []

    def emit(layer, w, g):
        if lands[w] is None:
            lands[w] = lax.empty(_slot_shape(g, w), g.dtype)
        pending.append((w, g, lands[w]))
        if not (w == 0 or (layer == 0 and w == 1)):
            return jnp.zeros((), F32)
        name = "scatter_start_%d_%d" % (layer, len(pending))
        done, token = _scatter_start(list(pending), layer, name)
        for (w_i, _, _), (ss, rs, g_thru, land) in zip(pending, done):
            started[layer, w_i], lands[w_i] = (ss, rs, g_thru), land
        pending.clear()
        return token[0, 0]

    loss_tile, dx, grads, dg_final = _local_step(_to_strips(x[0], placed, "to_strips_x"), target, fetch,
                                                 w_ff_in.shape[2] * N_CHIPS,
                                                 sinks, g_mix, g_group, g_mlp, g_final, emit)

    wmv = ((w_in, m_w_in, v_w_in), (w_o, m_w_o, v_w_o), (w_ff_in, m_w_ff_in, v_w_ff_in),
           (w_ff_out, m_w_ff_out, v_w_ff_out))
    big, after = [None] * N_BIG, dx
    for name, ws in (("swap_rest", (1, 2, 3)), ("swap_in", (0,))):
        own = []
        for w in ws:
            g0, g1, slots = _scatter_wait((started[0, w], started[1, w]), lands[w], after, w, "scatter_wait_%d" % w)
            own.append(_sum_slots(g0, g1, slots, w, pos_arr, "sum_slots_%d" % w))
        send_sems, recv_sems, own, zones = _swap_start(own, name + "_start")
        for w, g in zip(ws, own):
            big[w] = _adamw(wmv[w][0], g, wmv[w][1], wmv[w][2], None, pos_arr, "adamw_own_%d" % w)
        theirs = _swap_wait(send_sems, recv_sems, own, zones, big[ws[-1]][1], name + "_wait")
        for w, g in zip(ws, theirs):
            big[w] = _adamw(wmv[w][0], g, wmv[w][1], wmv[w][2], big[w], pos_arr, "adamw_other_%d" % w)
        after = big[ws[-1]][1]

    def both(i):
        return jnp.stack([grads[0][i][0], grads[1][i][0]])
    dconv = jnp.stack([grads[0][0][:3], grads[1][0][:3]])
    dsinks = jnp.stack([grads[0][1][0, ::HEAD_DIM], grads[1][1][0, ::HEAD_DIM]])
    part = _pack_small([both(2), both(3), both(4), dg_final[0], dconv, dsinks, loss_tile[0, 0]])

    def spread(shard):
        return lax.dynamic_update_slice(jnp.zeros((2, 3, CONV_CH), F32), shard, (0, 0, chip * conv_n))
    zero = jnp.zeros((), F32)
    packs = [_pack_small([a, b, c_, e, spread(f), g_, zero]) for a, b, c_, e, f, g_ in (
        (g_mix, g_group, g_mlp, g_final, conv_w, sinks),
        (m_g_mix, m_g_group, m_g_mlp, m_g_final, m_conv_w, m_sinks),
        (v_g_mix, v_g_group, v_g_mlp, v_g_final, v_conv_w, v_sinks))]
    shapes = [g_mix.shape, g_group.shape, g_mlp.shape, g_final.shape, (2, 3, CONV_CH), sinks.shape, ()]
    small = [_unpack_small(p, shapes) for p in _small_sync(part, *packs)]

    def shard_of(full):
        return lax.dynamic_slice(full, (0, 0, chip * conv_n), (2, 3, conv_n))
    small = [(s[0], s[1], s[2], s[3], shard_of(s[4]), s[5], s[6]) for s in small]
    loss = small[0][6]

    def ordered(kind):
        b = [big[i][kind] for i in range(N_BIG)]
        s = small[kind]
        return [b[0], s[4], s[5], s[0], s[1], b[1], s[2], b[2], b[3], s[3]]

    return (loss, dx[None], *ordered(0), *ordered(1), *ordered(2), *ordered(3))
```

```python
import functools

import jax
import jax.numpy as jnp
from jax import lax
from jax.experimental import pallas as pl
from jax.experimental.pallas import tpu as pltpu

HEAD_DIM = 64
N_HEADS = 6
C_GROUP = 3
A_WIDTH = N_HEADS * HEAD_DIM
C_KV_WIDTH = 2 * HEAD_DIM
CONV_CH = 256
ZA_W = 3 * A_WIDTH
ZB_W = 3 * CONV_CH
ZC_W = A_WIDTH + 2 * C_KV_WIDTH
IN_WIDTH = ZA_W + ZB_W + ZC_W
MIX_WIDTH = A_WIDTH + CONV_CH + A_WIDTH
DILATIONS = (1, 4, 16)
A_MAX_DIST = 128
C_MAX_DIST = 127
TQ = 128
EPS = 1e-6
SCALE = HEAD_DIM ** -0.5
NEG = -1e30
HALO = 8

ADAM_LR = 0.001
ADAM_B1 = 0.9
ADAM_B2 = 0.999
ADAM_EPS = 1e-08
ADAM_WD = 0.01
ADAM_STEP = 10

BF = jnp.bfloat16
F32 = jnp.float32
MESH = pl.DeviceIdType.MESH
VMEM_LIMIT = 56 * 1024 * 1024


def _cparams(*sem):
    return pltpu.CompilerParams(dimension_semantics=sem, vmem_limit_bytes=VMEM_LIMIT)


def _nt(a, b):
    return lax.dot_general(a, b, (((1,), (1,)), ((), ())), preferred_element_type=F32)


def _tn(a, b):
    return lax.dot_general(a, b, (((0,), (0,)), ((), ())), preferred_element_type=F32)


def _nn(a, b):
    return jnp.dot(a, b, preferred_element_type=F32)


def _rows(tb, w):
    return pl.BlockSpec((tb, w), lambda i: (i, 0))


def _whole(shape):
    return pl.BlockSpec(shape, lambda *_: (0,) * len(shape))


def _layer(shape, l):
    return pl.BlockSpec((None,) + shape, lambda *_: (l,) + (0,) * len(shape))


def _rms_scale(v):
    return lax.rsqrt(jnp.mean(v * v, axis=-1, keepdims=True) + EPS)


def _norm_bwd(dxhat, xhat, r):
    return r * (dxhat - xhat * jnp.mean(dxhat * xhat, axis=-1, keepdims=True))


def _qkv_fwd(x, g, w_all, l, tb):
    s, d = x.shape

    def body(x_ref, g_ref, w_ref, h_ref, za_ref, zb_ref, zc_ref):
        xv = x_ref[...]
        h = ((xv * _rms_scale(xv)) * g_ref[...]).astype(BF)
        h_ref[...] = h
        z = jnp.concatenate([_nn(h, w_ref[k]) for k in range(N_CHIPS)], axis=1)
        za_ref[...] = z[:, :ZA_W]
        zb_ref[...] = z[:, ZA_W:ZA_W + ZB_W]
        zc_ref[...] = z[:, ZA_W + ZB_W:]

    return pl.pallas_call(
        body, grid=(s // tb,), name="qkv_fwd",
        in_specs=[_rows(tb, d), _whole((1, d)), _layer((N_CHIPS, d, IN_WIDTH // N_CHIPS), l)],
        out_specs=[_rows(tb, d), _rows(tb, ZA_W), _rows(tb, ZB_W), _rows(tb, ZC_W)],
        out_shape=[jax.ShapeDtypeStruct((s, d), BF), jax.ShapeDtypeStruct((s, ZA_W), F32),
                   jax.ShapeDtypeStruct((s, ZB_W), F32), jax.ShapeDtypeStruct((s, ZC_W), F32)],
        compiler_params=_cparams("parallel"),
    )(x, g, w_all)


N_STRIPS = 16


def _strips(a):
    s, w = a.shape
    return a.reshape(4, 4, s // N_STRIPS, w)


P_ROWS = {16: TQ, 4: 32, 1: 8}


def _p_sub(s, dil, most):
    while (s // dil // TQ) % most:
        most //= 2
    return most


def _p_grid(s, dil, n_sub):
    nb = s // dil // TQ // n_sub
    return {16: (4, 4, nb), 4: (4, nb), 1: (nb,)}[dil]


def _p_spec(dil, cw, col, n_sub, prev=False):
    rows = P_ROWS[dil] * (1 if prev else n_sub)

    def blk(j):
        return jnp.maximum(n_sub * j - 1, 0) if prev else j
    if dil == 16:
        return pl.BlockSpec((None, None, rows, cw), lambda f, e, j: (f, e, blk(j), col))
    if dil == 4:
        return pl.BlockSpec((None, 4, rows, cw), lambda f, j: (f, 0, blk(j), col))
    return pl.BlockSpec((4, 4, rows, cw), lambda j: (0, 0, blk(j), col))


def _block_pos(i, dil):
    if dil == 16:
        return i
    if dil == 4:
        return 4 * (i % 32) + i // 32
    return 16 * (i % 8) + 4 * ((i // 8) % 4) + i // 32


def _band_mask(b, dil, max_dist):
    qi = _block_pos(lax.broadcasted_iota(jnp.int32, (TQ, 2 * TQ), 0), dil)
    col = lax.broadcasted_iota(jnp.int32, (TQ, 2 * TQ), 1)
    cur = col >= TQ
    dist = qi - _block_pos(col % TQ, dil) + jnp.where(cur, 0, TQ)
    return (dist >= 0) & (dist <= max_dist) & (cur | (b > 0))


def _hs(h):
    return slice(h * HEAD_DIM, (h + 1) * HEAD_DIM)


def _ld(ref, cols, rows=slice(None)):
    v = ref[..., rows, cols]
    return v.reshape(TQ, v.shape[-1])


def _st(ref, cols, val, rows=slice(None)):
    lead = ref.shape[:-2] + (ref.shape[-2] if rows == slice(None) else rows.stop - rows.start,)
    ref[..., rows, cols] = val.reshape(lead + (val.shape[-1],))


def _attn_fwd(z, dil, kw, kcol, vcol, n_rep, max_dist, name):
    s, zw = z.shape
    n_sub = _p_sub(s, dil, 2)
    grid = _p_grid(s, dil, n_sub)

    def body(q_ref, kp_ref, kc_ref, vp_ref, vc_ref, o_ref, lse_ref):
        for t in range(n_sub):
            rows = slice(t * P_ROWS[dil], (t + 1) * P_ROWS[dil])
            before = (slice(None),) if t == 0 else (slice((t - 1) * P_ROWS[dil], t * P_ROWS[dil]),)
            kb_ref, vb_ref = (kp_ref, vp_ref) if t == 0 else (kc_ref, vc_ref)
            mask = _band_mask(n_sub * pl.program_id(len(grid) - 1) if t == 0 else 1, dil, max_dist)
            scs, v2s = [], []
            for kh in range(N_HEADS // n_rep):
                k2 = jnp.concatenate([_ld(kb_ref, _hs(kh), *before), _ld(kc_ref, _hs(kh), rows)], axis=0).astype(BF)
                v2s.append(jnp.concatenate([_ld(vb_ref, _hs(kh), *before), _ld(vc_ref, _hs(kh), rows)],
                                           axis=0).astype(BF))
                for h in range(kh * n_rep, (kh + 1) * n_rep):
                    q = (_ld(q_ref, _hs(h), rows) * SCALE).astype(BF)
                    scs.append(jnp.where(mask, _nt(q, k2), NEG))
            for h, sc in enumerate(scs):
                m = jnp.max(sc, axis=1, keepdims=True)
                p = jnp.exp(sc - m)
                l = jnp.sum(p, axis=1, keepdims=True)
                _st(o_ref, _hs(h), _nn(p.astype(BF), v2s[h // n_rep]) / l, rows)
                _st(lse_ref, _hs(h), jnp.broadcast_to(m + jnp.log(l), (TQ, HEAD_DIM)), rows)

    res = pl.pallas_call(
        body, grid=grid, name=name,
        in_specs=[_p_spec(dil, A_WIDTH, 0, n_sub), _p_spec(dil, kw, kcol, n_sub, True), _p_spec(dil, kw, kcol, n_sub),
                  _p_spec(dil, kw, vcol, n_sub, True), _p_spec(dil, kw, vcol, n_sub)],
        out_specs=[_p_spec(dil, A_WIDTH, 0, n_sub)] * 2,
        out_shape=[jax.ShapeDtypeStruct((4, 4, s // N_STRIPS, A_WIDTH), F32)] * 2,
        compiler_params=_cparams(*(("parallel",) * len(grid))),
    )(*[_strips(z)] * 5)
    return [a.reshape(s, A_WIDTH) for a in res]


def _attn_merge(parts_a, part_c, sink_row, tb):
    s = part_c[0].shape[0]
    n_a = len(parts_a)

    def body(*refs):
        ins, sink_ref = refs[:2 * n_a + 2], refs[2 * n_a + 2]
        ya_ref, lsea_ref, yc_ref, lsec_ref = refs[2 * n_a + 3:]
        lses = [ins[2 * p + 1][...] for p in range(n_a)]
        m = functools.reduce(jnp.maximum, lses)
        ws = [jnp.exp(v - m) for v in lses]
        l = functools.reduce(jnp.add, ws)
        ya_ref[...] = functools.reduce(jnp.add, [w * ins[2 * p][...] for p, w in enumerate(ws)]) / l
        lsea_ref[...] = m + jnp.log(l)
        o_c, lse_c = [r[...] for r in ins[2 * n_a:]]
        sk = sink_ref[...]
        m2 = jnp.maximum(lse_c, sk)
        w = jnp.exp(lse_c - m2)
        l2 = w + jnp.exp(sk - m2)
        yc_ref[...] = o_c * (w / l2)
        lsec_ref[...] = m2 + jnp.log(l2)

    return pl.pallas_call(
        body, grid=(s // tb,), name="attn_merge",
        in_specs=[_rows(tb, A_WIDTH)] * (2 * n_a + 2) + [_whole((1, A_WIDTH))],
        out_specs=[_rows(tb, A_WIDTH)] * 4, out_shape=[jax.ShapeDtypeStruct((s, A_WIDTH), F32)] * 4,
        compiler_params=_cparams("parallel"),
    )(*[a for part in parts_a + [part_c] for a in part], sink_row)


def _shift_down(v, n, halo):
    rows = v.shape[0]
    out = pltpu.roll(v, n, 0)
    row = lax.broadcasted_iota(jnp.int32, v.shape, 0)
    for t in range(n):
        out = jnp.where(row == t, halo[HALO - n + t:HALO - n + t + 1, :], out)
    return out


def _shift_up(v, n, halo):
    rows = v.shape[0]
    out = pltpu.roll(v, rows - n, 0)
    row = lax.broadcasted_iota(jnp.int32, v.shape, 0)
    for t in range(n):
        out = jnp.where(row == rows - n + t, halo[t:t + 1, :], out)
    return out


def _strip(v, b):
    return v[b % 4, b // 4]


def _conv_strips(zb, prev, cw):
    gb = [_strip(zb, b)[:, :CONV_CH] for b in range(N_STRIPS)]
    gc = [_strip(zb, b)[:, CONV_CH:2 * CONV_CH] for b in range(N_STRIPS)]
    xb = [_strip(zb, b)[:, 2 * CONV_CH:] for b in range(N_STRIPS)]
    u = [g * v for g, v in zip(gc, xb)]
    uh = prev[:, :, CONV_CH:2 * CONV_CH] * prev[:, :, 2 * CONV_CH:]
    wrapped = {14: _shift_down(u[14], 1, uh[2]), 15: _shift_down(u[15], 1, uh[3])}
    u1 = [u[b - 1] if b >= 1 else wrapped[15] for b in range(N_STRIPS)]
    u2 = [u[b - 2] if b >= 2 else wrapped[14 + b] for b in range(N_STRIPS)]
    c = [cw[0:1, :] * u2[b] + cw[1:2, :] * u1[b] + cw[2:3, :] * u[b] for b in range(N_STRIPS)]
    return gb, gc, xb, u, u1, u2, c


def _strip_rows(ta, w):
    return pl.BlockSpec((4, 4, ta, w), lambda i: (0, 0, i, 0))


def _prev_rows(ta, w):
    return pl.BlockSpec((4, None, HALO, w), lambda i: (0, 3, jnp.maximum(i * (ta // HALO) - 1, 0), 0))


def _next_rows(ta, w, nblk):
    return pl.BlockSpec((4, None, HALO, w),
                        lambda i: (0, 0, jnp.minimum((i + 1) * (ta // HALO), nblk * (ta // HALO) - 1), 0))


def _mix_fwd(x, ya, yc, zb, cw, gg, wo_all, l, tb):
    s, d = x.shape
    ta = tb // N_STRIPS

    def body(x_ref, ya_ref, yc_ref, zb_ref, zbp_ref, cw_ref, gg_ref, wo_ref, x1_ref, yb_ref):
        i = pl.program_id(0)
        prev = jnp.where(i > 0, zbp_ref[...], 0.0)
        gb, _, _, _, _, _, c = _conv_strips(zb_ref[...], prev, cw_ref[...])
        for b in range(N_STRIPS):
            yb_ref[b % 4, b // 4] = gb[b] * c[b]
        yb = yb_ref[...].reshape(tb, CONV_CH)
        ya, yc = ya_ref[...].reshape(tb, A_WIDTH), yc_ref[...].reshape(tb, A_WIDTH)
        n = jnp.concatenate([ya * _rms_scale(ya), yb * _rms_scale(yb), yc * _rms_scale(yc)], axis=1)
        n = (n * gg_ref[...]).astype(BF)
        x1 = x_ref[...].reshape(tb, d) + _nn(n, wo_ref[...].reshape(MIX_WIDTH, d))
        x1_ref[...] = x1.reshape(4, 4, ta, d)

    res = pl.pallas_call(
        body, grid=(s // tb,), name="mix_fwd",
        in_specs=[_strip_rows(ta, d), _strip_rows(ta, A_WIDTH), _strip_rows(ta, A_WIDTH), _strip_rows(ta, ZB_W),
                  _prev_rows(ta, ZB_W), _whole((HALO, CONV_CH)), _whole((1, MIX_WIDTH)),
                  _layer((N_CHIPS, MIX_WIDTH // N_CHIPS, d), l)],
        out_specs=[_strip_rows(ta, d), _strip_rows(ta, CONV_CH)],
        out_shape=[jax.ShapeDtypeStruct((4, 4, s // N_STRIPS, d), F32),
                   jax.ShapeDtypeStruct((4, 4, s // N_STRIPS, CONV_CH), F32)],
        compiler_params=_cparams("parallel"),
    )(_strips(x), _strips(ya), _strips(yc), _strips(zb), _strips(zb), cw, gg, wo_all)
    return res[0].reshape(s, d), res[1].reshape(s, CONV_CH)


def _mlp_fwd(x1, g, w1_all, w2_all, l, tb, tf):
    s, d = x1.shape
    ff = w1_all.shape[1] * w1_all.shape[3]
    nj = ff // tf

    def body(x_ref, g_ref, w1_ref, w2_ref, x2_ref, h2_ref, ap_ref, acc):
        j = pl.program_id(1)

        @pl.when(j == 0)
        def _():
            xv = x_ref[...]
            h2_ref[...] = ((xv * _rms_scale(xv)) * g_ref[...]).astype(BF)
            acc[...] = jnp.zeros_like(acc)

        ap = _nn(h2_ref[...], w1_ref[...])
        ap_ref[...] = ap.astype(BF)
        a = jnp.square(jnp.maximum(ap, 0.0)).astype(BF)
        acc[...] += _nn(a, w2_ref[...])

        @pl.when(j == nj - 1)
        def _():
            x2_ref[...] = x_ref[...] + acc[...]

    return pl.pallas_call(
        body, grid=(s // tb, nj), name="mlp_fwd",
        in_specs=[pl.BlockSpec((tb, d), lambda i, j: (i, 0)), _whole((1, d)),
                  pl.BlockSpec((None, None, d, tf), lambda i, j: (l, j, 0, 0)),
                  pl.BlockSpec((None, None, tf, d), lambda i, j: (l, j, 0, 0))],
        out_specs=[pl.BlockSpec((tb, d), lambda i, j: (i, 0)), pl.BlockSpec((tb, d), lambda i, j: (i, 0)),
                   pl.BlockSpec((tb, tf), lambda i, j: (i, j))],
        out_shape=[jax.ShapeDtypeStruct((s, d), F32), jax.ShapeDtypeStruct((s, d), BF),
                   jax.ShapeDtypeStruct((s, ff), BF)],
        scratch_shapes=[pltpu.VMEM((tb, d), F32)],
        compiler_params=_cparams("parallel", "arbitrary"),
    )(x1, g, w1_all, w2_all)


def _loss_head(x, g, tgt, tb):
    s, d = x.shape

    def body(x_ref, g_ref, t_ref, dx_ref, loss_ref, dg_ref):
        i = pl.program_id(0)

        @pl.when(i == 0)
        def _():
            loss_ref[...] = jnp.zeros_like(loss_ref)
            dg_ref[...] = jnp.zeros_like(dg_ref)

        xv = x_ref[...]
        r = _rms_scale(xv)
        xhat = xv * r
        err = xhat * g_ref[...] - t_ref[...]
        part = jnp.sum(jnp.mean(jnp.square(err), axis=-1, keepdims=True), axis=0, keepdims=True)
        loss_ref[...] += 0.5 * part
        dy = err * (1.0 / d)
        dg_ref[...] += jnp.sum(dy * xhat, axis=0, keepdims=True)
        dx_ref[...] = _norm_bwd(dy * g_ref[...], xhat, r)

    return pl.pallas_call(
        body, grid=(s // tb,), name="loss_head",
        in_specs=[_rows(tb, d), _whole((1, d)), _rows(tb, d)],
        out_specs=[_rows(tb, d), _whole((HALO, 128)), _whole((HALO, d))],
        out_shape=[jax.ShapeDtypeStruct((s, d), F32), jax.ShapeDtypeStruct((HALO, 128), F32),
                   jax.ShapeDtypeStruct((HALO, d), F32)],
        compiler_params=_cparams("arbitrary"),
    )(x, g, tgt)


def _mlp_bwd(dx2, x1, ap, g, w1_all, w2_all, l, tb, tf):
    s, d = x1.shape
    ff = ap.shape[1]
    nj = ff // tf

    def body(dx2_ref, x1_ref, ap_ref, g_ref, w1_ref, w2_ref, dx1_ref, dap_ref, dg_ref, acc):
        i, j = pl.program_id(0), pl.program_id(1)

        @pl.when((i == 0) & (j == 0))
        def _():
            dg_ref[...] = jnp.zeros_like(dg_ref)

        @pl.when(j == 0)
        def _():
            acc[...] = jnp.zeros_like(acc)

        da = _nt(dx2_ref[...].astype(BF), w2_ref[...])
        dap = (da * (2.0 * jnp.maximum(ap_ref[...].astype(F32), 0.0))).astype(BF)
        dap_ref[...] = dap
        acc[...] += _nt(dap, w1_ref[...])

        @pl.when(j == nj - 1)
        def _():
            xv = x1_ref[...]
            r = _rms_scale(xv)
            xhat = xv * r
            dh = acc[...]
            dg_ref[...] += jnp.sum(dh * xhat, axis=0, keepdims=True)
            dx1_ref[...] = dx2_ref[...] + _norm_bwd(dh * g_ref[...], xhat, r)

    return pl.pallas_call(
        body, grid=(s // tb, nj), name="mlp_bwd",
        in_specs=[pl.BlockSpec((tb, d), lambda i, j: (i, 0)), pl.BlockSpec((tb, d), lambda i, j: (i, 0)),
                  pl.BlockSpec((tb, tf), lambda i, j: (i, j)),
                  _whole((1, d)), pl.BlockSpec((None, None, d, tf), lambda i, j: (l, j, 0, 0)),
                  pl.BlockSpec((None, None, tf, d), lambda i, j: (l, j, 0, 0))],
        out_specs=[pl.BlockSpec((tb, d), lambda i, j: (i, 0)), pl.BlockSpec((tb, tf), lambda i, j: (i, j)),
                   _whole((HALO, d))],
        out_shape=[jax.ShapeDtypeStruct((s, d), F32), jax.ShapeDtypeStruct((s, ff), BF),
                   jax.ShapeDtypeStruct((HALO, d), F32)],
        scratch_shapes=[pltpu.VMEM((tb, d), F32)],
        compiler_params=_cparams("arbitrary", "arbitrary"),
    )(dx2, x1, ap, g, w1_all, w2_all)


def _wgrad(a, b, tm, tn, ts, name, relu2=False):
    s, m = a.shape
    n = b.shape[1]
    ns = s // ts

    def body(a_ref, b_ref, o_ref, acc):
        k = pl.program_id(2)

        @pl.when(k == 0)
        def _():
            acc[...] = jnp.zeros_like(acc)

        av = a_ref[...]
        if relu2:
            av = jnp.square(jnp.maximum(av.astype(F32), 0.0)).astype(BF)
        acc[...] += _tn(av, b_ref[...].astype(BF))

        @pl.when(k == ns - 1)
        def _():
            o_ref[...] = acc[...].astype(BF)

    return pl.pallas_call(
        body, grid=(m // tm, n // tn, ns), name=name,
        in_specs=[pl.BlockSpec((ts, tm), lambda i, j, k: (k, i)), pl.BlockSpec((ts, tn), lambda i, j, k: (k, j))],
        out_specs=pl.BlockSpec((tm, tn), lambda i, j, k: (i, j)),
        out_shape=jax.ShapeDtypeStruct((m, n), BF),
        scratch_shapes=[pltpu.VMEM((tm, tn), F32)],
        compiler_params=_cparams("parallel", "parallel", "arbitrary"),
    )(a, b)


def _mix_bwd(dx1, ya, yb, yc, lse_c, sink_row, gg, wo_all, l, tb):
    s, d = dx1.shape

    def body(dx_ref, ya_ref, yb_ref, yc_ref, lse_ref, sink_ref, gg_ref, wo_ref,
             n_ref, dya_ref, dyc_ref, da_ref, dc_ref, dyb_ref, dg_ref, dsink_ref):
        i = pl.program_id(0)

        @pl.when(i == 0)
        def _():
            dg_ref[...] = jnp.zeros_like(dg_ref)
            dsink_ref[...] = jnp.zeros_like(dsink_ref)

        dn = _nt(dx_ref[...].astype(BF), wo_ref[...].reshape(MIX_WIDTH, d))
        ys = [ya_ref[...], yb_ref[...], yc_ref[...]]
        rs = [_rms_scale(v) for v in ys]
        nhat = jnp.concatenate([v * r for v, r in zip(ys, rs)], axis=1)
        gg = gg_ref[...]
        n_ref[...] = (nhat * gg).astype(BF)
        dg_ref[...] += jnp.sum(dn * nhat, axis=0, keepdims=True)
        dnh = dn * gg
        bounds = [(0, A_WIDTH), (A_WIDTH, A_WIDTH + CONV_CH), (A_WIDTH + CONV_CH, MIX_WIDTH)]
        dys = [_norm_bwd(dnh[:, lo:hi], nhat[:, lo:hi], r) for (lo, hi), r in zip(bounds, rs)]
        dyb_ref[...] = dys[1]
        head = [lax.broadcasted_iota(jnp.int32, (A_WIDTH, A_WIDTH), k) // HEAD_DIM for k in (0, 1)]
        ones = (head[0] == head[1]).astype(BF)
        for dy, y, dy_ref, dd_ref in ((dys[0], ys[0], dya_ref, da_ref), (dys[2], ys[2], dyc_ref, dc_ref)):
            dy_ref[...] = dy
            t = dy * y
            hi = t.astype(BF)
            dd_ref[...] = _nn(hi, ones) + _nn((t - hi.astype(F32)).astype(BF), ones)
        dsink_ref[...] -= jnp.sum(jnp.exp(sink_ref[...] - lse_ref[...]) * dc_ref[...], axis=0, keepdims=True)

    return pl.pallas_call(
        body, grid=(s // tb,), name="mix_bwd",
        in_specs=[_rows(tb, d), _rows(tb, A_WIDTH), _rows(tb, CONV_CH), _rows(tb, A_WIDTH), _rows(tb, A_WIDTH),
                  _whole((1, A_WIDTH)), _whole((1, MIX_WIDTH)), _layer((N_CHIPS, MIX_WIDTH // N_CHIPS, d), l)],
        out_specs=[_rows(tb, MIX_WIDTH), _rows(tb, A_WIDTH), _rows(tb, A_WIDTH), _rows(tb, A_WIDTH),
                   _rows(tb, A_WIDTH), _rows(tb, CONV_CH), _whole((HALO, MIX_WIDTH)), _whole((HALO, A_WIDTH))],
        out_shape=[jax.ShapeDtypeStruct((s, MIX_WIDTH), BF), jax.ShapeDtypeStruct((s, A_WIDTH), F32),
                   jax.ShapeDtypeStruct((s, A_WIDTH), F32), jax.ShapeDtypeStruct((s, A_WIDTH), F32),
                   jax.ShapeDtypeStruct((s, A_WIDTH), F32), jax.ShapeDtypeStruct((s, CONV_CH), F32),
                   jax.ShapeDtypeStruct((HALO, MIX_WIDTH), F32), jax.ShapeDtypeStruct((HALO, A_WIDTH), F32)],
        compiler_params=_cparams("arbitrary"),
    )(dx1, ya, yb, yc, lse_c, sink_row, gg, wo_all)


def _attn_bwd(z, dy, lse, dd, dil, kw, kcol, vcol, n_rep, max_dist, name):
    s, zw = z.shape
    n_sub = _p_sub(s, dil, 2) if n_rep == 1 else 1
    grid = _p_grid(s, dil, n_sub)
    n_kv = N_HEADS // n_rep
    dt = F32 if dil == 1 else BF

    def body(q_ref, kp_ref, kc_ref, vp_ref, vc_ref, dy_ref, lse_ref, dd_ref, dq_ref, dkp_ref, dkc_ref, dvp_ref, dvc_ref):
        for t in range(n_sub):
            rows = slice(t * P_ROWS[dil], (t + 1) * P_ROWS[dil])
            before = (slice(None),) if t == 0 else (slice((t - 1) * P_ROWS[dil], t * P_ROWS[dil]),)
            kb_ref, vb_ref = (kp_ref, vp_ref) if t == 0 else (kc_ref, vc_ref)
            mask = _band_mask(n_sub * pl.program_id(len(grid) - 1) if t == 0 else 1, dil, max_dist)
            k2s, qs, dys, scs, dps = [], [], [], [], []
            for kh in range(n_kv):
                k2s.append(jnp.concatenate([_ld(kb_ref, _hs(kh), *before), _ld(kc_ref, _hs(kh), rows)],
                                           axis=0).astype(BF))
                v2 = jnp.concatenate([_ld(vb_ref, _hs(kh), *before), _ld(vc_ref, _hs(kh), rows)], axis=0).astype(BF)
                for h in range(kh * n_rep, (kh + 1) * n_rep):
                    qs.append((_ld(q_ref, _hs(h), rows) * SCALE).astype(BF))
                    dys.append(_ld(dy_ref, _hs(h), rows).astype(BF))
                    scs.append(jnp.where(mask, _nt(qs[h], k2s[kh]), NEG))
                    dps.append(_nt(dys[h], v2))
            for kh in range(n_kv):
                k2 = k2s[kh]
                dk2 = jnp.zeros((2 * TQ, HEAD_DIM), F32)
                dv2 = jnp.zeros((2 * TQ, HEAD_DIM), F32)
                for h in range(kh * n_rep, (kh + 1) * n_rep):
                    lse_h = _ld(lse_ref, slice(h * HEAD_DIM, h * HEAD_DIM + 1), rows)
                    dd_h = _ld(dd_ref, slice(h * HEAD_DIM, h * HEAD_DIM + 1), rows)
                    p = jnp.exp(scs[h] - lse_h)
                    ds = (p * (dps[h] - dd_h)).astype(BF)
                    _st(dq_ref, _hs(h), (_nn(ds, k2) * SCALE).astype(dt), rows)
                    dk2 = dk2 + _tn(ds, qs[h])
                    dv2 = dv2 + _tn(p.astype(BF), dys[h])
                _st(dkp_ref, _hs(kh), dk2[:TQ].astype(dt), rows)
                _st(dkc_ref, _hs(kh), dk2[TQ:].astype(dt), rows)
                _st(dvp_ref, _hs(kh), dv2[:TQ].astype(dt), rows)
                _st(dvc_ref, _hs(kh), dv2[TQ:].astype(dt), rows)

    args = [_strips(z)] * 5 + [_strips(a) for a in (dy, lse, dd)]
    pair = _p_spec(dil, A_WIDTH, 0, n_sub)
    in_specs = [pair, _p_spec(dil, kw, kcol, n_sub, True), _p_spec(dil, kw, kcol, n_sub),
                _p_spec(dil, kw, vcol, n_sub, True), _p_spec(dil, kw, vcol, n_sub)] + [pair] * 3
    out_specs = [pair] + [_p_spec(dil, kw, 0, n_sub)] * 4
    na = s // N_STRIPS
    out_shape = [jax.ShapeDtypeStruct((4, 4, na, A_WIDTH), dt)] + [jax.ShapeDtypeStruct((4, 4, na, kw), dt)] * 4
    res = pl.pallas_call(
        body, grid=grid, name=name, in_specs=in_specs, out_specs=out_specs, out_shape=out_shape,
        compiler_params=_cparams(*(("parallel",) * len(grid))),
    )(*args)
    return [res[0].reshape(s, A_WIDTH)] + [a.reshape(s, kw) for a in res[1:]]


DZ_TA = 16


def _dz_assemble(parts_a, parts_c, dyb, zb, cw):
    s = zb.shape[0]
    na = s // N_STRIPS
    nb = na // DZ_TA

    def ahead(w, k):
        return pl.BlockSpec((4, 4, DZ_TA, w), lambda i: (0, 0, jnp.minimum(i + k, nb - 1), 0))

    args, in_specs = [], []
    for dil, (dq, dkp, dkc, dvp, dvc) in zip(DILATIONS + (1,), parts_a + [parts_c]):
        w = dkp.shape[1]
        here = _strip_rows(DZ_TA, w)
        if dil == 1:
            args += [dq, dkp, dkp, dkc, dvp, dvp, dvc]
            in_specs += [_strip_rows(DZ_TA, A_WIDTH), here, ahead(w, 1), here, here, ahead(w, 1), here]
        else:
            k = 8 * dil // DZ_TA
            args += [dq, dkp, dkc, dvp, dvc]
            in_specs += [_strip_rows(DZ_TA, A_WIDTH), ahead(w, k), here, ahead(w, k), here]
    n_att = len(args)
    args = [_strips(a) for a in args] + [_strips(dyb), _strips(dyb), _strips(zb), _strips(zb), _strips(zb), cw]
    in_specs += [_strip_rows(DZ_TA, CONV_CH), _next_rows(DZ_TA, CONV_CH, nb), _strip_rows(DZ_TA, ZB_W),
                 _prev_rows(DZ_TA, ZB_W), _next_rows(DZ_TA, ZB_W, nb), _whole((HALO, CONV_CH))]

    def body(*refs):
        att = list(refs[:n_att])
        dyb_ref, dybn_ref, zb_ref, zbp_ref, zbn_ref, cw_ref, dz_ref, dcw_ref = refs[n_att:]
        i = pl.program_id(0)

        @pl.when(i == 0)
        def _():
            dcw_ref[...] = jnp.zeros_like(dcw_ref)

        def shifted(dil):
            if dil == 1:
                dq_r, kp0, kp1, dkc_r, vp0, vp1, dvc_r = [att.pop(0) for _ in range(7)]
                live = i + 1 < nb
                half = DZ_TA // 2
                dkp = jnp.concatenate([kp0[:, :, half:, :], jnp.where(live, kp1[:, :, :half, :], 0.0)], axis=2)
                dvp = jnp.concatenate([vp0[:, :, half:, :], jnp.where(live, vp1[:, :, :half, :], 0.0)], axis=2)
            else:
                dq_r, dkp_r, dkc_r, dvp_r, dvc_r = [att.pop(0) for _ in range(5)]
                live = i + 8 * dil // DZ_TA < nb
                dkp = jnp.where(live, dkp_r[...].astype(F32), 0.0)
                dvp = jnp.where(live, dvp_r[...].astype(F32), 0.0)
            return dq_r[...].astype(F32), dkc_r[...].astype(F32) + dkp, dvc_r[...].astype(F32) + dvp

        dq, dk, dv = shifted(DILATIONS[0])
        for dil in DILATIONS[1:]:
            dq2, dk2, dv2 = shifted(dil)
            dq, dk, dv = dq + dq2, dk + dk2, dv + dv2
        dz_ref[:, :, :, 0:A_WIDTH] = dq.astype(BF)
        dz_ref[:, :, :, A_WIDTH:2 * A_WIDTH] = dk.astype(BF)
        dz_ref[:, :, :, 2 * A_WIDTH:ZA_W] = dv.astype(BF)
        dq, dk, dv = shifted(1)
        c0 = ZA_W + ZB_W
        dz_ref[:, :, :, c0:c0 + A_WIDTH] = dq.astype(BF)
        dz_ref[:, :, :, c0 + A_WIDTH:c0 + A_WIDTH + C_KV_WIDTH] = dk.astype(BF)
        dz_ref[:, :, :, c0 + A_WIDTH + C_KV_WIDTH:IN_WIDTH] = dv.astype(BF)

        cw = cw_ref[...]
        prev = jnp.where(i > 0, zbp_ref[...], 0.0)
        gb, gc, xb, u, u1, u2, c = _conv_strips(zb_ref[...], prev, cw)
        dyb = dyb_ref[...]
        dc = [_strip(dyb, b) * gb[b] for b in range(N_STRIPS)]
        dcn = jnp.where(i + 1 < nb, dybn_ref[...] * zbn_ref[:, :, :CONV_CH], 0.0)
        wrapped = [_shift_up(dc[0], 1, dcn[0]), _shift_up(dc[1], 1, dcn[1])]
        upd = [jnp.zeros((1, CONV_CH), F32)] * 3
        for b in range(N_STRIPS):
            dc1 = dc[b + 1] if b + 1 < N_STRIPS else wrapped[0]
            dc2 = dc[b + 2] if b + 2 < N_STRIPS else wrapped[b + 2 - N_STRIPS]
            du = cw[2:3, :] * dc[b] + cw[1:2, :] * dc1 + cw[0:1, :] * dc2
            f, e = b % 4, b // 4
            dz_ref[f, e, :, ZA_W:ZA_W + CONV_CH] = (_strip(dyb, b) * c[b]).astype(BF)
            dz_ref[f, e, :, ZA_W + CONV_CH:ZA_W + 2 * CONV_CH] = (du * xb[b]).astype(BF)
            dz_ref[f, e, :, ZA_W + 2 * CONV_CH:c0] = (du * gc[b]).astype(BF)
            for t, uu in enumerate((u2[b], u1[b], u[b])):
                upd[t] = upd[t] + jnp.sum(dc[b] * uu, axis=0, keepdims=True)
        row = lax.broadcasted_iota(jnp.int32, (HALO, CONV_CH), 0)
        tile = jnp.zeros((HALO, CONV_CH), F32)
        for t in range(3):
            tile = jnp.where(row == t, upd[t], tile)
        dcw_ref[...] += tile

    dz, dcw = pl.pallas_call(
        body, grid=(nb,), name="dz_assemble", in_specs=in_specs,
        out_specs=[_strip_rows(DZ_TA, IN_WIDTH), _whole((HALO, CONV_CH))],
        out_shape=[jax.ShapeDtypeStruct((4, 4, na, IN_WIDTH), BF), jax.ShapeDtypeStruct((HALO, CONV_CH), F32)],
        compiler_params=_cparams("arbitrary"),
    )(*args)
    return dz.reshape(s, IN_WIDTH), dcw


def _qkv_bwd(dz, dx1, x, g, w_all, l, tb, tokens_out):
    s, d = x.shape
    na, ta = s // N_STRIPS, tb // N_STRIPS

    def body(dz_ref, dx1_ref, x_ref, g_ref, w_ref, dx_ref, dg_ref):
        i = pl.program_id(0)

        @pl.when(i == 0)
        def _():
            dg_ref[...] = jnp.zeros_like(dg_ref)

        n = IN_WIDTH // N_CHIPS
        dz = dz_ref[...].reshape(tb, IN_WIDTH)
        dh = _nt(dz[:, 0:n], w_ref[0])
        for k in range(1, N_CHIPS):
            dh = dh + _nt(dz[:, k * n:(k + 1) * n], w_ref[k])
        xv = x_ref[...].reshape(tb, d)
        r = _rms_scale(xv)
        xhat = xv * r
        dg_ref[...] += jnp.sum(dh * xhat, axis=0, keepdims=True)
        dx = (dx1_ref[...].reshape(tb, d) + _norm_bwd(dh * g_ref[...], xhat, r)).reshape(4, 4, ta, d)
        if tokens_out:
            for b in range(N_STRIPS):
                dx_ref[:, b, :] = _strip(dx, b)
        else:
            dx_ref[...] = dx

    if tokens_out:
        dx_spec, dx_shape = pl.BlockSpec((ta, N_STRIPS, d), lambda i: (i, 0, 0)), (na, N_STRIPS, d)
    else:
        dx_spec, dx_shape = _strip_rows(ta, d), (4, 4, na, d)
    dx, dg = pl.pallas_call(
        body, grid=(s // tb,), name="qkv_bwd",
        in_specs=[_strip_rows(ta, IN_WIDTH), _strip_rows(ta, d), _strip_rows(ta, d), _whole((1, d)),
                  _layer((N_CHIPS, d, IN_WIDTH // N_CHIPS), l)],
        out_specs=[dx_spec, _whole((HALO, d))],
        out_shape=[jax.ShapeDtypeStruct(dx_shape, F32), jax.ShapeDtypeStruct((HALO, d), F32)],
        compiler_params=_cparams("arbitrary"),
    )(_strips(dz), _strips(dx1), _strips(x), g, w_all)
    return dx.reshape(s, d), dg


def _tile_rows(rows):
    return jnp.pad(rows, ((0, HALO - rows.shape[0]), (0, 0)))


def _to_strips(a, after, name):
    s, d = a.shape
    na = s // N_STRIPS
    ta = min(32, na)

    def body(a_ref, *rest):
        for b in range(N_STRIPS):
            rest[-1][b % 4, b // 4] = a_ref[:, b, :]

    return pl.pallas_call(
        body, grid=(na // ta,), name=name,
        in_specs=[pl.BlockSpec((ta, N_STRIPS, d), lambda i: (i, 0, 0))] + [ANY] * len(after),
        out_specs=_strip_rows(ta, d),
        out_shape=jax.ShapeDtypeStruct((4, 4, na, d), a.dtype), compiler_params=_cparams("parallel"),
    )(a.reshape(na, N_STRIPS, d), *after).reshape(s, d)


def _local_step(x, tgt, fetch, ff, sinks, g_mix, g_group, g_mlp, g_final, emit):
    s, d = x.shape
    depth = g_mix.shape[0]
    tb = min(512, s)
    tf = ff // N_CHIPS
    ts = min(1024, s)
    saved = []
    for l in range(depth):
        w_in, _, _, _, conv_w = fetch(0, l, x)
        cw = _tile_rows(conv_w[l])
        sk = jnp.repeat(sinks[l].reshape(N_HEADS), HEAD_DIM)[None]
        h, za, zb, zc = _qkv_fwd(x, g_mix[l][None], w_in, l, tb)
        parts_a = [_attn_fwd(za, dil, A_WIDTH, 1, 2, 1, A_MAX_DIST, "attn_a_fwd_%d" % dil) for dil in DILATIONS]
        part_c = _attn_fwd(zc, 1, C_KV_WIDTH, 3, 4, C_GROUP, C_MAX_DIST, "attn_c_fwd")
        ya, lse_a, yc, lse_c = _attn_merge(parts_a, part_c, sk, ts)
        w_in, w_o, w1, w2, _ = fetch(1, l, yc)
        x1, yb = _mix_fwd(x, ya, yc, zb, cw, g_group[l][None], w_o, l, ts)
        x2, h2, ap = _mlp_fwd(x1, g_mlp[l][None], w1, w2, l, ts, tf)
        saved.append((x, h, za, zb, zc, ya, lse_a, yc, lse_c, yb, x1, h2, ap, cw, sk))
        x = x2
    dx, loss_tile, dg_final = _loss_head(x, g_final[None], tgt, ts)
    grads = [None] * depth
    tok = jnp.zeros((), F32)
    for l in reversed(range(depth)):
        x0, h, za, zb, zc, ya, lse_a, yc, lse_c, yb, x1, h2, ap, cw, sk = saved[l]
        dx1, dap, dg_mlp = _mlp_bwd(dx, x1, ap, g_mlp[l][None] + tok, w1, w2, l, ts, tf)
        tok = emit(l, 3, _wgrad(ap, dx, min(1024, ff), d, 2 * ts, "wgrad_ff_out", relu2=True))
        tok = tok + emit(l, 2, _wgrad(h2, dap, d, min(1024, ff), 2 * ts, "wgrad_ff_in"))
        n, dya, dyc, dd_a, dd_c, dyb, dg_group, dsink = _mix_bwd(dx1, ya, yb, yc, lse_c, sk, g_group[l][None] + tok,
                                                                 w_o, l, tb)
        tok = emit(l, 1, _wgrad(n, dx1, MIX_WIDTH, d, ts, "wgrad_o"))
        cw = cw + tok
        parts_a = [_attn_bwd(za, dya, lse_a, dd_a, dil, A_WIDTH, 1, 2, 1, A_MAX_DIST, "attn_a_bwd_%d" % dil)
                   for dil in DILATIONS]
        parts_c = _attn_bwd(zc, dyc, lse_c, dd_c, 1, C_KV_WIDTH, 3, 4, C_GROUP, C_MAX_DIST, "attn_c_bwd")
        dz, dcw = _dz_assemble(parts_a, parts_c, dyb, zb, cw)
        tok = emit(l, 0, _wgrad(h, dz, d, IN_WIDTH // 4, 2 * ts, "wgrad_in"))
        dx, dg_mix = _qkv_bwd(dz, dx1, x0, g_mix[l][None] + tok, w_in, l, tb, l == 0)
        grads[l] = (dcw, dsink, dg_mix, dg_group, dg_mlp)
    return loss_tile, dx, grads, dg_final


ANY = pl.BlockSpec(memory_space=pl.ANY)
SHARD_AXES = (2, 1, 2, 1)
N_BIG = len(SHARD_AXES)
N_CHIPS = 4
N_DEV = 8


def _mesh_pos():
    return lax.axis_index("x"), lax.axis_index("y"), lax.axis_index("c")


def _flip(v, bit):
    return 1 - v if bit else v


def _place_shard(shard, chip_arr, name):
    _, rows, cols = shard.shape
    tr = min(256, rows)

    def body(chip_ref, x_ref, o_ref):
        o_ref[...] = x_ref[...].astype(BF)

    return pl.pallas_call(
        body, name=name,
        grid_spec=pltpu.PrefetchScalarGridSpec(
            num_scalar_prefetch=1, grid=(2, rows // tr),
            in_specs=[pl.BlockSpec((None, tr, cols), lambda l, i, chip: (l, i, 0))],
            out_specs=pl.BlockSpec((None, None, tr, cols), lambda l, i, chip: (l, chip[0], i, 0))),
        out_shape=jax.ShapeDtypeStruct((2, N_CHIPS, rows, cols), BF),
        compiler_params=_cparams("parallel", "parallel"),
    )(chip_arr, shard)


HBM = pl.BlockSpec(memory_space=pltpu.HBM)
SEM = pl.BlockSpec(memory_space=pltpu.SEMAPHORE)
EFFECT = pltpu.SideEffectType.DATAFLOW_SIDE_EFFECTING

GATHER_GROUPS = (((0, 0),), ((1, 0), (2, 0), (3, 0)), ((0, 1),), ((1, 1), (2, 1), (3, 1)))
GATHER_STARTS = ((0,), (1,), (2, 3))


def _gather_copies(arrs, group, send_sems, recv_sems):
    x, y, c = _mesh_pos()
    me = 2 * x + y
    out = []
    for i, (w, layer) in enumerate(group):
        mine = arrs[w].at[layer, me]
        for j, (qx, qy) in enumerate([(1 - x, y), (x, 1 - y), (1 - x, 1 - y)]):
            landed = arrs[w].at[layer, 2 * qx + qy]
            out.append(tuple(pltpu.make_async_remote_copy(
                src_ref=piece, dst_ref=piece, send_sem=send_sems.at[i * 3 + j], recv_sem=recv_sems.at[i * 3 + j],
                device_id=(qx, qy, c), device_id_type=MESH) for piece in (mine, landed)))
    return out


def _conv_copies(conv_src, conv_dst, send_sems, recv_sems):
    x, y, c = _mesh_pos()
    out = []
    for j, (qx, qy) in enumerate([(1 - x, y), (x, 1 - y), (1 - x, 1 - y)]):
        out.append(tuple(pltpu.make_async_remote_copy(
            src_ref=conv_src, dst_ref=conv_dst.at[q], send_sem=send_sems.at[j], recv_sem=recv_sems.at[j],
            device_id=(qx, qy, c), device_id_type=MESH) for q in (2 * x + y, 2 * qx + qy)))
    return out


def _gather_start(groups, arrs, conv, name, through=None):
    n_sems = 2 * (len(groups) + (conv is not None))
    mats = sorted({w for g in groups for w, _ in GATHER_GROUPS[g]})

    def body(*refs):
        arrs_ref = [None] * N_BIG
        for w, ref in zip(mats, refs):
            arrs_ref[w] = ref
        sems = refs[n_in:n_in + n_sems]
        if conv is not None:
            for cp, _ in _conv_copies(refs[len(mats)], refs[len(mats) + 1], sems[-2], sems[-1]):
                cp.start()
        for k, g in enumerate(groups):
            for cp, _ in _gather_copies(arrs_ref, GATHER_GROUPS[g], sems[2 * k], sems[2 * k + 1]):
                cp.start()

    sem_shapes = []
    for n in [len(GATHER_GROUPS[g]) for g in groups] + ([1] if conv is not None else []):
        sem_shapes += [pltpu.SemaphoreType.DMA((3 * n,))] * 2
    operands = [arrs[w] for w in mats] + ([] if conv is None else list(conv)) + ([] if through is None else [through])
    n_in = len(operands)
    res = pl.pallas_call(
        body, name=name,
        out_shape=tuple(sem_shapes) + tuple(pltpu.HBM(a.shape, a.dtype) for a in operands),
        in_specs=(HBM,) * n_in, out_specs=(SEM,) * n_sems + (HBM,) * n_in,
        input_output_aliases={i: n_sems + i for i in range(n_in)},
        compiler_params=pltpu.CompilerParams(has_side_effects=EFFECT),
    )(*[pltpu.with_memory_space_constraint(a, pltpu.HBM) for a in operands])
    arrs = list(arrs)
    for w, a in zip(mats, res[n_sems:]):
        arrs[w] = a
    return res[:n_sems], arrs, list(res[n_sems + len(mats):])


def _gather_wait(k, sems, arrs, conv, after, name):
    group = GATHER_GROUPS[k]
    mats = sorted({w for w, _ in group})
    n_conv = 0 if conv is None else 2

    def body(*refs):
        local = refs[:len(mats)]
        arrs_ref = [None] * N_BIG
        for w, ref in zip(mats, local):
            arrs_ref[w] = ref
        pos = len(mats) + n_conv
        copies = _gather_copies(arrs_ref, group, refs[pos], refs[pos + 1])
        if conv is not None:
            copies += _conv_copies(refs[len(mats)], refs[len(mats) + 1], refs[pos + 2], refs[pos + 3])
        for send, recv in copies:
            recv.wait_recv()
            send.wait_send()

    operands = [arrs[w] for w in mats] + ([] if conv is None else [conv[1], conv[2]])
    sem_ops = list(sems) + ([] if conv is None else list(conv[0]))
    n_op = len(operands)
    res = pl.pallas_call(
        body, name=name, out_shape=tuple(pltpu.HBM(a.shape, a.dtype) for a in operands),
        in_specs=(HBM,) * n_op + (SEM,) * len(sem_ops) + (ANY,) * len(after), out_specs=(HBM,) * n_op,
        input_output_aliases={i: i for i in range(n_op)},
        compiler_params=pltpu.CompilerParams(has_side_effects=EFFECT),
    )(*operands, *sem_ops, *after)
    arrs = list(arrs)
    for w, a in zip(mats, res):
        arrs[w] = a
    return arrs, (res[-1] if conv is not None else None)


def _grad_shard(ref, w, chip, n):
    start = pl.multiple_of(chip * n, 128)
    if SHARD_AXES[w] == 2:
        return ref.at[:, pl.ds(start, n)]
    return ref.at[pl.ds(start, n), :]


def _slot_shape(g, w):
    shape = list(g.shape)
    shape[SHARD_AXES[w] - 1] //= N_CHIPS
    return (N_DEV - 1,) + tuple(shape)


def _scatter_copies(g_ref, land_ref, send_sems, recv_sems, layer, w):
    x, y, c = _mesh_pos()
    n = g_ref.shape[SHARD_AXES[w] - 1] // N_CHIPS
    out = []
    for r in range(1, N_DEV):
        tx, ty, tc = _flip(x, r & 4), _flip(y, r & 2), _flip(c, r & 1)
        cp = pltpu.make_async_remote_copy(
            src_ref=_grad_shard(g_ref, w, 2 * tx + ty, n), dst_ref=land_ref.at[r - 1], send_sem=send_sems.at[r - 1],
            recv_sem=recv_sems.at[r - 1], device_id=(tx, ty, tc), device_id_type=MESH)
        out.append((cp, (c != layer) if r & 1 else (c == layer)))
    return out


def _scatter_start(items, layer, name):
    n = len(items)

    def body(*refs):
        for i, (w, _, _) in enumerate(items):
            g_ref, land_ref = refs[2 * i], refs[2 * i + 1]
            send_sems, recv_sems = refs[2 * n + 2 * i], refs[2 * n + 2 * i + 1]
            for cp, mine in _scatter_copies(g_ref, land_ref, send_sems, recv_sems, layer, w):
                @pl.when(mine)
                def _():
                    cp.start()
        refs[-1][...] = jnp.zeros_like(refs[-1])

    operands = [a for _, g, land in items for a in (g, land)]
    res = pl.pallas_call(
        body, name=name,
        out_shape=(pltpu.SemaphoreType.DMA((N_DEV - 1,)),) * (2 * n)
        + tuple(pltpu.HBM(a.shape, a.dtype) for a in operands) + (jax.ShapeDtypeStruct((HALO, 128), F32),),
        in_specs=(HBM,) * (2 * n),
        out_specs=(SEM,) * (2 * n) + (HBM,) * (2 * n) + (pl.BlockSpec(memory_space=pltpu.VMEM),),
        input_output_aliases={i: 2 * n + i for i in range(2 * n)},
        compiler_params=pltpu.CompilerParams(has_side_effects=EFFECT),
    )(*[pltpu.with_memory_space_constraint(a, pltpu.HBM) for a in operands])
    return [(res[2 * i], res[2 * i + 1], res[2 * n + 2 * i], res[2 * n + 2 * i + 1]) for i in range(n)], res[-1]


def _scatter_wait(started, land, after, w, name):
    def body(g0_ref, g1_ref, land_ref, ss0, rs0, ss1, rs1, after_ref, g0_out, g1_out, land_out):
        c = lax.axis_index("c")
        for layer, g_ref, ss, rs in ((0, g0_ref, ss0, rs0), (1, g1_ref, ss1, rs1)):
            for cp, mine in _scatter_copies(g_ref, land_ref, ss, rs, layer, w):
                @pl.when(mine)
                def _():
                    cp.wait_send()

                @pl.when(c == layer)
                def _():
                    cp.wait_recv()

    (ss0, rs0, g0), (ss1, rs1, g1) = started
    return pl.pallas_call(
        body, name=name,
        out_shape=(pltpu.HBM(g0.shape, g0.dtype), pltpu.HBM(g1.shape, g1.dtype), pltpu.HBM(land.shape, land.dtype)),
        in_specs=(HBM, HBM, HBM, SEM, SEM, SEM, SEM, ANY), out_specs=(HBM, HBM, HBM),
        input_output_aliases={0: 0, 1: 1, 2: 2}, compiler_params=pltpu.CompilerParams(has_side_effects=EFFECT),
    )(g0, g1, land, ss0, rs0, ss1, rs1, after)


def _sum_slots(g0, g1, slots, w, pos_arr, name):
    _, rows, cols = slots.shape
    tr = min(512, rows)
    nr = rows // tr
    if SHARD_AXES[w] == 2:
        own = pl.BlockSpec((tr, cols), lambda i, pos: (i, pos[0]))
    else:
        own = pl.BlockSpec((tr, cols), lambda i, pos: (pos[0] * nr + i, 0))

    def body(pos_ref, own0_ref, own1_ref, s_ref, o_ref):
        acc = jnp.where(pos_ref[1] == 0, own0_ref[...], own1_ref[...]).astype(F32)
        for r in range(N_DEV - 1):
            acc = acc + s_ref[r].astype(F32)
        o_ref[...] = acc

    return pl.pallas_call(
        body, name=name,
        grid_spec=pltpu.PrefetchScalarGridSpec(
            num_scalar_prefetch=1, grid=(nr,),
            in_specs=[own, own, pl.BlockSpec((N_DEV - 1, tr, cols), lambda i, pos: (0, i, 0))],
            out_specs=pl.BlockSpec((tr, cols), lambda i, pos: (i, 0))),
        out_shape=jax.ShapeDtypeStruct((rows, cols), F32), compiler_params=_cparams("parallel"),
    )(pos_arr, g0, g1, slots)


def _swap_copies(refs, n):
    x, y, c = _mesh_pos()
    return [pltpu.make_async_remote_copy(src_ref=refs[w], dst_ref=refs[n + w], send_sem=refs[2 * n].at[w],
                                         recv_sem=refs[2 * n + 1].at[w], device_id=(x, y, 1 - c), device_id_type=MESH)
            for w in range(n)]


def _swap_start(halves, name):
    n = len(halves)

    def body(*refs):
        for cp in _swap_copies(refs, n):
            cp.start()

    operands = list(halves) + [lax.empty(h.shape, h.dtype) for h in halves]
    res = pl.pallas_call(
        body, name=name,
        out_shape=(pltpu.SemaphoreType.DMA((n,)),) * 2 + tuple(pltpu.HBM(a.shape, a.dtype) for a in operands),
        in_specs=(HBM,) * (2 * n), out_specs=(SEM,) * 2 + (HBM,) * (2 * n),
        input_output_aliases={i: 2 + i for i in range(2 * n)},
        compiler_params=pltpu.CompilerParams(has_side_effects=EFFECT),
    )(*[pltpu.with_memory_space_constraint(a, pltpu.HBM) for a in operands])
    return res[0], res[1], list(res[2:2 + n]), list(res[2 + n:])


def _swap_wait(send_sems, recv_sems, halves, lands, after, name):
    n = len(halves)

    def body(*refs):
        for cp in _swap_copies(refs, n):
            cp.wait_send()
            cp.wait_recv()

    operands = list(halves) + list(lands)
    res = pl.pallas_call(
        body, name=name, out_shape=tuple(pltpu.HBM(a.shape, a.dtype) for a in operands),
        in_specs=(HBM,) * (2 * n) + (SEM, SEM, ANY), out_specs=(HBM,) * (2 * n),
        input_output_aliases={i: i for i in range(2 * n)},
        compiler_params=pltpu.CompilerParams(has_side_effects=EFFECT),
    )(*operands, send_sems, recv_sems, after)
    return list(res[n:])


def _adamw_math(w, g, m, v):
    m = ADAM_B1 * m + (1.0 - ADAM_B1) * g
    v = ADAM_B2 * v + (1.0 - ADAM_B2) * jnp.square(g)
    m_hat = m / (1.0 - ADAM_B1 ** ADAM_STEP)
    v_hat = v / (1.0 - ADAM_B2 ** ADAM_STEP)
    delta = -ADAM_LR * (m_hat / (jnp.sqrt(v_hat) + ADAM_EPS) + ADAM_WD * w)
    return delta, m, v


def _adamw(w, g, m, v, filled, pos_arr, name):
    shape = w.shape
    _, rows, cols = shape
    tr = min(256, rows)

    def body(pos_ref, w_ref, g_ref, m_ref, v_ref, *rest):
        go_ref, d_ref, m2_ref, v2_ref = rest[-4:]
        g = g_ref[...]
        go_ref[...] = g
        d_ref[...], m2_ref[...], v2_ref[...] = _adamw_math(w_ref[...], g, m_ref[...], v_ref[...])

    def layer(pos):
        return pos[1] if filled is None else 1 - pos[1]

    full = pl.BlockSpec((None, tr, cols), lambda i, pos: (layer(pos), i, 0))
    half = pl.BlockSpec((tr, cols), lambda i, pos: (i, 0))
    n_in = 5
    return pl.pallas_call(
        body, name=name,
        grid_spec=pltpu.PrefetchScalarGridSpec(
            num_scalar_prefetch=1, grid=(rows // tr,),
            in_specs=[full, half, full, full] + ([] if filled is None else [ANY] * 4), out_specs=[full] * 4),
        out_shape=[jax.ShapeDtypeStruct(shape, F32)] * 4,
        input_output_aliases={} if filled is None else {n_in + k: k for k in range(4)},
        compiler_params=_cparams("parallel"),
    )(pos_arr, w, g, m, v, *([] if filled is None else filled))


def _small_sync(part, w, m, v):
    rows, cols = part.shape

    def body(p_ref, w_ref, m_ref, v_ref, g_ref, d_ref, m2_ref, v2_ref, slots, send_sems, recv_sems):
        x, y, c = _mesh_pos()
        me = 4 * x + 2 * y + c
        slots[me] = p_ref[...]
        sends = []
        for r in range(1, N_DEV):
            to = (_flip(x, r & 4), _flip(y, r & 2), _flip(c, r & 1))
            sends.append(pltpu.make_async_remote_copy(
                src_ref=p_ref, dst_ref=slots.at[me], send_sem=send_sems.at[r - 1], recv_sem=recv_sems.at[r - 1],
                device_id=to, device_id_type=MESH))
        for cp in sends:
            cp.start()
        for cp in sends:
            cp.wait_recv()
        for cp in sends:
            cp.wait_send()
        g = slots[0]
        for i in range(1, N_DEV):
            g = g + slots[i]
        g_ref[...] = g
        d_ref[...], m2_ref[...], v2_ref[...] = _adamw_math(w_ref[...], g, m_ref[...], v_ref[...])

    vm = pl.BlockSpec(memory_space=pltpu.VMEM)
    return pl.pallas_call(
        body, name="small_sync", in_specs=[vm] * 4, out_specs=[vm] * 4,
        out_shape=[jax.ShapeDtypeStruct((rows, cols), F32)] * 4,
        scratch_shapes=[pltpu.VMEM((N_DEV, rows, cols), F32), pltpu.SemaphoreType.DMA((N_DEV - 1,)),
                        pltpu.SemaphoreType.DMA((N_DEV - 1,))],
    )(part, w, m, v)


PACK_W = 256


def _pack_rows(n):
    return -(-n // (HALO * PACK_W)) * HALO


def _pack_small(parts):
    out = []
    for a in parts:
        flat = a.reshape(-1)
        out.append(jnp.pad(flat, (0, _pack_rows(flat.size) * PACK_W - flat.size)).reshape(-1, PACK_W))
    return jnp.concatenate(out, axis=0)


def _unpack_small(p, shapes):
    out, row = [], 0
    for shape in shapes:
        n = 1
        for k in shape:
            n *= k
        out.append(p[row:row + _pack_rows(n)].reshape(-1)[:n].reshape(shape))
        row += _pack_rows(n)
    return out


def kernel(x, w_in, conv_w, sinks, g_mix, g_group, w_o, g_mlp, w_ff_in, w_ff_out, g_final, loss_target, m_w_in, m_conv_w, m_sinks, m_g_mix, m_g_group, m_w_o, m_g_mlp, m_w_ff_in, m_w_ff_out, m_g_final, v_w_in, v_conv_w, v_sinks, v_g_mix, v_g_group, v_w_o, v_g_mlp, v_w_ff_in, v_w_ff_out, v_g_final):
    chip = 2 * lax.axis_index("x") + lax.axis_index("y")
    conv_n = conv_w.shape[2]

    pos_arr = jnp.stack([chip, lax.axis_index("c")]).astype(jnp.int32)
    shards = (w_in, w_o, w_ff_in, w_ff_out)
    conv_tile = jnp.pad(conv_w.reshape(6, conv_n), ((0, HALO - 6), (0, 128 - conv_n)))
    placed = [_place_shard(w_in, pos_arr[:1], "place_shard_0"), None, None, None]
    sems_a, placed, conv_thru = _gather_start(
        GATHER_STARTS[0], placed, (conv_tile, lax.empty((N_CHIPS,) + conv_tile.shape, conv_tile.dtype)),
        "gather_start_0")
    for i in range(1, N_BIG):
        placed[i] = _place_shard(shards[i], pos_arr[:1], "place_shard_%d" % i)
    full = {"arrs": placed, "conv": None, "sems": list(sems_a[:2])}
    target = _to_strips(loss_target[0], placed[:1], "to_strips_target")

    def fetch(stage, layer, after):
        k = 2 * layer + stage
        sems = full["sems"][2 * k:2 * k + 2]
        if k == 0:
            full["arrs"], land = _gather_wait(0, sems, full["arrs"], (sems_a[-2:], *conv_thru), (after, target),
                                              "gather_wait_0")
            conv_all = lax.dynamic_update_slice(land, conv_tile[None], (chip, 0, 0))
            full["conv"] = conv_all[:, :6, :conv_n].reshape(N_CHIPS, 2, 3, conv_n).transpose(1, 2, 0, 3).reshape(
                2, 3, CONV_CH)
            sems_b, full["arrs"], rest = _gather_start(GATHER_STARTS[1], full["arrs"], None, "gather_start_1",
                                                       through=full["arrs"][0])
            full["arrs"][0] = rest[-1]
            full["sems"] += list(sems_b)
        else:
            full["arrs"], _ = _gather_wait(k, sems, full["arrs"], None, (after,), "gather_wait_%d" % k)
        if k == 1:
            sems_c, full["arrs"], _ = _gather_start(GATHER_STARTS[2], full["arrs"], None, "gather_start_2")
            full["sems"] += list(sems_c)
        return (*full["arrs"], full["conv"])

    lands, started, pending = [None] * N_BIG, {}, []

    def emit(layer, w, g):
        if lands[w] is None:
            lands[w] = lax.empty(_slot_shape(g, w), g.dtype)
        pending.append((w, g, lands[w]))
        if not (w == 0 or (layer == 0 and w == 1)):
            return jnp.zeros((), F32)
        name = "scatter_start_%d_%d" % (layer, len(pending))
        done, token = _scatter_start(list(pending), layer, name)
        for (w_i, _, _), (ss, rs, g_thru, land) in zip(pending, done):
            started[layer, w_i], lands[w_i] = (ss, rs, g_thru), land
        pending.clear()
        return token[0, 0]

    loss_tile, dx, grads, dg_final = _local_step(_to_strips(x[0], placed, "to_strips_x"), target, fetch,
                                                 w_ff_in.shape[2] * N_CHIPS,
                                                 sinks, g_mix, g_group, g_mlp, g_final, emit)

    wmv = ((w_in, m_w_in, v_w_in), (w_o, m_w_o, v_w_o), (w_ff_in, m_w_ff_in, v_w_ff_in),
           (w_ff_out, m_w_ff_out, v_w_ff_out))
    big, after = [None] * N_BIG, dx
    for name, ws in (("swap_rest", (1, 2, 3)), ("swap_in", (0,))):
        own = []
        for w in ws:
            g0, g1, slots = _scatter_wait((started[0, w], started[1, w]), lands[w], after, w, "scatter_wait_%d" % w)
            own.append(_sum_slots(g0, g1, slots, w, pos_arr, "sum_slots_%d" % w))
        send_sems, recv_sems, own, zones = _swap_start(own, name + "_start")
        for w, g in zip(ws, own):
            big[w] = _adamw(wmv[w][0], g, wmv[w][1], wmv[w][2], None, pos_arr, "adamw_own_%d" % w)
        theirs = _swap_wait(send_sems, recv_sems, own, zones, big[ws[-1]][1], name + "_wait")
        for w, g in zip(ws, theirs):
            big[w] = _adamw(wmv[w][0], g, wmv[w][1], wmv[w][2], big[w], pos_arr, "adamw_other_%d" % w)
        after = big[ws[-1]][1]

    def both(i):
        return jnp.stack([grads[0][i][0], grads[1][i][0]])
    dconv = jnp.stack([grads[0][0][:3], grads[1][0][:3]])
    dsinks = jnp.stack([grads[0][1][0, ::HEAD_DIM], grads[1][1][0, ::HEAD_DIM]])
    part = _pack_small([both(2), both(3), both(4), dg_final[0], dconv, dsinks, loss_tile[0, 0]])

    def spread(shard):
        return lax.dynamic_update_slice(jnp.zeros((2, 3, CONV_CH), F32), shard, (0, 0, chip * conv_n))
    zero = jnp.zeros((), F32)
    packs = [_pack_small([a, b, c_, e, spread(f), g_, zero]) for a, b, c_, e, f, g_ in (
        (g_mix, g_group, g_mlp, g_final, conv_w, sinks),
        (m_g_mix, m_g_group, m_g_mlp, m_g_final, m_conv_w, m_sinks),
        (v_g_mix, v_g_group, v_g_mlp, v_g_final, v_conv_w, v_sinks))]
    shapes = [g_mix.shape, g_group.shape, g_mlp.shape, g_final.shape, (2, 3, CONV_CH), sinks.shape, ()]
    small = [_unpack_small(p, shapes) for p in _small_sync(part, *packs)]

    def shard_of(full):
        return lax.dynamic_slice(full, (0, 0, chip * conv_n), (2, 3, conv_n))
    small = [(s[0], s[1], s[2], s[3], shard_of(s[4]), s[5], s[6]) for s in small]
    loss = small[0][6]

    def ordered(kind):
        b = [big[i][kind] for i in range(N_BIG)]
        s = small[kind]
        return [b[0], s[4], s[5], s[0], s[1], b[1], s[2], b[2], b[3], s[3]]

    return (loss, dx[None], *ordered(0), *ordered(1), *ordered(2), *ordered(3))
```

```python
import functools

import jax
import jax.numpy as jnp
from jax import lax
from jax.experimental import pallas as pl
from jax.experimental.pallas import tpu as pltpu

HEAD_DIM = 64
N_HEADS = 6
C_GROUP = 3
A_WIDTH = N_HEADS * HEAD_DIM
C_KV_WIDTH = 2 * HEAD_DIM
CONV_CH = 256
ZA_W = 3 * A_WIDTH
ZB_W = 3 * CONV_CH
ZC_W = A_WIDTH + 2 * C_KV_WIDTH
IN_WIDTH = ZA_W + ZB_W + ZC_W
MIX_WIDTH = A_WIDTH + CONV_CH + A_WIDTH
DILATIONS = (1, 4, 16)
A_MAX_DIST = 128
C_MAX_DIST = 127
TQ = 128
EPS = 1e-6
SCALE = HEAD_DIM ** -0.5
NEG = -1e30
HALO = 8

ADAM_LR = 0.001
ADAM_B1 = 0.9
ADAM_B2 = 0.999
ADAM_EPS = 1e-08
ADAM_WD = 0.01
ADAM_STEP = 10

BF = jnp.bfloat16
F32 = jnp.float32
MESH = pl.DeviceIdType.MESH
VMEM_LIMIT = 56 * 1024 * 1024


def _cparams(*sem):
    return pltpu.CompilerParams(dimension_semantics=sem, vmem_limit_bytes=VMEM_LIMIT)


def _nt(a, b):
    return lax.dot_general(a, b, (((1,), (1,)), ((), ())), preferred_element_type=F32)


def _tn(a, b):
    return lax.dot_general(a, b, (((0,), (0,)), ((), ())), preferred_element_type=F32)


def _nn(a, b):
    return jnp.dot(a, b, preferred_element_type=F32)


def _rows(tb, w):
    return pl.BlockSpec((tb, w), lambda i: (i, 0))


def _whole(shape):
    return pl.BlockSpec(shape, lambda *_: (0,) * len(shape))


def _layer(shape, l):
    return pl.BlockSpec((None,) + shape, lambda *_: (l,) + (0,) * len(shape))


def _rms_scale(v):
    return lax.rsqrt(jnp.mean(v * v, axis=-1, keepdims=True) + EPS)


def _norm_bwd(dxhat, xhat, r):
    return r * (dxhat - xhat * jnp.mean(dxhat * xhat, axis=-1, keepdims=True))


def _qkv_fwd(x, g, w_all, l, tb):
    s, d = x.shape

    def body(x_ref, g_ref, w_ref, h_ref, za_ref, zb_ref, zc_ref):
        xv = x_ref[...]
        h = ((xv * _rms_scale(xv)) * g_ref[...]).astype(BF)
        h_ref[...] = h
        z = jnp.concatenate([_nn(h, w_ref[k]) for k in range(N_CHIPS)], axis=1)
        za_ref[...] = z[:, :ZA_W]
        zb_ref[...] = z[:, ZA_W:ZA_W + ZB_W]
        zc_ref[...] = z[:, ZA_W + ZB_W:]

    return pl.pallas_call(
        body, grid=(s // tb,), name="qkv_fwd",
        in_specs=[_rows(tb, d), _whole((1, d)), _layer((N_CHIPS, d, IN_WIDTH // N_CHIPS), l)],
        out_specs=[_rows(tb, d), _rows(tb, ZA_W), _rows(tb, ZB_W), _rows(tb, ZC_W)],
        out_shape=[jax.ShapeDtypeStruct((s, d), BF), jax.ShapeDtypeStruct((s, ZA_W), F32),
                   jax.ShapeDtypeStruct((s, ZB_W), F32), jax.ShapeDtypeStruct((s, ZC_W), F32)],
        compiler_params=_cparams("parallel"),
    )(x, g, w_all)


N_STRIPS = 16


def _strips(a):
    s, w = a.shape
    return a.reshape(4, 4, s // N_STRIPS, w)


P_ROWS = {16: TQ, 4: 32, 1: 8}


def _p_sub(s, dil, most):
    while (s // dil // TQ) % most:
        most //= 2
    return most


def _p_grid(s, dil, n_sub):
    nb = s // dil // TQ // n_sub
    return {16: (4, 4, nb), 4: (4, nb), 1: (nb,)}[dil]


def _p_spec(dil, cw, col, n_sub, prev=False):
    rows = P_ROWS[dil] * (1 if prev else n_sub)

    def blk(j):
        return jnp.maximum(n_sub * j - 1, 0) if prev else j
    if dil == 16:
        return pl.BlockSpec((None, None, rows, cw), lambda f, e, j: (f, e, blk(j), col))
    if dil == 4:
        return pl.BlockSpec((None, 4, rows, cw), lambda f, j: (f, 0, blk(j), col))
    return pl.BlockSpec((4, 4, rows, cw), lambda j: (0, 0, blk(j), col))


def _block_pos(i, dil):
    if dil == 16:
        return i
    if dil == 4:
        return 4 * (i % 32) + i // 32
    return 16 * (i % 8) + 4 * ((i // 8) % 4) + i // 32


def _band_mask(b, dil, max_dist):
    qi = _block_pos(lax.broadcasted_iota(jnp.int32, (TQ, 2 * TQ), 0), dil)
    col = lax.broadcasted_iota(jnp.int32, (TQ, 2 * TQ), 1)
    cur = col >= TQ
    dist = qi - _block_pos(col % TQ, dil) + jnp.where(cur, 0, TQ)
    return (dist >= 0) & (dist <= max_dist) & (cur | (b > 0))


def _hs(h):
    return slice(h * HEAD_DIM, (h + 1) * HEAD_DIM)


def _ld(ref, cols, rows=slice(None)):
    v = ref[..., rows, cols]
    return v.reshape(TQ, v.shape[-1])


def _st(ref, cols, val, rows=slice(None)):
    lead = ref.shape[:-2] + (ref.shape[-2] if rows == slice(None) else rows.stop - rows.start,)
    ref[..., rows, cols] = val.reshape(lead + (val.shape[-1],))


def _attn_fwd(z, dil, kw, kcol, vcol, n_rep, max_dist, name):
    s, zw = z.shape
    n_sub = _p_sub(s, dil, 2)
    grid = _p_grid(s, dil, n_sub)

    def body(q_ref, kp_ref, kc_ref, vp_ref, vc_ref, o_ref, lse_ref):
        for t in range(n_sub):
            rows = slice(t * P_ROWS[dil], (t + 1) * P_ROWS[dil])
            before = (slice(None),) if t == 0 else (slice((t - 1) * P_ROWS[dil], t * P_ROWS[dil]),)
            kb_ref, vb_ref = (kp_ref, vp_ref) if t == 0 else (kc_ref, vc_ref)
            mask = _band_mask(n_sub * pl.program_id(len(grid) - 1) if t == 0 else 1, dil, max_dist)
            scs, v2s = [], []
            for kh in range(N_HEADS // n_rep):
                k2 = jnp.concatenate([_ld(kb_ref, _hs(kh), *before), _ld(kc_ref, _hs(kh), rows)], axis=0).astype(BF)
                v2s.append(jnp.concatenate([_ld(vb_ref, _hs(kh), *before), _ld(vc_ref, _hs(kh), rows)],
                                           axis=0).astype(BF))
                for h in range(kh * n_rep, (kh + 1) * n_rep):
                    q = (_ld(q_ref, _hs(h), rows) * SCALE).astype(BF)
                    scs.append(jnp.where(mask, _nt(q, k2), NEG))
            for h, sc in enumerate(scs):
                m = jnp.max(sc, axis=1, keepdims=True)
                p = jnp.exp(sc - m)
                l = jnp.sum(p, axis=1, keepdims=True)
                _st(o_ref, _hs(h), _nn(p.astype(BF), v2s[h // n_rep]) / l, rows)
                _st(lse_ref, _hs(h), jnp.broadcast_to(m + jnp.log(l), (TQ, HEAD_DIM)), rows)

    res = pl.pallas_call(
        body, grid=grid, name=name,
        in_specs=[_p_spec(dil, A_WIDTH, 0, n_sub), _p_spec(dil, kw, kcol, n_sub, True), _p_spec(dil, kw, kcol, n_sub),
                  _p_spec(dil, kw, vcol, n_sub, True), _p_spec(dil, kw, vcol, n_sub)],
        out_specs=[_p_spec(dil, A_WIDTH, 0, n_sub)] * 2,
        out_shape=[jax.ShapeDtypeStruct((4, 4, s // N_STRIPS, A_WIDTH), F32)] * 2,
        compiler_params=_cparams(*(("parallel",) * len(grid))),
    )(*[_strips(z)] * 5)
    return [a.reshape(s, A_WIDTH) for a in res]


def _attn_merge(parts_a, part_c, sink_row, tb):
    s = part_c[0].shape[0]
    n_a = len(parts_a)

    def body(*refs):
        ins, sink_ref = refs[:2 * n_a + 2], refs[2 * n_a + 2]
        ya_ref, lsea_ref, yc_ref, lsec_ref = refs[2 * n_a + 3:]
        lses = [ins[2 * p + 1][...] for p in range(n_a)]
        m = functools.reduce(jnp.maximum, lses)
        ws = [jnp.exp(v - m) for v in lses]
        l = functools.reduce(jnp.add, ws)
        ya_ref[...] = functools.reduce(jnp.add, [w * ins[2 * p][...] for p, w in enumerate(ws)]) / l
        lsea_ref[...] = m + jnp.log(l)
        o_c, lse_c = [r[...] for r in ins[2 * n_a:]]
        sk = sink_ref[...]
        m2 = jnp.maximum(lse_c, sk)
        w = jnp.exp(lse_c - m2)
        l2 = w + jnp.exp(sk - m2)
        yc_ref[...] = o_c * (w / l2)
        lsec_ref[...] = m2 + jnp.log(l2)

    return pl.pallas_call(
        body, grid=(s // tb,), name="attn_merge",
        in_specs=[_rows(tb, A_WIDTH)] * (2 * n_a + 2) + [_whole((1, A_WIDTH))],
        out_specs=[_rows(tb, A_WIDTH)] * 4, out_shape=[jax.ShapeDtypeStruct((s, A_WIDTH), F32)] * 4,
        compiler_params=_cparams("parallel"),
    )(*[a for part in parts_a + [part_c] for a in part], sink_row)


def _shift_down(v, n, halo):
    rows = v.shape[0]
    out = pltpu.roll(v, n, 0)
    row = lax.broadcasted_iota(jnp.int32, v.shape, 0)
    for t in range(n):
        out = jnp.where(row == t, halo[HALO - n + t:HALO - n + t + 1, :], out)
    return out


def _shift_up(v, n, halo):
    rows = v.shape[0]
    out = pltpu.roll(v, rows - n, 0)
    row = lax.broadcasted_iota(jnp.int32, v.shape, 0)
    for t in range(n):
        out = jnp.where(row == rows - n + t, halo[t:t + 1, :], out)
    return out


def _strip(v, b):
    return v[b % 4, b // 4]


def _conv_strips(zb, prev, cw):
    gb = [_strip(zb, b)[:, :CONV_CH] for b in range(N_STRIPS)]
    gc = [_strip(zb, b)[:, CONV_CH:2 * CONV_CH] for b in range(N_STRIPS)]
    xb = [_strip(zb, b)[:, 2 * CONV_CH:] for b in range(N_STRIPS)]
    u = [g * v for g, v in zip(gc, xb)]
    uh = prev[:, :, CONV_CH:2 * CONV_CH] * prev[:, :, 2 * CONV_CH:]
    wrapped = {14: _shift_down(u[14], 1, uh[2]), 15: _shift_down(u[15], 1, uh[3])}
    u1 = [u[b - 1] if b >= 1 else wrapped[15] for b in range(N_STRIPS)]
    u2 = [u[b - 2] if b >= 2 else wrapped[14 + b] for b in range(N_STRIPS)]
    c = [cw[0:1, :] * u2[b] + cw[1:2, :] * u1[b] + cw[2:3, :] * u[b] for b in range(N_STRIPS)]
    return gb, gc, xb, u, u1, u2, c


def _strip_rows(ta, w):
    return pl.BlockSpec((4, 4, ta, w), lambda i: (0, 0, i, 0))


def _prev_rows(ta, w):
    return pl.BlockSpec((4, None, HALO, w), lambda i: (0, 3, jnp.maximum(i * (ta // HALO) - 1, 0), 0))


def _next_rows(ta, w, nblk):
    return pl.BlockSpec((4, None, HALO, w),
                        lambda i: (0, 0, jnp.minimum((i + 1) * (ta // HALO), nblk * (ta // HALO) - 1), 0))


def _mix_fwd(x, ya, yc, zb, cw, gg, wo_all, l, tb):
    s, d = x.shape
    ta = tb // N_STRIPS

    def body(x_ref, ya_ref, yc_ref, zb_ref, zbp_ref, cw_ref, gg_ref, wo_ref, x1_ref, yb_ref):
        i = pl.program_id(0)
        prev = jnp.where(i > 0, zbp_ref[...], 0.0)
        gb, _, _, _, _, _, c = _conv_strips(zb_ref[...], prev, cw_ref[...])
        for b in range(N_STRIPS):
            yb_ref[b % 4, b // 4] = gb[b] * c[b]
        yb = yb_ref[...].reshape(tb, CONV_CH)
        ya, yc = ya_ref[...].reshape(tb, A_WIDTH), yc_ref[...].reshape(tb, A_WIDTH)
        n = jnp.concatenate([ya * _rms_scale(ya), yb * _rms_scale(yb), yc * _rms_scale(yc)], axis=1)
        n = (n * gg_ref[...]).astype(BF)
        x1 = x_ref[...].reshape(tb, d) + _nn(n, wo_ref[...].reshape(MIX_WIDTH, d))
        x1_ref[...] = x1.reshape(4, 4, ta, d)

    res = pl.pallas_call(
        body, grid=(s // tb,), name="mix_fwd",
        in_specs=[_strip_rows(ta, d), _strip_rows(ta, A_WIDTH), _strip_rows(ta, A_WIDTH), _strip_rows(ta, ZB_W),
                  _prev_rows(ta, ZB_W), _whole((HALO, CONV_CH)), _whole((1, MIX_WIDTH)),
                  _layer((N_CHIPS, MIX_WIDTH // N_CHIPS, d), l)],
        out_specs=[_strip_rows(ta, d), _strip_rows(ta, CONV_CH)],
        out_shape=[jax.ShapeDtypeStruct((4, 4, s // N_STRIPS, d), F32),
                   jax.ShapeDtypeStruct((4, 4, s // N_STRIPS, CONV_CH), F32)],
        compiler_params=_cparams("parallel"),
    )(_strips(x), _strips(ya), _strips(yc), _strips(zb), _strips(zb), cw, gg, wo_all)
    return res[0].reshape(s, d), res[1].reshape(s, CONV_CH)


def _mlp_fwd(x1, g, w1_all, w2_all, l, tb, tf):
    s, d = x1.shape
    ff = w1_all.shape[1] * w1_all.shape[3]
    nj = ff // tf

    def body(x_ref, g_ref, w1_ref, w2_ref, x2_ref, h2_ref, ap_ref, acc):
        j = pl.program_id(1)

        @pl.when(j == 0)
        def _():
            xv = x_ref[...]
            h2_ref[...] = ((xv * _rms_scale(xv)) * g_ref[...]).astype(BF)
            acc[...] = jnp.zeros_like(acc)

        ap = _nn(h2_ref[...], w1_ref[...])
        ap_ref[...] = ap.astype(BF)
        a = jnp.square(jnp.maximum(ap, 0.0)).astype(BF)
        acc[...] += _nn(a, w2_ref[...])

        @pl.when(j == nj - 1)
        def _():
            x2_ref[...] = x_ref[...] + acc[...]

    return pl.pallas_call(
        body, grid=(s // tb, nj), name="mlp_fwd",
        in_specs=[pl.BlockSpec((tb, d), lambda i, j: (i, 0)), _whole((1, d)),
                  pl.BlockSpec((None, None, d, tf), lambda i, j: (l, j, 0, 0)),
                  pl.BlockSpec((None, None, tf, d), lambda i, j: (l, j, 0, 0))],
        out_specs=[pl.BlockSpec((tb, d), lambda i, j: (i, 0)), pl.BlockSpec((tb, d), lambda i, j: (i, 0)),
                   pl.BlockSpec((tb, tf), lambda i, j: (i, j))],
        out_shape=[jax.ShapeDtypeStruct((s, d), F32), jax.ShapeDtypeStruct((s, d), BF),
                   jax.ShapeDtypeStruct((s, ff), BF)],
        scratch_shapes=[pltpu.VMEM((tb, d), F32)],
        compiler_params=_cparams("parallel", "arbitrary"),
    )(x1, g, w1_all, w2_all)


def _loss_head(x, g, tgt, tb):
    s, d = x.shape

    def body(x_ref, g_ref, t_ref, dx_ref, loss_ref, dg_ref):
        i = pl.program_id(0)

        @pl.when(i == 0)
        def _():
            loss_ref[...] = jnp.zeros_like(loss_ref)
            dg_ref[...] = jnp.zeros_like(dg_ref)

        xv = x_ref[...]
        r = _rms_scale(xv)
        xhat = xv * r
        err = xhat * g_ref[...] - t_ref[...]
        part = jnp.sum(jnp.mean(jnp.square(err), axis=-1, keepdims=True), axis=0, keepdims=True)
        loss_ref[...] += 0.5 * part
        dy = err * (1.0 / d)
        dg_ref[...] += jnp.sum(dy * xhat, axis=0, keepdims=True)
        dx_ref[...] = _norm_bwd(dy * g_ref[...], xhat, r)

    return pl.pallas_call(
        body, grid=(s // tb,), name="loss_head",
        in_specs=[_rows(tb, d), _whole((1, d)), _rows(tb, d)],
        out_specs=[_rows(tb, d), _whole((HALO, 128)), _whole((HALO, d))],
        out_shape=[jax.ShapeDtypeStruct((s, d), F32), jax.ShapeDtypeStruct((HALO, 128), F32),
                   jax.ShapeDtypeStruct((HALO, d), F32)],
        compiler_params=_cparams("arbitrary"),
    )(x, g, tgt)


def _mlp_bwd(dx2, x1, ap, g, w1_all, w2_all, l, tb, tf):
    s, d = x1.shape
    ff = ap.shape[1]
    nj = ff // tf

    def body(dx2_ref, x1_ref, ap_ref, g_ref, w1_ref, w2_ref, dx1_ref, dap_ref, dg_ref, acc):
        i, j = pl.program_id(0), pl.program_id(1)

        @pl.when((i == 0) & (j == 0))
        def _():
            dg_ref[...] = jnp.zeros_like(dg_ref)

        @pl.when(j == 0)
        def _():
            acc[...] = jnp.zeros_like(acc)

        da = _nt(dx2_ref[...].astype(BF), w2_ref[...])
        dap = (da * (2.0 * jnp.maximum(ap_ref[...].astype(F32), 0.0))).astype(BF)
        dap_ref[...] = dap
        acc[...] += _nt(dap, w1_ref[...])

        @pl.when(j == nj - 1)
        def _():
            xv = x1_ref[...]
            r = _rms_scale(xv)
            xhat = xv * r
            dh = acc[...]
            dg_ref[...] += jnp.sum(dh * xhat, axis=0, keepdims=True)
            dx1_ref[...] = dx2_ref[...] + _norm_bwd(dh * g_ref[...], xhat, r)

    return pl.pallas_call(
        body, grid=(s // tb, nj), name="mlp_bwd",
        in_specs=[pl.BlockSpec((tb, d), lambda i, j: (i, 0)), pl.BlockSpec((tb, d), lambda i, j: (i, 0)),
                  pl.BlockSpec((tb, tf), lambda i, j: (i, j)),
                  _whole((1, d)), pl.BlockSpec((None, None, d, tf), lambda i, j: (l, j, 0, 0)),
                  pl.BlockSpec((None, None, tf, d), lambda i, j: (l, j, 0, 0))],
        out_specs=[pl.BlockSpec((tb, d), lambda i, j: (i, 0)), pl.BlockSpec((tb, tf), lambda i, j: (i, j)),
                   _whole((HALO, d))],
        out_shape=[jax.ShapeDtypeStruct((s, d), F32), jax.ShapeDtypeStruct((s, ff), BF),
                   jax.ShapeDtypeStruct((HALO, d), F32)],
        scratch_shapes=[pltpu.VMEM((tb, d), F32)],
        compiler_params=_cparams("arbitrary", "arbitrary"),
    )(dx2, x1, ap, g, w1_all, w2_all)


def _wgrad(a, b, tm, tn, ts, name, relu2=False):
    s, m = a.shape
    n = b.shape[1]
    ns = s // ts

    def body(a_ref, b_ref, o_ref, acc):
        k = pl.program_id(2)

        @pl.when(k == 0)
        def _():
            acc[...] = jnp.zeros_like(acc)

        av = a_ref[...]
        if relu2:
            av = jnp.square(jnp.maximum(av.astype(F32), 0.0)).astype(BF)
        acc[...] += _tn(av, b_ref[...].astype(BF))

        @pl.when(k == ns - 1)
        def _():
            o_ref[...] = acc[...].astype(BF)

    return pl.pallas_call(
        body, grid=(m // tm, n // tn, ns), name=name,
        in_specs=[pl.BlockSpec((ts, tm), lambda i, j, k: (k, i)), pl.BlockSpec((ts, tn), lambda i, j, k: (k, j))],
        out_specs=pl.BlockSpec((tm, tn), lambda i, j, k: (i, j)),
        out_shape=jax.ShapeDtypeStruct((m, n), BF),
        scratch_shapes=[pltpu.VMEM((tm, tn), F32)],
        compiler_params=_cparams("parallel", "parallel", "arbitrary"),
    )(a, b)


def _mix_bwd(dx1, ya, yb, yc, lse_c, sink_row, gg, wo_all, l, tb):
    s, d = dx1.shape

    def body(dx_ref, ya_ref, yb_ref, yc_ref, lse_ref, sink_ref, gg_ref, wo_ref,
             n_ref, dya_ref, dyc_ref, da_ref, dc_ref, dyb_ref, dg_ref, dsink_ref):
        i = pl.program_id(0)

        @pl.when(i == 0)
        def _():
            dg_ref[...] = jnp.zeros_like(dg_ref)
            dsink_ref[...] = jnp.zeros_like(dsink_ref)

        dn = _nt(dx_ref[...].astype(BF), wo_ref[...].reshape(MIX_WIDTH, d))
        ys = [ya_ref[...], yb_ref[...], yc_ref[...]]
        rs = [_rms_scale(v) for v in ys]
        nhat = jnp.concatenate([v * r for v, r in zip(ys, rs)], axis=1)
        gg = gg_ref[...]
        n_ref[...] = (nhat * gg).astype(BF)
        dg_ref[...] += jnp.sum(dn * nhat, axis=0, keepdims=True)
        dnh = dn * gg
        bounds = [(0, A_WIDTH), (A_WIDTH, A_WIDTH + CONV_CH), (A_WIDTH + CONV_CH, MIX_WIDTH)]
        dys = [_norm_bwd(dnh[:, lo:hi], nhat[:, lo:hi], r) for (lo, hi), r in zip(bounds, rs)]
        dyb_ref[...] = dys[1]
        head = [lax.broadcasted_iota(jnp.int32, (A_WIDTH, A_WIDTH), k) // HEAD_DIM for k in (0, 1)]
        ones = (head[0] == head[1]).astype(BF)
        for dy, y, dy_ref, dd_ref in ((dys[0], ys[0], dya_ref, da_ref), (dys[2], ys[2], dyc_ref, dc_ref)):
            dy_ref[...] = dy
            t = dy * y
            hi = t.astype(BF)
            dd_ref[...] = _nn(hi, ones) + _nn((t - hi.astype(F32)).astype(BF), ones)
        dsink_ref[...] -= jnp.sum(jnp.exp(sink_ref[...] - lse_ref[...]) * dc_ref[...], axis=0, keepdims=True)

    return pl.pallas_call(
        body, grid=(s // tb,), name="mix_bwd",
        in_specs=[_rows(tb, d), _rows(tb, A_WIDTH), _rows(tb, CONV_CH), _rows(tb, A_WIDTH), _rows(tb, A_WIDTH),
                  _whole((1, A_WIDTH)), _whole((1, MIX_WIDTH)), _layer((N_CHIPS, MIX_WIDTH // N_CHIPS, d), l)],
        out_specs=[_rows(tb, MIX_WIDTH), _rows(tb, A_WIDTH), _rows(tb, A_WIDTH), _rows(tb, A_WIDTH),
                   _rows(tb, A_WIDTH), _rows(tb, CONV_CH), _whole((HALO, MIX_WIDTH)), _whole((HALO, A_WIDTH))],
        out_shape=[jax.ShapeDtypeStruct((s, MIX_WIDTH), BF), jax.ShapeDtypeStruct((s, A_WIDTH), F32),
                   jax.ShapeDtypeStruct((s, A_WIDTH), F32), jax.ShapeDtypeStruct((s, A_WIDTH), F32),
                   jax.ShapeDtypeStruct((s, A_WIDTH), F32), jax.ShapeDtypeStruct((s, CONV_CH), F32),
                   jax.ShapeDtypeStruct((HALO, MIX_WIDTH), F32), jax.ShapeDtypeStruct((HALO, A_WIDTH), F32)],
        compiler_params=_cparams("arbitrary"),
    )(dx1, ya, yb, yc, lse_c, sink_row, gg, wo_all)


def _attn_bwd(z, dy, lse, dd, dil, kw, kcol, vcol, n_rep, max_dist, name):
    s, zw = z.shape
    n_sub = _p_sub(s, dil, 2) if n_rep == 1 else 1
    grid = _p_grid(s, dil, n_sub)
    n_kv = N_HEADS // n_rep
    dt = F32 if dil == 1 else BF

    def body(q_ref, kp_ref, kc_ref, vp_ref, vc_ref, dy_ref, lse_ref, dd_ref, dq_ref, dkp_ref, dkc_ref, dvp_ref, dvc_ref):
        for t in range(n_sub):
            rows = slice(t * P_ROWS[dil], (t + 1) * P_ROWS[dil])
            before = (slice(None),) if t == 0 else (slice((t - 1) * P_ROWS[dil], t * P_ROWS[dil]),)
            kb_ref, vb_ref = (kp_ref, vp_ref) if t == 0 else (kc_ref, vc_ref)
            mask = _band_mask(n_sub * pl.program_id(len(grid) - 1) if t == 0 else 1, dil, max_dist)
            k2s, qs, dys, scs, dps = [], [], [], [], []
            for kh in range(n_kv):
                k2s.append(jnp.concatenate([_ld(kb_ref, _hs(kh), *before), _ld(kc_ref, _hs(kh), rows)],
                                           axis=0).astype(BF))
                v2 = jnp.concatenate([_ld(vb_ref, _hs(kh), *before), _ld(vc_ref, _hs(kh), rows)], axis=0).astype(BF)
                for h in range(kh * n_rep, (kh + 1) * n_rep):
                    qs.append((_ld(q_ref, _hs(h), rows) * SCALE).astype(BF))
                    dys.append(_ld(dy_ref, _hs(h), rows).astype(BF))
                    scs.append(jnp.where(mask, _nt(qs[h], k2s[kh]), NEG))
                    dps.append(_nt(dys[h], v2))
            for kh in range(n_kv):
                k2 = k2s[kh]
                dk2 = jnp.zeros((2 * TQ, HEAD_DIM), F32)
                dv2 = jnp.zeros((2 * TQ, HEAD_DIM), F32)
                for h in range(kh * n_rep, (kh + 1) * n_rep):
                    lse_h = _ld(lse_ref, slice(h * HEAD_DIM, h * HEAD_DIM + 1), rows)
                    dd_h = _ld(dd_ref, slice(h * HEAD_DIM, h * HEAD_DIM + 1), rows)
                    p = jnp.exp(scs[h] - lse_h)
                    ds = (p * (dps[h] - dd_h)).astype(BF)
                    _st(dq_ref, _hs(h), (_nn(ds, k2) * SCALE).astype(dt), rows)
                    dk2 = dk2 + _tn(ds, qs[h])
                    dv2 = dv2 + _tn(p.astype(BF), dys[h])
                _st(dkp_ref, _hs(kh), dk2[:TQ].astype(dt), rows)
                _st(dkc_ref, _hs(kh), dk2[TQ:].astype(dt), rows)
                _st(dvp_ref, _hs(kh), dv2[:TQ].astype(dt), rows)
                _st(dvc_ref, _hs(kh), dv2[TQ:].astype(dt), rows)

    args = [_strips(z)] * 5 + [_strips(a) for a in (dy, lse, dd)]
    pair = _p_spec(dil, A_WIDTH, 0, n_sub)
    in_specs = [pair, _p_spec(dil, kw, kcol, n_sub, True), _p_spec(dil, kw, kcol, n_sub),
                _p_spec(dil, kw, vcol, n_sub, True), _p_spec(dil, kw, vcol, n_sub)] + [pair] * 3
    out_specs = [pair] + [_p_spec(dil, kw, 0, n_sub)] * 4
    na = s // N_STRIPS
    out_shape = [jax.ShapeDtypeStruct((4, 4, na, A_WIDTH), dt)] + [jax.ShapeDtypeStruct((4, 4, na, kw), dt)] * 4
    res = pl.pallas_call(
        body, grid=grid, name=name, in_specs=in_specs, out_specs=out_specs, out_shape=out_shape,
        compiler_params=_cparams(*(("parallel",) * len(grid))),
    )(*args)
    return [res[0].reshape(s, A_WIDTH)] + [a.reshape(s, kw) for a in res[1:]]


DZ_TA = 16


def _dz_assemble(parts_a, parts_c, dyb, zb, cw):
    s = zb.shape[0]
    na = s // N_STRIPS
    nb = na // DZ_TA

    def ahead(w, k):
        return pl.BlockSpec((4, 4, DZ_TA, w), lambda i: (0, 0, jnp.minimum(i + k, nb - 1), 0))

    args, in_specs = [], []
    for dil, (dq, dkp, dkc, dvp, dvc) in zip(DILATIONS + (1,), parts_a + [parts_c]):
        w = dkp.shape[1]
        here = _strip_rows(DZ_TA, w)
        if dil == 1:
            args += [dq, dkp, dkp, dkc, dvp, dvp, dvc]
            in_specs += [_strip_rows(DZ_TA, A_WIDTH), here, ahead(w, 1), here, here, ahead(w, 1), here]
        else:
            k = 8 * dil // DZ_TA
            args += [dq, dkp, dkc, dvp, dvc]
            in_specs += [_strip_rows(DZ_TA, A_WIDTH), ahead(w, k), here, ahead(w, k), here]
    n_att = len(args)
    args = [_strips(a) for a in args] + [_strips(dyb), _strips(dyb), _strips(zb), _strips(zb), _strips(zb), cw]
    in_specs += [_strip_rows(DZ_TA, CONV_CH), _next_rows(DZ_TA, CONV_CH, nb), _strip_rows(DZ_TA, ZB_W),
                 _prev_rows(DZ_TA, ZB_W), _next_rows(DZ_TA, ZB_W, nb), _whole((HALO, CONV_CH))]

    def body(*refs):
        att = list(refs[:n_att])
        dyb_ref, dybn_ref, zb_ref, zbp_ref, zbn_ref, cw_ref, dz_ref, dcw_ref = refs[n_att:]
        i = pl.program_id(0)

        @pl.when(i == 0)
        def _():
            dcw_ref[...] = jnp.zeros_like(dcw_ref)

        def shifted(dil):
            if dil == 1:
                dq_r, kp0, kp1, dkc_r, vp0, vp1, dvc_r = [att.pop(0) for _ in range(7)]
                live = i + 1 < nb
                half = DZ_TA // 2
                dkp = jnp.concatenate([kp0[:, :, half:, :], jnp.where(live, kp1[:, :, :half, :], 0.0)], axis=2)
                dvp = jnp.concatenate([vp0[:, :, half:, :], jnp.where(live, vp1[:, :, :half, :], 0.0)], axis=2)
            else:
                dq_r, dkp_r, dkc_r, dvp_r, dvc_r = [att.pop(0) for _ in range(5)]
                live = i + 8 * dil // DZ_TA < nb
                dkp = jnp.where(live, dkp_r[...].astype(F32), 0.0)
                dvp = jnp.where(live, dvp_r[...].astype(F32), 0.0)
            return dq_r[...].astype(F32), dkc_r[...].astype(F32) + dkp, dvc_r[...].astype(F32) + dvp

        dq, dk, dv = shifted(DILATIONS[0])
        for dil in DILATIONS[1:]:
            dq2, dk2, dv2 = shifted(dil)
            dq, dk, dv = dq + dq2, dk + dk2, dv + dv2
        dz_ref[:, :, :, 0:A_WIDTH] = dq.astype(BF)
        dz_ref[:, :, :, A_WIDTH:2 * A_WIDTH] = dk.astype(BF)
        dz_ref[:, :, :, 2 * A_WIDTH:ZA_W] = dv.astype(BF)
        dq, dk, dv = shifted(1)
        c0 = ZA_W + ZB_W
        dz_ref[:, :, :, c0:c0 + A_WIDTH] = dq.astype(BF)
        dz_ref[:, :, :, c0 + A_WIDTH:c0 + A_WIDTH + C_KV_WIDTH] = dk.astype(BF)
        dz_ref[:, :, :, c0 + A_WIDTH + C_KV_WIDTH:IN_WIDTH] = dv.astype(BF)

        cw = cw_ref[...]
        prev = jnp.where(i > 0, zbp_ref[...], 0.0)
        gb, gc, xb, u, u1, u2, c = _conv_strips(zb_ref[...], prev, cw)
        dyb = dyb_ref[...]
        dc = [_strip(dyb, b) * gb[b] for b in range(N_STRIPS)]
        dcn = jnp.where(i + 1 < nb, dybn_ref[...] * zbn_ref[:, :, :CONV_CH], 0.0)
        wrapped = [_shift_up(dc[0], 1, dcn[0]), _shift_up(dc[1], 1, dcn[1])]
        upd = [jnp.zeros((1, CONV_CH), F32)] * 3
        for b in range(N_STRIPS):
            dc1 = dc[b + 1] if b + 1 < N_STRIPS else wrapped[0]
            dc2 = dc[b + 2] if b + 2 < N_STRIPS else wrapped[b + 2 - N_STRIPS]
            du = cw[2:3, :] * dc[b] + cw[1:2, :] * dc1 + cw[0:1, :] * dc2
            f, e = b % 4, b // 4
            dz_ref[f, e, :, ZA_W:ZA_W + CONV_CH] = (_strip(dyb, b) * c[b]).astype(BF)
            dz_ref[f, e, :, ZA_W + CONV_CH:ZA_W + 2 * CONV_CH] = (du * xb[b]).astype(BF)
            dz_ref[f, e, :, ZA_W + 2 * CONV_CH:c0] = (du * gc[b]).astype(BF)
            for t, uu in enumerate((u2[b], u1[b], u[b])):
                upd[t] = upd[t] + jnp.sum(dc[b] * uu, axis=0, keepdims=True)
        row = lax.broadcasted_iota(jnp.int32, (HALO, CONV_CH), 0)
        tile = jnp.zeros((HALO, CONV_CH), F32)
        for t in range(3):
            tile = jnp.where(row == t, upd[t], tile)
        dcw_ref[...] += tile

    dz, dcw = pl.pallas_call(
        body, grid=(nb,), name="dz_assemble", in_specs=in_specs,
        out_specs=[_strip_rows(DZ_TA, IN_WIDTH), _whole((HALO, CONV_CH))],
        out_shape=[jax.ShapeDtypeStruct((4, 4, na, IN_WIDTH), BF), jax.ShapeDtypeStruct((HALO, CONV_CH), F32)],
        compiler_params=_cparams("arbitrary"),
    )(*args)
    return dz.reshape(s, IN_WIDTH), dcw


def _qkv_bwd(dz, dx1, x, g, w_all, l, tb, tokens_out):
    s, d = x.shape
    na, ta = s // N_STRIPS, tb // N_STRIPS

    def body(dz_ref, dx1_ref, x_ref, g_ref, w_ref, dx_ref, dg_ref):
        i = pl.program_id(0)

        @pl.when(i == 0)
        def _():
            dg_ref[...] = jnp.zeros_like(dg_ref)

        n = IN_WIDTH // N_CHIPS
        dz = dz_ref[...].reshape(tb, IN_WIDTH)
        dh = _nt(dz[:, 0:n], w_ref[0])
        for k in range(1, N_CHIPS):
            dh = dh + _nt(dz[:, k * n:(k + 1) * n], w_ref[k])
        xv = x_ref[...].reshape(tb, d)
        r = _rms_scale(xv)
        xhat = xv * r
        dg_ref[...] += jnp.sum(dh * xhat, axis=0, keepdims=True)
        dx = (dx1_ref[...].reshape(tb, d) + _norm_bwd(dh * g_ref[...], xhat, r)).reshape(4, 4, ta, d)
        if tokens_out:
            for b in range(N_STRIPS):
                dx_ref[:, b, :] = _strip(dx, b)
        else:
            dx_ref[...] = dx

    if tokens_out:
        dx_spec, dx_shape = pl.BlockSpec((ta, N_STRIPS, d), lambda i: (i, 0, 0)), (na, N_STRIPS, d)
    else:
        dx_spec, dx_shape = _strip_rows(ta, d), (4, 4, na, d)
    dx, dg = pl.pallas_call(
        body, grid=(s // tb,), name="qkv_bwd",
        in_specs=[_strip_rows(ta, IN_WIDTH), _strip_rows(ta, d), _strip_rows(ta, d), _whole((1, d)),
                  _layer((N_CHIPS, d, IN_WIDTH // N_CHIPS), l)],
        out_specs=[dx_spec, _whole((HALO, d))],
        out_shape=[jax.ShapeDtypeStruct(dx_shape, F32), jax.ShapeDtypeStruct((HALO, d), F32)],
        compiler_params=_cparams("arbitrary"),
    )(_strips(dz), _strips(dx1), _strips(x), g, w_all)
    return dx.reshape(s, d), dg


def _tile_rows(rows):
    return jnp.pad(rows, ((0, HALO - rows.shape[0]), (0, 0)))


def _to_strips(a, after, name):
    s, d = a.shape
    na = s // N_STRIPS
    ta = min(32, na)

    def body(a_ref, *rest):
        for b in range(N_STRIPS):
            rest[-1][b % 4, b // 4] = a_ref[:, b, :]

    return pl.pallas_call(
        body, grid=(na // ta,), name=name,
        in_specs=[pl.BlockSpec((ta, N_STRIPS, d), lambda i: (i, 0, 0))] + [ANY] * len(after),
        out_specs=_strip_rows(ta, d),
        out_shape=jax.ShapeDtypeStruct((4, 4, na, d), a.dtype), compiler_params=_cparams("parallel"),
    )(a.reshape(na, N_STRIPS, d), *after).reshape(s, d)


def _local_step(x, tgt, fetch, ff, sinks, g_mix, g_group, g_mlp, g_final, emit):
    s, d = x.shape
    depth = g_mix.shape[0]
    tb = min(512, s)
    tf = ff // N_CHIPS
    ts = min(1024, s)
    saved = []
    for l in range(depth):
        w_in, _, _, _, conv_w = fetch(0, l, x)
        cw = _tile_rows(conv_w[l])
        sk = jnp.repeat(sinks[l].reshape(N_HEADS), HEAD_DIM)[None]
        h, za, zb, zc = _qkv_fwd(x, g_mix[l][None], w_in, l, tb)
        parts_a = [_attn_fwd(za, dil, A_WIDTH, 1, 2, 1, A_MAX_DIST, "attn_a_fwd_%d" % dil) for dil in DILATIONS]
        part_c = _attn_fwd(zc, 1, C_KV_WIDTH, 3, 4, C_GROUP, C_MAX_DIST, "attn_c_fwd")
        ya, lse_a, yc, lse_c = _attn_merge(parts_a, part_c, sk, ts)
        w_in, w_o, w1, w2, _ = fetch(1, l, yc)
        x1, yb = _mix_fwd(x, ya, yc, zb, cw, g_group[l][None], w_o, l, ts)
        w_in, w_o, w1, w2, _ = fetch(2, l, x1)
        x2, h2, ap = _mlp_fwd(x1, g_mlp[l][None], w1, w2, l, ts, tf)
        saved.append((x, h, za, zb, zc, ya, lse_a, yc, lse_c, yb, x1, h2, ap, cw, sk))
        x = x2
    dx, loss_tile, dg_final = _loss_head(x, g_final[None], tgt, ts)
    grads = [None] * depth
    tok = jnp.zeros((), F32)
    for l in reversed(range(depth)):
        x0, h, za, zb, zc, ya, lse_a, yc, lse_c, yb, x1, h2, ap, cw, sk = saved[l]
        dx1, dap, dg_mlp = _mlp_bwd(dx, x1, ap, g_mlp[l][None] + tok, w1, w2, l, ts, tf)
        tok = emit(l, 3, _wgrad(ap, dx, min(1024, ff), d, 2 * ts, "wgrad_ff_out", relu2=True))
        tok = tok + emit(l, 2, _wgrad(h2, dap, d, min(1024, ff), 2 * ts, "wgrad_ff_in"))
        n, dya, dyc, dd_a, dd_c, dyb, dg_group, dsink = _mix_bwd(dx1, ya, yb, yc, lse_c, sk, g_group[l][None] + tok,
                                                                 w_o, l, tb)
        tok = emit(l, 1, _wgrad(n, dx1, MIX_WIDTH, d, ts, "wgrad_o"))
        cw = cw + tok
        parts_a = [_attn_bwd(za, dya, lse_a, dd_a, dil, A_WIDTH, 1, 2, 1, A_MAX_DIST, "attn_a_bwd_%d" % dil)
                   for dil in DILATIONS]
        parts_c = _attn_bwd(zc, dyc, lse_c, dd_c, 1, C_KV_WIDTH, 3, 4, C_GROUP, C_MAX_DIST, "attn_c_bwd")
        dz, dcw = _dz_assemble(parts_a, parts_c, dyb, zb, cw)
        tok = emit(l, 0, _wgrad(h, dz, d, IN_WIDTH // 4, 2 * ts, "wgrad_in"))
        dx, dg_mix = _qkv_bwd(dz, dx1, x0, g_mix[l][None] + tok, w_in, l, tb, l == 0)
        grads[l] = (dcw, dsink, dg_mix, dg_group, dg_mlp)
    return loss_tile, dx, grads, dg_final


ANY = pl.BlockSpec(memory_space=pl.ANY)
SHARD_AXES = (2, 1, 2, 1)
N_BIG = len(SHARD_AXES)
N_CHIPS = 4
N_DEV = 8


def _mesh_pos():
    return lax.axis_index("x"), lax.axis_index("y"), lax.axis_index("c")


def _flip(v, bit):
    return 1 - v if bit else v


def _place_shard(shard, chip_arr, name):
    _, rows, cols = shard.shape
    tr = min(256, rows)

    def body(chip_ref, x_ref, o_ref):
        o_ref[...] = x_ref[...].astype(BF)

    return pl.pallas_call(
        body, name=name,
        grid_spec=pltpu.PrefetchScalarGridSpec(
            num_scalar_prefetch=1, grid=(2, rows // tr),
            in_specs=[pl.BlockSpec((None, tr, cols), lambda l, i, chip: (l, i, 0))],
            out_specs=pl.BlockSpec((None, None, tr, cols), lambda l, i, chip: (l, chip[0], i, 0))),
        out_shape=jax.ShapeDtypeStruct((2, N_CHIPS, rows, cols), BF),
        compiler_params=_cparams("parallel", "parallel"),
    )(chip_arr, shard)


HBM = pl.BlockSpec(memory_space=pltpu.HBM)
SEM = pl.BlockSpec(memory_space=pltpu.SEMAPHORE)
EFFECT = pltpu.SideEffectType.DATAFLOW_SIDE_EFFECTING

GATHER_GROUPS = (((0, 0),), ((1, 0),), ((2, 0), (3, 0)), ((0, 1),), ((1, 1), (2, 1), (3, 1)))
GATHER_STARTS = ((0,), (1, 2), (3, 4))
GATHER_STAGES = {(0, 0): 0, (1, 0): 1, (2, 0): 2, (0, 1): 3, (1, 1): 4}


def _gather_copies(arrs, group, send_sems, recv_sems):
    x, y, c = _mesh_pos()
    me = 2 * x + y
    out = []
    for i, (w, layer) in enumerate(group):
        mine = arrs[w].at[layer, me]
        for j, (qx, qy) in enumerate([(1 - x, y), (x, 1 - y), (1 - x, 1 - y)]):
            landed = arrs[w].at[layer, 2 * qx + qy]
            out.append(tuple(pltpu.make_async_remote_copy(
                src_ref=piece, dst_ref=piece, send_sem=send_sems.at[i * 3 + j], recv_sem=recv_sems.at[i * 3 + j],
                device_id=(qx, qy, c), device_id_type=MESH) for piece in (mine, landed)))
    return out


def _conv_copies(conv_src, conv_dst, send_sems, recv_sems):
    x, y, c = _mesh_pos()
    out = []
    for j, (qx, qy) in enumerate([(1 - x, y), (x, 1 - y), (1 - x, 1 - y)]):
        out.append(tuple(pltpu.make_async_remote_copy(
            src_ref=conv_src, dst_ref=conv_dst.at[q], send_sem=send_sems.at[j], recv_sem=recv_sems.at[j],
            device_id=(qx, qy, c), device_id_type=MESH) for q in (2 * x + y, 2 * qx + qy)))
    return out


def _gather_start(groups, arrs, conv, name, through=None):
    n_sems = 2 * (len(groups) + (conv is not None))
    mats = sorted({w for g in groups for w, _ in GATHER_GROUPS[g]})

    def body(*refs):
        arrs_ref = [None] * N_BIG
        for w, ref in zip(mats, refs):
            arrs_ref[w] = ref
        sems = refs[n_in:n_in + n_sems]
        if conv is not None:
            for cp, _ in _conv_copies(refs[len(mats)], refs[len(mats) + 1], sems[-2], sems[-1]):
                cp.start()
        for k, g in enumerate(groups):
            for cp, _ in _gather_copies(arrs_ref, GATHER_GROUPS[g], sems[2 * k], sems[2 * k + 1]):
                cp.start()

    sem_shapes = []
    for n in [len(GATHER_GROUPS[g]) for g in groups] + ([1] if conv is not None else []):
        sem_shapes += [pltpu.SemaphoreType.DMA((3 * n,))] * 2
    operands = [arrs[w] for w in mats] + ([] if conv is None else list(conv)) + ([] if through is None else [through])
    n_in = len(operands)
    res = pl.pallas_call(
        body, name=name,
        out_shape=tuple(sem_shapes) + tuple(pltpu.HBM(a.shape, a.dtype) for a in operands),
        in_specs=(HBM,) * n_in, out_specs=(SEM,) * n_sems + (HBM,) * n_in,
        input_output_aliases={i: n_sems + i for i in range(n_in)},
        compiler_params=pltpu.CompilerParams(has_side_effects=EFFECT),
    )(*[pltpu.with_memory_space_constraint(a, pltpu.HBM) for a in operands])
    arrs = list(arrs)
    for w, a in zip(mats, res[n_sems:]):
        arrs[w] = a
    return res[:n_sems], arrs, list(res[n_sems + len(mats):])


def _gather_wait(k, sems, arrs, conv, after, name):
    group = GATHER_GROUPS[k]
    mats = sorted({w for w, _ in group})
    n_conv = 0 if conv is None else 2

    def body(*refs):
        local = refs[:len(mats)]
        arrs_ref = [None] * N_BIG
        for w, ref in zip(mats, local):
            arrs_ref[w] = ref
        pos = len(mats) + n_conv
        copies = _gather_copies(arrs_ref, group, refs[pos], refs[pos + 1])
        if conv is not None:
            copies += _conv_copies(refs[len(mats)], refs[len(mats) + 1], refs[pos + 2], refs[pos + 3])
        for send, recv in copies:
            recv.wait_recv()
            send.wait_send()

    operands = [arrs[w] for w in mats] + ([] if conv is None else [conv[1], conv[2]])
    sem_ops = list(sems) + ([] if conv is None else list(conv[0]))
    n_op = len(operands)
    res = pl.pallas_call(
        body, name=name, out_shape=tuple(pltpu.HBM(a.shape, a.dtype) for a in operands),
        in_specs=(HBM,) * n_op + (SEM,) * len(sem_ops) + (ANY,) * len(after), out_specs=(HBM,) * n_op,
        input_output_aliases={i: i for i in range(n_op)},
        compiler_params=pltpu.CompilerParams(has_side_effects=EFFECT),
    )(*operands, *sem_ops, *after)
    arrs = list(arrs)
    for w, a in zip(mats, res):
        arrs[w] = a
    return arrs, (res[-1] if conv is not None else None)


def _grad_shard(ref, w, chip, n):
    start = pl.multiple_of(chip * n, 128)
    if SHARD_AXES[w] == 2:
        return ref.at[:, pl.ds(start, n)]
    return ref.at[pl.ds(start, n), :]


def _slot_shape(g, w):
    shape = list(g.shape)
    shape[SHARD_AXES[w] - 1] //= N_CHIPS
    return (N_DEV - 1,) + tuple(shape)


def _scatter_copies(g_ref, land_ref, send_sems, recv_sems, layer, w):
    x, y, c = _mesh_pos()
    n = g_ref.shape[SHARD_AXES[w] - 1] // N_CHIPS
    out = []
    for r in range(1, N_DEV):
        tx, ty, tc = _flip(x, r & 4), _flip(y, r & 2), _flip(c, r & 1)
        cp = pltpu.make_async_remote_copy(
            src_ref=_grad_shard(g_ref, w, 2 * tx + ty, n), dst_ref=land_ref.at[r - 1], send_sem=send_sems.at[r - 1],
            recv_sem=recv_sems.at[r - 1], device_id=(tx, ty, tc), device_id_type=MESH)
        out.append((cp, (c != layer) if r & 1 else (c == layer)))
    return out


def _scatter_start(items, layer, name):
    n = len(items)

    def body(*refs):
        for i, (w, _, _) in enumerate(items):
            g_ref, land_ref = refs[2 * i], refs[2 * i + 1]
            send_sems, recv_sems = refs[2 * n + 2 * i], refs[2 * n + 2 * i + 1]
            for cp, mine in _scatter_copies(g_ref, land_ref, send_sems, recv_sems, layer, w):
                @pl.when(mine)
                def _():
                    cp.start()
        refs[-1][...] = jnp.zeros_like(refs[-1])

    operands = [a for _, g, land in items for a in (g, land)]
    res = pl.pallas_call(
        body, name=name,
        out_shape=(pltpu.SemaphoreType.DMA((N_DEV - 1,)),) * (2 * n)
        + tuple(pltpu.HBM(a.shape, a.dtype) for a in operands) + (jax.ShapeDtypeStruct((HALO, 128), F32),),
        in_specs=(HBM,) * (2 * n),
        out_specs=(SEM,) * (2 * n) + (HBM,) * (2 * n) + (pl.BlockSpec(memory_space=pltpu.VMEM),),
        input_output_aliases={i: 2 * n + i for i in range(2 * n)},
        compiler_params=pltpu.CompilerParams(has_side_effects=EFFECT),
    )(*[pltpu.with_memory_space_constraint(a, pltpu.HBM) for a in operands])
    return [(res[2 * i], res[2 * i + 1], res[2 * n + 2 * i], res[2 * n + 2 * i + 1]) for i in range(n)], res[-1]


def _scatter_wait(started, land, after, w, name):
    def body(g0_ref, g1_ref, land_ref, ss0, rs0, ss1, rs1, after_ref, g0_out, g1_out, land_out):
        c = lax.axis_index("c")
        for layer, g_ref, ss, rs in ((0, g0_ref, ss0, rs0), (1, g1_ref, ss1, rs1)):
            for cp, mine in _scatter_copies(g_ref, land_ref, ss, rs, layer, w):
                @pl.when(mine)
                def _():
                    cp.wait_send()

                @pl.when(c == layer)
                def _():
                    cp.wait_recv()

    (ss0, rs0, g0), (ss1, rs1, g1) = started
    return pl.pallas_call(
        body, name=name,
        out_shape=(pltpu.HBM(g0.shape, g0.dtype), pltpu.HBM(g1.shape, g1.dtype), pltpu.HBM(land.shape, land.dtype)),
        in_specs=(HBM, HBM, HBM, SEM, SEM, SEM, SEM, ANY), out_specs=(HBM, HBM, HBM),
        input_output_aliases={0: 0, 1: 1, 2: 2}, compiler_params=pltpu.CompilerParams(has_side_effects=EFFECT),
    )(g0, g1, land, ss0, rs0, ss1, rs1, after)


def _sum_slots(g0, g1, slots, w, pos_arr, name):
    _, rows, cols = slots.shape
    tr = min(512, rows)
    nr = rows // tr
    if SHARD_AXES[w] == 2:
        own = pl.BlockSpec((tr, cols), lambda i, pos: (i, pos[0]))
    else:
        own = pl.BlockSpec((tr, cols), lambda i, pos: (pos[0] * nr + i, 0))

    def body(pos_ref, own0_ref, own1_ref, s_ref, o_ref):
        acc = jnp.where(pos_ref[1] == 0, own0_ref[...], own1_ref[...]).astype(F32)
        for r in range(N_DEV - 1):
            acc = acc + s_ref[r].astype(F32)
        o_ref[...] = acc

    return pl.pallas_call(
        body, name=name,
        grid_spec=pltpu.PrefetchScalarGridSpec(
            num_scalar_prefetch=1, grid=(nr,),
            in_specs=[own, own, pl.BlockSpec((N_DEV - 1, tr, cols), lambda i, pos: (0, i, 0))],
            out_specs=pl.BlockSpec((tr, cols), lambda i, pos: (i, 0))),
        out_shape=jax.ShapeDtypeStruct((rows, cols), F32), compiler_params=_cparams("parallel"),
    )(pos_arr, g0, g1, slots)


def _swap_copies(refs, n):
    x, y, c = _mesh_pos()
    return [pltpu.make_async_remote_copy(src_ref=refs[w], dst_ref=refs[n + w], send_sem=refs[2 * n].at[w],
                                         recv_sem=refs[2 * n + 1].at[w], device_id=(x, y, 1 - c), device_id_type=MESH)
            for w in range(n)]


def _swap_start(halves, name):
    n = len(halves)

    def body(*refs):
        for cp in _swap_copies(refs, n):
            cp.start()

    operands = list(halves) + [lax.empty(h.shape, h.dtype) for h in halves]
    res = pl.pallas_call(
        body, name=name,
        out_shape=(pltpu.SemaphoreType.DMA((n,)),) * 2 + tuple(pltpu.HBM(a.shape, a.dtype) for a in operands),
        in_specs=(HBM,) * (2 * n), out_specs=(SEM,) * 2 + (HBM,) * (2 * n),
        input_output_aliases={i: 2 + i for i in range(2 * n)},
        compiler_params=pltpu.CompilerParams(has_side_effects=EFFECT),
    )(*[pltpu.with_memory_space_constraint(a, pltpu.HBM) for a in operands])
    return res[0], res[1], list(res[2:2 + n]), list(res[2 + n:])


def _swap_wait(send_sems, recv_sems, halves, lands, after, name):
    n = len(halves)

    def body(*refs):
        for cp in _swap_copies(refs, n):
            cp.wait_send()
            cp.wait_recv()

    operands = list(halves) + list(lands)
    res = pl.pallas_call(
        body, name=name, out_shape=tuple(pltpu.HBM(a.shape, a.dtype) for a in operands),
        in_specs=(HBM,) * (2 * n) + (SEM, SEM, ANY), out_specs=(HBM,) * (2 * n),
        input_output_aliases={i: i for i in range(2 * n)},
        compiler_params=pltpu.CompilerParams(has_side_effects=EFFECT),
    )(*operands, send_sems, recv_sems, after)
    return list(res[n:])


def _adamw_math(w, g, m, v):
    m = ADAM_B1 * m + (1.0 - ADAM_B1) * g
    v = ADAM_B2 * v + (1.0 - ADAM_B2) * jnp.square(g)
    m_hat = m / (1.0 - ADAM_B1 ** ADAM_STEP)
    v_hat = v / (1.0 - ADAM_B2 ** ADAM_STEP)
    delta = -ADAM_LR * (m_hat / (jnp.sqrt(v_hat) + ADAM_EPS) + ADAM_WD * w)
    return delta, m, v


def _adamw(w, g, m, v, filled, pos_arr, name):
    shape = w.shape
    _, rows, cols = shape
    tr = min(256, rows)

    def body(pos_ref, w_ref, g_ref, m_ref, v_ref, *rest):
        go_ref, d_ref, m2_ref, v2_ref = rest[-4:]
        g = g_ref[...]
        go_ref[...] = g
        d_ref[...], m2_ref[...], v2_ref[...] = _adamw_math(w_ref[...], g, m_ref[...], v_ref[...])

    def layer(pos):
        return pos[1] if filled is None else 1 - pos[1]

    full = pl.BlockSpec((None, tr, cols), lambda i, pos: (layer(pos), i, 0))
    half = pl.BlockSpec((tr, cols), lambda i, pos: (i, 0))
    n_in = 5
    return pl.pallas_call(
        body, name=name,
        grid_spec=pltpu.PrefetchScalarGridSpec(
            num_scalar_prefetch=1, grid=(rows // tr,),
            in_specs=[full, half, full, full] + ([] if filled is None else [ANY] * 4), out_specs=[full] * 4),
        out_shape=[jax.ShapeDtypeStruct(shape, F32)] * 4,
        input_output_aliases={} if filled is None else {n_in + k: k for k in range(4)},
        compiler_params=_cparams("parallel"),
    )(pos_arr, w, g, m, v, *([] if filled is None else filled))


def _small_sync(part, w, m, v):
    rows, cols = part.shape

    def body(p_ref, w_ref, m_ref, v_ref, g_ref, d_ref, m2_ref, v2_ref, slots, send_sems, recv_sems):
        x, y, c = _mesh_pos()
        me = 4 * x + 2 * y + c
        slots[me] = p_ref[...]
        sends = []
        for r in range(1, N_DEV):
            to = (_flip(x, r & 4), _flip(y, r & 2), _flip(c, r & 1))
            sends.append(pltpu.make_async_remote_copy(
                src_ref=p_ref, dst_ref=slots.at[me], send_sem=send_sems.at[r - 1], recv_sem=recv_sems.at[r - 1],
                device_id=to, device_id_type=MESH))
        for cp in sends:
            cp.start()
        for cp in sends:
            cp.wait_recv()
        for cp in sends:
            cp.wait_send()
        g = slots[0]
        for i in range(1, N_DEV):
            g = g + slots[i]
        g_ref[...] = g
        d_ref[...], m2_ref[...], v2_ref[...] = _adamw_math(w_ref[...], g, m_ref[...], v_ref[...])

    vm = pl.BlockSpec(memory_space=pltpu.VMEM)
    return pl.pallas_call(
        body, name="small_sync", in_specs=[vm] * 4, out_specs=[vm] * 4,
        out_shape=[jax.ShapeDtypeStruct((rows, cols), F32)] * 4,
        scratch_shapes=[pltpu.VMEM((N_DEV, rows, cols), F32), pltpu.SemaphoreType.DMA((N_DEV - 1,)),
                        pltpu.SemaphoreType.DMA((N_DEV - 1,))],
    )(part, w, m, v)


PACK_W = 256


def _pack_rows(n):
    return -(-n // (HALO * PACK_W)) * HALO


def _pack_small(parts):
    out = []
    for a in parts:
        flat = a.reshape(-1)
        out.append(jnp.pad(flat, (0, _pack_rows(flat.size) * PACK_W - flat.size)).reshape(-1, PACK_W))
    return jnp.concatenate(out, axis=0)


def _unpack_small(p, shapes):
    out, row = [], 0
    for shape in shapes:
        n = 1
        for k in shape:
            n *= k
        out.append(p[row:row + _pack_rows(n)].reshape(-1)[:n].reshape(shape))
        row += _pack_rows(n)
    return out


def kernel(x, w_in, conv_w, sinks, g_mix, g_group, w_o, g_mlp, w_ff_in, w_ff_out, g_final, loss_target, m_w_in, m_conv_w, m_sinks, m_g_mix, m_g_group, m_w_o, m_g_mlp, m_w_ff_in, m_w_ff_out, m_g_final, v_w_in, v_conv_w, v_sinks, v_g_mix, v_g_group, v_w_o, v_g_mlp, v_w_ff_in, v_w_ff_out, v_g_final):
    chip = 2 * lax.axis_index("x") + lax.axis_index("y")
    conv_n = conv_w.shape[2]

    pos_arr = jnp.stack([chip, lax.axis_index("c")]).astype(jnp.int32)
    shards = (w_in, w_o, w_ff_in, w_ff_out)
    conv_tile = jnp.pad(conv_w.reshape(6, conv_n), ((0, HALO - 6), (0, 128 - conv_n)))
    placed = [_place_shard(w_in, pos_arr[:1], "place_shard_0"), None, None, None]
    sems_a, placed, conv_thru = _gather_start(
        GATHER_STARTS[0], placed, (conv_tile, lax.empty((N_CHIPS,) + conv_tile.shape, conv_tile.dtype)),
        "gather_start_0")
    for i in range(1, N_BIG):
        placed[i] = _place_shard(shards[i], pos_arr[:1], "place_shard_%d" % i)
    full = {"arrs": placed, "conv": None, "sems": list(sems_a[:2])}
    target = _to_strips(loss_target[0], placed[:1], "to_strips_target")

    def fetch(stage, layer, after):
        k = GATHER_STAGES.get((stage, layer))
        if k is None:
            return (*full["arrs"], full["conv"])
        sems = full["sems"][2 * k:2 * k + 2]
        if k == 0:
            full["arrs"], land = _gather_wait(0, sems, full["arrs"], (sems_a[-2:], *conv_thru), (after, target),
                                              "gather_wait_0")
            conv_all = lax.dynamic_update_slice(land, conv_tile[None], (chip, 0, 0))
            full["conv"] = conv_all[:, :6, :conv_n].reshape(N_CHIPS, 2, 3, conv_n).transpose(1, 2, 0, 3).reshape(
                2, 3, CONV_CH)
            sems_b, full["arrs"], rest = _gather_start(GATHER_STARTS[1], full["arrs"], None, "gather_start_1",
                                                       through=full["arrs"][0])
            full["arrs"][0] = rest[-1]
            full["sems"] += list(sems_b)
        else:
            full["arrs"], _ = _gather_wait(k, sems, full["arrs"], None, (after,), "gather_wait_%d" % k)
        if k == 2:
            sems_c, full["arrs"], _ = _gather_start(GATHER_STARTS[2], full["arrs"], None, "gather_start_2")
            full["sems"] += list(sems_c)
        return (*full["arrs"], full["conv"])

    lands, started, pending = [None] * N_BIG, {}, []

    def emit(layer, w, g):
        if lands[w] is None:
            lands[w] = lax.empty(_slot_shape(g, w), g.dtype)
        pending.append((w, g, lands[w]))
        if not (w == 0 or (layer == 0 and w == 1)):
            return jnp.zeros((), F32)
        name = "scatter_start_%d_%d" % (layer, len(pending))
        done, token = _scatter_start(list(pending), layer, name)
        for (w_i, _, _), (ss, rs, g_thru, land) in zip(pending, done):
            started[layer, w_i], lands[w_i] = (ss, rs, g_thru), land
        pending.clear()
        return token[0, 0]

    loss_tile, dx, grads, dg_final = _local_step(_to_strips(x[0], placed, "to_strips_x"), target, fetch,
                                                 w_ff_in.shape[2] * N_CHIPS,
                                                 sinks, g_mix, g_group, g_mlp, g_final, emit)

    wmv = ((w_in, m_w_in, v_w_in), (w_o, m_w_o, v_w_o), (w_ff_in, m_w_ff_in, v_w_ff_in),
           (w_ff_out, m_w_ff_out, v_w_ff_out))
    big, after = [None] * N_BIG, dx
    for name, ws in (("swap_rest", (1, 2, 3)), ("swap_in", (0,))):
        own = []
        for w in ws:
            g0, g1, slots = _scatter_wait((started[0, w], started[1, w]), lands[w], after, w, "scatter_wait_%d" % w)
            own.append(_sum_slots(g0, g1, slots, w, pos_arr, "sum_slots_%d" % w))
        send_sems, recv_sems, own, zones = _swap_start(own, name + "_start")
        for w, g in zip(ws, own):
            big[w] = _adamw(wmv[w][0], g, wmv[w][1], wmv[w][2], None, pos_arr, "adamw_own_%d" % w)
        theirs = _swap_wait(send_sems, recv_sems, own, zones, big[ws[-1]][1], name + "_wait")
        for w, g in zip(ws, theirs):
            big[w] = _adamw(wmv[w][0], g, wmv[w][1], wmv[w][2], big[w], pos_arr, "adamw_other_%d" % w)
        after = big[ws[-1]][1]

    def both(i):
        return jnp.stack([grads[0][i][0], grads[1][i][0]])
    dconv = jnp.stack([grads[0][0][:3], grads[1][0][:3]])
    dsinks = jnp.stack([grads[0][1][0, ::HEAD_DIM], grads[1][1][0, ::HEAD_DIM]])
    part = _pack_small([both(2), both(3), both(4), dg_final[0], dconv, dsinks, loss_tile[0, 0]])

    def spread(shard):
        return lax.dynamic_update_slice(jnp.zeros((2, 3, CONV_CH), F32), shard, (0, 0, chip * conv_n))
    zero = jnp.zeros((), F32)
    packs = [_pack_small([a, b, c_, e, spread(f), g_, zero]) for a, b, c_, e, f, g_ in (
        (g_mix, g_group, g_mlp, g_final, conv_w, sinks),
        (m_g_mix, m_g_group, m_g_mlp, m_g_final, m_conv_w, m_sinks),
        (v_g_mix, v_g_group, v_g_mlp, v_g_final, v_conv_w, v_sinks))]
    shapes = [g_mix.shape, g_group.shape, g_mlp.shape, g_final.shape, (2, 3, CONV_CH), sinks.shape, ()]
    small = [_unpack_small(p, shapes) for p in _small_sync(part, *packs)]

    def shard_of(full):
        return lax.dynamic_slice(full, (0, 0, chip * conv_n), (2, 3, conv_n))
    small = [(s[0], s[1], s[2], s[3], shard_of(s[4]), s[5], s[6]) for s in small]
    loss = small[0][6]

    def ordered(kind):
        b = [big[i][kind] for i in range(N_BIG)]
        s = small[kind]
        return [b[0], s[4], s[5], s[0], s[1], b[1], s[2], b[2], b[3], s[3]]

    return (loss, dx[None], *ordered(0), *ordered(1), *ordered(2), *ordered(3))
```

```python
import functools

import jax
import jax.numpy as jnp
from jax import lax
from jax.experimental import pallas as pl
from jax.experimental.pallas import tpu as pltpu

HEAD_DIM = 64
N_HEADS = 6
C_GROUP = 3
A_WIDTH = N_HEADS * HEAD_DIM
C_KV_WIDTH = 2 * HEAD_DIM
CONV_CH = 256
ZA_W = 3 * A_WIDTH
ZB_W = 3 * CONV_CH
ZC_W = A_WIDTH + 2 * C_KV_WIDTH
IN_WIDTH = ZA_W + ZB_W + ZC_W
MIX_WIDTH = A_WIDTH + CONV_CH + A_WIDTH
DILATIONS = (1, 4, 16)
A_MAX_DIST = 128
C_MAX_DIST = 127
TQ = 128
EPS = 1e-6
SCALE = HEAD_DIM ** -0.5
NEG = -1e30
HALO = 8

ADAM_LR = 0.001
ADAM_B1 = 0.9
ADAM_B2 = 0.999
ADAM_EPS = 1e-08
ADAM_WD = 0.01
ADAM_STEP = 10

BF = jnp.bfloat16
F32 = jnp.float32
MESH = pl.DeviceIdType.MESH
VMEM_LIMIT = 56 * 1024 * 1024


def _cparams(*sem):
    return pltpu.CompilerParams(dimension_semantics=sem, vmem_limit_bytes=VMEM_LIMIT)


def _nt(a, b):
    return lax.dot_general(a, b, (((1,), (1,)), ((), ())), preferred_element_type=F32)


def _tn(a, b):
    return lax.dot_general(a, b, (((0,), (0,)), ((), ())), preferred_element_type=F32)


def _nn(a, b):
    return jnp.dot(a, b, preferred_element_type=F32)


def _rows(tb, w):
    return pl.BlockSpec((tb, w), lambda i: (i, 0))


def _whole(shape):
    return pl.BlockSpec(shape, lambda *_: (0,) * len(shape))


def _layer(shape, l):
    return pl.BlockSpec((None,) + shape, lambda *_: (l,) + (0,) * len(shape))


def _rms_scale(v):
    return lax.rsqrt(jnp.mean(v * v, axis=-1, keepdims=True) + EPS)


def _norm_bwd(dxhat, xhat, r):
    return r * (dxhat - xhat * jnp.mean(dxhat * xhat, axis=-1, keepdims=True))


def _qkv_fwd(x, g, w_all, l, tb):
    s, d = x.shape

    def body(x_ref, g_ref, w_ref, h_ref, za_ref, zb_ref, zc_ref):
        xv = x_ref[...]
        h = ((xv * _rms_scale(xv)) * g_ref[...]).astype(BF)
        h_ref[...] = h
        z = jnp.concatenate([_nn(h, w_ref[k]) for k in range(N_CHIPS)], axis=1)
        za_ref[...] = z[:, :ZA_W]
        zb_ref[...] = z[:, ZA_W:ZA_W + ZB_W]
        zc_ref[...] = z[:, ZA_W + ZB_W:]

    return pl.pallas_call(
        body, grid=(s // tb,), name="qkv_fwd",
        in_specs=[_rows(tb, d), _whole((1, d)), _layer((N_CHIPS, d, IN_WIDTH // N_CHIPS), l)],
        out_specs=[_rows(tb, d), _rows(tb, ZA_W), _rows(tb, ZB_W), _rows(tb, ZC_W)],
        out_shape=[jax.ShapeDtypeStruct((s, d), BF), jax.ShapeDtypeStruct((s, ZA_W), F32),
                   jax.ShapeDtypeStruct((s, ZB_W), F32), jax.ShapeDtypeStruct((s, ZC_W), F32)],
        compiler_params=_cparams("parallel"),
    )(x, g, w_all)


N_STRIPS = 16


def _strips(a):
    s, w = a.shape
    return a.reshape(4, 4, s // N_STRIPS, w)


P_ROWS = {16: TQ, 4: 32, 1: 8}


def _p_sub(s, dil, most):
    while (s // dil // TQ) % most:
        most //= 2
    return most


def _p_grid(s, dil, n_sub):
    nb = s // dil // TQ // n_sub
    return {16: (4, 4, nb), 4: (4, nb), 1: (nb,)}[dil]


def _p_spec(dil, cw, col, n_sub, prev=False):
    rows = P_ROWS[dil] * (1 if prev else n_sub)

    def blk(j):
        return jnp.maximum(n_sub * j - 1, 0) if prev else j
    if dil == 16:
        return pl.BlockSpec((None, None, rows, cw), lambda f, e, j: (f, e, blk(j), col))
    if dil == 4:
        return pl.BlockSpec((None, 4, rows, cw), lambda f, j: (f, 0, blk(j), col))
    return pl.BlockSpec((4, 4, rows, cw), lambda j: (0, 0, blk(j), col))


def _block_pos(i, dil):
    if dil == 16:
        return i
    if dil == 4:
        return 4 * (i % 32) + i // 32
    return 16 * (i % 8) + 4 * ((i // 8) % 4) + i // 32


def _band_mask(b, dil, max_dist):
    qi = _block_pos(lax.broadcasted_iota(jnp.int32, (TQ, 2 * TQ), 0), dil)
    col = lax.broadcasted_iota(jnp.int32, (TQ, 2 * TQ), 1)
    cur = col >= TQ
    dist = qi - _block_pos(col % TQ, dil) + jnp.where(cur, 0, TQ)
    return (dist >= 0) & (dist <= max_dist) & (cur | (b > 0))


def _hs(h):
    return slice(h * HEAD_DIM, (h + 1) * HEAD_DIM)


def _ld(ref, cols, rows=slice(None)):
    v = ref[..., rows, cols]
    return v.reshape(TQ, v.shape[-1])


def _st(ref, cols, val, rows=slice(None)):
    lead = ref.shape[:-2] + (ref.shape[-2] if rows == slice(None) else rows.stop - rows.start,)
    ref[..., rows, cols] = val.reshape(lead + (val.shape[-1],))


def _attn_fwd(z, dil, kw, kcol, vcol, n_rep, max_dist, name):
    s, zw = z.shape
    n_sub = _p_sub(s, dil, 2)
    grid = _p_grid(s, dil, n_sub)

    def body(q_ref, kp_ref, kc_ref, vp_ref, vc_ref, o_ref, lse_ref):
        for t in range(n_sub):
            rows = slice(t * P_ROWS[dil], (t + 1) * P_ROWS[dil])
            before = (slice(None),) if t == 0 else (slice((t - 1) * P_ROWS[dil], t * P_ROWS[dil]),)
            kb_ref, vb_ref = (kp_ref, vp_ref) if t == 0 else (kc_ref, vc_ref)
            mask = _band_mask(n_sub * pl.program_id(len(grid) - 1) if t == 0 else 1, dil, max_dist)
            scs, v2s = [], []
            for kh in range(N_HEADS // n_rep):
                k2 = jnp.concatenate([_ld(kb_ref, _hs(kh), *before), _ld(kc_ref, _hs(kh), rows)], axis=0).astype(BF)
                v2s.append(jnp.concatenate([_ld(vb_ref, _hs(kh), *before), _ld(vc_ref, _hs(kh), rows)],
                                           axis=0).astype(BF))
                for h in range(kh * n_rep, (kh + 1) * n_rep):
                    q = (_ld(q_ref, _hs(h), rows) * SCALE).astype(BF)
                    scs.append(jnp.where(mask, _nt(q, k2), NEG))
            for h, sc in enumerate(scs):
                m = jnp.max(sc, axis=1, keepdims=True)
                p = jnp.exp(sc - m)
                l = jnp.sum(p, axis=1, keepdims=True)
                _st(o_ref, _hs(h), _nn(p.astype(BF), v2s[h // n_rep]) / l, rows)
                _st(lse_ref, _hs(h), jnp.broadcast_to(m + jnp.log(l), (TQ, HEAD_DIM)), rows)

    res = pl.pallas_call(
        body, grid=grid, name=name,
        in_specs=[_p_spec(dil, A_WIDTH, 0, n_sub), _p_spec(dil, kw, kcol, n_sub, True), _p_spec(dil, kw, kcol, n_sub),
                  _p_spec(dil, kw, vcol, n_sub, True), _p_spec(dil, kw, vcol, n_sub)],
        out_specs=[_p_spec(dil, A_WIDTH, 0, n_sub)] * 2,
        out_shape=[jax.ShapeDtypeStruct((4, 4, s // N_STRIPS, A_WIDTH), F32)] * 2,
        compiler_params=_cparams(*(("parallel",) * len(grid))),
    )(*[_strips(z)] * 5)
    return [a.reshape(s, A_WIDTH) for a in res]


def _attn_merge(parts_a, part_c, sink_row, tb):
    s = part_c[0].shape[0]
    n_a = len(parts_a)

    def body(*refs):
        ins, sink_ref = refs[:2 * n_a + 2], refs[2 * n_a + 2]
        ya_ref, lsea_ref, yc_ref, lsec_ref = refs[2 * n_a + 3:]
        lses = [ins[2 * p + 1][...] for p in range(n_a)]
        m = functools.reduce(jnp.maximum, lses)
        ws = [jnp.exp(v - m) for v in lses]
        l = functools.reduce(jnp.add, ws)
        ya_ref[...] = functools.reduce(jnp.add, [w * ins[2 * p][...] for p, w in enumerate(ws)]) / l
        lsea_ref[...] = m + jnp.log(l)
        o_c, lse_c = [r[...] for r in ins[2 * n_a:]]
        sk = sink_ref[...]
        m2 = jnp.maximum(lse_c, sk)
        w = jnp.exp(lse_c - m2)
        l2 = w + jnp.exp(sk - m2)
        yc_ref[...] = o_c * (w / l2)
        lsec_ref[...] = m2 + jnp.log(l2)

    return pl.pallas_call(
        body, grid=(s // tb,), name="attn_merge",
        in_specs=[_rows(tb, A_WIDTH)] * (2 * n_a + 2) + [_whole((1, A_WIDTH))],
        out_specs=[_rows(tb, A_WIDTH)] * 4, out_shape=[jax.ShapeDtypeStruct((s, A_WIDTH), F32)] * 4,
        compiler_params=_cparams("parallel"),
    )(*[a for part in parts_a + [part_c] for a in part], sink_row)


def _shift_down(v, n, halo):
    rows = v.shape[0]
    out = pltpu.roll(v, n, 0)
    row = lax.broadcasted_iota(jnp.int32, v.shape, 0)
    for t in range(n):
        out = jnp.where(row == t, halo[HALO - n + t:HALO - n + t + 1, :], out)
    return out


def _shift_up(v, n, halo):
    rows = v.shape[0]
    out = pltpu.roll(v, rows - n, 0)
    row = lax.broadcasted_iota(jnp.int32, v.shape, 0)
    for t in range(n):
        out = jnp.where(row == rows - n + t, halo[t:t + 1, :], out)
    return out


def _strip(v, b):
    return v[b % 4, b // 4]


def _conv_strips(zb, prev, cw):
    gb = [_strip(zb, b)[:, :CONV_CH] for b in range(N_STRIPS)]
    gc = [_strip(zb, b)[:, CONV_CH:2 * CONV_CH] for b in range(N_STRIPS)]
    xb = [_strip(zb, b)[:, 2 * CONV_CH:] for b in range(N_STRIPS)]
    u = [g * v for g, v in zip(gc, xb)]
    uh = prev[:, :, CONV_CH:2 * CONV_CH] * prev[:, :, 2 * CONV_CH:]
    wrapped = {14: _shift_down(u[14], 1, uh[2]), 15: _shift_down(u[15], 1, uh[3])}
    u1 = [u[b - 1] if b >= 1 else wrapped[15] for b in range(N_STRIPS)]
    u2 = [u[b - 2] if b >= 2 else wrapped[14 + b] for b in range(N_STRIPS)]
    c = [cw[0:1, :] * u2[b] + cw[1:2, :] * u1[b] + cw[2:3, :] * u[b] for b in range(N_STRIPS)]
    return gb, gc, xb, u, u1, u2, c


def _strip_rows(ta, w):
    return pl.BlockSpec((4, 4, ta, w), lambda i: (0, 0, i, 0))


def _prev_rows(ta, w):
    return pl.BlockSpec((4, None, HALO, w), lambda i: (0, 3, jnp.maximum(i * (ta // HALO) - 1, 0), 0))


def _next_rows(ta, w, nblk):
    return pl.BlockSpec((4, None, HALO, w),
                        lambda i: (0, 0, jnp.minimum((i + 1) * (ta // HALO), nblk * (ta // HALO) - 1), 0))


def _mix_fwd(x, ya, yc, zb, cw, gg, wo_all, l, tb):
    s, d = x.shape
    ta = tb // N_STRIPS

    def body(x_ref, ya_ref, yc_ref, zb_ref, zbp_ref, cw_ref, gg_ref, wo_ref, x1_ref, yb_ref):
        i = pl.program_id(0)
        prev = jnp.where(i > 0, zbp_ref[...], 0.0)
        gb, _, _, _, _, _, c = _conv_strips(zb_ref[...], prev, cw_ref[...])
        for b in range(N_STRIPS):
            yb_ref[b % 4, b // 4] = gb[b] * c[b]
        yb = yb_ref[...].reshape(tb, CONV_CH)
        ya, yc = ya_ref[...].reshape(tb, A_WIDTH), yc_ref[...].reshape(tb, A_WIDTH)
        n = jnp.concatenate([ya * _rms_scale(ya), yb * _rms_scale(yb), yc * _rms_scale(yc)], axis=1)
        n = (n * gg_ref[...]).astype(BF)
        x1 = x_ref[...].reshape(tb, d) + _nn(n, wo_ref[...].reshape(MIX_WIDTH, d))
        x1_ref[...] = x1.reshape(4, 4, ta, d)

    res = pl.pallas_call(
        body, grid=(s // tb,), name="mix_fwd",
        in_specs=[_strip_rows(ta, d), _strip_rows(ta, A_WIDTH), _strip_rows(ta, A_WIDTH), _strip_rows(ta, ZB_W),
                  _prev_rows(ta, ZB_W), _whole((HALO, CONV_CH)), _whole((1, MIX_WIDTH)),
                  _layer((N_CHIPS, MIX_WIDTH // N_CHIPS, d), l)],
        out_specs=[_strip_rows(ta, d), _strip_rows(ta, CONV_CH)],
        out_shape=[jax.ShapeDtypeStruct((4, 4, s // N_STRIPS, d), F32),
                   jax.ShapeDtypeStruct((4, 4, s // N_STRIPS, CONV_CH), F32)],
        compiler_params=_cparams("parallel"),
    )(_strips(x), _strips(ya), _strips(yc), _strips(zb), _strips(zb), cw, gg, wo_all)
    return res[0].reshape(s, d), res[1].reshape(s, CONV_CH)


def _mlp_fwd(x1, g, w1_all, w2_all, l, tb, tf):
    s, d = x1.shape
    ff = w1_all.shape[1] * w1_all.shape[3]
    nj = ff // tf

    def body(x_ref, g_ref, w1_ref, w2_ref, x2_ref, h2_ref, ap_ref, acc):
        j = pl.program_id(1)

        @pl.when(j == 0)
        def _():
            xv = x_ref[...]
            h2_ref[...] = ((xv * _rms_scale(xv)) * g_ref[...]).astype(BF)
            acc[...] = jnp.zeros_like(acc)

        ap = _nn(h2_ref[...], w1_ref[...])
        ap_ref[...] = ap.astype(BF)
        a = jnp.square(jnp.maximum(ap, 0.0)).astype(BF)
        acc[...] += _nn(a, w2_ref[...])

        @pl.when(j == nj - 1)
        def _():
            x2_ref[...] = x_ref[...] + acc[...]

    return pl.pallas_call(
        body, grid=(s // tb, nj), name="mlp_fwd",
        in_specs=[pl.BlockSpec((tb, d), lambda i, j: (i, 0)), _whole((1, d)),
                  pl.BlockSpec((None, None, d, tf), lambda i, j: (l, j, 0, 0)),
                  pl.BlockSpec((None, None, tf, d), lambda i, j: (l, j, 0, 0))],
        out_specs=[pl.BlockSpec((tb, d), lambda i, j: (i, 0)), pl.BlockSpec((tb, d), lambda i, j: (i, 0)),
                   pl.BlockSpec((tb, tf), lambda i, j: (i, j))],
        out_shape=[jax.ShapeDtypeStruct((s, d), F32), jax.ShapeDtypeStruct((s, d), BF),
                   jax.ShapeDtypeStruct((s, ff), BF)],
        scratch_shapes=[pltpu.VMEM((tb, d), F32)],
        compiler_params=_cparams("parallel", "arbitrary"),
    )(x1, g, w1_all, w2_all)


def _loss_head(x, g, tgt, tb):
    s, d = x.shape

    def body(x_ref, g_ref, t_ref, dx_ref, loss_ref, dg_ref):
        i = pl.program_id(0)

        @pl.when(i == 0)
        def _():
            loss_ref[...] = jnp.zeros_like(loss_ref)
            dg_ref[...] = jnp.zeros_like(dg_ref)

        xv = x_ref[...]
        r = _rms_scale(xv)
        xhat = xv * r
        err = xhat * g_ref[...] - t_ref[...]
        part = jnp.sum(jnp.mean(jnp.square(err), axis=-1, keepdims=True), axis=0, keepdims=True)
        loss_ref[...] += 0.5 * part
        dy = err * (1.0 / d)
        dg_ref[...] += jnp.sum(dy * xhat, axis=0, keepdims=True)
        dx_ref[...] = _norm_bwd(dy * g_ref[...], xhat, r)

    return pl.pallas_call(
        body, grid=(s // tb,), name="loss_head",
        in_specs=[_rows(tb, d), _whole((1, d)), _rows(tb, d)],
        out_specs=[_rows(tb, d), _whole((HALO, 128)), _whole((HALO, d))],
        out_shape=[jax.ShapeDtypeStruct((s, d), F32), jax.ShapeDtypeStruct((HALO, 128), F32),
                   jax.ShapeDtypeStruct((HALO, d), F32)],
        compiler_params=_cparams("arbitrary"),
    )(x, g, tgt)


def _mlp_bwd(dx2, x1, ap, g, w1_all, w2_all, l, tb, tf):
    s, d = x1.shape
    ff = ap.shape[1]
    nj = ff // tf

    def body(dx2_ref, x1_ref, ap_ref, g_ref, w1_ref, w2_ref, dx1_ref, dap_ref, dg_ref, acc):
        i, j = pl.program_id(0), pl.program_id(1)

        @pl.when((i == 0) & (j == 0))
        def _():
            dg_ref[...] = jnp.zeros_like(dg_ref)

        @pl.when(j == 0)
        def _():
            acc[...] = jnp.zeros_like(acc)

        da = _nt(dx2_ref[...].astype(BF), w2_ref[...])
        dap = (da * (2.0 * jnp.maximum(ap_ref[...].astype(F32), 0.0))).astype(BF)
        dap_ref[...] = dap
        acc[...] += _nt(dap, w1_ref[...])

        @pl.when(j == nj - 1)
        def _():
            xv = x1_ref[...]
            r = _rms_scale(xv)
            xhat = xv * r
            dh = acc[...]
            dg_ref[...] += jnp.sum(dh * xhat, axis=0, keepdims=True)
            dx1_ref[...] = dx2_ref[...] + _norm_bwd(dh * g_ref[...], xhat, r)

    return pl.pallas_call(
        body, grid=(s // tb, nj), name="mlp_bwd",
        in_specs=[pl.BlockSpec((tb, d), lambda i, j: (i, 0)), pl.BlockSpec((tb, d), lambda i, j: (i, 0)),
                  pl.BlockSpec((tb, tf), lambda i, j: (i, j)),
                  _whole((1, d)), pl.BlockSpec((None, None, d, tf), lambda i, j: (l, j, 0, 0)),
                  pl.BlockSpec((None, None, tf, d), lambda i, j: (l, j, 0, 0))],
        out_specs=[pl.BlockSpec((tb, d), lambda i, j: (i, 0)), pl.BlockSpec((tb, tf), lambda i, j: (i, j)),
                   _whole((HALO, d))],
        out_shape=[jax.ShapeDtypeStruct((s, d), F32), jax.ShapeDtypeStruct((s, ff), BF),
                   jax.ShapeDtypeStruct((HALO, d), F32)],
        scratch_shapes=[pltpu.VMEM((tb, d), F32)],
        compiler_params=_cparams("arbitrary", "arbitrary"),
    )(dx2, x1, ap, g, w1_all, w2_all)


def _wgrad(a, b, tm, tn, ts, name, relu2=False):
    s, m = a.shape
    n = b.shape[1]
    ns = s // ts

    def body(a_ref, b_ref, o_ref, acc):
        k = pl.program_id(2)

        @pl.when(k == 0)
        def _():
            acc[...] = jnp.zeros_like(acc)

        av = a_ref[...]
        if relu2:
            av = jnp.square(jnp.maximum(av.astype(F32), 0.0)).astype(BF)
        acc[...] += _tn(av, b_ref[...].astype(BF))

        @pl.when(k == ns - 1)
        def _():
            o_ref[...] = acc[...].astype(BF)

    return pl.pallas_call(
        body, grid=(m // tm, n // tn, ns), name=name,
        in_specs=[pl.BlockSpec((ts, tm), lambda i, j, k: (k, i)), pl.BlockSpec((ts, tn), lambda i, j, k: (k, j))],
        out_specs=pl.BlockSpec((tm, tn), lambda i, j, k: (i, j)),
        out_shape=jax.ShapeDtypeStruct((m, n), BF),
        scratch_shapes=[pltpu.VMEM((tm, tn), F32)],
        compiler_params=_cparams("parallel", "parallel", "arbitrary"),
    )(a, b)


def _mix_bwd(dx1, ya, yb, yc, lse_c, sink_row, gg, wo_all, l, tb):
    s, d = dx1.shape

    def body(dx_ref, ya_ref, yb_ref, yc_ref, lse_ref, sink_ref, gg_ref, wo_ref,
             n_ref, dya_ref, dyc_ref, da_ref, dc_ref, dyb_ref, dg_ref, dsink_ref):
        i = pl.program_id(0)

        @pl.when(i == 0)
        def _():
            dg_ref[...] = jnp.zeros_like(dg_ref)
            dsink_ref[...] = jnp.zeros_like(dsink_ref)

        dn = _nt(dx_ref[...].astype(BF), wo_ref[...].reshape(MIX_WIDTH, d))
        ys = [ya_ref[...], yb_ref[...], yc_ref[...]]
        rs = [_rms_scale(v) for v in ys]
        nhat = jnp.concatenate([v * r for v, r in zip(ys, rs)], axis=1)
        gg = gg_ref[...]
        n_ref[...] = (nhat * gg).astype(BF)
        dg_ref[...] += jnp.sum(dn * nhat, axis=0, keepdims=True)
        dnh = dn * gg
        bounds = [(0, A_WIDTH), (A_WIDTH, A_WIDTH + CONV_CH), (A_WIDTH + CONV_CH, MIX_WIDTH)]
        dys = [_norm_bwd(dnh[:, lo:hi], nhat[:, lo:hi], r) for (lo, hi), r in zip(bounds, rs)]
        dyb_ref[...] = dys[1]
        head = [lax.broadcasted_iota(jnp.int32, (A_WIDTH, A_WIDTH), k) // HEAD_DIM for k in (0, 1)]
        ones = (head[0] == head[1]).astype(BF)
        for dy, y, dy_ref, dd_ref in ((dys[0], ys[0], dya_ref, da_ref), (dys[2], ys[2], dyc_ref, dc_ref)):
            dy_ref[...] = dy
            t = dy * y
            hi = t.astype(BF)
            dd_ref[...] = _nn(hi, ones) + _nn((t - hi.astype(F32)).astype(BF), ones)
        dsink_ref[...] -= jnp.sum(jnp.exp(sink_ref[...] - lse_ref[...]) * dc_ref[...], axis=0, keepdims=True)

    return pl.pallas_call(
        body, grid=(s // tb,), name="mix_bwd",
        in_specs=[_rows(tb, d), _rows(tb, A_WIDTH), _rows(tb, CONV_CH), _rows(tb, A_WIDTH), _rows(tb, A_WIDTH),
                  _whole((1, A_WIDTH)), _whole((1, MIX_WIDTH)), _layer((N_CHIPS, MIX_WIDTH // N_CHIPS, d), l)],
        out_specs=[_rows(tb, MIX_WIDTH), _rows(tb, A_WIDTH), _rows(tb, A_WIDTH), _rows(tb, A_WIDTH),
                   _rows(tb, A_WIDTH), _rows(tb, CONV_CH), _whole((HALO, MIX_WIDTH)), _whole((HALO, A_WIDTH))],
        out_shape=[jax.ShapeDtypeStruct((s, MIX_WIDTH), BF), jax.ShapeDtypeStruct((s, A_WIDTH), F32),
                   jax.ShapeDtypeStruct((s, A_WIDTH), F32), jax.ShapeDtypeStruct((s, A_WIDTH), F32),
                   jax.ShapeDtypeStruct((s, A_WIDTH), F32), jax.ShapeDtypeStruct((s, CONV_CH), F32),
                   jax.ShapeDtypeStruct((HALO, MIX_WIDTH), F32), jax.ShapeDtypeStruct((HALO, A_WIDTH), F32)],
        compiler_params=_cparams("arbitrary"),
    )(dx1, ya, yb, yc, lse_c, sink_row, gg, wo_all)


def _attn_bwd(z, dy, lse, dd, dil, kw, kcol, vcol, n_rep, max_dist, name):
    s, zw = z.shape
    n_sub = _p_sub(s, dil, 2) if n_rep == 1 else 1
    grid = _p_grid(s, dil, n_sub)
    n_kv = N_HEADS // n_rep
    dt = F32 if P_ROWS[dil] * n_sub < 16 else BF

    def body(q_ref, kp_ref, kc_ref, vp_ref, vc_ref, dy_ref, lse_ref, dd_ref, dq_ref, dkp_ref, dkc_ref, dvp_ref, dvc_ref):
        for t in range(n_sub):
            rows = slice(t * P_ROWS[dil], (t + 1) * P_ROWS[dil])
            before = (slice(None),) if t == 0 else (slice((t - 1) * P_ROWS[dil], t * P_ROWS[dil]),)
            kb_ref, vb_ref = (kp_ref, vp_ref) if t == 0 else (kc_ref, vc_ref)
            mask = _band_mask(n_sub * pl.program_id(len(grid) - 1) if t == 0 else 1, dil, max_dist)
            k2s, qs, dys, scs, dps = [], [], [], [], []
            for kh in range(n_kv):
                k2s.append(jnp.concatenate([_ld(kb_ref, _hs(kh), *before), _ld(kc_ref, _hs(kh), rows)],
                                           axis=0).astype(BF))
                v2 = jnp.concatenate([_ld(vb_ref, _hs(kh), *before), _ld(vc_ref, _hs(kh), rows)], axis=0).astype(BF)
                for h in range(kh * n_rep, (kh + 1) * n_rep):
                    qs.append((_ld(q_ref, _hs(h), rows) * SCALE).astype(BF))
                    dys.append(_ld(dy_ref, _hs(h), rows).astype(BF))
                    scs.append(jnp.where(mask, _nt(qs[h], k2s[kh]), NEG))
                    dps.append(_nt(dys[h], v2))
            for kh in range(n_kv):
                k2 = k2s[kh]
                dk2 = jnp.zeros((2 * TQ, HEAD_DIM), F32)
                dv2 = jnp.zeros((2 * TQ, HEAD_DIM), F32)
                for h in range(kh * n_rep, (kh + 1) * n_rep):
                    lse_h = _ld(lse_ref, slice(h * HEAD_DIM, h * HEAD_DIM + 1), rows)
                    dd_h = _ld(dd_ref, slice(h * HEAD_DIM, h * HEAD_DIM + 1), rows)
                    p = jnp.exp(scs[h] - lse_h)
                    ds = (p * (dps[h] - dd_h)).astype(BF)
                    _st(dq_ref, _hs(h), (_nn(ds, k2) * SCALE).astype(dt), rows)
                    dk2 = dk2 + _tn(ds, qs[h])
                    dv2 = dv2 + _tn(p.astype(BF), dys[h])
                _st(dkp_ref, _hs(kh), dk2[:TQ].astype(dt), rows)
                _st(dkc_ref, _hs(kh), dk2[TQ:].astype(dt), rows)
                _st(dvp_ref, _hs(kh), dv2[:TQ].astype(dt), rows)
                _st(dvc_ref, _hs(kh), dv2[TQ:].astype(dt), rows)

    args = [_strips(z)] * 5 + [_strips(a) for a in (dy, lse, dd)]
    pair = _p_spec(dil, A_WIDTH, 0, n_sub)
    in_specs = [pair, _p_spec(dil, kw, kcol, n_sub, True), _p_spec(dil, kw, kcol, n_sub),
                _p_spec(dil, kw, vcol, n_sub, True), _p_spec(dil, kw, vcol, n_sub)] + [pair] * 3
    out_specs = [pair] + [_p_spec(dil, kw, 0, n_sub)] * 4
    na = s // N_STRIPS
    out_shape = [jax.ShapeDtypeStruct((4, 4, na, A_WIDTH), dt)] + [jax.ShapeDtypeStruct((4, 4, na, kw), dt)] * 4
    res = pl.pallas_call(
        body, grid=grid, name=name, in_specs=in_specs, out_specs=out_specs, out_shape=out_shape,
        compiler_params=_cparams(*(("parallel",) * len(grid))),
    )(*args)
    return [res[0].reshape(s, A_WIDTH)] + [a.reshape(s, kw) for a in res[1:]]


DZ_TA = 16


def _dz_assemble(parts_a, parts_c, dyb, zb, cw):
    s = zb.shape[0]
    na = s // N_STRIPS
    nb = na // DZ_TA

    def ahead(w, k):
        return pl.BlockSpec((4, 4, DZ_TA, w), lambda i: (0, 0, jnp.minimum(i + k, nb - 1), 0))

    args, in_specs = [], []
    for dil, (dq, dkp, dkc, dvp, dvc) in zip(DILATIONS + (1,), parts_a + [parts_c]):
        w = dkp.shape[1]
        here = _strip_rows(DZ_TA, w)
        if dil == 1:
            args += [dq, dkp, dkp, dkc, dvp, dvp, dvc]
            in_specs += [_strip_rows(DZ_TA, A_WIDTH), here, ahead(w, 1), here, here, ahead(w, 1), here]
        else:
            k = 8 * dil // DZ_TA
            args += [dq, dkp, dkc, dvp, dvc]
            in_specs += [_strip_rows(DZ_TA, A_WIDTH), ahead(w, k), here, ahead(w, k), here]
    n_att = len(args)
    args = [_strips(a) for a in args] + [_strips(dyb), _strips(dyb), _strips(zb), _strips(zb), _strips(zb), cw]
    in_specs += [_strip_rows(DZ_TA, CONV_CH), _next_rows(DZ_TA, CONV_CH, nb), _strip_rows(DZ_TA, ZB_W),
                 _prev_rows(DZ_TA, ZB_W), _next_rows(DZ_TA, ZB_W, nb), _whole((HALO, CONV_CH))]

    def body(*refs):
        att = list(refs[:n_att])
        dyb_ref, dybn_ref, zb_ref, zbp_ref, zbn_ref, cw_ref, dz_ref, dcw_ref = refs[n_att:]
        i = pl.program_id(0)

        @pl.when(i == 0)
        def _():
            dcw_ref[...] = jnp.zeros_like(dcw_ref)

        def shifted(dil):
            if dil == 1:
                dq_r, kp0, kp1, dkc_r, vp0, vp1, dvc_r = [att.pop(0) for _ in range(7)]
                live = i + 1 < nb
                half = DZ_TA // 2
                kp0, kp1, vp0, vp1 = [r[...].astype(F32) for r in (kp0, kp1, vp0, vp1)]
                dkp = jnp.concatenate([kp0[:, :, half:, :], jnp.where(live, kp1[:, :, :half, :], 0.0)], axis=2)
                dvp = jnp.concatenate([vp0[:, :, half:, :], jnp.where(live, vp1[:, :, :half, :], 0.0)], axis=2)
            else:
                dq_r, dkp_r, dkc_r, dvp_r, dvc_r = [att.pop(0) for _ in range(5)]
                live = i + 8 * dil // DZ_TA < nb
                dkp = jnp.where(live, dkp_r[...].astype(F32), 0.0)
                dvp = jnp.where(live, dvp_r[...].astype(F32), 0.0)
            return dq_r[...].astype(F32), dkc_r[...].astype(F32) + dkp, dvc_r[...].astype(F32) + dvp

        dq, dk, dv = shifted(DILATIONS[0])
        for dil in DILATIONS[1:]:
            dq2, dk2, dv2 = shifted(dil)
            dq, dk, dv = dq + dq2, dk + dk2, dv + dv2
        dz_ref[:, :, :, 0:A_WIDTH] = dq.astype(BF)
        dz_ref[:, :, :, A_WIDTH:2 * A_WIDTH] = dk.astype(BF)
        dz_ref[:, :, :, 2 * A_WIDTH:ZA_W] = dv.astype(BF)
        dq, dk, dv = shifted(1)
        c0 = ZA_W + ZB_W
        dz_ref[:, :, :, c0:c0 + A_WIDTH] = dq.astype(BF)
        dz_ref[:, :, :, c0 + A_WIDTH:c0 + A_WIDTH + C_KV_WIDTH] = dk.astype(BF)
        dz_ref[:, :, :, c0 + A_WIDTH + C_KV_WIDTH:IN_WIDTH] = dv.astype(BF)

        cw = cw_ref[...]
        prev = jnp.where(i > 0, zbp_ref[...], 0.0)
        gb, gc, xb, u, u1, u2, c = _conv_strips(zb_ref[...], prev, cw)
        dyb = dyb_ref[...]
        dc = [_strip(dyb, b) * gb[b] for b in range(N_STRIPS)]
        dcn = jnp.where(i + 1 < nb, dybn_ref[...] * zbn_ref[:, :, :CONV_CH], 0.0)
        wrapped = [_shift_up(dc[0], 1, dcn[0]), _shift_up(dc[1], 1, dcn[1])]
        upd = [jnp.zeros((1, CONV_CH), F32)] * 3
        for b in range(N_STRIPS):
            dc1 = dc[b + 1] if b + 1 < N_STRIPS else wrapped[0]
            dc2 = dc[b + 2] if b + 2 < N_STRIPS else wrapped[b + 2 - N_STRIPS]
            du = cw[2:3, :] * dc[b] + cw[1:2, :] * dc1 + cw[0:1, :] * dc2
            f, e = b % 4, b // 4
            dz_ref[f, e, :, ZA_W:ZA_W + CONV_CH] = (_strip(dyb, b) * c[b]).astype(BF)
            dz_ref[f, e, :, ZA_W + CONV_CH:ZA_W + 2 * CONV_CH] = (du * xb[b]).astype(BF)
            dz_ref[f, e, :, ZA_W + 2 * CONV_CH:c0] = (du * gc[b]).astype(BF)
            for t, uu in enumerate((u2[b], u1[b], u[b])):
                upd[t] = upd[t] + jnp.sum(dc[b] * uu, axis=0, keepdims=True)
        row = lax.broadcasted_iota(jnp.int32, (HALO, CONV_CH), 0)
        tile = jnp.zeros((HALO, CONV_CH), F32)
        for t in range(3):
            tile = jnp.where(row == t, upd[t], tile)
        dcw_ref[...] += tile

    dz, dcw = pl.pallas_call(
        body, grid=(nb,), name="dz_assemble", in_specs=in_specs,
        out_specs=[_strip_rows(DZ_TA, IN_WIDTH), _whole((HALO, CONV_CH))],
        out_shape=[jax.ShapeDtypeStruct((4, 4, na, IN_WIDTH), BF), jax.ShapeDtypeStruct((HALO, CONV_CH), F32)],
        compiler_params=_cparams("arbitrary"),
    )(*args)
    return dz.reshape(s, IN_WIDTH), dcw


def _qkv_bwd(dz, dx1, x, g, w_all, l, tb, tokens_out):
    s, d = x.shape
    na, ta = s // N_STRIPS, tb // N_STRIPS

    def body(dz_ref, dx1_ref, x_ref, g_ref, w_ref, dx_ref, dg_ref):
        i = pl.program_id(0)

        @pl.when(i == 0)
        def _():
            dg_ref[...] = jnp.zeros_like(dg_ref)

        n = IN_WIDTH // N_CHIPS
        dz = dz_ref[...].reshape(tb, IN_WIDTH)
        dh = _nt(dz[:, 0:n], w_ref[0])
        for k in range(1, N_CHIPS):
            dh = dh + _nt(dz[:, k * n:(k + 1) * n], w_ref[k])
        xv = x_ref[...].reshape(tb, d)
        r = _rms_scale(xv)
        xhat = xv * r
        dg_ref[...] += jnp.sum(dh * xhat, axis=0, keepdims=True)
        dx = (dx1_ref[...].reshape(tb, d) + _norm_bwd(dh * g_ref[...], xhat, r)).reshape(4, 4, ta, d)
        if tokens_out:
            for b in range(N_STRIPS):
                dx_ref[:, b, :] = _strip(dx, b)
        else:
            dx_ref[...] = dx

    if tokens_out:
        dx_spec, dx_shape = pl.BlockSpec((ta, N_STRIPS, d), lambda i: (i, 0, 0)), (na, N_STRIPS, d)
    else:
        dx_spec, dx_shape = _strip_rows(ta, d), (4, 4, na, d)
    dx, dg = pl.pallas_call(
        body, grid=(s // tb,), name="qkv_bwd",
        in_specs=[_strip_rows(ta, IN_WIDTH), _strip_rows(ta, d), _strip_rows(ta, d), _whole((1, d)),
                  _layer((N_CHIPS, d, IN_WIDTH // N_CHIPS), l)],
        out_specs=[dx_spec, _whole((HALO, d))],
        out_shape=[jax.ShapeDtypeStruct(dx_shape, F32), jax.ShapeDtypeStruct((HALO, d), F32)],
        compiler_params=_cparams("arbitrary"),
    )(_strips(dz), _strips(dx1), _strips(x), g, w_all)
    return dx.reshape(s, d), dg


def _tile_rows(rows):
    return jnp.pad(rows, ((0, HALO - rows.shape[0]), (0, 0)))


def _to_strips(a, after, name):
    s, d = a.shape
    na = s // N_STRIPS
    ta = min(32, na)

    def body(a_ref, *rest):
        for b in range(N_STRIPS):
            rest[-1][b % 4, b // 4] = a_ref[:, b, :]

    return pl.pallas_call(
        body, grid=(na // ta,), name=name,
        in_specs=[pl.BlockSpec((ta, N_STRIPS, d), lambda i: (i, 0, 0))] + [ANY] * len(after),
        out_specs=_strip_rows(ta, d),
        out_shape=jax.ShapeDtypeStruct((4, 4, na, d), a.dtype), compiler_params=_cparams("parallel"),
    )(a.reshape(na, N_STRIPS, d), *after).reshape(s, d)


def _local_step(x, tgt, fetch, ff, sinks, g_mix, g_group, g_mlp, g_final, emit):
    s, d = x.shape
    depth = g_mix.shape[0]
    tb = min(512, s)
    tf = ff // N_CHIPS
    ts = min(1024, s)
    saved = []
    for l in range(depth):
        w_in, _, _, _, conv_w = fetch(0, l, x)
        cw = _tile_rows(conv_w[l])
        sk = jnp.repeat(sinks[l].reshape(N_HEADS), HEAD_DIM)[None]
        h, za, zb, zc = _qkv_fwd(x, g_mix[l][None], w_in, l, tb)
        parts_a = [_attn_fwd(za, dil, A_WIDTH, 1, 2, 1, A_MAX_DIST, "attn_a_fwd_%d" % dil) for dil in DILATIONS]
        part_c = _attn_fwd(zc, 1, C_KV_WIDTH, 3, 4, C_GROUP, C_MAX_DIST, "attn_c_fwd")
        ya, lse_a, yc, lse_c = _attn_merge(parts_a, part_c, sk, ts)
        w_in, w_o, w1, w2, _ = fetch(1, l, yc)
        x1, yb = _mix_fwd(x, ya, yc, zb, cw, g_group[l][None], w_o, l, ts)
        w_in, w_o, w1, w2, _ = fetch(2, l, x1)
        x2, h2, ap = _mlp_fwd(x1, g_mlp[l][None], w1, w2, l, ts, tf)
        saved.append((x, h, za, zb, zc, ya, lse_a, yc, lse_c, yb, x1, h2, ap, cw, sk))
        x = x2
    dx, loss_tile, dg_final = _loss_head(x, g_final[None], tgt, ts)
    grads = [None] * depth
    tok = jnp.zeros((), F32)
    for l in reversed(range(depth)):
        x0, h, za, zb, zc, ya, lse_a, yc, lse_c, yb, x1, h2, ap, cw, sk = saved[l]
        dx1, dap, dg_mlp = _mlp_bwd(dx, x1, ap, g_mlp[l][None] + tok, w1, w2, l, ts, tf)
        tok = emit(l, 3, _wgrad(ap, dx, min(1024, ff), d, 2 * ts, "wgrad_ff_out", relu2=True))
        tok = tok + emit(l, 2, _wgrad(h2, dap, d, min(1024, ff), 2 * ts, "wgrad_ff_in"))
        n, dya, dyc, dd_a, dd_c, dyb, dg_group, dsink = _mix_bwd(dx1, ya, yb, yc, lse_c, sk, g_group[l][None] + tok,
                                                                 w_o, l, tb)
        tok = emit(l, 1, _wgrad(n, dx1, MIX_WIDTH, d, ts, "wgrad_o"))
        cw = cw + tok
        parts_a = [_attn_bwd(za, dya, lse_a, dd_a, dil, A_WIDTH, 1, 2, 1, A_MAX_DIST, "attn_a_bwd_%d" % dil)
                   for dil in DILATIONS]
        parts_c = _attn_bwd(zc, dyc, lse_c, dd_c, 1, C_KV_WIDTH, 3, 4, C_GROUP, C_MAX_DIST, "attn_c_bwd")
        dz, dcw = _dz_assemble(parts_a, parts_c, dyb, zb, cw)
        tok = emit(l, 0, _wgrad(h, dz, d, IN_WIDTH // 4, 2 * ts, "wgrad_in"))
        dx, dg_mix = _qkv_bwd(dz, dx1, x0, g_mix[l][None] + tok, w_in, l, tb, l == 0)
        grads[l] = (dcw, dsink, dg_mix, dg_group, dg_mlp)
    return loss_tile, dx, grads, dg_final


ANY = pl.BlockSpec(memory_space=pl.ANY)
SHARD_AXES = (2, 1, 2, 1)
N_BIG = len(SHARD_AXES)
N_CHIPS = 4
N_DEV = 8


def _mesh_pos():
    return lax.axis_index("x"), lax.axis_index("y"), lax.axis_index("c")


def _flip(v, bit):
    return 1 - v if bit else v


def _place_shard(shard, chip_arr, name):
    _, rows, cols = shard.shape
    tr = min(256, rows)

    def body(chip_ref, x_ref, o_ref):
        o_ref[...] = x_ref[...].astype(BF)

    return pl.pallas_call(
        body, name=name,
        grid_spec=pltpu.PrefetchScalarGridSpec(
            num_scalar_prefetch=1, grid=(2, rows // tr),
            in_specs=[pl.BlockSpec((None, tr, cols), lambda l, i, chip: (l, i, 0))],
            out_specs=pl.BlockSpec((None, None, tr, cols), lambda l, i, chip: (l, chip[0], i, 0))),
        out_shape=jax.ShapeDtypeStruct((2, N_CHIPS, rows, cols), BF),
        compiler_params=_cparams("parallel", "parallel"),
    )(chip_arr, shard)


HBM = pl.BlockSpec(memory_space=pltpu.HBM)
SEM = pl.BlockSpec(memory_space=pltpu.SEMAPHORE)
EFFECT = pltpu.SideEffectType.DATAFLOW_SIDE_EFFECTING

GATHER_GROUPS = (((0, 0),), ((1, 0),), ((2, 0), (3, 0)), ((0, 1),), ((1, 1), (2, 1), (3, 1)))
GATHER_STARTS = ((0,), (1, 2), (3, 4))
GATHER_STAGES = {(0, 0): 0, (1, 0): 1, (2, 0): 2, (0, 1): 3, (1, 1): 4}


def _gather_copies(arrs, group, send_sems, recv_sems):
    x, y, c = _mesh_pos()
    me = 2 * x + y
    out = []
    for i, (w, layer) in enumerate(group):
        mine = arrs[w].at[layer, me]
        for j, (qx, qy) in enumerate([(1 - x, y), (x, 1 - y), (1 - x, 1 - y)]):
            landed = arrs[w].at[layer, 2 * qx + qy]
            out.append(tuple(pltpu.make_async_remote_copy(
                src_ref=piece, dst_ref=piece, send_sem=send_sems.at[i * 3 + j], recv_sem=recv_sems.at[i * 3 + j],
                device_id=(qx, qy, c), device_id_type=MESH) for piece in (mine, landed)))
    return out


def _conv_copies(conv_src, conv_dst, send_sems, recv_sems):
    x, y, c = _mesh_pos()
    out = []
    for j, (qx, qy) in enumerate([(1 - x, y), (x, 1 - y), (1 - x, 1 - y)]):
        out.append(tuple(pltpu.make_async_remote_copy(
            src_ref=conv_src, dst_ref=conv_dst.at[q], send_sem=send_sems.at[j], recv_sem=recv_sems.at[j],
            device_id=(qx, qy, c), device_id_type=MESH) for q in (2 * x + y, 2 * qx + qy)))
    return out


def _gather_start(groups, arrs, conv, name, through=None):
    n_sems = 2 * (len(groups) + (conv is not None))
    mats = sorted({w for g in groups for w, _ in GATHER_GROUPS[g]})

    def body(*refs):
        arrs_ref = [None] * N_BIG
        for w, ref in zip(mats, refs):
            arrs_ref[w] = ref
        sems = refs[n_in:n_in + n_sems]
        if conv is not None:
            for cp, _ in _conv_copies(refs[len(mats)], refs[len(mats) + 1], sems[-2], sems[-1]):
                cp.start()
        for k, g in enumerate(groups):
            for cp, _ in _gather_copies(arrs_ref, GATHER_GROUPS[g], sems[2 * k], sems[2 * k + 1]):
                cp.start()

    sem_shapes = []
    for n in [len(GATHER_GROUPS[g]) for g in groups] + ([1] if conv is not None else []):
        sem_shapes += [pltpu.SemaphoreType.DMA((3 * n,))] * 2
    operands = [arrs[w] for w in mats] + ([] if conv is None else list(conv)) + ([] if through is None else [through])
    n_in = len(operands)
    res = pl.pallas_call(
        body, name=name,
        out_shape=tuple(sem_shapes) + tuple(pltpu.HBM(a.shape, a.dtype) for a in operands),
        in_specs=(HBM,) * n_in, out_specs=(SEM,) * n_sems + (HBM,) * n_in,
        input_output_aliases={i: n_sems + i for i in range(n_in)},
        compiler_params=pltpu.CompilerParams(has_side_effects=EFFECT),
    )(*[pltpu.with_memory_space_constraint(a, pltpu.HBM) for a in operands])
    arrs = list(arrs)
    for w, a in zip(mats, res[n_sems:]):
        arrs[w] = a
    return res[:n_sems], arrs, list(res[n_sems + len(mats):])


def _gather_wait(k, sems, arrs, conv, after, name):
    group = GATHER_GROUPS[k]
    mats = sorted({w for w, _ in group})
    n_conv = 0 if conv is None else 2

    def body(*refs):
        local = refs[:len(mats)]
        arrs_ref = [None] * N_BIG
        for w, ref in zip(mats, local):
            arrs_ref[w] = ref
        pos = len(mats) + n_conv
        copies = _gather_copies(arrs_ref, group, refs[pos], refs[pos + 1])
        if conv is not None:
            copies += _conv_copies(refs[len(mats)], refs[len(mats) + 1], refs[pos + 2], refs[pos + 3])
        for send, recv in copies:
            recv.wait_recv()
            send.wait_send()

    operands = [arrs[w] for w in mats] + ([] if conv is None else [conv[1], conv[2]])
    sem_ops = list(sems) + ([] if conv is None else list(conv[0]))
    n_op = len(operands)
    res = pl.pallas_call(
        body, name=name, out_shape=tuple(pltpu.HBM(a.shape, a.dtype) for a in operands),
        in_specs=(HBM,) * n_op + (SEM,) * len(sem_ops) + (ANY,) * len(after), out_specs=(HBM,) * n_op,
        input_output_aliases={i: i for i in range(n_op)},
        compiler_params=pltpu.CompilerParams(has_side_effects=EFFECT),
    )(*operands, *sem_ops, *after)
    arrs = list(arrs)
    for w, a in zip(mats, res):
        arrs[w] = a
    return arrs, (res[-1] if conv is not None else None)


def _grad_shard(ref, w, chip, n):
    start = pl.multiple_of(chip * n, 128)
    if SHARD_AXES[w] == 2:
        return ref.at[:, pl.ds(start, n)]
    return ref.at[pl.ds(start, n), :]


def _slot_shape(g, w):
    shape = list(g.shape)
    shape[SHARD_AXES[w] - 1] //= N_CHIPS
    return (N_DEV - 1,) + tuple(shape)


def _scatter_copies(g_ref, land_ref, send_sems, recv_sems, layer, w):
    x, y, c = _mesh_pos()
    n = g_ref.shape[SHARD_AXES[w] - 1] // N_CHIPS
    out = []
    for r in range(1, N_DEV):
        tx, ty, tc = _flip(x, r & 4), _flip(y, r & 2), _flip(c, r & 1)
        cp = pltpu.make_async_remote_copy(
            src_ref=_grad_shard(g_ref, w, 2 * tx + ty, n), dst_ref=land_ref.at[r - 1], send_sem=send_sems.at[r - 1],
            recv_sem=recv_sems.at[r - 1], device_id=(tx, ty, tc), device_id_type=MESH)
        out.append((cp, (c != layer) if r & 1 else (c == layer)))
    return out


def _scatter_start(items, layer, name):
    n = len(items)

    def body(*refs):
        for i, (w, _, _) in enumerate(items):
            g_ref, land_ref = refs[2 * i], refs[2 * i + 1]
            send_sems, recv_sems = refs[2 * n + 2 * i], refs[2 * n + 2 * i + 1]
            for cp, mine in _scatter_copies(g_ref, land_ref, send_sems, recv_sems, layer, w):
                @pl.when(mine)
                def _():
                    cp.start()
        refs[-1][...] = jnp.zeros_like(refs[-1])

    operands = [a for _, g, land in items for a in (g, land)]
    res = pl.pallas_call(
        body, name=name,
        out_shape=(pltpu.SemaphoreType.DMA((N_DEV - 1,)),) * (2 * n)
        + tuple(pltpu.HBM(a.shape, a.dtype) for a in operands) + (jax.ShapeDtypeStruct((HALO, 128), F32),),
        in_specs=(HBM,) * (2 * n),
        out_specs=(SEM,) * (2 * n) + (HBM,) * (2 * n) + (pl.BlockSpec(memory_space=pltpu.VMEM),),
        input_output_aliases={i: 2 * n + i for i in range(2 * n)},
        compiler_params=pltpu.CompilerParams(has_side_effects=EFFECT),
    )(*[pltpu.with_memory_space_constraint(a, pltpu.HBM) for a in operands])
    return [(res[2 * i], res[2 * i + 1], res[2 * n + 2 * i], res[2 * n + 2 * i + 1]) for i in range(n)], res[-1]


def _scatter_wait(started, land, after, w, name):
    def body(g0_ref, g1_ref, land_ref, ss0, rs0, ss1, rs1, after_ref, g0_out, g1_out, land_out):
        c = lax.axis_index("c")
        for layer, g_ref, ss, rs in ((0, g0_ref, ss0, rs0), (1, g1_ref, ss1, rs1)):
            for cp, mine in _scatter_copies(g_ref, land_ref, ss, rs, layer, w):
                @pl.when(mine)
                def _():
                    cp.wait_send()

                @pl.when(c == layer)
                def _():
                    cp.wait_recv()

    (ss0, rs0, g0), (ss1, rs1, g1) = started
    return pl.pallas_call(
        body, name=name,
        out_shape=(pltpu.HBM(g0.shape, g0.dtype), pltpu.HBM(g1.shape, g1.dtype), pltpu.HBM(land.shape, land.dtype)),
        in_specs=(HBM, HBM, HBM, SEM, SEM, SEM, SEM, ANY), out_specs=(HBM, HBM, HBM),
        input_output_aliases={0: 0, 1: 1, 2: 2}, compiler_params=pltpu.CompilerParams(has_side_effects=EFFECT),
    )(g0, g1, land, ss0, rs0, ss1, rs1, after)


def _sum_slots(g0, g1, slots, w, pos_arr, name):
    _, rows, cols = slots.shape
    tr = min(512, rows)
    nr = rows // tr
    if SHARD_AXES[w] == 2:
        own = pl.BlockSpec((tr, cols), lambda i, pos: (i, pos[0]))
    else:
        own = pl.BlockSpec((tr, cols), lambda i, pos: (pos[0] * nr + i, 0))

    def body(pos_ref, own0_ref, own1_ref, s_ref, o_ref):
        acc = jnp.where(pos_ref[1] == 0, own0_ref[...], own1_ref[...]).astype(F32)
        for r in range(N_DEV - 1):
            acc = acc + s_ref[r].astype(F32)
        o_ref[...] = acc

    return pl.pallas_call(
        body, name=name,
        grid_spec=pltpu.PrefetchScalarGridSpec(
            num_scalar_prefetch=1, grid=(nr,),
            in_specs=[own, own, pl.BlockSpec((N_DEV - 1, tr, cols), lambda i, pos: (0, i, 0))],
            out_specs=pl.BlockSpec((tr, cols), lambda i, pos: (i, 0))),
        out_shape=jax.ShapeDtypeStruct((rows, cols), F32), compiler_params=_cparams("parallel"),
    )(pos_arr, g0, g1, slots)


def _swap_copies(refs, n):
    x, y, c = _mesh_pos()
    return [pltpu.make_async_remote_copy(src_ref=refs[w], dst_ref=refs[n + w], send_sem=refs[2 * n].at[w],
                                         recv_sem=refs[2 * n + 1].at[w], device_id=(x, y, 1 - c), device_id_type=MESH)
            for w in range(n)]


def _swap_start(halves, name):
    n = len(halves)

    def body(*refs):
        for cp in _swap_copies(refs, n):
            cp.start()

    operands = list(halves) + [lax.empty(h.shape, h.dtype) for h in halves]
    res = pl.pallas_call(
        body, name=name,
        out_shape=(pltpu.SemaphoreType.DMA((n,)),) * 2 + tuple(pltpu.HBM(a.shape, a.dtype) for a in operands),
        in_specs=(HBM,) * (2 * n), out_specs=(SEM,) * 2 + (HBM,) * (2 * n),
        input_output_aliases={i: 2 + i for i in range(2 * n)},
        compiler_params=pltpu.CompilerParams(has_side_effects=EFFECT),
    )(*[pltpu.with_memory_space_constraint(a, pltpu.HBM) for a in operands])
    return res[0], res[1], list(res[2:2 + n]), list(res[2 + n:])


def _swap_wait(send_sems, recv_sems, halves, lands, after, name):
    n = len(halves)

    def body(*refs):
        for cp in _swap_copies(refs, n):
            cp.wait_send()
            cp.wait_recv()

    operands = list(halves) + list(lands)
    res = pl.pallas_call(
        body, name=name, out_shape=tuple(pltpu.HBM(a.shape, a.dtype) for a in operands),
        in_specs=(HBM,) * (2 * n) + (SEM, SEM, ANY), out_specs=(HBM,) * (2 * n),
        input_output_aliases={i: i for i in range(2 * n)},
        compiler_params=pltpu.CompilerParams(has_side_effects=EFFECT),
    )(*operands, send_sems, recv_sems, after)
    return list(res[n:])


def _adamw_math(w, g, m, v):
    m = ADAM_B1 * m + (1.0 - ADAM_B1) * g
    v = ADAM_B2 * v + (1.0 - ADAM_B2) * jnp.square(g)
    m_hat = m / (1.0 - ADAM_B1 ** ADAM_STEP)
    v_hat = v / (1.0 - ADAM_B2 ** ADAM_STEP)
    delta = -ADAM_LR * (m_hat / (jnp.sqrt(v_hat) + ADAM_EPS) + ADAM_WD * w)
    return delta, m, v


def _adamw(w, g, m, v, filled, pos_arr, name):
    shape = w.shape
    _, rows, cols = shape
    tr = min(256, rows)

    def body(pos_ref, w_ref, g_ref, m_ref, v_ref, *rest):
        go_ref, d_ref, m2_ref, v2_ref = rest[-4:]
        g = g_ref[...]
        go_ref[...] = g
        d_ref[...], m2_ref[...], v2_ref[...] = _adamw_math(w_ref[...], g, m_ref[...], v_ref[...])

    def layer(pos):
        return pos[1] if filled is None else 1 - pos[1]

    full = pl.BlockSpec((None, tr, cols), lambda i, pos: (layer(pos), i, 0))
    half = pl.BlockSpec((tr, cols), lambda i, pos: (i, 0))
    n_in = 5
    return pl.pallas_call(
        body, name=name,
        grid_spec=pltpu.PrefetchScalarGridSpec(
            num_scalar_prefetch=1, grid=(rows // tr,),
            in_specs=[full, half, full, full] + ([] if filled is None else [ANY] * 4), out_specs=[full] * 4),
        out_shape=[jax.ShapeDtypeStruct(shape, F32)] * 4,
        input_output_aliases={} if filled is None else {n_in + k: k for k in range(4)},
        compiler_params=_cparams("parallel"),
    )(pos_arr, w, g, m, v, *([] if filled is None else filled))


def _small_sync(part, w, m, v):
    rows, cols = part.shape

    def body(p_ref, w_ref, m_ref, v_ref, g_ref, d_ref, m2_ref, v2_ref, slots, send_sems, recv_sems):
        x, y, c = _mesh_pos()
        me = 4 * x + 2 * y + c
        slots[me] = p_ref[...]
        sends = []
        for r in range(1, N_DEV):
            to = (_flip(x, r & 4), _flip(y, r & 2), _flip(c, r & 1))
            sends.append(pltpu.make_async_remote_copy(
                src_ref=p_ref, dst_ref=slots.at[me], send_sem=send_sems.at[r - 1], recv_sem=recv_sems.at[r - 1],
                device_id=to, device_id_type=MESH))
        for cp in sends:
            cp.start()
        for cp in sends:
            cp.wait_recv()
        for cp in sends:
            cp.wait_send()
        g = slots[0]
        for i in range(1, N_DEV):
            g = g + slots[i]
        g_ref[...] = g
        d_ref[...], m2_ref[...], v2_ref[...] = _adamw_math(w_ref[...], g, m_ref[...], v_ref[...])

    vm = pl.BlockSpec(memory_space=pltpu.VMEM)
    return pl.pallas_call(
        body, name="small_sync", in_specs=[vm] * 4, out_specs=[vm] * 4,
        out_shape=[jax.ShapeDtypeStruct((rows, cols), F32)] * 4,
        scratch_shapes=[pltpu.VMEM((N_DEV, rows, cols), F32), pltpu.SemaphoreType.DMA((N_DEV - 1,)),
                        pltpu.SemaphoreType.DMA((N_DEV - 1,))],
    )(part, w, m, v)


PACK_W = 256


def _pack_rows(n):
    return -(-n // (HALO * PACK_W)) * HALO


def _pack_small(parts):
    out = []
    for a in parts:
        flat = a.reshape(-1)
        out.append(jnp.pad(flat, (0, _pack_rows(flat.size) * PACK_W - flat.size)).reshape(-1, PACK_W))
    return jnp.concatenate(out, axis=0)


def _unpack_small(p, shapes):
    out, row = [], 0
    for shape in shapes:
        n = 1
        for k in shape:
            n *= k
        out.append(p[row:row + _pack_rows(n)].reshape(-1)[:n].reshape(shape))
        row += _pack_rows(n)
    return out


def kernel(x, w_in, conv_w, sinks, g_mix, g_group, w_o, g_mlp, w_ff_in, w_ff_out, g_final, loss_target, m_w_in, m_conv_w, m_sinks, m_g_mix, m_g_group, m_w_o, m_g_mlp, m_w_ff_in, m_w_ff_out, m_g_final, v_w_in, v_conv_w, v_sinks, v_g_mix, v_g_group, v_w_o, v_g_mlp, v_w_ff_in, v_w_ff_out, v_g_final):
    chip = 2 * lax.axis_index("x") + lax.axis_index("y")
    conv_n = conv_w.shape[2]

    pos_arr = jnp.stack([chip, lax.axis_index("c")]).astype(jnp.int32)
    shards = (w_in, w_o, w_ff_in, w_ff_out)
    conv_tile = jnp.pad(conv_w.reshape(6, conv_n), ((0, HALO - 6), (0, 128 - conv_n)))
    placed = [_place_shard(w_in, pos_arr[:1], "place_shard_0"), None, None, None]
    sems_a, placed, conv_thru = _gather_start(
        GATHER_STARTS[0], placed, (conv_tile, lax.empty((N_CHIPS,) + conv_tile.shape, conv_tile.dtype)),
        "gather_start_0")
    for i in range(1, N_BIG):
        placed[i] = _place_shard(shards[i], pos_arr[:1], "place_shard_%d" % i)
    full = {"arrs": placed, "conv": None, "sems": list(sems_a[:2])}
    target = _to_strips(loss_target[0], placed[:1], "to_strips_target")

    def fetch(stage, layer, after):
        k = GATHER_STAGES.get((stage, layer))
        if k is None:
            return (*full["arrs"], full["conv"])
        sems = full["sems"][2 * k:2 * k + 2]
        if k == 0:
            full["arrs"], land = _gather_wait(0, sems, full["arrs"], (sems_a[-2:], *conv_thru), (after, target),
                                              "gather_wait_0")
            conv_all = lax.dynamic_update_slice(land, conv_tile[None], (chip, 0, 0))
            full["conv"] = conv_all[:, :6, :conv_n].reshape(N_CHIPS, 2, 3, conv_n).transpose(1, 2, 0, 3).reshape(
                2, 3, CONV_CH)
            sems_b, full["arrs"], rest = _gather_start(GATHER_STARTS[1], full["arrs"], None, "gather_start_1",
                                                       through=full["arrs"][0])
            full["arrs"][0] = rest[-1]
            full["sems"] += list(sems_b)
        else:
            full["arrs"], _ = _gather_wait(k, sems, full["arrs"], None, (after,), "gather_wait_%d" % k)
        if k == 2:
            sems_c, full["arrs"], _ = _gather_start(GATHER_STARTS[2], full["arrs"], None, "gather_start_2")
            full["sems"] += list(sems_c)
        return (*full["arrs"], full["conv"])

    lands, started, pending = [None] * N_BIG, {}, []

    def emit(layer, w, g):
        if lands[w] is None:
            lands[w] = lax.empty(_slot_shape(g, w), g.dtype)
        pending.append((w, g, lands[w]))
        if not (w == 0 or (layer == 0 and w == 1)):
            return jnp.zeros((), F32)
        name = "scatter_start_%d_%d" % (layer, len(pending))
        done, token = _scatter_start(list(pending), layer, name)
        for (w_i, _, _), (ss, rs, g_thru, land) in zip(pending, done):
            started[layer, w_i], lands[w_i] = (ss, rs, g_thru), land
        pending.clear()
        return token[0, 0]

    loss_tile, dx, grads, dg_final = _local_step(_to_strips(x[0], placed, "to_strips_x"), target, fetch,
                                                 w_ff_in.shape[2] * N_CHIPS,
                                                 sinks, g_mix, g_group, g_mlp, g_final, emit)

    wmv = ((w_in, m_w_in, v_w_in), (w_o, m_w_o, v_w_o), (w_ff_in, m_w_ff_in, v_w_ff_in),
           (w_ff_out, m_w_ff_out, v_w_ff_out))
    big, after = [None] * N_BIG, dx
    for name, ws in (("swap_rest", (1, 2, 3)), ("swap_in", (0,))):
        own = []
        for w in ws:
            g0, g1, slots = _scatter_wait((started[0, w], started[1, w]), lands[w], after, w, "scatter_wait_%d" % w)
            own.append(_sum_slots(g0, g1, slots, w, pos_arr, "sum_slots_%d" % w))
        send_sems, recv_sems, own, zones = _swap_start(own, name + "_start")
        for w, g in zip(ws, own):
            big[w] = _adamw(wmv[w][0], g, wmv[w][1], wmv[w][2], None, pos_arr, "adamw_own_%d" % w)
        theirs = _swap_wait(send_sems, recv_sems, own, zones, big[ws[-1]][1], name + "_wait")
        for w, g in zip(ws, theirs):
            big[w] = _adamw(wmv[w][0], g, wmv[w][1], wmv[w][2], big[w], pos_arr, "adamw_other_%d" % w)
        after = big[ws[-1]][1]

    def both(i):
        return jnp.stack([grads[0][i][0], grads[1][i][0]])
    dconv = jnp.stack([grads[0][0][:3], grads[1][0][:3]])
    dsinks = jnp.stack([grads[0][1][0, ::HEAD_DIM], grads[1][1][0, ::HEAD_DIM]])
    part = _pack_small([both(2), both(3), both(4), dg_final[0], dconv, dsinks, loss_tile[0, 0]])

    def spread(shard):
        return lax.dynamic_update_slice(jnp.zeros((2, 3, CONV_CH), F32), shard, (0, 0, chip * conv_n))
    zero = jnp.zeros((), F32)
    packs = [_pack_small([a, b, c_, e, spread(f), g_, zero]) for a, b, c_, e, f, g_ in (
        (g_mix, g_group, g_mlp, g_final, conv_w, sinks),
        (m_g_mix, m_g_group, m_g_mlp, m_g_final, m_conv_w, m_sinks),
        (v_g_mix, v_g_group, v_g_mlp, v_g_final, v_conv_w, v_sinks))]
    shapes = [g_mix.shape, g_group.shape, g_mlp.shape, g_final.shape, (2, 3, CONV_CH), sinks.shape, ()]
    small = [_unpack_small(p, shapes) for p in _small_sync(part, *packs)]

    def shard_of(full):
        return lax.dynamic_slice(full, (0, 0, chip * conv_n), (2, 3, conv_n))
    small = [(s[0], s[1], s[2], s[3], shard_of(s[4]), s[5], s[6]) for s in small]
    loss = small[0][6]

    def ordered(kind):
        b = [big[i][kind] for i in range(N_BIG)]
        s = small[kind]
        return [b[0], s[4], s[5], s[0], s[1], b[1], s[2], b[2], b[3], s[3]]

    return (loss, dx[None], *ordered(0), *ordered(1), *ordered(2), *ordered(3))
```

```python
import functools

import jax
import jax.numpy as jnp
from jax import lax
from jax.experimental import pallas as pl
from jax.experimental.pallas import tpu as pltpu

HEAD_DIM = 64
N_HEADS = 6
C_GROUP = 3
A_WIDTH = N_HEADS * HEAD_DIM
C_KV_WIDTH = 2 * HEAD_DIM
CONV_CH = 256
ZA_W = 3 * A_WIDTH
ZB_W = 3 * CONV_CH
ZC_W = A_WIDTH + 2 * C_KV_WIDTH
IN_WIDTH = ZA_W + ZB_W + ZC_W
MIX_WIDTH = A_WIDTH + CONV_CH + A_WIDTH
DILATIONS = (1, 4, 16)
A_MAX_DIST = 128
C_MAX_DIST = 127
TQ = 128
EPS = 1e-6
SCALE = HEAD_DIM ** -0.5
NEG = -1e30
HALO = 8

ADAM_LR = 0.001
ADAM_B1 = 0.9
ADAM_B2 = 0.999
ADAM_EPS = 1e-08
ADAM_WD = 0.01
ADAM_STEP = 10

BF = jnp.bfloat16
F32 = jnp.float32
MESH = pl.DeviceIdType.MESH
VMEM_LIMIT = 56 * 1024 * 1024


def _cparams(*sem):
    return pltpu.CompilerParams(dimension_semantics=sem, vmem_limit_bytes=VMEM_LIMIT)


def _nt(a, b):
    return lax.dot_general(a, b, (((1,), (1,)), ((), ())), preferred_element_type=F32)


def _tn(a, b):
    return lax.dot_general(a, b, (((0,), (0,)), ((), ())), preferred_element_type=F32)


def _nn(a, b):
    return jnp.dot(a, b, preferred_element_type=F32)


def _rows(tb, w):
    return pl.BlockSpec((tb, w), lambda i: (i, 0))


def _whole(shape):
    return pl.BlockSpec(shape, lambda *_: (0,) * len(shape))


def _layer(shape, l):
    return pl.BlockSpec((None,) + shape, lambda *_: (l,) + (0,) * len(shape))


def _rms_scale(v):
    return lax.rsqrt(jnp.mean(v * v, axis=-1, keepdims=True) + EPS)


def _norm_bwd(dxhat, xhat, r):
    return r * (dxhat - xhat * jnp.mean(dxhat * xhat, axis=-1, keepdims=True))


def _qkv_fwd(x, g, w_all, l, tb):
    s, d = x.shape

    def body(x_ref, g_ref, w_ref, h_ref, za_ref, zb_ref, zc_ref):
        xv = x_ref[...]
        h = ((xv * _rms_scale(xv)) * g_ref[...]).astype(BF)
        h_ref[...] = h
        z = jnp.concatenate([_nn(h, w_ref[k]) for k in range(N_CHIPS)], axis=1)
        za_ref[...] = z[:, :ZA_W]
        zb_ref[...] = z[:, ZA_W:ZA_W + ZB_W]
        zc_ref[...] = z[:, ZA_W + ZB_W:]

    return pl.pallas_call(
        body, grid=(s // tb,), name="qkv_fwd",
        in_specs=[_rows(tb, d), _whole((1, d)), _layer((N_CHIPS, d, IN_WIDTH // N_CHIPS), l)],
        out_specs=[_rows(tb, d), _rows(tb, ZA_W), _rows(tb, ZB_W), _rows(tb, ZC_W)],
        out_shape=[jax.ShapeDtypeStruct((s, d), BF), jax.ShapeDtypeStruct((s, ZA_W), F32),
                   jax.ShapeDtypeStruct((s, ZB_W), F32), jax.ShapeDtypeStruct((s, ZC_W), F32)],
        compiler_params=_cparams("parallel"),
    )(x, g, w_all)


N_STRIPS = 16


def _strips(a):
    s, w = a.shape
    return a.reshape(4, 4, s // N_STRIPS, w)


P_ROWS = {16: TQ, 4: 32, 1: 8}


def _p_sub(s, dil, most):
    while (s // dil // TQ) % most:
        most //= 2
    return most


def _p_grid(s, dil, n_sub):
    nb = s // dil // TQ // n_sub
    return {16: (4, 4, nb), 4: (4, nb), 1: (nb,)}[dil]


def _p_spec(dil, cw, col, n_sub, prev=False):
    rows = P_ROWS[dil] * (1 if prev else n_sub)

    def blk(j):
        return jnp.maximum(n_sub * j - 1, 0) if prev else j
    if dil == 16:
        return pl.BlockSpec((None, None, rows, cw), lambda f, e, j: (f, e, blk(j), col))
    if dil == 4:
        return pl.BlockSpec((None, 4, rows, cw), lambda f, j: (f, 0, blk(j), col))
    return pl.BlockSpec((4, 4, rows, cw), lambda j: (0, 0, blk(j), col))


def _block_pos(i, dil):
    if dil == 16:
        return i
    if dil == 4:
        return 4 * (i % 32) + i // 32
    return 16 * (i % 8) + 4 * ((i // 8) % 4) + i // 32


def _band_mask(b, dil, max_dist):
    qi = _block_pos(lax.broadcasted_iota(jnp.int32, (TQ, 2 * TQ), 0), dil)
    col = lax.broadcasted_iota(jnp.int32, (TQ, 2 * TQ), 1)
    cur = col >= TQ
    dist = qi - _block_pos(col % TQ, dil) + jnp.where(cur, 0, TQ)
    return (dist >= 0) & (dist <= max_dist) & (cur | (b > 0))


def _hs(h):
    return slice(h * HEAD_DIM, (h + 1) * HEAD_DIM)


def _ld(ref, cols, rows=slice(None)):
    v = ref[..., rows, cols]
    return v.reshape(TQ, v.shape[-1])


def _st(ref, cols, val, rows=slice(None)):
    lead = ref.shape[:-2] + (ref.shape[-2] if rows == slice(None) else rows.stop - rows.start,)
    ref[..., rows, cols] = val.reshape(lead + (val.shape[-1],))


def _attn_fwd(z, dil, kw, kcol, vcol, n_rep, max_dist, name):
    s, zw = z.shape
    n_sub = _p_sub(s, dil, 2)
    grid = _p_grid(s, dil, n_sub)
    o_dt = F32 if P_ROWS[dil] * n_sub < 16 else BF

    def body(q_ref, kp_ref, kc_ref, vp_ref, vc_ref, o_ref, lse_ref):
        for t in range(n_sub):
            rows = slice(t * P_ROWS[dil], (t + 1) * P_ROWS[dil])
            before = (slice(None),) if t == 0 else (slice((t - 1) * P_ROWS[dil], t * P_ROWS[dil]),)
            kb_ref, vb_ref = (kp_ref, vp_ref) if t == 0 else (kc_ref, vc_ref)
            mask = _band_mask(n_sub * pl.program_id(len(grid) - 1) if t == 0 else 1, dil, max_dist)
            scs, v2s = [], []
            for kh in range(N_HEADS // n_rep):
                k2 = jnp.concatenate([_ld(kb_ref, _hs(kh), *before), _ld(kc_ref, _hs(kh), rows)], axis=0).astype(BF)
                v2s.append(jnp.concatenate([_ld(vb_ref, _hs(kh), *before), _ld(vc_ref, _hs(kh), rows)],
                                           axis=0).astype(BF))
                for h in range(kh * n_rep, (kh + 1) * n_rep):
                    q = (_ld(q_ref, _hs(h), rows) * SCALE).astype(BF)
                    scs.append(jnp.where(mask, _nt(q, k2), NEG))
            for h, sc in enumerate(scs):
                m = jnp.max(sc, axis=1, keepdims=True)
                p = jnp.exp(sc - m)
                l = jnp.sum(p, axis=1, keepdims=True)
                _st(o_ref, _hs(h), (_nn(p.astype(BF), v2s[h // n_rep]) / l).astype(o_ref.dtype), rows)
                _st(lse_ref, _hs(h), jnp.broadcast_to(m + jnp.log(l), (TQ, HEAD_DIM)), rows)

    res = pl.pallas_call(
        body, grid=grid, name=name,
        in_specs=[_p_spec(dil, A_WIDTH, 0, n_sub), _p_spec(dil, kw, kcol, n_sub, True), _p_spec(dil, kw, kcol, n_sub),
                  _p_spec(dil, kw, vcol, n_sub, True), _p_spec(dil, kw, vcol, n_sub)],
        out_specs=[_p_spec(dil, A_WIDTH, 0, n_sub)] * 2,
        out_shape=[jax.ShapeDtypeStruct((4, 4, s // N_STRIPS, A_WIDTH), dt) for dt in (o_dt, F32)],
        compiler_params=_cparams(*(("parallel",) * len(grid))),
    )(*[_strips(z)] * 5)
    return [a.reshape(s, A_WIDTH) for a in res]


def _attn_merge(parts_a, part_c, sink_row, tb):
    s = part_c[0].shape[0]
    n_a = len(parts_a)

    def body(*refs):
        ins, sink_ref = refs[:2 * n_a + 2], refs[2 * n_a + 2]
        ya_ref, lsea_ref, yc_ref, lsec_ref = refs[2 * n_a + 3:]
        lses = [ins[2 * p + 1][...] for p in range(n_a)]
        m = functools.reduce(jnp.maximum, lses)
        ws = [jnp.exp(v - m) for v in lses]
        l = functools.reduce(jnp.add, ws)
        ya_ref[...] = functools.reduce(jnp.add, [w * ins[2 * p][...].astype(F32) for p, w in enumerate(ws)]) / l
        lsea_ref[...] = m + jnp.log(l)
        o_c, lse_c = [r[...].astype(F32) for r in ins[2 * n_a:]]
        sk = sink_ref[...]
        m2 = jnp.maximum(lse_c, sk)
        w = jnp.exp(lse_c - m2)
        l2 = w + jnp.exp(sk - m2)
        yc_ref[...] = o_c * (w / l2)
        lsec_ref[...] = m2 + jnp.log(l2)

    return pl.pallas_call(
        body, grid=(s // tb,), name="attn_merge",
        in_specs=[_rows(tb, A_WIDTH)] * (2 * n_a + 2) + [_whole((1, A_WIDTH))],
        out_specs=[_rows(tb, A_WIDTH)] * 4, out_shape=[jax.ShapeDtypeStruct((s, A_WIDTH), F32)] * 4,
        compiler_params=_cparams("parallel"),
    )(*[a for part in parts_a + [part_c] for a in part], sink_row)


def _shift_down(v, n, halo):
    rows = v.shape[0]
    out = pltpu.roll(v, n, 0)
    row = lax.broadcasted_iota(jnp.int32, v.shape, 0)
    for t in range(n):
        out = jnp.where(row == t, halo[HALO - n + t:HALO - n + t + 1, :], out)
    return out


def _shift_up(v, n, halo):
    rows = v.shape[0]
    out = pltpu.roll(v, rows - n, 0)
    row = lax.broadcasted_iota(jnp.int32, v.shape, 0)
    for t in range(n):
        out = jnp.where(row == rows - n + t, halo[t:t + 1, :], out)
    return out


def _strip(v, b):
    return v[b % 4, b // 4]


def _conv_strips(zb, prev, cw):
    gb = [_strip(zb, b)[:, :CONV_CH] for b in range(N_STRIPS)]
    gc = [_strip(zb, b)[:, CONV_CH:2 * CONV_CH] for b in range(N_STRIPS)]
    xb = [_strip(zb, b)[:, 2 * CONV_CH:] for b in range(N_STRIPS)]
    u = [g * v for g, v in zip(gc, xb)]
    uh = prev[:, :, CONV_CH:2 * CONV_CH] * prev[:, :, 2 * CONV_CH:]
    wrapped = {14: _shift_down(u[14], 1, uh[2]), 15: _shift_down(u[15], 1, uh[3])}
    u1 = [u[b - 1] if b >= 1 else wrapped[15] for b in range(N_STRIPS)]
    u2 = [u[b - 2] if b >= 2 else wrapped[14 + b] for b in range(N_STRIPS)]
    c = [cw[0:1, :] * u2[b] + cw[1:2, :] * u1[b] + cw[2:3, :] * u[b] for b in range(N_STRIPS)]
    return gb, gc, xb, u, u1, u2, c


def _strip_rows(ta, w):
    return pl.BlockSpec((4, 4, ta, w), lambda i: (0, 0, i, 0))


def _prev_rows(ta, w):
    return pl.BlockSpec((4, None, HALO, w), lambda i: (0, 3, jnp.maximum(i * (ta // HALO) - 1, 0), 0))


def _next_rows(ta, w, nblk):
    return pl.BlockSpec((4, None, HALO, w),
                        lambda i: (0, 0, jnp.minimum((i + 1) * (ta // HALO), nblk * (ta // HALO) - 1), 0))


def _mix_fwd(x, ya, yc, zb, cw, gg, wo_all, l, tb):
    s, d = x.shape
    ta = tb // N_STRIPS

    def body(x_ref, ya_ref, yc_ref, zb_ref, zbp_ref, cw_ref, gg_ref, wo_ref, x1_ref, yb_ref):
        i = pl.program_id(0)
        prev = jnp.where(i > 0, zbp_ref[...], 0.0)
        gb, _, _, _, _, _, c = _conv_strips(zb_ref[...], prev, cw_ref[...])
        for b in range(N_STRIPS):
            yb_ref[b % 4, b // 4] = gb[b] * c[b]
        yb = yb_ref[...].reshape(tb, CONV_CH)
        ya, yc = ya_ref[...].reshape(tb, A_WIDTH), yc_ref[...].reshape(tb, A_WIDTH)
        n = jnp.concatenate([ya * _rms_scale(ya), yb * _rms_scale(yb), yc * _rms_scale(yc)], axis=1)
        n = (n * gg_ref[...]).astype(BF)
        x1 = x_ref[...].reshape(tb, d) + _nn(n, wo_ref[...].reshape(MIX_WIDTH, d))
        x1_ref[...] = x1.reshape(4, 4, ta, d)

    res = pl.pallas_call(
        body, grid=(s // tb,), name="mix_fwd",
        in_specs=[_strip_rows(ta, d), _strip_rows(ta, A_WIDTH), _strip_rows(ta, A_WIDTH), _strip_rows(ta, ZB_W),
                  _prev_rows(ta, ZB_W), _whole((HALO, CONV_CH)), _whole((1, MIX_WIDTH)),
                  _layer((N_CHIPS, MIX_WIDTH // N_CHIPS, d), l)],
        out_specs=[_strip_rows(ta, d), _strip_rows(ta, CONV_CH)],
        out_shape=[jax.ShapeDtypeStruct((4, 4, s // N_STRIPS, d), F32),
                   jax.ShapeDtypeStruct((4, 4, s // N_STRIPS, CONV_CH), F32)],
        compiler_params=_cparams("parallel"),
    )(_strips(x), _strips(ya), _strips(yc), _strips(zb), _strips(zb), cw, gg, wo_all)
    return res[0].reshape(s, d), res[1].reshape(s, CONV_CH)


def _mlp_fwd(x1, g, w1_all, w2_all, l, tb, tf):
    s, d = x1.shape
    ff = w1_all.shape[1] * w1_all.shape[3]
    nj = ff // tf

    def body(x_ref, g_ref, w1_ref, w2_ref, x2_ref, h2_ref, ap_ref, acc):
        j = pl.program_id(1)

        @pl.when(j == 0)
        def _():
            xv = x_ref[...]
            h2_ref[...] = ((xv * _rms_scale(xv)) * g_ref[...]).astype(BF)
            acc[...] = jnp.zeros_like(acc)

        ap = _nn(h2_ref[...], w1_ref[...])
        ap_ref[...] = ap.astype(BF)
        a = jnp.square(jnp.maximum(ap, 0.0)).astype(BF)
        acc[...] += _nn(a, w2_ref[...])

        @pl.when(j == nj - 1)
        def _():
            x2_ref[...] = x_ref[...] + acc[...]

    return pl.pallas_call(
        body, grid=(s // tb, nj), name="mlp_fwd",
        in_specs=[pl.BlockSpec((tb, d), lambda i, j: (i, 0)), _whole((1, d)),
                  pl.BlockSpec((None, None, d, tf), lambda i, j: (l, j, 0, 0)),
                  pl.BlockSpec((None, None, tf, d), lambda i, j: (l, j, 0, 0))],
        out_specs=[pl.BlockSpec((tb, d), lambda i, j: (i, 0)), pl.BlockSpec((tb, d), lambda i, j: (i, 0)),
                   pl.BlockSpec((tb, tf), lambda i, j: (i, j))],
        out_shape=[jax.ShapeDtypeStruct((s, d), F32), jax.ShapeDtypeStruct((s, d), BF),
                   jax.ShapeDtypeStruct((s, ff), BF)],
        scratch_shapes=[pltpu.VMEM((tb, d), F32)],
        compiler_params=_cparams("parallel", "arbitrary"),
    )(x1, g, w1_all, w2_all)


def _loss_head(x, g, tgt, tb):
    s, d = x.shape

    def body(x_ref, g_ref, t_ref, dx_ref, loss_ref, dg_ref):
        i = pl.program_id(0)

        @pl.when(i == 0)
        def _():
            loss_ref[...] = jnp.zeros_like(loss_ref)
            dg_ref[...] = jnp.zeros_like(dg_ref)

        xv = x_ref[...]
        r = _rms_scale(xv)
        xhat = xv * r
        err = xhat * g_ref[...] - t_ref[...]
        part = jnp.sum(jnp.mean(jnp.square(err), axis=-1, keepdims=True), axis=0, keepdims=True)
        loss_ref[...] += 0.5 * part
        dy = err * (1.0 / d)
        dg_ref[...] += jnp.sum(dy * xhat, axis=0, keepdims=True)
        dx_ref[...] = _norm_bwd(dy * g_ref[...], xhat, r)

    return pl.pallas_call(
        body, grid=(s // tb,), name="loss_head",
        in_specs=[_rows(tb, d), _whole((1, d)), _rows(tb, d)],
        out_specs=[_rows(tb, d), _whole((HALO, 128)), _whole((HALO, d))],
        out_shape=[jax.ShapeDtypeStruct((s, d), F32), jax.ShapeDtypeStruct((HALO, 128), F32),
                   jax.ShapeDtypeStruct((HALO, d), F32)],
        compiler_params=_cparams("arbitrary"),
    )(x, g, tgt)


def _mlp_bwd(dx2, x1, ap, g, w1_all, w2_all, l, tb, tf):
    s, d = x1.shape
    ff = ap.shape[1]
    nj = ff // tf

    def body(dx2_ref, x1_ref, ap_ref, g_ref, w1_ref, w2_ref, dx1_ref, dap_ref, dg_ref, acc):
        i, j = pl.program_id(0), pl.program_id(1)

        @pl.when((i == 0) & (j == 0))
        def _():
            dg_ref[...] = jnp.zeros_like(dg_ref)

        @pl.when(j == 0)
        def _():
            acc[...] = jnp.zeros_like(acc)

        da = _nt(dx2_ref[...].astype(BF), w2_ref[...])
        dap = (da * (2.0 * jnp.maximum(ap_ref[...].astype(F32), 0.0))).astype(BF)
        dap_ref[...] = dap
        acc[...] += _nt(dap, w1_ref[...])

        @pl.when(j == nj - 1)
        def _():
            xv = x1_ref[...]
            r = _rms_scale(xv)
            xhat = xv * r
            dh = acc[...]
            dg_ref[...] += jnp.sum(dh * xhat, axis=0, keepdims=True)
            dx1_ref[...] = dx2_ref[...] + _norm_bwd(dh * g_ref[...], xhat, r)

    return pl.pallas_call(
        body, grid=(s // tb, nj), name="mlp_bwd",
        in_specs=[pl.BlockSpec((tb, d), lambda i, j: (i, 0)), pl.BlockSpec((tb, d), lambda i, j: (i, 0)),
                  pl.BlockSpec((tb, tf), lambda i, j: (i, j)),
                  _whole((1, d)), pl.BlockSpec((None, None, d, tf), lambda i, j: (l, j, 0, 0)),
                  pl.BlockSpec((None, None, tf, d), lambda i, j: (l, j, 0, 0))],
        out_specs=[pl.BlockSpec((tb, d), lambda i, j: (i, 0)), pl.BlockSpec((tb, tf), lambda i, j: (i, j)),
                   _whole((HALO, d))],
        out_shape=[jax.ShapeDtypeStruct((s, d), F32), jax.ShapeDtypeStruct((s, ff), BF),
                   jax.ShapeDtypeStruct((HALO, d), F32)],
        scratch_shapes=[pltpu.VMEM((tb, d), F32)],
        compiler_params=_cparams("arbitrary", "arbitrary"),
    )(dx2, x1, ap, g, w1_all, w2_all)


def _wgrad(a, b, tm, tn, ts, name, relu2=False):
    s, m = a.shape
    n = b.shape[1]
    ns = s // ts

    def body(a_ref, b_ref, o_ref, acc):
        k = pl.program_id(2)

        @pl.when(k == 0)
        def _():
            acc[...] = jnp.zeros_like(acc)

        av = a_ref[...]
        if relu2:
            av = jnp.square(jnp.maximum(av.astype(F32), 0.0)).astype(BF)
        acc[...] += _tn(av, b_ref[...].astype(BF))

        @pl.when(k == ns - 1)
        def _():
            o_ref[...] = acc[...].astype(BF)

    return pl.pallas_call(
        body, grid=(m // tm, n // tn, ns), name=name,
        in_specs=[pl.BlockSpec((ts, tm), lambda i, j, k: (k, i)), pl.BlockSpec((ts, tn), lambda i, j, k: (k, j))],
        out_specs=pl.BlockSpec((tm, tn), lambda i, j, k: (i, j)),
        out_shape=jax.ShapeDtypeStruct((m, n), BF),
        scratch_shapes=[pltpu.VMEM((tm, tn), F32)],
        compiler_params=_cparams("parallel", "parallel", "arbitrary"),
    )(a, b)


def _mix_bwd(dx1, ya, yb, yc, lse_c, sink_row, gg, wo_all, l, tb):
    s, d = dx1.shape

    def body(dx_ref, ya_ref, yb_ref, yc_ref, lse_ref, sink_ref, gg_ref, wo_ref,
             n_ref, dya_ref, dyc_ref, da_ref, dc_ref, dyb_ref, dg_ref, dsink_ref):
        i = pl.program_id(0)

        @pl.when(i == 0)
        def _():
            dg_ref[...] = jnp.zeros_like(dg_ref)
            dsink_ref[...] = jnp.zeros_like(dsink_ref)

        dn = _nt(dx_ref[...].astype(BF), wo_ref[...].reshape(MIX_WIDTH, d))
        ys = [ya_ref[...], yb_ref[...], yc_ref[...]]
        rs = [_rms_scale(v) for v in ys]
        nhat = jnp.concatenate([v * r for v, r in zip(ys, rs)], axis=1)
        gg = gg_ref[...]
        n_ref[...] = (nhat * gg).astype(BF)
        dg_ref[...] += jnp.sum(dn * nhat, axis=0, keepdims=True)
        dnh = dn * gg
        bounds = [(0, A_WIDTH), (A_WIDTH, A_WIDTH + CONV_CH), (A_WIDTH + CONV_CH, MIX_WIDTH)]
        dys = [_norm_bwd(dnh[:, lo:hi], nhat[:, lo:hi], r) for (lo, hi), r in zip(bounds, rs)]
        dyb_ref[...] = dys[1]
        head = [lax.broadcasted_iota(jnp.int32, (A_WIDTH, A_WIDTH), k) // HEAD_DIM for k in (0, 1)]
        ones = (head[0] == head[1]).astype(BF)
        for dy, y, dy_ref, dd_ref in ((dys[0], ys[0], dya_ref, da_ref), (dys[2], ys[2], dyc_ref, dc_ref)):
            dy_ref[...] = dy
            t = dy * y
            hi = t.astype(BF)
            dd_ref[...] = _nn(hi, ones) + _nn((t - hi.astype(F32)).astype(BF), ones)
        dsink_ref[...] -= jnp.sum(jnp.exp(sink_ref[...] - lse_ref[...]) * dc_ref[...], axis=0, keepdims=True)

    return pl.pallas_call(
        body, grid=(s // tb,), name="mix_bwd",
        in_specs=[_rows(tb, d), _rows(tb, A_WIDTH), _rows(tb, CONV_CH), _rows(tb, A_WIDTH), _rows(tb, A_WIDTH),
                  _whole((1, A_WIDTH)), _whole((1, MIX_WIDTH)), _layer((N_CHIPS, MIX_WIDTH // N_CHIPS, d), l)],
        out_specs=[_rows(tb, MIX_WIDTH), _rows(tb, A_WIDTH), _rows(tb, A_WIDTH), _rows(tb, A_WIDTH),
                   _rows(tb, A_WIDTH), _rows(tb, CONV_CH), _whole((HALO, MIX_WIDTH)), _whole((HALO, A_WIDTH))],
        out_shape=[jax.ShapeDtypeStruct((s, MIX_WIDTH), BF), jax.ShapeDtypeStruct((s, A_WIDTH), F32),
                   jax.ShapeDtypeStruct((s, A_WIDTH), F32), jax.ShapeDtypeStruct((s, A_WIDTH), F32),
                   jax.ShapeDtypeStruct((s, A_WIDTH), F32), jax.ShapeDtypeStruct((s, CONV_CH), F32),
                   jax.ShapeDtypeStruct((HALO, MIX_WIDTH), F32), jax.ShapeDtypeStruct((HALO, A_WIDTH), F32)],
        compiler_params=_cparams("arbitrary"),
    )(dx1, ya, yb, yc, lse_c, sink_row, gg, wo_all)


def _attn_bwd(z, dy, lse, dd, dil, kw, kcol, vcol, n_rep, max_dist, name):
    s, zw = z.shape
    n_sub = _p_sub(s, dil, 2) if n_rep == 1 else 1
    grid = _p_grid(s, dil, n_sub)
    n_kv = N_HEADS // n_rep
    dt = F32 if P_ROWS[dil] * n_sub < 16 else BF

    def body(q_ref, kp_ref, kc_ref, vp_ref, vc_ref, dy_ref, lse_ref, dd_ref, dq_ref, dkp_ref, dkc_ref, dvp_ref, dvc_ref):
        for t in range(n_sub):
            rows = slice(t * P_ROWS[dil], (t + 1) * P_ROWS[dil])
            before = (slice(None),) if t == 0 else (slice((t - 1) * P_ROWS[dil], t * P_ROWS[dil]),)
            kb_ref, vb_ref = (kp_ref, vp_ref) if t == 0 else (kc_ref, vc_ref)
            mask = _band_mask(n_sub * pl.program_id(len(grid) - 1) if t == 0 else 1, dil, max_dist)
            k2s, qs, dys, scs, dps = [], [], [], [], []
            for kh in range(n_kv):
                k2s.append(jnp.concatenate([_ld(kb_ref, _hs(kh), *before), _ld(kc_ref, _hs(kh), rows)],
                                           axis=0).astype(BF))
                v2 = jnp.concatenate([_ld(vb_ref, _hs(kh), *before), _ld(vc_ref, _hs(kh), rows)], axis=0).astype(BF)
                for h in range(kh * n_rep, (kh + 1) * n_rep):
                    qs.append((_ld(q_ref, _hs(h), rows) * SCALE).astype(BF))
                    dys.append(_ld(dy_ref, _hs(h), rows).astype(BF))
                    scs.append(jnp.where(mask, _nt(qs[h], k2s[kh]), NEG))
                    dps.append(_nt(dys[h], v2))
            for kh in range(n_kv):
                k2 = k2s[kh]
                dk2 = jnp.zeros((2 * TQ, HEAD_DIM), F32)
                dv2 = jnp.zeros((2 * TQ, HEAD_DIM), F32)
                for h in range(kh * n_rep, (kh + 1) * n_rep):
                    lse_h = _ld(lse_ref, slice(h * HEAD_DIM, h * HEAD_DIM + 1), rows)
                    dd_h = _ld(dd_ref, slice(h * HEAD_DIM, h * HEAD_DIM + 1), rows)
                    p = jnp.exp(scs[h] - lse_h)
                    ds = (p * (dps[h] - dd_h)).astype(BF)
                    _st(dq_ref, _hs(h), (_nn(ds, k2) * SCALE).astype(dt), rows)
                    dk2 = dk2 + _tn(ds, qs[h])
                    dv2 = dv2 + _tn(p.astype(BF), dys[h])
                _st(dkp_ref, _hs(kh), dk2[:TQ].astype(dt), rows)
                _st(dkc_ref, _hs(kh), dk2[TQ:].astype(dt), rows)
                _st(dvp_ref, _hs(kh), dv2[:TQ].astype(dt), rows)
                _st(dvc_ref, _hs(kh), dv2[TQ:].astype(dt), rows)

    args = [_strips(z)] * 5 + [_strips(a) for a in (dy, lse, dd)]
    pair = _p_spec(dil, A_WIDTH, 0, n_sub)
    in_specs = [pair, _p_spec(dil, kw, kcol, n_sub, True), _p_spec(dil, kw, kcol, n_sub),
                _p_spec(dil, kw, vcol, n_sub, True), _p_spec(dil, kw, vcol, n_sub)] + [pair] * 3
    out_specs = [pair] + [_p_spec(dil, kw, 0, n_sub)] * 4
    na = s // N_STRIPS
    out_shape = [jax.ShapeDtypeStruct((4, 4, na, A_WIDTH), dt)] + [jax.ShapeDtypeStruct((4, 4, na, kw), dt)] * 4
    res = pl.pallas_call(
        body, grid=grid, name=name, in_specs=in_specs, out_specs=out_specs, out_shape=out_shape,
        compiler_params=_cparams(*(("parallel",) * len(grid))),
    )(*args)
    return [res[0].reshape(s, A_WIDTH)] + [a.reshape(s, kw) for a in res[1:]]


DZ_TA = 16


def _dz_assemble(parts_a, parts_c, dyb, zb, cw):
    s = zb.shape[0]
    na = s // N_STRIPS
    nb = na // DZ_TA

    def ahead(w, k):
        return pl.BlockSpec((4, 4, DZ_TA, w), lambda i: (0, 0, jnp.minimum(i + k, nb - 1), 0))

    args, in_specs = [], []
    for dil, (dq, dkp, dkc, dvp, dvc) in zip(DILATIONS + (1,), parts_a + [parts_c]):
        w = dkp.shape[1]
        here = _strip_rows(DZ_TA, w)
        if dil == 1:
            args += [dq, dkp, dkp, dkc, dvp, dvp, dvc]
            in_specs += [_strip_rows(DZ_TA, A_WIDTH), here, ahead(w, 1), here, here, ahead(w, 1), here]
        else:
            k = 8 * dil // DZ_TA
            args += [dq, dkp, dkc, dvp, dvc]
            in_specs += [_strip_rows(DZ_TA, A_WIDTH), ahead(w, k), here, ahead(w, k), here]
    n_att = len(args)
    args = [_strips(a) for a in args] + [_strips(dyb), _strips(dyb), _strips(zb), _strips(zb), _strips(zb), cw]
    in_specs += [_strip_rows(DZ_TA, CONV_CH), _next_rows(DZ_TA, CONV_CH, nb), _strip_rows(DZ_TA, ZB_W),
                 _prev_rows(DZ_TA, ZB_W), _next_rows(DZ_TA, ZB_W, nb), _whole((HALO, CONV_CH))]

    def body(*refs):
        att = list(refs[:n_att])
        dyb_ref, dybn_ref, zb_ref, zbp_ref, zbn_ref, cw_ref, dz_ref, dcw_ref = refs[n_att:]
        i = pl.program_id(0)

        @pl.when(i == 0)
        def _():
            dcw_ref[...] = jnp.zeros_like(dcw_ref)

        def shifted(dil):
            if dil == 1:
                dq_r, kp0, kp1, dkc_r, vp0, vp1, dvc_r = [att.pop(0) for _ in range(7)]
                live = i + 1 < nb
                half = DZ_TA // 2
                kp0, kp1, vp0, vp1 = [r[...].astype(F32) for r in (kp0, kp1, vp0, vp1)]
                dkp = jnp.concatenate([kp0[:, :, half:, :], jnp.where(live, kp1[:, :, :half, :], 0.0)], axis=2)
                dvp = jnp.concatenate([vp0[:, :, half:, :], jnp.where(live, vp1[:, :, :half, :], 0.0)], axis=2)
            else:
                dq_r, dkp_r, dkc_r, dvp_r, dvc_r = [att.pop(0) for _ in range(5)]
                live = i + 8 * dil // DZ_TA < nb
                dkp = jnp.where(live, dkp_r[...].astype(F32), 0.0)
                dvp = jnp.where(live, dvp_r[...].astype(F32), 0.0)
            return dq_r[...].astype(F32), dkc_r[...].astype(F32) + dkp, dvc_r[...].astype(F32) + dvp

        dq, dk, dv = shifted(DILATIONS[0])
        for dil in DILATIONS[1:]:
            dq2, dk2, dv2 = shifted(dil)
            dq, dk, dv = dq + dq2, dk + dk2, dv + dv2
        dz_ref[:, :, :, 0:A_WIDTH] = dq.astype(BF)
        dz_ref[:, :, :, A_WIDTH:2 * A_WIDTH] = dk.astype(BF)
        dz_ref[:, :, :, 2 * A_WIDTH:ZA_W] = dv.astype(BF)
        dq, dk, dv = shifted(1)
        c0 = ZA_W + ZB_W
        dz_ref[:, :, :, c0:c0 + A_WIDTH] = dq.astype(BF)
        dz_ref[:, :, :, c0 + A_WIDTH:c0 + A_WIDTH + C_KV_WIDTH] = dk.astype(BF)
        dz_ref[:, :, :, c0 + A_WIDTH + C_KV_WIDTH:IN_WIDTH] = dv.astype(BF)

        cw = cw_ref[...]
        prev = jnp.where(i > 0, zbp_ref[...], 0.0)
        gb, gc, xb, u, u1, u2, c = _conv_strips(zb_ref[...], prev, cw)
        dyb = dyb_ref[...]
        dc = [_strip(dyb, b) * gb[b] for b in range(N_STRIPS)]
        dcn = jnp.where(i + 1 < nb, dybn_ref[...] * zbn_ref[:, :, :CONV_CH], 0.0)
        wrapped = [_shift_up(dc[0], 1, dcn[0]), _shift_up(dc[1], 1, dcn[1])]
        upd = [jnp.zeros((1, CONV_CH), F32)] * 3
        for b in range(N_STRIPS):
            dc1 = dc[b + 1] if b + 1 < N_STRIPS else wrapped[0]
            dc2 = dc[b + 2] if b + 2 < N_STRIPS else wrapped[b + 2 - N_STRIPS]
            du = cw[2:3, :] * dc[b] + cw[1:2, :] * dc1 + cw[0:1, :] * dc2
            f, e = b % 4, b // 4
            dz_ref[f, e, :, ZA_W:ZA_W + CONV_CH] = (_strip(dyb, b) * c[b]).astype(BF)
            dz_ref[f, e, :, ZA_W + CONV_CH:ZA_W + 2 * CONV_CH] = (du * xb[b]).astype(BF)
            dz_ref[f, e, :, ZA_W + 2 * CONV_CH:c0] = (du * gc[b]).astype(BF)
            for t, uu in enumerate((u2[b], u1[b], u[b])):
                upd[t] = upd[t] + jnp.sum(dc[b] * uu, axis=0, keepdims=True)
        row = lax.broadcasted_iota(jnp.int32, (HALO, CONV_CH), 0)
        tile = jnp.zeros((HALO, CONV_CH), F32)
        for t in range(3):
            tile = jnp.where(row == t, upd[t], tile)
        dcw_ref[...] += tile

    dz, dcw = pl.pallas_call(
        body, grid=(nb,), name="dz_assemble", in_specs=in_specs,
        out_specs=[_strip_rows(DZ_TA, IN_WIDTH), _whole((HALO, CONV_CH))],
        out_shape=[jax.ShapeDtypeStruct((4, 4, na, IN_WIDTH), BF), jax.ShapeDtypeStruct((HALO, CONV_CH), F32)],
        compiler_params=_cparams("arbitrary"),
    )(*args)
    return dz.reshape(s, IN_WIDTH), dcw


def _qkv_bwd(dz, dx1, x, g, w_all, l, tb, tokens_out):
    s, d = x.shape
    na, ta = s // N_STRIPS, tb // N_STRIPS

    def body(dz_ref, dx1_ref, x_ref, g_ref, w_ref, dx_ref, dg_ref):
        i = pl.program_id(0)

        @pl.when(i == 0)
        def _():
            dg_ref[...] = jnp.zeros_like(dg_ref)

        n = IN_WIDTH // N_CHIPS
        dz = dz_ref[...].reshape(tb, IN_WIDTH)
        dh = _nt(dz[:, 0:n], w_ref[0])
        for k in range(1, N_CHIPS):
            dh = dh + _nt(dz[:, k * n:(k + 1) * n], w_ref[k])
        xv = x_ref[...].reshape(tb, d)
        r = _rms_scale(xv)
        xhat = xv * r
        dg_ref[...] += jnp.sum(dh * xhat, axis=0, keepdims=True)
        dx = (dx1_ref[...].reshape(tb, d) + _norm_bwd(dh * g_ref[...], xhat, r)).reshape(4, 4, ta, d)
        if tokens_out:
            for b in range(N_STRIPS):
                dx_ref[:, b, :] = _strip(dx, b)
        else:
            dx_ref[...] = dx

    if tokens_out:
        dx_spec, dx_shape = pl.BlockSpec((ta, N_STRIPS, d), lambda i: (i, 0, 0)), (na, N_STRIPS, d)
    else:
        dx_spec, dx_shape = _strip_rows(ta, d), (4, 4, na, d)
    dx, dg = pl.pallas_call(
        body, grid=(s // tb,), name="qkv_bwd",
        in_specs=[_strip_rows(ta, IN_WIDTH), _strip_rows(ta, d), _strip_rows(ta, d), _whole((1, d)),
                  _layer((N_CHIPS, d, IN_WIDTH // N_CHIPS), l)],
        out_specs=[dx_spec, _whole((HALO, d))],
        out_shape=[jax.ShapeDtypeStruct(dx_shape, F32), jax.ShapeDtypeStruct((HALO, d), F32)],
        compiler_params=_cparams("arbitrary"),
    )(_strips(dz), _strips(dx1), _strips(x), g, w_all)
    return dx.reshape(s, d), dg


def _tile_rows(rows):
    return jnp.pad(rows, ((0, HALO - rows.shape[0]), (0, 0)))


def _to_strips(a, after, name):
    s, d = a.shape
    na = s // N_STRIPS
    ta = min(32, na)

    def body(a_ref, *rest):
        for b in range(N_STRIPS):
            rest[-1][b % 4, b // 4] = a_ref[:, b, :]

    return pl.pallas_call(
        body, grid=(na // ta,), name=name,
        in_specs=[pl.BlockSpec((ta, N_STRIPS, d), lambda i: (i, 0, 0))] + [ANY] * len(after),
        out_specs=_strip_rows(ta, d),
        out_shape=jax.ShapeDtypeStruct((4, 4, na, d), a.dtype), compiler_params=_cparams("parallel"),
    )(a.reshape(na, N_STRIPS, d), *after).reshape(s, d)


def _local_step(x, tgt, fetch, ff, sinks, g_mix, g_group, g_mlp, g_final, emit):
    s, d = x.shape
    depth = g_mix.shape[0]
    tb = min(512, s)
    tf = ff // N_CHIPS
    ts = min(1024, s)
    saved = []
    for l in range(depth):
        w_in, _, _, _, conv_w = fetch(0, l, x)
        cw = _tile_rows(conv_w[l])
        sk = jnp.repeat(sinks[l].reshape(N_HEADS), HEAD_DIM)[None]
        h, za, zb, zc = _qkv_fwd(x, g_mix[l][None], w_in, l, tb)
        parts_a = [_attn_fwd(za, dil, A_WIDTH, 1, 2, 1, A_MAX_DIST, "attn_a_fwd_%d" % dil) for dil in DILATIONS]
        part_c = _attn_fwd(zc, 1, C_KV_WIDTH, 3, 4, C_GROUP, C_MAX_DIST, "attn_c_fwd")
        ya, lse_a, yc, lse_c = _attn_merge(parts_a, part_c, sk, ts)
        w_in, w_o, w1, w2, _ = fetch(1, l, yc)
        x1, yb = _mix_fwd(x, ya, yc, zb, cw, g_group[l][None], w_o, l, ts)
        w_in, w_o, w1, w2, _ = fetch(2, l, x1)
        x2, h2, ap = _mlp_fwd(x1, g_mlp[l][None], w1, w2, l, ts, tf)
        saved.append((x, h, za, zb, zc, ya, lse_a, yc, lse_c, yb, x1, h2, ap, cw, sk))
        x = x2
    dx, loss_tile, dg_final = _loss_head(x, g_final[None], tgt, ts)
    grads = [None] * depth
    tok = jnp.zeros((), F32)
    for l in reversed(range(depth)):
        x0, h, za, zb, zc, ya, lse_a, yc, lse_c, yb, x1, h2, ap, cw, sk = saved[l]
        dx1, dap, dg_mlp = _mlp_bwd(dx, x1, ap, g_mlp[l][None] + tok, w1, w2, l, ts, tf)
        tok = emit(l, 3, _wgrad(ap, dx, min(1024, ff), d, 2 * ts, "wgrad_ff_out", relu2=True))
        tok = tok + emit(l, 2, _wgrad(h2, dap, d, min(1024, ff), 2 * ts, "wgrad_ff_in"))
        n, dya, dyc, dd_a, dd_c, dyb, dg_group, dsink = _mix_bwd(dx1, ya, yb, yc, lse_c, sk, g_group[l][None] + tok,
                                                                 w_o, l, tb)
        tok = emit(l, 1, _wgrad(n, dx1, MIX_WIDTH, d, ts, "wgrad_o"))
        cw = cw + tok
        parts_a = [_attn_bwd(za, dya, lse_a, dd_a, dil, A_WIDTH, 1, 2, 1, A_MAX_DIST, "attn_a_bwd_%d" % dil)
                   for dil in DILATIONS]
        parts_c = _attn_bwd(zc, dyc, lse_c, dd_c, 1, C_KV_WIDTH, 3, 4, C_GROUP, C_MAX_DIST, "attn_c_bwd")
        dz, dcw = _dz_assemble(parts_a, parts_c, dyb, zb, cw)
        tok = emit(l, 0, _wgrad(h, dz, d, IN_WIDTH // 4, 2 * ts, "wgrad_in"))
        dx, dg_mix = _qkv_bwd(dz, dx1, x0, g_mix[l][None] + tok, w_in, l, tb, l == 0)
        grads[l] = (dcw, dsink, dg_mix, dg_group, dg_mlp)
    return loss_tile, dx, grads, dg_final


ANY = pl.BlockSpec(memory_space=pl.ANY)
SHARD_AXES = (2, 1, 2, 1)
N_BIG = len(SHARD_AXES)
N_CHIPS = 4
N_DEV = 8


def _mesh_pos():
    return lax.axis_index("x"), lax.axis_index("y"), lax.axis_index("c")


def _flip(v, bit):
    return 1 - v if bit else v


def _place_shard(shard, chip_arr, name):
    _, rows, cols = shard.shape
    tr = min(256, rows)

    def body(chip_ref, x_ref, o_ref):
        o_ref[...] = x_ref[...].astype(BF)

    return pl.pallas_call(
        body, name=name,
        grid_spec=pltpu.PrefetchScalarGridSpec(
            num_scalar_prefetch=1, grid=(2, rows // tr),
            in_specs=[pl.BlockSpec((None, tr, cols), lambda l, i, chip: (l, i, 0))],
            out_specs=pl.BlockSpec((None, None, tr, cols), lambda l, i, chip: (l, chip[0], i, 0))),
        out_shape=jax.ShapeDtypeStruct((2, N_CHIPS, rows, cols), BF),
        compiler_params=_cparams("parallel", "parallel"),
    )(chip_arr, shard)


HBM = pl.BlockSpec(memory_space=pltpu.HBM)
SEM = pl.BlockSpec(memory_space=pltpu.SEMAPHORE)
EFFECT = pltpu.SideEffectType.DATAFLOW_SIDE_EFFECTING

GATHER_GROUPS = (((0, 0),), ((1, 0),), ((2, 0), (3, 0)), ((0, 1),), ((1, 1), (2, 1), (3, 1)))
GATHER_STARTS = ((0,), (1, 2), (3, 4))
GATHER_STAGES = {(0, 0): 0, (1, 0): 1, (2, 0): 2, (0, 1): 3, (1, 1): 4}


def _gather_copies(arrs, group, send_sems, recv_sems):
    x, y, c = _mesh_pos()
    me = 2 * x + y
    out = []
    for i, (w, layer) in enumerate(group):
        mine = arrs[w].at[layer, me]
        for j, (qx, qy) in enumerate([(1 - x, y), (x, 1 - y), (1 - x, 1 - y)]):
            landed = arrs[w].at[layer, 2 * qx + qy]
            out.append(tuple(pltpu.make_async_remote_copy(
                src_ref=piece, dst_ref=piece, send_sem=send_sems.at[i * 3 + j], recv_sem=recv_sems.at[i * 3 + j],
                device_id=(qx, qy, c), device_id_type=MESH) for piece in (mine, landed)))
    return out


def _conv_copies(conv_src, conv_dst, send_sems, recv_sems):
    x, y, c = _mesh_pos()
    out = []
    for j, (qx, qy) in enumerate([(1 - x, y), (x, 1 - y), (1 - x, 1 - y)]):
        out.append(tuple(pltpu.make_async_remote_copy(
            src_ref=conv_src, dst_ref=conv_dst.at[q], send_sem=send_sems.at[j], recv_sem=recv_sems.at[j],
            device_id=(qx, qy, c), device_id_type=MESH) for q in (2 * x + y, 2 * qx + qy)))
    return out


def _gather_start(groups, arrs, conv, name, through=None):
    n_sems = 2 * (len(groups) + (conv is not None))
    mats = sorted({w for g in groups for w, _ in GATHER_GROUPS[g]})

    def body(*refs):
        arrs_ref = [None] * N_BIG
        for w, ref in zip(mats, refs):
            arrs_ref[w] = ref
        sems = refs[n_in:n_in + n_sems]
        if conv is not None:
            for cp, _ in _conv_copies(refs[len(mats)], refs[len(mats) + 1], sems[-2], sems[-1]):
                cp.start()
        for k, g in enumerate(groups):
            for cp, _ in _gather_copies(arrs_ref, GATHER_GROUPS[g], sems[2 * k], sems[2 * k + 1]):
                cp.start()

    sem_shapes = []
    for n in [len(GATHER_GROUPS[g]) for g in groups] + ([1] if conv is not None else []):
        sem_shapes += [pltpu.SemaphoreType.DMA((3 * n,))] * 2
    operands = [arrs[w] for w in mats] + ([] if conv is None else list(conv)) + ([] if through is None else [through])
    n_in = len(operands)
    res = pl.pallas_call(
        body, name=name,
        out_shape=tuple(sem_shapes) + tuple(pltpu.HBM(a.shape, a.dtype) for a in operands),
        in_specs=(HBM,) * n_in, out_specs=(SEM,) * n_sems + (HBM,) * n_in,
        input_output_aliases={i: n_sems + i for i in range(n_in)},
        compiler_params=pltpu.CompilerParams(has_side_effects=EFFECT),
    )(*[pltpu.with_memory_space_constraint(a, pltpu.HBM) for a in operands])
    arrs = list(arrs)
    for w, a in zip(mats, res[n_sems:]):
        arrs[w] = a
    return res[:n_sems], arrs, list(res[n_sems + len(mats):])


def _gather_wait(k, sems, arrs, conv, after, name):
    group = GATHER_GROUPS[k]
    mats = sorted({w for w, _ in group})
    n_conv = 0 if conv is None else 2

    def body(*refs):
        local = refs[:len(mats)]
        arrs_ref = [None] * N_BIG
        for w, ref in zip(mats, local):
            arrs_ref[w] = ref
        pos = len(mats) + n_conv
        copies = _gather_copies(arrs_ref, group, refs[pos], refs[pos + 1])
        if conv is not None:
            copies += _conv_copies(refs[len(mats)], refs[len(mats) + 1], refs[pos + 2], refs[pos + 3])
        for send, recv in copies:
            recv.wait_recv()
            send.wait_send()

    operands = [arrs[w] for w in mats] + ([] if conv is None else [conv[1], conv[2]])
    sem_ops = list(sems) + ([] if conv is None else list(conv[0]))
    n_op = len(operands)
    res = pl.pallas_call(
        body, name=name, out_shape=tuple(pltpu.HBM(a.shape, a.dtype) for a in operands),
        in_specs=(HBM,) * n_op + (SEM,) * len(sem_ops) + (ANY,) * len(after), out_specs=(HBM,) * n_op,
        input_output_aliases={i: i for i in range(n_op)},
        compiler_params=pltpu.CompilerParams(has_side_effects=EFFECT),
    )(*operands, *sem_ops, *after)
    arrs = list(arrs)
    for w, a in zip(mats, res):
        arrs[w] = a
    return arrs, (res[-1] if conv is not None else None)


def _grad_shard(ref, w, chip, n):
    start = pl.multiple_of(chip * n, 128)
    if SHARD_AXES[w] == 2:
        return ref.at[:, pl.ds(start, n)]
    return ref.at[pl.ds(start, n), :]


def _slot_shape(g, w):
    shape = list(g.shape)
    shape[SHARD_AXES[w] - 1] //= N_CHIPS
    return (N_DEV - 1,) + tuple(shape)


def _scatter_copies(g_ref, land_ref, send_sems, recv_sems, layer, w):
    x, y, c = _mesh_pos()
    n = g_ref.shape[SHARD_AXES[w] - 1] // N_CHIPS
    out = []
    for r in range(1, N_DEV):
        tx, ty, tc = _flip(x, r & 4), _flip(y, r & 2), _flip(c, r & 1)
        cp = pltpu.make_async_remote_copy(
            src_ref=_grad_shard(g_ref, w, 2 * tx + ty, n), dst_ref=land_ref.at[r - 1], send_sem=send_sems.at[r - 1],
            recv_sem=recv_sems.at[r - 1], device_id=(tx, ty, tc), device_id_type=MESH)
        out.append((cp, (c != layer) if r & 1 else (c == layer)))
    return out


def _scatter_start(items, layer, name):
    n = len(items)

    def body(*refs):
        for i, (w, _, _) in enumerate(items):
            g_ref, land_ref = refs[2 * i], refs[2 * i + 1]
            send_sems, recv_sems = refs[2 * n + 2 * i], refs[2 * n + 2 * i + 1]
            for cp, mine in _scatter_copies(g_ref, land_ref, send_sems, recv_sems, layer, w):
                @pl.when(mine)
                def _():
                    cp.start()
        refs[-1][...] = jnp.zeros_like(refs[-1])

    operands = [a for _, g, land in items for a in (g, land)]
    res = pl.pallas_call(
        body, name=name,
        out_shape=(pltpu.SemaphoreType.DMA((N_DEV - 1,)),) * (2 * n)
        + tuple(pltpu.HBM(a.shape, a.dtype) for a in operands) + (jax.ShapeDtypeStruct((HALO, 128), F32),),
        in_specs=(HBM,) * (2 * n),
        out_specs=(SEM,) * (2 * n) + (HBM,) * (2 * n) + (pl.BlockSpec(memory_space=pltpu.VMEM),),
        input_output_aliases={i: 2 * n + i for i in range(2 * n)},
        compiler_params=pltpu.CompilerParams(has_side_effects=EFFECT),
    )(*[pltpu.with_memory_space_constraint(a, pltpu.HBM) for a in operands])
    return [(res[2 * i], res[2 * i + 1], res[2 * n + 2 * i], res[2 * n + 2 * i + 1]) for i in range(n)], res[-1]


def _scatter_wait(started, land, after, w, name):
    def body(g0_ref, g1_ref, land_ref, ss0, rs0, ss1, rs1, after_ref, g0_out, g1_out, land_out):
        c = lax.axis_index("c")
        for layer, g_ref, ss, rs in ((0, g0_ref, ss0, rs0), (1, g1_ref, ss1, rs1)):
            for cp, mine in _scatter_copies(g_ref, land_ref, ss, rs, layer, w):
                @pl.when(mine)
                def _():
                    cp.wait_send()

                @pl.when(c == layer)
                def _():
                    cp.wait_recv()

    (ss0, rs0, g0), (ss1, rs1, g1) = started
    return pl.pallas_call(
        body, name=name,
        out_shape=(pltpu.HBM(g0.shape, g0.dtype), pltpu.HBM(g1.shape, g1.dtype), pltpu.HBM(land.shape, land.dtype)),
        in_specs=(HBM, HBM, HBM, SEM, SEM, SEM, SEM, ANY), out_specs=(HBM, HBM, HBM),
        input_output_aliases={0: 0, 1: 1, 2: 2}, compiler_params=pltpu.CompilerParams(has_side_effects=EFFECT),
    )(g0, g1, land, ss0, rs0, ss1, rs1, after)


def _sum_slots(g0, g1, slots, w, pos_arr, name):
    _, rows, cols = slots.shape
    tr = min(512, rows)
    nr = rows // tr
    if SHARD_AXES[w] == 2:
        own = pl.BlockSpec((tr, cols), lambda i, pos: (i, pos[0]))
    else:
        own = pl.BlockSpec((tr, cols), lambda i, pos: (pos[0] * nr + i, 0))

    def body(pos_ref, own0_ref, own1_ref, s_ref, o_ref):
        acc = jnp.where(pos_ref[1] == 0, own0_ref[...], own1_ref[...]).astype(F32)
        for r in range(N_DEV - 1):
            acc = acc + s_ref[r].astype(F32)
        o_ref[...] = acc

    return pl.pallas_call(
        body, name=name,
        grid_spec=pltpu.PrefetchScalarGridSpec(
            num_scalar_prefetch=1, grid=(nr,),
            in_specs=[own, own, pl.BlockSpec((N_DEV - 1, tr, cols), lambda i, pos: (0, i, 0))],
            out_specs=pl.BlockSpec((tr, cols), lambda i, pos: (i, 0))),
        out_shape=jax.ShapeDtypeStruct((rows, cols), F32), compiler_params=_cparams("parallel"),
    )(pos_arr, g0, g1, slots)


def _swap_copies(refs, n):
    x, y, c = _mesh_pos()
    return [pltpu.make_async_remote_copy(src_ref=refs[w], dst_ref=refs[n + w], send_sem=refs[2 * n].at[w],
                                         recv_sem=refs[2 * n + 1].at[w], device_id=(x, y, 1 - c), device_id_type=MESH)
            for w in range(n)]


def _swap_start(halves, name):
    n = len(halves)

    def body(*refs):
        for cp in _swap_copies(refs, n):
            cp.start()

    operands = list(halves) + [lax.empty(h.shape, h.dtype) for h in halves]
    res = pl.pallas_call(
        body, name=name,
        out_shape=(pltpu.SemaphoreType.DMA((n,)),) * 2 + tuple(pltpu.HBM(a.shape, a.dtype) for a in operands),
        in_specs=(HBM,) * (2 * n), out_specs=(SEM,) * 2 + (HBM,) * (2 * n),
        input_output_aliases={i: 2 + i for i in range(2 * n)},
        compiler_params=pltpu.CompilerParams(has_side_effects=EFFECT),
    )(*[pltpu.with_memory_space_constraint(a, pltpu.HBM) for a in operands])
    return res[0], res[1], list(res[2:2 + n]), list(res[2 + n:])


def _swap_wait(send_sems, recv_sems, halves, lands, after, name):
    n = len(halves)

    def body(*refs):
        for cp in _swap_copies(refs, n):
            cp.wait_send()
            cp.wait_recv()

    operands = list(halves) + list(lands)
    res = pl.pallas_call(
        body, name=name, out_shape=tuple(pltpu.HBM(a.shape, a.dtype) for a in operands),
        in_specs=(HBM,) * (2 * n) + (SEM, SEM, ANY), out_specs=(HBM,) * (2 * n),
        input_output_aliases={i: i for i in range(2 * n)},
        compiler_params=pltpu.CompilerParams(has_side_effects=EFFECT),
    )(*operands, send_sems, recv_sems, after)
    return list(res[n:])


def _adamw_math(w, g, m, v):
    m = ADAM_B1 * m + (1.0 - ADAM_B1) * g
    v = ADAM_B2 * v + (1.0 - ADAM_B2) * jnp.square(g)
    m_hat = m / (1.0 - ADAM_B1 ** ADAM_STEP)
    v_hat = v / (1.0 - ADAM_B2 ** ADAM_STEP)
    delta = -ADAM_LR * (m_hat / (jnp.sqrt(v_hat) + ADAM_EPS) + ADAM_WD * w)
    return delta, m, v


def _adamw(w, g, m, v, filled, pos_arr, name):
    shape = w.shape
    _, rows, cols = shape
    tr = min(256, rows)

    def body(pos_ref, w_ref, g_ref, m_ref, v_ref, *rest):
        go_ref, d_ref, m2_ref, v2_ref = rest[-4:]
        g = g_ref[...]
        go_ref[...] = g
        d_ref[...], m2_ref[...], v2_ref[...] = _adamw_math(w_ref[...], g, m_ref[...], v_ref[...])

    def layer(pos):
        return pos[1] if filled is None else 1 - pos[1]

    full = pl.BlockSpec((None, tr, cols), lambda i, pos: (layer(pos), i, 0))
    half = pl.BlockSpec((tr, cols), lambda i, pos: (i, 0))
    n_in = 5
    return pl.pallas_call(
        body, name=name,
        grid_spec=pltpu.PrefetchScalarGridSpec(
            num_scalar_prefetch=1, grid=(rows // tr,),
            in_specs=[full, half, full, full] + ([] if filled is None else [ANY] * 4), out_specs=[full] * 4),
        out_shape=[jax.ShapeDtypeStruct(shape, F32)] * 4,
        input_output_aliases={} if filled is None else {n_in + k: k for k in range(4)},
        compiler_params=_cparams("parallel"),
    )(pos_arr, w, g, m, v, *([] if filled is None else filled))


def _small_sync(part, w, m, v):
    rows, cols = part.shape

    def body(p_ref, w_ref, m_ref, v_ref, g_ref, d_ref, m2_ref, v2_ref, slots, send_sems, recv_sems):
        x, y, c = _mesh_pos()
        me = 4 * x + 2 * y + c
        slots[me] = p_ref[...]
        sends = []
        for r in range(1, N_DEV):
            to = (_flip(x, r & 4), _flip(y, r & 2), _flip(c, r & 1))
            sends.append(pltpu.make_async_remote_copy(
                src_ref=p_ref, dst_ref=slots.at[me], send_sem=send_sems.at[r - 1], recv_sem=recv_sems.at[r - 1],
                device_id=to, device_id_type=MESH))
        for cp in sends:
            cp.start()
        for cp in sends:
            cp.wait_recv()
        for cp in sends:
            cp.wait_send()
        g = slots[0]
        for i in range(1, N_DEV):
            g = g + slots[i]
        g_ref[...] = g
        d_ref[...], m2_ref[...], v2_ref[...] = _adamw_math(w_ref[...], g, m_ref[...], v_ref[...])

    vm = pl.BlockSpec(memory_space=pltpu.VMEM)
    return pl.pallas_call(
        body, name="small_sync", in_specs=[vm] * 4, out_specs=[vm] * 4,
        out_shape=[jax.ShapeDtypeStruct((rows, cols), F32)] * 4,
        scratch_shapes=[pltpu.VMEM((N_DEV, rows, cols), F32), pltpu.SemaphoreType.DMA((N_DEV - 1,)),
                        pltpu.SemaphoreType.DMA((N_DEV - 1,))],
    )(part, w, m, v)


PACK_W = 256


def _pack_rows(n):
    return -(-n // (HALO * PACK_W)) * HALO


def _pack_small(parts):
    out = []
    for a in parts:
        flat = a.reshape(-1)
        out.append(jnp.pad(flat, (0, _pack_rows(flat.size) * PACK_W - flat.size)).reshape(-1, PACK_W))
    return jnp.concatenate(out, axis=0)


def _unpack_small(p, shapes):
    out, row = [], 0
    for shape in shapes:
        n = 1
        for k in shape:
            n *= k
        out.append(p[row:row + _pack_rows(n)].reshape(-1)[:n].reshape(shape))
        row += _pack_rows(n)
    return out


def kernel(x, w_in, conv_w, sinks, g_mix, g_group, w_o, g_mlp, w_ff_in, w_ff_out, g_final, loss_target, m_w_in, m_conv_w, m_sinks, m_g_mix, m_g_group, m_w_o, m_g_mlp, m_w_ff_in, m_w_ff_out, m_g_final, v_w_in, v_conv_w, v_sinks, v_g_mix, v_g_group, v_w_o, v_g_mlp, v_w_ff_in, v_w_ff_out, v_g_final):
    chip = 2 * lax.axis_index("x") + lax.axis_index("y")
    conv_n = conv_w.shape[2]

    pos_arr = jnp.stack([chip, lax.axis_index("c")]).astype(jnp.int32)
    shards = (w_in, w_o, w_ff_in, w_ff_out)
    conv_tile = jnp.pad(conv_w.reshape(6, conv_n), ((0, HALO - 6), (0, 128 - conv_n)))
    placed = [_place_shard(w_in, pos_arr[:1], "place_shard_0"), None, None, None]
    sems_a, placed, conv_thru = _gather_start(
        GATHER_STARTS[0], placed, (conv_tile, lax.empty((N_CHIPS,) + conv_tile.shape, conv_tile.dtype)),
        "gather_start_0")
    for i in range(1, N_BIG):
        placed[i] = _place_shard(shards[i], pos_arr[:1], "place_shard_%d" % i)
    full = {"arrs": placed, "conv": None, "sems": list(sems_a[:2])}
    target = _to_strips(loss_target[0], placed[:1], "to_strips_target")

    def fetch(stage, layer, after):
        k = GATHER_STAGES.get((stage, layer))
        if k is None:
            return (*full["arrs"], full["conv"])
        sems = full["sems"][2 * k:2 * k + 2]
        if k == 0:
            full["arrs"], land = _gather_wait(0, sems, full["arrs"], (sems_a[-2:], *conv_thru), (after, target),
                                              "gather_wait_0")
            conv_all = lax.dynamic_update_slice(land, conv_tile[None], (chip, 0, 0))
            full["conv"] = conv_all[:, :6, :conv_n].reshape(N_CHIPS, 2, 3, conv_n).transpose(1, 2, 0, 3).reshape(
                2, 3, CONV_CH)
            sems_b, full["arrs"], rest = _gather_start(GATHER_STARTS[1], full["arrs"], None, "gather_start_1",
                                                       through=full["arrs"][0])
            full["arrs"][0] = rest[-1]
            full["sems"] += list(sems_b)
        else:
            full["arrs"], _ = _gather_wait(k, sems, full["arrs"], None, (after,), "gather_wait_%d" % k)
        if k == 2:
            sems_c, full["arrs"], _ = _gather_start(GATHER_STARTS[2], full["arrs"], None, "gather_start_2")
            full["sems"] += list(sems_c)
        return (*full["arrs"], full["conv"])

    lands, started, pending = [None] * N_BIG, {}, []

    def emit(layer, w, g):
        if lands[w] is None:
            lands[w] = lax.empty(_slot_shape(g, w), g.dtype)
        pending.append((w, g, lands[w]))
        if not (w == 0 or (layer == 0 and w == 1)):
            return jnp.zeros((), F32)
        name = "scatter_start_%d_%d" % (layer, len(pending))
        done, token = _scatter_start(list(pending), layer, name)
        for (w_i, _, _), (ss, rs, g_thru, land) in zip(pending, done):
            started[layer, w_i], lands[w_i] = (ss, rs, g_thru), land
        pending.clear()
        return token[0, 0]

    loss_tile, dx, grads, dg_final = _local_step(_to_strips(x[0], placed, "to_strips_x"), target, fetch,
                                                 w_ff_in.shape[2] * N_CHIPS,
                                                 sinks, g_mix, g_group, g_mlp, g_final, emit)

    wmv = ((w_in, m_w_in, v_w_in), (w_o, m_w_o, v_w_o), (w_ff_in, m_w_ff_in, v_w_ff_in),
           (w_ff_out, m_w_ff_out, v_w_ff_out))
    big, after = [None] * N_BIG, dx
    for name, ws in (("swap_rest", (1, 2, 3)), ("swap_in", (0,))):
        own = []
        for w in ws:
            g0, g1, slots = _scatter_wait((started[0, w], started[1, w]), lands[w], after, w, "scatter_wait_%d" % w)
            own.append(_sum_slots(g0, g1, slots, w, pos_arr, "sum_slots_%d" % w))
        send_sems, recv_sems, own, zones = _swap_start(own, name + "_start")
        for w, g in zip(ws, own):
            big[w] = _adamw(wmv[w][0], g, wmv[w][1], wmv[w][2], None, pos_arr, "adamw_own_%d" % w)
        theirs = _swap_wait(send_sems, recv_sems, own, zones, big[ws[-1]][1], name + "_wait")
        for w, g in zip(ws, theirs):
            big[w] = _adamw(wmv[w][0], g, wmv[w][1], wmv[w][2], big[w], pos_arr, "adamw_other_%d" % w)
        after = big[ws[-1]][1]

    def both(i):
        return jnp.stack([grads[0][i][0], grads[1][i][0]])
    dconv = jnp.stack([grads[0][0][:3], grads[1][0][:3]])
    dsinks = jnp.stack([grads[0][1][0, ::HEAD_DIM], grads[1][1][0, ::HEAD_DIM]])
    part = _pack_small([both(2), both(3), both(4), dg_final[0], dconv, dsinks, loss_tile[0, 0]])

    def spread(shard):
        return lax.dynamic_update_slice(jnp.zeros((2, 3, CONV_CH), F32), shard, (0, 0, chip * conv_n))
    zero = jnp.zeros((), F32)
    packs = [_pack_small([a, b, c_, e, spread(f), g_, zero]) for a, b, c_, e, f, g_ in (
        (g_mix, g_group, g_mlp, g_final, conv_w, sinks),
        (m_g_mix, m_g_group, m_g_mlp, m_g_final, m_conv_w, m_sinks),
        (v_g_mix, v_g_group, v_g_mlp, v_g_final, v_conv_w, v_sinks))]
    shapes = [g_mix.shape, g_group.shape, g_mlp.shape, g_final.shape, (2, 3, CONV_CH), sinks.shape, ()]
    small = [_unpack_small(p, shapes) for p in _small_sync(part, *packs)]

    def shard_of(full):
        return lax.dynamic_slice(full, (0, 0, chip * conv_n), (2, 3, conv_n))
    small = [(s[0], s[1], s[2], s[3], shard_of(s[4]), s[5], s[6]) for s in small]
    loss = small[0][6]

    def ordered(kind):
        b = [big[i][kind] for i in range(N_BIG)]
        s = small[kind]
        return [b[0], s[4], s[5], s[0], s[1], b[1], s[2], b[2], b[3], s[3]]

    return (loss, dx[None], *ordered(0), *ordered(1), *ordered(2), *ordered(3))
```

```python
import functools

import jax
import jax.numpy as jnp
from jax import lax
from jax.experimental import pallas as pl
from jax.experimental.pallas import tpu as pltpu

HEAD_DIM = 64
N_HEADS = 6
C_GROUP = 3
A_WIDTH = N_HEADS * HEAD_DIM
C_KV_WIDTH = 2 * HEAD_DIM
CONV_CH = 256
ZA_W = 3 * A_WIDTH
ZB_W = 3 * CONV_CH
ZC_W = A_WIDTH + 2 * C_KV_WIDTH
IN_WIDTH = ZA_W + ZB_W + ZC_W
MIX_WIDTH = A_WIDTH + CONV_CH + A_WIDTH
DILATIONS = (1, 4, 16)
A_MAX_DIST = 128
C_MAX_DIST = 127
TQ = 128
EPS = 1e-6
SCALE = HEAD_DIM ** -0.5
NEG = -1e30
HALO = 8

ADAM_LR = 0.001
ADAM_B1 = 0.9
ADAM_B2 = 0.999
ADAM_EPS = 1e-08
ADAM_WD = 0.01
ADAM_STEP = 10

BF = jnp.bfloat16
F32 = jnp.float32
MESH = pl.DeviceIdType.MESH
VMEM_LIMIT = 56 * 1024 * 1024


def _cparams(*sem):
    return pltpu.CompilerParams(dimension_semantics=sem, vmem_limit_bytes=VMEM_LIMIT)


def _nt(a, b):
    return lax.dot_general(a, b, (((1,), (1,)), ((), ())), preferred_element_type=F32)


def _tn(a, b):
    return lax.dot_general(a, b, (((0,), (0,)), ((), ())), preferred_element_type=F32)


def _nn(a, b):
    return jnp.dot(a, b, preferred_element_type=F32)


def _rows(tb, w):
    return pl.BlockSpec((tb, w), lambda i: (i, 0))


def _whole(shape):
    return pl.BlockSpec(shape, lambda *_: (0,) * len(shape))


def _layer(shape, l):
    return pl.BlockSpec((None,) + shape, lambda *_: (l,) + (0,) * len(shape))


def _rms_scale(v):
    return lax.rsqrt(jnp.mean(v * v, axis=-1, keepdims=True) + EPS)


def _norm_bwd(dxhat, xhat, r):
    return r * (dxhat - xhat * jnp.mean(dxhat * xhat, axis=-1, keepdims=True))


def _qkv_fwd(x, g, w_all, l, tb):
    s, d = x.shape

    def body(x_ref, g_ref, w_ref, h_ref, za_ref, zb_ref, zc_ref):
        xv = x_ref[...]
        h = ((xv * _rms_scale(xv)) * g_ref[...]).astype(BF)
        h_ref[...] = h
        z = jnp.concatenate([_nn(h, w_ref[k]) for k in range(N_CHIPS)], axis=1)
        za_ref[...] = z[:, :ZA_W]
        zb_ref[...] = z[:, ZA_W:ZA_W + ZB_W]
        zc_ref[...] = z[:, ZA_W + ZB_W:]

    return pl.pallas_call(
        body, grid=(s // tb,), name="qkv_fwd",
        in_specs=[_rows(tb, d), _whole((1, d)), _layer((N_CHIPS, d, IN_WIDTH // N_CHIPS), l)],
        out_specs=[_rows(tb, d), _rows(tb, ZA_W), _rows(tb, ZB_W), _rows(tb, ZC_W)],
        out_shape=[jax.ShapeDtypeStruct((s, d), BF), jax.ShapeDtypeStruct((s, ZA_W), F32),
                   jax.ShapeDtypeStruct((s, ZB_W), F32), jax.ShapeDtypeStruct((s, ZC_W), F32)],
        compiler_params=_cparams("parallel"),
    )(x, g, w_all)


N_STRIPS = 16


def _strips(a):
    s, w = a.shape
    return a.reshape(4, 4, s // N_STRIPS, w)


P_ROWS = {16: TQ, 4: 32, 1: 8}


def _p_sub(s, dil, most):
    while (s // dil // TQ) % most:
        most //= 2
    return most


def _p_grid(s, dil, n_sub):
    nb = s // dil // TQ // n_sub
    return {16: (4, 4, nb), 4: (4, nb), 1: (nb,)}[dil]


def _p_spec(dil, cw, col, n_sub, prev=False):
    rows = P_ROWS[dil] * (1 if prev else n_sub)

    def blk(j):
        return jnp.maximum(n_sub * j - 1, 0) if prev else j
    if dil == 16:
        return pl.BlockSpec((None, None, rows, cw), lambda f, e, j: (f, e, blk(j), col))
    if dil == 4:
        return pl.BlockSpec((None, 4, rows, cw), lambda f, j: (f, 0, blk(j), col))
    return pl.BlockSpec((4, 4, rows, cw), lambda j: (0, 0, blk(j), col))


def _block_pos(i, dil):
    if dil == 16:
        return i
    if dil == 4:
        return 4 * (i % 32) + i // 32
    return 16 * (i % 8) + 4 * ((i // 8) % 4) + i // 32


def _band_mask(b, dil, max_dist):
    qi = _block_pos(lax.broadcasted_iota(jnp.int32, (TQ, 2 * TQ), 0), dil)
    col = lax.broadcasted_iota(jnp.int32, (TQ, 2 * TQ), 1)
    cur = col >= TQ
    dist = qi - _block_pos(col % TQ, dil) + jnp.where(cur, 0, TQ)
    return (dist >= 0) & (dist <= max_dist) & (cur | (b > 0))


def _hs(h):
    return slice(h * HEAD_DIM, (h + 1) * HEAD_DIM)


def _ld(ref, cols, rows=slice(None)):
    v = ref[..., rows, cols]
    return v.reshape(TQ, v.shape[-1])


def _st(ref, cols, val, rows=slice(None)):
    lead = ref.shape[:-2] + (ref.shape[-2] if rows == slice(None) else rows.stop - rows.start,)
    ref[..., rows, cols] = val.reshape(lead + (val.shape[-1],))


def _attn_fwd(z, dil, kw, kcol, vcol, n_rep, max_dist, name):
    s, zw = z.shape
    n_sub = _p_sub(s, dil, 2)
    grid = _p_grid(s, dil, n_sub)
    o_dt = F32 if P_ROWS[dil] * n_sub < 16 else BF

    def body(q_ref, kp_ref, kc_ref, vp_ref, vc_ref, o_ref, lse_ref):
        for t in range(n_sub):
            rows = slice(t * P_ROWS[dil], (t + 1) * P_ROWS[dil])
            before = (slice(None),) if t == 0 else (slice((t - 1) * P_ROWS[dil], t * P_ROWS[dil]),)
            kb_ref, vb_ref = (kp_ref, vp_ref) if t == 0 else (kc_ref, vc_ref)
            mask = _band_mask(n_sub * pl.program_id(len(grid) - 1) if t == 0 else 1, dil, max_dist)
            scs, v2s = [], []
            for kh in range(N_HEADS // n_rep):
                k2 = jnp.concatenate([_ld(kb_ref, _hs(kh), *before), _ld(kc_ref, _hs(kh), rows)], axis=0).astype(BF)
                v2s.append(jnp.concatenate([_ld(vb_ref, _hs(kh), *before), _ld(vc_ref, _hs(kh), rows)],
                                           axis=0).astype(BF))
                for h in range(kh * n_rep, (kh + 1) * n_rep):
                    q = (_ld(q_ref, _hs(h), rows) * SCALE).astype(BF)
                    scs.append(jnp.where(mask, _nt(q, k2), NEG))
            for h, sc in enumerate(scs):
                m = jnp.max(sc, axis=1, keepdims=True)
                p = jnp.exp(sc - m)
                l = jnp.sum(p, axis=1, keepdims=True)
                _st(o_ref, _hs(h), (_nn(p.astype(BF), v2s[h // n_rep]) / l).astype(o_ref.dtype), rows)
                _st(lse_ref, _hs(h), jnp.broadcast_to(m + jnp.log(l), (TQ, HEAD_DIM)), rows)

    res = pl.pallas_call(
        body, grid=grid, name=name,
        in_specs=[_p_spec(dil, A_WIDTH, 0, n_sub), _p_spec(dil, kw, kcol, n_sub, True), _p_spec(dil, kw, kcol, n_sub),
                  _p_spec(dil, kw, vcol, n_sub, True), _p_spec(dil, kw, vcol, n_sub)],
        out_specs=[_p_spec(dil, A_WIDTH, 0, n_sub)] * 2,
        out_shape=[jax.ShapeDtypeStruct((4, 4, s // N_STRIPS, A_WIDTH), dt) for dt in (o_dt, F32)],
        compiler_params=_cparams(*(("parallel",) * len(grid))),
    )(*[_strips(z)] * 5)
    return [a.reshape(s, A_WIDTH) for a in res]


def _attn_merge(parts_a, part_c, sink_row, tb):
    s = part_c[0].shape[0]
    n_a = len(parts_a)

    def body(*refs):
        ins, sink_ref = refs[:2 * n_a + 2], refs[2 * n_a + 2]
        ya_ref, lsea_ref, yc_ref, lsec_ref = refs[2 * n_a + 3:]
        lses = [ins[2 * p + 1][...] for p in range(n_a)]
        m = functools.reduce(jnp.maximum, lses)
        ws = [jnp.exp(v - m) for v in lses]
        l = functools.reduce(jnp.add, ws)
        ya_ref[...] = functools.reduce(jnp.add, [w * ins[2 * p][...].astype(F32) for p, w in enumerate(ws)]) / l
        lsea_ref[...] = m + jnp.log(l)
        o_c, lse_c = [r[...].astype(F32) for r in ins[2 * n_a:]]
        sk = sink_ref[...]
        m2 = jnp.maximum(lse_c, sk)
        w = jnp.exp(lse_c - m2)
        l2 = w + jnp.exp(sk - m2)
        yc_ref[...] = o_c * (w / l2)
        lsec_ref[...] = m2 + jnp.log(l2)

    return pl.pallas_call(
        body, grid=(s // tb,), name="attn_merge",
        in_specs=[_rows(tb, A_WIDTH)] * (2 * n_a + 2) + [_whole((1, A_WIDTH))],
        out_specs=[_rows(tb, A_WIDTH)] * 4, out_shape=[jax.ShapeDtypeStruct((s, A_WIDTH), F32)] * 4,
        compiler_params=_cparams("parallel"),
    )(*[a for part in parts_a + [part_c] for a in part], sink_row)


def _shift_down(v, n, halo):
    rows = v.shape[0]
    out = pltpu.roll(v, n, 0)
    row = lax.broadcasted_iota(jnp.int32, v.shape, 0)
    for t in range(n):
        out = jnp.where(row == t, halo[HALO - n + t:HALO - n + t + 1, :], out)
    return out


def _shift_up(v, n, halo):
    rows = v.shape[0]
    out = pltpu.roll(v, rows - n, 0)
    row = lax.broadcasted_iota(jnp.int32, v.shape, 0)
    for t in range(n):
        out = jnp.where(row == rows - n + t, halo[t:t + 1, :], out)
    return out


def _strip(v, b):
    return v[b % 4, b // 4]


def _conv_strips(zb, prev, cw):
    gb = [_strip(zb, b)[:, :CONV_CH] for b in range(N_STRIPS)]
    gc = [_strip(zb, b)[:, CONV_CH:2 * CONV_CH] for b in range(N_STRIPS)]
    xb = [_strip(zb, b)[:, 2 * CONV_CH:] for b in range(N_STRIPS)]
    u = [g * v for g, v in zip(gc, xb)]
    uh = prev[:, :, CONV_CH:2 * CONV_CH] * prev[:, :, 2 * CONV_CH:]
    wrapped = {14: _shift_down(u[14], 1, uh[2]), 15: _shift_down(u[15], 1, uh[3])}
    u1 = [u[b - 1] if b >= 1 else wrapped[15] for b in range(N_STRIPS)]
    u2 = [u[b - 2] if b >= 2 else wrapped[14 + b] for b in range(N_STRIPS)]
    c = [cw[0:1, :] * u2[b] + cw[1:2, :] * u1[b] + cw[2:3, :] * u[b] for b in range(N_STRIPS)]
    return gb, gc, xb, u, u1, u2, c


def _strip_rows(ta, w):
    return pl.BlockSpec((4, 4, ta, w), lambda i: (0, 0, i, 0))


def _prev_rows(ta, w):
    return pl.BlockSpec((4, None, HALO, w), lambda i: (0, 3, jnp.maximum(i * (ta // HALO) - 1, 0), 0))


def _next_rows(ta, w, nblk):
    return pl.BlockSpec((4, None, HALO, w),
                        lambda i: (0, 0, jnp.minimum((i + 1) * (ta // HALO), nblk * (ta // HALO) - 1), 0))


def _mix_fwd(x, ya, yc, zb, cw, gg, wo_all, l, tb):
    s, d = x.shape
    ta = tb // N_STRIPS

    def body(x_ref, ya_ref, yc_ref, zb_ref, zbp_ref, cw_ref, gg_ref, wo_ref, x1_ref, yb_ref):
        i = pl.program_id(0)
        prev = jnp.where(i > 0, zbp_ref[...], 0.0)
        gb, _, _, _, _, _, c = _conv_strips(zb_ref[...], prev, cw_ref[...])
        for b in range(N_STRIPS):
            yb_ref[b % 4, b // 4] = gb[b] * c[b]
        yb = yb_ref[...].reshape(tb, CONV_CH)
        ya, yc = ya_ref[...].reshape(tb, A_WIDTH), yc_ref[...].reshape(tb, A_WIDTH)
        n = jnp.concatenate([ya * _rms_scale(ya), yb * _rms_scale(yb), yc * _rms_scale(yc)], axis=1)
        n = (n * gg_ref[...]).astype(BF)
        x1 = x_ref[...].reshape(tb, d) + _nn(n, wo_ref[...].reshape(MIX_WIDTH, d))
        x1_ref[...] = x1.reshape(4, 4, ta, d)

    res = pl.pallas_call(
        body, grid=(s // tb,), name="mix_fwd",
        in_specs=[_strip_rows(ta, d), _strip_rows(ta, A_WIDTH), _strip_rows(ta, A_WIDTH), _strip_rows(ta, ZB_W),
                  _prev_rows(ta, ZB_W), _whole((HALO, CONV_CH)), _whole((1, MIX_WIDTH)),
                  _layer((N_CHIPS, MIX_WIDTH // N_CHIPS, d), l)],
        out_specs=[_strip_rows(ta, d), _strip_rows(ta, CONV_CH)],
        out_shape=[jax.ShapeDtypeStruct((4, 4, s // N_STRIPS, d), F32),
                   jax.ShapeDtypeStruct((4, 4, s // N_STRIPS, CONV_CH), F32)],
        compiler_params=_cparams("parallel"),
    )(_strips(x), _strips(ya), _strips(yc), _strips(zb), _strips(zb), cw, gg, wo_all)
    return res[0].reshape(s, d), res[1].reshape(s, CONV_CH)


def _mlp_fwd(x1, g, w1_all, w2_all, l, tb, tf):
    s, d = x1.shape
    ff = w1_all.shape[1] * w1_all.shape[3]
    nj = ff // tf

    def body(x_ref, g_ref, w1_ref, w2_ref, x2_ref, h2_ref, ap_ref, acc):
        j = pl.program_id(1)

        @pl.when(j == 0)
        def _():
            xv = x_ref[...]
            h2_ref[...] = ((xv * _rms_scale(xv)) * g_ref[...]).astype(BF)
            acc[...] = jnp.zeros_like(acc)

        ap = _nn(h2_ref[...], w1_ref[...])
        ap_ref[...] = ap.astype(BF)
        a = jnp.square(jnp.maximum(ap, 0.0)).astype(BF)
        acc[...] += _nn(a, w2_ref[...])

        @pl.when(j == nj - 1)
        def _():
            x2_ref[...] = x_ref[...] + acc[...]

    return pl.pallas_call(
        body, grid=(s // tb, nj), name="mlp_fwd",
        in_specs=[pl.BlockSpec((tb, d), lambda i, j: (i, 0)), _whole((1, d)),
                  pl.BlockSpec((None, None, d, tf), lambda i, j: (l, j, 0, 0)),
                  pl.BlockSpec((None, None, tf, d), lambda i, j: (l, j, 0, 0))],
        out_specs=[pl.BlockSpec((tb, d), lambda i, j: (i, 0)), pl.BlockSpec((tb, d), lambda i, j: (i, 0)),
                   pl.BlockSpec((tb, tf), lambda i, j: (i, j))],
        out_shape=[jax.ShapeDtypeStruct((s, d), F32), jax.ShapeDtypeStruct((s, d), BF),
                   jax.ShapeDtypeStruct((s, ff), BF)],
        scratch_shapes=[pltpu.VMEM((tb, d), F32)],
        compiler_params=_cparams("parallel", "arbitrary"),
    )(x1, g, w1_all, w2_all)


def _loss_head(x, g, tgt, tb):
    s, d = x.shape

    def body(x_ref, g_ref, t_ref, dx_ref, loss_ref, dg_ref):
        i = pl.program_id(0)

        @pl.when(i == 0)
        def _():
            loss_ref[...] = jnp.zeros_like(loss_ref)
            dg_ref[...] = jnp.zeros_like(dg_ref)

        xv = x_ref[...]
        r = _rms_scale(xv)
        xhat = xv * r
        err = xhat * g_ref[...] - t_ref[...]
        part = jnp.sum(jnp.mean(jnp.square(err), axis=-1, keepdims=True), axis=0, keepdims=True)
        loss_ref[...] += 0.5 * part
        dy = err * (1.0 / d)
        dg_ref[...] += jnp.sum(dy * xhat, axis=0, keepdims=True)
        dx_ref[...] = _norm_bwd(dy * g_ref[...], xhat, r)

    return pl.pallas_call(
        body, grid=(s // tb,), name="loss_head",
        in_specs=[_rows(tb, d), _whole((1, d)), _rows(tb, d)],
        out_specs=[_rows(tb, d), _whole((HALO, 128)), _whole((HALO, d))],
        out_shape=[jax.ShapeDtypeStruct((s, d), F32), jax.ShapeDtypeStruct((HALO, 128), F32),
                   jax.ShapeDtypeStruct((HALO, d), F32)],
        compiler_params=_cparams("arbitrary"),
    )(x, g, tgt)


def _mlp_bwd(dx2, x1, ap, g, w1_all, w2_all, l, tb, tf):
    s, d = x1.shape
    ff = ap.shape[1]
    nj = ff // tf

    def body(dx2_ref, x1_ref, ap_ref, g_ref, w1_ref, w2_ref, dx1_ref, dap_ref, dg_ref, acc, dx2_bf):
        i, j = pl.program_id(0), pl.program_id(1)

        @pl.when((i == 0) & (j == 0))
        def _():
            dg_ref[...] = jnp.zeros_like(dg_ref)

        @pl.when(j == 0)
        def _():
            acc[...] = jnp.zeros_like(acc)
            dx2_bf[...] = dx2_ref[...].astype(BF)

        da = _nt(dx2_bf[...], w2_ref[...])
        dap = (da * (2.0 * jnp.maximum(ap_ref[...].astype(F32), 0.0))).astype(BF)
        dap_ref[...] = dap
        acc[...] += _nt(dap, w1_ref[...])

        @pl.when(j == nj - 1)
        def _():
            xv = x1_ref[...]
            r = _rms_scale(xv)
            xhat = xv * r
            dh = acc[...]
            dg_ref[...] += jnp.sum(dh * xhat, axis=0, keepdims=True)
            dx1_ref[...] = dx2_ref[...] + _norm_bwd(dh * g_ref[...], xhat, r)

    return pl.pallas_call(
        body, grid=(s // tb, nj), name="mlp_bwd",
        in_specs=[pl.BlockSpec((tb, d), lambda i, j: (i, 0)), pl.BlockSpec((tb, d), lambda i, j: (i, 0)),
                  pl.BlockSpec((tb, tf), lambda i, j: (i, j)),
                  _whole((1, d)), pl.BlockSpec((None, None, d, tf), lambda i, j: (l, j, 0, 0)),
                  pl.BlockSpec((None, None, tf, d), lambda i, j: (l, j, 0, 0))],
        out_specs=[pl.BlockSpec((tb, d), lambda i, j: (i, 0)), pl.BlockSpec((tb, tf), lambda i, j: (i, j)),
                   _whole((HALO, d))],
        out_shape=[jax.ShapeDtypeStruct((s, d), F32), jax.ShapeDtypeStruct((s, ff), BF),
                   jax.ShapeDtypeStruct((HALO, d), F32)],
        scratch_shapes=[pltpu.VMEM((tb, d), F32), pltpu.VMEM((tb, d), BF)],
        compiler_params=_cparams("arbitrary", "arbitrary"),
    )(dx2, x1, ap, g, w1_all, w2_all)


def _wgrad(a, b, tm, tn, ts, name, relu2=False):
    s, m = a.shape
    n = b.shape[1]
    ns = s // ts

    def body(a_ref, b_ref, o_ref, acc):
        k = pl.program_id(2)

        @pl.when(k == 0)
        def _():
            acc[...] = jnp.zeros_like(acc)

        av = a_ref[...]
        if relu2:
            av = jnp.square(jnp.maximum(av.astype(F32), 0.0)).astype(BF)
        acc[...] += _tn(av, b_ref[...].astype(BF))

        @pl.when(k == ns - 1)
        def _():
            o_ref[...] = acc[...].astype(BF)

    return pl.pallas_call(
        body, grid=(m // tm, n // tn, ns), name=name,
        in_specs=[pl.BlockSpec((ts, tm), lambda i, j, k: (k, i)), pl.BlockSpec((ts, tn), lambda i, j, k: (k, j))],
        out_specs=pl.BlockSpec((tm, tn), lambda i, j, k: (i, j)),
        out_shape=jax.ShapeDtypeStruct((m, n), BF),
        scratch_shapes=[pltpu.VMEM((tm, tn), F32)],
        compiler_params=_cparams("parallel", "parallel", "arbitrary"),
    )(a, b)


def _mix_bwd(dx1, ya, yb, yc, lse_c, sink_row, gg, wo_all, l, tb):
    s, d = dx1.shape

    def body(dx_ref, ya_ref, yb_ref, yc_ref, lse_ref, sink_ref, gg_ref, wo_ref,
             n_ref, dya_ref, dyc_ref, da_ref, dc_ref, dyb_ref, dg_ref, dsink_ref):
        i = pl.program_id(0)

        @pl.when(i == 0)
        def _():
            dg_ref[...] = jnp.zeros_like(dg_ref)
            dsink_ref[...] = jnp.zeros_like(dsink_ref)

        dn = _nt(dx_ref[...].astype(BF), wo_ref[...].reshape(MIX_WIDTH, d))
        ys = [ya_ref[...], yb_ref[...], yc_ref[...]]
        rs = [_rms_scale(v) for v in ys]
        nhat = jnp.concatenate([v * r for v, r in zip(ys, rs)], axis=1)
        gg = gg_ref[...]
        n_ref[...] = (nhat * gg).astype(BF)
        dg_ref[...] += jnp.sum(dn * nhat, axis=0, keepdims=True)
        dnh = dn * gg
        bounds = [(0, A_WIDTH), (A_WIDTH, A_WIDTH + CONV_CH), (A_WIDTH + CONV_CH, MIX_WIDTH)]
        dys = [_norm_bwd(dnh[:, lo:hi], nhat[:, lo:hi], r) for (lo, hi), r in zip(bounds, rs)]
        dyb_ref[...] = dys[1]
        head = [lax.broadcasted_iota(jnp.int32, (A_WIDTH, A_WIDTH), k) // HEAD_DIM for k in (0, 1)]
        ones = (head[0] == head[1]).astype(BF)
        for dy, y, dy_ref, dd_ref in ((dys[0], ys[0], dya_ref, da_ref), (dys[2], ys[2], dyc_ref, dc_ref)):
            dy_ref[...] = dy
            t = dy * y
            hi = t.astype(BF)
            dd_ref[...] = _nn(hi, ones) + _nn((t - hi.astype(F32)).astype(BF), ones)
        dsink_ref[...] -= jnp.sum(jnp.exp(sink_ref[...] - lse_ref[...]) * dc_ref[...], axis=0, keepdims=True)

    return pl.pallas_call(
        body, grid=(s // tb,), name="mix_bwd",
        in_specs=[_rows(tb, d), _rows(tb, A_WIDTH), _rows(tb, CONV_CH), _rows(tb, A_WIDTH), _rows(tb, A_WIDTH),
                  _whole((1, A_WIDTH)), _whole((1, MIX_WIDTH)), _layer((N_CHIPS, MIX_WIDTH // N_CHIPS, d), l)],
        out_specs=[_rows(tb, MIX_WIDTH), _rows(tb, A_WIDTH), _rows(tb, A_WIDTH), _rows(tb, A_WIDTH),
                   _rows(tb, A_WIDTH), _rows(tb, CONV_CH), _whole((HALO, MIX_WIDTH)), _whole((HALO, A_WIDTH))],
        out_shape=[jax.ShapeDtypeStruct((s, MIX_WIDTH), BF), jax.ShapeDtypeStruct((s, A_WIDTH), F32),
                   jax.ShapeDtypeStruct((s, A_WIDTH), F32), jax.ShapeDtypeStruct((s, A_WIDTH), F32),
                   jax.ShapeDtypeStruct((s, A_WIDTH), F32), jax.ShapeDtypeStruct((s, CONV_CH), F32),
                   jax.ShapeDtypeStruct((HALO, MIX_WIDTH), F32), jax.ShapeDtypeStruct((HALO, A_WIDTH), F32)],
        compiler_params=_cparams("arbitrary"),
    )(dx1, ya, yb, yc, lse_c, sink_row, gg, wo_all)


def _attn_bwd(z, dy, lse, dd, dil, kw, kcol, vcol, n_rep, max_dist, name):
    s, zw = z.shape
    n_sub = _p_sub(s, dil, 2) if n_rep == 1 else 1
    grid = _p_grid(s, dil, n_sub)
    n_kv = N_HEADS // n_rep
    dt = F32 if P_ROWS[dil] * n_sub < 16 else BF

    def body(q_ref, kp_ref, kc_ref, vp_ref, vc_ref, dy_ref, lse_ref, dd_ref, dq_ref, dkp_ref, dkc_ref, dvp_ref, dvc_ref):
        for t in range(n_sub):
            rows = slice(t * P_ROWS[dil], (t + 1) * P_ROWS[dil])
            before = (slice(None),) if t == 0 else (slice((t - 1) * P_ROWS[dil], t * P_ROWS[dil]),)
            kb_ref, vb_ref = (kp_ref, vp_ref) if t == 0 else (kc_ref, vc_ref)
            mask = _band_mask(n_sub * pl.program_id(len(grid) - 1) if t == 0 else 1, dil, max_dist)
            k2s, qs, dys, scs, dps = [], [], [], [], []
            for kh in range(n_kv):
                k2s.append(jnp.concatenate([_ld(kb_ref, _hs(kh), *before), _ld(kc_ref, _hs(kh), rows)],
                                           axis=0).astype(BF))
                v2 = jnp.concatenate([_ld(vb_ref, _hs(kh), *before), _ld(vc_ref, _hs(kh), rows)], axis=0).astype(BF)
                for h in range(kh * n_rep, (kh + 1) * n_rep):
                    qs.append((_ld(q_ref, _hs(h), rows) * SCALE).astype(BF))
                    dys.append(_ld(dy_ref, _hs(h), rows).astype(BF))
                    scs.append(jnp.where(mask, _nt(qs[h], k2s[kh]), NEG))
                    dps.append(_nt(dys[h], v2))
            for kh in range(n_kv):
                k2 = k2s[kh]
                dk2 = jnp.zeros((2 * TQ, HEAD_DIM), F32)
                dv2 = jnp.zeros((2 * TQ, HEAD_DIM), F32)
                for h in range(kh * n_rep, (kh + 1) * n_rep):
                    lse_h = _ld(lse_ref, slice(h * HEAD_DIM, h * HEAD_DIM + 1), rows)
                    dd_h = _ld(dd_ref, slice(h * HEAD_DIM, h * HEAD_DIM + 1), rows)
                    p = jnp.exp(scs[h] - lse_h)
                    ds = (p * (dps[h] - dd_h)).astype(BF)
                    _st(dq_ref, _hs(h), (_nn(ds, k2) * SCALE).astype(dt), rows)
                    dk2 = dk2 + _tn(ds, qs[h])
                    dv2 = dv2 + _tn(p.astype(BF), dys[h])
                _st(dkp_ref, _hs(kh), dk2[:TQ].astype(dt), rows)
                _st(dkc_ref, _hs(kh), dk2[TQ:].astype(dt), rows)
                _st(dvp_ref, _hs(kh), dv2[:TQ].astype(dt), rows)
                _st(dvc_ref, _hs(kh), dv2[TQ:].astype(dt), rows)

    args = [_strips(z)] * 5 + [_strips(a) for a in (dy, lse, dd)]
    pair = _p_spec(dil, A_WIDTH, 0, n_sub)
    in_specs = [pair, _p_spec(dil, kw, kcol, n_sub, True), _p_spec(dil, kw, kcol, n_sub),
                _p_spec(dil, kw, vcol, n_sub, True), _p_spec(dil, kw, vcol, n_sub)] + [pair] * 3
    out_specs = [pair] + [_p_spec(dil, kw, 0, n_sub)] * 4
    na = s // N_STRIPS
    out_shape = [jax.ShapeDtypeStruct((4, 4, na, A_WIDTH), dt)] + [jax.ShapeDtypeStruct((4, 4, na, kw), dt)] * 4
    res = pl.pallas_call(
        body, grid=grid, name=name, in_specs=in_specs, out_specs=out_specs, out_shape=out_shape,
        compiler_params=_cparams(*(("parallel",) * len(grid))),
    )(*args)
    return [res[0].reshape(s, A_WIDTH)] + [a.reshape(s, kw) for a in res[1:]]


DZ_TA = 16


def _dz_assemble(parts_a, parts_c, dyb, zb, cw):
    s = zb.shape[0]
    na = s // N_STRIPS
    nb = na // DZ_TA

    def ahead(w, k):
        return pl.BlockSpec((4, 4, DZ_TA, w), lambda i: (0, 0, jnp.minimum(i + k, nb - 1), 0))

    args, in_specs = [], []
    for dil, (dq, dkp, dkc, dvp, dvc) in zip(DILATIONS + (1,), parts_a + [parts_c]):
        w = dkp.shape[1]
        here = _strip_rows(DZ_TA, w)
        if dil == 1:
            args += [dq, dkp, dkp, dkc, dvp, dvp, dvc]
            in_specs += [_strip_rows(DZ_TA, A_WIDTH), here, ahead(w, 1), here, here, ahead(w, 1), here]
        else:
            k = 8 * dil // DZ_TA
            args += [dq, dkp, dkc, dvp, dvc]
            in_specs += [_strip_rows(DZ_TA, A_WIDTH), ahead(w, k), here, ahead(w, k), here]
    n_att = len(args)
    args = [_strips(a) for a in args] + [_strips(dyb), _strips(dyb), _strips(zb), _strips(zb), _strips(zb), cw]
    in_specs += [_strip_rows(DZ_TA, CONV_CH), _next_rows(DZ_TA, CONV_CH, nb), _strip_rows(DZ_TA, ZB_W),
                 _prev_rows(DZ_TA, ZB_W), _next_rows(DZ_TA, ZB_W, nb), _whole((HALO, CONV_CH))]

    def body(*refs):
        att = list(refs[:n_att])
        dyb_ref, dybn_ref, zb_ref, zbp_ref, zbn_ref, cw_ref, dz_ref, dcw_ref = refs[n_att:]
        i = pl.program_id(0)

        @pl.when(i == 0)
        def _():
            dcw_ref[...] = jnp.zeros_like(dcw_ref)

        def shifted(dil):
            if dil == 1:
                dq_r, kp0, kp1, dkc_r, vp0, vp1, dvc_r = [att.pop(0) for _ in range(7)]
                live = i + 1 < nb
                half = DZ_TA // 2
                kp0, kp1, vp0, vp1 = [r[...].astype(F32) for r in (kp0, kp1, vp0, vp1)]
                dkp = jnp.concatenate([kp0[:, :, half:, :], jnp.where(live, kp1[:, :, :half, :], 0.0)], axis=2)
                dvp = jnp.concatenate([vp0[:, :, half:, :], jnp.where(live, vp1[:, :, :half, :], 0.0)], axis=2)
            else:
                dq_r, dkp_r, dkc_r, dvp_r, dvc_r = [att.pop(0) for _ in range(5)]
                live = i + 8 * dil // DZ_TA < nb
                dkp = jnp.where(live, dkp_r[...].astype(F32), 0.0)
                dvp = jnp.where(live, dvp_r[...].astype(F32), 0.0)
            return dq_r[...].astype(F32), dkc_r[...].astype(F32) + dkp, dvc_r[...].astype(F32) + dvp

        dq, dk, dv = shifted(DILATIONS[0])
        for dil in DILATIONS[1:]:
            dq2, dk2, dv2 = shifted(dil)
            dq, dk, dv = dq + dq2, dk + dk2, dv + dv2
        dz_ref[:, :, :, 0:A_WIDTH] = dq.astype(BF)
        dz_ref[:, :, :, A_WIDTH:2 * A_WIDTH] = dk.astype(BF)
        dz_ref[:, :, :, 2 * A_WIDTH:ZA_W] = dv.astype(BF)
        dq, dk, dv = shifted(1)
        c0 = ZA_W + ZB_W
        dz_ref[:, :, :, c0:c0 + A_WIDTH] = dq.astype(BF)
        dz_ref[:, :, :, c0 + A_WIDTH:c0 + A_WIDTH + C_KV_WIDTH] = dk.astype(BF)
        dz_ref[:, :, :, c0 + A_WIDTH + C_KV_WIDTH:IN_WIDTH] = dv.astype(BF)

        cw = cw_ref[...]
        prev = jnp.where(i > 0, zbp_ref[...], 0.0)
        gb, gc, xb, u, u1, u2, c = _conv_strips(zb_ref[...], prev, cw)
        dyb = dyb_ref[...]
        dc = [_strip(dyb, b) * gb[b] for b in range(N_STRIPS)]
        dcn = jnp.where(i + 1 < nb, dybn_ref[...] * zbn_ref[:, :, :CONV_CH], 0.0)
        wrapped = [_shift_up(dc[0], 1, dcn[0]), _shift_up(dc[1], 1, dcn[1])]
        upd = [jnp.zeros((1, CONV_CH), F32)] * 3
        for b in range(N_STRIPS):
            dc1 = dc[b + 1] if b + 1 < N_STRIPS else wrapped[0]
            dc2 = dc[b + 2] if b + 2 < N_STRIPS else wrapped[b + 2 - N_STRIPS]
            du = cw[2:3, :] * dc[b] + cw[1:2, :] * dc1 + cw[0:1, :] * dc2
            f, e = b % 4, b // 4
            dz_ref[f, e, :, ZA_W:ZA_W + CONV_CH] = (_strip(dyb, b) * c[b]).astype(BF)
            dz_ref[f, e, :, ZA_W + CONV_CH:ZA_W + 2 * CONV_CH] = (du * xb[b]).astype(BF)
            dz_ref[f, e, :, ZA_W + 2 * CONV_CH:c0] = (du * gc[b]).astype(BF)
            for t, uu in enumerate((u2[b], u1[b], u[b])):
                upd[t] = upd[t] + jnp.sum(dc[b] * uu, axis=0, keepdims=True)
        row = lax.broadcasted_iota(jnp.int32, (HALO, CONV_CH), 0)
        tile = jnp.zeros((HALO, CONV_CH), F32)
        for t in range(3):
            tile = jnp.where(row == t, upd[t], tile)
        dcw_ref[...] += tile

    dz, dcw = pl.pallas_call(
        body, grid=(nb,), name="dz_assemble", in_specs=in_specs,
        out_specs=[_strip_rows(DZ_TA, IN_WIDTH), _whole((HALO, CONV_CH))],
        out_shape=[jax.ShapeDtypeStruct((4, 4, na, IN_WIDTH), BF), jax.ShapeDtypeStruct((HALO, CONV_CH), F32)],
        compiler_params=_cparams("arbitrary"),
    )(*args)
    return dz.reshape(s, IN_WIDTH), dcw


def _qkv_bwd(dz, dx1, x, g, w_all, l, tb, tokens_out):
    s, d = x.shape
    na, ta = s // N_STRIPS, tb // N_STRIPS

    def body(dz_ref, dx1_ref, x_ref, g_ref, w_ref, dx_ref, dg_ref):
        i = pl.program_id(0)

        @pl.when(i == 0)
        def _():
            dg_ref[...] = jnp.zeros_like(dg_ref)

        n = IN_WIDTH // N_CHIPS
        dz = dz_ref[...].reshape(tb, IN_WIDTH)
        dh = _nt(dz[:, 0:n], w_ref[0])
        for k in range(1, N_CHIPS):
            dh = dh + _nt(dz[:, k * n:(k + 1) * n], w_ref[k])
        xv = x_ref[...].reshape(tb, d)
        r = _rms_scale(xv)
        xhat = xv * r
        dg_ref[...] += jnp.sum(dh * xhat, axis=0, keepdims=True)
        dx = (dx1_ref[...].reshape(tb, d) + _norm_bwd(dh * g_ref[...], xhat, r)).reshape(4, 4, ta, d)
        if tokens_out:
            for b in range(N_STRIPS):
                dx_ref[:, b, :] = _strip(dx, b)
        else:
            dx_ref[...] = dx

    if tokens_out:
        dx_spec, dx_shape = pl.BlockSpec((ta, N_STRIPS, d), lambda i: (i, 0, 0)), (na, N_STRIPS, d)
    else:
        dx_spec, dx_shape = _strip_rows(ta, d), (4, 4, na, d)
    dx, dg = pl.pallas_call(
        body, grid=(s // tb,), name="qkv_bwd",
        in_specs=[_strip_rows(ta, IN_WIDTH), _strip_rows(ta, d), _strip_rows(ta, d), _whole((1, d)),
                  _layer((N_CHIPS, d, IN_WIDTH // N_CHIPS), l)],
        out_specs=[dx_spec, _whole((HALO, d))],
        out_shape=[jax.ShapeDtypeStruct(dx_shape, F32), jax.ShapeDtypeStruct((HALO, d), F32)],
        compiler_params=_cparams("arbitrary"),
    )(_strips(dz), _strips(dx1), _strips(x), g, w_all)
    return dx.reshape(s, d), dg


def _tile_rows(rows):
    return jnp.pad(rows, ((0, HALO - rows.shape[0]), (0, 0)))


def _to_strips(a, after, name):
    s, d = a.shape
    na = s // N_STRIPS
    ta = min(32, na)

    def body(a_ref, *rest):
        for b in range(N_STRIPS):
            rest[-1][b % 4, b // 4] = a_ref[:, b, :]

    return pl.pallas_call(
        body, grid=(na // ta,), name=name,
        in_specs=[pl.BlockSpec((ta, N_STRIPS, d), lambda i: (i, 0, 0))] + [ANY] * len(after),
        out_specs=_strip_rows(ta, d),
        out_shape=jax.ShapeDtypeStruct((4, 4, na, d), a.dtype), compiler_params=_cparams("parallel"),
    )(a.reshape(na, N_STRIPS, d), *after).reshape(s, d)


def _local_step(x, tgt, fetch, ff, sinks, g_mix, g_group, g_mlp, g_final, emit):
    s, d = x.shape
    depth = g_mix.shape[0]
    tb = min(512, s)
    tf = ff // N_CHIPS
    ts = min(1024, s)
    saved = []
    for l in range(depth):
        w_in, _, _, _, conv_w = fetch(0, l, x)
        cw = _tile_rows(conv_w[l])
        sk = jnp.repeat(sinks[l].reshape(N_HEADS), HEAD_DIM)[None]
        h, za, zb, zc = _qkv_fwd(x, g_mix[l][None], w_in, l, tb)
        parts_a = [_attn_fwd(za, dil, A_WIDTH, 1, 2, 1, A_MAX_DIST, "attn_a_fwd_%d" % dil) for dil in DILATIONS]
        part_c = _attn_fwd(zc, 1, C_KV_WIDTH, 3, 4, C_GROUP, C_MAX_DIST, "attn_c_fwd")
        ya, lse_a, yc, lse_c = _attn_merge(parts_a, part_c, sk, ts)
        w_in, w_o, w1, w2, _ = fetch(1, l, yc)
        x1, yb = _mix_fwd(x, ya, yc, zb, cw, g_group[l][None], w_o, l, ts)
        w_in, w_o, w1, w2, _ = fetch(2, l, x1)
        x2, h2, ap = _mlp_fwd(x1, g_mlp[l][None], w1, w2, l, ts, tf)
        saved.append((x, h, za, zb, zc, ya, lse_a, yc, lse_c, yb, x1, h2, ap, cw, sk))
        x = x2
    dx, loss_tile, dg_final = _loss_head(x, g_final[None], tgt, ts)
    grads = [None] * depth
    tok = jnp.zeros((), F32)
    for l in reversed(range(depth)):
        x0, h, za, zb, zc, ya, lse_a, yc, lse_c, yb, x1, h2, ap, cw, sk = saved[l]
        dx1, dap, dg_mlp = _mlp_bwd(dx, x1, ap, g_mlp[l][None] + tok, w1, w2, l, ts, tf)
        tok = emit(l, 3, _wgrad(ap, dx, min(1024, ff), d, 2 * ts, "wgrad_ff_out", relu2=True))
        tok = tok + emit(l, 2, _wgrad(h2, dap, d, min(1024, ff), 2 * ts, "wgrad_ff_in"))
        n, dya, dyc, dd_a, dd_c, dyb, dg_group, dsink = _mix_bwd(dx1, ya, yb, yc, lse_c, sk, g_group[l][None] + tok,
                                                                 w_o, l, tb)
        tok = emit(l, 1, _wgrad(n, dx1, MIX_WIDTH, d, 2 * ts, "wgrad_o"))
        cw = cw + tok
        parts_a = [_attn_bwd(za, dya, lse_a, dd_a, dil, A_WIDTH, 1, 2, 1, A_MAX_DIST, "attn_a_bwd_%d" % dil)
                   for dil in DILATIONS]
        parts_c = _attn_bwd(zc, dyc, lse_c, dd_c, 1, C_KV_WIDTH, 3, 4, C_GROUP, C_MAX_DIST, "attn_c_bwd")
        dz, dcw = _dz_assemble(parts_a, parts_c, dyb, zb, cw)
        tok = emit(l, 0, _wgrad(h, dz, d, IN_WIDTH // 4, 2 * ts, "wgrad_in"))
        dx, dg_mix = _qkv_bwd(dz, dx1, x0, g_mix[l][None] + tok, w_in, l, tb, l == 0)
        grads[l] = (dcw, dsink, dg_mix, dg_group, dg_mlp)
    return loss_tile, dx, grads, dg_final


ANY = pl.BlockSpec(memory_space=pl.ANY)
SHARD_AXES = (2, 1, 2, 1)
N_BIG = len(SHARD_AXES)
N_CHIPS = 4
N_DEV = 8


def _mesh_pos():
    return lax.axis_index("x"), lax.axis_index("y"), lax.axis_index("c")


def _flip(v, bit):
    return 1 - v if bit else v


def _place_shard(shard, chip_arr, name):
    _, rows, cols = shard.shape
    tr = min(256, rows)

    def body(chip_ref, x_ref, o_ref):
        o_ref[...] = x_ref[...].astype(BF)

    return pl.pallas_call(
        body, name=name,
        grid_spec=pltpu.PrefetchScalarGridSpec(
            num_scalar_prefetch=1, grid=(2, rows // tr),
            in_specs=[pl.BlockSpec((None, tr, cols), lambda l, i, chip: (l, i, 0))],
            out_specs=pl.BlockSpec((None, None, tr, cols), lambda l, i, chip: (l, chip[0], i, 0))),
        out_shape=jax.ShapeDtypeStruct((2, N_CHIPS, rows, cols), BF),
        compiler_params=_cparams("parallel", "parallel"),
    )(chip_arr, shard)


HBM = pl.BlockSpec(memory_space=pltpu.HBM)
SEM = pl.BlockSpec(memory_space=pltpu.SEMAPHORE)
EFFECT = pltpu.SideEffectType.DATAFLOW_SIDE_EFFECTING

GATHER_GROUPS = (((0, 0),), ((1, 0),), ((2, 0), (3, 0)), ((0, 1),), ((1, 1), (2, 1), (3, 1)))
GATHER_STARTS = ((0,), (1, 2), (3, 4))
GATHER_STAGES = {(0, 0): 0, (1, 0): 1, (2, 0): 2, (0, 1): 3, (1, 1): 4}


def _gather_copies(arrs, group, send_sems, recv_sems):
    x, y, c = _mesh_pos()
    me = 2 * x + y
    out = []
    for i, (w, layer) in enumerate(group):
        mine = arrs[w].at[layer, me]
        for j, (qx, qy) in enumerate([(1 - x, y), (x, 1 - y), (1 - x, 1 - y)]):
            landed = arrs[w].at[layer, 2 * qx + qy]
            out.append(tuple(pltpu.make_async_remote_copy(
                src_ref=piece, dst_ref=piece, send_sem=send_sems.at[i * 3 + j], recv_sem=recv_sems.at[i * 3 + j],
                device_id=(qx, qy, c), device_id_type=MESH) for piece in (mine, landed)))
    return out


def _conv_copies(conv_src, conv_dst, send_sems, recv_sems):
    x, y, c = _mesh_pos()
    out = []
    for j, (qx, qy) in enumerate([(1 - x, y), (x, 1 - y), (1 - x, 1 - y)]):
        out.append(tuple(pltpu.make_async_remote_copy(
            src_ref=conv_src, dst_ref=conv_dst.at[q], send_sem=send_sems.at[j], recv_sem=recv_sems.at[j],
            device_id=(qx, qy, c), device_id_type=MESH) for q in (2 * x + y, 2 * qx + qy)))
    return out


def _gather_start(groups, arrs, conv, name, through=None):
    n_sems = 2 * (len(groups) + (conv is not None))
    mats = sorted({w for g in groups for w, _ in GATHER_GROUPS[g]})

    def body(*refs):
        arrs_ref = [None] * N_BIG
        for w, ref in zip(mats, refs):
            arrs_ref[w] = ref
        sems = refs[n_in:n_in + n_sems]
        if conv is not None:
            for cp, _ in _conv_copies(refs[len(mats)], refs[len(mats) + 1], sems[-2], sems[-1]):
                cp.start()
        for k, g in enumerate(groups):
            for cp, _ in _gather_copies(arrs_ref, GATHER_GROUPS[g], sems[2 * k], sems[2 * k + 1]):
                cp.start()

    sem_shapes = []
    for n in [len(GATHER_GROUPS[g]) for g in groups] + ([1] if conv is not None else []):
        sem_shapes += [pltpu.SemaphoreType.DMA((3 * n,))] * 2
    operands = [arrs[w] for w in mats] + ([] if conv is None else list(conv)) + ([] if through is None else [through])
    n_in = len(operands)
    res = pl.pallas_call(
        body, name=name,
        out_shape=tuple(sem_shapes) + tuple(pltpu.HBM(a.shape, a.dtype) for a in operands),
        in_specs=(HBM,) * n_in, out_specs=(SEM,) * n_sems + (HBM,) * n_in,
        input_output_aliases={i: n_sems + i for i in range(n_in)},
        compiler_params=pltpu.CompilerParams(has_side_effects=EFFECT),
    )(*[pltpu.with_memory_space_constraint(a, pltpu.HBM) for a in operands])
    arrs = list(arrs)
    for w, a in zip(mats, res[n_sems:]):
        arrs[w] = a
    return res[:n_sems], arrs, list(res[n_sems + len(mats):])


def _gather_wait(k, sems, arrs, conv, after, name):
    group = GATHER_GROUPS[k]
    mats = sorted({w for w, _ in group})
    n_conv = 0 if conv is None else 2

    def body(*refs):
        local = refs[:len(mats)]
        arrs_ref = [None] * N_BIG
        for w, ref in zip(mats, local):
            arrs_ref[w] = ref
        pos = len(mats) + n_conv
        copies = _gather_copies(arrs_ref, group, refs[pos], refs[pos + 1])
        if conv is not None:
            copies += _conv_copies(refs[len(mats)], refs[len(mats) + 1], refs[pos + 2], refs[pos + 3])
        for send, recv in copies:
            recv.wait_recv()
            send.wait_send()

    operands = [arrs[w] for w in mats] + ([] if conv is None else [conv[1], conv[2]])
    sem_ops = list(sems) + ([] if conv is None else list(conv[0]))
    n_op = len(operands)
    res = pl.pallas_call(
        body, name=name, out_shape=tuple(pltpu.HBM(a.shape, a.dtype) for a in operands),
        in_specs=(HBM,) * n_op + (SEM,) * len(sem_ops) + (ANY,) * len(after), out_specs=(HBM,) * n_op,
        input_output_aliases={i: i for i in range(n_op)},
        compiler_params=pltpu.CompilerParams(has_side_effects=EFFECT),
    )(*operands, *sem_ops, *after)
    arrs = list(arrs)
    for w, a in zip(mats, res):
        arrs[w] = a
    return arrs, (res[-1] if conv is not None else None)


def _grad_shard(ref, w, chip, n):
    start = pl.multiple_of(chip * n, 128)
    if SHARD_AXES[w] == 2:
        return ref.at[:, pl.ds(start, n)]
    return ref.at[pl.ds(start, n), :]


def _slot_shape(g, w):
    shape = list(g.shape)
    shape[SHARD_AXES[w] - 1] //= N_CHIPS
    return (N_DEV - 1,) + tuple(shape)


def _scatter_copies(g_ref, land_ref, send_sems, recv_sems, layer, w):
    x, y, c = _mesh_pos()
    n = g_ref.shape[SHARD_AXES[w] - 1] // N_CHIPS
    out = []
    for r in range(1, N_DEV):
        tx, ty, tc = _flip(x, r & 4), _flip(y, r & 2), _flip(c, r & 1)
        cp = pltpu.make_async_remote_copy(
            src_ref=_grad_shard(g_ref, w, 2 * tx + ty, n), dst_ref=land_ref.at[r - 1], send_sem=send_sems.at[r - 1],
            recv_sem=recv_sems.at[r - 1], device_id=(tx, ty, tc), device_id_type=MESH)
        out.append((cp, (c != layer) if r & 1 else (c == layer)))
    return out


def _scatter_start(items, layer, name):
    n = len(items)

    def body(*refs):
        for i, (w, _, _) in enumerate(items):
            g_ref, land_ref = refs[2 * i], refs[2 * i + 1]
            send_sems, recv_sems = refs[2 * n + 2 * i], refs[2 * n + 2 * i + 1]
            for cp, mine in _scatter_copies(g_ref, land_ref, send_sems, recv_sems, layer, w):
                @pl.when(mine)
                def _():
                    cp.start()
        refs[-1][...] = jnp.zeros_like(refs[-1])

    operands = [a for _, g, land in items for a in (g, land)]
    res = pl.pallas_call(
        body, name=name,
        out_shape=(pltpu.SemaphoreType.DMA((N_DEV - 1,)),) * (2 * n)
        + tuple(pltpu.HBM(a.shape, a.dtype) for a in operands) + (jax.ShapeDtypeStruct((HALO, 128), F32),),
        in_specs=(HBM,) * (2 * n),
        out_specs=(SEM,) * (2 * n) + (HBM,) * (2 * n) + (pl.BlockSpec(memory_space=pltpu.VMEM),),
        input_output_aliases={i: 2 * n + i for i in range(2 * n)},
        compiler_params=pltpu.CompilerParams(has_side_effects=EFFECT),
    )(*[pltpu.with_memory_space_constraint(a, pltpu.HBM) for a in operands])
    return [(res[2 * i], res[2 * i + 1], res[2 * n + 2 * i], res[2 * n + 2 * i + 1]) for i in range(n)], res[-1]


def _scatter_wait(started, land, after, w, name):
    def body(g0_ref, g1_ref, land_ref, ss0, rs0, ss1, rs1, after_ref, g0_out, g1_out, land_out):
        c = lax.axis_index("c")
        for layer, g_ref, ss, rs in ((0, g0_ref, ss0, rs0), (1, g1_ref, ss1, rs1)):
            for cp, mine in _scatter_copies(g_ref, land_ref, ss, rs, layer, w):
                @pl.when(mine)
                def _():
                    cp.wait_send()

                @pl.when(c == layer)
                def _():
                    cp.wait_recv()

    (ss0, rs0, g0), (ss1, rs1, g1) = started
    return pl.pallas_call(
        body, name=name,
        out_shape=(pltpu.HBM(g0.shape, g0.dtype), pltpu.HBM(g1.shape, g1.dtype), pltpu.HBM(land.shape, land.dtype)),
        in_specs=(HBM, HBM, HBM, SEM, SEM, SEM, SEM, ANY), out_specs=(HBM, HBM, HBM),
        input_output_aliases={0: 0, 1: 1, 2: 2}, compiler_params=pltpu.CompilerParams(has_side_effects=EFFECT),
    )(g0, g1, land, ss0, rs0, ss1, rs1, after)


def _sum_slots(g0, g1, slots, w, pos_arr, name):
    _, rows, cols = slots.shape
    tr = min(512, rows)
    nr = rows // tr
    if SHARD_AXES[w] == 2:
        own = pl.BlockSpec((tr, cols), lambda i, pos: (i, pos[0]))
    else:
        own = pl.BlockSpec((tr, cols), lambda i, pos: (pos[0] * nr + i, 0))

    def body(pos_ref, own0_ref, own1_ref, s_ref, o_ref):
        acc = jnp.where(pos_ref[1] == 0, own0_ref[...], own1_ref[...]).astype(F32)
        for r in range(N_DEV - 1):
            acc = acc + s_ref[r].astype(F32)
        o_ref[...] = acc

    return pl.pallas_call(
        body, name=name,
        grid_spec=pltpu.PrefetchScalarGridSpec(
            num_scalar_prefetch=1, grid=(nr,),
            in_specs=[own, own, pl.BlockSpec((N_DEV - 1, tr, cols), lambda i, pos: (0, i, 0))],
            out_specs=pl.BlockSpec((tr, cols), lambda i, pos: (i, 0))),
        out_shape=jax.ShapeDtypeStruct((rows, cols), F32), compiler_params=_cparams("parallel"),
    )(pos_arr, g0, g1, slots)


def _swap_copies(refs, n):
    x, y, c = _mesh_pos()
    return [pltpu.make_async_remote_copy(src_ref=refs[w], dst_ref=refs[n + w], send_sem=refs[2 * n].at[w],
                                         recv_sem=refs[2 * n + 1].at[w], device_id=(x, y, 1 - c), device_id_type=MESH)
            for w in range(n)]


def _swap_start(halves, name):
    n = len(halves)

    def body(*refs):
        for cp in _swap_copies(refs, n):
            cp.start()

    operands = list(halves) + [lax.empty(h.shape, h.dtype) for h in halves]
    res = pl.pallas_call(
        body, name=name,
        out_shape=(pltpu.SemaphoreType.DMA((n,)),) * 2 + tuple(pltpu.HBM(a.shape, a.dtype) for a in operands),
        in_specs=(HBM,) * (2 * n), out_specs=(SEM,) * 2 + (HBM,) * (2 * n),
        input_output_aliases={i: 2 + i for i in range(2 * n)},
        compiler_params=pltpu.CompilerParams(has_side_effects=EFFECT),
    )(*[pltpu.with_memory_space_constraint(a, pltpu.HBM) for a in operands])
    return res[0], res[1], list(res[2:2 + n]), list(res[2 + n:])


def _swap_wait(send_sems, recv_sems, halves, lands, after, name):
    n = len(halves)

    def body(*refs):
        for cp in _swap_copies(refs, n):
            cp.wait_send()
            cp.wait_recv()

    operands = list(halves) + list(lands)
    res = pl.pallas_call(
        body, name=name, out_shape=tuple(pltpu.HBM(a.shape, a.dtype) for a in operands),
        in_specs=(HBM,) * (2 * n) + (SEM, SEM, ANY), out_specs=(HBM,) * (2 * n),
        input_output_aliases={i: i for i in range(2 * n)},
        compiler_params=pltpu.CompilerParams(has_side_effects=EFFECT),
    )(*operands, send_sems, recv_sems, after)
    return list(res[n:])


def _adamw_math(w, g, m, v):
    m = ADAM_B1 * m + (1.0 - ADAM_B1) * g
    v = ADAM_B2 * v + (1.0 - ADAM_B2) * jnp.square(g)
    m_hat = m / (1.0 - ADAM_B1 ** ADAM_STEP)
    v_hat = v / (1.0 - ADAM_B2 ** ADAM_STEP)
    delta = -ADAM_LR * (m_hat / (jnp.sqrt(v_hat) + ADAM_EPS) + ADAM_WD * w)
    return delta, m, v


def _adamw(w, g, m, v, filled, pos_arr, name):
    shape = w.shape
    _, rows, cols = shape
    tr = min(256, rows)

    def body(pos_ref, w_ref, g_ref, m_ref, v_ref, *rest):
        go_ref, d_ref, m2_ref, v2_ref = rest[-4:]
        g = g_ref[...]
        go_ref[...] = g
        d_ref[...], m2_ref[...], v2_ref[...] = _adamw_math(w_ref[...], g, m_ref[...], v_ref[...])

    def layer(pos):
        return pos[1] if filled is None else 1 - pos[1]

    full = pl.BlockSpec((None, tr, cols), lambda i, pos: (layer(pos), i, 0))
    half = pl.BlockSpec((tr, cols), lambda i, pos: (i, 0))
    n_in = 5
    return pl.pallas_call(
        body, name=name,
        grid_spec=pltpu.PrefetchScalarGridSpec(
            num_scalar_prefetch=1, grid=(rows // tr,),
            in_specs=[full, half, full, full] + ([] if filled is None else [ANY] * 4), out_specs=[full] * 4),
        out_shape=[jax.ShapeDtypeStruct(shape, F32)] * 4,
        input_output_aliases={} if filled is None else {n_in + k: k for k in range(4)},
        compiler_params=_cparams("parallel"),
    )(pos_arr, w, g, m, v, *([] if filled is None else filled))


def _small_sync(part, w, m, v):
    rows, cols = part.shape

    def body(p_ref, w_ref, m_ref, v_ref, g_ref, d_ref, m2_ref, v2_ref, slots, send_sems, recv_sems):
        x, y, c = _mesh_pos()
        me = 4 * x + 2 * y + c
        slots[me] = p_ref[...]
        sends = []
        for r in range(1, N_DEV):
            to = (_flip(x, r & 4), _flip(y, r & 2), _flip(c, r & 1))
            sends.append(pltpu.make_async_remote_copy(
                src_ref=p_ref, dst_ref=slots.at[me], send_sem=send_sems.at[r - 1], recv_sem=recv_sems.at[r - 1],
                device_id=to, device_id_type=MESH))
        for cp in sends:
            cp.start()
        for cp in sends:
            cp.wait_recv()
        for cp in sends:
            cp.wait_send()
        g = slots[0]
        for i in range(1, N_DEV):
            g = g + slots[i]
        g_ref[...] = g
        d_ref[...], m2_ref[...], v2_ref[...] = _adamw_math(w_ref[...], g, m_ref[...], v_ref[...])

    vm = pl.BlockSpec(memory_space=pltpu.VMEM)
    return pl.pallas_call(
        body, name="small_sync", in_specs=[vm] * 4, out_specs=[vm] * 4,
        out_shape=[jax.ShapeDtypeStruct((rows, cols), F32)] * 4,
        scratch_shapes=[pltpu.VMEM((N_DEV, rows, cols), F32), pltpu.SemaphoreType.DMA((N_DEV - 1,)),
                        pltpu.SemaphoreType.DMA((N_DEV - 1,))],
    )(part, w, m, v)


PACK_W = 256


def _pack_rows(n):
    return -(-n // (HALO * PACK_W)) * HALO


def _pack_small(parts):
    out = []
    for a in parts:
        flat = a.reshape(-1)
        out.append(jnp.pad(flat, (0, _pack_rows(flat.size) * PACK_W - flat.size)).reshape(-1, PACK_W))
    return jnp.concatenate(out, axis=0)


def _unpack_small(p, shapes):
    out, row = [], 0
    for shape in shapes:
        n = 1
        for k in shape:
            n *= k
        out.append(p[row:row + _pack_rows(n)].reshape(-1)[:n].reshape(shape))
        row += _pack_rows(n)
    return out


def kernel(x, w_in, conv_w, sinks, g_mix, g_group, w_o, g_mlp, w_ff_in, w_ff_out, g_final, loss_target, m_w_in, m_conv_w, m_sinks, m_g_mix, m_g_group, m_w_o, m_g_mlp, m_w_ff_in, m_w_ff_out, m_g_final, v_w_in, v_conv_w, v_sinks, v_g_mix, v_g_group, v_w_o, v_g_mlp, v_w_ff_in, v_w_ff_out, v_g_final):
    chip = 2 * lax.axis_index("x") + lax.axis_index("y")
    conv_n = conv_w.shape[2]

    pos_arr = jnp.stack([chip, lax.axis_index("c")]).astype(jnp.int32)
    shards = (w_in, w_o, w_ff_in, w_ff_out)
    conv_tile = jnp.pad(conv_w.reshape(6, conv_n), ((0, HALO - 6), (0, 128 - conv_n)))
    placed = [_place_shard(w_in, pos_arr[:1], "place_shard_0"), None, None, None]
    sems_a, placed, conv_thru = _gather_start(
        GATHER_STARTS[0], placed, (conv_tile, lax.empty((N_CHIPS,) + conv_tile.shape, conv_tile.dtype)),
        "gather_start_0")
    for i in range(1, N_BIG):
        placed[i] = _place_shard(shards[i], pos_arr[:1], "place_shard_%d" % i)
    full = {"arrs": placed, "conv": None, "sems": list(sems_a[:2])}
    target = _to_strips(loss_target[0], placed[:1], "to_strips_target")

    def fetch(stage, layer, after):
        k = GATHER_STAGES.get((stage, layer))
        if k is None:
            return (*full["arrs"], full["conv"])
        sems = full["sems"][2 * k:2 * k + 2]
        if k == 0:
            full["arrs"], land = _gather_wait(0, sems, full["arrs"], (sems_a[-2:], *conv_thru), (after, target),
                                              "gather_wait_0")
            conv_all = lax.dynamic_update_slice(land, conv_tile[None], (chip, 0, 0))
            full["conv"] = conv_all[:, :6, :conv_n].reshape(N_CHIPS, 2, 3, conv_n).transpose(1, 2, 0, 3).reshape(
                2, 3, CONV_CH)
            sems_b, full["arrs"], rest = _gather_start(GATHER_STARTS[1], full["arrs"], None, "gather_start_1",
                                                       through=full["arrs"][0])
            full["arrs"][0] = rest[-1]
            full["sems"] += list(sems_b)
        else:
            full["arrs"], _ = _gather_wait(k, sems, full["arrs"], None, (after,), "gather_wait_%d" % k)
        if k == 2:
            sems_c, full["arrs"], _ = _gather_start(GATHER_STARTS[2], full["arrs"], None, "gather_start_2")
            full["sems"] += list(sems_c)
        return (*full["arrs"], full["conv"])

    lands, started, pending = [None] * N_BIG, {}, []

    def emit(layer, w, g):
        if lands[w] is None:
            lands[w] = lax.empty(_slot_shape(g, w), g.dtype)
        pending.append((w, g, lands[w]))
        if not (w == 0 or (layer == 0 and w == 1)):
            return jnp.zeros((), F32)
        name = "scatter_start_%d_%d" % (layer, len(pending))
        done, token = _scatter_start(list(pending), layer, name)
        for (w_i, _, _), (ss, rs, g_thru, land) in zip(pending, done):
            started[layer, w_i], lands[w_i] = (ss, rs, g_thru), land
        pending.clear()
        return token[0, 0]

    loss_tile, dx, grads, dg_final = _local_step(_to_strips(x[0], placed, "to_strips_x"), target, fetch,
                                                 w_ff_in.shape[2] * N_CHIPS,
                                                 sinks, g_mix, g_group, g_mlp, g_final, emit)

    wmv = ((w_in, m_w_in, v_w_in), (w_o, m_w_o, v_w_o), (w_ff_in, m_w_ff_in, v_w_ff_in),
           (w_ff_out, m_w_ff_out, v_w_ff_out))
    big, after = [None] * N_BIG, dx
    for name, ws in (("swap_rest", (1, 2, 3)), ("swap_in", (0,))):
        own = []
        for w in ws:
            g0, g1, slots = _scatter_wait((started[0, w], started[1, w]), lands[w], after, w, "scatter_wait_%d" % w)
            own.append(_sum_slots(g0, g1, slots, w, pos_arr, "sum_slots_%d" % w))
        send_sems, recv_sems, own, zones = _swap_start(own, name + "_start")
        for w, g in zip(ws, own):
            big[w] = _adamw(wmv[w][0], g, wmv[w][1], wmv[w][2], None, pos_arr, "adamw_own_%d" % w)
        theirs = _swap_wait(send_sems, recv_sems, own, zones, big[ws[-1]][1], name + "_wait")
        for w, g in zip(ws, theirs):
            big[w] = _adamw(wmv[w][0], g, wmv[w][1], wmv[w][2], big[w], pos_arr, "adamw_other_%d" % w)
        after = big[ws[-1]][1]

    def both(i):
        return jnp.stack([grads[0][i][0], grads[1][i][0]])
    dconv = jnp.stack([grads[0][0][:3], grads[1][0][:3]])
    dsinks = jnp.stack([grads[0][1][0, ::HEAD_DIM], grads[1][1][0, ::HEAD_DIM]])
    part = _pack_small([both(2), both(3), both(4), dg_final[0], dconv, dsinks, loss_tile[0, 0]])

    def spread(shard):
        return lax.dynamic_update_slice(jnp.zeros((2, 3, CONV_CH), F32), shard, (0, 0, chip * conv_n))
    zero = jnp.zeros((), F32)
    packs = [_pack_small([a, b, c_, e, spread(f), g_, zero]) for a, b, c_, e, f, g_ in (
        (g_mix, g_group, g_mlp, g_final, conv_w, sinks),
        (m_g_mix, m_g_group, m_g_mlp, m_g_final, m_conv_w, m_sinks),
        (v_g_mix, v_g_group, v_g_mlp, v_g_final, v_conv_w, v_sinks))]
    shapes = [g_mix.shape, g_group.shape, g_mlp.shape, g_final.shape, (2, 3, CONV_CH), sinks.shape, ()]
    small = [_unpack_small(p, shapes) for p in _small_sync(part, *packs)]

    def shard_of(full):
        return lax.dynamic_slice(full, (0, 0, chip * conv_n), (2, 3, conv_n))
    small = [(s[0], s[1], s[2], s[3], shard_of(s[4]), s[5], s[6]) for s in small]
    loss = small[0][6]

    def ordered(kind):
        b = [big[i][kind] for i in range(N_BIG)]
        s = small[kind]
        return [b[0], s[4], s[5], s[0], s[1], b[1], s[2], b[2], b[3], s[3]]

    return (loss, dx[None], *ordered(0), *ordered(1), *ordered(2), *ordered(3))
```

```python
import functools

import jax
import jax.numpy as jnp
from jax import lax
from jax.experimental import pallas as pl
from jax.experimental.pallas import tpu as pltpu

HEAD_DIM = 64
N_HEADS = 6
C_GROUP = 3
A_WIDTH = N_HEADS * HEAD_DIM
C_KV_WIDTH = 2 * HEAD_DIM
CONV_CH = 256
ZA_W = 3 * A_WIDTH
ZB_W = 3 * CONV_CH
ZC_W = A_WIDTH + 2 * C_KV_WIDTH
IN_WIDTH = ZA_W + ZB_W + ZC_W
MIX_WIDTH = A_WIDTH + CONV_CH + A_WIDTH
DILATIONS = (1, 4, 16)
A_MAX_DIST = 128
C_MAX_DIST = 127
TQ = 128
EPS = 1e-6
SCALE = HEAD_DIM ** -0.5
NEG = -1e30
HALO = 8

ADAM_LR = 0.001
ADAM_B1 = 0.9
ADAM_B2 = 0.999
ADAM_EPS = 1e-08
ADAM_WD = 0.01
ADAM_STEP = 10

BF = jnp.bfloat16
F32 = jnp.float32
MESH = pl.DeviceIdType.MESH
VMEM_LIMIT = 56 * 1024 * 1024


def _cparams(*sem):
    return pltpu.CompilerParams(dimension_semantics=sem, vmem_limit_bytes=VMEM_LIMIT)


def _nt(a, b):
    return lax.dot_general(a, b, (((1,), (1,)), ((), ())), preferred_element_type=F32)


def _tn(a, b):
    return lax.dot_general(a, b, (((0,), (0,)), ((), ())), preferred_element_type=F32)


def _nn(a, b):
    return jnp.dot(a, b, preferred_element_type=F32)


def _rows(tb, w):
    return pl.BlockSpec((tb, w), lambda i: (i, 0))


def _whole(shape):
    return pl.BlockSpec(shape, lambda *_: (0,) * len(shape))


def _layer(shape, l):
    return pl.BlockSpec((None,) + shape, lambda *_: (l,) + (0,) * len(shape))


def _rms_scale(v):
    return lax.rsqrt(jnp.mean(v * v, axis=-1, keepdims=True) + EPS)


def _norm_bwd(dxhat, xhat, r):
    return r * (dxhat - xhat * jnp.mean(dxhat * xhat, axis=-1, keepdims=True))


def _qkv_fwd(x, g, w_all, l, tb):
    s, d = x.shape

    def body(x_ref, g_ref, w_ref, h_ref, za_ref, zb_ref, zc_ref):
        xv = x_ref[...]
        h = ((xv * _rms_scale(xv)) * g_ref[...]).astype(BF)
        h_ref[...] = h
        z = jnp.concatenate([_nn(h, w_ref[k]) for k in range(N_CHIPS)], axis=1)
        za_ref[...] = z[:, :ZA_W]
        zb_ref[...] = z[:, ZA_W:ZA_W + ZB_W]
        zc_ref[...] = z[:, ZA_W + ZB_W:]

    return pl.pallas_call(
        body, grid=(s // tb,), name="qkv_fwd",
        in_specs=[_rows(tb, d), _whole((1, d)), _layer((N_CHIPS, d, IN_WIDTH // N_CHIPS), l)],
        out_specs=[_rows(tb, d), _rows(tb, ZA_W), _rows(tb, ZB_W), _rows(tb, ZC_W)],
        out_shape=[jax.ShapeDtypeStruct((s, d), BF), jax.ShapeDtypeStruct((s, ZA_W), F32),
                   jax.ShapeDtypeStruct((s, ZB_W), F32), jax.ShapeDtypeStruct((s, ZC_W), F32)],
        compiler_params=_cparams("parallel"),
    )(x, g, w_all)


N_STRIPS = 16


def _strips(a):
    s, w = a.shape
    return a.reshape(4, 4, s // N_STRIPS, w)


P_ROWS = {16: TQ, 4: 32, 1: 8}


def _p_sub(s, dil, most):
    while (s // dil // TQ) % most:
        most //= 2
    return most


def _p_grid(s, dil, n_sub):
    nb = s // dil // TQ // n_sub
    return {16: (4, 4, nb), 4: (4, nb), 1: (nb,)}[dil]


def _p_spec(dil, cw, col, n_sub, prev=False):
    rows = P_ROWS[dil] * (1 if prev else n_sub)

    def blk(j):
        return jnp.maximum(n_sub * j - 1, 0) if prev else j
    if dil == 16:
        return pl.BlockSpec((None, None, rows, cw), lambda f, e, j: (f, e, blk(j), col))
    if dil == 4:
        return pl.BlockSpec((None, 4, rows, cw), lambda f, j: (f, 0, blk(j), col))
    return pl.BlockSpec((4, 4, rows, cw), lambda j: (0, 0, blk(j), col))


def _block_pos(i, dil):
    if dil == 16:
        return i
    if dil == 4:
        return 4 * (i % 32) + i // 32
    return 16 * (i % 8) + 4 * ((i // 8) % 4) + i // 32


def _band_mask(b, dil, max_dist):
    qi = _block_pos(lax.broadcasted_iota(jnp.int32, (TQ, 2 * TQ), 0), dil)
    col = lax.broadcasted_iota(jnp.int32, (TQ, 2 * TQ), 1)
    cur = col >= TQ
    dist = qi - _block_pos(col % TQ, dil) + jnp.where(cur, 0, TQ)
    return (dist >= 0) & (dist <= max_dist) & (cur | (b > 0))


def _hs(h):
    return slice(h * HEAD_DIM, (h + 1) * HEAD_DIM)


def _ld(ref, cols, rows=slice(None)):
    v = ref[..., rows, cols]
    return v.reshape(TQ, v.shape[-1])


def _st(ref, cols, val, rows=slice(None)):
    lead = ref.shape[:-2] + (ref.shape[-2] if rows == slice(None) else rows.stop - rows.start,)
    ref[..., rows, cols] = val.reshape(lead + (val.shape[-1],))


def _attn_fwd(z, dil, kw, kcol, vcol, n_rep, max_dist, name):
    s, zw = z.shape
    n_sub = _p_sub(s, dil, 2)
    grid = _p_grid(s, dil, n_sub)
    o_dt = F32 if P_ROWS[dil] * n_sub < 16 else BF

    def body(q_ref, kp_ref, kc_ref, vp_ref, vc_ref, o_ref, lse_ref):
        for t in range(n_sub):
            rows = slice(t * P_ROWS[dil], (t + 1) * P_ROWS[dil])
            before = (slice(None),) if t == 0 else (slice((t - 1) * P_ROWS[dil], t * P_ROWS[dil]),)
            kb_ref, vb_ref = (kp_ref, vp_ref) if t == 0 else (kc_ref, vc_ref)
            mask = _band_mask(n_sub * pl.program_id(len(grid) - 1) if t == 0 else 1, dil, max_dist)
            scs, v2s = [], []
            for kh in range(N_HEADS // n_rep):
                k2 = jnp.concatenate([_ld(kb_ref, _hs(kh), *before), _ld(kc_ref, _hs(kh), rows)], axis=0).astype(BF)
                v2s.append(jnp.concatenate([_ld(vb_ref, _hs(kh), *before), _ld(vc_ref, _hs(kh), rows)],
                                           axis=0).astype(BF))
                for h in range(kh * n_rep, (kh + 1) * n_rep):
                    q = (_ld(q_ref, _hs(h), rows) * SCALE).astype(BF)
                    scs.append(jnp.where(mask, _nt(q, k2), NEG))
            for h, sc in enumerate(scs):
                m = jnp.max(sc, axis=1, keepdims=True)
                p = jnp.exp(sc - m)
                l = jnp.sum(p, axis=1, keepdims=True)
                _st(o_ref, _hs(h), (_nn(p.astype(BF), v2s[h // n_rep]) / l).astype(o_ref.dtype), rows)
                _st(lse_ref, _hs(h), jnp.broadcast_to(m + jnp.log(l), (TQ, HEAD_DIM)), rows)

    res = pl.pallas_call(
        body, grid=grid, name=name,
        in_specs=[_p_spec(dil, A_WIDTH, 0, n_sub), _p_spec(dil, kw, kcol, n_sub, True), _p_spec(dil, kw, kcol, n_sub),
                  _p_spec(dil, kw, vcol, n_sub, True), _p_spec(dil, kw, vcol, n_sub)],
        out_specs=[_p_spec(dil, A_WIDTH, 0, n_sub)] * 2,
        out_shape=[jax.ShapeDtypeStruct((4, 4, s // N_STRIPS, A_WIDTH), dt) for dt in (o_dt, F32)],
        compiler_params=_cparams(*(("parallel",) * len(grid))),
    )(*[_strips(z)] * 5)
    return [a.reshape(s, A_WIDTH) for a in res]


def _shift_down(v, n, halo):
    rows = v.shape[0]
    out = pltpu.roll(v, n, 0)
    row = lax.broadcasted_iota(jnp.int32, v.shape, 0)
    for t in range(n):
        out = jnp.where(row == t, halo[HALO - n + t:HALO - n + t + 1, :], out)
    return out


def _shift_up(v, n, halo):
    rows = v.shape[0]
    out = pltpu.roll(v, rows - n, 0)
    row = lax.broadcasted_iota(jnp.int32, v.shape, 0)
    for t in range(n):
        out = jnp.where(row == rows - n + t, halo[t:t + 1, :], out)
    return out


def _strip(v, b):
    return v[b % 4, b // 4]


def _conv_strips(zb, prev, cw):
    gb = [_strip(zb, b)[:, :CONV_CH] for b in range(N_STRIPS)]
    gc = [_strip(zb, b)[:, CONV_CH:2 * CONV_CH] for b in range(N_STRIPS)]
    xb = [_strip(zb, b)[:, 2 * CONV_CH:] for b in range(N_STRIPS)]
    u = [g * v for g, v in zip(gc, xb)]
    uh = prev[:, :, CONV_CH:2 * CONV_CH] * prev[:, :, 2 * CONV_CH:]
    wrapped = {14: _shift_down(u[14], 1, uh[2]), 15: _shift_down(u[15], 1, uh[3])}
    u1 = [u[b - 1] if b >= 1 else wrapped[15] for b in range(N_STRIPS)]
    u2 = [u[b - 2] if b >= 2 else wrapped[14 + b] for b in range(N_STRIPS)]
    c = [cw[0:1, :] * u2[b] + cw[1:2, :] * u1[b] + cw[2:3, :] * u[b] for b in range(N_STRIPS)]
    return gb, gc, xb, u, u1, u2, c


def _strip_rows(ta, w):
    return pl.BlockSpec((4, 4, ta, w), lambda i: (0, 0, i, 0))


def _prev_rows(ta, w):
    return pl.BlockSpec((4, None, HALO, w), lambda i: (0, 3, jnp.maximum(i * (ta // HALO) - 1, 0), 0))


def _next_rows(ta, w, nblk):
    return pl.BlockSpec((4, None, HALO, w),
                        lambda i: (0, 0, jnp.minimum((i + 1) * (ta // HALO), nblk * (ta // HALO) - 1), 0))


def _mix_fwd(x, parts_a, part_c, sink_row, zb, cw, gg, wo_all, l, tb):
    s, d = x.shape
    ta = tb // N_STRIPS
    n_a = len(parts_a)

    def body(*refs):
        x_ref, ins = refs[0], refs[1:2 * n_a + 3]
        sink_ref, zb_ref, zbp_ref, cw_ref, gg_ref, wo_ref = refs[2 * n_a + 3:2 * n_a + 9]
        x1_ref, yb_ref, ya_ref, lsea_ref, yc_ref, lsec_ref = refs[2 * n_a + 9:]
        lses = [ins[2 * p + 1][...].reshape(tb, A_WIDTH) for p in range(n_a)]
        m = functools.reduce(jnp.maximum, lses)
        ws = [jnp.exp(v - m) for v in lses]
        lsum = functools.reduce(jnp.add, ws)
        ya = functools.reduce(jnp.add, [w * ins[2 * p][...].reshape(tb, A_WIDTH).astype(F32)
                                        for p, w in enumerate(ws)]) / lsum
        lsea_ref[...] = (m + jnp.log(lsum)).reshape(4, 4, ta, A_WIDTH)
        o_c, lse_c = [r[...].reshape(tb, A_WIDTH).astype(F32) for r in ins[2 * n_a:]]
        sk = sink_ref[...]
        m2 = jnp.maximum(lse_c, sk)
        w = jnp.exp(lse_c - m2)
        l2 = w + jnp.exp(sk - m2)
        yc = o_c * (w / l2)
        lsec_ref[...] = (m2 + jnp.log(l2)).reshape(4, 4, ta, A_WIDTH)
        ya_ref[...] = ya.reshape(4, 4, ta, A_WIDTH)
        yc_ref[...] = yc.reshape(4, 4, ta, A_WIDTH)

        i = pl.program_id(0)
        prev = jnp.where(i > 0, zbp_ref[...], 0.0)
        gb, _, _, _, _, _, c = _conv_strips(zb_ref[...], prev, cw_ref[...])
        for b in range(N_STRIPS):
            yb_ref[b % 4, b // 4] = gb[b] * c[b]
        yb = yb_ref[...].reshape(tb, CONV_CH)
        n = jnp.concatenate([ya * _rms_scale(ya), yb * _rms_scale(yb), yc * _rms_scale(yc)], axis=1)
        n = (n * gg_ref[...]).astype(BF)
        x1 = x_ref[...].reshape(tb, d) + _nn(n, wo_ref[...].reshape(MIX_WIDTH, d))
        x1_ref[...] = x1.reshape(4, 4, ta, d)

    heads = _strip_rows(ta, A_WIDTH)
    na = s // N_STRIPS
    res = pl.pallas_call(
        body, grid=(s // tb,), name="mix_fwd",
        in_specs=[_strip_rows(ta, d)] + [heads] * (2 * n_a + 2)
        + [_whole((1, A_WIDTH)), _strip_rows(ta, ZB_W), _prev_rows(ta, ZB_W), _whole((HALO, CONV_CH)),
           _whole((1, MIX_WIDTH)), _layer((N_CHIPS, MIX_WIDTH // N_CHIPS, d), l)],
        out_specs=[_strip_rows(ta, d), _strip_rows(ta, CONV_CH)] + [heads] * 4,
        out_shape=[jax.ShapeDtypeStruct((4, 4, na, d), F32), jax.ShapeDtypeStruct((4, 4, na, CONV_CH), F32)]
        + [jax.ShapeDtypeStruct((4, 4, na, A_WIDTH), F32)] * 4,
        compiler_params=_cparams("parallel"),
    )(_strips(x), *[_strips(a) for part in parts_a + [part_c] for a in part], sink_row, _strips(zb), _strips(zb),
      cw, gg, wo_all)
    return [res[0].reshape(s, d), res[1].reshape(s, CONV_CH)] + [a.reshape(s, A_WIDTH) for a in res[2:]]


def _mlp_fwd(x1, g, w1_all, w2_all, l, tb, tf):
    s, d = x1.shape
    ff = w1_all.shape[1] * w1_all.shape[3]
    nj = ff // tf

    def body(x_ref, g_ref, w1_ref, w2_ref, x2_ref, h2_ref, ap_ref, acc):
        j = pl.program_id(1)

        @pl.when(j == 0)
        def _():
            xv = x_ref[...]
            h2_ref[...] = ((xv * _rms_scale(xv)) * g_ref[...]).astype(BF)
            acc[...] = jnp.zeros_like(acc)

        ap = _nn(h2_ref[...], w1_ref[...])
        ap_ref[...] = ap.astype(BF)
        a = jnp.square(jnp.maximum(ap, 0.0)).astype(BF)
        acc[...] += _nn(a, w2_ref[...])

        @pl.when(j == nj - 1)
        def _():
            x2_ref[...] = x_ref[...] + acc[...]

    return pl.pallas_call(
        body, grid=(s // tb, nj), name="mlp_fwd",
        in_specs=[pl.BlockSpec((tb, d), lambda i, j: (i, 0)), _whole((1, d)),
                  pl.BlockSpec((None, None, d, tf), lambda i, j: (l, j, 0, 0)),
                  pl.BlockSpec((None, None, tf, d), lambda i, j: (l, j, 0, 0))],
        out_specs=[pl.BlockSpec((tb, d), lambda i, j: (i, 0)), pl.BlockSpec((tb, d), lambda i, j: (i, 0)),
                   pl.BlockSpec((tb, tf), lambda i, j: (i, j))],
        out_shape=[jax.ShapeDtypeStruct((s, d), F32), jax.ShapeDtypeStruct((s, d), BF),
                   jax.ShapeDtypeStruct((s, ff), BF)],
        scratch_shapes=[pltpu.VMEM((tb, d), F32)],
        compiler_params=_cparams("parallel", "arbitrary"),
    )(x1, g, w1_all, w2_all)


def _loss_head(x, g, tgt, tb):
    s, d = x.shape

    def body(x_ref, g_ref, t_ref, dx_ref, loss_ref, dg_ref):
        i = pl.program_id(0)

        @pl.when(i == 0)
        def _():
            loss_ref[...] = jnp.zeros_like(loss_ref)
            dg_ref[...] = jnp.zeros_like(dg_ref)

        xv = x_ref[...]
        r = _rms_scale(xv)
        xhat = xv * r
        err = xhat * g_ref[...] - t_ref[...]
        part = jnp.sum(jnp.mean(jnp.square(err), axis=-1, keepdims=True), axis=0, keepdims=True)
        loss_ref[...] += 0.5 * part
        dy = err * (1.0 / d)
        dg_ref[...] += jnp.sum(dy * xhat, axis=0, keepdims=True)
        dx_ref[...] = _norm_bwd(dy * g_ref[...], xhat, r)

    return pl.pallas_call(
        body, grid=(s // tb,), name="loss_head",
        in_specs=[_rows(tb, d), _whole((1, d)), _rows(tb, d)],
        out_specs=[_rows(tb, d), _whole((HALO, 128)), _whole((HALO, d))],
        out_shape=[jax.ShapeDtypeStruct((s, d), F32), jax.ShapeDtypeStruct((HALO, 128), F32),
                   jax.ShapeDtypeStruct((HALO, d), F32)],
        compiler_params=_cparams("arbitrary"),
    )(x, g, tgt)


def _mlp_bwd(dx2, x1, ap, g, w1_all, w2_all, l, tb, tf):
    s, d = x1.shape
    ff = ap.shape[1]
    nj = ff // tf

    def body(dx2_ref, x1_ref, ap_ref, g_ref, w1_ref, w2_ref, dx1_ref, dap_ref, dg_ref, acc):
        i, j = pl.program_id(0), pl.program_id(1)

        @pl.when((i == 0) & (j == 0))
        def _():
            dg_ref[...] = jnp.zeros_like(dg_ref)

        @pl.when(j == 0)
        def _():
            acc[...] = jnp.zeros_like(acc)

        da = _nt(dx2_ref[...].astype(BF), w2_ref[...])
        dap = (da * (2.0 * jnp.maximum(ap_ref[...].astype(F32), 0.0))).astype(BF)
        dap_ref[...] = dap
        acc[...] += _nt(dap, w1_ref[...])

        @pl.when(j == nj - 1)
        def _():
            xv = x1_ref[...]
            r = _rms_scale(xv)
            xhat = xv * r
            dh = acc[...]
            dg_ref[...] += jnp.sum(dh * xhat, axis=0, keepdims=True)
            dx1_ref[...] = dx2_ref[...] + _norm_bwd(dh * g_ref[...], xhat, r)

    return pl.pallas_call(
        body, grid=(s // tb, nj), name="mlp_bwd",
        in_specs=[pl.BlockSpec((tb, d), lambda i, j: (i, 0)), pl.BlockSpec((tb, d), lambda i, j: (i, 0)),
                  pl.BlockSpec((tb, tf), lambda i, j: (i, j)),
                  _whole((1, d)), pl.BlockSpec((None, None, d, tf), lambda i, j: (l, j, 0, 0)),
                  pl.BlockSpec((None, None, tf, d), lambda i, j: (l, j, 0, 0))],
        out_specs=[pl.BlockSpec((tb, d), lambda i, j: (i, 0)), pl.BlockSpec((tb, tf), lambda i, j: (i, j)),
                   _whole((HALO, d))],
        out_shape=[jax.ShapeDtypeStruct((s, d), F32), jax.ShapeDtypeStruct((s, ff), BF),
                   jax.ShapeDtypeStruct((HALO, d), F32)],
        scratch_shapes=[pltpu.VMEM((tb, d), F32)],
        compiler_params=_cparams("arbitrary", "arbitrary"),
    )(dx2, x1, ap, g, w1_all, w2_all)


def _wgrad(a, b, tm, tn, ts, name, relu2=False):
    s, m = a.shape
    n = b.shape[1]
    ns = s // ts

    def body(a_ref, b_ref, o_ref, acc):
        k = pl.program_id(2)

        @pl.when(k == 0)
        def _():
            acc[...] = jnp.zeros_like(acc)

        av = a_ref[...]
        if relu2:
            av = jnp.square(jnp.maximum(av.astype(F32), 0.0)).astype(BF)
        acc[...] += _tn(av, b_ref[...].astype(BF))

        @pl.when(k == ns - 1)
        def _():
            o_ref[...] = acc[...].astype(BF)

    return pl.pallas_call(
        body, grid=(m // tm, n // tn, ns), name=name,
        in_specs=[pl.BlockSpec((ts, tm), lambda i, j, k: (k, i)), pl.BlockSpec((ts, tn), lambda i, j, k: (k, j))],
        out_specs=pl.BlockSpec((tm, tn), lambda i, j, k: (i, j)),
        out_shape=jax.ShapeDtypeStruct((m, n), BF),
        scratch_shapes=[pltpu.VMEM((tm, tn), F32)],
        compiler_params=_cparams("parallel", "parallel", "arbitrary"),
    )(a, b)


def _mix_bwd(dx1, ya, yb, yc, lse_c, sink_row, gg, wo_all, l, tb):
    s, d = dx1.shape

    def body(dx_ref, ya_ref, yb_ref, yc_ref, lse_ref, sink_ref, gg_ref, wo_ref,
             n_ref, dya_ref, dyc_ref, da_ref, dc_ref, dyb_ref, dg_ref, dsink_ref):
        i = pl.program_id(0)

        @pl.when(i == 0)
        def _():
            dg_ref[...] = jnp.zeros_like(dg_ref)
            dsink_ref[...] = jnp.zeros_like(dsink_ref)

        dn = _nt(dx_ref[...].astype(BF), wo_ref[...].reshape(MIX_WIDTH, d))
        ys = [ya_ref[...], yb_ref[...], yc_ref[...]]
        rs = [_rms_scale(v) for v in ys]
        nhat = jnp.concatenate([v * r for v, r in zip(ys, rs)], axis=1)
        gg = gg_ref[...]
        n_ref[...] = (nhat * gg).astype(BF)
        dg_ref[...] += jnp.sum(dn * nhat, axis=0, keepdims=True)
        dnh = dn * gg
        bounds = [(0, A_WIDTH), (A_WIDTH, A_WIDTH + CONV_CH), (A_WIDTH + CONV_CH, MIX_WIDTH)]
        dys = [_norm_bwd(dnh[:, lo:hi], nhat[:, lo:hi], r) for (lo, hi), r in zip(bounds, rs)]
        dyb_ref[...] = dys[1]
        head = [lax.broadcasted_iota(jnp.int32, (A_WIDTH, A_WIDTH), k) // HEAD_DIM for k in (0, 1)]
        ones = (head[0] == head[1]).astype(BF)
        for dy, y, dy_ref, dd_ref in ((dys[0], ys[0], dya_ref, da_ref), (dys[2], ys[2], dyc_ref, dc_ref)):
            dy_ref[...] = dy
            t = dy * y
            hi = t.astype(BF)
            dd_ref[...] = _nn(hi, ones) + _nn((t - hi.astype(F32)).astype(BF), ones)
        dsink_ref[...] -= jnp.sum(jnp.exp(sink_ref[...] - lse_ref[...]) * dc_ref[...], axis=0, keepdims=True)

    return pl.pallas_call(
        body, grid=(s // tb,), name="mix_bwd",
        in_specs=[_rows(tb, d), _rows(tb, A_WIDTH), _rows(tb, CONV_CH), _rows(tb, A_WIDTH), _rows(tb, A_WIDTH),
                  _whole((1, A_WIDTH)), _whole((1, MIX_WIDTH)), _layer((N_CHIPS, MIX_WIDTH // N_CHIPS, d), l)],
        out_specs=[_rows(tb, MIX_WIDTH), _rows(tb, A_WIDTH), _rows(tb, A_WIDTH), _rows(tb, A_WIDTH),
                   _rows(tb, A_WIDTH), _rows(tb, CONV_CH), _whole((HALO, MIX_WIDTH)), _whole((HALO, A_WIDTH))],
        out_shape=[jax.ShapeDtypeStruct((s, MIX_WIDTH), BF), jax.ShapeDtypeStruct((s, A_WIDTH), F32),
                   jax.ShapeDtypeStruct((s, A_WIDTH), F32), jax.ShapeDtypeStruct((s, A_WIDTH), F32),
                   jax.ShapeDtypeStruct((s, A_WIDTH), F32), jax.ShapeDtypeStruct((s, CONV_CH), F32),
                   jax.ShapeDtypeStruct((HALO, MIX_WIDTH), F32), jax.ShapeDtypeStruct((HALO, A_WIDTH), F32)],
        compiler_params=_cparams("arbitrary"),
    )(dx1, ya, yb, yc, lse_c, sink_row, gg, wo_all)


def _attn_bwd(z, dy, lse, dd, dil, kw, kcol, vcol, n_rep, max_dist, name):
    s, zw = z.shape
    n_sub = _p_sub(s, dil, 2) if n_rep == 1 else 1
    grid = _p_grid(s, dil, n_sub)
    n_kv = N_HEADS // n_rep
    dt = F32 if P_ROWS[dil] * n_sub < 16 else BF

    def body(q_ref, kp_ref, kc_ref, vp_ref, vc_ref, dy_ref, lse_ref, dd_ref, dq_ref, dkp_ref, dkc_ref, dvp_ref, dvc_ref):
        for t in range(n_sub):
            rows = slice(t * P_ROWS[dil], (t + 1) * P_ROWS[dil])
            before = (slice(None),) if t == 0 else (slice((t - 1) * P_ROWS[dil], t * P_ROWS[dil]),)
            kb_ref, vb_ref = (kp_ref, vp_ref) if t == 0 else (kc_ref, vc_ref)
            mask = _band_mask(n_sub * pl.program_id(len(grid) - 1) if t == 0 else 1, dil, max_dist)
            k2s, qs, dys, scs, dps = [], [], [], [], []
            for kh in range(n_kv):
                k2s.append(jnp.concatenate([_ld(kb_ref, _hs(kh), *before), _ld(kc_ref, _hs(kh), rows)],
                                           axis=0).astype(BF))
                v2 = jnp.concatenate([_ld(vb_ref, _hs(kh), *before), _ld(vc_ref, _hs(kh), rows)], axis=0).astype(BF)
                for h in range(kh * n_rep, (kh + 1) * n_rep):
                    qs.append((_ld(q_ref, _hs(h), rows) * SCALE).astype(BF))
                    dys.append(_ld(dy_ref, _hs(h), rows).astype(BF))
                    scs.append(jnp.where(mask, _nt(qs[h], k2s[kh]), NEG))
                    dps.append(_nt(dys[h], v2))
            for kh in range(n_kv):
                k2 = k2s[kh]
                dk2 = jnp.zeros((2 * TQ, HEAD_DIM), F32)
                dv2 = jnp.zeros((2 * TQ, HEAD_DIM), F32)
                for h in range(kh * n_rep, (kh + 1) * n_rep):
                    lse_h = _ld(lse_ref, slice(h * HEAD_DIM, h * HEAD_DIM + 1), rows)
                    dd_h = _ld(dd_ref, slice(h * HEAD_DIM, h * HEAD_DIM + 1), rows)
                    p = jnp.exp(scs[h] - lse_h)
                    ds = (p * (dps[h] - dd_h)).astype(BF)
                    _st(dq_ref, _hs(h), (_nn(ds, k2) * SCALE).astype(dt), rows)
                    dk2 = dk2 + _tn(ds, qs[h])
                    dv2 = dv2 + _tn(p.astype(BF), dys[h])
                _st(dkp_ref, _hs(kh), dk2[:TQ].astype(dt), rows)
                _st(dkc_ref, _hs(kh), dk2[TQ:].astype(dt), rows)
                _st(dvp_ref, _hs(kh), dv2[:TQ].astype(dt), rows)
                _st(dvc_ref, _hs(kh), dv2[TQ:].astype(dt), rows)

    args = [_strips(z)] * 5 + [_strips(a) for a in (dy, lse, dd)]
    pair = _p_spec(dil, A_WIDTH, 0, n_sub)
    in_specs = [pair, _p_spec(dil, kw, kcol, n_sub, True), _p_spec(dil, kw, kcol, n_sub),
                _p_spec(dil, kw, vcol, n_sub, True), _p_spec(dil, kw, vcol, n_sub)] + [pair] * 3
    out_specs = [pair] + [_p_spec(dil, kw, 0, n_sub)] * 4
    na = s // N_STRIPS
    out_shape = [jax.ShapeDtypeStruct((4, 4, na, A_WIDTH), dt)] + [jax.ShapeDtypeStruct((4, 4, na, kw), dt)] * 4
    res = pl.pallas_call(
        body, grid=grid, name=name, in_specs=in_specs, out_specs=out_specs, out_shape=out_shape,
        compiler_params=_cparams(*(("parallel",) * len(grid))),
    )(*args)
    return [res[0].reshape(s, A_WIDTH)] + [a.reshape(s, kw) for a in res[1:]]


DZ_TA = 16


def _dz_assemble(parts_a, parts_c, dyb, zb, cw):
    s = zb.shape[0]
    na = s // N_STRIPS
    nb = na // DZ_TA

    def ahead(w, k):
        return pl.BlockSpec((4, 4, DZ_TA, w), lambda i: (0, 0, jnp.minimum(i + k, nb - 1), 0))

    args, in_specs = [], []
    for dil, (dq, dkp, dkc, dvp, dvc) in zip(DILATIONS + (1,), parts_a + [parts_c]):
        w = dkp.shape[1]
        here = _strip_rows(DZ_TA, w)
        if dil == 1:
            args += [dq, dkp, dkp, dkc, dvp, dvp, dvc]
            in_specs += [_strip_rows(DZ_TA, A_WIDTH), here, ahead(w, 1), here, here, ahead(w, 1), here]
        else:
            k = 8 * dil // DZ_TA
            args += [dq, dkp, dkc, dvp, dvc]
            in_specs += [_strip_rows(DZ_TA, A_WIDTH), ahead(w, k), here, ahead(w, k), here]
    n_att = len(args)
    args = [_strips(a) for a in args] + [_strips(dyb), _strips(dyb), _strips(zb), _strips(zb), _strips(zb), cw]
    in_specs += [_strip_rows(DZ_TA, CONV_CH), _next_rows(DZ_TA, CONV_CH, nb), _strip_rows(DZ_TA, ZB_W),
                 _prev_rows(DZ_TA, ZB_W), _next_rows(DZ_TA, ZB_W, nb), _whole((HALO, CONV_CH))]

    def body(*refs):
        att = list(refs[:n_att])
        dyb_ref, dybn_ref, zb_ref, zbp_ref, zbn_ref, cw_ref, dz_ref, dcw_ref = refs[n_att:]
        i = pl.program_id(0)

        @pl.when(i == 0)
        def _():
            dcw_ref[...] = jnp.zeros_like(dcw_ref)

        def shifted(dil):
            if dil == 1:
                dq_r, kp0, kp1, dkc_r, vp0, vp1, dvc_r = [att.pop(0) for _ in range(7)]
                live = i + 1 < nb
                half = DZ_TA // 2
                kp0, kp1, vp0, vp1 = [r[...].astype(F32) for r in (kp0, kp1, vp0, vp1)]
                dkp = jnp.concatenate([kp0[:, :, half:, :], jnp.where(live, kp1[:, :, :half, :], 0.0)], axis=2)
                dvp = jnp.concatenate([vp0[:, :, half:, :], jnp.where(live, vp1[:, :, :half, :], 0.0)], axis=2)
            else:
                dq_r, dkp_r, dkc_r, dvp_r, dvc_r = [att.pop(0) for _ in range(5)]
                live = i + 8 * dil // DZ_TA < nb
                dkp = jnp.where(live, dkp_r[...].astype(F32), 0.0)
                dvp = jnp.where(live, dvp_r[...].astype(F32), 0.0)
            return dq_r[...].astype(F32), dkc_r[...].astype(F32) + dkp, dvc_r[...].astype(F32) + dvp

        dq, dk, dv = shifted(DILATIONS[0])
        for dil in DILATIONS[1:]:
            dq2, dk2, dv2 = shifted(dil)
            dq, dk, dv = dq + dq2, dk + dk2, dv + dv2
        dz_ref[:, :, :, 0:A_WIDTH] = dq.astype(BF)
        dz_ref[:, :, :, A_WIDTH:2 * A_WIDTH] = dk.astype(BF)
        dz_ref[:, :, :, 2 * A_WIDTH:ZA_W] = dv.astype(BF)
        dq, dk, dv = shifted(1)
        c0 = ZA_W + ZB_W
        dz_ref[:, :, :, c0:c0 + A_WIDTH] = dq.astype(BF)
        dz_ref[:, :, :, c0 + A_WIDTH:c0 + A_WIDTH + C_KV_WIDTH] = dk.astype(BF)
        dz_ref[:, :, :, c0 + A_WIDTH + C_KV_WIDTH:IN_WIDTH] = dv.astype(BF)

        cw = cw_ref[...]
        prev = jnp.where(i > 0, zbp_ref[...], 0.0)
        gb, gc, xb, u, u1, u2, c = _conv_strips(zb_ref[...], prev, cw)
        dyb = dyb_ref[...]
        dc = [_strip(dyb, b) * gb[b] for b in range(N_STRIPS)]
        dcn = jnp.where(i + 1 < nb, dybn_ref[...] * zbn_ref[:, :, :CONV_CH], 0.0)
        wrapped = [_shift_up(dc[0], 1, dcn[0]), _shift_up(dc[1], 1, dcn[1])]
        upd = [jnp.zeros((1, CONV_CH), F32)] * 3
        for b in range(N_STRIPS):
            dc1 = dc[b + 1] if b + 1 < N_STRIPS else wrapped[0]
            dc2 = dc[b + 2] if b + 2 < N_STRIPS else wrapped[b + 2 - N_STRIPS]
            du = cw[2:3, :] * dc[b] + cw[1:2, :] * dc1 + cw[0:1, :] * dc2
            f, e = b % 4, b // 4
            dz_ref[f, e, :, ZA_W:ZA_W + CONV_CH] = (_strip(dyb, b) * c[b]).astype(BF)
            dz_ref[f, e, :, ZA_W + CONV_CH:ZA_W + 2 * CONV_CH] = (du * xb[b]).astype(BF)
            dz_ref[f, e, :, ZA_W + 2 * CONV_CH:c0] = (du * gc[b]).astype(BF)
            for t, uu in enumerate((u2[b], u1[b], u[b])):
                upd[t] = upd[t] + jnp.sum(dc[b] * uu, axis=0, keepdims=True)
        row = lax.broadcasted_iota(jnp.int32, (HALO, CONV_CH), 0)
        tile = jnp.zeros((HALO, CONV_CH), F32)
        for t in range(3):
            tile = jnp.where(row == t, upd[t], tile)
        dcw_ref[...] += tile

    dz, dcw = pl.pallas_call(
        body, grid=(nb,), name="dz_assemble", in_specs=in_specs,
        out_specs=[_strip_rows(DZ_TA, IN_WIDTH), _whole((HALO, CONV_CH))],
        out_shape=[jax.ShapeDtypeStruct((4, 4, na, IN_WIDTH), BF), jax.ShapeDtypeStruct((HALO, CONV_CH), F32)],
        compiler_params=_cparams("arbitrary"),
    )(*args)
    return dz.reshape(s, IN_WIDTH), dcw


def _qkv_bwd(dz, dx1, x, g, w_all, l, tb, tokens_out):
    s, d = x.shape
    na, ta = s // N_STRIPS, tb // N_STRIPS

    def body(dz_ref, dx1_ref, x_ref, g_ref, w_ref, dx_ref, dg_ref):
        i = pl.program_id(0)

        @pl.when(i == 0)
        def _():
            dg_ref[...] = jnp.zeros_like(dg_ref)

        n = IN_WIDTH // N_CHIPS
        dz = dz_ref[...].reshape(tb, IN_WIDTH)
        dh = _nt(dz[:, 0:n], w_ref[0])
        for k in range(1, N_CHIPS):
            dh = dh + _nt(dz[:, k * n:(k + 1) * n], w_ref[k])
        xv = x_ref[...].reshape(tb, d)
        r = _rms_scale(xv)
        xhat = xv * r
        dg_ref[...] += jnp.sum(dh * xhat, axis=0, keepdims=True)
        dx = (dx1_ref[...].reshape(tb, d) + _norm_bwd(dh * g_ref[...], xhat, r)).reshape(4, 4, ta, d)
        if tokens_out:
            for b in range(N_STRIPS):
                dx_ref[:, b, :] = _strip(dx, b)
        else:
            dx_ref[...] = dx

    if tokens_out:
        dx_spec, dx_shape = pl.BlockSpec((ta, N_STRIPS, d), lambda i: (i, 0, 0)), (na, N_STRIPS, d)
    else:
        dx_spec, dx_shape = _strip_rows(ta, d), (4, 4, na, d)
    dx, dg = pl.pallas_call(
        body, grid=(s // tb,), name="qkv_bwd",
        in_specs=[_strip_rows(ta, IN_WIDTH), _strip_rows(ta, d), _strip_rows(ta, d), _whole((1, d)),
                  _layer((N_CHIPS, d, IN_WIDTH // N_CHIPS), l)],
        out_specs=[dx_spec, _whole((HALO, d))],
        out_shape=[jax.ShapeDtypeStruct(dx_shape, F32), jax.ShapeDtypeStruct((HALO, d), F32)],
        compiler_params=_cparams("arbitrary"),
    )(_strips(dz), _strips(dx1), _strips(x), g, w_all)
    return dx.reshape(s, d), dg


def _tile_rows(rows):
    return jnp.pad(rows, ((0, HALO - rows.shape[0]), (0, 0)))


def _to_strips(a, after, name):
    s, d = a.shape
    na = s // N_STRIPS
    ta = min(32, na)

    def body(a_ref, *rest):
        for b in range(N_STRIPS):
            rest[-1][b % 4, b // 4] = a_ref[:, b, :]

    return pl.pallas_call(
        body, grid=(na // ta,), name=name,
        in_specs=[pl.BlockSpec((ta, N_STRIPS, d), lambda i: (i, 0, 0))] + [ANY] * len(after),
        out_specs=_strip_rows(ta, d),
        out_shape=jax.ShapeDtypeStruct((4, 4, na, d), a.dtype), compiler_params=_cparams("parallel"),
    )(a.reshape(na, N_STRIPS, d), *after).reshape(s, d)


def _local_step(x, tgt, fetch, ff, sinks, g_mix, g_group, g_mlp, g_final, emit):
    s, d = x.shape
    depth = g_mix.shape[0]
    tb = min(512, s)
    tf = ff // N_CHIPS
    ts = min(1024, s)
    saved = []
    for l in range(depth):
        w_in, _, _, _, conv_w = fetch(0, l, x)
        cw = _tile_rows(conv_w[l])
        sk = jnp.repeat(sinks[l].reshape(N_HEADS), HEAD_DIM)[None]
        h, za, zb, zc = _qkv_fwd(x, g_mix[l][None], w_in, l, tb)
        parts_a = [_attn_fwd(za, dil, A_WIDTH, 1, 2, 1, A_MAX_DIST, "attn_a_fwd_%d" % dil) for dil in DILATIONS]
        part_c = _attn_fwd(zc, 1, C_KV_WIDTH, 3, 4, C_GROUP, C_MAX_DIST, "attn_c_fwd")
        w_in, w_o, w1, w2, _ = fetch(1, l, part_c[0])
        x1, yb, ya, lse_a, yc, lse_c = _mix_fwd(x, parts_a, part_c, sk, zb, cw, g_group[l][None], w_o, l, tb)
        w_in, w_o, w1, w2, _ = fetch(2, l, x1)
        x2, h2, ap = _mlp_fwd(x1, g_mlp[l][None], w1, w2, l, ts, tf)
        saved.append((x, h, za, zb, zc, ya, lse_a, yc, lse_c, yb, x1, h2, ap, cw, sk))
        x = x2
    dx, loss_tile, dg_final = _loss_head(x, g_final[None], tgt, ts)
    grads = [None] * depth
    tok = jnp.zeros((), F32)
    for l in reversed(range(depth)):
        x0, h, za, zb, zc, ya, lse_a, yc, lse_c, yb, x1, h2, ap, cw, sk = saved[l]
        dx1, dap, dg_mlp = _mlp_bwd(dx, x1, ap, g_mlp[l][None] + tok, w1, w2, l, ts, tf)
        tok = emit(l, 3, _wgrad(ap, dx, min(1024, ff), d, 2 * ts, "wgrad_ff_out", relu2=True))
        tok = tok + emit(l, 2, _wgrad(h2, dap, d, min(1024, ff), 2 * ts, "wgrad_ff_in"))
        n, dya, dyc, dd_a, dd_c, dyb, dg_group, dsink = _mix_bwd(dx1, ya, yb, yc, lse_c, sk, g_group[l][None] + tok,
                                                                 w_o, l, tb)
        tok = emit(l, 1, _wgrad(n, dx1, MIX_WIDTH, d, ts, "wgrad_o"))
        cw = cw + tok
        parts_a = [_attn_bwd(za, dya, lse_a, dd_a, dil, A_WIDTH, 1, 2, 1, A_MAX_DIST, "attn_a_bwd_%d" % dil)
                   for dil in DILATIONS]
        parts_c = _attn_bwd(zc, dyc, lse_c, dd_c, 1, C_KV_WIDTH, 3, 4, C_GROUP, C_MAX_DIST, "attn_c_bwd")
        dz, dcw = _dz_assemble(parts_a, parts_c, dyb, zb, cw)
        tok = emit(l, 0, _wgrad(h, dz, d, IN_WIDTH // 4, 2 * ts, "wgrad_in"))
        dx, dg_mix = _qkv_bwd(dz, dx1, x0, g_mix[l][None] + tok, w_in, l, tb, l == 0)
        grads[l] = (dcw, dsink, dg_mix, dg_group, dg_mlp)
    return loss_tile, dx, grads, dg_final


ANY = pl.BlockSpec(memory_space=pl.ANY)
SHARD_AXES = (2, 1, 2, 1)
N_BIG = len(SHARD_AXES)
N_CHIPS = 4
N_DEV = 8


def _mesh_pos():
    return lax.axis_index("x"), lax.axis_index("y"), lax.axis_index("c")


def _flip(v, bit):
    return 1 - v if bit else v


def _place_shard(shard, chip_arr, name):
    _, rows, cols = shard.shape
    tr = min(256, rows)

    def body(chip_ref, x_ref, o_ref):
        o_ref[...] = x_ref[...].astype(BF)

    return pl.pallas_call(
        body, name=name,
        grid_spec=pltpu.PrefetchScalarGridSpec(
            num_scalar_prefetch=1, grid=(2, rows // tr),
            in_specs=[pl.BlockSpec((None, tr, cols), lambda l, i, chip: (l, i, 0))],
            out_specs=pl.BlockSpec((None, None, tr, cols), lambda l, i, chip: (l, chip[0], i, 0))),
        out_shape=jax.ShapeDtypeStruct((2, N_CHIPS, rows, cols), BF),
        compiler_params=_cparams("parallel", "parallel"),
    )(chip_arr, shard)


HBM = pl.BlockSpec(memory_space=pltpu.HBM)
SEM = pl.BlockSpec(memory_space=pltpu.SEMAPHORE)
EFFECT = pltpu.SideEffectType.DATAFLOW_SIDE_EFFECTING

GATHER_GROUPS = (((0, 0),), ((1, 0),), ((2, 0), (3, 0)), ((0, 1),), ((1, 1), (2, 1), (3, 1)))
GATHER_STARTS = ((0,), (1, 2), (3, 4))
GATHER_STAGES = {(0, 0): 0, (1, 0): 1, (2, 0): 2, (0, 1): 3, (1, 1): 4}


def _gather_copies(arrs, group, send_sems, recv_sems):
    x, y, c = _mesh_pos()
    me = 2 * x + y
    out = []
    for i, (w, layer) in enumerate(group):
        mine = arrs[w].at[layer, me]
        for j, (qx, qy) in enumerate([(1 - x, y), (x, 1 - y), (1 - x, 1 - y)]):
            landed = arrs[w].at[layer, 2 * qx + qy]
            out.append(tuple(pltpu.make_async_remote_copy(
                src_ref=piece, dst_ref=piece, send_sem=send_sems.at[i * 3 + j], recv_sem=recv_sems.at[i * 3 + j],
                device_id=(qx, qy, c), device_id_type=MESH) for piece in (mine, landed)))
    return out


def _conv_copies(conv_src, conv_dst, send_sems, recv_sems):
    x, y, c = _mesh_pos()
    out = []
    for j, (qx, qy) in enumerate([(1 - x, y), (x, 1 - y), (1 - x, 1 - y)]):
        out.append(tuple(pltpu.make_async_remote_copy(
            src_ref=conv_src, dst_ref=conv_dst.at[q], send_sem=send_sems.at[j], recv_sem=recv_sems.at[j],
            device_id=(qx, qy, c), device_id_type=MESH) for q in (2 * x + y, 2 * qx + qy)))
    return out


def _gather_start(groups, arrs, conv, name, through=None):
    n_sems = 2 * (len(groups) + (conv is not None))
    mats = sorted({w for g in groups for w, _ in GATHER_GROUPS[g]})

    def body(*refs):
        arrs_ref = [None] * N_BIG
        for w, ref in zip(mats, refs):
            arrs_ref[w] = ref
        sems = refs[n_in:n_in + n_sems]
        if conv is not None:
            for cp, _ in _conv_copies(refs[len(mats)], refs[len(mats) + 1], sems[-2], sems[-1]):
                cp.start()
        for k, g in enumerate(groups):
            for cp, _ in _gather_copies(arrs_ref, GATHER_GROUPS[g], sems[2 * k], sems[2 * k + 1]):
                cp.start()

    sem_shapes = []
    for n in [len(GATHER_GROUPS[g]) for g in groups] + ([1] if conv is not None else []):
        sem_shapes += [pltpu.SemaphoreType.DMA((3 * n,))] * 2
    operands = [arrs[w] for w in mats] + ([] if conv is None else list(conv)) + ([] if through is None else [through])
    n_in = len(operands)
    res = pl.pallas_call(
        body, name=name,
        out_shape=tuple(sem_shapes) + tuple(pltpu.HBM(a.shape, a.dtype) for a in operands),
        in_specs=(HBM,) * n_in, out_specs=(SEM,) * n_sems + (HBM,) * n_in,
        input_output_aliases={i: n_sems + i for i in range(n_in)},
        compiler_params=pltpu.CompilerParams(has_side_effects=EFFECT),
    )(*[pltpu.with_memory_space_constraint(a, pltpu.HBM) for a in operands])
    arrs = list(arrs)
    for w, a in zip(mats, res[n_sems:]):
        arrs[w] = a
    return res[:n_sems], arrs, list(res[n_sems + len(mats):])


def _gather_wait(k, sems, arrs, conv, after, name):
    group = GATHER_GROUPS[k]
    mats = sorted({w for w, _ in group})
    n_conv = 0 if conv is None else 2

    def body(*refs):
        local = refs[:len(mats)]
        arrs_ref = [None] * N_BIG
        for w, ref in zip(mats, local):
            arrs_ref[w] = ref
        pos = len(mats) + n_conv
        copies = _gather_copies(arrs_ref, group, refs[pos], refs[pos + 1])
        if conv is not None:
            copies += _conv_copies(refs[len(mats)], refs[len(mats) + 1], refs[pos + 2], refs[pos + 3])
        for send, recv in copies:
            recv.wait_recv()
            send.wait_send()

    operands = [arrs[w] for w in mats] + ([] if conv is None else [conv[1], conv[2]])
    sem_ops = list(sems) + ([] if conv is None else list(conv[0]))
    n_op = len(operands)
    res = pl.pallas_call(
        body, name=name, out_shape=tuple(pltpu.HBM(a.shape, a.dtype) for a in operands),
        in_specs=(HBM,) * n_op + (SEM,) * len(sem_ops) + (ANY,) * len(after), out_specs=(HBM,) * n_op,
        input_output_aliases={i: i for i in range(n_op)},
        compiler_params=pltpu.CompilerParams(has_side_effects=EFFECT),
    )(*operands, *sem_ops, *after)
    arrs = list(arrs)
    for w, a in zip(mats, res):
        arrs[w] = a
    return arrs, (res[-1] if conv is not None else None)


def _grad_shard(ref, w, chip, n):
    start = pl.multiple_of(chip * n, 128)
    if SHARD_AXES[w] == 2:
        return ref.at[:, pl.ds(start, n)]
    return ref.at[pl.ds(start, n), :]


def _slot_shape(g, w):
    shape = list(g.shape)
    shape[SHARD_AXES[w] - 1] //= N_CHIPS
    return (N_DEV - 1,) + tuple(shape)


def _scatter_copies(g_ref, land_ref, send_sems, recv_sems, layer, w):
    x, y, c = _mesh_pos()
    n = g_ref.shape[SHARD_AXES[w] - 1] // N_CHIPS
    out = []
    for r in range(1, N_DEV):
        tx, ty, tc = _flip(x, r & 4), _flip(y, r & 2), _flip(c, r & 1)
        cp = pltpu.make_async_remote_copy(
            src_ref=_grad_shard(g_ref, w, 2 * tx + ty, n), dst_ref=land_ref.at[r - 1], send_sem=send_sems.at[r - 1],
            recv_sem=recv_sems.at[r - 1], device_id=(tx, ty, tc), device_id_type=MESH)
        out.append((cp, (c != layer) if r & 1 else (c == layer)))
    return out


def _scatter_start(items, layer, name):
    n = len(items)

    def body(*refs):
        for i, (w, _, _) in enumerate(items):
            g_ref, land_ref = refs[2 * i], refs[2 * i + 1]
            send_sems, recv_sems = refs[2 * n + 2 * i], refs[2 * n + 2 * i + 1]
            for cp, mine in _scatter_copies(g_ref, land_ref, send_sems, recv_sems, layer, w):
                @pl.when(mine)
                def _():
                    cp.start()
        refs[-1][...] = jnp.zeros_like(refs[-1])

    operands = [a for _, g, land in items for a in (g, land)]
    res = pl.pallas_call(
        body, name=name,
        out_shape=(pltpu.SemaphoreType.DMA((N_DEV - 1,)),) * (2 * n)
        + tuple(pltpu.HBM(a.shape, a.dtype) for a in operands) + (jax.ShapeDtypeStruct((HALO, 128), F32),),
        in_specs=(HBM,) * (2 * n),
        out_specs=(SEM,) * (2 * n) + (HBM,) * (2 * n) + (pl.BlockSpec(memory_space=pltpu.VMEM),),
        input_output_aliases={i: 2 * n + i for i in range(2 * n)},
        compiler_params=pltpu.CompilerParams(has_side_effects=EFFECT),
    )(*[pltpu.with_memory_space_constraint(a, pltpu.HBM) for a in operands])
    return [(res[2 * i], res[2 * i + 1], res[2 * n + 2 * i], res[2 * n + 2 * i + 1]) for i in range(n)], res[-1]


def _scatter_wait(started, land, after, w, name):
    def body(g0_ref, g1_ref, land_ref, ss0, rs0, ss1, rs1, after_ref, g0_out, g1_out, land_out):
        c = lax.axis_index("c")
        for layer, g_ref, ss, rs in ((0, g0_ref, ss0, rs0), (1, g1_ref, ss1, rs1)):
            for cp, mine in _scatter_copies(g_ref, land_ref, ss, rs, layer, w):
                @pl.when(mine)
                def _():
                    cp.wait_send()

                @pl.when(c == layer)
                def _():
                    cp.wait_recv()

    (ss0, rs0, g0), (ss1, rs1, g1) = started
    return pl.pallas_call(
        body, name=name,
        out_shape=(pltpu.HBM(g0.shape, g0.dtype), pltpu.HBM(g1.shape, g1.dtype), pltpu.HBM(land.shape, land.dtype)),
        in_specs=(HBM, HBM, HBM, SEM, SEM, SEM, SEM, ANY), out_specs=(HBM, HBM, HBM),
        input_output_aliases={0: 0, 1: 1, 2: 2}, compiler_params=pltpu.CompilerParams(has_side_effects=EFFECT),
    )(g0, g1, land, ss0, rs0, ss1, rs1, after)


def _sum_slots(g0, g1, slots, w, pos_arr, name):
    _, rows, cols = slots.shape
    tr = min(512, rows)
    nr = rows // tr
    if SHARD_AXES[w] == 2:
        own = pl.BlockSpec((tr, cols), lambda i, pos: (i, pos[0]))
    else:
        own = pl.BlockSpec((tr, cols), lambda i, pos: (pos[0] * nr + i, 0))

    def body(pos_ref, own0_ref, own1_ref, s_ref, o_ref):
        acc = jnp.where(pos_ref[1] == 0, own0_ref[...], own1_ref[...]).astype(F32)
        for r in range(N_DEV - 1):
            acc = acc + s_ref[r].astype(F32)
        o_ref[...] = acc

    return pl.pallas_call(
        body, name=name,
        grid_spec=pltpu.PrefetchScalarGridSpec(
            num_scalar_prefetch=1, grid=(nr,),
            in_specs=[own, own, pl.BlockSpec((N_DEV - 1, tr, cols), lambda i, pos: (0, i, 0))],
            out_specs=pl.BlockSpec((tr, cols), lambda i, pos: (i, 0))),
        out_shape=jax.ShapeDtypeStruct((rows, cols), F32), compiler_params=_cparams("parallel"),
    )(pos_arr, g0, g1, slots)


def _swap_copies(refs, n):
    x, y, c = _mesh_pos()
    return [pltpu.make_async_remote_copy(src_ref=refs[w], dst_ref=refs[n + w], send_sem=refs[2 * n].at[w],
                                         recv_sem=refs[2 * n + 1].at[w], device_id=(x, y, 1 - c), device_id_type=MESH)
            for w in range(n)]


def _swap_start(halves, name):
    n = len(halves)

    def body(*refs):
        for cp in _swap_copies(refs, n):
            cp.start()

    operands = list(halves) + [lax.empty(h.shape, h.dtype) for h in halves]
    res = pl.pallas_call(
        body, name=name,
        out_shape=(pltpu.SemaphoreType.DMA((n,)),) * 2 + tuple(pltpu.HBM(a.shape, a.dtype) for a in operands),
        in_specs=(HBM,) * (2 * n), out_specs=(SEM,) * 2 + (HBM,) * (2 * n),
        input_output_aliases={i: 2 + i for i in range(2 * n)},
        compiler_params=pltpu.CompilerParams(has_side_effects=EFFECT),
    )(*[pltpu.with_memory_space_constraint(a, pltpu.HBM) for a in operands])
    return res[0], res[1], list(res[2:2 + n]), list(res[2 + n:])


def _swap_wait(send_sems, recv_sems, halves, lands, after, name):
    n = len(halves)

    def body(*refs):
        for cp in _swap_copies(refs, n):
            cp.wait_send()
            cp.wait_recv()

    operands = list(halves) + list(lands)
    res = pl.pallas_call(
        body, name=name, out_shape=tuple(pltpu.HBM(a.shape, a.dtype) for a in operands),
        in_specs=(HBM,) * (2 * n) + (SEM, SEM, ANY), out_specs=(HBM,) * (2 * n),
        input_output_aliases={i: i for i in range(2 * n)},
        compiler_params=pltpu.CompilerParams(has_side_effects=EFFECT),
    )(*operands, send_sems, recv_sems, after)
    return list(res[n:])


def _adamw_math(w, g, m, v):
    m = ADAM_B1 * m + (1.0 - ADAM_B1) * g
    v = ADAM_B2 * v + (1.0 - ADAM_B2) * jnp.square(g)
    m_hat = m / (1.0 - ADAM_B1 ** ADAM_STEP)
    v_hat = v / (1.0 - ADAM_B2 ** ADAM_STEP)
    delta = -ADAM_LR * (m_hat / (jnp.sqrt(v_hat) + ADAM_EPS) + ADAM_WD * w)
    return delta, m, v


def _adamw(w, g, m, v, filled, pos_arr, name):
    shape = w.shape
    _, rows, cols = shape
    tr = min(256, rows)

    def body(pos_ref, w_ref, g_ref, m_ref, v_ref, *rest):
        go_ref, d_ref, m2_ref, v2_ref = rest[-4:]
        g = g_ref[...]
        go_ref[...] = g
        d_ref[...], m2_ref[...], v2_ref[...] = _adamw_math(w_ref[...], g, m_ref[...], v_ref[...])

    def layer(pos):
        return pos[1] if filled is None else 1 - pos[1]

    full = pl.BlockSpec((None, tr, cols), lambda i, pos: (layer(pos), i, 0))
    half = pl.BlockSpec((tr, cols), lambda i, pos: (i, 0))
    n_in = 5
    return pl.pallas_call(
        body, name=name,
        grid_spec=pltpu.PrefetchScalarGridSpec(
            num_scalar_prefetch=1, grid=(rows // tr,),
            in_specs=[full, half, full, full] + ([] if filled is None else [ANY] * 4), out_specs=[full] * 4),
        out_shape=[jax.ShapeDtypeStruct(shape, F32)] * 4,
        input_output_aliases={} if filled is None else {n_in + k: k for k in range(4)},
        compiler_params=_cparams("parallel"),
    )(pos_arr, w, g, m, v, *([] if filled is None else filled))


def _small_sync(part, w, m, v):
    rows, cols = part.shape

    def body(p_ref, w_ref, m_ref, v_ref, g_ref, d_ref, m2_ref, v2_ref, slots, send_sems, recv_sems):
        x, y, c = _mesh_pos()
        me = 4 * x + 2 * y + c
        slots[me] = p_ref[...]
        sends = []
        for r in range(1, N_DEV):
            to = (_flip(x, r & 4), _flip(y, r & 2), _flip(c, r & 1))
            sends.append(pltpu.make_async_remote_copy(
                src_ref=p_ref, dst_ref=slots.at[me], send_sem=send_sems.at[r - 1], recv_sem=recv_sems.at[r - 1],
                device_id=to, device_id_type=MESH))
        for cp in sends:
            cp.start()
        for cp in sends:
            cp.wait_recv()
        for cp in sends:
            cp.wait_send()
        g = slots[0]
        for i in range(1, N_DEV):
            g = g + slots[i]
        g_ref[...] = g
        d_ref[...], m2_ref[...], v2_ref[...] = _adamw_math(w_ref[...], g, m_ref[...], v_ref[...])

    vm = pl.BlockSpec(memory_space=pltpu.VMEM)
    return pl.pallas_call(
        body, name="small_sync", in_specs=[vm] * 4, out_specs=[vm] * 4,
        out_shape=[jax.ShapeDtypeStruct((rows, cols), F32)] * 4,
        scratch_shapes=[pltpu.VMEM((N_DEV, rows, cols), F32), pltpu.SemaphoreType.DMA((N_DEV - 1,)),
                        pltpu.SemaphoreType.DMA((N_DEV - 1,))],
    )(part, w, m, v)


PACK_W = 256


def _pack_rows(n):
    return -(-n // (HALO * PACK_W)) * HALO


def _pack_small(parts):
    out = []
    for a in parts:
        flat = a.reshape(-1)
        out.append(jnp.pad(flat, (0, _pack_rows(flat.size) * PACK_W - flat.size)).reshape(-1, PACK_W))
    return jnp.concatenate(out, axis=0)


def _unpack_small(p, shapes):
    out, row = [], 0
    for shape in shapes:
        n = 1
        for k in shape:
            n *= k
        out.append(p[row:row + _pack_rows(n)].reshape(-1)[:n].reshape(shape))
        row += _pack_rows(n)
    return out


def kernel(x, w_in, conv_w, sinks, g_mix, g_group, w_o, g_mlp, w_ff_in, w_ff_out, g_final, loss_target, m_w_in, m_conv_w, m_sinks, m_g_mix, m_g_group, m_w_o, m_g_mlp, m_w_ff_in, m_w_ff_out, m_g_final, v_w_in, v_conv_w, v_sinks, v_g_mix, v_g_group, v_w_o, v_g_mlp, v_w_ff_in, v_w_ff_out, v_g_final):
    chip = 2 * lax.axis_index("x") + lax.axis_index("y")
    conv_n = conv_w.shape[2]

    pos_arr = jnp.stack([chip, lax.axis_index("c")]).astype(jnp.int32)
    shards = (w_in, w_o, w_ff_in, w_ff_out)
    conv_tile = jnp.pad(conv_w.reshape(6, conv_n), ((0, HALO - 6), (0, 128 - conv_n)))
    placed = [_place_shard(w_in, pos_arr[:1], "place_shard_0"), None, None, None]
    sems_a, placed, conv_thru = _gather_start(
        GATHER_STARTS[0], placed, (conv_tile, lax.empty((N_CHIPS,) + conv_tile.shape, conv_tile.dtype)),
        "gather_start_0")
    for i in range(1, N_BIG):
        placed[i] = _place_shard(shards[i], pos_arr[:1], "place_shard_%d" % i)
    full = {"arrs": placed, "conv": None, "sems": list(sems_a[:2])}
    target = _to_strips(loss_target[0], placed[:1], "to_strips_target")

    def fetch(stage, layer, after):
        k = GATHER_STAGES.get((stage, layer))
        if k is None:
            return (*full["arrs"], full["conv"])
        sems = full["sems"][2 * k:2 * k + 2]
        if k == 0:
            full["arrs"], land = _gather_wait(0, sems, full["arrs"], (sems_a[-2:], *conv_thru), (after, target),
                                              "gather_wait_0")
            conv_all = lax.dynamic_update_slice(land, conv_tile[None], (chip, 0, 0))
            full["conv"] = conv_all[:, :6, :conv_n].reshape(N_CHIPS, 2, 3, conv_n).transpose(1, 2, 0, 3).reshape(
                2, 3, CONV_CH)
            sems_b, full["arrs"], rest = _gather_start(GATHER_STARTS[1], full["arrs"], None, "gather_start_1",
                                                       through=full["arrs"][0])
            full["arrs"][0] = rest[-1]
            full["sems"] += list(sems_b)
        else:
            full["arrs"], _ = _gather_wait(k, sems, full["arrs"], None, (after,), "gather_wait_%d" % k)
        if k == 2:
            sems_c, full["arrs"], _ = _gather_start(GATHER_STARTS[2], full["arrs"], None, "gather_start_2")
            full["sems"] += list(sems_c)
        return (*full["arrs"], full["conv"])

    lands, started, pending = [None] * N_BIG, {}, []

    def emit(layer, w, g):
        if lands[w] is None:
            lands[w] = lax.empty(_slot_shape(g, w), g.dtype)
        pending.append((w, g, lands[w]))
        if not (w == 0 or (layer == 0 and w == 1)):
            return jnp.zeros((), F32)
        name = "scatter_start_%d_%d" % (layer, len(pending))
        done, token = _scatter_start(list(pending), layer, name)
        for (w_i, _, _), (ss, rs, g_thru, land) in zip(pending, done):
            started[layer, w_i], lands[w_i] = (ss, rs, g_thru), land
        pending.clear()
        return token[0, 0]

    loss_tile, dx, grads, dg_final = _local_step(_to_strips(x[0], placed, "to_strips_x"), target, fetch,
                                                 w_ff_in.shape[2] * N_CHIPS,
                                                 sinks, g_mix, g_group, g_mlp, g_final, emit)

    wmv = ((w_in, m_w_in, v_w_in), (w_o, m_w_o, v_w_o), (w_ff_in, m_w_ff_in, v_w_ff_in),
           (w_ff_out, m_w_ff_out, v_w_ff_out))
    big, after = [None] * N_BIG, dx
    for name, ws in (("swap_rest", (1, 2, 3)), ("swap_in", (0,))):
        own = []
        for w in ws:
            g0, g1, slots = _scatter_wait((started[0, w], started[1, w]), lands[w], after, w, "scatter_wait_%d" % w)
            own.append(_sum_slots(g0, g1, slots, w, pos_arr, "sum_slots_%d" % w))
        send_sems, recv_sems, own, zones = _swap_start(own, name + "_start")
        for w, g in zip(ws, own):
            big[w] = _adamw(wmv[w][0], g, wmv[w][1], wmv[w][2], None, pos_arr, "adamw_own_%d" % w)
        theirs = _swap_wait(send_sems, recv_sems, own, zones, big[ws[-1]][1], name + "_wait")
        for w, g in zip(ws, theirs):
            big[w] = _adamw(wmv[w][0], g, wmv[w][1], wmv[w][2], big[w], pos_arr, "adamw_other_%d" % w)
        after = big[ws[-1]][1]

    def both(i):
        return jnp.stack([grads[0][i][0], grads[1][i][0]])
    dconv = jnp.stack([grads[0][0][:3], grads[1][0][:3]])
    dsinks = jnp.stack([grads[0][1][0, ::HEAD_DIM], grads[1][1][0, ::HEAD_DIM]])
    part = _pack_small([both(2), both(3), both(4), dg_final[0], dconv, dsinks, loss_tile[0, 0]])

    def spread(shard):
        return lax.dynamic_update_slice(jnp.zeros((2, 3, CONV_CH), F32), shard, (0, 0, chip * conv_n))
    zero = jnp.zeros((), F32)
    packs = [_pack_small([a, b, c_, e, spread(f), g_, zero]) for a, b, c_, e, f, g_ in (
        (g_mix, g_group, g_mlp, g_final, conv_w, sinks),
        (m_g_mix, m_g_group, m_g_mlp, m_g_final, m_conv_w, m_sinks),
        (v_g_mix, v_g_group, v_g_mlp, v_g_final, v_conv_w, v_sinks))]
    shapes = [g_mix.shape, g_group.shape, g_mlp.shape, g_final.shape, (2, 3, CONV_CH), sinks.shape, ()]
    small = [_unpack_small(p, shapes) for p in _small_sync(part, *packs)]

    def shard_of(full):
        return lax.dynamic_slice(full, (0, 0, chip * conv_n), (2, 3, conv_n))
    small = [(s[0], s[1], s[2], s[3], shard_of(s[4]), s[5], s[6]) for s in small]
    loss = small[0][6]

    def ordered(kind):
        b = [big[i][kind] for i in range(N_BIG)]
        s = small[kind]
        return [b[0], s[4], s[5], s[0], s[1], b[1], s[2], b[2], b[3], s[3]]

    return (loss, dx[None], *ordered(0), *ordered(1), *ordered(2), *ordered(3))
```

```python
import functools

import jax
import jax.numpy as jnp
from jax import lax
from jax.experimental import pallas as pl
from jax.experimental.pallas import tpu as pltpu

HEAD_DIM = 64
N_HEADS = 6
C_GROUP = 3
A_WIDTH = N_HEADS * HEAD_DIM
C_KV_WIDTH = 2 * HEAD_DIM
CONV_CH = 256
ZA_W = 3 * A_WIDTH
ZB_W = 3 * CONV_CH
ZC_W = A_WIDTH + 2 * C_KV_WIDTH
IN_WIDTH = ZA_W + ZB_W + ZC_W
MIX_WIDTH = A_WIDTH + CONV_CH + A_WIDTH
DILATIONS = (1, 4, 16)
A_MAX_DIST = 128
C_MAX_DIST = 127
TQ = 128
EPS = 1e-6
SCALE = HEAD_DIM ** -0.5
NEG = -1e30
HALO = 8

ADAM_LR = 0.001
ADAM_B1 = 0.9
ADAM_B2 = 0.999
ADAM_EPS = 1e-08
ADAM_WD = 0.01
ADAM_STEP = 10

BF = jnp.bfloat16
F32 = jnp.float32
MESH = pl.DeviceIdType.MESH
VMEM_LIMIT = 56 * 1024 * 1024


def _cparams(*sem):
    return pltpu.CompilerParams(dimension_semantics=sem, vmem_limit_bytes=VMEM_LIMIT)


def _nt(a, b):
    return lax.dot_general(a, b, (((1,), (1,)), ((), ())), preferred_element_type=F32)


def _tn(a, b):
    return lax.dot_general(a, b, (((0,), (0,)), ((), ())), preferred_element_type=F32)


def _nn(a, b):
    return jnp.dot(a, b, preferred_element_type=F32)


def _rows(tb, w):
    return pl.BlockSpec((tb, w), lambda i: (i, 0))


def _whole(shape):
    return pl.BlockSpec(shape, lambda *_: (0,) * len(shape))


def _layer(shape, l):
    return pl.BlockSpec((None,) + shape, lambda *_: (l,) + (0,) * len(shape))


def _rms_scale(v):
    return lax.rsqrt(jnp.mean(v * v, axis=-1, keepdims=True) + EPS)


def _norm_bwd(dxhat, xhat, r):
    return r * (dxhat - xhat * jnp.mean(dxhat * xhat, axis=-1, keepdims=True))


def _qkv_fwd(x, g, w_all, l, tb):
    s, d = x.shape

    def body(x_ref, g_ref, w_ref, h_ref, za_ref, zb_ref, zc_ref):
        xv = x_ref[...]
        h = ((xv * _rms_scale(xv)) * g_ref[...]).astype(BF)
        h_ref[...] = h
        z = jnp.concatenate([_nn(h, w_ref[k]) for k in range(N_CHIPS)], axis=1)
        za_ref[...] = z[:, :ZA_W]
        zb_ref[...] = z[:, ZA_W:ZA_W + ZB_W]
        zc_ref[...] = z[:, ZA_W + ZB_W:]

    return pl.pallas_call(
        body, grid=(s // tb,), name="qkv_fwd",
        in_specs=[_rows(tb, d), _whole((1, d)), _layer((N_CHIPS, d, IN_WIDTH // N_CHIPS), l)],
        out_specs=[_rows(tb, d), _rows(tb, ZA_W), _rows(tb, ZB_W), _rows(tb, ZC_W)],
        out_shape=[jax.ShapeDtypeStruct((s, d), BF), jax.ShapeDtypeStruct((s, ZA_W), F32),
                   jax.ShapeDtypeStruct((s, ZB_W), F32), jax.ShapeDtypeStruct((s, ZC_W), F32)],
        compiler_params=_cparams("parallel"),
    )(x, g, w_all)


N_STRIPS = 16


def _strips(a):
    s, w = a.shape
    return a.reshape(4, 4, s // N_STRIPS, w)


P_ROWS = {16: TQ, 4: 32, 1: 8}


def _p_sub(s, dil, most):
    while (s // dil // TQ) % most:
        most //= 2
    return most


def _p_grid(s, dil, n_sub):
    nb = s // dil // TQ // n_sub
    return {16: (4, 4, nb), 4: (4, nb), 1: (nb,)}[dil]


def _p_spec(dil, cw, col, n_sub, prev=False):
    rows = P_ROWS[dil] * (1 if prev else n_sub)

    def blk(j):
        return jnp.maximum(n_sub * j - 1, 0) if prev else j
    if dil == 16:
        return pl.BlockSpec((None, None, rows, cw), lambda f, e, j: (f, e, blk(j), col))
    if dil == 4:
        return pl.BlockSpec((None, 4, rows, cw), lambda f, j: (f, 0, blk(j), col))
    return pl.BlockSpec((4, 4, rows, cw), lambda j: (0, 0, blk(j), col))


def _block_pos(i, dil):
    if dil == 16:
        return i
    if dil == 4:
        return 4 * (i % 32) + i // 32
    return 16 * (i % 8) + 4 * ((i // 8) % 4) + i // 32


def _band_mask(b, dil, max_dist):
    qi = _block_pos(lax.broadcasted_iota(jnp.int32, (TQ, 2 * TQ), 0), dil)
    col = lax.broadcasted_iota(jnp.int32, (TQ, 2 * TQ), 1)
    cur = col >= TQ
    dist = qi - _block_pos(col % TQ, dil) + jnp.where(cur, 0, TQ)
    return (dist >= 0) & (dist <= max_dist) & (cur | (b > 0))


def _hs(h):
    return slice(h * HEAD_DIM, (h + 1) * HEAD_DIM)


def _ld(ref, cols, rows=slice(None)):
    v = ref[..., rows, cols]
    return v.reshape(TQ, v.shape[-1])


def _st(ref, cols, val, rows=slice(None)):
    lead = ref.shape[:-2] + (ref.shape[-2] if rows == slice(None) else rows.stop - rows.start,)
    ref[..., rows, cols] = val.reshape(lead + (val.shape[-1],))


def _attn_fwd(z, dil, kw, kcol, vcol, n_rep, max_dist, name):
    s, zw = z.shape
    n_sub = _p_sub(s, dil, 2)
    grid = _p_grid(s, dil, n_sub)
    o_dt = F32 if P_ROWS[dil] * n_sub < 16 else BF

    def body(q_ref, kp_ref, kc_ref, vp_ref, vc_ref, o_ref, lse_ref):
        for t in range(n_sub):
            rows = slice(t * P_ROWS[dil], (t + 1) * P_ROWS[dil])
            before = (slice(None),) if t == 0 else (slice((t - 1) * P_ROWS[dil], t * P_ROWS[dil]),)
            kb_ref, vb_ref = (kp_ref, vp_ref) if t == 0 else (kc_ref, vc_ref)
            mask = _band_mask(n_sub * pl.program_id(len(grid) - 1) if t == 0 else 1, dil, max_dist)
            scs, v2s = [], []
            for kh in range(N_HEADS // n_rep):
                k2 = jnp.concatenate([_ld(kb_ref, _hs(kh), *before), _ld(kc_ref, _hs(kh), rows)], axis=0).astype(BF)
                v2s.append(jnp.concatenate([_ld(vb_ref, _hs(kh), *before), _ld(vc_ref, _hs(kh), rows)],
                                           axis=0).astype(BF))
                for h in range(kh * n_rep, (kh + 1) * n_rep):
                    q = (_ld(q_ref, _hs(h), rows) * SCALE).astype(BF)
                    scs.append(jnp.where(mask, _nt(q, k2), NEG))
            for h, sc in enumerate(scs):
                m = jnp.max(sc, axis=1, keepdims=True)
                p = jnp.exp(sc - m)
                l = jnp.sum(p, axis=1, keepdims=True)
                _st(o_ref, _hs(h), (_nn(p.astype(BF), v2s[h // n_rep]) / l).astype(o_ref.dtype), rows)
                _st(lse_ref, _hs(h), jnp.broadcast_to(m + jnp.log(l), (TQ, HEAD_DIM)), rows)

    res = pl.pallas_call(
        body, grid=grid, name=name,
        in_specs=[_p_spec(dil, A_WIDTH, 0, n_sub), _p_spec(dil, kw, kcol, n_sub, True), _p_spec(dil, kw, kcol, n_sub),
                  _p_spec(dil, kw, vcol, n_sub, True), _p_spec(dil, kw, vcol, n_sub)],
        out_specs=[_p_spec(dil, A_WIDTH, 0, n_sub)] * 2,
        out_shape=[jax.ShapeDtypeStruct((4, 4, s // N_STRIPS, A_WIDTH), dt) for dt in (o_dt, F32)],
        compiler_params=_cparams(*(("parallel",) * len(grid))),
    )(*[_strips(z)] * 5)
    return [a.reshape(s, A_WIDTH) for a in res]


def _attn_merge(parts_a, part_c, sink_row, tb):
    s = part_c[0].shape[0]
    n_a = len(parts_a)

    def body(*refs):
        ins, sink_ref = refs[:2 * n_a + 2], refs[2 * n_a + 2]
        ya_ref, lsea_ref, yc_ref, lsec_ref = refs[2 * n_a + 3:]
        lses = [ins[2 * p + 1][...] for p in range(n_a)]
        m = functools.reduce(jnp.maximum, lses)
        ws = [jnp.exp(v - m) for v in lses]
        l = functools.reduce(jnp.add, ws)
        ya_ref[...] = functools.reduce(jnp.add, [w * ins[2 * p][...].astype(F32) for p, w in enumerate(ws)]) / l
        lsea_ref[...] = m + jnp.log(l)
        o_c, lse_c = [r[...].astype(F32) for r in ins[2 * n_a:]]
        sk = sink_ref[...]
        m2 = jnp.maximum(lse_c, sk)
        w = jnp.exp(lse_c - m2)
        l2 = w + jnp.exp(sk - m2)
        yc_ref[...] = o_c * (w / l2)
        lsec_ref[...] = m2 + jnp.log(l2)

    return pl.pallas_call(
        body, grid=(s // tb,), name="attn_merge",
        in_specs=[_rows(tb, A_WIDTH)] * (2 * n_a + 2) + [_whole((1, A_WIDTH))],
        out_specs=[_rows(tb, A_WIDTH)] * 4, out_shape=[jax.ShapeDtypeStruct((s, A_WIDTH), F32)] * 4,
        compiler_params=_cparams("parallel"),
    )(*[a for part in parts_a + [part_c] for a in part], sink_row)


def _shift_down(v, n, halo):
    rows = v.shape[0]
    out = pltpu.roll(v, n, 0)
    row = lax.broadcasted_iota(jnp.int32, v.shape, 0)
    for t in range(n):
        out = jnp.where(row == t, halo[HALO - n + t:HALO - n + t + 1, :], out)
    return out


def _shift_up(v, n, halo):
    rows = v.shape[0]
    out = pltpu.roll(v, rows - n, 0)
    row = lax.broadcasted_iota(jnp.int32, v.shape, 0)
    for t in range(n):
        out = jnp.where(row == rows - n + t, halo[t:t + 1, :], out)
    return out


def _strip(v, b):
    return v[b % 4, b // 4]


def _conv_strips(zb, prev, cw):
    gb = [_strip(zb, b)[:, :CONV_CH] for b in range(N_STRIPS)]
    gc = [_strip(zb, b)[:, CONV_CH:2 * CONV_CH] for b in range(N_STRIPS)]
    xb = [_strip(zb, b)[:, 2 * CONV_CH:] for b in range(N_STRIPS)]
    u = [g * v for g, v in zip(gc, xb)]
    uh = prev[:, :, CONV_CH:2 * CONV_CH] * prev[:, :, 2 * CONV_CH:]
    wrapped = {14: _shift_down(u[14], 1, uh[2]), 15: _shift_down(u[15], 1, uh[3])}
    u1 = [u[b - 1] if b >= 1 else wrapped[15] for b in range(N_STRIPS)]
    u2 = [u[b - 2] if b >= 2 else wrapped[14 + b] for b in range(N_STRIPS)]
    c = [cw[0:1, :] * u2[b] + cw[1:2, :] * u1[b] + cw[2:3, :] * u[b] for b in range(N_STRIPS)]
    return gb, gc, xb, u, u1, u2, c


def _strip_rows(ta, w):
    return pl.BlockSpec((4, 4, ta, w), lambda i: (0, 0, i, 0))


def _prev_rows(ta, w):
    return pl.BlockSpec((4, None, HALO, w), lambda i: (0, 3, jnp.maximum(i * (ta // HALO) - 1, 0), 0))


def _next_rows(ta, w, nblk):
    return pl.BlockSpec((4, None, HALO, w),
                        lambda i: (0, 0, jnp.minimum((i + 1) * (ta // HALO), nblk * (ta // HALO) - 1), 0))


def _mix_fwd(x, ya, yc, zb, cw, gg, wo_all, l, tb):
    s, d = x.shape
    ta = tb // N_STRIPS

    def body(x_ref, ya_ref, yc_ref, zb_ref, zbp_ref, cw_ref, gg_ref, wo_ref, x1_ref, yb_ref):
        i = pl.program_id(0)
        prev = jnp.where(i > 0, zbp_ref[...], 0.0)
        gb, _, _, _, _, _, c = _conv_strips(zb_ref[...], prev, cw_ref[...])
        for b in range(N_STRIPS):
            yb_ref[b % 4, b // 4] = gb[b] * c[b]
        yb = yb_ref[...].reshape(tb, CONV_CH)
        ya, yc = ya_ref[...].reshape(tb, A_WIDTH), yc_ref[...].reshape(tb, A_WIDTH)
        n = jnp.concatenate([ya * _rms_scale(ya), yb * _rms_scale(yb), yc * _rms_scale(yc)], axis=1)
        n = (n * gg_ref[...]).astype(BF)
        x1 = x_ref[...].reshape(tb, d) + _nn(n, wo_ref[...].reshape(MIX_WIDTH, d))
        x1_ref[...] = x1.reshape(4, 4, ta, d)

    res = pl.pallas_call(
        body, grid=(s // tb,), name="mix_fwd",
        in_specs=[_strip_rows(ta, d), _strip_rows(ta, A_WIDTH), _strip_rows(ta, A_WIDTH), _strip_rows(ta, ZB_W),
                  _prev_rows(ta, ZB_W), _whole((HALO, CONV_CH)), _whole((1, MIX_WIDTH)),
                  _layer((N_CHIPS, MIX_WIDTH // N_CHIPS, d), l)],
        out_specs=[_strip_rows(ta, d), _strip_rows(ta, CONV_CH)],
        out_shape=[jax.ShapeDtypeStruct((4, 4, s // N_STRIPS, d), F32),
                   jax.ShapeDtypeStruct((4, 4, s // N_STRIPS, CONV_CH), F32)],
        compiler_params=_cparams("parallel"),
    )(_strips(x), _strips(ya), _strips(yc), _strips(zb), _strips(zb), cw, gg, wo_all)
    return res[0].reshape(s, d), res[1].reshape(s, CONV_CH)


def _mlp_fwd(x1, g, w1_all, w2_all, l, tb, tf):
    s, d = x1.shape
    ff = w1_all.shape[1] * w1_all.shape[3]
    nj = ff // tf

    def body(x_ref, g_ref, w1_ref, w2_ref, x2_ref, h2_ref, ap_ref, acc):
        j = pl.program_id(1)

        @pl.when(j == 0)
        def _():
            xv = x_ref[...]
            h2_ref[...] = ((xv * _rms_scale(xv)) * g_ref[...]).astype(BF)
            acc[...] = jnp.zeros_like(acc)

        ap = _nn(h2_ref[...], w1_ref[...])
        ap_ref[...] = ap.astype(BF)
        a = jnp.square(jnp.maximum(ap, 0.0)).astype(BF)
        acc[...] += _nn(a, w2_ref[...])

        @pl.when(j == nj - 1)
        def _():
            x2_ref[...] = x_ref[...] + acc[...]

    return pl.pallas_call(
        body, grid=(s // tb, nj), name="mlp_fwd",
        in_specs=[pl.BlockSpec((tb, d), lambda i, j: (i, 0)), _whole((1, d)),
                  pl.BlockSpec((None, None, d, tf), lambda i, j: (l, j, 0, 0)),
                  pl.BlockSpec((None, None, tf, d), lambda i, j: (l, j, 0, 0))],
        out_specs=[pl.BlockSpec((tb, d), lambda i, j: (i, 0)), pl.BlockSpec((tb, d), lambda i, j: (i, 0)),
                   pl.BlockSpec((tb, tf), lambda i, j: (i, j))],
        out_shape=[jax.ShapeDtypeStruct((s, d), F32), jax.ShapeDtypeStruct((s, d), BF),
                   jax.ShapeDtypeStruct((s, ff), BF)],
        scratch_shapes=[pltpu.VMEM((tb, d), F32)],
        compiler_params=_cparams("parallel", "arbitrary"),
    )(x1, g, w1_all, w2_all)


def _loss_head(x, g, tgt, tb):
    s, d = x.shape

    def body(x_ref, g_ref, t_ref, dx_ref, loss_ref, dg_ref):
        i = pl.program_id(0)

        @pl.when(i == 0)
        def _():
            loss_ref[...] = jnp.zeros_like(loss_ref)
            dg_ref[...] = jnp.zeros_like(dg_ref)

        xv = x_ref[...]
        r = _rms_scale(xv)
        xhat = xv * r
        err = xhat * g_ref[...] - t_ref[...]
        part = jnp.sum(jnp.mean(jnp.square(err), axis=-1, keepdims=True), axis=0, keepdims=True)
        loss_ref[...] += 0.5 * part
        dy = err * (1.0 / d)
        dg_ref[...] += jnp.sum(dy * xhat, axis=0, keepdims=True)
        dx_ref[...] = _norm_bwd(dy * g_ref[...], xhat, r)

    return pl.pallas_call(
        body, grid=(s // tb,), name="loss_head",
        in_specs=[_rows(tb, d), _whole((1, d)), _rows(tb, d)],
        out_specs=[_rows(tb, d), _whole((HALO, 128)), _whole((HALO, d))],
        out_shape=[jax.ShapeDtypeStruct((s, d), F32), jax.ShapeDtypeStruct((HALO, 128), F32),
                   jax.ShapeDtypeStruct((HALO, d), F32)],
        compiler_params=_cparams("arbitrary"),
    )(x, g, tgt)


def _mlp_bwd(dx2, x1, ap, g, w1_all, w2_all, l, tb, tf):
    s, d = x1.shape
    ff = ap.shape[1]
    nj = ff // tf

    def body(dx2_ref, x1_ref, ap_ref, g_ref, w1_ref, w2_ref, dx1_ref, dap_ref, dg_ref, acc):
        i, j = pl.program_id(0), pl.program_id(1)

        @pl.when((i == 0) & (j == 0))
        def _():
            dg_ref[...] = jnp.zeros_like(dg_ref)

        @pl.when(j == 0)
        def _():
            acc[...] = jnp.zeros_like(acc)

        da = _nt(dx2_ref[...].astype(BF), w2_ref[...])
        dap = (da * (2.0 * jnp.maximum(ap_ref[...].astype(F32), 0.0))).astype(BF)
        dap_ref[...] = dap
        acc[...] += _nt(dap, w1_ref[...])

        @pl.when(j == nj - 1)
        def _():
            xv = x1_ref[...]
            r = _rms_scale(xv)
            xhat = xv * r
            dh = acc[...]
            dg_ref[...] += jnp.sum(dh * xhat, axis=0, keepdims=True)
            dx1_ref[...] = dx2_ref[...] + _norm_bwd(dh * g_ref[...], xhat, r)

    return pl.pallas_call(
        body, grid=(s // tb, nj), name="mlp_bwd",
        in_specs=[pl.BlockSpec((tb, d), lambda i, j: (i, 0)), pl.BlockSpec((tb, d), lambda i, j: (i, 0)),
                  pl.BlockSpec((tb, tf), lambda i, j: (i, j)),
                  _whole((1, d)), pl.BlockSpec((None, None, d, tf), lambda i, j: (l, j, 0, 0)),
                  pl.BlockSpec((None, None, tf, d), lambda i, j: (l, j, 0, 0))],
        out_specs=[pl.BlockSpec((tb, d), lambda i, j: (i, 0)), pl.BlockSpec((tb, tf), lambda i, j: (i, j)),
                   _whole((HALO, d))],
        out_shape=[jax.ShapeDtypeStruct((s, d), F32), jax.ShapeDtypeStruct((s, ff), BF),
                   jax.ShapeDtypeStruct((HALO, d), F32)],
        scratch_shapes=[pltpu.VMEM((tb, d), F32)],
        compiler_params=_cparams("arbitrary", "arbitrary"),
    )(dx2, x1, ap, g, w1_all, w2_all)


def _wgrad(a, b, tm, tn, ts, name, relu2=False):
    s, m = a.shape
    n = b.shape[1]
    ns = s // ts

    def body(a_ref, b_ref, o_ref, acc):
        k = pl.program_id(2)

        @pl.when(k == 0)
        def _():
            acc[...] = jnp.zeros_like(acc)

        av = a_ref[...]
        if relu2:
            av = jnp.square(jnp.maximum(av.astype(F32), 0.0)).astype(BF)
        acc[...] += _tn(av, b_ref[...].astype(BF))

        @pl.when(k == ns - 1)
        def _():
            o_ref[...] = acc[...].astype(BF)

    return pl.pallas_call(
        body, grid=(m // tm, n // tn, ns), name=name,
        in_specs=[pl.BlockSpec((ts, tm), lambda i, j, k: (k, i)), pl.BlockSpec((ts, tn), lambda i, j, k: (k, j))],
        out_specs=pl.BlockSpec((tm, tn), lambda i, j, k: (i, j)),
        out_shape=jax.ShapeDtypeStruct((m, n), BF),
        scratch_shapes=[pltpu.VMEM((tm, tn), F32)],
        compiler_params=_cparams("parallel", "parallel", "arbitrary"),
    )(a, b)


def _mix_bwd(dx1, ya, yb, yc, lse_c, sink_row, gg, wo_all, l, tb):
    s, d = dx1.shape

    def body(dx_ref, ya_ref, yb_ref, yc_ref, lse_ref, sink_ref, gg_ref, wo_ref,
             n_ref, dya_ref, dyc_ref, da_ref, dc_ref, dyb_ref, dg_ref, dsink_ref):
        i = pl.program_id(0)

        @pl.when(i == 0)
        def _():
            dg_ref[...] = jnp.zeros_like(dg_ref)
            dsink_ref[...] = jnp.zeros_like(dsink_ref)

        dn = _nt(dx_ref[...].astype(BF), wo_ref[...].reshape(MIX_WIDTH, d))
        ys = [ya_ref[...], yb_ref[...], yc_ref[...]]
        rs = [_rms_scale(v) for v in ys]
        nhat = jnp.concatenate([v * r for v, r in zip(ys, rs)], axis=1)
        gg = gg_ref[...]
        n_ref[...] = (nhat * gg).astype(BF)
        dg_ref[...] += jnp.sum(dn * nhat, axis=0, keepdims=True)
        dnh = dn * gg
        bounds = [(0, A_WIDTH), (A_WIDTH, A_WIDTH + CONV_CH), (A_WIDTH + CONV_CH, MIX_WIDTH)]
        dys = [_norm_bwd(dnh[:, lo:hi], nhat[:, lo:hi], r) for (lo, hi), r in zip(bounds, rs)]
        dyb_ref[...] = dys[1]
        head = [lax.broadcasted_iota(jnp.int32, (A_WIDTH, A_WIDTH), k) // HEAD_DIM for k in (0, 1)]
        ones = (head[0] == head[1]).astype(BF)
        for dy, y, dy_ref, dd_ref in ((dys[0], ys[0], dya_ref, da_ref), (dys[2], ys[2], dyc_ref, dc_ref)):
            dy_ref[...] = dy
            t = dy * y
            hi = t.astype(BF)
            dd_ref[...] = _nn(hi, ones) + _nn((t - hi.astype(F32)).astype(BF), ones)
        dsink_ref[...] -= jnp.sum(jnp.exp(sink_ref[...] - lse_ref[...]) * dc_ref[...], axis=0, keepdims=True)

    return pl.pallas_call(
        body, grid=(s // tb,), name="mix_bwd",
        in_specs=[_rows(tb, d), _rows(tb, A_WIDTH), _rows(tb, CONV_CH), _rows(tb, A_WIDTH), _rows(tb, A_WIDTH),
                  _whole((1, A_WIDTH)), _whole((1, MIX_WIDTH)), _layer((N_CHIPS, MIX_WIDTH // N_CHIPS, d), l)],
        out_specs=[_rows(tb, MIX_WIDTH), _rows(tb, A_WIDTH), _rows(tb, A_WIDTH), _rows(tb, A_WIDTH),
                   _rows(tb, A_WIDTH), _rows(tb, CONV_CH), _whole((HALO, MIX_WIDTH)), _whole((HALO, A_WIDTH))],
        out_shape=[jax.ShapeDtypeStruct((s, MIX_WIDTH), BF), jax.ShapeDtypeStruct((s, A_WIDTH), F32),
                   jax.ShapeDtypeStruct((s, A_WIDTH), F32), jax.ShapeDtypeStruct((s, A_WIDTH), F32),
                   jax.ShapeDtypeStruct((s, A_WIDTH), F32), jax.ShapeDtypeStruct((s, CONV_CH), F32),
                   jax.ShapeDtypeStruct((HALO, MIX_WIDTH), F32), jax.ShapeDtypeStruct((HALO, A_WIDTH), F32)],
        compiler_params=_cparams("arbitrary"),
    )(dx1, ya, yb, yc, lse_c, sink_row, gg, wo_all)


def _attn_bwd(z, dy, lse, dd, dil, kw, kcol, vcol, n_rep, max_dist, name):
    s, zw = z.shape
    n_sub = _p_sub(s, dil, 2) if n_rep == 1 else 1
    grid = _p_grid(s, dil, n_sub)
    n_kv = N_HEADS // n_rep
    dt = F32 if P_ROWS[dil] * n_sub < 16 else BF

    def body(q_ref, kp_ref, kc_ref, vp_ref, vc_ref, dy_ref, lse_ref, dd_ref, dq_ref, dkp_ref, dkc_ref, dvp_ref, dvc_ref):
        for t in range(n_sub):
            rows = slice(t * P_ROWS[dil], (t + 1) * P_ROWS[dil])
            before = (slice(None),) if t == 0 else (slice((t - 1) * P_ROWS[dil], t * P_ROWS[dil]),)
            kb_ref, vb_ref = (kp_ref, vp_ref) if t == 0 else (kc_ref, vc_ref)
            mask = _band_mask(n_sub * pl.program_id(len(grid) - 1) if t == 0 else 1, dil, max_dist)
            k2s, qs, dys, scs, dps = [], [], [], [], []
            for kh in range(n_kv):
                k2s.append(jnp.concatenate([_ld(kb_ref, _hs(kh), *before), _ld(kc_ref, _hs(kh), rows)],
                                           axis=0).astype(BF))
                v2 = jnp.concatenate([_ld(vb_ref, _hs(kh), *before), _ld(vc_ref, _hs(kh), rows)], axis=0).astype(BF)
                for h in range(kh * n_rep, (kh + 1) * n_rep):
                    qs.append((_ld(q_ref, _hs(h), rows) * SCALE).astype(BF))
                    dys.append(_ld(dy_ref, _hs(h), rows).astype(BF))
                    scs.append(jnp.where(mask, _nt(qs[h], k2s[kh]), NEG))
                    dps.append(_nt(dys[h], v2))
            for kh in range(n_kv):
                k2 = k2s[kh]
                dk2 = jnp.zeros((2 * TQ, HEAD_DIM), F32)
                dv2 = jnp.zeros((2 * TQ, HEAD_DIM), F32)
                for h in range(kh * n_rep, (kh + 1) * n_rep):
                    lse_h = _ld(lse_ref, slice(h * HEAD_DIM, h * HEAD_DIM + 1), rows)
                    dd_h = _ld(dd_ref, slice(h * HEAD_DIM, h * HEAD_DIM + 1), rows)
                    p = jnp.exp(scs[h] - lse_h)
                    ds = (p * (dps[h] - dd_h)).astype(BF)
                    _st(dq_ref, _hs(h), (_nn(ds, k2) * SCALE).astype(dt), rows)
                    dk2 = dk2 + _tn(ds, qs[h])
                    dv2 = dv2 + _tn(p.astype(BF), dys[h])
                _st(dkp_ref, _hs(kh), dk2[:TQ].astype(dt), rows)
                _st(dkc_ref, _hs(kh), dk2[TQ:].astype(dt), rows)
                _st(dvp_ref, _hs(kh), dv2[:TQ].astype(dt), rows)
                _st(dvc_ref, _hs(kh), dv2[TQ:].astype(dt), rows)

    args = [_strips(z)] * 5 + [_strips(a) for a in (dy, lse, dd)]
    pair = _p_spec(dil, A_WIDTH, 0, n_sub)
    in_specs = [pair, _p_spec(dil, kw, kcol, n_sub, True), _p_spec(dil, kw, kcol, n_sub),
                _p_spec(dil, kw, vcol, n_sub, True), _p_spec(dil, kw, vcol, n_sub)] + [pair] * 3
    out_specs = [pair] + [_p_spec(dil, kw, 0, n_sub)] * 4
    na = s // N_STRIPS
    out_shape = [jax.ShapeDtypeStruct((4, 4, na, A_WIDTH), dt)] + [jax.ShapeDtypeStruct((4, 4, na, kw), dt)] * 4
    res = pl.pallas_call(
        body, grid=grid, name=name, in_specs=in_specs, out_specs=out_specs, out_shape=out_shape,
        compiler_params=_cparams(*(("parallel",) * len(grid))),
    )(*args)
    return [res[0].reshape(s, A_WIDTH)] + [a.reshape(s, kw) for a in res[1:]]


DZ_TA = 16


def _dz_assemble(parts_a, parts_c, dyb, zb, cw):
    s = zb.shape[0]
    na = s // N_STRIPS
    nb = na // DZ_TA

    def ahead(w, k):
        return pl.BlockSpec((4, 4, DZ_TA, w), lambda i: (0, 0, jnp.minimum(i + k, nb - 1), 0))

    args, in_specs = [], []
    for dil, (dq, dkp, dkc, dvp, dvc) in zip(DILATIONS + (1,), parts_a + [parts_c]):
        w = dkp.shape[1]
        here = _strip_rows(DZ_TA, w)
        if dil == 1:
            args += [dq, dkp, dkp, dkc, dvp, dvp, dvc]
            in_specs += [_strip_rows(DZ_TA, A_WIDTH), here, ahead(w, 1), here, here, ahead(w, 1), here]
        else:
            k = 8 * dil // DZ_TA
            args += [dq, dkp, dkc, dvp, dvc]
            in_specs += [_strip_rows(DZ_TA, A_WIDTH), ahead(w, k), here, ahead(w, k), here]
    n_att = len(args)
    args = [_strips(a) for a in args] + [_strips(dyb), _strips(dyb), _strips(zb), _strips(zb), _strips(zb), cw]
    in_specs += [_strip_rows(DZ_TA, CONV_CH), _next_rows(DZ_TA, CONV_CH, nb), _strip_rows(DZ_TA, ZB_W),
                 _prev_rows(DZ_TA, ZB_W), _next_rows(DZ_TA, ZB_W, nb), _whole((HALO, CONV_CH))]

    def body(*refs):
        att = list(refs[:n_att])
        dyb_ref, dybn_ref, zb_ref, zbp_ref, zbn_ref, cw_ref, dz_ref, dcw_ref = refs[n_att:]
        i = pl.program_id(0)

        @pl.when(i == 0)
        def _():
            dcw_ref[...] = jnp.zeros_like(dcw_ref)

        def shifted(dil):
            if dil == 1:
                dq_r, kp0, kp1, dkc_r, vp0, vp1, dvc_r = [att.pop(0) for _ in range(7)]
                live = i + 1 < nb
                half = DZ_TA // 2
                kp0, kp1, vp0, vp1 = [r[...].astype(F32) for r in (kp0, kp1, vp0, vp1)]
                dkp = jnp.concatenate([kp0[:, :, half:, :], jnp.where(live, kp1[:, :, :half, :], 0.0)], axis=2)
                dvp = jnp.concatenate([vp0[:, :, half:, :], jnp.where(live, vp1[:, :, :half, :], 0.0)], axis=2)
            else:
                dq_r, dkp_r, dkc_r, dvp_r, dvc_r = [att.pop(0) for _ in range(5)]
                live = i + 8 * dil // DZ_TA < nb
                dkp = jnp.where(live, dkp_r[...].astype(F32), 0.0)
                dvp = jnp.where(live, dvp_r[...].astype(F32), 0.0)
            return dq_r[...].astype(F32), dkc_r[...].astype(F32) + dkp, dvc_r[...].astype(F32) + dvp

        dq, dk, dv = shifted(DILATIONS[0])
        for dil in DILATIONS[1:]:
            dq2, dk2, dv2 = shifted(dil)
            dq, dk, dv = dq + dq2, dk + dk2, dv + dv2
        dz_ref[:, :, :, 0:A_WIDTH] = dq.astype(BF)
        dz_ref[:, :, :, A_WIDTH:2 * A_WIDTH] = dk.astype(BF)
        dz_ref[:, :, :, 2 * A_WIDTH:ZA_W] = dv.astype(BF)
        dq, dk, dv = shifted(1)
        c0 = ZA_W + ZB_W
        dz_ref[:, :, :, c0:c0 + A_WIDTH] = dq.astype(BF)
        dz_ref[:, :, :, c0 + A_WIDTH:c0 + A_WIDTH + C_KV_WIDTH] = dk.astype(BF)
        dz_ref[:, :, :, c0 + A_WIDTH + C_KV_WIDTH:IN_WIDTH] = dv.astype(BF)

        cw = cw_ref[...]
        prev = jnp.where(i > 0, zbp_ref[...], 0.0)
        gb, gc, xb, u, u1, u2, c = _conv_strips(zb_ref[...], prev, cw)
        dyb = dyb_ref[...]
        dc = [_strip(dyb, b) * gb[b] for b in range(N_STRIPS)]
        dcn = jnp.where(i + 1 < nb, dybn_ref[...] * zbn_ref[:, :, :CONV_CH], 0.0)
        wrapped = [_shift_up(dc[0], 1, dcn[0]), _shift_up(dc[1], 1, dcn[1])]
        upd = [jnp.zeros((1, CONV_CH), F32)] * 3
        for b in range(N_STRIPS):
            dc1 = dc[b + 1] if b + 1 < N_STRIPS else wrapped[0]
            dc2 = dc[b + 2] if b + 2 < N_STRIPS else wrapped[b + 2 - N_STRIPS]
            du = cw[2:3, :] * dc[b] + cw[1:2, :] * dc1 + cw[0:1, :] * dc2
            f, e = b % 4, b // 4
            dz_ref[f, e, :, ZA_W:ZA_W + CONV_CH] = (_strip(dyb, b) * c[b]).astype(BF)
            dz_ref[f, e, :, ZA_W + CONV_CH:ZA_W + 2 * CONV_CH] = (du * xb[b]).astype(BF)
            dz_ref[f, e, :, ZA_W + 2 * CONV_CH:c0] = (du * gc[b]).astype(BF)
            for t, uu in enumerate((u2[b], u1[b], u[b])):
                upd[t] = upd[t] + jnp.sum(dc[b] * uu, axis=0, keepdims=True)
        row = lax.broadcasted_iota(jnp.int32, (HALO, CONV_CH), 0)
        tile = jnp.zeros((HALO, CONV_CH), F32)
        for t in range(3):
            tile = jnp.where(row == t, upd[t], tile)
        dcw_ref[...] += tile

    dz, dcw = pl.pallas_call(
        body, grid=(nb,), name="dz_assemble", in_specs=in_specs,
        out_specs=[_strip_rows(DZ_TA, IN_WIDTH), _whole((HALO, CONV_CH))],
        out_shape=[jax.ShapeDtypeStruct((4, 4, na, IN_WIDTH), BF), jax.ShapeDtypeStruct((HALO, CONV_CH), F32)],
        compiler_params=_cparams("arbitrary"),
    )(*args)
    return dz.reshape(s, IN_WIDTH), dcw


def _qkv_bwd(dz, dx1, x, g, w_all, l, tb, tokens_out):
    s, d = x.shape
    na, ta = s // N_STRIPS, tb // N_STRIPS

    def body(dz_ref, dx1_ref, x_ref, g_ref, w_ref, dx_ref, dg_ref):
        i = pl.program_id(0)

        @pl.when(i == 0)
        def _():
            dg_ref[...] = jnp.zeros_like(dg_ref)

        n = IN_WIDTH // N_CHIPS
        dz = dz_ref[...].reshape(tb, IN_WIDTH)
        dh = _nt(dz[:, 0:n], w_ref[0])
        for k in range(1, N_CHIPS):
            dh = dh + _nt(dz[:, k * n:(k + 1) * n], w_ref[k])
        xv = x_ref[...].reshape(tb, d)
        r = _rms_scale(xv)
        xhat = xv * r
        dg_ref[...] += jnp.sum(dh * xhat, axis=0, keepdims=True)
        dx = (dx1_ref[...].reshape(tb, d) + _norm_bwd(dh * g_ref[...], xhat, r)).reshape(4, 4, ta, d)
        if tokens_out:
            for b in range(N_STRIPS):
                dx_ref[:, b, :] = _strip(dx, b)
        else:
            dx_ref[...] = dx

    if tokens_out:
        dx_spec, dx_shape = pl.BlockSpec((ta, N_STRIPS, d), lambda i: (i, 0, 0)), (na, N_STRIPS, d)
    else:
        dx_spec, dx_shape = _strip_rows(ta, d), (4, 4, na, d)
    dx, dg = pl.pallas_call(
        body, grid=(s // tb,), name="qkv_bwd",
        in_specs=[_strip_rows(ta, IN_WIDTH), _strip_rows(ta, d), _strip_rows(ta, d), _whole((1, d)),
                  _layer((N_CHIPS, d, IN_WIDTH // N_CHIPS), l)],
        out_specs=[dx_spec, _whole((HALO, d))],
        out_shape=[jax.ShapeDtypeStruct(dx_shape, F32), jax.ShapeDtypeStruct((HALO, d), F32)],
        compiler_params=_cparams("arbitrary"),
    )(_strips(dz), _strips(dx1), _strips(x), g, w_all)
    return dx.reshape(s, d), dg


def _tile_rows(rows):
    return jnp.pad(rows, ((0, HALO - rows.shape[0]), (0, 0)))


def _to_strips(a, after, name):
    s, d = a.shape
    na = s // N_STRIPS
    ta = min(32, na)

    def body(a_ref, *rest):
        for b in range(N_STRIPS):
            rest[-1][b % 4, b // 4] = a_ref[:, b, :]

    return pl.pallas_call(
        body, grid=(na // ta,), name=name,
        in_specs=[pl.BlockSpec((ta, N_STRIPS, d), lambda i: (i, 0, 0))] + [ANY] * len(after),
        out_specs=_strip_rows(ta, d),
        out_shape=jax.ShapeDtypeStruct((4, 4, na, d), a.dtype), compiler_params=_cparams("parallel"),
    )(a.reshape(na, N_STRIPS, d), *after).reshape(s, d)


def _local_step(x, tgt, fetch, ff, sinks, g_mix, g_group, g_mlp, g_final, emit):
    s, d = x.shape
    depth = g_mix.shape[0]
    tb = min(512, s)
    tf = ff // N_CHIPS
    ts = min(1024, s)
    saved = []
    for l in range(depth):
        w_in, _, _, _, conv_w = fetch(0, l, x)
        cw = _tile_rows(conv_w[l])
        sk = jnp.repeat(sinks[l].reshape(N_HEADS), HEAD_DIM)[None]
        h, za, zb, zc = _qkv_fwd(x, g_mix[l][None], w_in, l, tb)
        parts_a = [_attn_fwd(za, dil, A_WIDTH, 1, 2, 1, A_MAX_DIST, "attn_a_fwd_%d" % dil) for dil in DILATIONS]
        part_c = _attn_fwd(zc, 1, C_KV_WIDTH, 3, 4, C_GROUP, C_MAX_DIST, "attn_c_fwd")
        ya, lse_a, yc, lse_c = _attn_merge(parts_a, part_c, sk, ts)
        w_in, w_o, w1, w2, _ = fetch(1, l, yc)
        x1, yb = _mix_fwd(x, ya, yc, zb, cw, g_group[l][None], w_o, l, ts)
        w_in, w_o, w1, w2, _ = fetch(2, l, x1)
        x2, h2, ap = _mlp_fwd(x1, g_mlp[l][None], w1, w2, l, ts, tf)
        saved.append((x, h, za, zb, zc, ya, lse_a, yc, lse_c, yb, x1, h2, ap, cw, sk))
        x = x2
    dx, loss_tile, dg_final = _loss_head(x, g_final[None], tgt, ts)
    grads = [None] * depth
    tok = jnp.zeros((), F32)
    for l in reversed(range(depth)):
        x0, h, za, zb, zc, ya, lse_a, yc, lse_c, yb, x1, h2, ap, cw, sk = saved[l]
        dx1, dap, dg_mlp = _mlp_bwd(dx, x1, ap, g_mlp[l][None] + tok, w1, w2, l, ts, tf)
        tok = emit(l, 3, _wgrad(ap, dx, min(1024, ff), d, 2 * ts, "wgrad_ff_out", relu2=True))
        tok = tok + emit(l, 2, _wgrad(h2, dap, d, min(1024, ff), 2 * ts, "wgrad_ff_in"))
        n, dya, dyc, dd_a, dd_c, dyb, dg_group, dsink = _mix_bwd(dx1, ya, yb, yc, lse_c, sk, g_group[l][None] + tok,
                                                                 w_o, l, tb)
        tok = emit(l, 1, _wgrad(n, dx1, MIX_WIDTH, d, ts, "wgrad_o"))
        cw = cw + tok
        parts_a = [_attn_bwd(za, dya, lse_a, dd_a, dil, A_WIDTH, 1, 2, 1, A_MAX_DIST, "attn_a_bwd_%d" % dil)
                   for dil in DILATIONS]
        parts_c = _attn_bwd(zc, dyc, lse_c, dd_c, 1, C_KV_WIDTH, 3, 4, C_GROUP, C_MAX_DIST, "attn_c_bwd")
        dz, dcw = _dz_assemble(parts_a, parts_c, dyb, zb, cw)
        tok = emit(l, 0, _wgrad(h, dz, d, IN_WIDTH // 4, 2 * ts, "wgrad_in"))
        dx, dg_mix = _qkv_bwd(dz, dx1, x0, g_mix[l][None] + tok, w_in, l, tb, l == 0)
        grads[l] = (dcw, dsink, dg_mix, dg_group, dg_mlp)
    return loss_tile, dx, grads, dg_final


ANY = pl.BlockSpec(memory_space=pl.ANY)
SHARD_AXES = (2, 1, 2, 1)
N_BIG = len(SHARD_AXES)
N_CHIPS = 4
N_DEV = 8


def _mesh_pos():
    return lax.axis_index("x"), lax.axis_index("y"), lax.axis_index("c")


def _flip(v, bit):
    return 1 - v if bit else v


def _place_shard(shard, chip_arr, name):
    _, rows, cols = shard.shape
    tr = min(256, rows)

    def body(chip_ref, x_ref, o_ref):
        o_ref[...] = x_ref[...].astype(BF)

    return pl.pallas_call(
        body, name=name,
        grid_spec=pltpu.PrefetchScalarGridSpec(
            num_scalar_prefetch=1, grid=(2, rows // tr),
            in_specs=[pl.BlockSpec((None, tr, cols), lambda l, i, chip: (l, i, 0))],
            out_specs=pl.BlockSpec((None, None, tr, cols), lambda l, i, chip: (l, chip[0], i, 0))),
        out_shape=jax.ShapeDtypeStruct((2, N_CHIPS, rows, cols), BF),
        compiler_params=_cparams("parallel", "parallel"),
    )(chip_arr, shard)


HBM = pl.BlockSpec(memory_space=pltpu.HBM)
SEM = pl.BlockSpec(memory_space=pltpu.SEMAPHORE)
EFFECT = pltpu.SideEffectType.DATAFLOW_SIDE_EFFECTING

GATHER_GROUPS = (((0, 0),), ((1, 0),), ((2, 0), (3, 0)), ((0, 1),), ((1, 1), (2, 1), (3, 1)))
GATHER_STARTS = ((0,), (1, 2), (3, 4))
GATHER_STAGES = {(0, 0): 0, (1, 0): 1, (2, 0): 2, (0, 1): 3, (1, 1): 4}


def _gather_copies(arrs, group, send_sems, recv_sems):
    x, y, c = _mesh_pos()
    me = 2 * x + y
    out = []
    for i, (w, layer) in enumerate(group):
        mine = arrs[w].at[layer, me]
        for j, (qx, qy) in enumerate([(1 - x, y), (x, 1 - y), (1 - x, 1 - y)]):
            landed = arrs[w].at[layer, 2 * qx + qy]
            out.append(tuple(pltpu.make_async_remote_copy(
                src_ref=piece, dst_ref=piece, send_sem=send_sems.at[i * 3 + j], recv_sem=recv_sems.at[i * 3 + j],
                device_id=(qx, qy, c), device_id_type=MESH) for piece in (mine, landed)))
    return out


def _conv_copies(conv_src, conv_dst, send_sems, recv_sems):
    x, y, c = _mesh_pos()
    out = []
    for j, (qx, qy) in enumerate([(1 - x, y), (x, 1 - y), (1 - x, 1 - y)]):
        out.append(tuple(pltpu.make_async_remote_copy(
            src_ref=conv_src, dst_ref=conv_dst.at[q], send_sem=send_sems.at[j], recv_sem=recv_sems.at[j],
            device_id=(qx, qy, c), device_id_type=MESH) for q in (2 * x + y, 2 * qx + qy)))
    return out


def _gather_start(groups, arrs, conv, name, through=None):
    n_sems = 2 * (len(groups) + (conv is not None))
    mats = sorted({w for g in groups for w, _ in GATHER_GROUPS[g]})

    def body(*refs):
        arrs_ref = [None] * N_BIG
        for w, ref in zip(mats, refs):
            arrs_ref[w] = ref
        sems = refs[n_in:n_in + n_sems]
        if conv is not None:
            for cp, _ in _conv_copies(refs[len(mats)], refs[len(mats) + 1], sems[-2], sems[-1]):
                cp.start()
        for k, g in enumerate(groups):
            for cp, _ in _gather_copies(arrs_ref, GATHER_GROUPS[g], sems[2 * k], sems[2 * k + 1]):
                cp.start()

    sem_shapes = []
    for n in [len(GATHER_GROUPS[g]) for g in groups] + ([1] if conv is not None else []):
        sem_shapes += [pltpu.SemaphoreType.DMA((3 * n,))] * 2
    operands = [arrs[w] for w in mats] + ([] if conv is None else list(conv)) + ([] if through is None else [through])
    n_in = len(operands)
    res = pl.pallas_call(
        body, name=name,
        out_shape=tuple(sem_shapes) + tuple(pltpu.HBM(a.shape, a.dtype) for a in operands),
        in_specs=(HBM,) * n_in, out_specs=(SEM,) * n_sems + (HBM,) * n_in,
        input_output_aliases={i: n_sems + i for i in range(n_in)},
        compiler_params=pltpu.CompilerParams(has_side_effects=EFFECT),
    )(*[pltpu.with_memory_space_constraint(a, pltpu.HBM) for a in operands])
    arrs = list(arrs)
    for w, a in zip(mats, res[n_sems:]):
        arrs[w] = a
    return res[:n_sems], arrs, list(res[n_sems + len(mats):])


def _gather_wait(k, sems, arrs, conv, after, name):
    group = GATHER_GROUPS[k]
    mats = sorted({w for w, _ in group})
    n_conv = 0 if conv is None else 2

    def body(*refs):
        local = refs[:len(mats)]
        arrs_ref = [None] * N_BIG
        for w, ref in zip(mats, local):
            arrs_ref[w] = ref
        pos = len(mats) + n_conv
        copies = _gather_copies(arrs_ref, group, refs[pos], refs[pos + 1])
        if conv is not None:
            copies += _conv_copies(refs[len(mats)], refs[len(mats) + 1], refs[pos + 2], refs[pos + 3])
        for send, recv in copies:
            recv.wait_recv()
            send.wait_send()

    operands = [arrs[w] for w in mats] + ([] if conv is None else [conv[1], conv[2]])
    sem_ops = list(sems) + ([] if conv is None else list(conv[0]))
    n_op = len(operands)
    res = pl.pallas_call(
        body, name=name, out_shape=tuple(pltpu.HBM(a.shape, a.dtype) for a in operands),
        in_specs=(HBM,) * n_op + (SEM,) * len(sem_ops) + (ANY,) * len(after), out_specs=(HBM,) * n_op,
        input_output_aliases={i: i for i in range(n_op)},
        compiler_params=pltpu.CompilerParams(has_side_effects=EFFECT),
    )(*operands, *sem_ops, *after)
    arrs = list(arrs)
    for w, a in zip(mats, res):
        arrs[w] = a
    return arrs, (res[-1] if conv is not None else None)


def _grad_shard(ref, w, chip, n):
    start = pl.multiple_of(chip * n, 128)
    if SHARD_AXES[w] == 2:
        return ref.at[:, pl.ds(start, n)]
    return ref.at[pl.ds(start, n), :]


def _slot_shape(g, w):
    shape = list(g.shape)
    shape[SHARD_AXES[w] - 1] //= N_CHIPS
    return (N_DEV - 1,) + tuple(shape)


def _scatter_copies(g_ref, land_ref, send_sems, recv_sems, layer, w):
    x, y, c = _mesh_pos()
    n = g_ref.shape[SHARD_AXES[w] - 1] // N_CHIPS
    out = []
    for r in range(1, N_DEV):
        tx, ty, tc = _flip(x, r & 4), _flip(y, r & 2), _flip(c, r & 1)
        cp = pltpu.make_async_remote_copy(
            src_ref=_grad_shard(g_ref, w, 2 * tx + ty, n), dst_ref=land_ref.at[r - 1], send_sem=send_sems.at[r - 1],
            recv_sem=recv_sems.at[r - 1], device_id=(tx, ty, tc), device_id_type=MESH)
        out.append((cp, (c != layer) if r & 1 else (c == layer)))
    return out


def _scatter_start(items, layer, name):
    n = len(items)

    def body(*refs):
        for i, (w, _, _) in enumerate(items):
            g_ref, land_ref = refs[2 * i], refs[2 * i + 1]
            send_sems, recv_sems = refs[2 * n + 2 * i], refs[2 * n + 2 * i + 1]
            for cp, mine in _scatter_copies(g_ref, land_ref, send_sems, recv_sems, layer, w):
                @pl.when(mine)
                def _():
                    cp.start()
        refs[-1][...] = jnp.zeros_like(refs[-1])

    operands = [a for _, g, land in items for a in (g, land)]
    res = pl.pallas_call(
        body, name=name,
        out_shape=(pltpu.SemaphoreType.DMA((N_DEV - 1,)),) * (2 * n)
        + tuple(pltpu.HBM(a.shape, a.dtype) for a in operands) + (jax.ShapeDtypeStruct((HALO, 128), F32),),
        in_specs=(HBM,) * (2 * n),
        out_specs=(SEM,) * (2 * n) + (HBM,) * (2 * n) + (pl.BlockSpec(memory_space=pltpu.VMEM),),
        input_output_aliases={i: 2 * n + i for i in range(2 * n)},
        compiler_params=pltpu.CompilerParams(has_side_effects=EFFECT),
    )(*[pltpu.with_memory_space_constraint(a, pltpu.HBM) for a in operands])
    return [(res[2 * i], res[2 * i + 1], res[2 * n + 2 * i], res[2 * n + 2 * i + 1]) for i in range(n)], res[-1]


def _scatter_wait(started, land, after, w, name):
    def body(g0_ref, g1_ref, land_ref, ss0, rs0, ss1, rs1, after_ref, g0_out, g1_out, land_out):
        c = lax.axis_index("c")
        for layer, g_ref, ss, rs in ((0, g0_ref, ss0, rs0), (1, g1_ref, ss1, rs1)):
            for cp, mine in _scatter_copies(g_ref, land_ref, ss, rs, layer, w):
                @pl.when(mine)
                def _():
                    cp.wait_send()

                @pl.when(c == layer)
                def _():
                    cp.wait_recv()

    (ss0, rs0, g0), (ss1, rs1, g1) = started
    return pl.pallas_call(
        body, name=name,
        out_shape=(pltpu.HBM(g0.shape, g0.dtype), pltpu.HBM(g1.shape, g1.dtype), pltpu.HBM(land.shape, land.dtype)),
        in_specs=(HBM, HBM, HBM, SEM, SEM, SEM, SEM, ANY), out_specs=(HBM, HBM, HBM),
        input_output_aliases={0: 0, 1: 1, 2: 2}, compiler_params=pltpu.CompilerParams(has_side_effects=EFFECT),
    )(g0, g1, land, ss0, rs0, ss1, rs1, after)


def _sum_slots(g0, g1, slots, w, pos_arr, name):
    _, rows, cols = slots.shape
    tr = min(512, rows)
    nr = rows // tr
    if SHARD_AXES[w] == 2:
        own = pl.BlockSpec((tr, cols), lambda i, pos: (i, pos[0]))
    else:
        own = pl.BlockSpec((tr, cols), lambda i, pos: (pos[0] * nr + i, 0))

    def body(pos_ref, own0_ref, own1_ref, s_ref, o_ref):
        acc = jnp.where(pos_ref[1] == 0, own0_ref[...], own1_ref[...]).astype(F32)
        for r in range(N_DEV - 1):
            acc = acc + s_ref[r].astype(F32)
        o_ref[...] = acc

    return pl.pallas_call(
        body, name=name,
        grid_spec=pltpu.PrefetchScalarGridSpec(
            num_scalar_prefetch=1, grid=(nr,),
            in_specs=[own, own, pl.BlockSpec((N_DEV - 1, tr, cols), lambda i, pos: (0, i, 0))],
            out_specs=pl.BlockSpec((tr, cols), lambda i, pos: (i, 0))),
        out_shape=jax.ShapeDtypeStruct((rows, cols), F32), compiler_params=_cparams("parallel"),
    )(pos_arr, g0, g1, slots)


def _swap_copies(refs, n):
    x, y, c = _mesh_pos()
    return [pltpu.make_async_remote_copy(src_ref=refs[w], dst_ref=refs[n + w], send_sem=refs[2 * n].at[w],
                                         recv_sem=refs[2 * n + 1].at[w], device_id=(x, y, 1 - c), device_id_type=MESH)
            for w in range(n)]


def _swap_start(halves, name):
    n = len(halves)

    def body(*refs):
        for cp in _swap_copies(refs, n):
            cp.start()

    operands = list(halves) + [lax.empty(h.shape, h.dtype) for h in halves]
    res = pl.pallas_call(
        body, name=name,
        out_shape=(pltpu.SemaphoreType.DMA((n,)),) * 2 + tuple(pltpu.HBM(a.shape, a.dtype) for a in operands),
        in_specs=(HBM,) * (2 * n), out_specs=(SEM,) * 2 + (HBM,) * (2 * n),
        input_output_aliases={i: 2 + i for i in range(2 * n)},
        compiler_params=pltpu.CompilerParams(has_side_effects=EFFECT),
    )(*[pltpu.with_memory_space_constraint(a, pltpu.HBM) for a in operands])
    return res[0], res[1], list(res[2:2 + n]), list(res[2 + n:])


def _swap_wait(send_sems, recv_sems, halves, lands, after, name):
    n = len(halves)

    def body(*refs):
        for cp in _swap_copies(refs, n):
            cp.wait_send()
            cp.wait_recv()

    operands = list(halves) + list(lands)
    res = pl.pallas_call(
        body, name=name, out_shape=tuple(pltpu.HBM(a.shape, a.dtype) for a in operands),
        in_specs=(HBM,) * (2 * n) + (SEM, SEM, ANY), out_specs=(HBM,) * (2 * n),
        input_output_aliases={i: i for i in range(2 * n)},
        compiler_params=pltpu.CompilerParams(has_side_effects=EFFECT),
    )(*operands, send_sems, recv_sems, after)
    return list(res[n:])


def _adamw_math(w, g, m, v):
    m = ADAM_B1 * m + (1.0 - ADAM_B1) * g
    v = ADAM_B2 * v + (1.0 - ADAM_B2) * jnp.square(g)
    m_hat = m / (1.0 - ADAM_B1 ** ADAM_STEP)
    v_hat = v / (1.0 - ADAM_B2 ** ADAM_STEP)
    delta = -ADAM_LR * (m_hat / (jnp.sqrt(v_hat) + ADAM_EPS) + ADAM_WD * w)
    return delta, m, v


def _adamw(w, g, m, v, filled, pos_arr, name):
    shape = w.shape
    _, rows, cols = shape
    tr = min(256, rows)

    def step(w_ref, g_ref, m_ref, v_ref, go_ref, d_ref, m2_ref, v2_ref):
        g = g_ref[...]
        go_ref[...] = g
        d_ref[...], m2_ref[...], v2_ref[...] = _adamw_math(w_ref[...], g, m_ref[...], v_ref[...])

    def body(pos_ref, w_ref, g_ref, m_ref, v_ref, *rest):
        layer = pos_ref[1] if filled is None else 1 - pos_ref[1]
        full_in = pl.BlockSpec((None, tr, cols), lambda i: (layer, i, 0), pipeline_mode=pl.Buffered(3))
        half = pl.BlockSpec((tr, cols), lambda i: (i, 0), pipeline_mode=pl.Buffered(3))
        full = pl.BlockSpec((None, tr, cols), lambda i: (layer, i, 0))
        pltpu.emit_pipeline(step, grid=(rows // tr,), in_specs=[full_in, half, full_in, full_in],
                            out_specs=[full] * 4)(w_ref, g_ref, m_ref, v_ref, *rest[-4:])

    n_in = 5
    return pl.pallas_call(
        body, name=name,
        in_specs=[pl.BlockSpec(memory_space=pltpu.SMEM)] + [ANY] * (4 if filled is None else 8), out_specs=[ANY] * 4,
        out_shape=[jax.ShapeDtypeStruct(shape, F32)] * 4,
        input_output_aliases={} if filled is None else {n_in + k: k for k in range(4)},
        compiler_params=_cparams(),
    )(pos_arr, w, g, m, v, *([] if filled is None else filled))


def _small_sync(part, w, m, v):
    rows, cols = part.shape

    def body(p_ref, w_ref, m_ref, v_ref, g_ref, d_ref, m2_ref, v2_ref, slots, send_sems, recv_sems):
        x, y, c = _mesh_pos()
        me = 4 * x + 2 * y + c
        slots[me] = p_ref[...]
        sends = []
        for r in range(1, N_DEV):
            to = (_flip(x, r & 4), _flip(y, r & 2), _flip(c, r & 1))
            sends.append(pltpu.make_async_remote_copy(
                src_ref=p_ref, dst_ref=slots.at[me], send_sem=send_sems.at[r - 1], recv_sem=recv_sems.at[r - 1],
                device_id=to, device_id_type=MESH))
        for cp in sends:
            cp.start()
        for cp in sends:
            cp.wait_recv()
        for cp in sends:
            cp.wait_send()
        g = slots[0]
        for i in range(1, N_DEV):
            g = g + slots[i]
        g_ref[...] = g
        d_ref[...], m2_ref[...], v2_ref[...] = _adamw_math(w_ref[...], g, m_ref[...], v_ref[...])

    vm = pl.BlockSpec(memory_space=pltpu.VMEM)
    return pl.pallas_call(
        body, name="small_sync", in_specs=[vm] * 4, out_specs=[vm] * 4,
        out_shape=[jax.ShapeDtypeStruct((rows, cols), F32)] * 4,
        scratch_shapes=[pltpu.VMEM((N_DEV, rows, cols), F32), pltpu.SemaphoreType.DMA((N_DEV - 1,)),
                        pltpu.SemaphoreType.DMA((N_DEV - 1,))],
    )(part, w, m, v)


PACK_W = 256


def _pack_rows(n):
    return -(-n // (HALO * PACK_W)) * HALO


def _pack_small(parts):
    out = []
    for a in parts:
        flat = a.reshape(-1)
        out.append(jnp.pad(flat, (0, _pack_rows(flat.size) * PACK_W - flat.size)).reshape(-1, PACK_W))
    return jnp.concatenate(out, axis=0)


def _unpack_small(p, shapes):
    out, row = [], 0
    for shape in shapes:
        n = 1
        for k in shape:
            n *= k
        out.append(p[row:row + _pack_rows(n)].reshape(-1)[:n].reshape(shape))
        row += _pack_rows(n)
    return out


def kernel(x, w_in, conv_w, sinks, g_mix, g_group, w_o, g_mlp, w_ff_in, w_ff_out, g_final, loss_target, m_w_in, m_conv_w, m_sinks, m_g_mix, m_g_group, m_w_o, m_g_mlp, m_w_ff_in, m_w_ff_out, m_g_final, v_w_in, v_conv_w, v_sinks, v_g_mix, v_g_group, v_w_o, v_g_mlp, v_w_ff_in, v_w_ff_out, v_g_final):
    chip = 2 * lax.axis_index("x") + lax.axis_index("y")
    conv_n = conv_w.shape[2]

    pos_arr = jnp.stack([chip, lax.axis_index("c")]).astype(jnp.int32)
    shards = (w_in, w_o, w_ff_in, w_ff_out)
    conv_tile = jnp.pad(conv_w.reshape(6, conv_n), ((0, HALO - 6), (0, 128 - conv_n)))
    placed = [_place_shard(w_in, pos_arr[:1], "place_shard_0"), None, None, None]
    sems_a, placed, conv_thru = _gather_start(
        GATHER_STARTS[0], placed, (conv_tile, lax.empty((N_CHIPS,) + conv_tile.shape, conv_tile.dtype)),
        "gather_start_0")
    for i in range(1, N_BIG):
        placed[i] = _place_shard(shards[i], pos_arr[:1], "place_shard_%d" % i)
    full = {"arrs": placed, "conv": None, "sems": list(sems_a[:2])}
    target = _to_strips(loss_target[0], placed[:1], "to_strips_target")

    def fetch(stage, layer, after):
        k = GATHER_STAGES.get((stage, layer))
        if k is None:
            return (*full["arrs"], full["conv"])
        sems = full["sems"][2 * k:2 * k + 2]
        if k == 0:
            full["arrs"], land = _gather_wait(0, sems, full["arrs"], (sems_a[-2:], *conv_thru), (after, target),
                                              "gather_wait_0")
            conv_all = lax.dynamic_update_slice(land, conv_tile[None], (chip, 0, 0))
            full["conv"] = conv_all[:, :6, :conv_n].reshape(N_CHIPS, 2, 3, conv_n).transpose(1, 2, 0, 3).reshape(
                2, 3, CONV_CH)
            sems_b, full["arrs"], rest = _gather_start(GATHER_STARTS[1], full["arrs"], None, "gather_start_1",
                                                       through=full["arrs"][0])
            full["arrs"][0] = rest[-1]
            full["sems"] += list(sems_b)
        else:
            full["arrs"], _ = _gather_wait(k, sems, full["arrs"], None, (after,), "gather_wait_%d" % k)
        if k == 2:
            sems_c, full["arrs"], _ = _gather_start(GATHER_STARTS[2], full["arrs"], None, "gather_start_2")
            full["sems"] += list(sems_c)
        return (*full["arrs"], full["conv"])

    lands, started, pending = [None] * N_BIG, {}, []

    def emit(layer, w, g):
        if lands[w] is None:
            lands[w] = lax.empty(_slot_shape(g, w), g.dtype)
        pending.append((w, g, lands[w]))
        if not (w == 0 or (layer == 0 and w == 1)):
            return jnp.zeros((), F32)
        name = "scatter_start_%d_%d" % (layer, len(pending))
        done, token = _scatter_start(list(pending), layer, name)
        for (w_i, _, _), (ss, rs, g_thru, land) in zip(pending, done):
            started[layer, w_i], lands[w_i] = (ss, rs, g_thru), land
        pending.clear()
        return token[0, 0]

    loss_tile, dx, grads, dg_final = _local_step(_to_strips(x[0], placed, "to_strips_x"), target, fetch,
                                                 w_ff_in.shape[2] * N_CHIPS,
                                                 sinks, g_mix, g_group, g_mlp, g_final, emit)

    wmv = ((w_in, m_w_in, v_w_in), (w_o, m_w_o, v_w_o), (w_ff_in, m_w_ff_in, v_w_ff_in),
           (w_ff_out, m_w_ff_out, v_w_ff_out))
    big, after = [None] * N_BIG, dx
    for name, ws in (("swap_rest", (1, 2, 3)), ("swap_in", (0,))):
        own = []
        for w in ws:
            g0, g1, slots = _scatter_wait((started[0, w], started[1, w]), lands[w], after, w, "scatter_wait_%d" % w)
            own.append(_sum_slots(g0, g1, slots, w, pos_arr, "sum_slots_%d" % w))
        send_sems, recv_sems, own, zones = _swap_start(own, name + "_start")
        for w, g in zip(ws, own):
            big[w] = _adamw(wmv[w][0], g, wmv[w][1], wmv[w][2], None, pos_arr, "adamw_own_%d" % w)
        theirs = _swap_wait(send_sems, recv_sems, own, zones, big[ws[-1]][1], name + "_wait")
        for w, g in zip(ws, theirs):
            big[w] = _adamw(wmv[w][0], g, wmv[w][1], wmv[w][2], big[w], pos_arr, "adamw_other_%d" % w)
        after = big[ws[-1]][1]

    def both(i):
        return jnp.stack([grads[0][i][0], grads[1][i][0]])
    dconv = jnp.stack([grads[0][0][:3], grads[1][0][:3]])
    dsinks = jnp.stack([grads[0][1][0, ::HEAD_DIM], grads[1][1][0, ::HEAD_DIM]])
    part = _pack_small([both(2), both(3), both(4), dg_final[0], dconv, dsinks, loss_tile[0, 0]])

    def spread(shard):
        return lax.dynamic_update_slice(jnp.zeros((2, 3, CONV_CH), F32), shard, (0, 0, chip * conv_n))
    zero = jnp.zeros((), F32)
    packs = [_pack_small([a, b, c_, e, spread(f), g_, zero]) for a, b, c_, e, f, g_ in (
        (g_mix, g_group, g_mlp, g_final, conv_w, sinks),
        (m_g_mix, m_g_group, m_g_mlp, m_g_final, m_conv_w, m_sinks),
        (v_g_mix, v_g_group, v_g_mlp, v_g_final, v_conv_w, v_sinks))]
    shapes = [g_mix.shape, g_group.shape, g_mlp.shape, g_final.shape, (2, 3, CONV_CH), sinks.shape, ()]
    small = [_unpack_small(p, shapes) for p in _small_sync(part, *packs)]

    def shard_of(full):
        return lax.dynamic_slice(full, (0, 0, chip * conv_n), (2, 3, conv_n))
    small = [(s[0], s[1], s[2], s[3], shard_of(s[4]), s[5], s[6]) for s in small]
    loss = small[0][6]

    def ordered(kind):
        b = [big[i][kind] for i in range(N_BIG)]
        s = small[kind]
        return [b[0], s[4], s[5], s[0], s[1], b[1], s[2], b[2], b[3], s[3]]

    return (loss, dx[None], *ordered(0), *ordered(1), *ordered(2), *ordered(3))
```

```python
import functools

import jax
import jax.numpy as jnp
from jax import lax
from jax.experimental import pallas as pl
from jax.experimental.pallas import tpu as pltpu

HEAD_DIM = 64
N_HEADS = 6
C_GROUP = 3
A_WIDTH = N_HEADS * HEAD_DIM
C_KV_WIDTH = 2 * HEAD_DIM
CONV_CH = 256
ZA_W = 3 * A_WIDTH
ZB_W = 3 * CONV_CH
ZC_W = A_WIDTH + 2 * C_KV_WIDTH
IN_WIDTH = ZA_W + ZB_W + ZC_W
MIX_WIDTH = A_WIDTH + CONV_CH + A_WIDTH
DILATIONS = (1, 4, 16)
A_MAX_DIST = 128
C_MAX_DIST = 127
TQ = 128
EPS = 1e-6
SCALE = HEAD_DIM ** -0.5
NEG = -1e30
HALO = 8

ADAM_LR = 0.001
ADAM_B1 = 0.9
ADAM_B2 = 0.999
ADAM_EPS = 1e-08
ADAM_WD = 0.01
ADAM_STEP = 10

BF = jnp.bfloat16
F32 = jnp.float32
MESH = pl.DeviceIdType.MESH
VMEM_LIMIT = 56 * 1024 * 1024


def _cparams(*sem):
    return pltpu.CompilerParams(dimension_semantics=sem, vmem_limit_bytes=VMEM_LIMIT)


def _nt(a, b):
    return lax.dot_general(a, b, (((1,), (1,)), ((), ())), preferred_element_type=F32)


def _tn(a, b):
    return lax.dot_general(a, b, (((0,), (0,)), ((), ())), preferred_element_type=F32)


def _nn(a, b):
    return jnp.dot(a, b, preferred_element_type=F32)


def _rows(tb, w):
    return pl.BlockSpec((tb, w), lambda i: (i, 0))


def _whole(shape):
    return pl.BlockSpec(shape, lambda *_: (0,) * len(shape))


def _layer(shape, l):
    return pl.BlockSpec((None,) + shape, lambda *_: (l,) + (0,) * len(shape))


def _rms_scale(v):
    return lax.rsqrt(jnp.mean(v * v, axis=-1, keepdims=True) + EPS)


def _norm_bwd(dxhat, xhat, r):
    return r * (dxhat - xhat * jnp.mean(dxhat * xhat, axis=-1, keepdims=True))


def _qkv_fwd(x, g, w_all, l, tb):
    s, d = x.shape

    def body(x_ref, g_ref, w_ref, h_ref, za_ref, zb_ref, zc_ref):
        xv = x_ref[...]
        h = ((xv * _rms_scale(xv)) * g_ref[...]).astype(BF)
        h_ref[...] = h
        z = jnp.concatenate([_nn(h, w_ref[k]) for k in range(N_CHIPS)], axis=1)
        za_ref[...] = z[:, :ZA_W]
        zb_ref[...] = z[:, ZA_W:ZA_W + ZB_W]
        zc_ref[...] = z[:, ZA_W + ZB_W:]

    return pl.pallas_call(
        body, grid=(s // tb,), name="qkv_fwd",
        in_specs=[_rows(tb, d), _whole((1, d)), _layer((N_CHIPS, d, IN_WIDTH // N_CHIPS), l)],
        out_specs=[_rows(tb, d), _rows(tb, ZA_W), _rows(tb, ZB_W), _rows(tb, ZC_W)],
        out_shape=[jax.ShapeDtypeStruct((s, d), BF), jax.ShapeDtypeStruct((s, ZA_W), F32),
                   jax.ShapeDtypeStruct((s, ZB_W), F32), jax.ShapeDtypeStruct((s, ZC_W), F32)],
        compiler_params=_cparams("parallel"),
    )(x, g, w_all)


N_STRIPS = 16


def _strips(a):
    s, w = a.shape
    return a.reshape(4, 4, s // N_STRIPS, w)


P_ROWS = {16: TQ, 4: 32, 1: 8}


def _p_sub(s, dil, most):
    while (s // dil // TQ) % most:
        most //= 2
    return most


def _p_grid(s, dil, n_sub):
    nb = s // dil // TQ // n_sub
    return {16: (4, 4, nb), 4: (4, nb), 1: (nb,)}[dil]


def _p_spec(dil, cw, col, n_sub, prev=False):
    rows = P_ROWS[dil] * (1 if prev else n_sub)

    def blk(j):
        return jnp.maximum(n_sub * j - 1, 0) if prev else j
    if dil == 16:
        return pl.BlockSpec((None, None, rows, cw), lambda f, e, j: (f, e, blk(j), col))
    if dil == 4:
        return pl.BlockSpec((None, 4, rows, cw), lambda f, j: (f, 0, blk(j), col))
    return pl.BlockSpec((4, 4, rows, cw), lambda j: (0, 0, blk(j), col))


def _block_pos(i, dil):
    if dil == 16:
        return i
    if dil == 4:
        return 4 * (i % 32) + i // 32
    return 16 * (i % 8) + 4 * ((i // 8) % 4) + i // 32


def _band_mask(b, dil, max_dist):
    qi = _block_pos(lax.broadcasted_iota(jnp.int32, (TQ, 2 * TQ), 0), dil)
    col = lax.broadcasted_iota(jnp.int32, (TQ, 2 * TQ), 1)
    cur = col >= TQ
    dist = qi - _block_pos(col % TQ, dil) + jnp.where(cur, 0, TQ)
    return (dist >= 0) & (dist <= max_dist) & (cur | (b > 0))


def _hs(h):
    return slice(h * HEAD_DIM, (h + 1) * HEAD_DIM)


def _ld(ref, cols, rows=slice(None)):
    v = ref[..., rows, cols]
    return v.reshape(TQ, v.shape[-1])


def _st(ref, cols, val, rows=slice(None)):
    lead = ref.shape[:-2] + (ref.shape[-2] if rows == slice(None) else rows.stop - rows.start,)
    ref[..., rows, cols] = val.reshape(lead + (val.shape[-1],))


def _attn_fwd(z, dil, kw, kcol, vcol, n_rep, max_dist, name):
    s, zw = z.shape
    n_sub = _p_sub(s, dil, 2)
    grid = _p_grid(s, dil, n_sub)
    o_dt = F32 if P_ROWS[dil] * n_sub < 16 else BF

    def body(q_ref, kp_ref, kc_ref, vp_ref, vc_ref, o_ref, lse_ref):
        for t in range(n_sub):
            rows = slice(t * P_ROWS[dil], (t + 1) * P_ROWS[dil])
            before = (slice(None),) if t == 0 else (slice((t - 1) * P_ROWS[dil], t * P_ROWS[dil]),)
            kb_ref, vb_ref = (kp_ref, vp_ref) if t == 0 else (kc_ref, vc_ref)
            mask = _band_mask(n_sub * pl.program_id(len(grid) - 1) if t == 0 else 1, dil, max_dist)
            scs, v2s = [], []
            for kh in range(N_HEADS // n_rep):
                k2 = jnp.concatenate([_ld(kb_ref, _hs(kh), *before), _ld(kc_ref, _hs(kh), rows)], axis=0).astype(BF)
                v2s.append(jnp.concatenate([_ld(vb_ref, _hs(kh), *before), _ld(vc_ref, _hs(kh), rows)],
                                           axis=0).astype(BF))
                for h in range(kh * n_rep, (kh + 1) * n_rep):
                    q = (_ld(q_ref, _hs(h), rows) * SCALE).astype(BF)
                    scs.append(jnp.where(mask, _nt(q, k2), NEG))
            for h, sc in enumerate(scs):
                m = jnp.max(sc, axis=1, keepdims=True)
                p = jnp.exp(sc - m)
                l = jnp.sum(p, axis=1, keepdims=True)
                _st(o_ref, _hs(h), (_nn(p.astype(BF), v2s[h // n_rep]) / l).astype(o_ref.dtype), rows)
                _st(lse_ref, _hs(h), jnp.broadcast_to(m + jnp.log(l), (TQ, HEAD_DIM)), rows)

    res = pl.pallas_call(
        body, grid=grid, name=name,
        in_specs=[_p_spec(dil, A_WIDTH, 0, n_sub), _p_spec(dil, kw, kcol, n_sub, True), _p_spec(dil, kw, kcol, n_sub),
                  _p_spec(dil, kw, vcol, n_sub, True), _p_spec(dil, kw, vcol, n_sub)],
        out_specs=[_p_spec(dil, A_WIDTH, 0, n_sub)] * 2,
        out_shape=[jax.ShapeDtypeStruct((4, 4, s // N_STRIPS, A_WIDTH), dt) for dt in (o_dt, F32)],
        compiler_params=_cparams(*(("parallel",) * len(grid))),
    )(*[_strips(z)] * 5)
    return [a.reshape(s, A_WIDTH) for a in res]


def _attn_merge(parts_a, part_c, sink_row, tb):
    s = part_c[0].shape[0]
    n_a = len(parts_a)

    def body(*refs):
        ins, sink_ref = refs[:2 * n_a + 2], refs[2 * n_a + 2]
        ya_ref, lsea_ref, yc_ref, lsec_ref = refs[2 * n_a + 3:]
        lses = [ins[2 * p + 1][...] for p in range(n_a)]
        m = functools.reduce(jnp.maximum, lses)
        ws = [jnp.exp(v - m) for v in lses]
        l = functools.reduce(jnp.add, ws)
        ya_ref[...] = functools.reduce(jnp.add, [w * ins[2 * p][...].astype(F32) for p, w in enumerate(ws)]) / l
        lsea_ref[...] = m + jnp.log(l)
        o_c, lse_c = [r[...].astype(F32) for r in ins[2 * n_a:]]
        sk = sink_ref[...]
        m2 = jnp.maximum(lse_c, sk)
        w = jnp.exp(lse_c - m2)
        l2 = w + jnp.exp(sk - m2)
        yc_ref[...] = o_c * (w / l2)
        lsec_ref[...] = m2 + jnp.log(l2)

    return pl.pallas_call(
        body, grid=(s // tb,), name="attn_merge",
        in_specs=[_rows(tb, A_WIDTH)] * (2 * n_a + 2) + [_whole((1, A_WIDTH))],
        out_specs=[_rows(tb, A_WIDTH)] * 4, out_shape=[jax.ShapeDtypeStruct((s, A_WIDTH), F32)] * 4,
        compiler_params=_cparams("parallel"),
    )(*[a for part in parts_a + [part_c] for a in part], sink_row)


def _shift_down(v, n, halo):
    rows = v.shape[0]
    out = pltpu.roll(v, n, 0)
    row = lax.broadcasted_iota(jnp.int32, v.shape, 0)
    for t in range(n):
        out = jnp.where(row == t, halo[HALO - n + t:HALO - n + t + 1, :], out)
    return out


def _shift_up(v, n, halo):
    rows = v.shape[0]
    out = pltpu.roll(v, rows - n, 0)
    row = lax.broadcasted_iota(jnp.int32, v.shape, 0)
    for t in range(n):
        out = jnp.where(row == rows - n + t, halo[t:t + 1, :], out)
    return out


def _strip(v, b):
    return v[b % 4, b // 4]


def _conv_strips(zb, prev, cw):
    gb = [_strip(zb, b)[:, :CONV_CH] for b in range(N_STRIPS)]
    gc = [_strip(zb, b)[:, CONV_CH:2 * CONV_CH] for b in range(N_STRIPS)]
    xb = [_strip(zb, b)[:, 2 * CONV_CH:] for b in range(N_STRIPS)]
    u = [g * v for g, v in zip(gc, xb)]
    uh = prev[:, :, CONV_CH:2 * CONV_CH] * prev[:, :, 2 * CONV_CH:]
    wrapped = {14: _shift_down(u[14], 1, uh[2]), 15: _shift_down(u[15], 1, uh[3])}
    u1 = [u[b - 1] if b >= 1 else wrapped[15] for b in range(N_STRIPS)]
    u2 = [u[b - 2] if b >= 2 else wrapped[14 + b] for b in range(N_STRIPS)]
    c = [cw[0:1, :] * u2[b] + cw[1:2, :] * u1[b] + cw[2:3, :] * u[b] for b in range(N_STRIPS)]
    return gb, gc, xb, u, u1, u2, c


def _strip_rows(ta, w):
    return pl.BlockSpec((4, 4, ta, w), lambda i: (0, 0, i, 0))


def _prev_rows(ta, w):
    return pl.BlockSpec((4, None, HALO, w), lambda i: (0, 3, jnp.maximum(i * (ta // HALO) - 1, 0), 0))


def _next_rows(ta, w, nblk):
    return pl.BlockSpec((4, None, HALO, w),
                        lambda i: (0, 0, jnp.minimum((i + 1) * (ta // HALO), nblk * (ta // HALO) - 1), 0))


def _mix_fwd(x, ya, yc, zb, cw, gg, wo_all, l, tb):
    s, d = x.shape
    ta = tb // N_STRIPS

    def body(x_ref, ya_ref, yc_ref, zb_ref, zbp_ref, cw_ref, gg_ref, wo_ref, x1_ref, yb_ref):
        i = pl.program_id(0)
        prev = jnp.where(i > 0, zbp_ref[...], 0.0)
        gb, _, _, _, _, _, c = _conv_strips(zb_ref[...], prev, cw_ref[...])
        for b in range(N_STRIPS):
            yb_ref[b % 4, b // 4] = gb[b] * c[b]
        yb = yb_ref[...].reshape(tb, CONV_CH)
        ya, yc = ya_ref[...].reshape(tb, A_WIDTH), yc_ref[...].reshape(tb, A_WIDTH)
        n = jnp.concatenate([ya * _rms_scale(ya), yb * _rms_scale(yb), yc * _rms_scale(yc)], axis=1)
        n = (n * gg_ref[...]).astype(BF)
        x1 = x_ref[...].reshape(tb, d) + _nn(n, wo_ref[...].reshape(MIX_WIDTH, d))
        x1_ref[...] = x1.reshape(4, 4, ta, d)

    res = pl.pallas_call(
        body, grid=(s // tb,), name="mix_fwd",
        in_specs=[_strip_rows(ta, d), _strip_rows(ta, A_WIDTH), _strip_rows(ta, A_WIDTH), _strip_rows(ta, ZB_W),
                  _prev_rows(ta, ZB_W), _whole((HALO, CONV_CH)), _whole((1, MIX_WIDTH)),
                  _layer((N_CHIPS, MIX_WIDTH // N_CHIPS, d), l)],
        out_specs=[_strip_rows(ta, d), _strip_rows(ta, CONV_CH)],
        out_shape=[jax.ShapeDtypeStruct((4, 4, s // N_STRIPS, d), F32),
                   jax.ShapeDtypeStruct((4, 4, s // N_STRIPS, CONV_CH), F32)],
        compiler_params=_cparams("parallel"),
    )(_strips(x), _strips(ya), _strips(yc), _strips(zb), _strips(zb), cw, gg, wo_all)
    return res[0].reshape(s, d), res[1].reshape(s, CONV_CH)


def _mlp_fwd(x1, g, w1_all, w2_all, l, tb, tf):
    s, d = x1.shape
    ff = w1_all.shape[1] * w1_all.shape[3]
    nj = ff // tf

    def body(x_ref, g_ref, w1_ref, w2_ref, x2_ref, h2_ref, ap_ref, acc):
        j = pl.program_id(1)

        @pl.when(j == 0)
        def _():
            xv = x_ref[...]
            h2_ref[...] = ((xv * _rms_scale(xv)) * g_ref[...]).astype(BF)
            acc[...] = jnp.zeros_like(acc)

        ap = _nn(h2_ref[...], w1_ref[...])
        ap_ref[...] = ap.astype(BF)
        a = jnp.square(jnp.maximum(ap, 0.0)).astype(BF)
        acc[...] += _nn(a, w2_ref[...])

        @pl.when(j == nj - 1)
        def _():
            x2_ref[...] = x_ref[...] + acc[...]

    return pl.pallas_call(
        body, grid=(s // tb, nj), name="mlp_fwd",
        in_specs=[pl.BlockSpec((tb, d), lambda i, j: (i, 0)), _whole((1, d)),
                  pl.BlockSpec((None, None, d, tf), lambda i, j: (l, j, 0, 0)),
                  pl.BlockSpec((None, None, tf, d), lambda i, j: (l, j, 0, 0))],
        out_specs=[pl.BlockSpec((tb, d), lambda i, j: (i, 0)), pl.BlockSpec((tb, d), lambda i, j: (i, 0)),
                   pl.BlockSpec((tb, tf), lambda i, j: (i, j))],
        out_shape=[jax.ShapeDtypeStruct((s, d), F32), jax.ShapeDtypeStruct((s, d), BF),
                   jax.ShapeDtypeStruct((s, ff), BF)],
        scratch_shapes=[pltpu.VMEM((tb, d), F32)],
        compiler_params=_cparams("parallel", "arbitrary"),
    )(x1, g, w1_all, w2_all)


def _loss_head(x, g, tgt, tb):
    s, d = x.shape

    def body(x_ref, g_ref, t_ref, dx_ref, loss_ref, dg_ref):
        i = pl.program_id(0)

        @pl.when(i == 0)
        def _():
            loss_ref[...] = jnp.zeros_like(loss_ref)
            dg_ref[...] = jnp.zeros_like(dg_ref)

        xv = x_ref[...]
        r = _rms_scale(xv)
        xhat = xv * r
        err = xhat * g_ref[...] - t_ref[...]
        part = jnp.sum(jnp.mean(jnp.square(err), axis=-1, keepdims=True), axis=0, keepdims=True)
        loss_ref[...] += 0.5 * part
        dy = err * (1.0 / d)
        dg_ref[...] += jnp.sum(dy * xhat, axis=0, keepdims=True)
        dx_ref[...] = _norm_bwd(dy * g_ref[...], xhat, r)

    return pl.pallas_call(
        body, grid=(s // tb,), name="loss_head",
        in_specs=[_rows(tb, d), _whole((1, d)), _rows(tb, d)],
        out_specs=[_rows(tb, d), _whole((HALO, 128)), _whole((HALO, d))],
        out_shape=[jax.ShapeDtypeStruct((s, d), F32), jax.ShapeDtypeStruct((HALO, 128), F32),
                   jax.ShapeDtypeStruct((HALO, d), F32)],
        compiler_params=_cparams("arbitrary"),
    )(x, g, tgt)


def _mlp_bwd(dx2, x1, ap, g, w1_all, w2_all, l, tb, tf):
    s, d = x1.shape
    ff = ap.shape[1]
    nj = ff // tf

    def body(dx2_ref, x1_ref, ap_ref, g_ref, w1_ref, w2_ref, dx1_ref, dap_ref, dg_ref, acc):
        i, j = pl.program_id(0), pl.program_id(1)

        @pl.when((i == 0) & (j == 0))
        def _():
            dg_ref[...] = jnp.zeros_like(dg_ref)

        @pl.when(j == 0)
        def _():
            acc[...] = jnp.zeros_like(acc)

        da = _nt(dx2_ref[...].astype(BF), w2_ref[...])
        dap = (da * (2.0 * jnp.maximum(ap_ref[...].astype(F32), 0.0))).astype(BF)
        dap_ref[...] = dap
        acc[...] += _nt(dap, w1_ref[...])

        @pl.when(j == nj - 1)
        def _():
            xv = x1_ref[...]
            r = _rms_scale(xv)
            xhat = xv * r
            dh = acc[...]
            dg_ref[...] += jnp.sum(dh * xhat, axis=0, keepdims=True)
            dx1_ref[...] = dx2_ref[...] + _norm_bwd(dh * g_ref[...], xhat, r)

    return pl.pallas_call(
        body, grid=(s // tb, nj), name="mlp_bwd",
        in_specs=[pl.BlockSpec((tb, d), lambda i, j: (i, 0)), pl.BlockSpec((tb, d), lambda i, j: (i, 0)),
                  pl.BlockSpec((tb, tf), lambda i, j: (i, j)),
                  _whole((1, d)), pl.BlockSpec((None, None, d, tf), lambda i, j: (l, j, 0, 0)),
                  pl.BlockSpec((None, None, tf, d), lambda i, j: (l, j, 0, 0))],
        out_specs=[pl.BlockSpec((tb, d), lambda i, j: (i, 0)), pl.BlockSpec((tb, tf), lambda i, j: (i, j)),
                   _whole((HALO, d))],
        out_shape=[jax.ShapeDtypeStruct((s, d), F32), jax.ShapeDtypeStruct((s, ff), BF),
                   jax.ShapeDtypeStruct((HALO, d), F32)],
        scratch_shapes=[pltpu.VMEM((tb, d), F32)],
        compiler_params=_cparams("arbitrary", "arbitrary"),
    )(dx2, x1, ap, g, w1_all, w2_all)


def _wgrad(a, b, tm, tn, ts, name, relu2=False):
    s, m = a.shape
    n = b.shape[1]
    ns = s // ts

    def body(a_ref, b_ref, o_ref, acc):
        k = pl.program_id(2)

        @pl.when(k == 0)
        def _():
            acc[...] = jnp.zeros_like(acc)

        av = a_ref[...]
        if relu2:
            av = jnp.square(jnp.maximum(av.astype(F32), 0.0)).astype(BF)
        acc[...] += _tn(av, b_ref[...].astype(BF))

        @pl.when(k == ns - 1)
        def _():
            o_ref[...] = acc[...].astype(BF)

    return pl.pallas_call(
        body, grid=(m // tm, n // tn, ns), name=name,
        in_specs=[pl.BlockSpec((ts, tm), lambda i, j, k: (k, i)), pl.BlockSpec((ts, tn), lambda i, j, k: (k, j))],
        out_specs=pl.BlockSpec((tm, tn), lambda i, j, k: (i, j)),
        out_shape=jax.ShapeDtypeStruct((m, n), BF),
        scratch_shapes=[pltpu.VMEM((tm, tn), F32)],
        compiler_params=_cparams("parallel", "parallel", "arbitrary"),
    )(a, b)


def _mix_bwd(dx1, ya, yb, yc, lse_c, sink_row, gg, wo_all, l, tb):
    s, d = dx1.shape

    def body(dx_ref, ya_ref, yb_ref, yc_ref, lse_ref, sink_ref, gg_ref, wo_ref,
             n_ref, dya_ref, dyc_ref, da_ref, dc_ref, dyb_ref, dg_ref, dsink_ref):
        i = pl.program_id(0)

        @pl.when(i == 0)
        def _():
            dg_ref[...] = jnp.zeros_like(dg_ref)
            dsink_ref[...] = jnp.zeros_like(dsink_ref)

        dn = _nt(dx_ref[...].astype(BF), wo_ref[...].reshape(MIX_WIDTH, d))
        ys = [ya_ref[...], yb_ref[...], yc_ref[...]]
        rs = [_rms_scale(v) for v in ys]
        nhat = jnp.concatenate([v * r for v, r in zip(ys, rs)], axis=1)
        gg = gg_ref[...]
        n_ref[...] = (nhat * gg).astype(BF)
        dg_ref[...] += jnp.sum(dn * nhat, axis=0, keepdims=True)
        dnh = dn * gg
        bounds = [(0, A_WIDTH), (A_WIDTH, A_WIDTH + CONV_CH), (A_WIDTH + CONV_CH, MIX_WIDTH)]
        dys = [_norm_bwd(dnh[:, lo:hi], nhat[:, lo:hi], r) for (lo, hi), r in zip(bounds, rs)]
        dyb_ref[...] = dys[1]
        head = [lax.broadcasted_iota(jnp.int32, (A_WIDTH, A_WIDTH), k) // HEAD_DIM for k in (0, 1)]
        ones = (head[0] == head[1]).astype(BF)
        for dy, y, dy_ref, dd_ref in ((dys[0], ys[0], dya_ref, da_ref), (dys[2], ys[2], dyc_ref, dc_ref)):
            dy_ref[...] = dy
            t = dy * y
            hi = t.astype(BF)
            dd_ref[...] = _nn(hi, ones) + _nn((t - hi.astype(F32)).astype(BF), ones)
        dsink_ref[...] -= jnp.sum(jnp.exp(sink_ref[...] - lse_ref[...]) * dc_ref[...], axis=0, keepdims=True)

    return pl.pallas_call(
        body, grid=(s // tb,), name="mix_bwd",
        in_specs=[_rows(tb, d), _rows(tb, A_WIDTH), _rows(tb, CONV_CH), _rows(tb, A_WIDTH), _rows(tb, A_WIDTH),
                  _whole((1, A_WIDTH)), _whole((1, MIX_WIDTH)), _layer((N_CHIPS, MIX_WIDTH // N_CHIPS, d), l)],
        out_specs=[_rows(tb, MIX_WIDTH), _rows(tb, A_WIDTH), _rows(tb, A_WIDTH), _rows(tb, A_WIDTH),
                   _rows(tb, A_WIDTH), _rows(tb, CONV_CH), _whole((HALO, MIX_WIDTH)), _whole((HALO, A_WIDTH))],
        out_shape=[jax.ShapeDtypeStruct((s, MIX_WIDTH), BF), jax.ShapeDtypeStruct((s, A_WIDTH), F32),
                   jax.ShapeDtypeStruct((s, A_WIDTH), F32), jax.ShapeDtypeStruct((s, A_WIDTH), F32),
                   jax.ShapeDtypeStruct((s, A_WIDTH), F32), jax.ShapeDtypeStruct((s, CONV_CH), F32),
                   jax.ShapeDtypeStruct((HALO, MIX_WIDTH), F32), jax.ShapeDtypeStruct((HALO, A_WIDTH), F32)],
        compiler_params=_cparams("arbitrary"),
    )(dx1, ya, yb, yc, lse_c, sink_row, gg, wo_all)


def _attn_bwd(z, dy, lse, dd, dil, kw, kcol, vcol, n_rep, max_dist, name):
    s, zw = z.shape
    n_sub = _p_sub(s, dil, 2) if n_rep == 1 else 1
    grid = _p_grid(s, dil, n_sub)
    n_kv = N_HEADS // n_rep
    dt = F32 if P_ROWS[dil] * n_sub < 16 else BF

    def body(q_ref, kp_ref, kc_ref, vp_ref, vc_ref, dy_ref, lse_ref, dd_ref, dq_ref, dkp_ref, dkc_ref, dvp_ref, dvc_ref):
        for t in range(n_sub):
            rows = slice(t * P_ROWS[dil], (t + 1) * P_ROWS[dil])
            before = (slice(None),) if t == 0 else (slice((t - 1) * P_ROWS[dil], t * P_ROWS[dil]),)
            kb_ref, vb_ref = (kp_ref, vp_ref) if t == 0 else (kc_ref, vc_ref)
            mask = _band_mask(n_sub * pl.program_id(len(grid) - 1) if t == 0 else 1, dil, max_dist)
            k2s, qs, dys, scs, dps = [], [], [], [], []
            for kh in range(n_kv):
                k2s.append(jnp.concatenate([_ld(kb_ref, _hs(kh), *before), _ld(kc_ref, _hs(kh), rows)],
                                           axis=0).astype(BF))
                v2 = jnp.concatenate([_ld(vb_ref, _hs(kh), *before), _ld(vc_ref, _hs(kh), rows)], axis=0).astype(BF)
                for h in range(kh * n_rep, (kh + 1) * n_rep):
                    qs.append((_ld(q_ref, _hs(h), rows) * SCALE).astype(BF))
                    dys.append(_ld(dy_ref, _hs(h), rows).astype(BF))
                    scs.append(jnp.where(mask, _nt(qs[h], k2s[kh]), NEG))
                    dps.append(_nt(dys[h], v2))
            for kh in range(n_kv):
                k2 = k2s[kh]
                dk2 = jnp.zeros((2 * TQ, HEAD_DIM), F32)
                dv2 = jnp.zeros((2 * TQ, HEAD_DIM), F32)
                for h in range(kh * n_rep, (kh + 1) * n_rep):
                    lse_h = _ld(lse_ref, slice(h * HEAD_DIM, h * HEAD_DIM + 1), rows)
                    dd_h = _ld(dd_ref, slice(h * HEAD_DIM, h * HEAD_DIM + 1), rows)
                    p = jnp.exp(scs[h] - lse_h)
                    ds = (p * (dps[h] - dd_h)).astype(BF)
                    _st(dq_ref, _hs(h), (_nn(ds, k2) * SCALE).astype(dt), rows)
                    dk2 = dk2 + _tn(ds, qs[h])
                    dv2 = dv2 + _tn(p.astype(BF), dys[h])
                _st(dkp_ref, _hs(kh), dk2[:TQ].astype(dt), rows)
                _st(dkc_ref, _hs(kh), dk2[TQ:].astype(dt), rows)
                _st(dvp_ref, _hs(kh), dv2[:TQ].astype(dt), rows)
                _st(dvc_ref, _hs(kh), dv2[TQ:].astype(dt), rows)

    args = [_strips(z)] * 5 + [_strips(a) for a in (dy, lse, dd)]
    pair = _p_spec(dil, A_WIDTH, 0, n_sub)
    in_specs = [pair, _p_spec(dil, kw, kcol, n_sub, True), _p_spec(dil, kw, kcol, n_sub),
                _p_spec(dil, kw, vcol, n_sub, True), _p_spec(dil, kw, vcol, n_sub)] + [pair] * 3
    out_specs = [pair] + [_p_spec(dil, kw, 0, n_sub)] * 4
    na = s // N_STRIPS
    out_shape = [jax.ShapeDtypeStruct((4, 4, na, A_WIDTH), dt)] + [jax.ShapeDtypeStruct((4, 4, na, kw), dt)] * 4
    res = pl.pallas_call(
        body, grid=grid, name=name, in_specs=in_specs, out_specs=out_specs, out_shape=out_shape,
        compiler_params=_cparams(*(("parallel",) * len(grid))),
    )(*args)
    return [res[0].reshape(s, A_WIDTH)] + [a.reshape(s, kw) for a in res[1:]]


DZ_TA = 16


def _dz_assemble(parts_a, parts_c, dyb, zb, cw):
    s = zb.shape[0]
    na = s // N_STRIPS
    nb = na // DZ_TA

    def ahead(w, k):
        return pl.BlockSpec((4, 4, DZ_TA, w), lambda i: (0, 0, jnp.minimum(i + k, nb - 1), 0))

    args, in_specs = [], []
    for dil, (dq, dkp, dkc, dvp, dvc) in zip(DILATIONS + (1,), parts_a + [parts_c]):
        w = dkp.shape[1]
        here = _strip_rows(DZ_TA, w)
        if dil == 1:
            args += [dq, dkp, dkp, dkc, dvp, dvp, dvc]
            in_specs += [_strip_rows(DZ_TA, A_WIDTH), here, ahead(w, 1), here, here, ahead(w, 1), here]
        else:
            k = 8 * dil // DZ_TA
            args += [dq, dkp, dkc, dvp, dvc]
            in_specs += [_strip_rows(DZ_TA, A_WIDTH), ahead(w, k), here, ahead(w, k), here]
    n_att = len(args)
    args = [_strips(a) for a in args] + [_strips(dyb), _strips(dyb), _strips(zb), _strips(zb), _strips(zb), cw]
    in_specs += [_strip_rows(DZ_TA, CONV_CH), _next_rows(DZ_TA, CONV_CH, nb), _strip_rows(DZ_TA, ZB_W),
                 _prev_rows(DZ_TA, ZB_W), _next_rows(DZ_TA, ZB_W, nb), _whole((HALO, CONV_CH))]

    def body(*refs):
        att = list(refs[:n_att])
        dyb_ref, dybn_ref, zb_ref, zbp_ref, zbn_ref, cw_ref, dz_ref, dcw_ref = refs[n_att:]
        i = pl.program_id(0)

        @pl.when(i == 0)
        def _():
            dcw_ref[...] = jnp.zeros_like(dcw_ref)

        def shifted(dil):
            if dil == 1:
                dq_r, kp0, kp1, dkc_r, vp0, vp1, dvc_r = [att.pop(0) for _ in range(7)]
                live = i + 1 < nb
                half = DZ_TA // 2
                kp0, kp1, vp0, vp1 = [r[...].astype(F32) for r in (kp0, kp1, vp0, vp1)]
                dkp = jnp.concatenate([kp0[:, :, half:, :], jnp.where(live, kp1[:, :, :half, :], 0.0)], axis=2)
                dvp = jnp.concatenate([vp0[:, :, half:, :], jnp.where(live, vp1[:, :, :half, :], 0.0)], axis=2)
            else:
                dq_r, dkp_r, dkc_r, dvp_r, dvc_r = [att.pop(0) for _ in range(5)]
                live = i + 8 * dil // DZ_TA < nb
                dkp = jnp.where(live, dkp_r[...].astype(F32), 0.0)
                dvp = jnp.where(live, dvp_r[...].astype(F32), 0.0)
            return dq_r[...].astype(F32), dkc_r[...].astype(F32) + dkp, dvc_r[...].astype(F32) + dvp

        dq, dk, dv = shifted(DILATIONS[0])
        for dil in DILATIONS[1:]:
            dq2, dk2, dv2 = shifted(dil)
            dq, dk, dv = dq + dq2, dk + dk2, dv + dv2
        dz_ref[:, :, :, 0:A_WIDTH] = dq.astype(BF)
        dz_ref[:, :, :, A_WIDTH:2 * A_WIDTH] = dk.astype(BF)
        dz_ref[:, :, :, 2 * A_WIDTH:ZA_W] = dv.astype(BF)
        dq, dk, dv = shifted(1)
        c0 = ZA_W + ZB_W
        dz_ref[:, :, :, c0:c0 + A_WIDTH] = dq.astype(BF)
        dz_ref[:, :, :, c0 + A_WIDTH:c0 + A_WIDTH + C_KV_WIDTH] = dk.astype(BF)
        dz_ref[:, :, :, c0 + A_WIDTH + C_KV_WIDTH:IN_WIDTH] = dv.astype(BF)

        cw = cw_ref[...]
        prev = jnp.where(i > 0, zbp_ref[...], 0.0)
        gb, gc, xb, u, u1, u2, c = _conv_strips(zb_ref[...], prev, cw)
        dyb = dyb_ref[...]
        dc = [_strip(dyb, b) * gb[b] for b in range(N_STRIPS)]
        dcn = jnp.where(i + 1 < nb, dybn_ref[...] * zbn_ref[:, :, :CONV_CH], 0.0)
        wrapped = [_shift_up(dc[0], 1, dcn[0]), _shift_up(dc[1], 1, dcn[1])]
        upd = [jnp.zeros((1, CONV_CH), F32)] * 3
        for b in range(N_STRIPS):
            dc1 = dc[b + 1] if b + 1 < N_STRIPS else wrapped[0]
            dc2 = dc[b + 2] if b + 2 < N_STRIPS else wrapped[b + 2 - N_STRIPS]
            du = cw[2:3, :] * dc[b] + cw[1:2, :] * dc1 + cw[0:1, :] * dc2
            f, e = b % 4, b // 4
            dz_ref[f, e, :, ZA_W:ZA_W + CONV_CH] = (_strip(dyb, b) * c[b]).astype(BF)
            dz_ref[f, e, :, ZA_W + CONV_CH:ZA_W + 2 * CONV_CH] = (du * xb[b]).astype(BF)
            dz_ref[f, e, :, ZA_W + 2 * CONV_CH:c0] = (du * gc[b]).astype(BF)
            for t, uu in enumerate((u2[b], u1[b], u[b])):
                upd[t] = upd[t] + jnp.sum(dc[b] * uu, axis=0, keepdims=True)
        row = lax.broadcasted_iota(jnp.int32, (HALO, CONV_CH), 0)
        tile = jnp.zeros((HALO, CONV_CH), F32)
        for t in range(3):
            tile = jnp.where(row == t, upd[t], tile)
        dcw_ref[...] += tile

    dz, dcw = pl.pallas_call(
        body, grid=(nb,), name="dz_assemble", in_specs=in_specs,
        out_specs=[_strip_rows(DZ_TA, IN_WIDTH), _whole((HALO, CONV_CH))],
        out_shape=[jax.ShapeDtypeStruct((4, 4, na, IN_WIDTH), BF), jax.ShapeDtypeStruct((HALO, CONV_CH), F32)],
        compiler_params=_cparams("arbitrary"),
    )(*args)
    return dz.reshape(s, IN_WIDTH), dcw


def _qkv_bwd(dz, dx1, x, g, w_all, l, tb, tokens_out):
    s, d = x.shape
    na, ta = s // N_STRIPS, tb // N_STRIPS

    def body(dz_ref, dx1_ref, x_ref, g_ref, w_ref, dx_ref, dg_ref):
        i = pl.program_id(0)

        @pl.when(i == 0)
        def _():
            dg_ref[...] = jnp.zeros_like(dg_ref)

        n = IN_WIDTH // N_CHIPS
        dz = dz_ref[...].reshape(tb, IN_WIDTH)
        dh = _nt(dz[:, 0:n], w_ref[0])
        for k in range(1, N_CHIPS):
            dh = dh + _nt(dz[:, k * n:(k + 1) * n], w_ref[k])
        xv = x_ref[...].reshape(tb, d)
        r = _rms_scale(xv)
        xhat = xv * r
        dg_ref[...] += jnp.sum(dh * xhat, axis=0, keepdims=True)
        dx = (dx1_ref[...].reshape(tb, d) + _norm_bwd(dh * g_ref[...], xhat, r)).reshape(4, 4, ta, d)
        if tokens_out:
            for b in range(N_STRIPS):
                dx_ref[:, b, :] = _strip(dx, b)
        else:
            dx_ref[...] = dx

    if tokens_out:
        dx_spec, dx_shape = pl.BlockSpec((ta, N_STRIPS, d), lambda i: (i, 0, 0)), (na, N_STRIPS, d)
    else:
        dx_spec, dx_shape = _strip_rows(ta, d), (4, 4, na, d)
    dx, dg = pl.pallas_call(
        body, grid=(s // tb,), name="qkv_bwd",
        in_specs=[_strip_rows(ta, IN_WIDTH), _strip_rows(ta, d), _strip_rows(ta, d), _whole((1, d)),
                  _layer((N_CHIPS, d, IN_WIDTH // N_CHIPS), l)],
        out_specs=[dx_spec, _whole((HALO, d))],
        out_shape=[jax.ShapeDtypeStruct(dx_shape, F32), jax.ShapeDtypeStruct((HALO, d), F32)],
        compiler_params=_cparams("arbitrary"),
    )(_strips(dz), _strips(dx1), _strips(x), g, w_all)
    return dx.reshape(s, d), dg


def _tile_rows(rows):
    return jnp.pad(rows, ((0, HALO - rows.shape[0]), (0, 0)))


def _to_strips(a, after, name):
    s, d = a.shape
    na = s // N_STRIPS
    ta = min(32, na)

    def body(a_ref, *rest):
        for b in range(N_STRIPS):
            rest[-1][b % 4, b // 4] = a_ref[:, b, :]

    return pl.pallas_call(
        body, grid=(na // ta,), name=name,
        in_specs=[pl.BlockSpec((ta, N_STRIPS, d), lambda i: (i, 0, 0))] + [ANY] * len(after),
        out_specs=_strip_rows(ta, d),
        out_shape=jax.ShapeDtypeStruct((4, 4, na, d), a.dtype), compiler_params=_cparams("parallel"),
    )(a.reshape(na, N_STRIPS, d), *after).reshape(s, d)


def _local_step(x, tgt, fetch, ff, sinks, g_mix, g_group, g_mlp, g_final, emit):
    s, d = x.shape
    depth = g_mix.shape[0]
    tb = min(512, s)
    tf = ff // N_CHIPS
    ts = min(1024, s)
    saved = []
    for l in range(depth):
        w_in, _, _, _, conv_w = fetch(0, l, x)
        cw = _tile_rows(conv_w[l])
        sk = jnp.repeat(sinks[l].reshape(N_HEADS), HEAD_DIM)[None]
        h, za, zb, zc = _qkv_fwd(x, g_mix[l][None], w_in, l, tb)
        parts_a = [_attn_fwd(za, dil, A_WIDTH, 1, 2, 1, A_MAX_DIST, "attn_a_fwd_%d" % dil) for dil in DILATIONS]
        part_c = _attn_fwd(zc, 1, C_KV_WIDTH, 3, 4, C_GROUP, C_MAX_DIST, "attn_c_fwd")
        ya, lse_a, yc, lse_c = _attn_merge(parts_a, part_c, sk, ts)
        w_in, w_o, w1, w2, _ = fetch(1, l, yc)
        x1, yb = _mix_fwd(x, ya, yc, zb, cw, g_group[l][None], w_o, l, ts)
        w_in, w_o, w1, w2, _ = fetch(2, l, x1)
        x2, h2, ap = _mlp_fwd(x1, g_mlp[l][None], w1, w2, l, ts, tf)
        saved.append((x, h, za, zb, zc, ya, lse_a, yc, lse_c, yb, x1, h2, ap, cw, sk))
        x = x2
    dx, loss_tile, dg_final = _loss_head(x, g_final[None], tgt, ts)
    grads = [None] * depth
    tok = jnp.zeros((), F32)
    for l in reversed(range(depth)):
        x0, h, za, zb, zc, ya, lse_a, yc, lse_c, yb, x1, h2, ap, cw, sk = saved[l]
        dx1, dap, dg_mlp = _mlp_bwd(dx, x1, ap, g_mlp[l][None] + tok, w1, w2, l, ts, tf)
        tok = emit(l, 3, _wgrad(ap, dx, min(1024, ff), d, 2 * ts, "wgrad_ff_out", relu2=True))
        tok = tok + emit(l, 2, _wgrad(h2, dap, d, min(1024, ff), 2 * ts, "wgrad_ff_in"))
        n, dya, dyc, dd_a, dd_c, dyb, dg_group, dsink = _mix_bwd(dx1, ya, yb, yc, lse_c, sk, g_group[l][None] + tok,
                                                                 w_o, l, tb)
        tok = emit(l, 1, _wgrad(n, dx1, MIX_WIDTH, d, ts, "wgrad_o"))
        cw = cw + tok
        parts_a = [_attn_bwd(za, dya, lse_a, dd_a, dil, A_WIDTH, 1, 2, 1, A_MAX_DIST, "attn_a_bwd_%d" % dil)
                   for dil in DILATIONS]
        parts_c = _attn_bwd(zc, dyc, lse_c, dd_c, 1, C_KV_WIDTH, 3, 4, C_GROUP, C_MAX_DIST, "attn_c_bwd")
        dz, dcw = _dz_assemble(parts_a, parts_c, dyb, zb, cw)
        tok = emit(l, 0, _wgrad(h, dz, d, IN_WIDTH // 4, 2 * ts, "wgrad_in"))
        dx, dg_mix = _qkv_bwd(dz, dx1, x0, g_mix[l][None] + tok, w_in, l, tb, l == 0)
        grads[l] = (dcw, dsink, dg_mix, dg_group, dg_mlp)
    return loss_tile, dx, grads, dg_final


ANY = pl.BlockSpec(memory_space=pl.ANY)
SHARD_AXES = (2, 1, 2, 1)
N_BIG = len(SHARD_AXES)
N_CHIPS = 4
N_DEV = 8


def _mesh_pos():
    return lax.axis_index("x"), lax.axis_index("y"), lax.axis_index("c")


def _flip(v, bit):
    return 1 - v if bit else v


def _place_shard(shard, chip_arr, name):
    _, rows, cols = shard.shape
    tr = min(256, rows)

    def body(chip_ref, x_ref, o_ref):
        o_ref[...] = x_ref[...].astype(BF)

    return pl.pallas_call(
        body, name=name,
        grid_spec=pltpu.PrefetchScalarGridSpec(
            num_scalar_prefetch=1, grid=(2, rows // tr),
            in_specs=[pl.BlockSpec((None, tr, cols), lambda l, i, chip: (l, i, 0))],
            out_specs=pl.BlockSpec((None, None, tr, cols), lambda l, i, chip: (l, chip[0], i, 0))),
        out_shape=jax.ShapeDtypeStruct((2, N_CHIPS, rows, cols), BF),
        compiler_params=_cparams("parallel", "parallel"),
    )(chip_arr, shard)


HBM = pl.BlockSpec(memory_space=pltpu.HBM)
SEM = pl.BlockSpec(memory_space=pltpu.SEMAPHORE)
EFFECT = pltpu.SideEffectType.DATAFLOW_SIDE_EFFECTING

GATHER_GROUPS = (((0, 0),), ((1, 0),), ((2, 0), (3, 0)), ((0, 1),), ((1, 1), (2, 1), (3, 1)))
GATHER_STARTS = ((0,), (1, 2), (3, 4))
GATHER_STAGES = {(0, 0): 0, (1, 0): 1, (2, 0): 2, (0, 1): 3, (1, 1): 4}


def _gather_copies(arrs, group, send_sems, recv_sems):
    x, y, c = _mesh_pos()
    me = 2 * x + y
    out = []
    for i, (w, layer) in enumerate(group):
        mine = arrs[w].at[layer, me]
        for j, (qx, qy) in enumerate([(1 - x, y), (x, 1 - y), (1 - x, 1 - y)]):
            landed = arrs[w].at[layer, 2 * qx + qy]
            out.append(tuple(pltpu.make_async_remote_copy(
                src_ref=piece, dst_ref=piece, send_sem=send_sems.at[i * 3 + j], recv_sem=recv_sems.at[i * 3 + j],
                device_id=(qx, qy, c), device_id_type=MESH) for piece in (mine, landed)))
    return out


def _conv_copies(conv_src, conv_dst, send_sems, recv_sems):
    x, y, c = _mesh_pos()
    out = []
    for j, (qx, qy) in enumerate([(1 - x, y), (x, 1 - y), (1 - x, 1 - y)]):
        out.append(tuple(pltpu.make_async_remote_copy(
            src_ref=conv_src, dst_ref=conv_dst.at[q], send_sem=send_sems.at[j], recv_sem=recv_sems.at[j],
            device_id=(qx, qy, c), device_id_type=MESH) for q in (2 * x + y, 2 * qx + qy)))
    return out


def _gather_start(groups, arrs, conv, name, through=None):
    n_sems = 2 * (len(groups) + (conv is not None))
    mats = sorted({w for g in groups for w, _ in GATHER_GROUPS[g]})

    def body(*refs):
        arrs_ref = [None] * N_BIG
        for w, ref in zip(mats, refs):
            arrs_ref[w] = ref
        sems = refs[n_in:n_in + n_sems]
        if conv is not None:
            for cp, _ in _conv_copies(refs[len(mats)], refs[len(mats) + 1], sems[-2], sems[-1]):
                cp.start()
        for k, g in enumerate(groups):
            for cp, _ in _gather_copies(arrs_ref, GATHER_GROUPS[g], sems[2 * k], sems[2 * k + 1]):
                cp.start()

    sem_shapes = []
    for n in [len(GATHER_GROUPS[g]) for g in groups] + ([1] if conv is not None else []):
        sem_shapes += [pltpu.SemaphoreType.DMA((3 * n,))] * 2
    operands = [arrs[w] for w in mats] + ([] if conv is None else list(conv)) + ([] if through is None else [through])
    n_in = len(operands)
    res = pl.pallas_call(
        body, name=name,
        out_shape=tuple(sem_shapes) + tuple(pltpu.HBM(a.shape, a.dtype) for a in operands),
        in_specs=(HBM,) * n_in, out_specs=(SEM,) * n_sems + (HBM,) * n_in,
        input_output_aliases={i: n_sems + i for i in range(n_in)},
        compiler_params=pltpu.CompilerParams(has_side_effects=EFFECT),
    )(*[pltpu.with_memory_space_constraint(a, pltpu.HBM) for a in operands])
    arrs = list(arrs)
    for w, a in zip(mats, res[n_sems:]):
        arrs[w] = a
    return res[:n_sems], arrs, list(res[n_sems + len(mats):])


def _gather_wait(k, sems, arrs, conv, after, name):
    group = GATHER_GROUPS[k]
    mats = sorted({w for w, _ in group})
    n_conv = 0 if conv is None else 2

    def body(*refs):
        local = refs[:len(mats)]
        arrs_ref = [None] * N_BIG
        for w, ref in zip(mats, local):
            arrs_ref[w] = ref
        pos = len(mats) + n_conv
        copies = _gather_copies(arrs_ref, group, refs[pos], refs[pos + 1])
        if conv is not None:
            copies += _conv_copies(refs[len(mats)], refs[len(mats) + 1], refs[pos + 2], refs[pos + 3])
        for send, recv in copies:
            recv.wait_recv()
            send.wait_send()

    operands = [arrs[w] for w in mats] + ([] if conv is None else [conv[1], conv[2]])
    sem_ops = list(sems) + ([] if conv is None else list(conv[0]))
    n_op = len(operands)
    res = pl.pallas_call(
        body, name=name, out_shape=tuple(pltpu.HBM(a.shape, a.dtype) for a in operands),
        in_specs=(HBM,) * n_op + (SEM,) * len(sem_ops) + (ANY,) * len(after), out_specs=(HBM,) * n_op,
        input_output_aliases={i: i for i in range(n_op)},
        compiler_params=pltpu.CompilerParams(has_side_effects=EFFECT),
    )(*operands, *sem_ops, *after)
    arrs = list(arrs)
    for w, a in zip(mats, res):
        arrs[w] = a
    return arrs, (res[-1] if conv is not None else None)


def _grad_shard(ref, w, chip, n):
    start = pl.multiple_of(chip * n, 128)
    if SHARD_AXES[w] == 2:
        return ref.at[:, pl.ds(start, n)]
    return ref.at[pl.ds(start, n), :]


def _slot_shape(g, w):
    shape = list(g.shape)
    shape[SHARD_AXES[w] - 1] //= N_CHIPS
    return (N_DEV - 1,) + tuple(shape)


def _scatter_copies(g_ref, land_ref, send_sems, recv_sems, layer, w):
    x, y, c = _mesh_pos()
    n = g_ref.shape[SHARD_AXES[w] - 1] // N_CHIPS
    out = []
    for r in range(1, N_DEV):
        tx, ty, tc = _flip(x, r & 4), _flip(y, r & 2), _flip(c, r & 1)
        cp = pltpu.make_async_remote_copy(
            src_ref=_grad_shard(g_ref, w, 2 * tx + ty, n), dst_ref=land_ref.at[r - 1], send_sem=send_sems.at[r - 1],
            recv_sem=recv_sems.at[r - 1], device_id=(tx, ty, tc), device_id_type=MESH)
        out.append((cp, (c != layer) if r & 1 else (c == layer)))
    return out


def _scatter_start(items, layer, name):
    n = len(items)

    def body(*refs):
        for i, (w, _, _) in enumerate(items):
            g_ref, land_ref = refs[2 * i], refs[2 * i + 1]
            send_sems, recv_sems = refs[2 * n + 2 * i], refs[2 * n + 2 * i + 1]
            for cp, mine in _scatter_copies(g_ref, land_ref, send_sems, recv_sems, layer, w):
                @pl.when(mine)
                def _():
                    cp.start()
        refs[-1][...] = jnp.zeros_like(refs[-1])

    operands = [a for _, g, land in items for a in (g, land)]
    res = pl.pallas_call(
        body, name=name,
        out_shape=(pltpu.SemaphoreType.DMA((N_DEV - 1,)),) * (2 * n)
        + tuple(pltpu.HBM(a.shape, a.dtype) for a in operands) + (jax.ShapeDtypeStruct((HALO, 128), F32),),
        in_specs=(HBM,) * (2 * n),
        out_specs=(SEM,) * (2 * n) + (HBM,) * (2 * n) + (pl.BlockSpec(memory_space=pltpu.VMEM),),
        input_output_aliases={i: 2 * n + i for i in range(2 * n)},
        compiler_params=pltpu.CompilerParams(has_side_effects=EFFECT),
    )(*[pltpu.with_memory_space_constraint(a, pltpu.HBM) for a in operands])
    return [(res[2 * i], res[2 * i + 1], res[2 * n + 2 * i], res[2 * n + 2 * i + 1]) for i in range(n)], res[-1]


def _scatter_wait(started, land, after, w, name):
    def body(g0_ref, g1_ref, land_ref, ss0, rs0, ss1, rs1, after_ref, g0_out, g1_out, land_out):
        c = lax.axis_index("c")
        for layer, g_ref, ss, rs in ((0, g0_ref, ss0, rs0), (1, g1_ref, ss1, rs1)):
            for cp, mine in _scatter_copies(g_ref, land_ref, ss, rs, layer, w):
                @pl.when(mine)
                def _():
                    cp.wait_send()

                @pl.when(c == layer)
                def _():
                    cp.wait_recv()

    (ss0, rs0, g0), (ss1, rs1, g1) = started
    return pl.pallas_call(
        body, name=name,
        out_shape=(pltpu.HBM(g0.shape, g0.dtype), pltpu.HBM(g1.shape, g1.dtype), pltpu.HBM(land.shape, land.dtype)),
        in_specs=(HBM, HBM, HBM, SEM, SEM, SEM, SEM, ANY), out_specs=(HBM, HBM, HBM),
        input_output_aliases={0: 0, 1: 1, 2: 2}, compiler_params=pltpu.CompilerParams(has_side_effects=EFFECT),
    )(g0, g1, land, ss0, rs0, ss1, rs1, after)


def _sum_slots(g0, g1, slots, w, pos_arr, name):
    _, rows, cols = slots.shape
    tr = min(512, rows)
    nr = rows // tr

    def body(pos_ref, g0_ref, g1_ref, slots_ref, out_ref):
        chip, core = pos_ref[0], pos_ref[1]

        def step(own0_ref, own1_ref, s_ref, o_ref):
            acc = jnp.where(core == 0, own0_ref[...], own1_ref[...]).astype(F32)
            for r in range(N_DEV - 1):
                acc = acc + s_ref[r].astype(F32)
            o_ref[...] = acc

        if SHARD_AXES[w] == 2:
            own = pl.BlockSpec((tr, cols), lambda i: (i, chip), pipeline_mode=pl.Buffered(3))
        else:
            own = pl.BlockSpec((tr, cols), lambda i: (chip * nr + i, 0), pipeline_mode=pl.Buffered(3))
        pltpu.emit_pipeline(
            step, grid=(nr,),
            in_specs=[own, own, pl.BlockSpec((N_DEV - 1, tr, cols), lambda i: (0, i, 0), pipeline_mode=pl.Buffered(3))],
            out_specs=[pl.BlockSpec((tr, cols), lambda i: (i, 0))])(g0_ref, g1_ref, slots_ref, out_ref)

    return pl.pallas_call(
        body, name=name, in_specs=[pl.BlockSpec(memory_space=pltpu.SMEM)] + [ANY] * 3, out_specs=ANY,
        out_shape=jax.ShapeDtypeStruct((rows, cols), F32), compiler_params=_cparams(),
    )(pos_arr, g0, g1, slots)


def _swap_copies(refs, n):
    x, y, c = _mesh_pos()
    return [pltpu.make_async_remote_copy(src_ref=refs[w], dst_ref=refs[n + w], send_sem=refs[2 * n].at[w],
                                         recv_sem=refs[2 * n + 1].at[w], device_id=(x, y, 1 - c), device_id_type=MESH)
            for w in range(n)]


def _swap_start(halves, name):
    n = len(halves)

    def body(*refs):
        for cp in _swap_copies(refs, n):
            cp.start()

    operands = list(halves) + [lax.empty(h.shape, h.dtype) for h in halves]
    res = pl.pallas_call(
        body, name=name,
        out_shape=(pltpu.SemaphoreType.DMA((n,)),) * 2 + tuple(pltpu.HBM(a.shape, a.dtype) for a in operands),
        in_specs=(HBM,) * (2 * n), out_specs=(SEM,) * 2 + (HBM,) * (2 * n),
        input_output_aliases={i: 2 + i for i in range(2 * n)},
        compiler_params=pltpu.CompilerParams(has_side_effects=EFFECT),
    )(*[pltpu.with_memory_space_constraint(a, pltpu.HBM) for a in operands])
    return res[0], res[1], list(res[2:2 + n]), list(res[2 + n:])


def _swap_wait(send_sems, recv_sems, halves, lands, after, name):
    n = len(halves)

    def body(*refs):
        for cp in _swap_copies(refs, n):
            cp.wait_send()
            cp.wait_recv()

    operands = list(halves) + list(lands)
    res = pl.pallas_call(
        body, name=name, out_shape=tuple(pltpu.HBM(a.shape, a.dtype) for a in operands),
        in_specs=(HBM,) * (2 * n) + (SEM, SEM, ANY), out_specs=(HBM,) * (2 * n),
        input_output_aliases={i: i for i in range(2 * n)},
        compiler_params=pltpu.CompilerParams(has_side_effects=EFFECT),
    )(*operands, send_sems, recv_sems, after)
    return list(res[n:])


def _adamw_math(w, g, m, v):
    m = ADAM_B1 * m + (1.0 - ADAM_B1) * g
    v = ADAM_B2 * v + (1.0 - ADAM_B2) * jnp.square(g)
    m_hat = m / (1.0 - ADAM_B1 ** ADAM_STEP)
    v_hat = v / (1.0 - ADAM_B2 ** ADAM_STEP)
    delta = -ADAM_LR * (m_hat / (jnp.sqrt(v_hat) + ADAM_EPS) + ADAM_WD * w)
    return delta, m, v


def _adamw(w, g, m, v, filled, pos_arr, name):
    shape = w.shape
    _, rows, cols = shape
    tr = min(256, rows)

    def step(w_ref, g_ref, m_ref, v_ref, go_ref, d_ref, m2_ref, v2_ref):
        g = g_ref[...]
        go_ref[...] = g
        d_ref[...], m2_ref[...], v2_ref[...] = _adamw_math(w_ref[...], g, m_ref[...], v_ref[...])

    def body(pos_ref, w_ref, g_ref, m_ref, v_ref, *rest):
        layer = pos_ref[1] if filled is None else 1 - pos_ref[1]
        full_in = pl.BlockSpec((None, tr, cols), lambda i: (layer, i, 0), pipeline_mode=pl.Buffered(3))
        half = pl.BlockSpec((tr, cols), lambda i: (i, 0), pipeline_mode=pl.Buffered(3))
        full = pl.BlockSpec((None, tr, cols), lambda i: (layer, i, 0))
        pltpu.emit_pipeline(step, grid=(rows // tr,), in_specs=[full_in, half, full_in, full_in],
                            out_specs=[full] * 4)(w_ref, g_ref, m_ref, v_ref, *rest[-4:])

    n_in = 5
    return pl.pallas_call(
        body, name=name,
        in_specs=[pl.BlockSpec(memory_space=pltpu.SMEM)] + [ANY] * (4 if filled is None else 8), out_specs=[ANY] * 4,
        out_shape=[jax.ShapeDtypeStruct(shape, F32)] * 4,
        input_output_aliases={} if filled is None else {n_in + k: k for k in range(4)},
        compiler_params=_cparams(),
    )(pos_arr, w, g, m, v, *([] if filled is None else filled))


def _small_sync(part, w, m, v):
    rows, cols = part.shape

    def body(p_ref, w_ref, m_ref, v_ref, g_ref, d_ref, m2_ref, v2_ref, slots, send_sems, recv_sems):
        x, y, c = _mesh_pos()
        me = 4 * x + 2 * y + c
        slots[me] = p_ref[...]
        sends = []
        for r in range(1, N_DEV):
            to = (_flip(x, r & 4), _flip(y, r & 2), _flip(c, r & 1))
            sends.append(pltpu.make_async_remote_copy(
                src_ref=p_ref, dst_ref=slots.at[me], send_sem=send_sems.at[r - 1], recv_sem=recv_sems.at[r - 1],
                device_id=to, device_id_type=MESH))
        for cp in sends:
            cp.start()
        for cp in sends:
            cp.wait_recv()
        for cp in sends:
            cp.wait_send()
        g = slots[0]
        for i in range(1, N_DEV):
            g = g + slots[i]
        g_ref[...] = g
        d_ref[...], m2_ref[...], v2_ref[...] = _adamw_math(w_ref[...], g, m_ref[...], v_ref[...])

    vm = pl.BlockSpec(memory_space=pltpu.VMEM)
    return pl.pallas_call(
        body, name="small_sync", in_specs=[vm] * 4, out_specs=[vm] * 4,
        out_shape=[jax.ShapeDtypeStruct((rows, cols), F32)] * 4,
        scratch_shapes=[pltpu.VMEM((N_DEV, rows, cols), F32), pltpu.SemaphoreType.DMA((N_DEV - 1,)),
                        pltpu.SemaphoreType.DMA((N_DEV - 1,))],
    )(part, w, m, v)


PACK_W = 256


def _pack_rows(n):
    return -(-n // (HALO * PACK_W)) * HALO


def _pack_small(parts):
    out = []
    for a in parts:
        flat = a.reshape(-1)
        out.append(jnp.pad(flat, (0, _pack_rows(flat.size) * PACK_W - flat.size)).reshape(-1, PACK_W))
    return jnp.concatenate(out, axis=0)


def _unpack_small(p, shapes):
    out, row = [], 0
    for shape in shapes:
        n = 1
        for k in shape:
            n *= k
        out.append(p[row:row + _pack_rows(n)].reshape(-1)[:n].reshape(shape))
        row += _pack_rows(n)
    return out


def kernel(x, w_in, conv_w, sinks, g_mix, g_group, w_o, g_mlp, w_ff_in, w_ff_out, g_final, loss_target, m_w_in, m_conv_w, m_sinks, m_g_mix, m_g_group, m_w_o, m_g_mlp, m_w_ff_in, m_w_ff_out, m_g_final, v_w_in, v_conv_w, v_sinks, v_g_mix, v_g_group, v_w_o, v_g_mlp, v_w_ff_in, v_w_ff_out, v_g_final):
    chip = 2 * lax.axis_index("x") + lax.axis_index("y")
    conv_n = conv_w.shape[2]

    pos_arr = jnp.stack([chip, lax.axis_index("c")]).astype(jnp.int32)
    shards = (w_in, w_o, w_ff_in, w_ff_out)
    conv_tile = jnp.pad(conv_w.reshape(6, conv_n), ((0, HALO - 6), (0, 128 - conv_n)))
    placed = [_place_shard(w_in, pos_arr[:1], "place_shard_0"), None, None, None]
    sems_a, placed, conv_thru = _gather_start(
        GATHER_STARTS[0], placed, (conv_tile, lax.empty((N_CHIPS,) + conv_tile.shape, conv_tile.dtype)),
        "gather_start_0")
    for i in range(1, N_BIG):
        placed[i] = _place_shard(shards[i], pos_arr[:1], "place_shard_%d" % i)
    full = {"arrs": placed, "conv": None, "sems": list(sems_a[:2])}
    target = _to_strips(loss_target[0], placed[:1], "to_strips_target")

    def fetch(stage, layer, after):
        k = GATHER_STAGES.get((stage, layer))
        if k is None:
            return (*full["arrs"], full["conv"])
        sems = full["sems"][2 * k:2 * k + 2]
        if k == 0:
            full["arrs"], land = _gather_wait(0, sems, full["arrs"], (sems_a[-2:], *conv_thru), (after, target),
                                              "gather_wait_0")
            conv_all = lax.dynamic_update_slice(land, conv_tile[None], (chip, 0, 0))
            full["conv"] = conv_all[:, :6, :conv_n].reshape(N_CHIPS, 2, 3, conv_n).transpose(1, 2, 0, 3).reshape(
                2, 3, CONV_CH)
            sems_b, full["arrs"], rest = _gather_start(GATHER_STARTS[1], full["arrs"], None, "gather_start_1",
                                                       through=full["arrs"][0])
            full["arrs"][0] = rest[-1]
            full["sems"] += list(sems_b)
        else:
            full["arrs"], _ = _gather_wait(k, sems, full["arrs"], None, (after,), "gather_wait_%d" % k)
        if k == 2:
            sems_c, full["arrs"], _ = _gather_start(GATHER_STARTS[2], full["arrs"], None, "gather_start_2")
            full["sems"] += list(sems_c)
        return (*full["arrs"], full["conv"])

    lands, started, pending = [None] * N_BIG, {}, []

    def emit(layer, w, g):
        if lands[w] is None:
            lands[w] = lax.empty(_slot_shape(g, w), g.dtype)
        pending.append((w, g, lands[w]))
        if not (w == 0 or (layer == 0 and w == 1)):
            return jnp.zeros((), F32)
        name = "scatter_start_%d_%d" % (layer, len(pending))
        done, token = _scatter_start(list(pending), layer, name)
        for (w_i, _, _), (ss, rs, g_thru, land) in zip(pending, done):
            started[layer, w_i], lands[w_i] = (ss, rs, g_thru), land
        pending.clear()
        return token[0, 0]

    loss_tile, dx, grads, dg_final = _local_step(_to_strips(x[0], placed, "to_strips_x"), target, fetch,
                                                 w_ff_in.shape[2] * N_CHIPS,
                                                 sinks, g_mix, g_group, g_mlp, g_final, emit)

    wmv = ((w_in, m_w_in, v_w_in), (w_o, m_w_o, v_w_o), (w_ff_in, m_w_ff_in, v_w_ff_in),
           (w_ff_out, m_w_ff_out, v_w_ff_out))
    big, after = [None] * N_BIG, dx
    for name, ws in (("swap_rest", (1, 2, 3)), ("swap_in", (0,))):
        own = []
        for w in ws:
            g0, g1, slots = _scatter_wait((started[0, w], started[1, w]), lands[w], after, w, "scatter_wait_%d" % w)
            own.append(_sum_slots(g0, g1, slots, w, pos_arr, "sum_slots_%d" % w))
        send_sems, recv_sems, own, zones = _swap_start(own, name + "_start")
        for w, g in zip(ws, own):
            big[w] = _adamw(wmv[w][0], g, wmv[w][1], wmv[w][2], None, pos_arr, "adamw_own_%d" % w)
        theirs = _swap_wait(send_sems, recv_sems, own, zones, big[ws[-1]][1], name + "_wait")
        for w, g in zip(ws, theirs):
            big[w] = _adamw(wmv[w][0], g, wmv[w][1], wmv[w][2], big[w], pos_arr, "adamw_other_%d" % w)
        after = big[ws[-1]][1]

    def both(i):
        return jnp.stack([grads[0][i][0], grads[1][i][0]])
    dconv = jnp.stack([grads[0][0][:3], grads[1][0][:3]])
    dsinks = jnp.stack([grads[0][1][0, ::HEAD_DIM], grads[1][1][0, ::HEAD_DIM]])
    part = _pack_small([both(2), both(3), both(4), dg_final[0], dconv, dsinks, loss_tile[0, 0]])

    def spread(shard):
        return lax.dynamic_update_slice(jnp.zeros((2, 3, CONV_CH), F32), shard, (0, 0, chip * conv_n))
    zero = jnp.zeros((), F32)
    packs = [_pack_small([a, b, c_, e, spread(f), g_, zero]) for a, b, c_, e, f, g_ in (
        (g_mix, g_group, g_mlp, g_final, conv_w, sinks),
        (m_g_mix, m_g_group, m_g_mlp, m_g_final, m_conv_w, m_sinks),
        (v_g_mix, v_g_group, v_g_mlp, v_g_final, v_conv_w, v_sinks))]
    shapes = [g_mix.shape, g_group.shape, g_mlp.shape, g_final.shape, (2, 3, CONV_CH), sinks.shape, ()]
    small = [_unpack_small(p, shapes) for p in _small_sync(part, *packs)]

    def shard_of(full):
        return lax.dynamic_slice(full, (0, 0, chip * conv_n), (2, 3, conv_n))
    small = [(s[0], s[1], s[2], s[3], shard_of(s[4]), s[5], s[6]) for s in small]
    loss = small[0][6]

    def ordered(kind):
        b = [big[i][kind] for i in range(N_BIG)]
        s = small[kind]
        return [b[0], s[4], s[5], s[0], s[1], b[1], s[2], b[2], b[3], s[3]]

    return (loss, dx[None], *ordered(0), *ordered(1), *ordered(2), *ordered(3))
```
